```python
import math
import jax, jax.numpy as jnp
from jax import lax
import numpy as np

D_MODEL = 1024
BATCH = 16
SEQ = 2048
DEPTH = 1

CHUNK = 64
MIX_WIDTH = D_MODEL
SB_WIDTH = MIX_WIDTH // 2
SB_HEAD_DIM = 64
SB_HEADS = SB_WIDTH // SB_HEAD_DIM
SSM_WIDTH = MIX_WIDTH - SB_WIDTH
SSM_GROUP = 16
SSM_GROUPS = SSM_WIDTH // SSM_GROUP
SSM_STATE = 64
D_FF = 4 * D_MODEL
QBLOCK = 128
EPS = 1e-6
DT_MIN = 1e-3
DT_MAX = 1e-1

kernel_name = "hybrid_stickbreaking_s5_block"


def rmsnorm(x, g):
    xf = x.astype(jnp.float32)
    y = xf * lax.rsqrt(jnp.mean(xf * xf, axis=-1, keepdims=True) + EPS)
    return y * g.astype(jnp.float32)


def stick_breaking_attention(q, k, v):
    L = q.shape[2]
    scale = 1.0 / math.sqrt(q.shape[-1])
    outs = []
    for i in range(L // QBLOCK):
        q0 = i * QBLOCK
        kend = q0 + QBLOCK
        qb = q[:, :, q0:kend]
        kb = k[:, :, :kend]
        vb = v[:, :, :kend]
        z = jnp.einsum('bhqd,bhkd->bhqk', qb, kb) * scale
        t_idx = q0 + jnp.arange(QBLOCK)[:, None]
        s_idx = jnp.arange(kend)[None, :]
        mask = s_idx < t_idx
        log_one_minus = jnp.where(mask, -jax.nn.softplus(z), 0.0)
        tail = lax.cumsum(log_one_minus, axis=3, reverse=True) - log_one_minus
        log_a = jax.nn.log_sigmoid(z) + tail
        a = jnp.where(mask, jnp.exp(log_a), 0.0)
        outs.append(jnp.einsum('bhqk,bhkd->bhqd', a, vb))
    return jnp.concatenate(outs, axis=2)


def s5_glu(u, lam_re, lam_im, log_dt, b_re, b_im, c_re, c_im, d_skip, w_glu, b_glu):
    Bsz, L, _ = u.shape
    ug = u.reshape(Bsz, L, SSM_GROUPS, SSM_GROUP)
    lam = lax.complex(lam_re.astype(jnp.float32), lam_im.astype(jnp.float32))
    dt = jnp.exp(log_dt.astype(jnp.float32))[:, None]
    lam_bar = jnp.exp(lam * dt)
    b_mat = lax.complex(b_re.astype(jnp.float32), b_im.astype(jnp.float32))
    c_mat = lax.complex(c_re.astype(jnp.float32), c_im.astype(jnp.float32))
    b_bar = ((lam_bar - 1.0) / lam)[:, :, None] * b_mat
    bu = jnp.einsum('blgh,gph->blgp', ug.astype(jnp.complex64), b_bar)
    lam_seq = jnp.broadcast_to(lam_bar[None, None], (1, L, SSM_GROUPS, SSM_STATE))

    def combine(e_i, e_j):
        a_i, s_i = e_i
        a_j, s_j = e_j
        return a_j * a_i, a_j * s_i + s_j

    _, states = lax.associative_scan(combine, (lam_seq, bu), axis=1)
    y = jnp.einsum('blgp,ghp->blgh', states, c_mat).real + d_skip.astype(jnp.float32)[None, None] * ug
    y = jax.nn.gelu(y.reshape(Bsz, L, SSM_WIDTH))
    gate = jax.nn.sigmoid(y @ w_glu.astype(jnp.float32) + b_glu.astype(jnp.float32))
    return y * gate


def _fwd_setup_inputs(seed: int = 0) -> dict:
    key = jax.random.key(seed)
    ks = jax.random.split(key, 24)
    f32 = jnp.float32
    G, P, H = SSM_GROUPS, SSM_STATE, SSM_GROUP
    x = jax.random.normal(ks[0], (BATCH, SEQ, D_MODEL), f32)
    norm1_g = 1.0 + 0.02 * jax.random.normal(ks[1], (D_MODEL,), f32)
    w_in = jax.random.normal(ks[2], (D_MODEL, 3 * SB_WIDTH + SSM_WIDTH), f32) * D_MODEL ** -0.5
    q_norm_g = 1.0 + 0.02 * jax.random.normal(ks[3], (SB_HEAD_DIM,), f32)
    k_norm_g = 1.0 + 0.02 * jax.random.normal(ks[4], (SB_HEAD_DIM,), f32)
    ssm_lambda_re = -0.5 + 0.01 * jax.random.normal(ks[5], (G, P), f32)
    ssm_lambda_im = math.pi * jnp.broadcast_to(jnp.arange(P, dtype=f32)[None], (G, P)) \
        + 0.01 * jax.random.normal(ks[6], (G, P), f32)
    ssm_log_dt = jax.random.uniform(ks[7], (G,), f32, math.log(DT_MIN), math.log(DT_MAX))
    ssm_b_re = jax.random.normal(ks[8], (G, P, H), f32) * (2.0 * H) ** -0.5
    ssm_b_im = jax.random.normal(ks[9], (G, P, H), f32) * (2.0 * H) ** -0.5
    ssm_c_re = jax.random.normal(ks[10], (G, H, P), f32) * (2.0 * P) ** -0.5
    ssm_c_im = jax.random.normal(ks[11], (G, H, P), f32) * (2.0 * P) ** -0.5
    ssm_d = jax.random.normal(ks[12], (G, H), f32)
    w_glu = jax.random.normal(ks[13], (SSM_WIDTH, SSM_WIDTH), f32) * SSM_WIDTH ** -0.5
    b_glu = 0.01 * jax.random.normal(ks[14], (SSM_WIDTH,), f32)
    attn_out_g = 1.0 + 0.02 * jax.random.normal(ks[15], (SB_WIDTH,), f32)
    ssm_out_g = 1.0 + 0.02 * jax.random.normal(ks[16], (SSM_WIDTH,), f32)
    w_out = jax.random.normal(ks[17], (MIX_WIDTH, D_MODEL), f32) * MIX_WIDTH ** -0.5
    norm2_g = 1.0 + 0.02 * jax.random.normal(ks[18], (D_MODEL,), f32)
    w_mlp_in = jax.random.normal(ks[19], (D_MODEL, D_FF), f32) * D_MODEL ** -0.5
    w_mlp_out = jax.random.normal(ks[20], (D_FF, D_MODEL), f32) * D_FF ** -0.5
    return {"x": x, "norm1_g": norm1_g, "w_in": w_in, "q_norm_g": q_norm_g, "k_norm_g": k_norm_g,
            "ssm_lambda_re": ssm_lambda_re, "ssm_lambda_im": ssm_lambda_im, "ssm_log_dt": ssm_log_dt,
            "ssm_b_re": ssm_b_re, "ssm_b_im": ssm_b_im, "ssm_c_re": ssm_c_re, "ssm_c_im": ssm_c_im,
            "ssm_d": ssm_d, "w_glu": w_glu, "b_glu": b_glu, "attn_out_g": attn_out_g,
            "ssm_out_g": ssm_out_g, "w_out": w_out, "norm2_g": norm2_g,
            "w_mlp_in": w_mlp_in, "w_mlp_out": w_mlp_out}


def _fwd_reference(x, norm1_g, w_in, q_norm_g, k_norm_g, ssm_lambda_re, ssm_lambda_im, ssm_log_dt,
              ssm_b_re, ssm_b_im, ssm_c_re, ssm_c_im, ssm_d, w_glu, b_glu, attn_out_g,
              ssm_out_g, w_out, norm2_g, w_mlp_in, w_mlp_out):
    Bsz, L, _ = x.shape
    h = x.astype(jnp.float32)
    for _layer in range(DEPTH):
        xn = rmsnorm(h, norm1_g)
        proj = xn @ w_in.astype(jnp.float32)
        q, k, v, u = jnp.split(proj, [SB_WIDTH, 2 * SB_WIDTH, 3 * SB_WIDTH], axis=-1)

        def heads(t):
            return t.reshape(Bsz, L, SB_HEADS, SB_HEAD_DIM).transpose(0, 2, 1, 3)

        qh = rmsnorm(heads(q), q_norm_g)
        kh = rmsnorm(heads(k), k_norm_g)
        vh = heads(v)
        sb = stick_breaking_attention(qh, kh, vh)
        sb = sb.transpose(0, 2, 1, 3).reshape(Bsz, L, SB_WIDTH)

        ssm = s5_glu(u, ssm_lambda_re, ssm_lambda_im, ssm_log_dt, ssm_b_re, ssm_b_im,
                     ssm_c_re, ssm_c_im, ssm_d, w_glu, b_glu)

        mixed = jnp.concatenate([rmsnorm(sb, attn_out_g), rmsnorm(ssm, ssm_out_g)], axis=-1)
        h = h + mixed @ w_out.astype(jnp.float32)

        hn = rmsnorm(h, norm2_g)
        a = jnp.square(jax.nn.relu(hn @ w_mlp_in.astype(jnp.float32)))
        h = h + a @ w_mlp_out.astype(jnp.float32)
    return h.astype(x.dtype)


import jax as _jax
import jax.numpy as _jnp

TWIN_FORMAT = 'train_step'
FWD_PARAMS = ['x', 'norm1_g', 'w_in', 'q_norm_g', 'k_norm_g', 'ssm_lambda_re', 'ssm_lambda_im', 'ssm_log_dt', 'ssm_b_re', 'ssm_b_im', 'ssm_c_re', 'ssm_c_im', 'ssm_d', 'w_glu', 'b_glu', 'attn_out_g', 'ssm_out_g', 'w_out', 'norm2_g', 'w_mlp_in', 'w_mlp_out']
TWIN_WEIGHTS = ['norm1_g', 'w_in', 'q_norm_g', 'k_norm_g', 'ssm_lambda_re', 'ssm_lambda_im', 'ssm_log_dt', 'ssm_b_re', 'ssm_b_im', 'ssm_c_re', 'ssm_c_im', 'ssm_d', 'w_glu', 'b_glu', 'attn_out_g', 'ssm_out_g', 'w_out', 'norm2_g', 'w_mlp_in', 'w_mlp_out']
TWIN_DIFF_INPUT = 'x'
TWIN_INPUTS = ['x', 'norm1_g', 'w_in', 'q_norm_g', 'k_norm_g', 'ssm_lambda_re', 'ssm_lambda_im', 'ssm_log_dt', 'ssm_b_re', 'ssm_b_im', 'ssm_c_re', 'ssm_c_im', 'ssm_d', 'w_glu', 'b_glu', 'attn_out_g', 'ssm_out_g', 'w_out', 'norm2_g', 'w_mlp_in', 'w_mlp_out', 'loss_target', 'm_norm1_g', 'm_w_in', 'm_q_norm_g', 'm_k_norm_g', 'm_ssm_lambda_re', 'm_ssm_lambda_im', 'm_ssm_log_dt', 'm_ssm_b_re', 'm_ssm_b_im', 'm_ssm_c_re', 'm_ssm_c_im', 'm_ssm_d', 'm_w_glu', 'm_b_glu', 'm_attn_out_g', 'm_ssm_out_g', 'm_w_out', 'm_norm2_g', 'm_w_mlp_in', 'm_w_mlp_out', 'v_norm1_g', 'v_w_in', 'v_q_norm_g', 'v_k_norm_g', 'v_ssm_lambda_re', 'v_ssm_lambda_im', 'v_ssm_log_dt', 'v_ssm_b_re', 'v_ssm_b_im', 'v_ssm_c_re', 'v_ssm_c_im', 'v_ssm_d', 'v_w_glu', 'v_b_glu', 'v_attn_out_g', 'v_ssm_out_g', 'v_w_out', 'v_norm2_g', 'v_w_mlp_in', 'v_w_mlp_out']
TWIN_OUTPUTS = ['loss', 'grad_x', 'grad_norm1_g', 'grad_w_in', 'grad_q_norm_g', 'grad_k_norm_g', 'grad_ssm_lambda_re', 'grad_ssm_lambda_im', 'grad_ssm_log_dt', 'grad_ssm_b_re', 'grad_ssm_b_im', 'grad_ssm_c_re', 'grad_ssm_c_im', 'grad_ssm_d', 'grad_w_glu', 'grad_b_glu', 'grad_attn_out_g', 'grad_ssm_out_g', 'grad_w_out', 'grad_norm2_g', 'grad_w_mlp_in', 'grad_w_mlp_out', 'delta_norm1_g', 'delta_w_in', 'delta_q_norm_g', 'delta_k_norm_g', 'delta_ssm_lambda_re', 'delta_ssm_lambda_im', 'delta_ssm_log_dt', 'delta_ssm_b_re', 'delta_ssm_b_im', 'delta_ssm_c_re', 'delta_ssm_c_im', 'delta_ssm_d', 'delta_w_glu', 'delta_b_glu', 'delta_attn_out_g', 'delta_ssm_out_g', 'delta_w_out', 'delta_norm2_g', 'delta_w_mlp_in', 'delta_w_mlp_out', 'new_m_norm1_g', 'new_m_w_in', 'new_m_q_norm_g', 'new_m_k_norm_g', 'new_m_ssm_lambda_re', 'new_m_ssm_lambda_im', 'new_m_ssm_log_dt', 'new_m_ssm_b_re', 'new_m_ssm_b_im', 'new_m_ssm_c_re', 'new_m_ssm_c_im', 'new_m_ssm_d', 'new_m_w_glu', 'new_m_b_glu', 'new_m_attn_out_g', 'new_m_ssm_out_g', 'new_m_w_out', 'new_m_norm2_g', 'new_m_w_mlp_in', 'new_m_w_mlp_out', 'new_v_norm1_g', 'new_v_w_in', 'new_v_q_norm_g', 'new_v_k_norm_g', 'new_v_ssm_lambda_re', 'new_v_ssm_lambda_im', 'new_v_ssm_log_dt', 'new_v_ssm_b_re', 'new_v_ssm_b_im', 'new_v_ssm_c_re', 'new_v_ssm_c_im', 'new_v_ssm_d', 'new_v_w_glu', 'new_v_b_glu', 'new_v_attn_out_g', 'new_v_ssm_out_g', 'new_v_w_out', 'new_v_norm2_g', 'new_v_w_mlp_in', 'new_v_w_mlp_out']
TWIN_LEAF_KINDS = {'loss': 'loss', 'grad_x': 'grad_x', 'grad_norm1_g': 'grad_w', 'grad_w_in': 'grad_w', 'grad_q_norm_g': 'grad_w', 'grad_k_norm_g': 'grad_w', 'grad_ssm_lambda_re': 'grad_w', 'grad_ssm_lambda_im': 'grad_w', 'grad_ssm_log_dt': 'grad_w', 'grad_ssm_b_re': 'grad_w', 'grad_ssm_b_im': 'grad_w', 'grad_ssm_c_re': 'grad_w', 'grad_ssm_c_im': 'grad_w', 'grad_ssm_d': 'grad_w', 'grad_w_glu': 'grad_w', 'grad_b_glu': 'grad_w', 'grad_attn_out_g': 'grad_w', 'grad_ssm_out_g': 'grad_w', 'grad_w_out': 'grad_w', 'grad_norm2_g': 'grad_w', 'grad_w_mlp_in': 'grad_w', 'grad_w_mlp_out': 'grad_w', 'delta_norm1_g': 'delta_w', 'delta_w_in': 'delta_w', 'delta_q_norm_g': 'delta_w', 'delta_k_norm_g': 'delta_w', 'delta_ssm_lambda_re': 'delta_w', 'delta_ssm_lambda_im': 'delta_w', 'delta_ssm_log_dt': 'delta_w', 'delta_ssm_b_re': 'delta_w', 'delta_ssm_b_im': 'delta_w', 'delta_ssm_c_re': 'delta_w', 'delta_ssm_c_im': 'delta_w', 'delta_ssm_d': 'delta_w', 'delta_w_glu': 'delta_w', 'delta_b_glu': 'delta_w', 'delta_attn_out_g': 'delta_w', 'delta_ssm_out_g': 'delta_w', 'delta_w_out': 'delta_w', 'delta_norm2_g': 'delta_w', 'delta_w_mlp_in': 'delta_w', 'delta_w_mlp_out': 'delta_w', 'new_m_norm1_g': 'new_m', 'new_m_w_in': 'new_m', 'new_m_q_norm_g': 'new_m', 'new_m_k_norm_g': 'new_m', 'new_m_ssm_lambda_re': 'new_m', 'new_m_ssm_lambda_im': 'new_m', 'new_m_ssm_log_dt': 'new_m', 'new_m_ssm_b_re': 'new_m', 'new_m_ssm_b_im': 'new_m', 'new_m_ssm_c_re': 'new_m', 'new_m_ssm_c_im': 'new_m', 'new_m_ssm_d': 'new_m', 'new_m_w_glu': 'new_m', 'new_m_b_glu': 'new_m', 'new_m_attn_out_g': 'new_m', 'new_m_ssm_out_g': 'new_m', 'new_m_w_out': 'new_m', 'new_m_norm2_g': 'new_m', 'new_m_w_mlp_in': 'new_m', 'new_m_w_mlp_out': 'new_m', 'new_v_norm1_g': 'new_v', 'new_v_w_in': 'new_v', 'new_v_q_norm_g': 'new_v', 'new_v_k_norm_g': 'new_v', 'new_v_ssm_lambda_re': 'new_v', 'new_v_ssm_lambda_im': 'new_v', 'new_v_ssm_log_dt': 'new_v', 'new_v_ssm_b_re': 'new_v', 'new_v_ssm_b_im': 'new_v', 'new_v_ssm_c_re': 'new_v', 'new_v_ssm_c_im': 'new_v', 'new_v_ssm_d': 'new_v', 'new_v_w_glu': 'new_v', 'new_v_b_glu': 'new_v', 'new_v_attn_out_g': 'new_v', 'new_v_ssm_out_g': 'new_v', 'new_v_w_out': 'new_v', 'new_v_norm2_g': 'new_v', 'new_v_w_mlp_in': 'new_v', 'new_v_w_mlp_out': 'new_v'}


def _forward(args):
    return _fwd_reference(*[args[k] for k in FWD_PARAMS])


def _output_shape():
    out = _jax.eval_shape(lambda: _forward(_fwd_setup_inputs(0)))
    return out.shape, out.dtype

N_MICROBATCH = 1
ADAM_LR = 0.001
ADAM_B1 = 0.9
ADAM_B2 = 0.999
ADAM_EPS = 1e-08
ADAM_WD = 0.01
ADAM_STEP = 10
PER_EXAMPLE_BATCH_AXIS = {'x': 0, 'loss_target': 0}
SHARED_INPUTS = []
_WEIGHT_DTYPES = {'norm1_g': _jnp.float32, 'w_in': _jnp.float32, 'q_norm_g': _jnp.float32, 'k_norm_g': _jnp.float32, 'ssm_lambda_re': _jnp.float32, 'ssm_lambda_im': _jnp.float32, 'ssm_log_dt': _jnp.float32, 'ssm_b_re': _jnp.float32, 'ssm_b_im': _jnp.float32, 'ssm_c_re': _jnp.float32, 'ssm_c_im': _jnp.float32, 'ssm_d': _jnp.float32, 'w_glu': _jnp.float32, 'b_glu': _jnp.float32, 'attn_out_g': _jnp.float32, 'ssm_out_g': _jnp.float32, 'w_out': _jnp.float32, 'norm2_g': _jnp.float32, 'w_mlp_in': _jnp.float32, 'w_mlp_out': _jnp.float32}
MOMENT_SCALE = {'norm1_g': 1.238432e+00, 'w_in': 7.181023e-01, 'q_norm_g': 1.096936e+00, 'k_norm_g': 1.110030e+00, 'ssm_lambda_re': 5.581475e-02, 'ssm_lambda_im': 3.736492e-02, 'ssm_log_dt': 2.972197e+01, 'ssm_b_re': 2.934372e-02, 'ssm_b_im': 3.184865e-02, 'ssm_c_re': 5.700525e-02, 'ssm_c_im': 5.751771e-02, 'ssm_d': 1.581672e+01, 'w_glu': 1.848743e+00, 'b_glu': 6.058898e+00, 'attn_out_g': 3.152113e+01, 'ssm_out_g': 6.201863e+01, 'w_out': 9.750097e+00, 'norm2_g': 9.642284e+01, 'w_mlp_in': 3.780511e+00, 'w_mlp_out': 1.081172e+01}


def _to_microbatches(a, axis):
    t = _jnp.moveaxis(a, axis, 0)
    t = t.reshape((N_MICROBATCH, t.shape[0] // N_MICROBATCH) + t.shape[1:])
    return _jnp.moveaxis(t, 1, axis + 1)


def setup_inputs(seed: int = 0) -> dict:
    inp = _fwd_setup_inputs(seed)
    key = _jax.random.fold_in(_jax.random.key(seed), 7919)
    shape, _ = _output_shape()
    out = dict(inp)
    out["loss_target"] = _jax.random.normal(_jax.random.fold_in(key, 0), shape, _jnp.float32)
    for i, name in enumerate(TWIN_WEIGHTS):
        w = inp[name].astype(_jnp.float32)
        if MOMENT_SCALE is None:
            s = _jnp.sqrt(_jnp.mean(_jnp.square(w)) + 1e-30)
        else:
            s = MOMENT_SCALE[name]
        km, kv = _jax.random.split(_jax.random.fold_in(key, i + 1))
        out[name] = w
        out["m_" + name] = s * _jax.random.normal(km, w.shape, _jnp.float32)
        out["v_" + name] = (s * s) * _jax.random.uniform(kv, w.shape, _jnp.float32, 0.5, 1.5)
    if N_MICROBATCH > 1:
        for name, axis in PER_EXAMPLE_BATCH_AXIS.items():
            out[name] = _to_microbatches(out[name], axis)
    return {'x': out['x'], 'norm1_g': out['norm1_g'], 'w_in': out['w_in'], 'q_norm_g': out['q_norm_g'], 'k_norm_g': out['k_norm_g'], 'ssm_lambda_re': out['ssm_lambda_re'], 'ssm_lambda_im': out['ssm_lambda_im'], 'ssm_log_dt': out['ssm_log_dt'], 'ssm_b_re': out['ssm_b_re'], 'ssm_b_im': out['ssm_b_im'], 'ssm_c_re': out['ssm_c_re'], 'ssm_c_im': out['ssm_c_im'], 'ssm_d': out['ssm_d'], 'w_glu': out['w_glu'], 'b_glu': out['b_glu'], 'attn_out_g': out['attn_out_g'], 'ssm_out_g': out['ssm_out_g'], 'w_out': out['w_out'], 'norm2_g': out['norm2_g'], 'w_mlp_in': out['w_mlp_in'], 'w_mlp_out': out['w_mlp_out'], 'loss_target': out['loss_target'], 'm_norm1_g': out['m_norm1_g'], 'm_w_in': out['m_w_in'], 'm_q_norm_g': out['m_q_norm_g'], 'm_k_norm_g': out['m_k_norm_g'], 'm_ssm_lambda_re': out['m_ssm_lambda_re'], 'm_ssm_lambda_im': out['m_ssm_lambda_im'], 'm_ssm_log_dt': out['m_ssm_log_dt'], 'm_ssm_b_re': out['m_ssm_b_re'], 'm_ssm_b_im': out['m_ssm_b_im'], 'm_ssm_c_re': out['m_ssm_c_re'], 'm_ssm_c_im': out['m_ssm_c_im'], 'm_ssm_d': out['m_ssm_d'], 'm_w_glu': out['m_w_glu'], 'm_b_glu': out['m_b_glu'], 'm_attn_out_g': out['m_attn_out_g'], 'm_ssm_out_g': out['m_ssm_out_g'], 'm_w_out': out['m_w_out'], 'm_norm2_g': out['m_norm2_g'], 'm_w_mlp_in': out['m_w_mlp_in'], 'm_w_mlp_out': out['m_w_mlp_out'], 'v_norm1_g': out['v_norm1_g'], 'v_w_in': out['v_w_in'], 'v_q_norm_g': out['v_q_norm_g'], 'v_k_norm_g': out['v_k_norm_g'], 'v_ssm_lambda_re': out['v_ssm_lambda_re'], 'v_ssm_lambda_im': out['v_ssm_lambda_im'], 'v_ssm_log_dt': out['v_ssm_log_dt'], 'v_ssm_b_re': out['v_ssm_b_re'], 'v_ssm_b_im': out['v_ssm_b_im'], 'v_ssm_c_re': out['v_ssm_c_re'], 'v_ssm_c_im': out['v_ssm_c_im'], 'v_ssm_d': out['v_ssm_d'], 'v_w_glu': out['v_w_glu'], 'v_b_glu': out['v_b_glu'], 'v_attn_out_g': out['v_attn_out_g'], 'v_ssm_out_g': out['v_ssm_out_g'], 'v_w_out': out['v_w_out'], 'v_norm2_g': out['v_norm2_g'], 'v_w_mlp_in': out['v_w_mlp_in'], 'v_w_mlp_out': out['v_w_mlp_out']}


def _loss(weights, diff, rest, loss_target):
    with _jax.named_scope("forward"):
        args = {**rest, TWIN_DIFF_INPUT: diff, **{k: w.astype(_WEIGHT_DTYPES[k]) for k, w in weights.items()}}
        y = _forward(args)
    with _jax.named_scope("loss_head"):
        err = _jnp.square(y.astype(_jnp.float32) - loss_target)
        return 0.5 * _jnp.sum(_jnp.mean(err, axis=-1)) if err.ndim else 0.5 * err


def _adamw(w, g, m, v):
    m = ADAM_B1 * m + (1.0 - ADAM_B1) * g
    v = ADAM_B2 * v + (1.0 - ADAM_B2) * _jnp.square(g)
    m_hat = m / (1.0 - ADAM_B1 ** ADAM_STEP)
    v_hat = v / (1.0 - ADAM_B2 ** ADAM_STEP)
    delta = -ADAM_LR * (m_hat / (_jnp.sqrt(v_hat) + ADAM_EPS) + ADAM_WD * w)
    return delta, m, v


def reference(x, norm1_g, w_in, q_norm_g, k_norm_g, ssm_lambda_re, ssm_lambda_im, ssm_log_dt, ssm_b_re, ssm_b_im, ssm_c_re, ssm_c_im, ssm_d, w_glu, b_glu, attn_out_g, ssm_out_g, w_out, norm2_g, w_mlp_in, w_mlp_out, loss_target, m_norm1_g, m_w_in, m_q_norm_g, m_k_norm_g, m_ssm_lambda_re, m_ssm_lambda_im, m_ssm_log_dt, m_ssm_b_re, m_ssm_b_im, m_ssm_c_re, m_ssm_c_im, m_ssm_d, m_w_glu, m_b_glu, m_attn_out_g, m_ssm_out_g, m_w_out, m_norm2_g, m_w_mlp_in, m_w_mlp_out, v_norm1_g, v_w_in, v_q_norm_g, v_k_norm_g, v_ssm_lambda_re, v_ssm_lambda_im, v_ssm_log_dt, v_ssm_b_re, v_ssm_b_im, v_ssm_c_re, v_ssm_c_im, v_ssm_d, v_w_glu, v_b_glu, v_attn_out_g, v_ssm_out_g, v_w_out, v_norm2_g, v_w_mlp_in, v_w_mlp_out):
    given = dict(x=x, norm1_g=norm1_g, w_in=w_in, q_norm_g=q_norm_g, k_norm_g=k_norm_g, ssm_lambda_re=ssm_lambda_re, ssm_lambda_im=ssm_lambda_im, ssm_log_dt=ssm_log_dt, ssm_b_re=ssm_b_re, ssm_b_im=ssm_b_im, ssm_c_re=ssm_c_re, ssm_c_im=ssm_c_im, ssm_d=ssm_d, w_glu=w_glu, b_glu=b_glu, attn_out_g=attn_out_g, ssm_out_g=ssm_out_g, w_out=w_out, norm2_g=norm2_g, w_mlp_in=w_mlp_in, w_mlp_out=w_mlp_out, loss_target=loss_target, m_norm1_g=m_norm1_g, m_w_in=m_w_in, m_q_norm_g=m_q_norm_g, m_k_norm_g=m_k_norm_g, m_ssm_lambda_re=m_ssm_lambda_re, m_ssm_lambda_im=m_ssm_lambda_im, m_ssm_log_dt=m_ssm_log_dt, m_ssm_b_re=m_ssm_b_re, m_ssm_b_im=m_ssm_b_im, m_ssm_c_re=m_ssm_c_re, m_ssm_c_im=m_ssm_c_im, m_ssm_d=m_ssm_d, m_w_glu=m_w_glu, m_b_glu=m_b_glu, m_attn_out_g=m_attn_out_g, m_ssm_out_g=m_ssm_out_g, m_w_out=m_w_out, m_norm2_g=m_norm2_g, m_w_mlp_in=m_w_mlp_in, m_w_mlp_out=m_w_mlp_out, v_norm1_g=v_norm1_g, v_w_in=v_w_in, v_q_norm_g=v_q_norm_g, v_k_norm_g=v_k_norm_g, v_ssm_lambda_re=v_ssm_lambda_re, v_ssm_lambda_im=v_ssm_lambda_im, v_ssm_log_dt=v_ssm_log_dt, v_ssm_b_re=v_ssm_b_re, v_ssm_b_im=v_ssm_b_im, v_ssm_c_re=v_ssm_c_re, v_ssm_c_im=v_ssm_c_im, v_ssm_d=v_ssm_d, v_w_glu=v_w_glu, v_b_glu=v_b_glu, v_attn_out_g=v_attn_out_g, v_ssm_out_g=v_ssm_out_g, v_w_out=v_w_out, v_norm2_g=v_norm2_g, v_w_mlp_in=v_w_mlp_in, v_w_mlp_out=v_w_mlp_out)
    weights = {n: given[n] for n in TWIN_WEIGHTS}
    shared = {n: given[n] for n in SHARED_INPUTS}
    per_example = {n: given[n] for n in ['x']}
    grad_fn = _jax.value_and_grad(_loss, argnums=(0, 1))

    def one_microbatch(ex, loss_target):
        ex = dict(ex)
        diff = ex.pop(TWIN_DIFF_INPUT)
        return grad_fn(weights, diff, {**shared, **ex}, loss_target)

    if N_MICROBATCH == 1:
        loss, (grad_w, grad_x) = one_microbatch(per_example, given["loss_target"])
    else:
        def body(carry, xs):
            loss_sum, grad_sum = carry
            l_k, (gw_k, gx_k) = one_microbatch(xs[0], xs[1])
            with _jax.named_scope("update"):
                return (loss_sum + l_k, _jax.tree.map(_jnp.add, grad_sum, gw_k)), gx_k

        init = (_jnp.zeros((), _jnp.float32), _jax.tree.map(_jnp.zeros_like, weights))
        (loss, grad_w), grad_x = _jax.lax.scan(body, init, (per_example, given["loss_target"]))
    with _jax.named_scope("update"):
        delta_w, new_m, new_v = {}, {}, {}
        for n in TWIN_WEIGHTS:
            delta_w[n], new_m[n], new_v[n] = _adamw(weights[n], grad_w[n], given["m_" + n], given["v_" + n])
    return (loss, grad_x, *[grad_w[n] for n in TWIN_WEIGHTS], *[delta_w[n] for n in TWIN_WEIGHTS],
            *[new_m[n] for n in TWIN_WEIGHTS], *[new_v[n] for n in TWIN_WEIGHTS])
```

```python
import functools
import math

import jax
import jax.numpy as jnp
from jax import lax
from jax.experimental import pallas as pl
from jax.experimental.pallas import tpu as pltpu

F32 = jnp.float32
BF16 = jnp.bfloat16

EPS = 1e-6
HEAD_DIM = 64
N_HEADS = 8
SB_WIDTH = 512
SSM_WIDTH = 512
SSM_GROUP = 16
SSM_GROUPS = 32
SSM_STATE = 64
QBLOCK = 128
N_CHUNK = 8
SSM_COLS = 4
LANES = 128
N_DEV = 8

ADAM_LR = 0.001
ADAM_B1 = 0.9
ADAM_B2 = 0.999
ADAM_EPS = 1e-08
ADAM_WD = 0.01
ADAM_STEP = 10

VMEM_LIMIT = 56 * 1024 * 1024

_NT = (((1,), (1,)), ((), ()))
_NN = (((1,), (0,)), ((), ()))
_TN = (((0,), (0,)), ((), ()))


def _dot(a, b, dims=_NN):
    return lax.dot_general(a, b, dims, preferred_element_type=F32)


def _params(sem):
    return pltpu.CompilerParams(dimension_semantics=sem, vmem_limit_bytes=VMEM_LIMIT)


def _matmul(name, a, b, *, ta=False, tb=False, extras=(), epilogue=None, out_dtypes=(F32,),
            col_blocked=False, tm=512, tn=512, tk=1024):
    M, K = (a.shape[1], a.shape[0]) if ta else a.shape
    N = b.shape[0] if tb else b.shape[1]
    tm, tn, tk = min(tm, M), min(tn, N), min(tk, K)
    assert M % tm == 0 and N % tn == 0 and K % tk == 0, (name, M, N, K)
    nk = K // tk
    n_ex, n_out = len(extras), len(out_dtypes)
    dims = (((0 if ta else 1,), (1 if tb else 0,)), ((), ()))

    def body(*refs):
        a_ref, b_ref = refs[0], refs[1]
        ex_refs = refs[2:2 + n_ex]
        o_refs = refs[2 + n_ex:2 + n_ex + n_out]
        acc_ref = refs[-1]
        k = pl.program_id(2)

        @pl.when(k == 0)
        def _():
            acc_ref[...] = jnp.zeros_like(acc_ref)

        acc_ref[...] += _dot(a_ref[...].astype(BF16), b_ref[...].astype(BF16), dims)

        @pl.when(k == nk - 1)
        def _():
            acc = acc_ref[...]
            outs = (acc,) if epilogue is None else epilogue(acc, *[e[...] for e in ex_refs])
            for o_ref, o in zip(o_refs, outs):
                o_ref[...] = o.astype(o_ref.dtype)

    a_spec = pl.BlockSpec((tk, tm), lambda i, j, k: (k, i)) if ta else pl.BlockSpec((tm, tk), lambda i, j, k: (i, k))
    b_spec = pl.BlockSpec((tn, tk), lambda i, j, k: (j, k)) if tb else pl.BlockSpec((tk, tn), lambda i, j, k: (k, j))
    ex_specs = [pl.BlockSpec((tm, tn), lambda i, j, k: (i, j)) for _ in extras]
    if col_blocked:
        out_specs = [pl.BlockSpec((None, tm, tn), lambda i, j, k: (j, i, 0)) for _ in out_dtypes]
        out_shape = [jax.ShapeDtypeStruct((N // tn, M, tn), dt) for dt in out_dtypes]
    else:
        out_specs = [pl.BlockSpec((tm, tn), lambda i, j, k: (i, j)) for _ in out_dtypes]
        out_shape = [jax.ShapeDtypeStruct((M, N), dt) for dt in out_dtypes]
    outs = pl.pallas_call(
        body, name=name, grid=(M // tm, N // tn, nk),
        in_specs=[a_spec, b_spec, *ex_specs], out_specs=out_specs, out_shape=out_shape,
        scratch_shapes=[pltpu.VMEM((tm, tn), F32)],
        compiler_params=_params(("parallel", "parallel", "arbitrary")),
    )(a, b, *extras)
    return outs[0] if n_out == 1 else outs


def _rowwise(name, fn, rows, small, outs, sums=(), tile=256):
    specs, args = [], []
    T = None
    for r in rows:
        arr, cb, w = r if isinstance(r, tuple) else (r, 0, r.shape[1])
        T = arr.shape[0]
        specs.append((w, cb))
        args.append(arr)
    tile = min(tile, T)
    assert T % tile == 0
    n_r, n_s, n_o, n_a = len(rows), len(small), len(outs), len(sums)

    def body(*refs):
        r_refs = refs[:n_r]
        s_refs = refs[n_r:n_r + n_s]
        o_refs = refs[n_r + n_s:n_r + n_s + n_o]
        a_refs = refs[n_r + n_s + n_o:]
        res = fn(*[r[...] for r in r_refs], *[s[...] for s in s_refs])
        res = res if isinstance(res, (tuple, list)) else (res,)
        for o_ref, o in zip(o_refs, res[:n_o]):
            o_ref[...] = o.astype(o_ref.dtype)

        @pl.when(pl.program_id(0) == 0)
        def _():
            for a_ref in a_refs:
                a_ref[...] = jnp.zeros_like(a_ref)

        for a_ref, v in zip(a_refs, res[n_o:]):
            a_ref[...] += v.astype(F32)

    in_specs = [pl.BlockSpec((tile, w), functools.partial(lambda i, cb: (i, cb), cb=cb)) for w, cb in specs]
    in_specs += [pl.BlockSpec(s.shape, functools.partial(lambda i, nd: (0,) * nd, nd=s.ndim)) for s in small]
    out_specs = [pl.BlockSpec((tile, w), lambda i: (i, 0)) for w, _ in outs]
    out_specs += [pl.BlockSpec(s, functools.partial(lambda i, nd: (0,) * nd, nd=len(s))) for s in sums]
    out_shape = [jax.ShapeDtypeStruct((T, w), dt) for w, dt in outs]
    out_shape += [jax.ShapeDtypeStruct(s, F32) for s in sums]
    res = pl.pallas_call(
        body, name=name, grid=(T // tile,), in_specs=in_specs, out_specs=out_specs, out_shape=out_shape,
        compiler_params=_params(("arbitrary",)),
    )(*args, *small)
    return res[0] if len(res) == 1 else res


def _rms(x, g):
    return x * lax.rsqrt(jnp.mean(x * x, axis=-1, keepdims=True) + EPS) * g


def _glu_branch(y, pre, b_glu, g_out):
    g = jax.nn.gelu(y)
    return _rms(g * jax.nn.sigmoid(pre + b_glu), g_out)


def _split_dot(x, tri_bf):
    hi = x.astype(BF16)
    lo = (x - hi.astype(F32)).astype(BF16)
    return _dot(hi, tri_bf) + _dot(lo, tri_bf)


def _softplus(z):
    return jnp.maximum(z, 0.0) + jnp.log1p(jnp.exp(-jnp.abs(z)))


def _attn_norm_to_scratch(q_ref, k_ref, v_ref, gq_ref, gk_ref, qn_s, kn_s, v_s, L):
    scale = 1.0 / math.sqrt(HEAD_DIM)
    for h in range(2):
        sl = slice(h * HEAD_DIM, (h + 1) * HEAD_DIM)

        def chunk(i, _):
            r = pl.ds(pl.multiple_of(i * QBLOCK, QBLOCK), QBLOCK)
            qn_s[h, r, :] = (_rms(q_ref[r, sl], gq_ref[:, sl]) * scale).astype(qn_s.dtype)
            kn_s[h, r, :] = _rms(k_ref[r, sl], gk_ref[:, sl]).astype(kn_s.dtype)
            v_s[h, r, :] = v_ref[r, sl].astype(v_s.dtype)
            return 0

        lax.fori_loop(0, L // QBLOCK, chunk, 0)


def _attn_fwd(proj, gq2, gk2, B, L):
    nq = L // QBLOCK
    n_hp = N_HEADS // 2

    def body(q_ref, k_ref, v_ref, gq_ref, gk_ref, o_ref, rt_ref, qn_s, kn_s, v_s):
        _attn_norm_to_scratch(q_ref, k_ref, v_ref, gq_ref, gk_ref, qn_s, kn_s, v_s, L)
        row = lax.broadcasted_iota(jnp.int32, (QBLOCK, QBLOCK), 0)
        col = lax.broadcasted_iota(jnp.int32, (QBLOCK, QBLOCK), 1)
        causal = col < row
        after = (row > col).astype(BF16)

        for h in range(2):
            sl = slice(h * HEAD_DIM, (h + 1) * HEAD_DIM)

            def q_block(i, _):
                rq = pl.ds(pl.multiple_of(i * QBLOCK, QBLOCK), QBLOCK)
                q_i = qn_s[h, rq, :]

                def k_block(n, carry):
                    right, acc = carry
                    kb = i - n
                    rk = pl.ds(pl.multiple_of(kb * QBLOCK, QBLOCK), QBLOCK)
                    z = _dot(q_i, kn_s[h, rk, :], _NT)
                    valid = col < row + jnp.where(kb < i, QBLOCK, 0)
                    sp = _softplus(z)
                    lom = jnp.where(valid, -sp, 0.0)
                    tail = _split_dot(lom, after) + right
                    a = jnp.where(valid, jnp.exp(z - sp + tail), 0.0)
                    acc = acc + _dot(a.astype(v_s.dtype), v_s[h, rk, :])
                    return right + jnp.sum(lom, axis=1, keepdims=True), acc

                right, acc = lax.fori_loop(
                    0, i + 1, k_block, (jnp.zeros((QBLOCK, 1), F32), jnp.zeros((QBLOCK, HEAD_DIM), F32)))
                o_ref[rq, sl] = acc
                rt_ref[rq, sl] = jnp.broadcast_to(right, (QBLOCK, HEAD_DIM))
                return 0

            lax.fori_loop(0, nq, q_block, 0)

    blk = lambda off: pl.BlockSpec((L, LANES), functools.partial(lambda b, p, off: (b, off + p), off=off))
    gspec = pl.BlockSpec((1, LANES), lambda b, p: (0, 0))
    ospec = pl.BlockSpec((L, LANES), lambda b, p: (b, p))
    return pl.pallas_call(
        body, name="attn_fwd", grid=(B, n_hp),
        in_specs=[blk(0), blk(n_hp), blk(2 * n_hp), gspec, gspec],
        out_specs=[ospec, ospec],
        out_shape=[jax.ShapeDtypeStruct((B * L, SB_WIDTH), F32)] * 2,
        scratch_shapes=[pltpu.VMEM((2, L, HEAD_DIM), BF16)] * 3,
        compiler_params=_params(("parallel", "parallel")),
    )(proj, proj, proj, gq2, gk2)


def _attn_bwd(proj, gq2, gk2, rtot, d_sb, B, L):
    nq = L // QBLOCK
    n_hp = N_HEADS // 2
    scale = 1.0 / math.sqrt(HEAD_DIM)

    def body(q_ref, k_ref, v_ref, gq_ref, gk_ref, rt_ref, do_ref, dq_ref, dk_ref, dv_ref, dgq_ref, dgk_ref,
             qn_s, kn_s, v_s, dqn_s, dkn_s, dv_s):
        _attn_norm_to_scratch(q_ref, k_ref, v_ref, gq_ref, gk_ref, qn_s, kn_s, v_s, L)
        dkn_s[...] = jnp.zeros_like(dkn_s)
        dv_s[...] = jnp.zeros_like(dv_s)
        row = lax.broadcasted_iota(jnp.int32, (QBLOCK, QBLOCK), 0)
        col = lax.broadcasted_iota(jnp.int32, (QBLOCK, QBLOCK), 1)
        causal = col < row
        after = (row > col).astype(BF16)
        before = (row < col).astype(BF16)

        @pl.when(jnp.logical_and(pl.program_id(0) == 0, pl.program_id(1) == 0))
        def _():
            dgq_ref[...] = jnp.zeros_like(dgq_ref)
            dgk_ref[...] = jnp.zeros_like(dgk_ref)

        for h in range(2):
            sl = slice(h * HEAD_DIM, (h + 1) * HEAD_DIM)

            def q_block(i, _):
                rq = pl.ds(pl.multiple_of(i * QBLOCK, QBLOCK), QBLOCK)
                q_i = qn_s[h, rq, :]
                do_i = do_ref[rq, sl].astype(v_s.dtype)
                total = rt_ref[rq, sl][:, 0:1]

                def k_block(kb, carry):
                    lom_left, dla_left, dq_acc = carry
                    rk = pl.ds(pl.multiple_of(kb * QBLOCK, QBLOCK), QBLOCK)
                    k_b = kn_s[h, rk, :]
                    v_b = v_s[h, rk, :]
                    z = _dot(q_i, k_b, _NT)
                    valid = col < row + jnp.where(kb < i, QBLOCK, 0)
                    sp = _softplus(z)
                    lom = jnp.where(valid, -sp, 0.0)
                    lom_sum = jnp.sum(lom, axis=1, keepdims=True)
                    right = total - lom_left - lom_sum
                    tail = _split_dot(lom, after) + right
                    a = jnp.where(valid, jnp.exp(z - sp + tail), 0.0)
                    da = _dot(do_i, v_b, _NT)
                    dla = a * da
                    d_lom = dla_left + _split_dot(dla, before)
                    beta = jnp.exp(z - sp)
                    dz = dla * (1.0 - beta) - jnp.where(valid, beta * d_lom, 0.0)
                    dz_b = dz.astype(v_s.dtype)
                    dq_acc = dq_acc + _dot(dz_b, k_b)
                    dkn_s[h, rk, :] += _dot(dz_b, q_i, _TN)
                    dv_s[h, rk, :] += _dot(a.astype(v_s.dtype), do_i, _TN)
                    return lom_left + lom_sum, dla_left + jnp.sum(dla, axis=1, keepdims=True), dq_acc

                zero = jnp.zeros((QBLOCK, 1), F32)
                _, _, dq_acc = lax.fori_loop(0, i + 1, k_block, (zero, zero, jnp.zeros((QBLOCK, HEAD_DIM), F32)))
                dqn_s[h, rq, :] = dq_acc * scale
                return 0

            lax.fori_loop(0, nq, q_block, 0)

            def norm_bwd(i, carry):
                dgq, dgk = carry
                r = pl.ds(pl.multiple_of(i * QBLOCK, QBLOCK), QBLOCK)
                _, vjp_q = jax.vjp(_rms, q_ref[r, sl], gq_ref[:, sl])
                dq, dgq_i = vjp_q(dqn_s[h, r, :])
                _, vjp_k = jax.vjp(_rms, k_ref[r, sl], gk_ref[:, sl])
                dk, dgk_i = vjp_k(dkn_s[h, r, :])
                dq_ref[r, sl] = dq
                dk_ref[r, sl] = dk
                dv_ref[r, sl] = dv_s[h, r, :]
                return dgq + dgq_i, dgk + dgk_i

            zg = jnp.zeros((1, HEAD_DIM), F32)
            dgq, dgk = lax.fori_loop(0, nq, norm_bwd, (zg, zg))
            dgq_ref[:, 0:HEAD_DIM] += dgq
            dgk_ref[:, 0:HEAD_DIM] += dgk

    blk = lambda off: pl.BlockSpec((L, LANES), functools.partial(lambda b, p, off: (b, off + p), off=off))
    gspec = pl.BlockSpec((1, LANES), lambda b, p: (0, 0))
    ospec = pl.BlockSpec((L, LANES), lambda b, p: (b, p))
    return pl.pallas_call(
        body, name="attn_bwd", grid=(B, n_hp),
        in_specs=[blk(0), blk(n_hp), blk(2 * n_hp), gspec, gspec, ospec, ospec],
        out_specs=[ospec, ospec, ospec, gspec, gspec],
        out_shape=[jax.ShapeDtypeStruct((B * L, SB_WIDTH), F32)] * 3 + [jax.ShapeDtypeStruct((1, LANES), F32)] * 2,
        scratch_shapes=[pltpu.VMEM((2, L, HEAD_DIM), BF16)] * 3 + [pltpu.VMEM((2, L, HEAD_DIM), F32)] * 3,
        compiler_params=_params(("arbitrary", "arbitrary")),
    )(proj, proj, proj, gq2, gk2, rtot, d_sb)


def _ssm_discretise(lam_re, lam_im, log_dt, b_re, b_im):
    dt = jnp.exp(log_dt)
    mag = jnp.exp(lam_re * dt)
    lbr = mag * jnp.cos(lam_im * dt)
    lbi = mag * jnp.sin(lam_im * dt)
    den = lam_re * lam_re + lam_im * lam_im
    nr, ni = lbr - 1.0, lbi
    cr = (nr * lam_re + ni * lam_im) / den
    ci = (ni * lam_re - nr * lam_im) / den
    return lbr, lbi, cr * b_re - ci * b_im, cr * b_im + ci * b_re


def _ssm_prep(lam_re, lam_im, log_dt, b_re_t, b_im_t):
    def body(lr, li, ld, br, bi, o_lr, o_li, o_br, o_bi):
        res = _ssm_discretise(lr[...], li[...], ld[...], br[...], bi[...])
        for o, v in zip((o_lr, o_li, o_br, o_bi), res):
            o[...] = v

    return pl.pallas_call(
        body, name="ssm_prep",
        out_shape=[jax.ShapeDtypeStruct(lam_re.shape, F32)] * 2 + [jax.ShapeDtypeStruct(b_re_t.shape, F32)] * 2,
    )(lam_re, lam_im, log_dt, b_re_t, b_im_t)


def _ssm_prep_bwd(lam_re, lam_im, log_dt, b_re_t, b_im_t, d_lr, d_li, d_br, d_bi):
    def body(lr, li, ld, br, bi, g_lr, g_li, g_br, g_bi, o_lr, o_li, o_ld, o_br, o_bi):
        _, vjp = jax.vjp(_ssm_discretise, lr[...], li[...], ld[...], br[...], bi[...])
        res = vjp((g_lr[...], g_li[...], g_br[...], g_bi[...]))
        for o, v in zip((o_lr, o_li, o_ld, o_br, o_bi), res):
            o[...] = v

    return pl.pallas_call(
        body, name="ssm_prep_bwd",
        out_shape=[jax.ShapeDtypeStruct(lam_re.shape, F32)] * 2 + [jax.ShapeDtypeStruct(log_dt.shape, F32)]
        + [jax.ShapeDtypeStruct(b_re_t.shape, F32)] * 2,
    )(lam_re, lam_im, log_dt, b_re_t, b_im_t, d_lr, d_li, d_br, d_bi)


def _block_diag(m):
    m4 = m.reshape(SSM_COLS, 8, SSM_GROUP, SSM_STATE)
    return jnp.einsum("aghp,gk->aghkp", m4, jnp.eye(8, dtype=m.dtype)).reshape(SSM_COLS, LANES, 512)


def _block_diag_take(d):
    d6 = d.reshape(SSM_COLS, 8, SSM_GROUP, 2, 8, SSM_STATE)
    return jnp.einsum("aghrgp->raghp", d6).reshape(2, SSM_GROUPS, SSM_GROUP, SSM_STATE)


def _cmul(ar, ai, br, bi):
    return ar * br - ai * bi, ar * bi + ai * br


def _power(lr, li, n):
    assert n & (n - 1) == 0
    for _ in range(n.bit_length() - 1):
        lr, li = _cmul(lr, li, lr, li)
    return lr, li


def _ssm_fwd(u_p, w_b, lam_r, lam_i, c_m, d_skip, B, L, tj):
    J = L // N_CHUNK
    njt = J // tj
    R = tj * N_CHUNK
    H = 512

    def body(u_ref, wb_ref, lr_ref, li_ref, cm_ref, d_ref, y_ref, x_ref, xin_ref, bu_s, st_s, xin_s):
        ph, jt = pl.program_id(2), pl.program_id(3)
        lr, li = lr_ref[...], li_ref[...]

        @pl.when(jnp.logical_and(ph == 0, jt == 0))
        def _():
            st_s[...] = jnp.zeros_like(st_s)

        bu_s[...] = _dot(u_ref[...].astype(BF16), wb_ref[...].astype(BF16))

        def scan(store):
            def step(j, carry):
                xr, xi = carry
                r = pl.ds(pl.multiple_of(j * N_CHUNK, N_CHUNK), N_CHUNK)
                nr = lr * xr - li * xi + bu_s[r, 0:H]
                ni = lr * xi + li * xr + bu_s[r, H:2 * H]
                if store:
                    x_ref[r, 0:H] = nr
                    x_ref[r, H:2 * H] = ni
                return nr, ni

            xr, xi = lax.fori_loop(0, tj, step, (st_s[:, 0:H], st_s[:, H:2 * H]))
            st_s[:, 0:H] = xr
            st_s[:, H:2 * H] = xi

        @pl.when(ph == 0)
        def _():
            scan(False)

            @pl.when(jt == njt - 1)
            def _():
                pr, pi = _power(lr[0:1], li[0:1], J)
                xin_s[0:1, :] = jnp.zeros((1, 2 * H), F32)
                for c in range(1, N_CHUNK):
                    qr, qi = _cmul(pr, pi, xin_s[c - 1:c, 0:H], xin_s[c - 1:c, H:2 * H])
                    xin_s[c:c + 1, 0:H] = qr + st_s[c - 1:c, 0:H]
                    xin_s[c:c + 1, H:2 * H] = qi + st_s[c - 1:c, H:2 * H]
                xin_ref[...] = xin_s[...]
                st_s[...] = xin_s[...]

        @pl.when(ph == 1)
        def _():
            scan(True)
            y = _dot(x_ref[...].astype(BF16), cm_ref[...].astype(BF16))
            y_ref[...] = y + d_ref[...] * u_ref[...]

    return pl.pallas_call(
        body, name="ssm_fwd", grid=(SSM_COLS, B, 2, njt),
        in_specs=[
            pl.BlockSpec((None, R, LANES), lambda i, b, ph, jt: (b, jt, i)),
            pl.BlockSpec((None, LANES, 2 * H), lambda i, b, ph, jt: (i, 0, 0)),
            pl.BlockSpec((None, N_CHUNK, H), lambda i, b, ph, jt: (i, 0, 0)),
            pl.BlockSpec((None, N_CHUNK, H), lambda i, b, ph, jt: (i, 0, 0)),
            pl.BlockSpec((None, 2 * H, LANES), lambda i, b, ph, jt: (i, 0, 0)),
            pl.BlockSpec((1, LANES), lambda i, b, ph, jt: (0, i)),
        ],
        out_specs=[
            pl.BlockSpec((None, R, LANES), lambda i, b, ph, jt: (b, jt * ph, i)),
            pl.BlockSpec((None, R, 2 * H), lambda i, b, ph, jt: (b, jt * ph, i)),
            pl.BlockSpec((None, None, N_CHUNK, 2 * H), lambda i, b, ph, jt: (b, i, 0, 0)),
        ],
        out_shape=[
            jax.ShapeDtypeStruct((B, L, SSM_WIDTH), F32),
            jax.ShapeDtypeStruct((B, L, SSM_COLS * 2 * H), F32),
            jax.ShapeDtypeStruct((B, SSM_COLS, N_CHUNK, 2 * H), F32),
        ],
        scratch_shapes=[pltpu.VMEM((R, 2 * H), F32), pltpu.VMEM((N_CHUNK, 2 * H), F32), pltpu.VMEM((N_CHUNK, 2 * H), F32)],
        compiler_params=_params(("arbitrary",) * 4),
    )(u_p, w_b, lam_r, lam_i, c_m, d_skip)


def _ssm_bwd(dy_p, u_p, x, xin, w_bt, lam_r, lam_i, c_mt, d_skip, B, L, tj):
    J = L // N_CHUNK
    njt = J // tj
    R = tj * N_CHUNK
    H = 512
    x4 = x.reshape(B, J, N_CHUNK, SSM_COLS * 2 * H)

    def body(dy_ref, u_ref, x_ref, xp_ref, xin_ref, wbt_ref, lr_ref, li_ref, cmt_ref, d_ref,
             du_ref, dwb_ref, dcm_ref, dlr_ref, dli_ref, dd_ref, ca_s, a_s, st_s, dl_s):
        b, ph, jt = pl.program_id(1), pl.program_id(2), pl.program_id(3)
        jr = njt - 1 - jt
        lr, li = lr_ref[...], -li_ref[...]

        @pl.when(jnp.logical_and(b == 0, jnp.logical_and(ph == 0, jt == 0)))
        def _():
            dwb_ref[...] = jnp.zeros_like(dwb_ref)
            dcm_ref[...] = jnp.zeros_like(dcm_ref)
            dlr_ref[...] = jnp.zeros_like(dlr_ref)
            dli_ref[...] = jnp.zeros_like(dli_ref)
            dd_ref[...] = jnp.zeros_like(dd_ref)
            dl_s[...] = jnp.zeros_like(dl_s)

        @pl.when(jnp.logical_and(ph == 0, jt == 0))
        def _():
            st_s[...] = jnp.zeros_like(st_s)

        ca_s[...] = _dot(dy_ref[...].astype(BF16), cmt_ref[...].astype(BF16))

        def scan(store):
            def step(n, carry):
                ar, ai = carry
                r = pl.ds(pl.multiple_of((tj - 1 - n) * N_CHUNK, N_CHUNK), N_CHUNK)
                nr = lr * ar - li * ai + ca_s[r, 0:H]
                ni = lr * ai + li * ar + ca_s[r, H:2 * H]
                if store:
                    a_s[r, 0:H] = nr
                    a_s[r, H:2 * H] = ni
                return nr, ni

            ar, ai = lax.fori_loop(0, tj, step, (st_s[:, 0:H], st_s[:, H:2 * H]))
            st_s[:, 0:H] = ar
            st_s[:, H:2 * H] = ai

        @pl.when(ph == 0)
        def _():
            scan(False)

            @pl.when(jt == njt - 1)
            def _():
                pr, pi = _power(lr[0:1], li[0:1], J)
                a_s[N_CHUNK - 1:N_CHUNK, :] = jnp.zeros((1, 2 * H), F32)
                for c in range(N_CHUNK - 2, -1, -1):
                    qr, qi = _cmul(pr, pi, a_s[c + 1:c + 2, 0:H], a_s[c + 1:c + 2, H:2 * H])
                    a_s[c:c + 1, 0:H] = qr + st_s[c + 1:c + 2, 0:H]
                    a_s[c:c + 1, H:2 * H] = qi + st_s[c + 1:c + 2, H:2 * H]
                st_s[...] = a_s[0:N_CHUNK, :]

        @pl.when(ph == 1)
        def _():
            scan(True)
            dy = dy_ref[...]
            u = u_ref[...]
            a_b = a_s[...].astype(BF16)
            du_ref[...] = _dot(a_b, wbt_ref[...].astype(BF16)) + d_ref[...] * dy
            dwb_ref[...] += _dot(u.astype(BF16), a_b, _TN)
            dcm_ref[...] += _dot(x_ref[...].astype(BF16), dy.astype(BF16), _TN)
            dd_ref[...] += jnp.sum(dy * u, axis=0, keepdims=True)

            first = jnp.where(jr == 0, xin_ref[...], xp_ref[...])
            a0r, a0i = a_s[0:N_CHUNK, 0:H], a_s[0:N_CHUNK, H:2 * H]
            acc0 = (a0r * first[:, 0:H] + a0i * first[:, H:2 * H], a0i * first[:, 0:H] - a0r * first[:, H:2 * H])

            def step(j, carry):
                sr, si = carry
                r = pl.ds(pl.multiple_of(j * N_CHUNK, N_CHUNK), N_CHUNK)
                rp = pl.ds(pl.multiple_of((j - 1) * N_CHUNK, N_CHUNK), N_CHUNK)
                ar, ai = a_s[r, 0:H], a_s[r, H:2 * H]
                xr, xi = x_ref[rp, 0:H], x_ref[rp, H:2 * H]
                return sr + ar * xr + ai * xi, si + ai * xr - ar * xi

            sr, si = lax.fori_loop(1, tj, step, acc0)
            dl_s[:, 0:H] += sr
            dl_s[:, H:2 * H] += si

            @pl.when(jnp.logical_and(b == B - 1, jt == njt - 1))
            def _():
                dlr_ref[...] = jnp.sum(dl_s[:, 0:H], axis=0, keepdims=True)
                dli_ref[...] = jnp.sum(dl_s[:, H:2 * H], axis=0, keepdims=True)
                dl_s[...] = jnp.zeros_like(dl_s)

    rev = lambda ph, jt: (njt - 1 - jt) * ph + (njt - 1) * (1 - ph)
    return pl.pallas_call(
        body, name="ssm_bwd", grid=(SSM_COLS, B, 2, njt),
        in_specs=[
            pl.BlockSpec((None, R, LANES), lambda i, b, ph, jt: (b, njt - 1 - jt, i)),
            pl.BlockSpec((None, R, LANES), lambda i, b, ph, jt: (b, njt - 1 - jt, i)),
            pl.BlockSpec((None, R, 2 * H), lambda i, b, ph, jt: (b, rev(ph, jt), i)),
            pl.BlockSpec((None, None, N_CHUNK, 2 * H),
                         lambda i, b, ph, jt: (b, jnp.maximum((njt - 1 - jt) * tj - 1, 0), 0, i)),
            pl.BlockSpec((None, None, N_CHUNK, 2 * H), lambda i, b, ph, jt: (b, i, 0, 0)),
            pl.BlockSpec((None, 2 * H, LANES), lambda i, b, ph, jt: (i, 0, 0)),
            pl.BlockSpec((None, N_CHUNK, H), lambda i, b, ph, jt: (i, 0, 0)),
            pl.BlockSpec((None, N_CHUNK, H), lambda i, b, ph, jt: (i, 0, 0)),
            pl.BlockSpec((None, LANES, 2 * H), lambda i, b, ph, jt: (i, 0, 0)),
            pl.BlockSpec((1, LANES), lambda i, b, ph, jt: (0, i)),
        ],
        out_specs=[
            pl.BlockSpec((None, R, LANES), lambda i, b, ph, jt: (b, rev(ph, jt), i)),
            pl.BlockSpec((None, LANES, 2 * H), lambda i, b, ph, jt: (i, 0, 0)),
            pl.BlockSpec((None, 2 * H, LANES), lambda i, b, ph, jt: (i, 0, 0)),
            pl.BlockSpec((None, 1, H), lambda i, b, ph, jt: (i, 0, 0)),
            pl.BlockSpec((None, 1, H), lambda i, b, ph, jt: (i, 0, 0)),
            pl.BlockSpec((1, LANES), lambda i, b, ph, jt: (0, i)),
        ],
        out_shape=[
            jax.ShapeDtypeStruct((B, L, SSM_WIDTH), F32),
            jax.ShapeDtypeStruct((SSM_COLS, LANES, 2 * H), F32),
            jax.ShapeDtypeStruct((SSM_COLS, 2 * H, LANES), F32),
            jax.ShapeDtypeStruct((SSM_COLS, 1, H), F32),
            jax.ShapeDtypeStruct((SSM_COLS, 1, H), F32),
            jax.ShapeDtypeStruct((1, SSM_WIDTH), F32),
        ],
        scratch_shapes=[pltpu.VMEM((R, 2 * H), F32), pltpu.VMEM((R, 2 * H), F32),
                        pltpu.VMEM((N_CHUNK, 2 * H), F32), pltpu.VMEM((N_CHUNK, 2 * H), F32)],
        compiler_params=_params(("arbitrary",) * 4),
    )(dy_p, u_p, x, x4, xin, w_bt, lam_r, lam_i, c_mt, d_skip)


def _to_scan_layout(t, B, L):
    C = t.shape[-1]
    return t.reshape(B, N_CHUNK, L // N_CHUNK, C).transpose(0, 2, 1, 3).reshape(B, L, C)


def _from_scan_layout(t, B, L):
    C = t.shape[-1]
    return t.reshape(B, L // N_CHUNK, N_CHUNK, C).transpose(0, 2, 1, 3).reshape(B * L, C)


def _local_step(x, target, p, w_in, w_glu, w_out, w_mlp_in, w_mlp_out, *, ssm_tile=32):
    B, L, D = x.shape
    T = B * L
    x2 = x.reshape(T, D)
    row = lambda v: v.reshape(1, -1)
    g1, g2, ga, gs, b_glu = row(p["norm1_g"]), row(p["norm2_g"]), row(p["attn_out_g"]), row(p["ssm_out_g"]), row(p["b_glu"])
    gq2 = jnp.tile(row(p["q_norm_g"]), (1, 2))
    gk2 = jnp.tile(row(p["k_norm_g"]), (1, 2))

    G, P, Hh = SSM_GROUPS, SSM_STATE, SSM_GROUP
    lam_re3, lam_im3 = p["ssm_lambda_re"].reshape(G, 1, P), p["ssm_lambda_im"].reshape(G, 1, P)
    log_dt3 = p["ssm_log_dt"].reshape(G, 1, 1)
    b_re_t, b_im_t = p["ssm_b_re"].transpose(0, 2, 1), p["ssm_b_im"].transpose(0, 2, 1)
    lbr, lbi, bbr, bbi = _ssm_prep(lam_re3, lam_im3, log_dt3, b_re_t, b_im_t)
    w_b = jnp.concatenate([_block_diag(bbr), _block_diag(bbi)], axis=2)
    c_mt = jnp.concatenate([_block_diag(p["ssm_c_re"]), -_block_diag(p["ssm_c_im"])], axis=2)
    w_bt, c_m = w_b.transpose(0, 2, 1), c_mt.transpose(0, 2, 1)
    lam_r = jnp.broadcast_to(lbr.reshape(SSM_COLS, 1, 512), (SSM_COLS, N_CHUNK, 512))
    lam_i = jnp.broadcast_to(lbi.reshape(SSM_COLS, 1, 512), (SSM_COLS, N_CHUNK, 512))
    d_skip = p["ssm_d"].reshape(1, SSM_WIDTH)

    xn = _rowwise("norm1", _rms, [x2], [g1], [(D, BF16)])
    proj = _matmul("proj", xn, w_in)
    sb, rtot = _attn_fwd(proj, gq2, gk2, B, L)
    u_p = _to_scan_layout(proj[:, 3 * SB_WIDTH:], B, L)
    y_p, xs, xin = _ssm_fwd(u_p, w_b, lam_r, lam_i, c_m, d_skip, B, L, ssm_tile)
    y2 = y_p.reshape(T, SSM_WIDTH)
    gel = _rowwise("gelu", jax.nn.gelu, [y2], [], [(SSM_WIDTH, BF16)])
    pre = _matmul("glu_gate", gel, w_glu)
    ssm_n = _rowwise("glu_out", _glu_branch, [y2, pre], [b_glu, gs], [(SSM_WIDTH, BF16)])
    sb_n = _rowwise("attn_out_norm", _rms, [sb], [ga], [(SB_WIDTH, BF16)])
    mixed = jnp.concatenate([sb_n, _from_scan_layout(ssm_n, B, L)], axis=1)
    h1 = _matmul("out_proj", mixed, w_out, extras=[x2], epilogue=lambda acc, r: (acc + r,))
    hn = _rowwise("norm2", _rms, [h1], [g2], [(D, BF16)])
    act, a_pre = _matmul("mlp_in", hn, w_mlp_in, out_dtypes=(BF16, BF16),
                         epilogue=lambda acc: (jnp.square(jnp.maximum(acc, 0.0)), acc))
    out = _matmul("mlp_out", act, w_mlp_out, extras=[h1], epilogue=lambda acc, r: (acc + r,))

    def loss_fn(o, t):
        diff = o - t
        part = jnp.sum(jnp.sum(diff * diff, axis=0, keepdims=True), axis=1, keepdims=True)
        return diff * (1.0 / D), part * (0.5 / D)

    d_out, loss = _rowwise("loss", loss_fn, [out, target.reshape(T, D)], [], [(D, F32)], sums=[(1, 1)])

    d_apre = _matmul("mlp_out_dx", d_out, w_mlp_out, tb=True, extras=[a_pre], out_dtypes=(BF16,),
                     epilogue=lambda acc, ap: (acc * (2.0 * jnp.maximum(ap.astype(F32), 0.0)),))
    g_w_mlp_out = _matmul("mlp_out_dw", act, d_out, ta=True)
    d_hn = _matmul("mlp_in_dx", d_apre, w_mlp_in, tb=True)
    g_w_mlp_in = _matmul("mlp_in_dw", hn, d_apre, ta=True, col_blocked=True, tn=w_mlp_in.shape[1] // N_DEV)

    def norm_bwd_res(h, dy, res, g):
        _, vjp = jax.vjp(_rms, h, g)
        dh, dg = vjp(dy)
        return res + dh, dg

    d_h1, g_norm2 = _rowwise("norm2_bwd", norm_bwd_res, [h1, d_hn, d_out], [g2], [(D, F32)], sums=[(1, D)])

    d_mixed = _matmul("out_proj_dx", d_h1, w_out, tb=True)
    g_w_out = _matmul("out_proj_dw", mixed, d_h1, ta=True)

    def norm_bwd(h, dy, g):
        _, vjp = jax.vjp(_rms, h, g)
        return vjp(dy)

    d_sb, g_attn_out = _rowwise("attn_out_norm_bwd", norm_bwd, [sb, (d_mixed, 0, SB_WIDTH)], [ga],
                                [(SB_WIDTH, F32)], sums=[(1, SB_WIDTH)])
    d_ssm_n = _to_scan_layout(d_mixed[:, SB_WIDTH:], B, L).reshape(T, SSM_WIDTH)

    def glu_bwd(y, pre_, dy, bg, g):
        _, vjp = jax.vjp(_glu_branch, y, pre_, bg, g)
        d_y, d_pre, d_bg, d_g = vjp(dy)
        return d_y, d_pre, d_bg, d_g

    d_y_direct, d_pre, g_b_glu, g_ssm_out = _rowwise(
        "glu_out_bwd", glu_bwd, [y2, pre, d_ssm_n], [b_glu, gs], [(SSM_WIDTH, F32), (SSM_WIDTH, BF16)],
        sums=[(1, SSM_WIDTH), (1, SSM_WIDTH)])
    d_gel = _matmul("glu_gate_dx", d_pre, w_glu, tb=True)
    g_w_glu = _matmul("glu_gate_dw", gel, d_pre, ta=True)

    def gelu_bwd(y, dg, dy0):
        _, vjp = jax.vjp(jax.nn.gelu, y)
        return dy0 + vjp(dg)[0]

    d_y = _rowwise("gelu_bwd", gelu_bwd, [y2, d_gel, d_y_direct], [], [(SSM_WIDTH, F32)])

    du_p, d_wb, d_cm, d_lr, d_li, g_d = _ssm_bwd(
        d_y.reshape(B, L, SSM_WIDTH), u_p, xs, xin, w_bt, lam_r, lam_i, c_mt, d_skip, B, L, ssm_tile)
    d_bb = _block_diag_take(d_wb.reshape(SSM_COLS, LANES, 2, 512))
    d_c = _block_diag_take(d_cm.transpose(0, 2, 1).reshape(SSM_COLS, LANES, 2, 512))
    g_lam_re, g_lam_im, g_log_dt, g_b_re_t, g_b_im_t = _ssm_prep_bwd(
        lam_re3, lam_im3, log_dt3, b_re_t, b_im_t,
        d_lr.reshape(G, 1, P), d_li.reshape(G, 1, P), d_bb[0], d_bb[1])
    d_q, d_k, d_v, g_q, g_k = _attn_bwd(proj, gq2, gk2, rtot, d_sb, B, L)

    d_proj = jnp.concatenate([d_q, d_k, d_v, _from_scan_layout(du_p, B, L)], axis=1)
    g_w_in = _matmul("proj_dw", xn, d_proj, ta=True, col_blocked=True, tn=w_in.shape[1] // N_DEV)
    d_xn = _matmul("proj_dx", d_proj, w_in, tb=True)
    grad_x, g_norm1 = _rowwise("norm1_bwd", norm_bwd_res, [x2, d_xn, d_h1], [g1], [(D, F32)], sums=[(1, D)])

    small = {
        "norm1_g": g_norm1.reshape(-1),
        "q_norm_g": g_q[0, :HEAD_DIM],
        "k_norm_g": g_k[0, :HEAD_DIM],
        "ssm_lambda_re": g_lam_re.reshape(G, P),
        "ssm_lambda_im": g_lam_im.reshape(G, P),
        "ssm_log_dt": g_log_dt.reshape(G),
        "ssm_b_re": g_b_re_t.transpose(0, 2, 1),
        "ssm_b_im": g_b_im_t.transpose(0, 2, 1),
        "ssm_c_re": d_c[0],
        "ssm_c_im": -d_c[1],
        "ssm_d": g_d.reshape(G, Hh),
        "b_glu": g_b_glu.reshape(-1),
        "attn_out_g": g_attn_out.reshape(-1),
        "ssm_out_g": g_ssm_out.reshape(-1),
        "norm2_g": g_norm2.reshape(-1),
    }
    big = {"w_in": g_w_in, "w_glu": g_w_glu, "w_out": g_w_out, "w_mlp_in": g_w_mlp_in, "w_mlp_out": g_w_mlp_out}
    return loss[0, 0], grad_x.reshape(B, L, D), small, big


_ANY = pl.BlockSpec(memory_space=pl.ANY)
_MESH = pl.DeviceIdType.MESH


def _all_gather(name, shards):
    n = len(shards)

    def body(*refs):
        in_refs, out_refs = refs[:n], refs[n:2 * n]
        send_sems, recv_sems, local_sems = refs[2 * n:]
        x, y, c = lax.axis_index("x"), lax.axis_index("y"), lax.axis_index("c")
        me, sibling = (x, y, c), (x, y, 1 - c)
        chips = [(1 - x, y), (x, 1 - y), (1 - x, 1 - y)]

        def copy(a, k, block, to, src=None):
            px, py, pc = block
            rows = out_refs[a].at[4 * px + 2 * py + pc]
            return pltpu.make_async_remote_copy(
                src_ref=rows if src is None else src, dst_ref=rows, send_sem=send_sems.at[a, k],
                recv_sem=recv_sems.at[a, k], device_id=to, device_id_type=_MESH)

        mine = [pltpu.make_async_copy(in_refs[a], out_refs[a].at[4 * x + 2 * y + c], local_sems.at[a]) for a in range(n)]
        first, passed = [], []
        for a in range(n):
            mine[a].start()
            first.append(copy(a, 0, me, sibling, src=in_refs[a]))
            first += [copy(a, 1 + j, me, (*chip, c), src=in_refs[a]) for j, chip in enumerate(chips)]
        for cp in first:
            cp.start()
        for j, chip in enumerate(chips):
            for a in range(n):
                copy(a, 1 + j, (*chip, c), me).wait_recv()
                fwd = copy(a, 4 + j, (*chip, c), sibling)
                fwd.start()
                passed.append(fwd)
        for a in range(n):
            copy(a, 0, sibling, me).wait_recv()
            for j, chip in enumerate(chips):
                copy(a, 4 + j, (*chip, 1 - c), me).wait_recv()
        for cp in first + passed:
            cp.wait_send()
        for cp in mine:
            cp.wait()

    return pl.pallas_call(
        body, name=name,
        in_specs=[_ANY] * n, out_specs=[_ANY] * n,
        out_shape=[jax.ShapeDtypeStruct((N_DEV, *s.shape), s.dtype) for s in shards],
        scratch_shapes=[pltpu.SemaphoreType.DMA((n, 7)), pltpu.SemaphoreType.DMA((n, 7)), pltpu.SemaphoreType.DMA((n,))],
    )(*shards)


def _sibling_exchange(name, grads):
    n = len(grads)

    def body(*refs):
        g_refs, r_refs = refs[:n], refs[n:2 * n]
        send_sems, recv_sems = refs[2 * n:]
        x, y, c = lax.axis_index("x"), lax.axis_index("y"), lax.axis_index("c")
        copies = [
            pltpu.make_async_remote_copy(
                src_ref=g_refs[a].at[k, 1 - c], dst_ref=r_refs[a].at[k], send_sem=send_sems.at[a, k],
                recv_sem=recv_sems.at[a, k], device_id=(x, y, 1 - c), device_id_type=_MESH)
            for a in range(n) for k in range(4)]
        for cp in copies:
            cp.start()
        for cp in copies:
            cp.wait()

    return pl.pallas_call(
        body, name=name, in_specs=[_ANY] * n, out_specs=[_ANY] * n,
        out_shape=[jax.ShapeDtypeStruct((4, *g.shape[2:]), g.dtype) for g in grads],
        scratch_shapes=[pltpu.SemaphoreType.DMA((n, 4)), pltpu.SemaphoreType.DMA((n, 4))],
    )(*grads)


def _chip_exchange(name, parts):
    n = len(parts)

    def body(*refs):
        p_refs, r_refs = refs[:n], refs[n:2 * n]
        send_sems, recv_sems = refs[2 * n:]
        x, y, c = lax.axis_index("x"), lax.axis_index("y"), lax.axis_index("c")
        chips = [(1 - x, y), (x, 1 - y), (1 - x, 1 - y)]
        copies = [
            pltpu.make_async_remote_copy(
                src_ref=p_refs[a].at[2 * px + py], dst_ref=r_refs[a].at[j], send_sem=send_sems.at[a, j],
                recv_sem=recv_sems.at[a, j], device_id=(px, py, c), device_id_type=_MESH)
            for a in range(n) for j, (px, py) in enumerate(chips)]
        for cp in copies:
            cp.start()
        for cp in copies:
            cp.wait()

    return pl.pallas_call(
        body, name=name, in_specs=[_ANY] * n, out_specs=[_ANY] * n,
        out_shape=[jax.ShapeDtypeStruct((3, *p.shape[1:]), p.dtype) for p in parts],
        scratch_shapes=[pltpu.SemaphoreType.DMA((n, 3)), pltpu.SemaphoreType.DMA((n, 3))],
    )(*parts)


def _chip_partial(name, g4, r1, core):
    _, _, r, c = g4.shape
    tr = min(r, 256)

    def body(core_ref, g_ref, r_ref, o32_ref, o16_ref):
        s = g_ref[...] + r_ref[...]
        o32_ref[...] = s
        o16_ref[...] = s.astype(o16_ref.dtype)

    spec = pl.BlockSpec((None, tr, c), lambda k, i, core_ref: (k, i, 0))
    return pl.pallas_call(
        body, name=name,
        grid_spec=pltpu.PrefetchScalarGridSpec(
            num_scalar_prefetch=1, grid=(4, r // tr),
            in_specs=[pl.BlockSpec((None, None, tr, c), lambda k, i, core_ref: (k, core_ref[0], i, 0)), spec],
            out_specs=[spec, spec]),
        out_shape=[jax.ShapeDtypeStruct((4, r, c), F32), jax.ShapeDtypeStruct((4, r, c), BF16)],
        compiler_params=_params(("parallel", "parallel")),
    )(core, g4, r1)


def _adamw(w, g, m, v):
    m = ADAM_B1 * m + (1.0 - ADAM_B1) * g
    v = ADAM_B2 * v + (1.0 - ADAM_B2) * jnp.square(g)
    m_hat = m / (1.0 - ADAM_B1 ** ADAM_STEP)
    v_hat = v / (1.0 - ADAM_B2 ** ADAM_STEP)
    delta = -ADAM_LR * (m_hat / (jnp.sqrt(v_hat) + ADAM_EPS) + ADAM_WD * w)
    return delta, m, v


def _adamw_shard(name, p32, r2, chip, w, m, v):
    r, c = w.shape
    tr = min(r, 256)

    def body(chip_ref, p_ref, r_ref, w_ref, m_ref, v_ref, g_out, d_out, m_out, v_out):
        g = p_ref[...]
        for j in range(3):
            g = g + r_ref[j].astype(F32)
        delta, m_new, v_new = _adamw(w_ref[...], g, m_ref[...], v_ref[...])
        g_out[...] = g
        d_out[...] = delta
        m_out[...] = m_new
        v_out[...] = v_new

    spec = pl.BlockSpec((tr, c), lambda i, chip_ref: (i, 0))
    return pl.pallas_call(
        body, name=name,
        grid_spec=pltpu.PrefetchScalarGridSpec(
            num_scalar_prefetch=1, grid=(r // tr,),
            in_specs=[pl.BlockSpec((None, tr, c), lambda i, chip_ref: (chip_ref[0], i, 0)),
                      pl.BlockSpec((3, tr, c), lambda i, chip_ref: (0, i, 0)), spec, spec, spec],
            out_specs=[spec] * 4),
        out_shape=[jax.ShapeDtypeStruct((r, c), F32)] * 4,
        compiler_params=_params(("parallel",)),
    )(chip, p32, r2, w, m, v)


def _adamw_small(name, parts, w, m, v):
    _, r, c = parts.shape
    tr = 8

    def body(p_ref, w_ref, m_ref, v_ref, g_out, d_out, m_out, v_out):
        g = p_ref[0]
        for j in range(1, N_DEV):
            g = g + p_ref[j]
        delta, m_new, v_new = _adamw(w_ref[...], g, m_ref[...], v_ref[...])
        g_out[...] = g
        d_out[...] = delta
        m_out[...] = m_new
        v_out[...] = v_new

    spec = pl.BlockSpec((tr, c), lambda i: (i, 0))
    return pl.pallas_call(
        body, name=name, grid=(r // tr,),
        in_specs=[pl.BlockSpec((N_DEV, tr, c), lambda i: (0, i, 0)), spec, spec, spec],
        out_specs=[spec] * 4, out_shape=[jax.ShapeDtypeStruct((r, c), F32)] * 4,
        compiler_params=_params(("parallel",)),
    )(parts, w, m, v)


_WEIGHTS = ["norm1_g", "w_in", "q_norm_g", "k_norm_g", "ssm_lambda_re", "ssm_lambda_im", "ssm_log_dt", "ssm_b_re",
            "ssm_b_im", "ssm_c_re", "ssm_c_im", "ssm_d", "w_glu", "b_glu", "attn_out_g", "ssm_out_g", "w_out",
            "norm2_g", "w_mlp_in", "w_mlp_out"]
_BIG = ["w_in", "w_glu", "w_out", "w_mlp_in", "w_mlp_out"]
_SMALL = [n for n in _WEIGHTS if n not in _BIG]
_PACK_COLS = 1024


def _pack(tree):
    flat = jnp.concatenate([tree[n].reshape(-1).astype(F32) for n in _SMALL])
    rows = -(-flat.shape[0] // (_PACK_COLS * 8)) * 8
    return jnp.pad(flat, (0, rows * _PACK_COLS - flat.shape[0])).reshape(rows, _PACK_COLS)


def _unpack(buf, like):
    flat, out, off = buf.reshape(-1), {}, 0
    for n in _SMALL:
        size = like[n].size
        out[n] = flat[off:off + size].reshape(like[n].shape)
        off += size
    return out


def kernel(x, norm1_g, w_in, q_norm_g, k_norm_g, ssm_lambda_re, ssm_lambda_im, ssm_log_dt, ssm_b_re, ssm_b_im, ssm_c_re, ssm_c_im, ssm_d, w_glu, b_glu, attn_out_g, ssm_out_g, w_out, norm2_g, w_mlp_in, w_mlp_out, loss_target, m_norm1_g, m_w_in, m_q_norm_g, m_k_norm_g, m_ssm_lambda_re, m_ssm_lambda_im, m_ssm_log_dt, m_ssm_b_re, m_ssm_b_im, m_ssm_c_re, m_ssm_c_im, m_ssm_d, m_w_glu, m_b_glu, m_attn_out_g, m_ssm_out_g, m_w_out, m_norm2_g, m_w_mlp_in, m_w_mlp_out, v_norm1_g, v_w_in, v_q_norm_g, v_k_norm_g, v_ssm_lambda_re, v_ssm_lambda_im, v_ssm_log_dt, v_ssm_b_re, v_ssm_b_im, v_ssm_c_re, v_ssm_c_im, v_ssm_d, v_w_glu, v_b_glu, v_attn_out_g, v_ssm_out_g, v_w_out, v_norm2_g, v_w_mlp_in, v_w_mlp_out):
    w = dict(zip(_WEIGHTS, (norm1_g, w_in, q_norm_g, k_norm_g, ssm_lambda_re, ssm_lambda_im, ssm_log_dt, ssm_b_re, ssm_b_im, ssm_c_re, ssm_c_im, ssm_d, w_glu, b_glu, attn_out_g, ssm_out_g, w_out, norm2_g, w_mlp_in, w_mlp_out)))
    m = dict(zip(_WEIGHTS, (m_norm1_g, m_w_in, m_q_norm_g, m_k_norm_g, m_ssm_lambda_re, m_ssm_lambda_im, m_ssm_log_dt, m_ssm_b_re, m_ssm_b_im, m_ssm_c_re, m_ssm_c_im, m_ssm_d, m_w_glu, m_b_glu, m_attn_out_g, m_ssm_out_g, m_w_out, m_norm2_g, m_w_mlp_in, m_w_mlp_out)))
    v = dict(zip(_WEIGHTS, (v_norm1_g, v_w_in, v_q_norm_g, v_k_norm_g, v_ssm_lambda_re, v_ssm_lambda_im, v_ssm_log_dt, v_ssm_b_re, v_ssm_b_im, v_ssm_c_re, v_ssm_c_im, v_ssm_d, v_w_glu, v_b_glu, v_attn_out_g, v_ssm_out_g, v_w_out, v_norm2_g, v_w_mlp_in, v_w_mlp_out)))
    axes = ("x", "y", "c")
    core = lax.axis_index("c").astype(jnp.int32).reshape(1)
    chip = (2 * lax.axis_index("x") + lax.axis_index("y")).astype(jnp.int32).reshape(1)

    gathered = dict(zip(_BIG, _all_gather("weights_all_gather", [w[n].astype(BF16) for n in _BIG])))
    full = {
        "w_in": gathered["w_in"].transpose(1, 0, 2).reshape(w_in.shape[0], -1),
        "w_glu": gathered["w_glu"].reshape(-1, w_glu.shape[1]),
        "w_out": gathered["w_out"].reshape(-1, w_out.shape[1]),
        "w_mlp_in": gathered["w_mlp_in"].transpose(1, 0, 2).reshape(w_mlp_in.shape[0], -1),
        "w_mlp_out": gathered["w_mlp_out"].reshape(-1, w_mlp_out.shape[1]),
    }

    loss_local, grad_x, g_small, g_big = _local_step(
        x, loss_target, {n: w[n] for n in _SMALL}, full["w_in"], full["w_glu"], full["w_out"], full["w_mlp_in"],
        full["w_mlp_out"])
    loss = lax.psum(loss_local, axes)

    g4 = [g_big[n].reshape(4, 2, *w[n].shape) for n in _BIG]
    r1 = _sibling_exchange("grads_sibling_exchange", g4)
    parts = [_chip_partial("chip_partial_" + n, a, b, core) for n, a, b in zip(_BIG, g4, r1)]
    r2 = _chip_exchange("grads_chip_exchange", [p16 for _, p16 in parts])
    grads, delta, new_m, new_v = {}, {}, {}, {}
    for n, (p32, _), r in zip(_BIG, parts, r2):
        grads[n], delta[n], new_m[n], new_v[n] = _adamw_shard("adamw_" + n, p32, r, chip, w[n], m[n], v[n])

    (small_parts,) = _all_gather("small_grads_all_gather", [_pack(g_small)])
    packed = _adamw_small("adamw_small", small_parts, _pack(w), _pack(m), _pack(v))
    for tree, buf in zip((grads, delta, new_m, new_v), packed):
        tree.update(_unpack(buf, w))

    return (loss, grad_x, *[grads[n] for n in _WEIGHTS], *[delta[n] for n in _WEIGHTS],
            *[new_m[n] for n in _WEIGHTS], *[new_v[n] for n in _WEIGHTS])
```

```python
import functools
import math

import jax
import jax.numpy as jnp
from jax import lax
from jax.experimental import pallas as pl
from jax.experimental.pallas import tpu as pltpu

F32 = jnp.float32
BF16 = jnp.bfloat16

EPS = 1e-6
HEAD_DIM = 64
N_HEADS = 8
SB_WIDTH = 512
SSM_WIDTH = 512
SSM_GROUP = 16
SSM_GROUPS = 32
SSM_STATE = 64
QBLOCK = 128
KBLOCK = 256
N_CHUNK = 8
SSM_COLS = 4
LANES = 128
N_DEV = 8

ADAM_LR = 0.001
ADAM_B1 = 0.9
ADAM_B2 = 0.999
ADAM_EPS = 1e-08
ADAM_WD = 0.01
ADAM_STEP = 10

VMEM_LIMIT = 56 * 1024 * 1024

_NT = (((1,), (1,)), ((), ()))
_NN = (((1,), (0,)), ((), ()))
_TN = (((0,), (0,)), ((), ()))


def _dot(a, b, dims=_NN):
    return lax.dot_general(a, b, dims, preferred_element_type=F32)


def _params(sem):
    return pltpu.CompilerParams(dimension_semantics=sem, vmem_limit_bytes=VMEM_LIMIT)


def _matmul(name, a, b, *, ta=False, tb=False, extras=(), epilogue=None, out_dtypes=(F32,),
            col_blocked=False, tm=512, tn=512, tk=1024):
    M, K = (a.shape[1], a.shape[0]) if ta else a.shape
    N = b.shape[0] if tb else b.shape[1]
    tm, tn, tk = min(tm, M), min(tn, N), min(tk, K)
    assert M % tm == 0 and N % tn == 0 and K % tk == 0, (name, M, N, K)
    nk = K // tk
    n_ex, n_out = len(extras), len(out_dtypes)
    dims = (((0 if ta else 1,), (1 if tb else 0,)), ((), ()))

    def body(*refs):
        a_ref, b_ref = refs[0], refs[1]
        ex_refs = refs[2:2 + n_ex]
        o_refs = refs[2 + n_ex:2 + n_ex + n_out]
        acc_ref = refs[-1]
        k = pl.program_id(2)

        @pl.when(k == 0)
        def _():
            acc_ref[...] = jnp.zeros_like(acc_ref)

        acc_ref[...] += _dot(a_ref[...].astype(BF16), b_ref[...].astype(BF16), dims)

        @pl.when(k == nk - 1)
        def _():
            acc = acc_ref[...]
            outs = (acc,) if epilogue is None else epilogue(acc, *[e[...] for e in ex_refs])
            for o_ref, o in zip(o_refs, outs):
                o_ref[...] = o.astype(o_ref.dtype)

    a_spec = pl.BlockSpec((tk, tm), lambda i, j, k: (k, i)) if ta else pl.BlockSpec((tm, tk), lambda i, j, k: (i, k))
    b_spec = pl.BlockSpec((tn, tk), lambda i, j, k: (j, k)) if tb else pl.BlockSpec((tk, tn), lambda i, j, k: (k, j))
    ex_specs = [pl.BlockSpec((tm, tn), lambda i, j, k: (i, j)) for _ in extras]
    if col_blocked:
        out_specs = [pl.BlockSpec((None, tm, tn), lambda i, j, k: (j, i, 0)) for _ in out_dtypes]
        out_shape = [jax.ShapeDtypeStruct((N // tn, M, tn), dt) for dt in out_dtypes]
    else:
        out_specs = [pl.BlockSpec((tm, tn), lambda i, j, k: (i, j)) for _ in out_dtypes]
        out_shape = [jax.ShapeDtypeStruct((M, N), dt) for dt in out_dtypes]
    outs = pl.pallas_call(
        body, name=name, grid=(M // tm, N // tn, nk),
        in_specs=[a_spec, b_spec, *ex_specs], out_specs=out_specs, out_shape=out_shape,
        scratch_shapes=[pltpu.VMEM((tm, tn), F32)],
        compiler_params=_params(("parallel", "parallel", "arbitrary")),
    )(a, b, *extras)
    return outs[0] if n_out == 1 else outs


def _rowwise(name, fn, rows, small, outs, sums=(), tile=256):
    specs, args = [], []
    T = None
    for r in rows:
        arr, cb, w = r if isinstance(r, tuple) else (r, 0, r.shape[1])
        T = arr.shape[0]
        specs.append((w, cb))
        args.append(arr)
    tile = min(tile, T)
    assert T % tile == 0
    n_r, n_s, n_o, n_a = len(rows), len(small), len(outs), len(sums)

    def body(*refs):
        r_refs = refs[:n_r]
        s_refs = refs[n_r:n_r + n_s]
        o_refs = refs[n_r + n_s:n_r + n_s + n_o]
        a_refs = refs[n_r + n_s + n_o:]
        res = fn(*[r[...] for r in r_refs], *[s[...] for s in s_refs])
        res = res if isinstance(res, (tuple, list)) else (res,)
        for o_ref, o in zip(o_refs, res[:n_o]):
            o_ref[...] = o.astype(o_ref.dtype)

        @pl.when(pl.program_id(0) == 0)
        def _():
            for a_ref in a_refs:
                a_ref[...] = jnp.zeros_like(a_ref)

        for a_ref, v in zip(a_refs, res[n_o:]):
            a_ref[...] += v.astype(F32)

    in_specs = [pl.BlockSpec((tile, w), functools.partial(lambda i, cb: (i, cb), cb=cb)) for w, cb in specs]
    in_specs += [pl.BlockSpec(s.shape, functools.partial(lambda i, nd: (0,) * nd, nd=s.ndim)) for s in small]
    out_specs = [pl.BlockSpec((tile, w), lambda i: (i, 0)) for w, _ in outs]
    out_specs += [pl.BlockSpec(s, functools.partial(lambda i, nd: (0,) * nd, nd=len(s))) for s in sums]
    out_shape = [jax.ShapeDtypeStruct((T, w), dt) for w, dt in outs]
    out_shape += [jax.ShapeDtypeStruct(s, F32) for s in sums]
    res = pl.pallas_call(
        body, name=name, grid=(T // tile,), in_specs=in_specs, out_specs=out_specs, out_shape=out_shape,
        compiler_params=_params(("arbitrary",)),
    )(*args, *small)
    return res[0] if len(res) == 1 else res


def _rms(x, g):
    return x * lax.rsqrt(jnp.mean(x * x, axis=-1, keepdims=True) + EPS) * g


def _glu_branch(y, pre, b_glu, g_out):
    g = jax.nn.gelu(y)
    return _rms(g * jax.nn.sigmoid(pre + b_glu), g_out)


def _split_dot(x, tri_bf):
    hi = x.astype(BF16)
    lo = (x - hi.astype(F32)).astype(BF16)
    return _dot(hi, tri_bf) + _dot(lo, tri_bf)


def _softplus(z):
    return jnp.maximum(z, 0.0) + jnp.log1p(jnp.exp(-jnp.abs(z)))


def _head(h):
    return slice(h * HEAD_DIM, (h + 1) * HEAD_DIM)


def _attn_norm_to_scratch(q_ref, k_ref, v_ref, gq_ref, gk_ref, qn_s, kn_s, v_s, L, qnt_s=None):
    scale = 1.0 / math.sqrt(HEAD_DIM)

    def chunk(i, _):
        r = pl.ds(pl.multiple_of(i * QBLOCK, QBLOCK), QBLOCK)
        qn = [_rms(q_ref[r, _head(h)], gq_ref[:, _head(h)]) * scale for h in range(2)]
        for h in range(2):
            qn_s[h, r, :] = qn[h].astype(qn_s.dtype)
            kn_s[h, r, :] = _rms(k_ref[r, _head(h)], gk_ref[:, _head(h)]).astype(kn_s.dtype)
            v_s[h, r, :] = v_ref[r, _head(h)].astype(v_s.dtype)
        if qnt_s is not None:
            qnt_s[:, r] = jnp.concatenate(qn, axis=1).T.astype(qnt_s.dtype)
        return 0

    lax.fori_loop(0, L // QBLOCK, chunk, 0)


def _key_blocks(i):
    return lax.div(i * QBLOCK, KBLOCK) + 1


def _valid(i, kb):
    row = lax.broadcasted_iota(jnp.int32, (QBLOCK, KBLOCK), 0)
    col = lax.broadcasted_iota(jnp.int32, (QBLOCK, KBLOCK), 1)
    return col + (kb * KBLOCK - i * QBLOCK) < row


def _attn_fwd(proj, gq2, gk2, B, L):
    nq = L // QBLOCK
    n_hp = N_HEADS // 2

    def body(q_ref, k_ref, v_ref, gq_ref, gk_ref, o_ref, rt_ref, qn_s, kn_s, v_s, after_s):
        _attn_norm_to_scratch(q_ref, k_ref, v_ref, gq_ref, gk_ref, qn_s, kn_s, v_s, L)
        r2 = lax.broadcasted_iota(jnp.int32, (KBLOCK, KBLOCK), 0)
        c2 = lax.broadcasted_iota(jnp.int32, (KBLOCK, KBLOCK), 1)
        after_s[...] = (r2 > c2).astype(after_s.dtype)

        def q_block(i, _):
            rq = pl.ds(pl.multiple_of(i * QBLOCK, QBLOCK), QBLOCK)
            q_i = [qn_s[h, rq, :] for h in range(2)]
            nkb = _key_blocks(i)

            def k_block(n, carry):
                kb = nkb - 1 - n
                rk = pl.ds(pl.multiple_of(kb * KBLOCK, KBLOCK), KBLOCK)
                valid = _valid(i, kb)
                hs = range(2)
                z = [_dot(q_i[h], kn_s[h, rk, :], _NT) for h in hs]
                sp = [_softplus(z[h]) for h in hs]
                lom = [jnp.where(valid, -sp[h], 0.0) for h in hs]
                tail = [_split_dot(lom[h], after_s[...]) + carry[h][0] for h in hs]
                a = [jnp.where(valid, jnp.exp(z[h] - sp[h] + tail[h]), 0.0) for h in hs]
                acc = [carry[h][1] + _dot(a[h].astype(v_s.dtype), v_s[h, rk, :]) for h in hs]
                return tuple((carry[h][0] + jnp.sum(lom[h], axis=1, keepdims=True), acc[h]) for h in hs)

            init = (jnp.zeros((QBLOCK, 1), F32), jnp.zeros((QBLOCK, HEAD_DIM), F32))
            res = lax.fori_loop(0, nkb, k_block, (init, init))
            o_ref[rq, :] = jnp.concatenate([res[0][1], res[1][1]], axis=1)
            rt_ref[rq, :] = jnp.concatenate(
                [jnp.broadcast_to(res[h][0], (QBLOCK, HEAD_DIM)) for h in range(2)], axis=1)
            return 0

        lax.fori_loop(0, nq, q_block, 0)

    blk = lambda off: pl.BlockSpec((L, LANES), functools.partial(lambda b, p, off: (b, off + p), off=off))
    gspec = pl.BlockSpec((1, LANES), lambda b, p: (0, 0))
    ospec = pl.BlockSpec((L, LANES), lambda b, p: (b, p))
    return pl.pallas_call(
        body, name="attn_fwd", grid=(B, n_hp),
        in_specs=[blk(0), blk(n_hp), blk(2 * n_hp), gspec, gspec],
        out_specs=[ospec, ospec],
        out_shape=[jax.ShapeDtypeStruct((B * L, SB_WIDTH), F32)] * 2,
        scratch_shapes=[pltpu.VMEM((2, L, HEAD_DIM), BF16)] * 3 + [pltpu.VMEM((KBLOCK, KBLOCK), BF16)],
        compiler_params=_params(("parallel", "parallel")),
    )(proj, proj, proj, gq2, gk2)


def _attn_bwd(proj, gq2, gk2, rtot, d_sb, B, L):
    nq = L // QBLOCK
    n_hp = N_HEADS // 2
    scale = 1.0 / math.sqrt(HEAD_DIM)

    def body(q_ref, k_ref, v_ref, gq_ref, gk_ref, rt_ref, do_ref, dq_ref, dk_ref, dv_ref, dgq_ref, dgk_ref,
             qn_s, kn_s, v_s, qnt_s, dkt_s, dvt_s, after_s, before_s):
        _attn_norm_to_scratch(q_ref, k_ref, v_ref, gq_ref, gk_ref, qn_s, kn_s, v_s, L, qnt_s)
        dkt_s[...] = jnp.zeros_like(dkt_s)
        dvt_s[...] = jnp.zeros_like(dvt_s)
        r2 = lax.broadcasted_iota(jnp.int32, (KBLOCK, KBLOCK), 0)
        c2 = lax.broadcasted_iota(jnp.int32, (KBLOCK, KBLOCK), 1)
        after_s[...] = (r2 > c2).astype(after_s.dtype)
        before_s[...] = (r2 < c2).astype(before_s.dtype)

        @pl.when(jnp.logical_and(pl.program_id(0) == 0, pl.program_id(1) == 0))
        def _():
            dgq_ref[...] = jnp.zeros_like(dgq_ref)
            dgk_ref[...] = jnp.zeros_like(dgk_ref)

        def q_block(i, dgq):
            rq = pl.ds(pl.multiple_of(i * QBLOCK, QBLOCK), QBLOCK)
            do2 = do_ref[rq, :]
            do_t = do2.T.astype(v_s.dtype)
            tot2 = rt_ref[rq, :]
            q_i = [qn_s[h, rq, :] for h in range(2)]
            qt_i = [qnt_s[_head(h), rq] for h in range(2)]
            do_i = [do2[:, _head(h)].astype(v_s.dtype) for h in range(2)]
            dot_i = [do_t[_head(h), :] for h in range(2)]
            total = [tot2[:, h * HEAD_DIM:h * HEAD_DIM + 1] for h in range(2)]

            def k_block(kb, carry):
                rk = pl.ds(pl.multiple_of(kb * KBLOCK, KBLOCK), KBLOCK)
                valid = _valid(i, kb)
                hs = range(2)
                k_b = [kn_s[h, rk, :] for h in hs]
                z = [_dot(q_i[h], k_b[h], _NT) for h in hs]
                da = [_dot(do_i[h], v_s[h, rk, :], _NT) for h in hs]
                sp = [_softplus(z[h]) for h in hs]
                lom = [jnp.where(valid, -sp[h], 0.0) for h in hs]
                lom_sum = [jnp.sum(lom[h], axis=1, keepdims=True) for h in hs]
                tail = [_split_dot(lom[h], after_s[...]) + (total[h] - carry[h][0] - lom_sum[h]) for h in hs]
                a = [jnp.where(valid, jnp.exp(z[h] - sp[h] + tail[h]), 0.0) for h in hs]
                dla = [a[h] * da[h] for h in hs]
                for h in hs:
                    dvt_s[_head(h), rk] += _dot(dot_i[h], a[h].astype(v_s.dtype))
                d_lom = [carry[h][1] + _split_dot(dla[h], before_s[...]) for h in hs]
                beta = [jnp.exp(z[h] - sp[h]) for h in hs]
                dz_b = [(dla[h] * (1.0 - beta[h]) - jnp.where(valid, beta[h] * d_lom[h], 0.0)).astype(v_s.dtype)
                        for h in hs]
                dq_acc = [carry[h][2] + _dot(dz_b[h], k_b[h]) for h in hs]
                for h in hs:
                    dkt_s[_head(h), rk] += _dot(qt_i[h], dz_b[h])
                return tuple((carry[h][0] + lom_sum[h], carry[h][1] + jnp.sum(dla[h], axis=1, keepdims=True),
                              dq_acc[h]) for h in hs)

            zero = jnp.zeros((QBLOCK, 1), F32)
            init = (zero, zero, jnp.zeros((QBLOCK, HEAD_DIM), F32))
            res = lax.fori_loop(0, _key_blocks(i), k_block, (init, init))
            dq = []
            for h in range(2):
                _, vjp_q = jax.vjp(_rms, q_ref[rq, _head(h)], gq_ref[:, _head(h)])
                dq_h, dgq_h = vjp_q(res[h][2] * scale)
                dq.append(dq_h)
                dgq = dgq + dgq_h
            dq_ref[rq, :] = jnp.concatenate(dq, axis=1)
            return dgq

        dgq = lax.fori_loop(0, nq, q_block, jnp.zeros((1, HEAD_DIM), F32))

        def k_norm_bwd(i, dgk):
            r = pl.ds(pl.multiple_of(i * QBLOCK, QBLOCK), QBLOCK)
            dkn = dkt_s[:, r].T
            dk = []
            for h in range(2):
                _, vjp_k = jax.vjp(_rms, k_ref[r, _head(h)], gk_ref[:, _head(h)])
                dk_h, dgk_h = vjp_k(dkn[:, _head(h)])
                dk.append(dk_h)
                dgk = dgk + dgk_h
            dk_ref[r, :] = jnp.concatenate(dk, axis=1)
            dv_ref[r, :] = dvt_s[:, r].T
            return dgk

        dgk = lax.fori_loop(0, nq, k_norm_bwd, jnp.zeros((1, HEAD_DIM), F32))
        dgq_ref[:, 0:HEAD_DIM] += dgq
        dgk_ref[:, 0:HEAD_DIM] += dgk

    blk = lambda off: pl.BlockSpec((L, LANES), functools.partial(lambda b, p, off: (b, off + p), off=off))
    gspec = pl.BlockSpec((1, LANES), lambda b, p: (0, 0))
    ospec = pl.BlockSpec((L, LANES), lambda b, p: (b, p))
    return pl.pallas_call(
        body, name="attn_bwd", grid=(B, n_hp),
        in_specs=[blk(0), blk(n_hp), blk(2 * n_hp), gspec, gspec, ospec, ospec],
        out_specs=[ospec, ospec, ospec, gspec, gspec],
        out_shape=[jax.ShapeDtypeStruct((B * L, SB_WIDTH), F32)] * 3 + [jax.ShapeDtypeStruct((1, LANES), F32)] * 2,
        scratch_shapes=[pltpu.VMEM((2, L, HEAD_DIM), BF16)] * 3 + [pltpu.VMEM((LANES, L), BF16)]
        + [pltpu.VMEM((LANES, L), F32)] * 2 + [pltpu.VMEM((KBLOCK, KBLOCK), BF16)] * 2,
        compiler_params=_params(("arbitrary", "arbitrary")),
    )(proj, proj, proj, gq2, gk2, rtot, d_sb)


def _ssm_discretise(lam_re, lam_im, log_dt, b_re, b_im):
    dt = jnp.exp(log_dt)
    mag = jnp.exp(lam_re * dt)
    lbr = mag * jnp.cos(lam_im * dt)
    lbi = mag * jnp.sin(lam_im * dt)
    den = lam_re * lam_re + lam_im * lam_im
    nr, ni = lbr - 1.0, lbi
    cr = (nr * lam_re + ni * lam_im) / den
    ci = (ni * lam_re - nr * lam_im) / den
    return lbr, lbi, cr * b_re - ci * b_im, cr * b_im + ci * b_re


def _ssm_prep(lam_re, lam_im, log_dt, b_re_t, b_im_t):
    def body(lr, li, ld, br, bi, o_lr, o_li, o_br, o_bi):
        res = _ssm_discretise(lr[...], li[...], ld[...], br[...], bi[...])
        for o, v in zip((o_lr, o_li, o_br, o_bi), res):
            o[...] = v

    return pl.pallas_call(
        body, name="ssm_prep",
        out_shape=[jax.ShapeDtypeStruct(lam_re.shape, F32)] * 2 + [jax.ShapeDtypeStruct(b_re_t.shape, F32)] * 2,
    )(lam_re, lam_im, log_dt, b_re_t, b_im_t)


def _ssm_prep_bwd(lam_re, lam_im, log_dt, b_re_t, b_im_t, d_lr, d_li, d_br, d_bi):
    def body(lr, li, ld, br, bi, g_lr, g_li, g_br, g_bi, o_lr, o_li, o_ld, o_br, o_bi):
        _, vjp = jax.vjp(_ssm_discretise, lr[...], li[...], ld[...], br[...], bi[...])
        res = vjp((g_lr[...], g_li[...], g_br[...], g_bi[...]))
        for o, v in zip((o_lr, o_li, o_ld, o_br, o_bi), res):
            o[...] = v

    return pl.pallas_call(
        body, name="ssm_prep_bwd",
        out_shape=[jax.ShapeDtypeStruct(lam_re.shape, F32)] * 2 + [jax.ShapeDtypeStruct(log_dt.shape, F32)]
        + [jax.ShapeDtypeStruct(b_re_t.shape, F32)] * 2,
    )(lam_re, lam_im, log_dt, b_re_t, b_im_t, d_lr, d_li, d_br, d_bi)


def _block_diag(m):
    m4 = m.reshape(SSM_COLS, 8, SSM_GROUP, SSM_STATE)
    return jnp.einsum("aghp,gk->aghkp", m4, jnp.eye(8, dtype=m.dtype)).reshape(SSM_COLS, LANES, 512)


def _block_diag_take(d):
    d6 = d.reshape(SSM_COLS, 8, SSM_GROUP, 2, 8, SSM_STATE)
    return jnp.einsum("aghrgp->raghp", d6).reshape(2, SSM_GROUPS, SSM_GROUP, SSM_STATE)


def _cmul(ar, ai, br, bi):
    return ar * br - ai * bi, ar * bi + ai * br


def _power(lr, li, n):
    assert n & (n - 1) == 0
    for _ in range(n.bit_length() - 1):
        lr, li = _cmul(lr, li, lr, li)
    return lr, li


def _ssm_fwd(u_p, w_b, lam_r, lam_i, c_m, d_skip, B, L, tj):
    J = L // N_CHUNK
    njt = J // tj
    R = tj * N_CHUNK
    H = 512

    def body(u_ref, wb_ref, lr_ref, li_ref, cm_ref, d_ref, y_ref, x_ref, xin_ref, bu_s, st_s, xin_s):
        ph, jt = pl.program_id(2), pl.program_id(3)
        lr, li = lr_ref[...], li_ref[...]

        @pl.when(jnp.logical_and(ph == 0, jt == 0))
        def _():
            st_s[...] = jnp.zeros_like(st_s)

        bu_s[...] = _dot(u_ref[...].astype(BF16), wb_ref[...].astype(BF16))

        def scan(store):
            def step(j, carry):
                xr, xi = carry
                r = pl.ds(pl.multiple_of(j * N_CHUNK, N_CHUNK), N_CHUNK)
                nr = lr * xr - li * xi + bu_s[r, 0:H]
                ni = lr * xi + li * xr + bu_s[r, H:2 * H]
                if store:
                    x_ref[r, 0:H] = nr
                    x_ref[r, H:2 * H] = ni
                return nr, ni

            xr, xi = lax.fori_loop(0, tj, step, (st_s[:, 0:H], st_s[:, H:2 * H]))
            st_s[:, 0:H] = xr
            st_s[:, H:2 * H] = xi

        @pl.when(ph == 0)
        def _():
            scan(False)

            @pl.when(jt == njt - 1)
            def _():
                pr, pi = _power(lr[0:1], li[0:1], J)
                xin_s[0:1, :] = jnp.zeros((1, 2 * H), F32)
                for c in range(1, N_CHUNK):
                    qr, qi = _cmul(pr, pi, xin_s[c - 1:c, 0:H], xin_s[c - 1:c, H:2 * H])
                    xin_s[c:c + 1, 0:H] = qr + st_s[c - 1:c, 0:H]
                    xin_s[c:c + 1, H:2 * H] = qi + st_s[c - 1:c, H:2 * H]
                xin_ref[...] = xin_s[...]
                st_s[...] = xin_s[...]

        @pl.when(ph == 1)
        def _():
            scan(True)
            y = _dot(x_ref[...].astype(BF16), cm_ref[...].astype(BF16))
            y_ref[...] = y + d_ref[...] * u_ref[...]

    return pl.pallas_call(
        body, name="ssm_fwd", grid=(SSM_COLS, B, 2, njt),
        in_specs=[
            pl.BlockSpec((None, R, LANES), lambda i, b, ph, jt: (b, jt, i)),
            pl.BlockSpec((None, LANES, 2 * H), lambda i, b, ph, jt: (i, 0, 0)),
            pl.BlockSpec((None, N_CHUNK, H), lambda i, b, ph, jt: (i, 0, 0)),
            pl.BlockSpec((None, N_CHUNK, H), lambda i, b, ph, jt: (i, 0, 0)),
            pl.BlockSpec((None, 2 * H, LANES), lambda i, b, ph, jt: (i, 0, 0)),
            pl.BlockSpec((1, LANES), lambda i, b, ph, jt: (0, i)),
        ],
        out_specs=[
            pl.BlockSpec((None, R, LANES), lambda i, b, ph, jt: (b, jt * ph, i)),
            pl.BlockSpec((None, R, 2 * H), lambda i, b, ph, jt: (b, jt * ph, i)),
            pl.BlockSpec((None, None, N_CHUNK, 2 * H), lambda i, b, ph, jt: (b, i, 0, 0)),
        ],
        out_shape=[
            jax.ShapeDtypeStruct((B, L, SSM_WIDTH), F32),
            jax.ShapeDtypeStruct((B, L, SSM_COLS * 2 * H), F32),
            jax.ShapeDtypeStruct((B, SSM_COLS, N_CHUNK, 2 * H), F32),
        ],
        scratch_shapes=[pltpu.VMEM((R, 2 * H), F32), pltpu.VMEM((N_CHUNK, 2 * H), F32), pltpu.VMEM((N_CHUNK, 2 * H), F32)],
        compiler_params=_params(("arbitrary",) * 4),
    )(u_p, w_b, lam_r, lam_i, c_m, d_skip)


def _ssm_bwd(dy_p, u_p, x, xin, w_bt, lam_r, lam_i, c_mt, d_skip, B, L, tj):
    J = L // N_CHUNK
    njt = J // tj
    R = tj * N_CHUNK
    H = 512
    x4 = x.reshape(B, J, N_CHUNK, SSM_COLS * 2 * H)

    def body(dy_ref, u_ref, x_ref, xp_ref, xin_ref, wbt_ref, lr_ref, li_ref, cmt_ref, d_ref,
             du_ref, dwb_ref, dcm_ref, dlr_ref, dli_ref, dd_ref, ca_s, a_s, st_s, dl_s):
        b, ph, jt = pl.program_id(1), pl.program_id(2), pl.program_id(3)
        jr = njt - 1 - jt
        lr, li = lr_ref[...], -li_ref[...]

        @pl.when(jnp.logical_and(b == 0, jnp.logical_and(ph == 0, jt == 0)))
        def _():
            dwb_ref[...] = jnp.zeros_like(dwb_ref)
            dcm_ref[...] = jnp.zeros_like(dcm_ref)
            dlr_ref[...] = jnp.zeros_like(dlr_ref)
            dli_ref[...] = jnp.zeros_like(dli_ref)
            dd_ref[...] = jnp.zeros_like(dd_ref)
            dl_s[...] = jnp.zeros_like(dl_s)

        @pl.when(jnp.logical_and(ph == 0, jt == 0))
        def _():
            st_s[...] = jnp.zeros_like(st_s)

        ca_s[...] = _dot(dy_ref[...].astype(BF16), cmt_ref[...].astype(BF16))

        def scan(store):
            def step(n, carry):
                ar, ai = carry
                r = pl.ds(pl.multiple_of((tj - 1 - n) * N_CHUNK, N_CHUNK), N_CHUNK)
                nr = lr * ar - li * ai + ca_s[r, 0:H]
                ni = lr * ai + li * ar + ca_s[r, H:2 * H]
                if store:
                    a_s[r, 0:H] = nr
                    a_s[r, H:2 * H] = ni
                return nr, ni

            ar, ai = lax.fori_loop(0, tj, step, (st_s[:, 0:H], st_s[:, H:2 * H]))
            st_s[:, 0:H] = ar
            st_s[:, H:2 * H] = ai

        @pl.when(ph == 0)
        def _():
            scan(False)

            @pl.when(jt == njt - 1)
            def _():
                pr, pi = _power(lr[0:1], li[0:1], J)
                a_s[N_CHUNK - 1:N_CHUNK, :] = jnp.zeros((1, 2 * H), F32)
                for c in range(N_CHUNK - 2, -1, -1):
                    qr, qi = _cmul(pr, pi, a_s[c + 1:c + 2, 0:H], a_s[c + 1:c + 2, H:2 * H])
                    a_s[c:c + 1, 0:H] = qr + st_s[c + 1:c + 2, 0:H]
                    a_s[c:c + 1, H:2 * H] = qi + st_s[c + 1:c + 2, H:2 * H]
                st_s[...] = a_s[0:N_CHUNK, :]

        @pl.when(ph == 1)
        def _():
            scan(True)
            dy = dy_ref[...]
            u = u_ref[...]
            a_b = a_s[...].astype(BF16)
            du_ref[...] = _dot(a_b, wbt_ref[...].astype(BF16)) + d_ref[...] * dy
            dwb_ref[...] += _dot(u.astype(BF16), a_b, _TN)
            dcm_ref[...] += _dot(x_ref[...].astype(BF16), dy.astype(BF16), _TN)
            dd_ref[...] += jnp.sum(dy * u, axis=0, keepdims=True)

            first = jnp.where(jr == 0, xin_ref[...], xp_ref[...])
            a0r, a0i = a_s[0:N_CHUNK, 0:H], a_s[0:N_CHUNK, H:2 * H]
            acc0 = (a0r * first[:, 0:H] + a0i * first[:, H:2 * H], a0i * first[:, 0:H] - a0r * first[:, H:2 * H])

            def step(j, carry):
                sr, si = carry
                r = pl.ds(pl.multiple_of(j * N_CHUNK, N_CHUNK), N_CHUNK)
                rp = pl.ds(pl.multiple_of((j - 1) * N_CHUNK, N_CHUNK), N_CHUNK)
                ar, ai = a_s[r, 0:H], a_s[r, H:2 * H]
                xr, xi = x_ref[rp, 0:H], x_ref[rp, H:2 * H]
                return sr + ar * xr + ai * xi, si + ai * xr - ar * xi

            sr, si = lax.fori_loop(1, tj, step, acc0)
            dl_s[:, 0:H] += sr
            dl_s[:, H:2 * H] += si

            @pl.when(jnp.logical_and(b == B - 1, jt == njt - 1))
            def _():
                dlr_ref[...] = jnp.sum(dl_s[:, 0:H], axis=0, keepdims=True)
                dli_ref[...] = jnp.sum(dl_s[:, H:2 * H], axis=0, keepdims=True)
                dl_s[...] = jnp.zeros_like(dl_s)

    rev = lambda ph, jt: (njt - 1 - jt) * ph + (njt - 1) * (1 - ph)
    return pl.pallas_call(
        body, name="ssm_bwd", grid=(SSM_COLS, B, 2, njt),
        in_specs=[
            pl.BlockSpec((None, R, LANES), lambda i, b, ph, jt: (b, njt - 1 - jt, i)),
            pl.BlockSpec((None, R, LANES), lambda i, b, ph, jt: (b, njt - 1 - jt, i)),
            pl.BlockSpec((None, R, 2 * H), lambda i, b, ph, jt: (b, rev(ph, jt), i)),
            pl.BlockSpec((None, None, N_CHUNK, 2 * H),
                         lambda i, b, ph, jt: (b, jnp.maximum((njt - 1 - jt) * tj - 1, 0), 0, i)),
            pl.BlockSpec((None, None, N_CHUNK, 2 * H), lambda i, b, ph, jt: (b, i, 0, 0)),
            pl.BlockSpec((None, 2 * H, LANES), lambda i, b, ph, jt: (i, 0, 0)),
            pl.BlockSpec((None, N_CHUNK, H), lambda i, b, ph, jt: (i, 0, 0)),
            pl.BlockSpec((None, N_CHUNK, H), lambda i, b, ph, jt: (i, 0, 0)),
            pl.BlockSpec((None, LANES, 2 * H), lambda i, b, ph, jt: (i, 0, 0)),
            pl.BlockSpec((1, LANES), lambda i, b, ph, jt: (0, i)),
        ],
        out_specs=[
            pl.BlockSpec((None, R, LANES), lambda i, b, ph, jt: (b, rev(ph, jt), i)),
            pl.BlockSpec((None, LANES, 2 * H), lambda i, b, ph, jt: (i, 0, 0)),
            pl.BlockSpec((None, 2 * H, LANES), lambda i, b, ph, jt: (i, 0, 0)),
            pl.BlockSpec((None, 1, H), lambda i, b, ph, jt: (i, 0, 0)),
            pl.BlockSpec((None, 1, H), lambda i, b, ph, jt: (i, 0, 0)),
            pl.BlockSpec((1, LANES), lambda i, b, ph, jt: (0, i)),
        ],
        out_shape=[
            jax.ShapeDtypeStruct((B, L, SSM_WIDTH), F32),
            jax.ShapeDtypeStruct((SSM_COLS, LANES, 2 * H), F32),
            jax.ShapeDtypeStruct((SSM_COLS, 2 * H, LANES), F32),
            jax.ShapeDtypeStruct((SSM_COLS, 1, H), F32),
            jax.ShapeDtypeStruct((SSM_COLS, 1, H), F32),
            jax.ShapeDtypeStruct((1, SSM_WIDTH), F32),
        ],
        scratch_shapes=[pltpu.VMEM((R, 2 * H), F32), pltpu.VMEM((R, 2 * H), F32),
                        pltpu.VMEM((N_CHUNK, 2 * H), F32), pltpu.VMEM((N_CHUNK, 2 * H), F32)],
        compiler_params=_params(("arbitrary",) * 4),
    )(dy_p, u_p, x, x4, xin, w_bt, lam_r, lam_i, c_mt, d_skip)


def _to_scan_layout(t, B, L):
    C = t.shape[-1]
    return t.reshape(B, N_CHUNK, L // N_CHUNK, C).transpose(0, 2, 1, 3).reshape(B, L, C)


def _from_scan_layout(t, B, L):
    C = t.shape[-1]
    return t.reshape(B, L // N_CHUNK, N_CHUNK, C).transpose(0, 2, 1, 3).reshape(B * L, C)


def _local_step(x, target, p, w_in, w_glu, w_out, w_mlp_in, w_mlp_out, *, ssm_tile=32):
    B, L, D = x.shape
    T = B * L
    x2 = x.reshape(T, D)
    row = lambda v: v.reshape(1, -1)
    g1, g2, ga, gs, b_glu = row(p["norm1_g"]), row(p["norm2_g"]), row(p["attn_out_g"]), row(p["ssm_out_g"]), row(p["b_glu"])
    gq2 = jnp.tile(row(p["q_norm_g"]), (1, 2))
    gk2 = jnp.tile(row(p["k_norm_g"]), (1, 2))

    G, P, Hh = SSM_GROUPS, SSM_STATE, SSM_GROUP
    lam_re3, lam_im3 = p["ssm_lambda_re"].reshape(G, 1, P), p["ssm_lambda_im"].reshape(G, 1, P)
    log_dt3 = p["ssm_log_dt"].reshape(G, 1, 1)
    b_re_t, b_im_t = p["ssm_b_re"].transpose(0, 2, 1), p["ssm_b_im"].transpose(0, 2, 1)
    lbr, lbi, bbr, bbi = _ssm_prep(lam_re3, lam_im3, log_dt3, b_re_t, b_im_t)
    w_b = jnp.concatenate([_block_diag(bbr), _block_diag(bbi)], axis=2)
    c_mt = jnp.concatenate([_block_diag(p["ssm_c_re"]), -_block_diag(p["ssm_c_im"])], axis=2)
    w_bt, c_m = w_b.transpose(0, 2, 1), c_mt.transpose(0, 2, 1)
    lam_r = jnp.broadcast_to(lbr.reshape(SSM_COLS, 1, 512), (SSM_COLS, N_CHUNK, 512))
    lam_i = jnp.broadcast_to(lbi.reshape(SSM_COLS, 1, 512), (SSM_COLS, N_CHUNK, 512))
    d_skip = p["ssm_d"].reshape(1, SSM_WIDTH)

    xn = _rowwise("norm1", _rms, [x2], [g1], [(D, BF16)])
    proj = _matmul("proj", xn, w_in)
    sb, rtot = _attn_fwd(proj, gq2, gk2, B, L)
    u_p = _to_scan_layout(proj[:, 3 * SB_WIDTH:], B, L)
    y_p, xs, xin = _ssm_fwd(u_p, w_b, lam_r, lam_i, c_m, d_skip, B, L, ssm_tile)
    y2 = y_p.reshape(T, SSM_WIDTH)
    gel = _rowwise("gelu", jax.nn.gelu, [y2], [], [(SSM_WIDTH, BF16)])
    pre = _matmul("glu_gate", gel, w_glu)
    ssm_n = _rowwise("glu_out", _glu_branch, [y2, pre], [b_glu, gs], [(SSM_WIDTH, BF16)])
    sb_n = _rowwise("attn_out_norm", _rms, [sb], [ga], [(SB_WIDTH, BF16)])
    mixed = jnp.concatenate([sb_n, _from_scan_layout(ssm_n, B, L)], axis=1)
    h1 = _matmul("out_proj", mixed, w_out, extras=[x2], epilogue=lambda acc, r: (acc + r,))
    hn = _rowwise("norm2", _rms, [h1], [g2], [(D, BF16)])
    act, a_pre = _matmul("mlp_in", hn, w_mlp_in, out_dtypes=(BF16, BF16),
                         epilogue=lambda acc: (jnp.square(jnp.maximum(acc, 0.0)), acc))
    out = _matmul("mlp_out", act, w_mlp_out, extras=[h1], epilogue=lambda acc, r: (acc + r,))

    def loss_fn(o, t):
        diff = o - t
        part = jnp.sum(jnp.sum(diff * diff, axis=0, keepdims=True), axis=1, keepdims=True)
        return diff * (1.0 / D), part * (0.5 / D)

    d_out, loss = _rowwise("loss", loss_fn, [out, target.reshape(T, D)], [], [(D, F32)], sums=[(1, 1)])

    d_apre = _matmul("mlp_out_dx", d_out, w_mlp_out, tb=True, extras=[a_pre], out_dtypes=(BF16,),
                     epilogue=lambda acc, ap: (acc * (2.0 * jnp.maximum(ap.astype(F32), 0.0)),))
    g_w_mlp_out = _matmul("mlp_out_dw", act, d_out, ta=True)
    d_hn = _matmul("mlp_in_dx", d_apre, w_mlp_in, tb=True)
    g_w_mlp_in = _matmul("mlp_in_dw", hn, d_apre, ta=True, col_blocked=True, tn=w_mlp_in.shape[1] // N_DEV)

    def norm_bwd_res(h, dy, res, g):
        _, vjp = jax.vjp(_rms, h, g)
        dh, dg = vjp(dy)
        return res + dh, dg

    d_h1, g_norm2 = _rowwise("norm2_bwd", norm_bwd_res, [h1, d_hn, d_out], [g2], [(D, F32)], sums=[(1, D)])

    d_mixed = _matmul("out_proj_dx", d_h1, w_out, tb=True)
    g_w_out = _matmul("out_proj_dw", mixed, d_h1, ta=True)

    def norm_bwd(h, dy, g):
        _, vjp = jax.vjp(_rms, h, g)
        return vjp(dy)

    d_sb, g_attn_out = _rowwise("attn_out_norm_bwd", norm_bwd, [sb, (d_mixed, 0, SB_WIDTH)], [ga],
                                [(SB_WIDTH, F32)], sums=[(1, SB_WIDTH)])
    d_ssm_n = _to_scan_layout(d_mixed[:, SB_WIDTH:], B, L).reshape(T, SSM_WIDTH)

    def glu_bwd(y, pre_, dy, bg, g):
        _, vjp = jax.vjp(_glu_branch, y, pre_, bg, g)
        d_y, d_pre, d_bg, d_g = vjp(dy)
        return d_y, d_pre, d_bg, d_g

    d_y_direct, d_pre, g_b_glu, g_ssm_out = _rowwise(
        "glu_out_bwd", glu_bwd, [y2, pre, d_ssm_n], [b_glu, gs], [(SSM_WIDTH, F32), (SSM_WIDTH, BF16)],
        sums=[(1, SSM_WIDTH), (1, SSM_WIDTH)])
    d_gel = _matmul("glu_gate_dx", d_pre, w_glu, tb=True)
    g_w_glu = _matmul("glu_gate_dw", gel, d_pre, ta=True)

    def gelu_bwd(y, dg, dy0):
        _, vjp = jax.vjp(jax.nn.gelu, y)
        return dy0 + vjp(dg)[0]

    d_y = _rowwise("gelu_bwd", gelu_bwd, [y2, d_gel, d_y_direct], [], [(SSM_WIDTH, F32)])

    du_p, d_wb, d_cm, d_lr, d_li, g_d = _ssm_bwd(
        d_y.reshape(B, L, SSM_WIDTH), u_p, xs, xin, w_bt, lam_r, lam_i, c_mt, d_skip, B, L, ssm_tile)
    d_bb = _block_diag_take(d_wb.reshape(SSM_COLS, LANES, 2, 512))
    d_c = _block_diag_take(d_cm.transpose(0, 2, 1).reshape(SSM_COLS, LANES, 2, 512))
    g_lam_re, g_lam_im, g_log_dt, g_b_re_t, g_b_im_t = _ssm_prep_bwd(
        lam_re3, lam_im3, log_dt3, b_re_t, b_im_t,
        d_lr.reshape(G, 1, P), d_li.reshape(G, 1, P), d_bb[0], d_bb[1])
    d_q, d_k, d_v, g_q, g_k = _attn_bwd(proj, gq2, gk2, rtot, d_sb, B, L)

    d_proj = jnp.concatenate([d_q, d_k, d_v, _from_scan_layout(du_p, B, L)], axis=1)
    g_w_in = _matmul("proj_dw", xn, d_proj, ta=True, col_blocked=True, tn=w_in.shape[1] // N_DEV)
    d_xn = _matmul("proj_dx", d_proj, w_in, tb=True)
    grad_x, g_norm1 = _rowwise("norm1_bwd", norm_bwd_res, [x2, d_xn, d_h1], [g1], [(D, F32)], sums=[(1, D)])

    small = {
        "norm1_g": g_norm1.reshape(-1),
        "q_norm_g": g_q[0, :HEAD_DIM],
        "k_norm_g": g_k[0, :HEAD_DIM],
        "ssm_lambda_re": g_lam_re.reshape(G, P),
        "ssm_lambda_im": g_lam_im.reshape(G, P),
        "ssm_log_dt": g_log_dt.reshape(G),
        "ssm_b_re": g_b_re_t.transpose(0, 2, 1),
        "ssm_b_im": g_b_im_t.transpose(0, 2, 1),
        "ssm_c_re": d_c[0],
        "ssm_c_im": -d_c[1],
        "ssm_d": g_d.reshape(G, Hh),
        "b_glu": g_b_glu.reshape(-1),
        "attn_out_g": g_attn_out.reshape(-1),
        "ssm_out_g": g_ssm_out.reshape(-1),
        "norm2_g": g_norm2.reshape(-1),
    }
    big = {"w_in": g_w_in, "w_glu": g_w_glu, "w_out": g_w_out, "w_mlp_in": g_w_mlp_in, "w_mlp_out": g_w_mlp_out}
    return loss[0, 0], grad_x.reshape(B, L, D), small, big


_ANY = pl.BlockSpec(memory_space=pl.ANY)
_MESH = pl.DeviceIdType.MESH


def _all_gather(name, shards):
    n = len(shards)

    def body(*refs):
        in_refs, out_refs = refs[:n], refs[n:2 * n]
        send_sems, recv_sems, local_sems = refs[2 * n:]
        x, y, c = lax.axis_index("x"), lax.axis_index("y"), lax.axis_index("c")
        me, sibling = (x, y, c), (x, y, 1 - c)
        chips = [(1 - x, y), (x, 1 - y), (1 - x, 1 - y)]

        def copy(a, k, block, to, src=None):
            px, py, pc = block
            rows = out_refs[a].at[4 * px + 2 * py + pc]
            return pltpu.make_async_remote_copy(
                src_ref=rows if src is None else src, dst_ref=rows, send_sem=send_sems.at[a, k],
                recv_sem=recv_sems.at[a, k], device_id=to, device_id_type=_MESH)

        mine = [pltpu.make_async_copy(in_refs[a], out_refs[a].at[4 * x + 2 * y + c], local_sems.at[a]) for a in range(n)]
        first, passed = [], []
        for a in range(n):
            mine[a].start()
            first.append(copy(a, 0, me, sibling, src=in_refs[a]))
            first += [copy(a, 1 + j, me, (*chip, c), src=in_refs[a]) for j, chip in enumerate(chips)]
        for cp in first:
            cp.start()
        for j, chip in enumerate(chips):
            for a in range(n):
                copy(a, 1 + j, (*chip, c), me).wait_recv()
                fwd = copy(a, 4 + j, (*chip, c), sibling)
                fwd.start()
                passed.append(fwd)
        for a in range(n):
            copy(a, 0, sibling, me).wait_recv()
            for j, chip in enumerate(chips):
                copy(a, 4 + j, (*chip, 1 - c), me).wait_recv()
        for cp in first + passed:
            cp.wait_send()
        for cp in mine:
            cp.wait()

    return pl.pallas_call(
        body, name=name,
        in_specs=[_ANY] * n, out_specs=[_ANY] * n,
        out_shape=[jax.ShapeDtypeStruct((N_DEV, *s.shape), s.dtype) for s in shards],
        scratch_shapes=[pltpu.SemaphoreType.DMA((n, 7)), pltpu.SemaphoreType.DMA((n, 7)), pltpu.SemaphoreType.DMA((n,))],
    )(*shards)


def _sibling_exchange(name, grads):
    n = len(grads)

    def body(*refs):
        g_refs, r_refs = refs[:n], refs[n:2 * n]
        send_sems, recv_sems = refs[2 * n:]
        x, y, c = lax.axis_index("x"), lax.axis_index("y"), lax.axis_index("c")
        copies = [
            pltpu.make_async_remote_copy(
                src_ref=g_refs[a].at[k, 1 - c], dst_ref=r_refs[a].at[k], send_sem=send_sems.at[a, k],
                recv_sem=recv_sems.at[a, k], device_id=(x, y, 1 - c), device_id_type=_MESH)
            for a in range(n) for k in range(4)]
        for cp in copies:
            cp.start()
        for cp in copies:
            cp.wait()

    return pl.pallas_call(
        body, name=name, in_specs=[_ANY] * n, out_specs=[_ANY] * n,
        out_shape=[jax.ShapeDtypeStruct((4, *g.shape[2:]), g.dtype) for g in grads],
        scratch_shapes=[pltpu.SemaphoreType.DMA((n, 4)), pltpu.SemaphoreType.DMA((n, 4))],
    )(*grads)


def _chip_exchange(name, parts):
    n = len(parts)

    def body(*refs):
        p_refs, r_refs = refs[:n], refs[n:2 * n]
        send_sems, recv_sems = refs[2 * n:]
        x, y, c = lax.axis_index("x"), lax.axis_index("y"), lax.axis_index("c")
        chips = [(1 - x, y), (x, 1 - y), (1 - x, 1 - y)]
        copies = [
            pltpu.make_async_remote_copy(
                src_ref=p_refs[a].at[2 * px + py], dst_ref=r_refs[a].at[j], send_sem=send_sems.at[a, j],
                recv_sem=recv_sems.at[a, j], device_id=(px, py, c), device_id_type=_MESH)
            for a in range(n) for j, (px, py) in enumerate(chips)]
        for cp in copies:
            cp.start()
        for cp in copies:
            cp.wait()

    return pl.pallas_call(
        body, name=name, in_specs=[_ANY] * n, out_specs=[_ANY] * n,
        out_shape=[jax.ShapeDtypeStruct((3, *p.shape[1:]), p.dtype) for p in parts],
        scratch_shapes=[pltpu.SemaphoreType.DMA((n, 3)), pltpu.SemaphoreType.DMA((n, 3))],
    )(*parts)


def _chip_partial(name, g4, r1, core):
    _, _, r, c = g4.shape
    tr = min(r, 256)

    def body(core_ref, g_ref, r_ref, o32_ref, o16_ref):
        s = g_ref[...] + r_ref[...]
        o32_ref[...] = s
        o16_ref[...] = s.astype(o16_ref.dtype)

    spec = pl.BlockSpec((None, tr, c), lambda k, i, core_ref: (k, i, 0))
    return pl.pallas_call(
        body, name=name,
        grid_spec=pltpu.PrefetchScalarGridSpec(
            num_scalar_prefetch=1, grid=(4, r // tr),
            in_specs=[pl.BlockSpec((None, None, tr, c), lambda k, i, core_ref: (k, core_ref[0], i, 0)), spec],
            out_specs=[spec, spec]),
        out_shape=[jax.ShapeDtypeStruct((4, r, c), F32), jax.ShapeDtypeStruct((4, r, c), BF16)],
        compiler_params=_params(("parallel", "parallel")),
    )(core, g4, r1)


def _adamw(w, g, m, v):
    m = ADAM_B1 * m + (1.0 - ADAM_B1) * g
    v = ADAM_B2 * v + (1.0 - ADAM_B2) * jnp.square(g)
    m_hat = m / (1.0 - ADAM_B1 ** ADAM_STEP)
    v_hat = v / (1.0 - ADAM_B2 ** ADAM_STEP)
    delta = -ADAM_LR * (m_hat / (jnp.sqrt(v_hat) + ADAM_EPS) + ADAM_WD * w)
    return delta, m, v


def _adamw_shard(name, p32, r2, chip, w, m, v):
    r, c = w.shape
    tr = min(r, 256)

    def body(chip_ref, p_ref, r_ref, w_ref, m_ref, v_ref, g_out, d_out, m_out, v_out):
        g = p_ref[...]
        for j in range(3):
            g = g + r_ref[j].astype(F32)
        delta, m_new, v_new = _adamw(w_ref[...], g, m_ref[...], v_ref[...])
        g_out[...] = g
        d_out[...] = delta
        m_out[...] = m_new
        v_out[...] = v_new

    spec = pl.BlockSpec((tr, c), lambda i, chip_ref: (i, 0))
    return pl.pallas_call(
        body, name=name,
        grid_spec=pltpu.PrefetchScalarGridSpec(
            num_scalar_prefetch=1, grid=(r // tr,),
            in_specs=[pl.BlockSpec((None, tr, c), lambda i, chip_ref: (chip_ref[0], i, 0)),
                      pl.BlockSpec((3, tr, c), lambda i, chip_ref: (0, i, 0)), spec, spec, spec],
            out_specs=[spec] * 4),
        out_shape=[jax.ShapeDtypeStruct((r, c), F32)] * 4,
        compiler_params=_params(("parallel",)),
    )(chip, p32, r2, w, m, v)


def _adamw_small(name, parts, w, m, v):
    _, r, c = parts.shape
    tr = 8

    def body(p_ref, w_ref, m_ref, v_ref, g_out, d_out, m_out, v_out):
        g = p_ref[0]
        for j in range(1, N_DEV):
            g = g + p_ref[j]
        delta, m_new, v_new = _adamw(w_ref[...], g, m_ref[...], v_ref[...])
        g_out[...] = g
        d_out[...] = delta
        m_out[...] = m_new
        v_out[...] = v_new

    spec = pl.BlockSpec((tr, c), lambda i: (i, 0))
    return pl.pallas_call(
        body, name=name, grid=(r // tr,),
        in_specs=[pl.BlockSpec((N_DEV, tr, c), lambda i: (0, i, 0)), spec, spec, spec],
        out_specs=[spec] * 4, out_shape=[jax.ShapeDtypeStruct((r, c), F32)] * 4,
        compiler_params=_params(("parallel",)),
    )(parts, w, m, v)


_WEIGHTS = ["norm1_g", "w_in", "q_norm_g", "k_norm_g", "ssm_lambda_re", "ssm_lambda_im", "ssm_log_dt", "ssm_b_re",
            "ssm_b_im", "ssm_c_re", "ssm_c_im", "ssm_d", "w_glu", "b_glu", "attn_out_g", "ssm_out_g", "w_out",
            "norm2_g", "w_mlp_in", "w_mlp_out"]
_BIG = ["w_in", "w_glu", "w_out", "w_mlp_in", "w_mlp_out"]
_SMALL = [n for n in _WEIGHTS if n not in _BIG]
_PACK_COLS = 1024


def _pack(tree):
    flat = jnp.concatenate([tree[n].reshape(-1).astype(F32) for n in _SMALL])
    rows = -(-flat.shape[0] // (_PACK_COLS * 8)) * 8
    return jnp.pad(flat, (0, rows * _PACK_COLS - flat.shape[0])).reshape(rows, _PACK_COLS)


def _unpack(buf, like):
    flat, out, off = buf.reshape(-1), {}, 0
    for n in _SMALL:
        size = like[n].size
        out[n] = flat[off:off + size].reshape(like[n].shape)
        off += size
    return out


def kernel(x, norm1_g, w_in, q_norm_g, k_norm_g, ssm_lambda_re, ssm_lambda_im, ssm_log_dt, ssm_b_re, ssm_b_im, ssm_c_re, ssm_c_im, ssm_d, w_glu, b_glu, attn_out_g, ssm_out_g, w_out, norm2_g, w_mlp_in, w_mlp_out, loss_target, m_norm1_g, m_w_in, m_q_norm_g, m_k_norm_g, m_ssm_lambda_re, m_ssm_lambda_im, m_ssm_log_dt, m_ssm_b_re, m_ssm_b_im, m_ssm_c_re, m_ssm_c_im, m_ssm_d, m_w_glu, m_b_glu, m_attn_out_g, m_ssm_out_g, m_w_out, m_norm2_g, m_w_mlp_in, m_w_mlp_out, v_norm1_g, v_w_in, v_q_norm_g, v_k_norm_g, v_ssm_lambda_re, v_ssm_lambda_im, v_ssm_log_dt, v_ssm_b_re, v_ssm_b_im, v_ssm_c_re, v_ssm_c_im, v_ssm_d, v_w_glu, v_b_glu, v_attn_out_g, v_ssm_out_g, v_w_out, v_norm2_g, v_w_mlp_in, v_w_mlp_out):
    w = dict(zip(_WEIGHTS, (norm1_g, w_in, q_norm_g, k_norm_g, ssm_lambda_re, ssm_lambda_im, ssm_log_dt, ssm_b_re, ssm_b_im, ssm_c_re, ssm_c_im, ssm_d, w_glu, b_glu, attn_out_g, ssm_out_g, w_out, norm2_g, w_mlp_in, w_mlp_out)))
    m = dict(zip(_WEIGHTS, (m_norm1_g, m_w_in, m_q_norm_g, m_k_norm_g, m_ssm_lambda_re, m_ssm_lambda_im, m_ssm_log_dt, m_ssm_b_re, m_ssm_b_im, m_ssm_c_re, m_ssm_c_im, m_ssm_d, m_w_glu, m_b_glu, m_attn_out_g, m_ssm_out_g, m_w_out, m_norm2_g, m_w_mlp_in, m_w_mlp_out)))
    v = dict(zip(_WEIGHTS, (v_norm1_g, v_w_in, v_q_norm_g, v_k_norm_g, v_ssm_lambda_re, v_ssm_lambda_im, v_ssm_log_dt, v_ssm_b_re, v_ssm_b_im, v_ssm_c_re, v_ssm_c_im, v_ssm_d, v_w_glu, v_b_glu, v_attn_out_g, v_ssm_out_g, v_w_out, v_norm2_g, v_w_mlp_in, v_w_mlp_out)))
    axes = ("x", "y", "c")
    core = lax.axis_index("c").astype(jnp.int32).reshape(1)
    chip = (2 * lax.axis_index("x") + lax.axis_index("y")).astype(jnp.int32).reshape(1)

    gathered = dict(zip(_BIG, _all_gather("weights_all_gather", [w[n].astype(BF16) for n in _BIG])))
    full = {
        "w_in": gathered["w_in"].transpose(1, 0, 2).reshape(w_in.shape[0], -1),
        "w_glu": gathered["w_glu"].reshape(-1, w_glu.shape[1]),
        "w_out": gathered["w_out"].reshape(-1, w_out.shape[1]),
        "w_mlp_in": gathered["w_mlp_in"].transpose(1, 0, 2).reshape(w_mlp_in.shape[0], -1),
        "w_mlp_out": gathered["w_mlp_out"].reshape(-1, w_mlp_out.shape[1]),
    }

    loss_local, grad_x, g_small, g_big = _local_step(
        x, loss_target, {n: w[n] for n in _SMALL}, full["w_in"], full["w_glu"], full["w_out"], full["w_mlp_in"],
        full["w_mlp_out"])
    loss = lax.psum(loss_local, axes)

    g4 = [g_big[n].reshape(4, 2, *w[n].shape) for n in _BIG]
    r1 = _sibling_exchange("grads_sibling_exchange", g4)
    parts = [_chip_partial("chip_partial_" + n, a, b, core) for n, a, b in zip(_BIG, g4, r1)]
    r2 = _chip_exchange("grads_chip_exchange", [p16 for _, p16 in parts])
    grads, delta, new_m, new_v = {}, {}, {}, {}
    for n, (p32, _), r in zip(_BIG, parts, r2):
        grads[n], delta[n], new_m[n], new_v[n] = _adamw_shard("adamw_" + n, p32, r, chip, w[n], m[n], v[n])

    (small_parts,) = _all_gather("small_grads_all_gather", [_pack(g_small)])
    packed = _adamw_small("adamw_small", small_parts, _pack(w), _pack(m), _pack(v))
    for tree, buf in zip((grads, delta, new_m, new_v), packed):
        tree.update(_unpack(buf, w))

    return (loss, grad_x, *[grads[n] for n in _WEIGHTS], *[delta[n] for n in _WEIGHTS],
            *[new_m[n] for n in _WEIGHTS], *[new_v[n] for n in _WEIGHTS])
```

```python
import functools
import math

import jax
import jax.numpy as jnp
from jax import lax
from jax.experimental import pallas as pl
from jax.experimental.pallas import tpu as pltpu

F32 = jnp.float32
BF16 = jnp.bfloat16

EPS = 1e-6
HEAD_DIM = 64
N_HEADS = 8
SB_WIDTH = 512
SSM_WIDTH = 512
SSM_GROUP = 16
SSM_GROUPS = 32
SSM_STATE = 64
QBLOCK = 128
KBLOCK = 256
N_CHUNK = 8
SSM_COLS = 4
LANES = 128
N_DEV = 8

ADAM_LR = 0.001
ADAM_B1 = 0.9
ADAM_B2 = 0.999
ADAM_EPS = 1e-08
ADAM_WD = 0.01
ADAM_STEP = 10

VMEM_LIMIT = 56 * 1024 * 1024

_NT = (((1,), (1,)), ((), ()))
_NN = (((1,), (0,)), ((), ()))
_TN = (((0,), (0,)), ((), ()))


def _dot(a, b, dims=_NN):
    return lax.dot_general(a, b, dims, preferred_element_type=F32)


def _params(sem):
    return pltpu.CompilerParams(dimension_semantics=sem, vmem_limit_bytes=VMEM_LIMIT)


def _matmul(name, a, b, *, ta=False, tb=False, extras=(), epilogue=None, out_dtypes=(F32,),
            col_blocked=False, tm=1024, tn=512, tk=1024):
    M, K = (a.shape[1], a.shape[0]) if ta else a.shape
    N = b.shape[0] if tb else b.shape[1]
    tm, tn, tk = min(tm, M), min(tn, N), min(tk, K)
    assert M % tm == 0 and N % tn == 0 and K % tk == 0, (name, M, N, K)
    nk = K // tk
    n_ex, n_out = len(extras), len(out_dtypes)
    dims = (((0 if ta else 1,), (1 if tb else 0,)), ((), ()))

    def body(*refs):
        a_ref, b_ref = refs[0], refs[1]
        ex_refs = refs[2:2 + n_ex]
        o_refs = refs[2 + n_ex:2 + n_ex + n_out]
        k = pl.program_id(2)
        part = _dot(a_ref[...].astype(BF16), b_ref[...].astype(BF16), dims)

        def finish(acc):
            outs = (acc,) if epilogue is None else epilogue(acc, *[e[...] for e in ex_refs])
            for o_ref, o in zip(o_refs, outs):
                o_ref[...] = o.astype(o_ref.dtype)

        if nk == 1:
            finish(part)
        else:
            acc_ref = refs[-1]

            @pl.when(k == 0)
            def _():
                acc_ref[...] = part

            @pl.when(jnp.logical_and(k > 0, k < nk - 1))
            def _():
                acc_ref[...] += part

            @pl.when(k == nk - 1)
            def _():
                finish(acc_ref[...] + part)

    a_spec = pl.BlockSpec((tk, tm), lambda i, j, k: (k, i)) if ta else pl.BlockSpec((tm, tk), lambda i, j, k: (i, k))
    b_spec = pl.BlockSpec((tn, tk), lambda i, j, k: (j, k)) if tb else pl.BlockSpec((tk, tn), lambda i, j, k: (k, j))
    ex_specs = [pl.BlockSpec((tm, tn), lambda i, j, k: (i, j)) for _ in extras]
    if col_blocked:
        out_specs = [pl.BlockSpec((None, tm, tn), lambda i, j, k: (j, i, 0)) for _ in out_dtypes]
        out_shape = [jax.ShapeDtypeStruct((N // tn, M, tn), dt) for dt in out_dtypes]
    else:
        out_specs = [pl.BlockSpec((tm, tn), lambda i, j, k: (i, j)) for _ in out_dtypes]
        out_shape = [jax.ShapeDtypeStruct((M, N), dt) for dt in out_dtypes]
    outs = pl.pallas_call(
        body, name=name, grid=(M // tm, N // tn, nk),
        in_specs=[a_spec, b_spec, *ex_specs], out_specs=out_specs, out_shape=out_shape,
        scratch_shapes=[pltpu.VMEM((tm, tn), F32)] if nk > 1 else [],
        compiler_params=_params(("parallel", "parallel", "arbitrary")),
    )(a, b, *extras)
    return outs[0] if n_out == 1 else outs


def _rowwise(name, fn, rows, small, outs, sums=(), tile=256):
    specs, args = [], []
    T = None
    for r in rows:
        arr, cb, w = r if isinstance(r, tuple) else (r, 0, r.shape[1])
        T = arr.shape[0]
        specs.append((w, cb))
        args.append(arr)
    tile = min(tile, T)
    assert T % tile == 0
    n_r, n_s, n_o, n_a = len(rows), len(small), len(outs), len(sums)

    def body(*refs):
        r_refs = refs[:n_r]
        s_refs = refs[n_r:n_r + n_s]
        o_refs = refs[n_r + n_s:n_r + n_s + n_o]
        a_refs = refs[n_r + n_s + n_o:]
        res = fn(*[r[...] for r in r_refs], *[s[...] for s in s_refs])
        res = res if isinstance(res, (tuple, list)) else (res,)
        for o_ref, o in zip(o_refs, res[:n_o]):
            o_ref[...] = o.astype(o_ref.dtype)

        @pl.when(pl.program_id(0) == 0)
        def _():
            for a_ref in a_refs:
                a_ref[...] = jnp.zeros_like(a_ref)

        for a_ref, v in zip(a_refs, res[n_o:]):
            a_ref[...] += v.astype(F32)

    in_specs = [pl.BlockSpec((tile, w), functools.partial(lambda i, cb: (i, cb), cb=cb)) for w, cb in specs]
    in_specs += [pl.BlockSpec(s.shape, functools.partial(lambda i, nd: (0,) * nd, nd=s.ndim)) for s in small]
    out_specs = [pl.BlockSpec((tile, w), lambda i: (i, 0)) for w, _ in outs]
    out_specs += [pl.BlockSpec(s, functools.partial(lambda i, nd: (0,) * nd, nd=len(s))) for s in sums]
    out_shape = [jax.ShapeDtypeStruct((T, w), dt) for w, dt in outs]
    out_shape += [jax.ShapeDtypeStruct(s, F32) for s in sums]
    res = pl.pallas_call(
        body, name=name, grid=(T // tile,), in_specs=in_specs, out_specs=out_specs, out_shape=out_shape,
        compiler_params=_params(("arbitrary",)),
    )(*args, *small)
    return res[0] if len(res) == 1 else res


def _rms(x, g):
    return x * lax.rsqrt(jnp.mean(x * x, axis=-1, keepdims=True) + EPS) * g


def _glu_branch(y, pre, b_glu, g_out):
    g = jax.nn.gelu(y)
    return _rms(g * jax.nn.sigmoid(pre + b_glu), g_out)


def _split_dot(x, tri_bf):
    hi = x.astype(BF16)
    lo = (x - hi.astype(F32)).astype(BF16)
    return _dot(hi, tri_bf) + _dot(lo, tri_bf)


def _softplus(z):
    return jnp.maximum(z, 0.0) + jnp.log1p(jnp.exp(-jnp.abs(z)))


def _head(h):
    return slice(h * HEAD_DIM, (h + 1) * HEAD_DIM)


def _attn_norm_to_scratch(q_ref, k_ref, v_ref, gq_ref, gk_ref, qn_s, kn_s, v_s, L, qnt_s=None):
    scale = 1.0 / math.sqrt(HEAD_DIM)

    def chunk(i, _):
        r = pl.ds(pl.multiple_of(i * QBLOCK, QBLOCK), QBLOCK)
        qn = [_rms(q_ref[r, _head(h)], gq_ref[:, _head(h)]) * scale for h in range(2)]
        for h in range(2):
            qn_s[h, r, :] = qn[h].astype(qn_s.dtype)
            kn_s[h, r, :] = _rms(k_ref[r, _head(h)], gk_ref[:, _head(h)]).astype(kn_s.dtype)
            v_s[h, r, :] = v_ref[r, _head(h)].astype(v_s.dtype)
        if qnt_s is not None:
            qnt_s[:, r] = jnp.concatenate(qn, axis=1).T.astype(qnt_s.dtype)
        return 0

    lax.fori_loop(0, L // QBLOCK, chunk, 0)


def _key_blocks(i):
    return lax.div(i * QBLOCK, KBLOCK) + 1


def _valid(i, kb):
    row = lax.broadcasted_iota(jnp.int32, (QBLOCK, KBLOCK), 0)
    col = lax.broadcasted_iota(jnp.int32, (QBLOCK, KBLOCK), 1)
    return col + (kb * KBLOCK - i * QBLOCK) < row


def _attn_fwd(proj, gq2, gk2, B, L):
    nq = L // QBLOCK
    n_hp = N_HEADS // 2

    def body(q_ref, k_ref, v_ref, gq_ref, gk_ref, o_ref, rt_ref, qn_s, kn_s, v_s, after_s):
        _attn_norm_to_scratch(q_ref, k_ref, v_ref, gq_ref, gk_ref, qn_s, kn_s, v_s, L)
        r2 = lax.broadcasted_iota(jnp.int32, (KBLOCK, KBLOCK), 0)
        c2 = lax.broadcasted_iota(jnp.int32, (KBLOCK, KBLOCK), 1)
        after_s[...] = (r2 > c2).astype(after_s.dtype)

        def q_block(i, _):
            rq = pl.ds(pl.multiple_of(i * QBLOCK, QBLOCK), QBLOCK)
            q_i = [qn_s[h, rq, :] for h in range(2)]
            nkb = _key_blocks(i)

            def k_block(n, carry):
                kb = nkb - 1 - n
                rk = pl.ds(pl.multiple_of(kb * KBLOCK, KBLOCK), KBLOCK)
                valid = _valid(i, kb)
                hs = range(2)
                z = [_dot(q_i[h], kn_s[h, rk, :], _NT) for h in hs]
                sp = [_softplus(z[h]) for h in hs]
                lom = [jnp.where(valid, -sp[h], 0.0) for h in hs]
                tail = [_split_dot(lom[h], after_s[...]) + carry[h][0] for h in hs]
                a = [jnp.where(valid, jnp.exp(z[h] - sp[h] + tail[h]), 0.0) for h in hs]
                acc = [carry[h][1] + _dot(a[h].astype(v_s.dtype), v_s[h, rk, :]) for h in hs]
                return tuple((carry[h][0] + jnp.sum(lom[h], axis=1, keepdims=True), acc[h]) for h in hs)

            init = (jnp.zeros((QBLOCK, 1), F32), jnp.zeros((QBLOCK, HEAD_DIM), F32))
            res = lax.fori_loop(0, nkb, k_block, (init, init))
            o_ref[rq, :] = jnp.concatenate([res[0][1], res[1][1]], axis=1)
            rt_ref[rq, :] = jnp.concatenate(
                [jnp.broadcast_to(res[h][0], (QBLOCK, HEAD_DIM)) for h in range(2)], axis=1)
            return 0

        lax.fori_loop(0, nq, q_block, 0)

    blk = lambda off: pl.BlockSpec((L, LANES), functools.partial(lambda b, p, off: (b, off + p), off=off))
    gspec = pl.BlockSpec((1, LANES), lambda b, p: (0, 0))
    ospec = pl.BlockSpec((L, LANES), lambda b, p: (b, p))
    return pl.pallas_call(
        body, name="attn_fwd", grid=(B, n_hp),
        in_specs=[blk(0), blk(n_hp), blk(2 * n_hp), gspec, gspec],
        out_specs=[ospec, ospec],
        out_shape=[jax.ShapeDtypeStruct((B * L, SB_WIDTH), F32)] * 2,
        scratch_shapes=[pltpu.VMEM((2, L, HEAD_DIM), BF16)] * 3 + [pltpu.VMEM((KBLOCK, KBLOCK), BF16)],
        compiler_params=_params(("parallel", "parallel")),
    )(proj, proj, proj, gq2, gk2)


def _attn_bwd(proj, gq2, gk2, rtot, d_sb, B, L):
    nq = L // QBLOCK
    n_hp = N_HEADS // 2
    scale = 1.0 / math.sqrt(HEAD_DIM)

    def body(q_ref, k_ref, v_ref, gq_ref, gk_ref, rt_ref, do_ref, dq_ref, dk_ref, dv_ref, dgq_ref, dgk_ref,
             qn_s, kn_s, v_s, qnt_s, dkt_s, dvt_s, after_s, before_s):
        _attn_norm_to_scratch(q_ref, k_ref, v_ref, gq_ref, gk_ref, qn_s, kn_s, v_s, L, qnt_s)
        dkt_s[...] = jnp.zeros_like(dkt_s)
        dvt_s[...] = jnp.zeros_like(dvt_s)
        r2 = lax.broadcasted_iota(jnp.int32, (KBLOCK, KBLOCK), 0)
        c2 = lax.broadcasted_iota(jnp.int32, (KBLOCK, KBLOCK), 1)
        after_s[...] = (r2 > c2).astype(after_s.dtype)
        before_s[...] = (r2 < c2).astype(before_s.dtype)

        @pl.when(jnp.logical_and(pl.program_id(0) == 0, pl.program_id(1) == 0))
        def _():
            dgq_ref[...] = jnp.zeros_like(dgq_ref)
            dgk_ref[...] = jnp.zeros_like(dgk_ref)

        def q_block(i, dgq):
            rq = pl.ds(pl.multiple_of(i * QBLOCK, QBLOCK), QBLOCK)
            do2 = do_ref[rq, :]
            do_t = do2.T.astype(v_s.dtype)
            tot2 = rt_ref[rq, :]
            q_i = [qn_s[h, rq, :] for h in range(2)]
            qt_i = [qnt_s[_head(h), rq] for h in range(2)]
            do_i = [do2[:, _head(h)].astype(v_s.dtype) for h in range(2)]
            dot_i = [do_t[_head(h), :] for h in range(2)]
            total = [tot2[:, h * HEAD_DIM:h * HEAD_DIM + 1] for h in range(2)]

            def k_block(kb, carry):
                rk = pl.ds(pl.multiple_of(kb * KBLOCK, KBLOCK), KBLOCK)
                valid = _valid(i, kb)
                hs = range(2)
                k_b = [kn_s[h, rk, :] for h in hs]
                z = [_dot(q_i[h], k_b[h], _NT) for h in hs]
                da = [_dot(do_i[h], v_s[h, rk, :], _NT) for h in hs]
                sp = [_softplus(z[h]) for h in hs]
                lom = [jnp.where(valid, -sp[h], 0.0) for h in hs]
                lom_sum = [jnp.sum(lom[h], axis=1, keepdims=True) for h in hs]
                tail = [_split_dot(lom[h], after_s[...]) + (total[h] - carry[h][0] - lom_sum[h]) for h in hs]
                a = [jnp.where(valid, jnp.exp(z[h] - sp[h] + tail[h]), 0.0) for h in hs]
                dla = [a[h] * da[h] for h in hs]
                for h in hs:
                    dvt_s[_head(h), rk] += _dot(dot_i[h], a[h].astype(v_s.dtype))
                d_lom = [carry[h][1] + _split_dot(dla[h], before_s[...]) for h in hs]
                beta = [jnp.exp(z[h] - sp[h]) for h in hs]
                dz_b = [(dla[h] * (1.0 - beta[h]) - jnp.where(valid, beta[h] * d_lom[h], 0.0)).astype(v_s.dtype)
                        for h in hs]
                dq_acc = [carry[h][2] + _dot(dz_b[h], k_b[h]) for h in hs]
                for h in hs:
                    dkt_s[_head(h), rk] += _dot(qt_i[h], dz_b[h])
                return tuple((carry[h][0] + lom_sum[h], carry[h][1] + jnp.sum(dla[h], axis=1, keepdims=True),
                              dq_acc[h]) for h in hs)

            zero = jnp.zeros((QBLOCK, 1), F32)
            init = (zero, zero, jnp.zeros((QBLOCK, HEAD_DIM), F32))
            res = lax.fori_loop(0, _key_blocks(i), k_block, (init, init))
            dq = []
            for h in range(2):
                _, vjp_q = jax.vjp(_rms, q_ref[rq, _head(h)], gq_ref[:, _head(h)])
                dq_h, dgq_h = vjp_q(res[h][2] * scale)
                dq.append(dq_h)
                dgq = dgq + dgq_h
            dq_ref[rq, :] = jnp.concatenate(dq, axis=1).astype(dq_ref.dtype)
            return dgq

        dgq = lax.fori_loop(0, nq, q_block, jnp.zeros((1, HEAD_DIM), F32))

        def k_norm_bwd(i, dgk):
            r = pl.ds(pl.multiple_of(i * QBLOCK, QBLOCK), QBLOCK)
            dkn = dkt_s[:, r].T
            dk = []
            for h in range(2):
                _, vjp_k = jax.vjp(_rms, k_ref[r, _head(h)], gk_ref[:, _head(h)])
                dk_h, dgk_h = vjp_k(dkn[:, _head(h)])
                dk.append(dk_h)
                dgk = dgk + dgk_h
            dk_ref[r, :] = jnp.concatenate(dk, axis=1).astype(dk_ref.dtype)
            dv_ref[r, :] = dvt_s[:, r].T.astype(dv_ref.dtype)
            return dgk

        dgk = lax.fori_loop(0, nq, k_norm_bwd, jnp.zeros((1, HEAD_DIM), F32))
        dgq_ref[:, 0:HEAD_DIM] += dgq
        dgk_ref[:, 0:HEAD_DIM] += dgk

    blk = lambda off: pl.BlockSpec((L, LANES), functools.partial(lambda b, p, off: (b, off + p), off=off))
    gspec = pl.BlockSpec((1, LANES), lambda b, p: (0, 0))
    ospec = pl.BlockSpec((L, LANES), lambda b, p: (b, p))
    return pl.pallas_call(
        body, name="attn_bwd", grid=(B, n_hp),
        in_specs=[blk(0), blk(n_hp), blk(2 * n_hp), gspec, gspec, ospec, ospec],
        out_specs=[ospec, ospec, ospec, gspec, gspec],
        out_shape=[jax.ShapeDtypeStruct((B * L, SB_WIDTH), BF16)] * 3 + [jax.ShapeDtypeStruct((1, LANES), F32)] * 2,
        scratch_shapes=[pltpu.VMEM((2, L, HEAD_DIM), BF16)] * 3 + [pltpu.VMEM((LANES, L), BF16)]
        + [pltpu.VMEM((LANES, L), F32)] * 2 + [pltpu.VMEM((KBLOCK, KBLOCK), BF16)] * 2,
        compiler_params=_params(("arbitrary", "arbitrary")),
    )(proj, proj, proj, gq2, gk2, rtot, d_sb)


def _ssm_discretise(lam_re, lam_im, log_dt, b_re, b_im):
    dt = jnp.exp(log_dt)
    mag = jnp.exp(lam_re * dt)
    lbr = mag * jnp.cos(lam_im * dt)
    lbi = mag * jnp.sin(lam_im * dt)
    den = lam_re * lam_re + lam_im * lam_im
    nr, ni = lbr - 1.0, lbi
    cr = (nr * lam_re + ni * lam_im) / den
    ci = (ni * lam_re - nr * lam_im) / den
    return lbr, lbi, cr * b_re - ci * b_im, cr * b_im + ci * b_re


def _ssm_prep(lam_re, lam_im, log_dt, b_re_t, b_im_t):
    def body(lr, li, ld, br, bi, o_lr, o_li, o_br, o_bi):
        res = _ssm_discretise(lr[...], li[...], ld[...], br[...], bi[...])
        for o, v in zip((o_lr, o_li, o_br, o_bi), res):
            o[...] = v

    return pl.pallas_call(
        body, name="ssm_prep",
        out_shape=[jax.ShapeDtypeStruct(lam_re.shape, F32)] * 2 + [jax.ShapeDtypeStruct(b_re_t.shape, F32)] * 2,
    )(lam_re, lam_im, log_dt, b_re_t, b_im_t)


def _ssm_prep_bwd(lam_re, lam_im, log_dt, b_re_t, b_im_t, d_lr, d_li, d_br, d_bi):
    def body(lr, li, ld, br, bi, g_lr, g_li, g_br, g_bi, o_lr, o_li, o_ld, o_br, o_bi):
        _, vjp = jax.vjp(_ssm_discretise, lr[...], li[...], ld[...], br[...], bi[...])
        res = vjp((g_lr[...], g_li[...], g_br[...], g_bi[...]))
        for o, v in zip((o_lr, o_li, o_ld, o_br, o_bi), res):
            o[...] = v

    return pl.pallas_call(
        body, name="ssm_prep_bwd",
        out_shape=[jax.ShapeDtypeStruct(lam_re.shape, F32)] * 2 + [jax.ShapeDtypeStruct(log_dt.shape, F32)]
        + [jax.ShapeDtypeStruct(b_re_t.shape, F32)] * 2,
    )(lam_re, lam_im, log_dt, b_re_t, b_im_t, d_lr, d_li, d_br, d_bi)


def _block_diag(m):
    m4 = m.reshape(SSM_COLS, 8, SSM_GROUP, SSM_STATE)
    return jnp.einsum("aghp,gk->aghkp", m4, jnp.eye(8, dtype=m.dtype)).reshape(SSM_COLS, LANES, 512)


def _block_diag_take(d):
    d6 = d.reshape(SSM_COLS, 8, SSM_GROUP, 2, 8, SSM_STATE)
    return jnp.einsum("aghrgp->raghp", d6).reshape(2, SSM_GROUPS, SSM_GROUP, SSM_STATE)


def _cmul(ar, ai, br, bi):
    return ar * br - ai * bi, ar * bi + ai * br


def _power(lr, li, n):
    assert n & (n - 1) == 0
    for _ in range(n.bit_length() - 1):
        lr, li = _cmul(lr, li, lr, li)
    return lr, li


def _ssm_fwd(u_p, w_b, lam_r, lam_i, c_m, d_skip, B, L, tj):
    J = L // N_CHUNK
    njt = J // tj
    R = tj * N_CHUNK
    H = 512

    def body(u_ref, wb_ref, lr_ref, li_ref, cm_ref, d_ref, y_ref, x_ref, xin_ref, bu_s, st_s, xin_s):
        ph, jt = pl.program_id(2), pl.program_id(3)
        lr, li = lr_ref[...], li_ref[...]

        @pl.when(jnp.logical_and(ph == 0, jt == 0))
        def _():
            st_s[...] = jnp.zeros_like(st_s)

        bu_s[...] = _dot(u_ref[...].astype(BF16), wb_ref[...].astype(BF16))

        def scan(store):
            def step(j, carry):
                xr, xi = carry
                r = pl.ds(pl.multiple_of(j * N_CHUNK, N_CHUNK), N_CHUNK)
                nr = lr * xr - li * xi + bu_s[r, 0:H]
                ni = lr * xi + li * xr + bu_s[r, H:2 * H]
                if store:
                    x_ref[r, 0:H] = nr
                    x_ref[r, H:2 * H] = ni
                return nr, ni

            xr, xi = lax.fori_loop(0, tj, step, (st_s[:, 0:H], st_s[:, H:2 * H]))
            st_s[:, 0:H] = xr
            st_s[:, H:2 * H] = xi

        @pl.when(ph == 0)
        def _():
            scan(False)

            @pl.when(jt == njt - 1)
            def _():
                pr, pi = _power(lr[0:1], li[0:1], J)
                xin_s[0:1, :] = jnp.zeros((1, 2 * H), F32)
                for c in range(1, N_CHUNK):
                    qr, qi = _cmul(pr, pi, xin_s[c - 1:c, 0:H], xin_s[c - 1:c, H:2 * H])
                    xin_s[c:c + 1, 0:H] = qr + st_s[c - 1:c, 0:H]
                    xin_s[c:c + 1, H:2 * H] = qi + st_s[c - 1:c, H:2 * H]
                xin_ref[...] = xin_s[...]
                st_s[...] = xin_s[...]

        @pl.when(ph == 1)
        def _():
            scan(True)
            y = _dot(x_ref[...].astype(BF16), cm_ref[...].astype(BF16))
            y_ref[...] = y + d_ref[...] * u_ref[...]

    return pl.pallas_call(
        body, name="ssm_fwd", grid=(SSM_COLS, B, 2, njt),
        in_specs=[
            pl.BlockSpec((None, R, LANES), lambda i, b, ph, jt: (b, jt, i)),
            pl.BlockSpec((None, LANES, 2 * H), lambda i, b, ph, jt: (i, 0, 0)),
            pl.BlockSpec((None, N_CHUNK, H), lambda i, b, ph, jt: (i, 0, 0)),
            pl.BlockSpec((None, N_CHUNK, H), lambda i, b, ph, jt: (i, 0, 0)),
            pl.BlockSpec((None, 2 * H, LANES), lambda i, b, ph, jt: (i, 0, 0)),
            pl.BlockSpec((1, LANES), lambda i, b, ph, jt: (0, i)),
        ],
        out_specs=[
            pl.BlockSpec((None, R, LANES), lambda i, b, ph, jt: (b, jt * ph, i)),
            pl.BlockSpec((None, R, 2 * H), lambda i, b, ph, jt: (b, jt * ph, i)),
            pl.BlockSpec((None, None, N_CHUNK, 2 * H), lambda i, b, ph, jt: (b, i, 0, 0)),
        ],
        out_shape=[
            jax.ShapeDtypeStruct((B, L, SSM_WIDTH), F32),
            jax.ShapeDtypeStruct((B, L, SSM_COLS * 2 * H), F32),
            jax.ShapeDtypeStruct((B, SSM_COLS, N_CHUNK, 2 * H), F32),
        ],
        scratch_shapes=[pltpu.VMEM((R, 2 * H), F32), pltpu.VMEM((N_CHUNK, 2 * H), F32), pltpu.VMEM((N_CHUNK, 2 * H), F32)],
        compiler_params=_params(("arbitrary",) * 4),
    )(u_p, w_b, lam_r, lam_i, c_m, d_skip)


def _ssm_bwd(dy_p, u_p, x, xin, w_bt, lam_r, lam_i, c_mt, d_skip, B, L, tj):
    J = L // N_CHUNK
    njt = J // tj
    R = tj * N_CHUNK
    H = 512
    x4 = x.reshape(B, J, N_CHUNK, SSM_COLS * 2 * H)

    def body(dy_ref, u_ref, x_ref, xp_ref, xin_ref, wbt_ref, lr_ref, li_ref, cmt_ref, d_ref,
             du_ref, dwb_ref, dcm_ref, dlr_ref, dli_ref, dd_ref, ca_s, a_s, st_s, dl_s):
        b, ph, jt = pl.program_id(1), pl.program_id(2), pl.program_id(3)
        jr = njt - 1 - jt
        lr, li = lr_ref[...], -li_ref[...]

        @pl.when(jnp.logical_and(b == 0, jnp.logical_and(ph == 0, jt == 0)))
        def _():
            dwb_ref[...] = jnp.zeros_like(dwb_ref)
            dcm_ref[...] = jnp.zeros_like(dcm_ref)
            dlr_ref[...] = jnp.zeros_like(dlr_ref)
            dli_ref[...] = jnp.zeros_like(dli_ref)
            dd_ref[...] = jnp.zeros_like(dd_ref)
            dl_s[...] = jnp.zeros_like(dl_s)

        @pl.when(jnp.logical_and(ph == 0, jt == 0))
        def _():
            st_s[...] = jnp.zeros_like(st_s)

        ca_s[...] = _dot(dy_ref[...].astype(BF16), cmt_ref[...].astype(BF16))

        def scan(store):
            def step(n, carry):
                ar, ai = carry
                r = pl.ds(pl.multiple_of((tj - 1 - n) * N_CHUNK, N_CHUNK), N_CHUNK)
                nr = lr * ar - li * ai + ca_s[r, 0:H]
                ni = lr * ai + li * ar + ca_s[r, H:2 * H]
                if store:
                    a_s[r, 0:H] = nr
                    a_s[r, H:2 * H] = ni
                return nr, ni

            ar, ai = lax.fori_loop(0, tj, step, (st_s[:, 0:H], st_s[:, H:2 * H]))
            st_s[:, 0:H] = ar
            st_s[:, H:2 * H] = ai

        @pl.when(ph == 0)
        def _():
            scan(False)

            @pl.when(jt == njt - 1)
            def _():
                pr, pi = _power(lr[0:1], li[0:1], J)
                a_s[N_CHUNK - 1:N_CHUNK, :] = jnp.zeros((1, 2 * H), F32)
                for c in range(N_CHUNK - 2, -1, -1):
                    qr, qi = _cmul(pr, pi, a_s[c + 1:c + 2, 0:H], a_s[c + 1:c + 2, H:2 * H])
                    a_s[c:c + 1, 0:H] = qr + st_s[c + 1:c + 2, 0:H]
                    a_s[c:c + 1, H:2 * H] = qi + st_s[c + 1:c + 2, H:2 * H]
                st_s[...] = a_s[0:N_CHUNK, :]

        @pl.when(ph == 1)
        def _():
            scan(True)
            dy = dy_ref[...]
            u = u_ref[...]
            a_b = a_s[...].astype(BF16)
            du_ref[...] = (_dot(a_b, wbt_ref[...].astype(BF16)) + d_ref[...] * dy).astype(du_ref.dtype)
            dwb_ref[...] += _dot(u.astype(BF16), a_b, _TN)
            dcm_ref[...] += _dot(x_ref[...].astype(BF16), dy.astype(BF16), _TN)
            dd_ref[...] += jnp.sum(dy * u, axis=0, keepdims=True)

            first = jnp.where(jr == 0, xin_ref[...], xp_ref[...])
            a0r, a0i = a_s[0:N_CHUNK, 0:H], a_s[0:N_CHUNK, H:2 * H]
            acc0 = (a0r * first[:, 0:H] + a0i * first[:, H:2 * H], a0i * first[:, 0:H] - a0r * first[:, H:2 * H])

            def step(j, carry):
                sr, si = carry
                r = pl.ds(pl.multiple_of(j * N_CHUNK, N_CHUNK), N_CHUNK)
                rp = pl.ds(pl.multiple_of((j - 1) * N_CHUNK, N_CHUNK), N_CHUNK)
                ar, ai = a_s[r, 0:H], a_s[r, H:2 * H]
                xr, xi = x_ref[rp, 0:H], x_ref[rp, H:2 * H]
                return sr + ar * xr + ai * xi, si + ai * xr - ar * xi

            sr, si = lax.fori_loop(1, tj, step, acc0)
            dl_s[:, 0:H] += sr
            dl_s[:, H:2 * H] += si

            @pl.when(jnp.logical_and(b == B - 1, jt == njt - 1))
            def _():
                dlr_ref[...] = jnp.sum(dl_s[:, 0:H], axis=0, keepdims=True)
                dli_ref[...] = jnp.sum(dl_s[:, H:2 * H], axis=0, keepdims=True)
                dl_s[...] = jnp.zeros_like(dl_s)

    rev = lambda ph, jt: (njt - 1 - jt) * ph + (njt - 1) * (1 - ph)
    return pl.pallas_call(
        body, name="ssm_bwd", grid=(SSM_COLS, B, 2, njt),
        in_specs=[
            pl.BlockSpec((None, R, LANES), lambda i, b, ph, jt: (b, njt - 1 - jt, i)),
            pl.BlockSpec((None, R, LANES), lambda i, b, ph, jt: (b, njt - 1 - jt, i)),
            pl.BlockSpec((None, R, 2 * H), lambda i, b, ph, jt: (b, rev(ph, jt), i)),
            pl.BlockSpec((None, None, N_CHUNK, 2 * H),
                         lambda i, b, ph, jt: (b, jnp.maximum((njt - 1 - jt) * tj - 1, 0), 0, i)),
            pl.BlockSpec((None, None, N_CHUNK, 2 * H), lambda i, b, ph, jt: (b, i, 0, 0)),
            pl.BlockSpec((None, 2 * H, LANES), lambda i, b, ph, jt: (i, 0, 0)),
            pl.BlockSpec((None, N_CHUNK, H), lambda i, b, ph, jt: (i, 0, 0)),
            pl.BlockSpec((None, N_CHUNK, H), lambda i, b, ph, jt: (i, 0, 0)),
            pl.BlockSpec((None, LANES, 2 * H), lambda i, b, ph, jt: (i, 0, 0)),
            pl.BlockSpec((1, LANES), lambda i, b, ph, jt: (0, i)),
        ],
        out_specs=[
            pl.BlockSpec((None, R, LANES), lambda i, b, ph, jt: (b, rev(ph, jt), i)),
            pl.BlockSpec((None, LANES, 2 * H), lambda i, b, ph, jt: (i, 0, 0)),
            pl.BlockSpec((None, 2 * H, LANES), lambda i, b, ph, jt: (i, 0, 0)),
            pl.BlockSpec((None, 1, H), lambda i, b, ph, jt: (i, 0, 0)),
            pl.BlockSpec((None, 1, H), lambda i, b, ph, jt: (i, 0, 0)),
            pl.BlockSpec((1, LANES), lambda i, b, ph, jt: (0, i)),
        ],
        out_shape=[
            jax.ShapeDtypeStruct((B, L, SSM_WIDTH), BF16),
            jax.ShapeDtypeStruct((SSM_COLS, LANES, 2 * H), F32),
            jax.ShapeDtypeStruct((SSM_COLS, 2 * H, LANES), F32),
            jax.ShapeDtypeStruct((SSM_COLS, 1, H), F32),
            jax.ShapeDtypeStruct((SSM_COLS, 1, H), F32),
            jax.ShapeDtypeStruct((1, SSM_WIDTH), F32),
        ],
        scratch_shapes=[pltpu.VMEM((R, 2 * H), F32), pltpu.VMEM((R, 2 * H), F32),
                        pltpu.VMEM((N_CHUNK, 2 * H), F32), pltpu.VMEM((N_CHUNK, 2 * H), F32)],
        compiler_params=_params(("arbitrary",) * 4),
    )(dy_p, u_p, x, x4, xin, w_bt, lam_r, lam_i, c_mt, d_skip)


def _to_scan_layout(t, B, L):
    C = t.shape[-1]
    return t.reshape(B, N_CHUNK, L // N_CHUNK, C).transpose(0, 2, 1, 3).reshape(B, L, C)


def _from_scan_layout(t, B, L):
    C = t.shape[-1]
    return t.reshape(B, L // N_CHUNK, N_CHUNK, C).transpose(0, 2, 1, 3).reshape(B * L, C)


def _local_step(x, target, p, w_in, w_glu, w_out, w_mlp_in, w_mlp_out, *, ssm_tile=32):
    B, L, D = x.shape
    T = B * L
    x2 = x.reshape(T, D)
    row = lambda v: v.reshape(1, -1)
    g1, g2, ga, gs, b_glu = row(p["norm1_g"]), row(p["norm2_g"]), row(p["attn_out_g"]), row(p["ssm_out_g"]), row(p["b_glu"])
    gq2 = jnp.tile(row(p["q_norm_g"]), (1, 2))
    gk2 = jnp.tile(row(p["k_norm_g"]), (1, 2))

    G, P, Hh = SSM_GROUPS, SSM_STATE, SSM_GROUP
    lam_re3, lam_im3 = p["ssm_lambda_re"].reshape(G, 1, P), p["ssm_lambda_im"].reshape(G, 1, P)
    log_dt3 = p["ssm_log_dt"].reshape(G, 1, 1)
    b_re_t, b_im_t = p["ssm_b_re"].transpose(0, 2, 1), p["ssm_b_im"].transpose(0, 2, 1)
    lbr, lbi, bbr, bbi = _ssm_prep(lam_re3, lam_im3, log_dt3, b_re_t, b_im_t)
    w_b = jnp.concatenate([_block_diag(bbr), _block_diag(bbi)], axis=2)
    c_mt = jnp.concatenate([_block_diag(p["ssm_c_re"]), -_block_diag(p["ssm_c_im"])], axis=2)
    w_bt, c_m = w_b.transpose(0, 2, 1), c_mt.transpose(0, 2, 1)
    lam_r = jnp.broadcast_to(lbr.reshape(SSM_COLS, 1, 512), (SSM_COLS, N_CHUNK, 512))
    lam_i = jnp.broadcast_to(lbi.reshape(SSM_COLS, 1, 512), (SSM_COLS, N_CHUNK, 512))
    d_skip = p["ssm_d"].reshape(1, SSM_WIDTH)

    xn = _rowwise("norm1", _rms, [x2], [g1], [(D, BF16)])
    proj = _matmul("proj", xn, w_in)
    sb, rtot = _attn_fwd(proj, gq2, gk2, B, L)
    u_p = _to_scan_layout(proj[:, 3 * SB_WIDTH:], B, L)
    y_p, xs, xin = _ssm_fwd(u_p, w_b, lam_r, lam_i, c_m, d_skip, B, L, ssm_tile)
    y2 = y_p.reshape(T, SSM_WIDTH)
    gel = _rowwise("gelu", jax.nn.gelu, [y2], [], [(SSM_WIDTH, BF16)])
    pre = _matmul("glu_gate", gel, w_glu)
    ssm_n = _rowwise("glu_out", _glu_branch, [y2, pre], [b_glu, gs], [(SSM_WIDTH, BF16)])
    sb_n = _rowwise("attn_out_norm", _rms, [sb], [ga], [(SB_WIDTH, BF16)])
    mixed = jnp.concatenate([sb_n, _from_scan_layout(ssm_n, B, L)], axis=1)
    h1 = _matmul("out_proj", mixed, w_out, extras=[x2], epilogue=lambda acc, r: (acc + r,))
    hn = _rowwise("norm2", _rms, [h1], [g2], [(D, BF16)])
    act, a_pre = _matmul("mlp_in", hn, w_mlp_in, out_dtypes=(BF16, BF16),
                         epilogue=lambda acc: (jnp.square(jnp.maximum(acc, 0.0)), acc))
    out = _matmul("mlp_out", act, w_mlp_out, extras=[h1], epilogue=lambda acc, r: (acc + r,))

    def loss_fn(o, t):
        diff = o - t
        part = jnp.sum(jnp.sum(diff * diff, axis=0, keepdims=True), axis=1, keepdims=True)
        d = diff * (1.0 / D)
        return d, d, part * (0.5 / D)

    d_out, d_out_b, loss = _rowwise("loss", loss_fn, [out, target.reshape(T, D)], [], [(D, F32), (D, BF16)],
                                    sums=[(1, 1)])

    d_apre = _matmul("mlp_out_dx", d_out_b, w_mlp_out, tb=True, extras=[a_pre], out_dtypes=(BF16,),
                     epilogue=lambda acc, ap: (acc * (2.0 * jnp.maximum(ap.astype(F32), 0.0)),))
    g_w_mlp_out = _matmul("mlp_out_dw", act, d_out_b, ta=True)
    d_hn = _matmul("mlp_in_dx", d_apre, w_mlp_in, tb=True)
    g_w_mlp_in = _matmul("mlp_in_dw", hn, d_apre, ta=True, col_blocked=True, tn=w_mlp_in.shape[1] // N_DEV)

    def norm_bwd_res(h, dy, res, g):
        _, vjp = jax.vjp(_rms, h, g)
        dh, dg = vjp(dy)
        return res + dh, dg

    def norm_bwd_res2(h, dy, res, g):
        d, dg = norm_bwd_res(h, dy, res, g)
        return d, d, dg

    d_h1, d_h1_b, g_norm2 = _rowwise("norm2_bwd", norm_bwd_res2, [h1, d_hn, d_out], [g2], [(D, F32), (D, BF16)],
                                     sums=[(1, D)])

    d_mixed = _matmul("out_proj_dx", d_h1_b, w_out, tb=True)
    g_w_out = _matmul("out_proj_dw", mixed, d_h1_b, ta=True)

    def norm_bwd(h, dy, g):
        _, vjp = jax.vjp(_rms, h, g)
        return vjp(dy)

    d_sb, g_attn_out = _rowwise("attn_out_norm_bwd", norm_bwd, [sb, (d_mixed, 0, SB_WIDTH)], [ga],
                                [(SB_WIDTH, F32)], sums=[(1, SB_WIDTH)])
    d_ssm_n = _to_scan_layout(d_mixed[:, SB_WIDTH:], B, L).reshape(T, SSM_WIDTH)

    def glu_bwd(y, pre_, dy, bg, g):
        _, vjp = jax.vjp(_glu_branch, y, pre_, bg, g)
        d_y, d_pre, d_bg, d_g = vjp(dy)
        return d_y, d_pre, d_bg, d_g

    d_y_direct, d_pre, g_b_glu, g_ssm_out = _rowwise(
        "glu_out_bwd", glu_bwd, [y2, pre, d_ssm_n], [b_glu, gs], [(SSM_WIDTH, F32), (SSM_WIDTH, BF16)],
        sums=[(1, SSM_WIDTH), (1, SSM_WIDTH)])
    d_gel = _matmul("glu_gate_dx", d_pre, w_glu, tb=True)
    g_w_glu = _matmul("glu_gate_dw", gel, d_pre, ta=True)

    def gelu_bwd(y, dg, dy0):
        _, vjp = jax.vjp(jax.nn.gelu, y)
        return dy0 + vjp(dg)[0]

    d_y = _rowwise("gelu_bwd", gelu_bwd, [y2, d_gel, d_y_direct], [], [(SSM_WIDTH, F32)])

    du_p, d_wb, d_cm, d_lr, d_li, g_d = _ssm_bwd(
        d_y.reshape(B, L, SSM_WIDTH), u_p, xs, xin, w_bt, lam_r, lam_i, c_mt, d_skip, B, L, ssm_tile)
    d_bb = _block_diag_take(d_wb.reshape(SSM_COLS, LANES, 2, 512))
    d_c = _block_diag_take(d_cm.transpose(0, 2, 1).reshape(SSM_COLS, LANES, 2, 512))
    g_lam_re, g_lam_im, g_log_dt, g_b_re_t, g_b_im_t = _ssm_prep_bwd(
        lam_re3, lam_im3, log_dt3, b_re_t, b_im_t,
        d_lr.reshape(G, 1, P), d_li.reshape(G, 1, P), d_bb[0], d_bb[1])
    d_q, d_k, d_v, g_q, g_k = _attn_bwd(proj, gq2, gk2, rtot, d_sb, B, L)

    d_proj = jnp.concatenate([d_q, d_k, d_v, _from_scan_layout(du_p, B, L)], axis=1)
    g_w_in = _matmul("proj_dw", xn, d_proj, ta=True, col_blocked=True, tn=w_in.shape[1] // N_DEV)
    d_xn = _matmul("proj_dx", d_proj, w_in, tb=True)
    grad_x, g_norm1 = _rowwise("norm1_bwd", norm_bwd_res, [x2, d_xn, d_h1], [g1], [(D, F32)], sums=[(1, D)])

    small = {
        "norm1_g": g_norm1.reshape(-1),
        "q_norm_g": g_q[0, :HEAD_DIM],
        "k_norm_g": g_k[0, :HEAD_DIM],
        "ssm_lambda_re": g_lam_re.reshape(G, P),
        "ssm_lambda_im": g_lam_im.reshape(G, P),
        "ssm_log_dt": g_log_dt.reshape(G),
        "ssm_b_re": g_b_re_t.transpose(0, 2, 1),
        "ssm_b_im": g_b_im_t.transpose(0, 2, 1),
        "ssm_c_re": d_c[0],
        "ssm_c_im": -d_c[1],
        "ssm_d": g_d.reshape(G, Hh),
        "b_glu": g_b_glu.reshape(-1),
        "attn_out_g": g_attn_out.reshape(-1),
        "ssm_out_g": g_ssm_out.reshape(-1),
        "norm2_g": g_norm2.reshape(-1),
    }
    big = {"w_in": g_w_in, "w_glu": g_w_glu, "w_out": g_w_out, "w_mlp_in": g_w_mlp_in, "w_mlp_out": g_w_mlp_out}
    return loss[0, 0], grad_x.reshape(B, L, D), small, big


_ANY = pl.BlockSpec(memory_space=pl.ANY)
_MESH = pl.DeviceIdType.MESH


def _all_gather(name, shards):
    n = len(shards)

    def body(*refs):
        in_refs, out_refs = refs[:n], refs[n:2 * n]
        send_sems, recv_sems, local_sems = refs[2 * n:]
        x, y, c = lax.axis_index("x"), lax.axis_index("y"), lax.axis_index("c")
        me, sibling = (x, y, c), (x, y, 1 - c)
        chips = [(1 - x, y), (x, 1 - y), (1 - x, 1 - y)]

        def copy(a, k, block, to, src=None):
            px, py, pc = block
            rows = out_refs[a].at[4 * px + 2 * py + pc]
            return pltpu.make_async_remote_copy(
                src_ref=rows if src is None else src, dst_ref=rows, send_sem=send_sems.at[a, k],
                recv_sem=recv_sems.at[a, k], device_id=to, device_id_type=_MESH)

        mine = [pltpu.make_async_copy(in_refs[a], out_refs[a].at[4 * x + 2 * y + c], local_sems.at[a]) for a in range(n)]
        first, passed = [], []
        for a in range(n):
            mine[a].start()
            first.append(copy(a, 0, me, sibling, src=in_refs[a]))
            first += [copy(a, 1 + j, me, (*chip, c), src=in_refs[a]) for j, chip in enumerate(chips)]
        for cp in first:
            cp.start()
        for j, chip in enumerate(chips):
            for a in range(n):
                copy(a, 1 + j, (*chip, c), me).wait_recv()
                fwd = copy(a, 4 + j, (*chip, c), sibling)
                fwd.start()
                passed.append(fwd)
        for a in range(n):
            copy(a, 0, sibling, me).wait_recv()
            for j, chip in enumerate(chips):
                copy(a, 4 + j, (*chip, 1 - c), me).wait_recv()
        for cp in first + passed:
            cp.wait_send()
        for cp in mine:
            cp.wait()

    return pl.pallas_call(
        body, name=name,
        in_specs=[_ANY] * n, out_specs=[_ANY] * n,
        out_shape=[jax.ShapeDtypeStruct((N_DEV, *s.shape), s.dtype) for s in shards],
        scratch_shapes=[pltpu.SemaphoreType.DMA((n, 7)), pltpu.SemaphoreType.DMA((n, 7)), pltpu.SemaphoreType.DMA((n,))],
    )(*shards)


def _sibling_exchange(name, grads):
    n = len(grads)

    def body(*refs):
        g_refs, r_refs = refs[:n], refs[n:2 * n]
        send_sems, recv_sems = refs[2 * n:]
        x, y, c = lax.axis_index("x"), lax.axis_index("y"), lax.axis_index("c")
        copies = [
            pltpu.make_async_remote_copy(
                src_ref=g_refs[a].at[k, 1 - c], dst_ref=r_refs[a].at[k], send_sem=send_sems.at[a, k],
                recv_sem=recv_sems.at[a, k], device_id=(x, y, 1 - c), device_id_type=_MESH)
            for a in range(n) for k in range(4)]
        for cp in copies:
            cp.start()
        for cp in copies:
            cp.wait()

    return pl.pallas_call(
        body, name=name, in_specs=[_ANY] * n, out_specs=[_ANY] * n,
        out_shape=[jax.ShapeDtypeStruct((4, *g.shape[2:]), g.dtype) for g in grads],
        scratch_shapes=[pltpu.SemaphoreType.DMA((n, 4)), pltpu.SemaphoreType.DMA((n, 4))],
    )(*grads)


def _chip_exchange(name, parts):
    n = len(parts)

    def body(*refs):
        p_refs, r_refs = refs[:n], refs[n:2 * n]
        send_sems, recv_sems = refs[2 * n:]
        x, y, c = lax.axis_index("x"), lax.axis_index("y"), lax.axis_index("c")
        chips = [(1 - x, y), (x, 1 - y), (1 - x, 1 - y)]
        copies = [
            pltpu.make_async_remote_copy(
                src_ref=p_refs[a].at[2 * px + py], dst_ref=r_refs[a].at[j], send_sem=send_sems.at[a, j],
                recv_sem=recv_sems.at[a, j], device_id=(px, py, c), device_id_type=_MESH)
            for a in range(n) for j, (px, py) in enumerate(chips)]
        for cp in copies:
            cp.start()
        for cp in copies:
            cp.wait()

    return pl.pallas_call(
        body, name=name, in_specs=[_ANY] * n, out_specs=[_ANY] * n,
        out_shape=[jax.ShapeDtypeStruct((3, *p.shape[1:]), p.dtype) for p in parts],
        scratch_shapes=[pltpu.SemaphoreType.DMA((n, 3)), pltpu.SemaphoreType.DMA((n, 3))],
    )(*parts)


def _chip_partial(name, g4, r1, core):
    _, _, r, c = g4.shape
    tr = min(r, 256)

    def body(core_ref, g_ref, r_ref, o32_ref, o16_ref):
        s = g_ref[...] + r_ref[...]
        o32_ref[...] = s
        o16_ref[...] = s.astype(o16_ref.dtype)

    spec = pl.BlockSpec((None, tr, c), lambda k, i, core_ref: (k, i, 0))
    return pl.pallas_call(
        body, name=name,
        grid_spec=pltpu.PrefetchScalarGridSpec(
            num_scalar_prefetch=1, grid=(4, r // tr),
            in_specs=[pl.BlockSpec((None, None, tr, c), lambda k, i, core_ref: (k, core_ref[0], i, 0)), spec],
            out_specs=[spec, spec]),
        out_shape=[jax.ShapeDtypeStruct((4, r, c), F32), jax.ShapeDtypeStruct((4, r, c), BF16)],
        compiler_params=_params(("parallel", "parallel")),
    )(core, g4, r1)


def _adamw(w, g, m, v):
    m = ADAM_B1 * m + (1.0 - ADAM_B1) * g
    v = ADAM_B2 * v + (1.0 - ADAM_B2) * jnp.square(g)
    m_hat = m / (1.0 - ADAM_B1 ** ADAM_STEP)
    v_hat = v / (1.0 - ADAM_B2 ** ADAM_STEP)
    delta = -ADAM_LR * (m_hat / (jnp.sqrt(v_hat) + ADAM_EPS) + ADAM_WD * w)
    return delta, m, v


def _adamw_shard(name, p32, r2, chip, w, m, v):
    r, c = w.shape
    tr = min(r, 256)

    def body(chip_ref, p_ref, r_ref, w_ref, m_ref, v_ref, g_out, d_out, m_out, v_out):
        g = p_ref[...]
        for j in range(3):
            g = g + r_ref[j].astype(F32)
        delta, m_new, v_new = _adamw(w_ref[...], g, m_ref[...], v_ref[...])
        g_out[...] = g
        d_out[...] = delta
        m_out[...] = m_new
        v_out[...] = v_new

    spec = pl.BlockSpec((tr, c), lambda i, chip_ref: (i, 0))
    return pl.pallas_call(
        body, name=name,
        grid_spec=pltpu.PrefetchScalarGridSpec(
            num_scalar_prefetch=1, grid=(r // tr,),
            in_specs=[pl.BlockSpec((None, tr, c), lambda i, chip_ref: (chip_ref[0], i, 0)),
                      pl.BlockSpec((3, tr, c), lambda i, chip_ref: (0, i, 0)), spec, spec, spec],
            out_specs=[spec] * 4),
        out_shape=[jax.ShapeDtypeStruct((r, c), F32)] * 4,
        compiler_params=_params(("parallel",)),
    )(chip, p32, r2, w, m, v)


def _adamw_small(name, parts, w, m, v):
    _, r, c = parts.shape
    tr = 8

    def body(p_ref, w_ref, m_ref, v_ref, g_out, d_out, m_out, v_out):
        g = p_ref[0]
        for j in range(1, N_DEV):
            g = g + p_ref[j]
        delta, m_new, v_new = _adamw(w_ref[...], g, m_ref[...], v_ref[...])
        g_out[...] = g
        d_out[...] = delta
        m_out[...] = m_new
        v_out[...] = v_new

    spec = pl.BlockSpec((tr, c), lambda i: (i, 0))
    return pl.pallas_call(
        body, name=name, grid=(r // tr,),
        in_specs=[pl.BlockSpec((N_DEV, tr, c), lambda i: (0, i, 0)), spec, spec, spec],
        out_specs=[spec] * 4, out_shape=[jax.ShapeDtypeStruct((r, c), F32)] * 4,
        compiler_params=_params(("parallel",)),
    )(parts, w, m, v)


_WEIGHTS = ["norm1_g", "w_in", "q_norm_g", "k_norm_g", "ssm_lambda_re", "ssm_lambda_im", "ssm_log_dt", "ssm_b_re",
            "ssm_b_im", "ssm_c_re", "ssm_c_im", "ssm_d", "w_glu", "b_glu", "attn_out_g", "ssm_out_g", "w_out",
            "norm2_g", "w_mlp_in", "w_mlp_out"]
_BIG = ["w_in", "w_glu", "w_out", "w_mlp_in", "w_mlp_out"]
_SMALL = [n for n in _WEIGHTS if n not in _BIG]
_PACK_COLS = 1024


def _pack(tree, last=None):
    flat = [tree[n].reshape(-1).astype(F32) for n in _SMALL]
    size = sum(f.shape[0] for f in flat)
    rows = -(-(size + 1) // (_PACK_COLS * 8)) * 8
    pad = jnp.zeros((rows * _PACK_COLS - size - 1,), F32)
    tail = jnp.zeros((1,), F32) if last is None else last.reshape(1).astype(F32)
    return jnp.concatenate(flat + [pad, tail]).reshape(rows, _PACK_COLS)


def _unpack(buf, like):
    flat, out, off = buf.reshape(-1), {}, 0
    for n in _SMALL:
        size = like[n].size
        out[n] = flat[off:off + size].reshape(like[n].shape)
        off += size
    return out


def kernel(x, norm1_g, w_in, q_norm_g, k_norm_g, ssm_lambda_re, ssm_lambda_im, ssm_log_dt, ssm_b_re, ssm_b_im, ssm_c_re, ssm_c_im, ssm_d, w_glu, b_glu, attn_out_g, ssm_out_g, w_out, norm2_g, w_mlp_in, w_mlp_out, loss_target, m_norm1_g, m_w_in, m_q_norm_g, m_k_norm_g, m_ssm_lambda_re, m_ssm_lambda_im, m_ssm_log_dt, m_ssm_b_re, m_ssm_b_im, m_ssm_c_re, m_ssm_c_im, m_ssm_d, m_w_glu, m_b_glu, m_attn_out_g, m_ssm_out_g, m_w_out, m_norm2_g, m_w_mlp_in, m_w_mlp_out, v_norm1_g, v_w_in, v_q_norm_g, v_k_norm_g, v_ssm_lambda_re, v_ssm_lambda_im, v_ssm_log_dt, v_ssm_b_re, v_ssm_b_im, v_ssm_c_re, v_ssm_c_im, v_ssm_d, v_w_glu, v_b_glu, v_attn_out_g, v_ssm_out_g, v_w_out, v_norm2_g, v_w_mlp_in, v_w_mlp_out):
    w = dict(zip(_WEIGHTS, (norm1_g, w_in, q_norm_g, k_norm_g, ssm_lambda_re, ssm_lambda_im, ssm_log_dt, ssm_b_re, ssm_b_im, ssm_c_re, ssm_c_im, ssm_d, w_glu, b_glu, attn_out_g, ssm_out_g, w_out, norm2_g, w_mlp_in, w_mlp_out)))
    m = dict(zip(_WEIGHTS, (m_norm1_g, m_w_in, m_q_norm_g, m_k_norm_g, m_ssm_lambda_re, m_ssm_lambda_im, m_ssm_log_dt, m_ssm_b_re, m_ssm_b_im, m_ssm_c_re, m_ssm_c_im, m_ssm_d, m_w_glu, m_b_glu, m_attn_out_g, m_ssm_out_g, m_w_out, m_norm2_g, m_w_mlp_in, m_w_mlp_out)))
    v = dict(zip(_WEIGHTS, (v_norm1_g, v_w_in, v_q_norm_g, v_k_norm_g, v_ssm_lambda_re, v_ssm_lambda_im, v_ssm_log_dt, v_ssm_b_re, v_ssm_b_im, v_ssm_c_re, v_ssm_c_im, v_ssm_d, v_w_glu, v_b_glu, v_attn_out_g, v_ssm_out_g, v_w_out, v_norm2_g, v_w_mlp_in, v_w_mlp_out)))
    core = lax.axis_index("c").astype(jnp.int32).reshape(1)
    chip = (2 * lax.axis_index("x") + lax.axis_index("y")).astype(jnp.int32).reshape(1)

    gathered = dict(zip(_BIG, _all_gather("weights_all_gather", [w[n].astype(BF16) for n in _BIG])))
    full = {
        "w_in": gathered["w_in"].transpose(1, 0, 2).reshape(w_in.shape[0], -1),
        "w_glu": gathered["w_glu"].reshape(-1, w_glu.shape[1]),
        "w_out": gathered["w_out"].reshape(-1, w_out.shape[1]),
        "w_mlp_in": gathered["w_mlp_in"].transpose(1, 0, 2).reshape(w_mlp_in.shape[0], -1),
        "w_mlp_out": gathered["w_mlp_out"].reshape(-1, w_mlp_out.shape[1]),
    }

    loss_local, grad_x, g_small, g_big = _local_step(
        x, loss_target, {n: w[n] for n in _SMALL}, full["w_in"], full["w_glu"], full["w_out"], full["w_mlp_in"],
        full["w_mlp_out"])

    g4 = [g_big[n].reshape(4, 2, *w[n].shape) for n in _BIG]
    r1 = _sibling_exchange("grads_sibling_exchange", g4)
    parts = [_chip_partial("chip_partial_" + n, a, b, core) for n, a, b in zip(_BIG, g4, r1)]
    r2 = _chip_exchange("grads_chip_exchange", [p16 for _, p16 in parts])
    grads, delta, new_m, new_v = {}, {}, {}, {}
    for n, (p32, _), r in zip(_BIG, parts, r2):
        grads[n], delta[n], new_m[n], new_v[n] = _adamw_shard("adamw_" + n, p32, r, chip, w[n], m[n], v[n])

    (small_parts,) = _all_gather("small_grads_all_gather", [_pack(g_small, last=loss_local)])
    packed = _adamw_small("adamw_small", small_parts, _pack(w), _pack(m), _pack(v))
    for tree, buf in zip((grads, delta, new_m, new_v), packed):
        tree.update(_unpack(buf, w))
    loss = packed[0][-1, -1]

    return (loss, grad_x, *[grads[n] for n in _WEIGHTS], *[delta[n] for n in _WEIGHTS],
            *[new_m[n] for n in _WEIGHTS], *[new_v[n] for n in _WEIGHTS])
```

```python
import functools
import math

import jax
import jax.numpy as jnp
from jax import lax
from jax.experimental import pallas as pl
from jax.experimental.pallas import tpu as pltpu

F32 = jnp.float32
BF16 = jnp.bfloat16

EPS = 1e-6
HEAD_DIM = 64
N_HEADS = 8
SB_WIDTH = 512
SSM_WIDTH = 512
SSM_GROUP = 16
SSM_GROUPS = 32
SSM_STATE = 64
QBLOCK = 128
KBLOCK = 256
N_CHUNK = 8
SSM_COLS = 4
LANES = 128
N_DEV = 8

ADAM_LR = 0.001
ADAM_B1 = 0.9
ADAM_B2 = 0.999
ADAM_EPS = 1e-08
ADAM_WD = 0.01
ADAM_STEP = 10

VMEM_LIMIT = 56 * 1024 * 1024

_NT = (((1,), (1,)), ((), ()))
_NN = (((1,), (0,)), ((), ()))
_TN = (((0,), (0,)), ((), ()))


def _dot(a, b, dims=_NN):
    return lax.dot_general(a, b, dims, preferred_element_type=F32)


def _params(sem):
    return pltpu.CompilerParams(dimension_semantics=sem, vmem_limit_bytes=VMEM_LIMIT)


def _matmul(name, a, b, *, ta=False, tb=False, extras=(), epilogue=None, out_dtypes=(F32,),
            col_blocked=False, tm=1024, tn=512, tk=1024):
    M, K = (a.shape[1], a.shape[0]) if ta else a.shape
    N = b.shape[0] if tb else b.shape[1]
    tm, tn, tk = min(tm, M), min(tn, N), min(tk, K)
    assert M % tm == 0 and N % tn == 0 and K % tk == 0, (name, M, N, K)
    nk = K // tk
    n_ex, n_out = len(extras), len(out_dtypes)
    dims = (((0 if ta else 1,), (1 if tb else 0,)), ((), ()))

    def body(*refs):
        a_ref, b_ref = refs[0], refs[1]
        ex_refs = refs[2:2 + n_ex]
        o_refs = refs[2 + n_ex:2 + n_ex + n_out]
        k = pl.program_id(2)
        part = _dot(a_ref[...].astype(BF16), b_ref[...].astype(BF16), dims)

        def finish(acc):
            outs = (acc,) if epilogue is None else epilogue(acc, *[e[...] for e in ex_refs])
            for o_ref, o in zip(o_refs, outs):
                o_ref[...] = o.astype(o_ref.dtype)

        if nk == 1:
            finish(part)
        else:
            acc_ref = refs[-1]

            @pl.when(k == 0)
            def _():
                acc_ref[...] = part

            @pl.when(jnp.logical_and(k > 0, k < nk - 1))
            def _():
                acc_ref[...] += part

            @pl.when(k == nk - 1)
            def _():
                finish(acc_ref[...] + part)

    a_spec = pl.BlockSpec((tk, tm), lambda i, j, k: (k, i)) if ta else pl.BlockSpec((tm, tk), lambda i, j, k: (i, k))
    b_spec = pl.BlockSpec((tn, tk), lambda i, j, k: (j, k)) if tb else pl.BlockSpec((tk, tn), lambda i, j, k: (k, j))
    ex_specs = [pl.BlockSpec((tm, tn), lambda i, j, k: (i, j)) for _ in extras]
    if col_blocked:
        out_specs = [pl.BlockSpec((None, tm, tn), lambda i, j, k: (j, i, 0)) for _ in out_dtypes]
        out_shape = [jax.ShapeDtypeStruct((N // tn, M, tn), dt) for dt in out_dtypes]
    else:
        out_specs = [pl.BlockSpec((tm, tn), lambda i, j, k: (i, j)) for _ in out_dtypes]
        out_shape = [jax.ShapeDtypeStruct((M, N), dt) for dt in out_dtypes]
    outs = pl.pallas_call(
        body, name=name, grid=(M // tm, N // tn, nk),
        in_specs=[a_spec, b_spec, *ex_specs], out_specs=out_specs, out_shape=out_shape,
        scratch_shapes=[pltpu.VMEM((tm, tn), F32)] if nk > 1 else [],
        compiler_params=_params(("parallel", "parallel", "arbitrary")),
    )(a, b, *extras)
    return outs[0] if n_out == 1 else outs


def _rowwise(name, fn, rows, small, outs, sums=(), tile=256):
    specs, args = [], []
    T = None
    for r in rows:
        arr, cb, w = r if isinstance(r, tuple) else (r, 0, r.shape[1])
        T = arr.shape[0]
        specs.append((w, cb))
        args.append(arr)
    tile = min(tile, T)
    assert T % tile == 0
    n_r, n_s, n_o, n_a = len(rows), len(small), len(outs), len(sums)

    def body(*refs):
        r_refs = refs[:n_r]
        s_refs = refs[n_r:n_r + n_s]
        o_refs = refs[n_r + n_s:n_r + n_s + n_o]
        a_refs = refs[n_r + n_s + n_o:]
        res = fn(*[r[...] for r in r_refs], *[s[...] for s in s_refs])
        res = res if isinstance(res, (tuple, list)) else (res,)
        for o_ref, o in zip(o_refs, res[:n_o]):
            o_ref[...] = o.astype(o_ref.dtype)

        @pl.when(pl.program_id(0) == 0)
        def _():
            for a_ref in a_refs:
                a_ref[...] = jnp.zeros_like(a_ref)

        for a_ref, v in zip(a_refs, res[n_o:]):
            a_ref[...] += v.astype(F32)

    in_specs = [pl.BlockSpec((tile, w), functools.partial(lambda i, cb: (i, cb), cb=cb)) for w, cb in specs]
    in_specs += [pl.BlockSpec(s.shape, functools.partial(lambda i, nd: (0,) * nd, nd=s.ndim)) for s in small]
    out_specs = [pl.BlockSpec((tile, w), lambda i: (i, 0)) for w, _ in outs]
    out_specs += [pl.BlockSpec(s, functools.partial(lambda i, nd: (0,) * nd, nd=len(s))) for s in sums]
    out_shape = [jax.ShapeDtypeStruct((T, w), dt) for w, dt in outs]
    out_shape += [jax.ShapeDtypeStruct(s, F32) for s in sums]
    res = pl.pallas_call(
        body, name=name, grid=(T // tile,), in_specs=in_specs, out_specs=out_specs, out_shape=out_shape,
        compiler_params=_params(("arbitrary",)),
    )(*args, *small)
    return res[0] if len(res) == 1 else res


def _rms(x, g):
    return x * lax.rsqrt(jnp.mean(x * x, axis=-1, keepdims=True) + EPS) * g


def _glu_branch(y, pre, b_glu, g_out):
    g = jax.nn.gelu(y)
    return _rms(g * jax.nn.sigmoid(pre + b_glu), g_out)


def _split_dot(x, tri_bf):
    hi = x.astype(BF16)
    lo = (x - hi.astype(F32)).astype(BF16)
    return _dot(hi, tri_bf) + _dot(lo, tri_bf)


def _softplus(z):
    return jnp.maximum(z, 0.0) + jnp.log(1.0 + jnp.exp(-jnp.abs(z)))


def _head(h):
    return slice(h * HEAD_DIM, (h + 1) * HEAD_DIM)


def _head_mean(x, seg):
    return _split_dot(x, seg) * (1.0 / HEAD_DIM)


def _qk_norm(proj, gq, gk):
    T = proj.shape[0]
    scale = 1.0 / math.sqrt(HEAD_DIM)
    idx = jnp.arange(SB_WIDTH) // HEAD_DIM
    seg = (idx[:, None] == idx[None, :]).astype(BF16)

    def fn(q, k, v, gq_, gk_, seg_):
        qn = q * lax.rsqrt(_head_mean(q * q, seg_) + EPS) * (gq_ * scale)
        kn = k * lax.rsqrt(_head_mean(k * k, seg_) + EPS) * gk_
        return qn, kn, v

    return _rowwise("qk_norm", fn, [(proj, 0, SB_WIDTH), (proj, 1, SB_WIDTH), (proj, 2, SB_WIDTH)], [gq, gk, seg],
                    [(SB_WIDTH, BF16)] * 3)


def _qk_norm_bwd(proj, gq, gk, d_qn, d_kn):
    scale = 1.0 / math.sqrt(HEAD_DIM)
    idx = jnp.arange(SB_WIDTH) // HEAD_DIM
    seg = (idx[:, None] == idx[None, :]).astype(BF16)

    def one(x, g, dy, seg_):
        r = lax.rsqrt(_head_mean(x * x, seg_) + EPS)
        gdy = g * dy
        dx = r * gdy - x * (r * r * r) * _head_mean(gdy * x, seg_)
        dg = jnp.sum(dy * x * r, axis=0, keepdims=True)
        return dx, sum(dg[:, _head(h)] for h in range(N_HEADS))

    def fn(q, k, dqn, dkn, gq_, gk_, seg_):
        dq, dgq = one(q, gq_, dqn * scale, seg_)
        dk, dgk = one(k, gk_, dkn, seg_)
        return dq, dk, dgq, dgk

    return _rowwise("qk_norm_bwd", fn, [(proj, 0, SB_WIDTH), (proj, 1, SB_WIDTH), d_qn, d_kn], [gq, gk, seg],
                    [(SB_WIDTH, BF16)] * 2, sums=[(1, HEAD_DIM)] * 2)


def _split_heads(refs, scratch, L):
    def chunk(i, _):
        r = pl.ds(pl.multiple_of(i * QBLOCK, QBLOCK), QBLOCK)
        for ref, s in zip(refs, scratch):
            for h in range(2):
                s[h, r, :] = ref[r, _head(h)]
        return 0

    lax.fori_loop(0, L // QBLOCK, chunk, 0)


Q_HALVES = KBLOCK // QBLOCK
_CHAINS = [(h, r) for h in range(2) for r in range(Q_HALVES)]


def _valid(i, kb):
    row = lax.broadcasted_iota(jnp.int32, (QBLOCK, KBLOCK), 0)
    col = lax.broadcasted_iota(jnp.int32, (QBLOCK, KBLOCK), 1)
    return col + (kb * KBLOCK - i * QBLOCK) < row


def _attn_fwd(qn, kn, vb, B, L):
    n_pairs = L // KBLOCK
    n_hp = N_HEADS // 2

    def body(q_ref, k_ref, v_ref, o_ref, rt_ref, q_s, k_s, v_s, after_s):
        _split_heads((q_ref, k_ref, v_ref), (q_s, k_s, v_s), L)
        r2 = lax.broadcasted_iota(jnp.int32, (KBLOCK, KBLOCK), 0)
        c2 = lax.broadcasted_iota(jnp.int32, (KBLOCK, KBLOCK), 1)
        after_s[...] = (r2 > c2).astype(after_s.dtype)

        def q_pair(p, _):
            rows = [pl.ds(pl.multiple_of((p * Q_HALVES + r) * QBLOCK, QBLOCK), QBLOCK) for r in range(Q_HALVES)]
            q_c = [q_s[h, rows[r], :] for h, r in _CHAINS]
            cs = range(len(_CHAINS))

            def k_block(n, carry):
                kb = p - n
                rk = pl.ds(pl.multiple_of(kb * KBLOCK, KBLOCK), KBLOCK)
                valid = [_valid(p * Q_HALVES + r, kb) for r in range(Q_HALVES)]
                z = [_dot(q_c[c], k_s[_CHAINS[c][0], rk, :], _NT) for c in cs]
                sp = [_softplus(z[c]) for c in cs]
                lom = [jnp.where(valid[_CHAINS[c][1]], -sp[c], 0.0) for c in cs]
                tail = [_split_dot(lom[c], after_s[...]) + carry[c][0] for c in cs]
                a = [jnp.where(valid[_CHAINS[c][1]], jnp.exp(z[c] - sp[c] + tail[c]), 0.0) for c in cs]
                acc = [carry[c][1] + _dot(a[c].astype(v_s.dtype), v_s[_CHAINS[c][0], rk, :]) for c in cs]
                return tuple((carry[c][0] + jnp.sum(lom[c], axis=1, keepdims=True), acc[c]) for c in cs)

            init = (jnp.zeros((QBLOCK, 1), F32), jnp.zeros((QBLOCK, HEAD_DIM), F32))
            res = lax.fori_loop(0, p + 1, k_block, (init,) * len(_CHAINS))
            for r in range(Q_HALVES):
                mine = [res[c] for c in cs if _CHAINS[c][1] == r]
                o_ref[rows[r], :] = jnp.concatenate([m[1] for m in mine], axis=1)
                rt_ref[rows[r], :] = jnp.concatenate(
                    [jnp.broadcast_to(m[0], (QBLOCK, HEAD_DIM)) for m in mine], axis=1)
            return 0

        lax.fori_loop(0, n_pairs, q_pair, 0)

    spec = pl.BlockSpec((L, LANES), lambda b, p: (b, p))
    return pl.pallas_call(
        body, name="attn_fwd", grid=(B, n_hp),
        in_specs=[spec] * 3, out_specs=[spec, spec],
        out_shape=[jax.ShapeDtypeStruct((B * L, SB_WIDTH), F32)] * 2,
        scratch_shapes=[pltpu.VMEM((2, L, HEAD_DIM), BF16)] * 3 + [pltpu.VMEM((KBLOCK, KBLOCK), BF16)],
        compiler_params=_params(("parallel", "parallel")),
    )(qn, kn, vb)


def _attn_bwd(qn, kn, vb, rtot, d_sb, B, L):
    n_pairs = L // KBLOCK
    n_hp = N_HEADS // 2

    def body(q_ref, k_ref, v_ref, rt_ref, do_ref, dq_ref, dk_ref, dv_ref,
             q_s, k_s, v_s, qt_s, dkt_s, dvt_s, after_s, before_s):
        _split_heads((q_ref, k_ref, v_ref), (q_s, k_s, v_s), L)

        def transpose_q(i, _):
            r = pl.ds(pl.multiple_of(i * QBLOCK, QBLOCK), QBLOCK)
            qt_s[:, r] = q_ref[r, :].astype(F32).T.astype(qt_s.dtype)
            return 0

        lax.fori_loop(0, L // QBLOCK, transpose_q, 0)
        dkt_s[...] = jnp.zeros_like(dkt_s)
        dvt_s[...] = jnp.zeros_like(dvt_s)
        r2 = lax.broadcasted_iota(jnp.int32, (KBLOCK, KBLOCK), 0)
        c2 = lax.broadcasted_iota(jnp.int32, (KBLOCK, KBLOCK), 1)
        after_s[...] = (r2 > c2).astype(after_s.dtype)
        before_s[...] = (r2 < c2).astype(before_s.dtype)

        def q_pair(p, _):
            rows = [pl.ds(pl.multiple_of((p * Q_HALVES + r) * QBLOCK, QBLOCK), QBLOCK) for r in range(Q_HALVES)]
            pair = pl.ds(pl.multiple_of(p * KBLOCK, KBLOCK), KBLOCK)
            do2 = do_ref[pair, :]
            do_t = do2.T.astype(v_s.dtype)
            tot2 = rt_ref[pair, :]
            cs = range(len(_CHAINS))
            hs = range(2)
            q_c = [q_s[h, rows[r], :] for h, r in _CHAINS]
            do_c = [do2[r * QBLOCK:(r + 1) * QBLOCK, _head(h)].astype(v_s.dtype) for h, r in _CHAINS]
            total = [tot2[r * QBLOCK:(r + 1) * QBLOCK, h * HEAD_DIM:h * HEAD_DIM + 1] for h, r in _CHAINS]
            qt_h = [qt_s[_head(h), pair] for h in hs]
            dot_h = [do_t[_head(h), :] for h in hs]

            def k_block(kb, carry):
                rk = pl.ds(pl.multiple_of(kb * KBLOCK, KBLOCK), KBLOCK)
                valid = [_valid(p * Q_HALVES + r, kb) for r in range(Q_HALVES)]
                ok = [valid[_CHAINS[c][1]] for c in cs]
                k_b = [k_s[h, rk, :] for h in hs]
                z = [_dot(q_c[c], k_b[_CHAINS[c][0]], _NT) for c in cs]
                da = [_dot(do_c[c], v_s[_CHAINS[c][0], rk, :], _NT) for c in cs]
                sp = [_softplus(z[c]) for c in cs]
                lom = [jnp.where(ok[c], -sp[c], 0.0) for c in cs]
                lom_sum = [jnp.sum(lom[c], axis=1, keepdims=True) for c in cs]
                tail = [_split_dot(lom[c], after_s[...]) + (total[c] - carry[c][0] - lom_sum[c]) for c in cs]
                a = [jnp.where(ok[c], jnp.exp(z[c] - sp[c] + tail[c]), 0.0) for c in cs]
                dla = [a[c] * da[c] for c in cs]
                for h in hs:
                    a_h = jnp.concatenate([a[c].astype(v_s.dtype) for c in cs if _CHAINS[c][0] == h], axis=0)
                    dvt_s[_head(h), rk] += _dot(dot_h[h], a_h)
                d_lom = [carry[c][1] + _split_dot(dla[c], before_s[...]) for c in cs]
                beta = [jnp.exp(z[c] - sp[c]) for c in cs]
                dz_b = [(dla[c] * (1.0 - beta[c]) - jnp.where(ok[c], beta[c] * d_lom[c], 0.0)).astype(v_s.dtype)
                        for c in cs]
                dq_acc = [carry[c][2] + _dot(dz_b[c], k_b[_CHAINS[c][0]]) for c in cs]
                for h in hs:
                    dz_h = jnp.concatenate([dz_b[c] for c in cs if _CHAINS[c][0] == h], axis=0)
                    dkt_s[_head(h), rk] += _dot(qt_h[h], dz_h)
                return tuple((carry[c][0] + lom_sum[c], carry[c][1] + jnp.sum(dla[c], axis=1, keepdims=True),
                              dq_acc[c]) for c in cs)

            zero = jnp.zeros((QBLOCK, 1), F32)
            init = (zero, zero, jnp.zeros((QBLOCK, HEAD_DIM), F32))
            res = lax.fori_loop(0, p + 1, k_block, (init,) * len(_CHAINS))
            for r in range(Q_HALVES):
                dq_ref[rows[r], :] = jnp.concatenate([res[c][2] for c in cs if _CHAINS[c][1] == r], axis=1)
            return 0

        lax.fori_loop(0, n_pairs, q_pair, 0)

        def transpose_out(i, _):
            r = pl.ds(pl.multiple_of(i * QBLOCK, QBLOCK), QBLOCK)
            dk_ref[r, :] = dkt_s[:, r].T
            dv_ref[r, :] = dvt_s[:, r].T.astype(dv_ref.dtype)
            return 0

        lax.fori_loop(0, L // QBLOCK, transpose_out, 0)

    spec = pl.BlockSpec((L, LANES), lambda b, p: (b, p))
    return pl.pallas_call(
        body, name="attn_bwd", grid=(B, n_hp),
        in_specs=[spec] * 5, out_specs=[spec] * 3,
        out_shape=[jax.ShapeDtypeStruct((B * L, SB_WIDTH), F32)] * 2 + [jax.ShapeDtypeStruct((B * L, SB_WIDTH), BF16)],
        scratch_shapes=[pltpu.VMEM((2, L, HEAD_DIM), BF16)] * 3 + [pltpu.VMEM((LANES, L), BF16)]
        + [pltpu.VMEM((LANES, L), F32)] * 2 + [pltpu.VMEM((KBLOCK, KBLOCK), BF16)] * 2,
        compiler_params=_params(("parallel", "parallel")),
    )(qn, kn, vb, rtot, d_sb)


def _ssm_discretise(lam_re, lam_im, log_dt, b_re, b_im):
    dt = jnp.exp(log_dt)
    mag = jnp.exp(lam_re * dt)
    lbr = mag * jnp.cos(lam_im * dt)
    lbi = mag * jnp.sin(lam_im * dt)
    den = lam_re * lam_re + lam_im * lam_im
    nr, ni = lbr - 1.0, lbi
    cr = (nr * lam_re + ni * lam_im) / den
    ci = (ni * lam_re - nr * lam_im) / den
    return lbr, lbi, cr * b_re - ci * b_im, cr * b_im + ci * b_re


def _ssm_prep(lam_re, lam_im, log_dt, b_re_t, b_im_t):
    def body(lr, li, ld, br, bi, o_lr, o_li, o_br, o_bi):
        res = _ssm_discretise(lr[...], li[...], ld[...], br[...], bi[...])
        for o, v in zip((o_lr, o_li, o_br, o_bi), res):
            o[...] = v

    return pl.pallas_call(
        body, name="ssm_prep",
        out_shape=[jax.ShapeDtypeStruct(lam_re.shape, F32)] * 2 + [jax.ShapeDtypeStruct(b_re_t.shape, F32)] * 2,
    )(lam_re, lam_im, log_dt, b_re_t, b_im_t)


def _ssm_prep_bwd(lam_re, lam_im, log_dt, b_re_t, b_im_t, d_lr, d_li, d_br, d_bi):
    def body(lr, li, ld, br, bi, g_lr, g_li, g_br, g_bi, o_lr, o_li, o_ld, o_br, o_bi):
        _, vjp = jax.vjp(_ssm_discretise, lr[...], li[...], ld[...], br[...], bi[...])
        res = vjp((g_lr[...], g_li[...], g_br[...], g_bi[...]))
        for o, v in zip((o_lr, o_li, o_ld, o_br, o_bi), res):
            o[...] = v

    return pl.pallas_call(
        body, name="ssm_prep_bwd",
        out_shape=[jax.ShapeDtypeStruct(lam_re.shape, F32)] * 2 + [jax.ShapeDtypeStruct(log_dt.shape, F32)]
        + [jax.ShapeDtypeStruct(b_re_t.shape, F32)] * 2,
    )(lam_re, lam_im, log_dt, b_re_t, b_im_t, d_lr, d_li, d_br, d_bi)


def _block_diag(m):
    m4 = m.reshape(SSM_COLS, 8, SSM_GROUP, SSM_STATE)
    return jnp.einsum("aghp,gk->aghkp", m4, jnp.eye(8, dtype=m.dtype)).reshape(SSM_COLS, LANES, 512)


def _block_diag_take(d):
    d6 = d.reshape(SSM_COLS, 8, SSM_GROUP, 2, 8, SSM_STATE)
    return jnp.einsum("aghrgp->raghp", d6).reshape(2, SSM_GROUPS, SSM_GROUP, SSM_STATE)


def _cmul(ar, ai, br, bi):
    return ar * br - ai * bi, ar * bi + ai * br


def _power(lr, li, n):
    assert n & (n - 1) == 0
    for _ in range(n.bit_length() - 1):
        lr, li = _cmul(lr, li, lr, li)
    return lr, li


def _ssm_fwd(u_p, w_b, lam_r, lam_i, c_m, d_skip, B, L, tj):
    J = L // N_CHUNK
    njt = J // tj
    R = tj * N_CHUNK
    H = 512

    def body(u_ref, wb_ref, lr_ref, li_ref, cm_ref, d_ref, y_ref, x_ref, xin_ref, bu_s, st_s, xin_s):
        ph, jt = pl.program_id(2), pl.program_id(3)
        lr, li = lr_ref[...], li_ref[...]

        @pl.when(jnp.logical_and(ph == 0, jt == 0))
        def _():
            st_s[...] = jnp.zeros_like(st_s)

        bu_s[...] = _dot(u_ref[...].astype(BF16), wb_ref[...].astype(BF16))

        def scan(store):
            def step(j, carry):
                xr, xi = carry
                r = pl.ds(pl.multiple_of(j * N_CHUNK, N_CHUNK), N_CHUNK)
                nr = lr * xr - li * xi + bu_s[r, 0:H]
                ni = lr * xi + li * xr + bu_s[r, H:2 * H]
                if store:
                    x_ref[r, 0:H] = nr
                    x_ref[r, H:2 * H] = ni
                return nr, ni

            xr, xi = lax.fori_loop(0, tj, step, (st_s[:, 0:H], st_s[:, H:2 * H]))
            st_s[:, 0:H] = xr
            st_s[:, H:2 * H] = xi

        @pl.when(ph == 0)
        def _():
            scan(False)

            @pl.when(jt == njt - 1)
            def _():
                pr, pi = _power(lr[0:1], li[0:1], J)
                xin_s[0:1, :] = jnp.zeros((1, 2 * H), F32)
                for c in range(1, N_CHUNK):
                    qr, qi = _cmul(pr, pi, xin_s[c - 1:c, 0:H], xin_s[c - 1:c, H:2 * H])
                    xin_s[c:c + 1, 0:H] = qr + st_s[c - 1:c, 0:H]
                    xin_s[c:c + 1, H:2 * H] = qi + st_s[c - 1:c, H:2 * H]
                xin_ref[...] = xin_s[...]
                st_s[...] = xin_s[...]

        @pl.when(ph == 1)
        def _():
            scan(True)
            y = _dot(x_ref[...].astype(BF16), cm_ref[...].astype(BF16))
            y_ref[...] = y + d_ref[...] * u_ref[...]

    return pl.pallas_call(
        body, name="ssm_fwd", grid=(SSM_COLS, B, 2, njt),
        in_specs=[
            pl.BlockSpec((None, R, LANES), lambda i, b, ph, jt: (b, jt, i)),
            pl.BlockSpec((None, LANES, 2 * H), lambda i, b, ph, jt: (i, 0, 0)),
            pl.BlockSpec((None, N_CHUNK, H), lambda i, b, ph, jt: (i, 0, 0)),
            pl.BlockSpec((None, N_CHUNK, H), lambda i, b, ph, jt: (i, 0, 0)),
            pl.BlockSpec((None, 2 * H, LANES), lambda i, b, ph, jt: (i, 0, 0)),
            pl.BlockSpec((1, LANES), lambda i, b, ph, jt: (0, i)),
        ],
        out_specs=[
            pl.BlockSpec((None, R, LANES), lambda i, b, ph, jt: (b, jt * ph, i)),
            pl.BlockSpec((None, R, 2 * H), lambda i, b, ph, jt: (b, jt * ph, i)),
            pl.BlockSpec((None, None, N_CHUNK, 2 * H), lambda i, b, ph, jt: (b, i, 0, 0)),
        ],
        out_shape=[
            jax.ShapeDtypeStruct((B, L, SSM_WIDTH), F32),
            jax.ShapeDtypeStruct((B, L, SSM_COLS * 2 * H), F32),
            jax.ShapeDtypeStruct((B, SSM_COLS, N_CHUNK, 2 * H), F32),
        ],
        scratch_shapes=[pltpu.VMEM((R, 2 * H), F32), pltpu.VMEM((N_CHUNK, 2 * H), F32), pltpu.VMEM((N_CHUNK, 2 * H), F32)],
        compiler_params=_params(("arbitrary",) * 4),
    )(u_p, w_b, lam_r, lam_i, c_m, d_skip)


def _ssm_bwd(dy_p, u_p, x, xin, w_bt, lam_r, lam_i, c_mt, d_skip, B, L, tj):
    J = L // N_CHUNK
    njt = J // tj
    R = tj * N_CHUNK
    H = 512
    x4 = x.reshape(B, J, N_CHUNK, SSM_COLS * 2 * H)

    def body(dy_ref, u_ref, x_ref, xp_ref, xin_ref, wbt_ref, lr_ref, li_ref, cmt_ref, d_ref,
             du_ref, dwb_ref, dcm_ref, dlr_ref, dli_ref, dd_ref, ca_s, a_s, st_s, dl_s):
        b, ph, jt = pl.program_id(1), pl.program_id(2), pl.program_id(3)
        jr = njt - 1 - jt
        lr, li = lr_ref[...], -li_ref[...]

        @pl.when(jnp.logical_and(b == 0, jnp.logical_and(ph == 0, jt == 0)))
        def _():
            dwb_ref[...] = jnp.zeros_like(dwb_ref)
            dcm_ref[...] = jnp.zeros_like(dcm_ref)
            dlr_ref[...] = jnp.zeros_like(dlr_ref)
            dli_ref[...] = jnp.zeros_like(dli_ref)
            dd_ref[...] = jnp.zeros_like(dd_ref)
            dl_s[...] = jnp.zeros_like(dl_s)

        @pl.when(jnp.logical_and(ph == 0, jt == 0))
        def _():
            st_s[...] = jnp.zeros_like(st_s)

        ca_s[...] = _dot(dy_ref[...].astype(BF16), cmt_ref[...].astype(BF16))

        def scan(store):
            def step(n, carry):
                ar, ai = carry
                r = pl.ds(pl.multiple_of((tj - 1 - n) * N_CHUNK, N_CHUNK), N_CHUNK)
                nr = lr * ar - li * ai + ca_s[r, 0:H]
                ni = lr * ai + li * ar + ca_s[r, H:2 * H]
                if store:
                    a_s[r, 0:H] = nr
                    a_s[r, H:2 * H] = ni
                return nr, ni

            ar, ai = lax.fori_loop(0, tj, step, (st_s[:, 0:H], st_s[:, H:2 * H]))
            st_s[:, 0:H] = ar
            st_s[:, H:2 * H] = ai

        @pl.when(ph == 0)
        def _():
            scan(False)

            @pl.when(jt == njt - 1)
            def _():
                pr, pi = _power(lr[0:1], li[0:1], J)
                a_s[N_CHUNK - 1:N_CHUNK, :] = jnp.zeros((1, 2 * H), F32)
                for c in range(N_CHUNK - 2, -1, -1):
                    qr, qi = _cmul(pr, pi, a_s[c + 1:c + 2, 0:H], a_s[c + 1:c + 2, H:2 * H])
                    a_s[c:c + 1, 0:H] = qr + st_s[c + 1:c + 2, 0:H]
                    a_s[c:c + 1, H:2 * H] = qi + st_s[c + 1:c + 2, H:2 * H]
                st_s[...] = a_s[0:N_CHUNK, :]

        @pl.when(ph == 1)
        def _():
            scan(True)
            dy = dy_ref[...]
            u = u_ref[...]
            a_b = a_s[...].astype(BF16)
            du_ref[...] = (_dot(a_b, wbt_ref[...].astype(BF16)) + d_ref[...] * dy).astype(du_ref.dtype)
            dwb_ref[...] += _dot(u.astype(BF16), a_b, _TN)
            dcm_ref[...] += _dot(x_ref[...].astype(BF16), dy.astype(BF16), _TN)
            dd_ref[...] += jnp.sum(dy * u, axis=0, keepdims=True)

            first = jnp.where(jr == 0, xin_ref[...], xp_ref[...])
            a0r, a0i = a_s[0:N_CHUNK, 0:H], a_s[0:N_CHUNK, H:2 * H]
            acc0 = (a0r * first[:, 0:H] + a0i * first[:, H:2 * H], a0i * first[:, 0:H] - a0r * first[:, H:2 * H])

            def step(j, carry):
                sr, si = carry
                r = pl.ds(pl.multiple_of(j * N_CHUNK, N_CHUNK), N_CHUNK)
                rp = pl.ds(pl.multiple_of((j - 1) * N_CHUNK, N_CHUNK), N_CHUNK)
                ar, ai = a_s[r, 0:H], a_s[r, H:2 * H]
                xr, xi = x_ref[rp, 0:H], x_ref[rp, H:2 * H]
                return sr + ar * xr + ai * xi, si + ai * xr - ar * xi

            sr, si = lax.fori_loop(1, tj, step, acc0)
            dl_s[:, 0:H] += sr
            dl_s[:, H:2 * H] += si

            @pl.when(jnp.logical_and(b == B - 1, jt == njt - 1))
            def _():
                dlr_ref[...] = jnp.sum(dl_s[:, 0:H], axis=0, keepdims=True)
                dli_ref[...] = jnp.sum(dl_s[:, H:2 * H], axis=0, keepdims=True)
                dl_s[...] = jnp.zeros_like(dl_s)

    rev = lambda ph, jt: (njt - 1 - jt) * ph + (njt - 1) * (1 - ph)
    return pl.pallas_call(
        body, name="ssm_bwd", grid=(SSM_COLS, B, 2, njt),
        in_specs=[
            pl.BlockSpec((None, R, LANES), lambda i, b, ph, jt: (b, njt - 1 - jt, i)),
            pl.BlockSpec((None, R, LANES), lambda i, b, ph, jt: (b, njt - 1 - jt, i)),
            pl.BlockSpec((None, R, 2 * H), lambda i, b, ph, jt: (b, rev(ph, jt), i)),
            pl.BlockSpec((None, None, N_CHUNK, 2 * H),
                         lambda i, b, ph, jt: (b, jnp.maximum((njt - 1 - jt) * tj - 1, 0), 0, i)),
            pl.BlockSpec((None, None, N_CHUNK, 2 * H), lambda i, b, ph, jt: (b, i, 0, 0)),
            pl.BlockSpec((None, 2 * H, LANES), lambda i, b, ph, jt: (i, 0, 0)),
            pl.BlockSpec((None, N_CHUNK, H), lambda i, b, ph, jt: (i, 0, 0)),
            pl.BlockSpec((None, N_CHUNK, H), lambda i, b, ph, jt: (i, 0, 0)),
            pl.BlockSpec((None, LANES, 2 * H), lambda i, b, ph, jt: (i, 0, 0)),
            pl.BlockSpec((1, LANES), lambda i, b, ph, jt: (0, i)),
        ],
        out_specs=[
            pl.BlockSpec((None, R, LANES), lambda i, b, ph, jt: (b, rev(ph, jt), i)),
            pl.BlockSpec((None, LANES, 2 * H), lambda i, b, ph, jt: (i, 0, 0)),
            pl.BlockSpec((None, 2 * H, LANES), lambda i, b, ph, jt: (i, 0, 0)),
            pl.BlockSpec((None, 1, H), lambda i, b, ph, jt: (i, 0, 0)),
            pl.BlockSpec((None, 1, H), lambda i, b, ph, jt: (i, 0, 0)),
            pl.BlockSpec((1, LANES), lambda i, b, ph, jt: (0, i)),
        ],
        out_shape=[
            jax.ShapeDtypeStruct((B, L, SSM_WIDTH), BF16),
            jax.ShapeDtypeStruct((SSM_COLS, LANES, 2 * H), F32),
            jax.ShapeDtypeStruct((SSM_COLS, 2 * H, LANES), F32),
            jax.ShapeDtypeStruct((SSM_COLS, 1, H), F32),
            jax.ShapeDtypeStruct((SSM_COLS, 1, H), F32),
            jax.ShapeDtypeStruct((1, SSM_WIDTH), F32),
        ],
        scratch_shapes=[pltpu.VMEM((R, 2 * H), F32), pltpu.VMEM((R, 2 * H), F32),
                        pltpu.VMEM((N_CHUNK, 2 * H), F32), pltpu.VMEM((N_CHUNK, 2 * H), F32)],
        compiler_params=_params(("arbitrary",) * 4),
    )(dy_p, u_p, x, x4, xin, w_bt, lam_r, lam_i, c_mt, d_skip)


def _to_scan_layout(t, B, L):
    C = t.shape[-1]
    return t.reshape(B, N_CHUNK, L // N_CHUNK, C).transpose(0, 2, 1, 3).reshape(B, L, C)


def _from_scan_layout(t, B, L):
    C = t.shape[-1]
    return t.reshape(B, L // N_CHUNK, N_CHUNK, C).transpose(0, 2, 1, 3).reshape(B * L, C)


def _local_step(x, target, p, w_in, w_glu, w_out, w_mlp_in, w_mlp_out, *, ssm_tile=32):
    B, L, D = x.shape
    T = B * L
    x2 = x.reshape(T, D)
    row = lambda v: v.reshape(1, -1)
    g1, g2, ga, gs, b_glu = row(p["norm1_g"]), row(p["norm2_g"]), row(p["attn_out_g"]), row(p["ssm_out_g"]), row(p["b_glu"])
    gq8 = jnp.tile(row(p["q_norm_g"]), (1, N_HEADS))
    gk8 = jnp.tile(row(p["k_norm_g"]), (1, N_HEADS))

    G, P, Hh = SSM_GROUPS, SSM_STATE, SSM_GROUP
    lam_re3, lam_im3 = p["ssm_lambda_re"].reshape(G, 1, P), p["ssm_lambda_im"].reshape(G, 1, P)
    log_dt3 = p["ssm_log_dt"].reshape(G, 1, 1)
    b_re_t, b_im_t = p["ssm_b_re"].transpose(0, 2, 1), p["ssm_b_im"].transpose(0, 2, 1)
    lbr, lbi, bbr, bbi = _ssm_prep(lam_re3, lam_im3, log_dt3, b_re_t, b_im_t)
    w_b = jnp.concatenate([_block_diag(bbr), _block_diag(bbi)], axis=2)
    c_mt = jnp.concatenate([_block_diag(p["ssm_c_re"]), -_block_diag(p["ssm_c_im"])], axis=2)
    w_bt, c_m = w_b.transpose(0, 2, 1), c_mt.transpose(0, 2, 1)
    lam_r = jnp.broadcast_to(lbr.reshape(SSM_COLS, 1, 512), (SSM_COLS, N_CHUNK, 512))
    lam_i = jnp.broadcast_to(lbi.reshape(SSM_COLS, 1, 512), (SSM_COLS, N_CHUNK, 512))
    d_skip = p["ssm_d"].reshape(1, SSM_WIDTH)

    xn = _rowwise("norm1", _rms, [x2], [g1], [(D, BF16)])
    proj = _matmul("proj", xn, w_in)
    qn, kn, vb = _qk_norm(proj, gq8, gk8)
    sb, rtot = _attn_fwd(qn, kn, vb, B, L)
    u_p = _to_scan_layout(proj[:, 3 * SB_WIDTH:], B, L)
    y_p, xs, xin = _ssm_fwd(u_p, w_b, lam_r, lam_i, c_m, d_skip, B, L, ssm_tile)
    y2 = y_p.reshape(T, SSM_WIDTH)
    gel = _rowwise("gelu", jax.nn.gelu, [y2], [], [(SSM_WIDTH, BF16)])
    pre = _matmul("glu_gate", gel, w_glu)
    ssm_n = _rowwise("glu_out", _glu_branch, [y2, pre], [b_glu, gs], [(SSM_WIDTH, BF16)])
    sb_n = _rowwise("attn_out_norm", _rms, [sb], [ga], [(SB_WIDTH, BF16)])
    mixed = jnp.concatenate([sb_n, _from_scan_layout(ssm_n, B, L)], axis=1)
    h1 = _matmul("out_proj", mixed, w_out, extras=[x2], epilogue=lambda acc, r: (acc + r,))
    hn = _rowwise("norm2", _rms, [h1], [g2], [(D, BF16)])
    act, a_pre = _matmul("mlp_in", hn, w_mlp_in, out_dtypes=(BF16, BF16),
                         epilogue=lambda acc: (jnp.square(jnp.maximum(acc, 0.0)), acc))
    out = _matmul("mlp_out", act, w_mlp_out, extras=[h1], epilogue=lambda acc, r: (acc + r,))

    def loss_fn(o, t):
        diff = o - t
        part = jnp.sum(jnp.sum(diff * diff, axis=0, keepdims=True), axis=1, keepdims=True)
        d = diff * (1.0 / D)
        return d, d, part * (0.5 / D)

    d_out, d_out_b, loss = _rowwise("loss", loss_fn, [out, target.reshape(T, D)], [], [(D, F32), (D, BF16)],
                                    sums=[(1, 1)])

    d_apre = _matmul("mlp_out_dx", d_out_b, w_mlp_out, tb=True, extras=[a_pre], out_dtypes=(BF16,),
                     epilogue=lambda acc, ap: (acc * (2.0 * jnp.maximum(ap.astype(F32), 0.0)),))
    g_w_mlp_out = _matmul("mlp_out_dw", act, d_out_b, ta=True)
    d_hn = _matmul("mlp_in_dx", d_apre, w_mlp_in, tb=True)
    g_w_mlp_in = _matmul("mlp_in_dw", hn, d_apre, ta=True, col_blocked=True, tn=w_mlp_in.shape[1] // N_DEV)

    def norm_bwd_res(h, dy, res, g):
        _, vjp = jax.vjp(_rms, h, g)
        dh, dg = vjp(dy)
        return res + dh, dg

    def norm_bwd_res2(h, dy, res, g):
        d, dg = norm_bwd_res(h, dy, res, g)
        return d, d, dg

    d_h1, d_h1_b, g_norm2 = _rowwise("norm2_bwd", norm_bwd_res2, [h1, d_hn, d_out], [g2], [(D, F32), (D, BF16)],
                                     sums=[(1, D)])

    d_mixed = _matmul("out_proj_dx", d_h1_b, w_out, tb=True)
    g_w_out = _matmul("out_proj_dw", mixed, d_h1_b, ta=True)

    def norm_bwd(h, dy, g):
        _, vjp = jax.vjp(_rms, h, g)
        return vjp(dy)

    d_sb, g_attn_out = _rowwise("attn_out_norm_bwd", norm_bwd, [sb, (d_mixed, 0, SB_WIDTH)], [ga],
                                [(SB_WIDTH, F32)], sums=[(1, SB_WIDTH)])
    d_ssm_n = _to_scan_layout(d_mixed[:, SB_WIDTH:], B, L).reshape(T, SSM_WIDTH)

    def glu_bwd(y, pre_, dy, bg, g):
        _, vjp = jax.vjp(_glu_branch, y, pre_, bg, g)
        d_y, d_pre, d_bg, d_g = vjp(dy)
        return d_y, d_pre, d_bg, d_g

    d_y_direct, d_pre, g_b_glu, g_ssm_out = _rowwise(
        "glu_out_bwd", glu_bwd, [y2, pre, d_ssm_n], [b_glu, gs], [(SSM_WIDTH, F32), (SSM_WIDTH, BF16)],
        sums=[(1, SSM_WIDTH), (1, SSM_WIDTH)])
    d_gel = _matmul("glu_gate_dx", d_pre, w_glu, tb=True)
    g_w_glu = _matmul("glu_gate_dw", gel, d_pre, ta=True)

    def gelu_bwd(y, dg, dy0):
        _, vjp = jax.vjp(jax.nn.gelu, y)
        return dy0 + vjp(dg)[0]

    d_y = _rowwise("gelu_bwd", gelu_bwd, [y2, d_gel, d_y_direct], [], [(SSM_WIDTH, F32)])

    du_p, d_wb, d_cm, d_lr, d_li, g_d = _ssm_bwd(
        d_y.reshape(B, L, SSM_WIDTH), u_p, xs, xin, w_bt, lam_r, lam_i, c_mt, d_skip, B, L, ssm_tile)
    d_bb = _block_diag_take(d_wb.reshape(SSM_COLS, LANES, 2, 512))
    d_c = _block_diag_take(d_cm.transpose(0, 2, 1).reshape(SSM_COLS, LANES, 2, 512))
    g_lam_re, g_lam_im, g_log_dt, g_b_re_t, g_b_im_t = _ssm_prep_bwd(
        lam_re3, lam_im3, log_dt3, b_re_t, b_im_t,
        d_lr.reshape(G, 1, P), d_li.reshape(G, 1, P), d_bb[0], d_bb[1])
    d_qn, d_kn, d_v = _attn_bwd(qn, kn, vb, rtot, d_sb, B, L)
    d_q, d_k, g_q, g_k = _qk_norm_bwd(proj, gq8, gk8, d_qn, d_kn)

    d_proj = jnp.concatenate([d_q, d_k, d_v, _from_scan_layout(du_p, B, L)], axis=1)
    g_w_in = _matmul("proj_dw", xn, d_proj, ta=True, col_blocked=True, tn=w_in.shape[1] // N_DEV)
    d_xn = _matmul("proj_dx", d_proj, w_in, tb=True)
    grad_x, g_norm1 = _rowwise("norm1_bwd", norm_bwd_res, [x2, d_xn, d_h1], [g1], [(D, F32)], sums=[(1, D)])

    small = {
        "norm1_g": g_norm1.reshape(-1),
        "q_norm_g": g_q.reshape(-1),
        "k_norm_g": g_k.reshape(-1),
        "ssm_lambda_re": g_lam_re.reshape(G, P),
        "ssm_lambda_im": g_lam_im.reshape(G, P),
        "ssm_log_dt": g_log_dt.reshape(G),
        "ssm_b_re": g_b_re_t.transpose(0, 2, 1),
        "ssm_b_im": g_b_im_t.transpose(0, 2, 1),
        "ssm_c_re": d_c[0],
        "ssm_c_im": -d_c[1],
        "ssm_d": g_d.reshape(G, Hh),
        "b_glu": g_b_glu.reshape(-1),
        "attn_out_g": g_attn_out.reshape(-1),
        "ssm_out_g": g_ssm_out.reshape(-1),
        "norm2_g": g_norm2.reshape(-1),
    }
    big = {"w_in": g_w_in, "w_glu": g_w_glu, "w_out": g_w_out, "w_mlp_in": g_w_mlp_in, "w_mlp_out": g_w_mlp_out}
    return loss[0, 0], grad_x.reshape(B, L, D), small, big


_ANY = pl.BlockSpec(memory_space=pl.ANY)
_MESH = pl.DeviceIdType.MESH


def _all_gather(name, shards):
    n = len(shards)

    def body(*refs):
        in_refs, out_refs = refs[:n], refs[n:2 * n]
        send_sems, recv_sems, local_sems = refs[2 * n:]
        x, y, c = lax.axis_index("x"), lax.axis_index("y"), lax.axis_index("c")
        me, sibling = (x, y, c), (x, y, 1 - c)
        chips = [(1 - x, y), (x, 1 - y), (1 - x, 1 - y)]

        def copy(a, k, block, to, src=None):
            px, py, pc = block
            rows = out_refs[a].at[4 * px + 2 * py + pc]
            return pltpu.make_async_remote_copy(
                src_ref=rows if src is None else src, dst_ref=rows, send_sem=send_sems.at[a, k],
                recv_sem=recv_sems.at[a, k], device_id=to, device_id_type=_MESH)

        mine = [pltpu.make_async_copy(in_refs[a], out_refs[a].at[4 * x + 2 * y + c], local_sems.at[a]) for a in range(n)]
        first, passed = [], []
        for a in range(n):
            mine[a].start()
            first.append(copy(a, 0, me, sibling, src=in_refs[a]))
            first += [copy(a, 1 + j, me, (*chip, c), src=in_refs[a]) for j, chip in enumerate(chips)]
        for cp in first:
            cp.start()
        for j, chip in enumerate(chips):
            for a in range(n):
                copy(a, 1 + j, (*chip, c), me).wait_recv()
                fwd = copy(a, 4 + j, (*chip, c), sibling)
                fwd.start()
                passed.append(fwd)
        for a in range(n):
            copy(a, 0, sibling, me).wait_recv()
            for j, chip in enumerate(chips):
                copy(a, 4 + j, (*chip, 1 - c), me).wait_recv()
        for cp in first + passed:
            cp.wait_send()
        for cp in mine:
            cp.wait()

    return pl.pallas_call(
        body, name=name,
        in_specs=[_ANY] * n, out_specs=[_ANY] * n,
        out_shape=[jax.ShapeDtypeStruct((N_DEV, *s.shape), s.dtype) for s in shards],
        scratch_shapes=[pltpu.SemaphoreType.DMA((n, 7)), pltpu.SemaphoreType.DMA((n, 7)), pltpu.SemaphoreType.DMA((n,))],
    )(*shards)


def _sibling_exchange(name, grads):
    n = len(grads)

    def body(*refs):
        g_refs, r_refs = refs[:n], refs[n:2 * n]
        send_sems, recv_sems = refs[2 * n:]
        x, y, c = lax.axis_index("x"), lax.axis_index("y"), lax.axis_index("c")
        copies = [
            pltpu.make_async_remote_copy(
                src_ref=g_refs[a].at[k, 1 - c], dst_ref=r_refs[a].at[k], send_sem=send_sems.at[a, k],
                recv_sem=recv_sems.at[a, k], device_id=(x, y, 1 - c), device_id_type=_MESH)
            for a in range(n) for k in range(4)]
        for cp in copies:
            cp.start()
        for cp in copies:
            cp.wait()

    return pl.pallas_call(
        body, name=name, in_specs=[_ANY] * n, out_specs=[_ANY] * n,
        out_shape=[jax.ShapeDtypeStruct((4, *g.shape[2:]), g.dtype) for g in grads],
        scratch_shapes=[pltpu.SemaphoreType.DMA((n, 4)), pltpu.SemaphoreType.DMA((n, 4))],
    )(*grads)


def _chip_exchange(name, parts):
    n = len(parts)

    def body(*refs):
        p_refs, r_refs = refs[:n], refs[n:2 * n]
        send_sems, recv_sems = refs[2 * n:]
        x, y, c = lax.axis_index("x"), lax.axis_index("y"), lax.axis_index("c")
        chips = [(1 - x, y), (x, 1 - y), (1 - x, 1 - y)]
        copies = [
            pltpu.make_async_remote_copy(
                src_ref=p_refs[a].at[2 * px + py], dst_ref=r_refs[a].at[j], send_sem=send_sems.at[a, j],
                recv_sem=recv_sems.at[a, j], device_id=(px, py, c), device_id_type=_MESH)
            for a in range(n) for j, (px, py) in enumerate(chips)]
        for cp in copies:
            cp.start()
        for cp in copies:
            cp.wait()

    return pl.pallas_call(
        body, name=name, in_specs=[_ANY] * n, out_specs=[_ANY] * n,
        out_shape=[jax.ShapeDtypeStruct((3, *p.shape[1:]), p.dtype) for p in parts],
        scratch_shapes=[pltpu.SemaphoreType.DMA((n, 3)), pltpu.SemaphoreType.DMA((n, 3))],
    )(*parts)


def _chip_partial(name, g4, r1, core):
    _, _, r, c = g4.shape
    tr = min(r, 256)

    def body(core_ref, g_ref, r_ref, o32_ref, o16_ref):
        s = g_ref[...] + r_ref[...]
        o32_ref[...] = s
        o16_ref[...] = s.astype(o16_ref.dtype)

    spec = pl.BlockSpec((None, tr, c), lambda k, i, core_ref: (k, i, 0))
    return pl.pallas_call(
        body, name=name,
        grid_spec=pltpu.PrefetchScalarGridSpec(
            num_scalar_prefetch=1, grid=(4, r // tr),
            in_specs=[pl.BlockSpec((None, None, tr, c), lambda k, i, core_ref: (k, core_ref[0], i, 0)), spec],
            out_specs=[spec, spec]),
        out_shape=[jax.ShapeDtypeStruct((4, r, c), F32), jax.ShapeDtypeStruct((4, r, c), BF16)],
        compiler_params=_params(("parallel", "parallel")),
    )(core, g4, r1)


def _adamw(w, g, m, v):
    m = ADAM_B1 * m + (1.0 - ADAM_B1) * g
    v = ADAM_B2 * v + (1.0 - ADAM_B2) * jnp.square(g)
    m_hat = m / (1.0 - ADAM_B1 ** ADAM_STEP)
    v_hat = v / (1.0 - ADAM_B2 ** ADAM_STEP)
    delta = -ADAM_LR * (m_hat / (jnp.sqrt(v_hat) + ADAM_EPS) + ADAM_WD * w)
    return delta, m, v


def _adamw_shard(name, p32, r2, chip, w, m, v):
    r, c = w.shape
    tr = min(r, 256)

    def body(chip_ref, p_ref, r_ref, w_ref, m_ref, v_ref, g_out, d_out, m_out, v_out):
        g = p_ref[...]
        for j in range(3):
            g = g + r_ref[j].astype(F32)
        delta, m_new, v_new = _adamw(w_ref[...], g, m_ref[...], v_ref[...])
        g_out[...] = g
        d_out[...] = delta
        m_out[...] = m_new
        v_out[...] = v_new

    spec = pl.BlockSpec((tr, c), lambda i, chip_ref: (i, 0))
    return pl.pallas_call(
        body, name=name,
        grid_spec=pltpu.PrefetchScalarGridSpec(
            num_scalar_prefetch=1, grid=(r // tr,),
            in_specs=[pl.BlockSpec((None, tr, c), lambda i, chip_ref: (chip_ref[0], i, 0)),
                      pl.BlockSpec((3, tr, c), lambda i, chip_ref: (0, i, 0)), spec, spec, spec],
            out_specs=[spec] * 4),
        out_shape=[jax.ShapeDtypeStruct((r, c), F32)] * 4,
        compiler_params=_params(("parallel",)),
    )(chip, p32, r2, w, m, v)


def _adamw_small(name, parts, w, m, v):
    _, r, c = parts.shape
    tr = 8

    def body(p_ref, w_ref, m_ref, v_ref, g_out, d_out, m_out, v_out):
        g = p_ref[0]
        for j in range(1, N_DEV):
            g = g + p_ref[j]
        delta, m_new, v_new = _adamw(w_ref[...], g, m_ref[...], v_ref[...])
        g_out[...] = g
        d_out[...] = delta
        m_out[...] = m_new
        v_out[...] = v_new

    spec = pl.BlockSpec((tr, c), lambda i: (i, 0))
    return pl.pallas_call(
        body, name=name, grid=(r // tr,),
        in_specs=[pl.BlockSpec((N_DEV, tr, c), lambda i: (0, i, 0)), spec, spec, spec],
        out_specs=[spec] * 4, out_shape=[jax.ShapeDtypeStruct((r, c), F32)] * 4,
        compiler_params=_params(("parallel",)),
    )(parts, w, m, v)


_WEIGHTS = ["norm1_g", "w_in", "q_norm_g", "k_norm_g", "ssm_lambda_re", "ssm_lambda_im", "ssm_log_dt", "ssm_b_re",
            "ssm_b_im", "ssm_c_re", "ssm_c_im", "ssm_d", "w_glu", "b_glu", "attn_out_g", "ssm_out_g", "w_out",
            "norm2_g", "w_mlp_in", "w_mlp_out"]
_BIG = ["w_in", "w_glu", "w_out", "w_mlp_in", "w_mlp_out"]
_SMALL = [n for n in _WEIGHTS if n not in _BIG]
_PACK_COLS = 1024


def _pack(tree, last=None):
    flat = [tree[n].reshape(-1).astype(F32) for n in _SMALL]
    size = sum(f.shape[0] for f in flat)
    rows = -(-(size + 1) // (_PACK_COLS * 8)) * 8
    pad = jnp.zeros((rows * _PACK_COLS - size - 1,), F32)
    tail = jnp.zeros((1,), F32) if last is None else last.reshape(1).astype(F32)
    return jnp.concatenate(flat + [pad, tail]).reshape(rows, _PACK_COLS)


def _unpack(buf, like):
    flat, out, off = buf.reshape(-1), {}, 0
    for n in _SMALL:
        size = like[n].size
        out[n] = flat[off:off + size].reshape(like[n].shape)
        off += size
    return out


def kernel(x, norm1_g, w_in, q_norm_g, k_norm_g, ssm_lambda_re, ssm_lambda_im, ssm_log_dt, ssm_b_re, ssm_b_im, ssm_c_re, ssm_c_im, ssm_d, w_glu, b_glu, attn_out_g, ssm_out_g, w_out, norm2_g, w_mlp_in, w_mlp_out, loss_target, m_norm1_g, m_w_in, m_q_norm_g, m_k_norm_g, m_ssm_lambda_re, m_ssm_lambda_im, m_ssm_log_dt, m_ssm_b_re, m_ssm_b_im, m_ssm_c_re, m_ssm_c_im, m_ssm_d, m_w_glu, m_b_glu, m_attn_out_g, m_ssm_out_g, m_w_out, m_norm2_g, m_w_mlp_in, m_w_mlp_out, v_norm1_g, v_w_in, v_q_norm_g, v_k_norm_g, v_ssm_lambda_re, v_ssm_lambda_im, v_ssm_log_dt, v_ssm_b_re, v_ssm_b_im, v_ssm_c_re, v_ssm_c_im, v_ssm_d, v_w_glu, v_b_glu, v_attn_out_g, v_ssm_out_g, v_w_out, v_norm2_g, v_w_mlp_in, v_w_mlp_out):
    w = dict(zip(_WEIGHTS, (norm1_g, w_in, q_norm_g, k_norm_g, ssm_lambda_re, ssm_lambda_im, ssm_log_dt, ssm_b_re, ssm_b_im, ssm_c_re, ssm_c_im, ssm_d, w_glu, b_glu, attn_out_g, ssm_out_g, w_out, norm2_g, w_mlp_in, w_mlp_out)))
    m = dict(zip(_WEIGHTS, (m_norm1_g, m_w_in, m_q_norm_g, m_k_norm_g, m_ssm_lambda_re, m_ssm_lambda_im, m_ssm_log_dt, m_ssm_b_re, m_ssm_b_im, m_ssm_c_re, m_ssm_c_im, m_ssm_d, m_w_glu, m_b_glu, m_attn_out_g, m_ssm_out_g, m_w_out, m_norm2_g, m_w_mlp_in, m_w_mlp_out)))
    v = dict(zip(_WEIGHTS, (v_norm1_g, v_w_in, v_q_norm_g, v_k_norm_g, v_ssm_lambda_re, v_ssm_lambda_im, v_ssm_log_dt, v_ssm_b_re, v_ssm_b_im, v_ssm_c_re, v_ssm_c_im, v_ssm_d, v_w_glu, v_b_glu, v_attn_out_g, v_ssm_out_g, v_w_out, v_norm2_g, v_w_mlp_in, v_w_mlp_out)))
    core = lax.axis_index("c").astype(jnp.int32).reshape(1)
    chip = (2 * lax.axis_index("x") + lax.axis_index("y")).astype(jnp.int32).reshape(1)

    gathered = dict(zip(_BIG, _all_gather("weights_all_gather", [w[n].astype(BF16) for n in _BIG])))
    full = {
        "w_in": gathered["w_in"].transpose(1, 0, 2).reshape(w_in.shape[0], -1),
        "w_glu": gathered["w_glu"].reshape(-1, w_glu.shape[1]),
        "w_out": gathered["w_out"].reshape(-1, w_out.shape[1]),
        "w_mlp_in": gathered["w_mlp_in"].transpose(1, 0, 2).reshape(w_mlp_in.shape[0], -1),
        "w_mlp_out": gathered["w_mlp_out"].reshape(-1, w_mlp_out.shape[1]),
    }

    loss_local, grad_x, g_small, g_big = _local_step(
        x, loss_target, {n: w[n] for n in _SMALL}, full["w_in"], full["w_glu"], full["w_out"], full["w_mlp_in"],
        full["w_mlp_out"])

    g4 = [g_big[n].reshape(4, 2, *w[n].shape) for n in _BIG]
    r1 = _sibling_exchange("grads_sibling_exchange", g4)
    parts = [_chip_partial("chip_partial_" + n, a, b, core) for n, a, b in zip(_BIG, g4, r1)]
    r2 = _chip_exchange("grads_chip_exchange", [p16 for _, p16 in parts])
    grads, delta, new_m, new_v = {}, {}, {}, {}
    for n, (p32, _), r in zip(_BIG, parts, r2):
        grads[n], delta[n], new_m[n], new_v[n] = _adamw_shard("adamw_" + n, p32, r, chip, w[n], m[n], v[n])

    (small_parts,) = _all_gather("small_grads_all_gather", [_pack(g_small, last=loss_local)])
    packed = _adamw_small("adamw_small", small_parts, _pack(w), _pack(m), _pack(v))
    for tree, buf in zip((grads, delta, new_m, new_v), packed):
        tree.update(_unpack(buf, w))
    loss = packed[0][-1, -1]

    return (loss, grad_x, *[grads[n] for n in _WEIGHTS], *[delta[n] for n in _WEIGHTS],
            *[new_m[n] for n in _WEIGHTS], *[new_v[n] for n in _WEIGHTS])
```

```python
import functools
import math

import jax
import jax.numpy as jnp
from jax import lax
from jax.experimental import pallas as pl
from jax.experimental.pallas import tpu as pltpu

F32 = jnp.float32
BF16 = jnp.bfloat16

EPS = 1e-6
HEAD_DIM = 64
N_HEADS = 8
SB_WIDTH = 512
SSM_WIDTH = 512
SSM_GROUP = 16
SSM_GROUPS = 32
SSM_STATE = 64
QBLOCK = 128
KBLOCK = 256
N_CHUNK = 8
SSM_COLS = 4
LANES = 128
N_DEV = 8

ADAM_LR = 0.001
ADAM_B1 = 0.9
ADAM_B2 = 0.999
ADAM_EPS = 1e-08
ADAM_WD = 0.01
ADAM_STEP = 10

VMEM_LIMIT = 56 * 1024 * 1024

_NT = (((1,), (1,)), ((), ()))
_NN = (((1,), (0,)), ((), ()))
_TN = (((0,), (0,)), ((), ()))


def _dot(a, b, dims=_NN):
    return lax.dot_general(a, b, dims, preferred_element_type=F32)


def _params(sem):
    return pltpu.CompilerParams(dimension_semantics=sem, vmem_limit_bytes=VMEM_LIMIT)


def _matmul(name, a, b, *, ta=False, tb=False, extras=(), epilogue=None, out_dtypes=(F32,),
            col_blocked=False, tm=1024, tn=512, tk=1024):
    M, K = (a.shape[1], a.shape[0]) if ta else a.shape
    N = b.shape[0] if tb else b.shape[1]
    tm, tn, tk = min(tm, M), min(tn, N), min(tk, K)
    assert M % tm == 0 and N % tn == 0 and K % tk == 0, (name, M, N, K)
    nk = K // tk
    n_ex, n_out = len(extras), len(out_dtypes)
    dims = (((0 if ta else 1,), (1 if tb else 0,)), ((), ()))

    def body(*refs):
        a_ref, b_ref = refs[0], refs[1]
        ex_refs = refs[2:2 + n_ex]
        o_refs = refs[2 + n_ex:2 + n_ex + n_out]
        k = pl.program_id(2)
        part = _dot(a_ref[...].astype(BF16), b_ref[...].astype(BF16), dims)

        def finish(acc):
            outs = (acc,) if epilogue is None else epilogue(acc, *[e[...] for e in ex_refs])
            for o_ref, o in zip(o_refs, outs):
                o_ref[...] = o.astype(o_ref.dtype)

        if nk == 1:
            finish(part)
        else:
            acc_ref = refs[-1]

            @pl.when(k == 0)
            def _():
                acc_ref[...] = part

            @pl.when(jnp.logical_and(k > 0, k < nk - 1))
            def _():
                acc_ref[...] += part

            @pl.when(k == nk - 1)
            def _():
                finish(acc_ref[...] + part)

    a_spec = pl.BlockSpec((tk, tm), lambda i, j, k: (k, i)) if ta else pl.BlockSpec((tm, tk), lambda i, j, k: (i, k))
    b_spec = pl.BlockSpec((tn, tk), lambda i, j, k: (j, k)) if tb else pl.BlockSpec((tk, tn), lambda i, j, k: (k, j))
    ex_specs = [pl.BlockSpec((tm, tn), lambda i, j, k: (i, j)) for _ in extras]
    if col_blocked:
        out_specs = [pl.BlockSpec((None, tm, tn), lambda i, j, k: (j, i, 0)) for _ in out_dtypes]
        out_shape = [jax.ShapeDtypeStruct((N // tn, M, tn), dt) for dt in out_dtypes]
    else:
        out_specs = [pl.BlockSpec((tm, tn), lambda i, j, k: (i, j)) for _ in out_dtypes]
        out_shape = [jax.ShapeDtypeStruct((M, N), dt) for dt in out_dtypes]
    outs = pl.pallas_call(
        body, name=name, grid=(M // tm, N // tn, nk),
        in_specs=[a_spec, b_spec, *ex_specs], out_specs=out_specs, out_shape=out_shape,
        scratch_shapes=[pltpu.VMEM((tm, tn), F32)] if nk > 1 else [],
        compiler_params=_params(("parallel", "parallel", "arbitrary")),
    )(a, b, *extras)
    return outs[0] if n_out == 1 else outs


def _rowwise(name, fn, rows, small, outs, sums=(), tile=256):
    specs, args = [], []
    T = None
    for r in rows:
        arr, cb, w = r if isinstance(r, tuple) else (r, 0, r.shape[1])
        T = arr.shape[0]
        specs.append((w, cb))
        args.append(arr)
    tile = min(tile, T)
    assert T % tile == 0
    n_r, n_s, n_o, n_a = len(rows), len(small), len(outs), len(sums)

    def body(*refs):
        r_refs = refs[:n_r]
        s_refs = refs[n_r:n_r + n_s]
        o_refs = refs[n_r + n_s:n_r + n_s + n_o]
        a_refs = refs[n_r + n_s + n_o:]
        res = fn(*[r[...] for r in r_refs], *[s[...] for s in s_refs])
        res = res if isinstance(res, (tuple, list)) else (res,)
        for o_ref, o in zip(o_refs, res[:n_o]):
            o_ref[...] = o.astype(o_ref.dtype)

        @pl.when(pl.program_id(0) == 0)
        def _():
            for a_ref in a_refs:
                a_ref[...] = jnp.zeros_like(a_ref)

        for a_ref, v in zip(a_refs, res[n_o:]):
            a_ref[...] += v.astype(F32)

    in_specs = [pl.BlockSpec((tile, w), functools.partial(lambda i, cb: (i, cb), cb=cb)) for w, cb in specs]
    in_specs += [pl.BlockSpec(s.shape, functools.partial(lambda i, nd: (0,) * nd, nd=s.ndim)) for s in small]
    out_specs = [pl.BlockSpec((tile, w), lambda i: (i, 0)) for w, _ in outs]
    out_specs += [pl.BlockSpec(s, functools.partial(lambda i, nd: (0,) * nd, nd=len(s))) for s in sums]
    out_shape = [jax.ShapeDtypeStruct((T, w), dt) for w, dt in outs]
    out_shape += [jax.ShapeDtypeStruct(s, F32) for s in sums]
    res = pl.pallas_call(
        body, name=name, grid=(T // tile,), in_specs=in_specs, out_specs=out_specs, out_shape=out_shape,
        compiler_params=_params(("arbitrary",)),
    )(*args, *small)
    return res[0] if len(res) == 1 else res


def _rms(x, g):
    return x * lax.rsqrt(jnp.mean(x * x, axis=-1, keepdims=True) + EPS) * g


def _glu_branch(y, pre, b_glu, g_out):
    g = jax.nn.gelu(y)
    return _rms(g * jax.nn.sigmoid(pre + b_glu), g_out)


def _split_dot(x, tri_bf):
    hi = x.astype(BF16)
    lo = (x - hi.astype(F32)).astype(BF16)
    return _dot(hi, tri_bf) + _dot(lo, tri_bf)


def _softplus(z):
    return jnp.maximum(z, 0.0) + jnp.log(1.0 + jnp.exp(-jnp.abs(z)))


def _head(h):
    return slice(h * HEAD_DIM, (h + 1) * HEAD_DIM)


def _head_mean(x, seg):
    return _split_dot(x, seg) * (1.0 / HEAD_DIM)


def _qk_norm(proj, gq, gk):
    T = proj.shape[0]
    scale = 1.0 / math.sqrt(HEAD_DIM)
    idx = jnp.arange(SB_WIDTH) // HEAD_DIM
    seg = (idx[:, None] == idx[None, :]).astype(BF16)

    def fn(q, k, v, gq_, gk_, seg_):
        qn = q * lax.rsqrt(_head_mean(q * q, seg_) + EPS) * (gq_ * scale)
        kn = k * lax.rsqrt(_head_mean(k * k, seg_) + EPS) * gk_
        return qn, kn, v

    return _rowwise("qk_norm", fn, [(proj, 0, SB_WIDTH), (proj, 1, SB_WIDTH), (proj, 2, SB_WIDTH)], [gq, gk, seg],
                    [(SB_WIDTH, BF16)] * 3)


def _qk_norm_bwd(proj, gq, gk, d_qn, d_kn):
    scale = 1.0 / math.sqrt(HEAD_DIM)
    idx = jnp.arange(SB_WIDTH) // HEAD_DIM
    seg = (idx[:, None] == idx[None, :]).astype(BF16)

    def one(x, g, dy, seg_):
        r = lax.rsqrt(_head_mean(x * x, seg_) + EPS)
        gdy = g * dy
        dx = r * gdy - x * (r * r * r) * _head_mean(gdy * x, seg_)
        dg = jnp.sum(dy * x * r, axis=0, keepdims=True)
        return dx, sum(dg[:, _head(h)] for h in range(N_HEADS))

    def fn(q, k, dqn, dkn, gq_, gk_, seg_):
        dq, dgq = one(q, gq_, dqn * scale, seg_)
        dk, dgk = one(k, gk_, dkn, seg_)
        return dq, dk, dgq, dgk

    return _rowwise("qk_norm_bwd", fn, [(proj, 0, SB_WIDTH), (proj, 1, SB_WIDTH), d_qn, d_kn], [gq, gk, seg],
                    [(SB_WIDTH, BF16)] * 2, sums=[(1, HEAD_DIM)] * 2)


def _split_heads(refs, scratch, L):
    def chunk(i, _):
        r = pl.ds(pl.multiple_of(i * QBLOCK, QBLOCK), QBLOCK)
        for ref, s in zip(refs, scratch):
            for h in range(2):
                s[h, r, :] = ref[r, _head(h)]
        return 0

    lax.fori_loop(0, L // QBLOCK, chunk, 0)


Q_HALVES = KBLOCK // QBLOCK
_CHAINS = [(h, r) for h in range(2) for r in range(Q_HALVES)]


def _valid(i, kb):
    row = lax.broadcasted_iota(jnp.int32, (QBLOCK, KBLOCK), 0)
    col = lax.broadcasted_iota(jnp.int32, (QBLOCK, KBLOCK), 1)
    return col + (kb * KBLOCK - i * QBLOCK) < row


def _attn_fwd(qn, kn, vb, B, L):
    n_pairs = L // KBLOCK
    n_hp = N_HEADS // 2

    def body(q_ref, k_ref, v_ref, o_ref, rt_ref, q_s, k_s, v_s, after_s):
        _split_heads((q_ref, k_ref, v_ref), (q_s, k_s, v_s), L)
        r2 = lax.broadcasted_iota(jnp.int32, (KBLOCK, KBLOCK), 0)
        c2 = lax.broadcasted_iota(jnp.int32, (KBLOCK, KBLOCK), 1)
        after_s[...] = (r2 > c2).astype(after_s.dtype)

        def q_pair(p, _):
            rows = [pl.ds(pl.multiple_of((p * Q_HALVES + r) * QBLOCK, QBLOCK), QBLOCK) for r in range(Q_HALVES)]
            q_c = [q_s[h, rows[r], :] for h, r in _CHAINS]
            cs = range(len(_CHAINS))

            def k_block(n, carry):
                kb = p - n
                rk = pl.ds(pl.multiple_of(kb * KBLOCK, KBLOCK), KBLOCK)
                valid = [_valid(p * Q_HALVES + r, kb) for r in range(Q_HALVES)]
                z = [_dot(q_c[c], k_s[_CHAINS[c][0], rk, :], _NT) for c in cs]
                sp = [_softplus(z[c]) for c in cs]
                lom = [jnp.where(valid[_CHAINS[c][1]], -sp[c], 0.0) for c in cs]
                tail = [_split_dot(lom[c], after_s[...]) + carry[c][0] for c in cs]
                a = [jnp.where(valid[_CHAINS[c][1]], jnp.exp(z[c] - sp[c] + tail[c]), 0.0) for c in cs]
                acc = [carry[c][1] + _dot(a[c].astype(v_s.dtype), v_s[_CHAINS[c][0], rk, :]) for c in cs]
                return tuple((carry[c][0] + jnp.sum(lom[c], axis=1, keepdims=True), acc[c]) for c in cs)

            init = (jnp.zeros((QBLOCK, 1), F32), jnp.zeros((QBLOCK, HEAD_DIM), F32))
            res = lax.fori_loop(0, p + 1, k_block, (init,) * len(_CHAINS))
            for r in range(Q_HALVES):
                mine = [res[c] for c in cs if _CHAINS[c][1] == r]
                o_ref[rows[r], :] = jnp.concatenate([m[1] for m in mine], axis=1)
                rt_ref[rows[r], :] = jnp.concatenate(
                    [jnp.broadcast_to(m[0], (QBLOCK, HEAD_DIM)) for m in mine], axis=1)
            return 0

        lax.fori_loop(0, n_pairs, q_pair, 0)

    spec = pl.BlockSpec((L, LANES), lambda b, p: (b, p))
    return pl.pallas_call(
        body, name="attn_fwd", grid=(B, n_hp),
        in_specs=[spec] * 3, out_specs=[spec, spec],
        out_shape=[jax.ShapeDtypeStruct((B * L, SB_WIDTH), F32)] * 2,
        scratch_shapes=[pltpu.VMEM((2, L, HEAD_DIM), BF16)] * 3 + [pltpu.VMEM((KBLOCK, KBLOCK), BF16)],
        compiler_params=_params(("parallel", "parallel")),
    )(qn, kn, vb)


def _attn_bwd(qn, kn, vb, rtot, d_sb, B, L):
    n_pairs = L // KBLOCK
    n_hp = N_HEADS // 2

    def body(q_ref, k_ref, v_ref, rt_ref, do_ref, dq_ref, dk_ref, dv_ref,
             q_s, k_s, v_s, qt_s, dkt_s, dvt_s, after_s, before_s):
        _split_heads((q_ref, k_ref, v_ref), (q_s, k_s, v_s), L)

        def transpose_q(i, _):
            r = pl.ds(pl.multiple_of(i * QBLOCK, QBLOCK), QBLOCK)
            qt_s[:, r] = q_ref[r, :].astype(F32).T.astype(qt_s.dtype)
            return 0

        lax.fori_loop(0, L // QBLOCK, transpose_q, 0)
        dkt_s[...] = jnp.zeros_like(dkt_s)
        dvt_s[...] = jnp.zeros_like(dvt_s)
        r2 = lax.broadcasted_iota(jnp.int32, (KBLOCK, KBLOCK), 0)
        c2 = lax.broadcasted_iota(jnp.int32, (KBLOCK, KBLOCK), 1)
        after_s[...] = (r2 > c2).astype(after_s.dtype)
        before_s[...] = (r2 < c2).astype(before_s.dtype)

        def q_pair(p, _):
            rows = [pl.ds(pl.multiple_of((p * Q_HALVES + r) * QBLOCK, QBLOCK), QBLOCK) for r in range(Q_HALVES)]
            pair = pl.ds(pl.multiple_of(p * KBLOCK, KBLOCK), KBLOCK)
            do2 = do_ref[pair, :]
            do_t = do2.T.astype(v_s.dtype)
            tot2 = rt_ref[pair, :]
            cs = range(len(_CHAINS))
            hs = range(2)
            q_c = [q_s[h, rows[r], :] for h, r in _CHAINS]
            do_c = [do2[r * QBLOCK:(r + 1) * QBLOCK, _head(h)].astype(v_s.dtype) for h, r in _CHAINS]
            total = [tot2[r * QBLOCK:(r + 1) * QBLOCK, h * HEAD_DIM:h * HEAD_DIM + 1] for h, r in _CHAINS]
            qt_h = [qt_s[_head(h), pair] for h in hs]
            dot_h = [do_t[_head(h), :] for h in hs]

            def k_block(kb, carry):
                rk = pl.ds(pl.multiple_of(kb * KBLOCK, KBLOCK), KBLOCK)
                valid = [_valid(p * Q_HALVES + r, kb) for r in range(Q_HALVES)]
                ok = [valid[_CHAINS[c][1]] for c in cs]
                k_b = [k_s[h, rk, :] for h in hs]
                z = [_dot(q_c[c], k_b[_CHAINS[c][0]], _NT) for c in cs]
                da = [_dot(do_c[c], v_s[_CHAINS[c][0], rk, :], _NT) for c in cs]
                sp = [_softplus(z[c]) for c in cs]
                lom = [jnp.where(ok[c], -sp[c], 0.0) for c in cs]
                lom_sum = [jnp.sum(lom[c], axis=1, keepdims=True) for c in cs]
                tail = [_split_dot(lom[c], after_s[...]) + (total[c] - carry[c][0] - lom_sum[c]) for c in cs]
                a = [jnp.where(ok[c], jnp.exp(z[c] - sp[c] + tail[c]), 0.0) for c in cs]
                dla = [a[c] * da[c] for c in cs]
                for h in hs:
                    a_h = jnp.concatenate([a[c].astype(v_s.dtype) for c in cs if _CHAINS[c][0] == h], axis=0)
                    dvt_s[_head(h), rk] += _dot(dot_h[h], a_h)
                d_lom = [carry[c][1] + _split_dot(dla[c], before_s[...]) for c in cs]
                beta = [jnp.exp(z[c] - sp[c]) for c in cs]
                dz_b = [(dla[c] * (1.0 - beta[c]) - jnp.where(ok[c], beta[c] * d_lom[c], 0.0)).astype(v_s.dtype)
                        for c in cs]
                dq_acc = [carry[c][2] + _dot(dz_b[c], k_b[_CHAINS[c][0]]) for c in cs]
                for h in hs:
                    dz_h = jnp.concatenate([dz_b[c] for c in cs if _CHAINS[c][0] == h], axis=0)
                    dkt_s[_head(h), rk] += _dot(qt_h[h], dz_h)
                return tuple((carry[c][0] + lom_sum[c], carry[c][1] + jnp.sum(dla[c], axis=1, keepdims=True),
                              dq_acc[c]) for c in cs)

            zero = jnp.zeros((QBLOCK, 1), F32)
            init = (zero, zero, jnp.zeros((QBLOCK, HEAD_DIM), F32))
            res = lax.fori_loop(0, p + 1, k_block, (init,) * len(_CHAINS))
            for r in range(Q_HALVES):
                dq_ref[rows[r], :] = jnp.concatenate([res[c][2] for c in cs if _CHAINS[c][1] == r], axis=1)
            return 0

        lax.fori_loop(0, n_pairs, q_pair, 0)

        def transpose_out(i, _):
            r = pl.ds(pl.multiple_of(i * QBLOCK, QBLOCK), QBLOCK)
            dk_ref[r, :] = dkt_s[:, r].T
            dv_ref[r, :] = dvt_s[:, r].T.astype(dv_ref.dtype)
            return 0

        lax.fori_loop(0, L // QBLOCK, transpose_out, 0)

    spec = pl.BlockSpec((L, LANES), lambda b, p: (b, p))
    return pl.pallas_call(
        body, name="attn_bwd", grid=(B, n_hp),
        in_specs=[spec] * 5, out_specs=[spec] * 3,
        out_shape=[jax.ShapeDtypeStruct((B * L, SB_WIDTH), F32)] * 2 + [jax.ShapeDtypeStruct((B * L, SB_WIDTH), BF16)],
        scratch_shapes=[pltpu.VMEM((2, L, HEAD_DIM), BF16)] * 3 + [pltpu.VMEM((LANES, L), BF16)]
        + [pltpu.VMEM((LANES, L), F32)] * 2 + [pltpu.VMEM((KBLOCK, KBLOCK), BF16)] * 2,
        compiler_params=_params(("parallel", "parallel")),
    )(qn, kn, vb, rtot, d_sb)


def _ssm_discretise(lam_re, lam_im, log_dt, b_re, b_im):
    dt = jnp.exp(log_dt)
    mag = jnp.exp(lam_re * dt)
    lbr = mag * jnp.cos(lam_im * dt)
    lbi = mag * jnp.sin(lam_im * dt)
    den = lam_re * lam_re + lam_im * lam_im
    nr, ni = lbr - 1.0, lbi
    cr = (nr * lam_re + ni * lam_im) / den
    ci = (ni * lam_re - nr * lam_im) / den
    return lbr, lbi, cr * b_re - ci * b_im, cr * b_im + ci * b_re


def _ssm_prep(lam_re, lam_im, log_dt, b_re_t, b_im_t):
    def body(lr, li, ld, br, bi, o_lr, o_li, o_br, o_bi):
        res = _ssm_discretise(lr[...], li[...], ld[...], br[...], bi[...])
        for o, v in zip((o_lr, o_li, o_br, o_bi), res):
            o[...] = v

    return pl.pallas_call(
        body, name="ssm_prep",
        out_shape=[jax.ShapeDtypeStruct(lam_re.shape, F32)] * 2 + [jax.ShapeDtypeStruct(b_re_t.shape, F32)] * 2,
    )(lam_re, lam_im, log_dt, b_re_t, b_im_t)


def _ssm_prep_bwd(lam_re, lam_im, log_dt, b_re_t, b_im_t, d_lr, d_li, d_br, d_bi):
    def body(lr, li, ld, br, bi, g_lr, g_li, g_br, g_bi, o_lr, o_li, o_ld, o_br, o_bi):
        _, vjp = jax.vjp(_ssm_discretise, lr[...], li[...], ld[...], br[...], bi[...])
        res = vjp((g_lr[...], g_li[...], g_br[...], g_bi[...]))
        for o, v in zip((o_lr, o_li, o_ld, o_br, o_bi), res):
            o[...] = v

    return pl.pallas_call(
        body, name="ssm_prep_bwd",
        out_shape=[jax.ShapeDtypeStruct(lam_re.shape, F32)] * 2 + [jax.ShapeDtypeStruct(log_dt.shape, F32)]
        + [jax.ShapeDtypeStruct(b_re_t.shape, F32)] * 2,
    )(lam_re, lam_im, log_dt, b_re_t, b_im_t, d_lr, d_li, d_br, d_bi)


def _block_diag(m):
    m4 = m.reshape(SSM_COLS, 8, SSM_GROUP, SSM_STATE)
    return jnp.einsum("aghp,gk->aghkp", m4, jnp.eye(8, dtype=m.dtype)).reshape(SSM_COLS, LANES, 512)


def _block_diag_take(d):
    d6 = d.reshape(SSM_COLS, 8, SSM_GROUP, 2, 8, SSM_STATE)
    return jnp.einsum("aghrgp->raghp", d6).reshape(2, SSM_GROUPS, SSM_GROUP, SSM_STATE)


def _cmul(ar, ai, br, bi):
    return ar * br - ai * bi, ar * bi + ai * br


def _power(lr, li, n):
    assert n & (n - 1) == 0
    for _ in range(n.bit_length() - 1):
        lr, li = _cmul(lr, li, lr, li)
    return lr, li


def _ssm_fwd(u_p, w_b, lam_r, lam_i, c_m, d_skip, B, L, tj):
    J = L // N_CHUNK
    njt = J // tj
    R = tj * N_CHUNK
    H = 512

    def body(u_ref, wb_ref, lr_ref, li_ref, cm_ref, d_ref, y_ref, x_ref, xin_ref, bu_s, st_s, xin_s):
        ph, jt = pl.program_id(2), pl.program_id(3)
        lr, li = lr_ref[...], li_ref[...]

        @pl.when(jnp.logical_and(ph == 0, jt == 0))
        def _():
            st_s[...] = jnp.zeros_like(st_s)

        bu_s[...] = _dot(u_ref[...].astype(BF16), wb_ref[...].astype(BF16))

        def scan(store):
            def step(j, carry):
                xr, xi = carry
                r = pl.ds(pl.multiple_of(j * N_CHUNK, N_CHUNK), N_CHUNK)
                nr = lr * xr - li * xi + bu_s[r, 0:H]
                ni = lr * xi + li * xr + bu_s[r, H:2 * H]
                if store:
                    x_ref[r, 0:H] = nr
                    x_ref[r, H:2 * H] = ni
                return nr, ni

            xr, xi = lax.fori_loop(0, tj, step, (st_s[:, 0:H], st_s[:, H:2 * H]))
            st_s[:, 0:H] = xr
            st_s[:, H:2 * H] = xi

        @pl.when(ph == 0)
        def _():
            scan(False)

            @pl.when(jt == njt - 1)
            def _():
                pr, pi = _power(lr[0:1], li[0:1], J)
                xin_s[0:1, :] = jnp.zeros((1, 2 * H), F32)
                for c in range(1, N_CHUNK):
                    qr, qi = _cmul(pr, pi, xin_s[c - 1:c, 0:H], xin_s[c - 1:c, H:2 * H])
                    xin_s[c:c + 1, 0:H] = qr + st_s[c - 1:c, 0:H]
                    xin_s[c:c + 1, H:2 * H] = qi + st_s[c - 1:c, H:2 * H]
                xin_ref[...] = xin_s[...]
                st_s[...] = xin_s[...]

        @pl.when(ph == 1)
        def _():
            scan(True)
            y = _dot(x_ref[...].astype(BF16), cm_ref[...].astype(BF16))
            y_ref[...] = y + d_ref[...] * u_ref[...]

    return pl.pallas_call(
        body, name="ssm_fwd", grid=(SSM_COLS, B, 2, njt),
        in_specs=[
            pl.BlockSpec((None, R, LANES), lambda i, b, ph, jt: (b, jt, i)),
            pl.BlockSpec((None, LANES, 2 * H), lambda i, b, ph, jt: (i, 0, 0)),
            pl.BlockSpec((None, N_CHUNK, H), lambda i, b, ph, jt: (i, 0, 0)),
            pl.BlockSpec((None, N_CHUNK, H), lambda i, b, ph, jt: (i, 0, 0)),
            pl.BlockSpec((None, 2 * H, LANES), lambda i, b, ph, jt: (i, 0, 0)),
            pl.BlockSpec((1, LANES), lambda i, b, ph, jt: (0, i)),
        ],
        out_specs=[
            pl.BlockSpec((None, R, LANES), lambda i, b, ph, jt: (b, jt * ph, i)),
            pl.BlockSpec((None, R, 2 * H), lambda i, b, ph, jt: (b, jt * ph, i)),
            pl.BlockSpec((None, None, N_CHUNK, 2 * H), lambda i, b, ph, jt: (b, i, 0, 0)),
        ],
        out_shape=[
            jax.ShapeDtypeStruct((B, L, SSM_WIDTH), F32),
            jax.ShapeDtypeStruct((B, L, SSM_COLS * 2 * H), F32),
            jax.ShapeDtypeStruct((B, SSM_COLS, N_CHUNK, 2 * H), F32),
        ],
        scratch_shapes=[pltpu.VMEM((R, 2 * H), F32), pltpu.VMEM((N_CHUNK, 2 * H), F32), pltpu.VMEM((N_CHUNK, 2 * H), F32)],
        compiler_params=_params(("arbitrary",) * 4),
    )(u_p, w_b, lam_r, lam_i, c_m, d_skip)


def _ssm_bwd(dy_p, u_p, x, xin, w_bt, lam_r, lam_i, c_mt, d_skip, B, L, tj):
    J = L // N_CHUNK
    njt = J // tj
    R = tj * N_CHUNK
    H = 512
    x4 = x.reshape(B, J, N_CHUNK, SSM_COLS * 2 * H)

    def body(dy_ref, u_ref, x_ref, xp_ref, xin_ref, wbt_ref, lr_ref, li_ref, cmt_ref, d_ref,
             du_ref, dwb_ref, dcm_ref, dlr_ref, dli_ref, dd_ref, ca_s, a_s, st_s, dl_s):
        b, ph, jt = pl.program_id(1), pl.program_id(2), pl.program_id(3)
        jr = njt - 1 - jt
        lr, li = lr_ref[...], -li_ref[...]

        @pl.when(jnp.logical_and(b == 0, jnp.logical_and(ph == 0, jt == 0)))
        def _():
            dwb_ref[...] = jnp.zeros_like(dwb_ref)
            dcm_ref[...] = jnp.zeros_like(dcm_ref)
            dlr_ref[...] = jnp.zeros_like(dlr_ref)
            dli_ref[...] = jnp.zeros_like(dli_ref)
            dd_ref[...] = jnp.zeros_like(dd_ref)
            dl_s[...] = jnp.zeros_like(dl_s)

        @pl.when(jnp.logical_and(ph == 0, jt == 0))
        def _():
            st_s[...] = jnp.zeros_like(st_s)

        ca_s[...] = _dot(dy_ref[...].astype(BF16), cmt_ref[...].astype(BF16))

        def scan(store):
            def step(n, carry):
                ar, ai = carry
                r = pl.ds(pl.multiple_of((tj - 1 - n) * N_CHUNK, N_CHUNK), N_CHUNK)
                nr = lr * ar - li * ai + ca_s[r, 0:H]
                ni = lr * ai + li * ar + ca_s[r, H:2 * H]
                if store:
                    a_s[r, 0:H] = nr
                    a_s[r, H:2 * H] = ni
                return nr, ni

            ar, ai = lax.fori_loop(0, tj, step, (st_s[:, 0:H], st_s[:, H:2 * H]))
            st_s[:, 0:H] = ar
            st_s[:, H:2 * H] = ai

        @pl.when(ph == 0)
        def _():
            scan(False)

            @pl.when(jt == njt - 1)
            def _():
                pr, pi = _power(lr[0:1], li[0:1], J)
                a_s[N_CHUNK - 1:N_CHUNK, :] = jnp.zeros((1, 2 * H), F32)
                for c in range(N_CHUNK - 2, -1, -1):
                    qr, qi = _cmul(pr, pi, a_s[c + 1:c + 2, 0:H], a_s[c + 1:c + 2, H:2 * H])
                    a_s[c:c + 1, 0:H] = qr + st_s[c + 1:c + 2, 0:H]
                    a_s[c:c + 1, H:2 * H] = qi + st_s[c + 1:c + 2, H:2 * H]
                st_s[...] = a_s[0:N_CHUNK, :]

        @pl.when(ph == 1)
        def _():
            scan(True)
            dy = dy_ref[...]
            u = u_ref[...]
            a_b = a_s[...].astype(BF16)
            du_ref[...] = (_dot(a_b, wbt_ref[...].astype(BF16)) + d_ref[...] * dy).astype(du_ref.dtype)
            dwb_ref[...] += _dot(u.astype(BF16), a_b, _TN)
            dcm_ref[...] += _dot(x_ref[...].astype(BF16), dy.astype(BF16), _TN)
            dd_ref[...] += jnp.sum(dy * u, axis=0, keepdims=True)

            first = jnp.where(jr == 0, xin_ref[...], xp_ref[...])
            a0r, a0i = a_s[0:N_CHUNK, 0:H], a_s[0:N_CHUNK, H:2 * H]
            acc0 = (a0r * first[:, 0:H] + a0i * first[:, H:2 * H], a0i * first[:, 0:H] - a0r * first[:, H:2 * H])

            def step(j, carry):
                sr, si = carry
                r = pl.ds(pl.multiple_of(j * N_CHUNK, N_CHUNK), N_CHUNK)
                rp = pl.ds(pl.multiple_of((j - 1) * N_CHUNK, N_CHUNK), N_CHUNK)
                ar, ai = a_s[r, 0:H], a_s[r, H:2 * H]
                xr, xi = x_ref[rp, 0:H], x_ref[rp, H:2 * H]
                return sr + ar * xr + ai * xi, si + ai * xr - ar * xi

            sr, si = lax.fori_loop(1, tj, step, acc0)
            dl_s[:, 0:H] += sr
            dl_s[:, H:2 * H] += si

            @pl.when(jnp.logical_and(b == B - 1, jt == njt - 1))
            def _():
                dlr_ref[...] = jnp.sum(dl_s[:, 0:H], axis=0, keepdims=True)
                dli_ref[...] = jnp.sum(dl_s[:, H:2 * H], axis=0, keepdims=True)
                dl_s[...] = jnp.zeros_like(dl_s)

    rev = lambda ph, jt: (njt - 1 - jt) * ph + (njt - 1) * (1 - ph)
    return pl.pallas_call(
        body, name="ssm_bwd", grid=(SSM_COLS, B, 2, njt),
        in_specs=[
            pl.BlockSpec((None, R, LANES), lambda i, b, ph, jt: (b, njt - 1 - jt, i)),
            pl.BlockSpec((None, R, LANES), lambda i, b, ph, jt: (b, njt - 1 - jt, i)),
            pl.BlockSpec((None, R, 2 * H), lambda i, b, ph, jt: (b, rev(ph, jt), i)),
            pl.BlockSpec((None, None, N_CHUNK, 2 * H),
                         lambda i, b, ph, jt: (b, jnp.maximum((njt - 1 - jt) * tj - 1, 0), 0, i)),
            pl.BlockSpec((None, None, N_CHUNK, 2 * H), lambda i, b, ph, jt: (b, i, 0, 0)),
            pl.BlockSpec((None, 2 * H, LANES), lambda i, b, ph, jt: (i, 0, 0)),
            pl.BlockSpec((None, N_CHUNK, H), lambda i, b, ph, jt: (i, 0, 0)),
            pl.BlockSpec((None, N_CHUNK, H), lambda i, b, ph, jt: (i, 0, 0)),
            pl.BlockSpec((None, LANES, 2 * H), lambda i, b, ph, jt: (i, 0, 0)),
            pl.BlockSpec((1, LANES), lambda i, b, ph, jt: (0, i)),
        ],
        out_specs=[
            pl.BlockSpec((None, R, LANES), lambda i, b, ph, jt: (b, rev(ph, jt), i)),
            pl.BlockSpec((None, LANES, 2 * H), lambda i, b, ph, jt: (i, 0, 0)),
            pl.BlockSpec((None, 2 * H, LANES), lambda i, b, ph, jt: (i, 0, 0)),
            pl.BlockSpec((None, 1, H), lambda i, b, ph, jt: (i, 0, 0)),
            pl.BlockSpec((None, 1, H), lambda i, b, ph, jt: (i, 0, 0)),
            pl.BlockSpec((1, LANES), lambda i, b, ph, jt: (0, i)),
        ],
        out_shape=[
            jax.ShapeDtypeStruct((B, L, SSM_WIDTH), BF16),
            jax.ShapeDtypeStruct((SSM_COLS, LANES, 2 * H), F32),
            jax.ShapeDtypeStruct((SSM_COLS, 2 * H, LANES), F32),
            jax.ShapeDtypeStruct((SSM_COLS, 1, H), F32),
            jax.ShapeDtypeStruct((SSM_COLS, 1, H), F32),
            jax.ShapeDtypeStruct((1, SSM_WIDTH), F32),
        ],
        scratch_shapes=[pltpu.VMEM((R, 2 * H), F32), pltpu.VMEM((R, 2 * H), F32),
                        pltpu.VMEM((N_CHUNK, 2 * H), F32), pltpu.VMEM((N_CHUNK, 2 * H), F32)],
        compiler_params=_params(("arbitrary",) * 4),
    )(dy_p, u_p, x, x4, xin, w_bt, lam_r, lam_i, c_mt, d_skip)


def _to_scan_layout(t, B, L):
    C = t.shape[-1]
    return t.reshape(B, N_CHUNK, L // N_CHUNK, C).transpose(0, 2, 1, 3).reshape(B, L, C)


def _from_scan_layout(t, B, L):
    C = t.shape[-1]
    return t.reshape(B, L // N_CHUNK, N_CHUNK, C).transpose(0, 2, 1, 3).reshape(B * L, C)


def _local_step(x, target, p, w_in, late_weights, mlp_grads_ready=None, order=None, *, ssm_tile=32):
    B, L, D = x.shape
    T = B * L
    x2 = x.reshape(T, D)
    row = lambda v: v.reshape(1, -1)
    g1, g2, ga, gs, b_glu = row(p["norm1_g"]), row(p["norm2_g"]), row(p["attn_out_g"]), row(p["ssm_out_g"]), row(p["b_glu"])
    g1_first = g1 if order is None else g1 + order
    gq8 = jnp.tile(row(p["q_norm_g"]), (1, N_HEADS))
    gk8 = jnp.tile(row(p["k_norm_g"]), (1, N_HEADS))

    G, P, Hh = SSM_GROUPS, SSM_STATE, SSM_GROUP
    lam_re3, lam_im3 = p["ssm_lambda_re"].reshape(G, 1, P), p["ssm_lambda_im"].reshape(G, 1, P)
    log_dt3 = p["ssm_log_dt"].reshape(G, 1, 1)
    b_re_t, b_im_t = p["ssm_b_re"].transpose(0, 2, 1), p["ssm_b_im"].transpose(0, 2, 1)
    lbr, lbi, bbr, bbi = _ssm_prep(lam_re3, lam_im3, log_dt3, b_re_t, b_im_t)
    w_b = jnp.concatenate([_block_diag(bbr), _block_diag(bbi)], axis=2)
    c_mt = jnp.concatenate([_block_diag(p["ssm_c_re"]), -_block_diag(p["ssm_c_im"])], axis=2)
    w_bt, c_m = w_b.transpose(0, 2, 1), c_mt.transpose(0, 2, 1)
    lam_r = jnp.broadcast_to(lbr.reshape(SSM_COLS, 1, 512), (SSM_COLS, N_CHUNK, 512))
    lam_i = jnp.broadcast_to(lbi.reshape(SSM_COLS, 1, 512), (SSM_COLS, N_CHUNK, 512))
    d_skip = p["ssm_d"].reshape(1, SSM_WIDTH)

    xn = _rowwise("norm1", _rms, [x2], [g1_first], [(D, BF16)])
    proj = _matmul("proj", xn, w_in)
    qn, kn, vb = _qk_norm(proj, gq8, gk8)
    sb, rtot = _attn_fwd(qn, kn, vb, B, L)
    u_p = _to_scan_layout(proj[:, 3 * SB_WIDTH:], B, L)
    y_p, xs, xin = _ssm_fwd(u_p, w_b, lam_r, lam_i, c_m, d_skip, B, L, ssm_tile)
    y2 = y_p.reshape(T, SSM_WIDTH)
    gel = _rowwise("gelu", jax.nn.gelu, [y2], [], [(SSM_WIDTH, BF16)])
    w_glu, w_out, w_mlp_in, w_mlp_out = late_weights(gel)
    pre = _matmul("glu_gate", gel, w_glu)
    ssm_n = _rowwise("glu_out", _glu_branch, [y2, pre], [b_glu, gs], [(SSM_WIDTH, BF16)])
    sb_n = _rowwise("attn_out_norm", _rms, [sb], [ga], [(SB_WIDTH, BF16)])
    mixed = jnp.concatenate([sb_n, _from_scan_layout(ssm_n, B, L)], axis=1)
    h1 = _matmul("out_proj", mixed, w_out, extras=[x2], epilogue=lambda acc, r: (acc + r,))
    hn = _rowwise("norm2", _rms, [h1], [g2], [(D, BF16)])
    act, a_pre = _matmul("mlp_in", hn, w_mlp_in, out_dtypes=(BF16, BF16),
                         epilogue=lambda acc: (jnp.square(jnp.maximum(acc, 0.0)), acc))
    out = _matmul("mlp_out", act, w_mlp_out, extras=[h1], epilogue=lambda acc, r: (acc + r,))

    def loss_fn(o, t):
        diff = o - t
        part = jnp.sum(jnp.sum(diff * diff, axis=0, keepdims=True), axis=1, keepdims=True)
        d = diff * (1.0 / D)
        return d, d, part * (0.5 / D)

    d_out, d_out_b, loss = _rowwise("loss", loss_fn, [out, target.reshape(T, D)], [], [(D, F32), (D, BF16)],
                                    sums=[(1, 1)])

    d_apre = _matmul("mlp_out_dx", d_out_b, w_mlp_out, tb=True, extras=[a_pre], out_dtypes=(BF16,),
                     epilogue=lambda acc, ap: (acc * (2.0 * jnp.maximum(ap.astype(F32), 0.0)),))
    both = lambda acc: (acc, acc)
    g_w_mlp_out, g_w_mlp_out_b = _matmul("mlp_out_dw", act, d_out_b, ta=True, out_dtypes=(F32, BF16), epilogue=both)
    g_w_mlp_in, g_w_mlp_in_b = _matmul("mlp_in_dw", hn, d_apre, ta=True, col_blocked=True, out_dtypes=(F32, BF16),
                                       epilogue=both, tn=w_mlp_in.shape[1] // N_DEV)
    if mlp_grads_ready is not None:
        g2 = g2 + mlp_grads_ready(g_w_mlp_out, g_w_mlp_out_b, g_w_mlp_in, g_w_mlp_in_b)
    d_hn = _matmul("mlp_in_dx", d_apre, w_mlp_in, tb=True)

    def norm_bwd_res(h, dy, res, g):
        _, vjp = jax.vjp(_rms, h, g)
        dh, dg = vjp(dy)
        return res + dh, dg

    def norm_bwd_res2(h, dy, res, g):
        d, dg = norm_bwd_res(h, dy, res, g)
        return d, d, dg

    d_h1, d_h1_b, g_norm2 = _rowwise("norm2_bwd", norm_bwd_res2, [h1, d_hn, d_out], [g2], [(D, F32), (D, BF16)],
                                     sums=[(1, D)])

    d_mixed = _matmul("out_proj_dx", d_h1_b, w_out, tb=True)
    g_w_out = _matmul("out_proj_dw", mixed, d_h1_b, ta=True)

    def norm_bwd(h, dy, g):
        _, vjp = jax.vjp(_rms, h, g)
        return vjp(dy)

    d_sb, g_attn_out = _rowwise("attn_out_norm_bwd", norm_bwd, [sb, (d_mixed, 0, SB_WIDTH)], [ga],
                                [(SB_WIDTH, F32)], sums=[(1, SB_WIDTH)])
    d_ssm_n = _to_scan_layout(d_mixed[:, SB_WIDTH:], B, L).reshape(T, SSM_WIDTH)

    def glu_bwd(y, pre_, dy, bg, g):
        _, vjp = jax.vjp(_glu_branch, y, pre_, bg, g)
        d_y, d_pre, d_bg, d_g = vjp(dy)
        return d_y, d_pre, d_bg, d_g

    d_y_direct, d_pre, g_b_glu, g_ssm_out = _rowwise(
        "glu_out_bwd", glu_bwd, [y2, pre, d_ssm_n], [b_glu, gs], [(SSM_WIDTH, F32), (SSM_WIDTH, BF16)],
        sums=[(1, SSM_WIDTH), (1, SSM_WIDTH)])
    d_gel = _matmul("glu_gate_dx", d_pre, w_glu, tb=True)
    g_w_glu = _matmul("glu_gate_dw", gel, d_pre, ta=True)

    def gelu_bwd(y, dg, dy0):
        _, vjp = jax.vjp(jax.nn.gelu, y)
        return dy0 + vjp(dg)[0]

    d_y = _rowwise("gelu_bwd", gelu_bwd, [y2, d_gel, d_y_direct], [], [(SSM_WIDTH, F32)])

    du_p, d_wb, d_cm, d_lr, d_li, g_d = _ssm_bwd(
        d_y.reshape(B, L, SSM_WIDTH), u_p, xs, xin, w_bt, lam_r, lam_i, c_mt, d_skip, B, L, ssm_tile)
    d_bb = _block_diag_take(d_wb.reshape(SSM_COLS, LANES, 2, 512))
    d_c = _block_diag_take(d_cm.transpose(0, 2, 1).reshape(SSM_COLS, LANES, 2, 512))
    g_lam_re, g_lam_im, g_log_dt, g_b_re_t, g_b_im_t = _ssm_prep_bwd(
        lam_re3, lam_im3, log_dt3, b_re_t, b_im_t,
        d_lr.reshape(G, 1, P), d_li.reshape(G, 1, P), d_bb[0], d_bb[1])
    d_qn, d_kn, d_v = _attn_bwd(qn, kn, vb, rtot, d_sb, B, L)
    d_q, d_k, g_q, g_k = _qk_norm_bwd(proj, gq8, gk8, d_qn, d_kn)

    d_proj = jnp.concatenate([d_q, d_k, d_v, _from_scan_layout(du_p, B, L)], axis=1)
    g_w_in = _matmul("proj_dw", xn, d_proj, ta=True, col_blocked=True, tn=w_in.shape[1] // N_DEV)
    d_xn = _matmul("proj_dx", d_proj, w_in, tb=True)
    grad_x, g_norm1 = _rowwise("norm1_bwd", norm_bwd_res, [x2, d_xn, d_h1], [g1], [(D, F32)], sums=[(1, D)])

    small = {
        "norm1_g": g_norm1.reshape(-1),
        "q_norm_g": g_q.reshape(-1),
        "k_norm_g": g_k.reshape(-1),
        "ssm_lambda_re": g_lam_re.reshape(G, P),
        "ssm_lambda_im": g_lam_im.reshape(G, P),
        "ssm_log_dt": g_log_dt.reshape(G),
        "ssm_b_re": g_b_re_t.transpose(0, 2, 1),
        "ssm_b_im": g_b_im_t.transpose(0, 2, 1),
        "ssm_c_re": d_c[0],
        "ssm_c_im": -d_c[1],
        "ssm_d": g_d.reshape(G, Hh),
        "b_glu": g_b_glu.reshape(-1),
        "attn_out_g": g_attn_out.reshape(-1),
        "ssm_out_g": g_ssm_out.reshape(-1),
        "norm2_g": g_norm2.reshape(-1),
    }
    big = {"w_in": g_w_in, "w_glu": g_w_glu, "w_out": g_w_out, "w_mlp_in": g_w_mlp_in, "w_mlp_out": g_w_mlp_out}
    return loss[0, 0], grad_x.reshape(B, L, D), small, big


_ANY = pl.BlockSpec(memory_space=pl.ANY)
_MESH = pl.DeviceIdType.MESH


def _all_gather(name, shards):
    n = len(shards)

    def body(*refs):
        in_refs, out_refs = refs[:n], refs[n:2 * n]
        send_sems, recv_sems, local_sems = refs[2 * n:]
        x, y, c = lax.axis_index("x"), lax.axis_index("y"), lax.axis_index("c")
        me, sibling = (x, y, c), (x, y, 1 - c)
        chips = [(1 - x, y), (x, 1 - y), (1 - x, 1 - y)]

        def copy(a, k, block, to, src=None):
            px, py, pc = block
            rows = out_refs[a].at[4 * px + 2 * py + pc]
            return pltpu.make_async_remote_copy(
                src_ref=rows if src is None else src, dst_ref=rows, send_sem=send_sems.at[a, k],
                recv_sem=recv_sems.at[a, k], device_id=to, device_id_type=_MESH)

        mine = [pltpu.make_async_copy(in_refs[a], out_refs[a].at[4 * x + 2 * y + c], local_sems.at[a]) for a in range(n)]
        first, passed = [], []
        for a in range(n):
            mine[a].start()
            first.append(copy(a, 0, me, sibling, src=in_refs[a]))
            first += [copy(a, 1 + j, me, (*chip, c), src=in_refs[a]) for j, chip in enumerate(chips)]
        for cp in first:
            cp.start()
        for j, chip in enumerate(chips):
            for a in range(n):
                copy(a, 1 + j, (*chip, c), me).wait_recv()
                fwd = copy(a, 4 + j, (*chip, c), sibling)
                fwd.start()
                passed.append(fwd)
        for a in range(n):
            copy(a, 0, sibling, me).wait_recv()
            for j, chip in enumerate(chips):
                copy(a, 4 + j, (*chip, 1 - c), me).wait_recv()
        for cp in first + passed:
            cp.wait_send()
        for cp in mine:
            cp.wait()

    return pl.pallas_call(
        body, name=name,
        in_specs=[_ANY] * n, out_specs=[_ANY] * n,
        out_shape=[jax.ShapeDtypeStruct((N_DEV, *s.shape), s.dtype) for s in shards],
        scratch_shapes=[pltpu.SemaphoreType.DMA((n, 7)), pltpu.SemaphoreType.DMA((n, 7)), pltpu.SemaphoreType.DMA((n,))],
    )(*shards)


def _sibling_exchange(name, grads):
    n = len(grads)

    def body(*refs):
        g_refs, r_refs = refs[:n], refs[n:2 * n]
        send_sems, recv_sems = refs[2 * n:]
        x, y, c = lax.axis_index("x"), lax.axis_index("y"), lax.axis_index("c")
        copies = [
            pltpu.make_async_remote_copy(
                src_ref=g_refs[a].at[k, 1 - c], dst_ref=r_refs[a].at[k], send_sem=send_sems.at[a, k],
                recv_sem=recv_sems.at[a, k], device_id=(x, y, 1 - c), device_id_type=_MESH)
            for a in range(n) for k in range(4)]
        for cp in copies:
            cp.start()
        for cp in copies:
            cp.wait()

    return pl.pallas_call(
        body, name=name, in_specs=[_ANY] * n, out_specs=[_ANY] * n,
        out_shape=[jax.ShapeDtypeStruct((4, *g.shape[2:]), g.dtype) for g in grads],
        scratch_shapes=[pltpu.SemaphoreType.DMA((n, 4)), pltpu.SemaphoreType.DMA((n, 4))],
    )(*grads)


def _chip_exchange(name, parts):
    n = len(parts)

    def body(*refs):
        p_refs, r_refs = refs[:n], refs[n:2 * n]
        send_sems, recv_sems = refs[2 * n:]
        x, y, c = lax.axis_index("x"), lax.axis_index("y"), lax.axis_index("c")
        chips = [(1 - x, y), (x, 1 - y), (1 - x, 1 - y)]
        copies = [
            pltpu.make_async_remote_copy(
                src_ref=p_refs[a].at[2 * px + py], dst_ref=r_refs[a].at[j], send_sem=send_sems.at[a, j],
                recv_sem=recv_sems.at[a, j], device_id=(px, py, c), device_id_type=_MESH)
            for a in range(n) for j, (px, py) in enumerate(chips)]
        for cp in copies:
            cp.start()
        for cp in copies:
            cp.wait()

    return pl.pallas_call(
        body, name=name, in_specs=[_ANY] * n, out_specs=[_ANY] * n,
        out_shape=[jax.ShapeDtypeStruct((3, *p.shape[1:]), p.dtype) for p in parts],
        scratch_shapes=[pltpu.SemaphoreType.DMA((n, 3)), pltpu.SemaphoreType.DMA((n, 3))],
    )(*parts)


_HBM = pl.BlockSpec(memory_space=pltpu.HBM)
_SEM = pl.BlockSpec(memory_space=pltpu.SEMAPHORE)
_EFFECT = pltpu.SideEffectType.DATAFLOW_SIDE_EFFECTING
_FLIPS = [(dx, dy, dc) for dx in (0, 1) for dy in (0, 1) for dc in (0, 1) if (dx, dy, dc) != (0, 0, 0)]


def _exchange_start(name, srcs, lands, per_peer):
    n = len(srcs)

    def body(*refs):
        src_refs, land_refs = refs[:n], refs[n:2 * n]
        send_sems, recv_sems = refs[2 * n:3 * n], refs[3 * n:4 * n]
        token = refs[-1]
        x, y, c = lax.axis_index("x"), lax.axis_index("y"), lax.axis_index("c")
        me = 4 * x + 2 * y + c
        for dx, dy, dc in _FLIPS:
            px, py, pc = (1 - x if dx else x), (1 - y if dy else y), (1 - c if dc else c)
            for a in range(n):
                pltpu.make_async_remote_copy(
                    src_ref=src_refs[a].at[4 * px + 2 * py + pc] if per_peer else src_refs[a],
                    dst_ref=land_refs[a].at[me], send_sem=send_sems[a], recv_sem=recv_sems[a],
                    device_id=(px, py, pc), device_id_type=_MESH).start()
        token[...] = jnp.zeros_like(token)

    hbm = lambda t: pltpu.with_memory_space_constraint(t, pltpu.HBM)
    res = pl.pallas_call(
        body, name=name,
        out_shape=(*[pltpu.SemaphoreType.DMA(())] * (2 * n), *[pltpu.HBM(t.shape, t.dtype) for t in (*srcs, *lands)],
                   jax.ShapeDtypeStruct((8, LANES), F32)),
        in_specs=[_HBM] * (2 * n),
        out_specs=(*[_SEM] * (2 * n), *[_HBM] * (2 * n), pl.BlockSpec(memory_space=pltpu.VMEM)),
        input_output_aliases={i: 2 * n + i for i in range(2 * n)},
        compiler_params=pltpu.CompilerParams(has_side_effects=_EFFECT),
    )(*[hbm(t) for t in (*srcs, *lands)])
    return res[:-1], res[-1]


def _exchange_wait(name, handle, after):
    n = len(handle) // 4
    sems, thru = handle[:2 * n], handle[2 * n:]

    def body(*refs):
        land_refs = refs[n:2 * n]
        send_sems, recv_sems = refs[2 * n:3 * n], refs[3 * n:4 * n]
        me = (lax.axis_index("x"), lax.axis_index("y"), lax.axis_index("c"))
        for a in range(n):
            seven = land_refs[a].at[pl.ds(0, len(_FLIPS))]
            all_copies = pltpu.make_async_remote_copy(
                src_ref=seven, dst_ref=seven, send_sem=send_sems[a], recv_sem=recv_sems[a], device_id=me,
                device_id_type=_MESH)
            all_copies.wait_send()
            all_copies.wait_recv()

    res = pl.pallas_call(
        body, name=name, out_shape=tuple(pltpu.HBM(t.shape, t.dtype) for t in thru),
        in_specs=[*[_HBM] * (2 * n), *[_SEM] * (2 * n), _ANY], out_specs=tuple([_HBM] * (2 * n)),
        input_output_aliases={i: i for i in range(2 * n)},
        compiler_params=pltpu.CompilerParams(has_side_effects=_EFFECT),
    )(*thru, *sems, after)
    return res[n:]


def _adamw_gathered(name, own, parts, me, w, m, v):
    r, c = w.shape
    tr = min(r, 256)

    def body(me_ref, own_ref, p_ref, w_ref, m_ref, v_ref, g_out, d_out, m_out, v_out):
        g = own_ref[...]
        for j in range(N_DEV):
            g = g + p_ref[j].astype(F32)
        delta, m_new, v_new = _adamw(w_ref[...], g, m_ref[...], v_ref[...])
        g_out[...] = g
        d_out[...] = delta
        m_out[...] = m_new
        v_out[...] = v_new

    spec = pl.BlockSpec((tr, c), lambda i, me_ref: (i, 0))
    return pl.pallas_call(
        body, name=name,
        grid_spec=pltpu.PrefetchScalarGridSpec(
            num_scalar_prefetch=1, grid=(r // tr,),
            in_specs=[pl.BlockSpec((None, tr, c), lambda i, me_ref: (me_ref[0], i, 0)),
                      pl.BlockSpec((N_DEV, tr, c), lambda i, me_ref: (0, i, 0)), spec, spec, spec],
            out_specs=[spec] * 4),
        out_shape=[jax.ShapeDtypeStruct((r, c), F32)] * 4,
        compiler_params=_params(("parallel",)),
    )(me, own, parts, w, m, v)


def _chip_partial(name, g4, r1, core):
    _, _, r, c = g4.shape
    tr = min(r, 256)

    def body(core_ref, g_ref, r_ref, o32_ref, o16_ref):
        s = g_ref[...] + r_ref[...]
        o32_ref[...] = s
        o16_ref[...] = s.astype(o16_ref.dtype)

    spec = pl.BlockSpec((None, tr, c), lambda k, i, core_ref: (k, i, 0))
    return pl.pallas_call(
        body, name=name,
        grid_spec=pltpu.PrefetchScalarGridSpec(
            num_scalar_prefetch=1, grid=(4, r // tr),
            in_specs=[pl.BlockSpec((None, None, tr, c), lambda k, i, core_ref: (k, core_ref[0], i, 0)), spec],
            out_specs=[spec, spec]),
        out_shape=[jax.ShapeDtypeStruct((4, r, c), F32), jax.ShapeDtypeStruct((4, r, c), BF16)],
        compiler_params=_params(("parallel", "parallel")),
    )(core, g4, r1)


def _adamw(w, g, m, v):
    m = ADAM_B1 * m + (1.0 - ADAM_B1) * g
    v = ADAM_B2 * v + (1.0 - ADAM_B2) * jnp.square(g)
    m_hat = m / (1.0 - ADAM_B1 ** ADAM_STEP)
    v_hat = v / (1.0 - ADAM_B2 ** ADAM_STEP)
    delta = -ADAM_LR * (m_hat / (jnp.sqrt(v_hat) + ADAM_EPS) + ADAM_WD * w)
    return delta, m, v


def _adamw_shard(name, p32, r2, chip, w, m, v):
    r, c = w.shape
    tr = min(r, 256)

    def body(chip_ref, p_ref, r_ref, w_ref, m_ref, v_ref, g_out, d_out, m_out, v_out):
        g = p_ref[...]
        for j in range(3):
            g = g + r_ref[j].astype(F32)
        delta, m_new, v_new = _adamw(w_ref[...], g, m_ref[...], v_ref[...])
        g_out[...] = g
        d_out[...] = delta
        m_out[...] = m_new
        v_out[...] = v_new

    spec = pl.BlockSpec((tr, c), lambda i, chip_ref: (i, 0))
    return pl.pallas_call(
        body, name=name,
        grid_spec=pltpu.PrefetchScalarGridSpec(
            num_scalar_prefetch=1, grid=(r // tr,),
            in_specs=[pl.BlockSpec((None, tr, c), lambda i, chip_ref: (chip_ref[0], i, 0)),
                      pl.BlockSpec((3, tr, c), lambda i, chip_ref: (0, i, 0)), spec, spec, spec],
            out_specs=[spec] * 4),
        out_shape=[jax.ShapeDtypeStruct((r, c), F32)] * 4,
        compiler_params=_params(("parallel",)),
    )(chip, p32, r2, w, m, v)


def _adamw_small(name, parts, w, m, v):
    _, r, c = parts.shape
    tr = 8

    def body(p_ref, w_ref, m_ref, v_ref, g_out, d_out, m_out, v_out):
        g = p_ref[0]
        for j in range(1, N_DEV):
            g = g + p_ref[j]
        delta, m_new, v_new = _adamw(w_ref[...], g, m_ref[...], v_ref[...])
        g_out[...] = g
        d_out[...] = delta
        m_out[...] = m_new
        v_out[...] = v_new

    spec = pl.BlockSpec((tr, c), lambda i: (i, 0))
    return pl.pallas_call(
        body, name=name, grid=(r // tr,),
        in_specs=[pl.BlockSpec((N_DEV, tr, c), lambda i: (0, i, 0)), spec, spec, spec],
        out_specs=[spec] * 4, out_shape=[jax.ShapeDtypeStruct((r, c), F32)] * 4,
        compiler_params=_params(("parallel",)),
    )(parts, w, m, v)


_WEIGHTS = ["norm1_g", "w_in", "q_norm_g", "k_norm_g", "ssm_lambda_re", "ssm_lambda_im", "ssm_log_dt", "ssm_b_re",
            "ssm_b_im", "ssm_c_re", "ssm_c_im", "ssm_d", "w_glu", "b_glu", "attn_out_g", "ssm_out_g", "w_out",
            "norm2_g", "w_mlp_in", "w_mlp_out"]
_BIG = ["w_in", "w_glu", "w_out", "w_mlp_in", "w_mlp_out"]
_SMALL = [n for n in _WEIGHTS if n not in _BIG]
_PACK_COLS = 1024


def _pack(tree, last=None):
    flat = [tree[n].reshape(-1).astype(F32) for n in _SMALL]
    size = sum(f.shape[0] for f in flat)
    rows = -(-(size + 1) // (_PACK_COLS * 8)) * 8
    pad = jnp.zeros((rows * _PACK_COLS - size - 1,), F32)
    tail = jnp.zeros((1,), F32) if last is None else last.reshape(1).astype(F32)
    return jnp.concatenate(flat + [pad, tail]).reshape(rows, _PACK_COLS)


def _unpack(buf, like):
    flat, out, off = buf.reshape(-1), {}, 0
    for n in _SMALL:
        size = like[n].size
        out[n] = flat[off:off + size].reshape(like[n].shape)
        off += size
    return out


def kernel(x, norm1_g, w_in, q_norm_g, k_norm_g, ssm_lambda_re, ssm_lambda_im, ssm_log_dt, ssm_b_re, ssm_b_im, ssm_c_re, ssm_c_im, ssm_d, w_glu, b_glu, attn_out_g, ssm_out_g, w_out, norm2_g, w_mlp_in, w_mlp_out, loss_target, m_norm1_g, m_w_in, m_q_norm_g, m_k_norm_g, m_ssm_lambda_re, m_ssm_lambda_im, m_ssm_log_dt, m_ssm_b_re, m_ssm_b_im, m_ssm_c_re, m_ssm_c_im, m_ssm_d, m_w_glu, m_b_glu, m_attn_out_g, m_ssm_out_g, m_w_out, m_norm2_g, m_w_mlp_in, m_w_mlp_out, v_norm1_g, v_w_in, v_q_norm_g, v_k_norm_g, v_ssm_lambda_re, v_ssm_lambda_im, v_ssm_log_dt, v_ssm_b_re, v_ssm_b_im, v_ssm_c_re, v_ssm_c_im, v_ssm_d, v_w_glu, v_b_glu, v_attn_out_g, v_ssm_out_g, v_w_out, v_norm2_g, v_w_mlp_in, v_w_mlp_out):
    w = dict(zip(_WEIGHTS, (norm1_g, w_in, q_norm_g, k_norm_g, ssm_lambda_re, ssm_lambda_im, ssm_log_dt, ssm_b_re, ssm_b_im, ssm_c_re, ssm_c_im, ssm_d, w_glu, b_glu, attn_out_g, ssm_out_g, w_out, norm2_g, w_mlp_in, w_mlp_out)))
    m = dict(zip(_WEIGHTS, (m_norm1_g, m_w_in, m_q_norm_g, m_k_norm_g, m_ssm_lambda_re, m_ssm_lambda_im, m_ssm_log_dt, m_ssm_b_re, m_ssm_b_im, m_ssm_c_re, m_ssm_c_im, m_ssm_d, m_w_glu, m_b_glu, m_attn_out_g, m_ssm_out_g, m_w_out, m_norm2_g, m_w_mlp_in, m_w_mlp_out)))
    v = dict(zip(_WEIGHTS, (v_norm1_g, v_w_in, v_q_norm_g, v_k_norm_g, v_ssm_lambda_re, v_ssm_lambda_im, v_ssm_log_dt, v_ssm_b_re, v_ssm_b_im, v_ssm_c_re, v_ssm_c_im, v_ssm_d, v_w_glu, v_b_glu, v_attn_out_g, v_ssm_out_g, v_w_out, v_norm2_g, v_w_mlp_in, v_w_mlp_out)))
    core = lax.axis_index("c").astype(jnp.int32).reshape(1)
    chip = (2 * lax.axis_index("x") + lax.axis_index("y")).astype(jnp.int32).reshape(1)

    me = (2 * chip + core).astype(jnp.int32)

    def landing(own=None, like=None):
        zone = jnp.zeros((N_DEV, *like.shape), like.dtype)
        return zone if own is None else lax.dynamic_update_slice(zone, own[None], (me[0], 0, 0))

    (w_in_blocks,) = _all_gather("w_in_all_gather", [w_in.astype(BF16)])
    w_in_full = w_in_blocks.transpose(1, 0, 2).reshape(w_in.shape[0], -1)
    late = [n for n in _BIG if n != "w_in"]
    shards = [w[n].astype(BF16) for n in late]
    weights_handle, weights_token = _exchange_start(
        "weights_send", shards, [landing(s, s) for s in shards], per_peer=False)

    def late_weights(after):
        got = dict(zip(late, _exchange_wait("weights_arrive", weights_handle, after)))
        return (got["w_glu"].reshape(-1, w_glu.shape[1]), got["w_out"].reshape(-1, w_out.shape[1]),
                got["w_mlp_in"].transpose(1, 0, 2).reshape(w_mlp_in.shape[0], -1),
                got["w_mlp_out"].reshape(-1, w_mlp_out.shape[1]))

    mlp = ["w_mlp_out", "w_mlp_in"]
    sent = {}

    def mlp_grads_ready(g_out, g_out_b, g_in, g_in_b):
        sent["own"] = [g_out.reshape(N_DEV, *w_mlp_out.shape), g_in]
        srcs = [g_out_b.reshape(N_DEV, *w_mlp_out.shape), g_in_b]
        sent["handle"], token = _exchange_start(
            "mlp_grads_send", srcs, [landing(like=s[0]) for s in srcs], per_peer=True)
        return token[0, 0]

    loss_local, grad_x, g_small, g_big = _local_step(
        x, loss_target, {n: w[n] for n in _SMALL}, w_in_full, late_weights, mlp_grads_ready, weights_token[0, 0])

    grads, delta, new_m, new_v = {}, {}, {}, {}
    mlp_parts = _exchange_wait("mlp_grads_arrive", sent["handle"], grad_x)
    for n, own, part in zip(mlp, sent["own"], mlp_parts):
        grads[n], delta[n], new_m[n], new_v[n] = _adamw_gathered("adamw_" + n, own, part, me, w[n], m[n], v[n])

    rest = [n for n in _BIG if n not in mlp]
    g4 = [g_big[n].reshape(4, 2, *w[n].shape) for n in rest]
    r1 = _sibling_exchange("grads_sibling_exchange", g4)
    parts = [_chip_partial("chip_partial_" + n, a, b, core) for n, a, b in zip(rest, g4, r1)]
    r2 = _chip_exchange("grads_chip_exchange", [p16 for _, p16 in parts])
    for n, (p32, _), r in zip(rest, parts, r2):
        grads[n], delta[n], new_m[n], new_v[n] = _adamw_shard("adamw_" + n, p32, r, chip, w[n], m[n], v[n])

    (small_parts,) = _all_gather("small_grads_all_gather", [_pack(g_small, last=loss_local)])
    packed = _adamw_small("adamw_small", small_parts, _pack(w), _pack(m), _pack(v))
    for tree, buf in zip((grads, delta, new_m, new_v), packed):
        tree.update(_unpack(buf, w))
    loss = packed[0][-1, -1]

    return (loss, grad_x, *[grads[n] for n in _WEIGHTS], *[delta[n] for n in _WEIGHTS],
            *[new_m[n] for n in _WEIGHTS], *[new_v[n] for n in _WEIGHTS])
```

```python
import functools
import math

import jax
import jax.numpy as jnp
from jax import lax
from jax.experimental import pallas as pl
from jax.experimental.pallas import tpu as pltpu

F32 = jnp.float32
BF16 = jnp.bfloat16

EPS = 1e-6
HEAD_DIM = 64
N_HEADS = 8
SB_WIDTH = 512
SSM_WIDTH = 512
SSM_GROUP = 16
SSM_GROUPS = 32
SSM_STATE = 64
QBLOCK = 128
KBLOCK = 256
N_CHUNK = 8
SSM_COLS = 4
LANES = 128
N_DEV = 8

ADAM_LR = 0.001
ADAM_B1 = 0.9
ADAM_B2 = 0.999
ADAM_EPS = 1e-08
ADAM_WD = 0.01
ADAM_STEP = 10

VMEM_LIMIT = 56 * 1024 * 1024

_NT = (((1,), (1,)), ((), ()))
_NN = (((1,), (0,)), ((), ()))
_TN = (((0,), (0,)), ((), ()))


def _dot(a, b, dims=_NN):
    return lax.dot_general(a, b, dims, preferred_element_type=F32)


def _params(sem):
    return pltpu.CompilerParams(dimension_semantics=sem, vmem_limit_bytes=VMEM_LIMIT)


def _matmul(name, a, b, *, ta=False, tb=False, extras=(), epilogue=None, out_dtypes=(F32,),
            col_blocked=False, tm=1024, tn=512, tk=1024):
    M, K = (a.shape[1], a.shape[0]) if ta else a.shape
    N = b.shape[0] if tb else b.shape[1]
    tm, tn, tk = min(tm, M), min(tn, N), min(tk, K)
    assert M % tm == 0 and N % tn == 0 and K % tk == 0, (name, M, N, K)
    nk = K // tk
    n_ex, n_out = len(extras), len(out_dtypes)
    dims = (((0 if ta else 1,), (1 if tb else 0,)), ((), ()))

    def body(*refs):
        a_ref, b_ref = refs[0], refs[1]
        ex_refs = refs[2:2 + n_ex]
        o_refs = refs[2 + n_ex:2 + n_ex + n_out]
        k = pl.program_id(2)
        part = _dot(a_ref[...].astype(BF16), b_ref[...].astype(BF16), dims)

        def finish(acc):
            outs = (acc,) if epilogue is None else epilogue(acc, *[e[...] for e in ex_refs])
            for o_ref, o in zip(o_refs, outs):
                o_ref[...] = o.astype(o_ref.dtype)

        if nk == 1:
            finish(part)
        else:
            acc_ref = refs[-1]

            @pl.when(k == 0)
            def _():
                acc_ref[...] = part

            @pl.when(jnp.logical_and(k > 0, k < nk - 1))
            def _():
                acc_ref[...] += part

            @pl.when(k == nk - 1)
            def _():
                finish(acc_ref[...] + part)

    a_spec = pl.BlockSpec((tk, tm), lambda i, j, k: (k, i)) if ta else pl.BlockSpec((tm, tk), lambda i, j, k: (i, k))
    b_spec = pl.BlockSpec((tn, tk), lambda i, j, k: (j, k)) if tb else pl.BlockSpec((tk, tn), lambda i, j, k: (k, j))
    ex_specs = [pl.BlockSpec((tm, tn), lambda i, j, k: (i, j)) for _ in extras]
    if col_blocked:
        out_specs = [pl.BlockSpec((None, tm, tn), lambda i, j, k: (j, i, 0)) for _ in out_dtypes]
        out_shape = [jax.ShapeDtypeStruct((N // tn, M, tn), dt) for dt in out_dtypes]
    else:
        out_specs = [pl.BlockSpec((tm, tn), lambda i, j, k: (i, j)) for _ in out_dtypes]
        out_shape = [jax.ShapeDtypeStruct((M, N), dt) for dt in out_dtypes]
    outs = pl.pallas_call(
        body, name=name, grid=(M // tm, N // tn, nk),
        in_specs=[a_spec, b_spec, *ex_specs], out_specs=out_specs, out_shape=out_shape,
        scratch_shapes=[pltpu.VMEM((tm, tn), F32)] if nk > 1 else [],
        compiler_params=_params(("parallel", "parallel", "arbitrary")),
    )(a, b, *extras)
    return outs[0] if n_out == 1 else outs


def _rowwise(name, fn, rows, small, outs, sums=(), tile=256):
    specs, args = [], []
    T = None
    for r in rows:
        arr, cb, w = r if isinstance(r, tuple) else (r, 0, r.shape[1])
        T = arr.shape[0]
        specs.append((w, cb))
        args.append(arr)
    tile = min(tile, T)
    assert T % tile == 0
    n_r, n_s, n_o, n_a = len(rows), len(small), len(outs), len(sums)

    def body(*refs):
        r_refs = refs[:n_r]
        s_refs = refs[n_r:n_r + n_s]
        o_refs = refs[n_r + n_s:n_r + n_s + n_o]
        a_refs = refs[n_r + n_s + n_o:]
        res = fn(*[r[...] for r in r_refs], *[s[...] for s in s_refs])
        res = res if isinstance(res, (tuple, list)) else (res,)
        for o_ref, o in zip(o_refs, res[:n_o]):
            o_ref[...] = o.astype(o_ref.dtype)

        @pl.when(pl.program_id(0) == 0)
        def _():
            for a_ref in a_refs:
                a_ref[...] = jnp.zeros_like(a_ref)

        for a_ref, v in zip(a_refs, res[n_o:]):
            a_ref[...] += v.astype(F32)

    in_specs = [pl.BlockSpec((tile, w), functools.partial(lambda i, cb: (i, cb), cb=cb)) for w, cb in specs]
    in_specs += [pl.BlockSpec(s.shape, functools.partial(lambda i, nd: (0,) * nd, nd=s.ndim)) for s in small]
    out_specs = [pl.BlockSpec((tile, w), lambda i: (i, 0)) for w, _ in outs]
    out_specs += [pl.BlockSpec(s, functools.partial(lambda i, nd: (0,) * nd, nd=len(s))) for s in sums]
    out_shape = [jax.ShapeDtypeStruct((T, w), dt) for w, dt in outs]
    out_shape += [jax.ShapeDtypeStruct(s, F32) for s in sums]
    res = pl.pallas_call(
        body, name=name, grid=(T // tile,), in_specs=in_specs, out_specs=out_specs, out_shape=out_shape,
        compiler_params=_params(("arbitrary",)),
    )(*args, *small)
    return res[0] if len(res) == 1 else res


def _rms(x, g):
    return x * lax.rsqrt(jnp.mean(x * x, axis=-1, keepdims=True) + EPS) * g


def _glu_branch(y, pre, b_glu, g_out):
    g = jax.nn.gelu(y)
    return _rms(g * jax.nn.sigmoid(pre + b_glu), g_out)


def _split_dot(x, tri_bf):
    hi = x.astype(BF16)
    lo = (x - hi.astype(F32)).astype(BF16)
    return _dot(hi, tri_bf) + _dot(lo, tri_bf)


def _softplus(z):
    return jnp.maximum(z, 0.0) + jnp.log(1.0 + jnp.exp(-jnp.abs(z)))


def _head(h):
    return slice(h * HEAD_DIM, (h + 1) * HEAD_DIM)


def _head_mean(x, seg):
    return _split_dot(x, seg) * (1.0 / HEAD_DIM)


def _qk_norm(proj, gq, gk):
    T = proj.shape[0]
    scale = 1.0 / math.sqrt(HEAD_DIM)
    idx = jnp.arange(SB_WIDTH) // HEAD_DIM
    seg = (idx[:, None] == idx[None, :]).astype(BF16)

    def fn(q, k, v, gq_, gk_, seg_):
        qn = q * lax.rsqrt(_head_mean(q * q, seg_) + EPS) * (gq_ * scale)
        kn = k * lax.rsqrt(_head_mean(k * k, seg_) + EPS) * gk_
        return qn, kn, v

    return _rowwise("qk_norm", fn, [(proj, 0, SB_WIDTH), (proj, 1, SB_WIDTH), (proj, 2, SB_WIDTH)], [gq, gk, seg],
                    [(SB_WIDTH, BF16)] * 3)


def _qk_norm_bwd(proj, gq, gk, d_qn, d_kn):
    scale = 1.0 / math.sqrt(HEAD_DIM)
    idx = jnp.arange(SB_WIDTH) // HEAD_DIM
    seg = (idx[:, None] == idx[None, :]).astype(BF16)

    def one(x, g, dy, seg_):
        r = lax.rsqrt(_head_mean(x * x, seg_) + EPS)
        gdy = g * dy
        dx = r * gdy - x * (r * r * r) * _head_mean(gdy * x, seg_)
        dg = jnp.sum(dy * x * r, axis=0, keepdims=True)
        return dx, sum(dg[:, _head(h)] for h in range(N_HEADS))

    def fn(q, k, dqn, dkn, gq_, gk_, seg_):
        dq, dgq = one(q, gq_, dqn * scale, seg_)
        dk, dgk = one(k, gk_, dkn, seg_)
        return dq, dk, dgq, dgk

    return _rowwise("qk_norm_bwd", fn, [(proj, 0, SB_WIDTH), (proj, 1, SB_WIDTH), d_qn, d_kn], [gq, gk, seg],
                    [(SB_WIDTH, BF16)] * 2, sums=[(1, HEAD_DIM)] * 2)


def _split_heads(refs, scratch, L):
    def chunk(i, _):
        r = pl.ds(pl.multiple_of(i * QBLOCK, QBLOCK), QBLOCK)
        for ref, s in zip(refs, scratch):
            for h in range(2):
                s[h, r, :] = ref[r, _head(h)]
        return 0

    lax.fori_loop(0, L // QBLOCK, chunk, 0)


Q_HALVES = KBLOCK // QBLOCK
_CHAINS = [(h, r) for h in range(2) for r in range(Q_HALVES)]


def _valid(i, kb):
    row = lax.broadcasted_iota(jnp.int32, (QBLOCK, KBLOCK), 0)
    col = lax.broadcasted_iota(jnp.int32, (QBLOCK, KBLOCK), 1)
    return col + (kb * KBLOCK - i * QBLOCK) < row


def _attn_fwd(qn, kn, vb, B, L):
    n_pairs = L // KBLOCK
    n_hp = N_HEADS // 2

    def body(q_ref, k_ref, v_ref, o_ref, rt_ref, q_s, k_s, v_s, after_s):
        _split_heads((q_ref, k_ref, v_ref), (q_s, k_s, v_s), L)
        r2 = lax.broadcasted_iota(jnp.int32, (KBLOCK, KBLOCK), 0)
        c2 = lax.broadcasted_iota(jnp.int32, (KBLOCK, KBLOCK), 1)
        after_s[...] = (r2 > c2).astype(after_s.dtype)

        def q_pair(p, _):
            rows = [pl.ds(pl.multiple_of((p * Q_HALVES + r) * QBLOCK, QBLOCK), QBLOCK) for r in range(Q_HALVES)]
            q_c = [q_s[h, rows[r], :] for h, r in _CHAINS]
            cs = range(len(_CHAINS))

            def k_block(n, carry):
                kb = p - n
                rk = pl.ds(pl.multiple_of(kb * KBLOCK, KBLOCK), KBLOCK)
                valid = [_valid(p * Q_HALVES + r, kb) for r in range(Q_HALVES)]
                z = [_dot(q_c[c], k_s[_CHAINS[c][0], rk, :], _NT) for c in cs]
                sp = [_softplus(z[c]) for c in cs]
                lom = [jnp.where(valid[_CHAINS[c][1]], -sp[c], 0.0) for c in cs]
                tail = [_split_dot(lom[c], after_s[...]) + carry[c][0] for c in cs]
                a = [jnp.where(valid[_CHAINS[c][1]], jnp.exp(z[c] - sp[c] + tail[c]), 0.0) for c in cs]
                acc = [carry[c][1] + _dot(a[c].astype(v_s.dtype), v_s[_CHAINS[c][0], rk, :]) for c in cs]
                return tuple((carry[c][0] + jnp.sum(lom[c], axis=1, keepdims=True), acc[c]) for c in cs)

            init = (jnp.zeros((QBLOCK, 1), F32), jnp.zeros((QBLOCK, HEAD_DIM), F32))
            res = lax.fori_loop(0, p + 1, k_block, (init,) * len(_CHAINS))
            for r in range(Q_HALVES):
                mine = [res[c] for c in cs if _CHAINS[c][1] == r]
                o_ref[rows[r], :] = jnp.concatenate([m[1] for m in mine], axis=1)
                rt_ref[rows[r], :] = jnp.concatenate(
                    [jnp.broadcast_to(m[0], (QBLOCK, HEAD_DIM)) for m in mine], axis=1)
            return 0

        lax.fori_loop(0, n_pairs, q_pair, 0)

    spec = pl.BlockSpec((L, LANES), lambda b, p: (b, p))
    return pl.pallas_call(
        body, name="attn_fwd", grid=(B, n_hp),
        in_specs=[spec] * 3, out_specs=[spec, spec],
        out_shape=[jax.ShapeDtypeStruct((B * L, SB_WIDTH), F32)] * 2,
        scratch_shapes=[pltpu.VMEM((2, L, HEAD_DIM), BF16)] * 3 + [pltpu.VMEM((KBLOCK, KBLOCK), BF16)],
        compiler_params=_params(("parallel", "parallel")),
    )(qn, kn, vb)


def _attn_bwd(qn, kn, vb, rtot, d_sb, B, L):
    n_pairs = L // KBLOCK
    n_hp = N_HEADS // 2

    def body(q_ref, k_ref, v_ref, rt_ref, do_ref, dq_ref, dk_ref, dv_ref,
             q_s, k_s, v_s, qt_s, dkt_s, dvt_s, after_s, before_s):
        _split_heads((q_ref, k_ref, v_ref), (q_s, k_s, v_s), L)

        def transpose_q(i, _):
            r = pl.ds(pl.multiple_of(i * QBLOCK, QBLOCK), QBLOCK)
            qt_s[:, r] = q_ref[r, :].astype(F32).T.astype(qt_s.dtype)
            return 0

        lax.fori_loop(0, L // QBLOCK, transpose_q, 0)
        dkt_s[...] = jnp.zeros_like(dkt_s)
        dvt_s[...] = jnp.zeros_like(dvt_s)
        r2 = lax.broadcasted_iota(jnp.int32, (KBLOCK, KBLOCK), 0)
        c2 = lax.broadcasted_iota(jnp.int32, (KBLOCK, KBLOCK), 1)
        after_s[...] = (r2 > c2).astype(after_s.dtype)
        before_s[...] = (r2 < c2).astype(before_s.dtype)

        def q_pair(p, _):
            rows = [pl.ds(pl.multiple_of((p * Q_HALVES + r) * QBLOCK, QBLOCK), QBLOCK) for r in range(Q_HALVES)]
            pair = pl.ds(pl.multiple_of(p * KBLOCK, KBLOCK), KBLOCK)
            do2 = do_ref[pair, :]
            do_t = do2.T.astype(v_s.dtype)
            tot2 = rt_ref[pair, :]
            cs = range(len(_CHAINS))
            hs = range(2)
            q_c = [q_s[h, rows[r], :] for h, r in _CHAINS]
            do_c = [do2[r * QBLOCK:(r + 1) * QBLOCK, _head(h)].astype(v_s.dtype) for h, r in _CHAINS]
            total = [tot2[r * QBLOCK:(r + 1) * QBLOCK, h * HEAD_DIM:h * HEAD_DIM + 1] for h, r in _CHAINS]
            qt_h = [qt_s[_head(h), pair] for h in hs]
            dot_h = [do_t[_head(h), :] for h in hs]

            def k_block(kb, carry):
                rk = pl.ds(pl.multiple_of(kb * KBLOCK, KBLOCK), KBLOCK)
                valid = [_valid(p * Q_HALVES + r, kb) for r in range(Q_HALVES)]
                ok = [valid[_CHAINS[c][1]] for c in cs]
                k_b = [k_s[h, rk, :] for h in hs]
                z = [_dot(q_c[c], k_b[_CHAINS[c][0]], _NT) for c in cs]
                da = [_dot(do_c[c], v_s[_CHAINS[c][0], rk, :], _NT) for c in cs]
                sp = [_softplus(z[c]) for c in cs]
                lom = [jnp.where(ok[c], -sp[c], 0.0) for c in cs]
                lom_sum = [jnp.sum(lom[c], axis=1, keepdims=True) for c in cs]
                tail = [_split_dot(lom[c], after_s[...]) + (total[c] - carry[c][0] - lom_sum[c]) for c in cs]
                a = [jnp.where(ok[c], jnp.exp(z[c] - sp[c] + tail[c]), 0.0) for c in cs]
                dla = [a[c] * da[c] for c in cs]
                for h in hs:
                    a_h = jnp.concatenate([a[c].astype(v_s.dtype) for c in cs if _CHAINS[c][0] == h], axis=0)
                    dvt_s[_head(h), rk] += _dot(dot_h[h], a_h)
                d_lom = [carry[c][1] + _split_dot(dla[c], before_s[...]) for c in cs]
                beta = [jnp.exp(z[c] - sp[c]) for c in cs]
                dz_b = [(dla[c] * (1.0 - beta[c]) - jnp.where(ok[c], beta[c] * d_lom[c], 0.0)).astype(v_s.dtype)
                        for c in cs]
                dq_acc = [carry[c][2] + _dot(dz_b[c], k_b[_CHAINS[c][0]]) for c in cs]
                for h in hs:
                    dz_h = jnp.concatenate([dz_b[c] for c in cs if _CHAINS[c][0] == h], axis=0)
                    dkt_s[_head(h), rk] += _dot(qt_h[h], dz_h)
                return tuple((carry[c][0] + lom_sum[c], carry[c][1] + jnp.sum(dla[c], axis=1, keepdims=True),
                              dq_acc[c]) for c in cs)

            zero = jnp.zeros((QBLOCK, 1), F32)
            init = (zero, zero, jnp.zeros((QBLOCK, HEAD_DIM), F32))
            res = lax.fori_loop(0, p + 1, k_block, (init,) * len(_CHAINS))
            for r in range(Q_HALVES):
                dq_ref[rows[r], :] = jnp.concatenate([res[c][2] for c in cs if _CHAINS[c][1] == r], axis=1)
            return 0

        lax.fori_loop(0, n_pairs, q_pair, 0)

        def transpose_out(i, _):
            r = pl.ds(pl.multiple_of(i * QBLOCK, QBLOCK), QBLOCK)
            dk_ref[r, :] = dkt_s[:, r].T
            dv_ref[r, :] = dvt_s[:, r].T.astype(dv_ref.dtype)
            return 0

        lax.fori_loop(0, L // QBLOCK, transpose_out, 0)

    spec = pl.BlockSpec((L, LANES), lambda b, p: (b, p))
    return pl.pallas_call(
        body, name="attn_bwd", grid=(B, n_hp),
        in_specs=[spec] * 5, out_specs=[spec] * 3,
        out_shape=[jax.ShapeDtypeStruct((B * L, SB_WIDTH), F32)] * 2 + [jax.ShapeDtypeStruct((B * L, SB_WIDTH), BF16)],
        scratch_shapes=[pltpu.VMEM((2, L, HEAD_DIM), BF16)] * 3 + [pltpu.VMEM((LANES, L), BF16)]
        + [pltpu.VMEM((LANES, L), F32)] * 2 + [pltpu.VMEM((KBLOCK, KBLOCK), BF16)] * 2,
        compiler_params=_params(("parallel", "parallel")),
    )(qn, kn, vb, rtot, d_sb)


def _ssm_discretise(lam_re, lam_im, log_dt, b_re, b_im):
    dt = jnp.exp(log_dt)
    mag = jnp.exp(lam_re * dt)
    lbr = mag * jnp.cos(lam_im * dt)
    lbi = mag * jnp.sin(lam_im * dt)
    den = lam_re * lam_re + lam_im * lam_im
    nr, ni = lbr - 1.0, lbi
    cr = (nr * lam_re + ni * lam_im) / den
    ci = (ni * lam_re - nr * lam_im) / den
    return lbr, lbi, cr * b_re - ci * b_im, cr * b_im + ci * b_re


def _ssm_prep(lam_re, lam_im, log_dt, b_re_t, b_im_t):
    def body(lr, li, ld, br, bi, o_lr, o_li, o_br, o_bi):
        res = _ssm_discretise(lr[...], li[...], ld[...], br[...], bi[...])
        for o, v in zip((o_lr, o_li, o_br, o_bi), res):
            o[...] = v

    return pl.pallas_call(
        body, name="ssm_prep",
        out_shape=[jax.ShapeDtypeStruct(lam_re.shape, F32)] * 2 + [jax.ShapeDtypeStruct(b_re_t.shape, F32)] * 2,
    )(lam_re, lam_im, log_dt, b_re_t, b_im_t)


def _ssm_prep_bwd(lam_re, lam_im, log_dt, b_re_t, b_im_t, d_lr, d_li, d_br, d_bi):
    def body(lr, li, ld, br, bi, g_lr, g_li, g_br, g_bi, o_lr, o_li, o_ld, o_br, o_bi):
        _, vjp = jax.vjp(_ssm_discretise, lr[...], li[...], ld[...], br[...], bi[...])
        res = vjp((g_lr[...], g_li[...], g_br[...], g_bi[...]))
        for o, v in zip((o_lr, o_li, o_ld, o_br, o_bi), res):
            o[...] = v

    return pl.pallas_call(
        body, name="ssm_prep_bwd",
        out_shape=[jax.ShapeDtypeStruct(lam_re.shape, F32)] * 2 + [jax.ShapeDtypeStruct(log_dt.shape, F32)]
        + [jax.ShapeDtypeStruct(b_re_t.shape, F32)] * 2,
    )(lam_re, lam_im, log_dt, b_re_t, b_im_t, d_lr, d_li, d_br, d_bi)


def _block_diag(m):
    m4 = m.reshape(SSM_COLS, 8, SSM_GROUP, SSM_STATE)
    return jnp.einsum("aghp,gk->aghkp", m4, jnp.eye(8, dtype=m.dtype)).reshape(SSM_COLS, LANES, 512)


def _block_diag_take(d):
    d6 = d.reshape(SSM_COLS, 8, SSM_GROUP, 2, 8, SSM_STATE)
    return jnp.einsum("aghrgp->raghp", d6).reshape(2, SSM_GROUPS, SSM_GROUP, SSM_STATE)


def _cmul(ar, ai, br, bi):
    return ar * br - ai * bi, ar * bi + ai * br


def _power(lr, li, n):
    assert n & (n - 1) == 0
    for _ in range(n.bit_length() - 1):
        lr, li = _cmul(lr, li, lr, li)
    return lr, li


def _ssm_fwd(u_p, w_b, lam_r, lam_i, c_m, d_skip, B, L, tj):
    J = L // N_CHUNK
    njt = J // tj
    R = tj * N_CHUNK
    H = 512

    def body(u_ref, wb_ref, lr_ref, li_ref, cm_ref, d_ref, y_ref, x_ref, xin_ref, bu_s, st_s, xin_s):
        ph, jt = pl.program_id(2), pl.program_id(3)
        lr, li = lr_ref[...], li_ref[...]

        @pl.when(jnp.logical_and(ph == 0, jt == 0))
        def _():
            st_s[...] = jnp.zeros_like(st_s)

        bu_s[...] = _dot(u_ref[...].astype(BF16), wb_ref[...].astype(BF16))

        def scan(store):
            def step(j, carry):
                xr, xi = carry
                r = pl.ds(pl.multiple_of(j * N_CHUNK, N_CHUNK), N_CHUNK)
                nr = lr * xr - li * xi + bu_s[r, 0:H]
                ni = lr * xi + li * xr + bu_s[r, H:2 * H]
                if store:
                    x_ref[r, 0:H] = nr
                    x_ref[r, H:2 * H] = ni
                return nr, ni

            xr, xi = lax.fori_loop(0, tj, step, (st_s[:, 0:H], st_s[:, H:2 * H]))
            st_s[:, 0:H] = xr
            st_s[:, H:2 * H] = xi

        @pl.when(ph == 0)
        def _():
            scan(False)

            @pl.when(jt == njt - 1)
            def _():
                pr, pi = _power(lr[0:1], li[0:1], J)
                xin_s[0:1, :] = jnp.zeros((1, 2 * H), F32)
                for c in range(1, N_CHUNK):
                    qr, qi = _cmul(pr, pi, xin_s[c - 1:c, 0:H], xin_s[c - 1:c, H:2 * H])
                    xin_s[c:c + 1, 0:H] = qr + st_s[c - 1:c, 0:H]
                    xin_s[c:c + 1, H:2 * H] = qi + st_s[c - 1:c, H:2 * H]
                xin_ref[...] = xin_s[...]
                st_s[...] = xin_s[...]

        @pl.when(ph == 1)
        def _():
            scan(True)
            y = _dot(x_ref[...].astype(BF16), cm_ref[...].astype(BF16))
            y_ref[...] = y + d_ref[...] * u_ref[...]

    return pl.pallas_call(
        body, name="ssm_fwd", grid=(SSM_COLS, B, 2, njt),
        in_specs=[
            pl.BlockSpec((None, R, LANES), lambda i, b, ph, jt: (b, jt, i)),
            pl.BlockSpec((None, LANES, 2 * H), lambda i, b, ph, jt: (i, 0, 0)),
            pl.BlockSpec((None, N_CHUNK, H), lambda i, b, ph, jt: (i, 0, 0)),
            pl.BlockSpec((None, N_CHUNK, H), lambda i, b, ph, jt: (i, 0, 0)),
            pl.BlockSpec((None, 2 * H, LANES), lambda i, b, ph, jt: (i, 0, 0)),
            pl.BlockSpec((1, LANES), lambda i, b, ph, jt: (0, i)),
        ],
        out_specs=[
            pl.BlockSpec((None, R, LANES), lambda i, b, ph, jt: (b, jt * ph, i)),
            pl.BlockSpec((None, R, 2 * H), lambda i, b, ph, jt: (b, jt * ph, i)),
            pl.BlockSpec((None, None, N_CHUNK, 2 * H), lambda i, b, ph, jt: (b, i, 0, 0)),
        ],
        out_shape=[
            jax.ShapeDtypeStruct((B, L, SSM_WIDTH), F32),
            jax.ShapeDtypeStruct((B, L, SSM_COLS * 2 * H), F32),
            jax.ShapeDtypeStruct((B, SSM_COLS, N_CHUNK, 2 * H), F32),
        ],
        scratch_shapes=[pltpu.VMEM((R, 2 * H), F32), pltpu.VMEM((N_CHUNK, 2 * H), F32), pltpu.VMEM((N_CHUNK, 2 * H), F32)],
        compiler_params=_params(("arbitrary",) * 4),
    )(u_p, w_b, lam_r, lam_i, c_m, d_skip)


def _ssm_bwd(dy_p, u_p, x, xin, w_bt, lam_r, lam_i, c_mt, d_skip, B, L, tj):
    J = L // N_CHUNK
    njt = J // tj
    R = tj * N_CHUNK
    H = 512
    x4 = x.reshape(B, J, N_CHUNK, SSM_COLS * 2 * H)

    def body(dy_ref, u_ref, x_ref, xp_ref, xin_ref, wbt_ref, lr_ref, li_ref, cmt_ref, d_ref,
             du_ref, dwb_ref, dcm_ref, dlr_ref, dli_ref, dd_ref, ca_s, a_s, st_s, dl_s):
        b, ph, jt = pl.program_id(1), pl.program_id(2), pl.program_id(3)
        jr = njt - 1 - jt
        lr, li = lr_ref[...], -li_ref[...]

        @pl.when(jnp.logical_and(b == 0, jnp.logical_and(ph == 0, jt == 0)))
        def _():
            dwb_ref[...] = jnp.zeros_like(dwb_ref)
            dcm_ref[...] = jnp.zeros_like(dcm_ref)
            dlr_ref[...] = jnp.zeros_like(dlr_ref)
            dli_ref[...] = jnp.zeros_like(dli_ref)
            dd_ref[...] = jnp.zeros_like(dd_ref)
            dl_s[...] = jnp.zeros_like(dl_s)

        @pl.when(jnp.logical_and(ph == 0, jt == 0))
        def _():
            st_s[...] = jnp.zeros_like(st_s)

        ca_s[...] = _dot(dy_ref[...].astype(BF16), cmt_ref[...].astype(BF16))

        def scan(store):
            def step(n, carry):
                ar, ai = carry
                r = pl.ds(pl.multiple_of((tj - 1 - n) * N_CHUNK, N_CHUNK), N_CHUNK)
                nr = lr * ar - li * ai + ca_s[r, 0:H]
                ni = lr * ai + li * ar + ca_s[r, H:2 * H]
                if store:
                    a_s[r, 0:H] = nr
                    a_s[r, H:2 * H] = ni
                return nr, ni

            ar, ai = lax.fori_loop(0, tj, step, (st_s[:, 0:H], st_s[:, H:2 * H]))
            st_s[:, 0:H] = ar
            st_s[:, H:2 * H] = ai

        @pl.when(ph == 0)
        def _():
            scan(False)

            @pl.when(jt == njt - 1)
            def _():
                pr, pi = _power(lr[0:1], li[0:1], J)
                a_s[N_CHUNK - 1:N_CHUNK, :] = jnp.zeros((1, 2 * H), F32)
                for c in range(N_CHUNK - 2, -1, -1):
                    qr, qi = _cmul(pr, pi, a_s[c + 1:c + 2, 0:H], a_s[c + 1:c + 2, H:2 * H])
                    a_s[c:c + 1, 0:H] = qr + st_s[c + 1:c + 2, 0:H]
                    a_s[c:c + 1, H:2 * H] = qi + st_s[c + 1:c + 2, H:2 * H]
                st_s[...] = a_s[0:N_CHUNK, :]

        @pl.when(ph == 1)
        def _():
            scan(True)
            dy = dy_ref[...]
            u = u_ref[...]
            a_b = a_s[...].astype(BF16)
            du_ref[...] = (_dot(a_b, wbt_ref[...].astype(BF16)) + d_ref[...] * dy).astype(du_ref.dtype)
            dwb_ref[...] += _dot(u.astype(BF16), a_b, _TN)
            dcm_ref[...] += _dot(x_ref[...].astype(BF16), dy.astype(BF16), _TN)
            dd_ref[...] += jnp.sum(dy * u, axis=0, keepdims=True)

            first = jnp.where(jr == 0, xin_ref[...], xp_ref[...])
            a0r, a0i = a_s[0:N_CHUNK, 0:H], a_s[0:N_CHUNK, H:2 * H]
            acc0 = (a0r * first[:, 0:H] + a0i * first[:, H:2 * H], a0i * first[:, 0:H] - a0r * first[:, H:2 * H])

            def step(j, carry):
                sr, si = carry
                r = pl.ds(pl.multiple_of(j * N_CHUNK, N_CHUNK), N_CHUNK)
                rp = pl.ds(pl.multiple_of((j - 1) * N_CHUNK, N_CHUNK), N_CHUNK)
                ar, ai = a_s[r, 0:H], a_s[r, H:2 * H]
                xr, xi = x_ref[rp, 0:H], x_ref[rp, H:2 * H]
                return sr + ar * xr + ai * xi, si + ai * xr - ar * xi

            sr, si = lax.fori_loop(1, tj, step, acc0)
            dl_s[:, 0:H] += sr
            dl_s[:, H:2 * H] += si

            @pl.when(jnp.logical_and(b == B - 1, jt == njt - 1))
            def _():
                dlr_ref[...] = jnp.sum(dl_s[:, 0:H], axis=0, keepdims=True)
                dli_ref[...] = jnp.sum(dl_s[:, H:2 * H], axis=0, keepdims=True)
                dl_s[...] = jnp.zeros_like(dl_s)

    rev = lambda ph, jt: (njt - 1 - jt) * ph + (njt - 1) * (1 - ph)
    return pl.pallas_call(
        body, name="ssm_bwd", grid=(SSM_COLS, B, 2, njt),
        in_specs=[
            pl.BlockSpec((None, R, LANES), lambda i, b, ph, jt: (b, njt - 1 - jt, i)),
            pl.BlockSpec((None, R, LANES), lambda i, b, ph, jt: (b, njt - 1 - jt, i)),
            pl.BlockSpec((None, R, 2 * H), lambda i, b, ph, jt: (b, rev(ph, jt), i)),
            pl.BlockSpec((None, None, N_CHUNK, 2 * H),
                         lambda i, b, ph, jt: (b, jnp.maximum((njt - 1 - jt) * tj - 1, 0), 0, i)),
            pl.BlockSpec((None, None, N_CHUNK, 2 * H), lambda i, b, ph, jt: (b, i, 0, 0)),
            pl.BlockSpec((None, 2 * H, LANES), lambda i, b, ph, jt: (i, 0, 0)),
            pl.BlockSpec((None, N_CHUNK, H), lambda i, b, ph, jt: (i, 0, 0)),
            pl.BlockSpec((None, N_CHUNK, H), lambda i, b, ph, jt: (i, 0, 0)),
            pl.BlockSpec((None, LANES, 2 * H), lambda i, b, ph, jt: (i, 0, 0)),
            pl.BlockSpec((1, LANES), lambda i, b, ph, jt: (0, i)),
        ],
        out_specs=[
            pl.BlockSpec((None, R, LANES), lambda i, b, ph, jt: (b, rev(ph, jt), i)),
            pl.BlockSpec((None, LANES, 2 * H), lambda i, b, ph, jt: (i, 0, 0)),
            pl.BlockSpec((None, 2 * H, LANES), lambda i, b, ph, jt: (i, 0, 0)),
            pl.BlockSpec((None, 1, H), lambda i, b, ph, jt: (i, 0, 0)),
            pl.BlockSpec((None, 1, H), lambda i, b, ph, jt: (i, 0, 0)),
            pl.BlockSpec((1, LANES), lambda i, b, ph, jt: (0, i)),
        ],
        out_shape=[
            jax.ShapeDtypeStruct((B, L, SSM_WIDTH), BF16),
            jax.ShapeDtypeStruct((SSM_COLS, LANES, 2 * H), F32),
            jax.ShapeDtypeStruct((SSM_COLS, 2 * H, LANES), F32),
            jax.ShapeDtypeStruct((SSM_COLS, 1, H), F32),
            jax.ShapeDtypeStruct((SSM_COLS, 1, H), F32),
            jax.ShapeDtypeStruct((1, SSM_WIDTH), F32),
        ],
        scratch_shapes=[pltpu.VMEM((R, 2 * H), F32), pltpu.VMEM((R, 2 * H), F32),
                        pltpu.VMEM((N_CHUNK, 2 * H), F32), pltpu.VMEM((N_CHUNK, 2 * H), F32)],
        compiler_params=_params(("arbitrary",) * 4),
    )(dy_p, u_p, x, x4, xin, w_bt, lam_r, lam_i, c_mt, d_skip)


def _to_scan_layout(t, B, L):
    C = t.shape[-1]
    return t.reshape(B, N_CHUNK, L // N_CHUNK, C).transpose(0, 2, 1, 3).reshape(B, L, C)


def _from_scan_layout(t, B, L):
    C = t.shape[-1]
    return t.reshape(B, L // N_CHUNK, N_CHUNK, C).transpose(0, 2, 1, 3).reshape(B * L, C)


def _local_step(x, target, p, w_in, late_weights, mlp_grads_ready=None, rest_grads_ready=None, order=None, *,
                ssm_tile=32):
    B, L, D = x.shape
    T = B * L
    x2 = x.reshape(T, D)
    row = lambda v: v.reshape(1, -1)
    g1, g2, ga, gs, b_glu = row(p["norm1_g"]), row(p["norm2_g"]), row(p["attn_out_g"]), row(p["ssm_out_g"]), row(p["b_glu"])
    g1_first = g1 if order is None else g1 + order
    gq8 = jnp.tile(row(p["q_norm_g"]), (1, N_HEADS))
    gk8 = jnp.tile(row(p["k_norm_g"]), (1, N_HEADS))

    G, P, Hh = SSM_GROUPS, SSM_STATE, SSM_GROUP
    lam_re3, lam_im3 = p["ssm_lambda_re"].reshape(G, 1, P), p["ssm_lambda_im"].reshape(G, 1, P)
    log_dt3 = p["ssm_log_dt"].reshape(G, 1, 1)
    b_re_t, b_im_t = p["ssm_b_re"].transpose(0, 2, 1), p["ssm_b_im"].transpose(0, 2, 1)
    lbr, lbi, bbr, bbi = _ssm_prep(lam_re3, lam_im3, log_dt3, b_re_t, b_im_t)
    w_b = jnp.concatenate([_block_diag(bbr), _block_diag(bbi)], axis=2)
    c_mt = jnp.concatenate([_block_diag(p["ssm_c_re"]), -_block_diag(p["ssm_c_im"])], axis=2)
    w_bt, c_m = w_b.transpose(0, 2, 1), c_mt.transpose(0, 2, 1)
    lam_r = jnp.broadcast_to(lbr.reshape(SSM_COLS, 1, 512), (SSM_COLS, N_CHUNK, 512))
    lam_i = jnp.broadcast_to(lbi.reshape(SSM_COLS, 1, 512), (SSM_COLS, N_CHUNK, 512))
    d_skip = p["ssm_d"].reshape(1, SSM_WIDTH)

    xn = _rowwise("norm1", _rms, [x2], [g1_first], [(D, BF16)])
    proj = _matmul("proj", xn, w_in)
    qn, kn, vb = _qk_norm(proj, gq8, gk8)
    sb, rtot = _attn_fwd(qn, kn, vb, B, L)
    u_p = _to_scan_layout(proj[:, 3 * SB_WIDTH:], B, L)
    y_p, xs, xin = _ssm_fwd(u_p, w_b, lam_r, lam_i, c_m, d_skip, B, L, ssm_tile)
    y2 = y_p.reshape(T, SSM_WIDTH)
    gel = _rowwise("gelu", jax.nn.gelu, [y2], [], [(SSM_WIDTH, BF16)])
    w_glu, w_out, w_mlp_in, w_mlp_out = late_weights(gel)
    pre = _matmul("glu_gate", gel, w_glu)
    ssm_n = _rowwise("glu_out", _glu_branch, [y2, pre], [b_glu, gs], [(SSM_WIDTH, BF16)])
    sb_n = _rowwise("attn_out_norm", _rms, [sb], [ga], [(SB_WIDTH, BF16)])
    mixed = jnp.concatenate([sb_n, _from_scan_layout(ssm_n, B, L)], axis=1)
    h1 = _matmul("out_proj", mixed, w_out, extras=[x2], epilogue=lambda acc, r: (acc + r,))
    hn = _rowwise("norm2", _rms, [h1], [g2], [(D, BF16)])
    act, a_pre = _matmul("mlp_in", hn, w_mlp_in, out_dtypes=(BF16, BF16),
                         epilogue=lambda acc: (jnp.square(jnp.maximum(acc, 0.0)), acc))
    out = _matmul("mlp_out", act, w_mlp_out, extras=[h1], epilogue=lambda acc, r: (acc + r,))

    def loss_fn(o, t):
        diff = o - t
        part = jnp.sum(jnp.sum(diff * diff, axis=0, keepdims=True), axis=1, keepdims=True)
        d = diff * (1.0 / D)
        return d, d, part * (0.5 / D)

    d_out, d_out_b, loss = _rowwise("loss", loss_fn, [out, target.reshape(T, D)], [], [(D, F32), (D, BF16)],
                                    sums=[(1, 1)])

    d_apre = _matmul("mlp_out_dx", d_out_b, w_mlp_out, tb=True, extras=[a_pre], out_dtypes=(BF16,),
                     epilogue=lambda acc, ap: (acc * (2.0 * jnp.maximum(ap.astype(F32), 0.0)),))
    both = lambda acc: (acc, acc)
    g_w_mlp_out, g_w_mlp_out_b = _matmul("mlp_out_dw", act, d_out_b, ta=True, out_dtypes=(F32, BF16), epilogue=both)
    g_w_mlp_in, g_w_mlp_in_b = _matmul("mlp_in_dw", hn, d_apre, ta=True, col_blocked=True, out_dtypes=(F32, BF16),
                                       epilogue=both, tn=w_mlp_in.shape[1] // N_DEV)
    if mlp_grads_ready is not None:
        g2 = g2 + mlp_grads_ready(g_w_mlp_out, g_w_mlp_out_b, g_w_mlp_in, g_w_mlp_in_b)
    d_hn = _matmul("mlp_in_dx", d_apre, w_mlp_in, tb=True)

    def norm_bwd_res(h, dy, res, g):
        _, vjp = jax.vjp(_rms, h, g)
        dh, dg = vjp(dy)
        return res + dh, dg

    def norm_bwd_res2(h, dy, res, g):
        d, dg = norm_bwd_res(h, dy, res, g)
        return d, d, dg

    d_h1, d_h1_b, g_norm2 = _rowwise("norm2_bwd", norm_bwd_res2, [h1, d_hn, d_out], [g2], [(D, F32), (D, BF16)],
                                     sums=[(1, D)])

    d_mixed = _matmul("out_proj_dx", d_h1_b, w_out, tb=True)
    g_w_out, g_w_out_b = _matmul("out_proj_dw", mixed, d_h1_b, ta=True, out_dtypes=(F32, BF16), epilogue=both)

    def norm_bwd(h, dy, g):
        _, vjp = jax.vjp(_rms, h, g)
        return vjp(dy)

    d_sb, g_attn_out = _rowwise("attn_out_norm_bwd", norm_bwd, [sb, (d_mixed, 0, SB_WIDTH)], [ga],
                                [(SB_WIDTH, F32)], sums=[(1, SB_WIDTH)])
    d_ssm_n = _to_scan_layout(d_mixed[:, SB_WIDTH:], B, L).reshape(T, SSM_WIDTH)

    def glu_bwd(y, pre_, dy, bg, g):
        _, vjp = jax.vjp(_glu_branch, y, pre_, bg, g)
        d_y, d_pre, d_bg, d_g = vjp(dy)
        return d_y, d_pre, d_bg, d_g

    d_y_direct, d_pre, g_b_glu, g_ssm_out = _rowwise(
        "glu_out_bwd", glu_bwd, [y2, pre, d_ssm_n], [b_glu, gs], [(SSM_WIDTH, F32), (SSM_WIDTH, BF16)],
        sums=[(1, SSM_WIDTH), (1, SSM_WIDTH)])
    d_gel = _matmul("glu_gate_dx", d_pre, w_glu, tb=True)
    g_w_glu, g_w_glu_b = _matmul("glu_gate_dw", gel, d_pre, ta=True, out_dtypes=(F32, BF16), epilogue=both)

    def gelu_bwd(y, dg, dy0):
        _, vjp = jax.vjp(jax.nn.gelu, y)
        return dy0 + vjp(dg)[0]

    d_y = _rowwise("gelu_bwd", gelu_bwd, [y2, d_gel, d_y_direct], [], [(SSM_WIDTH, F32)])

    du_p, d_wb, d_cm, d_lr, d_li, g_d = _ssm_bwd(
        d_y.reshape(B, L, SSM_WIDTH), u_p, xs, xin, w_bt, lam_r, lam_i, c_mt, d_skip, B, L, ssm_tile)
    d_bb = _block_diag_take(d_wb.reshape(SSM_COLS, LANES, 2, 512))
    d_c = _block_diag_take(d_cm.transpose(0, 2, 1).reshape(SSM_COLS, LANES, 2, 512))
    g_lam_re, g_lam_im, g_log_dt, g_b_re_t, g_b_im_t = _ssm_prep_bwd(
        lam_re3, lam_im3, log_dt3, b_re_t, b_im_t,
        d_lr.reshape(G, 1, P), d_li.reshape(G, 1, P), d_bb[0], d_bb[1])
    d_qn, d_kn, d_v = _attn_bwd(qn, kn, vb, rtot, d_sb, B, L)
    d_q, d_k, g_q, g_k = _qk_norm_bwd(proj, gq8, gk8, d_qn, d_kn)

    d_proj = jnp.concatenate([d_q, d_k, d_v, _from_scan_layout(du_p, B, L)], axis=1)
    g_w_in, g_w_in_b = _matmul("proj_dw", xn, d_proj, ta=True, col_blocked=True, out_dtypes=(F32, BF16),
                               epilogue=both, tn=w_in.shape[1] // N_DEV)
    if rest_grads_ready is not None:
        g1 = g1 + rest_grads_ready([g_w_in, g_w_glu, g_w_out], [g_w_in_b, g_w_glu_b, g_w_out_b])
    d_xn = _matmul("proj_dx", d_proj, w_in, tb=True)
    grad_x, g_norm1 = _rowwise("norm1_bwd", norm_bwd_res, [x2, d_xn, d_h1], [g1], [(D, F32)], sums=[(1, D)])

    small = {
        "norm1_g": g_norm1.reshape(-1),
        "q_norm_g": g_q.reshape(-1),
        "k_norm_g": g_k.reshape(-1),
        "ssm_lambda_re": g_lam_re.reshape(G, P),
        "ssm_lambda_im": g_lam_im.reshape(G, P),
        "ssm_log_dt": g_log_dt.reshape(G),
        "ssm_b_re": g_b_re_t.transpose(0, 2, 1),
        "ssm_b_im": g_b_im_t.transpose(0, 2, 1),
        "ssm_c_re": d_c[0],
        "ssm_c_im": -d_c[1],
        "ssm_d": g_d.reshape(G, Hh),
        "b_glu": g_b_glu.reshape(-1),
        "attn_out_g": g_attn_out.reshape(-1),
        "ssm_out_g": g_ssm_out.reshape(-1),
        "norm2_g": g_norm2.reshape(-1),
    }
    big = {"w_in": g_w_in, "w_glu": g_w_glu, "w_out": g_w_out, "w_mlp_in": g_w_mlp_in, "w_mlp_out": g_w_mlp_out}
    return loss[0, 0], grad_x.reshape(B, L, D), small, big


_ANY = pl.BlockSpec(memory_space=pl.ANY)
_MESH = pl.DeviceIdType.MESH


def _all_gather(name, shards):
    n = len(shards)

    def body(*refs):
        in_refs, out_refs = refs[:n], refs[n:2 * n]
        send_sems, recv_sems, local_sems = refs[2 * n:]
        x, y, c = lax.axis_index("x"), lax.axis_index("y"), lax.axis_index("c")
        me, sibling = (x, y, c), (x, y, 1 - c)
        chips = [(1 - x, y), (x, 1 - y), (1 - x, 1 - y)]

        def copy(a, k, block, to, src=None):
            px, py, pc = block
            rows = out_refs[a].at[4 * px + 2 * py + pc]
            return pltpu.make_async_remote_copy(
                src_ref=rows if src is None else src, dst_ref=rows, send_sem=send_sems.at[a, k],
                recv_sem=recv_sems.at[a, k], device_id=to, device_id_type=_MESH)

        mine = [pltpu.make_async_copy(in_refs[a], out_refs[a].at[4 * x + 2 * y + c], local_sems.at[a]) for a in range(n)]
        first, passed = [], []
        for a in range(n):
            mine[a].start()
            first.append(copy(a, 0, me, sibling, src=in_refs[a]))
            first += [copy(a, 1 + j, me, (*chip, c), src=in_refs[a]) for j, chip in enumerate(chips)]
        for cp in first:
            cp.start()
        for j, chip in enumerate(chips):
            for a in range(n):
                copy(a, 1 + j, (*chip, c), me).wait_recv()
                fwd = copy(a, 4 + j, (*chip, c), sibling)
                fwd.start()
                passed.append(fwd)
        for a in range(n):
            copy(a, 0, sibling, me).wait_recv()
            for j, chip in enumerate(chips):
                copy(a, 4 + j, (*chip, 1 - c), me).wait_recv()
        for cp in first + passed:
            cp.wait_send()
        for cp in mine:
            cp.wait()

    return pl.pallas_call(
        body, name=name,
        in_specs=[_ANY] * n, out_specs=[_ANY] * n,
        out_shape=[jax.ShapeDtypeStruct((N_DEV, *s.shape), s.dtype) for s in shards],
        scratch_shapes=[pltpu.SemaphoreType.DMA((n, 7)), pltpu.SemaphoreType.DMA((n, 7)), pltpu.SemaphoreType.DMA((n,))],
    )(*shards)


_HBM = pl.BlockSpec(memory_space=pltpu.HBM)
_SEM = pl.BlockSpec(memory_space=pltpu.SEMAPHORE)
_EFFECT = pltpu.SideEffectType.DATAFLOW_SIDE_EFFECTING
_FLIPS = [(dx, dy, dc) for dx in (0, 1) for dy in (0, 1) for dc in (0, 1) if (dx, dy, dc) != (0, 0, 0)]


def _exchange_start(name, srcs, lands, per_peer):
    n = len(srcs)

    def body(*refs):
        src_refs, land_refs = refs[:n], refs[n:2 * n]
        send_sems, recv_sems = refs[2 * n:3 * n], refs[3 * n:4 * n]
        token = refs[-1]
        x, y, c = lax.axis_index("x"), lax.axis_index("y"), lax.axis_index("c")
        me = 4 * x + 2 * y + c
        for dx, dy, dc in _FLIPS:
            px, py, pc = (1 - x if dx else x), (1 - y if dy else y), (1 - c if dc else c)
            for a in range(n):
                pltpu.make_async_remote_copy(
                    src_ref=src_refs[a].at[4 * px + 2 * py + pc] if per_peer else src_refs[a],
                    dst_ref=land_refs[a].at[me], send_sem=send_sems[a], recv_sem=recv_sems[a],
                    device_id=(px, py, pc), device_id_type=_MESH).start()
        token[...] = jnp.zeros_like(token)

    hbm = lambda t: pltpu.with_memory_space_constraint(t, pltpu.HBM)
    res = pl.pallas_call(
        body, name=name,
        out_shape=(*[pltpu.SemaphoreType.DMA(())] * (2 * n), *[pltpu.HBM(t.shape, t.dtype) for t in (*srcs, *lands)],
                   jax.ShapeDtypeStruct((8, LANES), F32)),
        in_specs=[_HBM] * (2 * n),
        out_specs=(*[_SEM] * (2 * n), *[_HBM] * (2 * n), pl.BlockSpec(memory_space=pltpu.VMEM)),
        input_output_aliases={i: 2 * n + i for i in range(2 * n)},
        compiler_params=pltpu.CompilerParams(has_side_effects=_EFFECT),
    )(*[hbm(t) for t in (*srcs, *lands)])
    return res[:-1], res[-1]


def _exchange_wait(name, handle, after):
    n = len(handle) // 4
    sems, thru = handle[:2 * n], handle[2 * n:]

    def body(*refs):
        land_refs = refs[n:2 * n]
        send_sems, recv_sems = refs[2 * n:3 * n], refs[3 * n:4 * n]
        me = (lax.axis_index("x"), lax.axis_index("y"), lax.axis_index("c"))
        for a in range(n):
            seven = land_refs[a].at[pl.ds(0, len(_FLIPS))]
            all_copies = pltpu.make_async_remote_copy(
                src_ref=seven, dst_ref=seven, send_sem=send_sems[a], recv_sem=recv_sems[a], device_id=me,
                device_id_type=_MESH)
            all_copies.wait_send()
            all_copies.wait_recv()

    res = pl.pallas_call(
        body, name=name, out_shape=tuple(pltpu.HBM(t.shape, t.dtype) for t in thru),
        in_specs=[*[_HBM] * (2 * n), *[_SEM] * (2 * n), _ANY], out_specs=tuple([_HBM] * (2 * n)),
        input_output_aliases={i: i for i in range(2 * n)},
        compiler_params=pltpu.CompilerParams(has_side_effects=_EFFECT),
    )(*thru, *sems, after)
    return res[n:]


def _adamw_gathered(name, own, parts, me, w, m, v):
    r, c = w.shape
    tr = min(r, 256)

    def body(me_ref, own_ref, p_ref, w_ref, m_ref, v_ref, g_out, d_out, m_out, v_out):
        g = own_ref[...]
        for j in range(N_DEV):
            g = g + p_ref[j].astype(F32)
        delta, m_new, v_new = _adamw(w_ref[...], g, m_ref[...], v_ref[...])
        g_out[...] = g
        d_out[...] = delta
        m_out[...] = m_new
        v_out[...] = v_new

    spec = pl.BlockSpec((tr, c), lambda i, me_ref: (i, 0))
    return pl.pallas_call(
        body, name=name,
        grid_spec=pltpu.PrefetchScalarGridSpec(
            num_scalar_prefetch=1, grid=(r // tr,),
            in_specs=[pl.BlockSpec((None, tr, c), lambda i, me_ref: (me_ref[0], i, 0)),
                      pl.BlockSpec((N_DEV, tr, c), lambda i, me_ref: (0, i, 0)), spec, spec, spec],
            out_specs=[spec] * 4),
        out_shape=[jax.ShapeDtypeStruct((r, c), F32)] * 4,
        compiler_params=_params(("parallel",)),
    )(me, own, parts, w, m, v)


def _adamw(w, g, m, v):
    m = ADAM_B1 * m + (1.0 - ADAM_B1) * g
    v = ADAM_B2 * v + (1.0 - ADAM_B2) * jnp.square(g)
    m_hat = m / (1.0 - ADAM_B1 ** ADAM_STEP)
    v_hat = v / (1.0 - ADAM_B2 ** ADAM_STEP)
    delta = -ADAM_LR * (m_hat / (jnp.sqrt(v_hat) + ADAM_EPS) + ADAM_WD * w)
    return delta, m, v


def _adamw_small(name, parts, w, m, v):
    _, r, c = parts.shape
    tr = 8

    def body(p_ref, w_ref, m_ref, v_ref, g_out, d_out, m_out, v_out):
        g = p_ref[0]
        for j in range(1, N_DEV):
            g = g + p_ref[j]
        delta, m_new, v_new = _adamw(w_ref[...], g, m_ref[...], v_ref[...])
        g_out[...] = g
        d_out[...] = delta
        m_out[...] = m_new
        v_out[...] = v_new

    spec = pl.BlockSpec((tr, c), lambda i: (i, 0))
    return pl.pallas_call(
        body, name=name, grid=(r // tr,),
        in_specs=[pl.BlockSpec((N_DEV, tr, c), lambda i: (0, i, 0)), spec, spec, spec],
        out_specs=[spec] * 4, out_shape=[jax.ShapeDtypeStruct((r, c), F32)] * 4,
        compiler_params=_params(("parallel",)),
    )(parts, w, m, v)


_WEIGHTS = ["norm1_g", "w_in", "q_norm_g", "k_norm_g", "ssm_lambda_re", "ssm_lambda_im", "ssm_log_dt", "ssm_b_re",
            "ssm_b_im", "ssm_c_re", "ssm_c_im", "ssm_d", "w_glu", "b_glu", "attn_out_g", "ssm_out_g", "w_out",
            "norm2_g", "w_mlp_in", "w_mlp_out"]
_BIG = ["w_in", "w_glu", "w_out", "w_mlp_in", "w_mlp_out"]
_SMALL = [n for n in _WEIGHTS if n not in _BIG]
_PACK_COLS = 1024


def _pack(tree, last=None):
    flat = [tree[n].reshape(-1).astype(F32) for n in _SMALL]
    size = sum(f.shape[0] for f in flat)
    rows = -(-(size + 1) // (_PACK_COLS * 8)) * 8
    pad = jnp.zeros((rows * _PACK_COLS - size - 1,), F32)
    tail = jnp.zeros((1,), F32) if last is None else last.reshape(1).astype(F32)
    return jnp.concatenate(flat + [pad, tail]).reshape(rows, _PACK_COLS)


def _unpack(buf, like):
    flat, out, off = buf.reshape(-1), {}, 0
    for n in _SMALL:
        size = like[n].size
        out[n] = flat[off:off + size].reshape(like[n].shape)
        off += size
    return out


def kernel(x, norm1_g, w_in, q_norm_g, k_norm_g, ssm_lambda_re, ssm_lambda_im, ssm_log_dt, ssm_b_re, ssm_b_im, ssm_c_re, ssm_c_im, ssm_d, w_glu, b_glu, attn_out_g, ssm_out_g, w_out, norm2_g, w_mlp_in, w_mlp_out, loss_target, m_norm1_g, m_w_in, m_q_norm_g, m_k_norm_g, m_ssm_lambda_re, m_ssm_lambda_im, m_ssm_log_dt, m_ssm_b_re, m_ssm_b_im, m_ssm_c_re, m_ssm_c_im, m_ssm_d, m_w_glu, m_b_glu, m_attn_out_g, m_ssm_out_g, m_w_out, m_norm2_g, m_w_mlp_in, m_w_mlp_out, v_norm1_g, v_w_in, v_q_norm_g, v_k_norm_g, v_ssm_lambda_re, v_ssm_lambda_im, v_ssm_log_dt, v_ssm_b_re, v_ssm_b_im, v_ssm_c_re, v_ssm_c_im, v_ssm_d, v_w_glu, v_b_glu, v_attn_out_g, v_ssm_out_g, v_w_out, v_norm2_g, v_w_mlp_in, v_w_mlp_out):
    w = dict(zip(_WEIGHTS, (norm1_g, w_in, q_norm_g, k_norm_g, ssm_lambda_re, ssm_lambda_im, ssm_log_dt, ssm_b_re, ssm_b_im, ssm_c_re, ssm_c_im, ssm_d, w_glu, b_glu, attn_out_g, ssm_out_g, w_out, norm2_g, w_mlp_in, w_mlp_out)))
    m = dict(zip(_WEIGHTS, (m_norm1_g, m_w_in, m_q_norm_g, m_k_norm_g, m_ssm_lambda_re, m_ssm_lambda_im, m_ssm_log_dt, m_ssm_b_re, m_ssm_b_im, m_ssm_c_re, m_ssm_c_im, m_ssm_d, m_w_glu, m_b_glu, m_attn_out_g, m_ssm_out_g, m_w_out, m_norm2_g, m_w_mlp_in, m_w_mlp_out)))
    v = dict(zip(_WEIGHTS, (v_norm1_g, v_w_in, v_q_norm_g, v_k_norm_g, v_ssm_lambda_re, v_ssm_lambda_im, v_ssm_log_dt, v_ssm_b_re, v_ssm_b_im, v_ssm_c_re, v_ssm_c_im, v_ssm_d, v_w_glu, v_b_glu, v_attn_out_g, v_ssm_out_g, v_w_out, v_norm2_g, v_w_mlp_in, v_w_mlp_out)))
    core = lax.axis_index("c").astype(jnp.int32).reshape(1)
    chip = (2 * lax.axis_index("x") + lax.axis_index("y")).astype(jnp.int32).reshape(1)

    me = (2 * chip + core).astype(jnp.int32)

    def landing(own=None, like=None):
        zone = jnp.zeros((N_DEV, *like.shape), like.dtype)
        return zone if own is None else lax.dynamic_update_slice(zone, own[None], (me[0], 0, 0))

    (w_in_blocks,) = _all_gather("w_in_all_gather", [w_in.astype(BF16)])
    w_in_full = w_in_blocks.transpose(1, 0, 2).reshape(w_in.shape[0], -1)
    late = [n for n in _BIG if n != "w_in"]
    shards = [w[n].astype(BF16) for n in late]
    w_in_blocks, shards = lax.optimization_barrier((w_in_blocks, shards))
    weights_handle, weights_token = _exchange_start(
        "weights_send", shards, [landing(s, s) for s in shards], per_peer=False)

    def late_weights(after):
        got = dict(zip(late, _exchange_wait("weights_arrive", weights_handle, after)))
        return (got["w_glu"].reshape(-1, w_glu.shape[1]), got["w_out"].reshape(-1, w_out.shape[1]),
                got["w_mlp_in"].transpose(1, 0, 2).reshape(w_mlp_in.shape[0], -1),
                got["w_mlp_out"].reshape(-1, w_mlp_out.shape[1]))

    mlp = ["w_mlp_out", "w_mlp_in"]
    sent = {}

    def send_grads(name, names, own, own_b):
        blocks = lambda g, n: g.reshape(N_DEV, *w[n].shape)
        sent[name + "_own"] = [blocks(g, n) for g, n in zip(own, names)]
        srcs = [blocks(g, n) for g, n in zip(own_b, names)]
        sent[name], token = _exchange_start(name, srcs, [landing(like=s[0]) for s in srcs], per_peer=True)
        return token[0, 0]

    def mlp_grads_ready(g_out, g_out_b, g_in, g_in_b):
        return send_grads("mlp_grads_send", mlp, [g_out, g_in], [g_out_b, g_in_b])

    rest = ["w_in", "w_glu", "w_out"]

    def rest_grads_ready(own, own_b):
        return send_grads("rest_grads_send", rest, own, own_b)

    loss_local, grad_x, g_small, g_big = _local_step(
        x, loss_target, {n: w[n] for n in _SMALL}, w_in_full, late_weights, mlp_grads_ready, rest_grads_ready,
        weights_token[0, 0])

    grads, delta, new_m, new_v = {}, {}, {}, {}
    (small_parts,) = _all_gather("small_grads_all_gather", [_pack(g_small, last=loss_local)])
    packed = _adamw_small("adamw_small", small_parts, _pack(w), _pack(m), _pack(v))
    for tree, buf in zip((grads, delta, new_m, new_v), packed):
        tree.update(_unpack(buf, w))
    loss = packed[0][-1, -1]

    for send, arrive, names, after in (("mlp_grads_send", "mlp_grads_arrive", mlp, grad_x),
                                       ("rest_grads_send", "rest_grads_arrive", rest, packed[0])):
        for n, own, part in zip(names, sent[send + "_own"], _exchange_wait(arrive, sent[send], after)):
            grads[n], delta[n], new_m[n], new_v[n] = _adamw_gathered("adamw_" + n, own, part, me, w[n], m[n], v[n])

    return (loss, grad_x, *[grads[n] for n in _WEIGHTS], *[delta[n] for n in _WEIGHTS],
            *[new_m[n] for n in _WEIGHTS], *[new_v[n] for n in _WEIGHTS])
```

```python
import functools
import math

import jax
import jax.numpy as jnp
from jax import lax
from jax.experimental import pallas as pl
from jax.experimental.pallas import tpu as pltpu

F32 = jnp.float32
BF16 = jnp.bfloat16

EPS = 1e-6
HEAD_DIM = 64
N_HEADS = 8
SB_WIDTH = 512
SSM_WIDTH = 512
SSM_GROUP = 16
SSM_GROUPS = 32
SSM_STATE = 64
QBLOCK = 128
KBLOCK = 256
N_CHUNK = 8
SSM_COLS = 4
LANES = 128
N_DEV = 8

ADAM_LR = 0.001
ADAM_B1 = 0.9
ADAM_B2 = 0.999
ADAM_EPS = 1e-08
ADAM_WD = 0.01
ADAM_STEP = 10

VMEM_LIMIT = 56 * 1024 * 1024

_NT = (((1,), (1,)), ((), ()))
_NN = (((1,), (0,)), ((), ()))
_TN = (((0,), (0,)), ((), ()))


def _dot(a, b, dims=_NN):
    return lax.dot_general(a, b, dims, preferred_element_type=F32)


def _params(sem):
    return pltpu.CompilerParams(dimension_semantics=sem, vmem_limit_bytes=VMEM_LIMIT)


def _matmul(name, a, b, *, ta=False, tb=False, extras=(), epilogue=None, out_dtypes=(F32,),
            col_blocked=False, tm=1024, tn=512, tk=1024):
    M, K = (a.shape[1], a.shape[0]) if ta else a.shape
    N = b.shape[0] if tb else b.shape[1]
    tm, tn, tk = min(tm, M), min(tn, N), min(tk, K)
    assert M % tm == 0 and N % tn == 0 and K % tk == 0, (name, M, N, K)
    nk = K // tk
    n_ex, n_out = len(extras), len(out_dtypes)
    dims = (((0 if ta else 1,), (1 if tb else 0,)), ((), ()))

    def body(*refs):
        a_ref, b_ref = refs[0], refs[1]
        ex_refs = refs[2:2 + n_ex]
        o_refs = refs[2 + n_ex:2 + n_ex + n_out]
        k = pl.program_id(2)
        part = _dot(a_ref[...].astype(BF16), b_ref[...].astype(BF16), dims)

        def finish(acc):
            outs = (acc,) if epilogue is None else epilogue(acc, *[e[...] for e in ex_refs])
            for o_ref, o in zip(o_refs, outs):
                o_ref[...] = o.astype(o_ref.dtype)

        if nk == 1:
            finish(part)
        else:
            acc_ref = refs[-1]

            @pl.when(k == 0)
            def _():
                acc_ref[...] = part

            @pl.when(jnp.logical_and(k > 0, k < nk - 1))
            def _():
                acc_ref[...] += part

            @pl.when(k == nk - 1)
            def _():
                finish(acc_ref[...] + part)

    a_spec = pl.BlockSpec((tk, tm), lambda i, j, k: (k, i)) if ta else pl.BlockSpec((tm, tk), lambda i, j, k: (i, k))
    b_spec = pl.BlockSpec((tn, tk), lambda i, j, k: (j, k)) if tb else pl.BlockSpec((tk, tn), lambda i, j, k: (k, j))
    ex_specs = [pl.BlockSpec((tm, tn), lambda i, j, k: (i, j)) for _ in extras]
    if col_blocked:
        out_specs = [pl.BlockSpec((None, tm, tn), lambda i, j, k: (j, i, 0)) for _ in out_dtypes]
        out_shape = [jax.ShapeDtypeStruct((N // tn, M, tn), dt) for dt in out_dtypes]
    else:
        out_specs = [pl.BlockSpec((tm, tn), lambda i, j, k: (i, j)) for _ in out_dtypes]
        out_shape = [jax.ShapeDtypeStruct((M, N), dt) for dt in out_dtypes]
    outs = pl.pallas_call(
        body, name=name, grid=(M // tm, N // tn, nk),
        in_specs=[a_spec, b_spec, *ex_specs], out_specs=out_specs, out_shape=out_shape,
        scratch_shapes=[pltpu.VMEM((tm, tn), F32)] if nk > 1 else [],
        compiler_params=_params(("parallel", "parallel", "arbitrary")),
    )(a, b, *extras)
    return outs[0] if n_out == 1 else outs


def _rowwise(name, fn, rows, small, outs, sums=(), tile=256):
    specs, args = [], []
    T = None
    for r in rows:
        arr, cb, w = r if isinstance(r, tuple) else (r, 0, r.shape[1])
        T = arr.shape[0]
        specs.append((w, cb))
        args.append(arr)
    tile = min(tile, T)
    assert T % tile == 0
    n_r, n_s, n_o, n_a = len(rows), len(small), len(outs), len(sums)

    def body(*refs):
        r_refs = refs[:n_r]
        s_refs = refs[n_r:n_r + n_s]
        o_refs = refs[n_r + n_s:n_r + n_s + n_o]
        a_refs = refs[n_r + n_s + n_o:]
        res = fn(*[r[...] for r in r_refs], *[s[...] for s in s_refs])
        res = res if isinstance(res, (tuple, list)) else (res,)
        for o_ref, o in zip(o_refs, res[:n_o]):
            o_ref[...] = o.astype(o_ref.dtype)

        @pl.when(pl.program_id(0) == 0)
        def _():
            for a_ref in a_refs:
                a_ref[...] = jnp.zeros_like(a_ref)

        for a_ref, v in zip(a_refs, res[n_o:]):
            a_ref[...] += v.astype(F32)

    in_specs = [pl.BlockSpec((tile, w), functools.partial(lambda i, cb: (i, cb), cb=cb)) for w, cb in specs]
    in_specs += [pl.BlockSpec(s.shape, functools.partial(lambda i, nd: (0,) * nd, nd=s.ndim)) for s in small]
    out_specs = [pl.BlockSpec((tile, w), lambda i: (i, 0)) for w, _ in outs]
    out_specs += [pl.BlockSpec(s, functools.partial(lambda i, nd: (0,) * nd, nd=len(s))) for s in sums]
    out_shape = [jax.ShapeDtypeStruct((T, w), dt) for w, dt in outs]
    out_shape += [jax.ShapeDtypeStruct(s, F32) for s in sums]
    res = pl.pallas_call(
        body, name=name, grid=(T // tile,), in_specs=in_specs, out_specs=out_specs, out_shape=out_shape,
        compiler_params=_params(("arbitrary",)),
    )(*args, *small)
    return res[0] if len(res) == 1 else res


def _rms(x, g):
    return x * lax.rsqrt(jnp.mean(x * x, axis=-1, keepdims=True) + EPS) * g


def _glu_branch(y, pre, b_glu, g_out):
    g = jax.nn.gelu(y)
    return _rms(g * jax.nn.sigmoid(pre + b_glu), g_out)


def _split_dot(x, tri_bf):
    hi = x.astype(BF16)
    lo = (x - hi.astype(F32)).astype(BF16)
    return _dot(hi, tri_bf) + _dot(lo, tri_bf)


def _softplus(z):
    return jnp.maximum(z, 0.0) + jnp.log(1.0 + jnp.exp(-jnp.abs(z)))


def _head(h):
    return slice(h * HEAD_DIM, (h + 1) * HEAD_DIM)


def _head_mean(x, seg):
    return _split_dot(x, seg) * (1.0 / HEAD_DIM)


def _qk_norm(proj, gq, gk):
    T = proj.shape[0]
    scale = 1.0 / math.sqrt(HEAD_DIM)
    idx = jnp.arange(SB_WIDTH) // HEAD_DIM
    seg = (idx[:, None] == idx[None, :]).astype(BF16)

    def fn(q, k, v, gq_, gk_, seg_):
        qn = q * lax.rsqrt(_head_mean(q * q, seg_) + EPS) * (gq_ * scale)
        kn = k * lax.rsqrt(_head_mean(k * k, seg_) + EPS) * gk_
        return qn, kn, v

    return _rowwise("qk_norm", fn, [(proj, 0, SB_WIDTH), (proj, 1, SB_WIDTH), (proj, 2, SB_WIDTH)], [gq, gk, seg],
                    [(SB_WIDTH, BF16)] * 3)


def _qk_norm_bwd(proj, gq, gk, d_qn, d_kn):
    scale = 1.0 / math.sqrt(HEAD_DIM)
    idx = jnp.arange(SB_WIDTH) // HEAD_DIM
    seg = (idx[:, None] == idx[None, :]).astype(BF16)

    def one(x, g, dy, seg_):
        r = lax.rsqrt(_head_mean(x * x, seg_) + EPS)
        gdy = g * dy
        dx = r * gdy - x * (r * r * r) * _head_mean(gdy * x, seg_)
        dg = jnp.sum(dy * x * r, axis=0, keepdims=True)
        return dx, sum(dg[:, _head(h)] for h in range(N_HEADS))

    def fn(q, k, dqn, dkn, gq_, gk_, seg_):
        dq, dgq = one(q, gq_, dqn * scale, seg_)
        dk, dgk = one(k, gk_, dkn, seg_)
        return dq, dk, dgq, dgk

    return _rowwise("qk_norm_bwd", fn, [(proj, 0, SB_WIDTH), (proj, 1, SB_WIDTH), d_qn, d_kn], [gq, gk, seg],
                    [(SB_WIDTH, BF16)] * 2, sums=[(1, HEAD_DIM)] * 2)


def _split_heads(refs, scratch, L):
    def chunk(i, _):
        r = pl.ds(pl.multiple_of(i * QBLOCK, QBLOCK), QBLOCK)
        for ref, s in zip(refs, scratch):
            for h in range(2):
                s[h, r, :] = ref[r, _head(h)]
        return 0

    lax.fori_loop(0, L // QBLOCK, chunk, 0)


Q_HALVES = KBLOCK // QBLOCK
_CHAINS = [(h, r) for h in range(2) for r in range(Q_HALVES)]


def _valid(i, kb):
    row = lax.broadcasted_iota(jnp.int32, (QBLOCK, KBLOCK), 0)
    col = lax.broadcasted_iota(jnp.int32, (QBLOCK, KBLOCK), 1)
    return col + (kb * KBLOCK - i * QBLOCK) < row


def _attn_fwd(qn, kn, vb, B, L):
    n_pairs = L // KBLOCK
    n_hp = N_HEADS // 2

    def body(q_ref, k_ref, v_ref, o_ref, rt_ref, q_s, k_s, v_s, after_s):
        _split_heads((q_ref, k_ref, v_ref), (q_s, k_s, v_s), L)
        r2 = lax.broadcasted_iota(jnp.int32, (KBLOCK, KBLOCK), 0)
        c2 = lax.broadcasted_iota(jnp.int32, (KBLOCK, KBLOCK), 1)
        after_s[...] = (r2 > c2).astype(after_s.dtype)

        def q_pair(p, _):
            rows = [pl.ds(pl.multiple_of((p * Q_HALVES + r) * QBLOCK, QBLOCK), QBLOCK) for r in range(Q_HALVES)]
            q_c = [q_s[h, rows[r], :] for h, r in _CHAINS]
            cs = range(len(_CHAINS))

            def k_block(kb, carry, diagonal):
                rk = pl.ds(pl.multiple_of(kb * KBLOCK, KBLOCK), KBLOCK)
                if diagonal:
                    valid = [_valid(p * Q_HALVES + r, kb) for r in range(Q_HALVES)]
                    keep = lambda c, t: jnp.where(valid[_CHAINS[c][1]], t, 0.0)
                else:
                    keep = lambda c, t: t
                z = [_dot(q_c[c], k_s[_CHAINS[c][0], rk, :], _NT) for c in cs]
                sp = [_softplus(z[c]) for c in cs]
                lom = [keep(c, -sp[c]) for c in cs]
                tail = [_split_dot(lom[c], after_s[...]) + carry[c][0] for c in cs]
                a = [keep(c, jnp.exp(z[c] - sp[c] + tail[c])) for c in cs]
                acc = [carry[c][1] + _dot(a[c].astype(v_s.dtype), v_s[_CHAINS[c][0], rk, :]) for c in cs]
                return tuple((carry[c][0] + jnp.sum(lom[c], axis=1, keepdims=True), acc[c]) for c in cs)

            init = (jnp.zeros((QBLOCK, 1), F32), jnp.zeros((QBLOCK, HEAD_DIM), F32))
            first = k_block(p, (init,) * len(_CHAINS), True)
            res = lax.fori_loop(1, p + 1, lambda n, carry: k_block(p - n, carry, False), first)
            for r in range(Q_HALVES):
                mine = [res[c] for c in cs if _CHAINS[c][1] == r]
                o_ref[rows[r], :] = jnp.concatenate([m[1] for m in mine], axis=1)
                rt_ref[rows[r], :] = jnp.concatenate(
                    [jnp.broadcast_to(m[0], (QBLOCK, HEAD_DIM)) for m in mine], axis=1)
            return 0

        lax.fori_loop(0, n_pairs, q_pair, 0)

    spec = pl.BlockSpec((L, LANES), lambda b, p: (b, p))
    return pl.pallas_call(
        body, name="attn_fwd", grid=(B, n_hp),
        in_specs=[spec] * 3, out_specs=[spec, spec],
        out_shape=[jax.ShapeDtypeStruct((B * L, SB_WIDTH), F32)] * 2,
        scratch_shapes=[pltpu.VMEM((2, L, HEAD_DIM), BF16)] * 3 + [pltpu.VMEM((KBLOCK, KBLOCK), BF16)],
        compiler_params=_params(("parallel", "parallel")),
    )(qn, kn, vb)


def _attn_bwd(qn, kn, vb, rtot, d_sb, B, L):
    n_pairs = L // KBLOCK
    n_hp = N_HEADS // 2

    def body(q_ref, k_ref, v_ref, rt_ref, do_ref, dq_ref, dk_ref, dv_ref,
             q_s, k_s, v_s, qt_s, dkt_s, dvt_s, after_s, before_s):
        _split_heads((q_ref, k_ref, v_ref), (q_s, k_s, v_s), L)

        def transpose_q(i, _):
            r = pl.ds(pl.multiple_of(i * QBLOCK, QBLOCK), QBLOCK)
            qt_s[:, r] = q_ref[r, :].astype(F32).T.astype(qt_s.dtype)
            return 0

        lax.fori_loop(0, L // QBLOCK, transpose_q, 0)
        dkt_s[...] = jnp.zeros_like(dkt_s)
        dvt_s[...] = jnp.zeros_like(dvt_s)
        r2 = lax.broadcasted_iota(jnp.int32, (KBLOCK, KBLOCK), 0)
        c2 = lax.broadcasted_iota(jnp.int32, (KBLOCK, KBLOCK), 1)
        after_s[...] = (r2 > c2).astype(after_s.dtype)
        before_s[...] = (r2 < c2).astype(before_s.dtype)

        def q_pair(p, _):
            rows = [pl.ds(pl.multiple_of((p * Q_HALVES + r) * QBLOCK, QBLOCK), QBLOCK) for r in range(Q_HALVES)]
            pair = pl.ds(pl.multiple_of(p * KBLOCK, KBLOCK), KBLOCK)
            do2 = do_ref[pair, :]
            do_t = do2.T.astype(v_s.dtype)
            tot2 = rt_ref[pair, :]
            cs = range(len(_CHAINS))
            hs = range(2)
            q_c = [q_s[h, rows[r], :] for h, r in _CHAINS]
            do_c = [do2[r * QBLOCK:(r + 1) * QBLOCK, _head(h)].astype(v_s.dtype) for h, r in _CHAINS]
            total = [tot2[r * QBLOCK:(r + 1) * QBLOCK, h * HEAD_DIM:h * HEAD_DIM + 1] for h, r in _CHAINS]
            qt_h = [qt_s[_head(h), pair] for h in hs]
            dot_h = [do_t[_head(h), :] for h in hs]

            def k_block(kb, carry, diagonal):
                rk = pl.ds(pl.multiple_of(kb * KBLOCK, KBLOCK), KBLOCK)
                if diagonal:
                    valid = [_valid(p * Q_HALVES + r, kb) for r in range(Q_HALVES)]
                    keep = lambda c, t: jnp.where(valid[_CHAINS[c][1]], t, 0.0)
                else:
                    keep = lambda c, t: t
                k_b = [k_s[h, rk, :] for h in hs]
                z = [_dot(q_c[c], k_b[_CHAINS[c][0]], _NT) for c in cs]
                da = [_dot(do_c[c], v_s[_CHAINS[c][0], rk, :], _NT) for c in cs]
                sp = [_softplus(z[c]) for c in cs]
                lom = [keep(c, -sp[c]) for c in cs]
                lom_sum = [jnp.sum(lom[c], axis=1, keepdims=True) for c in cs]
                tail = [_split_dot(lom[c], after_s[...]) + (total[c] - carry[c][0] - lom_sum[c]) for c in cs]
                a = [keep(c, jnp.exp(z[c] - sp[c] + tail[c])) for c in cs]
                dla = [a[c] * da[c] for c in cs]
                for h in hs:
                    a_h = jnp.concatenate([a[c].astype(v_s.dtype) for c in cs if _CHAINS[c][0] == h], axis=0)
                    dvt_s[_head(h), rk] += _dot(dot_h[h], a_h)
                d_lom = [carry[c][1] + _split_dot(dla[c], before_s[...]) for c in cs]
                beta = [jnp.exp(z[c] - sp[c]) for c in cs]
                dz_b = [(dla[c] * (1.0 - beta[c]) - keep(c, beta[c] * d_lom[c])).astype(v_s.dtype) for c in cs]
                dq_acc = [carry[c][2] + _dot(dz_b[c], k_b[_CHAINS[c][0]]) for c in cs]
                for h in hs:
                    dz_h = jnp.concatenate([dz_b[c] for c in cs if _CHAINS[c][0] == h], axis=0)
                    dkt_s[_head(h), rk] += _dot(qt_h[h], dz_h)
                return tuple((carry[c][0] + lom_sum[c], carry[c][1] + jnp.sum(dla[c], axis=1, keepdims=True),
                              dq_acc[c]) for c in cs)

            zero = jnp.zeros((QBLOCK, 1), F32)
            init = (zero, zero, jnp.zeros((QBLOCK, HEAD_DIM), F32))
            before = lax.fori_loop(0, p, lambda kb, carry: k_block(kb, carry, False), (init,) * len(_CHAINS))
            res = k_block(p, before, True)
            for r in range(Q_HALVES):
                dq_ref[rows[r], :] = jnp.concatenate([res[c][2] for c in cs if _CHAINS[c][1] == r], axis=1)
            return 0

        lax.fori_loop(0, n_pairs, q_pair, 0)

        def transpose_out(i, _):
            r = pl.ds(pl.multiple_of(i * QBLOCK, QBLOCK), QBLOCK)
            dk_ref[r, :] = dkt_s[:, r].T
            dv_ref[r, :] = dvt_s[:, r].T.astype(dv_ref.dtype)
            return 0

        lax.fori_loop(0, L // QBLOCK, transpose_out, 0)

    spec = pl.BlockSpec((L, LANES), lambda b, p: (b, p))
    return pl.pallas_call(
        body, name="attn_bwd", grid=(B, n_hp),
        in_specs=[spec] * 5, out_specs=[spec] * 3,
        out_shape=[jax.ShapeDtypeStruct((B * L, SB_WIDTH), F32)] * 2 + [jax.ShapeDtypeStruct((B * L, SB_WIDTH), BF16)],
        scratch_shapes=[pltpu.VMEM((2, L, HEAD_DIM), BF16)] * 3 + [pltpu.VMEM((LANES, L), BF16)]
        + [pltpu.VMEM((LANES, L), F32)] * 2 + [pltpu.VMEM((KBLOCK, KBLOCK), BF16)] * 2,
        compiler_params=_params(("parallel", "parallel")),
    )(qn, kn, vb, rtot, d_sb)


def _ssm_discretise(lam_re, lam_im, log_dt, b_re, b_im):
    dt = jnp.exp(log_dt)
    mag = jnp.exp(lam_re * dt)
    lbr = mag * jnp.cos(lam_im * dt)
    lbi = mag * jnp.sin(lam_im * dt)
    den = lam_re * lam_re + lam_im * lam_im
    nr, ni = lbr - 1.0, lbi
    cr = (nr * lam_re + ni * lam_im) / den
    ci = (ni * lam_re - nr * lam_im) / den
    return lbr, lbi, cr * b_re - ci * b_im, cr * b_im + ci * b_re


def _ssm_prep(lam_re, lam_im, log_dt, b_re_t, b_im_t):
    def body(lr, li, ld, br, bi, o_lr, o_li, o_br, o_bi):
        res = _ssm_discretise(lr[...], li[...], ld[...], br[...], bi[...])
        for o, v in zip((o_lr, o_li, o_br, o_bi), res):
            o[...] = v

    return pl.pallas_call(
        body, name="ssm_prep",
        out_shape=[jax.ShapeDtypeStruct(lam_re.shape, F32)] * 2 + [jax.ShapeDtypeStruct(b_re_t.shape, F32)] * 2,
    )(lam_re, lam_im, log_dt, b_re_t, b_im_t)


def _ssm_prep_bwd(lam_re, lam_im, log_dt, b_re_t, b_im_t, d_lr, d_li, d_br, d_bi):
    def body(lr, li, ld, br, bi, g_lr, g_li, g_br, g_bi, o_lr, o_li, o_ld, o_br, o_bi):
        _, vjp = jax.vjp(_ssm_discretise, lr[...], li[...], ld[...], br[...], bi[...])
        res = vjp((g_lr[...], g_li[...], g_br[...], g_bi[...]))
        for o, v in zip((o_lr, o_li, o_ld, o_br, o_bi), res):
            o[...] = v

    return pl.pallas_call(
        body, name="ssm_prep_bwd",
        out_shape=[jax.ShapeDtypeStruct(lam_re.shape, F32)] * 2 + [jax.ShapeDtypeStruct(log_dt.shape, F32)]
        + [jax.ShapeDtypeStruct(b_re_t.shape, F32)] * 2,
    )(lam_re, lam_im, log_dt, b_re_t, b_im_t, d_lr, d_li, d_br, d_bi)


def _block_diag(m):
    m4 = m.reshape(SSM_COLS, 8, SSM_GROUP, SSM_STATE)
    return jnp.einsum("aghp,gk->aghkp", m4, jnp.eye(8, dtype=m.dtype)).reshape(SSM_COLS, LANES, 512)


def _block_diag_take(d):
    d6 = d.reshape(SSM_COLS, 8, SSM_GROUP, 2, 8, SSM_STATE)
    return jnp.einsum("aghrgp->raghp", d6).reshape(2, SSM_GROUPS, SSM_GROUP, SSM_STATE)


def _cmul(ar, ai, br, bi):
    return ar * br - ai * bi, ar * bi + ai * br


def _power(lr, li, n):
    assert n & (n - 1) == 0
    for _ in range(n.bit_length() - 1):
        lr, li = _cmul(lr, li, lr, li)
    return lr, li


def _ssm_fwd(u_p, w_b, lam_r, lam_i, c_m, d_skip, B, L, tj):
    J = L // N_CHUNK
    njt = J // tj
    R = tj * N_CHUNK
    H = 512

    def body(u_ref, wb_ref, lr_ref, li_ref, cm_ref, d_ref, y_ref, x_ref, xin_ref, bu_s, st_s, xin_s):
        ph, jt = pl.program_id(2), pl.program_id(3)
        lr, li = lr_ref[...], li_ref[...]

        @pl.when(jnp.logical_and(ph == 0, jt == 0))
        def _():
            st_s[...] = jnp.zeros_like(st_s)

        bu_s[...] = _dot(u_ref[...].astype(BF16), wb_ref[...].astype(BF16))

        def scan(store):
            def step(j, carry):
                xr, xi = carry
                r = pl.ds(pl.multiple_of(j * N_CHUNK, N_CHUNK), N_CHUNK)
                nr = lr * xr - li * xi + bu_s[r, 0:H]
                ni = lr * xi + li * xr + bu_s[r, H:2 * H]
                if store:
                    x_ref[r, 0:H] = nr
                    x_ref[r, H:2 * H] = ni
                return nr, ni

            xr, xi = lax.fori_loop(0, tj, step, (st_s[:, 0:H], st_s[:, H:2 * H]))
            st_s[:, 0:H] = xr
            st_s[:, H:2 * H] = xi

        @pl.when(ph == 0)
        def _():
            scan(False)

            @pl.when(jt == njt - 1)
            def _():
                pr, pi = _power(lr[0:1], li[0:1], J)
                xin_s[0:1, :] = jnp.zeros((1, 2 * H), F32)
                for c in range(1, N_CHUNK):
                    qr, qi = _cmul(pr, pi, xin_s[c - 1:c, 0:H], xin_s[c - 1:c, H:2 * H])
                    xin_s[c:c + 1, 0:H] = qr + st_s[c - 1:c, 0:H]
                    xin_s[c:c + 1, H:2 * H] = qi + st_s[c - 1:c, H:2 * H]
                xin_ref[...] = xin_s[...]
                st_s[...] = xin_s[...]

        @pl.when(ph == 1)
        def _():
            scan(True)
            y = _dot(x_ref[...].astype(BF16), cm_ref[...].astype(BF16))
            y_ref[...] = y + d_ref[...] * u_ref[...]

    return pl.pallas_call(
        body, name="ssm_fwd", grid=(SSM_COLS, B, 2, njt),
        in_specs=[
            pl.BlockSpec((None, R, LANES), lambda i, b, ph, jt: (b, jt, i)),
            pl.BlockSpec((None, LANES, 2 * H), lambda i, b, ph, jt: (i, 0, 0)),
            pl.BlockSpec((None, N_CHUNK, H), lambda i, b, ph, jt: (i, 0, 0)),
            pl.BlockSpec((None, N_CHUNK, H), lambda i, b, ph, jt: (i, 0, 0)),
            pl.BlockSpec((None, 2 * H, LANES), lambda i, b, ph, jt: (i, 0, 0)),
            pl.BlockSpec((1, LANES), lambda i, b, ph, jt: (0, i)),
        ],
        out_specs=[
            pl.BlockSpec((None, R, LANES), lambda i, b, ph, jt: (b, jt * ph, i)),
            pl.BlockSpec((None, R, 2 * H), lambda i, b, ph, jt: (b, jt * ph, i)),
            pl.BlockSpec((None, None, N_CHUNK, 2 * H), lambda i, b, ph, jt: (b, i, 0, 0)),
        ],
        out_shape=[
            jax.ShapeDtypeStruct((B, L, SSM_WIDTH), F32),
            jax.ShapeDtypeStruct((B, L, SSM_COLS * 2 * H), F32),
            jax.ShapeDtypeStruct((B, SSM_COLS, N_CHUNK, 2 * H), F32),
        ],
        scratch_shapes=[pltpu.VMEM((R, 2 * H), F32), pltpu.VMEM((N_CHUNK, 2 * H), F32), pltpu.VMEM((N_CHUNK, 2 * H), F32)],
        compiler_params=_params(("arbitrary",) * 4),
    )(u_p, w_b, lam_r, lam_i, c_m, d_skip)


def _ssm_bwd(dy_p, u_p, x, xin, w_bt, lam_r, lam_i, c_mt, d_skip, B, L, tj):
    J = L // N_CHUNK
    njt = J // tj
    R = tj * N_CHUNK
    H = 512
    x4 = x.reshape(B, J, N_CHUNK, SSM_COLS * 2 * H)

    def body(dy_ref, u_ref, x_ref, xp_ref, xin_ref, wbt_ref, lr_ref, li_ref, cmt_ref, d_ref,
             du_ref, dwb_ref, dcm_ref, dlr_ref, dli_ref, dd_ref, ca_s, a_s, st_s, dl_s):
        b, ph, jt = pl.program_id(1), pl.program_id(2), pl.program_id(3)
        jr = njt - 1 - jt
        lr, li = lr_ref[...], -li_ref[...]

        @pl.when(jnp.logical_and(b == 0, jnp.logical_and(ph == 0, jt == 0)))
        def _():
            dwb_ref[...] = jnp.zeros_like(dwb_ref)
            dcm_ref[...] = jnp.zeros_like(dcm_ref)
            dlr_ref[...] = jnp.zeros_like(dlr_ref)
            dli_ref[...] = jnp.zeros_like(dli_ref)
            dd_ref[...] = jnp.zeros_like(dd_ref)
            dl_s[...] = jnp.zeros_like(dl_s)

        @pl.when(jnp.logical_and(ph == 0, jt == 0))
        def _():
            st_s[...] = jnp.zeros_like(st_s)

        ca_s[...] = _dot(dy_ref[...].astype(BF16), cmt_ref[...].astype(BF16))

        def scan(store):
            def step(n, carry):
                ar, ai = carry
                r = pl.ds(pl.multiple_of((tj - 1 - n) * N_CHUNK, N_CHUNK), N_CHUNK)
                nr = lr * ar - li * ai + ca_s[r, 0:H]
                ni = lr * ai + li * ar + ca_s[r, H:2 * H]
                if store:
                    a_s[r, 0:H] = nr
                    a_s[r, H:2 * H] = ni
                return nr, ni

            ar, ai = lax.fori_loop(0, tj, step, (st_s[:, 0:H], st_s[:, H:2 * H]))
            st_s[:, 0:H] = ar
            st_s[:, H:2 * H] = ai

        @pl.when(ph == 0)
        def _():
            scan(False)

            @pl.when(jt == njt - 1)
            def _():
                pr, pi = _power(lr[0:1], li[0:1], J)
                a_s[N_CHUNK - 1:N_CHUNK, :] = jnp.zeros((1, 2 * H), F32)
                for c in range(N_CHUNK - 2, -1, -1):
                    qr, qi = _cmul(pr, pi, a_s[c + 1:c + 2, 0:H], a_s[c + 1:c + 2, H:2 * H])
                    a_s[c:c + 1, 0:H] = qr + st_s[c + 1:c + 2, 0:H]
                    a_s[c:c + 1, H:2 * H] = qi + st_s[c + 1:c + 2, H:2 * H]
                st_s[...] = a_s[0:N_CHUNK, :]

        @pl.when(ph == 1)
        def _():
            scan(True)
            dy = dy_ref[...]
            u = u_ref[...]
            a_b = a_s[...].astype(BF16)
            du_ref[...] = (_dot(a_b, wbt_ref[...].astype(BF16)) + d_ref[...] * dy).astype(du_ref.dtype)
            dwb_ref[...] += _dot(u.astype(BF16), a_b, _TN)
            dcm_ref[...] += _dot(x_ref[...].astype(BF16), dy.astype(BF16), _TN)
            dd_ref[...] += jnp.sum(dy * u, axis=0, keepdims=True)

            first = jnp.where(jr == 0, xin_ref[...], xp_ref[...])
            a0r, a0i = a_s[0:N_CHUNK, 0:H], a_s[0:N_CHUNK, H:2 * H]
            acc0 = (a0r * first[:, 0:H] + a0i * first[:, H:2 * H], a0i * first[:, 0:H] - a0r * first[:, H:2 * H])

            def step(j, carry):
                sr, si = carry
                r = pl.ds(pl.multiple_of(j * N_CHUNK, N_CHUNK), N_CHUNK)
                rp = pl.ds(pl.multiple_of((j - 1) * N_CHUNK, N_CHUNK), N_CHUNK)
                ar, ai = a_s[r, 0:H], a_s[r, H:2 * H]
                xr, xi = x_ref[rp, 0:H], x_ref[rp, H:2 * H]
                return sr + ar * xr + ai * xi, si + ai * xr - ar * xi

            sr, si = lax.fori_loop(1, tj, step, acc0)
            dl_s[:, 0:H] += sr
            dl_s[:, H:2 * H] += si

            @pl.when(jnp.logical_and(b == B - 1, jt == njt - 1))
            def _():
                dlr_ref[...] = jnp.sum(dl_s[:, 0:H], axis=0, keepdims=True)
                dli_ref[...] = jnp.sum(dl_s[:, H:2 * H], axis=0, keepdims=True)
                dl_s[...] = jnp.zeros_like(dl_s)

    rev = lambda ph, jt: (njt - 1 - jt) * ph + (njt - 1) * (1 - ph)
    return pl.pallas_call(
        body, name="ssm_bwd", grid=(SSM_COLS, B, 2, njt),
        in_specs=[
            pl.BlockSpec((None, R, LANES), lambda i, b, ph, jt: (b, njt - 1 - jt, i)),
            pl.BlockSpec((None, R, LANES), lambda i, b, ph, jt: (b, njt - 1 - jt, i)),
            pl.BlockSpec((None, R, 2 * H), lambda i, b, ph, jt: (b, rev(ph, jt), i)),
            pl.BlockSpec((None, None, N_CHUNK, 2 * H),
                         lambda i, b, ph, jt: (b, jnp.maximum((njt - 1 - jt) * tj - 1, 0), 0, i)),
            pl.BlockSpec((None, None, N_CHUNK, 2 * H), lambda i, b, ph, jt: (b, i, 0, 0)),
            pl.BlockSpec((None, 2 * H, LANES), lambda i, b, ph, jt: (i, 0, 0)),
            pl.BlockSpec((None, N_CHUNK, H), lambda i, b, ph, jt: (i, 0, 0)),
            pl.BlockSpec((None, N_CHUNK, H), lambda i, b, ph, jt: (i, 0, 0)),
            pl.BlockSpec((None, LANES, 2 * H), lambda i, b, ph, jt: (i, 0, 0)),
            pl.BlockSpec((1, LANES), lambda i, b, ph, jt: (0, i)),
        ],
        out_specs=[
            pl.BlockSpec((None, R, LANES), lambda i, b, ph, jt: (b, rev(ph, jt), i)),
            pl.BlockSpec((None, LANES, 2 * H), lambda i, b, ph, jt: (i, 0, 0)),
            pl.BlockSpec((None, 2 * H, LANES), lambda i, b, ph, jt: (i, 0, 0)),
            pl.BlockSpec((None, 1, H), lambda i, b, ph, jt: (i, 0, 0)),
            pl.BlockSpec((None, 1, H), lambda i, b, ph, jt: (i, 0, 0)),
            pl.BlockSpec((1, LANES), lambda i, b, ph, jt: (0, i)),
        ],
        out_shape=[
            jax.ShapeDtypeStruct((B, L, SSM_WIDTH), BF16),
            jax.ShapeDtypeStruct((SSM_COLS, LANES, 2 * H), F32),
            jax.ShapeDtypeStruct((SSM_COLS, 2 * H, LANES), F32),
            jax.ShapeDtypeStruct((SSM_COLS, 1, H), F32),
            jax.ShapeDtypeStruct((SSM_COLS, 1, H), F32),
            jax.ShapeDtypeStruct((1, SSM_WIDTH), F32),
        ],
        scratch_shapes=[pltpu.VMEM((R, 2 * H), F32), pltpu.VMEM((R, 2 * H), F32),
                        pltpu.VMEM((N_CHUNK, 2 * H), F32), pltpu.VMEM((N_CHUNK, 2 * H), F32)],
        compiler_params=_params(("arbitrary",) * 4),
    )(dy_p, u_p, x, x4, xin, w_bt, lam_r, lam_i, c_mt, d_skip)


def _to_scan_layout(t, B, L):
    C = t.shape[-1]
    return t.reshape(B, N_CHUNK, L // N_CHUNK, C).transpose(0, 2, 1, 3).reshape(B, L, C)


def _from_scan_layout(t, B, L):
    C = t.shape[-1]
    return t.reshape(B, L // N_CHUNK, N_CHUNK, C).transpose(0, 2, 1, 3).reshape(B * L, C)


def _local_step(x, target, p, w_in, late_weights, mlp_grads_ready=None, rest_grads_ready=None, order=None, *,
                ssm_tile=128):
    B, L, D = x.shape
    T = B * L
    x2 = x.reshape(T, D)
    row = lambda v: v.reshape(1, -1)
    g1, g2, ga, gs, b_glu = row(p["norm1_g"]), row(p["norm2_g"]), row(p["attn_out_g"]), row(p["ssm_out_g"]), row(p["b_glu"])
    g1_first = g1 if order is None else g1 + order
    gq8 = jnp.tile(row(p["q_norm_g"]), (1, N_HEADS))
    gk8 = jnp.tile(row(p["k_norm_g"]), (1, N_HEADS))

    G, P, Hh = SSM_GROUPS, SSM_STATE, SSM_GROUP
    lam_re3, lam_im3 = p["ssm_lambda_re"].reshape(G, 1, P), p["ssm_lambda_im"].reshape(G, 1, P)
    log_dt3 = p["ssm_log_dt"].reshape(G, 1, 1)
    b_re_t, b_im_t = p["ssm_b_re"].transpose(0, 2, 1), p["ssm_b_im"].transpose(0, 2, 1)
    lbr, lbi, bbr, bbi = _ssm_prep(lam_re3, lam_im3, log_dt3, b_re_t, b_im_t)
    w_b = jnp.concatenate([_block_diag(bbr), _block_diag(bbi)], axis=2)
    c_mt = jnp.concatenate([_block_diag(p["ssm_c_re"]), -_block_diag(p["ssm_c_im"])], axis=2)
    w_bt, c_m = w_b.transpose(0, 2, 1), c_mt.transpose(0, 2, 1)
    lam_r = jnp.broadcast_to(lbr.reshape(SSM_COLS, 1, 512), (SSM_COLS, N_CHUNK, 512))
    lam_i = jnp.broadcast_to(lbi.reshape(SSM_COLS, 1, 512), (SSM_COLS, N_CHUNK, 512))
    d_skip = p["ssm_d"].reshape(1, SSM_WIDTH)

    xn = _rowwise("norm1", _rms, [x2], [g1_first], [(D, BF16)])
    proj = _matmul("proj", xn, w_in)
    qn, kn, vb = _qk_norm(proj, gq8, gk8)
    sb, rtot = _attn_fwd(qn, kn, vb, B, L)
    u_p = _to_scan_layout(proj[:, 3 * SB_WIDTH:], B, L)
    y_p, xs, xin = _ssm_fwd(u_p, w_b, lam_r, lam_i, c_m, d_skip, B, L, ssm_tile)
    y2 = y_p.reshape(T, SSM_WIDTH)
    gel = _rowwise("gelu", jax.nn.gelu, [y2], [], [(SSM_WIDTH, BF16)])
    w_glu, w_out, w_mlp_in, w_mlp_out = late_weights(gel)
    pre = _matmul("glu_gate", gel, w_glu)
    ssm_n = _rowwise("glu_out", _glu_branch, [y2, pre], [b_glu, gs], [(SSM_WIDTH, BF16)])
    sb_n = _rowwise("attn_out_norm", _rms, [sb], [ga], [(SB_WIDTH, BF16)])
    mixed = jnp.concatenate([sb_n, _from_scan_layout(ssm_n, B, L)], axis=1)
    h1 = _matmul("out_proj", mixed, w_out, extras=[x2], epilogue=lambda acc, r: (acc + r,))
    hn = _rowwise("norm2", _rms, [h1], [g2], [(D, BF16)])
    act, a_pre = _matmul("mlp_in", hn, w_mlp_in, out_dtypes=(BF16, BF16),
                         epilogue=lambda acc: (jnp.square(jnp.maximum(acc, 0.0)), acc))
    out = _matmul("mlp_out", act, w_mlp_out, extras=[h1], epilogue=lambda acc, r: (acc + r,))

    def loss_fn(o, t):
        diff = o - t
        part = jnp.sum(jnp.sum(diff * diff, axis=0, keepdims=True), axis=1, keepdims=True)
        d = diff * (1.0 / D)
        return d, d, part * (0.5 / D)

    d_out, d_out_b, loss = _rowwise("loss", loss_fn, [out, target.reshape(T, D)], [], [(D, F32), (D, BF16)],
                                    sums=[(1, 1)])

    d_apre = _matmul("mlp_out_dx", d_out_b, w_mlp_out, tb=True, extras=[a_pre], out_dtypes=(BF16,),
                     epilogue=lambda acc, ap: (acc * (2.0 * jnp.maximum(ap.astype(F32), 0.0)),))
    both = lambda acc: (acc, acc)
    g_w_mlp_out, g_w_mlp_out_b = _matmul("mlp_out_dw", act, d_out_b, ta=True, out_dtypes=(F32, BF16), epilogue=both)
    g_w_mlp_in, g_w_mlp_in_b = _matmul("mlp_in_dw", hn, d_apre, ta=True, col_blocked=True, out_dtypes=(F32, BF16),
                                       epilogue=both, tn=w_mlp_in.shape[1] // N_DEV)
    if mlp_grads_ready is not None:
        g2 = g2 + mlp_grads_ready(g_w_mlp_out, g_w_mlp_out_b, g_w_mlp_in, g_w_mlp_in_b)
    d_hn = _matmul("mlp_in_dx", d_apre, w_mlp_in, tb=True)

    def norm_bwd_res(h, dy, res, g):
        _, vjp = jax.vjp(_rms, h, g)
        dh, dg = vjp(dy)
        return res + dh, dg

    def norm_bwd_res2(h, dy, res, g):
        d, dg = norm_bwd_res(h, dy, res, g)
        return d, d, dg

    d_h1, d_h1_b, g_norm2 = _rowwise("norm2_bwd", norm_bwd_res2, [h1, d_hn, d_out], [g2], [(D, F32), (D, BF16)],
                                     sums=[(1, D)])

    d_mixed = _matmul("out_proj_dx", d_h1_b, w_out, tb=True)
    g_w_out, g_w_out_b = _matmul("out_proj_dw", mixed, d_h1_b, ta=True, out_dtypes=(F32, BF16), epilogue=both)

    def norm_bwd(h, dy, g):
        _, vjp = jax.vjp(_rms, h, g)
        return vjp(dy)

    d_sb, g_attn_out = _rowwise("attn_out_norm_bwd", norm_bwd, [sb, (d_mixed, 0, SB_WIDTH)], [ga],
                                [(SB_WIDTH, F32)], sums=[(1, SB_WIDTH)])
    d_ssm_n = _to_scan_layout(d_mixed[:, SB_WIDTH:], B, L).reshape(T, SSM_WIDTH)

    def glu_bwd(y, pre_, dy, bg, g):
        _, vjp = jax.vjp(_glu_branch, y, pre_, bg, g)
        d_y, d_pre, d_bg, d_g = vjp(dy)
        return d_y, d_pre, d_bg, d_g

    d_y_direct, d_pre, g_b_glu, g_ssm_out = _rowwise(
        "glu_out_bwd", glu_bwd, [y2, pre, d_ssm_n], [b_glu, gs], [(SSM_WIDTH, F32), (SSM_WIDTH, BF16)],
        sums=[(1, SSM_WIDTH), (1, SSM_WIDTH)])
    d_gel = _matmul("glu_gate_dx", d_pre, w_glu, tb=True)
    g_w_glu, g_w_glu_b = _matmul("glu_gate_dw", gel, d_pre, ta=True, out_dtypes=(F32, BF16), epilogue=both)

    def gelu_bwd(y, dg, dy0):
        _, vjp = jax.vjp(jax.nn.gelu, y)
        return dy0 + vjp(dg)[0]

    d_y = _rowwise("gelu_bwd", gelu_bwd, [y2, d_gel, d_y_direct], [], [(SSM_WIDTH, F32)])

    du_p, d_wb, d_cm, d_lr, d_li, g_d = _ssm_bwd(
        d_y.reshape(B, L, SSM_WIDTH), u_p, xs, xin, w_bt, lam_r, lam_i, c_mt, d_skip, B, L, ssm_tile)
    d_bb = _block_diag_take(d_wb.reshape(SSM_COLS, LANES, 2, 512))
    d_c = _block_diag_take(d_cm.transpose(0, 2, 1).reshape(SSM_COLS, LANES, 2, 512))
    g_lam_re, g_lam_im, g_log_dt, g_b_re_t, g_b_im_t = _ssm_prep_bwd(
        lam_re3, lam_im3, log_dt3, b_re_t, b_im_t,
        d_lr.reshape(G, 1, P), d_li.reshape(G, 1, P), d_bb[0], d_bb[1])
    d_qn, d_kn, d_v = _attn_bwd(qn, kn, vb, rtot, d_sb, B, L)
    d_q, d_k, g_q, g_k = _qk_norm_bwd(proj, gq8, gk8, d_qn, d_kn)

    d_proj = jnp.concatenate([d_q, d_k, d_v, _from_scan_layout(du_p, B, L)], axis=1)
    g_w_in, g_w_in_b = _matmul("proj_dw", xn, d_proj, ta=True, col_blocked=True, out_dtypes=(F32, BF16),
                               epilogue=both, tn=w_in.shape[1] // N_DEV)
    if rest_grads_ready is not None:
        g1 = g1 + rest_grads_ready([g_w_in, g_w_glu, g_w_out], [g_w_in_b, g_w_glu_b, g_w_out_b])
    d_xn = _matmul("proj_dx", d_proj, w_in, tb=True)
    grad_x, g_norm1 = _rowwise("norm1_bwd", norm_bwd_res, [x2, d_xn, d_h1], [g1], [(D, F32)], sums=[(1, D)])

    small = {
        "norm1_g": g_norm1.reshape(-1),
        "q_norm_g": g_q.reshape(-1),
        "k_norm_g": g_k.reshape(-1),
        "ssm_lambda_re": g_lam_re.reshape(G, P),
        "ssm_lambda_im": g_lam_im.reshape(G, P),
        "ssm_log_dt": g_log_dt.reshape(G),
        "ssm_b_re": g_b_re_t.transpose(0, 2, 1),
        "ssm_b_im": g_b_im_t.transpose(0, 2, 1),
        "ssm_c_re": d_c[0],
        "ssm_c_im": -d_c[1],
        "ssm_d": g_d.reshape(G, Hh),
        "b_glu": g_b_glu.reshape(-1),
        "attn_out_g": g_attn_out.reshape(-1),
        "ssm_out_g": g_ssm_out.reshape(-1),
        "norm2_g": g_norm2.reshape(-1),
    }
    big = {"w_in": g_w_in, "w_glu": g_w_glu, "w_out": g_w_out, "w_mlp_in": g_w_mlp_in, "w_mlp_out": g_w_mlp_out}
    return loss[0, 0], grad_x.reshape(B, L, D), small, big


_ANY = pl.BlockSpec(memory_space=pl.ANY)
_MESH = pl.DeviceIdType.MESH


def _all_gather(name, shards):
    n = len(shards)

    def body(*refs):
        in_refs, out_refs = refs[:n], refs[n:2 * n]
        send_sems, recv_sems, local_sems = refs[2 * n:]
        x, y, c = lax.axis_index("x"), lax.axis_index("y"), lax.axis_index("c")
        me, sibling = (x, y, c), (x, y, 1 - c)
        chips = [(1 - x, y), (x, 1 - y), (1 - x, 1 - y)]

        def copy(a, k, block, to, src=None):
            px, py, pc = block
            rows = out_refs[a].at[4 * px + 2 * py + pc]
            return pltpu.make_async_remote_copy(
                src_ref=rows if src is None else src, dst_ref=rows, send_sem=send_sems.at[a, k],
                recv_sem=recv_sems.at[a, k], device_id=to, device_id_type=_MESH)

        mine = [pltpu.make_async_copy(in_refs[a], out_refs[a].at[4 * x + 2 * y + c], local_sems.at[a]) for a in range(n)]
        first, passed = [], []
        for a in range(n):
            mine[a].start()
            first.append(copy(a, 0, me, sibling, src=in_refs[a]))
            first += [copy(a, 1 + j, me, (*chip, c), src=in_refs[a]) for j, chip in enumerate(chips)]
        for cp in first:
            cp.start()
        for j, chip in enumerate(chips):
            for a in range(n):
                copy(a, 1 + j, (*chip, c), me).wait_recv()
                fwd = copy(a, 4 + j, (*chip, c), sibling)
                fwd.start()
                passed.append(fwd)
        for a in range(n):
            copy(a, 0, sibling, me).wait_recv()
            for j, chip in enumerate(chips):
                copy(a, 4 + j, (*chip, 1 - c), me).wait_recv()
        for cp in first + passed:
            cp.wait_send()
        for cp in mine:
            cp.wait()

    return pl.pallas_call(
        body, name=name,
        in_specs=[_ANY] * n, out_specs=[_ANY] * n,
        out_shape=[jax.ShapeDtypeStruct((N_DEV, *s.shape), s.dtype) for s in shards],
        scratch_shapes=[pltpu.SemaphoreType.DMA((n, 7)), pltpu.SemaphoreType.DMA((n, 7)), pltpu.SemaphoreType.DMA((n,))],
    )(*shards)


_HBM = pl.BlockSpec(memory_space=pltpu.HBM)
_SEM = pl.BlockSpec(memory_space=pltpu.SEMAPHORE)
_EFFECT = pltpu.SideEffectType.DATAFLOW_SIDE_EFFECTING
_FLIPS = [(dx, dy, dc) for dx in (0, 1) for dy in (0, 1) for dc in (0, 1) if (dx, dy, dc) != (0, 0, 0)]


def _exchange_start(name, srcs, lands, per_peer):
    n = len(srcs)

    def body(*refs):
        src_refs, land_refs = refs[:n], refs[n:2 * n]
        send_sems, recv_sems = refs[2 * n:3 * n], refs[3 * n:4 * n]
        token = refs[-1]
        x, y, c = lax.axis_index("x"), lax.axis_index("y"), lax.axis_index("c")
        me = 4 * x + 2 * y + c
        for dx, dy, dc in _FLIPS:
            px, py, pc = (1 - x if dx else x), (1 - y if dy else y), (1 - c if dc else c)
            for a in range(n):
                pltpu.make_async_remote_copy(
                    src_ref=src_refs[a].at[4 * px + 2 * py + pc] if per_peer else src_refs[a],
                    dst_ref=land_refs[a].at[me], send_sem=send_sems[a], recv_sem=recv_sems[a],
                    device_id=(px, py, pc), device_id_type=_MESH).start()
        token[...] = jnp.zeros_like(token)

    hbm = lambda t: pltpu.with_memory_space_constraint(t, pltpu.HBM)
    res = pl.pallas_call(
        body, name=name,
        out_shape=(*[pltpu.SemaphoreType.DMA(())] * (2 * n), *[pltpu.HBM(t.shape, t.dtype) for t in (*srcs, *lands)],
                   jax.ShapeDtypeStruct((8, LANES), F32)),
        in_specs=[_HBM] * (2 * n),
        out_specs=(*[_SEM] * (2 * n), *[_HBM] * (2 * n), pl.BlockSpec(memory_space=pltpu.VMEM)),
        input_output_aliases={i: 2 * n + i for i in range(2 * n)},
        compiler_params=pltpu.CompilerParams(has_side_effects=_EFFECT),
    )(*[hbm(t) for t in (*srcs, *lands)])
    return res[:-1], res[-1]


def _exchange_wait(name, handle, after):
    n = len(handle) // 4
    sems, thru = handle[:2 * n], handle[2 * n:]

    def body(*refs):
        land_refs = refs[n:2 * n]
        send_sems, recv_sems = refs[2 * n:3 * n], refs[3 * n:4 * n]
        me = (lax.axis_index("x"), lax.axis_index("y"), lax.axis_index("c"))
        for a in range(n):
            seven = land_refs[a].at[pl.ds(0, len(_FLIPS))]
            all_copies = pltpu.make_async_remote_copy(
                src_ref=seven, dst_ref=seven, send_sem=send_sems[a], recv_sem=recv_sems[a], device_id=me,
                device_id_type=_MESH)
            all_copies.wait_send()
            all_copies.wait_recv()

    res = pl.pallas_call(
        body, name=name, out_shape=tuple(pltpu.HBM(t.shape, t.dtype) for t in thru),
        in_specs=[*[_HBM] * (2 * n), *[_SEM] * (2 * n), _ANY], out_specs=tuple([_HBM] * (2 * n)),
        input_output_aliases={i: i for i in range(2 * n)},
        compiler_params=pltpu.CompilerParams(has_side_effects=_EFFECT),
    )(*thru, *sems, after)
    return res[n:]


def _adamw_gathered(name, own, parts, me, w, m, v):
    r, c = w.shape
    tr = min(r, 256)

    def body(me_ref, own_ref, p_ref, w_ref, m_ref, v_ref, g_out, d_out, m_out, v_out):
        g = own_ref[...]
        for j in range(N_DEV):
            g = g + p_ref[j].astype(F32)
        delta, m_new, v_new = _adamw(w_ref[...], g, m_ref[...], v_ref[...])
        g_out[...] = g
        d_out[...] = delta
        m_out[...] = m_new
        v_out[...] = v_new

    spec = pl.BlockSpec((tr, c), lambda i, me_ref: (i, 0))
    return pl.pallas_call(
        body, name=name,
        grid_spec=pltpu.PrefetchScalarGridSpec(
            num_scalar_prefetch=1, grid=(r // tr,),
            in_specs=[pl.BlockSpec((None, tr, c), lambda i, me_ref: (me_ref[0], i, 0)),
                      pl.BlockSpec((N_DEV, tr, c), lambda i, me_ref: (0, i, 0)), spec, spec, spec],
            out_specs=[spec] * 4),
        out_shape=[jax.ShapeDtypeStruct((r, c), F32)] * 4,
        compiler_params=_params(("parallel",)),
    )(me, own, parts, w, m, v)


def _adamw(w, g, m, v):
    m = ADAM_B1 * m + (1.0 - ADAM_B1) * g
    v = ADAM_B2 * v + (1.0 - ADAM_B2) * jnp.square(g)
    m_hat = m / (1.0 - ADAM_B1 ** ADAM_STEP)
    v_hat = v / (1.0 - ADAM_B2 ** ADAM_STEP)
    delta = -ADAM_LR * (m_hat / (jnp.sqrt(v_hat) + ADAM_EPS) + ADAM_WD * w)
    return delta, m, v


def _adamw_small(name, parts, w, m, v):
    _, r, c = parts.shape
    tr = 8

    def body(p_ref, w_ref, m_ref, v_ref, g_out, d_out, m_out, v_out):
        g = p_ref[0]
        for j in range(1, N_DEV):
            g = g + p_ref[j]
        delta, m_new, v_new = _adamw(w_ref[...], g, m_ref[...], v_ref[...])
        g_out[...] = g
        d_out[...] = delta
        m_out[...] = m_new
        v_out[...] = v_new

    spec = pl.BlockSpec((tr, c), lambda i: (i, 0))
    return pl.pallas_call(
        body, name=name, grid=(r // tr,),
        in_specs=[pl.BlockSpec((N_DEV, tr, c), lambda i: (0, i, 0)), spec, spec, spec],
        out_specs=[spec] * 4, out_shape=[jax.ShapeDtypeStruct((r, c), F32)] * 4,
        compiler_params=_params(("parallel",)),
    )(parts, w, m, v)


_WEIGHTS = ["norm1_g", "w_in", "q_norm_g", "k_norm_g", "ssm_lambda_re", "ssm_lambda_im", "ssm_log_dt", "ssm_b_re",
            "ssm_b_im", "ssm_c_re", "ssm_c_im", "ssm_d", "w_glu", "b_glu", "attn_out_g", "ssm_out_g", "w_out",
            "norm2_g", "w_mlp_in", "w_mlp_out"]
_BIG = ["w_in", "w_glu", "w_out", "w_mlp_in", "w_mlp_out"]
_SMALL = [n for n in _WEIGHTS if n not in _BIG]
_PACK_COLS = 1024


def _pack(tree, last=None):
    flat = [tree[n].reshape(-1).astype(F32) for n in _SMALL]
    size = sum(f.shape[0] for f in flat)
    rows = -(-(size + 1) // (_PACK_COLS * 8)) * 8
    pad = jnp.zeros((rows * _PACK_COLS - size - 1,), F32)
    tail = jnp.zeros((1,), F32) if last is None else last.reshape(1).astype(F32)
    return jnp.concatenate(flat + [pad, tail]).reshape(rows, _PACK_COLS)


def _unpack(buf, like):
    flat, out, off = buf.reshape(-1), {}, 0
    for n in _SMALL:
        size = like[n].size
        out[n] = flat[off:off + size].reshape(like[n].shape)
        off += size
    return out


def kernel(x, norm1_g, w_in, q_norm_g, k_norm_g, ssm_lambda_re, ssm_lambda_im, ssm_log_dt, ssm_b_re, ssm_b_im, ssm_c_re, ssm_c_im, ssm_d, w_glu, b_glu, attn_out_g, ssm_out_g, w_out, norm2_g, w_mlp_in, w_mlp_out, loss_target, m_norm1_g, m_w_in, m_q_norm_g, m_k_norm_g, m_ssm_lambda_re, m_ssm_lambda_im, m_ssm_log_dt, m_ssm_b_re, m_ssm_b_im, m_ssm_c_re, m_ssm_c_im, m_ssm_d, m_w_glu, m_b_glu, m_attn_out_g, m_ssm_out_g, m_w_out, m_norm2_g, m_w_mlp_in, m_w_mlp_out, v_norm1_g, v_w_in, v_q_norm_g, v_k_norm_g, v_ssm_lambda_re, v_ssm_lambda_im, v_ssm_log_dt, v_ssm_b_re, v_ssm_b_im, v_ssm_c_re, v_ssm_c_im, v_ssm_d, v_w_glu, v_b_glu, v_attn_out_g, v_ssm_out_g, v_w_out, v_norm2_g, v_w_mlp_in, v_w_mlp_out):
    w = dict(zip(_WEIGHTS, (norm1_g, w_in, q_norm_g, k_norm_g, ssm_lambda_re, ssm_lambda_im, ssm_log_dt, ssm_b_re, ssm_b_im, ssm_c_re, ssm_c_im, ssm_d, w_glu, b_glu, attn_out_g, ssm_out_g, w_out, norm2_g, w_mlp_in, w_mlp_out)))
    m = dict(zip(_WEIGHTS, (m_norm1_g, m_w_in, m_q_norm_g, m_k_norm_g, m_ssm_lambda_re, m_ssm_lambda_im, m_ssm_log_dt, m_ssm_b_re, m_ssm_b_im, m_ssm_c_re, m_ssm_c_im, m_ssm_d, m_w_glu, m_b_glu, m_attn_out_g, m_ssm_out_g, m_w_out, m_norm2_g, m_w_mlp_in, m_w_mlp_out)))
    v = dict(zip(_WEIGHTS, (v_norm1_g, v_w_in, v_q_norm_g, v_k_norm_g, v_ssm_lambda_re, v_ssm_lambda_im, v_ssm_log_dt, v_ssm_b_re, v_ssm_b_im, v_ssm_c_re, v_ssm_c_im, v_ssm_d, v_w_glu, v_b_glu, v_attn_out_g, v_ssm_out_g, v_w_out, v_norm2_g, v_w_mlp_in, v_w_mlp_out)))
    core = lax.axis_index("c").astype(jnp.int32).reshape(1)
    chip = (2 * lax.axis_index("x") + lax.axis_index("y")).astype(jnp.int32).reshape(1)

    me = (2 * chip + core).astype(jnp.int32)

    def landing(own=None, like=None):
        own = jnp.zeros_like(like) if own is None else own
        return lax.dynamic_update_slice(lax.empty((N_DEV, *like.shape), like.dtype), own[None], (me[0], 0, 0))

    (w_in_blocks,) = _all_gather("w_in_all_gather", [w_in.astype(BF16)])
    w_in_full = w_in_blocks.transpose(1, 0, 2).reshape(w_in.shape[0], -1)
    late = [n for n in _BIG if n != "w_in"]
    shards = [w[n].astype(BF16) for n in late]
    w_in_blocks, shards = lax.optimization_barrier((w_in_blocks, shards))
    weights_handle, weights_token = _exchange_start(
        "weights_send", shards, [landing(s, s) for s in shards], per_peer=False)

    def late_weights(after):
        got = dict(zip(late, _exchange_wait("weights_arrive", weights_handle, after)))
        return (got["w_glu"].reshape(-1, w_glu.shape[1]), got["w_out"].reshape(-1, w_out.shape[1]),
                got["w_mlp_in"].transpose(1, 0, 2).reshape(w_mlp_in.shape[0], -1),
                got["w_mlp_out"].reshape(-1, w_mlp_out.shape[1]))

    mlp = ["w_mlp_out", "w_mlp_in"]
    sent = {}

    def send_grads(name, names, own, own_b):
        blocks = lambda g, n: g.reshape(N_DEV, *w[n].shape)
        sent[name + "_own"] = [blocks(g, n) for g, n in zip(own, names)]
        srcs = [blocks(g, n) for g, n in zip(own_b, names)]
        sent[name], token = _exchange_start(name, srcs, [landing(like=s[0]) for s in srcs], per_peer=True)
        return token[0, 0]

    def mlp_grads_ready(g_out, g_out_b, g_in, g_in_b):
        return send_grads("mlp_grads_send", mlp, [g_out, g_in], [g_out_b, g_in_b])

    rest = ["w_in", "w_glu", "w_out"]

    def rest_grads_ready(own, own_b):
        return send_grads("rest_grads_send", rest, own, own_b)

    loss_local, grad_x, g_small, g_big = _local_step(
        x, loss_target, {n: w[n] for n in _SMALL}, w_in_full, late_weights, mlp_grads_ready, rest_grads_ready,
        weights_token[0, 0])

    grads, delta, new_m, new_v = {}, {}, {}, {}
    (small_parts,) = _all_gather("small_grads_all_gather", [_pack(g_small, last=loss_local)])
    packed = _adamw_small("adamw_small", small_parts, _pack(w), _pack(m), _pack(v))
    for tree, buf in zip((grads, delta, new_m, new_v), packed):
        tree.update(_unpack(buf, w))
    loss = packed[0][-1, -1]

    for send, arrive, names, after in (("mlp_grads_send", "mlp_grads_arrive", mlp, grad_x),
                                       ("rest_grads_send", "rest_grads_arrive", rest, packed[0])):
        for n, own, part in zip(names, sent[send + "_own"], _exchange_wait(arrive, sent[send], after)):
            grads[n], delta[n], new_m[n], new_v[n] = _adamw_gathered("adamw_" + n, own, part, me, w[n], m[n], v[n])

    return (loss, grad_x, *[grads[n] for n in _WEIGHTS], *[delta[n] for n in _WEIGHTS],
            *[new_m[n] for n in _WEIGHTS], *[new_v[n] for n in _WEIGHTS])
```

```python
import functools
import math

import jax
import jax.numpy as jnp
from jax import lax
from jax.experimental import pallas as pl
from jax.experimental.pallas import tpu as pltpu

F32 = jnp.float32
BF16 = jnp.bfloat16

EPS = 1e-6
HEAD_DIM = 64
N_HEADS = 8
SB_WIDTH = 512
SSM_WIDTH = 512
SSM_GROUP = 16
SSM_GROUPS = 32
SSM_STATE = 64
QBLOCK = 128
KBLOCK = 256
N_CHUNK = 8
SSM_COLS = 4
LANES = 128
N_DEV = 8

ADAM_LR = 0.001
ADAM_B1 = 0.9
ADAM_B2 = 0.999
ADAM_EPS = 1e-08
ADAM_WD = 0.01
ADAM_STEP = 10

VMEM_LIMIT = 56 * 1024 * 1024

_NT = (((1,), (1,)), ((), ()))
_NN = (((1,), (0,)), ((), ()))
_TN = (((0,), (0,)), ((), ()))


def _dot(a, b, dims=_NN):
    return lax.dot_general(a, b, dims, preferred_element_type=F32)


def _params(sem):
    return pltpu.CompilerParams(dimension_semantics=sem, vmem_limit_bytes=VMEM_LIMIT)


def _matmul(name, a, b, *, ta=False, tb=False, extras=(), epilogue=None, out_dtypes=(F32,),
            col_blocked=False, tm=1024, tn=512, tk=4096):
    M, K = (a.shape[1], a.shape[0]) if ta else a.shape
    N = b.shape[0] if tb else b.shape[1]
    tm, tn, tk = min(tm, M), min(tn, N), min(tk, K)
    assert M % tm == 0 and N % tn == 0 and K % tk == 0, (name, M, N, K)
    nk = K // tk
    n_ex, n_out = len(extras), len(out_dtypes)
    dims = (((0 if ta else 1,), (1 if tb else 0,)), ((), ()))

    def body(*refs):
        a_ref, b_ref = refs[0], refs[1]
        ex_refs = refs[2:2 + n_ex]
        o_refs = refs[2 + n_ex:2 + n_ex + n_out]
        k = pl.program_id(2)
        part = _dot(a_ref[...].astype(BF16), b_ref[...].astype(BF16), dims)

        def finish(acc):
            outs = (acc,) if epilogue is None else epilogue(acc, *[e[...] for e in ex_refs])
            for o_ref, o in zip(o_refs, outs):
                o_ref[...] = o.astype(o_ref.dtype)

        if nk == 1:
            finish(part)
        else:
            acc_ref = refs[-1]

            @pl.when(k == 0)
            def _():
                acc_ref[...] = part

            @pl.when(jnp.logical_and(k > 0, k < nk - 1))
            def _():
                acc_ref[...] += part

            @pl.when(k == nk - 1)
            def _():
                finish(acc_ref[...] + part)

    a_spec = pl.BlockSpec((tk, tm), lambda i, j, k: (k, i)) if ta else pl.BlockSpec((tm, tk), lambda i, j, k: (i, k))
    b_spec = pl.BlockSpec((tn, tk), lambda i, j, k: (j, k)) if tb else pl.BlockSpec((tk, tn), lambda i, j, k: (k, j))
    ex_specs = [pl.BlockSpec((tm, tn), lambda i, j, k: (i, j)) for _ in extras]
    if col_blocked:
        out_specs = [pl.BlockSpec((None, tm, tn), lambda i, j, k: (j, i, 0)) for _ in out_dtypes]
        out_shape = [jax.ShapeDtypeStruct((N // tn, M, tn), dt) for dt in out_dtypes]
    else:
        out_specs = [pl.BlockSpec((tm, tn), lambda i, j, k: (i, j)) for _ in out_dtypes]
        out_shape = [jax.ShapeDtypeStruct((M, N), dt) for dt in out_dtypes]
    outs = pl.pallas_call(
        body, name=name, grid=(M // tm, N // tn, nk),
        in_specs=[a_spec, b_spec, *ex_specs], out_specs=out_specs, out_shape=out_shape,
        scratch_shapes=[pltpu.VMEM((tm, tn), F32)] if nk > 1 else [],
        compiler_params=_params(("parallel", "parallel", "arbitrary")),
    )(a, b, *extras)
    return outs[0] if n_out == 1 else outs


def _rowwise(name, fn, rows, small, outs, sums=(), tile=256):
    specs, args = [], []
    T = None
    for r in rows:
        arr, cb, w = r if isinstance(r, tuple) else (r, 0, r.shape[1])
        T = arr.shape[0]
        specs.append((w, cb))
        args.append(arr)
    tile = min(tile, T)
    assert T % tile == 0
    n_r, n_s, n_o, n_a = len(rows), len(small), len(outs), len(sums)

    def body(*refs):
        r_refs = refs[:n_r]
        s_refs = refs[n_r:n_r + n_s]
        o_refs = refs[n_r + n_s:n_r + n_s + n_o]
        a_refs = refs[n_r + n_s + n_o:]
        res = fn(*[r[...] for r in r_refs], *[s[...] for s in s_refs])
        res = res if isinstance(res, (tuple, list)) else (res,)
        for o_ref, o in zip(o_refs, res[:n_o]):
            o_ref[...] = o.astype(o_ref.dtype)

        @pl.when(pl.program_id(0) == 0)
        def _():
            for a_ref in a_refs:
                a_ref[...] = jnp.zeros_like(a_ref)

        for a_ref, v in zip(a_refs, res[n_o:]):
            a_ref[...] += v.astype(F32)

    in_specs = [pl.BlockSpec((tile, w), functools.partial(lambda i, cb: (i, cb), cb=cb)) for w, cb in specs]
    in_specs += [pl.BlockSpec(s.shape, functools.partial(lambda i, nd: (0,) * nd, nd=s.ndim)) for s in small]
    out_specs = [pl.BlockSpec((tile, w), lambda i: (i, 0)) for w, _ in outs]
    out_specs += [pl.BlockSpec(s, functools.partial(lambda i, nd: (0,) * nd, nd=len(s))) for s in sums]
    out_shape = [jax.ShapeDtypeStruct((T, w), dt) for w, dt in outs]
    out_shape += [jax.ShapeDtypeStruct(s, F32) for s in sums]
    res = pl.pallas_call(
        body, name=name, grid=(T // tile,), in_specs=in_specs, out_specs=out_specs, out_shape=out_shape,
        compiler_params=_params(("arbitrary",)),
    )(*args, *small)
    return res[0] if len(res) == 1 else res


def _rms(x, g):
    return x * lax.rsqrt(jnp.mean(x * x, axis=-1, keepdims=True) + EPS) * g


def _glu_branch(y, pre, b_glu, g_out):
    g = jax.nn.gelu(y)
    return _rms(g * jax.nn.sigmoid(pre + b_glu), g_out)


def _split_dot(x, tri_bf):
    hi = x.astype(BF16)
    lo = (x - hi.astype(F32)).astype(BF16)
    return _dot(hi, tri_bf) + _dot(lo, tri_bf)


def _softplus(z):
    return jnp.maximum(z, 0.0) + jnp.log(1.0 + jnp.exp(-jnp.abs(z)))


def _head(h):
    return slice(h * HEAD_DIM, (h + 1) * HEAD_DIM)


def _head_mean(x, seg):
    return _split_dot(x, seg) * (1.0 / HEAD_DIM)


def _qk_norm(proj, gq, gk):
    T = proj.shape[0]
    scale = 1.0 / math.sqrt(HEAD_DIM)
    idx = jnp.arange(SB_WIDTH) // HEAD_DIM
    seg = (idx[:, None] == idx[None, :]).astype(BF16)

    def fn(q, k, v, gq_, gk_, seg_):
        qn = q * lax.rsqrt(_head_mean(q * q, seg_) + EPS) * (gq_ * scale)
        kn = k * lax.rsqrt(_head_mean(k * k, seg_) + EPS) * gk_
        return qn, kn, v

    return _rowwise("qk_norm", fn, [(proj, 0, SB_WIDTH), (proj, 1, SB_WIDTH), (proj, 2, SB_WIDTH)], [gq, gk, seg],
                    [(SB_WIDTH, BF16)] * 3)


def _qk_norm_bwd(proj, gq, gk, d_qn, d_kn):
    scale = 1.0 / math.sqrt(HEAD_DIM)
    idx = jnp.arange(SB_WIDTH) // HEAD_DIM
    seg = (idx[:, None] == idx[None, :]).astype(BF16)

    def one(x, g, dy, seg_):
        r = lax.rsqrt(_head_mean(x * x, seg_) + EPS)
        gdy = g * dy
        dx = r * gdy - x * (r * r * r) * _head_mean(gdy * x, seg_)
        dg = jnp.sum(dy * x * r, axis=0, keepdims=True)
        return dx, sum(dg[:, _head(h)] for h in range(N_HEADS))

    def fn(q, k, dqn, dkn, gq_, gk_, seg_):
        dq, dgq = one(q, gq_, dqn * scale, seg_)
        dk, dgk = one(k, gk_, dkn, seg_)
        return dq, dk, dgq, dgk

    return _rowwise("qk_norm_bwd", fn, [(proj, 0, SB_WIDTH), (proj, 1, SB_WIDTH), d_qn, d_kn], [gq, gk, seg],
                    [(SB_WIDTH, BF16)] * 2, sums=[(1, HEAD_DIM)] * 2)


def _split_heads(refs, scratch, L):
    def chunk(i, _):
        r = pl.ds(pl.multiple_of(i * QBLOCK, QBLOCK), QBLOCK)
        for ref, s in zip(refs, scratch):
            for h in range(2):
                s[h, r, :] = ref[r, _head(h)]
        return 0

    lax.fori_loop(0, L // QBLOCK, chunk, 0)


Q_HALVES = KBLOCK // QBLOCK
_CHAINS = [(h, r) for h in range(2) for r in range(Q_HALVES)]


def _valid(i, kb):
    row = lax.broadcasted_iota(jnp.int32, (QBLOCK, KBLOCK), 0)
    col = lax.broadcasted_iota(jnp.int32, (QBLOCK, KBLOCK), 1)
    return col + (kb * KBLOCK - i * QBLOCK) < row


def _attn_fwd(qn, kn, vb, B, L):
    n_pairs = L // KBLOCK
    n_hp = N_HEADS // 2

    def body(q_ref, k_ref, v_ref, o_ref, rt_ref, q_s, k_s, v_s, after_s):
        _split_heads((q_ref, k_ref, v_ref), (q_s, k_s, v_s), L)
        r2 = lax.broadcasted_iota(jnp.int32, (KBLOCK, KBLOCK), 0)
        c2 = lax.broadcasted_iota(jnp.int32, (KBLOCK, KBLOCK), 1)
        after_s[...] = (r2 > c2).astype(after_s.dtype)

        def q_pair(p, _):
            rows = [pl.ds(pl.multiple_of((p * Q_HALVES + r) * QBLOCK, QBLOCK), QBLOCK) for r in range(Q_HALVES)]
            q_c = [q_s[h, rows[r], :] for h, r in _CHAINS]
            cs = range(len(_CHAINS))

            def k_block(kb, carry, diagonal):
                rk = pl.ds(pl.multiple_of(kb * KBLOCK, KBLOCK), KBLOCK)
                if diagonal:
                    valid = [_valid(p * Q_HALVES + r, kb) for r in range(Q_HALVES)]
                    keep = lambda c, t: jnp.where(valid[_CHAINS[c][1]], t, 0.0)
                else:
                    keep = lambda c, t: t
                z = [_dot(q_c[c], k_s[_CHAINS[c][0], rk, :], _NT) for c in cs]
                sp = [_softplus(z[c]) for c in cs]
                lom = [keep(c, -sp[c]) for c in cs]
                tail = [_split_dot(lom[c], after_s[...]) + carry[c][0] for c in cs]
                a = [keep(c, jnp.exp(z[c] - sp[c] + tail[c])) for c in cs]
                acc = [carry[c][1] + _dot(a[c].astype(v_s.dtype), v_s[_CHAINS[c][0], rk, :]) for c in cs]
                return tuple((carry[c][0] + jnp.sum(lom[c], axis=1, keepdims=True), acc[c]) for c in cs)

            init = (jnp.zeros((QBLOCK, 1), F32), jnp.zeros((QBLOCK, HEAD_DIM), F32))
            first = k_block(p, (init,) * len(_CHAINS), True)
            res = lax.fori_loop(1, p + 1, lambda n, carry: k_block(p - n, carry, False), first)
            for r in range(Q_HALVES):
                mine = [res[c] for c in cs if _CHAINS[c][1] == r]
                o_ref[rows[r], :] = jnp.concatenate([m[1] for m in mine], axis=1)
                rt_ref[rows[r], :] = jnp.concatenate(
                    [jnp.broadcast_to(m[0], (QBLOCK, HEAD_DIM)) for m in mine], axis=1)
            return 0

        lax.fori_loop(0, n_pairs, q_pair, 0)

    spec = pl.BlockSpec((L, LANES), lambda b, p: (b, p))
    return pl.pallas_call(
        body, name="attn_fwd", grid=(B, n_hp),
        in_specs=[spec] * 3, out_specs=[spec, spec],
        out_shape=[jax.ShapeDtypeStruct((B * L, SB_WIDTH), F32)] * 2,
        scratch_shapes=[pltpu.VMEM((2, L, HEAD_DIM), BF16)] * 3 + [pltpu.VMEM((KBLOCK, KBLOCK), BF16)],
        compiler_params=_params(("parallel", "parallel")),
    )(qn, kn, vb)


def _attn_bwd(qn, kn, vb, rtot, d_sb, B, L):
    n_pairs = L // KBLOCK
    n_hp = N_HEADS // 2

    def body(q_ref, k_ref, v_ref, rt_ref, do_ref, dq_ref, dk_ref, dv_ref,
             q_s, k_s, v_s, qt_s, dkt_s, dvt_s, after_s, before_s):
        _split_heads((q_ref, k_ref, v_ref), (q_s, k_s, v_s), L)

        def transpose_q(i, _):
            r = pl.ds(pl.multiple_of(i * QBLOCK, QBLOCK), QBLOCK)
            qt_s[:, r] = q_ref[r, :].astype(F32).T.astype(qt_s.dtype)
            return 0

        lax.fori_loop(0, L // QBLOCK, transpose_q, 0)
        dkt_s[...] = jnp.zeros_like(dkt_s)
        dvt_s[...] = jnp.zeros_like(dvt_s)
        r2 = lax.broadcasted_iota(jnp.int32, (KBLOCK, KBLOCK), 0)
        c2 = lax.broadcasted_iota(jnp.int32, (KBLOCK, KBLOCK), 1)
        after_s[...] = (r2 > c2).astype(after_s.dtype)
        before_s[...] = (r2 < c2).astype(before_s.dtype)

        def q_pair(p, _):
            rows = [pl.ds(pl.multiple_of((p * Q_HALVES + r) * QBLOCK, QBLOCK), QBLOCK) for r in range(Q_HALVES)]
            pair = pl.ds(pl.multiple_of(p * KBLOCK, KBLOCK), KBLOCK)
            do2 = do_ref[pair, :]
            do_t = do2.T.astype(v_s.dtype)
            tot2 = rt_ref[pair, :]
            cs = range(len(_CHAINS))
            hs = range(2)
            q_c = [q_s[h, rows[r], :] for h, r in _CHAINS]
            do_c = [do2[r * QBLOCK:(r + 1) * QBLOCK, _head(h)].astype(v_s.dtype) for h, r in _CHAINS]
            total = [tot2[r * QBLOCK:(r + 1) * QBLOCK, h * HEAD_DIM:h * HEAD_DIM + 1] for h, r in _CHAINS]
            qt_h = [qt_s[_head(h), pair] for h in hs]
            dot_h = [do_t[_head(h), :] for h in hs]

            def k_block(kb, carry, diagonal):
                rk = pl.ds(pl.multiple_of(kb * KBLOCK, KBLOCK), KBLOCK)
                if diagonal:
                    valid = [_valid(p * Q_HALVES + r, kb) for r in range(Q_HALVES)]
                    keep = lambda c, t: jnp.where(valid[_CHAINS[c][1]], t, 0.0)
                else:
                    keep = lambda c, t: t
                k_b = [k_s[h, rk, :] for h in hs]
                z = [_dot(q_c[c], k_b[_CHAINS[c][0]], _NT) for c in cs]
                da = [_dot(do_c[c], v_s[_CHAINS[c][0], rk, :], _NT) for c in cs]
                sp = [_softplus(z[c]) for c in cs]
                lom = [keep(c, -sp[c]) for c in cs]
                lom_sum = [jnp.sum(lom[c], axis=1, keepdims=True) for c in cs]
                tail = [_split_dot(lom[c], after_s[...]) + (total[c] - carry[c][0] - lom_sum[c]) for c in cs]
                a = [keep(c, jnp.exp(z[c] - sp[c] + tail[c])) for c in cs]
                dla = [a[c] * da[c] for c in cs]
                for h in hs:
                    a_h = jnp.concatenate([a[c].astype(v_s.dtype) for c in cs if _CHAINS[c][0] == h], axis=0)
                    dvt_s[_head(h), rk] += _dot(dot_h[h], a_h)
                d_lom = [carry[c][1] + _split_dot(dla[c], before_s[...]) for c in cs]
                beta = [jnp.exp(z[c] - sp[c]) for c in cs]
                dz_b = [(dla[c] * (1.0 - beta[c]) - keep(c, beta[c] * d_lom[c])).astype(v_s.dtype) for c in cs]
                dq_acc = [carry[c][2] + _dot(dz_b[c], k_b[_CHAINS[c][0]]) for c in cs]
                for h in hs:
                    dz_h = jnp.concatenate([dz_b[c] for c in cs if _CHAINS[c][0] == h], axis=0)
                    dkt_s[_head(h), rk] += _dot(qt_h[h], dz_h)
                return tuple((carry[c][0] + lom_sum[c], carry[c][1] + jnp.sum(dla[c], axis=1, keepdims=True),
                              dq_acc[c]) for c in cs)

            zero = jnp.zeros((QBLOCK, 1), F32)
            init = (zero, zero, jnp.zeros((QBLOCK, HEAD_DIM), F32))
            before = lax.fori_loop(0, p, lambda kb, carry: k_block(kb, carry, False), (init,) * len(_CHAINS))
            res = k_block(p, before, True)
            for r in range(Q_HALVES):
                dq_ref[rows[r], :] = jnp.concatenate([res[c][2] for c in cs if _CHAINS[c][1] == r], axis=1)
            return 0

        lax.fori_loop(0, n_pairs, q_pair, 0)

        def transpose_out(i, _):
            r = pl.ds(pl.multiple_of(i * QBLOCK, QBLOCK), QBLOCK)
            dk_ref[r, :] = dkt_s[:, r].T
            dv_ref[r, :] = dvt_s[:, r].T.astype(dv_ref.dtype)
            return 0

        lax.fori_loop(0, L // QBLOCK, transpose_out, 0)

    spec = pl.BlockSpec((L, LANES), lambda b, p: (b, p))
    return pl.pallas_call(
        body, name="attn_bwd", grid=(B, n_hp),
        in_specs=[spec] * 5, out_specs=[spec] * 3,
        out_shape=[jax.ShapeDtypeStruct((B * L, SB_WIDTH), F32)] * 2 + [jax.ShapeDtypeStruct((B * L, SB_WIDTH), BF16)],
        scratch_shapes=[pltpu.VMEM((2, L, HEAD_DIM), BF16)] * 3 + [pltpu.VMEM((LANES, L), BF16)]
        + [pltpu.VMEM((LANES, L), F32)] * 2 + [pltpu.VMEM((KBLOCK, KBLOCK), BF16)] * 2,
        compiler_params=_params(("parallel", "parallel")),
    )(qn, kn, vb, rtot, d_sb)


def _ssm_discretise(lam_re, lam_im, log_dt, b_re, b_im):
    dt = jnp.exp(log_dt)
    mag = jnp.exp(lam_re * dt)
    lbr = mag * jnp.cos(lam_im * dt)
    lbi = mag * jnp.sin(lam_im * dt)
    den = lam_re * lam_re + lam_im * lam_im
    nr, ni = lbr - 1.0, lbi
    cr = (nr * lam_re + ni * lam_im) / den
    ci = (ni * lam_re - nr * lam_im) / den
    return lbr, lbi, cr * b_re - ci * b_im, cr * b_im + ci * b_re


def _ssm_prep(lam_re, lam_im, log_dt, b_re_t, b_im_t):
    def body(lr, li, ld, br, bi, o_lr, o_li, o_br, o_bi):
        res = _ssm_discretise(lr[...], li[...], ld[...], br[...], bi[...])
        for o, v in zip((o_lr, o_li, o_br, o_bi), res):
            o[...] = v

    return pl.pallas_call(
        body, name="ssm_prep",
        out_shape=[jax.ShapeDtypeStruct(lam_re.shape, F32)] * 2 + [jax.ShapeDtypeStruct(b_re_t.shape, F32)] * 2,
    )(lam_re, lam_im, log_dt, b_re_t, b_im_t)


def _ssm_prep_bwd(lam_re, lam_im, log_dt, b_re_t, b_im_t, d_lr, d_li, d_br, d_bi):
    def body(lr, li, ld, br, bi, g_lr, g_li, g_br, g_bi, o_lr, o_li, o_ld, o_br, o_bi):
        _, vjp = jax.vjp(_ssm_discretise, lr[...], li[...], ld[...], br[...], bi[...])
        res = vjp((g_lr[...], g_li[...], g_br[...], g_bi[...]))
        for o, v in zip((o_lr, o_li, o_ld, o_br, o_bi), res):
            o[...] = v

    return pl.pallas_call(
        body, name="ssm_prep_bwd",
        out_shape=[jax.ShapeDtypeStruct(lam_re.shape, F32)] * 2 + [jax.ShapeDtypeStruct(log_dt.shape, F32)]
        + [jax.ShapeDtypeStruct(b_re_t.shape, F32)] * 2,
    )(lam_re, lam_im, log_dt, b_re_t, b_im_t, d_lr, d_li, d_br, d_bi)


def _block_diag(m):
    m4 = m.reshape(SSM_COLS, 8, SSM_GROUP, SSM_STATE)
    return jnp.einsum("aghp,gk->aghkp", m4, jnp.eye(8, dtype=m.dtype)).reshape(SSM_COLS, LANES, 512)


def _block_diag_take(d):
    d6 = d.reshape(SSM_COLS, 8, SSM_GROUP, 2, 8, SSM_STATE)
    return jnp.einsum("aghrgp->raghp", d6).reshape(2, SSM_GROUPS, SSM_GROUP, SSM_STATE)


def _cmul(ar, ai, br, bi):
    return ar * br - ai * bi, ar * bi + ai * br


def _power(lr, li, n):
    assert n & (n - 1) == 0
    for _ in range(n.bit_length() - 1):
        lr, li = _cmul(lr, li, lr, li)
    return lr, li


def _ssm_fwd(u_p, w_b, lam_r, lam_i, c_m, d_skip, B, L, tj):
    J = L // N_CHUNK
    njt = J // tj
    R = tj * N_CHUNK
    H = 512

    def body(u_ref, wb_ref, lr_ref, li_ref, cm_ref, d_ref, y_ref, x_ref, xin_ref, bu_s, st_s, xin_s):
        ph, jt = pl.program_id(2), pl.program_id(3)
        lr, li = lr_ref[...], li_ref[...]

        @pl.when(jnp.logical_and(ph == 0, jt == 0))
        def _():
            st_s[...] = jnp.zeros_like(st_s)

        bu_s[...] = _dot(u_ref[...].astype(BF16), wb_ref[...].astype(BF16))

        def scan(store):
            def step(j, carry):
                xr, xi = carry
                r = pl.ds(pl.multiple_of(j * N_CHUNK, N_CHUNK), N_CHUNK)
                nr = lr * xr - li * xi + bu_s[r, 0:H]
                ni = lr * xi + li * xr + bu_s[r, H:2 * H]
                if store:
                    x_ref[r, 0:H] = nr
                    x_ref[r, H:2 * H] = ni
                return nr, ni

            xr, xi = lax.fori_loop(0, tj, step, (st_s[:, 0:H], st_s[:, H:2 * H]))
            st_s[:, 0:H] = xr
            st_s[:, H:2 * H] = xi

        @pl.when(ph == 0)
        def _():
            scan(False)

            @pl.when(jt == njt - 1)
            def _():
                pr, pi = _power(lr[0:1], li[0:1], J)
                xin_s[0:1, :] = jnp.zeros((1, 2 * H), F32)
                for c in range(1, N_CHUNK):
                    qr, qi = _cmul(pr, pi, xin_s[c - 1:c, 0:H], xin_s[c - 1:c, H:2 * H])
                    xin_s[c:c + 1, 0:H] = qr + st_s[c - 1:c, 0:H]
                    xin_s[c:c + 1, H:2 * H] = qi + st_s[c - 1:c, H:2 * H]
                xin_ref[...] = xin_s[...]
                st_s[...] = xin_s[...]

        @pl.when(ph == 1)
        def _():
            scan(True)
            y = _dot(x_ref[...].astype(BF16), cm_ref[...].astype(BF16))
            y_ref[...] = y + d_ref[...] * u_ref[...]

    return pl.pallas_call(
        body, name="ssm_fwd", grid=(SSM_COLS, B, 2, njt),
        in_specs=[
            pl.BlockSpec((None, R, LANES), lambda i, b, ph, jt: (b, jt, i)),
            pl.BlockSpec((None, LANES, 2 * H), lambda i, b, ph, jt: (i, 0, 0)),
            pl.BlockSpec((None, N_CHUNK, H), lambda i, b, ph, jt: (i, 0, 0)),
            pl.BlockSpec((None, N_CHUNK, H), lambda i, b, ph, jt: (i, 0, 0)),
            pl.BlockSpec((None, 2 * H, LANES), lambda i, b, ph, jt: (i, 0, 0)),
            pl.BlockSpec((1, LANES), lambda i, b, ph, jt: (0, i)),
        ],
        out_specs=[
            pl.BlockSpec((None, R, LANES), lambda i, b, ph, jt: (b, jt * ph, i)),
            pl.BlockSpec((None, R, 2 * H), lambda i, b, ph, jt: (b, jt * ph, i)),
            pl.BlockSpec((None, None, N_CHUNK, 2 * H), lambda i, b, ph, jt: (b, i, 0, 0)),
        ],
        out_shape=[
            jax.ShapeDtypeStruct((B, L, SSM_WIDTH), F32),
            jax.ShapeDtypeStruct((B, L, SSM_COLS * 2 * H), F32),
            jax.ShapeDtypeStruct((B, SSM_COLS, N_CHUNK, 2 * H), F32),
        ],
        scratch_shapes=[pltpu.VMEM((R, 2 * H), F32), pltpu.VMEM((N_CHUNK, 2 * H), F32), pltpu.VMEM((N_CHUNK, 2 * H), F32)],
        compiler_params=_params(("arbitrary",) * 4),
    )(u_p, w_b, lam_r, lam_i, c_m, d_skip)


def _ssm_bwd(dy_p, u_p, x, xin, w_bt, lam_r, lam_i, c_mt, d_skip, B, L, tj):
    J = L // N_CHUNK
    njt = J // tj
    R = tj * N_CHUNK
    H = 512
    x4 = x.reshape(B, J, N_CHUNK, SSM_COLS * 2 * H)

    def body(dy_ref, u_ref, x_ref, xp_ref, xin_ref, wbt_ref, lr_ref, li_ref, cmt_ref, d_ref,
             du_ref, dwb_ref, dcm_ref, dlr_ref, dli_ref, dd_ref, ca_s, a_s, st_s, dl_s):
        b, ph, jt = pl.program_id(1), pl.program_id(2), pl.program_id(3)
        jr = njt - 1 - jt
        lr, li = lr_ref[...], -li_ref[...]

        @pl.when(jnp.logical_and(b == 0, jnp.logical_and(ph == 0, jt == 0)))
        def _():
            dwb_ref[...] = jnp.zeros_like(dwb_ref)
            dcm_ref[...] = jnp.zeros_like(dcm_ref)
            dlr_ref[...] = jnp.zeros_like(dlr_ref)
            dli_ref[...] = jnp.zeros_like(dli_ref)
            dd_ref[...] = jnp.zeros_like(dd_ref)
            dl_s[...] = jnp.zeros_like(dl_s)

        @pl.when(jnp.logical_and(ph == 0, jt == 0))
        def _():
            st_s[...] = jnp.zeros_like(st_s)

        ca_s[...] = _dot(dy_ref[...].astype(BF16), cmt_ref[...].astype(BF16))

        def scan(store):
            def step(n, carry):
                ar, ai = carry
                r = pl.ds(pl.multiple_of((tj - 1 - n) * N_CHUNK, N_CHUNK), N_CHUNK)
                nr = lr * ar - li * ai + ca_s[r, 0:H]
                ni = lr * ai + li * ar + ca_s[r, H:2 * H]
                if store:
                    a_s[r, 0:H] = nr
                    a_s[r, H:2 * H] = ni
                return nr, ni

            ar, ai = lax.fori_loop(0, tj, step, (st_s[:, 0:H], st_s[:, H:2 * H]))
            st_s[:, 0:H] = ar
            st_s[:, H:2 * H] = ai

        @pl.when(ph == 0)
        def _():
            scan(False)

            @pl.when(jt == njt - 1)
            def _():
                pr, pi = _power(lr[0:1], li[0:1], J)
                a_s[N_CHUNK - 1:N_CHUNK, :] = jnp.zeros((1, 2 * H), F32)
                for c in range(N_CHUNK - 2, -1, -1):
                    qr, qi = _cmul(pr, pi, a_s[c + 1:c + 2, 0:H], a_s[c + 1:c + 2, H:2 * H])
                    a_s[c:c + 1, 0:H] = qr + st_s[c + 1:c + 2, 0:H]
                    a_s[c:c + 1, H:2 * H] = qi + st_s[c + 1:c + 2, H:2 * H]
                st_s[...] = a_s[0:N_CHUNK, :]

        @pl.when(ph == 1)
        def _():
            scan(True)
            dy = dy_ref[...]
            u = u_ref[...]
            a_b = a_s[...].astype(BF16)
            du_ref[...] = (_dot(a_b, wbt_ref[...].astype(BF16)) + d_ref[...] * dy).astype(du_ref.dtype)
            dwb_ref[...] += _dot(u.astype(BF16), a_b, _TN)
            dcm_ref[...] += _dot(x_ref[...].astype(BF16), dy.astype(BF16), _TN)
            dd_ref[...] += jnp.sum(dy * u, axis=0, keepdims=True)

            first = jnp.where(jr == 0, xin_ref[...], xp_ref[...])
            a0r, a0i = a_s[0:N_CHUNK, 0:H], a_s[0:N_CHUNK, H:2 * H]
            acc0 = (a0r * first[:, 0:H] + a0i * first[:, H:2 * H], a0i * first[:, 0:H] - a0r * first[:, H:2 * H])

            def step(j, carry):
                sr, si = carry
                r = pl.ds(pl.multiple_of(j * N_CHUNK, N_CHUNK), N_CHUNK)
                rp = pl.ds(pl.multiple_of((j - 1) * N_CHUNK, N_CHUNK), N_CHUNK)
                ar, ai = a_s[r, 0:H], a_s[r, H:2 * H]
                xr, xi = x_ref[rp, 0:H], x_ref[rp, H:2 * H]
                return sr + ar * xr + ai * xi, si + ai * xr - ar * xi

            sr, si = lax.fori_loop(1, tj, step, acc0)
            dl_s[:, 0:H] += sr
            dl_s[:, H:2 * H] += si

            @pl.when(jnp.logical_and(b == B - 1, jt == njt - 1))
            def _():
                dlr_ref[...] = jnp.sum(dl_s[:, 0:H], axis=0, keepdims=True)
                dli_ref[...] = jnp.sum(dl_s[:, H:2 * H], axis=0, keepdims=True)
                dl_s[...] = jnp.zeros_like(dl_s)

    rev = lambda ph, jt: (njt - 1 - jt) * ph + (njt - 1) * (1 - ph)
    return pl.pallas_call(
        body, name="ssm_bwd", grid=(SSM_COLS, B, 2, njt),
        in_specs=[
            pl.BlockSpec((None, R, LANES), lambda i, b, ph, jt: (b, njt - 1 - jt, i)),
            pl.BlockSpec((None, R, LANES), lambda i, b, ph, jt: (b, njt - 1 - jt, i)),
            pl.BlockSpec((None, R, 2 * H), lambda i, b, ph, jt: (b, rev(ph, jt), i)),
            pl.BlockSpec((None, None, N_CHUNK, 2 * H),
                         lambda i, b, ph, jt: (b, jnp.maximum((njt - 1 - jt) * tj - 1, 0), 0, i)),
            pl.BlockSpec((None, None, N_CHUNK, 2 * H), lambda i, b, ph, jt: (b, i, 0, 0)),
            pl.BlockSpec((None, 2 * H, LANES), lambda i, b, ph, jt: (i, 0, 0)),
            pl.BlockSpec((None, N_CHUNK, H), lambda i, b, ph, jt: (i, 0, 0)),
            pl.BlockSpec((None, N_CHUNK, H), lambda i, b, ph, jt: (i, 0, 0)),
            pl.BlockSpec((None, LANES, 2 * H), lambda i, b, ph, jt: (i, 0, 0)),
            pl.BlockSpec((1, LANES), lambda i, b, ph, jt: (0, i)),
        ],
        out_specs=[
            pl.BlockSpec((None, R, LANES), lambda i, b, ph, jt: (b, rev(ph, jt), i)),
            pl.BlockSpec((None, LANES, 2 * H), lambda i, b, ph, jt: (i, 0, 0)),
            pl.BlockSpec((None, 2 * H, LANES), lambda i, b, ph, jt: (i, 0, 0)),
            pl.BlockSpec((None, 1, H), lambda i, b, ph, jt: (i, 0, 0)),
            pl.BlockSpec((None, 1, H), lambda i, b, ph, jt: (i, 0, 0)),
            pl.BlockSpec((1, LANES), lambda i, b, ph, jt: (0, i)),
        ],
        out_shape=[
            jax.ShapeDtypeStruct((B, L, SSM_WIDTH), BF16),
            jax.ShapeDtypeStruct((SSM_COLS, LANES, 2 * H), F32),
            jax.ShapeDtypeStruct((SSM_COLS, 2 * H, LANES), F32),
            jax.ShapeDtypeStruct((SSM_COLS, 1, H), F32),
            jax.ShapeDtypeStruct((SSM_COLS, 1, H), F32),
            jax.ShapeDtypeStruct((1, SSM_WIDTH), F32),
        ],
        scratch_shapes=[pltpu.VMEM((R, 2 * H), F32), pltpu.VMEM((R, 2 * H), F32),
                        pltpu.VMEM((N_CHUNK, 2 * H), F32), pltpu.VMEM((N_CHUNK, 2 * H), F32)],
        compiler_params=_params(("arbitrary",) * 4),
    )(dy_p, u_p, x, x4, xin, w_bt, lam_r, lam_i, c_mt, d_skip)


def _to_scan_layout(t, B, L):
    C = t.shape[-1]
    return t.reshape(B, N_CHUNK, L // N_CHUNK, C).transpose(0, 2, 1, 3).reshape(B, L, C)


def _from_scan_layout(t, B, L):
    C = t.shape[-1]
    return t.reshape(B, L // N_CHUNK, N_CHUNK, C).transpose(0, 2, 1, 3).reshape(B * L, C)


def _local_step(x, target, p, w_in, late_weights, mlp_grads_ready=None, rest_grads_ready=None, order=None, *,
                ssm_tile=128):
    B, L, D = x.shape
    T = B * L
    x2 = x.reshape(T, D)
    row = lambda v: v.reshape(1, -1)
    g1, g2, ga, gs, b_glu = row(p["norm1_g"]), row(p["norm2_g"]), row(p["attn_out_g"]), row(p["ssm_out_g"]), row(p["b_glu"])
    g1_first = g1 if order is None else g1 + order
    gq8 = jnp.tile(row(p["q_norm_g"]), (1, N_HEADS))
    gk8 = jnp.tile(row(p["k_norm_g"]), (1, N_HEADS))

    G, P, Hh = SSM_GROUPS, SSM_STATE, SSM_GROUP
    lam_re3, lam_im3 = p["ssm_lambda_re"].reshape(G, 1, P), p["ssm_lambda_im"].reshape(G, 1, P)
    log_dt3 = p["ssm_log_dt"].reshape(G, 1, 1)
    b_re_t, b_im_t = p["ssm_b_re"].transpose(0, 2, 1), p["ssm_b_im"].transpose(0, 2, 1)
    lbr, lbi, bbr, bbi = _ssm_prep(lam_re3, lam_im3, log_dt3, b_re_t, b_im_t)
    w_b = jnp.concatenate([_block_diag(bbr), _block_diag(bbi)], axis=2)
    c_mt = jnp.concatenate([_block_diag(p["ssm_c_re"]), -_block_diag(p["ssm_c_im"])], axis=2)
    w_bt, c_m = w_b.transpose(0, 2, 1), c_mt.transpose(0, 2, 1)
    lam_r = jnp.broadcast_to(lbr.reshape(SSM_COLS, 1, 512), (SSM_COLS, N_CHUNK, 512))
    lam_i = jnp.broadcast_to(lbi.reshape(SSM_COLS, 1, 512), (SSM_COLS, N_CHUNK, 512))
    d_skip = p["ssm_d"].reshape(1, SSM_WIDTH)

    xn = _rowwise("norm1", _rms, [x2], [g1_first], [(D, BF16)])
    proj = _matmul("proj", xn, w_in, tn=1024)
    qn, kn, vb = _qk_norm(proj, gq8, gk8)
    sb, rtot = _attn_fwd(qn, kn, vb, B, L)
    u_p = _to_scan_layout(proj[:, 3 * SB_WIDTH:], B, L)
    y_p, xs, xin = _ssm_fwd(u_p, w_b, lam_r, lam_i, c_m, d_skip, B, L, ssm_tile)
    y2 = y_p.reshape(T, SSM_WIDTH)
    gel = _rowwise("gelu", jax.nn.gelu, [y2], [], [(SSM_WIDTH, BF16)])
    w_glu, w_out, w_mlp_in, w_mlp_out = late_weights(gel)
    pre = _matmul("glu_gate", gel, w_glu)
    ssm_n = _rowwise("glu_out", _glu_branch, [y2, pre], [b_glu, gs], [(SSM_WIDTH, BF16)])
    sb_n = _rowwise("attn_out_norm", _rms, [sb], [ga], [(SB_WIDTH, BF16)])
    mixed = jnp.concatenate([sb_n, _from_scan_layout(ssm_n, B, L)], axis=1)
    h1 = _matmul("out_proj", mixed, w_out, extras=[x2], epilogue=lambda acc, r: (acc + r,), tn=1024)
    hn = _rowwise("norm2", _rms, [h1], [g2], [(D, BF16)])
    act, a_pre = _matmul("mlp_in", hn, w_mlp_in, out_dtypes=(BF16, BF16), tn=1024,
                         epilogue=lambda acc: (jnp.square(jnp.maximum(acc, 0.0)), acc))
    out = _matmul("mlp_out", act, w_mlp_out, extras=[h1], epilogue=lambda acc, r: (acc + r,))

    def loss_fn(o, t):
        diff = o - t
        part = jnp.sum(jnp.sum(diff * diff, axis=0, keepdims=True), axis=1, keepdims=True)
        d = diff * (1.0 / D)
        return d, d, part * (0.5 / D)

    d_out, d_out_b, loss = _rowwise("loss", loss_fn, [out, target.reshape(T, D)], [], [(D, F32), (D, BF16)],
                                    sums=[(1, 1)])

    d_apre = _matmul("mlp_out_dx", d_out_b, w_mlp_out, tb=True, extras=[a_pre], out_dtypes=(BF16,), tn=1024,
                     epilogue=lambda acc, ap: (acc * (2.0 * jnp.maximum(ap.astype(F32), 0.0)),))
    both = lambda acc: (acc, acc)
    g_w_mlp_out, g_w_mlp_out_b = _matmul("mlp_out_dw", act, d_out_b, ta=True, out_dtypes=(F32, BF16), epilogue=both)
    g_w_mlp_in, g_w_mlp_in_b = _matmul("mlp_in_dw", hn, d_apre, ta=True, col_blocked=True, out_dtypes=(F32, BF16),
                                       epilogue=both, tn=w_mlp_in.shape[1] // N_DEV)
    if mlp_grads_ready is not None:
        g2 = g2 + mlp_grads_ready(g_w_mlp_out, g_w_mlp_out_b, g_w_mlp_in, g_w_mlp_in_b)
    d_hn = _matmul("mlp_in_dx", d_apre, w_mlp_in, tb=True)

    def norm_bwd_res(h, dy, res, g):
        _, vjp = jax.vjp(_rms, h, g)
        dh, dg = vjp(dy)
        return res + dh, dg

    def norm_bwd_res2(h, dy, res, g):
        d, dg = norm_bwd_res(h, dy, res, g)
        return d, d, dg

    d_h1, d_h1_b, g_norm2 = _rowwise("norm2_bwd", norm_bwd_res2, [h1, d_hn, d_out], [g2], [(D, F32), (D, BF16)],
                                     sums=[(1, D)])

    d_mixed = _matmul("out_proj_dx", d_h1_b, w_out, tb=True, tn=1024)
    g_w_out, g_w_out_b = _matmul("out_proj_dw", mixed, d_h1_b, ta=True, out_dtypes=(F32, BF16), epilogue=both)

    def norm_bwd(h, dy, g):
        _, vjp = jax.vjp(_rms, h, g)
        return vjp(dy)

    d_sb, g_attn_out = _rowwise("attn_out_norm_bwd", norm_bwd, [sb, (d_mixed, 0, SB_WIDTH)], [ga],
                                [(SB_WIDTH, F32)], sums=[(1, SB_WIDTH)])
    d_ssm_n = _to_scan_layout(d_mixed[:, SB_WIDTH:], B, L).reshape(T, SSM_WIDTH)

    def glu_bwd(y, pre_, dy, bg, g):
        _, vjp = jax.vjp(_glu_branch, y, pre_, bg, g)
        d_y, d_pre, d_bg, d_g = vjp(dy)
        return d_y, d_pre, d_bg, d_g

    d_y_direct, d_pre, g_b_glu, g_ssm_out = _rowwise(
        "glu_out_bwd", glu_bwd, [y2, pre, d_ssm_n], [b_glu, gs], [(SSM_WIDTH, F32), (SSM_WIDTH, BF16)],
        sums=[(1, SSM_WIDTH), (1, SSM_WIDTH)])
    d_gel = _matmul("glu_gate_dx", d_pre, w_glu, tb=True)
    g_w_glu, g_w_glu_b = _matmul("glu_gate_dw", gel, d_pre, ta=True, out_dtypes=(F32, BF16), epilogue=both)

    def gelu_bwd(y, dg, dy0):
        _, vjp = jax.vjp(jax.nn.gelu, y)
        return dy0 + vjp(dg)[0]

    d_y = _rowwise("gelu_bwd", gelu_bwd, [y2, d_gel, d_y_direct], [], [(SSM_WIDTH, F32)])

    du_p, d_wb, d_cm, d_lr, d_li, g_d = _ssm_bwd(
        d_y.reshape(B, L, SSM_WIDTH), u_p, xs, xin, w_bt, lam_r, lam_i, c_mt, d_skip, B, L, ssm_tile)
    d_bb = _block_diag_take(d_wb.reshape(SSM_COLS, LANES, 2, 512))
    d_c = _block_diag_take(d_cm.transpose(0, 2, 1).reshape(SSM_COLS, LANES, 2, 512))
    g_lam_re, g_lam_im, g_log_dt, g_b_re_t, g_b_im_t = _ssm_prep_bwd(
        lam_re3, lam_im3, log_dt3, b_re_t, b_im_t,
        d_lr.reshape(G, 1, P), d_li.reshape(G, 1, P), d_bb[0], d_bb[1])
    d_qn, d_kn, d_v = _attn_bwd(qn, kn, vb, rtot, d_sb, B, L)
    d_q, d_k, g_q, g_k = _qk_norm_bwd(proj, gq8, gk8, d_qn, d_kn)

    d_proj = jnp.concatenate([d_q, d_k, d_v, _from_scan_layout(du_p, B, L)], axis=1)
    g_w_in, g_w_in_b = _matmul("proj_dw", xn, d_proj, ta=True, col_blocked=True, out_dtypes=(F32, BF16),
                               epilogue=both, tn=w_in.shape[1] // N_DEV)
    if rest_grads_ready is not None:
        g1 = g1 + rest_grads_ready([g_w_in, g_w_glu, g_w_out], [g_w_in_b, g_w_glu_b, g_w_out_b])
    d_xn = _matmul("proj_dx", d_proj, w_in, tb=True, tn=1024)
    grad_x, g_norm1 = _rowwise("norm1_bwd", norm_bwd_res, [x2, d_xn, d_h1], [g1], [(D, F32)], sums=[(1, D)])

    small = {
        "norm1_g": g_norm1.reshape(-1),
        "q_norm_g": g_q.reshape(-1),
        "k_norm_g": g_k.reshape(-1),
        "ssm_lambda_re": g_lam_re.reshape(G, P),
        "ssm_lambda_im": g_lam_im.reshape(G, P),
        "ssm_log_dt": g_log_dt.reshape(G),
        "ssm_b_re": g_b_re_t.transpose(0, 2, 1),
        "ssm_b_im": g_b_im_t.transpose(0, 2, 1),
        "ssm_c_re": d_c[0],
        "ssm_c_im": -d_c[1],
        "ssm_d": g_d.reshape(G, Hh),
        "b_glu": g_b_glu.reshape(-1),
        "attn_out_g": g_attn_out.reshape(-1),
        "ssm_out_g": g_ssm_out.reshape(-1),
        "norm2_g": g_norm2.reshape(-1),
    }
    big = {"w_in": g_w_in, "w_glu": g_w_glu, "w_out": g_w_out, "w_mlp_in": g_w_mlp_in, "w_mlp_out": g_w_mlp_out}
    return loss[0, 0], grad_x.reshape(B, L, D), small, big


_ANY = pl.BlockSpec(memory_space=pl.ANY)
_MESH = pl.DeviceIdType.MESH


def _all_gather(name, shards):
    n = len(shards)

    def body(*refs):
        in_refs, out_refs = refs[:n], refs[n:2 * n]
        send_sems, recv_sems, local_sems = refs[2 * n:]
        x, y, c = lax.axis_index("x"), lax.axis_index("y"), lax.axis_index("c")
        me, sibling = (x, y, c), (x, y, 1 - c)
        chips = [(1 - x, y), (x, 1 - y), (1 - x, 1 - y)]

        def copy(a, k, block, to, src=None):
            px, py, pc = block
            rows = out_refs[a].at[4 * px + 2 * py + pc]
            return pltpu.make_async_remote_copy(
                src_ref=rows if src is None else src, dst_ref=rows, send_sem=send_sems.at[a, k],
                recv_sem=recv_sems.at[a, k], device_id=to, device_id_type=_MESH)

        mine = [pltpu.make_async_copy(in_refs[a], out_refs[a].at[4 * x + 2 * y + c], local_sems.at[a]) for a in range(n)]
        first, passed = [], []
        for a in range(n):
            mine[a].start()
            first.append(copy(a, 0, me, sibling, src=in_refs[a]))
            first += [copy(a, 1 + j, me, (*chip, c), src=in_refs[a]) for j, chip in enumerate(chips)]
        for cp in first:
            cp.start()
        for j, chip in enumerate(chips):
            for a in range(n):
                copy(a, 1 + j, (*chip, c), me).wait_recv()
                fwd = copy(a, 4 + j, (*chip, c), sibling)
                fwd.start()
                passed.append(fwd)
        for a in range(n):
            copy(a, 0, sibling, me).wait_recv()
            for j, chip in enumerate(chips):
                copy(a, 4 + j, (*chip, 1 - c), me).wait_recv()
        for cp in first + passed:
            cp.wait_send()
        for cp in mine:
            cp.wait()

    return pl.pallas_call(
        body, name=name,
        in_specs=[_ANY] * n, out_specs=[_ANY] * n,
        out_shape=[jax.ShapeDtypeStruct((N_DEV, *s.shape), s.dtype) for s in shards],
        scratch_shapes=[pltpu.SemaphoreType.DMA((n, 7)), pltpu.SemaphoreType.DMA((n, 7)), pltpu.SemaphoreType.DMA((n,))],
    )(*shards)


_HBM = pl.BlockSpec(memory_space=pltpu.HBM)
_SEM = pl.BlockSpec(memory_space=pltpu.SEMAPHORE)
_EFFECT = pltpu.SideEffectType.DATAFLOW_SIDE_EFFECTING
_FLIPS = [(dx, dy, dc) for dx in (0, 1) for dy in (0, 1) for dc in (0, 1) if (dx, dy, dc) != (0, 0, 0)]


def _exchange_start(name, srcs, lands, per_peer):
    n = len(srcs)

    def body(*refs):
        src_refs, land_refs = refs[:n], refs[n:2 * n]
        send_sems, recv_sems = refs[2 * n:3 * n], refs[3 * n:4 * n]
        token = refs[-1]
        x, y, c = lax.axis_index("x"), lax.axis_index("y"), lax.axis_index("c")
        me = 4 * x + 2 * y + c
        for dx, dy, dc in _FLIPS:
            px, py, pc = (1 - x if dx else x), (1 - y if dy else y), (1 - c if dc else c)
            for a in range(n):
                pltpu.make_async_remote_copy(
                    src_ref=src_refs[a].at[4 * px + 2 * py + pc] if per_peer else src_refs[a],
                    dst_ref=land_refs[a].at[me], send_sem=send_sems[a], recv_sem=recv_sems[a],
                    device_id=(px, py, pc), device_id_type=_MESH).start()
        token[...] = jnp.zeros_like(token)

    hbm = lambda t: pltpu.with_memory_space_constraint(t, pltpu.HBM)
    res = pl.pallas_call(
        body, name=name,
        out_shape=(*[pltpu.SemaphoreType.DMA(())] * (2 * n), *[pltpu.HBM(t.shape, t.dtype) for t in (*srcs, *lands)],
                   jax.ShapeDtypeStruct((8, LANES), F32)),
        in_specs=[_HBM] * (2 * n),
        out_specs=(*[_SEM] * (2 * n), *[_HBM] * (2 * n), pl.BlockSpec(memory_space=pltpu.VMEM)),
        input_output_aliases={i: 2 * n + i for i in range(2 * n)},
        compiler_params=pltpu.CompilerParams(has_side_effects=_EFFECT),
    )(*[hbm(t) for t in (*srcs, *lands)])
    return res[:-1], res[-1]


def _exchange_wait(name, handle, after):
    n = len(handle) // 4
    sems, thru = handle[:2 * n], handle[2 * n:]

    def body(*refs):
        land_refs = refs[n:2 * n]
        send_sems, recv_sems = refs[2 * n:3 * n], refs[3 * n:4 * n]
        me = (lax.axis_index("x"), lax.axis_index("y"), lax.axis_index("c"))
        for a in range(n):
            seven = land_refs[a].at[pl.ds(0, len(_FLIPS))]
            all_copies = pltpu.make_async_remote_copy(
                src_ref=seven, dst_ref=seven, send_sem=send_sems[a], recv_sem=recv_sems[a], device_id=me,
                device_id_type=_MESH)
            all_copies.wait_send()
            all_copies.wait_recv()

    res = pl.pallas_call(
        body, name=name, out_shape=tuple(pltpu.HBM(t.shape, t.dtype) for t in thru),
        in_specs=[*[_HBM] * (2 * n), *[_SEM] * (2 * n), _ANY], out_specs=tuple([_HBM] * (2 * n)),
        input_output_aliases={i: i for i in range(2 * n)},
        compiler_params=pltpu.CompilerParams(has_side_effects=_EFFECT),
    )(*thru, *sems, after)
    return res[n:]


def _adamw_gathered(name, own, parts, me, w, m, v):
    r, c = w.shape
    tr = min(r, 256)

    def body(me_ref, own_ref, p_ref, w_ref, m_ref, v_ref, g_out, d_out, m_out, v_out):
        g = own_ref[...]
        for j in range(N_DEV):
            g = g + p_ref[j].astype(F32)
        delta, m_new, v_new = _adamw(w_ref[...], g, m_ref[...], v_ref[...])
        g_out[...] = g
        d_out[...] = delta
        m_out[...] = m_new
        v_out[...] = v_new

    spec = pl.BlockSpec((tr, c), lambda i, me_ref: (i, 0))
    return pl.pallas_call(
        body, name=name,
        grid_spec=pltpu.PrefetchScalarGridSpec(
            num_scalar_prefetch=1, grid=(r // tr,),
            in_specs=[pl.BlockSpec((None, tr, c), lambda i, me_ref: (me_ref[0], i, 0)),
                      pl.BlockSpec((N_DEV, tr, c), lambda i, me_ref: (0, i, 0)), spec, spec, spec],
            out_specs=[spec] * 4),
        out_shape=[jax.ShapeDtypeStruct((r, c), F32)] * 4,
        compiler_params=_params(("parallel",)),
    )(me, own, parts, w, m, v)


def _adamw(w, g, m, v):
    m = ADAM_B1 * m + (1.0 - ADAM_B1) * g
    v = ADAM_B2 * v + (1.0 - ADAM_B2) * jnp.square(g)
    m_hat = m / (1.0 - ADAM_B1 ** ADAM_STEP)
    v_hat = v / (1.0 - ADAM_B2 ** ADAM_STEP)
    delta = -ADAM_LR * (m_hat / (jnp.sqrt(v_hat) + ADAM_EPS) + ADAM_WD * w)
    return delta, m, v


def _adamw_small(name, parts, w, m, v):
    _, r, c = parts.shape
    tr = 8

    def body(p_ref, w_ref, m_ref, v_ref, g_out, d_out, m_out, v_out):
        g = p_ref[0]
        for j in range(1, N_DEV):
            g = g + p_ref[j]
        delta, m_new, v_new = _adamw(w_ref[...], g, m_ref[...], v_ref[...])
        g_out[...] = g
        d_out[...] = delta
        m_out[...] = m_new
        v_out[...] = v_new

    spec = pl.BlockSpec((tr, c), lambda i: (i, 0))
    return pl.pallas_call(
        body, name=name, grid=(r // tr,),
        in_specs=[pl.BlockSpec((N_DEV, tr, c), lambda i: (0, i, 0)), spec, spec, spec],
        out_specs=[spec] * 4, out_shape=[jax.ShapeDtypeStruct((r, c), F32)] * 4,
        compiler_params=_params(("parallel",)),
    )(parts, w, m, v)


_WEIGHTS = ["norm1_g", "w_in", "q_norm_g", "k_norm_g", "ssm_lambda_re", "ssm_lambda_im", "ssm_log_dt", "ssm_b_re",
            "ssm_b_im", "ssm_c_re", "ssm_c_im", "ssm_d", "w_glu", "b_glu", "attn_out_g", "ssm_out_g", "w_out",
            "norm2_g", "w_mlp_in", "w_mlp_out"]
_BIG = ["w_in", "w_glu", "w_out", "w_mlp_in", "w_mlp_out"]
_SMALL = [n for n in _WEIGHTS if n not in _BIG]
_PACK_COLS = 1024


def _pack(tree, last=None):
    flat = [tree[n].reshape(-1).astype(F32) for n in _SMALL]
    size = sum(f.shape[0] for f in flat)
    rows = -(-(size + 1) // (_PACK_COLS * 8)) * 8
    pad = jnp.zeros((rows * _PACK_COLS - size - 1,), F32)
    tail = jnp.zeros((1,), F32) if last is None else last.reshape(1).astype(F32)
    return jnp.concatenate(flat + [pad, tail]).reshape(rows, _PACK_COLS)


def _unpack(buf, like):
    flat, out, off = buf.reshape(-1), {}, 0
    for n in _SMALL:
        size = like[n].size
        out[n] = flat[off:off + size].reshape(like[n].shape)
        off += size
    return out


def kernel(x, norm1_g, w_in, q_norm_g, k_norm_g, ssm_lambda_re, ssm_lambda_im, ssm_log_dt, ssm_b_re, ssm_b_im, ssm_c_re, ssm_c_im, ssm_d, w_glu, b_glu, attn_out_g, ssm_out_g, w_out, norm2_g, w_mlp_in, w_mlp_out, loss_target, m_norm1_g, m_w_in, m_q_norm_g, m_k_norm_g, m_ssm_lambda_re, m_ssm_lambda_im, m_ssm_log_dt, m_ssm_b_re, m_ssm_b_im, m_ssm_c_re, m_ssm_c_im, m_ssm_d, m_w_glu, m_b_glu, m_attn_out_g, m_ssm_out_g, m_w_out, m_norm2_g, m_w_mlp_in, m_w_mlp_out, v_norm1_g, v_w_in, v_q_norm_g, v_k_norm_g, v_ssm_lambda_re, v_ssm_lambda_im, v_ssm_log_dt, v_ssm_b_re, v_ssm_b_im, v_ssm_c_re, v_ssm_c_im, v_ssm_d, v_w_glu, v_b_glu, v_attn_out_g, v_ssm_out_g, v_w_out, v_norm2_g, v_w_mlp_in, v_w_mlp_out):
    w = dict(zip(_WEIGHTS, (norm1_g, w_in, q_norm_g, k_norm_g, ssm_lambda_re, ssm_lambda_im, ssm_log_dt, ssm_b_re, ssm_b_im, ssm_c_re, ssm_c_im, ssm_d, w_glu, b_glu, attn_out_g, ssm_out_g, w_out, norm2_g, w_mlp_in, w_mlp_out)))
    m = dict(zip(_WEIGHTS, (m_norm1_g, m_w_in, m_q_norm_g, m_k_norm_g, m_ssm_lambda_re, m_ssm_lambda_im, m_ssm_log_dt, m_ssm_b_re, m_ssm_b_im, m_ssm_c_re, m_ssm_c_im, m_ssm_d, m_w_glu, m_b_glu, m_attn_out_g, m_ssm_out_g, m_w_out, m_norm2_g, m_w_mlp_in, m_w_mlp_out)))
    v = dict(zip(_WEIGHTS, (v_norm1_g, v_w_in, v_q_norm_g, v_k_norm_g, v_ssm_lambda_re, v_ssm_lambda_im, v_ssm_log_dt, v_ssm_b_re, v_ssm_b_im, v_ssm_c_re, v_ssm_c_im, v_ssm_d, v_w_glu, v_b_glu, v_attn_out_g, v_ssm_out_g, v_w_out, v_norm2_g, v_w_mlp_in, v_w_mlp_out)))
    core = lax.axis_index("c").astype(jnp.int32).reshape(1)
    chip = (2 * lax.axis_index("x") + lax.axis_index("y")).astype(jnp.int32).reshape(1)

    me = (2 * chip + core).astype(jnp.int32)

    def landing(own=None, like=None):
        own = jnp.zeros_like(like) if own is None else own
        return lax.dynamic_update_slice(lax.empty((N_DEV, *like.shape), like.dtype), own[None], (me[0], 0, 0))

    (w_in_blocks,) = _all_gather("w_in_all_gather", [w_in.astype(BF16)])
    w_in_full = w_in_blocks.transpose(1, 0, 2).reshape(w_in.shape[0], -1)
    late = [n for n in _BIG if n != "w_in"]
    shards = [w[n].astype(BF16) for n in late]
    w_in_blocks, shards = lax.optimization_barrier((w_in_blocks, shards))
    weights_handle, weights_token = _exchange_start(
        "weights_send", shards, [landing(s, s) for s in shards], per_peer=False)

    def late_weights(after):
        got = dict(zip(late, _exchange_wait("weights_arrive", weights_handle, after)))
        return (got["w_glu"].reshape(-1, w_glu.shape[1]), got["w_out"].reshape(-1, w_out.shape[1]),
                got["w_mlp_in"].transpose(1, 0, 2).reshape(w_mlp_in.shape[0], -1),
                got["w_mlp_out"].reshape(-1, w_mlp_out.shape[1]))

    mlp = ["w_mlp_out", "w_mlp_in"]
    sent = {}

    def send_grads(name, names, own, own_b):
        blocks = lambda g, n: g.reshape(N_DEV, *w[n].shape)
        sent[name + "_own"] = [blocks(g, n) for g, n in zip(own, names)]
        srcs = [blocks(g, n) for g, n in zip(own_b, names)]
        sent[name], token = _exchange_start(name, srcs, [landing(like=s[0]) for s in srcs], per_peer=True)
        return token[0, 0]

    def mlp_grads_ready(g_out, g_out_b, g_in, g_in_b):
        return send_grads("mlp_grads_send", mlp, [g_out, g_in], [g_out_b, g_in_b])

    rest = ["w_in", "w_glu", "w_out"]

    def rest_grads_ready(own, own_b):
        return send_grads("rest_grads_send", rest, own, own_b)

    loss_local, grad_x, g_small, g_big = _local_step(
        x, loss_target, {n: w[n] for n in _SMALL}, w_in_full, late_weights, mlp_grads_ready, rest_grads_ready,
        weights_token[0, 0])

    grads, delta, new_m, new_v = {}, {}, {}, {}
    (small_parts,) = _all_gather("small_grads_all_gather", [_pack(g_small, last=loss_local)])
    packed = _adamw_small("adamw_small", small_parts, _pack(w), _pack(m), _pack(v))
    for tree, buf in zip((grads, delta, new_m, new_v), packed):
        tree.update(_unpack(buf, w))
    loss = packed[0][-1, -1]

    for send, arrive, names, after in (("mlp_grads_send", "mlp_grads_arrive", mlp, grad_x),
                                       ("rest_grads_send", "rest_grads_arrive", rest, packed[0])):
        for n, own, part in zip(names, sent[send + "_own"], _exchange_wait(arrive, sent[send], after)):
            grads[n], delta[n], new_m[n], new_v[n] = _adamw_gathered("adamw_" + n, own, part, me, w[n], m[n], v[n])

    return (loss, grad_x, *[grads[n] for n in _WEIGHTS], *[delta[n] for n in _WEIGHTS],
            *[new_m[n] for n in _WEIGHTS], *[new_v[n] for n in _WEIGHTS])
```

```python
import functools
import math

import jax
import jax.numpy as jnp
from jax import lax
from jax.experimental import pallas as pl
from jax.experimental.pallas import tpu as pltpu

F32 = jnp.float32
BF16 = jnp.bfloat16

EPS = 1e-6
HEAD_DIM = 64
N_HEADS = 8
SB_WIDTH = 512
SSM_WIDTH = 512
SSM_GROUP = 16
SSM_GROUPS = 32
SSM_STATE = 64
QBLOCK = 128
KBLOCK = 256
N_CHUNK = 8
SSM_COLS = 4
LANES = 128
N_DEV = 8

ADAM_LR = 0.001
ADAM_B1 = 0.9
ADAM_B2 = 0.999
ADAM_EPS = 1e-08
ADAM_WD = 0.01
ADAM_STEP = 10

VMEM_LIMIT = 56 * 1024 * 1024

_NT = (((1,), (1,)), ((), ()))
_NN = (((1,), (0,)), ((), ()))
_TN = (((0,), (0,)), ((), ()))


def _dot(a, b, dims=_NN):
    return lax.dot_general(a, b, dims, preferred_element_type=F32)


def _params(sem):
    return pltpu.CompilerParams(dimension_semantics=sem, vmem_limit_bytes=VMEM_LIMIT)


def _matmul(name, a, b, *, ta=False, tb=False, extras=(), epilogue=None, out_dtypes=(F32,),
            col_blocked=False, tm=1024, tn=512, tk=4096):
    M, K = (a.shape[1], a.shape[0]) if ta else a.shape
    N = b.shape[0] if tb else b.shape[1]
    tm, tn, tk = min(tm, M), min(tn, N), min(tk, K)
    assert M % tm == 0 and N % tn == 0 and K % tk == 0, (name, M, N, K)
    nk = K // tk
    n_ex, n_out = len(extras), len(out_dtypes)
    dims = (((0 if ta else 1,), (1 if tb else 0,)), ((), ()))

    def body(*refs):
        a_ref, b_ref = refs[0], refs[1]
        ex_refs = refs[2:2 + n_ex]
        o_refs = refs[2 + n_ex:2 + n_ex + n_out]
        k = pl.program_id(2)
        part = _dot(a_ref[...].astype(BF16), b_ref[...].astype(BF16), dims)

        def finish(acc):
            outs = (acc,) if epilogue is None else epilogue(acc, *[e[...] for e in ex_refs])
            for o_ref, o in zip(o_refs, outs):
                o_ref[...] = o.astype(o_ref.dtype)

        if nk == 1:
            finish(part)
        else:
            acc_ref = refs[-1]

            @pl.when(k == 0)
            def _():
                acc_ref[...] = part

            @pl.when(jnp.logical_and(k > 0, k < nk - 1))
            def _():
                acc_ref[...] += part

            @pl.when(k == nk - 1)
            def _():
                finish(acc_ref[...] + part)

    a_spec = pl.BlockSpec((tk, tm), lambda i, j, k: (k, i)) if ta else pl.BlockSpec((tm, tk), lambda i, j, k: (i, k))
    b_spec = pl.BlockSpec((tn, tk), lambda i, j, k: (j, k)) if tb else pl.BlockSpec((tk, tn), lambda i, j, k: (k, j))
    ex_specs = [pl.BlockSpec((tm, tn), lambda i, j, k: (i, j)) for _ in extras]
    if col_blocked:
        out_specs = [pl.BlockSpec((None, tm, tn), lambda i, j, k: (j, i, 0)) for _ in out_dtypes]
        out_shape = [jax.ShapeDtypeStruct((N // tn, M, tn), dt) for dt in out_dtypes]
    else:
        out_specs = [pl.BlockSpec((tm, tn), lambda i, j, k: (i, j)) for _ in out_dtypes]
        out_shape = [jax.ShapeDtypeStruct((M, N), dt) for dt in out_dtypes]
    outs = pl.pallas_call(
        body, name=name, grid=(M // tm, N // tn, nk),
        in_specs=[a_spec, b_spec, *ex_specs], out_specs=out_specs, out_shape=out_shape,
        scratch_shapes=[pltpu.VMEM((tm, tn), F32)] if nk > 1 else [],
        compiler_params=_params(("parallel", "parallel", "arbitrary")),
    )(a, b, *extras)
    return outs[0] if n_out == 1 else outs


def _rowwise(name, fn, rows, small, outs, sums=(), tile=256):
    specs, args = [], []
    T = None
    for r in rows:
        arr, cb, w = r if isinstance(r, tuple) else (r, 0, r.shape[1])
        T = arr.shape[0]
        specs.append((w, cb))
        args.append(arr)
    tile = min(tile, T)
    assert T % tile == 0
    n_r, n_s, n_o, n_a = len(rows), len(small), len(outs), len(sums)

    def body(*refs):
        r_refs = refs[:n_r]
        s_refs = refs[n_r:n_r + n_s]
        o_refs = refs[n_r + n_s:n_r + n_s + n_o]
        a_refs = refs[n_r + n_s + n_o:]
        res = fn(*[r[...] for r in r_refs], *[s[...] for s in s_refs])
        res = res if isinstance(res, (tuple, list)) else (res,)
        for o_ref, o in zip(o_refs, res[:n_o]):
            o_ref[...] = o.astype(o_ref.dtype)

        @pl.when(pl.program_id(0) == 0)
        def _():
            for a_ref in a_refs:
                a_ref[...] = jnp.zeros_like(a_ref)

        for a_ref, v in zip(a_refs, res[n_o:]):
            a_ref[...] += v.astype(F32)

    in_specs = [pl.BlockSpec((tile, w), functools.partial(lambda i, cb: (i, cb), cb=cb)) for w, cb in specs]
    in_specs += [pl.BlockSpec(s.shape, functools.partial(lambda i, nd: (0,) * nd, nd=s.ndim)) for s in small]
    out_specs = [pl.BlockSpec((tile, w), lambda i: (i, 0)) for w, _ in outs]
    out_specs += [pl.BlockSpec(s, functools.partial(lambda i, nd: (0,) * nd, nd=len(s))) for s in sums]
    out_shape = [jax.ShapeDtypeStruct((T, w), dt) for w, dt in outs]
    out_shape += [jax.ShapeDtypeStruct(s, F32) for s in sums]
    res = pl.pallas_call(
        body, name=name, grid=(T // tile,), in_specs=in_specs, out_specs=out_specs, out_shape=out_shape,
        compiler_params=_params(("arbitrary",)),
    )(*args, *small)
    return res[0] if len(res) == 1 else res


def _rms(x, g):
    return x * lax.rsqrt(jnp.mean(x * x, axis=-1, keepdims=True) + EPS) * g


def _glu_branch(y, pre, b_glu, g_out):
    g = jax.nn.gelu(y)
    return _rms(g * jax.nn.sigmoid(pre + b_glu), g_out)


def _split_dot(x, tri_bf):
    hi = x.astype(BF16)
    lo = (x - hi.astype(F32)).astype(BF16)
    return _dot(hi, tri_bf) + _dot(lo, tri_bf)


def _softplus(z):
    return jnp.maximum(z, 0.0) + jnp.log(1.0 + jnp.exp(-jnp.abs(z)))


def _head(h):
    return slice(h * HEAD_DIM, (h + 1) * HEAD_DIM)


def _head_mean(x, seg):
    return _split_dot(x, seg) * (1.0 / HEAD_DIM)


def _qk_norm(proj, gq, gk):
    T = proj.shape[0]
    scale = 1.0 / math.sqrt(HEAD_DIM)
    idx = jnp.arange(SB_WIDTH) // HEAD_DIM
    seg = (idx[:, None] == idx[None, :]).astype(BF16)

    def fn(q, k, v, gq_, gk_, seg_):
        qn = q * lax.rsqrt(_head_mean(q * q, seg_) + EPS) * (gq_ * scale)
        kn = k * lax.rsqrt(_head_mean(k * k, seg_) + EPS) * gk_
        return qn, kn, v

    return _rowwise("qk_norm", fn, [(proj, 0, SB_WIDTH), (proj, 1, SB_WIDTH), (proj, 2, SB_WIDTH)], [gq, gk, seg],
                    [(SB_WIDTH, BF16)] * 3)


def _qk_norm_bwd(proj, gq, gk, d_qn, d_kn):
    scale = 1.0 / math.sqrt(HEAD_DIM)
    idx = jnp.arange(SB_WIDTH) // HEAD_DIM
    seg = (idx[:, None] == idx[None, :]).astype(BF16)

    def one(x, g, dy, seg_):
        r = lax.rsqrt(_head_mean(x * x, seg_) + EPS)
        gdy = g * dy
        dx = r * gdy - x * (r * r * r) * _head_mean(gdy * x, seg_)
        dg = jnp.sum(dy * x * r, axis=0, keepdims=True)
        return dx, sum(dg[:, _head(h)] for h in range(N_HEADS))

    def fn(q, k, dqn, dkn, gq_, gk_, seg_):
        dq, dgq = one(q, gq_, dqn * scale, seg_)
        dk, dgk = one(k, gk_, dkn, seg_)
        return dq, dk, dgq, dgk

    return _rowwise("qk_norm_bwd", fn, [(proj, 0, SB_WIDTH), (proj, 1, SB_WIDTH), d_qn, d_kn], [gq, gk, seg],
                    [(SB_WIDTH, BF16)] * 2, sums=[(1, HEAD_DIM)] * 2)


def _split_heads(refs, scratch, L):
    def chunk(i, _):
        r = pl.ds(pl.multiple_of(i * QBLOCK, QBLOCK), QBLOCK)
        for ref, s in zip(refs, scratch):
            for h in range(2):
                s[h, r, :] = ref[r, _head(h)]
        return 0

    lax.fori_loop(0, L // QBLOCK, chunk, 0)


Q_HALVES = KBLOCK // QBLOCK
_CHAINS = [(h, r) for h in range(2) for r in range(Q_HALVES)]


def _valid(i, kb):
    row = lax.broadcasted_iota(jnp.int32, (QBLOCK, KBLOCK), 0)
    col = lax.broadcasted_iota(jnp.int32, (QBLOCK, KBLOCK), 1)
    return col + (kb * KBLOCK - i * QBLOCK) < row


def _attn_fwd(qn, kn, vb, B, L):
    n_pairs = L // KBLOCK
    n_hp = N_HEADS // 2

    def body(q_ref, k_ref, v_ref, o_ref, rt_ref, q_s, k_s, v_s, after_s, z_s):
        _split_heads((q_ref, k_ref, v_ref), (q_s, k_s, v_s), L)
        r2 = lax.broadcasted_iota(jnp.int32, (KBLOCK, KBLOCK), 0)
        c2 = lax.broadcasted_iota(jnp.int32, (KBLOCK, KBLOCK), 1)
        after_s[...] = (r2 > c2).astype(after_s.dtype)

        def q_pair(p, _):
            rows = [pl.ds(pl.multiple_of((p * Q_HALVES + r) * QBLOCK, QBLOCK), QBLOCK) for r in range(Q_HALVES)]
            q_c = [q_s[h, rows[r], :] for h, r in _CHAINS]
            cs = range(len(_CHAINS))

            def scores(kb):
                rk = pl.ds(pl.multiple_of(kb * KBLOCK, KBLOCK), KBLOCK)
                return [_dot(q_c[c], k_s[_CHAINS[c][0], rk, :], _NT) for c in cs]

            def k_block(kb, carry, diagonal):
                rk = pl.ds(pl.multiple_of(kb * KBLOCK, KBLOCK), KBLOCK)
                if diagonal:
                    valid = [_valid(p * Q_HALVES + r, kb) for r in range(Q_HALVES)]
                    keep = lambda c, t: jnp.where(valid[_CHAINS[c][1]], t, 0.0)
                    z = scores(kb)
                else:
                    keep = lambda c, t: t
                    z = [z_s[(kb + 1) & 1, c] for c in cs]
                ahead = scores(jnp.maximum(kb - 1, 0))
                for c in cs:
                    z_s[kb & 1, c] = ahead[c]
                sp = [_softplus(z[c]) for c in cs]
                lom = [keep(c, -sp[c]) for c in cs]
                tail = [_split_dot(lom[c], after_s[...]) + carry[c][0] for c in cs]
                a = [keep(c, jnp.exp(z[c] - sp[c] + tail[c])) for c in cs]
                acc = [carry[c][1] + _dot(a[c].astype(v_s.dtype), v_s[_CHAINS[c][0], rk, :]) for c in cs]
                return tuple((carry[c][0] + jnp.sum(lom[c], axis=1, keepdims=True), acc[c]) for c in cs)

            init = (jnp.zeros((QBLOCK, 1), F32), jnp.zeros((QBLOCK, HEAD_DIM), F32))
            first = k_block(p, (init,) * len(_CHAINS), True)
            res = lax.fori_loop(1, p + 1, lambda n, carry: k_block(p - n, carry, False), first)
            for r in range(Q_HALVES):
                mine = [res[c] for c in cs if _CHAINS[c][1] == r]
                o_ref[rows[r], :] = jnp.concatenate([m[1] for m in mine], axis=1)
                rt_ref[rows[r], :] = jnp.concatenate(
                    [jnp.broadcast_to(m[0], (QBLOCK, HEAD_DIM)) for m in mine], axis=1)
            return 0

        lax.fori_loop(0, n_pairs, q_pair, 0)

    spec = pl.BlockSpec((L, LANES), lambda b, p: (b, p))
    return pl.pallas_call(
        body, name="attn_fwd", grid=(B, n_hp),
        in_specs=[spec] * 3, out_specs=[spec, spec],
        out_shape=[jax.ShapeDtypeStruct((B * L, SB_WIDTH), F32)] * 2,
        scratch_shapes=[pltpu.VMEM((2, L, HEAD_DIM), BF16)] * 3 + [pltpu.VMEM((KBLOCK, KBLOCK), BF16)]
        + [pltpu.VMEM((2, len(_CHAINS), QBLOCK, KBLOCK), F32)],
        compiler_params=_params(("parallel", "parallel")),
    )(qn, kn, vb)


def _attn_bwd(qn, kn, vb, rtot, d_sb, B, L):
    n_pairs = L // KBLOCK
    n_hp = N_HEADS // 2

    def body(q_ref, k_ref, v_ref, rt_ref, do_ref, dq_ref, dk_ref, dv_ref,
             q_s, k_s, v_s, qt_s, dkt_s, dvt_s, after_s, before_s, zd_s):
        _split_heads((q_ref, k_ref, v_ref), (q_s, k_s, v_s), L)

        def transpose_q(i, _):
            r = pl.ds(pl.multiple_of(i * QBLOCK, QBLOCK), QBLOCK)
            qt_s[:, r] = q_ref[r, :].astype(F32).T.astype(qt_s.dtype)
            return 0

        lax.fori_loop(0, L // QBLOCK, transpose_q, 0)
        dkt_s[...] = jnp.zeros_like(dkt_s)
        dvt_s[...] = jnp.zeros_like(dvt_s)
        r2 = lax.broadcasted_iota(jnp.int32, (KBLOCK, KBLOCK), 0)
        c2 = lax.broadcasted_iota(jnp.int32, (KBLOCK, KBLOCK), 1)
        after_s[...] = (r2 > c2).astype(after_s.dtype)
        before_s[...] = (r2 < c2).astype(before_s.dtype)

        def q_pair(p, _):
            rows = [pl.ds(pl.multiple_of((p * Q_HALVES + r) * QBLOCK, QBLOCK), QBLOCK) for r in range(Q_HALVES)]
            pair = pl.ds(pl.multiple_of(p * KBLOCK, KBLOCK), KBLOCK)
            do2 = do_ref[pair, :]
            do_t = do2.T.astype(v_s.dtype)
            tot2 = rt_ref[pair, :]
            cs = range(len(_CHAINS))
            hs = range(2)
            q_c = [q_s[h, rows[r], :] for h, r in _CHAINS]
            do_c = [do2[r * QBLOCK:(r + 1) * QBLOCK, _head(h)].astype(v_s.dtype) for h, r in _CHAINS]
            total = [tot2[r * QBLOCK:(r + 1) * QBLOCK, h * HEAD_DIM:h * HEAD_DIM + 1] for h, r in _CHAINS]
            qt_h = [qt_s[_head(h), pair] for h in hs]
            dot_h = [do_t[_head(h), :] for h in hs]

            def scores_ahead(kb):
                rk = pl.ds(pl.multiple_of(kb * KBLOCK, KBLOCK), KBLOCK)
                for c in cs:
                    h = _CHAINS[c][0]
                    zd_s[kb & 1, 2 * c] = _dot(q_c[c], k_s[h, rk, :], _NT)
                    zd_s[kb & 1, 2 * c + 1] = _dot(do_c[c], v_s[h, rk, :], _NT)

            def k_block(kb, carry, diagonal):
                rk = pl.ds(pl.multiple_of(kb * KBLOCK, KBLOCK), KBLOCK)
                if diagonal:
                    valid = [_valid(p * Q_HALVES + r, kb) for r in range(Q_HALVES)]
                    keep = lambda c, t: jnp.where(valid[_CHAINS[c][1]], t, 0.0)
                else:
                    keep = lambda c, t: t
                    scores_ahead(kb + 1)
                k_b = [k_s[h, rk, :] for h in hs]
                z = [zd_s[kb & 1, 2 * c] for c in cs]
                da = [zd_s[kb & 1, 2 * c + 1] for c in cs]
                sp = [_softplus(z[c]) for c in cs]
                lom = [keep(c, -sp[c]) for c in cs]
                lom_sum = [jnp.sum(lom[c], axis=1, keepdims=True) for c in cs]
                tail = [_split_dot(lom[c], after_s[...]) + (total[c] - carry[c][0] - lom_sum[c]) for c in cs]
                a = [keep(c, jnp.exp(z[c] - sp[c] + tail[c])) for c in cs]
                dla = [a[c] * da[c] for c in cs]
                for h in hs:
                    a_h = jnp.concatenate([a[c].astype(v_s.dtype) for c in cs if _CHAINS[c][0] == h], axis=0)
                    dvt_s[_head(h), rk] += _dot(dot_h[h], a_h)
                d_lom = [carry[c][1] + _split_dot(dla[c], before_s[...]) for c in cs]
                beta = [jnp.exp(z[c] - sp[c]) for c in cs]
                dz_b = [(dla[c] * (1.0 - beta[c]) - keep(c, beta[c] * d_lom[c])).astype(v_s.dtype) for c in cs]
                dq_acc = [carry[c][2] + _dot(dz_b[c], k_b[_CHAINS[c][0]]) for c in cs]
                for h in hs:
                    dz_h = jnp.concatenate([dz_b[c] for c in cs if _CHAINS[c][0] == h], axis=0)
                    dkt_s[_head(h), rk] += _dot(qt_h[h], dz_h)
                return tuple((carry[c][0] + lom_sum[c], carry[c][1] + jnp.sum(dla[c], axis=1, keepdims=True),
                              dq_acc[c]) for c in cs)

            zero = jnp.zeros((QBLOCK, 1), F32)
            init = (zero, zero, jnp.zeros((QBLOCK, HEAD_DIM), F32))
            scores_ahead(0)
            before = lax.fori_loop(0, p, lambda kb, carry: k_block(kb, carry, False), (init,) * len(_CHAINS))
            res = k_block(p, before, True)
            for r in range(Q_HALVES):
                dq_ref[rows[r], :] = jnp.concatenate([res[c][2] for c in cs if _CHAINS[c][1] == r], axis=1)
            return 0

        lax.fori_loop(0, n_pairs, q_pair, 0)

        def transpose_out(i, _):
            r = pl.ds(pl.multiple_of(i * QBLOCK, QBLOCK), QBLOCK)
            dk_ref[r, :] = dkt_s[:, r].T
            dv_ref[r, :] = dvt_s[:, r].T.astype(dv_ref.dtype)
            return 0

        lax.fori_loop(0, L // QBLOCK, transpose_out, 0)

    spec = pl.BlockSpec((L, LANES), lambda b, p: (b, p))
    return pl.pallas_call(
        body, name="attn_bwd", grid=(B, n_hp),
        in_specs=[spec] * 5, out_specs=[spec] * 3,
        out_shape=[jax.ShapeDtypeStruct((B * L, SB_WIDTH), F32)] * 2 + [jax.ShapeDtypeStruct((B * L, SB_WIDTH), BF16)],
        scratch_shapes=[pltpu.VMEM((2, L, HEAD_DIM), BF16)] * 3 + [pltpu.VMEM((LANES, L), BF16)]
        + [pltpu.VMEM((LANES, L), F32)] * 2 + [pltpu.VMEM((KBLOCK, KBLOCK), BF16)] * 2
        + [pltpu.VMEM((2, 2 * len(_CHAINS), QBLOCK, KBLOCK), F32)],
        compiler_params=_params(("parallel", "parallel")),
    )(qn, kn, vb, rtot, d_sb)


def _ssm_discretise(lam_re, lam_im, log_dt, b_re, b_im):
    dt = jnp.exp(log_dt)
    mag = jnp.exp(lam_re * dt)
    lbr = mag * jnp.cos(lam_im * dt)
    lbi = mag * jnp.sin(lam_im * dt)
    den = lam_re * lam_re + lam_im * lam_im
    nr, ni = lbr - 1.0, lbi
    cr = (nr * lam_re + ni * lam_im) / den
    ci = (ni * lam_re - nr * lam_im) / den
    return lbr, lbi, cr * b_re - ci * b_im, cr * b_im + ci * b_re


def _ssm_prep(lam_re, lam_im, log_dt, b_re_t, b_im_t):
    def body(lr, li, ld, br, bi, o_lr, o_li, o_br, o_bi):
        res = _ssm_discretise(lr[...], li[...], ld[...], br[...], bi[...])
        for o, v in zip((o_lr, o_li, o_br, o_bi), res):
            o[...] = v

    return pl.pallas_call(
        body, name="ssm_prep",
        out_shape=[jax.ShapeDtypeStruct(lam_re.shape, F32)] * 2 + [jax.ShapeDtypeStruct(b_re_t.shape, F32)] * 2,
    )(lam_re, lam_im, log_dt, b_re_t, b_im_t)


def _ssm_prep_bwd(lam_re, lam_im, log_dt, b_re_t, b_im_t, d_lr, d_li, d_br, d_bi):
    def body(lr, li, ld, br, bi, g_lr, g_li, g_br, g_bi, o_lr, o_li, o_ld, o_br, o_bi):
        _, vjp = jax.vjp(_ssm_discretise, lr[...], li[...], ld[...], br[...], bi[...])
        res = vjp((g_lr[...], g_li[...], g_br[...], g_bi[...]))
        for o, v in zip((o_lr, o_li, o_ld, o_br, o_bi), res):
            o[...] = v

    return pl.pallas_call(
        body, name="ssm_prep_bwd",
        out_shape=[jax.ShapeDtypeStruct(lam_re.shape, F32)] * 2 + [jax.ShapeDtypeStruct(log_dt.shape, F32)]
        + [jax.ShapeDtypeStruct(b_re_t.shape, F32)] * 2,
    )(lam_re, lam_im, log_dt, b_re_t, b_im_t, d_lr, d_li, d_br, d_bi)


def _block_diag(m):
    m4 = m.reshape(SSM_COLS, 8, SSM_GROUP, SSM_STATE)
    return jnp.einsum("aghp,gk->aghkp", m4, jnp.eye(8, dtype=m.dtype)).reshape(SSM_COLS, LANES, 512)


def _block_diag_take(d):
    d6 = d.reshape(SSM_COLS, 8, SSM_GROUP, 2, 8, SSM_STATE)
    return jnp.einsum("aghrgp->raghp", d6).reshape(2, SSM_GROUPS, SSM_GROUP, SSM_STATE)


def _cmul(ar, ai, br, bi):
    return ar * br - ai * bi, ar * bi + ai * br


def _power(lr, li, n):
    assert n & (n - 1) == 0
    for _ in range(n.bit_length() - 1):
        lr, li = _cmul(lr, li, lr, li)
    return lr, li


def _ssm_fwd(u_p, w_b, lam_r, lam_i, c_m, d_skip, B, L, tj):
    J = L // N_CHUNK
    njt = J // tj
    R = tj * N_CHUNK
    H = 512

    def body(u_ref, wb_ref, lr_ref, li_ref, cm_ref, d_ref, y_ref, x_ref, xin_ref, bu_s, st_s, xin_s):
        ph, jt = pl.program_id(2), pl.program_id(3)
        lr, li = lr_ref[...], li_ref[...]

        @pl.when(jnp.logical_and(ph == 0, jt == 0))
        def _():
            st_s[...] = jnp.zeros_like(st_s)

        bu_s[...] = _dot(u_ref[...].astype(BF16), wb_ref[...].astype(BF16))

        def scan(store):
            def step(j, carry):
                xr, xi = carry
                r = pl.ds(pl.multiple_of(j * N_CHUNK, N_CHUNK), N_CHUNK)
                nr = lr * xr - li * xi + bu_s[r, 0:H]
                ni = lr * xi + li * xr + bu_s[r, H:2 * H]
                if store:
                    x_ref[r, 0:H] = nr
                    x_ref[r, H:2 * H] = ni
                return nr, ni

            xr, xi = lax.fori_loop(0, tj, step, (st_s[:, 0:H], st_s[:, H:2 * H]))
            st_s[:, 0:H] = xr
            st_s[:, H:2 * H] = xi

        @pl.when(ph == 0)
        def _():
            scan(False)

            @pl.when(jt == njt - 1)
            def _():
                pr, pi = _power(lr[0:1], li[0:1], J)
                xin_s[0:1, :] = jnp.zeros((1, 2 * H), F32)
                for c in range(1, N_CHUNK):
                    qr, qi = _cmul(pr, pi, xin_s[c - 1:c, 0:H], xin_s[c - 1:c, H:2 * H])
                    xin_s[c:c + 1, 0:H] = qr + st_s[c - 1:c, 0:H]
                    xin_s[c:c + 1, H:2 * H] = qi + st_s[c - 1:c, H:2 * H]
                xin_ref[...] = xin_s[...]
                st_s[...] = xin_s[...]

        @pl.when(ph == 1)
        def _():
            scan(True)
            y = _dot(x_ref[...].astype(BF16), cm_ref[...].astype(BF16))
            y_ref[...] = y + d_ref[...] * u_ref[...]

    return pl.pallas_call(
        body, name="ssm_fwd", grid=(SSM_COLS, B, 2, njt),
        in_specs=[
            pl.BlockSpec((None, R, LANES), lambda i, b, ph, jt: (b, jt, i)),
            pl.BlockSpec((None, LANES, 2 * H), lambda i, b, ph, jt: (i, 0, 0)),
            pl.BlockSpec((None, N_CHUNK, H), lambda i, b, ph, jt: (i, 0, 0)),
            pl.BlockSpec((None, N_CHUNK, H), lambda i, b, ph, jt: (i, 0, 0)),
            pl.BlockSpec((None, 2 * H, LANES), lambda i, b, ph, jt: (i, 0, 0)),
            pl.BlockSpec((1, LANES), lambda i, b, ph, jt: (0, i)),
        ],
        out_specs=[
            pl.BlockSpec((None, R, LANES), lambda i, b, ph, jt: (b, jt * ph, i)),
            pl.BlockSpec((None, R, 2 * H), lambda i, b, ph, jt: (b, jt * ph, i)),
            pl.BlockSpec((None, None, N_CHUNK, 2 * H), lambda i, b, ph, jt: (b, i, 0, 0)),
        ],
        out_shape=[
            jax.ShapeDtypeStruct((B, L, SSM_WIDTH), F32),
            jax.ShapeDtypeStruct((B, L, SSM_COLS * 2 * H), F32),
            jax.ShapeDtypeStruct((B, SSM_COLS, N_CHUNK, 2 * H), F32),
        ],
        scratch_shapes=[pltpu.VMEM((R, 2 * H), F32), pltpu.VMEM((N_CHUNK, 2 * H), F32), pltpu.VMEM((N_CHUNK, 2 * H), F32)],
        compiler_params=_params(("arbitrary",) * 4),
    )(u_p, w_b, lam_r, lam_i, c_m, d_skip)


def _ssm_bwd(dy_p, u_p, x, xin, w_bt, lam_r, lam_i, c_mt, d_skip, B, L, tj):
    J = L // N_CHUNK
    njt = J // tj
    R = tj * N_CHUNK
    H = 512
    x4 = x.reshape(B, J, N_CHUNK, SSM_COLS * 2 * H)

    def body(dy_ref, u_ref, x_ref, xp_ref, xin_ref, wbt_ref, lr_ref, li_ref, cmt_ref, d_ref,
             du_ref, dwb_ref, dcm_ref, dlr_ref, dli_ref, dd_ref, ca_s, a_s, st_s, dl_s):
        b, ph, jt = pl.program_id(1), pl.program_id(2), pl.program_id(3)
        jr = njt - 1 - jt
        lr, li = lr_ref[...], -li_ref[...]

        @pl.when(jnp.logical_and(b == 0, jnp.logical_and(ph == 0, jt == 0)))
        def _():
            dwb_ref[...] = jnp.zeros_like(dwb_ref)
            dcm_ref[...] = jnp.zeros_like(dcm_ref)
            dlr_ref[...] = jnp.zeros_like(dlr_ref)
            dli_ref[...] = jnp.zeros_like(dli_ref)
            dd_ref[...] = jnp.zeros_like(dd_ref)
            dl_s[...] = jnp.zeros_like(dl_s)

        @pl.when(jnp.logical_and(ph == 0, jt == 0))
        def _():
            st_s[...] = jnp.zeros_like(st_s)

        ca_s[...] = _dot(dy_ref[...].astype(BF16), cmt_ref[...].astype(BF16))

        def scan(store):
            def step(n, carry):
                ar, ai = carry
                r = pl.ds(pl.multiple_of((tj - 1 - n) * N_CHUNK, N_CHUNK), N_CHUNK)
                nr = lr * ar - li * ai + ca_s[r, 0:H]
                ni = lr * ai + li * ar + ca_s[r, H:2 * H]
                if store:
                    a_s[r, 0:H] = nr
                    a_s[r, H:2 * H] = ni
                return nr, ni

            ar, ai = lax.fori_loop(0, tj, step, (st_s[:, 0:H], st_s[:, H:2 * H]))
            st_s[:, 0:H] = ar
            st_s[:, H:2 * H] = ai

        @pl.when(ph == 0)
        def _():
            scan(False)

            @pl.when(jt == njt - 1)
            def _():
                pr, pi = _power(lr[0:1], li[0:1], J)
                a_s[N_CHUNK - 1:N_CHUNK, :] = jnp.zeros((1, 2 * H), F32)
                for c in range(N_CHUNK - 2, -1, -1):
                    qr, qi = _cmul(pr, pi, a_s[c + 1:c + 2, 0:H], a_s[c + 1:c + 2, H:2 * H])
                    a_s[c:c + 1, 0:H] = qr + st_s[c + 1:c + 2, 0:H]
                    a_s[c:c + 1, H:2 * H] = qi + st_s[c + 1:c + 2, H:2 * H]
                st_s[...] = a_s[0:N_CHUNK, :]

        @pl.when(ph == 1)
        def _():
            scan(True)
            dy = dy_ref[...]
            u = u_ref[...]
            a_b = a_s[...].astype(BF16)
            du_ref[...] = (_dot(a_b, wbt_ref[...].astype(BF16)) + d_ref[...] * dy).astype(du_ref.dtype)
            dwb_ref[...] += _dot(u.astype(BF16), a_b, _TN)
            dcm_ref[...] += _dot(x_ref[...].astype(BF16), dy.astype(BF16), _TN)
            dd_ref[...] += jnp.sum(dy * u, axis=0, keepdims=True)

            first = jnp.where(jr == 0, xin_ref[...], xp_ref[...])
            a0r, a0i = a_s[0:N_CHUNK, 0:H], a_s[0:N_CHUNK, H:2 * H]
            acc0 = (a0r * first[:, 0:H] + a0i * first[:, H:2 * H], a0i * first[:, 0:H] - a0r * first[:, H:2 * H])

            def step(j, carry):
                sr, si = carry
                r = pl.ds(pl.multiple_of(j * N_CHUNK, N_CHUNK), N_CHUNK)
                rp = pl.ds(pl.multiple_of((j - 1) * N_CHUNK, N_CHUNK), N_CHUNK)
                ar, ai = a_s[r, 0:H], a_s[r, H:2 * H]
                xr, xi = x_ref[rp, 0:H], x_ref[rp, H:2 * H]
                return sr + ar * xr + ai * xi, si + ai * xr - ar * xi

            sr, si = lax.fori_loop(1, tj, step, acc0)
            dl_s[:, 0:H] += sr
            dl_s[:, H:2 * H] += si

            @pl.when(jnp.logical_and(b == B - 1, jt == njt - 1))
            def _():
                dlr_ref[...] = jnp.sum(dl_s[:, 0:H], axis=0, keepdims=True)
                dli_ref[...] = jnp.sum(dl_s[:, H:2 * H], axis=0, keepdims=True)
                dl_s[...] = jnp.zeros_like(dl_s)

    rev = lambda ph, jt: (njt - 1 - jt) * ph + (njt - 1) * (1 - ph)
    return pl.pallas_call(
        body, name="ssm_bwd", grid=(SSM_COLS, B, 2, njt),
        in_specs=[
            pl.BlockSpec((None, R, LANES), lambda i, b, ph, jt: (b, njt - 1 - jt, i)),
            pl.BlockSpec((None, R, LANES), lambda i, b, ph, jt: (b, njt - 1 - jt, i)),
            pl.BlockSpec((None, R, 2 * H), lambda i, b, ph, jt: (b, rev(ph, jt), i)),
            pl.BlockSpec((None, None, N_CHUNK, 2 * H),
                         lambda i, b, ph, jt: (b, jnp.maximum((njt - 1 - jt) * tj - 1, 0), 0, i)),
            pl.BlockSpec((None, None, N_CHUNK, 2 * H), lambda i, b, ph, jt: (b, i, 0, 0)),
            pl.BlockSpec((None, 2 * H, LANES), lambda i, b, ph, jt: (i, 0, 0)),
            pl.BlockSpec((None, N_CHUNK, H), lambda i, b, ph, jt: (i, 0, 0)),
            pl.BlockSpec((None, N_CHUNK, H), lambda i, b, ph, jt: (i, 0, 0)),
            pl.BlockSpec((None, LANES, 2 * H), lambda i, b, ph, jt: (i, 0, 0)),
            pl.BlockSpec((1, LANES), lambda i, b, ph, jt: (0, i)),
        ],
        out_specs=[
            pl.BlockSpec((None, R, LANES), lambda i, b, ph, jt: (b, rev(ph, jt), i)),
            pl.BlockSpec((None, LANES, 2 * H), lambda i, b, ph, jt: (i, 0, 0)),
            pl.BlockSpec((None, 2 * H, LANES), lambda i, b, ph, jt: (i, 0, 0)),
            pl.BlockSpec((None, 1, H), lambda i, b, ph, jt: (i, 0, 0)),
            pl.BlockSpec((None, 1, H), lambda i, b, ph, jt: (i, 0, 0)),
            pl.BlockSpec((1, LANES), lambda i, b, ph, jt: (0, i)),
        ],
        out_shape=[
            jax.ShapeDtypeStruct((B, L, SSM_WIDTH), BF16),
            jax.ShapeDtypeStruct((SSM_COLS, LANES, 2 * H), F32),
            jax.ShapeDtypeStruct((SSM_COLS, 2 * H, LANES), F32),
            jax.ShapeDtypeStruct((SSM_COLS, 1, H), F32),
            jax.ShapeDtypeStruct((SSM_COLS, 1, H), F32),
            jax.ShapeDtypeStruct((1, SSM_WIDTH), F32),
        ],
        scratch_shapes=[pltpu.VMEM((R, 2 * H), F32), pltpu.VMEM((R, 2 * H), F32),
                        pltpu.VMEM((N_CHUNK, 2 * H), F32), pltpu.VMEM((N_CHUNK, 2 * H), F32)],
        compiler_params=_params(("arbitrary",) * 4),
    )(dy_p, u_p, x, x4, xin, w_bt, lam_r, lam_i, c_mt, d_skip)


def _to_scan_layout(t, B, L):
    C = t.shape[-1]
    return t.reshape(B, N_CHUNK, L // N_CHUNK, C).transpose(0, 2, 1, 3).reshape(B, L, C)


def _from_scan_layout(t, B, L):
    C = t.shape[-1]
    return t.reshape(B, L // N_CHUNK, N_CHUNK, C).transpose(0, 2, 1, 3).reshape(B * L, C)


def _local_step(x, target, p, w_in, late_weights, mlp_grads_ready=None, rest_grads_ready=None, order=None, *,
                ssm_tile=128):
    B, L, D = x.shape
    T = B * L
    x2 = x.reshape(T, D)
    row = lambda v: v.reshape(1, -1)
    g1, g2, ga, gs, b_glu = row(p["norm1_g"]), row(p["norm2_g"]), row(p["attn_out_g"]), row(p["ssm_out_g"]), row(p["b_glu"])
    g1_first = g1 if order is None else g1 + order
    gq8 = jnp.tile(row(p["q_norm_g"]), (1, N_HEADS))
    gk8 = jnp.tile(row(p["k_norm_g"]), (1, N_HEADS))

    G, P, Hh = SSM_GROUPS, SSM_STATE, SSM_GROUP
    lam_re3, lam_im3 = p["ssm_lambda_re"].reshape(G, 1, P), p["ssm_lambda_im"].reshape(G, 1, P)
    log_dt3 = p["ssm_log_dt"].reshape(G, 1, 1)
    b_re_t, b_im_t = p["ssm_b_re"].transpose(0, 2, 1), p["ssm_b_im"].transpose(0, 2, 1)
    lbr, lbi, bbr, bbi = _ssm_prep(lam_re3, lam_im3, log_dt3, b_re_t, b_im_t)
    w_b = jnp.concatenate([_block_diag(bbr), _block_diag(bbi)], axis=2)
    c_mt = jnp.concatenate([_block_diag(p["ssm_c_re"]), -_block_diag(p["ssm_c_im"])], axis=2)
    w_bt, c_m = w_b.transpose(0, 2, 1), c_mt.transpose(0, 2, 1)
    lam_r = jnp.broadcast_to(lbr.reshape(SSM_COLS, 1, 512), (SSM_COLS, N_CHUNK, 512))
    lam_i = jnp.broadcast_to(lbi.reshape(SSM_COLS, 1, 512), (SSM_COLS, N_CHUNK, 512))
    d_skip = p["ssm_d"].reshape(1, SSM_WIDTH)

    xn = _rowwise("norm1", _rms, [x2], [g1_first], [(D, BF16)])
    proj = _matmul("proj", xn, w_in, tn=1024)
    qn, kn, vb = _qk_norm(proj, gq8, gk8)
    sb, rtot = _attn_fwd(qn, kn, vb, B, L)
    u_p = _to_scan_layout(proj[:, 3 * SB_WIDTH:], B, L)
    y_p, xs, xin = _ssm_fwd(u_p, w_b, lam_r, lam_i, c_m, d_skip, B, L, ssm_tile)
    y2 = y_p.reshape(T, SSM_WIDTH)
    gel = _rowwise("gelu", jax.nn.gelu, [y2], [], [(SSM_WIDTH, BF16)])
    w_glu, w_out, w_mlp_in, w_mlp_out = late_weights(gel)
    pre = _matmul("glu_gate", gel, w_glu)
    ssm_n = _rowwise("glu_out", _glu_branch, [y2, pre], [b_glu, gs], [(SSM_WIDTH, BF16)])
    sb_n = _rowwise("attn_out_norm", _rms, [sb], [ga], [(SB_WIDTH, BF16)])
    mixed = jnp.concatenate([sb_n, _from_scan_layout(ssm_n, B, L)], axis=1)
    h1 = _matmul("out_proj", mixed, w_out, extras=[x2], epilogue=lambda acc, r: (acc + r,), tn=1024)
    hn = _rowwise("norm2", _rms, [h1], [g2], [(D, BF16)])
    act, a_pre = _matmul("mlp_in", hn, w_mlp_in, out_dtypes=(BF16, BF16), tn=1024,
                         epilogue=lambda acc: (jnp.square(jnp.maximum(acc, 0.0)), acc))
    out = _matmul("mlp_out", act, w_mlp_out, extras=[h1], epilogue=lambda acc, r: (acc + r,))

    def loss_fn(o, t):
        diff = o - t
        part = jnp.sum(jnp.sum(diff * diff, axis=0, keepdims=True), axis=1, keepdims=True)
        d = diff * (1.0 / D)
        return d, d, part * (0.5 / D)

    d_out, d_out_b, loss = _rowwise("loss", loss_fn, [out, target.reshape(T, D)], [], [(D, F32), (D, BF16)],
                                    sums=[(1, 1)])

    d_apre = _matmul("mlp_out_dx", d_out_b, w_mlp_out, tb=True, extras=[a_pre], out_dtypes=(BF16,), tn=1024,
                     epilogue=lambda acc, ap: (acc * (2.0 * jnp.maximum(ap.astype(F32), 0.0)),))
    both = lambda acc: (acc, acc)
    g_w_mlp_out, g_w_mlp_out_b = _matmul("mlp_out_dw", act, d_out_b, ta=True, out_dtypes=(F32, BF16), epilogue=both)
    g_w_mlp_in, g_w_mlp_in_b = _matmul("mlp_in_dw", hn, d_apre, ta=True, col_blocked=True, out_dtypes=(F32, BF16),
                                       epilogue=both, tn=w_mlp_in.shape[1] // N_DEV)
    if mlp_grads_ready is not None:
        g2 = g2 + mlp_grads_ready(g_w_mlp_out, g_w_mlp_out_b, g_w_mlp_in, g_w_mlp_in_b)
    d_hn = _matmul("mlp_in_dx", d_apre, w_mlp_in, tb=True)

    def norm_bwd_res(h, dy, res, g):
        _, vjp = jax.vjp(_rms, h, g)
        dh, dg = vjp(dy)
        return res + dh, dg

    def norm_bwd_res2(h, dy, res, g):
        d, dg = norm_bwd_res(h, dy, res, g)
        return d, d, dg

    d_h1, d_h1_b, g_norm2 = _rowwise("norm2_bwd", norm_bwd_res2, [h1, d_hn, d_out], [g2], [(D, F32), (D, BF16)],
                                     sums=[(1, D)])

    d_mixed = _matmul("out_proj_dx", d_h1_b, w_out, tb=True, tn=1024)
    g_w_out, g_w_out_b = _matmul("out_proj_dw", mixed, d_h1_b, ta=True, out_dtypes=(F32, BF16), epilogue=both)

    def norm_bwd(h, dy, g):
        _, vjp = jax.vjp(_rms, h, g)
        return vjp(dy)

    d_sb, g_attn_out = _rowwise("attn_out_norm_bwd", norm_bwd, [sb, (d_mixed, 0, SB_WIDTH)], [ga],
                                [(SB_WIDTH, F32)], sums=[(1, SB_WIDTH)])
    d_ssm_n = _to_scan_layout(d_mixed[:, SB_WIDTH:], B, L).reshape(T, SSM_WIDTH)

    def glu_bwd(y, pre_, dy, bg, g):
        _, vjp = jax.vjp(_glu_branch, y, pre_, bg, g)
        d_y, d_pre, d_bg, d_g = vjp(dy)
        return d_y, d_pre, d_bg, d_g

    d_y_direct, d_pre, g_b_glu, g_ssm_out = _rowwise(
        "glu_out_bwd", glu_bwd, [y2, pre, d_ssm_n], [b_glu, gs], [(SSM_WIDTH, F32), (SSM_WIDTH, BF16)],
        sums=[(1, SSM_WIDTH), (1, SSM_WIDTH)])
    d_gel = _matmul("glu_gate_dx", d_pre, w_glu, tb=True)
    g_w_glu, g_w_glu_b = _matmul("glu_gate_dw", gel, d_pre, ta=True, out_dtypes=(F32, BF16), epilogue=both)

    def gelu_bwd(y, dg, dy0):
        _, vjp = jax.vjp(jax.nn.gelu, y)
        return dy0 + vjp(dg)[0]

    d_y = _rowwise("gelu_bwd", gelu_bwd, [y2, d_gel, d_y_direct], [], [(SSM_WIDTH, F32)])

    du_p, d_wb, d_cm, d_lr, d_li, g_d = _ssm_bwd(
        d_y.reshape(B, L, SSM_WIDTH), u_p, xs, xin, w_bt, lam_r, lam_i, c_mt, d_skip, B, L, ssm_tile)
    d_bb = _block_diag_take(d_wb.reshape(SSM_COLS, LANES, 2, 512))
    d_c = _block_diag_take(d_cm.transpose(0, 2, 1).reshape(SSM_COLS, LANES, 2, 512))
    g_lam_re, g_lam_im, g_log_dt, g_b_re_t, g_b_im_t = _ssm_prep_bwd(
        lam_re3, lam_im3, log_dt3, b_re_t, b_im_t,
        d_lr.reshape(G, 1, P), d_li.reshape(G, 1, P), d_bb[0], d_bb[1])
    d_qn, d_kn, d_v = _attn_bwd(qn, kn, vb, rtot, d_sb, B, L)
    d_q, d_k, g_q, g_k = _qk_norm_bwd(proj, gq8, gk8, d_qn, d_kn)

    d_proj = jnp.concatenate([d_q, d_k, d_v, _from_scan_layout(du_p, B, L)], axis=1)
    g_w_in, g_w_in_b = _matmul("proj_dw", xn, d_proj, ta=True, col_blocked=True, out_dtypes=(F32, BF16),
                               epilogue=both, tn=w_in.shape[1] // N_DEV)
    if rest_grads_ready is not None:
        g1 = g1 + rest_grads_ready([g_w_in, g_w_glu, g_w_out], [g_w_in_b, g_w_glu_b, g_w_out_b])
    d_xn = _matmul("proj_dx", d_proj, w_in, tb=True, tn=1024)
    grad_x, g_norm1 = _rowwise("norm1_bwd", norm_bwd_res, [x2, d_xn, d_h1], [g1], [(D, F32)], sums=[(1, D)])

    small = {
        "norm1_g": g_norm1.reshape(-1),
        "q_norm_g": g_q.reshape(-1),
        "k_norm_g": g_k.reshape(-1),
        "ssm_lambda_re": g_lam_re.reshape(G, P),
        "ssm_lambda_im": g_lam_im.reshape(G, P),
        "ssm_log_dt": g_log_dt.reshape(G),
        "ssm_b_re": g_b_re_t.transpose(0, 2, 1),
        "ssm_b_im": g_b_im_t.transpose(0, 2, 1),
        "ssm_c_re": d_c[0],
        "ssm_c_im": -d_c[1],
        "ssm_d": g_d.reshape(G, Hh),
        "b_glu": g_b_glu.reshape(-1),
        "attn_out_g": g_attn_out.reshape(-1),
        "ssm_out_g": g_ssm_out.reshape(-1),
        "norm2_g": g_norm2.reshape(-1),
    }
    big = {"w_in": g_w_in, "w_glu": g_w_glu, "w_out": g_w_out, "w_mlp_in": g_w_mlp_in, "w_mlp_out": g_w_mlp_out}
    return loss[0, 0], grad_x.reshape(B, L, D), small, big


_ANY = pl.BlockSpec(memory_space=pl.ANY)
_MESH = pl.DeviceIdType.MESH


def _all_gather(name, shards):
    n = len(shards)

    def body(*refs):
        in_refs, out_refs = refs[:n], refs[n:2 * n]
        send_sems, recv_sems, local_sems = refs[2 * n:]
        x, y, c = lax.axis_index("x"), lax.axis_index("y"), lax.axis_index("c")
        me, sibling = (x, y, c), (x, y, 1 - c)
        chips = [(1 - x, y), (x, 1 - y), (1 - x, 1 - y)]

        def copy(a, k, block, to, src=None):
            px, py, pc = block
            rows = out_refs[a].at[4 * px + 2 * py + pc]
            return pltpu.make_async_remote_copy(
                src_ref=rows if src is None else src, dst_ref=rows, send_sem=send_sems.at[a, k],
                recv_sem=recv_sems.at[a, k], device_id=to, device_id_type=_MESH)

        mine = [pltpu.make_async_copy(in_refs[a], out_refs[a].at[4 * x + 2 * y + c], local_sems.at[a]) for a in range(n)]
        first, passed = [], []
        for a in range(n):
            mine[a].start()
            first.append(copy(a, 0, me, sibling, src=in_refs[a]))
            first += [copy(a, 1 + j, me, (*chip, c), src=in_refs[a]) for j, chip in enumerate(chips)]
        for cp in first:
            cp.start()
        for j, chip in enumerate(chips):
            for a in range(n):
                copy(a, 1 + j, (*chip, c), me).wait_recv()
                fwd = copy(a, 4 + j, (*chip, c), sibling)
                fwd.start()
                passed.append(fwd)
        for a in range(n):
            copy(a, 0, sibling, me).wait_recv()
            for j, chip in enumerate(chips):
                copy(a, 4 + j, (*chip, 1 - c), me).wait_recv()
        for cp in first + passed:
            cp.wait_send()
        for cp in mine:
            cp.wait()

    return pl.pallas_call(
        body, name=name,
        in_specs=[_ANY] * n, out_specs=[_ANY] * n,
        out_shape=[jax.ShapeDtypeStruct((N_DEV, *s.shape), s.dtype) for s in shards],
        scratch_shapes=[pltpu.SemaphoreType.DMA((n, 7)), pltpu.SemaphoreType.DMA((n, 7)), pltpu.SemaphoreType.DMA((n,))],
    )(*shards)


_HBM = pl.BlockSpec(memory_space=pltpu.HBM)
_SEM = pl.BlockSpec(memory_space=pltpu.SEMAPHORE)
_EFFECT = pltpu.SideEffectType.DATAFLOW_SIDE_EFFECTING
_FLIPS = [(dx, dy, dc) for dx in (0, 1) for dy in (0, 1) for dc in (0, 1) if (dx, dy, dc) != (0, 0, 0)]


def _exchange_start(name, srcs, lands, per_peer):
    n = len(srcs)

    def body(*refs):
        src_refs, land_refs = refs[:n], refs[n:2 * n]
        send_sems, recv_sems = refs[2 * n:3 * n], refs[3 * n:4 * n]
        token = refs[-1]
        x, y, c = lax.axis_index("x"), lax.axis_index("y"), lax.axis_index("c")
        me = 4 * x + 2 * y + c
        for dx, dy, dc in _FLIPS:
            px, py, pc = (1 - x if dx else x), (1 - y if dy else y), (1 - c if dc else c)
            for a in range(n):
                pltpu.make_async_remote_copy(
                    src_ref=src_refs[a].at[4 * px + 2 * py + pc] if per_peer else src_refs[a],
                    dst_ref=land_refs[a].at[me], send_sem=send_sems[a], recv_sem=recv_sems[a],
                    device_id=(px, py, pc), device_id_type=_MESH).start()
        token[...] = jnp.zeros_like(token)

    hbm = lambda t: pltpu.with_memory_space_constraint(t, pltpu.HBM)
    res = pl.pallas_call(
        body, name=name,
        out_shape=(*[pltpu.SemaphoreType.DMA(())] * (2 * n), *[pltpu.HBM(t.shape, t.dtype) for t in (*srcs, *lands)],
                   jax.ShapeDtypeStruct((8, LANES), F32)),
        in_specs=[_HBM] * (2 * n),
        out_specs=(*[_SEM] * (2 * n), *[_HBM] * (2 * n), pl.BlockSpec(memory_space=pltpu.VMEM)),
        input_output_aliases={i: 2 * n + i for i in range(2 * n)},
        compiler_params=pltpu.CompilerParams(has_side_effects=_EFFECT),
    )(*[hbm(t) for t in (*srcs, *lands)])
    return res[:-1], res[-1]


def _exchange_wait(name, handle, after):
    n = len(handle) // 4
    sems, thru = handle[:2 * n], handle[2 * n:]

    def body(*refs):
        land_refs = refs[n:2 * n]
        send_sems, recv_sems = refs[2 * n:3 * n], refs[3 * n:4 * n]
        me = (lax.axis_index("x"), lax.axis_index("y"), lax.axis_index("c"))
        for a in range(n):
            seven = land_refs[a].at[pl.ds(0, len(_FLIPS))]
            all_copies = pltpu.make_async_remote_copy(
                src_ref=seven, dst_ref=seven, send_sem=send_sems[a], recv_sem=recv_sems[a], device_id=me,
                device_id_type=_MESH)
            all_copies.wait_send()
            all_copies.wait_recv()

    res = pl.pallas_call(
        body, name=name, out_shape=tuple(pltpu.HBM(t.shape, t.dtype) for t in thru),
        in_specs=[*[_HBM] * (2 * n), *[_SEM] * (2 * n), _ANY], out_specs=tuple([_HBM] * (2 * n)),
        input_output_aliases={i: i for i in range(2 * n)},
        compiler_params=pltpu.CompilerParams(has_side_effects=_EFFECT),
    )(*thru, *sems, after)
    return res[n:]


def _adamw_gathered(name, own, parts, me, w, m, v):
    r, c = w.shape
    tr = min(r, 256)

    def body(me_ref, own_ref, p_ref, w_ref, m_ref, v_ref, g_out, d_out, m_out, v_out):
        g = own_ref[...]
        for j in range(N_DEV):
            g = g + p_ref[j].astype(F32)
        delta, m_new, v_new = _adamw(w_ref[...], g, m_ref[...], v_ref[...])
        g_out[...] = g
        d_out[...] = delta
        m_out[...] = m_new
        v_out[...] = v_new

    spec = pl.BlockSpec((tr, c), lambda i, me_ref: (i, 0))
    return pl.pallas_call(
        body, name=name,
        grid_spec=pltpu.PrefetchScalarGridSpec(
            num_scalar_prefetch=1, grid=(r // tr,),
            in_specs=[pl.BlockSpec((None, tr, c), lambda i, me_ref: (me_ref[0], i, 0)),
                      pl.BlockSpec((N_DEV, tr, c), lambda i, me_ref: (0, i, 0)), spec, spec, spec],
            out_specs=[spec] * 4),
        out_shape=[jax.ShapeDtypeStruct((r, c), F32)] * 4,
        compiler_params=_params(("parallel",)),
    )(me, own, parts, w, m, v)


def _adamw(w, g, m, v):
    m = ADAM_B1 * m + (1.0 - ADAM_B1) * g
    v = ADAM_B2 * v + (1.0 - ADAM_B2) * jnp.square(g)
    m_hat = m / (1.0 - ADAM_B1 ** ADAM_STEP)
    v_hat = v / (1.0 - ADAM_B2 ** ADAM_STEP)
    delta = -ADAM_LR * (m_hat / (jnp.sqrt(v_hat) + ADAM_EPS) + ADAM_WD * w)
    return delta, m, v


def _adamw_small(name, parts, w, m, v):
    _, r, c = parts.shape
    tr = 8

    def body(p_ref, w_ref, m_ref, v_ref, g_out, d_out, m_out, v_out):
        g = p_ref[0]
        for j in range(1, N_DEV):
            g = g + p_ref[j]
        delta, m_new, v_new = _adamw(w_ref[...], g, m_ref[...], v_ref[...])
        g_out[...] = g
        d_out[...] = delta
        m_out[...] = m_new
        v_out[...] = v_new

    spec = pl.BlockSpec((tr, c), lambda i: (i, 0))
    return pl.pallas_call(
        body, name=name, grid=(r // tr,),
        in_specs=[pl.BlockSpec((N_DEV, tr, c), lambda i: (0, i, 0)), spec, spec, spec],
        out_specs=[spec] * 4, out_shape=[jax.ShapeDtypeStruct((r, c), F32)] * 4,
        compiler_params=_params(("parallel",)),
    )(parts, w, m, v)


_WEIGHTS = ["norm1_g", "w_in", "q_norm_g", "k_norm_g", "ssm_lambda_re", "ssm_lambda_im", "ssm_log_dt", "ssm_b_re",
            "ssm_b_im", "ssm_c_re", "ssm_c_im", "ssm_d", "w_glu", "b_glu", "attn_out_g", "ssm_out_g", "w_out",
            "norm2_g", "w_mlp_in", "w_mlp_out"]
_BIG = ["w_in", "w_glu", "w_out", "w_mlp_in", "w_mlp_out"]
_SMALL = [n for n in _WEIGHTS if n not in _BIG]
_PACK_COLS = 1024


def _pack(tree, last=None):
    flat = [tree[n].reshape(-1).astype(F32) for n in _SMALL]
    size = sum(f.shape[0] for f in flat)
    rows = -(-(size + 1) // (_PACK_COLS * 8)) * 8
    pad = jnp.zeros((rows * _PACK_COLS - size - 1,), F32)
    tail = jnp.zeros((1,), F32) if last is None else last.reshape(1).astype(F32)
    return jnp.concatenate(flat + [pad, tail]).reshape(rows, _PACK_COLS)


def _unpack(buf, like):
    flat, out, off = buf.reshape(-1), {}, 0
    for n in _SMALL:
        size = like[n].size
        out[n] = flat[off:off + size].reshape(like[n].shape)
        off += size
    return out


def kernel(x, norm1_g, w_in, q_norm_g, k_norm_g, ssm_lambda_re, ssm_lambda_im, ssm_log_dt, ssm_b_re, ssm_b_im, ssm_c_re, ssm_c_im, ssm_d, w_glu, b_glu, attn_out_g, ssm_out_g, w_out, norm2_g, w_mlp_in, w_mlp_out, loss_target, m_norm1_g, m_w_in, m_q_norm_g, m_k_norm_g, m_ssm_lambda_re, m_ssm_lambda_im, m_ssm_log_dt, m_ssm_b_re, m_ssm_b_im, m_ssm_c_re, m_ssm_c_im, m_ssm_d, m_w_glu, m_b_glu, m_attn_out_g, m_ssm_out_g, m_w_out, m_norm2_g, m_w_mlp_in, m_w_mlp_out, v_norm1_g, v_w_in, v_q_norm_g, v_k_norm_g, v_ssm_lambda_re, v_ssm_lambda_im, v_ssm_log_dt, v_ssm_b_re, v_ssm_b_im, v_ssm_c_re, v_ssm_c_im, v_ssm_d, v_w_glu, v_b_glu, v_attn_out_g, v_ssm_out_g, v_w_out, v_norm2_g, v_w_mlp_in, v_w_mlp_out):
    w = dict(zip(_WEIGHTS, (norm1_g, w_in, q_norm_g, k_norm_g, ssm_lambda_re, ssm_lambda_im, ssm_log_dt, ssm_b_re, ssm_b_im, ssm_c_re, ssm_c_im, ssm_d, w_glu, b_glu, attn_out_g, ssm_out_g, w_out, norm2_g, w_mlp_in, w_mlp_out)))
    m = dict(zip(_WEIGHTS, (m_norm1_g, m_w_in, m_q_norm_g, m_k_norm_g, m_ssm_lambda_re, m_ssm_lambda_im, m_ssm_log_dt, m_ssm_b_re, m_ssm_b_im, m_ssm_c_re, m_ssm_c_im, m_ssm_d, m_w_glu, m_b_glu, m_attn_out_g, m_ssm_out_g, m_w_out, m_norm2_g, m_w_mlp_in, m_w_mlp_out)))
    v = dict(zip(_WEIGHTS, (v_norm1_g, v_w_in, v_q_norm_g, v_k_norm_g, v_ssm_lambda_re, v_ssm_lambda_im, v_ssm_log_dt, v_ssm_b_re, v_ssm_b_im, v_ssm_c_re, v_ssm_c_im, v_ssm_d, v_w_glu, v_b_glu, v_attn_out_g, v_ssm_out_g, v_w_out, v_norm2_g, v_w_mlp_in, v_w_mlp_out)))
    core = lax.axis_index("c").astype(jnp.int32).reshape(1)
    chip = (2 * lax.axis_index("x") + lax.axis_index("y")).astype(jnp.int32).reshape(1)

    me = (2 * chip + core).astype(jnp.int32)

    def landing(own=None, like=None):
        own = jnp.zeros_like(like) if own is None else own
        return lax.dynamic_update_slice(lax.empty((N_DEV, *like.shape), like.dtype), own[None], (me[0], 0, 0))

    (w_in_blocks,) = _all_gather("w_in_all_gather", [w_in.astype(BF16)])
    w_in_full = w_in_blocks.transpose(1, 0, 2).reshape(w_in.shape[0], -1)
    late = [n for n in _BIG if n != "w_in"]
    shards = [w[n].astype(BF16) for n in late]
    w_in_blocks, shards = lax.optimization_barrier((w_in_blocks, shards))
    weights_handle, weights_token = _exchange_start(
        "weights_send", shards, [landing(s, s) for s in shards], per_peer=False)

    def late_weights(after):
        got = dict(zip(late, _exchange_wait("weights_arrive", weights_handle, after)))
        return (got["w_glu"].reshape(-1, w_glu.shape[1]), got["w_out"].reshape(-1, w_out.shape[1]),
                got["w_mlp_in"].transpose(1, 0, 2).reshape(w_mlp_in.shape[0], -1),
                got["w_mlp_out"].reshape(-1, w_mlp_out.shape[1]))

    mlp = ["w_mlp_out", "w_mlp_in"]
    sent = {}

    def send_grads(name, names, own, own_b):
        blocks = lambda g, n: g.reshape(N_DEV, *w[n].shape)
        sent[name + "_own"] = [blocks(g, n) for g, n in zip(own, names)]
        srcs = [blocks(g, n) for g, n in zip(own_b, names)]
        sent[name], token = _exchange_start(name, srcs, [landing(like=s[0]) for s in srcs], per_peer=True)
        return token[0, 0]

    def mlp_grads_ready(g_out, g_out_b, g_in, g_in_b):
        return send_grads("mlp_grads_send", mlp, [g_out, g_in], [g_out_b, g_in_b])

    rest = ["w_in", "w_glu", "w_out"]

    def rest_grads_ready(own, own_b):
        return send_grads("rest_grads_send", rest, own, own_b)

    loss_local, grad_x, g_small, g_big = _local_step(
        x, loss_target, {n: w[n] for n in _SMALL}, w_in_full, late_weights, mlp_grads_ready, rest_grads_ready,
        weights_token[0, 0])

    grads, delta, new_m, new_v = {}, {}, {}, {}
    small = _pack(g_small, last=loss_local)
    small_handle, small_token = _exchange_start("small_grads_send", [small], [landing(small, small)], per_peer=False)

    for send, arrive, names in (("mlp_grads_send", "mlp_grads_arrive", mlp),
                                ("rest_grads_send", "rest_grads_arrive", rest)):
        for n, own, part in zip(names, sent[send + "_own"], _exchange_wait(arrive, sent[send], small_token)):
            grads[n], delta[n], new_m[n], new_v[n] = _adamw_gathered("adamw_" + n, own, part, me, w[n], m[n], v[n])

    (small_parts,) = _exchange_wait("small_grads_arrive", small_handle, new_v[rest[-1]])
    packed = _adamw_small("adamw_small", small_parts, _pack(w), _pack(m), _pack(v))
    for tree, buf in zip((grads, delta, new_m, new_v), packed):
        tree.update(_unpack(buf, w))
    loss = packed[0][-1, -1]

    return (loss, grad_x, *[grads[n] for n in _WEIGHTS], *[delta[n] for n in _WEIGHTS],
            *[new_m[n] for n in _WEIGHTS], *[new_v[n] for n in _WEIGHTS])
```

```python
import functools
import math

import jax
import jax.numpy as jnp
from jax import lax
from jax.experimental import pallas as pl
from jax.experimental.pallas import tpu as pltpu

F32 = jnp.float32
BF16 = jnp.bfloat16

EPS = 1e-6
HEAD_DIM = 64
N_HEADS = 8
SB_WIDTH = 512
SSM_WIDTH = 512
SSM_GROUP = 16
SSM_GROUPS = 32
SSM_STATE = 64
QBLOCK = 128
KBLOCK = 256
N_CHUNK = 8
SSM_COLS = 4
LANES = 128
N_DEV = 8

ADAM_LR = 0.001
ADAM_B1 = 0.9
ADAM_B2 = 0.999
ADAM_EPS = 1e-08
ADAM_WD = 0.01
ADAM_STEP = 10

VMEM_LIMIT = 56 * 1024 * 1024

_NT = (((1,), (1,)), ((), ()))
_NN = (((1,), (0,)), ((), ()))
_TN = (((0,), (0,)), ((), ()))


def _dot(a, b, dims=_NN):
    return lax.dot_general(a, b, dims, preferred_element_type=F32)


def _params(sem):
    return pltpu.CompilerParams(dimension_semantics=sem, vmem_limit_bytes=VMEM_LIMIT)


def _matmul(name, a, b, *, ta=False, tb=False, extras=(), epilogue=None, out_dtypes=(F32,),
            col_blocked=False, tm=1024, tn=512, tk=4096):
    M, K = (a.shape[1], a.shape[0]) if ta else a.shape
    N = b.shape[0] if tb else b.shape[1]
    tm, tn, tk = min(tm, M), min(tn, N), min(tk, K)
    assert M % tm == 0 and N % tn == 0 and K % tk == 0, (name, M, N, K)
    nk = K // tk
    n_ex, n_out = len(extras), len(out_dtypes)
    dims = (((0 if ta else 1,), (1 if tb else 0,)), ((), ()))

    def body(*refs):
        a_ref, b_ref = refs[0], refs[1]
        ex_refs = refs[2:2 + n_ex]
        o_refs = refs[2 + n_ex:2 + n_ex + n_out]
        k = pl.program_id(2)
        part = _dot(a_ref[...].astype(BF16), b_ref[...].astype(BF16), dims)

        def finish(acc):
            outs = (acc,) if epilogue is None else epilogue(acc, *[e[...] for e in ex_refs])
            for o_ref, o in zip(o_refs, outs):
                o_ref[...] = o.astype(o_ref.dtype)

        if nk == 1:
            finish(part)
        else:
            acc_ref = refs[-1]

            @pl.when(k == 0)
            def _():
                acc_ref[...] = part

            @pl.when(jnp.logical_and(k > 0, k < nk - 1))
            def _():
                acc_ref[...] += part

            @pl.when(k == nk - 1)
            def _():
                finish(acc_ref[...] + part)

    a_spec = pl.BlockSpec((tk, tm), lambda i, j, k: (k, i)) if ta else pl.BlockSpec((tm, tk), lambda i, j, k: (i, k))
    b_spec = pl.BlockSpec((tn, tk), lambda i, j, k: (j, k)) if tb else pl.BlockSpec((tk, tn), lambda i, j, k: (k, j))
    ex_specs = [pl.BlockSpec((tm, tn), lambda i, j, k: (i, j)) for _ in extras]
    if col_blocked:
        out_specs = [pl.BlockSpec((None, tm, tn), lambda i, j, k: (j, i, 0)) for _ in out_dtypes]
        out_shape = [jax.ShapeDtypeStruct((N // tn, M, tn), dt) for dt in out_dtypes]
    else:
        out_specs = [pl.BlockSpec((tm, tn), lambda i, j, k: (i, j)) for _ in out_dtypes]
        out_shape = [jax.ShapeDtypeStruct((M, N), dt) for dt in out_dtypes]
    outs = pl.pallas_call(
        body, name=name, grid=(M // tm, N // tn, nk),
        in_specs=[a_spec, b_spec, *ex_specs], out_specs=out_specs, out_shape=out_shape,
        scratch_shapes=[pltpu.VMEM((tm, tn), F32)] if nk > 1 else [],
        compiler_params=_params(("parallel", "parallel", "arbitrary")),
    )(a, b, *extras)
    return outs[0] if n_out == 1 else outs


def _rowwise(name, fn, rows, small, outs, sums=(), tile=256):
    specs, args = [], []
    T = None
    for r in rows:
        arr, cb, w = r if isinstance(r, tuple) else (r, 0, r.shape[1])
        T = arr.shape[0]
        specs.append((w, cb))
        args.append(arr)
    tile = min(tile, T)
    assert T % tile == 0
    n_r, n_s, n_o, n_a = len(rows), len(small), len(outs), len(sums)

    def body(*refs):
        r_refs = refs[:n_r]
        s_refs = refs[n_r:n_r + n_s]
        o_refs = refs[n_r + n_s:n_r + n_s + n_o]
        a_refs = refs[n_r + n_s + n_o:]
        res = fn(*[r[...] for r in r_refs], *[s[...] for s in s_refs])
        res = res if isinstance(res, (tuple, list)) else (res,)
        for o_ref, o in zip(o_refs, res[:n_o]):
            o_ref[...] = o.astype(o_ref.dtype)

        @pl.when(pl.program_id(0) == 0)
        def _():
            for a_ref in a_refs:
                a_ref[...] = jnp.zeros_like(a_ref)

        for a_ref, v in zip(a_refs, res[n_o:]):
            a_ref[...] += v.astype(F32)

    in_specs = [pl.BlockSpec((tile, w), functools.partial(lambda i, cb: (i, cb), cb=cb)) for w, cb in specs]
    in_specs += [pl.BlockSpec(s.shape, functools.partial(lambda i, nd: (0,) * nd, nd=s.ndim)) for s in small]
    out_specs = [pl.BlockSpec((tile, w), lambda i: (i, 0)) for w, _ in outs]
    out_specs += [pl.BlockSpec(s, functools.partial(lambda i, nd: (0,) * nd, nd=len(s))) for s in sums]
    out_shape = [jax.ShapeDtypeStruct((T, w), dt) for w, dt in outs]
    out_shape += [jax.ShapeDtypeStruct(s, F32) for s in sums]
    res = pl.pallas_call(
        body, name=name, grid=(T // tile,), in_specs=in_specs, out_specs=out_specs, out_shape=out_shape,
        compiler_params=_params(("arbitrary",)),
    )(*args, *small)
    return res[0] if len(res) == 1 else res


def _rms(x, g):
    return x * lax.rsqrt(jnp.mean(x * x, axis=-1, keepdims=True) + EPS) * g


def _glu_branch(y, pre, b_glu, g_out):
    g = jax.nn.gelu(y)
    return _rms(g * jax.nn.sigmoid(pre + b_glu), g_out)


def _split_dot(x, tri_bf):
    hi = x.astype(BF16)
    lo = (x - hi.astype(F32)).astype(BF16)
    return _dot(hi, tri_bf) + _dot(lo, tri_bf)


def _softplus(z):
    return jnp.maximum(z, 0.0) + jnp.log(1.0 + jnp.exp(-jnp.abs(z)))


def _head(h):
    return slice(h * HEAD_DIM, (h + 1) * HEAD_DIM)


def _head_mean(x, seg):
    return _split_dot(x, seg) * (1.0 / HEAD_DIM)


def _qk_norm(proj, gq, gk):
    T = proj.shape[0]
    scale = 1.0 / math.sqrt(HEAD_DIM)
    idx = jnp.arange(SB_WIDTH) // HEAD_DIM
    seg = (idx[:, None] == idx[None, :]).astype(BF16)

    def fn(q, k, v, gq_, gk_, seg_):
        qn = q * lax.rsqrt(_head_mean(q * q, seg_) + EPS) * (gq_ * scale)
        kn = k * lax.rsqrt(_head_mean(k * k, seg_) + EPS) * gk_
        return qn, kn, v

    return _rowwise("qk_norm", fn, [(proj, 0, SB_WIDTH), (proj, 1, SB_WIDTH), (proj, 2, SB_WIDTH)], [gq, gk, seg],
                    [(SB_WIDTH, BF16)] * 3)


def _qk_norm_bwd(proj, gq, gk, d_qn, d_kn):
    scale = 1.0 / math.sqrt(HEAD_DIM)
    idx = jnp.arange(SB_WIDTH) // HEAD_DIM
    seg = (idx[:, None] == idx[None, :]).astype(BF16)

    def one(x, g, dy, seg_):
        r = lax.rsqrt(_head_mean(x * x, seg_) + EPS)
        gdy = g * dy
        dx = r * gdy - x * (r * r * r) * _head_mean(gdy * x, seg_)
        dg = jnp.sum(dy * x * r, axis=0, keepdims=True)
        return dx, sum(dg[:, _head(h)] for h in range(N_HEADS))

    def fn(q, k, dqn, dkn, gq_, gk_, seg_):
        dq, dgq = one(q, gq_, dqn * scale, seg_)
        dk, dgk = one(k, gk_, dkn, seg_)
        return dq, dk, dgq, dgk

    return _rowwise("qk_norm_bwd", fn, [(proj, 0, SB_WIDTH), (proj, 1, SB_WIDTH), d_qn, d_kn], [gq, gk, seg],
                    [(SB_WIDTH, BF16)] * 2, sums=[(1, HEAD_DIM)] * 2)


def _split_heads(refs, scratch, L):
    def chunk(i, _):
        r = pl.ds(pl.multiple_of(i * QBLOCK, QBLOCK), QBLOCK)
        for ref, s in zip(refs, scratch):
            for h in range(2):
                s[h, r, :] = ref[r, _head(h)]
        return 0

    lax.fori_loop(0, L // QBLOCK, chunk, 0)


Q_HALVES = KBLOCK // QBLOCK
_CHAINS = [(h, r) for h in range(2) for r in range(Q_HALVES)]


def _valid(i, kb):
    row = lax.broadcasted_iota(jnp.int32, (QBLOCK, KBLOCK), 0)
    col = lax.broadcasted_iota(jnp.int32, (QBLOCK, KBLOCK), 1)
    return col + (kb * KBLOCK - i * QBLOCK) < row


def _attn_fwd(qn, kn, vb, B, L):
    n_pairs = L // KBLOCK
    n_hp = N_HEADS // 2

    def body(q_ref, k_ref, v_ref, o_ref, rt_ref, q_s, k_s, v_s, after_s, z_s):
        _split_heads((q_ref, k_ref, v_ref), (q_s, k_s, v_s), L)
        r2 = lax.broadcasted_iota(jnp.int32, (KBLOCK, KBLOCK), 0)
        c2 = lax.broadcasted_iota(jnp.int32, (KBLOCK, KBLOCK), 1)
        after_s[...] = (r2 > c2).astype(after_s.dtype)

        def q_pair(p, _):
            rows = [pl.ds(pl.multiple_of((p * Q_HALVES + r) * QBLOCK, QBLOCK), QBLOCK) for r in range(Q_HALVES)]
            q_c = [q_s[h, rows[r], :] for h, r in _CHAINS]
            cs = range(len(_CHAINS))

            def scores(kb):
                rk = pl.ds(pl.multiple_of(kb * KBLOCK, KBLOCK), KBLOCK)
                return [_dot(q_c[c], k_s[_CHAINS[c][0], rk, :], _NT) for c in cs]

            def k_block(kb, carry, diagonal):
                rk = pl.ds(pl.multiple_of(kb * KBLOCK, KBLOCK), KBLOCK)
                if diagonal:
                    valid = [_valid(p * Q_HALVES + r, kb) for r in range(Q_HALVES)]
                    keep = lambda c, t: jnp.where(valid[_CHAINS[c][1]], t, 0.0)
                    z = scores(kb)
                else:
                    keep = lambda c, t: t
                    z = [z_s[(kb + 1) & 1, c] for c in cs]
                ahead = scores(jnp.maximum(kb - 1, 0))
                for c in cs:
                    z_s[kb & 1, c] = ahead[c]
                sp = [_softplus(z[c]) for c in cs]
                lom = [keep(c, -sp[c]) for c in cs]
                tail = [_split_dot(lom[c], after_s[...]) + carry[c][0] for c in cs]
                a = [keep(c, jnp.exp(z[c] - sp[c] + tail[c])) for c in cs]
                acc = [carry[c][1] + _dot(a[c].astype(v_s.dtype), v_s[_CHAINS[c][0], rk, :]) for c in cs]
                return tuple((carry[c][0] + jnp.sum(lom[c], axis=1, keepdims=True), acc[c]) for c in cs)

            init = (jnp.zeros((QBLOCK, 1), F32), jnp.zeros((QBLOCK, HEAD_DIM), F32))
            first = k_block(p, (init,) * len(_CHAINS), True)
            res = lax.fori_loop(1, p + 1, lambda n, carry: k_block(p - n, carry, False), first)
            for r in range(Q_HALVES):
                mine = [res[c] for c in cs if _CHAINS[c][1] == r]
                o_ref[rows[r], :] = jnp.concatenate([m[1] for m in mine], axis=1)
                rt_ref[rows[r], :] = jnp.concatenate(
                    [jnp.broadcast_to(m[0], (QBLOCK, HEAD_DIM)) for m in mine], axis=1)
            return 0

        lax.fori_loop(0, n_pairs, q_pair, 0)

    spec = pl.BlockSpec((L, LANES), lambda b, p: (b, p))
    return pl.pallas_call(
        body, name="attn_fwd", grid=(B, n_hp),
        in_specs=[spec] * 3, out_specs=[spec, spec],
        out_shape=[jax.ShapeDtypeStruct((B * L, SB_WIDTH), F32)] * 2,
        scratch_shapes=[pltpu.VMEM((2, L, HEAD_DIM), BF16)] * 3 + [pltpu.VMEM((KBLOCK, KBLOCK), BF16)]
        + [pltpu.VMEM((2, len(_CHAINS), QBLOCK, KBLOCK), F32)],
        compiler_params=_params(("parallel", "parallel")),
    )(qn, kn, vb)


def _attn_bwd(qn, kn, vb, rtot, d_sb, B, L):
    n_pairs = L // KBLOCK
    n_hp = N_HEADS // 2

    def body(q_ref, k_ref, v_ref, rt_ref, do_ref, dq_ref, dk_ref, dv_ref,
             q_s, k_s, v_s, qt_s, dkt_s, dvt_s, after_s, before_s):
        _split_heads((q_ref, k_ref, v_ref), (q_s, k_s, v_s), L)

        def transpose_q(i, _):
            r = pl.ds(pl.multiple_of(i * QBLOCK, QBLOCK), QBLOCK)
            qt_s[:, r] = q_ref[r, :].astype(F32).T.astype(qt_s.dtype)
            return 0

        lax.fori_loop(0, L // QBLOCK, transpose_q, 0)
        dkt_s[...] = jnp.zeros_like(dkt_s)
        dvt_s[...] = jnp.zeros_like(dvt_s)
        r2 = lax.broadcasted_iota(jnp.int32, (KBLOCK, KBLOCK), 0)
        c2 = lax.broadcasted_iota(jnp.int32, (KBLOCK, KBLOCK), 1)
        after_s[...] = (r2 > c2).astype(after_s.dtype)
        before_s[...] = (r2 < c2).astype(before_s.dtype)

        def q_pair(p, _):
            rows = [pl.ds(pl.multiple_of((p * Q_HALVES + r) * QBLOCK, QBLOCK), QBLOCK) for r in range(Q_HALVES)]
            pair = pl.ds(pl.multiple_of(p * KBLOCK, KBLOCK), KBLOCK)
            do2 = do_ref[pair, :]
            do_t = do2.T.astype(v_s.dtype)
            tot2 = rt_ref[pair, :]
            cs = range(len(_CHAINS))
            hs = range(2)
            q_c = [q_s[h, rows[r], :] for h, r in _CHAINS]
            do_c = [do2[r * QBLOCK:(r + 1) * QBLOCK, _head(h)].astype(v_s.dtype) for h, r in _CHAINS]
            total = [tot2[r * QBLOCK:(r + 1) * QBLOCK, h * HEAD_DIM:h * HEAD_DIM + 1] for h, r in _CHAINS]
            qt_h = [qt_s[_head(h), pair] for h in hs]
            dot_h = [do_t[_head(h), :] for h in hs]

            def k_block(kb, carry, diagonal):
                rk = pl.ds(pl.multiple_of(kb * KBLOCK, KBLOCK), KBLOCK)
                if diagonal:
                    valid = [_valid(p * Q_HALVES + r, kb) for r in range(Q_HALVES)]
                    keep = lambda c, t: jnp.where(valid[_CHAINS[c][1]], t, 0.0)
                else:
                    keep = lambda c, t: t
                k_b = [k_s[h, rk, :] for h in hs]
                z = [_dot(q_c[c], k_b[_CHAINS[c][0]], _NT) for c in cs]
                da = [_dot(do_c[c], v_s[_CHAINS[c][0], rk, :], _NT) for c in cs]
                sp = [_softplus(z[c]) for c in cs]
                lsig = [z[c] - sp[c] for c in cs]
                lom = [keep(c, -sp[c]) for c in cs]
                lom_sum = [jnp.sum(lom[c], axis=1, keepdims=True) for c in cs]
                tail = [_split_dot(lom[c], after_s[...]) + (total[c] - carry[c][0] - lom_sum[c]) for c in cs]
                a = [keep(c, jnp.exp(lsig[c] + tail[c])) for c in cs]
                dla = [a[c] * da[c] for c in cs]
                for h in hs:
                    a_h = jnp.concatenate([a[c].astype(v_s.dtype) for c in cs if _CHAINS[c][0] == h], axis=0)
                    dvt_s[_head(h), rk] += _dot(dot_h[h], a_h)
                d_lom = [carry[c][1] + _split_dot(dla[c], before_s[...]) for c in cs]
                beta = [jnp.exp(lsig[c]) for c in cs]
                dz_b = [(dla[c] * (1.0 - beta[c]) - keep(c, beta[c] * d_lom[c])).astype(v_s.dtype) for c in cs]
                dq_acc = [carry[c][2] + _dot(dz_b[c], k_b[_CHAINS[c][0]]) for c in cs]
                for h in hs:
                    dz_h = jnp.concatenate([dz_b[c] for c in cs if _CHAINS[c][0] == h], axis=0)
                    dkt_s[_head(h), rk] += _dot(qt_h[h], dz_h)
                return tuple((carry[c][0] + lom_sum[c], carry[c][1] + jnp.sum(dla[c], axis=1, keepdims=True),
                              dq_acc[c]) for c in cs)

            zero = jnp.zeros((QBLOCK, 1), F32)
            init = (zero, zero, jnp.zeros((QBLOCK, HEAD_DIM), F32))
            before = lax.fori_loop(0, p, lambda kb, carry: k_block(kb, carry, False), (init,) * len(_CHAINS))
            res = k_block(p, before, True)
            for r in range(Q_HALVES):
                dq_ref[rows[r], :] = jnp.concatenate([res[c][2] for c in cs if _CHAINS[c][1] == r], axis=1)
            return 0

        lax.fori_loop(0, n_pairs, q_pair, 0)

        def transpose_out(i, _):
            r = pl.ds(pl.multiple_of(i * QBLOCK, QBLOCK), QBLOCK)
            dk_ref[r, :] = dkt_s[:, r].T
            dv_ref[r, :] = dvt_s[:, r].T.astype(dv_ref.dtype)
            return 0

        lax.fori_loop(0, L // QBLOCK, transpose_out, 0)

    spec = pl.BlockSpec((L, LANES), lambda b, p: (b, p))
    return pl.pallas_call(
        body, name="attn_bwd", grid=(B, n_hp),
        in_specs=[spec] * 5, out_specs=[spec] * 3,
        out_shape=[jax.ShapeDtypeStruct((B * L, SB_WIDTH), F32)] * 2 + [jax.ShapeDtypeStruct((B * L, SB_WIDTH), BF16)],
        scratch_shapes=[pltpu.VMEM((2, L, HEAD_DIM), BF16)] * 3 + [pltpu.VMEM((LANES, L), BF16)]
        + [pltpu.VMEM((LANES, L), F32)] * 2 + [pltpu.VMEM((KBLOCK, KBLOCK), BF16)] * 2,
        compiler_params=_params(("parallel", "parallel")),
    )(qn, kn, vb, rtot, d_sb)


def _ssm_discretise(lam_re, lam_im, log_dt, b_re, b_im):
    dt = jnp.exp(log_dt)
    mag = jnp.exp(lam_re * dt)
    lbr = mag * jnp.cos(lam_im * dt)
    lbi = mag * jnp.sin(lam_im * dt)
    den = lam_re * lam_re + lam_im * lam_im
    nr, ni = lbr - 1.0, lbi
    cr = (nr * lam_re + ni * lam_im) / den
    ci = (ni * lam_re - nr * lam_im) / den
    return lbr, lbi, cr * b_re - ci * b_im, cr * b_im + ci * b_re


def _ssm_prep(lam_re, lam_im, log_dt, b_re_t, b_im_t):
    def body(lr, li, ld, br, bi, o_lr, o_li, o_br, o_bi):
        res = _ssm_discretise(lr[...], li[...], ld[...], br[...], bi[...])
        for o, v in zip((o_lr, o_li, o_br, o_bi), res):
            o[...] = v

    return pl.pallas_call(
        body, name="ssm_prep",
        out_shape=[jax.ShapeDtypeStruct(lam_re.shape, F32)] * 2 + [jax.ShapeDtypeStruct(b_re_t.shape, F32)] * 2,
    )(lam_re, lam_im, log_dt, b_re_t, b_im_t)


def _ssm_prep_bwd(lam_re, lam_im, log_dt, b_re_t, b_im_t, d_lr, d_li, d_br, d_bi):
    def body(lr, li, ld, br, bi, g_lr, g_li, g_br, g_bi, o_lr, o_li, o_ld, o_br, o_bi):
        _, vjp = jax.vjp(_ssm_discretise, lr[...], li[...], ld[...], br[...], bi[...])
        res = vjp((g_lr[...], g_li[...], g_br[...], g_bi[...]))
        for o, v in zip((o_lr, o_li, o_ld, o_br, o_bi), res):
            o[...] = v

    return pl.pallas_call(
        body, name="ssm_prep_bwd",
        out_shape=[jax.ShapeDtypeStruct(lam_re.shape, F32)] * 2 + [jax.ShapeDtypeStruct(log_dt.shape, F32)]
        + [jax.ShapeDtypeStruct(b_re_t.shape, F32)] * 2,
    )(lam_re, lam_im, log_dt, b_re_t, b_im_t, d_lr, d_li, d_br, d_bi)


def _block_diag(m):
    m4 = m.reshape(SSM_COLS, 8, SSM_GROUP, SSM_STATE)
    return jnp.einsum("aghp,gk->aghkp", m4, jnp.eye(8, dtype=m.dtype)).reshape(SSM_COLS, LANES, 512)


def _block_diag_take(d):
    d6 = d.reshape(SSM_COLS, 8, SSM_GROUP, 2, 8, SSM_STATE)
    return jnp.einsum("aghrgp->raghp", d6).reshape(2, SSM_GROUPS, SSM_GROUP, SSM_STATE)


def _cmul(ar, ai, br, bi):
    return ar * br - ai * bi, ar * bi + ai * br


def _power(lr, li, n):
    assert n & (n - 1) == 0
    for _ in range(n.bit_length() - 1):
        lr, li = _cmul(lr, li, lr, li)
    return lr, li


def _ssm_fwd(u_p, w_b, lam_r, lam_i, c_m, d_skip, B, L, tj):
    J = L // N_CHUNK
    njt = J // tj
    R = tj * N_CHUNK
    H = 512

    def body(u_ref, wb_ref, lr_ref, li_ref, cm_ref, d_ref, y_ref, x_ref, xin_ref, bu_s, st_s, xin_s):
        ph, jt = pl.program_id(2), pl.program_id(3)
        lr, li = lr_ref[...], li_ref[...]

        @pl.when(jnp.logical_and(ph == 0, jt == 0))
        def _():
            st_s[...] = jnp.zeros_like(st_s)

        bu_s[...] = _dot(u_ref[...].astype(BF16), wb_ref[...].astype(BF16))

        def scan(store):
            def step(j, carry):
                xr, xi = carry
                r = pl.ds(pl.multiple_of(j * N_CHUNK, N_CHUNK), N_CHUNK)
                nr = lr * xr - li * xi + bu_s[r, 0:H]
                ni = lr * xi + li * xr + bu_s[r, H:2 * H]
                if store:
                    x_ref[r, 0:H] = nr
                    x_ref[r, H:2 * H] = ni
                return nr, ni

            xr, xi = lax.fori_loop(0, tj, step, (st_s[:, 0:H], st_s[:, H:2 * H]))
            st_s[:, 0:H] = xr
            st_s[:, H:2 * H] = xi

        @pl.when(ph == 0)
        def _():
            scan(False)

            @pl.when(jt == njt - 1)
            def _():
                pr, pi = _power(lr[0:1], li[0:1], J)
                xin_s[0:1, :] = jnp.zeros((1, 2 * H), F32)
                for c in range(1, N_CHUNK):
                    qr, qi = _cmul(pr, pi, xin_s[c - 1:c, 0:H], xin_s[c - 1:c, H:2 * H])
                    xin_s[c:c + 1, 0:H] = qr + st_s[c - 1:c, 0:H]
                    xin_s[c:c + 1, H:2 * H] = qi + st_s[c - 1:c, H:2 * H]
                xin_ref[...] = xin_s[...]
                st_s[...] = xin_s[...]

        @pl.when(ph == 1)
        def _():
            scan(True)
            y = _dot(x_ref[...].astype(BF16), cm_ref[...].astype(BF16))
            y_ref[...] = y + d_ref[...] * u_ref[...]

    return pl.pallas_call(
        body, name="ssm_fwd", grid=(SSM_COLS, B, 2, njt),
        in_specs=[
            pl.BlockSpec((None, R, LANES), lambda i, b, ph, jt: (b, jt, i)),
            pl.BlockSpec((None, LANES, 2 * H), lambda i, b, ph, jt: (i, 0, 0)),
            pl.BlockSpec((None, N_CHUNK, H), lambda i, b, ph, jt: (i, 0, 0)),
            pl.BlockSpec((None, N_CHUNK, H), lambda i, b, ph, jt: (i, 0, 0)),
            pl.BlockSpec((None, 2 * H, LANES), lambda i, b, ph, jt: (i, 0, 0)),
            pl.BlockSpec((1, LANES), lambda i, b, ph, jt: (0, i)),
        ],
        out_specs=[
            pl.BlockSpec((None, R, LANES), lambda i, b, ph, jt: (b, jt * ph, i)),
            pl.BlockSpec((None, R, 2 * H), lambda i, b, ph, jt: (b, jt * ph, i)),
            pl.BlockSpec((None, None, N_CHUNK, 2 * H), lambda i, b, ph, jt: (b, i, 0, 0)),
        ],
        out_shape=[
            jax.ShapeDtypeStruct((B, L, SSM_WIDTH), F32),
            jax.ShapeDtypeStruct((B, L, SSM_COLS * 2 * H), F32),
            jax.ShapeDtypeStruct((B, SSM_COLS, N_CHUNK, 2 * H), F32),
        ],
        scratch_shapes=[pltpu.VMEM((R, 2 * H), F32), pltpu.VMEM((N_CHUNK, 2 * H), F32), pltpu.VMEM((N_CHUNK, 2 * H), F32)],
        compiler_params=_params(("arbitrary",) * 4),
    )(u_p, w_b, lam_r, lam_i, c_m, d_skip)


def _ssm_bwd(dy_p, u_p, x, xin, w_bt, lam_r, lam_i, c_mt, d_skip, B, L, tj):
    J = L // N_CHUNK
    njt = J // tj
    R = tj * N_CHUNK
    H = 512
    x4 = x.reshape(B, J, N_CHUNK, SSM_COLS * 2 * H)

    def body(dy_ref, u_ref, x_ref, xp_ref, xin_ref, wbt_ref, lr_ref, li_ref, cmt_ref, d_ref,
             du_ref, dwb_ref, dcm_ref, dlr_ref, dli_ref, dd_ref, ca_s, a_s, st_s, dl_s):
        b, ph, jt = pl.program_id(1), pl.program_id(2), pl.program_id(3)
        jr = njt - 1 - jt
        lr, li = lr_ref[...], -li_ref[...]

        @pl.when(jnp.logical_and(b == 0, jnp.logical_and(ph == 0, jt == 0)))
        def _():
            dwb_ref[...] = jnp.zeros_like(dwb_ref)
            dcm_ref[...] = jnp.zeros_like(dcm_ref)
            dlr_ref[...] = jnp.zeros_like(dlr_ref)
            dli_ref[...] = jnp.zeros_like(dli_ref)
            dd_ref[...] = jnp.zeros_like(dd_ref)
            dl_s[...] = jnp.zeros_like(dl_s)

        @pl.when(jnp.logical_and(ph == 0, jt == 0))
        def _():
            st_s[...] = jnp.zeros_like(st_s)

        ca_s[...] = _dot(dy_ref[...].astype(BF16), cmt_ref[...].astype(BF16))

        def scan(store):
            def step(n, carry):
                ar, ai = carry
                r = pl.ds(pl.multiple_of((tj - 1 - n) * N_CHUNK, N_CHUNK), N_CHUNK)
                nr = lr * ar - li * ai + ca_s[r, 0:H]
                ni = lr * ai + li * ar + ca_s[r, H:2 * H]
                if store:
                    a_s[r, 0:H] = nr
                    a_s[r, H:2 * H] = ni
                return nr, ni

            ar, ai = lax.fori_loop(0, tj, step, (st_s[:, 0:H], st_s[:, H:2 * H]))
            st_s[:, 0:H] = ar
            st_s[:, H:2 * H] = ai

        @pl.when(ph == 0)
        def _():
            scan(False)

            @pl.when(jt == njt - 1)
            def _():
                pr, pi = _power(lr[0:1], li[0:1], J)
                a_s[N_CHUNK - 1:N_CHUNK, :] = jnp.zeros((1, 2 * H), F32)
                for c in range(N_CHUNK - 2, -1, -1):
                    qr, qi = _cmul(pr, pi, a_s[c + 1:c + 2, 0:H], a_s[c + 1:c + 2, H:2 * H])
                    a_s[c:c + 1, 0:H] = qr + st_s[c + 1:c + 2, 0:H]
                    a_s[c:c + 1, H:2 * H] = qi + st_s[c + 1:c + 2, H:2 * H]
                st_s[...] = a_s[0:N_CHUNK, :]

        @pl.when(ph == 1)
        def _():
            scan(True)
            dy = dy_ref[...]
            u = u_ref[...]
            a_b = a_s[...].astype(BF16)
            du_ref[...] = (_dot(a_b, wbt_ref[...].astype(BF16)) + d_ref[...] * dy).astype(du_ref.dtype)
            dwb_ref[...] += _dot(u.astype(BF16), a_b, _TN)
            dcm_ref[...] += _dot(x_ref[...].astype(BF16), dy.astype(BF16), _TN)
            dd_ref[...] += jnp.sum(dy * u, axis=0, keepdims=True)

            first = jnp.where(jr == 0, xin_ref[...], xp_ref[...])
            a0r, a0i = a_s[0:N_CHUNK, 0:H], a_s[0:N_CHUNK, H:2 * H]
            acc0 = (a0r * first[:, 0:H] + a0i * first[:, H:2 * H], a0i * first[:, 0:H] - a0r * first[:, H:2 * H])

            def step(j, carry):
                sr, si = carry
                r = pl.ds(pl.multiple_of(j * N_CHUNK, N_CHUNK), N_CHUNK)
                rp = pl.ds(pl.multiple_of((j - 1) * N_CHUNK, N_CHUNK), N_CHUNK)
                ar, ai = a_s[r, 0:H], a_s[r, H:2 * H]
                xr, xi = x_ref[rp, 0:H], x_ref[rp, H:2 * H]
                return sr + ar * xr + ai * xi, si + ai * xr - ar * xi

            sr, si = lax.fori_loop(1, tj, step, acc0)
            dl_s[:, 0:H] += sr
            dl_s[:, H:2 * H] += si

            @pl.when(jnp.logical_and(b == B - 1, jt == njt - 1))
            def _():
                dlr_ref[...] = jnp.sum(dl_s[:, 0:H], axis=0, keepdims=True)
                dli_ref[...] = jnp.sum(dl_s[:, H:2 * H], axis=0, keepdims=True)
                dl_s[...] = jnp.zeros_like(dl_s)

    rev = lambda ph, jt: (njt - 1 - jt) * ph + (njt - 1) * (1 - ph)
    return pl.pallas_call(
        body, name="ssm_bwd", grid=(SSM_COLS, B, 2, njt),
        in_specs=[
            pl.BlockSpec((None, R, LANES), lambda i, b, ph, jt: (b, njt - 1 - jt, i)),
            pl.BlockSpec((None, R, LANES), lambda i, b, ph, jt: (b, njt - 1 - jt, i)),
            pl.BlockSpec((None, R, 2 * H), lambda i, b, ph, jt: (b, rev(ph, jt), i)),
            pl.BlockSpec((None, None, N_CHUNK, 2 * H),
                         lambda i, b, ph, jt: (b, jnp.maximum((njt - 1 - jt) * tj - 1, 0), 0, i)),
            pl.BlockSpec((None, None, N_CHUNK, 2 * H), lambda i, b, ph, jt: (b, i, 0, 0)),
            pl.BlockSpec((None, 2 * H, LANES), lambda i, b, ph, jt: (i, 0, 0)),
            pl.BlockSpec((None, N_CHUNK, H), lambda i, b, ph, jt: (i, 0, 0)),
            pl.BlockSpec((None, N_CHUNK, H), lambda i, b, ph, jt: (i, 0, 0)),
            pl.BlockSpec((None, LANES, 2 * H), lambda i, b, ph, jt: (i, 0, 0)),
            pl.BlockSpec((1, LANES), lambda i, b, ph, jt: (0, i)),
        ],
        out_specs=[
            pl.BlockSpec((None, R, LANES), lambda i, b, ph, jt: (b, rev(ph, jt), i)),
            pl.BlockSpec((None, LANES, 2 * H), lambda i, b, ph, jt: (i, 0, 0)),
            pl.BlockSpec((None, 2 * H, LANES), lambda i, b, ph, jt: (i, 0, 0)),
            pl.BlockSpec((None, 1, H), lambda i, b, ph, jt: (i, 0, 0)),
            pl.BlockSpec((None, 1, H), lambda i, b, ph, jt: (i, 0, 0)),
            pl.BlockSpec((1, LANES), lambda i, b, ph, jt: (0, i)),
        ],
        out_shape=[
            jax.ShapeDtypeStruct((B, L, SSM_WIDTH), BF16),
            jax.ShapeDtypeStruct((SSM_COLS, LANES, 2 * H), F32),
            jax.ShapeDtypeStruct((SSM_COLS, 2 * H, LANES), F32),
            jax.ShapeDtypeStruct((SSM_COLS, 1, H), F32),
            jax.ShapeDtypeStruct((SSM_COLS, 1, H), F32),
            jax.ShapeDtypeStruct((1, SSM_WIDTH), F32),
        ],
        scratch_shapes=[pltpu.VMEM((R, 2 * H), F32), pltpu.VMEM((R, 2 * H), F32),
                        pltpu.VMEM((N_CHUNK, 2 * H), F32), pltpu.VMEM((N_CHUNK, 2 * H), F32)],
        compiler_params=_params(("arbitrary",) * 4),
    )(dy_p, u_p, x, x4, xin, w_bt, lam_r, lam_i, c_mt, d_skip)


def _to_scan_layout(t, B, L):
    C = t.shape[-1]
    return t.reshape(B, N_CHUNK, L // N_CHUNK, C).transpose(0, 2, 1, 3).reshape(B, L, C)


def _from_scan_layout(t, B, L):
    C = t.shape[-1]
    return t.reshape(B, L // N_CHUNK, N_CHUNK, C).transpose(0, 2, 1, 3).reshape(B * L, C)


def _local_step(x, target, p, w_in, late_weights, mlp_grads_ready=None, rest_grads_ready=None, order=None, *,
                ssm_tile=128):
    B, L, D = x.shape
    T = B * L
    x2 = x.reshape(T, D)
    row = lambda v: v.reshape(1, -1)
    g1, g2, ga, gs, b_glu = row(p["norm1_g"]), row(p["norm2_g"]), row(p["attn_out_g"]), row(p["ssm_out_g"]), row(p["b_glu"])
    g1_first = g1 if order is None else g1 + order
    gq8 = jnp.tile(row(p["q_norm_g"]), (1, N_HEADS))
    gk8 = jnp.tile(row(p["k_norm_g"]), (1, N_HEADS))

    G, P, Hh = SSM_GROUPS, SSM_STATE, SSM_GROUP
    lam_re3, lam_im3 = p["ssm_lambda_re"].reshape(G, 1, P), p["ssm_lambda_im"].reshape(G, 1, P)
    log_dt3 = p["ssm_log_dt"].reshape(G, 1, 1)
    b_re_t, b_im_t = p["ssm_b_re"].transpose(0, 2, 1), p["ssm_b_im"].transpose(0, 2, 1)
    lbr, lbi, bbr, bbi = _ssm_prep(lam_re3, lam_im3, log_dt3, b_re_t, b_im_t)
    w_b = jnp.concatenate([_block_diag(bbr), _block_diag(bbi)], axis=2)
    c_mt = jnp.concatenate([_block_diag(p["ssm_c_re"]), -_block_diag(p["ssm_c_im"])], axis=2)
    w_bt, c_m = w_b.transpose(0, 2, 1), c_mt.transpose(0, 2, 1)
    lam_r = jnp.broadcast_to(lbr.reshape(SSM_COLS, 1, 512), (SSM_COLS, N_CHUNK, 512))
    lam_i = jnp.broadcast_to(lbi.reshape(SSM_COLS, 1, 512), (SSM_COLS, N_CHUNK, 512))
    d_skip = p["ssm_d"].reshape(1, SSM_WIDTH)

    xn = _rowwise("norm1", _rms, [x2], [g1_first], [(D, BF16)])
    proj = _matmul("proj", xn, w_in, tn=1024)
    qn, kn, vb = _qk_norm(proj, gq8, gk8)
    sb, rtot = _attn_fwd(qn, kn, vb, B, L)
    u_p = _to_scan_layout(proj[:, 3 * SB_WIDTH:], B, L)
    y_p, xs, xin = _ssm_fwd(u_p, w_b, lam_r, lam_i, c_m, d_skip, B, L, ssm_tile)
    y2 = y_p.reshape(T, SSM_WIDTH)
    gel = _rowwise("gelu", jax.nn.gelu, [y2], [], [(SSM_WIDTH, BF16)])
    w_glu, w_out, w_mlp_in, w_mlp_out = late_weights(gel)
    pre = _matmul("glu_gate", gel, w_glu)
    ssm_n = _rowwise("glu_out", _glu_branch, [y2, pre], [b_glu, gs], [(SSM_WIDTH, BF16)])
    sb_n = _rowwise("attn_out_norm", _rms, [sb], [ga], [(SB_WIDTH, BF16)])
    mixed = jnp.concatenate([sb_n, _from_scan_layout(ssm_n, B, L)], axis=1)
    h1 = _matmul("out_proj", mixed, w_out, extras=[x2], epilogue=lambda acc, r: (acc + r,), tn=1024)
    hn = _rowwise("norm2", _rms, [h1], [g2], [(D, BF16)])
    act, a_pre = _matmul("mlp_in", hn, w_mlp_in, out_dtypes=(BF16, BF16), tn=1024,
                         epilogue=lambda acc: (jnp.square(jnp.maximum(acc, 0.0)), acc))
    out = _matmul("mlp_out", act, w_mlp_out, extras=[h1], epilogue=lambda acc, r: (acc + r,))

    def loss_fn(o, t):
        diff = o - t
        part = jnp.sum(jnp.sum(diff * diff, axis=0, keepdims=True), axis=1, keepdims=True)
        d = diff * (1.0 / D)
        return d, d, part * (0.5 / D)

    d_out, d_out_b, loss = _rowwise("loss", loss_fn, [out, target.reshape(T, D)], [], [(D, F32), (D, BF16)],
                                    sums=[(1, 1)])

    d_apre = _matmul("mlp_out_dx", d_out_b, w_mlp_out, tb=True, extras=[a_pre], out_dtypes=(BF16,), tn=1024,
                     epilogue=lambda acc, ap: (acc * (2.0 * jnp.maximum(ap.astype(F32), 0.0)),))
    both = lambda acc: (acc, acc)
    g_w_mlp_out, g_w_mlp_out_b = _matmul("mlp_out_dw", act, d_out_b, ta=True, out_dtypes=(F32, BF16), epilogue=both)
    g_w_mlp_in, g_w_mlp_in_b = _matmul("mlp_in_dw", hn, d_apre, ta=True, col_blocked=True, out_dtypes=(F32, BF16),
                                       epilogue=both, tn=w_mlp_in.shape[1] // N_DEV)
    if mlp_grads_ready is not None:
        g2 = g2 + mlp_grads_ready(g_w_mlp_out, g_w_mlp_out_b, g_w_mlp_in, g_w_mlp_in_b)
    d_hn = _matmul("mlp_in_dx", d_apre, w_mlp_in, tb=True)

    def norm_bwd_res(h, dy, res, g):
        _, vjp = jax.vjp(_rms, h, g)
        dh, dg = vjp(dy)
        return res + dh, dg

    def norm_bwd_res2(h, dy, res, g):
        d, dg = norm_bwd_res(h, dy, res, g)
        return d, d, dg

    d_h1, d_h1_b, g_norm2 = _rowwise("norm2_bwd", norm_bwd_res2, [h1, d_hn, d_out], [g2], [(D, F32), (D, BF16)],
                                     sums=[(1, D)])

    d_mixed = _matmul("out_proj_dx", d_h1_b, w_out, tb=True, tn=1024)
    g_w_out, g_w_out_b = _matmul("out_proj_dw", mixed, d_h1_b, ta=True, out_dtypes=(F32, BF16), epilogue=both)

    def norm_bwd(h, dy, g):
        _, vjp = jax.vjp(_rms, h, g)
        return vjp(dy)

    d_sb, g_attn_out = _rowwise("attn_out_norm_bwd", norm_bwd, [sb, (d_mixed, 0, SB_WIDTH)], [ga],
                                [(SB_WIDTH, F32)], sums=[(1, SB_WIDTH)])
    d_ssm_n = _to_scan_layout(d_mixed[:, SB_WIDTH:], B, L).reshape(T, SSM_WIDTH)

    def glu_bwd(y, pre_, dy, bg, g):
        _, vjp = jax.vjp(_glu_branch, y, pre_, bg, g)
        d_y, d_pre, d_bg, d_g = vjp(dy)
        return d_y, d_pre, d_bg, d_g

    d_y_direct, d_pre, g_b_glu, g_ssm_out = _rowwise(
        "glu_out_bwd", glu_bwd, [y2, pre, d_ssm_n], [b_glu, gs], [(SSM_WIDTH, F32), (SSM_WIDTH, BF16)],
        sums=[(1, SSM_WIDTH), (1, SSM_WIDTH)])
    d_gel = _matmul("glu_gate_dx", d_pre, w_glu, tb=True)
    g_w_glu, g_w_glu_b = _matmul("glu_gate_dw", gel, d_pre, ta=True, out_dtypes=(F32, BF16), epilogue=both)

    def gelu_bwd(y, dg, dy0):
        _, vjp = jax.vjp(jax.nn.gelu, y)
        return dy0 + vjp(dg)[0]

    d_y = _rowwise("gelu_bwd", gelu_bwd, [y2, d_gel, d_y_direct], [], [(SSM_WIDTH, F32)])

    du_p, d_wb, d_cm, d_lr, d_li, g_d = _ssm_bwd(
        d_y.reshape(B, L, SSM_WIDTH), u_p, xs, xin, w_bt, lam_r, lam_i, c_mt, d_skip, B, L, ssm_tile)
    d_bb = _block_diag_take(d_wb.reshape(SSM_COLS, LANES, 2, 512))
    d_c = _block_diag_take(d_cm.transpose(0, 2, 1).reshape(SSM_COLS, LANES, 2, 512))
    g_lam_re, g_lam_im, g_log_dt, g_b_re_t, g_b_im_t = _ssm_prep_bwd(
        lam_re3, lam_im3, log_dt3, b_re_t, b_im_t,
        d_lr.reshape(G, 1, P), d_li.reshape(G, 1, P), d_bb[0], d_bb[1])
    d_qn, d_kn, d_v = _attn_bwd(qn, kn, vb, rtot, d_sb, B, L)
    d_q, d_k, g_q, g_k = _qk_norm_bwd(proj, gq8, gk8, d_qn, d_kn)

    d_proj = jnp.concatenate([d_q, d_k, d_v, _from_scan_layout(du_p, B, L)], axis=1)
    g_w_in, g_w_in_b = _matmul("proj_dw", xn, d_proj, ta=True, col_blocked=True, out_dtypes=(F32, BF16),
                               epilogue=both, tn=w_in.shape[1] // N_DEV)
    if rest_grads_ready is not None:
        g1 = g1 + rest_grads_ready([g_w_in, g_w_glu, g_w_out], [g_w_in_b, g_w_glu_b, g_w_out_b])
    d_xn = _matmul("proj_dx", d_proj, w_in, tb=True, tn=1024)
    grad_x, g_norm1 = _rowwise("norm1_bwd", norm_bwd_res, [x2, d_xn, d_h1], [g1], [(D, F32)], sums=[(1, D)])

    small = {
        "norm1_g": g_norm1.reshape(-1),
        "q_norm_g": g_q.reshape(-1),
        "k_norm_g": g_k.reshape(-1),
        "ssm_lambda_re": g_lam_re.reshape(G, P),
        "ssm_lambda_im": g_lam_im.reshape(G, P),
        "ssm_log_dt": g_log_dt.reshape(G),
        "ssm_b_re": g_b_re_t.transpose(0, 2, 1),
        "ssm_b_im": g_b_im_t.transpose(0, 2, 1),
        "ssm_c_re": d_c[0],
        "ssm_c_im": -d_c[1],
        "ssm_d": g_d.reshape(G, Hh),
        "b_glu": g_b_glu.reshape(-1),
        "attn_out_g": g_attn_out.reshape(-1),
        "ssm_out_g": g_ssm_out.reshape(-1),
        "norm2_g": g_norm2.reshape(-1),
    }
    big = {"w_in": g_w_in, "w_glu": g_w_glu, "w_out": g_w_out, "w_mlp_in": g_w_mlp_in, "w_mlp_out": g_w_mlp_out}
    return loss[0, 0], grad_x.reshape(B, L, D), small, big


_ANY = pl.BlockSpec(memory_space=pl.ANY)
_MESH = pl.DeviceIdType.MESH


def _all_gather(name, shards):
    n = len(shards)

    def body(*refs):
        in_refs, out_refs = refs[:n], refs[n:2 * n]
        send_sems, recv_sems, local_sems = refs[2 * n:]
        x, y, c = lax.axis_index("x"), lax.axis_index("y"), lax.axis_index("c")
        me, sibling = (x, y, c), (x, y, 1 - c)
        chips = [(1 - x, y), (x, 1 - y), (1 - x, 1 - y)]

        def copy(a, k, block, to, src=None):
            px, py, pc = block
            rows = out_refs[a].at[4 * px + 2 * py + pc]
            return pltpu.make_async_remote_copy(
                src_ref=rows if src is None else src, dst_ref=rows, send_sem=send_sems.at[a, k],
                recv_sem=recv_sems.at[a, k], device_id=to, device_id_type=_MESH)

        mine = [pltpu.make_async_copy(in_refs[a], out_refs[a].at[4 * x + 2 * y + c], local_sems.at[a]) for a in range(n)]
        first, passed = [], []
        for a in range(n):
            mine[a].start()
            first.append(copy(a, 0, me, sibling, src=in_refs[a]))
            first += [copy(a, 1 + j, me, (*chip, c), src=in_refs[a]) for j, chip in enumerate(chips)]
        for cp in first:
            cp.start()
        for j, chip in enumerate(chips):
            for a in range(n):
                copy(a, 1 + j, (*chip, c), me).wait_recv()
                fwd = copy(a, 4 + j, (*chip, c), sibling)
                fwd.start()
                passed.append(fwd)
        for a in range(n):
            copy(a, 0, sibling, me).wait_recv()
            for j, chip in enumerate(chips):
                copy(a, 4 + j, (*chip, 1 - c), me).wait_recv()
        for cp in first + passed:
            cp.wait_send()
        for cp in mine:
            cp.wait()

    return pl.pallas_call(
        body, name=name,
        in_specs=[_ANY] * n, out_specs=[_ANY] * n,
        out_shape=[jax.ShapeDtypeStruct((N_DEV, *s.shape), s.dtype) for s in shards],
        scratch_shapes=[pltpu.SemaphoreType.DMA((n, 7)), pltpu.SemaphoreType.DMA((n, 7)), pltpu.SemaphoreType.DMA((n,))],
    )(*shards)


_HBM = pl.BlockSpec(memory_space=pltpu.HBM)
_SEM = pl.BlockSpec(memory_space=pltpu.SEMAPHORE)
_EFFECT = pltpu.SideEffectType.DATAFLOW_SIDE_EFFECTING
_FLIPS = [(dx, dy, dc) for dx in (0, 1) for dy in (0, 1) for dc in (0, 1) if (dx, dy, dc) != (0, 0, 0)]


def _exchange_start(name, srcs, lands, per_peer):
    n = len(srcs)

    def body(*refs):
        src_refs, land_refs = refs[:n], refs[n:2 * n]
        send_sems, recv_sems = refs[2 * n:3 * n], refs[3 * n:4 * n]
        token = refs[-1]
        x, y, c = lax.axis_index("x"), lax.axis_index("y"), lax.axis_index("c")
        me = 4 * x + 2 * y + c
        for dx, dy, dc in _FLIPS:
            px, py, pc = (1 - x if dx else x), (1 - y if dy else y), (1 - c if dc else c)
            for a in range(n):
                pltpu.make_async_remote_copy(
                    src_ref=src_refs[a].at[4 * px + 2 * py + pc] if per_peer else src_refs[a],
                    dst_ref=land_refs[a].at[me], send_sem=send_sems[a], recv_sem=recv_sems[a],
                    device_id=(px, py, pc), device_id_type=_MESH).start()
        token[...] = jnp.zeros_like(token)

    hbm = lambda t: pltpu.with_memory_space_constraint(t, pltpu.HBM)
    res = pl.pallas_call(
        body, name=name,
        out_shape=(*[pltpu.SemaphoreType.DMA(())] * (2 * n), *[pltpu.HBM(t.shape, t.dtype) for t in (*srcs, *lands)],
                   jax.ShapeDtypeStruct((8, LANES), F32)),
        in_specs=[_HBM] * (2 * n),
        out_specs=(*[_SEM] * (2 * n), *[_HBM] * (2 * n), pl.BlockSpec(memory_space=pltpu.VMEM)),
        input_output_aliases={i: 2 * n + i for i in range(2 * n)},
        compiler_params=pltpu.CompilerParams(has_side_effects=_EFFECT),
    )(*[hbm(t) for t in (*srcs, *lands)])
    return res[:-1], res[-1]


def _exchange_wait(name, handle, after):
    n = len(handle) // 4
    sems, thru = handle[:2 * n], handle[2 * n:]

    def body(*refs):
        land_refs = refs[n:2 * n]
        send_sems, recv_sems = refs[2 * n:3 * n], refs[3 * n:4 * n]
        me = (lax.axis_index("x"), lax.axis_index("y"), lax.axis_index("c"))
        for a in range(n):
            seven = land_refs[a].at[pl.ds(0, len(_FLIPS))]
            all_copies = pltpu.make_async_remote_copy(
                src_ref=seven, dst_ref=seven, send_sem=send_sems[a], recv_sem=recv_sems[a], device_id=me,
                device_id_type=_MESH)
            all_copies.wait_send()
            all_copies.wait_recv()

    res = pl.pallas_call(
        body, name=name, out_shape=tuple(pltpu.HBM(t.shape, t.dtype) for t in thru),
        in_specs=[*[_HBM] * (2 * n), *[_SEM] * (2 * n), _ANY], out_specs=tuple([_HBM] * (2 * n)),
        input_output_aliases={i: i for i in range(2 * n)},
        compiler_params=pltpu.CompilerParams(has_side_effects=_EFFECT),
    )(*thru, *sems, after)
    return res[n:]


def _adamw_gathered(name, own, parts, me, w, m, v):
    r, c = w.shape
    tr = min(r, 256)

    def body(me_ref, own_ref, p_ref, w_ref, m_ref, v_ref, g_out, d_out, m_out, v_out):
        g = own_ref[...]
        for j in range(N_DEV):
            g = g + p_ref[j].astype(F32)
        delta, m_new, v_new = _adamw(w_ref[...], g, m_ref[...], v_ref[...])
        g_out[...] = g
        d_out[...] = delta
        m_out[...] = m_new
        v_out[...] = v_new

    spec = pl.BlockSpec((tr, c), lambda i, me_ref: (i, 0))
    return pl.pallas_call(
        body, name=name,
        grid_spec=pltpu.PrefetchScalarGridSpec(
            num_scalar_prefetch=1, grid=(r // tr,),
            in_specs=[pl.BlockSpec((None, tr, c), lambda i, me_ref: (me_ref[0], i, 0)),
                      pl.BlockSpec((N_DEV, tr, c), lambda i, me_ref: (0, i, 0)), spec, spec, spec],
            out_specs=[spec] * 4),
        out_shape=[jax.ShapeDtypeStruct((r, c), F32)] * 4,
        compiler_params=_params(("parallel",)),
    )(me, own, parts, w, m, v)


def _adamw(w, g, m, v):
    m = ADAM_B1 * m + (1.0 - ADAM_B1) * g
    v = ADAM_B2 * v + (1.0 - ADAM_B2) * jnp.square(g)
    m_hat = m / (1.0 - ADAM_B1 ** ADAM_STEP)
    v_hat = v / (1.0 - ADAM_B2 ** ADAM_STEP)
    delta = -ADAM_LR * (m_hat / (jnp.sqrt(v_hat) + ADAM_EPS) + ADAM_WD * w)
    return delta, m, v


def _adamw_small(name, parts, w, m, v):
    _, r, c = parts.shape
    tr = 8

    def body(p_ref, w_ref, m_ref, v_ref, g_out, d_out, m_out, v_out):
        g = p_ref[0]
        for j in range(1, N_DEV):
            g = g + p_ref[j]
        delta, m_new, v_new = _adamw(w_ref[...], g, m_ref[...], v_ref[...])
        g_out[...] = g
        d_out[...] = delta
        m_out[...] = m_new
        v_out[...] = v_new

    spec = pl.BlockSpec((tr, c), lambda i: (i, 0))
    return pl.pallas_call(
        body, name=name, grid=(r // tr,),
        in_specs=[pl.BlockSpec((N_DEV, tr, c), lambda i: (0, i, 0)), spec, spec, spec],
        out_specs=[spec] * 4, out_shape=[jax.ShapeDtypeStruct((r, c), F32)] * 4,
        compiler_params=_params(("parallel",)),
    )(parts, w, m, v)


_WEIGHTS = ["norm1_g", "w_in", "q_norm_g", "k_norm_g", "ssm_lambda_re", "ssm_lambda_im", "ssm_log_dt", "ssm_b_re",
            "ssm_b_im", "ssm_c_re", "ssm_c_im", "ssm_d", "w_glu", "b_glu", "attn_out_g", "ssm_out_g", "w_out",
            "norm2_g", "w_mlp_in", "w_mlp_out"]
_BIG = ["w_in", "w_glu", "w_out", "w_mlp_in", "w_mlp_out"]
_SMALL = [n for n in _WEIGHTS if n not in _BIG]
_PACK_COLS = 1024


def _pack(tree, last=None):
    flat = [tree[n].reshape(-1).astype(F32) for n in _SMALL]
    size = sum(f.shape[0] for f in flat)
    rows = -(-(size + 1) // (_PACK_COLS * 8)) * 8
    pad = jnp.zeros((rows * _PACK_COLS - size - 1,), F32)
    tail = jnp.zeros((1,), F32) if last is None else last.reshape(1).astype(F32)
    return jnp.concatenate(flat + [pad, tail]).reshape(rows, _PACK_COLS)


def _unpack(buf, like):
    flat, out, off = buf.reshape(-1), {}, 0
    for n in _SMALL:
        size = like[n].size
        out[n] = flat[off:off + size].reshape(like[n].shape)
        off += size
    return out


def kernel(x, norm1_g, w_in, q_norm_g, k_norm_g, ssm_lambda_re, ssm_lambda_im, ssm_log_dt, ssm_b_re, ssm_b_im, ssm_c_re, ssm_c_im, ssm_d, w_glu, b_glu, attn_out_g, ssm_out_g, w_out, norm2_g, w_mlp_in, w_mlp_out, loss_target, m_norm1_g, m_w_in, m_q_norm_g, m_k_norm_g, m_ssm_lambda_re, m_ssm_lambda_im, m_ssm_log_dt, m_ssm_b_re, m_ssm_b_im, m_ssm_c_re, m_ssm_c_im, m_ssm_d, m_w_glu, m_b_glu, m_attn_out_g, m_ssm_out_g, m_w_out, m_norm2_g, m_w_mlp_in, m_w_mlp_out, v_norm1_g, v_w_in, v_q_norm_g, v_k_norm_g, v_ssm_lambda_re, v_ssm_lambda_im, v_ssm_log_dt, v_ssm_b_re, v_ssm_b_im, v_ssm_c_re, v_ssm_c_im, v_ssm_d, v_w_glu, v_b_glu, v_attn_out_g, v_ssm_out_g, v_w_out, v_norm2_g, v_w_mlp_in, v_w_mlp_out):
    w = dict(zip(_WEIGHTS, (norm1_g, w_in, q_norm_g, k_norm_g, ssm_lambda_re, ssm_lambda_im, ssm_log_dt, ssm_b_re, ssm_b_im, ssm_c_re, ssm_c_im, ssm_d, w_glu, b_glu, attn_out_g, ssm_out_g, w_out, norm2_g, w_mlp_in, w_mlp_out)))
    m = dict(zip(_WEIGHTS, (m_norm1_g, m_w_in, m_q_norm_g, m_k_norm_g, m_ssm_lambda_re, m_ssm_lambda_im, m_ssm_log_dt, m_ssm_b_re, m_ssm_b_im, m_ssm_c_re, m_ssm_c_im, m_ssm_d, m_w_glu, m_b_glu, m_attn_out_g, m_ssm_out_g, m_w_out, m_norm2_g, m_w_mlp_in, m_w_mlp_out)))
    v = dict(zip(_WEIGHTS, (v_norm1_g, v_w_in, v_q_norm_g, v_k_norm_g, v_ssm_lambda_re, v_ssm_lambda_im, v_ssm_log_dt, v_ssm_b_re, v_ssm_b_im, v_ssm_c_re, v_ssm_c_im, v_ssm_d, v_w_glu, v_b_glu, v_attn_out_g, v_ssm_out_g, v_w_out, v_norm2_g, v_w_mlp_in, v_w_mlp_out)))
    core = lax.axis_index("c").astype(jnp.int32).reshape(1)
    chip = (2 * lax.axis_index("x") + lax.axis_index("y")).astype(jnp.int32).reshape(1)

    me = (2 * chip + core).astype(jnp.int32)

    def landing(own=None, like=None):
        own = jnp.zeros_like(like) if own is None else own
        return lax.dynamic_update_slice(lax.empty((N_DEV, *like.shape), like.dtype), own[None], (me[0], 0, 0))

    (w_in_blocks,) = _all_gather("w_in_all_gather", [w_in.astype(BF16)])
    w_in_full = w_in_blocks.transpose(1, 0, 2).reshape(w_in.shape[0], -1)
    late = [n for n in _BIG if n != "w_in"]
    shards = [w[n].astype(BF16) for n in late]
    w_in_blocks, shards = lax.optimization_barrier((w_in_blocks, shards))
    weights_handle, weights_token = _exchange_start(
        "weights_send", shards, [landing(s, s) for s in shards], per_peer=False)

    def late_weights(after):
        got = dict(zip(late, _exchange_wait("weights_arrive", weights_handle, after)))
        return (got["w_glu"].reshape(-1, w_glu.shape[1]), got["w_out"].reshape(-1, w_out.shape[1]),
                got["w_mlp_in"].transpose(1, 0, 2).reshape(w_mlp_in.shape[0], -1),
                got["w_mlp_out"].reshape(-1, w_mlp_out.shape[1]))

    mlp = ["w_mlp_out", "w_mlp_in"]
    sent = {}

    def send_grads(name, names, own, own_b):
        blocks = lambda g, n: g.reshape(N_DEV, *w[n].shape)
        sent[name + "_own"] = [blocks(g, n) for g, n in zip(own, names)]
        srcs = [blocks(g, n) for g, n in zip(own_b, names)]
        sent[name], token = _exchange_start(name, srcs, [landing(like=s[0]) for s in srcs], per_peer=True)
        return token[0, 0]

    def mlp_grads_ready(g_out, g_out_b, g_in, g_in_b):
        return send_grads("mlp_grads_send", mlp, [g_out, g_in], [g_out_b, g_in_b])

    rest = ["w_in", "w_glu", "w_out"]

    def rest_grads_ready(own, own_b):
        return send_grads("rest_grads_send", rest, own, own_b)

    loss_local, grad_x, g_small, g_big = _local_step(
        x, loss_target, {n: w[n] for n in _SMALL}, w_in_full, late_weights, mlp_grads_ready, rest_grads_ready,
        weights_token[0, 0])

    grads, delta, new_m, new_v = {}, {}, {}, {}
    small = _pack(g_small, last=loss_local)
    small_handle, small_token = _exchange_start("small_grads_send", [small], [landing(small, small)], per_peer=False)

    for send, arrive, names in (("mlp_grads_send", "mlp_grads_arrive", mlp),
                                ("rest_grads_send", "rest_grads_arrive", rest)):
        for n, own, part in zip(names, sent[send + "_own"], _exchange_wait(arrive, sent[send], small_token)):
            grads[n], delta[n], new_m[n], new_v[n] = _adamw_gathered("adamw_" + n, own, part, me, w[n], m[n], v[n])

    shards_done = lax.optimization_barrier(tuple(new_v[n] for n in _BIG))
    (small_parts,) = _exchange_wait("small_grads_arrive", small_handle, shards_done[-1])
    packed = _adamw_small("adamw_small", small_parts, _pack(w), _pack(m), _pack(v))
    for tree, buf in zip((grads, delta, new_m, new_v), packed):
        tree.update(_unpack(buf, w))
    loss = packed[0][-1, -1]

    return (loss, grad_x, *[grads[n] for n in _WEIGHTS], *[delta[n] for n in _WEIGHTS],
            *[new_m[n] for n in _WEIGHTS], *[new_v[n] for n in _WEIGHTS])
```

```python
import functools
import math

import jax
import jax.numpy as jnp
from jax import lax
from jax.experimental import pallas as pl
from jax.experimental.pallas import tpu as pltpu

F32 = jnp.float32
BF16 = jnp.bfloat16

EPS = 1e-6
HEAD_DIM = 64
N_HEADS = 8
SB_WIDTH = 512
SSM_WIDTH = 512
SSM_GROUP = 16
SSM_GROUPS = 32
SSM_STATE = 64
QBLOCK = 128
KBLOCK = 256
N_CHUNK = 8
SSM_COLS = 4
LANES = 128
N_DEV = 8

ADAM_LR = 0.001
ADAM_B1 = 0.9
ADAM_B2 = 0.999
ADAM_EPS = 1e-08
ADAM_WD = 0.01
ADAM_STEP = 10

VMEM_LIMIT = 56 * 1024 * 1024

_NT = (((1,), (1,)), ((), ()))
_NN = (((1,), (0,)), ((), ()))
_TN = (((0,), (0,)), ((), ()))


def _dot(a, b, dims=_NN):
    return lax.dot_general(a, b, dims, preferred_element_type=F32)


def _params(sem):
    return pltpu.CompilerParams(dimension_semantics=sem, vmem_limit_bytes=VMEM_LIMIT)


def _matmul(name, a, b, *, ta=False, tb=False, extras=(), epilogue=None, out_dtypes=(F32,), sums=(),
            col_blocked=False, tm=1024, tn=512, tk=4096):
    M, K = (a.shape[1], a.shape[0]) if ta else a.shape
    N = b.shape[0] if tb else b.shape[1]
    tm, tn, tk = min(tm, M), min(tn, N), min(tk, K)
    assert M % tm == 0 and N % tn == 0 and K % tk == 0, (name, M, N, K)
    assert not sums or (tn == N and tk == K), name
    nk = K // tk
    n_ex, n_out, n_sum = len(extras), len(out_dtypes), len(sums)
    dims = (((0 if ta else 1,), (1 if tb else 0,)), ((), ()))

    def body(*refs):
        a_ref, b_ref = refs[0], refs[1]
        ex_refs = refs[2:2 + n_ex]
        o_refs = refs[2 + n_ex:2 + n_ex + n_out]
        s_refs = refs[2 + n_ex + n_out:2 + n_ex + n_out + n_sum]
        k = pl.program_id(2)
        part = _dot(a_ref[...].astype(BF16), b_ref[...].astype(BF16), dims)

        def finish(acc):
            outs = (acc,) if epilogue is None else epilogue(acc, *[e[...] for e in ex_refs])
            for o_ref, o in zip(o_refs, outs[:n_out]):
                o_ref[...] = o.astype(o_ref.dtype)
            if n_sum:
                @pl.when(pl.program_id(0) == 0)
                def _():
                    for s_ref in s_refs:
                        s_ref[...] = jnp.zeros_like(s_ref)

                for s_ref, v in zip(s_refs, outs[n_out:]):
                    s_ref[...] += v

        if nk == 1:
            finish(part)
        else:
            acc_ref = refs[-1]

            @pl.when(k == 0)
            def _():
                acc_ref[...] = part

            @pl.when(jnp.logical_and(k > 0, k < nk - 1))
            def _():
                acc_ref[...] += part

            @pl.when(k == nk - 1)
            def _():
                finish(acc_ref[...] + part)

    a_spec = pl.BlockSpec((tk, tm), lambda i, j, k: (k, i)) if ta else pl.BlockSpec((tm, tk), lambda i, j, k: (i, k))
    b_spec = pl.BlockSpec((tn, tk), lambda i, j, k: (j, k)) if tb else pl.BlockSpec((tk, tn), lambda i, j, k: (k, j))
    ex_specs = [pl.BlockSpec((1, tn), lambda i, j, k: (0, j)) if e.shape[0] == 1 else
                pl.BlockSpec((tm, tn), lambda i, j, k: (i, j)) for e in extras]
    if col_blocked:
        out_specs = [pl.BlockSpec((None, tm, tn), lambda i, j, k: (j, i, 0)) for _ in out_dtypes]
        out_shape = [jax.ShapeDtypeStruct((N // tn, M, tn), dt) for dt in out_dtypes]
    else:
        out_specs = [pl.BlockSpec((tm, tn), lambda i, j, k: (i, j)) for _ in out_dtypes]
        out_shape = [jax.ShapeDtypeStruct((M, N), dt) for dt in out_dtypes]
    out_specs += [pl.BlockSpec(s, lambda i, j, k: (0, 0)) for s in sums]
    out_shape += [jax.ShapeDtypeStruct(s, F32) for s in sums]
    outs = pl.pallas_call(
        body, name=name, grid=(M // tm, N // tn, nk),
        in_specs=[a_spec, b_spec, *ex_specs], out_specs=out_specs, out_shape=out_shape,
        scratch_shapes=[pltpu.VMEM((tm, tn), F32)] if nk > 1 else [],
        compiler_params=_params(("arbitrary",) * 3 if sums else ("parallel", "parallel", "arbitrary")),
    )(a, b, *extras)
    return outs[0] if len(outs) == 1 else outs


def _rowwise(name, fn, rows, small, outs, sums=(), tile=256):
    specs, args = [], []
    T = None
    for r in rows:
        arr, cb, w = r if isinstance(r, tuple) else (r, 0, r.shape[1])
        T = arr.shape[0]
        specs.append((w, cb))
        args.append(arr)
    tile = min(tile, T)
    assert T % tile == 0
    n_r, n_s, n_o, n_a = len(rows), len(small), len(outs), len(sums)

    def body(*refs):
        r_refs = refs[:n_r]
        s_refs = refs[n_r:n_r + n_s]
        o_refs = refs[n_r + n_s:n_r + n_s + n_o]
        a_refs = refs[n_r + n_s + n_o:]
        res = fn(*[r[...] for r in r_refs], *[s[...] for s in s_refs])
        res = res if isinstance(res, (tuple, list)) else (res,)
        for o_ref, o in zip(o_refs, res[:n_o]):
            o_ref[...] = o.astype(o_ref.dtype)

        @pl.when(pl.program_id(0) == 0)
        def _():
            for a_ref in a_refs:
                a_ref[...] = jnp.zeros_like(a_ref)

        for a_ref, v in zip(a_refs, res[n_o:]):
            a_ref[...] += v.astype(F32)

    in_specs = [pl.BlockSpec((tile, w), functools.partial(lambda i, cb: (i, cb), cb=cb)) for w, cb in specs]
    in_specs += [pl.BlockSpec(s.shape, functools.partial(lambda i, nd: (0,) * nd, nd=s.ndim)) for s in small]
    out_specs = [pl.BlockSpec((tile, w), lambda i: (i, 0)) for w, _ in outs]
    out_specs += [pl.BlockSpec(s, functools.partial(lambda i, nd: (0,) * nd, nd=len(s))) for s in sums]
    out_shape = [jax.ShapeDtypeStruct((T, w), dt) for w, dt in outs]
    out_shape += [jax.ShapeDtypeStruct(s, F32) for s in sums]
    res = pl.pallas_call(
        body, name=name, grid=(T // tile,), in_specs=in_specs, out_specs=out_specs, out_shape=out_shape,
        compiler_params=_params(("arbitrary",)),
    )(*args, *small)
    return res[0] if len(res) == 1 else res


def _rms(x, g):
    return x * lax.rsqrt(jnp.mean(x * x, axis=-1, keepdims=True) + EPS) * g


def _glu_branch(y, pre, b_glu, g_out):
    g = jax.nn.gelu(y)
    return _rms(g * jax.nn.sigmoid(pre + b_glu), g_out)


def _split_dot(x, tri_bf):
    hi = x.astype(BF16)
    lo = (x - hi.astype(F32)).astype(BF16)
    return _dot(hi, tri_bf) + _dot(lo, tri_bf)


def _softplus(z):
    return jnp.maximum(z, 0.0) + jnp.log(1.0 + jnp.exp(-jnp.abs(z)))


def _head(h):
    return slice(h * HEAD_DIM, (h + 1) * HEAD_DIM)


def _head_mean(x, seg):
    return _split_dot(x, seg) * (1.0 / HEAD_DIM)


def _qk_norm(proj, gq, gk):
    scale = 1.0 / math.sqrt(HEAD_DIM)
    idx = jnp.arange(SB_WIDTH) // HEAD_DIM
    seg = (idx[:, None] == idx[None, :]).astype(BF16)

    def fn(q, k, v, gq_, gk_, seg_):
        qn = q * lax.rsqrt(_head_mean(q * q, seg_) + EPS) * (gq_ * scale)
        kn = k * lax.rsqrt(_head_mean(k * k, seg_) + EPS) * gk_
        return qn, kn, v

    return _rowwise("qk_norm", fn, [(proj, 0, SB_WIDTH), (proj, 1, SB_WIDTH), (proj, 2, SB_WIDTH)], [gq, gk, seg],
                    [(SB_WIDTH, BF16)] * 3)


def _qk_norm_bwd(proj, gq, gk, d_qn, d_kn):
    scale = 1.0 / math.sqrt(HEAD_DIM)
    idx = jnp.arange(SB_WIDTH) // HEAD_DIM
    seg = (idx[:, None] == idx[None, :]).astype(BF16)

    def one(x, g, dy, seg_):
        r = lax.rsqrt(_head_mean(x * x, seg_) + EPS)
        gdy = g * dy
        dx = r * gdy - x * (r * r * r) * _head_mean(gdy * x, seg_)
        dg = jnp.sum(dy * x * r, axis=0, keepdims=True)
        return dx, sum(dg[:, _head(h)] for h in range(N_HEADS))

    def fn(q, k, dqn, dkn, gq_, gk_, seg_):
        dq, dgq = one(q, gq_, dqn * scale, seg_)
        dk, dgk = one(k, gk_, dkn, seg_)
        return dq, dk, dgq, dgk

    return _rowwise("qk_norm_bwd", fn, [(proj, 0, SB_WIDTH), (proj, 1, SB_WIDTH), d_qn, d_kn], [gq, gk, seg],
                    [(SB_WIDTH, BF16)] * 2, sums=[(1, HEAD_DIM)] * 2)


def _split_heads(refs, scratch, L):
    def chunk(i, _):
        r = pl.ds(pl.multiple_of(i * QBLOCK, QBLOCK), QBLOCK)
        for ref, s in zip(refs, scratch):
            for h in range(2):
                s[h, r, :] = ref[r, _head(h)]
        return 0

    lax.fori_loop(0, L // QBLOCK, chunk, 0)


Q_HALVES = KBLOCK // QBLOCK
_CHAINS = [(h, r) for h in range(2) for r in range(Q_HALVES)]


def _valid(i, kb):
    row = lax.broadcasted_iota(jnp.int32, (QBLOCK, KBLOCK), 0)
    col = lax.broadcasted_iota(jnp.int32, (QBLOCK, KBLOCK), 1)
    return col + (kb * KBLOCK - i * QBLOCK) < row


def _attn_fwd(qn, kn, vb, B, L):
    n_pairs = L // KBLOCK
    n_hp = N_HEADS // 2

    def body(q_ref, k_ref, v_ref, o_ref, rt_ref, q_s, k_s, v_s, after_s, z_s):
        _split_heads((q_ref, k_ref, v_ref), (q_s, k_s, v_s), L)
        r2 = lax.broadcasted_iota(jnp.int32, (KBLOCK, KBLOCK), 0)
        c2 = lax.broadcasted_iota(jnp.int32, (KBLOCK, KBLOCK), 1)
        after_s[...] = (r2 > c2).astype(after_s.dtype)

        def q_pair(p, _):
            rows = [pl.ds(pl.multiple_of((p * Q_HALVES + r) * QBLOCK, QBLOCK), QBLOCK) for r in range(Q_HALVES)]
            q_c = [q_s[h, rows[r], :] for h, r in _CHAINS]
            cs = range(len(_CHAINS))

            def scores(kb):
                rk = pl.ds(pl.multiple_of(kb * KBLOCK, KBLOCK), KBLOCK)
                return [_dot(q_c[c], k_s[_CHAINS[c][0], rk, :], _NT) for c in cs]

            def k_block(kb, carry, diagonal):
                rk = pl.ds(pl.multiple_of(kb * KBLOCK, KBLOCK), KBLOCK)
                if diagonal:
                    valid = [_valid(p * Q_HALVES + r, kb) for r in range(Q_HALVES)]
                    keep = lambda c, t: jnp.where(valid[_CHAINS[c][1]], t, 0.0)
                    z = scores(kb)
                else:
                    keep = lambda c, t: t
                    z = [z_s[(kb + 1) & 1, c] for c in cs]
                ahead = scores(jnp.maximum(kb - 1, 0))
                for c in cs:
                    z_s[kb & 1, c] = ahead[c]
                sp = [_softplus(z[c]) for c in cs]
                lom = [keep(c, -sp[c]) for c in cs]
                tail = [_split_dot(lom[c], after_s[...]) + carry[c][0] for c in cs]
                a = [keep(c, jnp.exp(z[c] - sp[c] + tail[c])) for c in cs]
                acc = [carry[c][1] + _dot(a[c].astype(v_s.dtype), v_s[_CHAINS[c][0], rk, :]) for c in cs]
                return tuple((carry[c][0] + jnp.sum(lom[c], axis=1, keepdims=True), acc[c]) for c in cs)

            init = (jnp.zeros((QBLOCK, 1), F32), jnp.zeros((QBLOCK, HEAD_DIM), F32))
            first = k_block(p, (init,) * len(_CHAINS), True)
            res = lax.fori_loop(1, p + 1, lambda n, carry: k_block(p - n, carry, False), first)
            for r in range(Q_HALVES):
                mine = [res[c] for c in cs if _CHAINS[c][1] == r]
                o_ref[rows[r], :] = jnp.concatenate([m[1] for m in mine], axis=1)
                rt_ref[rows[r], :] = jnp.concatenate(
                    [jnp.broadcast_to(m[0], (QBLOCK, HEAD_DIM)) for m in mine], axis=1)
            return 0

        lax.fori_loop(0, n_pairs, q_pair, 0)

    spec = pl.BlockSpec((L, LANES), lambda b, p: (b, p))
    return pl.pallas_call(
        body, name="attn_fwd", grid=(B, n_hp),
        in_specs=[spec] * 3, out_specs=[spec, spec],
        out_shape=[jax.ShapeDtypeStruct((B * L, SB_WIDTH), F32)] * 2,
        scratch_shapes=[pltpu.VMEM((2, L, HEAD_DIM), BF16)] * 3 + [pltpu.VMEM((KBLOCK, KBLOCK), BF16)]
        + [pltpu.VMEM((2, len(_CHAINS), QBLOCK, KBLOCK), F32)],
        compiler_params=_params(("parallel", "parallel")),
    )(qn, kn, vb)


def _attn_bwd(qn, kn, vb, rtot, d_sb, B, L):
    n_pairs = L // KBLOCK
    n_hp = N_HEADS // 2

    def body(q_ref, k_ref, v_ref, rt_ref, do_ref, dq_ref, dk_ref, dv_ref,
             q_s, k_s, v_s, qt_s, dkt_s, dvt_s, after_s, before_s):
        _split_heads((q_ref, k_ref, v_ref), (q_s, k_s, v_s), L)

        def transpose_q(i, _):
            r = pl.ds(pl.multiple_of(i * QBLOCK, QBLOCK), QBLOCK)
            qt_s[:, r] = q_ref[r, :].astype(F32).T.astype(qt_s.dtype)
            return 0

        lax.fori_loop(0, L // QBLOCK, transpose_q, 0)
        dkt_s[...] = jnp.zeros_like(dkt_s)
        dvt_s[...] = jnp.zeros_like(dvt_s)
        r2 = lax.broadcasted_iota(jnp.int32, (KBLOCK, KBLOCK), 0)
        c2 = lax.broadcasted_iota(jnp.int32, (KBLOCK, KBLOCK), 1)
        after_s[...] = (r2 > c2).astype(after_s.dtype)
        before_s[...] = (r2 < c2).astype(before_s.dtype)

        def q_pair(p, _):
            rows = [pl.ds(pl.multiple_of((p * Q_HALVES + r) * QBLOCK, QBLOCK), QBLOCK) for r in range(Q_HALVES)]
            pair = pl.ds(pl.multiple_of(p * KBLOCK, KBLOCK), KBLOCK)
            do2 = do_ref[pair, :]
            do_t = do2.T.astype(v_s.dtype)
            tot2 = rt_ref[pair, :]
            cs = range(len(_CHAINS))
            hs = range(2)
            q_c = [q_s[h, rows[r], :] for h, r in _CHAINS]
            do_c = [do2[r * QBLOCK:(r + 1) * QBLOCK, _head(h)].astype(v_s.dtype) for h, r in _CHAINS]
            total = [tot2[r * QBLOCK:(r + 1) * QBLOCK, h * HEAD_DIM:h * HEAD_DIM + 1] for h, r in _CHAINS]
            qt_h = [qt_s[_head(h), pair] for h in hs]
            dot_h = [do_t[_head(h), :] for h in hs]

            def k_block(kb, carry, diagonal):
                rk = pl.ds(pl.multiple_of(kb * KBLOCK, KBLOCK), KBLOCK)
                if diagonal:
                    valid = [_valid(p * Q_HALVES + r, kb) for r in range(Q_HALVES)]
                    keep = lambda c, t: jnp.where(valid[_CHAINS[c][1]], t, 0.0)
                else:
                    keep = lambda c, t: t
                k_b = [k_s[h, rk, :] for h in hs]
                z = [_dot(q_c[c], k_b[_CHAINS[c][0]], _NT) for c in cs]
                da = [_dot(do_c[c], v_s[_CHAINS[c][0], rk, :], _NT) for c in cs]
                sp = [_softplus(z[c]) for c in cs]
                lsig = [z[c] - sp[c] for c in cs]
                lom = [keep(c, -sp[c]) for c in cs]
                lom_sum = [jnp.sum(lom[c], axis=1, keepdims=True) for c in cs]
                tail = [_split_dot(lom[c], after_s[...]) + (total[c] - carry[c][0] - lom_sum[c]) for c in cs]
                a = [keep(c, jnp.exp(lsig[c] + tail[c])) for c in cs]
                dla = [a[c] * da[c] for c in cs]
                for h in hs:
                    a_h = jnp.concatenate([a[c].astype(v_s.dtype) for c in cs if _CHAINS[c][0] == h], axis=0)
                    dvt_s[_head(h), rk] += _dot(dot_h[h], a_h)
                d_lom = [carry[c][1] + _split_dot(dla[c], before_s[...]) for c in cs]
                beta = [jnp.exp(lsig[c]) for c in cs]
                dz_b = [(dla[c] * (1.0 - beta[c]) - keep(c, beta[c] * d_lom[c])).astype(v_s.dtype) for c in cs]
                dq_acc = [carry[c][2] + _dot(dz_b[c], k_b[_CHAINS[c][0]]) for c in cs]
                for h in hs:
                    dz_h = jnp.concatenate([dz_b[c] for c in cs if _CHAINS[c][0] == h], axis=0)
                    dkt_s[_head(h), rk] += _dot(qt_h[h], dz_h)
                return tuple((carry[c][0] + lom_sum[c], carry[c][1] + jnp.sum(dla[c], axis=1, keepdims=True),
                              dq_acc[c]) for c in cs)

            zero = jnp.zeros((QBLOCK, 1), F32)
            init = (zero, zero, jnp.zeros((QBLOCK, HEAD_DIM), F32))
            before = lax.fori_loop(0, p, lambda kb, carry: k_block(kb, carry, False), (init,) * len(_CHAINS))
            res = k_block(p, before, True)
            for r in range(Q_HALVES):
                dq_ref[rows[r], :] = jnp.concatenate([res[c][2] for c in cs if _CHAINS[c][1] == r], axis=1)
            return 0

        lax.fori_loop(0, n_pairs, q_pair, 0)

        def transpose_out(i, _):
            r = pl.ds(pl.multiple_of(i * QBLOCK, QBLOCK), QBLOCK)
            dk_ref[r, :] = dkt_s[:, r].T
            dv_ref[r, :] = dvt_s[:, r].T.astype(dv_ref.dtype)
            return 0

        lax.fori_loop(0, L // QBLOCK, transpose_out, 0)

    spec = pl.BlockSpec((L, LANES), lambda b, p: (b, p))
    return pl.pallas_call(
        body, name="attn_bwd", grid=(B, n_hp),
        in_specs=[spec] * 5, out_specs=[spec] * 3,
        out_shape=[jax.ShapeDtypeStruct((B * L, SB_WIDTH), F32)] * 2 + [jax.ShapeDtypeStruct((B * L, SB_WIDTH), BF16)],
        scratch_shapes=[pltpu.VMEM((2, L, HEAD_DIM), BF16)] * 3 + [pltpu.VMEM((LANES, L), BF16)]
        + [pltpu.VMEM((LANES, L), F32)] * 2 + [pltpu.VMEM((KBLOCK, KBLOCK), BF16)] * 2,
        compiler_params=_params(("parallel", "parallel")),
    )(qn, kn, vb, rtot, d_sb)


def _ssm_discretise(lam_re, lam_im, log_dt, b_re, b_im):
    dt = jnp.exp(log_dt)
    mag = jnp.exp(lam_re * dt)
    lbr = mag * jnp.cos(lam_im * dt)
    lbi = mag * jnp.sin(lam_im * dt)
    den = lam_re * lam_re + lam_im * lam_im
    nr, ni = lbr - 1.0, lbi
    cr = (nr * lam_re + ni * lam_im) / den
    ci = (ni * lam_re - nr * lam_im) / den
    return lbr, lbi, cr * b_re - ci * b_im, cr * b_im + ci * b_re


def _ssm_prep(lam_re, lam_im, log_dt, b_re_t, b_im_t):
    def body(lr, li, ld, br, bi, o_lr, o_li, o_br, o_bi):
        res = _ssm_discretise(lr[...], li[...], ld[...], br[...], bi[...])
        for o, v in zip((o_lr, o_li, o_br, o_bi), res):
            o[...] = v

    return pl.pallas_call(
        body, name="ssm_prep",
        out_shape=[jax.ShapeDtypeStruct(lam_re.shape, F32)] * 2 + [jax.ShapeDtypeStruct(b_re_t.shape, F32)] * 2,
    )(lam_re, lam_im, log_dt, b_re_t, b_im_t)


def _ssm_prep_bwd(lam_re, lam_im, log_dt, b_re_t, b_im_t, d_lr, d_li, d_br, d_bi):
    def body(lr, li, ld, br, bi, g_lr, g_li, g_br, g_bi, o_lr, o_li, o_ld, o_br, o_bi):
        _, vjp = jax.vjp(_ssm_discretise, lr[...], li[...], ld[...], br[...], bi[...])
        res = vjp((g_lr[...], g_li[...], g_br[...], g_bi[...]))
        for o, v in zip((o_lr, o_li, o_ld, o_br, o_bi), res):
            o[...] = v

    return pl.pallas_call(
        body, name="ssm_prep_bwd",
        out_shape=[jax.ShapeDtypeStruct(lam_re.shape, F32)] * 2 + [jax.ShapeDtypeStruct(log_dt.shape, F32)]
        + [jax.ShapeDtypeStruct(b_re_t.shape, F32)] * 2,
    )(lam_re, lam_im, log_dt, b_re_t, b_im_t, d_lr, d_li, d_br, d_bi)


def _block_diag(m):
    m4 = m.reshape(SSM_COLS, 8, SSM_GROUP, SSM_STATE)
    return jnp.einsum("aghp,gk->aghkp", m4, jnp.eye(8, dtype=m.dtype)).reshape(SSM_COLS, LANES, 512)


def _block_diag_take(d):
    d6 = d.reshape(SSM_COLS, 8, SSM_GROUP, 2, 8, SSM_STATE)
    return jnp.einsum("aghrgp->raghp", d6).reshape(2, SSM_GROUPS, SSM_GROUP, SSM_STATE)


def _cmul(ar, ai, br, bi):
    return ar * br - ai * bi, ar * bi + ai * br


def _power(lr, li, n):
    assert n & (n - 1) == 0
    for _ in range(n.bit_length() - 1):
        lr, li = _cmul(lr, li, lr, li)
    return lr, li


def _ssm_fwd(u_p, w_b, lam_r, lam_i, c_m, d_skip, B, L, tj):
    J = L // N_CHUNK
    njt = J // tj
    R = tj * N_CHUNK
    H = 512

    def body(u_ref, wb_ref, lr_ref, li_ref, cm_ref, d_ref, y_ref, gel_ref, x_ref, xin_ref, bu_s, st_s, xin_s):
        ph, jt = pl.program_id(2), pl.program_id(3)
        lr, li = lr_ref[...], li_ref[...]

        @pl.when(jnp.logical_and(ph == 0, jt == 0))
        def _():
            st_s[...] = jnp.zeros_like(st_s)

        @pl.when(ph == 0)
        def _():
            bu_s[jt] = _dot(u_ref[...].astype(BF16), wb_ref[...].astype(BF16))

        def scan(store):
            def step(j, carry):
                xr, xi = carry
                r = pl.ds(pl.multiple_of(j * N_CHUNK, N_CHUNK), N_CHUNK)
                nr = lr * xr - li * xi + bu_s[jt, r, 0:H]
                ni = lr * xi + li * xr + bu_s[jt, r, H:2 * H]
                if store:
                    x_ref[r, 0:H] = nr
                    x_ref[r, H:2 * H] = ni
                return nr, ni

            xr, xi = lax.fori_loop(0, tj, step, (st_s[:, 0:H], st_s[:, H:2 * H]))
            st_s[:, 0:H] = xr
            st_s[:, H:2 * H] = xi

        @pl.when(ph == 0)
        def _():
            scan(False)

            @pl.when(jt == njt - 1)
            def _():
                pr, pi = _power(lr[0:1], li[0:1], J)
                xin_s[0:1, :] = jnp.zeros((1, 2 * H), F32)
                for c in range(1, N_CHUNK):
                    qr, qi = _cmul(pr, pi, xin_s[c - 1:c, 0:H], xin_s[c - 1:c, H:2 * H])
                    xin_s[c:c + 1, 0:H] = qr + st_s[c - 1:c, 0:H]
                    xin_s[c:c + 1, H:2 * H] = qi + st_s[c - 1:c, H:2 * H]
                xin_ref[...] = xin_s[...]
                st_s[...] = xin_s[...]

        @pl.when(ph == 1)
        def _():
            scan(True)
            y = _dot(x_ref[...].astype(BF16), cm_ref[...].astype(BF16)) + d_ref[...] * u_ref[...]
            y_ref[...] = y
            gel_ref[...] = jax.nn.gelu(y).astype(gel_ref.dtype)

    return pl.pallas_call(
        body, name="ssm_fwd", grid=(SSM_COLS, B, 2, njt),
        in_specs=[
            pl.BlockSpec((None, R, LANES), lambda i, b, ph, jt: (b, jt, i)),
            pl.BlockSpec((None, LANES, 2 * H), lambda i, b, ph, jt: (i, 0, 0)),
            pl.BlockSpec((None, N_CHUNK, H), lambda i, b, ph, jt: (i, 0, 0)),
            pl.BlockSpec((None, N_CHUNK, H), lambda i, b, ph, jt: (i, 0, 0)),
            pl.BlockSpec((None, 2 * H, LANES), lambda i, b, ph, jt: (i, 0, 0)),
            pl.BlockSpec((1, LANES), lambda i, b, ph, jt: (0, i)),
        ],
        out_specs=[
            pl.BlockSpec((None, R, LANES), lambda i, b, ph, jt: (b, jt * ph, i)),
            pl.BlockSpec((None, R, LANES), lambda i, b, ph, jt: (b, jt * ph, i)),
            pl.BlockSpec((None, R, 2 * H), lambda i, b, ph, jt: (b, jt * ph, i)),
            pl.BlockSpec((None, None, N_CHUNK, 2 * H), lambda i, b, ph, jt: (b, i, 0, 0)),
        ],
        out_shape=[
            jax.ShapeDtypeStruct((B, L, SSM_WIDTH), F32),
            jax.ShapeDtypeStruct((B, L, SSM_WIDTH), BF16),
            jax.ShapeDtypeStruct((B, L, SSM_COLS * 2 * H), F32),
            jax.ShapeDtypeStruct((B, SSM_COLS, N_CHUNK, 2 * H), F32),
        ],
        scratch_shapes=[pltpu.VMEM((njt, R, 2 * H), F32), pltpu.VMEM((N_CHUNK, 2 * H), F32),
                        pltpu.VMEM((N_CHUNK, 2 * H), F32)],
        compiler_params=_params(("arbitrary",) * 4),
    )(u_p, w_b, lam_r, lam_i, c_m, d_skip)


def _ssm_bwd(dy_p, u_p, x, xin, w_bt, lam_r, lam_i, c_mt, d_skip, B, L, tj):
    J = L // N_CHUNK
    njt = J // tj
    R = tj * N_CHUNK
    H = 512
    x4 = x.reshape(B, J, N_CHUNK, SSM_COLS * 2 * H)

    def body(dy_ref, u_ref, x_ref, xp_ref, xin_ref, wbt_ref, lr_ref, li_ref, cmt_ref, d_ref,
             du_ref, dwb_ref, dcm_ref, dlr_ref, dli_ref, dd_ref, ca_s, a_s, st_s, dl_s):
        b, ph, jt = pl.program_id(1), pl.program_id(2), pl.program_id(3)
        jr = njt - 1 - jt
        lr, li = lr_ref[...], -li_ref[...]

        @pl.when(jnp.logical_and(b == 0, jnp.logical_and(ph == 0, jt == 0)))
        def _():
            dwb_ref[...] = jnp.zeros_like(dwb_ref)
            dcm_ref[...] = jnp.zeros_like(dcm_ref)
            dlr_ref[...] = jnp.zeros_like(dlr_ref)
            dli_ref[...] = jnp.zeros_like(dli_ref)
            dd_ref[...] = jnp.zeros_like(dd_ref)
            dl_s[...] = jnp.zeros_like(dl_s)

        @pl.when(jnp.logical_and(ph == 0, jt == 0))
        def _():
            st_s[...] = jnp.zeros_like(st_s)

        @pl.when(ph == 0)
        def _():
            ca_s[jt] = _dot(dy_ref[...].astype(BF16), cmt_ref[...].astype(BF16))

        def scan(store):
            def step(n, carry):
                ar, ai = carry
                r = pl.ds(pl.multiple_of((tj - 1 - n) * N_CHUNK, N_CHUNK), N_CHUNK)
                nr = lr * ar - li * ai + ca_s[jt, r, 0:H]
                ni = lr * ai + li * ar + ca_s[jt, r, H:2 * H]
                if store:
                    a_s[r, 0:H] = nr
                    a_s[r, H:2 * H] = ni
                return nr, ni

            ar, ai = lax.fori_loop(0, tj, step, (st_s[:, 0:H], st_s[:, H:2 * H]))
            st_s[:, 0:H] = ar
            st_s[:, H:2 * H] = ai

        @pl.when(ph == 0)
        def _():
            scan(False)

            @pl.when(jt == njt - 1)
            def _():
                pr, pi = _power(lr[0:1], li[0:1], J)
                a_s[N_CHUNK - 1:N_CHUNK, :] = jnp.zeros((1, 2 * H), F32)
                for c in range(N_CHUNK - 2, -1, -1):
                    qr, qi = _cmul(pr, pi, a_s[c + 1:c + 2, 0:H], a_s[c + 1:c + 2, H:2 * H])
                    a_s[c:c + 1, 0:H] = qr + st_s[c + 1:c + 2, 0:H]
                    a_s[c:c + 1, H:2 * H] = qi + st_s[c + 1:c + 2, H:2 * H]
                st_s[...] = a_s[0:N_CHUNK, :]

        @pl.when(ph == 1)
        def _():
            scan(True)
            dy = dy_ref[...]
            u = u_ref[...]
            a_b = a_s[...].astype(BF16)
            du_ref[...] = (_dot(a_b, wbt_ref[...].astype(BF16)) + d_ref[...] * dy).astype(du_ref.dtype)
            dwb_ref[...] += _dot(u.astype(BF16), a_b, _TN)
            dcm_ref[...] += _dot(x_ref[...].astype(BF16), dy.astype(BF16), _TN)
            dd_ref[...] += jnp.sum(dy * u, axis=0, keepdims=True)

            first = jnp.where(jr == 0, xin_ref[...], xp_ref[...])
            a0r, a0i = a_s[0:N_CHUNK, 0:H], a_s[0:N_CHUNK, H:2 * H]
            acc0 = (a0r * first[:, 0:H] + a0i * first[:, H:2 * H], a0i * first[:, 0:H] - a0r * first[:, H:2 * H])

            def step(j, carry):
                sr, si = carry
                r = pl.ds(pl.multiple_of(j * N_CHUNK, N_CHUNK), N_CHUNK)
                rp = pl.ds(pl.multiple_of((j - 1) * N_CHUNK, N_CHUNK), N_CHUNK)
                ar, ai = a_s[r, 0:H], a_s[r, H:2 * H]
                xr, xi = x_ref[rp, 0:H], x_ref[rp, H:2 * H]
                return sr + ar * xr + ai * xi, si + ai * xr - ar * xi

            sr, si = lax.fori_loop(1, tj, step, acc0)
            dl_s[:, 0:H] += sr
            dl_s[:, H:2 * H] += si

            @pl.when(jnp.logical_and(b == B - 1, jt == njt - 1))
            def _():
                dlr_ref[...] = jnp.sum(dl_s[:, 0:H], axis=0, keepdims=True)
                dli_ref[...] = jnp.sum(dl_s[:, H:2 * H], axis=0, keepdims=True)
                dl_s[...] = jnp.zeros_like(dl_s)

    rev = lambda ph, jt: (njt - 1 - jt) * ph + (njt - 1) * (1 - ph)
    return pl.pallas_call(
        body, name="ssm_bwd", grid=(SSM_COLS, B, 2, njt),
        in_specs=[
            pl.BlockSpec((None, R, LANES), lambda i, b, ph, jt: (b, njt - 1 - jt, i)),
            pl.BlockSpec((None, R, LANES), lambda i, b, ph, jt: (b, njt - 1 - jt, i)),
            pl.BlockSpec((None, R, 2 * H), lambda i, b, ph, jt: (b, rev(ph, jt), i)),
            pl.BlockSpec((None, None, N_CHUNK, 2 * H),
                         lambda i, b, ph, jt: (b, jnp.maximum((njt - 1 - jt) * tj - 1, 0), 0, i)),
            pl.BlockSpec((None, None, N_CHUNK, 2 * H), lambda i, b, ph, jt: (b, i, 0, 0)),
            pl.BlockSpec((None, 2 * H, LANES), lambda i, b, ph, jt: (i, 0, 0)),
            pl.BlockSpec((None, N_CHUNK, H), lambda i, b, ph, jt: (i, 0, 0)),
            pl.BlockSpec((None, N_CHUNK, H), lambda i, b, ph, jt: (i, 0, 0)),
            pl.BlockSpec((None, LANES, 2 * H), lambda i, b, ph, jt: (i, 0, 0)),
            pl.BlockSpec((1, LANES), lambda i, b, ph, jt: (0, i)),
        ],
        out_specs=[
            pl.BlockSpec((None, R, LANES), lambda i, b, ph, jt: (b, rev(ph, jt), i)),
            pl.BlockSpec((None, LANES, 2 * H), lambda i, b, ph, jt: (i, 0, 0)),
            pl.BlockSpec((None, 2 * H, LANES), lambda i, b, ph, jt: (i, 0, 0)),
            pl.BlockSpec((None, 1, H), lambda i, b, ph, jt: (i, 0, 0)),
            pl.BlockSpec((None, 1, H), lambda i, b, ph, jt: (i, 0, 0)),
            pl.BlockSpec((1, LANES), lambda i, b, ph, jt: (0, i)),
        ],
        out_shape=[
            jax.ShapeDtypeStruct((B, L, SSM_WIDTH), BF16),
            jax.ShapeDtypeStruct((SSM_COLS, LANES, 2 * H), F32),
            jax.ShapeDtypeStruct((SSM_COLS, 2 * H, LANES), F32),
            jax.ShapeDtypeStruct((SSM_COLS, 1, H), F32),
            jax.ShapeDtypeStruct((SSM_COLS, 1, H), F32),
            jax.ShapeDtypeStruct((1, SSM_WIDTH), F32),
        ],
        scratch_shapes=[pltpu.VMEM((njt, R, 2 * H), F32), pltpu.VMEM((R, 2 * H), F32),
                        pltpu.VMEM((N_CHUNK, 2 * H), F32), pltpu.VMEM((N_CHUNK, 2 * H), F32)],
        compiler_params=_params(("arbitrary",) * 4),
    )(dy_p, u_p, x, x4, xin, w_bt, lam_r, lam_i, c_mt, d_skip)


def _to_scan_layout(t, B, L):
    C = t.shape[-1]
    return t.reshape(B, N_CHUNK, L // N_CHUNK, C).transpose(0, 2, 1, 3).reshape(B, L, C)


def _from_scan_layout(t, B, L):
    C = t.shape[-1]
    return t.reshape(B, L // N_CHUNK, N_CHUNK, C).transpose(0, 2, 1, 3).reshape(B * L, C)


def _local_step(x, target, p, w_in, late_weights, mlp_grads_ready=None, rest_grads_ready=None, order=None, *,
                ssm_tile=128):
    B, L, D = x.shape
    T = B * L
    x2 = x.reshape(T, D)
    row = lambda v: v.reshape(1, -1)
    g1, g2, ga, gs, b_glu = row(p["norm1_g"]), row(p["norm2_g"]), row(p["attn_out_g"]), row(p["ssm_out_g"]), row(p["b_glu"])
    g1_first = g1 if order is None else g1 + order
    gq8 = jnp.tile(row(p["q_norm_g"]), (1, N_HEADS))
    gk8 = jnp.tile(row(p["k_norm_g"]), (1, N_HEADS))

    G, P, Hh = SSM_GROUPS, SSM_STATE, SSM_GROUP
    lam_re3, lam_im3 = p["ssm_lambda_re"].reshape(G, 1, P), p["ssm_lambda_im"].reshape(G, 1, P)
    log_dt3 = p["ssm_log_dt"].reshape(G, 1, 1)
    b_re_t, b_im_t = p["ssm_b_re"].transpose(0, 2, 1), p["ssm_b_im"].transpose(0, 2, 1)
    lbr, lbi, bbr, bbi = _ssm_prep(lam_re3, lam_im3, log_dt3, b_re_t, b_im_t)
    w_b = jnp.concatenate([_block_diag(bbr), _block_diag(bbi)], axis=2)
    c_mt = jnp.concatenate([_block_diag(p["ssm_c_re"]), -_block_diag(p["ssm_c_im"])], axis=2)
    w_bt, c_m = w_b.transpose(0, 2, 1), c_mt.transpose(0, 2, 1)
    lam_r = jnp.broadcast_to(lbr.reshape(SSM_COLS, 1, 512), (SSM_COLS, N_CHUNK, 512))
    lam_i = jnp.broadcast_to(lbi.reshape(SSM_COLS, 1, 512), (SSM_COLS, N_CHUNK, 512))
    d_skip = p["ssm_d"].reshape(1, SSM_WIDTH)

    xn = _rowwise("norm1", _rms, [x2], [g1_first], [(D, BF16)])
    proj = _matmul("proj", xn, w_in, tn=1024)
    qn, kn, vb = _qk_norm(proj, gq8, gk8)
    sb, rtot = _attn_fwd(qn, kn, vb, B, L)
    u_p = _to_scan_layout(proj[:, 3 * SB_WIDTH:], B, L)
    y_p, gel_p, xs, xin = _ssm_fwd(u_p, w_b, lam_r, lam_i, c_m, d_skip, B, L, ssm_tile)
    y2, gel = y_p.reshape(T, SSM_WIDTH), gel_p.reshape(T, SSM_WIDTH)
    w_glu, w_out, w_mlp_in, w_mlp_out = late_weights(gel)
    pre = _matmul("glu_gate", gel, w_glu)
    ssm_n = _rowwise("glu_out", _glu_branch, [y2, pre], [b_glu, gs], [(SSM_WIDTH, BF16)])
    sb_n = _rowwise("attn_out_norm", _rms, [sb], [ga], [(SB_WIDTH, BF16)])
    mixed = jnp.concatenate([sb_n, _from_scan_layout(ssm_n, B, L)], axis=1)
    def residual_and_norm(acc, res, g):
        h = acc + res
        return h, _rms(h, g)

    h1, hn = _matmul("out_proj", mixed, w_out, extras=[x2, g2], out_dtypes=(F32, BF16), tn=D,
                     epilogue=residual_and_norm)
    act, a_pre = _matmul("mlp_in", hn, w_mlp_in, out_dtypes=(BF16, BF16), tn=1024,
                         epilogue=lambda acc: (jnp.square(jnp.maximum(acc, 0.0)), acc))

    def loss_fn(acc, h, t):
        diff = acc + h - t
        part = jnp.sum(jnp.sum(diff * diff, axis=0, keepdims=True), axis=1, keepdims=True)
        d = diff * (1.0 / D)
        return d, d, part * (0.5 / D)

    d_out, d_out_b, loss = _matmul("mlp_out", act, w_mlp_out, extras=[h1, target.reshape(T, D)],
                                   out_dtypes=(F32, BF16), sums=[(1, 1)], epilogue=loss_fn, tm=512, tn=D)

    d_apre = _matmul("mlp_out_dx", d_out_b, w_mlp_out, tb=True, extras=[a_pre], out_dtypes=(BF16,), tn=1024,
                     epilogue=lambda acc, ap: (acc * (2.0 * jnp.maximum(ap.astype(F32), 0.0)),))
    both = lambda acc: (acc, acc)
    g_w_mlp_out, g_w_mlp_out_b = _matmul("mlp_out_dw", act, d_out_b, ta=True, out_dtypes=(F32, BF16), epilogue=both)
    g_w_mlp_in, g_w_mlp_in_b = _matmul("mlp_in_dw", hn, d_apre, ta=True, col_blocked=True, out_dtypes=(F32, BF16),
                                       epilogue=both, tn=w_mlp_in.shape[1] // N_DEV)
    if mlp_grads_ready is not None:
        g2 = g2 + mlp_grads_ready(g_w_mlp_out, g_w_mlp_out_b, g_w_mlp_in, g_w_mlp_in_b)

    def norm_bwd_res(dy, h, res, g):
        _, vjp = jax.vjp(_rms, h, g)
        dh, dg = vjp(dy)
        return res + dh, dg

    def norm_bwd_res2(dy, h, res, g):
        d, dg = norm_bwd_res(dy, h, res, g)
        return d, d, dg

    d_h1, d_h1_b, g_norm2 = _matmul("mlp_in_dx", d_apre, w_mlp_in, tb=True, extras=[h1, d_out, g2],
                                    out_dtypes=(F32, BF16), sums=[(1, D)], epilogue=norm_bwd_res2, tm=512, tn=D)

    d_mixed = _matmul("out_proj_dx", d_h1_b, w_out, tb=True, tn=1024)
    g_w_out, g_w_out_b = _matmul("out_proj_dw", mixed, d_h1_b, ta=True, out_dtypes=(F32, BF16), epilogue=both)

    def norm_bwd(h, dy, g):
        _, vjp = jax.vjp(_rms, h, g)
        return vjp(dy)

    d_sb, g_attn_out = _rowwise("attn_out_norm_bwd", norm_bwd, [sb, (d_mixed, 0, SB_WIDTH)], [ga],
                                [(SB_WIDTH, F32)], sums=[(1, SB_WIDTH)])
    d_ssm_n = _to_scan_layout(d_mixed[:, SB_WIDTH:], B, L).reshape(T, SSM_WIDTH)

    def glu_bwd(y, pre_, dy, bg, g):
        _, vjp = jax.vjp(_glu_branch, y, pre_, bg, g)
        d_y, d_pre, d_bg, d_g = vjp(dy)
        return d_y, d_pre, d_bg, d_g

    d_y_direct, d_pre, g_b_glu, g_ssm_out = _rowwise(
        "glu_out_bwd", glu_bwd, [y2, pre, d_ssm_n], [b_glu, gs], [(SSM_WIDTH, F32), (SSM_WIDTH, BF16)],
        sums=[(1, SSM_WIDTH), (1, SSM_WIDTH)])
    g_w_glu, g_w_glu_b = _matmul("glu_gate_dw", gel, d_pre, ta=True, out_dtypes=(F32, BF16), epilogue=both)

    def gelu_bwd(dg, y, dy0):
        _, vjp = jax.vjp(jax.nn.gelu, y)
        return (dy0 + vjp(dg)[0],)

    d_y = _matmul("glu_gate_dx", d_pre, w_glu, tb=True, extras=[y2, d_y_direct], epilogue=gelu_bwd)

    du_p, d_wb, d_cm, d_lr, d_li, g_d = _ssm_bwd(
        d_y.reshape(B, L, SSM_WIDTH), u_p, xs, xin, w_bt, lam_r, lam_i, c_mt, d_skip, B, L, ssm_tile)
    d_bb = _block_diag_take(d_wb.reshape(SSM_COLS, LANES, 2, 512))
    d_c = _block_diag_take(d_cm.transpose(0, 2, 1).reshape(SSM_COLS, LANES, 2, 512))
    g_lam_re, g_lam_im, g_log_dt, g_b_re_t, g_b_im_t = _ssm_prep_bwd(
        lam_re3, lam_im3, log_dt3, b_re_t, b_im_t,
        d_lr.reshape(G, 1, P), d_li.reshape(G, 1, P), d_bb[0], d_bb[1])
    d_qn, d_kn, d_v = _attn_bwd(qn, kn, vb, rtot, d_sb, B, L)
    d_q, d_k, g_q, g_k = _qk_norm_bwd(proj, gq8, gk8, d_qn, d_kn)

    d_proj = jnp.concatenate([d_q, d_k, d_v, _from_scan_layout(du_p, B, L)], axis=1)
    g_w_in, g_w_in_b = _matmul("proj_dw", xn, d_proj, ta=True, col_blocked=True, out_dtypes=(F32, BF16),
                               epilogue=both, tn=w_in.shape[1] // N_DEV)
    if rest_grads_ready is not None:
        g1 = g1 + rest_grads_ready([g_w_in, g_w_glu, g_w_out], [g_w_in_b, g_w_glu_b, g_w_out_b])
    grad_x, g_norm1 = _matmul("proj_dx", d_proj, w_in, tb=True, extras=[x2, d_h1, g1], sums=[(1, D)],
                              epilogue=norm_bwd_res, tm=512, tn=D)

    small = {
        "norm1_g": g_norm1.reshape(-1),
        "q_norm_g": g_q.reshape(-1),
        "k_norm_g": g_k.reshape(-1),
        "ssm_lambda_re": g_lam_re.reshape(G, P),
        "ssm_lambda_im": g_lam_im.reshape(G, P),
        "ssm_log_dt": g_log_dt.reshape(G),
        "ssm_b_re": g_b_re_t.transpose(0, 2, 1),
        "ssm_b_im": g_b_im_t.transpose(0, 2, 1),
        "ssm_c_re": d_c[0],
        "ssm_c_im": -d_c[1],
        "ssm_d": g_d.reshape(G, Hh),
        "b_glu": g_b_glu.reshape(-1),
        "attn_out_g": g_attn_out.reshape(-1),
        "ssm_out_g": g_ssm_out.reshape(-1),
        "norm2_g": g_norm2.reshape(-1),
    }
    big = {"w_in": g_w_in, "w_glu": g_w_glu, "w_out": g_w_out, "w_mlp_in": g_w_mlp_in, "w_mlp_out": g_w_mlp_out}
    return loss[0, 0], grad_x.reshape(B, L, D), small, big


_ANY = pl.BlockSpec(memory_space=pl.ANY)
_MESH = pl.DeviceIdType.MESH


def _all_gather(name, shards):
    n = len(shards)

    def body(*refs):
        in_refs, out_refs = refs[:n], refs[n:2 * n]
        send_sems, recv_sems, local_sems = refs[2 * n:]
        x, y, c = lax.axis_index("x"), lax.axis_index("y"), lax.axis_index("c")
        me, sibling = (x, y, c), (x, y, 1 - c)
        chips = [(1 - x, y), (x, 1 - y), (1 - x, 1 - y)]

        def copy(a, k, block, to, src=None):
            px, py, pc = block
            rows = out_refs[a].at[4 * px + 2 * py + pc]
            return pltpu.make_async_remote_copy(
                src_ref=rows if src is None else src, dst_ref=rows, send_sem=send_sems.at[a, k],
                recv_sem=recv_sems.at[a, k], device_id=to, device_id_type=_MESH)

        mine = [pltpu.make_async_copy(in_refs[a], out_refs[a].at[4 * x + 2 * y + c], local_sems.at[a]) for a in range(n)]
        first, passed = [], []
        for a in range(n):
            mine[a].start()
            first.append(copy(a, 0, me, sibling, src=in_refs[a]))
            first += [copy(a, 1 + j, me, (*chip, c), src=in_refs[a]) for j, chip in enumerate(chips)]
        for cp in first:
            cp.start()
        for j, chip in enumerate(chips):
            for a in range(n):
                copy(a, 1 + j, (*chip, c), me).wait_recv()
                fwd = copy(a, 4 + j, (*chip, c), sibling)
                fwd.start()
                passed.append(fwd)
        for a in range(n):
            copy(a, 0, sibling, me).wait_recv()
            for j, chip in enumerate(chips):
                copy(a, 4 + j, (*chip, 1 - c), me).wait_recv()
        for cp in first + passed:
            cp.wait_send()
        for cp in mine:
            cp.wait()

    return pl.pallas_call(
        body, name=name,
        in_specs=[_ANY] * n, out_specs=[_ANY] * n,
        out_shape=[jax.ShapeDtypeStruct((N_DEV, *s.shape), s.dtype) for s in shards],
        scratch_shapes=[pltpu.SemaphoreType.DMA((n, 7)), pltpu.SemaphoreType.DMA((n, 7)), pltpu.SemaphoreType.DMA((n,))],
    )(*shards)


_HBM = pl.BlockSpec(memory_space=pltpu.HBM)
_SEM = pl.BlockSpec(memory_space=pltpu.SEMAPHORE)
_EFFECT = pltpu.SideEffectType.DATAFLOW_SIDE_EFFECTING
_FLIPS = [(dx, dy, dc) for dx in (0, 1) for dy in (0, 1) for dc in (0, 1) if (dx, dy, dc) != (0, 0, 0)]


def _exchange_start(name, srcs, lands, per_peer):
    n = len(srcs)

    def body(*refs):
        src_refs, land_refs = refs[:n], refs[n:2 * n]
        send_sems, recv_sems = refs[2 * n:3 * n], refs[3 * n:4 * n]
        token = refs[-1]
        x, y, c = lax.axis_index("x"), lax.axis_index("y"), lax.axis_index("c")
        me = 4 * x + 2 * y + c
        for dx, dy, dc in _FLIPS:
            px, py, pc = (1 - x if dx else x), (1 - y if dy else y), (1 - c if dc else c)
            for a in range(n):
                pltpu.make_async_remote_copy(
                    src_ref=src_refs[a].at[4 * px + 2 * py + pc] if per_peer else src_refs[a],
                    dst_ref=land_refs[a].at[me], send_sem=send_sems[a], recv_sem=recv_sems[a],
                    device_id=(px, py, pc), device_id_type=_MESH).start()
        token[...] = jnp.zeros_like(token)

    hbm = lambda t: pltpu.with_memory_space_constraint(t, pltpu.HBM)
    res = pl.pallas_call(
        body, name=name,
        out_shape=(*[pltpu.SemaphoreType.DMA(())] * (2 * n), *[pltpu.HBM(t.shape, t.dtype) for t in (*srcs, *lands)],
                   jax.ShapeDtypeStruct((8, LANES), F32)),
        in_specs=[_HBM] * (2 * n),
        out_specs=(*[_SEM] * (2 * n), *[_HBM] * (2 * n), pl.BlockSpec(memory_space=pltpu.VMEM)),
        input_output_aliases={i: 2 * n + i for i in range(2 * n)},
        compiler_params=pltpu.CompilerParams(has_side_effects=_EFFECT),
    )(*[hbm(t) for t in (*srcs, *lands)])
    return res[:-1], res[-1]


def _exchange_wait(name, handle, after):
    n = len(handle) // 4
    sems, thru = handle[:2 * n], handle[2 * n:]

    def body(*refs):
        land_refs = refs[n:2 * n]
        send_sems, recv_sems = refs[2 * n:3 * n], refs[3 * n:4 * n]
        me = (lax.axis_index("x"), lax.axis_index("y"), lax.axis_index("c"))
        for a in range(n):
            seven = land_refs[a].at[pl.ds(0, len(_FLIPS))]
            all_copies = pltpu.make_async_remote_copy(
                src_ref=seven, dst_ref=seven, send_sem=send_sems[a], recv_sem=recv_sems[a], device_id=me,
                device_id_type=_MESH)
            all_copies.wait_send()
            all_copies.wait_recv()

    res = pl.pallas_call(
        body, name=name, out_shape=tuple(pltpu.HBM(t.shape, t.dtype) for t in thru),
        in_specs=[*[_HBM] * (2 * n), *[_SEM] * (2 * n), _ANY], out_specs=tuple([_HBM] * (2 * n)),
        input_output_aliases={i: i for i in range(2 * n)},
        compiler_params=pltpu.CompilerParams(has_side_effects=_EFFECT),
    )(*thru, *sems, after)
    return res[n:]


def _adamw_gathered(name, own, parts, me, w, m, v):
    r, c = w.shape
    tr = min(r, 256)

    def body(me_ref, own_ref, p_ref, w_ref, m_ref, v_ref, g_out, d_out, m_out, v_out):
        g = own_ref[...]
        for j in range(N_DEV):
            g = g + p_ref[j].astype(F32)
        delta, m_new, v_new = _adamw(w_ref[...], g, m_ref[...], v_ref[...])
        g_out[...] = g
        d_out[...] = delta
        m_out[...] = m_new
        v_out[...] = v_new

    spec = pl.BlockSpec((tr, c), lambda i, me_ref: (i, 0))
    return pl.pallas_call(
        body, name=name,
        grid_spec=pltpu.PrefetchScalarGridSpec(
            num_scalar_prefetch=1, grid=(r // tr,),
            in_specs=[pl.BlockSpec((None, tr, c), lambda i, me_ref: (me_ref[0], i, 0)),
                      pl.BlockSpec((N_DEV, tr, c), lambda i, me_ref: (0, i, 0)), spec, spec, spec],
            out_specs=[spec] * 4),
        out_shape=[jax.ShapeDtypeStruct((r, c), F32)] * 4,
        compiler_params=_params(("parallel",)),
    )(me, own, parts, w, m, v)


def _adamw(w, g, m, v):
    m = ADAM_B1 * m + (1.0 - ADAM_B1) * g
    v = ADAM_B2 * v + (1.0 - ADAM_B2) * jnp.square(g)
    m_hat = m / (1.0 - ADAM_B1 ** ADAM_STEP)
    v_hat = v / (1.0 - ADAM_B2 ** ADAM_STEP)
    delta = -ADAM_LR * (m_hat / (jnp.sqrt(v_hat) + ADAM_EPS) + ADAM_WD * w)
    return delta, m, v


def _adamw_small(name, parts, w, m, v):
    _, r, c = parts.shape
    tr = 8

    def body(p_ref, w_ref, m_ref, v_ref, g_out, d_out, m_out, v_out):
        g = p_ref[0]
        for j in range(1, N_DEV):
            g = g + p_ref[j]
        delta, m_new, v_new = _adamw(w_ref[...], g, m_ref[...], v_ref[...])
        g_out[...] = g
        d_out[...] = delta
        m_out[...] = m_new
        v_out[...] = v_new

    spec = pl.BlockSpec((tr, c), lambda i: (i, 0))
    return pl.pallas_call(
        body, name=name, grid=(r // tr,),
        in_specs=[pl.BlockSpec((N_DEV, tr, c), lambda i: (0, i, 0)), spec, spec, spec],
        out_specs=[spec] * 4, out_shape=[jax.ShapeDtypeStruct((r, c), F32)] * 4,
        compiler_params=_params(("parallel",)),
    )(parts, w, m, v)


_WEIGHTS = ["norm1_g", "w_in", "q_norm_g", "k_norm_g", "ssm_lambda_re", "ssm_lambda_im", "ssm_log_dt", "ssm_b_re",
            "ssm_b_im", "ssm_c_re", "ssm_c_im", "ssm_d", "w_glu", "b_glu", "attn_out_g", "ssm_out_g", "w_out",
            "norm2_g", "w_mlp_in", "w_mlp_out"]
_BIG = ["w_in", "w_glu", "w_out", "w_mlp_in", "w_mlp_out"]
_SMALL = [n for n in _WEIGHTS if n not in _BIG]
_PACK_COLS = 1024


def _pack(tree, last=None):
    flat = [tree[n].reshape(-1).astype(F32) for n in _SMALL]
    size = sum(f.shape[0] for f in flat)
    rows = -(-(size + 1) // (_PACK_COLS * 8)) * 8
    pad = jnp.zeros((rows * _PACK_COLS - size - 1,), F32)
    tail = jnp.zeros((1,), F32) if last is None else last.reshape(1).astype(F32)
    return jnp.concatenate(flat + [pad, tail]).reshape(rows, _PACK_COLS)


def _unpack(buf, like):
    flat, out, off = buf.reshape(-1), {}, 0
    for n in _SMALL:
        size = like[n].size
        out[n] = flat[off:off + size].reshape(like[n].shape)
        off += size
    return out


def kernel(x, norm1_g, w_in, q_norm_g, k_norm_g, ssm_lambda_re, ssm_lambda_im, ssm_log_dt, ssm_b_re, ssm_b_im, ssm_c_re, ssm_c_im, ssm_d, w_glu, b_glu, attn_out_g, ssm_out_g, w_out, norm2_g, w_mlp_in, w_mlp_out, loss_target, m_norm1_g, m_w_in, m_q_norm_g, m_k_norm_g, m_ssm_lambda_re, m_ssm_lambda_im, m_ssm_log_dt, m_ssm_b_re, m_ssm_b_im, m_ssm_c_re, m_ssm_c_im, m_ssm_d, m_w_glu, m_b_glu, m_attn_out_g, m_ssm_out_g, m_w_out, m_norm2_g, m_w_mlp_in, m_w_mlp_out, v_norm1_g, v_w_in, v_q_norm_g, v_k_norm_g, v_ssm_lambda_re, v_ssm_lambda_im, v_ssm_log_dt, v_ssm_b_re, v_ssm_b_im, v_ssm_c_re, v_ssm_c_im, v_ssm_d, v_w_glu, v_b_glu, v_attn_out_g, v_ssm_out_g, v_w_out, v_norm2_g, v_w_mlp_in, v_w_mlp_out):
    w = dict(zip(_WEIGHTS, (norm1_g, w_in, q_norm_g, k_norm_g, ssm_lambda_re, ssm_lambda_im, ssm_log_dt, ssm_b_re, ssm_b_im, ssm_c_re, ssm_c_im, ssm_d, w_glu, b_glu, attn_out_g, ssm_out_g, w_out, norm2_g, w_mlp_in, w_mlp_out)))
    m = dict(zip(_WEIGHTS, (m_norm1_g, m_w_in, m_q_norm_g, m_k_norm_g, m_ssm_lambda_re, m_ssm_lambda_im, m_ssm_log_dt, m_ssm_b_re, m_ssm_b_im, m_ssm_c_re, m_ssm_c_im, m_ssm_d, m_w_glu, m_b_glu, m_attn_out_g, m_ssm_out_g, m_w_out, m_norm2_g, m_w_mlp_in, m_w_mlp_out)))
    v = dict(zip(_WEIGHTS, (v_norm1_g, v_w_in, v_q_norm_g, v_k_norm_g, v_ssm_lambda_re, v_ssm_lambda_im, v_ssm_log_dt, v_ssm_b_re, v_ssm_b_im, v_ssm_c_re, v_ssm_c_im, v_ssm_d, v_w_glu, v_b_glu, v_attn_out_g, v_ssm_out_g, v_w_out, v_norm2_g, v_w_mlp_in, v_w_mlp_out)))
    core = lax.axis_index("c").astype(jnp.int32).reshape(1)
    chip = (2 * lax.axis_index("x") + lax.axis_index("y")).astype(jnp.int32).reshape(1)

    me = (2 * chip + core).astype(jnp.int32)

    def landing(own=None, like=None):
        own = jnp.zeros_like(like) if own is None else own
        return lax.dynamic_update_slice(lax.empty((N_DEV, *like.shape), like.dtype), own[None], (me[0], 0, 0))

    (w_in_blocks,) = _all_gather("w_in_all_gather", [w_in.astype(BF16)])
    w_in_full = w_in_blocks.transpose(1, 0, 2).reshape(w_in.shape[0], -1)
    late = [n for n in _BIG if n != "w_in"]
    shards = [w[n].astype(BF16) for n in late]
    w_in_blocks, shards = lax.optimization_barrier((w_in_blocks, shards))
    weights_handle, weights_token = _exchange_start(
        "weights_send", shards, [landing(s, s) for s in shards], per_peer=False)

    def late_weights(after):
        got = dict(zip(late, _exchange_wait("weights_arrive", weights_handle, after)))
        return (got["w_glu"].reshape(-1, w_glu.shape[1]), got["w_out"].reshape(-1, w_out.shape[1]),
                got["w_mlp_in"].transpose(1, 0, 2).reshape(w_mlp_in.shape[0], -1),
                got["w_mlp_out"].reshape(-1, w_mlp_out.shape[1]))

    mlp = ["w_mlp_out", "w_mlp_in"]
    sent = {}

    def send_grads(name, names, own, own_b):
        blocks = lambda g, n: g.reshape(N_DEV, *w[n].shape)
        sent[name + "_own"] = [blocks(g, n) for g, n in zip(own, names)]
        srcs = [blocks(g, n) for g, n in zip(own_b, names)]
        sent[name], token = _exchange_start(name, srcs, [landing(like=s[0]) for s in srcs], per_peer=True)
        return token[0, 0]

    def mlp_grads_ready(g_out, g_out_b, g_in, g_in_b):
        return send_grads("mlp_grads_send", mlp, [g_out, g_in], [g_out_b, g_in_b])

    rest = ["w_in", "w_glu", "w_out"]

    def rest_grads_ready(own, own_b):
        return send_grads("rest_grads_send", rest, own, own_b)

    loss_local, grad_x, g_small, g_big = _local_step(
        x, loss_target, {n: w[n] for n in _SMALL}, w_in_full, late_weights, mlp_grads_ready, rest_grads_ready,
        weights_token[0, 0])

    grads, delta, new_m, new_v = {}, {}, {}, {}
    small = _pack(g_small, last=loss_local)
    small_handle, small_token = _exchange_start("small_grads_send", [small], [landing(small, small)], per_peer=False)

    for send, arrive, names in (("mlp_grads_send", "mlp_grads_arrive", mlp),
                                ("rest_grads_send", "rest_grads_arrive", rest)):
        for n, own, part in zip(names, sent[send + "_own"], _exchange_wait(arrive, sent[send], small_token)):
            grads[n], delta[n], new_m[n], new_v[n] = _adamw_gathered("adamw_" + n, own, part, me, w[n], m[n], v[n])

    shards_done = lax.optimization_barrier(tuple(new_v[n] for n in _BIG))
    (small_parts,) = _exchange_wait("small_grads_arrive", small_handle, shards_done[-1])
    packed = _adamw_small("adamw_small", small_parts, _pack(w), _pack(m), _pack(v))
    for tree, buf in zip((grads, delta, new_m, new_v), packed):
        tree.update(_unpack(buf, w))
    loss = packed[0][-1, -1]

    return (loss, grad_x, *[grads[n] for n in _WEIGHTS], *[delta[n] for n in _WEIGHTS],
            *[new_m[n] for n in _WEIGHTS], *[new_v[n] for n in _WEIGHTS])
```

```python
import functools
import math

import jax
import jax.numpy as jnp
from jax import lax
from jax.experimental import pallas as pl
from jax.experimental.pallas import tpu as pltpu

F32 = jnp.float32
BF16 = jnp.bfloat16

EPS = 1e-6
HEAD_DIM = 64
N_HEADS = 8
SB_WIDTH = 512
SSM_WIDTH = 512
SSM_GROUP = 16
SSM_GROUPS = 32
SSM_STATE = 64
QBLOCK = 128
KBLOCK = 256
N_CHUNK = 8
SSM_COLS = 4
LANES = 128
N_DEV = 8

ADAM_LR = 0.001
ADAM_B1 = 0.9
ADAM_B2 = 0.999
ADAM_EPS = 1e-08
ADAM_WD = 0.01
ADAM_STEP = 10

VMEM_LIMIT = 56 * 1024 * 1024

_NT = (((1,), (1,)), ((), ()))
_NN = (((1,), (0,)), ((), ()))
_TN = (((0,), (0,)), ((), ()))


def _dot(a, b, dims=_NN):
    return lax.dot_general(a, b, dims, preferred_element_type=F32)


def _params(sem):
    return pltpu.CompilerParams(dimension_semantics=sem, vmem_limit_bytes=VMEM_LIMIT)


def _matmul(name, a, b, *, ta=False, tb=False, extras=(), epilogue=None, out_dtypes=(F32,), sums=(),
            col_blocked=False, tm=1024, tn=512, tk=4096):
    M, K = (a.shape[1], a.shape[0]) if ta else a.shape
    N = b.shape[0] if tb else b.shape[1]
    tm, tn, tk = min(tm, M), min(tn, N), min(tk, K)
    assert M % tm == 0 and N % tn == 0 and K % tk == 0, (name, M, N, K)
    assert not sums or (tn == N and tk == K), name
    nk = K // tk
    n_ex, n_out, n_sum = len(extras), len(out_dtypes), len(sums)
    dims = (((0 if ta else 1,), (1 if tb else 0,)), ((), ()))

    def body(*refs):
        a_ref, b_ref = refs[0], refs[1]
        ex_refs = refs[2:2 + n_ex]
        o_refs = refs[2 + n_ex:2 + n_ex + n_out]
        s_refs = refs[2 + n_ex + n_out:2 + n_ex + n_out + n_sum]
        k = pl.program_id(2)
        part = _dot(a_ref[...].astype(BF16), b_ref[...].astype(BF16), dims)

        def finish(acc):
            outs = (acc,) if epilogue is None else epilogue(acc, *[e[...] for e in ex_refs])
            for o_ref, o in zip(o_refs, outs[:n_out]):
                o_ref[...] = o.astype(o_ref.dtype)
            if n_sum:
                @pl.when(pl.program_id(0) == 0)
                def _():
                    for s_ref in s_refs:
                        s_ref[...] = jnp.zeros_like(s_ref)

                for s_ref, v in zip(s_refs, outs[n_out:]):
                    s_ref[...] += v

        if nk == 1:
            finish(part)
        else:
            acc_ref = refs[-1]

            @pl.when(k == 0)
            def _():
                acc_ref[...] = part

            @pl.when(jnp.logical_and(k > 0, k < nk - 1))
            def _():
                acc_ref[...] += part

            @pl.when(k == nk - 1)
            def _():
                finish(acc_ref[...] + part)

    a_spec = pl.BlockSpec((tk, tm), lambda i, j, k: (k, i)) if ta else pl.BlockSpec((tm, tk), lambda i, j, k: (i, k))
    b_spec = pl.BlockSpec((tn, tk), lambda i, j, k: (j, k)) if tb else pl.BlockSpec((tk, tn), lambda i, j, k: (k, j))
    ex_specs = [pl.BlockSpec((1, tn), lambda i, j, k: (0, j)) if e.shape[0] == 1 else
                pl.BlockSpec((tm, tn), lambda i, j, k: (i, j)) for e in extras]
    if col_blocked:
        out_specs = [pl.BlockSpec((None, tm, tn), lambda i, j, k: (j, i, 0)) for _ in out_dtypes]
        out_shape = [jax.ShapeDtypeStruct((N // tn, M, tn), dt) for dt in out_dtypes]
    else:
        out_specs = [pl.BlockSpec((tm, tn), lambda i, j, k: (i, j)) for _ in out_dtypes]
        out_shape = [jax.ShapeDtypeStruct((M, N), dt) for dt in out_dtypes]
    out_specs += [pl.BlockSpec(s, lambda i, j, k: (0, 0)) for s in sums]
    out_shape += [jax.ShapeDtypeStruct(s, F32) for s in sums]
    outs = pl.pallas_call(
        body, name=name, grid=(M // tm, N // tn, nk),
        in_specs=[a_spec, b_spec, *ex_specs], out_specs=out_specs, out_shape=out_shape,
        scratch_shapes=[pltpu.VMEM((tm, tn), F32)] if nk > 1 else [],
        compiler_params=_params(("arbitrary",) * 3 if sums else ("parallel", "parallel", "arbitrary")),
    )(a, b, *extras)
    return outs[0] if len(outs) == 1 else outs


def _rowwise(name, fn, rows, small, outs, sums=(), tile=256):
    specs, args = [], []
    T = None
    for r in rows:
        arr, cb, w = r if isinstance(r, tuple) else (r, 0, r.shape[1])
        T = arr.shape[0]
        specs.append((w, cb))
        args.append(arr)
    tile = min(tile, T)
    assert T % tile == 0
    n_r, n_s, n_o, n_a = len(rows), len(small), len(outs), len(sums)

    def body(*refs):
        r_refs = refs[:n_r]
        s_refs = refs[n_r:n_r + n_s]
        o_refs = refs[n_r + n_s:n_r + n_s + n_o]
        a_refs = refs[n_r + n_s + n_o:]
        res = fn(*[r[...] for r in r_refs], *[s[...] for s in s_refs])
        res = res if isinstance(res, (tuple, list)) else (res,)
        for o_ref, o in zip(o_refs, res[:n_o]):
            o_ref[...] = o.astype(o_ref.dtype)

        @pl.when(pl.program_id(0) == 0)
        def _():
            for a_ref in a_refs:
                a_ref[...] = jnp.zeros_like(a_ref)

        for a_ref, v in zip(a_refs, res[n_o:]):
            a_ref[...] += v.astype(F32)

    in_specs = [pl.BlockSpec((tile, w), functools.partial(lambda i, cb: (i, cb), cb=cb)) for w, cb in specs]
    in_specs += [pl.BlockSpec(s.shape, functools.partial(lambda i, nd: (0,) * nd, nd=s.ndim)) for s in small]
    out_specs = [pl.BlockSpec((tile, w), lambda i: (i, 0)) for w, _ in outs]
    out_specs += [pl.BlockSpec(s, functools.partial(lambda i, nd: (0,) * nd, nd=len(s))) for s in sums]
    out_shape = [jax.ShapeDtypeStruct((T, w), dt) for w, dt in outs]
    out_shape += [jax.ShapeDtypeStruct(s, F32) for s in sums]
    res = pl.pallas_call(
        body, name=name, grid=(T // tile,), in_specs=in_specs, out_specs=out_specs, out_shape=out_shape,
        compiler_params=_params(("arbitrary",)),
    )(*args, *small)
    return res[0] if len(res) == 1 else res


def _rms(x, g):
    return x * lax.rsqrt(jnp.mean(x * x, axis=-1, keepdims=True) + EPS) * g


def _glu_branch(y, pre, b_glu, g_out):
    g = jax.nn.gelu(y)
    return _rms(g * jax.nn.sigmoid(pre + b_glu), g_out)


def _split_dot(x, tri_bf):
    hi = x.astype(BF16)
    lo = (x - hi.astype(F32)).astype(BF16)
    return _dot(hi, tri_bf) + _dot(lo, tri_bf)


def _softplus(z):
    return jnp.maximum(z, 0.0) + jnp.log(1.0 + jnp.exp(-jnp.abs(z)))


def _head(h):
    return slice(h * HEAD_DIM, (h + 1) * HEAD_DIM)


def _head_mean(x, seg):
    return _split_dot(x, seg) * (1.0 / HEAD_DIM)


def _qk_norm(proj, gq, gk):
    scale = 1.0 / math.sqrt(HEAD_DIM)
    idx = jnp.arange(SB_WIDTH) // HEAD_DIM
    seg = (idx[:, None] == idx[None, :]).astype(BF16)

    def fn(q, k, v, gq_, gk_, seg_):
        qn = q * lax.rsqrt(_head_mean(q * q, seg_) + EPS) * (gq_ * scale)
        kn = k * lax.rsqrt(_head_mean(k * k, seg_) + EPS) * gk_
        return qn, kn, v

    return _rowwise("qk_norm", fn, [(proj, 0, SB_WIDTH), (proj, 1, SB_WIDTH), (proj, 2, SB_WIDTH)], [gq, gk, seg],
                    [(SB_WIDTH, BF16)] * 3)


def _qk_norm_bwd(proj, gq, gk, d_qn, d_kn):
    scale = 1.0 / math.sqrt(HEAD_DIM)
    idx = jnp.arange(SB_WIDTH) // HEAD_DIM
    seg = (idx[:, None] == idx[None, :]).astype(BF16)

    def one(x, g, dy, seg_):
        r = lax.rsqrt(_head_mean(x * x, seg_) + EPS)
        gdy = g * dy
        dx = r * gdy - x * (r * r * r) * _head_mean(gdy * x, seg_)
        dg = jnp.sum(dy * x * r, axis=0, keepdims=True)
        return dx, sum(dg[:, _head(h)] for h in range(N_HEADS))

    def fn(q, k, dqn, dkn, gq_, gk_, seg_):
        dq, dgq = one(q, gq_, dqn * scale, seg_)
        dk, dgk = one(k, gk_, dkn, seg_)
        return dq, dk, dgq, dgk

    return _rowwise("qk_norm_bwd", fn, [(proj, 0, SB_WIDTH), (proj, 1, SB_WIDTH), d_qn, d_kn], [gq, gk, seg],
                    [(SB_WIDTH, BF16)] * 2, sums=[(1, HEAD_DIM)] * 2)


def _split_heads(refs, scratch, L):
    def chunk(i, _):
        r = pl.ds(pl.multiple_of(i * QBLOCK, QBLOCK), QBLOCK)
        for ref, s in zip(refs, scratch):
            for h in range(2):
                s[h, r, :] = ref[r, _head(h)]
        return 0

    lax.fori_loop(0, L // QBLOCK, chunk, 0)


Q_HALVES = KBLOCK // QBLOCK
_CHAINS = [(h, r) for h in range(2) for r in range(Q_HALVES)]


def _valid(i, kb):
    row = lax.broadcasted_iota(jnp.int32, (QBLOCK, KBLOCK), 0)
    col = lax.broadcasted_iota(jnp.int32, (QBLOCK, KBLOCK), 1)
    return col + (kb * KBLOCK - i * QBLOCK) < row


def _attn_fwd(qn, kn, vb, B, L):
    n_pairs = L // KBLOCK
    n_hp = N_HEADS // 2
    nc = len(_CHAINS)

    def body(q_ref, k_ref, v_ref, o_ref, ab_ref, q_s, k_s, v_s, after_s, z_s, stage_s, sems):
        _split_heads((q_ref, k_ref, v_ref), (q_s, k_s, v_s), L)
        r2 = lax.broadcasted_iota(jnp.int32, (KBLOCK, KBLOCK), 0)
        c2 = lax.broadcasted_iota(jnp.int32, (KBLOCK, KBLOCK), 1)
        after_s[...] = (r2 > c2).astype(after_s.dtype)
        g = pl.program_id(0) * n_hp + pl.program_id(1)

        def q_pair(p, _):
            rows = [pl.ds(pl.multiple_of((p * Q_HALVES + r) * QBLOCK, QBLOCK), QBLOCK) for r in range(Q_HALVES)]
            q_c = [q_s[h, rows[r], :] for h, r in _CHAINS]
            cs = range(nc)

            def scores(kb):
                rk = pl.ds(pl.multiple_of(kb * KBLOCK, KBLOCK), KBLOCK)
                return [_dot(q_c[c], k_s[_CHAINS[c][0], rk, :], _NT) for c in cs]

            def saved(kb):
                return pltpu.make_async_copy(stage_s.at[kb & 1], ab_ref.at[g, p, kb], sems.at[kb & 1])

            def k_block(kb, carry, diagonal):
                rk = pl.ds(pl.multiple_of(kb * KBLOCK, KBLOCK), KBLOCK)
                if diagonal:
                    valid = [_valid(p * Q_HALVES + r, kb) for r in range(Q_HALVES)]
                    keep = lambda c, t: jnp.where(valid[_CHAINS[c][1]], t, 0.0)
                    z = scores(kb)
                else:
                    keep = lambda c, t: t
                    z = [z_s[(kb + 1) & 1, c] for c in cs]
                ahead = scores(jnp.maximum(kb - 1, 0))
                for c in cs:
                    z_s[kb & 1, c] = ahead[c]
                sp = [_softplus(z[c]) for c in cs]
                lsig = [z[c] - sp[c] for c in cs]
                lom = [keep(c, -sp[c]) for c in cs]
                tail = [_split_dot(lom[c], after_s[...]) + carry[c][0] for c in cs]
                a = [keep(c, jnp.exp(lsig[c] + tail[c])).astype(v_s.dtype) for c in cs]
                acc = [carry[c][1] + _dot(a[c], v_s[_CHAINS[c][0], rk, :]) for c in cs]
                for c in cs:
                    stage_s[kb & 1, 0, c] = a[c]
                    stage_s[kb & 1, 1, c] = jnp.exp(lsig[c]).astype(stage_s.dtype)
                saved(kb).start()
                return tuple((carry[c][0] + jnp.sum(lom[c], axis=1, keepdims=True), acc[c]) for c in cs)

            def next_block(n, carry):
                kb = p - n

                @pl.when(n >= 2)
                def _():
                    saved(kb + 2).wait()

                return k_block(kb, carry, False)

            init = (jnp.zeros((QBLOCK, 1), F32), jnp.zeros((QBLOCK, HEAD_DIM), F32))
            first = k_block(p, (init,) * nc, True)
            res = lax.fori_loop(1, p + 1, next_block, first)
            saved(0).wait()

            @pl.when(p >= 1)
            def _():
                saved(1).wait()

            for r in range(Q_HALVES):
                o_ref[rows[r], :] = jnp.concatenate([res[c][1] for c in cs if _CHAINS[c][1] == r], axis=1)
            return 0

        lax.fori_loop(0, n_pairs, q_pair, 0)

    spec = pl.BlockSpec((L, LANES), lambda b, p: (b, p))
    return pl.pallas_call(
        body, name="attn_fwd", grid=(B, n_hp),
        in_specs=[spec] * 3, out_specs=[spec, _ANY],
        out_shape=[jax.ShapeDtypeStruct((B * L, SB_WIDTH), F32),
                   jax.ShapeDtypeStruct((B * n_hp, n_pairs, n_pairs, 2, nc, QBLOCK, KBLOCK), BF16)],
        scratch_shapes=[pltpu.VMEM((2, L, HEAD_DIM), BF16)] * 3 + [pltpu.VMEM((KBLOCK, KBLOCK), BF16)]
        + [pltpu.VMEM((2, nc, QBLOCK, KBLOCK), F32), pltpu.VMEM((2, 2, nc, QBLOCK, KBLOCK), BF16),
           pltpu.SemaphoreType.DMA((2,))],
        compiler_params=_params(("parallel", "parallel")),
    )(qn, kn, vb)


def _attn_bwd(qn, kn, vb, ab, d_sb, B, L):
    n_pairs = L // KBLOCK
    n_hp = N_HEADS // 2
    nc = len(_CHAINS)

    def body(q_ref, k_ref, v_ref, do_ref, ab_ref, dq_ref, dk_ref, dv_ref,
             k_s, v_s, qt_s, dkt_s, dvt_s, before_s, stage_s, sems):
        _split_heads((k_ref, v_ref), (k_s, v_s), L)
        g = pl.program_id(0) * n_hp + pl.program_id(1)

        def transpose_q(i, _):
            r = pl.ds(pl.multiple_of(i * QBLOCK, QBLOCK), QBLOCK)
            qt_s[:, r] = q_ref[r, :].astype(F32).T.astype(qt_s.dtype)
            return 0

        lax.fori_loop(0, L // QBLOCK, transpose_q, 0)
        dkt_s[...] = jnp.zeros_like(dkt_s)
        dvt_s[...] = jnp.zeros_like(dvt_s)
        r2 = lax.broadcasted_iota(jnp.int32, (KBLOCK, KBLOCK), 0)
        c2 = lax.broadcasted_iota(jnp.int32, (KBLOCK, KBLOCK), 1)
        before_s[...] = (r2 < c2).astype(before_s.dtype)

        def q_pair(p, _):
            rows = [pl.ds(pl.multiple_of((p * Q_HALVES + r) * QBLOCK, QBLOCK), QBLOCK) for r in range(Q_HALVES)]
            pair = pl.ds(pl.multiple_of(p * KBLOCK, KBLOCK), KBLOCK)
            do2 = do_ref[pair, :]
            do_t = do2.T.astype(v_s.dtype)
            cs = range(nc)
            hs = range(2)
            do_c = [do2[r * QBLOCK:(r + 1) * QBLOCK, _head(h)].astype(v_s.dtype) for h, r in _CHAINS]
            qt_h = [qt_s[_head(h), pair] for h in hs]
            dot_h = [do_t[_head(h), :] for h in hs]

            def kept(kb):
                return pltpu.make_async_copy(ab_ref.at[g, p, kb], stage_s.at[kb & 1], sems.at[kb & 1])

            def k_block(kb, carry, diagonal):
                rk = pl.ds(pl.multiple_of(kb * KBLOCK, KBLOCK), KBLOCK)
                if diagonal:
                    valid = [_valid(p * Q_HALVES + r, kb) for r in range(Q_HALVES)]
                    keep = lambda c, t: jnp.where(valid[_CHAINS[c][1]], t, 0.0)
                else:
                    keep = lambda c, t: t
                    kept(kb + 1).start()
                kept(kb).wait()
                k_b = [k_s[h, rk, :] for h in hs]
                da = [_dot(do_c[c], v_s[_CHAINS[c][0], rk, :], _NT) for c in cs]
                a = [stage_s[kb & 1, 0, c] for c in cs]
                dla = [a[c].astype(F32) * da[c] for c in cs]
                for h in hs:
                    a_h = jnp.concatenate([a[c] for c in cs if _CHAINS[c][0] == h], axis=0)
                    dvt_s[_head(h), rk] += _dot(dot_h[h], a_h)
                d_lom = [carry[c][0] + _split_dot(dla[c], before_s[...]) for c in cs]
                beta = [stage_s[kb & 1, 1, c].astype(F32) for c in cs]
                dz_b = [(dla[c] * (1.0 - beta[c]) - keep(c, beta[c] * d_lom[c])).astype(v_s.dtype) for c in cs]
                dq_acc = [carry[c][1] + _dot(dz_b[c], k_b[_CHAINS[c][0]]) for c in cs]
                for h in hs:
                    dz_h = jnp.concatenate([dz_b[c] for c in cs if _CHAINS[c][0] == h], axis=0)
                    dkt_s[_head(h), rk] += _dot(qt_h[h], dz_h)
                return tuple((carry[c][0] + jnp.sum(dla[c], axis=1, keepdims=True), dq_acc[c]) for c in cs)

            init = (jnp.zeros((QBLOCK, 1), F32), jnp.zeros((QBLOCK, HEAD_DIM), F32))
            kept(0).start()
            before = lax.fori_loop(0, p, lambda kb, carry: k_block(kb, carry, False), (init,) * nc)
            res = k_block(p, before, True)
            for r in range(Q_HALVES):
                dq_ref[rows[r], :] = jnp.concatenate([res[c][1] for c in cs if _CHAINS[c][1] == r], axis=1)
            return 0

        lax.fori_loop(0, n_pairs, q_pair, 0)

        def transpose_out(i, _):
            r = pl.ds(pl.multiple_of(i * QBLOCK, QBLOCK), QBLOCK)
            dk_ref[r, :] = dkt_s[:, r].T
            dv_ref[r, :] = dvt_s[:, r].T.astype(dv_ref.dtype)
            return 0

        lax.fori_loop(0, L // QBLOCK, transpose_out, 0)

    spec = pl.BlockSpec((L, LANES), lambda b, p: (b, p))
    return pl.pallas_call(
        body, name="attn_bwd", grid=(B, n_hp),
        in_specs=[spec] * 4 + [_ANY], out_specs=[spec] * 3,
        out_shape=[jax.ShapeDtypeStruct((B * L, SB_WIDTH), F32)] * 2 + [jax.ShapeDtypeStruct((B * L, SB_WIDTH), BF16)],
        scratch_shapes=[pltpu.VMEM((2, L, HEAD_DIM), BF16)] * 2 + [pltpu.VMEM((LANES, L), BF16)]
        + [pltpu.VMEM((LANES, L), F32)] * 2 + [pltpu.VMEM((KBLOCK, KBLOCK), BF16)]
        + [pltpu.VMEM((2, 2, nc, QBLOCK, KBLOCK), BF16), pltpu.SemaphoreType.DMA((2,))],
        compiler_params=_params(("parallel", "parallel")),
    )(qn, kn, vb, d_sb, ab)


def _ssm_discretise(lam_re, lam_im, log_dt, b_re, b_im):
    dt = jnp.exp(log_dt)
    mag = jnp.exp(lam_re * dt)
    lbr = mag * jnp.cos(lam_im * dt)
    lbi = mag * jnp.sin(lam_im * dt)
    den = lam_re * lam_re + lam_im * lam_im
    nr, ni = lbr - 1.0, lbi
    cr = (nr * lam_re + ni * lam_im) / den
    ci = (ni * lam_re - nr * lam_im) / den
    return lbr, lbi, cr * b_re - ci * b_im, cr * b_im + ci * b_re


def _ssm_prep(lam_re, lam_im, log_dt, b_re_t, b_im_t):
    def body(lr, li, ld, br, bi, o_lr, o_li, o_br, o_bi):
        res = _ssm_discretise(lr[...], li[...], ld[...], br[...], bi[...])
        for o, v in zip((o_lr, o_li, o_br, o_bi), res):
            o[...] = v

    return pl.pallas_call(
        body, name="ssm_prep",
        out_shape=[jax.ShapeDtypeStruct(lam_re.shape, F32)] * 2 + [jax.ShapeDtypeStruct(b_re_t.shape, F32)] * 2,
    )(lam_re, lam_im, log_dt, b_re_t, b_im_t)


def _ssm_prep_bwd(lam_re, lam_im, log_dt, b_re_t, b_im_t, d_lr, d_li, d_br, d_bi):
    def body(lr, li, ld, br, bi, g_lr, g_li, g_br, g_bi, o_lr, o_li, o_ld, o_br, o_bi):
        _, vjp = jax.vjp(_ssm_discretise, lr[...], li[...], ld[...], br[...], bi[...])
        res = vjp((g_lr[...], g_li[...], g_br[...], g_bi[...]))
        for o, v in zip((o_lr, o_li, o_ld, o_br, o_bi), res):
            o[...] = v

    return pl.pallas_call(
        body, name="ssm_prep_bwd",
        out_shape=[jax.ShapeDtypeStruct(lam_re.shape, F32)] * 2 + [jax.ShapeDtypeStruct(log_dt.shape, F32)]
        + [jax.ShapeDtypeStruct(b_re_t.shape, F32)] * 2,
    )(lam_re, lam_im, log_dt, b_re_t, b_im_t, d_lr, d_li, d_br, d_bi)


def _block_diag(m):
    m4 = m.reshape(SSM_COLS, 8, SSM_GROUP, SSM_STATE)
    return jnp.einsum("aghp,gk->aghkp", m4, jnp.eye(8, dtype=m.dtype)).reshape(SSM_COLS, LANES, 512)


def _block_diag_take(d):
    d6 = d.reshape(SSM_COLS, 8, SSM_GROUP, 2, 8, SSM_STATE)
    return jnp.einsum("aghrgp->raghp", d6).reshape(2, SSM_GROUPS, SSM_GROUP, SSM_STATE)


def _cmul(ar, ai, br, bi):
    return ar * br - ai * bi, ar * bi + ai * br


def _power(lr, li, n):
    assert n & (n - 1) == 0
    for _ in range(n.bit_length() - 1):
        lr, li = _cmul(lr, li, lr, li)
    return lr, li


def _ssm_fwd(u_p, w_b, lam_r, lam_i, c_m, d_skip, B, L, tj):
    J = L // N_CHUNK
    njt = J // tj
    R = tj * N_CHUNK
    H = 512

    def body(u_ref, wb_ref, lr_ref, li_ref, cm_ref, d_ref, y_ref, gel_ref, x_ref, xin_ref, bu_s, st_s, xin_s):
        ph, jt = pl.program_id(2), pl.program_id(3)
        lr, li = lr_ref[...], li_ref[...]

        @pl.when(jnp.logical_and(ph == 0, jt == 0))
        def _():
            st_s[...] = jnp.zeros_like(st_s)

        @pl.when(ph == 0)
        def _():
            bu_s[jt] = _dot(u_ref[...].astype(BF16), wb_ref[...].astype(BF16))

        def scan(store):
            def step(j, carry):
                xr, xi = carry
                r = pl.ds(pl.multiple_of(j * N_CHUNK, N_CHUNK), N_CHUNK)
                nr = lr * xr - li * xi + bu_s[jt, r, 0:H]
                ni = lr * xi + li * xr + bu_s[jt, r, H:2 * H]
                if store:
                    x_ref[r, 0:H] = nr
                    x_ref[r, H:2 * H] = ni
                return nr, ni

            xr, xi = lax.fori_loop(0, tj, step, (st_s[:, 0:H], st_s[:, H:2 * H]))
            st_s[:, 0:H] = xr
            st_s[:, H:2 * H] = xi

        @pl.when(ph == 0)
        def _():
            scan(False)

            @pl.when(jt == njt - 1)
            def _():
                pr, pi = _power(lr[0:1], li[0:1], J)
                xin_s[0:1, :] = jnp.zeros((1, 2 * H), F32)
                for c in range(1, N_CHUNK):
                    qr, qi = _cmul(pr, pi, xin_s[c - 1:c, 0:H], xin_s[c - 1:c, H:2 * H])
                    xin_s[c:c + 1, 0:H] = qr + st_s[c - 1:c, 0:H]
                    xin_s[c:c + 1, H:2 * H] = qi + st_s[c - 1:c, H:2 * H]
                xin_ref[...] = xin_s[...]
                st_s[...] = xin_s[...]

        @pl.when(ph == 1)
        def _():
            scan(True)
            y = _dot(x_ref[...].astype(BF16), cm_ref[...].astype(BF16)) + d_ref[...] * u_ref[...]
            y_ref[...] = y
            gel_ref[...] = jax.nn.gelu(y).astype(gel_ref.dtype)

    return pl.pallas_call(
        body, name="ssm_fwd", grid=(SSM_COLS, B, 2, njt),
        in_specs=[
            pl.BlockSpec((None, R, LANES), lambda i, b, ph, jt: (b, jt, i)),
            pl.BlockSpec((None, LANES, 2 * H), lambda i, b, ph, jt: (i, 0, 0)),
            pl.BlockSpec((None, N_CHUNK, H), lambda i, b, ph, jt: (i, 0, 0)),
            pl.BlockSpec((None, N_CHUNK, H), lambda i, b, ph, jt: (i, 0, 0)),
            pl.BlockSpec((None, 2 * H, LANES), lambda i, b, ph, jt: (i, 0, 0)),
            pl.BlockSpec((1, LANES), lambda i, b, ph, jt: (0, i)),
        ],
        out_specs=[
            pl.BlockSpec((None, R, LANES), lambda i, b, ph, jt: (b, jt * ph, i)),
            pl.BlockSpec((None, R, LANES), lambda i, b, ph, jt: (b, jt * ph, i)),
            pl.BlockSpec((None, R, 2 * H), lambda i, b, ph, jt: (b, jt * ph, i)),
            pl.BlockSpec((None, None, N_CHUNK, 2 * H), lambda i, b, ph, jt: (b, i, 0, 0)),
        ],
        out_shape=[
            jax.ShapeDtypeStruct((B, L, SSM_WIDTH), F32),
            jax.ShapeDtypeStruct((B, L, SSM_WIDTH), BF16),
            jax.ShapeDtypeStruct((B, L, SSM_COLS * 2 * H), F32),
            jax.ShapeDtypeStruct((B, SSM_COLS, N_CHUNK, 2 * H), F32),
        ],
        scratch_shapes=[pltpu.VMEM((njt, R, 2 * H), F32), pltpu.VMEM((N_CHUNK, 2 * H), F32),
                        pltpu.VMEM((N_CHUNK, 2 * H), F32)],
        compiler_params=_params(("arbitrary",) * 4),
    )(u_p, w_b, lam_r, lam_i, c_m, d_skip)


def _ssm_bwd(dy_p, u_p, x, xin, w_bt, lam_r, lam_i, c_mt, d_skip, B, L, tj):
    J = L // N_CHUNK
    njt = J // tj
    R = tj * N_CHUNK
    H = 512
    x4 = x.reshape(B, J, N_CHUNK, SSM_COLS * 2 * H)

    def body(dy_ref, u_ref, x_ref, xp_ref, xin_ref, wbt_ref, lr_ref, li_ref, cmt_ref, d_ref,
             du_ref, dwb_ref, dcm_ref, dlr_ref, dli_ref, dd_ref, ca_s, a_s, st_s, dl_s):
        b, ph, jt = pl.program_id(1), pl.program_id(2), pl.program_id(3)
        jr = njt - 1 - jt
        lr, li = lr_ref[...], -li_ref[...]

        @pl.when(jnp.logical_and(b == 0, jnp.logical_and(ph == 0, jt == 0)))
        def _():
            dwb_ref[...] = jnp.zeros_like(dwb_ref)
            dcm_ref[...] = jnp.zeros_like(dcm_ref)
            dlr_ref[...] = jnp.zeros_like(dlr_ref)
            dli_ref[...] = jnp.zeros_like(dli_ref)
            dd_ref[...] = jnp.zeros_like(dd_ref)
            dl_s[...] = jnp.zeros_like(dl_s)

        @pl.when(jnp.logical_and(ph == 0, jt == 0))
        def _():
            st_s[...] = jnp.zeros_like(st_s)

        @pl.when(ph == 0)
        def _():
            ca_s[jt] = _dot(dy_ref[...].astype(BF16), cmt_ref[...].astype(BF16))

        def scan(store):
            def step(n, carry):
                ar, ai = carry
                r = pl.ds(pl.multiple_of((tj - 1 - n) * N_CHUNK, N_CHUNK), N_CHUNK)
                nr = lr * ar - li * ai + ca_s[jt, r, 0:H]
                ni = lr * ai + li * ar + ca_s[jt, r, H:2 * H]
                if store:
                    a_s[r, 0:H] = nr
                    a_s[r, H:2 * H] = ni
                return nr, ni

            ar, ai = lax.fori_loop(0, tj, step, (st_s[:, 0:H], st_s[:, H:2 * H]))
            st_s[:, 0:H] = ar
            st_s[:, H:2 * H] = ai

        @pl.when(ph == 0)
        def _():
            scan(False)

            @pl.when(jt == njt - 1)
            def _():
                pr, pi = _power(lr[0:1], li[0:1], J)
                a_s[N_CHUNK - 1:N_CHUNK, :] = jnp.zeros((1, 2 * H), F32)
                for c in range(N_CHUNK - 2, -1, -1):
                    qr, qi = _cmul(pr, pi, a_s[c + 1:c + 2, 0:H], a_s[c + 1:c + 2, H:2 * H])
                    a_s[c:c + 1, 0:H] = qr + st_s[c + 1:c + 2, 0:H]
                    a_s[c:c + 1, H:2 * H] = qi + st_s[c + 1:c + 2, H:2 * H]
                st_s[...] = a_s[0:N_CHUNK, :]

        @pl.when(ph == 1)
        def _():
            scan(True)
            dy = dy_ref[...]
            u = u_ref[...]
            a_b = a_s[...].astype(BF16)
            du_ref[...] = (_dot(a_b, wbt_ref[...].astype(BF16)) + d_ref[...] * dy).astype(du_ref.dtype)
            dwb_ref[...] += _dot(u.astype(BF16), a_b, _TN)
            dcm_ref[...] += _dot(x_ref[...].astype(BF16), dy.astype(BF16), _TN)
            dd_ref[...] += jnp.sum(dy * u, axis=0, keepdims=True)

            first = jnp.where(jr == 0, xin_ref[...], xp_ref[...])
            a0r, a0i = a_s[0:N_CHUNK, 0:H], a_s[0:N_CHUNK, H:2 * H]
            acc0 = (a0r * first[:, 0:H] + a0i * first[:, H:2 * H], a0i * first[:, 0:H] - a0r * first[:, H:2 * H])

            def step(j, carry):
                sr, si = carry
                r = pl.ds(pl.multiple_of(j * N_CHUNK, N_CHUNK), N_CHUNK)
                rp = pl.ds(pl.multiple_of((j - 1) * N_CHUNK, N_CHUNK), N_CHUNK)
                ar, ai = a_s[r, 0:H], a_s[r, H:2 * H]
                xr, xi = x_ref[rp, 0:H], x_ref[rp, H:2 * H]
                return sr + ar * xr + ai * xi, si + ai * xr - ar * xi

            sr, si = lax.fori_loop(1, tj, step, acc0)
            dl_s[:, 0:H] += sr
            dl_s[:, H:2 * H] += si

            @pl.when(jnp.logical_and(b == B - 1, jt == njt - 1))
            def _():
                dlr_ref[...] = jnp.sum(dl_s[:, 0:H], axis=0, keepdims=True)
                dli_ref[...] = jnp.sum(dl_s[:, H:2 * H], axis=0, keepdims=True)
                dl_s[...] = jnp.zeros_like(dl_s)

    rev = lambda ph, jt: (njt - 1 - jt) * ph + (njt - 1) * (1 - ph)
    return pl.pallas_call(
        body, name="ssm_bwd", grid=(SSM_COLS, B, 2, njt),
        in_specs=[
            pl.BlockSpec((None, R, LANES), lambda i, b, ph, jt: (b, njt - 1 - jt, i)),
            pl.BlockSpec((None, R, LANES), lambda i, b, ph, jt: (b, njt - 1 - jt, i)),
            pl.BlockSpec((None, R, 2 * H), lambda i, b, ph, jt: (b, rev(ph, jt), i)),
            pl.BlockSpec((None, None, N_CHUNK, 2 * H),
                         lambda i, b, ph, jt: (b, jnp.maximum((njt - 1 - jt) * tj - 1, 0), 0, i)),
            pl.BlockSpec((None, None, N_CHUNK, 2 * H), lambda i, b, ph, jt: (b, i, 0, 0)),
            pl.BlockSpec((None, 2 * H, LANES), lambda i, b, ph, jt: (i, 0, 0)),
            pl.BlockSpec((None, N_CHUNK, H), lambda i, b, ph, jt: (i, 0, 0)),
            pl.BlockSpec((None, N_CHUNK, H), lambda i, b, ph, jt: (i, 0, 0)),
            pl.BlockSpec((None, LANES, 2 * H), lambda i, b, ph, jt: (i, 0, 0)),
            pl.BlockSpec((1, LANES), lambda i, b, ph, jt: (0, i)),
        ],
        out_specs=[
            pl.BlockSpec((None, R, LANES), lambda i, b, ph, jt: (b, rev(ph, jt), i)),
            pl.BlockSpec((None, LANES, 2 * H), lambda i, b, ph, jt: (i, 0, 0)),
            pl.BlockSpec((None, 2 * H, LANES), lambda i, b, ph, jt: (i, 0, 0)),
            pl.BlockSpec((None, 1, H), lambda i, b, ph, jt: (i, 0, 0)),
            pl.BlockSpec((None, 1, H), lambda i, b, ph, jt: (i, 0, 0)),
            pl.BlockSpec((1, LANES), lambda i, b, ph, jt: (0, i)),
        ],
        out_shape=[
            jax.ShapeDtypeStruct((B, L, SSM_WIDTH), BF16),
            jax.ShapeDtypeStruct((SSM_COLS, LANES, 2 * H), F32),
            jax.ShapeDtypeStruct((SSM_COLS, 2 * H, LANES), F32),
            jax.ShapeDtypeStruct((SSM_COLS, 1, H), F32),
            jax.ShapeDtypeStruct((SSM_COLS, 1, H), F32),
            jax.ShapeDtypeStruct((1, SSM_WIDTH), F32),
        ],
        scratch_shapes=[pltpu.VMEM((njt, R, 2 * H), F32), pltpu.VMEM((R, 2 * H), F32),
                        pltpu.VMEM((N_CHUNK, 2 * H), F32), pltpu.VMEM((N_CHUNK, 2 * H), F32)],
        compiler_params=_params(("arbitrary",) * 4),
    )(dy_p, u_p, x, x4, xin, w_bt, lam_r, lam_i, c_mt, d_skip)


def _to_scan_layout(t, B, L):
    C = t.shape[-1]
    return t.reshape(B, N_CHUNK, L // N_CHUNK, C).transpose(0, 2, 1, 3).reshape(B, L, C)


def _from_scan_layout(t, B, L):
    C = t.shape[-1]
    return t.reshape(B, L // N_CHUNK, N_CHUNK, C).transpose(0, 2, 1, 3).reshape(B * L, C)


def _local_step(x, target, p, w_in, late_weights, mlp_grads_ready=None, rest_grads_ready=None, order=None, *,
                ssm_tile=128):
    B, L, D = x.shape
    T = B * L
    x2 = x.reshape(T, D)
    row = lambda v: v.reshape(1, -1)
    g1, g2, ga, gs, b_glu = row(p["norm1_g"]), row(p["norm2_g"]), row(p["attn_out_g"]), row(p["ssm_out_g"]), row(p["b_glu"])
    g1_first = g1 if order is None else g1 + order
    gq8 = jnp.tile(row(p["q_norm_g"]), (1, N_HEADS))
    gk8 = jnp.tile(row(p["k_norm_g"]), (1, N_HEADS))

    G, P, Hh = SSM_GROUPS, SSM_STATE, SSM_GROUP
    lam_re3, lam_im3 = p["ssm_lambda_re"].reshape(G, 1, P), p["ssm_lambda_im"].reshape(G, 1, P)
    log_dt3 = p["ssm_log_dt"].reshape(G, 1, 1)
    b_re_t, b_im_t = p["ssm_b_re"].transpose(0, 2, 1), p["ssm_b_im"].transpose(0, 2, 1)
    lbr, lbi, bbr, bbi = _ssm_prep(lam_re3, lam_im3, log_dt3, b_re_t, b_im_t)
    w_b = jnp.concatenate([_block_diag(bbr), _block_diag(bbi)], axis=2)
    c_mt = jnp.concatenate([_block_diag(p["ssm_c_re"]), -_block_diag(p["ssm_c_im"])], axis=2)
    w_bt, c_m = w_b.transpose(0, 2, 1), c_mt.transpose(0, 2, 1)
    lam_r = jnp.broadcast_to(lbr.reshape(SSM_COLS, 1, 512), (SSM_COLS, N_CHUNK, 512))
    lam_i = jnp.broadcast_to(lbi.reshape(SSM_COLS, 1, 512), (SSM_COLS, N_CHUNK, 512))
    d_skip = p["ssm_d"].reshape(1, SSM_WIDTH)

    xn = _rowwise("norm1", _rms, [x2], [g1_first], [(D, BF16)])
    proj = _matmul("proj", xn, w_in, tn=1024)
    qn, kn, vb = _qk_norm(proj, gq8, gk8)
    sb, attn_kept = _attn_fwd(qn, kn, vb, B, L)
    u_p = _to_scan_layout(proj[:, 3 * SB_WIDTH:], B, L)
    y_p, gel_p, xs, xin = _ssm_fwd(u_p, w_b, lam_r, lam_i, c_m, d_skip, B, L, ssm_tile)
    y2, gel = y_p.reshape(T, SSM_WIDTH), gel_p.reshape(T, SSM_WIDTH)
    w_glu, w_out, w_mlp_in, w_mlp_out = late_weights(gel)
    pre = _matmul("glu_gate", gel, w_glu)
    ssm_n = _rowwise("glu_out", _glu_branch, [y2, pre], [b_glu, gs], [(SSM_WIDTH, BF16)])
    sb_n = _rowwise("attn_out_norm", _rms, [sb], [ga], [(SB_WIDTH, BF16)])
    mixed = jnp.concatenate([sb_n, _from_scan_layout(ssm_n, B, L)], axis=1)
    def residual_and_norm(acc, res, g):
        h = acc + res
        return h, _rms(h, g)

    h1, hn = _matmul("out_proj", mixed, w_out, extras=[x2, g2], out_dtypes=(F32, BF16), tn=D,
                     epilogue=residual_and_norm)
    act, a_pre = _matmul("mlp_in", hn, w_mlp_in, out_dtypes=(BF16, BF16), tn=1024,
                         epilogue=lambda acc: (jnp.square(jnp.maximum(acc, 0.0)), acc))

    def loss_fn(acc, h, t):
        diff = acc + h - t
        part = jnp.sum(jnp.sum(diff * diff, axis=0, keepdims=True), axis=1, keepdims=True)
        d = diff * (1.0 / D)
        return d, d, part * (0.5 / D)

    d_out, d_out_b, loss = _matmul("mlp_out", act, w_mlp_out, extras=[h1, target.reshape(T, D)],
                                   out_dtypes=(F32, BF16), sums=[(1, 1)], epilogue=loss_fn, tm=512, tn=D)

    d_apre = _matmul("mlp_out_dx", d_out_b, w_mlp_out, tb=True, extras=[a_pre], out_dtypes=(BF16,), tn=1024,
                     epilogue=lambda acc, ap: (acc * (2.0 * jnp.maximum(ap.astype(F32), 0.0)),))
    both = lambda acc: (acc, acc)
    g_w_mlp_out, g_w_mlp_out_b = _matmul("mlp_out_dw", act, d_out_b, ta=True, out_dtypes=(F32, BF16), epilogue=both)
    g_w_mlp_in, g_w_mlp_in_b = _matmul("mlp_in_dw", hn, d_apre, ta=True, col_blocked=True, out_dtypes=(F32, BF16),
                                       epilogue=both, tn=w_mlp_in.shape[1] // N_DEV)
    if mlp_grads_ready is not None:
        g2 = g2 + mlp_grads_ready(g_w_mlp_out, g_w_mlp_out_b, g_w_mlp_in, g_w_mlp_in_b)

    def norm_bwd_res(dy, h, res, g):
        _, vjp = jax.vjp(_rms, h, g)
        dh, dg = vjp(dy)
        return res + dh, dg

    def norm_bwd_res2(dy, h, res, g):
        d, dg = norm_bwd_res(dy, h, res, g)
        return d, d, dg

    d_h1, d_h1_b, g_norm2 = _matmul("mlp_in_dx", d_apre, w_mlp_in, tb=True, extras=[h1, d_out, g2],
                                    out_dtypes=(F32, BF16), sums=[(1, D)], epilogue=norm_bwd_res2, tm=512, tn=D)

    d_mixed = _matmul("out_proj_dx", d_h1_b, w_out, tb=True, tn=1024)
    g_w_out, g_w_out_b = _matmul("out_proj_dw", mixed, d_h1_b, ta=True, out_dtypes=(F32, BF16), epilogue=both)

    def norm_bwd(h, dy, g):
        _, vjp = jax.vjp(_rms, h, g)
        return vjp(dy)

    d_sb, g_attn_out = _rowwise("attn_out_norm_bwd", norm_bwd, [sb, (d_mixed, 0, SB_WIDTH)], [ga],
                                [(SB_WIDTH, F32)], sums=[(1, SB_WIDTH)])
    d_ssm_n = _to_scan_layout(d_mixed[:, SB_WIDTH:], B, L).reshape(T, SSM_WIDTH)

    def glu_bwd(y, pre_, dy, bg, g):
        _, vjp = jax.vjp(_glu_branch, y, pre_, bg, g)
        d_y, d_pre, d_bg, d_g = vjp(dy)
        return d_y, d_pre, d_bg, d_g

    d_y_direct, d_pre, g_b_glu, g_ssm_out = _rowwise(
        "glu_out_bwd", glu_bwd, [y2, pre, d_ssm_n], [b_glu, gs], [(SSM_WIDTH, F32), (SSM_WIDTH, BF16)],
        sums=[(1, SSM_WIDTH), (1, SSM_WIDTH)])
    g_w_glu, g_w_glu_b = _matmul("glu_gate_dw", gel, d_pre, ta=True, out_dtypes=(F32, BF16), epilogue=both)

    def gelu_bwd(dg, y, dy0):
        _, vjp = jax.vjp(jax.nn.gelu, y)
        return (dy0 + vjp(dg)[0],)

    d_y = _matmul("glu_gate_dx", d_pre, w_glu, tb=True, extras=[y2, d_y_direct], epilogue=gelu_bwd)

    du_p, d_wb, d_cm, d_lr, d_li, g_d = _ssm_bwd(
        d_y.reshape(B, L, SSM_WIDTH), u_p, xs, xin, w_bt, lam_r, lam_i, c_mt, d_skip, B, L, ssm_tile)
    d_bb = _block_diag_take(d_wb.reshape(SSM_COLS, LANES, 2, 512))
    d_c = _block_diag_take(d_cm.transpose(0, 2, 1).reshape(SSM_COLS, LANES, 2, 512))
    g_lam_re, g_lam_im, g_log_dt, g_b_re_t, g_b_im_t = _ssm_prep_bwd(
        lam_re3, lam_im3, log_dt3, b_re_t, b_im_t,
        d_lr.reshape(G, 1, P), d_li.reshape(G, 1, P), d_bb[0], d_bb[1])
    d_qn, d_kn, d_v = _attn_bwd(qn, kn, vb, attn_kept, d_sb, B, L)
    d_q, d_k, g_q, g_k = _qk_norm_bwd(proj, gq8, gk8, d_qn, d_kn)

    d_proj = jnp.concatenate([d_q, d_k, d_v, _from_scan_layout(du_p, B, L)], axis=1)
    g_w_in, g_w_in_b = _matmul("proj_dw", xn, d_proj, ta=True, col_blocked=True, out_dtypes=(F32, BF16),
                               epilogue=both, tn=w_in.shape[1] // N_DEV)
    if rest_grads_ready is not None:
        g1 = g1 + rest_grads_ready([g_w_in, g_w_glu, g_w_out], [g_w_in_b, g_w_glu_b, g_w_out_b])
    grad_x, g_norm1 = _matmul("proj_dx", d_proj, w_in, tb=True, extras=[x2, d_h1, g1], sums=[(1, D)],
                              epilogue=norm_bwd_res, tm=512, tn=D)

    small = {
        "norm1_g": g_norm1.reshape(-1),
        "q_norm_g": g_q.reshape(-1),
        "k_norm_g": g_k.reshape(-1),
        "ssm_lambda_re": g_lam_re.reshape(G, P),
        "ssm_lambda_im": g_lam_im.reshape(G, P),
        "ssm_log_dt": g_log_dt.reshape(G),
        "ssm_b_re": g_b_re_t.transpose(0, 2, 1),
        "ssm_b_im": g_b_im_t.transpose(0, 2, 1),
        "ssm_c_re": d_c[0],
        "ssm_c_im": -d_c[1],
        "ssm_d": g_d.reshape(G, Hh),
        "b_glu": g_b_glu.reshape(-1),
        "attn_out_g": g_attn_out.reshape(-1),
        "ssm_out_g": g_ssm_out.reshape(-1),
        "norm2_g": g_norm2.reshape(-1),
    }
    big = {"w_in": g_w_in, "w_glu": g_w_glu, "w_out": g_w_out, "w_mlp_in": g_w_mlp_in, "w_mlp_out": g_w_mlp_out}
    return loss[0, 0], grad_x.reshape(B, L, D), small, big


_ANY = pl.BlockSpec(memory_space=pl.ANY)
_MESH = pl.DeviceIdType.MESH


def _all_gather(name, shards):
    n = len(shards)

    def body(*refs):
        in_refs, out_refs = refs[:n], refs[n:2 * n]
        send_sems, recv_sems, local_sems = refs[2 * n:]
        x, y, c = lax.axis_index("x"), lax.axis_index("y"), lax.axis_index("c")
        me, sibling = (x, y, c), (x, y, 1 - c)
        chips = [(1 - x, y), (x, 1 - y), (1 - x, 1 - y)]

        def copy(a, k, block, to, src=None):
            px, py, pc = block
            rows = out_refs[a].at[4 * px + 2 * py + pc]
            return pltpu.make_async_remote_copy(
                src_ref=rows if src is None else src, dst_ref=rows, send_sem=send_sems.at[a, k],
                recv_sem=recv_sems.at[a, k], device_id=to, device_id_type=_MESH)

        mine = [pltpu.make_async_copy(in_refs[a], out_refs[a].at[4 * x + 2 * y + c], local_sems.at[a]) for a in range(n)]
        first, passed = [], []
        for a in range(n):
            mine[a].start()
            first.append(copy(a, 0, me, sibling, src=in_refs[a]))
            first += [copy(a, 1 + j, me, (*chip, c), src=in_refs[a]) for j, chip in enumerate(chips)]
        for cp in first:
            cp.start()
        for j, chip in enumerate(chips):
            for a in range(n):
                copy(a, 1 + j, (*chip, c), me).wait_recv()
                fwd = copy(a, 4 + j, (*chip, c), sibling)
                fwd.start()
                passed.append(fwd)
        for a in range(n):
            copy(a, 0, sibling, me).wait_recv()
            for j, chip in enumerate(chips):
                copy(a, 4 + j, (*chip, 1 - c), me).wait_recv()
        for cp in first + passed:
            cp.wait_send()
        for cp in mine:
            cp.wait()

    return pl.pallas_call(
        body, name=name,
        in_specs=[_ANY] * n, out_specs=[_ANY] * n,
        out_shape=[jax.ShapeDtypeStruct((N_DEV, *s.shape), s.dtype) for s in shards],
        scratch_shapes=[pltpu.SemaphoreType.DMA((n, 7)), pltpu.SemaphoreType.DMA((n, 7)), pltpu.SemaphoreType.DMA((n,))],
    )(*shards)


_HBM = pl.BlockSpec(memory_space=pltpu.HBM)
_SEM = pl.BlockSpec(memory_space=pltpu.SEMAPHORE)
_EFFECT = pltpu.SideEffectType.DATAFLOW_SIDE_EFFECTING
_FLIPS = [(dx, dy, dc) for dx in (0, 1) for dy in (0, 1) for dc in (0, 1) if (dx, dy, dc) != (0, 0, 0)]


def _exchange_start(name, srcs, lands, per_peer):
    n = len(srcs)

    def body(*refs):
        src_refs, land_refs = refs[:n], refs[n:2 * n]
        send_sems, recv_sems = refs[2 * n:3 * n], refs[3 * n:4 * n]
        token = refs[-1]
        x, y, c = lax.axis_index("x"), lax.axis_index("y"), lax.axis_index("c")
        me = 4 * x + 2 * y + c
        for dx, dy, dc in _FLIPS:
            px, py, pc = (1 - x if dx else x), (1 - y if dy else y), (1 - c if dc else c)
            for a in range(n):
                pltpu.make_async_remote_copy(
                    src_ref=src_refs[a].at[4 * px + 2 * py + pc] if per_peer else src_refs[a],
                    dst_ref=land_refs[a].at[me], send_sem=send_sems[a], recv_sem=recv_sems[a],
                    device_id=(px, py, pc), device_id_type=_MESH).start()
        token[...] = jnp.zeros_like(token)

    hbm = lambda t: pltpu.with_memory_space_constraint(t, pltpu.HBM)
    res = pl.pallas_call(
        body, name=name,
        out_shape=(*[pltpu.SemaphoreType.DMA(())] * (2 * n), *[pltpu.HBM(t.shape, t.dtype) for t in (*srcs, *lands)],
                   jax.ShapeDtypeStruct((8, LANES), F32)),
        in_specs=[_HBM] * (2 * n),
        out_specs=(*[_SEM] * (2 * n), *[_HBM] * (2 * n), pl.BlockSpec(memory_space=pltpu.VMEM)),
        input_output_aliases={i: 2 * n + i for i in range(2 * n)},
        compiler_params=pltpu.CompilerParams(has_side_effects=_EFFECT),
    )(*[hbm(t) for t in (*srcs, *lands)])
    return res[:-1], res[-1]


def _exchange_wait(name, handle, after):
    n = len(handle) // 4
    sems, thru = handle[:2 * n], handle[2 * n:]

    def body(*refs):
        land_refs = refs[n:2 * n]
        send_sems, recv_sems = refs[2 * n:3 * n], refs[3 * n:4 * n]
        me = (lax.axis_index("x"), lax.axis_index("y"), lax.axis_index("c"))
        for a in range(n):
            seven = land_refs[a].at[pl.ds(0, len(_FLIPS))]
            all_copies = pltpu.make_async_remote_copy(
                src_ref=seven, dst_ref=seven, send_sem=send_sems[a], recv_sem=recv_sems[a], device_id=me,
                device_id_type=_MESH)
            all_copies.wait_send()
            all_copies.wait_recv()

    res = pl.pallas_call(
        body, name=name, out_shape=tuple(pltpu.HBM(t.shape, t.dtype) for t in thru),
        in_specs=[*[_HBM] * (2 * n), *[_SEM] * (2 * n), _ANY], out_specs=tuple([_HBM] * (2 * n)),
        input_output_aliases={i: i for i in range(2 * n)},
        compiler_params=pltpu.CompilerParams(has_side_effects=_EFFECT),
    )(*thru, *sems, after)
    return res[n:]


def _adamw_gathered(name, own, parts, me, w, m, v):
    r, c = w.shape
    tr = min(r, 256)

    def body(me_ref, own_ref, p_ref, w_ref, m_ref, v_ref, g_out, d_out, m_out, v_out):
        g = own_ref[...]
        for j in range(N_DEV):
            g = g + p_ref[j].astype(F32)
        delta, m_new, v_new = _adamw(w_ref[...], g, m_ref[...], v_ref[...])
        g_out[...] = g
        d_out[...] = delta
        m_out[...] = m_new
        v_out[...] = v_new

    spec = pl.BlockSpec((tr, c), lambda i, me_ref: (i, 0))
    return pl.pallas_call(
        body, name=name,
        grid_spec=pltpu.PrefetchScalarGridSpec(
            num_scalar_prefetch=1, grid=(r // tr,),
            in_specs=[pl.BlockSpec((None, tr, c), lambda i, me_ref: (me_ref[0], i, 0)),
                      pl.BlockSpec((N_DEV, tr, c), lambda i, me_ref: (0, i, 0)), spec, spec, spec],
            out_specs=[spec] * 4),
        out_shape=[jax.ShapeDtypeStruct((r, c), F32)] * 4,
        compiler_params=_params(("parallel",)),
    )(me, own, parts, w, m, v)


def _adamw(w, g, m, v):
    m = ADAM_B1 * m + (1.0 - ADAM_B1) * g
    v = ADAM_B2 * v + (1.0 - ADAM_B2) * jnp.square(g)
    m_hat = m / (1.0 - ADAM_B1 ** ADAM_STEP)
    v_hat = v / (1.0 - ADAM_B2 ** ADAM_STEP)
    delta = -ADAM_LR * (m_hat / (jnp.sqrt(v_hat) + ADAM_EPS) + ADAM_WD * w)
    return delta, m, v


def _adamw_small(name, parts, w, m, v):
    _, r, c = parts.shape
    tr = 8

    def body(p_ref, w_ref, m_ref, v_ref, g_out, d_out, m_out, v_out):
        g = p_ref[0]
        for j in range(1, N_DEV):
            g = g + p_ref[j]
        delta, m_new, v_new = _adamw(w_ref[...], g, m_ref[...], v_ref[...])
        g_out[...] = g
        d_out[...] = delta
        m_out[...] = m_new
        v_out[...] = v_new

    spec = pl.BlockSpec((tr, c), lambda i: (i, 0))
    return pl.pallas_call(
        body, name=name, grid=(r // tr,),
        in_specs=[pl.BlockSpec((N_DEV, tr, c), lambda i: (0, i, 0)), spec, spec, spec],
        out_specs=[spec] * 4, out_shape=[jax.ShapeDtypeStruct((r, c), F32)] * 4,
        compiler_params=_params(("parallel",)),
    )(parts, w, m, v)


_WEIGHTS = ["norm1_g", "w_in", "q_norm_g", "k_norm_g", "ssm_lambda_re", "ssm_lambda_im", "ssm_log_dt", "ssm_b_re",
            "ssm_b_im", "ssm_c_re", "ssm_c_im", "ssm_d", "w_glu", "b_glu", "attn_out_g", "ssm_out_g", "w_out",
            "norm2_g", "w_mlp_in", "w_mlp_out"]
_BIG = ["w_in", "w_glu", "w_out", "w_mlp_in", "w_mlp_out"]
_SMALL = [n for n in _WEIGHTS if n not in _BIG]
_PACK_COLS = 1024


def _pack(tree, last=None):
    flat = [tree[n].reshape(-1).astype(F32) for n in _SMALL]
    size = sum(f.shape[0] for f in flat)
    rows = -(-(size + 1) // (_PACK_COLS * 8)) * 8
    pad = jnp.zeros((rows * _PACK_COLS - size - 1,), F32)
    tail = jnp.zeros((1,), F32) if last is None else last.reshape(1).astype(F32)
    return jnp.concatenate(flat + [pad, tail]).reshape(rows, _PACK_COLS)


def _unpack(buf, like):
    flat, out, off = buf.reshape(-1), {}, 0
    for n in _SMALL:
        size = like[n].size
        out[n] = flat[off:off + size].reshape(like[n].shape)
        off += size
    return out


def kernel(x, norm1_g, w_in, q_norm_g, k_norm_g, ssm_lambda_re, ssm_lambda_im, ssm_log_dt, ssm_b_re, ssm_b_im, ssm_c_re, ssm_c_im, ssm_d, w_glu, b_glu, attn_out_g, ssm_out_g, w_out, norm2_g, w_mlp_in, w_mlp_out, loss_target, m_norm1_g, m_w_in, m_q_norm_g, m_k_norm_g, m_ssm_lambda_re, m_ssm_lambda_im, m_ssm_log_dt, m_ssm_b_re, m_ssm_b_im, m_ssm_c_re, m_ssm_c_im, m_ssm_d, m_w_glu, m_b_glu, m_attn_out_g, m_ssm_out_g, m_w_out, m_norm2_g, m_w_mlp_in, m_w_mlp_out, v_norm1_g, v_w_in, v_q_norm_g, v_k_norm_g, v_ssm_lambda_re, v_ssm_lambda_im, v_ssm_log_dt, v_ssm_b_re, v_ssm_b_im, v_ssm_c_re, v_ssm_c_im, v_ssm_d, v_w_glu, v_b_glu, v_attn_out_g, v_ssm_out_g, v_w_out, v_norm2_g, v_w_mlp_in, v_w_mlp_out):
    w = dict(zip(_WEIGHTS, (norm1_g, w_in, q_norm_g, k_norm_g, ssm_lambda_re, ssm_lambda_im, ssm_log_dt, ssm_b_re, ssm_b_im, ssm_c_re, ssm_c_im, ssm_d, w_glu, b_glu, attn_out_g, ssm_out_g, w_out, norm2_g, w_mlp_in, w_mlp_out)))
    m = dict(zip(_WEIGHTS, (m_norm1_g, m_w_in, m_q_norm_g, m_k_norm_g, m_ssm_lambda_re, m_ssm_lambda_im, m_ssm_log_dt, m_ssm_b_re, m_ssm_b_im, m_ssm_c_re, m_ssm_c_im, m_ssm_d, m_w_glu, m_b_glu, m_attn_out_g, m_ssm_out_g, m_w_out, m_norm2_g, m_w_mlp_in, m_w_mlp_out)))
    v = dict(zip(_WEIGHTS, (v_norm1_g, v_w_in, v_q_norm_g, v_k_norm_g, v_ssm_lambda_re, v_ssm_lambda_im, v_ssm_log_dt, v_ssm_b_re, v_ssm_b_im, v_ssm_c_re, v_ssm_c_im, v_ssm_d, v_w_glu, v_b_glu, v_attn_out_g, v_ssm_out_g, v_w_out, v_norm2_g, v_w_mlp_in, v_w_mlp_out)))
    core = lax.axis_index("c").astype(jnp.int32).reshape(1)
    chip = (2 * lax.axis_index("x") + lax.axis_index("y")).astype(jnp.int32).reshape(1)

    me = (2 * chip + core).astype(jnp.int32)

    def landing(own=None, like=None):
        own = jnp.zeros_like(like) if own is None else own
        return lax.dynamic_update_slice(lax.empty((N_DEV, *like.shape), like.dtype), own[None], (me[0], 0, 0))

    (w_in_blocks,) = _all_gather("w_in_all_gather", [w_in.astype(BF16)])
    w_in_full = w_in_blocks.transpose(1, 0, 2).reshape(w_in.shape[0], -1)
    late = [n for n in _BIG if n != "w_in"]
    shards = [w[n].astype(BF16) for n in late]
    w_in_blocks, shards = lax.optimization_barrier((w_in_blocks, shards))
    weights_handle, weights_token = _exchange_start(
        "weights_send", shards, [landing(s, s) for s in shards], per_peer=False)

    def late_weights(after):
        got = dict(zip(late, _exchange_wait("weights_arrive", weights_handle, after)))
        return (got["w_glu"].reshape(-1, w_glu.shape[1]), got["w_out"].reshape(-1, w_out.shape[1]),
                got["w_mlp_in"].transpose(1, 0, 2).reshape(w_mlp_in.shape[0], -1),
                got["w_mlp_out"].reshape(-1, w_mlp_out.shape[1]))

    mlp = ["w_mlp_out", "w_mlp_in"]
    sent = {}

    def send_grads(name, names, own, own_b):
        blocks = lambda g, n: g.reshape(N_DEV, *w[n].shape)
        sent[name + "_own"] = [blocks(g, n) for g, n in zip(own, names)]
        srcs = [blocks(g, n) for g, n in zip(own_b, names)]
        sent[name], token = _exchange_start(name, srcs, [landing(like=s[0]) for s in srcs], per_peer=True)
        return token[0, 0]

    def mlp_grads_ready(g_out, g_out_b, g_in, g_in_b):
        return send_grads("mlp_grads_send", mlp, [g_out, g_in], [g_out_b, g_in_b])

    rest = ["w_in", "w_glu", "w_out"]

    def rest_grads_ready(own, own_b):
        return send_grads("rest_grads_send", rest, own, own_b)

    loss_local, grad_x, g_small, g_big = _local_step(
        x, loss_target, {n: w[n] for n in _SMALL}, w_in_full, late_weights, mlp_grads_ready, rest_grads_ready,
        weights_token[0, 0])

    grads, delta, new_m, new_v = {}, {}, {}, {}
    small = _pack(g_small, last=loss_local)
    small_handle, small_token = _exchange_start("small_grads_send", [small], [landing(small, small)], per_peer=False)

    for send, arrive, names in (("mlp_grads_send", "mlp_grads_arrive", mlp),
                                ("rest_grads_send", "rest_grads_arrive", rest)):
        for n, own, part in zip(names, sent[send + "_own"], _exchange_wait(arrive, sent[send], small_token)):
            grads[n], delta[n], new_m[n], new_v[n] = _adamw_gathered("adamw_" + n, own, part, me, w[n], m[n], v[n])

    shards_done = lax.optimization_barrier(tuple(new_v[n] for n in _BIG))
    (small_parts,) = _exchange_wait("small_grads_arrive", small_handle, shards_done[-1])
    packed = _adamw_small("adamw_small", small_parts, _pack(w), _pack(m), _pack(v))
    for tree, buf in zip((grads, delta, new_m, new_v), packed):
        tree.update(_unpack(buf, w))
    loss = packed[0][-1, -1]

    return (loss, grad_x, *[grads[n] for n in _WEIGHTS], *[delta[n] for n in _WEIGHTS],
            *[new_m[n] for n in _WEIGHTS], *[new_v[n] for n in _WEIGHTS])
```

```python
import functools
import math

import jax
import jax.numpy as jnp
from jax import lax
from jax.experimental import pallas as pl
from jax.experimental.pallas import tpu as pltpu

F32 = jnp.float32
BF16 = jnp.bfloat16

EPS = 1e-6
HEAD_DIM = 64
N_HEADS = 8
SB_WIDTH = 512
SSM_WIDTH = 512
SSM_GROUP = 16
SSM_GROUPS = 32
SSM_STATE = 64
QBLOCK = 128
KBLOCK = 256
N_CHUNK = 8
SSM_COLS = 4
LANES = 128
N_DEV = 8

ADAM_LR = 0.001
ADAM_B1 = 0.9
ADAM_B2 = 0.999
ADAM_EPS = 1e-08
ADAM_WD = 0.01
ADAM_STEP = 10

VMEM_LIMIT = 56 * 1024 * 1024

_NT = (((1,), (1,)), ((), ()))
_NN = (((1,), (0,)), ((), ()))
_TN = (((0,), (0,)), ((), ()))


def _dot(a, b, dims=_NN):
    return lax.dot_general(a, b, dims, preferred_element_type=F32)


def _params(sem):
    return pltpu.CompilerParams(dimension_semantics=sem, vmem_limit_bytes=VMEM_LIMIT)


def _matmul(name, a, b, *, ta=False, tb=False, extras=(), epilogue=None, out_dtypes=(F32,), sums=(),
            col_blocked=False, tm=1024, tn=512, tk=4096):
    M, K = (a.shape[1], a.shape[0]) if ta else a.shape
    N = b.shape[0] if tb else b.shape[1]
    tm, tn, tk = min(tm, M), min(tn, N), min(tk, K)
    assert M % tm == 0 and N % tn == 0 and K % tk == 0, (name, M, N, K)
    assert not sums or (tn == N and tk == K), name
    nk = K // tk
    n_ex, n_out, n_sum = len(extras), len(out_dtypes), len(sums)
    dims = (((0 if ta else 1,), (1 if tb else 0,)), ((), ()))

    def body(*refs):
        a_ref, b_ref = refs[0], refs[1]
        ex_refs = refs[2:2 + n_ex]
        o_refs = refs[2 + n_ex:2 + n_ex + n_out]
        s_refs = refs[2 + n_ex + n_out:2 + n_ex + n_out + n_sum]
        k = pl.program_id(2)
        part = _dot(a_ref[...].astype(BF16), b_ref[...].astype(BF16), dims)

        def finish(acc):
            outs = (acc,) if epilogue is None else epilogue(acc, *[e[...] for e in ex_refs])
            for o_ref, o in zip(o_refs, outs[:n_out]):
                o_ref[...] = o.astype(o_ref.dtype)
            if n_sum:
                @pl.when(pl.program_id(0) == 0)
                def _():
                    for s_ref in s_refs:
                        s_ref[...] = jnp.zeros_like(s_ref)

                for s_ref, v in zip(s_refs, outs[n_out:]):
                    s_ref[...] += v

        if nk == 1:
            finish(part)
        else:
            acc_ref = refs[-1]

            @pl.when(k == 0)
            def _():
                acc_ref[...] = part

            @pl.when(jnp.logical_and(k > 0, k < nk - 1))
            def _():
                acc_ref[...] += part

            @pl.when(k == nk - 1)
            def _():
                finish(acc_ref[...] + part)

    a_spec = pl.BlockSpec((tk, tm), lambda i, j, k: (k, i)) if ta else pl.BlockSpec((tm, tk), lambda i, j, k: (i, k))
    b_spec = pl.BlockSpec((tn, tk), lambda i, j, k: (j, k)) if tb else pl.BlockSpec((tk, tn), lambda i, j, k: (k, j))
    ex_specs = [pl.BlockSpec((1, tn), lambda i, j, k: (0, j)) if e.shape[0] == 1 else
                pl.BlockSpec((tm, tn), lambda i, j, k: (i, j)) for e in extras]
    if col_blocked:
        out_specs = [pl.BlockSpec((None, tm, tn), lambda i, j, k: (j, i, 0)) for _ in out_dtypes]
        out_shape = [jax.ShapeDtypeStruct((N // tn, M, tn), dt) for dt in out_dtypes]
    else:
        out_specs = [pl.BlockSpec((tm, tn), lambda i, j, k: (i, j)) for _ in out_dtypes]
        out_shape = [jax.ShapeDtypeStruct((M, N), dt) for dt in out_dtypes]
    out_specs += [pl.BlockSpec(s, lambda i, j, k: (0, 0)) for s in sums]
    out_shape += [jax.ShapeDtypeStruct(s, F32) for s in sums]
    outs = pl.pallas_call(
        body, name=name, grid=(M // tm, N // tn, nk),
        in_specs=[a_spec, b_spec, *ex_specs], out_specs=out_specs, out_shape=out_shape,
        scratch_shapes=[pltpu.VMEM((tm, tn), F32)] if nk > 1 else [],
        compiler_params=_params(("arbitrary",) * 3 if sums else ("parallel", "parallel", "arbitrary")),
    )(a, b, *extras)
    return outs[0] if len(outs) == 1 else outs


def _rowwise(name, fn, rows, small, outs, sums=(), tile=256):
    specs, args = [], []
    T = None
    for r in rows:
        arr, cb, w = r if isinstance(r, tuple) else (r, 0, r.shape[1])
        T = arr.shape[0]
        specs.append((w, cb))
        args.append(arr)
    tile = min(tile, T)
    assert T % tile == 0
    n_r, n_s, n_o, n_a = len(rows), len(small), len(outs), len(sums)

    def body(*refs):
        r_refs = refs[:n_r]
        s_refs = refs[n_r:n_r + n_s]
        o_refs = refs[n_r + n_s:n_r + n_s + n_o]
        a_refs = refs[n_r + n_s + n_o:]
        res = fn(*[r[...] for r in r_refs], *[s[...] for s in s_refs])
        res = res if isinstance(res, (tuple, list)) else (res,)
        for o_ref, o in zip(o_refs, res[:n_o]):
            o_ref[...] = o.astype(o_ref.dtype)

        @pl.when(pl.program_id(0) == 0)
        def _():
            for a_ref in a_refs:
                a_ref[...] = jnp.zeros_like(a_ref)

        for a_ref, v in zip(a_refs, res[n_o:]):
            a_ref[...] += v.astype(F32)

    in_specs = [pl.BlockSpec((tile, w), functools.partial(lambda i, cb: (i, cb), cb=cb)) for w, cb in specs]
    in_specs += [pl.BlockSpec(s.shape, functools.partial(lambda i, nd: (0,) * nd, nd=s.ndim)) for s in small]
    out_specs = [pl.BlockSpec((tile, w), lambda i: (i, 0)) for w, _ in outs]
    out_specs += [pl.BlockSpec(s, functools.partial(lambda i, nd: (0,) * nd, nd=len(s))) for s in sums]
    out_shape = [jax.ShapeDtypeStruct((T, w), dt) for w, dt in outs]
    out_shape += [jax.ShapeDtypeStruct(s, F32) for s in sums]
    res = pl.pallas_call(
        body, name=name, grid=(T // tile,), in_specs=in_specs, out_specs=out_specs, out_shape=out_shape,
        compiler_params=_params(("arbitrary",)),
    )(*args, *small)
    return res[0] if len(res) == 1 else res


def _rms(x, g):
    return x * lax.rsqrt(jnp.mean(x * x, axis=-1, keepdims=True) + EPS) * g


def _glu_branch(y, pre, b_glu, g_out):
    g = jax.nn.gelu(y)
    return _rms(g * jax.nn.sigmoid(pre + b_glu), g_out)


def _split_dot(x, tri_bf):
    hi = x.astype(BF16)
    lo = (x - hi.astype(F32)).astype(BF16)
    return _dot(hi, tri_bf) + _dot(lo, tri_bf)


def _softplus(z):
    return jnp.maximum(z, 0.0) + jnp.log(1.0 + jnp.exp(-jnp.abs(z)))


def _head(h):
    return slice(h * HEAD_DIM, (h + 1) * HEAD_DIM)


def _head_mean(x, seg):
    return _split_dot(x, seg) * (1.0 / HEAD_DIM)


def _qk_norm(proj, gq, gk):
    scale = 1.0 / math.sqrt(HEAD_DIM)
    idx = jnp.arange(SB_WIDTH) // HEAD_DIM
    seg = (idx[:, None] == idx[None, :]).astype(BF16)

    def fn(q, k, v, gq_, gk_, seg_):
        qn = q * lax.rsqrt(_head_mean(q * q, seg_) + EPS) * (gq_ * scale)
        kn = k * lax.rsqrt(_head_mean(k * k, seg_) + EPS) * gk_
        return qn, kn, v

    return _rowwise("qk_norm", fn, [(proj, 0, SB_WIDTH), (proj, 1, SB_WIDTH), (proj, 2, SB_WIDTH)], [gq, gk, seg],
                    [(SB_WIDTH, BF16)] * 3)


def _qk_norm_bwd(proj, gq, gk, d_qn, d_kn):
    scale = 1.0 / math.sqrt(HEAD_DIM)
    idx = jnp.arange(SB_WIDTH) // HEAD_DIM
    seg = (idx[:, None] == idx[None, :]).astype(BF16)

    def one(x, g, dy, seg_):
        r = lax.rsqrt(_head_mean(x * x, seg_) + EPS)
        gdy = g * dy
        dx = r * gdy - x * (r * r * r) * _head_mean(gdy * x, seg_)
        dg = jnp.sum(dy * x * r, axis=0, keepdims=True)
        return dx, sum(dg[:, _head(h)] for h in range(N_HEADS))

    def fn(q, k, dqn, dkn, gq_, gk_, seg_):
        dq, dgq = one(q, gq_, dqn * scale, seg_)
        dk, dgk = one(k, gk_, dkn, seg_)
        return dq, dk, dgq, dgk

    return _rowwise("qk_norm_bwd", fn, [(proj, 0, SB_WIDTH), (proj, 1, SB_WIDTH), d_qn, d_kn], [gq, gk, seg],
                    [(SB_WIDTH, BF16)] * 2, sums=[(1, HEAD_DIM)] * 2)


def _split_heads(refs, scratch, L):
    def chunk(i, _):
        r = pl.ds(pl.multiple_of(i * QBLOCK, QBLOCK), QBLOCK)
        for ref, s in zip(refs, scratch):
            for h in range(2):
                s[h, r, :] = ref[r, _head(h)]
        return 0

    lax.fori_loop(0, L // QBLOCK, chunk, 0)


class _Copies:
    def __init__(self, copies):
        self.copies = copies

    def start(self):
        for cp in self.copies:
            cp.start()

    def wait(self):
        for cp in self.copies:
            cp.wait()


Q_HALVES = KBLOCK // QBLOCK
_CHAINS = [(h, r) for h in range(2) for r in range(Q_HALVES)]


def _valid(i, kb):
    row = lax.broadcasted_iota(jnp.int32, (QBLOCK, KBLOCK), 0)
    col = lax.broadcasted_iota(jnp.int32, (QBLOCK, KBLOCK), 1)
    return col + (kb * KBLOCK - i * QBLOCK) < row


def _attn_fwd(qn, kn, vb, B, L):
    n_pairs = L // KBLOCK
    n_hp = N_HEADS // 2
    nc = len(_CHAINS)

    def body(q_ref, k_ref, v_ref, o_ref, ab_ref, q_s, k_s, v_s, after_s, z_s, stage_s, sems):
        _split_heads((q_ref, k_ref, v_ref), (q_s, k_s, v_s), L)
        r2 = lax.broadcasted_iota(jnp.int32, (KBLOCK, KBLOCK), 0)
        c2 = lax.broadcasted_iota(jnp.int32, (KBLOCK, KBLOCK), 1)
        after_s[...] = (r2 > c2).astype(after_s.dtype)
        g = pl.program_id(0) * n_hp + pl.program_id(1)

        def q_pair(p, _):
            rows = [pl.ds(pl.multiple_of((p * Q_HALVES + r) * QBLOCK, QBLOCK), QBLOCK) for r in range(Q_HALVES)]
            q_c = [q_s[h, rows[r], :] for h, r in _CHAINS]
            cs = range(nc)

            def scores(kb):
                rk = pl.ds(pl.multiple_of(kb * KBLOCK, KBLOCK), KBLOCK)
                return [_dot(q_c[c], k_s[_CHAINS[c][0], rk, :], _NT) for c in cs]

            def saved(kb):
                return _Copies([pltpu.make_async_copy(stage_s.at[kb & 1, c], ab_ref.at[g, p, kb, c], sems.at[kb & 1, c])
                                for c in cs])

            def k_block(kb, carry, diagonal):
                rk = pl.ds(pl.multiple_of(kb * KBLOCK, KBLOCK), KBLOCK)
                if diagonal:
                    valid = [_valid(p * Q_HALVES + r, kb) for r in range(Q_HALVES)]
                    keep = lambda c, t: jnp.where(valid[_CHAINS[c][1]], t, 0.0)
                    z = scores(kb)
                else:
                    keep = lambda c, t: t
                    z = [z_s[(kb + 1) & 1, c] for c in cs]
                ahead = scores(jnp.maximum(kb - 1, 0))
                for c in cs:
                    z_s[kb & 1, c] = ahead[c]
                sp = [_softplus(z[c]) for c in cs]
                lsig = [z[c] - sp[c] for c in cs]
                lom = [keep(c, -sp[c]) for c in cs]
                tail = [_split_dot(lom[c], after_s[...]) + carry[c][0] for c in cs]
                a = [keep(c, jnp.exp(lsig[c] + tail[c])).astype(v_s.dtype) for c in cs]
                acc = [carry[c][1] + _dot(a[c], v_s[_CHAINS[c][0], rk, :]) for c in cs]
                for c in cs:
                    stage_s[kb & 1, c, 0] = a[c]
                    stage_s[kb & 1, c, 1] = jnp.exp(lsig[c]).astype(stage_s.dtype)
                saved(kb).start()
                return tuple((carry[c][0] + jnp.sum(lom[c], axis=1, keepdims=True), acc[c]) for c in cs)

            def next_block(n, carry):
                kb = p - n

                @pl.when(n >= 2)
                def _():
                    saved(kb + 2).wait()

                return k_block(kb, carry, False)

            init = (jnp.zeros((QBLOCK, 1), F32), jnp.zeros((QBLOCK, HEAD_DIM), F32))
            first = k_block(p, (init,) * nc, True)
            res = lax.fori_loop(1, p + 1, next_block, first)
            saved(0).wait()

            @pl.when(p >= 1)
            def _():
                saved(1).wait()

            for r in range(Q_HALVES):
                o_ref[rows[r], :] = jnp.concatenate([res[c][1] for c in cs if _CHAINS[c][1] == r], axis=1)
            return 0

        lax.fori_loop(0, n_pairs, q_pair, 0)

    spec = pl.BlockSpec((L, LANES), lambda b, p: (b, p))
    return pl.pallas_call(
        body, name="attn_fwd", grid=(B, n_hp),
        in_specs=[spec] * 3, out_specs=[spec, _ANY],
        out_shape=[jax.ShapeDtypeStruct((B * L, SB_WIDTH), F32),
                   jax.ShapeDtypeStruct((B * n_hp, n_pairs, n_pairs, nc, 2, QBLOCK, KBLOCK), BF16)],
        scratch_shapes=[pltpu.VMEM((2, L, HEAD_DIM), BF16)] * 3 + [pltpu.VMEM((KBLOCK, KBLOCK), BF16)]
        + [pltpu.VMEM((2, nc, QBLOCK, KBLOCK), F32), pltpu.VMEM((2, nc, 2, QBLOCK, KBLOCK), BF16),
           pltpu.SemaphoreType.DMA((2, nc))],
        compiler_params=_params(("parallel", "parallel")),
    )(qn, kn, vb)


def _attn_bwd(qn, kn, vb, ab, d_sb, B, L):
    n_pairs = L // KBLOCK
    n_hp = N_HEADS // 2
    nc = len(_CHAINS)

    def body(q_ref, k_ref, v_ref, do_ref, ab_ref, dq_ref, dk_ref, dv_ref,
             k_s, v_s, qt_s, dkt_s, dvt_s, before_s, stage_s, sems):
        _split_heads((k_ref, v_ref), (k_s, v_s), L)
        g = pl.program_id(0) * n_hp + pl.program_id(1)

        def transpose_q(i, _):
            r = pl.ds(pl.multiple_of(i * QBLOCK, QBLOCK), QBLOCK)
            qt_s[:, r] = q_ref[r, :].astype(F32).T.astype(qt_s.dtype)
            return 0

        lax.fori_loop(0, L // QBLOCK, transpose_q, 0)
        dkt_s[...] = jnp.zeros_like(dkt_s)
        dvt_s[...] = jnp.zeros_like(dvt_s)
        r2 = lax.broadcasted_iota(jnp.int32, (KBLOCK, KBLOCK), 0)
        c2 = lax.broadcasted_iota(jnp.int32, (KBLOCK, KBLOCK), 1)
        before_s[...] = (r2 < c2).astype(before_s.dtype)

        def q_pair(p, _):
            rows = [pl.ds(pl.multiple_of((p * Q_HALVES + r) * QBLOCK, QBLOCK), QBLOCK) for r in range(Q_HALVES)]
            pair = pl.ds(pl.multiple_of(p * KBLOCK, KBLOCK), KBLOCK)
            do2 = do_ref[pair, :]
            do_t = do2.T.astype(v_s.dtype)
            cs = range(nc)
            hs = range(2)
            do_c = [do2[r * QBLOCK:(r + 1) * QBLOCK, _head(h)].astype(v_s.dtype) for h, r in _CHAINS]
            qt_h = [qt_s[_head(h), pair] for h in hs]
            dot_h = [do_t[_head(h), :] for h in hs]

            def kept(kb):
                return _Copies([pltpu.make_async_copy(ab_ref.at[g, p, kb, c], stage_s.at[kb & 1, c], sems.at[kb & 1, c])
                                for c in cs])

            def k_block(kb, carry, diagonal):
                rk = pl.ds(pl.multiple_of(kb * KBLOCK, KBLOCK), KBLOCK)
                if diagonal:
                    valid = [_valid(p * Q_HALVES + r, kb) for r in range(Q_HALVES)]
                    keep = lambda c, t: jnp.where(valid[_CHAINS[c][1]], t, 0.0)
                else:
                    keep = lambda c, t: t
                    kept(kb + 1).start()
                kept(kb).wait()
                k_b = [k_s[h, rk, :] for h in hs]
                da = [_dot(do_c[c], v_s[_CHAINS[c][0], rk, :], _NT) for c in cs]
                a = [stage_s[kb & 1, c, 0] for c in cs]
                dla = [a[c].astype(F32) * da[c] for c in cs]
                for h in hs:
                    a_h = jnp.concatenate([a[c] for c in cs if _CHAINS[c][0] == h], axis=0)
                    dvt_s[_head(h), rk] += _dot(dot_h[h], a_h)
                d_lom = [carry[c][0] + _split_dot(dla[c], before_s[...]) for c in cs]
                beta = [stage_s[kb & 1, c, 1].astype(F32) for c in cs]
                dz_b = [(dla[c] * (1.0 - beta[c]) - keep(c, beta[c] * d_lom[c])).astype(v_s.dtype) for c in cs]
                dq_acc = [carry[c][1] + _dot(dz_b[c], k_b[_CHAINS[c][0]]) for c in cs]
                for h in hs:
                    dz_h = jnp.concatenate([dz_b[c] for c in cs if _CHAINS[c][0] == h], axis=0)
                    dkt_s[_head(h), rk] += _dot(qt_h[h], dz_h)
                return tuple((carry[c][0] + jnp.sum(dla[c], axis=1, keepdims=True), dq_acc[c]) for c in cs)

            init = (jnp.zeros((QBLOCK, 1), F32), jnp.zeros((QBLOCK, HEAD_DIM), F32))
            kept(0).start()
            before = lax.fori_loop(0, p, lambda kb, carry: k_block(kb, carry, False), (init,) * nc)
            res = k_block(p, before, True)
            for r in range(Q_HALVES):
                dq_ref[rows[r], :] = jnp.concatenate([res[c][1] for c in cs if _CHAINS[c][1] == r], axis=1)
            return 0

        lax.fori_loop(0, n_pairs, q_pair, 0)

        def transpose_out(i, _):
            r = pl.ds(pl.multiple_of(i * QBLOCK, QBLOCK), QBLOCK)
            dk_ref[r, :] = dkt_s[:, r].T
            dv_ref[r, :] = dvt_s[:, r].T.astype(dv_ref.dtype)
            return 0

        lax.fori_loop(0, L // QBLOCK, transpose_out, 0)

    spec = pl.BlockSpec((L, LANES), lambda b, p: (b, p))
    return pl.pallas_call(
        body, name="attn_bwd", grid=(B, n_hp),
        in_specs=[spec] * 4 + [_ANY], out_specs=[spec] * 3,
        out_shape=[jax.ShapeDtypeStruct((B * L, SB_WIDTH), F32)] * 2 + [jax.ShapeDtypeStruct((B * L, SB_WIDTH), BF16)],
        scratch_shapes=[pltpu.VMEM((2, L, HEAD_DIM), BF16)] * 2 + [pltpu.VMEM((LANES, L), BF16)]
        + [pltpu.VMEM((LANES, L), F32)] * 2 + [pltpu.VMEM((KBLOCK, KBLOCK), BF16)]
        + [pltpu.VMEM((2, nc, 2, QBLOCK, KBLOCK), BF16), pltpu.SemaphoreType.DMA((2, nc))],
        compiler_params=_params(("parallel", "parallel")),
    )(qn, kn, vb, d_sb, ab)


def _ssm_discretise(lam_re, lam_im, log_dt, b_re, b_im):
    dt = jnp.exp(log_dt)
    mag = jnp.exp(lam_re * dt)
    lbr = mag * jnp.cos(lam_im * dt)
    lbi = mag * jnp.sin(lam_im * dt)
    den = lam_re * lam_re + lam_im * lam_im
    nr, ni = lbr - 1.0, lbi
    cr = (nr * lam_re + ni * lam_im) / den
    ci = (ni * lam_re - nr * lam_im) / den
    return lbr, lbi, cr * b_re - ci * b_im, cr * b_im + ci * b_re


def _ssm_prep(lam_re, lam_im, log_dt, b_re_t, b_im_t):
    def body(lr, li, ld, br, bi, o_lr, o_li, o_br, o_bi):
        res = _ssm_discretise(lr[...], li[...], ld[...], br[...], bi[...])
        for o, v in zip((o_lr, o_li, o_br, o_bi), res):
            o[...] = v

    return pl.pallas_call(
        body, name="ssm_prep",
        out_shape=[jax.ShapeDtypeStruct(lam_re.shape, F32)] * 2 + [jax.ShapeDtypeStruct(b_re_t.shape, F32)] * 2,
    )(lam_re, lam_im, log_dt, b_re_t, b_im_t)


def _ssm_prep_bwd(lam_re, lam_im, log_dt, b_re_t, b_im_t, d_lr, d_li, d_br, d_bi):
    def body(lr, li, ld, br, bi, g_lr, g_li, g_br, g_bi, o_lr, o_li, o_ld, o_br, o_bi):
        _, vjp = jax.vjp(_ssm_discretise, lr[...], li[...], ld[...], br[...], bi[...])
        res = vjp((g_lr[...], g_li[...], g_br[...], g_bi[...]))
        for o, v in zip((o_lr, o_li, o_ld, o_br, o_bi), res):
            o[...] = v

    return pl.pallas_call(
        body, name="ssm_prep_bwd",
        out_shape=[jax.ShapeDtypeStruct(lam_re.shape, F32)] * 2 + [jax.ShapeDtypeStruct(log_dt.shape, F32)]
        + [jax.ShapeDtypeStruct(b_re_t.shape, F32)] * 2,
    )(lam_re, lam_im, log_dt, b_re_t, b_im_t, d_lr, d_li, d_br, d_bi)


def _block_diag(m):
    m4 = m.reshape(SSM_COLS, 8, SSM_GROUP, SSM_STATE)
    return jnp.einsum("aghp,gk->aghkp", m4, jnp.eye(8, dtype=m.dtype)).reshape(SSM_COLS, LANES, 512)


def _block_diag_take(d):
    d6 = d.reshape(SSM_COLS, 8, SSM_GROUP, 2, 8, SSM_STATE)
    return jnp.einsum("aghrgp->raghp", d6).reshape(2, SSM_GROUPS, SSM_GROUP, SSM_STATE)


def _cmul(ar, ai, br, bi):
    return ar * br - ai * bi, ar * bi + ai * br


def _power(lr, li, n):
    assert n & (n - 1) == 0
    for _ in range(n.bit_length() - 1):
        lr, li = _cmul(lr, li, lr, li)
    return lr, li


def _ssm_fwd(u_p, w_b, lam_r, lam_i, c_m, d_skip, B, L, tj):
    J = L // N_CHUNK
    njt = J // tj
    R = tj * N_CHUNK
    H = 512

    def body(u_ref, wb_ref, lr_ref, li_ref, cm_ref, d_ref, y_ref, gel_ref, x_ref, xin_ref, bu_s, st_s, xin_s):
        ph, jt = pl.program_id(2), pl.program_id(3)
        lr, li = lr_ref[...], li_ref[...]

        @pl.when(jnp.logical_and(ph == 0, jt == 0))
        def _():
            st_s[...] = jnp.zeros_like(st_s)

        @pl.when(ph == 0)
        def _():
            bu_s[jt] = _dot(u_ref[...].astype(BF16), wb_ref[...].astype(BF16))

        def scan(store):
            def step(j, carry):
                xr, xi = carry
                r = pl.ds(pl.multiple_of(j * N_CHUNK, N_CHUNK), N_CHUNK)
                nr = lr * xr - li * xi + bu_s[jt, r, 0:H]
                ni = lr * xi + li * xr + bu_s[jt, r, H:2 * H]
                if store:
                    x_ref[r, 0:H] = nr
                    x_ref[r, H:2 * H] = ni
                return nr, ni

            xr, xi = lax.fori_loop(0, tj, step, (st_s[:, 0:H], st_s[:, H:2 * H]))
            st_s[:, 0:H] = xr
            st_s[:, H:2 * H] = xi

        @pl.when(ph == 0)
        def _():
            scan(False)

            @pl.when(jt == njt - 1)
            def _():
                pr, pi = _power(lr[0:1], li[0:1], J)
                xin_s[0:1, :] = jnp.zeros((1, 2 * H), F32)
                for c in range(1, N_CHUNK):
                    qr, qi = _cmul(pr, pi, xin_s[c - 1:c, 0:H], xin_s[c - 1:c, H:2 * H])
                    xin_s[c:c + 1, 0:H] = qr + st_s[c - 1:c, 0:H]
                    xin_s[c:c + 1, H:2 * H] = qi + st_s[c - 1:c, H:2 * H]
                xin_ref[...] = xin_s[...]
                st_s[...] = xin_s[...]

        @pl.when(ph == 1)
        def _():
            scan(True)
            y = _dot(x_ref[...].astype(BF16), cm_ref[...].astype(BF16)) + d_ref[...] * u_ref[...]
            y_ref[...] = y
            gel_ref[...] = jax.nn.gelu(y).astype(gel_ref.dtype)

    return pl.pallas_call(
        body, name="ssm_fwd", grid=(SSM_COLS, B, 2, njt),
        in_specs=[
            pl.BlockSpec((None, R, LANES), lambda i, b, ph, jt: (b, jt, i)),
            pl.BlockSpec((None, LANES, 2 * H), lambda i, b, ph, jt: (i, 0, 0)),
            pl.BlockSpec((None, N_CHUNK, H), lambda i, b, ph, jt: (i, 0, 0)),
            pl.BlockSpec((None, N_CHUNK, H), lambda i, b, ph, jt: (i, 0, 0)),
            pl.BlockSpec((None, 2 * H, LANES), lambda i, b, ph, jt: (i, 0, 0)),
            pl.BlockSpec((1, LANES), lambda i, b, ph, jt: (0, i)),
        ],
        out_specs=[
            pl.BlockSpec((None, R, LANES), lambda i, b, ph, jt: (b, jt * ph, i)),
            pl.BlockSpec((None, R, LANES), lambda i, b, ph, jt: (b, jt * ph, i)),
            pl.BlockSpec((None, R, 2 * H), lambda i, b, ph, jt: (b, jt * ph, i)),
            pl.BlockSpec((None, None, N_CHUNK, 2 * H), lambda i, b, ph, jt: (b, i, 0, 0)),
        ],
        out_shape=[
            jax.ShapeDtypeStruct((B, L, SSM_WIDTH), F32),
            jax.ShapeDtypeStruct((B, L, SSM_WIDTH), BF16),
            jax.ShapeDtypeStruct((B, L, SSM_COLS * 2 * H), F32),
            jax.ShapeDtypeStruct((B, SSM_COLS, N_CHUNK, 2 * H), F32),
        ],
        scratch_shapes=[pltpu.VMEM((njt, R, 2 * H), F32), pltpu.VMEM((N_CHUNK, 2 * H), F32),
                        pltpu.VMEM((N_CHUNK, 2 * H), F32)],
        compiler_params=_params(("arbitrary",) * 4),
    )(u_p, w_b, lam_r, lam_i, c_m, d_skip)


def _ssm_bwd(dy_p, u_p, x, xin, w_bt, lam_r, lam_i, c_mt, d_skip, B, L, tj):
    J = L // N_CHUNK
    njt = J // tj
    R = tj * N_CHUNK
    H = 512
    x4 = x.reshape(B, J, N_CHUNK, SSM_COLS * 2 * H)

    def body(dy_ref, u_ref, x_ref, xp_ref, xin_ref, wbt_ref, lr_ref, li_ref, cmt_ref, d_ref,
             du_ref, dwb_ref, dcm_ref, dlr_ref, dli_ref, dd_ref, ca_s, a_s, st_s, dl_s):
        b, ph, jt = pl.program_id(1), pl.program_id(2), pl.program_id(3)
        jr = njt - 1 - jt
        lr, li = lr_ref[...], -li_ref[...]

        @pl.when(jnp.logical_and(b == 0, jnp.logical_and(ph == 0, jt == 0)))
        def _():
            dwb_ref[...] = jnp.zeros_like(dwb_ref)
            dcm_ref[...] = jnp.zeros_like(dcm_ref)
            dlr_ref[...] = jnp.zeros_like(dlr_ref)
            dli_ref[...] = jnp.zeros_like(dli_ref)
            dd_ref[...] = jnp.zeros_like(dd_ref)
            dl_s[...] = jnp.zeros_like(dl_s)

        @pl.when(jnp.logical_and(ph == 0, jt == 0))
        def _():
            st_s[...] = jnp.zeros_like(st_s)

        @pl.when(ph == 0)
        def _():
            ca_s[jt] = _dot(dy_ref[...].astype(BF16), cmt_ref[...].astype(BF16))

        def scan(store):
            def step(n, carry):
                ar, ai = carry
                r = pl.ds(pl.multiple_of((tj - 1 - n) * N_CHUNK, N_CHUNK), N_CHUNK)
                nr = lr * ar - li * ai + ca_s[jt, r, 0:H]
                ni = lr * ai + li * ar + ca_s[jt, r, H:2 * H]
                if store:
                    a_s[r, 0:H] = nr
                    a_s[r, H:2 * H] = ni
                return nr, ni

            ar, ai = lax.fori_loop(0, tj, step, (st_s[:, 0:H], st_s[:, H:2 * H]))
            st_s[:, 0:H] = ar
            st_s[:, H:2 * H] = ai

        @pl.when(ph == 0)
        def _():
            scan(False)

            @pl.when(jt == njt - 1)
            def _():
                pr, pi = _power(lr[0:1], li[0:1], J)
                a_s[N_CHUNK - 1:N_CHUNK, :] = jnp.zeros((1, 2 * H), F32)
                for c in range(N_CHUNK - 2, -1, -1):
                    qr, qi = _cmul(pr, pi, a_s[c + 1:c + 2, 0:H], a_s[c + 1:c + 2, H:2 * H])
                    a_s[c:c + 1, 0:H] = qr + st_s[c + 1:c + 2, 0:H]
                    a_s[c:c + 1, H:2 * H] = qi + st_s[c + 1:c + 2, H:2 * H]
                st_s[...] = a_s[0:N_CHUNK, :]

        @pl.when(ph == 1)
        def _():
            scan(True)
            dy = dy_ref[...]
            u = u_ref[...]
            a_b = a_s[...].astype(BF16)
            du_ref[...] = (_dot(a_b, wbt_ref[...].astype(BF16)) + d_ref[...] * dy).astype(du_ref.dtype)
            dwb_ref[...] += _dot(u.astype(BF16), a_b, _TN)
            dcm_ref[...] += _dot(x_ref[...].astype(BF16), dy.astype(BF16), _TN)
            dd_ref[...] += jnp.sum(dy * u, axis=0, keepdims=True)

            first = jnp.where(jr == 0, xin_ref[...], xp_ref[...])
            a0r, a0i = a_s[0:N_CHUNK, 0:H], a_s[0:N_CHUNK, H:2 * H]
            acc0 = (a0r * first[:, 0:H] + a0i * first[:, H:2 * H], a0i * first[:, 0:H] - a0r * first[:, H:2 * H])

            def step(j, carry):
                sr, si = carry
                r = pl.ds(pl.multiple_of(j * N_CHUNK, N_CHUNK), N_CHUNK)
                rp = pl.ds(pl.multiple_of((j - 1) * N_CHUNK, N_CHUNK), N_CHUNK)
                ar, ai = a_s[r, 0:H], a_s[r, H:2 * H]
                xr, xi = x_ref[rp, 0:H], x_ref[rp, H:2 * H]
                return sr + ar * xr + ai * xi, si + ai * xr - ar * xi

            sr, si = lax.fori_loop(1, tj, step, acc0)
            dl_s[:, 0:H] += sr
            dl_s[:, H:2 * H] += si

            @pl.when(jnp.logical_and(b == B - 1, jt == njt - 1))
            def _():
                dlr_ref[...] = jnp.sum(dl_s[:, 0:H], axis=0, keepdims=True)
                dli_ref[...] = jnp.sum(dl_s[:, H:2 * H], axis=0, keepdims=True)
                dl_s[...] = jnp.zeros_like(dl_s)

    rev = lambda ph, jt: (njt - 1 - jt) * ph + (njt - 1) * (1 - ph)
    return pl.pallas_call(
        body, name="ssm_bwd", grid=(SSM_COLS, B, 2, njt),
        in_specs=[
            pl.BlockSpec((None, R, LANES), lambda i, b, ph, jt: (b, njt - 1 - jt, i)),
            pl.BlockSpec((None, R, LANES), lambda i, b, ph, jt: (b, njt - 1 - jt, i)),
            pl.BlockSpec((None, R, 2 * H), lambda i, b, ph, jt: (b, rev(ph, jt), i)),
            pl.BlockSpec((None, None, N_CHUNK, 2 * H),
                         lambda i, b, ph, jt: (b, jnp.maximum((njt - 1 - jt) * tj - 1, 0), 0, i)),
            pl.BlockSpec((None, None, N_CHUNK, 2 * H), lambda i, b, ph, jt: (b, i, 0, 0)),
            pl.BlockSpec((None, 2 * H, LANES), lambda i, b, ph, jt: (i, 0, 0)),
            pl.BlockSpec((None, N_CHUNK, H), lambda i, b, ph, jt: (i, 0, 0)),
            pl.BlockSpec((None, N_CHUNK, H), lambda i, b, ph, jt: (i, 0, 0)),
            pl.BlockSpec((None, LANES, 2 * H), lambda i, b, ph, jt: (i, 0, 0)),
            pl.BlockSpec((1, LANES), lambda i, b, ph, jt: (0, i)),
        ],
        out_specs=[
            pl.BlockSpec((None, R, LANES), lambda i, b, ph, jt: (b, rev(ph, jt), i)),
            pl.BlockSpec((None, LANES, 2 * H), lambda i, b, ph, jt: (i, 0, 0)),
            pl.BlockSpec((None, 2 * H, LANES), lambda i, b, ph, jt: (i, 0, 0)),
            pl.BlockSpec((None, 1, H), lambda i, b, ph, jt: (i, 0, 0)),
            pl.BlockSpec((None, 1, H), lambda i, b, ph, jt: (i, 0, 0)),
            pl.BlockSpec((1, LANES), lambda i, b, ph, jt: (0, i)),
        ],
        out_shape=[
            jax.ShapeDtypeStruct((B, L, SSM_WIDTH), BF16),
            jax.ShapeDtypeStruct((SSM_COLS, LANES, 2 * H), F32),
            jax.ShapeDtypeStruct((SSM_COLS, 2 * H, LANES), F32),
            jax.ShapeDtypeStruct((SSM_COLS, 1, H), F32),
            jax.ShapeDtypeStruct((SSM_COLS, 1, H), F32),
            jax.ShapeDtypeStruct((1, SSM_WIDTH), F32),
        ],
        scratch_shapes=[pltpu.VMEM((njt, R, 2 * H), F32), pltpu.VMEM((R, 2 * H), F32),
                        pltpu.VMEM((N_CHUNK, 2 * H), F32), pltpu.VMEM((N_CHUNK, 2 * H), F32)],
        compiler_params=_params(("arbitrary",) * 4),
    )(dy_p, u_p, x, x4, xin, w_bt, lam_r, lam_i, c_mt, d_skip)


def _to_scan_layout(t, B, L):
    C = t.shape[-1]
    return t.reshape(B, N_CHUNK, L // N_CHUNK, C).transpose(0, 2, 1, 3).reshape(B, L, C)


def _from_scan_layout(t, B, L):
    C = t.shape[-1]
    return t.reshape(B, L // N_CHUNK, N_CHUNK, C).transpose(0, 2, 1, 3).reshape(B * L, C)


def _local_step(x, target, p, w_in, late_weights, mlp_grads_ready=None, rest_grads_ready=None, order=None, *,
                ssm_tile=128):
    B, L, D = x.shape
    T = B * L
    x2 = x.reshape(T, D)
    row = lambda v: v.reshape(1, -1)
    g1, g2, ga, gs, b_glu = row(p["norm1_g"]), row(p["norm2_g"]), row(p["attn_out_g"]), row(p["ssm_out_g"]), row(p["b_glu"])
    g1_first = g1 if order is None else g1 + order
    gq8 = jnp.tile(row(p["q_norm_g"]), (1, N_HEADS))
    gk8 = jnp.tile(row(p["k_norm_g"]), (1, N_HEADS))

    G, P, Hh = SSM_GROUPS, SSM_STATE, SSM_GROUP
    lam_re3, lam_im3 = p["ssm_lambda_re"].reshape(G, 1, P), p["ssm_lambda_im"].reshape(G, 1, P)
    log_dt3 = p["ssm_log_dt"].reshape(G, 1, 1)
    b_re_t, b_im_t = p["ssm_b_re"].transpose(0, 2, 1), p["ssm_b_im"].transpose(0, 2, 1)
    lbr, lbi, bbr, bbi = _ssm_prep(lam_re3, lam_im3, log_dt3, b_re_t, b_im_t)
    w_b = jnp.concatenate([_block_diag(bbr), _block_diag(bbi)], axis=2)
    c_mt = jnp.concatenate([_block_diag(p["ssm_c_re"]), -_block_diag(p["ssm_c_im"])], axis=2)
    w_bt, c_m = w_b.transpose(0, 2, 1), c_mt.transpose(0, 2, 1)
    lam_r = jnp.broadcast_to(lbr.reshape(SSM_COLS, 1, 512), (SSM_COLS, N_CHUNK, 512))
    lam_i = jnp.broadcast_to(lbi.reshape(SSM_COLS, 1, 512), (SSM_COLS, N_CHUNK, 512))
    d_skip = p["ssm_d"].reshape(1, SSM_WIDTH)

    xn = _rowwise("norm1", _rms, [x2], [g1_first], [(D, BF16)])
    proj = _matmul("proj", xn, w_in, tn=1024)
    qn, kn, vb = _qk_norm(proj, gq8, gk8)
    sb, attn_kept = _attn_fwd(qn, kn, vb, B, L)
    u_p = _to_scan_layout(proj[:, 3 * SB_WIDTH:], B, L)
    y_p, gel_p, xs, xin = _ssm_fwd(u_p, w_b, lam_r, lam_i, c_m, d_skip, B, L, ssm_tile)
    y2, gel = y_p.reshape(T, SSM_WIDTH), gel_p.reshape(T, SSM_WIDTH)
    w_glu, w_out, w_mlp_in, w_mlp_out = late_weights(gel)
    pre = _matmul("glu_gate", gel, w_glu)
    ssm_n = _rowwise("glu_out", _glu_branch, [y2, pre], [b_glu, gs], [(SSM_WIDTH, BF16)])
    sb_n = _rowwise("attn_out_norm", _rms, [sb], [ga], [(SB_WIDTH, BF16)])
    mixed = jnp.concatenate([sb_n, _from_scan_layout(ssm_n, B, L)], axis=1)
    def residual_and_norm(acc, res, g):
        h = acc + res
        return h, _rms(h, g)

    h1, hn = _matmul("out_proj", mixed, w_out, extras=[x2, g2], out_dtypes=(F32, BF16), tn=D,
                     epilogue=residual_and_norm)
    act, a_pre = _matmul("mlp_in", hn, w_mlp_in, out_dtypes=(BF16, BF16), tn=1024,
                         epilogue=lambda acc: (jnp.square(jnp.maximum(acc, 0.0)), acc))

    def loss_fn(acc, h, t):
        diff = acc + h - t
        part = jnp.sum(jnp.sum(diff * diff, axis=0, keepdims=True), axis=1, keepdims=True)
        d = diff * (1.0 / D)
        return d, d, part * (0.5 / D)

    d_out, d_out_b, loss = _matmul("mlp_out", act, w_mlp_out, extras=[h1, target.reshape(T, D)],
                                   out_dtypes=(F32, BF16), sums=[(1, 1)], epilogue=loss_fn, tm=512, tn=D)

    d_apre = _matmul("mlp_out_dx", d_out_b, w_mlp_out, tb=True, extras=[a_pre], out_dtypes=(BF16,), tn=1024,
                     epilogue=lambda acc, ap: (acc * (2.0 * jnp.maximum(ap.astype(F32), 0.0)),))
    both = lambda acc: (acc, acc)
    g_w_mlp_out, g_w_mlp_out_b = _matmul("mlp_out_dw", act, d_out_b, ta=True, out_dtypes=(F32, BF16), epilogue=both)
    g_w_mlp_in, g_w_mlp_in_b = _matmul("mlp_in_dw", hn, d_apre, ta=True, col_blocked=True, out_dtypes=(F32, BF16),
                                       epilogue=both, tn=w_mlp_in.shape[1] // N_DEV)
    if mlp_grads_ready is not None:
        g2 = g2 + mlp_grads_ready(g_w_mlp_out, g_w_mlp_out_b, g_w_mlp_in, g_w_mlp_in_b)

    def norm_bwd_res(dy, h, res, g):
        _, vjp = jax.vjp(_rms, h, g)
        dh, dg = vjp(dy)
        return res + dh, dg

    def norm_bwd_res2(dy, h, res, g):
        d, dg = norm_bwd_res(dy, h, res, g)
        return d, d, dg

    d_h1, d_h1_b, g_norm2 = _matmul("mlp_in_dx", d_apre, w_mlp_in, tb=True, extras=[h1, d_out, g2],
                                    out_dtypes=(F32, BF16), sums=[(1, D)], epilogue=norm_bwd_res2, tm=512, tn=D)

    d_mixed = _matmul("out_proj_dx", d_h1_b, w_out, tb=True, tn=1024)
    g_w_out, g_w_out_b = _matmul("out_proj_dw", mixed, d_h1_b, ta=True, out_dtypes=(F32, BF16), epilogue=both)

    def norm_bwd(h, dy, g):
        _, vjp = jax.vjp(_rms, h, g)
        return vjp(dy)

    d_sb, g_attn_out = _rowwise("attn_out_norm_bwd", norm_bwd, [sb, (d_mixed, 0, SB_WIDTH)], [ga],
                                [(SB_WIDTH, F32)], sums=[(1, SB_WIDTH)])
    d_ssm_n = _to_scan_layout(d_mixed[:, SB_WIDTH:], B, L).reshape(T, SSM_WIDTH)

    def glu_bwd(y, pre_, dy, bg, g):
        _, vjp = jax.vjp(_glu_branch, y, pre_, bg, g)
        d_y, d_pre, d_bg, d_g = vjp(dy)
        return d_y, d_pre, d_bg, d_g

    d_y_direct, d_pre, g_b_glu, g_ssm_out = _rowwise(
        "glu_out_bwd", glu_bwd, [y2, pre, d_ssm_n], [b_glu, gs], [(SSM_WIDTH, F32), (SSM_WIDTH, BF16)],
        sums=[(1, SSM_WIDTH), (1, SSM_WIDTH)])
    g_w_glu, g_w_glu_b = _matmul("glu_gate_dw", gel, d_pre, ta=True, out_dtypes=(F32, BF16), epilogue=both)

    def gelu_bwd(dg, y, dy0):
        _, vjp = jax.vjp(jax.nn.gelu, y)
        return (dy0 + vjp(dg)[0],)

    d_y = _matmul("glu_gate_dx", d_pre, w_glu, tb=True, extras=[y2, d_y_direct], epilogue=gelu_bwd)

    du_p, d_wb, d_cm, d_lr, d_li, g_d = _ssm_bwd(
        d_y.reshape(B, L, SSM_WIDTH), u_p, xs, xin, w_bt, lam_r, lam_i, c_mt, d_skip, B, L, ssm_tile)
    d_bb = _block_diag_take(d_wb.reshape(SSM_COLS, LANES, 2, 512))
    d_c = _block_diag_take(d_cm.transpose(0, 2, 1).reshape(SSM_COLS, LANES, 2, 512))
    g_lam_re, g_lam_im, g_log_dt, g_b_re_t, g_b_im_t = _ssm_prep_bwd(
        lam_re3, lam_im3, log_dt3, b_re_t, b_im_t,
        d_lr.reshape(G, 1, P), d_li.reshape(G, 1, P), d_bb[0], d_bb[1])
    d_qn, d_kn, d_v = _attn_bwd(qn, kn, vb, attn_kept, d_sb, B, L)
    d_q, d_k, g_q, g_k = _qk_norm_bwd(proj, gq8, gk8, d_qn, d_kn)

    d_proj = jnp.concatenate([d_q, d_k, d_v, _from_scan_layout(du_p, B, L)], axis=1)
    g_w_in, g_w_in_b = _matmul("proj_dw", xn, d_proj, ta=True, col_blocked=True, out_dtypes=(F32, BF16),
                               epilogue=both, tn=w_in.shape[1] // N_DEV)
    if rest_grads_ready is not None:
        g1 = g1 + rest_grads_ready([g_w_in, g_w_glu, g_w_out], [g_w_in_b, g_w_glu_b, g_w_out_b])
    grad_x, g_norm1 = _matmul("proj_dx", d_proj, w_in, tb=True, extras=[x2, d_h1, g1], sums=[(1, D)],
                              epilogue=norm_bwd_res, tm=512, tn=D)

    small = {
        "norm1_g": g_norm1.reshape(-1),
        "q_norm_g": g_q.reshape(-1),
        "k_norm_g": g_k.reshape(-1),
        "ssm_lambda_re": g_lam_re.reshape(G, P),
        "ssm_lambda_im": g_lam_im.reshape(G, P),
        "ssm_log_dt": g_log_dt.reshape(G),
        "ssm_b_re": g_b_re_t.transpose(0, 2, 1),
        "ssm_b_im": g_b_im_t.transpose(0, 2, 1),
        "ssm_c_re": d_c[0],
        "ssm_c_im": -d_c[1],
        "ssm_d": g_d.reshape(G, Hh),
        "b_glu": g_b_glu.reshape(-1),
        "attn_out_g": g_attn_out.reshape(-1),
        "ssm_out_g": g_ssm_out.reshape(-1),
        "norm2_g": g_norm2.reshape(-1),
    }
    big = {"w_in": g_w_in, "w_glu": g_w_glu, "w_out": g_w_out, "w_mlp_in": g_w_mlp_in, "w_mlp_out": g_w_mlp_out}
    return loss[0, 0], grad_x.reshape(B, L, D), small, big


_ANY = pl.BlockSpec(memory_space=pl.ANY)
_MESH = pl.DeviceIdType.MESH


def _all_gather(name, shards):
    n = len(shards)

    def body(*refs):
        in_refs, out_refs = refs[:n], refs[n:2 * n]
        send_sems, recv_sems, local_sems = refs[2 * n:]
        x, y, c = lax.axis_index("x"), lax.axis_index("y"), lax.axis_index("c")
        me, sibling = (x, y, c), (x, y, 1 - c)
        chips = [(1 - x, y), (x, 1 - y), (1 - x, 1 - y)]

        def copy(a, k, block, to, src=None):
            px, py, pc = block
            rows = out_refs[a].at[4 * px + 2 * py + pc]
            return pltpu.make_async_remote_copy(
                src_ref=rows if src is None else src, dst_ref=rows, send_sem=send_sems.at[a, k],
                recv_sem=recv_sems.at[a, k], device_id=to, device_id_type=_MESH)

        mine = [pltpu.make_async_copy(in_refs[a], out_refs[a].at[4 * x + 2 * y + c], local_sems.at[a]) for a in range(n)]
        first, passed = [], []
        for a in range(n):
            mine[a].start()
            first.append(copy(a, 0, me, sibling, src=in_refs[a]))
            first += [copy(a, 1 + j, me, (*chip, c), src=in_refs[a]) for j, chip in enumerate(chips)]
        for cp in first:
            cp.start()
        for j, chip in enumerate(chips):
            for a in range(n):
                copy(a, 1 + j, (*chip, c), me).wait_recv()
                fwd = copy(a, 4 + j, (*chip, c), sibling)
                fwd.start()
                passed.append(fwd)
        for a in range(n):
            copy(a, 0, sibling, me).wait_recv()
            for j, chip in enumerate(chips):
                copy(a, 4 + j, (*chip, 1 - c), me).wait_recv()
        for cp in first + passed:
            cp.wait_send()
        for cp in mine:
            cp.wait()

    return pl.pallas_call(
        body, name=name,
        in_specs=[_ANY] * n, out_specs=[_ANY] * n,
        out_shape=[jax.ShapeDtypeStruct((N_DEV, *s.shape), s.dtype) for s in shards],
        scratch_shapes=[pltpu.SemaphoreType.DMA((n, 7)), pltpu.SemaphoreType.DMA((n, 7)), pltpu.SemaphoreType.DMA((n,))],
    )(*shards)


_HBM = pl.BlockSpec(memory_space=pltpu.HBM)
_SEM = pl.BlockSpec(memory_space=pltpu.SEMAPHORE)
_EFFECT = pltpu.SideEffectType.DATAFLOW_SIDE_EFFECTING
_FLIPS = [(dx, dy, dc) for dx in (0, 1) for dy in (0, 1) for dc in (0, 1) if (dx, dy, dc) != (0, 0, 0)]


def _exchange_start(name, srcs, lands, per_peer):
    n = len(srcs)

    def body(*refs):
        src_refs, land_refs = refs[:n], refs[n:2 * n]
        send_sems, recv_sems = refs[2 * n:3 * n], refs[3 * n:4 * n]
        token = refs[-1]
        x, y, c = lax.axis_index("x"), lax.axis_index("y"), lax.axis_index("c")
        me = 4 * x + 2 * y + c
        for dx, dy, dc in _FLIPS:
            px, py, pc = (1 - x if dx else x), (1 - y if dy else y), (1 - c if dc else c)
            for a in range(n):
                pltpu.make_async_remote_copy(
                    src_ref=src_refs[a].at[4 * px + 2 * py + pc] if per_peer else src_refs[a],
                    dst_ref=land_refs[a].at[me], send_sem=send_sems[a], recv_sem=recv_sems[a],
                    device_id=(px, py, pc), device_id_type=_MESH).start()
        token[...] = jnp.zeros_like(token)

    hbm = lambda t: pltpu.with_memory_space_constraint(t, pltpu.HBM)
    res = pl.pallas_call(
        body, name=name,
        out_shape=(*[pltpu.SemaphoreType.DMA(())] * (2 * n), *[pltpu.HBM(t.shape, t.dtype) for t in (*srcs, *lands)],
                   jax.ShapeDtypeStruct((8, LANES), F32)),
        in_specs=[_HBM] * (2 * n),
        out_specs=(*[_SEM] * (2 * n), *[_HBM] * (2 * n), pl.BlockSpec(memory_space=pltpu.VMEM)),
        input_output_aliases={i: 2 * n + i for i in range(2 * n)},
        compiler_params=pltpu.CompilerParams(has_side_effects=_EFFECT),
    )(*[hbm(t) for t in (*srcs, *lands)])
    return res[:-1], res[-1]


def _exchange_wait(name, handle, after):
    n = len(handle) // 4
    sems, thru = handle[:2 * n], handle[2 * n:]

    def body(*refs):
        land_refs = refs[n:2 * n]
        send_sems, recv_sems = refs[2 * n:3 * n], refs[3 * n:4 * n]
        me = (lax.axis_index("x"), lax.axis_index("y"), lax.axis_index("c"))
        for a in range(n):
            seven = land_refs[a].at[pl.ds(0, len(_FLIPS))]
            all_copies = pltpu.make_async_remote_copy(
                src_ref=seven, dst_ref=seven, send_sem=send_sems[a], recv_sem=recv_sems[a], device_id=me,
                device_id_type=_MESH)
            all_copies.wait_send()
            all_copies.wait_recv()

    res = pl.pallas_call(
        body, name=name, out_shape=tuple(pltpu.HBM(t.shape, t.dtype) for t in thru),
        in_specs=[*[_HBM] * (2 * n), *[_SEM] * (2 * n), _ANY], out_specs=tuple([_HBM] * (2 * n)),
        input_output_aliases={i: i for i in range(2 * n)},
        compiler_params=pltpu.CompilerParams(has_side_effects=_EFFECT),
    )(*thru, *sems, after)
    return res[n:]


def _adamw_gathered(name, own, parts, me, w, m, v):
    r, c = w.shape
    tr = min(r, 256)

    def body(me_ref, own_ref, p_ref, w_ref, m_ref, v_ref, g_out, d_out, m_out, v_out):
        g = own_ref[...]
        for j in range(N_DEV):
            g = g + p_ref[j].astype(F32)
        delta, m_new, v_new = _adamw(w_ref[...], g, m_ref[...], v_ref[...])
        g_out[...] = g
        d_out[...] = delta
        m_out[...] = m_new
        v_out[...] = v_new

    spec = pl.BlockSpec((tr, c), lambda i, me_ref: (i, 0))
    return pl.pallas_call(
        body, name=name,
        grid_spec=pltpu.PrefetchScalarGridSpec(
            num_scalar_prefetch=1, grid=(r // tr,),
            in_specs=[pl.BlockSpec((None, tr, c), lambda i, me_ref: (me_ref[0], i, 0)),
                      pl.BlockSpec((N_DEV, tr, c), lambda i, me_ref: (0, i, 0)), spec, spec, spec],
            out_specs=[spec] * 4),
        out_shape=[jax.ShapeDtypeStruct((r, c), F32)] * 4,
        compiler_params=_params(("parallel",)),
    )(me, own, parts, w, m, v)


def _adamw(w, g, m, v):
    m = ADAM_B1 * m + (1.0 - ADAM_B1) * g
    v = ADAM_B2 * v + (1.0 - ADAM_B2) * jnp.square(g)
    m_hat = m / (1.0 - ADAM_B1 ** ADAM_STEP)
    v_hat = v / (1.0 - ADAM_B2 ** ADAM_STEP)
    delta = -ADAM_LR * (m_hat / (jnp.sqrt(v_hat) + ADAM_EPS) + ADAM_WD * w)
    return delta, m, v


def _adamw_small(name, parts, w, m, v):
    _, r, c = parts.shape
    tr = 8

    def body(p_ref, w_ref, m_ref, v_ref, g_out, d_out, m_out, v_out):
        g = p_ref[0]
        for j in range(1, N_DEV):
            g = g + p_ref[j]
        delta, m_new, v_new = _adamw(w_ref[...], g, m_ref[...], v_ref[...])
        g_out[...] = g
        d_out[...] = delta
        m_out[...] = m_new
        v_out[...] = v_new

    spec = pl.BlockSpec((tr, c), lambda i: (i, 0))
    return pl.pallas_call(
        body, name=name, grid=(r // tr,),
        in_specs=[pl.BlockSpec((N_DEV, tr, c), lambda i: (0, i, 0)), spec, spec, spec],
        out_specs=[spec] * 4, out_shape=[jax.ShapeDtypeStruct((r, c), F32)] * 4,
        compiler_params=_params(("parallel",)),
    )(parts, w, m, v)


_WEIGHTS = ["norm1_g", "w_in", "q_norm_g", "k_norm_g", "ssm_lambda_re", "ssm_lambda_im", "ssm_log_dt", "ssm_b_re",
            "ssm_b_im", "ssm_c_re", "ssm_c_im", "ssm_d", "w_glu", "b_glu", "attn_out_g", "ssm_out_g", "w_out",
            "norm2_g", "w_mlp_in", "w_mlp_out"]
_BIG = ["w_in", "w_glu", "w_out", "w_mlp_in", "w_mlp_out"]
_SMALL = [n for n in _WEIGHTS if n not in _BIG]
_PACK_COLS = 1024


def _pack(tree, last=None):
    flat = [tree[n].reshape(-1).astype(F32) for n in _SMALL]
    size = sum(f.shape[0] for f in flat)
    rows = -(-(size + 1) // (_PACK_COLS * 8)) * 8
    pad = jnp.zeros((rows * _PACK_COLS - size - 1,), F32)
    tail = jnp.zeros((1,), F32) if last is None else last.reshape(1).astype(F32)
    return jnp.concatenate(flat + [pad, tail]).reshape(rows, _PACK_COLS)


def _unpack(buf, like):
    flat, out, off = buf.reshape(-1), {}, 0
    for n in _SMALL:
        size = like[n].size
        out[n] = flat[off:off + size].reshape(like[n].shape)
        off += size
    return out


def kernel(x, norm1_g, w_in, q_norm_g, k_norm_g, ssm_lambda_re, ssm_lambda_im, ssm_log_dt, ssm_b_re, ssm_b_im, ssm_c_re, ssm_c_im, ssm_d, w_glu, b_glu, attn_out_g, ssm_out_g, w_out, norm2_g, w_mlp_in, w_mlp_out, loss_target, m_norm1_g, m_w_in, m_q_norm_g, m_k_norm_g, m_ssm_lambda_re, m_ssm_lambda_im, m_ssm_log_dt, m_ssm_b_re, m_ssm_b_im, m_ssm_c_re, m_ssm_c_im, m_ssm_d, m_w_glu, m_b_glu, m_attn_out_g, m_ssm_out_g, m_w_out, m_norm2_g, m_w_mlp_in, m_w_mlp_out, v_norm1_g, v_w_in, v_q_norm_g, v_k_norm_g, v_ssm_lambda_re, v_ssm_lambda_im, v_ssm_log_dt, v_ssm_b_re, v_ssm_b_im, v_ssm_c_re, v_ssm_c_im, v_ssm_d, v_w_glu, v_b_glu, v_attn_out_g, v_ssm_out_g, v_w_out, v_norm2_g, v_w_mlp_in, v_w_mlp_out):
    w = dict(zip(_WEIGHTS, (norm1_g, w_in, q_norm_g, k_norm_g, ssm_lambda_re, ssm_lambda_im, ssm_log_dt, ssm_b_re, ssm_b_im, ssm_c_re, ssm_c_im, ssm_d, w_glu, b_glu, attn_out_g, ssm_out_g, w_out, norm2_g, w_mlp_in, w_mlp_out)))
    m = dict(zip(_WEIGHTS, (m_norm1_g, m_w_in, m_q_norm_g, m_k_norm_g, m_ssm_lambda_re, m_ssm_lambda_im, m_ssm_log_dt, m_ssm_b_re, m_ssm_b_im, m_ssm_c_re, m_ssm_c_im, m_ssm_d, m_w_glu, m_b_glu, m_attn_out_g, m_ssm_out_g, m_w_out, m_norm2_g, m_w_mlp_in, m_w_mlp_out)))
    v = dict(zip(_WEIGHTS, (v_norm1_g, v_w_in, v_q_norm_g, v_k_norm_g, v_ssm_lambda_re, v_ssm_lambda_im, v_ssm_log_dt, v_ssm_b_re, v_ssm_b_im, v_ssm_c_re, v_ssm_c_im, v_ssm_d, v_w_glu, v_b_glu, v_attn_out_g, v_ssm_out_g, v_w_out, v_norm2_g, v_w_mlp_in, v_w_mlp_out)))
    core = lax.axis_index("c").astype(jnp.int32).reshape(1)
    chip = (2 * lax.axis_index("x") + lax.axis_index("y")).astype(jnp.int32).reshape(1)

    me = (2 * chip + core).astype(jnp.int32)

    def landing(own=None, like=None):
        own = jnp.zeros_like(like) if own is None else own
        return lax.dynamic_update_slice(lax.empty((N_DEV, *like.shape), like.dtype), own[None], (me[0], 0, 0))

    (w_in_blocks,) = _all_gather("w_in_all_gather", [w_in.astype(BF16)])
    w_in_full = w_in_blocks.transpose(1, 0, 2).reshape(w_in.shape[0], -1)
    late = [n for n in _BIG if n != "w_in"]
    shards = [w[n].astype(BF16) for n in late]
    w_in_blocks, shards = lax.optimization_barrier((w_in_blocks, shards))
    weights_handle, weights_token = _exchange_start(
        "weights_send", shards, [landing(s, s) for s in shards], per_peer=False)

    def late_weights(after):
        got = dict(zip(late, _exchange_wait("weights_arrive", weights_handle, after)))
        return (got["w_glu"].reshape(-1, w_glu.shape[1]), got["w_out"].reshape(-1, w_out.shape[1]),
                got["w_mlp_in"].transpose(1, 0, 2).reshape(w_mlp_in.shape[0], -1),
                got["w_mlp_out"].reshape(-1, w_mlp_out.shape[1]))

    mlp = ["w_mlp_out", "w_mlp_in"]
    sent = {}

    def send_grads(name, names, own, own_b):
        blocks = lambda g, n: g.reshape(N_DEV, *w[n].shape)
        sent[name + "_own"] = [blocks(g, n) for g, n in zip(own, names)]
        srcs = [blocks(g, n) for g, n in zip(own_b, names)]
        sent[name], token = _exchange_start(name, srcs, [landing(like=s[0]) for s in srcs], per_peer=True)
        return token[0, 0]

    def mlp_grads_ready(g_out, g_out_b, g_in, g_in_b):
        return send_grads("mlp_grads_send", mlp, [g_out, g_in], [g_out_b, g_in_b])

    rest = ["w_in", "w_glu", "w_out"]

    def rest_grads_ready(own, own_b):
        return send_grads("rest_grads_send", rest, own, own_b)

    loss_local, grad_x, g_small, g_big = _local_step(
        x, loss_target, {n: w[n] for n in _SMALL}, w_in_full, late_weights, mlp_grads_ready, rest_grads_ready,
        weights_token[0, 0])

    grads, delta, new_m, new_v = {}, {}, {}, {}
    small = _pack(g_small, last=loss_local)
    small_handle, small_token = _exchange_start("small_grads_send", [small], [landing(small, small)], per_peer=False)

    for send, arrive, names in (("mlp_grads_send", "mlp_grads_arrive", mlp),
                                ("rest_grads_send", "rest_grads_arrive", rest)):
        for n, own, part in zip(names, sent[send + "_own"], _exchange_wait(arrive, sent[send], small_token)):
            grads[n], delta[n], new_m[n], new_v[n] = _adamw_gathered("adamw_" + n, own, part, me, w[n], m[n], v[n])

    shards_done = lax.optimization_barrier(tuple(new_v[n] for n in _BIG))
    (small_parts,) = _exchange_wait("small_grads_arrive", small_handle, shards_done[-1])
    packed = _adamw_small("adamw_small", small_parts, _pack(w), _pack(m), _pack(v))
    for tree, buf in zip((grads, delta, new_m, new_v), packed):
        tree.update(_unpack(buf, w))
    loss = packed[0][-1, -1]

    return (loss, grad_x, *[grads[n] for n in _WEIGHTS], *[delta[n] for n in _WEIGHTS],
            *[new_m[n] for n in _WEIGHTS], *[new_v[n] for n in _WEIGHTS])
```

```python
import functools
import math

import jax
import jax.numpy as jnp
from jax import lax
from jax.experimental import pallas as pl
from jax.experimental.pallas import tpu as pltpu

F32 = jnp.float32
BF16 = jnp.bfloat16

EPS = 1e-6
HEAD_DIM = 64
N_HEADS = 8
SB_WIDTH = 512
SSM_WIDTH = 512
SSM_GROUP = 16
SSM_GROUPS = 32
SSM_STATE = 64
QBLOCK = 128
KBLOCK = 256
N_CHUNK = 8
SSM_COLS = 4
LANES = 128
N_DEV = 8

ADAM_LR = 0.001
ADAM_B1 = 0.9
ADAM_B2 = 0.999
ADAM_EPS = 1e-08
ADAM_WD = 0.01
ADAM_STEP = 10

VMEM_LIMIT = 56 * 1024 * 1024

_NT = (((1,), (1,)), ((), ()))
_NN = (((1,), (0,)), ((), ()))
_TN = (((0,), (0,)), ((), ()))


def _dot(a, b, dims=_NN):
    return lax.dot_general(a, b, dims, preferred_element_type=F32)


def _params(sem):
    return pltpu.CompilerParams(dimension_semantics=sem, vmem_limit_bytes=VMEM_LIMIT)


def _matmul(name, a, b, *, ta=False, tb=False, extras=(), epilogue=None, out_dtypes=(F32,), sums=(),
            col_blocked=False, tm=1024, tn=512, tk=4096):
    M, K = (a.shape[1], a.shape[0]) if ta else a.shape
    N = b.shape[0] if tb else b.shape[1]
    tm, tn, tk = min(tm, M), min(tn, N), min(tk, K)
    assert M % tm == 0 and N % tn == 0 and K % tk == 0, (name, M, N, K)
    assert not sums or (tn == N and tk == K), name
    nk = K // tk
    n_ex, n_out, n_sum = len(extras), len(out_dtypes), len(sums)
    dims = (((0 if ta else 1,), (1 if tb else 0,)), ((), ()))

    def body(*refs):
        a_ref, b_ref = refs[0], refs[1]
        ex_refs = refs[2:2 + n_ex]
        o_refs = refs[2 + n_ex:2 + n_ex + n_out]
        s_refs = refs[2 + n_ex + n_out:2 + n_ex + n_out + n_sum]
        k = pl.program_id(2)
        part = _dot(a_ref[...].astype(BF16), b_ref[...].astype(BF16), dims)

        def finish(acc):
            outs = (acc,) if epilogue is None else epilogue(acc, *[e[...] for e in ex_refs])
            for o_ref, o in zip(o_refs, outs[:n_out]):
                o_ref[...] = o.astype(o_ref.dtype)
            if n_sum:
                @pl.when(pl.program_id(0) == 0)
                def _():
                    for s_ref in s_refs:
                        s_ref[...] = jnp.zeros_like(s_ref)

                for s_ref, v in zip(s_refs, outs[n_out:]):
                    s_ref[...] += v

        if nk == 1:
            finish(part)
        else:
            acc_ref = refs[-1]

            @pl.when(k == 0)
            def _():
                acc_ref[...] = part

            @pl.when(jnp.logical_and(k > 0, k < nk - 1))
            def _():
                acc_ref[...] += part

            @pl.when(k == nk - 1)
            def _():
                finish(acc_ref[...] + part)

    a_spec = pl.BlockSpec((tk, tm), lambda i, j, k: (k, i)) if ta else pl.BlockSpec((tm, tk), lambda i, j, k: (i, k))
    b_spec = pl.BlockSpec((tn, tk), lambda i, j, k: (j, k)) if tb else pl.BlockSpec((tk, tn), lambda i, j, k: (k, j))
    ex_specs = [pl.BlockSpec((1, tn), lambda i, j, k: (0, j)) if e.shape[0] == 1 else
                pl.BlockSpec((tm, tn), lambda i, j, k: (i, j)) for e in extras]
    if col_blocked:
        out_specs = [pl.BlockSpec((None, tm, tn), lambda i, j, k: (j, i, 0)) for _ in out_dtypes]
        out_shape = [jax.ShapeDtypeStruct((N // tn, M, tn), dt) for dt in out_dtypes]
    else:
        out_specs = [pl.BlockSpec((tm, tn), lambda i, j, k: (i, j)) for _ in out_dtypes]
        out_shape = [jax.ShapeDtypeStruct((M, N), dt) for dt in out_dtypes]
    out_specs += [pl.BlockSpec(s, lambda i, j, k: (0, 0)) for s in sums]
    out_shape += [jax.ShapeDtypeStruct(s, F32) for s in sums]
    outs = pl.pallas_call(
        body, name=name, grid=(M // tm, N // tn, nk),
        in_specs=[a_spec, b_spec, *ex_specs], out_specs=out_specs, out_shape=out_shape,
        scratch_shapes=[pltpu.VMEM((tm, tn), F32)] if nk > 1 else [],
        compiler_params=_params(("arbitrary",) * 3 if sums else ("parallel", "parallel", "arbitrary")),
    )(a, b, *extras)
    return outs[0] if len(outs) == 1 else outs


def _rowwise(name, fn, rows, small, outs, sums=(), tile=256):
    specs, args = [], []
    T = None
    for r in rows:
        arr, cb, w = r if isinstance(r, tuple) else (r, 0, r.shape[1])
        T = arr.shape[0]
        specs.append((w, cb))
        args.append(arr)
    tile = min(tile, T)
    assert T % tile == 0
    n_r, n_s, n_o, n_a = len(rows), len(small), len(outs), len(sums)

    def body(*refs):
        r_refs = refs[:n_r]
        s_refs = refs[n_r:n_r + n_s]
        o_refs = refs[n_r + n_s:n_r + n_s + n_o]
        a_refs = refs[n_r + n_s + n_o:]
        res = fn(*[r[...] for r in r_refs], *[s[...] for s in s_refs])
        res = res if isinstance(res, (tuple, list)) else (res,)
        for o_ref, o in zip(o_refs, res[:n_o]):
            o_ref[...] = o.astype(o_ref.dtype)

        @pl.when(pl.program_id(0) == 0)
        def _():
            for a_ref in a_refs:
                a_ref[...] = jnp.zeros_like(a_ref)

        for a_ref, v in zip(a_refs, res[n_o:]):
            a_ref[...] += v.astype(F32)

    in_specs = [pl.BlockSpec((tile, w), functools.partial(lambda i, cb: (i, cb), cb=cb)) for w, cb in specs]
    in_specs += [pl.BlockSpec(s.shape, functools.partial(lambda i, nd: (0,) * nd, nd=s.ndim)) for s in small]
    out_specs = [pl.BlockSpec((tile, w), lambda i: (i, 0)) for w, _ in outs]
    out_specs += [pl.BlockSpec(s, functools.partial(lambda i, nd: (0,) * nd, nd=len(s))) for s in sums]
    out_shape = [jax.ShapeDtypeStruct((T, w), dt) for w, dt in outs]
    out_shape += [jax.ShapeDtypeStruct(s, F32) for s in sums]
    res = pl.pallas_call(
        body, name=name, grid=(T // tile,), in_specs=in_specs, out_specs=out_specs, out_shape=out_shape,
        compiler_params=_params(("arbitrary",)),
    )(*args, *small)
    return res[0] if len(res) == 1 else res


def _rms(x, g):
    return x * lax.rsqrt(jnp.mean(x * x, axis=-1, keepdims=True) + EPS) * g


def _glu_branch(y, pre, b_glu, g_out):
    g = jax.nn.gelu(y)
    return _rms(g * jax.nn.sigmoid(pre + b_glu), g_out)


def _split_dot(x, tri_bf):
    hi = x.astype(BF16)
    lo = (x - hi.astype(F32)).astype(BF16)
    return _dot(hi, tri_bf) + _dot(lo, tri_bf)


def _softplus(z):
    return jnp.maximum(z, 0.0) + jnp.log(1.0 + jnp.exp(-jnp.abs(z)))


def _head(h):
    return slice(h * HEAD_DIM, (h + 1) * HEAD_DIM)


def _head_mean(x, seg):
    return _split_dot(x, seg) * (1.0 / HEAD_DIM)


def _qk_norm(proj, gq, gk):
    scale = 1.0 / math.sqrt(HEAD_DIM)
    idx = jnp.arange(SB_WIDTH) // HEAD_DIM
    seg = (idx[:, None] == idx[None, :]).astype(BF16)

    def fn(q, k, v, gq_, gk_, seg_):
        qn = q * lax.rsqrt(_head_mean(q * q, seg_) + EPS) * (gq_ * scale)
        kn = k * lax.rsqrt(_head_mean(k * k, seg_) + EPS) * gk_
        return qn, kn, v

    return _rowwise("qk_norm", fn, [(proj, 0, SB_WIDTH), (proj, 1, SB_WIDTH), (proj, 2, SB_WIDTH)], [gq, gk, seg],
                    [(SB_WIDTH, BF16)] * 3)


def _qk_norm_bwd(proj, gq, gk, d_qn, d_kn):
    scale = 1.0 / math.sqrt(HEAD_DIM)
    idx = jnp.arange(SB_WIDTH) // HEAD_DIM
    seg = (idx[:, None] == idx[None, :]).astype(BF16)

    def one(x, g, dy, seg_):
        r = lax.rsqrt(_head_mean(x * x, seg_) + EPS)
        gdy = g * dy
        dx = r * gdy - x * (r * r * r) * _head_mean(gdy * x, seg_)
        dg = jnp.sum(dy * x * r, axis=0, keepdims=True)
        return dx, sum(dg[:, _head(h)] for h in range(N_HEADS))

    def fn(q, k, dqn, dkn, gq_, gk_, seg_):
        dq, dgq = one(q, gq_, dqn * scale, seg_)
        dk, dgk = one(k, gk_, dkn, seg_)
        return dq, dk, dgq, dgk

    return _rowwise("qk_norm_bwd", fn, [(proj, 0, SB_WIDTH), (proj, 1, SB_WIDTH), d_qn, d_kn], [gq, gk, seg],
                    [(SB_WIDTH, BF16)] * 2, sums=[(1, HEAD_DIM)] * 2)


def _split_heads(refs, scratch, L):
    def chunk(i, _):
        r = pl.ds(pl.multiple_of(i * QBLOCK, QBLOCK), QBLOCK)
        for ref, s in zip(refs, scratch):
            for h in range(2):
                s[h, r, :] = ref[r, _head(h)]
        return 0

    lax.fori_loop(0, L // QBLOCK, chunk, 0)


Q_HALVES = KBLOCK // QBLOCK
_CHAINS = [(h, r) for h in range(2) for r in range(Q_HALVES)]


def _valid(i, kb):
    row = lax.broadcasted_iota(jnp.int32, (QBLOCK, KBLOCK), 0)
    col = lax.broadcasted_iota(jnp.int32, (QBLOCK, KBLOCK), 1)
    return col + (kb * KBLOCK - i * QBLOCK) < row


def _attn_fwd(qn, kn, vb, B, L):
    n_pairs = L // KBLOCK
    n_hp = N_HEADS // 2
    nc = len(_CHAINS)

    def body(q_ref, k_ref, v_ref, o_ref, a_ref, q_s, k_s, v_s, after_s, z_s, stage_s, sems):
        _split_heads((q_ref, k_ref, v_ref), (q_s, k_s, v_s), L)
        r2 = lax.broadcasted_iota(jnp.int32, (KBLOCK, KBLOCK), 0)
        c2 = lax.broadcasted_iota(jnp.int32, (KBLOCK, KBLOCK), 1)
        after_s[...] = (r2 > c2).astype(after_s.dtype)
        g = pl.program_id(0) * n_hp + pl.program_id(1)

        def q_pair(p, _):
            rows = [pl.ds(pl.multiple_of((p * Q_HALVES + r) * QBLOCK, QBLOCK), QBLOCK) for r in range(Q_HALVES)]
            q_c = [q_s[h, rows[r], :] for h, r in _CHAINS]
            cs = range(nc)

            def scores(kb):
                rk = pl.ds(pl.multiple_of(kb * KBLOCK, KBLOCK), KBLOCK)
                return [_dot(q_c[c], k_s[_CHAINS[c][0], rk, :], _NT) for c in cs]

            def saved(kb):
                return pltpu.make_async_copy(stage_s.at[kb & 1], a_ref.at[g, p, kb], sems.at[kb & 1])

            def k_block(kb, carry, diagonal):
                rk = pl.ds(pl.multiple_of(kb * KBLOCK, KBLOCK), KBLOCK)
                if diagonal:
                    valid = [_valid(p * Q_HALVES + r, kb) for r in range(Q_HALVES)]
                    keep = lambda c, t: jnp.where(valid[_CHAINS[c][1]], t, 0.0)
                    z = scores(kb)
                else:
                    keep = lambda c, t: t
                    z = [z_s[(kb + 1) & 1, c] for c in cs]
                ahead = scores(jnp.maximum(kb - 1, 0))
                for c in cs:
                    z_s[kb & 1, c] = ahead[c]
                sp = [_softplus(z[c]) for c in cs]
                lsig = [z[c] - sp[c] for c in cs]
                lom = [keep(c, -sp[c]) for c in cs]
                tail = [_split_dot(lom[c], after_s[...]) + carry[c][0] for c in cs]
                a = [keep(c, jnp.exp(lsig[c] + tail[c])).astype(v_s.dtype) for c in cs]
                acc = [carry[c][1] + _dot(a[c], v_s[_CHAINS[c][0], rk, :]) for c in cs]
                for c in cs:
                    stage_s[kb & 1, c] = a[c]
                saved(kb).start()
                return tuple((carry[c][0] + jnp.sum(lom[c], axis=1, keepdims=True), acc[c]) for c in cs)

            def next_block(n, carry):
                kb = p - n

                @pl.when(n >= 2)
                def _():
                    saved(kb + 2).wait()

                return k_block(kb, carry, False)

            init = (jnp.zeros((QBLOCK, 1), F32), jnp.zeros((QBLOCK, HEAD_DIM), F32))
            first = k_block(p, (init,) * nc, True)
            res = lax.fori_loop(1, p + 1, next_block, first)
            saved(0).wait()

            @pl.when(p >= 1)
            def _():
                saved(1).wait()

            for r in range(Q_HALVES):
                o_ref[rows[r], :] = jnp.concatenate([res[c][1] for c in cs if _CHAINS[c][1] == r], axis=1)
            return 0

        lax.fori_loop(0, n_pairs, q_pair, 0)

    spec = pl.BlockSpec((L, LANES), lambda b, p: (b, p))
    return pl.pallas_call(
        body, name="attn_fwd", grid=(B, n_hp),
        in_specs=[spec] * 3, out_specs=[spec, _ANY],
        out_shape=[jax.ShapeDtypeStruct((B * L, SB_WIDTH), F32),
                   jax.ShapeDtypeStruct((B * n_hp, n_pairs, n_pairs, nc, QBLOCK, KBLOCK), BF16)],
        scratch_shapes=[pltpu.VMEM((2, L, HEAD_DIM), BF16)] * 3 + [pltpu.VMEM((KBLOCK, KBLOCK), BF16)]
        + [pltpu.VMEM((2, nc, QBLOCK, KBLOCK), F32), pltpu.VMEM((2, nc, QBLOCK, KBLOCK), BF16),
           pltpu.SemaphoreType.DMA((2,))],
        compiler_params=_params(("parallel", "parallel")),
    )(qn, kn, vb)


def _attn_bwd(qn, kn, vb, kept_a, d_sb, B, L):
    n_pairs = L // KBLOCK
    n_hp = N_HEADS // 2
    nc = len(_CHAINS)
    slots = 3

    def body(q_ref, k_ref, v_ref, do_ref, a_ref, dq_ref, dk_ref, dv_ref,
             q_s, k_s, v_s, qt_s, dkt_s, dvt_s, before_s, stage_s, sems):
        _split_heads((q_ref, k_ref, v_ref), (q_s, k_s, v_s), L)
        g = pl.program_id(0) * n_hp + pl.program_id(1)

        def transpose_q(i, _):
            r = pl.ds(pl.multiple_of(i * QBLOCK, QBLOCK), QBLOCK)
            qt_s[:, r] = q_ref[r, :].astype(F32).T.astype(qt_s.dtype)
            return 0

        lax.fori_loop(0, L // QBLOCK, transpose_q, 0)
        dkt_s[...] = jnp.zeros_like(dkt_s)
        dvt_s[...] = jnp.zeros_like(dvt_s)
        r2 = lax.broadcasted_iota(jnp.int32, (KBLOCK, KBLOCK), 0)
        c2 = lax.broadcasted_iota(jnp.int32, (KBLOCK, KBLOCK), 1)
        before_s[...] = (r2 < c2).astype(before_s.dtype)

        def q_pair(p, _):
            rows = [pl.ds(pl.multiple_of((p * Q_HALVES + r) * QBLOCK, QBLOCK), QBLOCK) for r in range(Q_HALVES)]
            pair = pl.ds(pl.multiple_of(p * KBLOCK, KBLOCK), KBLOCK)
            do2 = do_ref[pair, :]
            do_t = do2.T.astype(v_s.dtype)
            cs = range(nc)
            hs = range(2)
            q_c = [q_s[h, rows[r], :] for h, r in _CHAINS]
            do_c = [do2[r * QBLOCK:(r + 1) * QBLOCK, _head(h)].astype(v_s.dtype) for h, r in _CHAINS]
            qt_h = [qt_s[_head(h), pair] for h in hs]
            dot_h = [do_t[_head(h), :] for h in hs]

            def kept(kb):
                slot = lax.rem(kb, slots)
                return pltpu.make_async_copy(a_ref.at[g, p, kb], stage_s.at[slot], sems.at[slot])

            def k_block(kb, carry, diagonal):
                rk = pl.ds(pl.multiple_of(kb * KBLOCK, KBLOCK), KBLOCK)
                if diagonal:
                    valid = [_valid(p * Q_HALVES + r, kb) for r in range(Q_HALVES)]
                    keep = lambda c, t: jnp.where(valid[_CHAINS[c][1]], t, 0.0)
                else:
                    keep = lambda c, t: t

                    @pl.when(kb + 2 <= p)
                    def _():
                        kept(kb + 2).start()

                kept(kb).wait()
                slot = lax.rem(kb, slots)
                k_b = [k_s[h, rk, :] for h in hs]
                z = [_dot(q_c[c], k_b[_CHAINS[c][0]], _NT) for c in cs]
                da = [_dot(do_c[c], v_s[_CHAINS[c][0], rk, :], _NT) for c in cs]
                a = [stage_s[slot, c] for c in cs]
                dla = [a[c].astype(F32) * da[c] for c in cs]
                for h in hs:
                    a_h = jnp.concatenate([a[c] for c in cs if _CHAINS[c][0] == h], axis=0)
                    dvt_s[_head(h), rk] += _dot(dot_h[h], a_h)
                d_lom = [carry[c][0] + _split_dot(dla[c], before_s[...]) for c in cs]
                beta = [jax.nn.sigmoid(z[c]) for c in cs]
                dz_b = [(dla[c] * (1.0 - beta[c]) - keep(c, beta[c] * d_lom[c])).astype(v_s.dtype) for c in cs]
                dq_acc = [carry[c][1] + _dot(dz_b[c], k_b[_CHAINS[c][0]]) for c in cs]
                for h in hs:
                    dz_h = jnp.concatenate([dz_b[c] for c in cs if _CHAINS[c][0] == h], axis=0)
                    dkt_s[_head(h), rk] += _dot(qt_h[h], dz_h)
                return tuple((carry[c][0] + jnp.sum(dla[c], axis=1, keepdims=True), dq_acc[c]) for c in cs)

            init = (jnp.zeros((QBLOCK, 1), F32), jnp.zeros((QBLOCK, HEAD_DIM), F32))
            kept(0).start()

            @pl.when(p >= 1)
            def _():
                kept(1).start()

            before =lax.fori_loop(0, p, lambda kb, carry: k_block(kb, carry, False), (init,) * nc)
            res = k_block(p, before, True)
            for r in range(Q_HALVES):
                dq_ref[rows[r], :] = jnp.concatenate([res[c][1] for c in cs if _CHAINS[c][1] == r], axis=1)
            return 0

        lax.fori_loop(0, n_pairs, q_pair, 0)

        def transpose_out(i, _):
            r = pl.ds(pl.multiple_of(i * QBLOCK, QBLOCK), QBLOCK)
            dk_ref[r, :] = dkt_s[:, r].T
            dv_ref[r, :] = dvt_s[:, r].T.astype(dv_ref.dtype)
            return 0

        lax.fori_loop(0, L // QBLOCK, transpose_out, 0)

    spec = pl.BlockSpec((L, LANES), lambda b, p: (b, p))
    return pl.pallas_call(
        body, name="attn_bwd", grid=(B, n_hp),
        in_specs=[spec] * 4 + [_ANY], out_specs=[spec] * 3,
        out_shape=[jax.ShapeDtypeStruct((B * L, SB_WIDTH), F32)] * 2 + [jax.ShapeDtypeStruct((B * L, SB_WIDTH), BF16)],
        scratch_shapes=[pltpu.VMEM((2, L, HEAD_DIM), BF16)] * 3 + [pltpu.VMEM((LANES, L), BF16)]
        + [pltpu.VMEM((LANES, L), F32)] * 2 + [pltpu.VMEM((KBLOCK, KBLOCK), BF16)]
        + [pltpu.VMEM((slots, nc, QBLOCK, KBLOCK), BF16), pltpu.SemaphoreType.DMA((slots,))],
        compiler_params=_params(("parallel", "parallel")),
    )(qn, kn, vb, d_sb, kept_a)


def _ssm_discretise(lam_re, lam_im, log_dt, b_re, b_im):
    dt = jnp.exp(log_dt)
    mag = jnp.exp(lam_re * dt)
    lbr = mag * jnp.cos(lam_im * dt)
    lbi = mag * jnp.sin(lam_im * dt)
    den = lam_re * lam_re + lam_im * lam_im
    nr, ni = lbr - 1.0, lbi
    cr = (nr * lam_re + ni * lam_im) / den
    ci = (ni * lam_re - nr * lam_im) / den
    return lbr, lbi, cr * b_re - ci * b_im, cr * b_im + ci * b_re


def _ssm_prep(lam_re, lam_im, log_dt, b_re_t, b_im_t):
    def body(lr, li, ld, br, bi, o_lr, o_li, o_br, o_bi):
        res = _ssm_discretise(lr[...], li[...], ld[...], br[...], bi[...])
        for o, v in zip((o_lr, o_li, o_br, o_bi), res):
            o[...] = v

    return pl.pallas_call(
        body, name="ssm_prep",
        out_shape=[jax.ShapeDtypeStruct(lam_re.shape, F32)] * 2 + [jax.ShapeDtypeStruct(b_re_t.shape, F32)] * 2,
    )(lam_re, lam_im, log_dt, b_re_t, b_im_t)


def _ssm_prep_bwd(lam_re, lam_im, log_dt, b_re_t, b_im_t, d_lr, d_li, d_br, d_bi):
    def body(lr, li, ld, br, bi, g_lr, g_li, g_br, g_bi, o_lr, o_li, o_ld, o_br, o_bi):
        _, vjp = jax.vjp(_ssm_discretise, lr[...], li[...], ld[...], br[...], bi[...])
        res = vjp((g_lr[...], g_li[...], g_br[...], g_bi[...]))
        for o, v in zip((o_lr, o_li, o_ld, o_br, o_bi), res):
            o[...] = v

    return pl.pallas_call(
        body, name="ssm_prep_bwd",
        out_shape=[jax.ShapeDtypeStruct(lam_re.shape, F32)] * 2 + [jax.ShapeDtypeStruct(log_dt.shape, F32)]
        + [jax.ShapeDtypeStruct(b_re_t.shape, F32)] * 2,
    )(lam_re, lam_im, log_dt, b_re_t, b_im_t, d_lr, d_li, d_br, d_bi)


def _block_diag(m):
    m4 = m.reshape(SSM_COLS, 8, SSM_GROUP, SSM_STATE)
    return jnp.einsum("aghp,gk->aghkp", m4, jnp.eye(8, dtype=m.dtype)).reshape(SSM_COLS, LANES, 512)


def _block_diag_take(d):
    d6 = d.reshape(SSM_COLS, 8, SSM_GROUP, 2, 8, SSM_STATE)
    return jnp.einsum("aghrgp->raghp", d6).reshape(2, SSM_GROUPS, SSM_GROUP, SSM_STATE)


def _cmul(ar, ai, br, bi):
    return ar * br - ai * bi, ar * bi + ai * br


def _power(lr, li, n):
    assert n & (n - 1) == 0
    for _ in range(n.bit_length() - 1):
        lr, li = _cmul(lr, li, lr, li)
    return lr, li


def _ssm_fwd(u_p, w_b, lam_r, lam_i, c_m, d_skip, B, L, tj):
    J = L // N_CHUNK
    njt = J // tj
    R = tj * N_CHUNK
    H = 512

    def body(u_ref, wb_ref, lr_ref, li_ref, cm_ref, d_ref, y_ref, gel_ref, x_ref, xin_ref, bu_s, st_s, xin_s):
        ph, jt = pl.program_id(2), pl.program_id(3)
        lr, li = lr_ref[...], li_ref[...]

        @pl.when(jnp.logical_and(ph == 0, jt == 0))
        def _():
            st_s[...] = jnp.zeros_like(st_s)

        @pl.when(ph == 0)
        def _():
            bu_s[jt] = _dot(u_ref[...].astype(BF16), wb_ref[...].astype(BF16))

        def scan(store):
            def step(j, carry):
                xr, xi = carry
                r = pl.ds(pl.multiple_of(j * N_CHUNK, N_CHUNK), N_CHUNK)
                nr = lr * xr - li * xi + bu_s[jt, r, 0:H]
                ni = lr * xi + li * xr + bu_s[jt, r, H:2 * H]
                if store:
                    x_ref[r, 0:H] = nr
                    x_ref[r, H:2 * H] = ni
                return nr, ni

            xr, xi = lax.fori_loop(0, tj, step, (st_s[:, 0:H], st_s[:, H:2 * H]))
            st_s[:, 0:H] = xr
            st_s[:, H:2 * H] = xi

        @pl.when(ph == 0)
        def _():
            scan(False)

            @pl.when(jt == njt - 1)
            def _():
                pr, pi = _power(lr[0:1], li[0:1], J)
                xin_s[0:1, :] = jnp.zeros((1, 2 * H), F32)
                for c in range(1, N_CHUNK):
                    qr, qi = _cmul(pr, pi, xin_s[c - 1:c, 0:H], xin_s[c - 1:c, H:2 * H])
                    xin_s[c:c + 1, 0:H] = qr + st_s[c - 1:c, 0:H]
                    xin_s[c:c + 1, H:2 * H] = qi + st_s[c - 1:c, H:2 * H]
                xin_ref[...] = xin_s[...]
                st_s[...] = xin_s[...]

        @pl.when(ph == 1)
        def _():
            scan(True)
            y = _dot(x_ref[...].astype(BF16), cm_ref[...].astype(BF16)) + d_ref[...] * u_ref[...]
            y_ref[...] = y
            gel_ref[...] = jax.nn.gelu(y).astype(gel_ref.dtype)

    return pl.pallas_call(
        body, name="ssm_fwd", grid=(SSM_COLS, B, 2, njt),
        in_specs=[
            pl.BlockSpec((None, R, LANES), lambda i, b, ph, jt: (b, jt, i)),
            pl.BlockSpec((None, LANES, 2 * H), lambda i, b, ph, jt: (i, 0, 0)),
            pl.BlockSpec((None, N_CHUNK, H), lambda i, b, ph, jt: (i, 0, 0)),
            pl.BlockSpec((None, N_CHUNK, H), lambda i, b, ph, jt: (i, 0, 0)),
            pl.BlockSpec((None, 2 * H, LANES), lambda i, b, ph, jt: (i, 0, 0)),
            pl.BlockSpec((1, LANES), lambda i, b, ph, jt: (0, i)),
        ],
        out_specs=[
            pl.BlockSpec((None, R, LANES), lambda i, b, ph, jt: (b, jt * ph, i)),
            pl.BlockSpec((None, R, LANES), lambda i, b, ph, jt: (b, jt * ph, i)),
            pl.BlockSpec((None, R, 2 * H), lambda i, b, ph, jt: (b, jt * ph, i)),
            pl.BlockSpec((None, None, N_CHUNK, 2 * H), lambda i, b, ph, jt: (b, i, 0, 0)),
        ],
        out_shape=[
            jax.ShapeDtypeStruct((B, L, SSM_WIDTH), F32),
            jax.ShapeDtypeStruct((B, L, SSM_WIDTH), BF16),
            jax.ShapeDtypeStruct((B, L, SSM_COLS * 2 * H), F32),
            jax.ShapeDtypeStruct((B, SSM_COLS, N_CHUNK, 2 * H), F32),
        ],
        scratch_shapes=[pltpu.VMEM((njt, R, 2 * H), F32), pltpu.VMEM((N_CHUNK, 2 * H), F32),
                        pltpu.VMEM((N_CHUNK, 2 * H), F32)],
        compiler_params=_params(("arbitrary",) * 4),
    )(u_p, w_b, lam_r, lam_i, c_m, d_skip)


def _ssm_bwd(dy_p, u_p, x, xin, w_bt, lam_r, lam_i, c_mt, d_skip, B, L, tj):
    J = L // N_CHUNK
    njt = J // tj
    R = tj * N_CHUNK
    H = 512
    x4 = x.reshape(B, J, N_CHUNK, SSM_COLS * 2 * H)

    def body(dy_ref, u_ref, x_ref, xp_ref, xin_ref, wbt_ref, lr_ref, li_ref, cmt_ref, d_ref,
             du_ref, dwb_ref, dcm_ref, dlr_ref, dli_ref, dd_ref, ca_s, a_s, st_s, dl_s):
        b, ph, jt = pl.program_id(1), pl.program_id(2), pl.program_id(3)
        jr = njt - 1 - jt
        lr, li = lr_ref[...], -li_ref[...]

        @pl.when(jnp.logical_and(b == 0, jnp.logical_and(ph == 0, jt == 0)))
        def _():
            dwb_ref[...] = jnp.zeros_like(dwb_ref)
            dcm_ref[...] = jnp.zeros_like(dcm_ref)
            dlr_ref[...] = jnp.zeros_like(dlr_ref)
            dli_ref[...] = jnp.zeros_like(dli_ref)
            dd_ref[...] = jnp.zeros_like(dd_ref)
            dl_s[...] = jnp.zeros_like(dl_s)

        @pl.when(jnp.logical_and(ph == 0, jt == 0))
        def _():
            st_s[...] = jnp.zeros_like(st_s)

        @pl.when(ph == 0)
        def _():
            ca_s[jt] = _dot(dy_ref[...].astype(BF16), cmt_ref[...].astype(BF16))

        def scan(store):
            def step(n, carry):
                ar, ai = carry
                r = pl.ds(pl.multiple_of((tj - 1 - n) * N_CHUNK, N_CHUNK), N_CHUNK)
                nr = lr * ar - li * ai + ca_s[jt, r, 0:H]
                ni = lr * ai + li * ar + ca_s[jt, r, H:2 * H]
                if store:
                    a_s[r, 0:H] = nr
                    a_s[r, H:2 * H] = ni
                return nr, ni

            ar, ai = lax.fori_loop(0, tj, step, (st_s[:, 0:H], st_s[:, H:2 * H]))
            st_s[:, 0:H] = ar
            st_s[:, H:2 * H] = ai

        @pl.when(ph == 0)
        def _():
            scan(False)

            @pl.when(jt == njt - 1)
            def _():
                pr, pi = _power(lr[0:1], li[0:1], J)
                a_s[N_CHUNK - 1:N_CHUNK, :] = jnp.zeros((1, 2 * H), F32)
                for c in range(N_CHUNK - 2, -1, -1):
                    qr, qi = _cmul(pr, pi, a_s[c + 1:c + 2, 0:H], a_s[c + 1:c + 2, H:2 * H])
                    a_s[c:c + 1, 0:H] = qr + st_s[c + 1:c + 2, 0:H]
                    a_s[c:c + 1, H:2 * H] = qi + st_s[c + 1:c + 2, H:2 * H]
                st_s[...] = a_s[0:N_CHUNK, :]

        @pl.when(ph == 1)
        def _():
            scan(True)
            dy = dy_ref[...]
            u = u_ref[...]
            a_b = a_s[...].astype(BF16)
            du_ref[...] = (_dot(a_b, wbt_ref[...].astype(BF16)) + d_ref[...] * dy).astype(du_ref.dtype)
            dwb_ref[...] += _dot(u.astype(BF16), a_b, _TN)
            dcm_ref[...] += _dot(x_ref[...].astype(BF16), dy.astype(BF16), _TN)
            dd_ref[...] += jnp.sum(dy * u, axis=0, keepdims=True)

            first = jnp.where(jr == 0, xin_ref[...], xp_ref[...])
            a0r, a0i = a_s[0:N_CHUNK, 0:H], a_s[0:N_CHUNK, H:2 * H]
            acc0 = (a0r * first[:, 0:H] + a0i * first[:, H:2 * H], a0i * first[:, 0:H] - a0r * first[:, H:2 * H])

            def step(j, carry):
                sr, si = carry
                r = pl.ds(pl.multiple_of(j * N_CHUNK, N_CHUNK), N_CHUNK)
                rp = pl.ds(pl.multiple_of((j - 1) * N_CHUNK, N_CHUNK), N_CHUNK)
                ar, ai = a_s[r, 0:H], a_s[r, H:2 * H]
                xr, xi = x_ref[rp, 0:H], x_ref[rp, H:2 * H]
                return sr + ar * xr + ai * xi, si + ai * xr - ar * xi

            sr, si = lax.fori_loop(1, tj, step, acc0)
            dl_s[:, 0:H] += sr
            dl_s[:, H:2 * H] += si

            @pl.when(jnp.logical_and(b == B - 1, jt == njt - 1))
            def _():
                dlr_ref[...] = jnp.sum(dl_s[:, 0:H], axis=0, keepdims=True)
                dli_ref[...] = jnp.sum(dl_s[:, H:2 * H], axis=0, keepdims=True)
                dl_s[...] = jnp.zeros_like(dl_s)

    rev = lambda ph, jt: (njt - 1 - jt) * ph + (njt - 1) * (1 - ph)
    return pl.pallas_call(
        body, name="ssm_bwd", grid=(SSM_COLS, B, 2, njt),
        in_specs=[
            pl.BlockSpec((None, R, LANES), lambda i, b, ph, jt: (b, njt - 1 - jt, i)),
            pl.BlockSpec((None, R, LANES), lambda i, b, ph, jt: (b, njt - 1 - jt, i)),
            pl.BlockSpec((None, R, 2 * H), lambda i, b, ph, jt: (b, rev(ph, jt), i)),
            pl.BlockSpec((None, None, N_CHUNK, 2 * H),
                         lambda i, b, ph, jt: (b, jnp.maximum((njt - 1 - jt) * tj - 1, 0), 0, i)),
            pl.BlockSpec((None, None, N_CHUNK, 2 * H), lambda i, b, ph, jt: (b, i, 0, 0)),
            pl.BlockSpec((None, 2 * H, LANES), lambda i, b, ph, jt: (i, 0, 0)),
            pl.BlockSpec((None, N_CHUNK, H), lambda i, b, ph, jt: (i, 0, 0)),
            pl.BlockSpec((None, N_CHUNK, H), lambda i, b, ph, jt: (i, 0, 0)),
            pl.BlockSpec((None, LANES, 2 * H), lambda i, b, ph, jt: (i, 0, 0)),
            pl.BlockSpec((1, LANES), lambda i, b, ph, jt: (0, i)),
        ],
        out_specs=[
            pl.BlockSpec((None, R, LANES), lambda i, b, ph, jt: (b, rev(ph, jt), i)),
            pl.BlockSpec((None, LANES, 2 * H), lambda i, b, ph, jt: (i, 0, 0)),
            pl.BlockSpec((None, 2 * H, LANES), lambda i, b, ph, jt: (i, 0, 0)),
            pl.BlockSpec((None, 1, H), lambda i, b, ph, jt: (i, 0, 0)),
            pl.BlockSpec((None, 1, H), lambda i, b, ph, jt: (i, 0, 0)),
            pl.BlockSpec((1, LANES), lambda i, b, ph, jt: (0, i)),
        ],
        out_shape=[
            jax.ShapeDtypeStruct((B, L, SSM_WIDTH), BF16),
            jax.ShapeDtypeStruct((SSM_COLS, LANES, 2 * H), F32),
            jax.ShapeDtypeStruct((SSM_COLS, 2 * H, LANES), F32),
            jax.ShapeDtypeStruct((SSM_COLS, 1, H), F32),
            jax.ShapeDtypeStruct((SSM_COLS, 1, H), F32),
            jax.ShapeDtypeStruct((1, SSM_WIDTH), F32),
        ],
        scratch_shapes=[pltpu.VMEM((njt, R, 2 * H), F32), pltpu.VMEM((R, 2 * H), F32),
                        pltpu.VMEM((N_CHUNK, 2 * H), F32), pltpu.VMEM((N_CHUNK, 2 * H), F32)],
        compiler_params=_params(("arbitrary",) * 4),
    )(dy_p, u_p, x, x4, xin, w_bt, lam_r, lam_i, c_mt, d_skip)


def _to_scan_layout(t, B, L):
    C = t.shape[-1]
    return t.reshape(B, N_CHUNK, L // N_CHUNK, C).transpose(0, 2, 1, 3).reshape(B, L, C)


def _from_scan_layout(t, B, L):
    C = t.shape[-1]
    return t.reshape(B, L // N_CHUNK, N_CHUNK, C).transpose(0, 2, 1, 3).reshape(B * L, C)


def _local_step(x, target, p, w_in, late_weights, mlp_grads_ready=None, rest_grads_ready=None, order=None, *,
                ssm_tile=128):
    B, L, D = x.shape
    T = B * L
    x2 = x.reshape(T, D)
    row = lambda v: v.reshape(1, -1)
    g1, g2, ga, gs, b_glu = row(p["norm1_g"]), row(p["norm2_g"]), row(p["attn_out_g"]), row(p["ssm_out_g"]), row(p["b_glu"])
    g1_first = g1 if order is None else g1 + order
    gq8 = jnp.tile(row(p["q_norm_g"]), (1, N_HEADS))
    gk8 = jnp.tile(row(p["k_norm_g"]), (1, N_HEADS))

    G, P, Hh = SSM_GROUPS, SSM_STATE, SSM_GROUP
    lam_re3, lam_im3 = p["ssm_lambda_re"].reshape(G, 1, P), p["ssm_lambda_im"].reshape(G, 1, P)
    log_dt3 = p["ssm_log_dt"].reshape(G, 1, 1)
    b_re_t, b_im_t = p["ssm_b_re"].transpose(0, 2, 1), p["ssm_b_im"].transpose(0, 2, 1)
    lbr, lbi, bbr, bbi = _ssm_prep(lam_re3, lam_im3, log_dt3, b_re_t, b_im_t)
    w_b = jnp.concatenate([_block_diag(bbr), _block_diag(bbi)], axis=2)
    c_mt = jnp.concatenate([_block_diag(p["ssm_c_re"]), -_block_diag(p["ssm_c_im"])], axis=2)
    w_bt, c_m = w_b.transpose(0, 2, 1), c_mt.transpose(0, 2, 1)
    lam_r = jnp.broadcast_to(lbr.reshape(SSM_COLS, 1, 512), (SSM_COLS, N_CHUNK, 512))
    lam_i = jnp.broadcast_to(lbi.reshape(SSM_COLS, 1, 512), (SSM_COLS, N_CHUNK, 512))
    d_skip = p["ssm_d"].reshape(1, SSM_WIDTH)

    xn = _rowwise("norm1", _rms, [x2], [g1_first], [(D, BF16)])
    proj = _matmul("proj", xn, w_in, tn=1024)
    qn, kn, vb = _qk_norm(proj, gq8, gk8)
    sb, attn_kept = _attn_fwd(qn, kn, vb, B, L)
    u_p = _to_scan_layout(proj[:, 3 * SB_WIDTH:], B, L)
    y_p, gel_p, xs, xin = _ssm_fwd(u_p, w_b, lam_r, lam_i, c_m, d_skip, B, L, ssm_tile)
    y2, gel = y_p.reshape(T, SSM_WIDTH), gel_p.reshape(T, SSM_WIDTH)
    w_glu, w_out, w_mlp_in, w_mlp_out = late_weights(gel)
    pre = _matmul("glu_gate", gel, w_glu)
    ssm_n = _rowwise("glu_out", _glu_branch, [y2, pre], [b_glu, gs], [(SSM_WIDTH, BF16)])
    sb_n = _rowwise("attn_out_norm", _rms, [sb], [ga], [(SB_WIDTH, BF16)])
    mixed = jnp.concatenate([sb_n, _from_scan_layout(ssm_n, B, L)], axis=1)
    def residual_and_norm(acc, res, g):
        h = acc + res
        return h, _rms(h, g)

    h1, hn = _matmul("out_proj", mixed, w_out, extras=[x2, g2], out_dtypes=(F32, BF16), tn=D,
                     epilogue=residual_and_norm)
    act, a_pre = _matmul("mlp_in", hn, w_mlp_in, out_dtypes=(BF16, BF16), tn=1024,
                         epilogue=lambda acc: (jnp.square(jnp.maximum(acc, 0.0)), acc))

    def loss_fn(acc, h, t):
        diff = acc + h - t
        part = jnp.sum(jnp.sum(diff * diff, axis=0, keepdims=True), axis=1, keepdims=True)
        d = diff * (1.0 / D)
        return d, d, part * (0.5 / D)

    d_out, d_out_b, loss = _matmul("mlp_out", act, w_mlp_out, extras=[h1, target.reshape(T, D)],
                                   out_dtypes=(F32, BF16), sums=[(1, 1)], epilogue=loss_fn, tm=512, tn=D)

    d_apre = _matmul("mlp_out_dx", d_out_b, w_mlp_out, tb=True, extras=[a_pre], out_dtypes=(BF16,), tn=1024,
                     epilogue=lambda acc, ap: (acc * (2.0 * jnp.maximum(ap.astype(F32), 0.0)),))
    both = lambda acc: (acc, acc)
    g_w_mlp_out, g_w_mlp_out_b = _matmul("mlp_out_dw", act, d_out_b, ta=True, out_dtypes=(F32, BF16), epilogue=both)
    g_w_mlp_in, g_w_mlp_in_b = _matmul("mlp_in_dw", hn, d_apre, ta=True, col_blocked=True, out_dtypes=(F32, BF16),
                                       epilogue=both, tn=w_mlp_in.shape[1] // N_DEV)
    if mlp_grads_ready is not None:
        g2 = g2 + mlp_grads_ready(g_w_mlp_out, g_w_mlp_out_b, g_w_mlp_in, g_w_mlp_in_b)

    def norm_bwd_res(dy, h, res, g):
        _, vjp = jax.vjp(_rms, h, g)
        dh, dg = vjp(dy)
        return res + dh, dg

    def norm_bwd_res2(dy, h, res, g):
        d, dg = norm_bwd_res(dy, h, res, g)
        return d, d, dg

    d_h1, d_h1_b, g_norm2 = _matmul("mlp_in_dx", d_apre, w_mlp_in, tb=True, extras=[h1, d_out, g2],
                                    out_dtypes=(F32, BF16), sums=[(1, D)], epilogue=norm_bwd_res2, tm=512, tn=D)

    d_mixed = _matmul("out_proj_dx", d_h1_b, w_out, tb=True, tn=1024)
    g_w_out, g_w_out_b = _matmul("out_proj_dw", mixed, d_h1_b, ta=True, out_dtypes=(F32, BF16), epilogue=both)

    def norm_bwd(h, dy, g):
        _, vjp = jax.vjp(_rms, h, g)
        return vjp(dy)

    d_sb, g_attn_out = _rowwise("attn_out_norm_bwd", norm_bwd, [sb, (d_mixed, 0, SB_WIDTH)], [ga],
                                [(SB_WIDTH, F32)], sums=[(1, SB_WIDTH)])
    d_ssm_n = _to_scan_layout(d_mixed[:, SB_WIDTH:], B, L).reshape(T, SSM_WIDTH)

    def glu_bwd(y, pre_, dy, bg, g):
        _, vjp = jax.vjp(_glu_branch, y, pre_, bg, g)
        d_y, d_pre, d_bg, d_g = vjp(dy)
        return d_y, d_pre, d_bg, d_g

    d_y_direct, d_pre, g_b_glu, g_ssm_out = _rowwise(
        "glu_out_bwd", glu_bwd, [y2, pre, d_ssm_n], [b_glu, gs], [(SSM_WIDTH, F32), (SSM_WIDTH, BF16)],
        sums=[(1, SSM_WIDTH), (1, SSM_WIDTH)])
    g_w_glu, g_w_glu_b = _matmul("glu_gate_dw", gel, d_pre, ta=True, out_dtypes=(F32, BF16), epilogue=both)

    def gelu_bwd(dg, y, dy0):
        _, vjp = jax.vjp(jax.nn.gelu, y)
        return (dy0 + vjp(dg)[0],)

    d_y = _matmul("glu_gate_dx", d_pre, w_glu, tb=True, extras=[y2, d_y_direct], epilogue=gelu_bwd)

    du_p, d_wb, d_cm, d_lr, d_li, g_d = _ssm_bwd(
        d_y.reshape(B, L, SSM_WIDTH), u_p, xs, xin, w_bt, lam_r, lam_i, c_mt, d_skip, B, L, ssm_tile)
    d_bb = _block_diag_take(d_wb.reshape(SSM_COLS, LANES, 2, 512))
    d_c = _block_diag_take(d_cm.transpose(0, 2, 1).reshape(SSM_COLS, LANES, 2, 512))
    g_lam_re, g_lam_im, g_log_dt, g_b_re_t, g_b_im_t = _ssm_prep_bwd(
        lam_re3, lam_im3, log_dt3, b_re_t, b_im_t,
        d_lr.reshape(G, 1, P), d_li.reshape(G, 1, P), d_bb[0], d_bb[1])
    d_qn, d_kn, d_v = _attn_bwd(qn, kn, vb, attn_kept, d_sb, B, L)
    d_q, d_k, g_q, g_k = _qk_norm_bwd(proj, gq8, gk8, d_qn, d_kn)

    d_proj = jnp.concatenate([d_q, d_k, d_v, _from_scan_layout(du_p, B, L)], axis=1)
    g_w_in, g_w_in_b = _matmul("proj_dw", xn, d_proj, ta=True, col_blocked=True, out_dtypes=(F32, BF16),
                               epilogue=both, tn=w_in.shape[1] // N_DEV)
    if rest_grads_ready is not None:
        g1 = g1 + rest_grads_ready([g_w_in, g_w_glu, g_w_out], [g_w_in_b, g_w_glu_b, g_w_out_b])
    grad_x, g_norm1 = _matmul("proj_dx", d_proj, w_in, tb=True, extras=[x2, d_h1, g1], sums=[(1, D)],
                              epilogue=norm_bwd_res, tm=512, tn=D)

    small = {
        "norm1_g": g_norm1.reshape(-1),
        "q_norm_g": g_q.reshape(-1),
        "k_norm_g": g_k.reshape(-1),
        "ssm_lambda_re": g_lam_re.reshape(G, P),
        "ssm_lambda_im": g_lam_im.reshape(G, P),
        "ssm_log_dt": g_log_dt.reshape(G),
        "ssm_b_re": g_b_re_t.transpose(0, 2, 1),
        "ssm_b_im": g_b_im_t.transpose(0, 2, 1),
        "ssm_c_re": d_c[0],
        "ssm_c_im": -d_c[1],
        "ssm_d": g_d.reshape(G, Hh),
        "b_glu": g_b_glu.reshape(-1),
        "attn_out_g": g_attn_out.reshape(-1),
        "ssm_out_g": g_ssm_out.reshape(-1),
        "norm2_g": g_norm2.reshape(-1),
    }
    big = {"w_in": g_w_in, "w_glu": g_w_glu, "w_out": g_w_out, "w_mlp_in": g_w_mlp_in, "w_mlp_out": g_w_mlp_out}
    return loss[0, 0], grad_x.reshape(B, L, D), small, big


_ANY = pl.BlockSpec(memory_space=pl.ANY)
_MESH = pl.DeviceIdType.MESH


def _all_gather(name, shards):
    n = len(shards)

    def body(*refs):
        in_refs, out_refs = refs[:n], refs[n:2 * n]
        send_sems, recv_sems, local_sems = refs[2 * n:]
        x, y, c = lax.axis_index("x"), lax.axis_index("y"), lax.axis_index("c")
        me, sibling = (x, y, c), (x, y, 1 - c)
        chips = [(1 - x, y), (x, 1 - y), (1 - x, 1 - y)]

        def copy(a, k, block, to, src=None):
            px, py, pc = block
            rows = out_refs[a].at[4 * px + 2 * py + pc]
            return pltpu.make_async_remote_copy(
                src_ref=rows if src is None else src, dst_ref=rows, send_sem=send_sems.at[a, k],
                recv_sem=recv_sems.at[a, k], device_id=to, device_id_type=_MESH)

        mine = [pltpu.make_async_copy(in_refs[a], out_refs[a].at[4 * x + 2 * y + c], local_sems.at[a]) for a in range(n)]
        first, passed = [], []
        for a in range(n):
            mine[a].start()
            first.append(copy(a, 0, me, sibling, src=in_refs[a]))
            first += [copy(a, 1 + j, me, (*chip, c), src=in_refs[a]) for j, chip in enumerate(chips)]
        for cp in first:
            cp.start()
        for j, chip in enumerate(chips):
            for a in range(n):
                copy(a, 1 + j, (*chip, c), me).wait_recv()
                fwd = copy(a, 4 + j, (*chip, c), sibling)
                fwd.start()
                passed.append(fwd)
        for a in range(n):
            copy(a, 0, sibling, me).wait_recv()
            for j, chip in enumerate(chips):
                copy(a, 4 + j, (*chip, 1 - c), me).wait_recv()
        for cp in first + passed:
            cp.wait_send()
        for cp in mine:
            cp.wait()

    return pl.pallas_call(
        body, name=name,
        in_specs=[_ANY] * n, out_specs=[_ANY] * n,
        out_shape=[jax.ShapeDtypeStruct((N_DEV, *s.shape), s.dtype) for s in shards],
        scratch_shapes=[pltpu.SemaphoreType.DMA((n, 7)), pltpu.SemaphoreType.DMA((n, 7)), pltpu.SemaphoreType.DMA((n,))],
    )(*shards)


_HBM = pl.BlockSpec(memory_space=pltpu.HBM)
_SEM = pl.BlockSpec(memory_space=pltpu.SEMAPHORE)
_EFFECT = pltpu.SideEffectType.DATAFLOW_SIDE_EFFECTING
_FLIPS = [(dx, dy, dc) for dx in (0, 1) for dy in (0, 1) for dc in (0, 1) if (dx, dy, dc) != (0, 0, 0)]


def _exchange_start(name, srcs, lands, per_peer):
    n = len(srcs)

    def body(*refs):
        src_refs, land_refs = refs[:n], refs[n:2 * n]
        send_sems, recv_sems = refs[2 * n:3 * n], refs[3 * n:4 * n]
        token = refs[-1]
        x, y, c = lax.axis_index("x"), lax.axis_index("y"), lax.axis_index("c")
        me = 4 * x + 2 * y + c
        for dx, dy, dc in _FLIPS:
            px, py, pc = (1 - x if dx else x), (1 - y if dy else y), (1 - c if dc else c)
            for a in range(n):
                pltpu.make_async_remote_copy(
                    src_ref=src_refs[a].at[4 * px + 2 * py + pc] if per_peer else src_refs[a],
                    dst_ref=land_refs[a].at[me], send_sem=send_sems[a], recv_sem=recv_sems[a],
                    device_id=(px, py, pc), device_id_type=_MESH).start()
        token[...] = jnp.zeros_like(token)

    hbm = lambda t: pltpu.with_memory_space_constraint(t, pltpu.HBM)
    res = pl.pallas_call(
        body, name=name,
        out_shape=(*[pltpu.SemaphoreType.DMA(())] * (2 * n), *[pltpu.HBM(t.shape, t.dtype) for t in (*srcs, *lands)],
                   jax.ShapeDtypeStruct((8, LANES), F32)),
        in_specs=[_HBM] * (2 * n),
        out_specs=(*[_SEM] * (2 * n), *[_HBM] * (2 * n), pl.BlockSpec(memory_space=pltpu.VMEM)),
        input_output_aliases={i: 2 * n + i for i in range(2 * n)},
        compiler_params=pltpu.CompilerParams(has_side_effects=_EFFECT),
    )(*[hbm(t) for t in (*srcs, *lands)])
    return res[:-1], res[-1]


def _exchange_wait(name, handle, after):
    n = len(handle) // 4
    sems, thru = handle[:2 * n], handle[2 * n:]

    def body(*refs):
        land_refs = refs[n:2 * n]
        send_sems, recv_sems = refs[2 * n:3 * n], refs[3 * n:4 * n]
        me = (lax.axis_index("x"), lax.axis_index("y"), lax.axis_index("c"))
        for a in range(n):
            seven = land_refs[a].at[pl.ds(0, len(_FLIPS))]
            all_copies = pltpu.make_async_remote_copy(
                src_ref=seven, dst_ref=seven, send_sem=send_sems[a], recv_sem=recv_sems[a], device_id=me,
                device_id_type=_MESH)
            all_copies.wait_send()
            all_copies.wait_recv()

    res = pl.pallas_call(
        body, name=name, out_shape=tuple(pltpu.HBM(t.shape, t.dtype) for t in thru),
        in_specs=[*[_HBM] * (2 * n), *[_SEM] * (2 * n), _ANY], out_specs=tuple([_HBM] * (2 * n)),
        input_output_aliases={i: i for i in range(2 * n)},
        compiler_params=pltpu.CompilerParams(has_side_effects=_EFFECT),
    )(*thru, *sems, after)
    return res[n:]


def _adamw_gathered(name, own, parts, me, w, m, v):
    r, c = w.shape
    tr = min(r, 256)

    def body(me_ref, own_ref, p_ref, w_ref, m_ref, v_ref, g_out, d_out, m_out, v_out):
        g = own_ref[...]
        for j in range(N_DEV):
            g = g + p_ref[j].astype(F32)
        delta, m_new, v_new = _adamw(w_ref[...], g, m_ref[...], v_ref[...])
        g_out[...] = g
        d_out[...] = delta
        m_out[...] = m_new
        v_out[...] = v_new

    spec = pl.BlockSpec((tr, c), lambda i, me_ref: (i, 0))
    return pl.pallas_call(
        body, name=name,
        grid_spec=pltpu.PrefetchScalarGridSpec(
            num_scalar_prefetch=1, grid=(r // tr,),
            in_specs=[pl.BlockSpec((None, tr, c), lambda i, me_ref: (me_ref[0], i, 0)),
                      pl.BlockSpec((N_DEV, tr, c), lambda i, me_ref: (0, i, 0)), spec, spec, spec],
            out_specs=[spec] * 4),
        out_shape=[jax.ShapeDtypeStruct((r, c), F32)] * 4,
        compiler_params=_params(("parallel",)),
    )(me, own, parts, w, m, v)


def _adamw(w, g, m, v):
    m = ADAM_B1 * m + (1.0 - ADAM_B1) * g
    v = ADAM_B2 * v + (1.0 - ADAM_B2) * jnp.square(g)
    m_hat = m / (1.0 - ADAM_B1 ** ADAM_STEP)
    v_hat = v / (1.0 - ADAM_B2 ** ADAM_STEP)
    delta = -ADAM_LR * (m_hat / (jnp.sqrt(v_hat) + ADAM_EPS) + ADAM_WD * w)
    return delta, m, v


def _adamw_small(name, parts, w, m, v):
    _, r, c = parts.shape
    tr = 8

    def body(p_ref, w_ref, m_ref, v_ref, g_out, d_out, m_out, v_out):
        g = p_ref[0]
        for j in range(1, N_DEV):
            g = g + p_ref[j]
        delta, m_new, v_new = _adamw(w_ref[...], g, m_ref[...], v_ref[...])
        g_out[...] = g
        d_out[...] = delta
        m_out[...] = m_new
        v_out[...] = v_new

    spec = pl.BlockSpec((tr, c), lambda i: (i, 0))
    return pl.pallas_call(
        body, name=name, grid=(r // tr,),
        in_specs=[pl.BlockSpec((N_DEV, tr, c), lambda i: (0, i, 0)), spec, spec, spec],
        out_specs=[spec] * 4, out_shape=[jax.ShapeDtypeStruct((r, c), F32)] * 4,
        compiler_params=_params(("parallel",)),
    )(parts, w, m, v)


_WEIGHTS = ["norm1_g", "w_in", "q_norm_g", "k_norm_g", "ssm_lambda_re", "ssm_lambda_im", "ssm_log_dt", "ssm_b_re",
            "ssm_b_im", "ssm_c_re", "ssm_c_im", "ssm_d", "w_glu", "b_glu", "attn_out_g", "ssm_out_g", "w_out",
            "norm2_g", "w_mlp_in", "w_mlp_out"]
_BIG = ["w_in", "w_glu", "w_out", "w_mlp_in", "w_mlp_out"]
_SMALL = [n for n in _WEIGHTS if n not in _BIG]
_PACK_COLS = 1024


def _pack(tree, last=None):
    flat = [tree[n].reshape(-1).astype(F32) for n in _SMALL]
    size = sum(f.shape[0] for f in flat)
    rows = -(-(size + 1) // (_PACK_COLS * 8)) * 8
    pad = jnp.zeros((rows * _PACK_COLS - size - 1,), F32)
    tail = jnp.zeros((1,), F32) if last is None else last.reshape(1).astype(F32)
    return jnp.concatenate(flat + [pad, tail]).reshape(rows, _PACK_COLS)


def _unpack(buf, like):
    flat, out, off = buf.reshape(-1), {}, 0
    for n in _SMALL:
        size = like[n].size
        out[n] = flat[off:off + size].reshape(like[n].shape)
        off += size
    return out


def kernel(x, norm1_g, w_in, q_norm_g, k_norm_g, ssm_lambda_re, ssm_lambda_im, ssm_log_dt, ssm_b_re, ssm_b_im, ssm_c_re, ssm_c_im, ssm_d, w_glu, b_glu, attn_out_g, ssm_out_g, w_out, norm2_g, w_mlp_in, w_mlp_out, loss_target, m_norm1_g, m_w_in, m_q_norm_g, m_k_norm_g, m_ssm_lambda_re, m_ssm_lambda_im, m_ssm_log_dt, m_ssm_b_re, m_ssm_b_im, m_ssm_c_re, m_ssm_c_im, m_ssm_d, m_w_glu, m_b_glu, m_attn_out_g, m_ssm_out_g, m_w_out, m_norm2_g, m_w_mlp_in, m_w_mlp_out, v_norm1_g, v_w_in, v_q_norm_g, v_k_norm_g, v_ssm_lambda_re, v_ssm_lambda_im, v_ssm_log_dt, v_ssm_b_re, v_ssm_b_im, v_ssm_c_re, v_ssm_c_im, v_ssm_d, v_w_glu, v_b_glu, v_attn_out_g, v_ssm_out_g, v_w_out, v_norm2_g, v_w_mlp_in, v_w_mlp_out):
    w = dict(zip(_WEIGHTS, (norm1_g, w_in, q_norm_g, k_norm_g, ssm_lambda_re, ssm_lambda_im, ssm_log_dt, ssm_b_re, ssm_b_im, ssm_c_re, ssm_c_im, ssm_d, w_glu, b_glu, attn_out_g, ssm_out_g, w_out, norm2_g, w_mlp_in, w_mlp_out)))
    m = dict(zip(_WEIGHTS, (m_norm1_g, m_w_in, m_q_norm_g, m_k_norm_g, m_ssm_lambda_re, m_ssm_lambda_im, m_ssm_log_dt, m_ssm_b_re, m_ssm_b_im, m_ssm_c_re, m_ssm_c_im, m_ssm_d, m_w_glu, m_b_glu, m_attn_out_g, m_ssm_out_g, m_w_out, m_norm2_g, m_w_mlp_in, m_w_mlp_out)))
    v = dict(zip(_WEIGHTS, (v_norm1_g, v_w_in, v_q_norm_g, v_k_norm_g, v_ssm_lambda_re, v_ssm_lambda_im, v_ssm_log_dt, v_ssm_b_re, v_ssm_b_im, v_ssm_c_re, v_ssm_c_im, v_ssm_d, v_w_glu, v_b_glu, v_attn_out_g, v_ssm_out_g, v_w_out, v_norm2_g, v_w_mlp_in, v_w_mlp_out)))
    core = lax.axis_index("c").astype(jnp.int32).reshape(1)
    chip = (2 * lax.axis_index("x") + lax.axis_index("y")).astype(jnp.int32).reshape(1)

    me = (2 * chip + core).astype(jnp.int32)

    def landing(own=None, like=None):
        own = jnp.zeros_like(like) if own is None else own
        return lax.dynamic_update_slice(lax.empty((N_DEV, *like.shape), like.dtype), own[None], (me[0], 0, 0))

    (w_in_blocks,) = _all_gather("w_in_all_gather", [w_in.astype(BF16)])
    w_in_full = w_in_blocks.transpose(1, 0, 2).reshape(w_in.shape[0], -1)
    late = [n for n in _BIG if n != "w_in"]
    shards = [w[n].astype(BF16) for n in late]
    w_in_blocks, shards = lax.optimization_barrier((w_in_blocks, shards))
    weights_handle, weights_token = _exchange_start(
        "weights_send", shards, [landing(s, s) for s in shards], per_peer=False)

    def late_weights(after):
        got = dict(zip(late, _exchange_wait("weights_arrive", weights_handle, after)))
        return (got["w_glu"].reshape(-1, w_glu.shape[1]), got["w_out"].reshape(-1, w_out.shape[1]),
                got["w_mlp_in"].transpose(1, 0, 2).reshape(w_mlp_in.shape[0], -1),
                got["w_mlp_out"].reshape(-1, w_mlp_out.shape[1]))

    mlp = ["w_mlp_out", "w_mlp_in"]
    sent = {}

    def send_grads(name, names, own, own_b):
        blocks = lambda g, n: g.reshape(N_DEV, *w[n].shape)
        sent[name + "_own"] = [blocks(g, n) for g, n in zip(own, names)]
        srcs = [blocks(g, n) for g, n in zip(own_b, names)]
        sent[name], token = _exchange_start(name, srcs, [landing(like=s[0]) for s in srcs], per_peer=True)
        return token[0, 0]

    def mlp_grads_ready(g_out, g_out_b, g_in, g_in_b):
        return send_grads("mlp_grads_send", mlp, [g_out, g_in], [g_out_b, g_in_b])

    rest = ["w_in", "w_glu", "w_out"]

    def rest_grads_ready(own, own_b):
        return send_grads("rest_grads_send", rest, own, own_b)

    loss_local, grad_x, g_small, g_big = _local_step(
        x, loss_target, {n: w[n] for n in _SMALL}, w_in_full, late_weights, mlp_grads_ready, rest_grads_ready,
        weights_token[0, 0])

    grads, delta, new_m, new_v = {}, {}, {}, {}
    small = _pack(g_small, last=loss_local)
    small_handle, small_token = _exchange_start("small_grads_send", [small], [landing(small, small)], per_peer=False)

    for send, arrive, names in (("mlp_grads_send", "mlp_grads_arrive", mlp),
                                ("rest_grads_send", "rest_grads_arrive", rest)):
        for n, own, part in zip(names, sent[send + "_own"], _exchange_wait(arrive, sent[send], small_token)):
            grads[n], delta[n], new_m[n], new_v[n] = _adamw_gathered("adamw_" + n, own, part, me, w[n], m[n], v[n])

    shards_done = lax.optimization_barrier(tuple(new_v[n] for n in _BIG))
    (small_parts,) = _exchange_wait("small_grads_arrive", small_handle, shards_done[-1])
    packed = _adamw_small("adamw_small", small_parts, _pack(w), _pack(m), _pack(v))
    for tree, buf in zip((grads, delta, new_m, new_v), packed):
        tree.update(_unpack(buf, w))
    loss = packed[0][-1, -1]

    return (loss, grad_x, *[grads[n] for n in _WEIGHTS], *[delta[n] for n in _WEIGHTS],
            *[new_m[n] for n in _WEIGHTS], *[new_v[n] for n in _WEIGHTS])
```

```python
import functools
import math

import jax
import jax.numpy as jnp
from jax import lax
from jax.experimental import pallas as pl
from jax.experimental.pallas import tpu as pltpu

F32 = jnp.float32
BF16 = jnp.bfloat16

EPS = 1e-6
HEAD_DIM = 64
N_HEADS = 8
SB_WIDTH = 512
SSM_WIDTH = 512
SSM_GROUP = 16
SSM_GROUPS = 32
SSM_STATE = 64
QBLOCK = 128
KBLOCK = 256
N_CHUNK = 8
SSM_COLS = 4
LANES = 128
N_DEV = 8

ADAM_LR = 0.001
ADAM_B1 = 0.9
ADAM_B2 = 0.999
ADAM_EPS = 1e-08
ADAM_WD = 0.01
ADAM_STEP = 10

VMEM_LIMIT = 56 * 1024 * 1024

_NT = (((1,), (1,)), ((), ()))
_NN = (((1,), (0,)), ((), ()))
_TN = (((0,), (0,)), ((), ()))


def _dot(a, b, dims=_NN):
    return lax.dot_general(a, b, dims, preferred_element_type=F32)


def _params(sem):
    return pltpu.CompilerParams(dimension_semantics=sem, vmem_limit_bytes=VMEM_LIMIT)


def _matmul(name, a, b, *, ta=False, tb=False, extras=(), epilogue=None, out_dtypes=(F32,), sums=(),
            col_blocked=False, tm=1024, tn=512, tk=4096):
    M, K = (a.shape[1], a.shape[0]) if ta else a.shape
    N = b.shape[0] if tb else b.shape[1]
    tm, tn, tk = min(tm, M), min(tn, N), min(tk, K)
    assert M % tm == 0 and N % tn == 0 and K % tk == 0, (name, M, N, K)
    assert not sums or (tn == N and tk == K), name
    nk = K // tk
    n_ex, n_out, n_sum = len(extras), len(out_dtypes), len(sums)
    dims = (((0 if ta else 1,), (1 if tb else 0,)), ((), ()))

    def body(*refs):
        a_ref, b_ref = refs[0], refs[1]
        ex_refs = refs[2:2 + n_ex]
        o_refs = refs[2 + n_ex:2 + n_ex + n_out]
        s_refs = refs[2 + n_ex + n_out:2 + n_ex + n_out + n_sum]
        k = pl.program_id(2)
        part = _dot(a_ref[...].astype(BF16), b_ref[...].astype(BF16), dims)

        def finish(acc):
            outs = (acc,) if epilogue is None else epilogue(acc, *[e[...] for e in ex_refs])
            for o_ref, o in zip(o_refs, outs[:n_out]):
                o_ref[...] = o.astype(o_ref.dtype)
            if n_sum:
                @pl.when(pl.program_id(0) == 0)
                def _():
                    for s_ref in s_refs:
                        s_ref[...] = jnp.zeros_like(s_ref)

                for s_ref, v in zip(s_refs, outs[n_out:]):
                    s_ref[...] += v

        if nk == 1:
            finish(part)
        else:
            acc_ref = refs[-1]

            @pl.when(k == 0)
            def _():
                acc_ref[...] = part

            @pl.when(jnp.logical_and(k > 0, k < nk - 1))
            def _():
                acc_ref[...] += part

            @pl.when(k == nk - 1)
            def _():
                finish(acc_ref[...] + part)

    a_spec = pl.BlockSpec((tk, tm), lambda i, j, k: (k, i)) if ta else pl.BlockSpec((tm, tk), lambda i, j, k: (i, k))
    b_spec = pl.BlockSpec((tn, tk), lambda i, j, k: (j, k)) if tb else pl.BlockSpec((tk, tn), lambda i, j, k: (k, j))
    ex_specs = [pl.BlockSpec((1, tn), lambda i, j, k: (0, j)) if e.shape[0] == 1 else
                pl.BlockSpec((tm, tn), lambda i, j, k: (i, j)) for e in extras]
    if col_blocked:
        out_specs = [pl.BlockSpec((None, tm, tn), lambda i, j, k: (j, i, 0)) for _ in out_dtypes]
        out_shape = [jax.ShapeDtypeStruct((N // tn, M, tn), dt) for dt in out_dtypes]
    else:
        out_specs = [pl.BlockSpec((tm, tn), lambda i, j, k: (i, j)) for _ in out_dtypes]
        out_shape = [jax.ShapeDtypeStruct((M, N), dt) for dt in out_dtypes]
    out_specs += [pl.BlockSpec(s, lambda i, j, k: (0, 0)) for s in sums]
    out_shape += [jax.ShapeDtypeStruct(s, F32) for s in sums]
    outs = pl.pallas_call(
        body, name=name, grid=(M // tm, N // tn, nk),
        in_specs=[a_spec, b_spec, *ex_specs], out_specs=out_specs, out_shape=out_shape,
        scratch_shapes=[pltpu.VMEM((tm, tn), F32)] if nk > 1 else [],
        compiler_params=_params(("arbitrary",) * 3 if sums else ("parallel", "parallel", "arbitrary")),
    )(a, b, *extras)
    return outs[0] if len(outs) == 1 else outs


def _rowwise(name, fn, rows, small, outs, sums=(), tile=256):
    specs, args = [], []
    T = None
    for r in rows:
        arr, cb, w = r if isinstance(r, tuple) else (r, 0, r.shape[1])
        T = arr.shape[0]
        specs.append((w, cb))
        args.append(arr)
    tile = min(tile, T)
    assert T % tile == 0
    n_r, n_s, n_o, n_a = len(rows), len(small), len(outs), len(sums)

    def body(*refs):
        r_refs = refs[:n_r]
        s_refs = refs[n_r:n_r + n_s]
        o_refs = refs[n_r + n_s:n_r + n_s + n_o]
        a_refs = refs[n_r + n_s + n_o:]
        res = fn(*[r[...] for r in r_refs], *[s[...] for s in s_refs])
        res = res if isinstance(res, (tuple, list)) else (res,)
        for o_ref, o in zip(o_refs, res[:n_o]):
            o_ref[...] = o.astype(o_ref.dtype)

        @pl.when(pl.program_id(0) == 0)
        def _():
            for a_ref in a_refs:
                a_ref[...] = jnp.zeros_like(a_ref)

        for a_ref, v in zip(a_refs, res[n_o:]):
            a_ref[...] += v.astype(F32)

    in_specs = [pl.BlockSpec((tile, w), functools.partial(lambda i, cb: (i, cb), cb=cb)) for w, cb in specs]
    in_specs += [pl.BlockSpec(s.shape, functools.partial(lambda i, nd: (0,) * nd, nd=s.ndim)) for s in small]
    out_specs = [pl.BlockSpec((tile, w), lambda i: (i, 0)) for w, _ in outs]
    out_specs += [pl.BlockSpec(s, functools.partial(lambda i, nd: (0,) * nd, nd=len(s))) for s in sums]
    out_shape = [jax.ShapeDtypeStruct((T, w), dt) for w, dt in outs]
    out_shape += [jax.ShapeDtypeStruct(s, F32) for s in sums]
    res = pl.pallas_call(
        body, name=name, grid=(T // tile,), in_specs=in_specs, out_specs=out_specs, out_shape=out_shape,
        compiler_params=_params(("arbitrary",)),
    )(*args, *small)
    return res[0] if len(res) == 1 else res


def _rms(x, g):
    return x * lax.rsqrt(jnp.mean(x * x, axis=-1, keepdims=True) + EPS) * g


def _glu_branch(y, pre, b_glu, g_out):
    g = jax.nn.gelu(y)
    return _rms(g * jax.nn.sigmoid(pre + b_glu), g_out)


def _split_dot(x, tri_bf):
    hi = x.astype(BF16)
    lo = (x - hi.astype(F32)).astype(BF16)
    return _dot(hi, tri_bf) + _dot(lo, tri_bf)


def _softplus(z):
    return jnp.maximum(z, 0.0) + jnp.log(1.0 + jnp.exp(-jnp.abs(z)))


def _head(h):
    return slice(h * HEAD_DIM, (h + 1) * HEAD_DIM)


def _head_mean(x, seg):
    return _split_dot(x, seg) * (1.0 / HEAD_DIM)


def _qk_norm(proj, gq, gk):
    scale = 1.0 / math.sqrt(HEAD_DIM)
    idx = jnp.arange(SB_WIDTH) // HEAD_DIM
    seg = (idx[:, None] == idx[None, :]).astype(BF16)

    def fn(q, k, v, gq_, gk_, seg_):
        qn = q * lax.rsqrt(_head_mean(q * q, seg_) + EPS) * (gq_ * scale)
        kn = k * lax.rsqrt(_head_mean(k * k, seg_) + EPS) * gk_
        return qn, kn, v

    return _rowwise("qk_norm", fn, [(proj, 0, SB_WIDTH), (proj, 1, SB_WIDTH), (proj, 2, SB_WIDTH)], [gq, gk, seg],
                    [(SB_WIDTH, BF16)] * 3)


def _qk_norm_bwd(proj, gq, gk, d_qn, d_kn):
    scale = 1.0 / math.sqrt(HEAD_DIM)
    idx = jnp.arange(SB_WIDTH) // HEAD_DIM
    seg = (idx[:, None] == idx[None, :]).astype(BF16)

    def one(x, g, dy, seg_):
        r = lax.rsqrt(_head_mean(x * x, seg_) + EPS)
        gdy = g * dy
        dx = r * gdy - x * (r * r * r) * _head_mean(gdy * x, seg_)
        dg = jnp.sum(dy * x * r, axis=0, keepdims=True)
        return dx, sum(dg[:, _head(h)] for h in range(N_HEADS))

    def fn(q, k, dqn, dkn, gq_, gk_, seg_):
        dq, dgq = one(q, gq_, dqn * scale, seg_)
        dk, dgk = one(k, gk_, dkn, seg_)
        return dq, dk, dgq, dgk

    return _rowwise("qk_norm_bwd", fn, [(proj, 0, SB_WIDTH), (proj, 1, SB_WIDTH), d_qn, d_kn], [gq, gk, seg],
                    [(SB_WIDTH, BF16)] * 2, sums=[(1, HEAD_DIM)] * 2)


def _split_heads(refs, scratch, L):
    def chunk(i, _):
        r = pl.ds(pl.multiple_of(i * QBLOCK, QBLOCK), QBLOCK)
        for ref, s in zip(refs, scratch):
            for h in range(2):
                s[h, r, :] = ref[r, _head(h)]
        return 0

    lax.fori_loop(0, L // QBLOCK, chunk, 0)


Q_HALVES = KBLOCK // QBLOCK
_CHAINS = [(h, r) for h in range(2) for r in range(Q_HALVES)]


def _valid(i, kb):
    row = lax.broadcasted_iota(jnp.int32, (QBLOCK, KBLOCK), 0)
    col = lax.broadcasted_iota(jnp.int32, (QBLOCK, KBLOCK), 1)
    return col + (kb * KBLOCK - i * QBLOCK) < row


def _attn_fwd(qn, kn, vb, B, L):
    n_pairs = L // KBLOCK
    n_hp = N_HEADS // 2
    nc = len(_CHAINS)
    slots = 3

    def body(q_ref, k_ref, v_ref, o_ref, a_ref, q_s, k_s, v_s, after_s, z_s, stage_s, sems):
        _split_heads((q_ref, k_ref, v_ref), (q_s, k_s, v_s), L)
        r2 = lax.broadcasted_iota(jnp.int32, (KBLOCK, KBLOCK), 0)
        c2 = lax.broadcasted_iota(jnp.int32, (KBLOCK, KBLOCK), 1)
        after_s[...] = (r2 > c2).astype(after_s.dtype)
        g = pl.program_id(0) * n_hp + pl.program_id(1)

        def q_pair(p, _):
            rows = [pl.ds(pl.multiple_of((p * Q_HALVES + r) * QBLOCK, QBLOCK), QBLOCK) for r in range(Q_HALVES)]
            q_c = [q_s[h, rows[r], :] for h, r in _CHAINS]
            cs = range(nc)

            def scores(kb):
                rk = pl.ds(pl.multiple_of(kb * KBLOCK, KBLOCK), KBLOCK)
                return [_dot(q_c[c], k_s[_CHAINS[c][0], rk, :], _NT) for c in cs]

            def saved(kb):
                slot = lax.rem(kb, slots)
                return pltpu.make_async_copy(stage_s.at[slot], a_ref.at[g, p, kb], sems.at[slot])

            def k_block(kb, carry, diagonal):
                rk = pl.ds(pl.multiple_of(kb * KBLOCK, KBLOCK), KBLOCK)
                if diagonal:
                    valid = [_valid(p * Q_HALVES + r, kb) for r in range(Q_HALVES)]
                    keep = lambda c, t: jnp.where(valid[_CHAINS[c][1]], t, 0.0)
                    z = scores(kb)
                else:
                    keep = lambda c, t: t
                    z = [z_s[(kb + 1) & 1, c] for c in cs]
                ahead = scores(jnp.maximum(kb - 1, 0))
                for c in cs:
                    z_s[kb & 1, c] = ahead[c]
                sp = [_softplus(z[c]) for c in cs]
                lsig = [z[c] - sp[c] for c in cs]
                lom = [keep(c, -sp[c]) for c in cs]
                tail = [_split_dot(lom[c], after_s[...]) + carry[c][0] for c in cs]
                a = [keep(c, jnp.exp(lsig[c] + tail[c])).astype(v_s.dtype) for c in cs]
                acc = [carry[c][1] + _dot(a[c], v_s[_CHAINS[c][0], rk, :]) for c in cs]
                for c in cs:
                    stage_s[lax.rem(kb, slots), c] = a[c]
                saved(kb).start()
                return tuple((carry[c][0] + jnp.sum(lom[c], axis=1, keepdims=True), acc[c]) for c in cs)

            def next_block(n, carry):
                kb = p - n

                @pl.when(n >= slots)
                def _():
                    saved(kb + slots).wait()

                return k_block(kb, carry, False)

            init = (jnp.zeros((QBLOCK, 1), F32), jnp.zeros((QBLOCK, HEAD_DIM), F32))
            first = k_block(p, (init,) * nc, True)
            res = lax.fori_loop(1, p + 1, next_block, first)
            saved(0).wait()
            for late in range(1, slots):
                @pl.when(p >= late)
                def _():
                    saved(late).wait()

            for r in range(Q_HALVES):
                o_ref[rows[r], :] = jnp.concatenate([res[c][1] for c in cs if _CHAINS[c][1] == r], axis=1)
            return 0

        lax.fori_loop(0, n_pairs, q_pair, 0)

    spec = pl.BlockSpec((L, LANES), lambda b, p: (b, p))
    return pl.pallas_call(
        body, name="attn_fwd", grid=(B, n_hp),
        in_specs=[spec] * 3, out_specs=[spec, _ANY],
        out_shape=[jax.ShapeDtypeStruct((B * L, SB_WIDTH), F32),
                   jax.ShapeDtypeStruct((B * n_hp, n_pairs, n_pairs, nc, QBLOCK, KBLOCK), BF16)],
        scratch_shapes=[pltpu.VMEM((2, L, HEAD_DIM), BF16)] * 3 + [pltpu.VMEM((KBLOCK, KBLOCK), BF16)]
        + [pltpu.VMEM((2, nc, QBLOCK, KBLOCK), F32), pltpu.VMEM((slots, nc, QBLOCK, KBLOCK), BF16),
           pltpu.SemaphoreType.DMA((slots,))],
        compiler_params=_params(("parallel", "parallel")),
    )(qn, kn, vb)


def _attn_bwd(qn, kn, vb, kept_a, d_sb, B, L):
    n_pairs = L // KBLOCK
    n_hp = N_HEADS // 2
    nc = len(_CHAINS)
    slots = 3

    def body(q_ref, k_ref, v_ref, do_ref, a_ref, dq_ref, dk_ref, dv_ref,
             q_s, k_s, v_s, qt_s, dkt_s, dvt_s, before_s, stage_s, sems):
        _split_heads((q_ref, k_ref, v_ref), (q_s, k_s, v_s), L)
        g = pl.program_id(0) * n_hp + pl.program_id(1)

        def transpose_q(i, _):
            r = pl.ds(pl.multiple_of(i * QBLOCK, QBLOCK), QBLOCK)
            qt_s[:, r] = q_ref[r, :].astype(F32).T.astype(qt_s.dtype)
            return 0

        lax.fori_loop(0, L // QBLOCK, transpose_q, 0)
        dkt_s[...] = jnp.zeros_like(dkt_s)
        dvt_s[...] = jnp.zeros_like(dvt_s)
        r2 = lax.broadcasted_iota(jnp.int32, (KBLOCK, KBLOCK), 0)
        c2 = lax.broadcasted_iota(jnp.int32, (KBLOCK, KBLOCK), 1)
        before_s[...] = (r2 < c2).astype(before_s.dtype)

        def q_pair(p, _):
            rows = [pl.ds(pl.multiple_of((p * Q_HALVES + r) * QBLOCK, QBLOCK), QBLOCK) for r in range(Q_HALVES)]
            pair = pl.ds(pl.multiple_of(p * KBLOCK, KBLOCK), KBLOCK)
            do2 = do_ref[pair, :]
            do_t = do2.T.astype(v_s.dtype)
            cs = range(nc)
            hs = range(2)
            q_c = [q_s[h, rows[r], :] for h, r in _CHAINS]
            do_c = [do2[r * QBLOCK:(r + 1) * QBLOCK, _head(h)].astype(v_s.dtype) for h, r in _CHAINS]
            qt_h = [qt_s[_head(h), pair] for h in hs]
            dot_h = [do_t[_head(h), :] for h in hs]

            def kept(kb):
                slot = lax.rem(kb, slots)
                return pltpu.make_async_copy(a_ref.at[g, p, kb], stage_s.at[slot], sems.at[slot])

            def k_block(kb, carry, diagonal):
                rk = pl.ds(pl.multiple_of(kb * KBLOCK, KBLOCK), KBLOCK)
                if diagonal:
                    valid = [_valid(p * Q_HALVES + r, kb) for r in range(Q_HALVES)]
                    keep = lambda c, t: jnp.where(valid[_CHAINS[c][1]], t, 0.0)
                else:
                    keep = lambda c, t: t

                    @pl.when(kb + 2 <= p)
                    def _():
                        kept(kb + 2).start()

                kept(kb).wait()
                slot = lax.rem(kb, slots)
                k_b = [k_s[h, rk, :] for h in hs]
                z = [_dot(q_c[c], k_b[_CHAINS[c][0]], _NT) for c in cs]
                da = [_dot(do_c[c], v_s[_CHAINS[c][0], rk, :], _NT) for c in cs]
                a = [stage_s[slot, c] for c in cs]
                dla = [a[c].astype(F32) * da[c] for c in cs]
                for h in hs:
                    a_h = jnp.concatenate([a[c] for c in cs if _CHAINS[c][0] == h], axis=0)
                    dvt_s[_head(h), rk] += _dot(dot_h[h], a_h)
                d_lom = [carry[c][0] + _split_dot(dla[c], before_s[...]) for c in cs]
                beta = [jax.nn.sigmoid(z[c]) for c in cs]
                dz_b = [(dla[c] * (1.0 - beta[c]) - keep(c, beta[c] * d_lom[c])).astype(v_s.dtype) for c in cs]
                dq_acc = [carry[c][1] + _dot(dz_b[c], k_b[_CHAINS[c][0]]) for c in cs]
                for h in hs:
                    dz_h = jnp.concatenate([dz_b[c] for c in cs if _CHAINS[c][0] == h], axis=0)
                    dkt_s[_head(h), rk] += _dot(qt_h[h], dz_h)
                return tuple((carry[c][0] + jnp.sum(dla[c], axis=1, keepdims=True), dq_acc[c]) for c in cs)

            init = (jnp.zeros((QBLOCK, 1), F32), jnp.zeros((QBLOCK, HEAD_DIM), F32))
            kept(0).start()

            @pl.when(p >= 1)
            def _():
                kept(1).start()

            before =lax.fori_loop(0, p, lambda kb, carry: k_block(kb, carry, False), (init,) * nc)
            res = k_block(p, before, True)
            for r in range(Q_HALVES):
                dq_ref[rows[r], :] = jnp.concatenate([res[c][1] for c in cs if _CHAINS[c][1] == r], axis=1)
            return 0

        lax.fori_loop(0, n_pairs, q_pair, 0)

        def transpose_out(i, _):
            r = pl.ds(pl.multiple_of(i * QBLOCK, QBLOCK), QBLOCK)
            dk_ref[r, :] = dkt_s[:, r].T
            dv_ref[r, :] = dvt_s[:, r].T.astype(dv_ref.dtype)
            return 0

        lax.fori_loop(0, L // QBLOCK, transpose_out, 0)

    spec = pl.BlockSpec((L, LANES), lambda b, p: (b, p))
    return pl.pallas_call(
        body, name="attn_bwd", grid=(B, n_hp),
        in_specs=[spec] * 4 + [_ANY], out_specs=[spec] * 3,
        out_shape=[jax.ShapeDtypeStruct((B * L, SB_WIDTH), F32)] * 2 + [jax.ShapeDtypeStruct((B * L, SB_WIDTH), BF16)],
        scratch_shapes=[pltpu.VMEM((2, L, HEAD_DIM), BF16)] * 3 + [pltpu.VMEM((LANES, L), BF16)]
        + [pltpu.VMEM((LANES, L), F32)] * 2 + [pltpu.VMEM((KBLOCK, KBLOCK), BF16)]
        + [pltpu.VMEM((slots, nc, QBLOCK, KBLOCK), BF16), pltpu.SemaphoreType.DMA((slots,))],
        compiler_params=_params(("parallel", "parallel")),
    )(qn, kn, vb, d_sb, kept_a)


def _ssm_discretise(lam_re, lam_im, log_dt, b_re, b_im):
    dt = jnp.exp(log_dt)
    mag = jnp.exp(lam_re * dt)
    lbr = mag * jnp.cos(lam_im * dt)
    lbi = mag * jnp.sin(lam_im * dt)
    den = lam_re * lam_re + lam_im * lam_im
    nr, ni = lbr - 1.0, lbi
    cr = (nr * lam_re + ni * lam_im) / den
    ci = (ni * lam_re - nr * lam_im) / den
    return lbr, lbi, cr * b_re - ci * b_im, cr * b_im + ci * b_re


def _ssm_prep(lam_re, lam_im, log_dt, b_re_t, b_im_t):
    def body(lr, li, ld, br, bi, o_lr, o_li, o_br, o_bi):
        res = _ssm_discretise(lr[...], li[...], ld[...], br[...], bi[...])
        for o, v in zip((o_lr, o_li, o_br, o_bi), res):
            o[...] = v

    return pl.pallas_call(
        body, name="ssm_prep",
        out_shape=[jax.ShapeDtypeStruct(lam_re.shape, F32)] * 2 + [jax.ShapeDtypeStruct(b_re_t.shape, F32)] * 2,
    )(lam_re, lam_im, log_dt, b_re_t, b_im_t)


def _ssm_prep_bwd(lam_re, lam_im, log_dt, b_re_t, b_im_t, d_lr, d_li, d_br, d_bi):
    def body(lr, li, ld, br, bi, g_lr, g_li, g_br, g_bi, o_lr, o_li, o_ld, o_br, o_bi):
        _, vjp = jax.vjp(_ssm_discretise, lr[...], li[...], ld[...], br[...], bi[...])
        res = vjp((g_lr[...], g_li[...], g_br[...], g_bi[...]))
        for o, v in zip((o_lr, o_li, o_ld, o_br, o_bi), res):
            o[...] = v

    return pl.pallas_call(
        body, name="ssm_prep_bwd",
        out_shape=[jax.ShapeDtypeStruct(lam_re.shape, F32)] * 2 + [jax.ShapeDtypeStruct(log_dt.shape, F32)]
        + [jax.ShapeDtypeStruct(b_re_t.shape, F32)] * 2,
    )(lam_re, lam_im, log_dt, b_re_t, b_im_t, d_lr, d_li, d_br, d_bi)


def _block_diag(m):
    m4 = m.reshape(SSM_COLS, 8, SSM_GROUP, SSM_STATE)
    return jnp.einsum("aghp,gk->aghkp", m4, jnp.eye(8, dtype=m.dtype)).reshape(SSM_COLS, LANES, 512)


def _block_diag_take(d):
    d6 = d.reshape(SSM_COLS, 8, SSM_GROUP, 2, 8, SSM_STATE)
    return jnp.einsum("aghrgp->raghp", d6).reshape(2, SSM_GROUPS, SSM_GROUP, SSM_STATE)


def _cmul(ar, ai, br, bi):
    return ar * br - ai * bi, ar * bi + ai * br


def _power(lr, li, n):
    assert n & (n - 1) == 0
    for _ in range(n.bit_length() - 1):
        lr, li = _cmul(lr, li, lr, li)
    return lr, li


def _ssm_fwd(u_p, w_b, lam_r, lam_i, c_m, d_skip, B, L, tj):
    J = L // N_CHUNK
    njt = J // tj
    R = tj * N_CHUNK
    H = 512

    def body(u_ref, wb_ref, lr_ref, li_ref, cm_ref, d_ref, y_ref, gel_ref, x_ref, xin_ref, bu_s, st_s, xin_s):
        ph, jt = pl.program_id(2), pl.program_id(3)
        lr, li = lr_ref[...], li_ref[...]

        @pl.when(jnp.logical_and(ph == 0, jt == 0))
        def _():
            st_s[...] = jnp.zeros_like(st_s)

        @pl.when(ph == 0)
        def _():
            bu_s[jt] = _dot(u_ref[...].astype(BF16), wb_ref[...].astype(BF16))

        def scan(store):
            def step(j, carry):
                xr, xi = carry
                r = pl.ds(pl.multiple_of(j * N_CHUNK, N_CHUNK), N_CHUNK)
                nr = lr * xr - li * xi + bu_s[jt, r, 0:H]
                ni = lr * xi + li * xr + bu_s[jt, r, H:2 * H]
                if store:
                    x_ref[r, 0:H] = nr
                    x_ref[r, H:2 * H] = ni
                return nr, ni

            xr, xi = lax.fori_loop(0, tj, step, (st_s[:, 0:H], st_s[:, H:2 * H]))
            st_s[:, 0:H] = xr
            st_s[:, H:2 * H] = xi

        @pl.when(ph == 0)
        def _():
            scan(False)

            @pl.when(jt == njt - 1)
            def _():
                pr, pi = _power(lr[0:1], li[0:1], J)
                xin_s[0:1, :] = jnp.zeros((1, 2 * H), F32)
                for c in range(1, N_CHUNK):
                    qr, qi = _cmul(pr, pi, xin_s[c - 1:c, 0:H], xin_s[c - 1:c, H:2 * H])
                    xin_s[c:c + 1, 0:H] = qr + st_s[c - 1:c, 0:H]
                    xin_s[c:c + 1, H:2 * H] = qi + st_s[c - 1:c, H:2 * H]
                xin_ref[...] = xin_s[...]
                st_s[...] = xin_s[...]

        @pl.when(ph == 1)
        def _():
            scan(True)
            y = _dot(x_ref[...].astype(BF16), cm_ref[...].astype(BF16)) + d_ref[...] * u_ref[...]
            y_ref[...] = y
            gel_ref[...] = jax.nn.gelu(y).astype(gel_ref.dtype)

    return pl.pallas_call(
        body, name="ssm_fwd", grid=(SSM_COLS, B, 2, njt),
        in_specs=[
            pl.BlockSpec((None, R, LANES), lambda i, b, ph, jt: (b, jt, i)),
            pl.BlockSpec((None, LANES, 2 * H), lambda i, b, ph, jt: (i, 0, 0)),
            pl.BlockSpec((None, N_CHUNK, H), lambda i, b, ph, jt: (i, 0, 0)),
            pl.BlockSpec((None, N_CHUNK, H), lambda i, b, ph, jt: (i, 0, 0)),
            pl.BlockSpec((None, 2 * H, LANES), lambda i, b, ph, jt: (i, 0, 0)),
            pl.BlockSpec((1, LANES), lambda i, b, ph, jt: (0, i)),
        ],
        out_specs=[
            pl.BlockSpec((None, R, LANES), lambda i, b, ph, jt: (b, jt * ph, i)),
            pl.BlockSpec((None, R, LANES), lambda i, b, ph, jt: (b, jt * ph, i)),
            pl.BlockSpec((None, R, 2 * H), lambda i, b, ph, jt: (b, jt * ph, i)),
            pl.BlockSpec((None, None, N_CHUNK, 2 * H), lambda i, b, ph, jt: (b, i, 0, 0)),
        ],
        out_shape=[
            jax.ShapeDtypeStruct((B, L, SSM_WIDTH), F32),
            jax.ShapeDtypeStruct((B, L, SSM_WIDTH), BF16),
            jax.ShapeDtypeStruct((B, L, SSM_COLS * 2 * H), F32),
            jax.ShapeDtypeStruct((B, SSM_COLS, N_CHUNK, 2 * H), F32),
        ],
        scratch_shapes=[pltpu.VMEM((njt, R, 2 * H), F32), pltpu.VMEM((N_CHUNK, 2 * H), F32),
                        pltpu.VMEM((N_CHUNK, 2 * H), F32)],
        compiler_params=_params(("arbitrary",) * 4),
    )(u_p, w_b, lam_r, lam_i, c_m, d_skip)


def _ssm_bwd(dy_p, u_p, x, xin, w_bt, lam_r, lam_i, c_mt, d_skip, B, L, tj):
    J = L // N_CHUNK
    njt = J // tj
    R = tj * N_CHUNK
    H = 512
    x4 = x.reshape(B, J, N_CHUNK, SSM_COLS * 2 * H)

    def body(dy_ref, u_ref, x_ref, xp_ref, xin_ref, wbt_ref, lr_ref, li_ref, cmt_ref, d_ref,
             du_ref, dwb_ref, dcm_ref, dlr_ref, dli_ref, dd_ref, ca_s, a_s, st_s, dl_s):
        b, ph, jt = pl.program_id(1), pl.program_id(2), pl.program_id(3)
        jr = njt - 1 - jt
        lr, li = lr_ref[...], -li_ref[...]

        @pl.when(jnp.logical_and(b == 0, jnp.logical_and(ph == 0, jt == 0)))
        def _():
            dwb_ref[...] = jnp.zeros_like(dwb_ref)
            dcm_ref[...] = jnp.zeros_like(dcm_ref)
            dlr_ref[...] = jnp.zeros_like(dlr_ref)
            dli_ref[...] = jnp.zeros_like(dli_ref)
            dd_ref[...] = jnp.zeros_like(dd_ref)
            dl_s[...] = jnp.zeros_like(dl_s)

        @pl.when(jnp.logical_and(ph == 0, jt == 0))
        def _():
            st_s[...] = jnp.zeros_like(st_s)

        @pl.when(ph == 0)
        def _():
            ca_s[jt] = _dot(dy_ref[...].astype(BF16), cmt_ref[...].astype(BF16))

        def scan(store):
            def step(n, carry):
                ar, ai = carry
                r = pl.ds(pl.multiple_of((tj - 1 - n) * N_CHUNK, N_CHUNK), N_CHUNK)
                nr = lr * ar - li * ai + ca_s[jt, r, 0:H]
                ni = lr * ai + li * ar + ca_s[jt, r, H:2 * H]
                if store:
                    a_s[r, 0:H] = nr
                    a_s[r, H:2 * H] = ni
                return nr, ni

            ar, ai = lax.fori_loop(0, tj, step, (st_s[:, 0:H], st_s[:, H:2 * H]))
            st_s[:, 0:H] = ar
            st_s[:, H:2 * H] = ai

        @pl.when(ph == 0)
        def _():
            scan(False)

            @pl.when(jt == njt - 1)
            def _():
                pr, pi = _power(lr[0:1], li[0:1], J)
                a_s[N_CHUNK - 1:N_CHUNK, :] = jnp.zeros((1, 2 * H), F32)
                for c in range(N_CHUNK - 2, -1, -1):
                    qr, qi = _cmul(pr, pi, a_s[c + 1:c + 2, 0:H], a_s[c + 1:c + 2, H:2 * H])
                    a_s[c:c + 1, 0:H] = qr + st_s[c + 1:c + 2, 0:H]
                    a_s[c:c + 1, H:2 * H] = qi + st_s[c + 1:c + 2, H:2 * H]
                st_s[...] = a_s[0:N_CHUNK, :]

        @pl.when(ph == 1)
        def _():
            scan(True)
            dy = dy_ref[...]
            u = u_ref[...]
            a_b = a_s[...].astype(BF16)
            du_ref[...] = (_dot(a_b, wbt_ref[...].astype(BF16)) + d_ref[...] * dy).astype(du_ref.dtype)
            dwb_ref[...] += _dot(u.astype(BF16), a_b, _TN)
            dcm_ref[...] += _dot(x_ref[...].astype(BF16), dy.astype(BF16), _TN)
            dd_ref[...] += jnp.sum(dy * u, axis=0, keepdims=True)

            first = jnp.where(jr == 0, xin_ref[...], xp_ref[...])
            a0r, a0i = a_s[0:N_CHUNK, 0:H], a_s[0:N_CHUNK, H:2 * H]
            acc0 = (a0r * first[:, 0:H] + a0i * first[:, H:2 * H], a0i * first[:, 0:H] - a0r * first[:, H:2 * H])

            def step(j, carry):
                sr, si = carry
                r = pl.ds(pl.multiple_of(j * N_CHUNK, N_CHUNK), N_CHUNK)
                rp = pl.ds(pl.multiple_of((j - 1) * N_CHUNK, N_CHUNK), N_CHUNK)
                ar, ai = a_s[r, 0:H], a_s[r, H:2 * H]
                xr, xi = x_ref[rp, 0:H], x_ref[rp, H:2 * H]
                return sr + ar * xr + ai * xi, si + ai * xr - ar * xi

            sr, si = lax.fori_loop(1, tj, step, acc0)
            dl_s[:, 0:H] += sr
            dl_s[:, H:2 * H] += si

            @pl.when(jnp.logical_and(b == B - 1, jt == njt - 1))
            def _():
                dlr_ref[...] = jnp.sum(dl_s[:, 0:H], axis=0, keepdims=True)
                dli_ref[...] = jnp.sum(dl_s[:, H:2 * H], axis=0, keepdims=True)
                dl_s[...] = jnp.zeros_like(dl_s)

    rev = lambda ph, jt: (njt - 1 - jt) * ph + (njt - 1) * (1 - ph)
    return pl.pallas_call(
        body, name="ssm_bwd", grid=(SSM_COLS, B, 2, njt),
        in_specs=[
            pl.BlockSpec((None, R, LANES), lambda i, b, ph, jt: (b, njt - 1 - jt, i)),
            pl.BlockSpec((None, R, LANES), lambda i, b, ph, jt: (b, njt - 1 - jt, i)),
            pl.BlockSpec((None, R, 2 * H), lambda i, b, ph, jt: (b, rev(ph, jt), i)),
            pl.BlockSpec((None, None, N_CHUNK, 2 * H),
                         lambda i, b, ph, jt: (b, jnp.maximum((njt - 1 - jt) * tj - 1, 0), 0, i)),
            pl.BlockSpec((None, None, N_CHUNK, 2 * H), lambda i, b, ph, jt: (b, i, 0, 0)),
            pl.BlockSpec((None, 2 * H, LANES), lambda i, b, ph, jt: (i, 0, 0)),
            pl.BlockSpec((None, N_CHUNK, H), lambda i, b, ph, jt: (i, 0, 0)),
            pl.BlockSpec((None, N_CHUNK, H), lambda i, b, ph, jt: (i, 0, 0)),
            pl.BlockSpec((None, LANES, 2 * H), lambda i, b, ph, jt: (i, 0, 0)),
            pl.BlockSpec((1, LANES), lambda i, b, ph, jt: (0, i)),
        ],
        out_specs=[
            pl.BlockSpec((None, R, LANES), lambda i, b, ph, jt: (b, rev(ph, jt), i)),
            pl.BlockSpec((None, LANES, 2 * H), lambda i, b, ph, jt: (i, 0, 0)),
            pl.BlockSpec((None, 2 * H, LANES), lambda i, b, ph, jt: (i, 0, 0)),
            pl.BlockSpec((None, 1, H), lambda i, b, ph, jt: (i, 0, 0)),
            pl.BlockSpec((None, 1, H), lambda i, b, ph, jt: (i, 0, 0)),
            pl.BlockSpec((1, LANES), lambda i, b, ph, jt: (0, i)),
        ],
        out_shape=[
            jax.ShapeDtypeStruct((B, L, SSM_WIDTH), BF16),
            jax.ShapeDtypeStruct((SSM_COLS, LANES, 2 * H), F32),
            jax.ShapeDtypeStruct((SSM_COLS, 2 * H, LANES), F32),
            jax.ShapeDtypeStruct((SSM_COLS, 1, H), F32),
            jax.ShapeDtypeStruct((SSM_COLS, 1, H), F32),
            jax.ShapeDtypeStruct((1, SSM_WIDTH), F32),
        ],
        scratch_shapes=[pltpu.VMEM((njt, R, 2 * H), F32), pltpu.VMEM((R, 2 * H), F32),
                        pltpu.VMEM((N_CHUNK, 2 * H), F32), pltpu.VMEM((N_CHUNK, 2 * H), F32)],
        compiler_params=_params(("arbitrary",) * 4),
    )(dy_p, u_p, x, x4, xin, w_bt, lam_r, lam_i, c_mt, d_skip)


def _to_scan_layout(t, B, L):
    C = t.shape[-1]
    return t.reshape(B, N_CHUNK, L // N_CHUNK, C).transpose(0, 2, 1, 3).reshape(B, L, C)


def _from_scan_layout(t, B, L):
    C = t.shape[-1]
    return t.reshape(B, L // N_CHUNK, N_CHUNK, C).transpose(0, 2, 1, 3).reshape(B * L, C)


def _local_step(x, target, p, first_weight, late_weights, mlp_grads_ready=None, rest_grads_ready=None, order=None,
                *, ssm_tile=128):
    B, L, D = x.shape
    T = B * L
    x2 = x.reshape(T, D)
    row = lambda v: v.reshape(1, -1)
    g1, g2, ga, gs, b_glu = row(p["norm1_g"]), row(p["norm2_g"]), row(p["attn_out_g"]), row(p["ssm_out_g"]), row(p["b_glu"])
    g1_first = g1 if order is None else g1 + order
    gq8 = jnp.tile(row(p["q_norm_g"]), (1, N_HEADS))
    gk8 = jnp.tile(row(p["k_norm_g"]), (1, N_HEADS))

    G, P, Hh = SSM_GROUPS, SSM_STATE, SSM_GROUP
    lam_re3, lam_im3 = p["ssm_lambda_re"].reshape(G, 1, P), p["ssm_lambda_im"].reshape(G, 1, P)
    log_dt3 = p["ssm_log_dt"].reshape(G, 1, 1)
    b_re_t, b_im_t = p["ssm_b_re"].transpose(0, 2, 1), p["ssm_b_im"].transpose(0, 2, 1)
    lbr, lbi, bbr, bbi = _ssm_prep(lam_re3, lam_im3, log_dt3, b_re_t, b_im_t)
    w_b = jnp.concatenate([_block_diag(bbr), _block_diag(bbi)], axis=2)
    c_mt = jnp.concatenate([_block_diag(p["ssm_c_re"]), -_block_diag(p["ssm_c_im"])], axis=2)
    w_bt, c_m = w_b.transpose(0, 2, 1), c_mt.transpose(0, 2, 1)
    lam_r = jnp.broadcast_to(lbr.reshape(SSM_COLS, 1, 512), (SSM_COLS, N_CHUNK, 512))
    lam_i = jnp.broadcast_to(lbi.reshape(SSM_COLS, 1, 512), (SSM_COLS, N_CHUNK, 512))
    d_skip = p["ssm_d"].reshape(1, SSM_WIDTH)

    xn = _rowwise("norm1", _rms, [x2], [g1_first], [(D, BF16)])
    w_in = first_weight(xn)
    proj = _matmul("proj", xn, w_in, tn=1024)
    qn, kn, vb = _qk_norm(proj, gq8, gk8)
    sb, attn_kept = _attn_fwd(qn, kn, vb, B, L)
    u_p = _to_scan_layout(proj[:, 3 * SB_WIDTH:], B, L)
    y_p, gel_p, xs, xin = _ssm_fwd(u_p, w_b, lam_r, lam_i, c_m, d_skip, B, L, ssm_tile)
    y2, gel = y_p.reshape(T, SSM_WIDTH), gel_p.reshape(T, SSM_WIDTH)
    w_glu, w_out, w_mlp_in, w_mlp_out = late_weights(gel)
    pre = _matmul("glu_gate", gel, w_glu)
    ssm_n = _rowwise("glu_out", _glu_branch, [y2, pre], [b_glu, gs], [(SSM_WIDTH, BF16)])
    sb_n = _rowwise("attn_out_norm", _rms, [sb], [ga], [(SB_WIDTH, BF16)])
    mixed = jnp.concatenate([sb_n, _from_scan_layout(ssm_n, B, L)], axis=1)
    def residual_and_norm(acc, res, g):
        h = acc + res
        return h, _rms(h, g)

    h1, hn = _matmul("out_proj", mixed, w_out, extras=[x2, g2], out_dtypes=(F32, BF16), tn=D,
                     epilogue=residual_and_norm)
    act, a_pre = _matmul("mlp_in", hn, w_mlp_in, out_dtypes=(BF16, BF16), tn=1024,
                         epilogue=lambda acc: (jnp.square(jnp.maximum(acc, 0.0)), acc))

    def loss_fn(acc, h, t):
        diff = acc + h - t
        part = jnp.sum(jnp.sum(diff * diff, axis=0, keepdims=True), axis=1, keepdims=True)
        d = diff * (1.0 / D)
        return d, d, part * (0.5 / D)

    d_out, d_out_b, loss = _matmul("mlp_out", act, w_mlp_out, extras=[h1, target.reshape(T, D)],
                                   out_dtypes=(F32, BF16), sums=[(1, 1)], epilogue=loss_fn, tm=512, tn=D)

    d_apre = _matmul("mlp_out_dx", d_out_b, w_mlp_out, tb=True, extras=[a_pre], out_dtypes=(BF16,), tn=1024,
                     epilogue=lambda acc, ap: (acc * (2.0 * jnp.maximum(ap.astype(F32), 0.0)),))
    both = lambda acc: (acc, acc)
    g_w_mlp_out, g_w_mlp_out_b = _matmul("mlp_out_dw", act, d_out_b, ta=True, out_dtypes=(F32, BF16), epilogue=both)
    g_w_mlp_in, g_w_mlp_in_b = _matmul("mlp_in_dw", hn, d_apre, ta=True, col_blocked=True, out_dtypes=(F32, BF16),
                                       epilogue=both, tn=w_mlp_in.shape[1] // N_DEV)
    if mlp_grads_ready is not None:
        g2 = g2 + mlp_grads_ready(g_w_mlp_out, g_w_mlp_out_b, g_w_mlp_in, g_w_mlp_in_b)

    def norm_bwd_res(dy, h, res, g):
        _, vjp = jax.vjp(_rms, h, g)
        dh, dg = vjp(dy)
        return res + dh, dg

    def norm_bwd_res2(dy, h, res, g):
        d, dg = norm_bwd_res(dy, h, res, g)
        return d, d, dg

    d_h1, d_h1_b, g_norm2 = _matmul("mlp_in_dx", d_apre, w_mlp_in, tb=True, extras=[h1, d_out, g2],
                                    out_dtypes=(F32, BF16), sums=[(1, D)], epilogue=norm_bwd_res2, tm=512, tn=D)

    d_mixed = _matmul("out_proj_dx", d_h1_b, w_out, tb=True, tn=1024)
    g_w_out, g_w_out_b = _matmul("out_proj_dw", mixed, d_h1_b, ta=True, out_dtypes=(F32, BF16), epilogue=both)

    def norm_bwd(h, dy, g):
        _, vjp = jax.vjp(_rms, h, g)
        return vjp(dy)

    d_sb, g_attn_out = _rowwise("attn_out_norm_bwd", norm_bwd, [sb, (d_mixed, 0, SB_WIDTH)], [ga],
                                [(SB_WIDTH, F32)], sums=[(1, SB_WIDTH)])
    d_ssm_n = _to_scan_layout(d_mixed[:, SB_WIDTH:], B, L).reshape(T, SSM_WIDTH)

    def glu_bwd(y, pre_, dy, bg, g):
        _, vjp = jax.vjp(_glu_branch, y, pre_, bg, g)
        d_y, d_pre, d_bg, d_g = vjp(dy)
        return d_y, d_pre, d_bg, d_g

    d_y_direct, d_pre, g_b_glu, g_ssm_out = _rowwise(
        "glu_out_bwd", glu_bwd, [y2, pre, d_ssm_n], [b_glu, gs], [(SSM_WIDTH, F32), (SSM_WIDTH, BF16)],
        sums=[(1, SSM_WIDTH), (1, SSM_WIDTH)])
    g_w_glu, g_w_glu_b = _matmul("glu_gate_dw", gel, d_pre, ta=True, out_dtypes=(F32, BF16), epilogue=both)

    def gelu_bwd(dg, y, dy0):
        _, vjp = jax.vjp(jax.nn.gelu, y)
        return (dy0 + vjp(dg)[0],)

    d_y = _matmul("glu_gate_dx", d_pre, w_glu, tb=True, extras=[y2, d_y_direct], epilogue=gelu_bwd)

    du_p, d_wb, d_cm, d_lr, d_li, g_d = _ssm_bwd(
        d_y.reshape(B, L, SSM_WIDTH), u_p, xs, xin, w_bt, lam_r, lam_i, c_mt, d_skip, B, L, ssm_tile)
    d_bb = _block_diag_take(d_wb.reshape(SSM_COLS, LANES, 2, 512))
    d_c = _block_diag_take(d_cm.transpose(0, 2, 1).reshape(SSM_COLS, LANES, 2, 512))
    g_lam_re, g_lam_im, g_log_dt, g_b_re_t, g_b_im_t = _ssm_prep_bwd(
        lam_re3, lam_im3, log_dt3, b_re_t, b_im_t,
        d_lr.reshape(G, 1, P), d_li.reshape(G, 1, P), d_bb[0], d_bb[1])
    d_qn, d_kn, d_v = _attn_bwd(qn, kn, vb, attn_kept, d_sb, B, L)
    d_q, d_k, g_q, g_k = _qk_norm_bwd(proj, gq8, gk8, d_qn, d_kn)

    d_proj = jnp.concatenate([d_q, d_k, d_v, _from_scan_layout(du_p, B, L)], axis=1)
    g_w_in, g_w_in_b = _matmul("proj_dw", xn, d_proj, ta=True, col_blocked=True, out_dtypes=(F32, BF16),
                               epilogue=both, tn=w_in.shape[1] // N_DEV)
    if rest_grads_ready is not None:
        g1 = g1 + rest_grads_ready([g_w_in, g_w_glu, g_w_out], [g_w_in_b, g_w_glu_b, g_w_out_b])
    grad_x, g_norm1 = _matmul("proj_dx", d_proj, w_in, tb=True, extras=[x2, d_h1, g1], sums=[(1, D)],
                              epilogue=norm_bwd_res, tm=512, tn=D)

    small = {
        "norm1_g": g_norm1.reshape(-1),
        "q_norm_g": g_q.reshape(-1),
        "k_norm_g": g_k.reshape(-1),
        "ssm_lambda_re": g_lam_re.reshape(G, P),
        "ssm_lambda_im": g_lam_im.reshape(G, P),
        "ssm_log_dt": g_log_dt.reshape(G),
        "ssm_b_re": g_b_re_t.transpose(0, 2, 1),
        "ssm_b_im": g_b_im_t.transpose(0, 2, 1),
        "ssm_c_re": d_c[0],
        "ssm_c_im": -d_c[1],
        "ssm_d": g_d.reshape(G, Hh),
        "b_glu": g_b_glu.reshape(-1),
        "attn_out_g": g_attn_out.reshape(-1),
        "ssm_out_g": g_ssm_out.reshape(-1),
        "norm2_g": g_norm2.reshape(-1),
    }
    big = {"w_in": g_w_in, "w_glu": g_w_glu, "w_out": g_w_out, "w_mlp_in": g_w_mlp_in, "w_mlp_out": g_w_mlp_out}
    return loss[0, 0], grad_x.reshape(B, L, D), small, big


_ANY = pl.BlockSpec(memory_space=pl.ANY)
_MESH = pl.DeviceIdType.MESH


def _all_gather(name, shards):
    n = len(shards)

    def body(*refs):
        in_refs, out_refs = refs[:n], refs[n:2 * n]
        send_sems, recv_sems, local_sems = refs[2 * n:]
        x, y, c = lax.axis_index("x"), lax.axis_index("y"), lax.axis_index("c")
        me, sibling = (x, y, c), (x, y, 1 - c)
        chips = [(1 - x, y), (x, 1 - y), (1 - x, 1 - y)]

        def copy(a, k, block, to, src=None):
            px, py, pc = block
            rows = out_refs[a].at[4 * px + 2 * py + pc]
            return pltpu.make_async_remote_copy(
                src_ref=rows if src is None else src, dst_ref=rows, send_sem=send_sems.at[a, k],
                recv_sem=recv_sems.at[a, k], device_id=to, device_id_type=_MESH)

        mine = [pltpu.make_async_copy(in_refs[a], out_refs[a].at[4 * x + 2 * y + c], local_sems.at[a]) for a in range(n)]
        first, passed = [], []
        for a in range(n):
            mine[a].start()
            first.append(copy(a, 0, me, sibling, src=in_refs[a]))
            first += [copy(a, 1 + j, me, (*chip, c), src=in_refs[a]) for j, chip in enumerate(chips)]
        for cp in first:
            cp.start()
        for j, chip in enumerate(chips):
            for a in range(n):
                copy(a, 1 + j, (*chip, c), me).wait_recv()
                fwd = copy(a, 4 + j, (*chip, c), sibling)
                fwd.start()
                passed.append(fwd)
        for a in range(n):
            copy(a, 0, sibling, me).wait_recv()
            for j, chip in enumerate(chips):
                copy(a, 4 + j, (*chip, 1 - c), me).wait_recv()
        for cp in first + passed:
            cp.wait_send()
        for cp in mine:
            cp.wait()

    return pl.pallas_call(
        body, name=name,
        in_specs=[_ANY] * n, out_specs=[_ANY] * n,
        out_shape=[jax.ShapeDtypeStruct((N_DEV, *s.shape), s.dtype) for s in shards],
        scratch_shapes=[pltpu.SemaphoreType.DMA((n, 7)), pltpu.SemaphoreType.DMA((n, 7)), pltpu.SemaphoreType.DMA((n,))],
    )(*shards)


_HBM = pl.BlockSpec(memory_space=pltpu.HBM)
_SEM = pl.BlockSpec(memory_space=pltpu.SEMAPHORE)
_EFFECT = pltpu.SideEffectType.DATAFLOW_SIDE_EFFECTING
_FLIPS = [(dx, dy, dc) for dx in (0, 1) for dy in (0, 1) for dc in (0, 1) if (dx, dy, dc) != (0, 0, 0)]


def _exchange_start(name, srcs, lands, per_peer):
    n = len(srcs)

    def body(*refs):
        src_refs, land_refs = refs[:n], refs[n:2 * n]
        send_sems, recv_sems = refs[2 * n:3 * n], refs[3 * n:4 * n]
        token = refs[-1]
        x, y, c = lax.axis_index("x"), lax.axis_index("y"), lax.axis_index("c")
        me = 4 * x + 2 * y + c
        for dx, dy, dc in _FLIPS:
            px, py, pc = (1 - x if dx else x), (1 - y if dy else y), (1 - c if dc else c)
            for a in range(n):
                pltpu.make_async_remote_copy(
                    src_ref=src_refs[a].at[4 * px + 2 * py + pc] if per_peer else src_refs[a],
                    dst_ref=land_refs[a].at[me], send_sem=send_sems[a], recv_sem=recv_sems[a],
                    device_id=(px, py, pc), device_id_type=_MESH).start()
        token[...] = jnp.zeros_like(token)

    hbm = lambda t: pltpu.with_memory_space_constraint(t, pltpu.HBM)
    res = pl.pallas_call(
        body, name=name,
        out_shape=(*[pltpu.SemaphoreType.DMA(())] * (2 * n), *[pltpu.HBM(t.shape, t.dtype) for t in (*srcs, *lands)],
                   jax.ShapeDtypeStruct((8, LANES), F32)),
        in_specs=[_HBM] * (2 * n),
        out_specs=(*[_SEM] * (2 * n), *[_HBM] * (2 * n), pl.BlockSpec(memory_space=pltpu.VMEM)),
        input_output_aliases={i: 2 * n + i for i in range(2 * n)},
        compiler_params=pltpu.CompilerParams(has_side_effects=_EFFECT),
    )(*[hbm(t) for t in (*srcs, *lands)])
    return res[:-1], res[-1]


def _exchange_wait(name, handle, after):
    n = len(handle) // 4
    sems, thru = handle[:2 * n], handle[2 * n:]

    def body(*refs):
        land_refs = refs[n:2 * n]
        send_sems, recv_sems = refs[2 * n:3 * n], refs[3 * n:4 * n]
        me = (lax.axis_index("x"), lax.axis_index("y"), lax.axis_index("c"))
        for a in range(n):
            seven = land_refs[a].at[pl.ds(0, len(_FLIPS))]
            all_copies = pltpu.make_async_remote_copy(
                src_ref=seven, dst_ref=seven, send_sem=send_sems[a], recv_sem=recv_sems[a], device_id=me,
                device_id_type=_MESH)
            all_copies.wait_send()
            all_copies.wait_recv()

    res = pl.pallas_call(
        body, name=name, out_shape=tuple(pltpu.HBM(t.shape, t.dtype) for t in thru),
        in_specs=[*[_HBM] * (2 * n), *[_SEM] * (2 * n), _ANY], out_specs=tuple([_HBM] * (2 * n)),
        input_output_aliases={i: i for i in range(2 * n)},
        compiler_params=pltpu.CompilerParams(has_side_effects=_EFFECT),
    )(*thru, *sems, after)
    return res[n:]


def _adamw_gathered(name, own, parts, me, w, m, v):
    r, c = w.shape
    tr = min(r, 256)

    def body(me_ref, own_ref, p_ref, w_ref, m_ref, v_ref, g_out, d_out, m_out, v_out):
        g = own_ref[...]
        for j in range(N_DEV):
            g = g + p_ref[j].astype(F32)
        delta, m_new, v_new = _adamw(w_ref[...], g, m_ref[...], v_ref[...])
        g_out[...] = g
        d_out[...] = delta
        m_out[...] = m_new
        v_out[...] = v_new

    spec = pl.BlockSpec((tr, c), lambda i, me_ref: (i, 0))
    return pl.pallas_call(
        body, name=name,
        grid_spec=pltpu.PrefetchScalarGridSpec(
            num_scalar_prefetch=1, grid=(r // tr,),
            in_specs=[pl.BlockSpec((None, tr, c), lambda i, me_ref: (me_ref[0], i, 0)),
                      pl.BlockSpec((N_DEV, tr, c), lambda i, me_ref: (0, i, 0)), spec, spec, spec],
            out_specs=[spec] * 4),
        out_shape=[jax.ShapeDtypeStruct((r, c), F32)] * 4,
        compiler_params=_params(("parallel",)),
    )(me, own, parts, w, m, v)


def _adamw(w, g, m, v):
    m = ADAM_B1 * m + (1.0 - ADAM_B1) * g
    v = ADAM_B2 * v + (1.0 - ADAM_B2) * jnp.square(g)
    m_hat = m / (1.0 - ADAM_B1 ** ADAM_STEP)
    v_hat = v / (1.0 - ADAM_B2 ** ADAM_STEP)
    delta = -ADAM_LR * (m_hat / (jnp.sqrt(v_hat) + ADAM_EPS) + ADAM_WD * w)
    return delta, m, v


def _adamw_small(name, parts, w, m, v):
    _, r, c = parts.shape
    tr = 8

    def body(p_ref, w_ref, m_ref, v_ref, g_out, d_out, m_out, v_out):
        g = p_ref[0]
        for j in range(1, N_DEV):
            g = g + p_ref[j]
        delta, m_new, v_new = _adamw(w_ref[...], g, m_ref[...], v_ref[...])
        g_out[...] = g
        d_out[...] = delta
        m_out[...] = m_new
        v_out[...] = v_new

    spec = pl.BlockSpec((tr, c), lambda i: (i, 0))
    return pl.pallas_call(
        body, name=name, grid=(r // tr,),
        in_specs=[pl.BlockSpec((N_DEV, tr, c), lambda i: (0, i, 0)), spec, spec, spec],
        out_specs=[spec] * 4, out_shape=[jax.ShapeDtypeStruct((r, c), F32)] * 4,
        compiler_params=_params(("parallel",)),
    )(parts, w, m, v)


_WEIGHTS = ["norm1_g", "w_in", "q_norm_g", "k_norm_g", "ssm_lambda_re", "ssm_lambda_im", "ssm_log_dt", "ssm_b_re",
            "ssm_b_im", "ssm_c_re", "ssm_c_im", "ssm_d", "w_glu", "b_glu", "attn_out_g", "ssm_out_g", "w_out",
            "norm2_g", "w_mlp_in", "w_mlp_out"]
_BIG = ["w_in", "w_glu", "w_out", "w_mlp_in", "w_mlp_out"]
_SMALL = [n for n in _WEIGHTS if n not in _BIG]
_PACK_COLS = 1024


def _pack(tree, last=None):
    flat = [tree[n].reshape(-1).astype(F32) for n in _SMALL]
    size = sum(f.shape[0] for f in flat)
    rows = -(-(size + 1) // (_PACK_COLS * 8)) * 8
    pad = jnp.zeros((rows * _PACK_COLS - size - 1,), F32)
    tail = jnp.zeros((1,), F32) if last is None else last.reshape(1).astype(F32)
    return jnp.concatenate(flat + [pad, tail]).reshape(rows, _PACK_COLS)


def _unpack(buf, like):
    flat, out, off = buf.reshape(-1), {}, 0
    for n in _SMALL:
        size = like[n].size
        out[n] = flat[off:off + size].reshape(like[n].shape)
        off += size
    return out


def kernel(x, norm1_g, w_in, q_norm_g, k_norm_g, ssm_lambda_re, ssm_lambda_im, ssm_log_dt, ssm_b_re, ssm_b_im, ssm_c_re, ssm_c_im, ssm_d, w_glu, b_glu, attn_out_g, ssm_out_g, w_out, norm2_g, w_mlp_in, w_mlp_out, loss_target, m_norm1_g, m_w_in, m_q_norm_g, m_k_norm_g, m_ssm_lambda_re, m_ssm_lambda_im, m_ssm_log_dt, m_ssm_b_re, m_ssm_b_im, m_ssm_c_re, m_ssm_c_im, m_ssm_d, m_w_glu, m_b_glu, m_attn_out_g, m_ssm_out_g, m_w_out, m_norm2_g, m_w_mlp_in, m_w_mlp_out, v_norm1_g, v_w_in, v_q_norm_g, v_k_norm_g, v_ssm_lambda_re, v_ssm_lambda_im, v_ssm_log_dt, v_ssm_b_re, v_ssm_b_im, v_ssm_c_re, v_ssm_c_im, v_ssm_d, v_w_glu, v_b_glu, v_attn_out_g, v_ssm_out_g, v_w_out, v_norm2_g, v_w_mlp_in, v_w_mlp_out):
    w = dict(zip(_WEIGHTS, (norm1_g, w_in, q_norm_g, k_norm_g, ssm_lambda_re, ssm_lambda_im, ssm_log_dt, ssm_b_re, ssm_b_im, ssm_c_re, ssm_c_im, ssm_d, w_glu, b_glu, attn_out_g, ssm_out_g, w_out, norm2_g, w_mlp_in, w_mlp_out)))
    m = dict(zip(_WEIGHTS, (m_norm1_g, m_w_in, m_q_norm_g, m_k_norm_g, m_ssm_lambda_re, m_ssm_lambda_im, m_ssm_log_dt, m_ssm_b_re, m_ssm_b_im, m_ssm_c_re, m_ssm_c_im, m_ssm_d, m_w_glu, m_b_glu, m_attn_out_g, m_ssm_out_g, m_w_out, m_norm2_g, m_w_mlp_in, m_w_mlp_out)))
    v = dict(zip(_WEIGHTS, (v_norm1_g, v_w_in, v_q_norm_g, v_k_norm_g, v_ssm_lambda_re, v_ssm_lambda_im, v_ssm_log_dt, v_ssm_b_re, v_ssm_b_im, v_ssm_c_re, v_ssm_c_im, v_ssm_d, v_w_glu, v_b_glu, v_attn_out_g, v_ssm_out_g, v_w_out, v_norm2_g, v_w_mlp_in, v_w_mlp_out)))
    core = lax.axis_index("c").astype(jnp.int32).reshape(1)
    chip = (2 * lax.axis_index("x") + lax.axis_index("y")).astype(jnp.int32).reshape(1)

    me = (2 * chip + core).astype(jnp.int32)

    def landing(own=None, like=None):
        own = jnp.zeros_like(like) if own is None else own
        return lax.dynamic_update_slice(lax.empty((N_DEV, *like.shape), like.dtype), own[None], (me[0], 0, 0))

    w_in_shard = w_in.astype(BF16)
    w_in_handle, w_in_token = _exchange_start("w_in_send", [w_in_shard], [landing(w_in_shard, w_in_shard)], per_peer=False)
    late = [n for n in _BIG if n != "w_in"]
    shards = [w[n].astype(BF16) for n in late]
    w_in_token, shards = lax.optimization_barrier((w_in_token, shards))
    weights_handle, weights_token = _exchange_start(
        "weights_send", shards, [landing(s, s) for s in shards], per_peer=False)

    def first_weight(after):
        (blocks,) = _exchange_wait("w_in_arrive", w_in_handle, after)
        return blocks.transpose(1, 0, 2).reshape(w_in.shape[0], -1)

    def late_weights(after):
        got = dict(zip(late, _exchange_wait("weights_arrive", weights_handle, after)))
        return (got["w_glu"].reshape(-1, w_glu.shape[1]), got["w_out"].reshape(-1, w_out.shape[1]),
                got["w_mlp_in"].transpose(1, 0, 2).reshape(w_mlp_in.shape[0], -1),
                got["w_mlp_out"].reshape(-1, w_mlp_out.shape[1]))

    mlp = ["w_mlp_out", "w_mlp_in"]
    sent = {}

    def send_grads(name, names, own, own_b):
        blocks = lambda g, n: g.reshape(N_DEV, *w[n].shape)
        sent[name + "_own"] = [blocks(g, n) for g, n in zip(own, names)]
        srcs = [blocks(g, n) for g, n in zip(own_b, names)]
        sent[name], token = _exchange_start(name, srcs, [landing(like=s[0]) for s in srcs], per_peer=True)
        return token[0, 0]

    def mlp_grads_ready(g_out, g_out_b, g_in, g_in_b):
        return send_grads("mlp_grads_send", mlp, [g_out, g_in], [g_out_b, g_in_b])

    rest = ["w_in", "w_glu", "w_out"]

    def rest_grads_ready(own, own_b):
        return send_grads("rest_grads_send", rest, own, own_b)

    loss_local, grad_x, g_small, g_big = _local_step(
        x, loss_target, {n: w[n] for n in _SMALL}, first_weight, late_weights, mlp_grads_ready, rest_grads_ready,
        weights_token[0, 0])

    grads, delta, new_m, new_v = {}, {}, {}, {}
    small = _pack(g_small, last=loss_local)
    small_handle, small_token = _exchange_start("small_grads_send", [small], [landing(small, small)], per_peer=False)

    for send, arrive, names in (("mlp_grads_send", "mlp_grads_arrive", mlp),
                                ("rest_grads_send", "rest_grads_arrive", rest)):
        for n, own, part in zip(names, sent[send + "_own"], _exchange_wait(arrive, sent[send], small_token)):
            grads[n], delta[n], new_m[n], new_v[n] = _adamw_gathered("adamw_" + n, own, part, me, w[n], m[n], v[n])

    shards_done = lax.optimization_barrier(tuple(new_v[n] for n in _BIG))
    (small_parts,) = _exchange_wait("small_grads_arrive", small_handle, shards_done[-1])
    packed = _adamw_small("adamw_small", small_parts, _pack(w), _pack(m), _pack(v))
    for tree, buf in zip((grads, delta, new_m, new_v), packed):
        tree.update(_unpack(buf, w))
    loss = packed[0][-1, -1]

    return (loss, grad_x, *[grads[n] for n in _WEIGHTS], *[delta[n] for n in _WEIGHTS],
            *[new_m[n] for n in _WEIGHTS], *[new_v[n] for n in _WEIGHTS])
```

```python
import functools
import math

import jax
import jax.numpy as jnp
from jax import lax
from jax.experimental import pallas as pl
from jax.experimental.pallas import tpu as pltpu

F32 = jnp.float32
BF16 = jnp.bfloat16

EPS = 1e-6
HEAD_DIM = 64
N_HEADS = 8
SB_WIDTH = 512
SSM_WIDTH = 512
SSM_GROUP = 16
SSM_GROUPS = 32
SSM_STATE = 64
QBLOCK = 128
KBLOCK = 256
N_CHUNK = 8
SSM_COLS = 4
LANES = 128
N_DEV = 8

ADAM_LR = 0.001
ADAM_B1 = 0.9
ADAM_B2 = 0.999
ADAM_EPS = 1e-08
ADAM_WD = 0.01
ADAM_STEP = 10

VMEM_LIMIT = 56 * 1024 * 1024

_NT = (((1,), (1,)), ((), ()))
_NN = (((1,), (0,)), ((), ()))
_TN = (((0,), (0,)), ((), ()))


def _dot(a, b, dims=_NN):
    return lax.dot_general(a, b, dims, preferred_element_type=F32)


def _params(sem):
    return pltpu.CompilerParams(dimension_semantics=sem, vmem_limit_bytes=VMEM_LIMIT)


def _matmul(name, a, b, *, ta=False, tb=False, extras=(), epilogue=None, out_dtypes=(F32,), sums=(),
            prologue=None, col_blocked=False, tm=1024, tn=512, tk=4096):
    M, K = (a.shape[1], a.shape[0]) if ta else a.shape
    N = b.shape[0] if tb else b.shape[1]
    tm, tn, tk = min(tm, M), min(tn, N), min(tk, K)
    assert M % tm == 0 and N % tn == 0 and K % tk == 0, (name, M, N, K)
    assert not (sums or prologue) or (tn == N and tk == K), name
    assert not prologue or not (ta or col_blocked), name
    nk = K // tk
    n_ex, n_out, n_sum = len(extras), len(out_dtypes), len(sums)
    n_pro = 1 if prologue else 0
    dims = (((0 if ta else 1,), (1 if tb else 0,)), ((), ()))

    def body(*refs):
        a_ref, b_ref = refs[0], refs[1]
        ex_refs = refs[2 + n_pro:2 + n_pro + n_ex]
        o_refs = refs[2 + n_pro + n_ex:2 + n_pro + n_ex + n_out]
        s_refs = refs[2 + n_pro + n_ex + n_out:2 + n_pro + n_ex + n_out + n_sum]
        k = pl.program_id(2)
        if prologue:
            left = prologue[0](a_ref[...], refs[2][...]).astype(BF16)
            refs[2 + n_pro + n_ex + n_out + n_sum][...] = left
        else:
            left = a_ref[...].astype(BF16)
        part = _dot(left, b_ref[...].astype(BF16), dims)

        def finish(acc):
            outs = (acc,) if epilogue is None else epilogue(acc, *[e[...] for e in ex_refs])
            for o_ref, o in zip(o_refs, outs[:n_out]):
                o_ref[...] = o.astype(o_ref.dtype)
            if n_sum:
                @pl.when(pl.program_id(0) == 0)
                def _():
                    for s_ref in s_refs:
                        s_ref[...] = jnp.zeros_like(s_ref)

                for s_ref, v in zip(s_refs, outs[n_out:]):
                    s_ref[...] += v

        if nk == 1:
            finish(part)
        else:
            acc_ref = refs[-1]

            @pl.when(k == 0)
            def _():
                acc_ref[...] = part

            @pl.when(jnp.logical_and(k > 0, k < nk - 1))
            def _():
                acc_ref[...] += part

            @pl.when(k == nk - 1)
            def _():
                finish(acc_ref[...] + part)

    a_spec = pl.BlockSpec((tk, tm), lambda i, j, k: (k, i)) if ta else pl.BlockSpec((tm, tk), lambda i, j, k: (i, k))
    b_spec = pl.BlockSpec((tn, tk), lambda i, j, k: (j, k)) if tb else pl.BlockSpec((tk, tn), lambda i, j, k: (k, j))
    ex_specs = [pl.BlockSpec((1, tn), lambda i, j, k: (0, j)) if e.shape[0] == 1 else
                pl.BlockSpec((tm, tn), lambda i, j, k: (i, j)) for e in extras]
    if col_blocked:
        out_specs = [pl.BlockSpec((None, tm, tn), lambda i, j, k: (j, i, 0)) for _ in out_dtypes]
        out_shape = [jax.ShapeDtypeStruct((N // tn, M, tn), dt) for dt in out_dtypes]
    else:
        out_specs = [pl.BlockSpec((tm, tn), lambda i, j, k: (i, j)) for _ in out_dtypes]
        out_shape = [jax.ShapeDtypeStruct((M, N), dt) for dt in out_dtypes]
    out_specs += [pl.BlockSpec(s, lambda i, j, k: (0, 0)) for s in sums]
    out_shape += [jax.ShapeDtypeStruct(s, F32) for s in sums]
    pro_specs, pro_args = [], []
    if prologue:
        pro_specs, pro_args = [pl.BlockSpec((1, tk), lambda i, j, k: (0, 0))], [prologue[1]]
        out_specs.append(pl.BlockSpec((tm, tk), lambda i, j, k: (i, 0)))
        out_shape.append(jax.ShapeDtypeStruct((M, K), BF16))
    outs = pl.pallas_call(
        body, name=name, grid=(M // tm, N // tn, nk),
        in_specs=[a_spec, b_spec, *pro_specs, *ex_specs], out_specs=out_specs, out_shape=out_shape,
        scratch_shapes=[pltpu.VMEM((tm, tn), F32)] if nk > 1 else [],
        compiler_params=_params(("arbitrary",) * 3 if sums else ("parallel", "parallel", "arbitrary")),
    )(a, b, *pro_args, *extras)
    return outs[0] if len(outs) == 1 else outs


def _rowwise(name, fn, rows, small, outs, sums=(), tile=256):
    specs, args = [], []
    T = None
    for r in rows:
        arr, cb, w = r if isinstance(r, tuple) else (r, 0, r.shape[1])
        T = arr.shape[0]
        specs.append((w, cb))
        args.append(arr)
    tile = min(tile, T)
    assert T % tile == 0
    n_r, n_s, n_o, n_a = len(rows), len(small), len(outs), len(sums)

    def body(*refs):
        r_refs = refs[:n_r]
        s_refs = refs[n_r:n_r + n_s]
        o_refs = refs[n_r + n_s:n_r + n_s + n_o]
        a_refs = refs[n_r + n_s + n_o:]
        res = fn(*[r[...] for r in r_refs], *[s[...] for s in s_refs])
        res = res if isinstance(res, (tuple, list)) else (res,)
        for o_ref, o in zip(o_refs, res[:n_o]):
            o_ref[...] = o.astype(o_ref.dtype)

        @pl.when(pl.program_id(0) == 0)
        def _():
            for a_ref in a_refs:
                a_ref[...] = jnp.zeros_like(a_ref)

        for a_ref, v in zip(a_refs, res[n_o:]):
            a_ref[...] += v.astype(F32)

    in_specs = [pl.BlockSpec((tile, w), functools.partial(lambda i, cb: (i, cb), cb=cb)) for w, cb in specs]
    in_specs += [pl.BlockSpec(s.shape, functools.partial(lambda i, nd: (0,) * nd, nd=s.ndim)) for s in small]
    out_specs = [pl.BlockSpec((tile, w), lambda i: (i, 0)) for w, _ in outs]
    out_specs += [pl.BlockSpec(s, functools.partial(lambda i, nd: (0,) * nd, nd=len(s))) for s in sums]
    out_shape = [jax.ShapeDtypeStruct((T, w), dt) for w, dt in outs]
    out_shape += [jax.ShapeDtypeStruct(s, F32) for s in sums]
    res = pl.pallas_call(
        body, name=name, grid=(T // tile,), in_specs=in_specs, out_specs=out_specs, out_shape=out_shape,
        compiler_params=_params(("arbitrary",)),
    )(*args, *small)
    return res[0] if len(res) == 1 else res


def _rms(x, g):
    return x * lax.rsqrt(jnp.mean(x * x, axis=-1, keepdims=True) + EPS) * g


def _glu_branch(y, pre, b_glu, g_out):
    g = jax.nn.gelu(y)
    return _rms(g * jax.nn.sigmoid(pre + b_glu), g_out)


def _split_dot(x, tri_bf):
    hi = x.astype(BF16)
    lo = (x - hi.astype(F32)).astype(BF16)
    return _dot(hi, tri_bf) + _dot(lo, tri_bf)


def _softplus(z):
    return jnp.maximum(z, 0.0) + jnp.log(1.0 + jnp.exp(-jnp.abs(z)))


def _head(h):
    return slice(h * HEAD_DIM, (h + 1) * HEAD_DIM)


def _head_mean(x, seg):
    return _split_dot(x, seg) * (1.0 / HEAD_DIM)


def _qk_norm(proj, gq, gk):
    scale = 1.0 / math.sqrt(HEAD_DIM)
    idx = jnp.arange(SB_WIDTH) // HEAD_DIM
    seg = (idx[:, None] == idx[None, :]).astype(BF16)

    def fn(q, k, v, gq_, gk_, seg_):
        qn = q * lax.rsqrt(_head_mean(q * q, seg_) + EPS) * (gq_ * scale)
        kn = k * lax.rsqrt(_head_mean(k * k, seg_) + EPS) * gk_
        return qn, kn, v

    return _rowwise("qk_norm", fn, [(proj, 0, SB_WIDTH), (proj, 1, SB_WIDTH), (proj, 2, SB_WIDTH)], [gq, gk, seg],
                    [(SB_WIDTH, BF16)] * 3)


def _qk_norm_bwd(proj, gq, gk, d_qn, d_kn):
    scale = 1.0 / math.sqrt(HEAD_DIM)
    idx = jnp.arange(SB_WIDTH) // HEAD_DIM
    seg = (idx[:, None] == idx[None, :]).astype(BF16)

    def one(x, g, dy, seg_):
        r = lax.rsqrt(_head_mean(x * x, seg_) + EPS)
        gdy = g * dy
        dx = r * gdy - x * (r * r * r) * _head_mean(gdy * x, seg_)
        dg = jnp.sum(dy * x * r, axis=0, keepdims=True)
        return dx, sum(dg[:, _head(h)] for h in range(N_HEADS))

    def fn(q, k, dqn, dkn, gq_, gk_, seg_):
        dq, dgq = one(q, gq_, dqn * scale, seg_)
        dk, dgk = one(k, gk_, dkn, seg_)
        return dq, dk, dgq, dgk

    return _rowwise("qk_norm_bwd", fn, [(proj, 0, SB_WIDTH), (proj, 1, SB_WIDTH), d_qn, d_kn], [gq, gk, seg],
                    [(SB_WIDTH, BF16)] * 2, sums=[(1, HEAD_DIM)] * 2)


def _split_heads(refs, scratch, L):
    def chunk(i, _):
        r = pl.ds(pl.multiple_of(i * QBLOCK, QBLOCK), QBLOCK)
        for ref, s in zip(refs, scratch):
            for h in range(2):
                s[h, r, :] = ref[r, _head(h)]
        return 0

    lax.fori_loop(0, L // QBLOCK, chunk, 0)


Q_HALVES = KBLOCK // QBLOCK
_CHAINS = [(h, r) for h in range(2) for r in range(Q_HALVES)]


def _valid(i, kb):
    row = lax.broadcasted_iota(jnp.int32, (QBLOCK, KBLOCK), 0)
    col = lax.broadcasted_iota(jnp.int32, (QBLOCK, KBLOCK), 1)
    return col + (kb * KBLOCK - i * QBLOCK) < row


def _attn_fwd(qn, kn, vb, B, L):
    n_pairs = L // KBLOCK
    n_hp = N_HEADS // 2
    nc = len(_CHAINS)

    def body(q_ref, k_ref, v_ref, o_ref, a_ref, q_s, k_s, v_s, after_s, z_s, stage_s, sems):
        _split_heads((q_ref, k_ref, v_ref), (q_s, k_s, v_s), L)
        r2 = lax.broadcasted_iota(jnp.int32, (KBLOCK, KBLOCK), 0)
        c2 = lax.broadcasted_iota(jnp.int32, (KBLOCK, KBLOCK), 1)
        after_s[...] = (r2 > c2).astype(after_s.dtype)
        g = pl.program_id(0) * n_hp + pl.program_id(1)

        def q_pair(p, _):
            rows = [pl.ds(pl.multiple_of((p * Q_HALVES + r) * QBLOCK, QBLOCK), QBLOCK) for r in range(Q_HALVES)]
            q_c = [q_s[h, rows[r], :] for h, r in _CHAINS]
            cs = range(nc)

            def scores(kb):
                rk = pl.ds(pl.multiple_of(kb * KBLOCK, KBLOCK), KBLOCK)
                return [_dot(q_c[c], k_s[_CHAINS[c][0], rk, :], _NT) for c in cs]

            def saved(kb):
                return pltpu.make_async_copy(stage_s.at[kb & 1], a_ref.at[g, p, kb], sems.at[kb & 1])

            def k_block(kb, carry, diagonal):
                rk = pl.ds(pl.multiple_of(kb * KBLOCK, KBLOCK), KBLOCK)
                if diagonal:
                    valid = [_valid(p * Q_HALVES + r, kb) for r in range(Q_HALVES)]
                    keep = lambda c, t: jnp.where(valid[_CHAINS[c][1]], t, 0.0)
                    z = scores(kb)
                else:
                    keep = lambda c, t: t
                    z = [z_s[(kb + 1) & 1, c] for c in cs]
                ahead = scores(jnp.maximum(kb - 1, 0))
                for c in cs:
                    z_s[kb & 1, c] = ahead[c]
                sp = [_softplus(z[c]) for c in cs]
                lsig = [z[c] - sp[c] for c in cs]
                lom = [keep(c, -sp[c]) for c in cs]
                tail = [_split_dot(lom[c], after_s[...]) + carry[c][0] for c in cs]
                a = [keep(c, jnp.exp(lsig[c] + tail[c])).astype(v_s.dtype) for c in cs]
                acc = [carry[c][1] + _dot(a[c], v_s[_CHAINS[c][0], rk, :]) for c in cs]
                for c in cs:
                    stage_s[kb & 1, c] = a[c]
                saved(kb).start()
                return tuple((carry[c][0] + jnp.sum(lom[c], axis=1, keepdims=True), acc[c]) for c in cs)

            def next_block(n, carry):
                kb = p - n

                @pl.when(n >= 2)
                def _():
                    saved(kb + 2).wait()

                return k_block(kb, carry, False)

            init = (jnp.zeros((QBLOCK, 1), F32), jnp.zeros((QBLOCK, HEAD_DIM), F32))
            first = k_block(p, (init,) * nc, True)
            res = lax.fori_loop(1, p + 1, next_block, first)
            saved(0).wait()

            @pl.when(p >= 1)
            def _():
                saved(1).wait()

            for r in range(Q_HALVES):
                o_ref[rows[r], :] = jnp.concatenate([res[c][1] for c in cs if _CHAINS[c][1] == r], axis=1)
            return 0

        lax.fori_loop(0, n_pairs, q_pair, 0)

    spec = pl.BlockSpec((L, LANES), lambda b, p: (b, p))
    return pl.pallas_call(
        body, name="attn_fwd", grid=(B, n_hp),
        in_specs=[spec] * 3, out_specs=[spec, _ANY],
        out_shape=[jax.ShapeDtypeStruct((B * L, SB_WIDTH), F32),
                   jax.ShapeDtypeStruct((B * n_hp, n_pairs, n_pairs, nc, QBLOCK, KBLOCK), BF16)],
        scratch_shapes=[pltpu.VMEM((2, L, HEAD_DIM), BF16)] * 3 + [pltpu.VMEM((KBLOCK, KBLOCK), BF16)]
        + [pltpu.VMEM((2, nc, QBLOCK, KBLOCK), F32), pltpu.VMEM((2, nc, QBLOCK, KBLOCK), BF16),
           pltpu.SemaphoreType.DMA((2,))],
        compiler_params=_params(("parallel", "parallel")),
    )(qn, kn, vb)


def _attn_bwd(qn, kn, vb, kept_a, d_sb, B, L):
    n_pairs = L // KBLOCK
    n_hp = N_HEADS // 2
    nc = len(_CHAINS)
    slots = 3

    def body(q_ref, k_ref, v_ref, do_ref, a_ref, dq_ref, dk_ref, dv_ref,
             q_s, k_s, v_s, qt_s, dkt_s, dvt_s, before_s, stage_s, sems):
        _split_heads((q_ref, k_ref, v_ref), (q_s, k_s, v_s), L)
        g = pl.program_id(0) * n_hp + pl.program_id(1)

        def transpose_q(i, _):
            r = pl.ds(pl.multiple_of(i * QBLOCK, QBLOCK), QBLOCK)
            qt_s[:, r] = q_ref[r, :].astype(F32).T.astype(qt_s.dtype)
            return 0

        lax.fori_loop(0, L // QBLOCK, transpose_q, 0)
        dkt_s[...] = jnp.zeros_like(dkt_s)
        dvt_s[...] = jnp.zeros_like(dvt_s)
        r2 = lax.broadcasted_iota(jnp.int32, (KBLOCK, KBLOCK), 0)
        c2 = lax.broadcasted_iota(jnp.int32, (KBLOCK, KBLOCK), 1)
        before_s[...] = (r2 < c2).astype(before_s.dtype)

        def q_pair(p, _):
            rows = [pl.ds(pl.multiple_of((p * Q_HALVES + r) * QBLOCK, QBLOCK), QBLOCK) for r in range(Q_HALVES)]
            pair = pl.ds(pl.multiple_of(p * KBLOCK, KBLOCK), KBLOCK)
            do2 = do_ref[pair, :]
            do_t = do2.T.astype(v_s.dtype)
            cs = range(nc)
            hs = range(2)
            q_c = [q_s[h, rows[r], :] for h, r in _CHAINS]
            do_c = [do2[r * QBLOCK:(r + 1) * QBLOCK, _head(h)].astype(v_s.dtype) for h, r in _CHAINS]
            qt_h = [qt_s[_head(h), pair] for h in hs]
            dot_h = [do_t[_head(h), :] for h in hs]

            def kept(kb):
                slot = lax.rem(kb, slots)
                return pltpu.make_async_copy(a_ref.at[g, p, kb], stage_s.at[slot], sems.at[slot])

            def k_block(kb, carry, diagonal):
                rk = pl.ds(pl.multiple_of(kb * KBLOCK, KBLOCK), KBLOCK)
                if diagonal:
                    valid = [_valid(p * Q_HALVES + r, kb) for r in range(Q_HALVES)]
                    keep = lambda c, t: jnp.where(valid[_CHAINS[c][1]], t, 0.0)
                else:
                    keep = lambda c, t: t

                    @pl.when(kb + 2 <= p)
                    def _():
                        kept(kb + 2).start()

                kept(kb).wait()
                slot = lax.rem(kb, slots)
                k_b = [k_s[h, rk, :] for h in hs]
                z = [_dot(q_c[c], k_b[_CHAINS[c][0]], _NT) for c in cs]
                da = [_dot(do_c[c], v_s[_CHAINS[c][0], rk, :], _NT) for c in cs]
                a = [stage_s[slot, c] for c in cs]
                dla = [a[c].astype(F32) * da[c] for c in cs]
                for h in hs:
                    a_h = jnp.concatenate([a[c] for c in cs if _CHAINS[c][0] == h], axis=0)
                    dvt_s[_head(h), rk] += _dot(dot_h[h], a_h)
                d_lom = [carry[c][0] + _split_dot(dla[c], before_s[...]) for c in cs]
                beta = [jax.nn.sigmoid(z[c]) for c in cs]
                dz_b = [(dla[c] * (1.0 - beta[c]) - keep(c, beta[c] * d_lom[c])).astype(v_s.dtype) for c in cs]
                dq_acc = [carry[c][1] + _dot(dz_b[c], k_b[_CHAINS[c][0]]) for c in cs]
                for h in hs:
                    dz_h = jnp.concatenate([dz_b[c] for c in cs if _CHAINS[c][0] == h], axis=0)
                    dkt_s[_head(h), rk] += _dot(qt_h[h], dz_h)
                return tuple((carry[c][0] + jnp.sum(dla[c], axis=1, keepdims=True), dq_acc[c]) for c in cs)

            init = (jnp.zeros((QBLOCK, 1), F32), jnp.zeros((QBLOCK, HEAD_DIM), F32))
            kept(0).start()

            @pl.when(p >= 1)
            def _():
                kept(1).start()

            before =lax.fori_loop(0, p, lambda kb, carry: k_block(kb, carry, False), (init,) * nc)
            res = k_block(p, before, True)
            for r in range(Q_HALVES):
                dq_ref[rows[r], :] = jnp.concatenate([res[c][1] for c in cs if _CHAINS[c][1] == r], axis=1)
            return 0

        lax.fori_loop(0, n_pairs, q_pair, 0)

        def transpose_out(i, _):
            r = pl.ds(pl.multiple_of(i * QBLOCK, QBLOCK), QBLOCK)
            dk_ref[r, :] = dkt_s[:, r].T
            dv_ref[r, :] = dvt_s[:, r].T.astype(dv_ref.dtype)
            return 0

        lax.fori_loop(0, L // QBLOCK, transpose_out, 0)

    spec = pl.BlockSpec((L, LANES), lambda b, p: (b, p))
    return pl.pallas_call(
        body, name="attn_bwd", grid=(B, n_hp),
        in_specs=[spec] * 4 + [_ANY], out_specs=[spec] * 3,
        out_shape=[jax.ShapeDtypeStruct((B * L, SB_WIDTH), F32)] * 2 + [jax.ShapeDtypeStruct((B * L, SB_WIDTH), BF16)],
        scratch_shapes=[pltpu.VMEM((2, L, HEAD_DIM), BF16)] * 3 + [pltpu.VMEM((LANES, L), BF16)]
        + [pltpu.VMEM((LANES, L), F32)] * 2 + [pltpu.VMEM((KBLOCK, KBLOCK), BF16)]
        + [pltpu.VMEM((slots, nc, QBLOCK, KBLOCK), BF16), pltpu.SemaphoreType.DMA((slots,))],
        compiler_params=_params(("parallel", "parallel")),
    )(qn, kn, vb, d_sb, kept_a)


def _ssm_discretise(lam_re, lam_im, log_dt, b_re, b_im):
    dt = jnp.exp(log_dt)
    mag = jnp.exp(lam_re * dt)
    lbr = mag * jnp.cos(lam_im * dt)
    lbi = mag * jnp.sin(lam_im * dt)
    den = lam_re * lam_re + lam_im * lam_im
    nr, ni = lbr - 1.0, lbi
    cr = (nr * lam_re + ni * lam_im) / den
    ci = (ni * lam_re - nr * lam_im) / den
    return lbr, lbi, cr * b_re - ci * b_im, cr * b_im + ci * b_re


def _ssm_prep(lam_re, lam_im, log_dt, b_re_t, b_im_t):
    def body(lr, li, ld, br, bi, o_lr, o_li, o_br, o_bi):
        res = _ssm_discretise(lr[...], li[...], ld[...], br[...], bi[...])
        for o, v in zip((o_lr, o_li, o_br, o_bi), res):
            o[...] = v

    return pl.pallas_call(
        body, name="ssm_prep",
        out_shape=[jax.ShapeDtypeStruct(lam_re.shape, F32)] * 2 + [jax.ShapeDtypeStruct(b_re_t.shape, F32)] * 2,
    )(lam_re, lam_im, log_dt, b_re_t, b_im_t)


def _ssm_prep_bwd(lam_re, lam_im, log_dt, b_re_t, b_im_t, d_lr, d_li, d_br, d_bi):
    def body(lr, li, ld, br, bi, g_lr, g_li, g_br, g_bi, o_lr, o_li, o_ld, o_br, o_bi):
        _, vjp = jax.vjp(_ssm_discretise, lr[...], li[...], ld[...], br[...], bi[...])
        res = vjp((g_lr[...], g_li[...], g_br[...], g_bi[...]))
        for o, v in zip((o_lr, o_li, o_ld, o_br, o_bi), res):
            o[...] = v

    return pl.pallas_call(
        body, name="ssm_prep_bwd",
        out_shape=[jax.ShapeDtypeStruct(lam_re.shape, F32)] * 2 + [jax.ShapeDtypeStruct(log_dt.shape, F32)]
        + [jax.ShapeDtypeStruct(b_re_t.shape, F32)] * 2,
    )(lam_re, lam_im, log_dt, b_re_t, b_im_t, d_lr, d_li, d_br, d_bi)


def _block_diag(m):
    m4 = m.reshape(SSM_COLS, 8, SSM_GROUP, SSM_STATE)
    return jnp.einsum("aghp,gk->aghkp", m4, jnp.eye(8, dtype=m.dtype)).reshape(SSM_COLS, LANES, 512)


def _block_diag_take(d):
    d6 = d.reshape(SSM_COLS, 8, SSM_GROUP, 2, 8, SSM_STATE)
    return jnp.einsum("aghrgp->raghp", d6).reshape(2, SSM_GROUPS, SSM_GROUP, SSM_STATE)


def _cmul(ar, ai, br, bi):
    return ar * br - ai * bi, ar * bi + ai * br


def _power(lr, li, n):
    assert n & (n - 1) == 0
    for _ in range(n.bit_length() - 1):
        lr, li = _cmul(lr, li, lr, li)
    return lr, li


def _ssm_fwd(u_p, w_b, lam_r, lam_i, c_m, d_skip, B, L, tj):
    J = L // N_CHUNK
    njt = J // tj
    R = tj * N_CHUNK
    H = 512

    def body(u_ref, wb_ref, lr_ref, li_ref, cm_ref, d_ref, y_ref, gel_ref, x_ref, xin_ref, bu_s, st_s, xin_s):
        ph, jt = pl.program_id(2), pl.program_id(3)
        lr, li = lr_ref[...], li_ref[...]

        @pl.when(jnp.logical_and(ph == 0, jt == 0))
        def _():
            st_s[...] = jnp.zeros_like(st_s)

        @pl.when(ph == 0)
        def _():
            bu_s[jt] = _dot(u_ref[...].astype(BF16), wb_ref[...].astype(BF16))

        def scan(store):
            def step(j, carry):
                xr, xi = carry
                r = pl.ds(pl.multiple_of(j * N_CHUNK, N_CHUNK), N_CHUNK)
                nr = lr * xr - li * xi + bu_s[jt, r, 0:H]
                ni = lr * xi + li * xr + bu_s[jt, r, H:2 * H]
                if store:
                    x_ref[r, 0:H] = nr
                    x_ref[r, H:2 * H] = ni
                return nr, ni

            xr, xi = lax.fori_loop(0, tj, step, (st_s[:, 0:H], st_s[:, H:2 * H]))
            st_s[:, 0:H] = xr
            st_s[:, H:2 * H] = xi

        @pl.when(ph == 0)
        def _():
            scan(False)

            @pl.when(jt == njt - 1)
            def _():
                pr, pi = _power(lr[0:1], li[0:1], J)
                xin_s[0:1, :] = jnp.zeros((1, 2 * H), F32)
                for c in range(1, N_CHUNK):
                    qr, qi = _cmul(pr, pi, xin_s[c - 1:c, 0:H], xin_s[c - 1:c, H:2 * H])
                    xin_s[c:c + 1, 0:H] = qr + st_s[c - 1:c, 0:H]
                    xin_s[c:c + 1, H:2 * H] = qi + st_s[c - 1:c, H:2 * H]
                xin_ref[...] = xin_s[...]
                st_s[...] = xin_s[...]

        @pl.when(ph == 1)
        def _():
            scan(True)
            y = _dot(x_ref[...].astype(BF16), cm_ref[...].astype(BF16)) + d_ref[...] * u_ref[...]
            y_ref[...] = y
            gel_ref[...] = jax.nn.gelu(y).astype(gel_ref.dtype)

    return pl.pallas_call(
        body, name="ssm_fwd", grid=(SSM_COLS, B, 2, njt),
        in_specs=[
            pl.BlockSpec((None, R, LANES), lambda i, b, ph, jt: (b, jt, i)),
            pl.BlockSpec((None, LANES, 2 * H), lambda i, b, ph, jt: (i, 0, 0)),
            pl.BlockSpec((None, N_CHUNK, H), lambda i, b, ph, jt: (i, 0, 0)),
            pl.BlockSpec((None, N_CHUNK, H), lambda i, b, ph, jt: (i, 0, 0)),
            pl.BlockSpec((None, 2 * H, LANES), lambda i, b, ph, jt: (i, 0, 0)),
            pl.BlockSpec((1, LANES), lambda i, b, ph, jt: (0, i)),
        ],
        out_specs=[
            pl.BlockSpec((None, R, LANES), lambda i, b, ph, jt: (b, jt * ph, i)),
            pl.BlockSpec((None, R, LANES), lambda i, b, ph, jt: (b, jt * ph, i)),
            pl.BlockSpec((None, R, 2 * H), lambda i, b, ph, jt: (b, jt * ph, i)),
            pl.BlockSpec((None, None, N_CHUNK, 2 * H), lambda i, b, ph, jt: (b, i, 0, 0)),
        ],
        out_shape=[
            jax.ShapeDtypeStruct((B, L, SSM_WIDTH), F32),
            jax.ShapeDtypeStruct((B, L, SSM_WIDTH), BF16),
            jax.ShapeDtypeStruct((B, L, SSM_COLS * 2 * H), F32),
            jax.ShapeDtypeStruct((B, SSM_COLS, N_CHUNK, 2 * H), F32),
        ],
        scratch_shapes=[pltpu.VMEM((njt, R, 2 * H), F32), pltpu.VMEM((N_CHUNK, 2 * H), F32),
                        pltpu.VMEM((N_CHUNK, 2 * H), F32)],
        compiler_params=_params(("arbitrary",) * 4),
    )(u_p, w_b, lam_r, lam_i, c_m, d_skip)


def _ssm_bwd(dy_p, u_p, x, xin, w_bt, lam_r, lam_i, c_mt, d_skip, B, L, tj):
    J = L // N_CHUNK
    njt = J // tj
    R = tj * N_CHUNK
    H = 512
    x4 = x.reshape(B, J, N_CHUNK, SSM_COLS * 2 * H)

    def body(dy_ref, u_ref, x_ref, xp_ref, xin_ref, wbt_ref, lr_ref, li_ref, cmt_ref, d_ref,
             du_ref, dwb_ref, dcm_ref, dlr_ref, dli_ref, dd_ref, ca_s, a_s, st_s, dl_s):
        b, ph, jt = pl.program_id(1), pl.program_id(2), pl.program_id(3)
        jr = njt - 1 - jt
        lr, li = lr_ref[...], -li_ref[...]

        @pl.when(jnp.logical_and(b == 0, jnp.logical_and(ph == 0, jt == 0)))
        def _():
            dwb_ref[...] = jnp.zeros_like(dwb_ref)
            dcm_ref[...] = jnp.zeros_like(dcm_ref)
            dlr_ref[...] = jnp.zeros_like(dlr_ref)
            dli_ref[...] = jnp.zeros_like(dli_ref)
            dd_ref[...] = jnp.zeros_like(dd_ref)
            dl_s[...] = jnp.zeros_like(dl_s)

        @pl.when(jnp.logical_and(ph == 0, jt == 0))
        def _():
            st_s[...] = jnp.zeros_like(st_s)

        @pl.when(ph == 0)
        def _():
            ca_s[jt] = _dot(dy_ref[...].astype(BF16), cmt_ref[...].astype(BF16))

        def scan(store):
            def step(n, carry):
                ar, ai = carry
                r = pl.ds(pl.multiple_of((tj - 1 - n) * N_CHUNK, N_CHUNK), N_CHUNK)
                nr = lr * ar - li * ai + ca_s[jt, r, 0:H]
                ni = lr * ai + li * ar + ca_s[jt, r, H:2 * H]
                if store:
                    a_s[r, 0:H] = nr
                    a_s[r, H:2 * H] = ni
                return nr, ni

            ar, ai = lax.fori_loop(0, tj, step, (st_s[:, 0:H], st_s[:, H:2 * H]))
            st_s[:, 0:H] = ar
            st_s[:, H:2 * H] = ai

        @pl.when(ph == 0)
        def _():
            scan(False)

            @pl.when(jt == njt - 1)
            def _():
                pr, pi = _power(lr[0:1], li[0:1], J)
                a_s[N_CHUNK - 1:N_CHUNK, :] = jnp.zeros((1, 2 * H), F32)
                for c in range(N_CHUNK - 2, -1, -1):
                    qr, qi = _cmul(pr, pi, a_s[c + 1:c + 2, 0:H], a_s[c + 1:c + 2, H:2 * H])
                    a_s[c:c + 1, 0:H] = qr + st_s[c + 1:c + 2, 0:H]
                    a_s[c:c + 1, H:2 * H] = qi + st_s[c + 1:c + 2, H:2 * H]
                st_s[...] = a_s[0:N_CHUNK, :]

        @pl.when(ph == 1)
        def _():
            scan(True)
            dy = dy_ref[...]
            u = u_ref[...]
            a_b = a_s[...].astype(BF16)
            du_ref[...] = (_dot(a_b, wbt_ref[...].astype(BF16)) + d_ref[...] * dy).astype(du_ref.dtype)
            dwb_ref[...] += _dot(u.astype(BF16), a_b, _TN)
            dcm_ref[...] += _dot(x_ref[...].astype(BF16), dy.astype(BF16), _TN)
            dd_ref[...] += jnp.sum(dy * u, axis=0, keepdims=True)

            first = jnp.where(jr == 0, xin_ref[...], xp_ref[...])
            a0r, a0i = a_s[0:N_CHUNK, 0:H], a_s[0:N_CHUNK, H:2 * H]
            acc0 = (a0r * first[:, 0:H] + a0i * first[:, H:2 * H], a0i * first[:, 0:H] - a0r * first[:, H:2 * H])

            def step(j, carry):
                sr, si = carry
                r = pl.ds(pl.multiple_of(j * N_CHUNK, N_CHUNK), N_CHUNK)
                rp = pl.ds(pl.multiple_of((j - 1) * N_CHUNK, N_CHUNK), N_CHUNK)
                ar, ai = a_s[r, 0:H], a_s[r, H:2 * H]
                xr, xi = x_ref[rp, 0:H], x_ref[rp, H:2 * H]
                return sr + ar * xr + ai * xi, si + ai * xr - ar * xi

            sr, si = lax.fori_loop(1, tj, step, acc0)
            dl_s[:, 0:H] += sr
            dl_s[:, H:2 * H] += si

            @pl.when(jnp.logical_and(b == B - 1, jt == njt - 1))
            def _():
                dlr_ref[...] = jnp.sum(dl_s[:, 0:H], axis=0, keepdims=True)
                dli_ref[...] = jnp.sum(dl_s[:, H:2 * H], axis=0, keepdims=True)
                dl_s[...] = jnp.zeros_like(dl_s)

    rev = lambda ph, jt: (njt - 1 - jt) * ph + (njt - 1) * (1 - ph)
    return pl.pallas_call(
        body, name="ssm_bwd", grid=(SSM_COLS, B, 2, njt),
        in_specs=[
            pl.BlockSpec((None, R, LANES), lambda i, b, ph, jt: (b, njt - 1 - jt, i)),
            pl.BlockSpec((None, R, LANES), lambda i, b, ph, jt: (b, njt - 1 - jt, i)),
            pl.BlockSpec((None, R, 2 * H), lambda i, b, ph, jt: (b, rev(ph, jt), i)),
            pl.BlockSpec((None, None, N_CHUNK, 2 * H),
                         lambda i, b, ph, jt: (b, jnp.maximum((njt - 1 - jt) * tj - 1, 0), 0, i)),
            pl.BlockSpec((None, None, N_CHUNK, 2 * H), lambda i, b, ph, jt: (b, i, 0, 0)),
            pl.BlockSpec((None, 2 * H, LANES), lambda i, b, ph, jt: (i, 0, 0)),
            pl.BlockSpec((None, N_CHUNK, H), lambda i, b, ph, jt: (i, 0, 0)),
            pl.BlockSpec((None, N_CHUNK, H), lambda i, b, ph, jt: (i, 0, 0)),
            pl.BlockSpec((None, LANES, 2 * H), lambda i, b, ph, jt: (i, 0, 0)),
            pl.BlockSpec((1, LANES), lambda i, b, ph, jt: (0, i)),
        ],
        out_specs=[
            pl.BlockSpec((None, R, LANES), lambda i, b, ph, jt: (b, rev(ph, jt), i)),
            pl.BlockSpec((None, LANES, 2 * H), lambda i, b, ph, jt: (i, 0, 0)),
            pl.BlockSpec((None, 2 * H, LANES), lambda i, b, ph, jt: (i, 0, 0)),
            pl.BlockSpec((None, 1, H), lambda i, b, ph, jt: (i, 0, 0)),
            pl.BlockSpec((None, 1, H), lambda i, b, ph, jt: (i, 0, 0)),
            pl.BlockSpec((1, LANES), lambda i, b, ph, jt: (0, i)),
        ],
        out_shape=[
            jax.ShapeDtypeStruct((B, L, SSM_WIDTH), BF16),
            jax.ShapeDtypeStruct((SSM_COLS, LANES, 2 * H), F32),
            jax.ShapeDtypeStruct((SSM_COLS, 2 * H, LANES), F32),
            jax.ShapeDtypeStruct((SSM_COLS, 1, H), F32),
            jax.ShapeDtypeStruct((SSM_COLS, 1, H), F32),
            jax.ShapeDtypeStruct((1, SSM_WIDTH), F32),
        ],
        scratch_shapes=[pltpu.VMEM((njt, R, 2 * H), F32), pltpu.VMEM((R, 2 * H), F32),
                        pltpu.VMEM((N_CHUNK, 2 * H), F32), pltpu.VMEM((N_CHUNK, 2 * H), F32)],
        compiler_params=_params(("arbitrary",) * 4),
    )(dy_p, u_p, x, x4, xin, w_bt, lam_r, lam_i, c_mt, d_skip)


def _to_scan_layout(t, B, L):
    C = t.shape[-1]
    return t.reshape(B, N_CHUNK, L // N_CHUNK, C).transpose(0, 2, 1, 3).reshape(B, L, C)


def _from_scan_layout(t, B, L):
    C = t.shape[-1]
    return t.reshape(B, L // N_CHUNK, N_CHUNK, C).transpose(0, 2, 1, 3).reshape(B * L, C)


def _local_step(x, target, p, w_in, late_weights, mlp_grads_ready=None, rest_grads_ready=None, order=None, *,
                ssm_tile=128):
    B, L, D = x.shape
    T = B * L
    x2 = x.reshape(T, D)
    row = lambda v: v.reshape(1, -1)
    g1, g2, ga, gs, b_glu = row(p["norm1_g"]), row(p["norm2_g"]), row(p["attn_out_g"]), row(p["ssm_out_g"]), row(p["b_glu"])
    g1_first = g1 if order is None else g1 + order
    gq8 = jnp.tile(row(p["q_norm_g"]), (1, N_HEADS))
    gk8 = jnp.tile(row(p["k_norm_g"]), (1, N_HEADS))

    G, P, Hh = SSM_GROUPS, SSM_STATE, SSM_GROUP
    lam_re3, lam_im3 = p["ssm_lambda_re"].reshape(G, 1, P), p["ssm_lambda_im"].reshape(G, 1, P)
    log_dt3 = p["ssm_log_dt"].reshape(G, 1, 1)
    b_re_t, b_im_t = p["ssm_b_re"].transpose(0, 2, 1), p["ssm_b_im"].transpose(0, 2, 1)
    lbr, lbi, bbr, bbi = _ssm_prep(lam_re3, lam_im3, log_dt3, b_re_t, b_im_t)
    w_b = jnp.concatenate([_block_diag(bbr), _block_diag(bbi)], axis=2)
    c_mt = jnp.concatenate([_block_diag(p["ssm_c_re"]), -_block_diag(p["ssm_c_im"])], axis=2)
    w_bt, c_m = w_b.transpose(0, 2, 1), c_mt.transpose(0, 2, 1)
    lam_r = jnp.broadcast_to(lbr.reshape(SSM_COLS, 1, 512), (SSM_COLS, N_CHUNK, 512))
    lam_i = jnp.broadcast_to(lbi.reshape(SSM_COLS, 1, 512), (SSM_COLS, N_CHUNK, 512))
    d_skip = p["ssm_d"].reshape(1, SSM_WIDTH)

    proj, xn = _matmul("proj", x2, w_in, prologue=(_rms, g1_first), tn=w_in.shape[1])
    qn, kn, vb = _qk_norm(proj, gq8, gk8)
    sb, attn_kept = _attn_fwd(qn, kn, vb, B, L)
    u_p = _to_scan_layout(proj[:, 3 * SB_WIDTH:], B, L)
    y_p, gel_p, xs, xin = _ssm_fwd(u_p, w_b, lam_r, lam_i, c_m, d_skip, B, L, ssm_tile)
    y2, gel = y_p.reshape(T, SSM_WIDTH), gel_p.reshape(T, SSM_WIDTH)
    w_glu, w_out, w_mlp_in, w_mlp_out = late_weights(gel)
    pre = _matmul("glu_gate", gel, w_glu)
    ssm_n = _rowwise("glu_out", _glu_branch, [y2, pre], [b_glu, gs], [(SSM_WIDTH, BF16)])
    sb_n = _rowwise("attn_out_norm", _rms, [sb], [ga], [(SB_WIDTH, BF16)])
    mixed = jnp.concatenate([sb_n, _from_scan_layout(ssm_n, B, L)], axis=1)
    def residual_and_norm(acc, res, g):
        h = acc + res
        return h, _rms(h, g)

    h1, hn = _matmul("out_proj", mixed, w_out, extras=[x2, g2], out_dtypes=(F32, BF16), tn=D,
                     epilogue=residual_and_norm)
    act, a_pre = _matmul("mlp_in", hn, w_mlp_in, out_dtypes=(BF16, BF16), tn=1024,
                         epilogue=lambda acc: (jnp.square(jnp.maximum(acc, 0.0)), acc))

    def loss_fn(acc, h, t):
        diff = acc + h - t
        part = jnp.sum(jnp.sum(diff * diff, axis=0, keepdims=True), axis=1, keepdims=True)
        d = diff * (1.0 / D)
        return d, d, part * (0.5 / D)

    d_out, d_out_b, loss = _matmul("mlp_out", act, w_mlp_out, extras=[h1, target.reshape(T, D)],
                                   out_dtypes=(F32, BF16), sums=[(1, 1)], epilogue=loss_fn, tm=512, tn=D)

    d_apre = _matmul("mlp_out_dx", d_out_b, w_mlp_out, tb=True, extras=[a_pre], out_dtypes=(BF16,), tn=1024,
                     epilogue=lambda acc, ap: (acc * (2.0 * jnp.maximum(ap.astype(F32), 0.0)),))
    both = lambda acc: (acc, acc)
    g_w_mlp_out, g_w_mlp_out_b = _matmul("mlp_out_dw", act, d_out_b, ta=True, out_dtypes=(F32, BF16), epilogue=both)
    g_w_mlp_in, g_w_mlp_in_b = _matmul("mlp_in_dw", hn, d_apre, ta=True, col_blocked=True, out_dtypes=(F32, BF16),
                                       epilogue=both, tn=w_mlp_in.shape[1] // N_DEV)
    if mlp_grads_ready is not None:
        g2 = g2 + mlp_grads_ready(g_w_mlp_out, g_w_mlp_out_b, g_w_mlp_in, g_w_mlp_in_b)

    def norm_bwd_res(dy, h, res, g):
        _, vjp = jax.vjp(_rms, h, g)
        dh, dg = vjp(dy)
        return res + dh, dg

    def norm_bwd_res2(dy, h, res, g):
        d, dg = norm_bwd_res(dy, h, res, g)
        return d, d, dg

    d_h1, d_h1_b, g_norm2 = _matmul("mlp_in_dx", d_apre, w_mlp_in, tb=True, extras=[h1, d_out, g2],
                                    out_dtypes=(F32, BF16), sums=[(1, D)], epilogue=norm_bwd_res2, tm=512, tn=D)

    d_mixed = _matmul("out_proj_dx", d_h1_b, w_out, tb=True, tn=1024)
    g_w_out, g_w_out_b = _matmul("out_proj_dw", mixed, d_h1_b, ta=True, out_dtypes=(F32, BF16), epilogue=both)

    def norm_bwd(h, dy, g):
        _, vjp = jax.vjp(_rms, h, g)
        return vjp(dy)

    d_sb, g_attn_out = _rowwise("attn_out_norm_bwd", norm_bwd, [sb, (d_mixed, 0, SB_WIDTH)], [ga],
                                [(SB_WIDTH, F32)], sums=[(1, SB_WIDTH)])
    d_ssm_n = _to_scan_layout(d_mixed[:, SB_WIDTH:], B, L).reshape(T, SSM_WIDTH)

    def glu_bwd(y, pre_, dy, bg, g):
        _, vjp = jax.vjp(_glu_branch, y, pre_, bg, g)
        d_y, d_pre, d_bg, d_g = vjp(dy)
        return d_y, d_pre, d_bg, d_g

    d_y_direct, d_pre, g_b_glu, g_ssm_out = _rowwise(
        "glu_out_bwd", glu_bwd, [y2, pre, d_ssm_n], [b_glu, gs], [(SSM_WIDTH, F32), (SSM_WIDTH, BF16)],
        sums=[(1, SSM_WIDTH), (1, SSM_WIDTH)])
    g_w_glu, g_w_glu_b = _matmul("glu_gate_dw", gel, d_pre, ta=True, out_dtypes=(F32, BF16), epilogue=both)

    def gelu_bwd(dg, y, dy0):
        _, vjp = jax.vjp(jax.nn.gelu, y)
        return (dy0 + vjp(dg)[0],)

    d_y = _matmul("glu_gate_dx", d_pre, w_glu, tb=True, extras=[y2, d_y_direct], epilogue=gelu_bwd)

    du_p, d_wb, d_cm, d_lr, d_li, g_d = _ssm_bwd(
        d_y.reshape(B, L, SSM_WIDTH), u_p, xs, xin, w_bt, lam_r, lam_i, c_mt, d_skip, B, L, ssm_tile)
    d_bb = _block_diag_take(d_wb.reshape(SSM_COLS, LANES, 2, 512))
    d_c = _block_diag_take(d_cm.transpose(0, 2, 1).reshape(SSM_COLS, LANES, 2, 512))
    g_lam_re, g_lam_im, g_log_dt, g_b_re_t, g_b_im_t = _ssm_prep_bwd(
        lam_re3, lam_im3, log_dt3, b_re_t, b_im_t,
        d_lr.reshape(G, 1, P), d_li.reshape(G, 1, P), d_bb[0], d_bb[1])
    d_qn, d_kn, d_v = _attn_bwd(qn, kn, vb, attn_kept, d_sb, B, L)
    d_q, d_k, g_q, g_k = _qk_norm_bwd(proj, gq8, gk8, d_qn, d_kn)

    d_proj = jnp.concatenate([d_q, d_k, d_v, _from_scan_layout(du_p, B, L)], axis=1)
    g_w_in, g_w_in_b = _matmul("proj_dw", xn, d_proj, ta=True, col_blocked=True, out_dtypes=(F32, BF16),
                               epilogue=both, tn=w_in.shape[1] // N_DEV)
    if rest_grads_ready is not None:
        g1 = g1 + rest_grads_ready([g_w_in, g_w_glu, g_w_out], [g_w_in_b, g_w_glu_b, g_w_out_b])
    grad_x, g_norm1 = _matmul("proj_dx", d_proj, w_in, tb=True, extras=[x2, d_h1, g1], sums=[(1, D)],
                              epilogue=norm_bwd_res, tm=512, tn=D)

    small = {
        "norm1_g": g_norm1.reshape(-1),
        "q_norm_g": g_q.reshape(-1),
        "k_norm_g": g_k.reshape(-1),
        "ssm_lambda_re": g_lam_re.reshape(G, P),
        "ssm_lambda_im": g_lam_im.reshape(G, P),
        "ssm_log_dt": g_log_dt.reshape(G),
        "ssm_b_re": g_b_re_t.transpose(0, 2, 1),
        "ssm_b_im": g_b_im_t.transpose(0, 2, 1),
        "ssm_c_re": d_c[0],
        "ssm_c_im": -d_c[1],
        "ssm_d": g_d.reshape(G, Hh),
        "b_glu": g_b_glu.reshape(-1),
        "attn_out_g": g_attn_out.reshape(-1),
        "ssm_out_g": g_ssm_out.reshape(-1),
        "norm2_g": g_norm2.reshape(-1),
    }
    big = {"w_in": g_w_in, "w_glu": g_w_glu, "w_out": g_w_out, "w_mlp_in": g_w_mlp_in, "w_mlp_out": g_w_mlp_out}
    return loss[0, 0], grad_x.reshape(B, L, D), small, big


_ANY = pl.BlockSpec(memory_space=pl.ANY)
_MESH = pl.DeviceIdType.MESH


def _all_gather(name, shards):
    n = len(shards)

    def body(*refs):
        in_refs, out_refs = refs[:n], refs[n:2 * n]
        send_sems, recv_sems, local_sems = refs[2 * n:]
        x, y, c = lax.axis_index("x"), lax.axis_index("y"), lax.axis_index("c")
        me, sibling = (x, y, c), (x, y, 1 - c)
        chips = [(1 - x, y), (x, 1 - y), (1 - x, 1 - y)]

        def copy(a, k, block, to, src=None):
            px, py, pc = block
            rows = out_refs[a].at[4 * px + 2 * py + pc]
            return pltpu.make_async_remote_copy(
                src_ref=rows if src is None else src, dst_ref=rows, send_sem=send_sems.at[a, k],
                recv_sem=recv_sems.at[a, k], device_id=to, device_id_type=_MESH)

        mine = [pltpu.make_async_copy(in_refs[a], out_refs[a].at[4 * x + 2 * y + c], local_sems.at[a]) for a in range(n)]
        first, passed = [], []
        for a in range(n):
            mine[a].start()
            first.append(copy(a, 0, me, sibling, src=in_refs[a]))
            first += [copy(a, 1 + j, me, (*chip, c), src=in_refs[a]) for j, chip in enumerate(chips)]
        for cp in first:
            cp.start()
        for j, chip in enumerate(chips):
            for a in range(n):
                copy(a, 1 + j, (*chip, c), me).wait_recv()
                fwd = copy(a, 4 + j, (*chip, c), sibling)
                fwd.start()
                passed.append(fwd)
        for a in range(n):
            copy(a, 0, sibling, me).wait_recv()
            for j, chip in enumerate(chips):
                copy(a, 4 + j, (*chip, 1 - c), me).wait_recv()
        for cp in first + passed:
            cp.wait_send()
        for cp in mine:
            cp.wait()

    return pl.pallas_call(
        body, name=name,
        in_specs=[_ANY] * n, out_specs=[_ANY] * n,
        out_shape=[jax.ShapeDtypeStruct((N_DEV, *s.shape), s.dtype) for s in shards],
        scratch_shapes=[pltpu.SemaphoreType.DMA((n, 7)), pltpu.SemaphoreType.DMA((n, 7)), pltpu.SemaphoreType.DMA((n,))],
    )(*shards)


_HBM = pl.BlockSpec(memory_space=pltpu.HBM)
_SEM = pl.BlockSpec(memory_space=pltpu.SEMAPHORE)
_EFFECT = pltpu.SideEffectType.DATAFLOW_SIDE_EFFECTING
_FLIPS = [(dx, dy, dc) for dx in (0, 1) for dy in (0, 1) for dc in (0, 1) if (dx, dy, dc) != (0, 0, 0)]


def _exchange_start(name, srcs, lands, per_peer):
    n = len(srcs)

    def body(*refs):
        src_refs, land_refs = refs[:n], refs[n:2 * n]
        send_sems, recv_sems = refs[2 * n:3 * n], refs[3 * n:4 * n]
        token = refs[-1]
        x, y, c = lax.axis_index("x"), lax.axis_index("y"), lax.axis_index("c")
        me = 4 * x + 2 * y + c
        for dx, dy, dc in _FLIPS:
            px, py, pc = (1 - x if dx else x), (1 - y if dy else y), (1 - c if dc else c)
            for a in range(n):
                pltpu.make_async_remote_copy(
                    src_ref=src_refs[a].at[4 * px + 2 * py + pc] if per_peer else src_refs[a],
                    dst_ref=land_refs[a].at[me], send_sem=send_sems[a], recv_sem=recv_sems[a],
                    device_id=(px, py, pc), device_id_type=_MESH).start()
        token[...] = jnp.zeros_like(token)

    hbm = lambda t: pltpu.with_memory_space_constraint(t, pltpu.HBM)
    res = pl.pallas_call(
        body, name=name,
        out_shape=(*[pltpu.SemaphoreType.DMA(())] * (2 * n), *[pltpu.HBM(t.shape, t.dtype) for t in (*srcs, *lands)],
                   jax.ShapeDtypeStruct((8, LANES), F32)),
        in_specs=[_HBM] * (2 * n),
        out_specs=(*[_SEM] * (2 * n), *[_HBM] * (2 * n), pl.BlockSpec(memory_space=pltpu.VMEM)),
        input_output_aliases={i: 2 * n + i for i in range(2 * n)},
        compiler_params=pltpu.CompilerParams(has_side_effects=_EFFECT),
    )(*[hbm(t) for t in (*srcs, *lands)])
    return res[:-1], res[-1]


def _exchange_wait(name, handle, after):
    n = len(handle) // 4
    sems, thru = handle[:2 * n], handle[2 * n:]

    def body(*refs):
        land_refs = refs[n:2 * n]
        send_sems, recv_sems = refs[2 * n:3 * n], refs[3 * n:4 * n]
        me = (lax.axis_index("x"), lax.axis_index("y"), lax.axis_index("c"))
        for a in range(n):
            seven = land_refs[a].at[pl.ds(0, len(_FLIPS))]
            all_copies = pltpu.make_async_remote_copy(
                src_ref=seven, dst_ref=seven, send_sem=send_sems[a], recv_sem=recv_sems[a], device_id=me,
                device_id_type=_MESH)
            all_copies.wait_send()
            all_copies.wait_recv()

    res = pl.pallas_call(
        body, name=name, out_shape=tuple(pltpu.HBM(t.shape, t.dtype) for t in thru),
        in_specs=[*[_HBM] * (2 * n), *[_SEM] * (2 * n), _ANY], out_specs=tuple([_HBM] * (2 * n)),
        input_output_aliases={i: i for i in range(2 * n)},
        compiler_params=pltpu.CompilerParams(has_side_effects=_EFFECT),
    )(*thru, *sems, after)
    return res[n:]


def _adamw_gathered(name, own, parts, me, w, m, v):
    r, c = w.shape
    tr = min(r, 256)

    def body(me_ref, own_ref, p_ref, w_ref, m_ref, v_ref, g_out, d_out, m_out, v_out):
        g = own_ref[...]
        for j in range(N_DEV):
            g = g + p_ref[j].astype(F32)
        delta, m_new, v_new = _adamw(w_ref[...], g, m_ref[...], v_ref[...])
        g_out[...] = g
        d_out[...] = delta
        m_out[...] = m_new
        v_out[...] = v_new

    spec = pl.BlockSpec((tr, c), lambda i, me_ref: (i, 0))
    return pl.pallas_call(
        body, name=name,
        grid_spec=pltpu.PrefetchScalarGridSpec(
            num_scalar_prefetch=1, grid=(r // tr,),
            in_specs=[pl.BlockSpec((None, tr, c), lambda i, me_ref: (me_ref[0], i, 0)),
                      pl.BlockSpec((N_DEV, tr, c), lambda i, me_ref: (0, i, 0)), spec, spec, spec],
            out_specs=[spec] * 4),
        out_shape=[jax.ShapeDtypeStruct((r, c), F32)] * 4,
        compiler_params=_params(("parallel",)),
    )(me, own, parts, w, m, v)


def _adamw(w, g, m, v):
    m = ADAM_B1 * m + (1.0 - ADAM_B1) * g
    v = ADAM_B2 * v + (1.0 - ADAM_B2) * jnp.square(g)
    m_hat = m / (1.0 - ADAM_B1 ** ADAM_STEP)
    v_hat = v / (1.0 - ADAM_B2 ** ADAM_STEP)
    delta = -ADAM_LR * (m_hat / (jnp.sqrt(v_hat) + ADAM_EPS) + ADAM_WD * w)
    return delta, m, v


def _adamw_small(name, parts, w, m, v):
    _, r, c = parts.shape
    tr = 8

    def body(p_ref, w_ref, m_ref, v_ref, g_out, d_out, m_out, v_out):
        g = p_ref[0]
        for j in range(1, N_DEV):
            g = g + p_ref[j]
        delta, m_new, v_new = _adamw(w_ref[...], g, m_ref[...], v_ref[...])
        g_out[...] = g
        d_out[...] = delta
        m_out[...] = m_new
        v_out[...] = v_new

    spec = pl.BlockSpec((tr, c), lambda i: (i, 0))
    return pl.pallas_call(
        body, name=name, grid=(r // tr,),
        in_specs=[pl.BlockSpec((N_DEV, tr, c), lambda i: (0, i, 0)), spec, spec, spec],
        out_specs=[spec] * 4, out_shape=[jax.ShapeDtypeStruct((r, c), F32)] * 4,
        compiler_params=_params(("parallel",)),
    )(parts, w, m, v)


_WEIGHTS = ["norm1_g", "w_in", "q_norm_g", "k_norm_g", "ssm_lambda_re", "ssm_lambda_im", "ssm_log_dt", "ssm_b_re",
            "ssm_b_im", "ssm_c_re", "ssm_c_im", "ssm_d", "w_glu", "b_glu", "attn_out_g", "ssm_out_g", "w_out",
            "norm2_g", "w_mlp_in", "w_mlp_out"]
_BIG = ["w_in", "w_glu", "w_out", "w_mlp_in", "w_mlp_out"]
_SMALL = [n for n in _WEIGHTS if n not in _BIG]
_PACK_COLS = 1024


def _pack(tree, last=None):
    flat = [tree[n].reshape(-1).astype(F32) for n in _SMALL]
    size = sum(f.shape[0] for f in flat)
    rows = -(-(size + 1) // (_PACK_COLS * 8)) * 8
    pad = jnp.zeros((rows * _PACK_COLS - size - 1,), F32)
    tail = jnp.zeros((1,), F32) if last is None else last.reshape(1).astype(F32)
    return jnp.concatenate(flat + [pad, tail]).reshape(rows, _PACK_COLS)


def _unpack(buf, like):
    flat, out, off = buf.reshape(-1), {}, 0
    for n in _SMALL:
        size = like[n].size
        out[n] = flat[off:off + size].reshape(like[n].shape)
        off += size
    return out


def kernel(x, norm1_g, w_in, q_norm_g, k_norm_g, ssm_lambda_re, ssm_lambda_im, ssm_log_dt, ssm_b_re, ssm_b_im, ssm_c_re, ssm_c_im, ssm_d, w_glu, b_glu, attn_out_g, ssm_out_g, w_out, norm2_g, w_mlp_in, w_mlp_out, loss_target, m_norm1_g, m_w_in, m_q_norm_g, m_k_norm_g, m_ssm_lambda_re, m_ssm_lambda_im, m_ssm_log_dt, m_ssm_b_re, m_ssm_b_im, m_ssm_c_re, m_ssm_c_im, m_ssm_d, m_w_glu, m_b_glu, m_attn_out_g, m_ssm_out_g, m_w_out, m_norm2_g, m_w_mlp_in, m_w_mlp_out, v_norm1_g, v_w_in, v_q_norm_g, v_k_norm_g, v_ssm_lambda_re, v_ssm_lambda_im, v_ssm_log_dt, v_ssm_b_re, v_ssm_b_im, v_ssm_c_re, v_ssm_c_im, v_ssm_d, v_w_glu, v_b_glu, v_attn_out_g, v_ssm_out_g, v_w_out, v_norm2_g, v_w_mlp_in, v_w_mlp_out):
    w = dict(zip(_WEIGHTS, (norm1_g, w_in, q_norm_g, k_norm_g, ssm_lambda_re, ssm_lambda_im, ssm_log_dt, ssm_b_re, ssm_b_im, ssm_c_re, ssm_c_im, ssm_d, w_glu, b_glu, attn_out_g, ssm_out_g, w_out, norm2_g, w_mlp_in, w_mlp_out)))
    m = dict(zip(_WEIGHTS, (m_norm1_g, m_w_in, m_q_norm_g, m_k_norm_g, m_ssm_lambda_re, m_ssm_lambda_im, m_ssm_log_dt, m_ssm_b_re, m_ssm_b_im, m_ssm_c_re, m_ssm_c_im, m_ssm_d, m_w_glu, m_b_glu, m_attn_out_g, m_ssm_out_g, m_w_out, m_norm2_g, m_w_mlp_in, m_w_mlp_out)))
    v = dict(zip(_WEIGHTS, (v_norm1_g, v_w_in, v_q_norm_g, v_k_norm_g, v_ssm_lambda_re, v_ssm_lambda_im, v_ssm_log_dt, v_ssm_b_re, v_ssm_b_im, v_ssm_c_re, v_ssm_c_im, v_ssm_d, v_w_glu, v_b_glu, v_attn_out_g, v_ssm_out_g, v_w_out, v_norm2_g, v_w_mlp_in, v_w_mlp_out)))
    core = lax.axis_index("c").astype(jnp.int32).reshape(1)
    chip = (2 * lax.axis_index("x") + lax.axis_index("y")).astype(jnp.int32).reshape(1)

    me = (2 * chip + core).astype(jnp.int32)

    def landing(own=None, like=None):
        own = jnp.zeros_like(like) if own is None else own
        return lax.dynamic_update_slice(lax.empty((N_DEV, *like.shape), like.dtype), own[None], (me[0], 0, 0))

    (w_in_blocks,) = _all_gather("w_in_all_gather", [w_in.astype(BF16)])
    w_in_full = w_in_blocks.transpose(1, 0, 2).reshape(w_in.shape[0], -1)
    late = [n for n in _BIG if n != "w_in"]
    shards = [w[n].astype(BF16) for n in late]
    w_in_blocks, shards = lax.optimization_barrier((w_in_blocks, shards))
    weights_handle, weights_token = _exchange_start(
        "weights_send", shards, [landing(s, s) for s in shards], per_peer=False)

    def late_weights(after):
        got = dict(zip(late, _exchange_wait("weights_arrive", weights_handle, after)))
        return (got["w_glu"].reshape(-1, w_glu.shape[1]), got["w_out"].reshape(-1, w_out.shape[1]),
                got["w_mlp_in"].transpose(1, 0, 2).reshape(w_mlp_in.shape[0], -1),
                got["w_mlp_out"].reshape(-1, w_mlp_out.shape[1]))

    mlp = ["w_mlp_out", "w_mlp_in"]
    sent = {}

    def send_grads(name, names, own, own_b):
        blocks = lambda g, n: g.reshape(N_DEV, *w[n].shape)
        sent[name + "_own"] = [blocks(g, n) for g, n in zip(own, names)]
        srcs = [blocks(g, n) for g, n in zip(own_b, names)]
        sent[name], token = _exchange_start(name, srcs, [landing(like=s[0]) for s in srcs], per_peer=True)
        return token[0, 0]

    def mlp_grads_ready(g_out, g_out_b, g_in, g_in_b):
        return send_grads("mlp_grads_send", mlp, [g_out, g_in], [g_out_b, g_in_b])

    rest = ["w_in", "w_glu", "w_out"]

    def rest_grads_ready(own, own_b):
        return send_grads("rest_grads_send", rest, own, own_b)

    loss_local, grad_x, g_small, g_big = _local_step(
        x, loss_target, {n: w[n] for n in _SMALL}, w_in_full, late_weights, mlp_grads_ready, rest_grads_ready,
        weights_token[0, 0])

    grads, delta, new_m, new_v = {}, {}, {}, {}
    small = _pack(g_small, last=loss_local)
    small_handle, small_token = _exchange_start("small_grads_send", [small], [landing(small, small)], per_peer=False)

    for send, arrive, names in (("mlp_grads_send", "mlp_grads_arrive", mlp),
                                ("rest_grads_send", "rest_grads_arrive", rest)):
        for n, own, part in zip(names, sent[send + "_own"], _exchange_wait(arrive, sent[send], small_token)):
            grads[n], delta[n], new_m[n], new_v[n] = _adamw_gathered("adamw_" + n, own, part, me, w[n], m[n], v[n])

    shards_done = lax.optimization_barrier(tuple(new_v[n] for n in _BIG))
    (small_parts,) = _exchange_wait("small_grads_arrive", small_handle, shards_done[-1])
    packed = _adamw_small("adamw_small", small_parts, _pack(w), _pack(m), _pack(v))
    for tree, buf in zip((grads, delta, new_m, new_v), packed):
        tree.update(_unpack(buf, w))
    loss = packed[0][-1, -1]

    return (loss, grad_x, *[grads[n] for n in _WEIGHTS], *[delta[n] for n in _WEIGHTS],
            *[new_m[n] for n in _WEIGHTS], *[new_v[n] for n in _WEIGHTS])
```

```python
import functools
import math

import jax
import jax.numpy as jnp
from jax import lax
from jax.experimental import pallas as pl
from jax.experimental.pallas import tpu as pltpu

F32 = jnp.float32
BF16 = jnp.bfloat16

EPS = 1e-6
HEAD_DIM = 64
N_HEADS = 8
SB_WIDTH = 512
SSM_WIDTH = 512
SSM_GROUP = 16
SSM_GROUPS = 32
SSM_STATE = 64
QBLOCK = 128
KBLOCK = 256
N_CHUNK = 8
SSM_COLS = 4
LANES = 128
N_DEV = 8

ADAM_LR = 0.001
ADAM_B1 = 0.9
ADAM_B2 = 0.999
ADAM_EPS = 1e-08
ADAM_WD = 0.01
ADAM_STEP = 10

VMEM_LIMIT = 56 * 1024 * 1024

_NT = (((1,), (1,)), ((), ()))
_NN = (((1,), (0,)), ((), ()))
_TN = (((0,), (0,)), ((), ()))


def _dot(a, b, dims=_NN):
    return lax.dot_general(a, b, dims, preferred_element_type=F32)


def _params(sem):
    return pltpu.CompilerParams(dimension_semantics=sem, vmem_limit_bytes=VMEM_LIMIT)


def _matmul(name, a, b, *, ta=False, tb=False, extras=(), epilogue=None, out_dtypes=(F32,), sums=(),
            prologue=None, col_blocked=False, tm=1024, tn=512, tk=4096):
    M, K = (a.shape[1], a.shape[0]) if ta else a.shape
    N = b.shape[0] if tb else b.shape[1]
    tm, tn, tk = min(tm, M), min(tn, N), min(tk, K)
    assert M % tm == 0 and N % tn == 0 and K % tk == 0, (name, M, N, K)
    assert not (sums or prologue) or (tn == N and tk == K), name
    assert not prologue or not (ta or col_blocked), name
    nk = K // tk
    n_ex, n_out, n_sum = len(extras), len(out_dtypes), len(sums)
    n_pro = 1 if prologue else 0
    dims = (((0 if ta else 1,), (1 if tb else 0,)), ((), ()))

    def body(*refs):
        a_ref, b_ref = refs[0], refs[1]
        ex_refs = refs[2 + n_pro:2 + n_pro + n_ex]
        o_refs = refs[2 + n_pro + n_ex:2 + n_pro + n_ex + n_out]
        s_refs = refs[2 + n_pro + n_ex + n_out:2 + n_pro + n_ex + n_out + n_sum]
        k = pl.program_id(2)
        if prologue:
            left = prologue[0](a_ref[...], refs[2][...]).astype(BF16)
            refs[2 + n_pro + n_ex + n_out + n_sum][...] = left
        else:
            left = a_ref[...].astype(BF16)
        part = _dot(left, b_ref[...].astype(BF16), dims)

        def finish(acc):
            outs = (acc,) if epilogue is None else epilogue(acc, *[e[...] for e in ex_refs])
            for o_ref, o in zip(o_refs, outs[:n_out]):
                o_ref[...] = o.astype(o_ref.dtype)
            if n_sum:
                @pl.when(pl.program_id(0) == 0)
                def _():
                    for s_ref in s_refs:
                        s_ref[...] = jnp.zeros_like(s_ref)

                for s_ref, v in zip(s_refs, outs[n_out:]):
                    s_ref[...] += v

        if nk == 1:
            finish(part)
        else:
            acc_ref = refs[-1]

            @pl.when(k == 0)
            def _():
                acc_ref[...] = part

            @pl.when(jnp.logical_and(k > 0, k < nk - 1))
            def _():
                acc_ref[...] += part

            @pl.when(k == nk - 1)
            def _():
                finish(acc_ref[...] + part)

    a_spec = pl.BlockSpec((tk, tm), lambda i, j, k: (k, i)) if ta else pl.BlockSpec((tm, tk), lambda i, j, k: (i, k))
    b_spec = pl.BlockSpec((tn, tk), lambda i, j, k: (j, k)) if tb else pl.BlockSpec((tk, tn), lambda i, j, k: (k, j))
    ex_specs = [pl.BlockSpec((1, tn), lambda i, j, k: (0, j)) if e.shape[0] == 1 else
                pl.BlockSpec((tm, tn), lambda i, j, k: (i, j)) for e in extras]
    if col_blocked:
        out_specs = [pl.BlockSpec((None, tm, tn), lambda i, j, k: (j, i, 0)) for _ in out_dtypes]
        out_shape = [jax.ShapeDtypeStruct((N // tn, M, tn), dt) for dt in out_dtypes]
    else:
        wide = [(dt, N) if not isinstance(dt, tuple) else dt for dt in out_dtypes]
        assert all(w == N for _, w in wide) or tn == N, name
        out_specs = [pl.BlockSpec((tm, tn if w == N else w), lambda i, j, k: (i, j)) for _, w in wide]
        out_shape = [jax.ShapeDtypeStruct((M, w), dt) for dt, w in wide]
    out_specs += [pl.BlockSpec(s, lambda i, j, k: (0, 0)) for s in sums]
    out_shape += [jax.ShapeDtypeStruct(s, F32) for s in sums]
    pro_specs, pro_args = [], []
    if prologue:
        pro_specs, pro_args = [pl.BlockSpec((1, tk), lambda i, j, k: (0, 0))], [prologue[1]]
        out_specs.append(pl.BlockSpec((tm, tk), lambda i, j, k: (i, 0)))
        out_shape.append(jax.ShapeDtypeStruct((M, K), BF16))
    outs = pl.pallas_call(
        body, name=name, grid=(M // tm, N // tn, nk),
        in_specs=[a_spec, b_spec, *pro_specs, *ex_specs], out_specs=out_specs, out_shape=out_shape,
        scratch_shapes=[pltpu.VMEM((tm, tn), F32)] if nk > 1 else [],
        compiler_params=_params(("arbitrary",) * 3 if sums else ("parallel", "parallel", "arbitrary")),
    )(a, b, *pro_args, *extras)
    return outs[0] if len(outs) == 1 else outs


def _rowwise(name, fn, rows, small, outs, sums=(), tile=256):
    specs, args = [], []
    T = None
    for r in rows:
        arr, cb, w = r if isinstance(r, tuple) else (r, 0, r.shape[1])
        T = arr.shape[0]
        specs.append((w, cb))
        args.append(arr)
    tile = min(tile, T)
    assert T % tile == 0
    n_r, n_s, n_o, n_a = len(rows), len(small), len(outs), len(sums)

    def body(*refs):
        r_refs = refs[:n_r]
        s_refs = refs[n_r:n_r + n_s]
        o_refs = refs[n_r + n_s:n_r + n_s + n_o]
        a_refs = refs[n_r + n_s + n_o:]
        res = fn(*[r[...] for r in r_refs], *[s[...] for s in s_refs])
        res = res if isinstance(res, (tuple, list)) else (res,)
        for o_ref, o in zip(o_refs, res[:n_o]):
            o_ref[...] = o.astype(o_ref.dtype)

        @pl.when(pl.program_id(0) == 0)
        def _():
            for a_ref in a_refs:
                a_ref[...] = jnp.zeros_like(a_ref)

        for a_ref, v in zip(a_refs, res[n_o:]):
            a_ref[...] += v.astype(F32)

    in_specs = [pl.BlockSpec((tile, w), functools.partial(lambda i, cb: (i, cb), cb=cb)) for w, cb in specs]
    in_specs += [pl.BlockSpec(s.shape, functools.partial(lambda i, nd: (0,) * nd, nd=s.ndim)) for s in small]
    out_specs = [pl.BlockSpec((tile, w), lambda i: (i, 0)) for w, _ in outs]
    out_specs += [pl.BlockSpec(s, functools.partial(lambda i, nd: (0,) * nd, nd=len(s))) for s in sums]
    out_shape = [jax.ShapeDtypeStruct((T, w), dt) for w, dt in outs]
    out_shape += [jax.ShapeDtypeStruct(s, F32) for s in sums]
    res = pl.pallas_call(
        body, name=name, grid=(T // tile,), in_specs=in_specs, out_specs=out_specs, out_shape=out_shape,
        compiler_params=_params(("arbitrary",)),
    )(*args, *small)
    return res[0] if len(res) == 1 else res


def _rms(x, g):
    return x * lax.rsqrt(jnp.mean(x * x, axis=-1, keepdims=True) + EPS) * g


def _glu_branch(y, pre, b_glu, g_out):
    g = jax.nn.gelu(y)
    return _rms(g * jax.nn.sigmoid(pre + b_glu), g_out)


def _split_dot(x, tri_bf):
    hi = x.astype(BF16)
    lo = (x - hi.astype(F32)).astype(BF16)
    return _dot(hi, tri_bf) + _dot(lo, tri_bf)


def _softplus(z):
    return jnp.maximum(z, 0.0) + jnp.log(1.0 + jnp.exp(-jnp.abs(z)))


def _head(h):
    return slice(h * HEAD_DIM, (h + 1) * HEAD_DIM)


def _head_mean(x, seg):
    return _split_dot(x, seg) * (1.0 / HEAD_DIM)


def _qk_norm(proj, gains):
    r = lax.div(lax.broadcasted_iota(jnp.int32, (SB_WIDTH, SB_WIDTH), 0), HEAD_DIM)
    c = lax.div(lax.broadcasted_iota(jnp.int32, (SB_WIDTH, SB_WIDTH), 1), HEAD_DIM)
    seg = (r == c).astype(BF16)
    q, k, v = (proj[:, i * SB_WIDTH:(i + 1) * SB_WIDTH] for i in range(3))
    qn = q * lax.rsqrt(_head_mean(q * q, seg) + EPS) * gains[:, 0:SB_WIDTH]
    kn = k * lax.rsqrt(_head_mean(k * k, seg) + EPS) * gains[:, SB_WIDTH:2 * SB_WIDTH]
    return proj, qn, kn, v


def _qk_norm_bwd(proj, gq, gk, d_qn, d_kn):
    scale = 1.0 / math.sqrt(HEAD_DIM)
    idx = jnp.arange(SB_WIDTH) // HEAD_DIM
    seg = (idx[:, None] == idx[None, :]).astype(BF16)

    def one(x, g, dy, seg_):
        r = lax.rsqrt(_head_mean(x * x, seg_) + EPS)
        gdy = g * dy
        dx = r * gdy - x * (r * r * r) * _head_mean(gdy * x, seg_)
        dg = jnp.sum(dy * x * r, axis=0, keepdims=True)
        return dx, sum(dg[:, _head(h)] for h in range(N_HEADS))

    def fn(q, k, dqn, dkn, gq_, gk_, seg_):
        dq, dgq = one(q, gq_, dqn * scale, seg_)
        dk, dgk = one(k, gk_, dkn, seg_)
        return dq, dk, dgq, dgk

    return _rowwise("qk_norm_bwd", fn, [(proj, 0, SB_WIDTH), (proj, 1, SB_WIDTH), d_qn, d_kn], [gq, gk, seg],
                    [(SB_WIDTH, BF16)] * 2, sums=[(1, HEAD_DIM)] * 2)


def _split_heads(refs, scratch, L):
    def chunk(i, _):
        r = pl.ds(pl.multiple_of(i * QBLOCK, QBLOCK), QBLOCK)
        for ref, s in zip(refs, scratch):
            for h in range(2):
                s[h, r, :] = ref[r, _head(h)]
        return 0

    lax.fori_loop(0, L // QBLOCK, chunk, 0)


Q_HALVES = KBLOCK // QBLOCK
_CHAINS = [(h, r) for h in range(2) for r in range(Q_HALVES)]


def _valid(i, kb):
    row = lax.broadcasted_iota(jnp.int32, (QBLOCK, KBLOCK), 0)
    col = lax.broadcasted_iota(jnp.int32, (QBLOCK, KBLOCK), 1)
    return col + (kb * KBLOCK - i * QBLOCK) < row


def _attn_fwd(qn, kn, vb, B, L):
    n_pairs = L // KBLOCK
    n_hp = N_HEADS // 2
    nc = len(_CHAINS)

    def body(q_ref, k_ref, v_ref, o_ref, a_ref, q_s, k_s, v_s, after_s, z_s, stage_s, sems):
        _split_heads((q_ref, k_ref, v_ref), (q_s, k_s, v_s), L)
        r2 = lax.broadcasted_iota(jnp.int32, (KBLOCK, KBLOCK), 0)
        c2 = lax.broadcasted_iota(jnp.int32, (KBLOCK, KBLOCK), 1)
        after_s[...] = (r2 > c2).astype(after_s.dtype)
        g = pl.program_id(0) * n_hp + pl.program_id(1)

        def q_pair(p, _):
            rows = [pl.ds(pl.multiple_of((p * Q_HALVES + r) * QBLOCK, QBLOCK), QBLOCK) for r in range(Q_HALVES)]
            q_c = [q_s[h, rows[r], :] for h, r in _CHAINS]
            cs = range(nc)

            def scores(kb):
                rk = pl.ds(pl.multiple_of(kb * KBLOCK, KBLOCK), KBLOCK)
                return [_dot(q_c[c], k_s[_CHAINS[c][0], rk, :], _NT) for c in cs]

            def saved(kb):
                return pltpu.make_async_copy(stage_s.at[kb & 1], a_ref.at[g, p, kb], sems.at[kb & 1])

            def k_block(kb, carry, diagonal):
                rk = pl.ds(pl.multiple_of(kb * KBLOCK, KBLOCK), KBLOCK)
                if diagonal:
                    valid = [_valid(p * Q_HALVES + r, kb) for r in range(Q_HALVES)]
                    keep = lambda c, t: jnp.where(valid[_CHAINS[c][1]], t, 0.0)
                    z = scores(kb)
                else:
                    keep = lambda c, t: t
                    z = [z_s[(kb + 1) & 1, c] for c in cs]
                ahead = scores(jnp.maximum(kb - 1, 0))
                for c in cs:
                    z_s[kb & 1, c] = ahead[c]
                sp = [_softplus(z[c]) for c in cs]
                lsig = [z[c] - sp[c] for c in cs]
                lom = [keep(c, -sp[c]) for c in cs]
                tail = [_split_dot(lom[c], after_s[...]) + carry[c][0] for c in cs]
                a = [keep(c, jnp.exp(lsig[c] + tail[c])).astype(v_s.dtype) for c in cs]
                acc = [carry[c][1] + _dot(a[c], v_s[_CHAINS[c][0], rk, :]) for c in cs]
                for c in cs:
                    stage_s[kb & 1, c] = a[c]
                saved(kb).start()
                return tuple((carry[c][0] + jnp.sum(lom[c], axis=1, keepdims=True), acc[c]) for c in cs)

            def next_block(n, carry):
                kb = p - n

                @pl.when(n >= 2)
                def _():
                    saved(kb + 2).wait()

                return k_block(kb, carry, False)

            init = (jnp.zeros((QBLOCK, 1), F32), jnp.zeros((QBLOCK, HEAD_DIM), F32))
            first = k_block(p, (init,) * nc, True)
            res = lax.fori_loop(1, p + 1, next_block, first)
            saved(0).wait()

            @pl.when(p >= 1)
            def _():
                saved(1).wait()

            for r in range(Q_HALVES):
                o_ref[rows[r], :] = jnp.concatenate([res[c][1] for c in cs if _CHAINS[c][1] == r], axis=1)
            return 0

        lax.fori_loop(0, n_pairs, q_pair, 0)

    spec = pl.BlockSpec((L, LANES), lambda b, p: (b, p))
    return pl.pallas_call(
        body, name="attn_fwd", grid=(B, n_hp),
        in_specs=[spec] * 3, out_specs=[spec, _ANY],
        out_shape=[jax.ShapeDtypeStruct((B * L, SB_WIDTH), F32),
                   jax.ShapeDtypeStruct((B * n_hp, n_pairs, n_pairs, nc, QBLOCK, KBLOCK), BF16)],
        scratch_shapes=[pltpu.VMEM((2, L, HEAD_DIM), BF16)] * 3 + [pltpu.VMEM((KBLOCK, KBLOCK), BF16)]
        + [pltpu.VMEM((2, nc, QBLOCK, KBLOCK), F32), pltpu.VMEM((2, nc, QBLOCK, KBLOCK), BF16),
           pltpu.SemaphoreType.DMA((2,))],
        compiler_params=_params(("parallel", "parallel")),
    )(qn, kn, vb)


def _attn_bwd(qn, kn, vb, kept_a, d_sb, B, L):
    n_pairs = L // KBLOCK
    n_hp = N_HEADS // 2
    nc = len(_CHAINS)
    slots = 3

    def body(q_ref, k_ref, v_ref, do_ref, a_ref, dq_ref, dk_ref, dv_ref,
             q_s, k_s, v_s, qt_s, dkt_s, dvt_s, before_s, stage_s, sems):
        _split_heads((q_ref, k_ref, v_ref), (q_s, k_s, v_s), L)
        g = pl.program_id(0) * n_hp + pl.program_id(1)

        def transpose_q(i, _):
            r = pl.ds(pl.multiple_of(i * QBLOCK, QBLOCK), QBLOCK)
            qt_s[:, r] = q_ref[r, :].astype(F32).T.astype(qt_s.dtype)
            return 0

        lax.fori_loop(0, L // QBLOCK, transpose_q, 0)
        dkt_s[...] = jnp.zeros_like(dkt_s)
        dvt_s[...] = jnp.zeros_like(dvt_s)
        r2 = lax.broadcasted_iota(jnp.int32, (KBLOCK, KBLOCK), 0)
        c2 = lax.broadcasted_iota(jnp.int32, (KBLOCK, KBLOCK), 1)
        before_s[...] = (r2 < c2).astype(before_s.dtype)

        def q_pair(p, _):
            rows = [pl.ds(pl.multiple_of((p * Q_HALVES + r) * QBLOCK, QBLOCK), QBLOCK) for r in range(Q_HALVES)]
            pair = pl.ds(pl.multiple_of(p * KBLOCK, KBLOCK), KBLOCK)
            do2 = do_ref[pair, :]
            do_t = do2.T.astype(v_s.dtype)
            cs = range(nc)
            hs = range(2)
            q_c = [q_s[h, rows[r], :] for h, r in _CHAINS]
            do_c = [do2[r * QBLOCK:(r + 1) * QBLOCK, _head(h)].astype(v_s.dtype) for h, r in _CHAINS]
            qt_h = [qt_s[_head(h), pair] for h in hs]
            dot_h = [do_t[_head(h), :] for h in hs]

            def kept(kb):
                slot = lax.rem(kb, slots)
                return pltpu.make_async_copy(a_ref.at[g, p, kb], stage_s.at[slot], sems.at[slot])

            def k_block(kb, carry, diagonal):
                rk = pl.ds(pl.multiple_of(kb * KBLOCK, KBLOCK), KBLOCK)
                if diagonal:
                    valid = [_valid(p * Q_HALVES + r, kb) for r in range(Q_HALVES)]
                    keep = lambda c, t: jnp.where(valid[_CHAINS[c][1]], t, 0.0)
                else:
                    keep = lambda c, t: t

                    @pl.when(kb + 2 <= p)
                    def _():
                        kept(kb + 2).start()

                kept(kb).wait()
                slot = lax.rem(kb, slots)
                k_b = [k_s[h, rk, :] for h in hs]
                z = [_dot(q_c[c], k_b[_CHAINS[c][0]], _NT) for c in cs]
                da = [_dot(do_c[c], v_s[_CHAINS[c][0], rk, :], _NT) for c in cs]
                a = [stage_s[slot, c] for c in cs]
                dla = [a[c].astype(F32) * da[c] for c in cs]
                for h in hs:
                    a_h = jnp.concatenate([a[c] for c in cs if _CHAINS[c][0] == h], axis=0)
                    dvt_s[_head(h), rk] += _dot(dot_h[h], a_h)
                d_lom = [carry[c][0] + _split_dot(dla[c], before_s[...]) for c in cs]
                beta = [jax.nn.sigmoid(z[c]) for c in cs]
                dz_b = [(dla[c] * (1.0 - beta[c]) - keep(c, beta[c] * d_lom[c])).astype(v_s.dtype) for c in cs]
                dq_acc = [carry[c][1] + _dot(dz_b[c], k_b[_CHAINS[c][0]]) for c in cs]
                for h in hs:
                    dz_h = jnp.concatenate([dz_b[c] for c in cs if _CHAINS[c][0] == h], axis=0)
                    dkt_s[_head(h), rk] += _dot(qt_h[h], dz_h)
                return tuple((carry[c][0] + jnp.sum(dla[c], axis=1, keepdims=True), dq_acc[c]) for c in cs)

            init = (jnp.zeros((QBLOCK, 1), F32), jnp.zeros((QBLOCK, HEAD_DIM), F32))
            kept(0).start()

            @pl.when(p >= 1)
            def _():
                kept(1).start()

            before =lax.fori_loop(0, p, lambda kb, carry: k_block(kb, carry, False), (init,) * nc)
            res = k_block(p, before, True)
            for r in range(Q_HALVES):
                dq_ref[rows[r], :] = jnp.concatenate([res[c][1] for c in cs if _CHAINS[c][1] == r], axis=1)
            return 0

        lax.fori_loop(0, n_pairs, q_pair, 0)

        def transpose_out(i, _):
            r = pl.ds(pl.multiple_of(i * QBLOCK, QBLOCK), QBLOCK)
            dk_ref[r, :] = dkt_s[:, r].T
            dv_ref[r, :] = dvt_s[:, r].T.astype(dv_ref.dtype)
            return 0

        lax.fori_loop(0, L // QBLOCK, transpose_out, 0)

    spec = pl.BlockSpec((L, LANES), lambda b, p: (b, p))
    return pl.pallas_call(
        body, name="attn_bwd", grid=(B, n_hp),
        in_specs=[spec] * 4 + [_ANY], out_specs=[spec] * 3,
        out_shape=[jax.ShapeDtypeStruct((B * L, SB_WIDTH), F32)] * 2 + [jax.ShapeDtypeStruct((B * L, SB_WIDTH), BF16)],
        scratch_shapes=[pltpu.VMEM((2, L, HEAD_DIM), BF16)] * 3 + [pltpu.VMEM((LANES, L), BF16)]
        + [pltpu.VMEM((LANES, L), F32)] * 2 + [pltpu.VMEM((KBLOCK, KBLOCK), BF16)]
        + [pltpu.VMEM((slots, nc, QBLOCK, KBLOCK), BF16), pltpu.SemaphoreType.DMA((slots,))],
        compiler_params=_params(("parallel", "parallel")),
    )(qn, kn, vb, d_sb, kept_a)


def _ssm_discretise(lam_re, lam_im, log_dt, b_re, b_im):
    dt = jnp.exp(log_dt)
    mag = jnp.exp(lam_re * dt)
    lbr = mag * jnp.cos(lam_im * dt)
    lbi = mag * jnp.sin(lam_im * dt)
    den = lam_re * lam_re + lam_im * lam_im
    nr, ni = lbr - 1.0, lbi
    cr = (nr * lam_re + ni * lam_im) / den
    ci = (ni * lam_re - nr * lam_im) / den
    return lbr, lbi, cr * b_re - ci * b_im, cr * b_im + ci * b_re


def _ssm_prep(lam_re, lam_im, log_dt, b_re_t, b_im_t):
    def body(lr, li, ld, br, bi, o_lr, o_li, o_br, o_bi):
        res = _ssm_discretise(lr[...], li[...], ld[...], br[...], bi[...])
        for o, v in zip((o_lr, o_li, o_br, o_bi), res):
            o[...] = v

    return pl.pallas_call(
        body, name="ssm_prep",
        out_shape=[jax.ShapeDtypeStruct(lam_re.shape, F32)] * 2 + [jax.ShapeDtypeStruct(b_re_t.shape, F32)] * 2,
    )(lam_re, lam_im, log_dt, b_re_t, b_im_t)


def _ssm_prep_bwd(lam_re, lam_im, log_dt, b_re_t, b_im_t, d_lr, d_li, d_br, d_bi):
    def body(lr, li, ld, br, bi, g_lr, g_li, g_br, g_bi, o_lr, o_li, o_ld, o_br, o_bi):
        _, vjp = jax.vjp(_ssm_discretise, lr[...], li[...], ld[...], br[...], bi[...])
        res = vjp((g_lr[...], g_li[...], g_br[...], g_bi[...]))
        for o, v in zip((o_lr, o_li, o_ld, o_br, o_bi), res):
            o[...] = v

    return pl.pallas_call(
        body, name="ssm_prep_bwd",
        out_shape=[jax.ShapeDtypeStruct(lam_re.shape, F32)] * 2 + [jax.ShapeDtypeStruct(log_dt.shape, F32)]
        + [jax.ShapeDtypeStruct(b_re_t.shape, F32)] * 2,
    )(lam_re, lam_im, log_dt, b_re_t, b_im_t, d_lr, d_li, d_br, d_bi)


def _block_diag(m):
    m4 = m.reshape(SSM_COLS, 8, SSM_GROUP, SSM_STATE)
    return jnp.einsum("aghp,gk->aghkp", m4, jnp.eye(8, dtype=m.dtype)).reshape(SSM_COLS, LANES, 512)


def _block_diag_take(d):
    d6 = d.reshape(SSM_COLS, 8, SSM_GROUP, 2, 8, SSM_STATE)
    return jnp.einsum("aghrgp->raghp", d6).reshape(2, SSM_GROUPS, SSM_GROUP, SSM_STATE)


def _cmul(ar, ai, br, bi):
    return ar * br - ai * bi, ar * bi + ai * br


def _power(lr, li, n):
    assert n & (n - 1) == 0
    for _ in range(n.bit_length() - 1):
        lr, li = _cmul(lr, li, lr, li)
    return lr, li


def _ssm_fwd(u_p, w_b, lam_r, lam_i, c_m, d_skip, B, L, tj):
    J = L // N_CHUNK
    njt = J // tj
    R = tj * N_CHUNK
    H = 512

    def body(u_ref, wb_ref, lr_ref, li_ref, cm_ref, d_ref, y_ref, gel_ref, x_ref, xin_ref, bu_s, st_s, xin_s):
        ph, jt = pl.program_id(2), pl.program_id(3)
        lr, li = lr_ref[...], li_ref[...]

        @pl.when(jnp.logical_and(ph == 0, jt == 0))
        def _():
            st_s[...] = jnp.zeros_like(st_s)

        @pl.when(ph == 0)
        def _():
            bu_s[jt] = _dot(u_ref[...].astype(BF16), wb_ref[...].astype(BF16))

        def scan(store):
            def step(j, carry):
                xr, xi = carry
                r = pl.ds(pl.multiple_of(j * N_CHUNK, N_CHUNK), N_CHUNK)
                nr = lr * xr - li * xi + bu_s[jt, r, 0:H]
                ni = lr * xi + li * xr + bu_s[jt, r, H:2 * H]
                if store:
                    x_ref[r, 0:H] = nr
                    x_ref[r, H:2 * H] = ni
                return nr, ni

            xr, xi = lax.fori_loop(0, tj, step, (st_s[:, 0:H], st_s[:, H:2 * H]))
            st_s[:, 0:H] = xr
            st_s[:, H:2 * H] = xi

        @pl.when(ph == 0)
        def _():
            scan(False)

            @pl.when(jt == njt - 1)
            def _():
                pr, pi = _power(lr[0:1], li[0:1], J)
                xin_s[0:1, :] = jnp.zeros((1, 2 * H), F32)
                for c in range(1, N_CHUNK):
                    qr, qi = _cmul(pr, pi, xin_s[c - 1:c, 0:H], xin_s[c - 1:c, H:2 * H])
                    xin_s[c:c + 1, 0:H] = qr + st_s[c - 1:c, 0:H]
                    xin_s[c:c + 1, H:2 * H] = qi + st_s[c - 1:c, H:2 * H]
                xin_ref[...] = xin_s[...]
                st_s[...] = xin_s[...]

        @pl.when(ph == 1)
        def _():
            scan(True)
            y = _dot(x_ref[...].astype(BF16), cm_ref[...].astype(BF16)) + d_ref[...] * u_ref[...]
            y_ref[...] = y
            gel_ref[...] = jax.nn.gelu(y).astype(gel_ref.dtype)

    return pl.pallas_call(
        body, name="ssm_fwd", grid=(SSM_COLS, B, 2, njt),
        in_specs=[
            pl.BlockSpec((None, R, LANES), lambda i, b, ph, jt: (b, jt, i)),
            pl.BlockSpec((None, LANES, 2 * H), lambda i, b, ph, jt: (i, 0, 0)),
            pl.BlockSpec((None, N_CHUNK, H), lambda i, b, ph, jt: (i, 0, 0)),
            pl.BlockSpec((None, N_CHUNK, H), lambda i, b, ph, jt: (i, 0, 0)),
            pl.BlockSpec((None, 2 * H, LANES), lambda i, b, ph, jt: (i, 0, 0)),
            pl.BlockSpec((1, LANES), lambda i, b, ph, jt: (0, i)),
        ],
        out_specs=[
            pl.BlockSpec((None, R, LANES), lambda i, b, ph, jt: (b, jt * ph, i)),
            pl.BlockSpec((None, R, LANES), lambda i, b, ph, jt: (b, jt * ph, i)),
            pl.BlockSpec((None, R, 2 * H), lambda i, b, ph, jt: (b, jt * ph, i)),
            pl.BlockSpec((None, None, N_CHUNK, 2 * H), lambda i, b, ph, jt: (b, i, 0, 0)),
        ],
        out_shape=[
            jax.ShapeDtypeStruct((B, L, SSM_WIDTH), F32),
            jax.ShapeDtypeStruct((B, L, SSM_WIDTH), BF16),
            jax.ShapeDtypeStruct((B, L, SSM_COLS * 2 * H), F32),
            jax.ShapeDtypeStruct((B, SSM_COLS, N_CHUNK, 2 * H), F32),
        ],
        scratch_shapes=[pltpu.VMEM((njt, R, 2 * H), F32), pltpu.VMEM((N_CHUNK, 2 * H), F32),
                        pltpu.VMEM((N_CHUNK, 2 * H), F32)],
        compiler_params=_params(("arbitrary",) * 4),
    )(u_p, w_b, lam_r, lam_i, c_m, d_skip)


def _ssm_bwd(dy_p, u_p, x, xin, w_bt, lam_r, lam_i, c_mt, d_skip, B, L, tj):
    J = L // N_CHUNK
    njt = J // tj
    R = tj * N_CHUNK
    H = 512
    x4 = x.reshape(B, J, N_CHUNK, SSM_COLS * 2 * H)

    def body(dy_ref, u_ref, x_ref, xp_ref, xin_ref, wbt_ref, lr_ref, li_ref, cmt_ref, d_ref,
             du_ref, dwb_ref, dcm_ref, dlr_ref, dli_ref, dd_ref, ca_s, a_s, st_s, dl_s):
        b, ph, jt = pl.program_id(1), pl.program_id(2), pl.program_id(3)
        jr = njt - 1 - jt
        lr, li = lr_ref[...], -li_ref[...]

        @pl.when(jnp.logical_and(b == 0, jnp.logical_and(ph == 0, jt == 0)))
        def _():
            dwb_ref[...] = jnp.zeros_like(dwb_ref)
            dcm_ref[...] = jnp.zeros_like(dcm_ref)
            dlr_ref[...] = jnp.zeros_like(dlr_ref)
            dli_ref[...] = jnp.zeros_like(dli_ref)
            dd_ref[...] = jnp.zeros_like(dd_ref)
            dl_s[...] = jnp.zeros_like(dl_s)

        @pl.when(jnp.logical_and(ph == 0, jt == 0))
        def _():
            st_s[...] = jnp.zeros_like(st_s)

        @pl.when(ph == 0)
        def _():
            ca_s[jt] = _dot(dy_ref[...].astype(BF16), cmt_ref[...].astype(BF16))

        def scan(store):
            def step(n, carry):
                ar, ai = carry
                r = pl.ds(pl.multiple_of((tj - 1 - n) * N_CHUNK, N_CHUNK), N_CHUNK)
                nr = lr * ar - li * ai + ca_s[jt, r, 0:H]
                ni = lr * ai + li * ar + ca_s[jt, r, H:2 * H]
                if store:
                    a_s[r, 0:H] = nr
                    a_s[r, H:2 * H] = ni
                return nr, ni

            ar, ai = lax.fori_loop(0, tj, step, (st_s[:, 0:H], st_s[:, H:2 * H]))
            st_s[:, 0:H] = ar
            st_s[:, H:2 * H] = ai

        @pl.when(ph == 0)
        def _():
            scan(False)

            @pl.when(jt == njt - 1)
            def _():
                pr, pi = _power(lr[0:1], li[0:1], J)
                a_s[N_CHUNK - 1:N_CHUNK, :] = jnp.zeros((1, 2 * H), F32)
                for c in range(N_CHUNK - 2, -1, -1):
                    qr, qi = _cmul(pr, pi, a_s[c + 1:c + 2, 0:H], a_s[c + 1:c + 2, H:2 * H])
                    a_s[c:c + 1, 0:H] = qr + st_s[c + 1:c + 2, 0:H]
                    a_s[c:c + 1, H:2 * H] = qi + st_s[c + 1:c + 2, H:2 * H]
                st_s[...] = a_s[0:N_CHUNK, :]

        @pl.when(ph == 1)
        def _():
            scan(True)
            dy = dy_ref[...]
            u = u_ref[...]
            a_b = a_s[...].astype(BF16)
            du_ref[...] = (_dot(a_b, wbt_ref[...].astype(BF16)) + d_ref[...] * dy).astype(du_ref.dtype)
            dwb_ref[...] += _dot(u.astype(BF16), a_b, _TN)
            dcm_ref[...] += _dot(x_ref[...].astype(BF16), dy.astype(BF16), _TN)
            dd_ref[...] += jnp.sum(dy * u, axis=0, keepdims=True)

            first = jnp.where(jr == 0, xin_ref[...], xp_ref[...])
            a0r, a0i = a_s[0:N_CHUNK, 0:H], a_s[0:N_CHUNK, H:2 * H]
            acc0 = (a0r * first[:, 0:H] + a0i * first[:, H:2 * H], a0i * first[:, 0:H] - a0r * first[:, H:2 * H])

            def step(j, carry):
                sr, si = carry
                r = pl.ds(pl.multiple_of(j * N_CHUNK, N_CHUNK), N_CHUNK)
                rp = pl.ds(pl.multiple_of((j - 1) * N_CHUNK, N_CHUNK), N_CHUNK)
                ar, ai = a_s[r, 0:H], a_s[r, H:2 * H]
                xr, xi = x_ref[rp, 0:H], x_ref[rp, H:2 * H]
                return sr + ar * xr + ai * xi, si + ai * xr - ar * xi

            sr, si = lax.fori_loop(1, tj, step, acc0)
            dl_s[:, 0:H] += sr
            dl_s[:, H:2 * H] += si

            @pl.when(jnp.logical_and(b == B - 1, jt == njt - 1))
            def _():
                dlr_ref[...] = jnp.sum(dl_s[:, 0:H], axis=0, keepdims=True)
                dli_ref[...] = jnp.sum(dl_s[:, H:2 * H], axis=0, keepdims=True)
                dl_s[...] = jnp.zeros_like(dl_s)

    rev = lambda ph, jt: (njt - 1 - jt) * ph + (njt - 1) * (1 - ph)
    return pl.pallas_call(
        body, name="ssm_bwd", grid=(SSM_COLS, B, 2, njt),
        in_specs=[
            pl.BlockSpec((None, R, LANES), lambda i, b, ph, jt: (b, njt - 1 - jt, i)),
            pl.BlockSpec((None, R, LANES), lambda i, b, ph, jt: (b, njt - 1 - jt, i)),
            pl.BlockSpec((None, R, 2 * H), lambda i, b, ph, jt: (b, rev(ph, jt), i)),
            pl.BlockSpec((None, None, N_CHUNK, 2 * H),
                         lambda i, b, ph, jt: (b, jnp.maximum((njt - 1 - jt) * tj - 1, 0), 0, i)),
            pl.BlockSpec((None, None, N_CHUNK, 2 * H), lambda i, b, ph, jt: (b, i, 0, 0)),
            pl.BlockSpec((None, 2 * H, LANES), lambda i, b, ph, jt: (i, 0, 0)),
            pl.BlockSpec((None, N_CHUNK, H), lambda i, b, ph, jt: (i, 0, 0)),
            pl.BlockSpec((None, N_CHUNK, H), lambda i, b, ph, jt: (i, 0, 0)),
            pl.BlockSpec((None, LANES, 2 * H), lambda i, b, ph, jt: (i, 0, 0)),
            pl.BlockSpec((1, LANES), lambda i, b, ph, jt: (0, i)),
        ],
        out_specs=[
            pl.BlockSpec((None, R, LANES), lambda i, b, ph, jt: (b, rev(ph, jt), i)),
            pl.BlockSpec((None, LANES, 2 * H), lambda i, b, ph, jt: (i, 0, 0)),
            pl.BlockSpec((None, 2 * H, LANES), lambda i, b, ph, jt: (i, 0, 0)),
            pl.BlockSpec((None, 1, H), lambda i, b, ph, jt: (i, 0, 0)),
            pl.BlockSpec((None, 1, H), lambda i, b, ph, jt: (i, 0, 0)),
            pl.BlockSpec((1, LANES), lambda i, b, ph, jt: (0, i)),
        ],
        out_shape=[
            jax.ShapeDtypeStruct((B, L, SSM_WIDTH), BF16),
            jax.ShapeDtypeStruct((SSM_COLS, LANES, 2 * H), F32),
            jax.ShapeDtypeStruct((SSM_COLS, 2 * H, LANES), F32),
            jax.ShapeDtypeStruct((SSM_COLS, 1, H), F32),
            jax.ShapeDtypeStruct((SSM_COLS, 1, H), F32),
            jax.ShapeDtypeStruct((1, SSM_WIDTH), F32),
        ],
        scratch_shapes=[pltpu.VMEM((njt, R, 2 * H), F32), pltpu.VMEM((R, 2 * H), F32),
                        pltpu.VMEM((N_CHUNK, 2 * H), F32), pltpu.VMEM((N_CHUNK, 2 * H), F32)],
        compiler_params=_params(("arbitrary",) * 4),
    )(dy_p, u_p, x, x4, xin, w_bt, lam_r, lam_i, c_mt, d_skip)


def _to_scan_layout(t, B, L):
    C = t.shape[-1]
    return t.reshape(B, N_CHUNK, L // N_CHUNK, C).transpose(0, 2, 1, 3).reshape(B, L, C)


def _from_scan_layout(t, B, L):
    C = t.shape[-1]
    return t.reshape(B, L // N_CHUNK, N_CHUNK, C).transpose(0, 2, 1, 3).reshape(B * L, C)


def _local_step(x, target, p, w_in, late_weights, mlp_grads_ready=None, rest_grads_ready=None, order=None, *,
                ssm_tile=128):
    B, L, D = x.shape
    T = B * L
    x2 = x.reshape(T, D)
    row = lambda v: v.reshape(1, -1)
    g1, g2, ga, gs, b_glu = row(p["norm1_g"]), row(p["norm2_g"]), row(p["attn_out_g"]), row(p["ssm_out_g"]), row(p["b_glu"])
    g1_first = g1 if order is None else g1 + order
    gq8 = jnp.tile(row(p["q_norm_g"]), (1, N_HEADS))
    gk8 = jnp.tile(row(p["k_norm_g"]), (1, N_HEADS))

    G, P, Hh = SSM_GROUPS, SSM_STATE, SSM_GROUP
    lam_re3, lam_im3 = p["ssm_lambda_re"].reshape(G, 1, P), p["ssm_lambda_im"].reshape(G, 1, P)
    log_dt3 = p["ssm_log_dt"].reshape(G, 1, 1)
    b_re_t, b_im_t = p["ssm_b_re"].transpose(0, 2, 1), p["ssm_b_im"].transpose(0, 2, 1)
    lbr, lbi, bbr, bbi = _ssm_prep(lam_re3, lam_im3, log_dt3, b_re_t, b_im_t)
    w_b = jnp.concatenate([_block_diag(bbr), _block_diag(bbi)], axis=2)
    c_mt = jnp.concatenate([_block_diag(p["ssm_c_re"]), -_block_diag(p["ssm_c_im"])], axis=2)
    w_bt, c_m = w_b.transpose(0, 2, 1), c_mt.transpose(0, 2, 1)
    lam_r = jnp.broadcast_to(lbr.reshape(SSM_COLS, 1, 512), (SSM_COLS, N_CHUNK, 512))
    lam_i = jnp.broadcast_to(lbi.reshape(SSM_COLS, 1, 512), (SSM_COLS, N_CHUNK, 512))
    d_skip = p["ssm_d"].reshape(1, SSM_WIDTH)

    qk_gains = jnp.concatenate([gq8 * (1.0 / math.sqrt(HEAD_DIM)), gk8, jnp.ones((1, 2 * SB_WIDTH), F32)], axis=1)
    head = (BF16, SB_WIDTH)
    proj, qn, kn, vb, xn = _matmul("proj", x2, w_in, prologue=(_rms, g1_first), extras=[qk_gains],
                                   out_dtypes=(F32, head, head, head), epilogue=_qk_norm, tm=512, tn=w_in.shape[1])
    sb, attn_kept = _attn_fwd(qn, kn, vb, B, L)
    u_p = _to_scan_layout(proj[:, 3 * SB_WIDTH:], B, L)
    y_p, gel_p, xs, xin = _ssm_fwd(u_p, w_b, lam_r, lam_i, c_m, d_skip, B, L, ssm_tile)
    y2, gel = y_p.reshape(T, SSM_WIDTH), gel_p.reshape(T, SSM_WIDTH)
    w_glu, w_out, w_mlp_in, w_mlp_out = late_weights(gel)
    pre, ssm_n = _matmul("glu_gate", gel, w_glu, extras=[y2, b_glu, gs], out_dtypes=(F32, BF16),
                         epilogue=lambda acc, y, b, g: (acc, _glu_branch(y, acc, b, g)))
    sb_n = _rowwise("attn_out_norm", _rms, [sb], [ga], [(SB_WIDTH, BF16)])
    mixed = jnp.concatenate([sb_n, _from_scan_layout(ssm_n, B, L)], axis=1)
    def residual_and_norm(acc, res, g):
        h = acc + res
        return h, _rms(h, g)

    h1, hn = _matmul("out_proj", mixed, w_out, extras=[x2, g2], out_dtypes=(F32, BF16), tn=D,
                     epilogue=residual_and_norm)
    act, a_pre = _matmul("mlp_in", hn, w_mlp_in, out_dtypes=(BF16, BF16), tn=1024,
                         epilogue=lambda acc: (jnp.square(jnp.maximum(acc, 0.0)), acc))

    def loss_fn(acc, h, t):
        diff = acc + h - t
        part = jnp.sum(jnp.sum(diff * diff, axis=0, keepdims=True), axis=1, keepdims=True)
        d = diff * (1.0 / D)
        return d, d, part * (0.5 / D)

    d_out, d_out_b, loss = _matmul("mlp_out", act, w_mlp_out, extras=[h1, target.reshape(T, D)],
                                   out_dtypes=(F32, BF16), sums=[(1, 1)], epilogue=loss_fn, tm=512, tn=D)

    d_apre = _matmul("mlp_out_dx", d_out_b, w_mlp_out, tb=True, extras=[a_pre], out_dtypes=(BF16,), tn=1024,
                     epilogue=lambda acc, ap: (acc * (2.0 * jnp.maximum(ap.astype(F32), 0.0)),))
    both = lambda acc: (acc, acc)
    g_w_mlp_out, g_w_mlp_out_b = _matmul("mlp_out_dw", act, d_out_b, ta=True, out_dtypes=(F32, BF16), epilogue=both)
    g_w_mlp_in, g_w_mlp_in_b = _matmul("mlp_in_dw", hn, d_apre, ta=True, col_blocked=True, out_dtypes=(F32, BF16),
                                       epilogue=both, tn=w_mlp_in.shape[1] // N_DEV)
    if mlp_grads_ready is not None:
        g2 = g2 + mlp_grads_ready(g_w_mlp_out, g_w_mlp_out_b, g_w_mlp_in, g_w_mlp_in_b)

    def norm_bwd_res(dy, h, res, g):
        _, vjp = jax.vjp(_rms, h, g)
        dh, dg = vjp(dy)
        return res + dh, dg

    def norm_bwd_res2(dy, h, res, g):
        d, dg = norm_bwd_res(dy, h, res, g)
        return d, d, dg

    d_h1, d_h1_b, g_norm2 = _matmul("mlp_in_dx", d_apre, w_mlp_in, tb=True, extras=[h1, d_out, g2],
                                    out_dtypes=(F32, BF16), sums=[(1, D)], epilogue=norm_bwd_res2, tm=512, tn=D)

    d_mixed = _matmul("out_proj_dx", d_h1_b, w_out, tb=True, tn=1024)
    g_w_out, g_w_out_b = _matmul("out_proj_dw", mixed, d_h1_b, ta=True, out_dtypes=(F32, BF16), epilogue=both)

    def norm_bwd(h, dy, g):
        _, vjp = jax.vjp(_rms, h, g)
        return vjp(dy)

    d_sb, g_attn_out = _rowwise("attn_out_norm_bwd", norm_bwd, [sb, (d_mixed, 0, SB_WIDTH)], [ga],
                                [(SB_WIDTH, F32)], sums=[(1, SB_WIDTH)])
    d_ssm_n = _to_scan_layout(d_mixed[:, SB_WIDTH:], B, L).reshape(T, SSM_WIDTH)

    def glu_bwd(y, pre_, dy, bg, g):
        _, vjp = jax.vjp(_glu_branch, y, pre_, bg, g)
        d_y, d_pre, d_bg, d_g = vjp(dy)
        return d_y, d_pre, d_bg, d_g

    d_y_direct, d_pre, g_b_glu, g_ssm_out = _rowwise(
        "glu_out_bwd", glu_bwd, [y2, pre, d_ssm_n], [b_glu, gs], [(SSM_WIDTH, F32), (SSM_WIDTH, BF16)],
        sums=[(1, SSM_WIDTH), (1, SSM_WIDTH)])
    g_w_glu, g_w_glu_b = _matmul("glu_gate_dw", gel, d_pre, ta=True, out_dtypes=(F32, BF16), epilogue=both)

    def gelu_bwd(dg, y, dy0):
        _, vjp = jax.vjp(jax.nn.gelu, y)
        return (dy0 + vjp(dg)[0],)

    d_y = _matmul("glu_gate_dx", d_pre, w_glu, tb=True, extras=[y2, d_y_direct], epilogue=gelu_bwd)

    du_p, d_wb, d_cm, d_lr, d_li, g_d = _ssm_bwd(
        d_y.reshape(B, L, SSM_WIDTH), u_p, xs, xin, w_bt, lam_r, lam_i, c_mt, d_skip, B, L, ssm_tile)
    d_bb = _block_diag_take(d_wb.reshape(SSM_COLS, LANES, 2, 512))
    d_c = _block_diag_take(d_cm.transpose(0, 2, 1).reshape(SSM_COLS, LANES, 2, 512))
    g_lam_re, g_lam_im, g_log_dt, g_b_re_t, g_b_im_t = _ssm_prep_bwd(
        lam_re3, lam_im3, log_dt3, b_re_t, b_im_t,
        d_lr.reshape(G, 1, P), d_li.reshape(G, 1, P), d_bb[0], d_bb[1])
    d_qn, d_kn, d_v = _attn_bwd(qn, kn, vb, attn_kept, d_sb, B, L)
    d_q, d_k, g_q, g_k = _qk_norm_bwd(proj, gq8, gk8, d_qn, d_kn)

    d_proj = jnp.concatenate([d_q, d_k, d_v, _from_scan_layout(du_p, B, L)], axis=1)
    g_w_in, g_w_in_b = _matmul("proj_dw", xn, d_proj, ta=True, col_blocked=True, out_dtypes=(F32, BF16),
                               epilogue=both, tn=w_in.shape[1] // N_DEV)
    if rest_grads_ready is not None:
        g1 = g1 + rest_grads_ready([g_w_in, g_w_glu, g_w_out], [g_w_in_b, g_w_glu_b, g_w_out_b])
    grad_x, g_norm1 = _matmul("proj_dx", d_proj, w_in, tb=True, extras=[x2, d_h1, g1], sums=[(1, D)],
                              epilogue=norm_bwd_res, tm=512, tn=D)

    small = {
        "norm1_g": g_norm1.reshape(-1),
        "q_norm_g": g_q.reshape(-1),
        "k_norm_g": g_k.reshape(-1),
        "ssm_lambda_re": g_lam_re.reshape(G, P),
        "ssm_lambda_im": g_lam_im.reshape(G, P),
        "ssm_log_dt": g_log_dt.reshape(G),
        "ssm_b_re": g_b_re_t.transpose(0, 2, 1),
        "ssm_b_im": g_b_im_t.transpose(0, 2, 1),
        "ssm_c_re": d_c[0],
        "ssm_c_im": -d_c[1],
        "ssm_d": g_d.reshape(G, Hh),
        "b_glu": g_b_glu.reshape(-1),
        "attn_out_g": g_attn_out.reshape(-1),
        "ssm_out_g": g_ssm_out.reshape(-1),
        "norm2_g": g_norm2.reshape(-1),
    }
    big = {"w_in": g_w_in, "w_glu": g_w_glu, "w_out": g_w_out, "w_mlp_in": g_w_mlp_in, "w_mlp_out": g_w_mlp_out}
    return loss[0, 0], grad_x.reshape(B, L, D), small, big


_ANY = pl.BlockSpec(memory_space=pl.ANY)
_MESH = pl.DeviceIdType.MESH


def _all_gather(name, shards):
    n = len(shards)

    def body(*refs):
        in_refs, out_refs = refs[:n], refs[n:2 * n]
        send_sems, recv_sems, local_sems = refs[2 * n:]
        x, y, c = lax.axis_index("x"), lax.axis_index("y"), lax.axis_index("c")
        me, sibling = (x, y, c), (x, y, 1 - c)
        chips = [(1 - x, y), (x, 1 - y), (1 - x, 1 - y)]

        def copy(a, k, block, to, src=None):
            px, py, pc = block
            rows = out_refs[a].at[4 * px + 2 * py + pc]
            return pltpu.make_async_remote_copy(
                src_ref=rows if src is None else src, dst_ref=rows, send_sem=send_sems.at[a, k],
                recv_sem=recv_sems.at[a, k], device_id=to, device_id_type=_MESH)

        mine = [pltpu.make_async_copy(in_refs[a], out_refs[a].at[4 * x + 2 * y + c], local_sems.at[a]) for a in range(n)]
        first, passed = [], []
        for a in range(n):
            mine[a].start()
            first.append(copy(a, 0, me, sibling, src=in_refs[a]))
            first += [copy(a, 1 + j, me, (*chip, c), src=in_refs[a]) for j, chip in enumerate(chips)]
        for cp in first:
            cp.start()
        for j, chip in enumerate(chips):
            for a in range(n):
                copy(a, 1 + j, (*chip, c), me).wait_recv()
                fwd = copy(a, 4 + j, (*chip, c), sibling)
                fwd.start()
                passed.append(fwd)
        for a in range(n):
            copy(a, 0, sibling, me).wait_recv()
            for j, chip in enumerate(chips):
                copy(a, 4 + j, (*chip, 1 - c), me).wait_recv()
        for cp in first + passed:
            cp.wait_send()
        for cp in mine:
            cp.wait()

    return pl.pallas_call(
        body, name=name,
        in_specs=[_ANY] * n, out_specs=[_ANY] * n,
        out_shape=[jax.ShapeDtypeStruct((N_DEV, *s.shape), s.dtype) for s in shards],
        scratch_shapes=[pltpu.SemaphoreType.DMA((n, 7)), pltpu.SemaphoreType.DMA((n, 7)), pltpu.SemaphoreType.DMA((n,))],
    )(*shards)


_HBM = pl.BlockSpec(memory_space=pltpu.HBM)
_SEM = pl.BlockSpec(memory_space=pltpu.SEMAPHORE)
_EFFECT = pltpu.SideEffectType.DATAFLOW_SIDE_EFFECTING
_FLIPS = [(dx, dy, dc) for dx in (0, 1) for dy in (0, 1) for dc in (0, 1) if (dx, dy, dc) != (0, 0, 0)]


def _exchange_start(name, srcs, lands, per_peer):
    n = len(srcs)

    def body(*refs):
        src_refs, land_refs = refs[:n], refs[n:2 * n]
        send_sems, recv_sems = refs[2 * n:3 * n], refs[3 * n:4 * n]
        token = refs[-1]
        x, y, c = lax.axis_index("x"), lax.axis_index("y"), lax.axis_index("c")
        me = 4 * x + 2 * y + c
        for dx, dy, dc in _FLIPS:
            px, py, pc = (1 - x if dx else x), (1 - y if dy else y), (1 - c if dc else c)
            for a in range(n):
                pltpu.make_async_remote_copy(
                    src_ref=src_refs[a].at[4 * px + 2 * py + pc] if per_peer else src_refs[a],
                    dst_ref=land_refs[a].at[me], send_sem=send_sems[a], recv_sem=recv_sems[a],
                    device_id=(px, py, pc), device_id_type=_MESH).start()
        token[...] = jnp.zeros_like(token)

    hbm = lambda t: pltpu.with_memory_space_constraint(t, pltpu.HBM)
    res = pl.pallas_call(
        body, name=name,
        out_shape=(*[pltpu.SemaphoreType.DMA(())] * (2 * n), *[pltpu.HBM(t.shape, t.dtype) for t in (*srcs, *lands)],
                   jax.ShapeDtypeStruct((8, LANES), F32)),
        in_specs=[_HBM] * (2 * n),
        out_specs=(*[_SEM] * (2 * n), *[_HBM] * (2 * n), pl.BlockSpec(memory_space=pltpu.VMEM)),
        input_output_aliases={i: 2 * n + i for i in range(2 * n)},
        compiler_params=pltpu.CompilerParams(has_side_effects=_EFFECT),
    )(*[hbm(t) for t in (*srcs, *lands)])
    return res[:-1], res[-1]


def _exchange_wait(name, handle, after):
    n = len(handle) // 4
    sems, thru = handle[:2 * n], handle[2 * n:]

    def body(*refs):
        land_refs = refs[n:2 * n]
        send_sems, recv_sems = refs[2 * n:3 * n], refs[3 * n:4 * n]
        me = (lax.axis_index("x"), lax.axis_index("y"), lax.axis_index("c"))
        for a in range(n):
            seven = land_refs[a].at[pl.ds(0, len(_FLIPS))]
            all_copies = pltpu.make_async_remote_copy(
                src_ref=seven, dst_ref=seven, send_sem=send_sems[a], recv_sem=recv_sems[a], device_id=me,
                device_id_type=_MESH)
            all_copies.wait_send()
            all_copies.wait_recv()

    res = pl.pallas_call(
        body, name=name, out_shape=tuple(pltpu.HBM(t.shape, t.dtype) for t in thru),
        in_specs=[*[_HBM] * (2 * n), *[_SEM] * (2 * n), _ANY], out_specs=tuple([_HBM] * (2 * n)),
        input_output_aliases={i: i for i in range(2 * n)},
        compiler_params=pltpu.CompilerParams(has_side_effects=_EFFECT),
    )(*thru, *sems, after)
    return res[n:]


def _adamw_gathered(name, own, parts, me, w, m, v):
    r, c = w.shape
    tr = min(r, 256)

    def body(me_ref, own_ref, p_ref, w_ref, m_ref, v_ref, g_out, d_out, m_out, v_out):
        g = own_ref[...]
        for j in range(N_DEV):
            g = g + p_ref[j].astype(F32)
        delta, m_new, v_new = _adamw(w_ref[...], g, m_ref[...], v_ref[...])
        g_out[...] = g
        d_out[...] = delta
        m_out[...] = m_new
        v_out[...] = v_new

    spec = pl.BlockSpec((tr, c), lambda i, me_ref: (i, 0))
    return pl.pallas_call(
        body, name=name,
        grid_spec=pltpu.PrefetchScalarGridSpec(
            num_scalar_prefetch=1, grid=(r // tr,),
            in_specs=[pl.BlockSpec((None, tr, c), lambda i, me_ref: (me_ref[0], i, 0)),
                      pl.BlockSpec((N_DEV, tr, c), lambda i, me_ref: (0, i, 0)), spec, spec, spec],
            out_specs=[spec] * 4),
        out_shape=[jax.ShapeDtypeStruct((r, c), F32)] * 4,
        compiler_params=_params(("parallel",)),
    )(me, own, parts, w, m, v)


def _adamw(w, g, m, v):
    m = ADAM_B1 * m + (1.0 - ADAM_B1) * g
    v = ADAM_B2 * v + (1.0 - ADAM_B2) * jnp.square(g)
    m_hat = m / (1.0 - ADAM_B1 ** ADAM_STEP)
    v_hat = v / (1.0 - ADAM_B2 ** ADAM_STEP)
    delta = -ADAM_LR * (m_hat / (jnp.sqrt(v_hat) + ADAM_EPS) + ADAM_WD * w)
    return delta, m, v


def _adamw_small(name, parts, w, m, v):
    _, r, c = parts.shape
    tr = 8

    def body(p_ref, w_ref, m_ref, v_ref, g_out, d_out, m_out, v_out):
        g = p_ref[0]
        for j in range(1, N_DEV):
            g = g + p_ref[j]
        delta, m_new, v_new = _adamw(w_ref[...], g, m_ref[...], v_ref[...])
        g_out[...] = g
        d_out[...] = delta
        m_out[...] = m_new
        v_out[...] = v_new

    spec = pl.BlockSpec((tr, c), lambda i: (i, 0))
    return pl.pallas_call(
        body, name=name, grid=(r // tr,),
        in_specs=[pl.BlockSpec((N_DEV, tr, c), lambda i: (0, i, 0)), spec, spec, spec],
        out_specs=[spec] * 4, out_shape=[jax.ShapeDtypeStruct((r, c), F32)] * 4,
        compiler_params=_params(("parallel",)),
    )(parts, w, m, v)


_WEIGHTS = ["norm1_g", "w_in", "q_norm_g", "k_norm_g", "ssm_lambda_re", "ssm_lambda_im", "ssm_log_dt", "ssm_b_re",
            "ssm_b_im", "ssm_c_re", "ssm_c_im", "ssm_d", "w_glu", "b_glu", "attn_out_g", "ssm_out_g", "w_out",
            "norm2_g", "w_mlp_in", "w_mlp_out"]
_BIG = ["w_in", "w_glu", "w_out", "w_mlp_in", "w_mlp_out"]
_SMALL = [n for n in _WEIGHTS if n not in _BIG]
_PACK_COLS = 1024


def _pack(tree, last=None):
    flat = [tree[n].reshape(-1).astype(F32) for n in _SMALL]
    size = sum(f.shape[0] for f in flat)
    rows = -(-(size + 1) // (_PACK_COLS * 8)) * 8
    pad = jnp.zeros((rows * _PACK_COLS - size - 1,), F32)
    tail = jnp.zeros((1,), F32) if last is None else last.reshape(1).astype(F32)
    return jnp.concatenate(flat + [pad, tail]).reshape(rows, _PACK_COLS)


def _unpack(buf, like):
    flat, out, off = buf.reshape(-1), {}, 0
    for n in _SMALL:
        size = like[n].size
        out[n] = flat[off:off + size].reshape(like[n].shape)
        off += size
    return out


def kernel(x, norm1_g, w_in, q_norm_g, k_norm_g, ssm_lambda_re, ssm_lambda_im, ssm_log_dt, ssm_b_re, ssm_b_im, ssm_c_re, ssm_c_im, ssm_d, w_glu, b_glu, attn_out_g, ssm_out_g, w_out, norm2_g, w_mlp_in, w_mlp_out, loss_target, m_norm1_g, m_w_in, m_q_norm_g, m_k_norm_g, m_ssm_lambda_re, m_ssm_lambda_im, m_ssm_log_dt, m_ssm_b_re, m_ssm_b_im, m_ssm_c_re, m_ssm_c_im, m_ssm_d, m_w_glu, m_b_glu, m_attn_out_g, m_ssm_out_g, m_w_out, m_norm2_g, m_w_mlp_in, m_w_mlp_out, v_norm1_g, v_w_in, v_q_norm_g, v_k_norm_g, v_ssm_lambda_re, v_ssm_lambda_im, v_ssm_log_dt, v_ssm_b_re, v_ssm_b_im, v_ssm_c_re, v_ssm_c_im, v_ssm_d, v_w_glu, v_b_glu, v_attn_out_g, v_ssm_out_g, v_w_out, v_norm2_g, v_w_mlp_in, v_w_mlp_out):
    w = dict(zip(_WEIGHTS, (norm1_g, w_in, q_norm_g, k_norm_g, ssm_lambda_re, ssm_lambda_im, ssm_log_dt, ssm_b_re, ssm_b_im, ssm_c_re, ssm_c_im, ssm_d, w_glu, b_glu, attn_out_g, ssm_out_g, w_out, norm2_g, w_mlp_in, w_mlp_out)))
    m = dict(zip(_WEIGHTS, (m_norm1_g, m_w_in, m_q_norm_g, m_k_norm_g, m_ssm_lambda_re, m_ssm_lambda_im, m_ssm_log_dt, m_ssm_b_re, m_ssm_b_im, m_ssm_c_re, m_ssm_c_im, m_ssm_d, m_w_glu, m_b_glu, m_attn_out_g, m_ssm_out_g, m_w_out, m_norm2_g, m_w_mlp_in, m_w_mlp_out)))
    v = dict(zip(_WEIGHTS, (v_norm1_g, v_w_in, v_q_norm_g, v_k_norm_g, v_ssm_lambda_re, v_ssm_lambda_im, v_ssm_log_dt, v_ssm_b_re, v_ssm_b_im, v_ssm_c_re, v_ssm_c_im, v_ssm_d, v_w_glu, v_b_glu, v_attn_out_g, v_ssm_out_g, v_w_out, v_norm2_g, v_w_mlp_in, v_w_mlp_out)))
    core = lax.axis_index("c").astype(jnp.int32).reshape(1)
    chip = (2 * lax.axis_index("x") + lax.axis_index("y")).astype(jnp.int32).reshape(1)

    me = (2 * chip + core).astype(jnp.int32)

    def landing(own=None, like=None):
        own = jnp.zeros_like(like) if own is None else own
        return lax.dynamic_update_slice(lax.empty((N_DEV, *like.shape), like.dtype), own[None], (me[0], 0, 0))

    (w_in_blocks,) = _all_gather("w_in_all_gather", [w_in.astype(BF16)])
    w_in_full = w_in_blocks.transpose(1, 0, 2).reshape(w_in.shape[0], -1)
    late = [n for n in _BIG if n != "w_in"]
    shards = [w[n].astype(BF16) for n in late]
    w_in_blocks, shards = lax.optimization_barrier((w_in_blocks, shards))
    weights_handle, weights_token = _exchange_start(
        "weights_send", shards, [landing(s, s) for s in shards], per_peer=False)

    def late_weights(after):
        got = dict(zip(late, _exchange_wait("weights_arrive", weights_handle, after)))
        return (got["w_glu"].reshape(-1, w_glu.shape[1]), got["w_out"].reshape(-1, w_out.shape[1]),
                got["w_mlp_in"].transpose(1, 0, 2).reshape(w_mlp_in.shape[0], -1),
                got["w_mlp_out"].reshape(-1, w_mlp_out.shape[1]))

    mlp = ["w_mlp_out", "w_mlp_in"]
    sent = {}

    def send_grads(name, names, own, own_b):
        blocks = lambda g, n: g.reshape(N_DEV, *w[n].shape)
        sent[name + "_own"] = [blocks(g, n) for g, n in zip(own, names)]
        srcs = [blocks(g, n) for g, n in zip(own_b, names)]
        sent[name], token = _exchange_start(name, srcs, [landing(like=s[0]) for s in srcs], per_peer=True)
        return token[0, 0]

    def mlp_grads_ready(g_out, g_out_b, g_in, g_in_b):
        return send_grads("mlp_grads_send", mlp, [g_out, g_in], [g_out_b, g_in_b])

    rest = ["w_in", "w_glu", "w_out"]

    def rest_grads_ready(own, own_b):
        return send_grads("rest_grads_send", rest, own, own_b)

    loss_local, grad_x, g_small, g_big = _local_step(
        x, loss_target, {n: w[n] for n in _SMALL}, w_in_full, late_weights, mlp_grads_ready, rest_grads_ready,
        weights_token[0, 0])

    grads, delta, new_m, new_v = {}, {}, {}, {}
    small = _pack(g_small, last=loss_local)
    small_handle, small_token = _exchange_start("small_grads_send", [small], [landing(small, small)], per_peer=False)

    for send, arrive, names in (("mlp_grads_send", "mlp_grads_arrive", mlp),
                                ("rest_grads_send", "rest_grads_arrive", rest)):
        for n, own, part in zip(names, sent[send + "_own"], _exchange_wait(arrive, sent[send], small_token)):
            grads[n], delta[n], new_m[n], new_v[n] = _adamw_gathered("adamw_" + n, own, part, me, w[n], m[n], v[n])

    shards_done = lax.optimization_barrier(tuple(new_v[n] for n in _BIG))
    (small_parts,) = _exchange_wait("small_grads_arrive", small_handle, shards_done[-1])
    packed = _adamw_small("adamw_small", small_parts, _pack(w), _pack(m), _pack(v))
    for tree, buf in zip((grads, delta, new_m, new_v), packed):
        tree.update(_unpack(buf, w))
    loss = packed[0][-1, -1]

    return (loss, grad_x, *[grads[n] for n in _WEIGHTS], *[delta[n] for n in _WEIGHTS],
            *[new_m[n] for n in _WEIGHTS], *[new_v[n] for n in _WEIGHTS])
```

```python
import functools
import math

import jax
import jax.numpy as jnp
from jax import lax
from jax.experimental import pallas as pl
from jax.experimental.pallas import tpu as pltpu

F32 = jnp.float32
BF16 = jnp.bfloat16

EPS = 1e-6
HEAD_DIM = 64
N_HEADS = 8
SB_WIDTH = 512
SSM_WIDTH = 512
SSM_GROUP = 16
SSM_GROUPS = 32
SSM_STATE = 64
QBLOCK = 128
KBLOCK = 256
N_CHUNK = 8
SSM_COLS = 4
LANES = 128
N_DEV = 8

ADAM_LR = 0.001
ADAM_B1 = 0.9
ADAM_B2 = 0.999
ADAM_EPS = 1e-08
ADAM_WD = 0.01
ADAM_STEP = 10

VMEM_LIMIT = 56 * 1024 * 1024

_NT = (((1,), (1,)), ((), ()))
_NN = (((1,), (0,)), ((), ()))
_TN = (((0,), (0,)), ((), ()))


def _dot(a, b, dims=_NN):
    return lax.dot_general(a, b, dims, preferred_element_type=F32)


def _params(sem):
    return pltpu.CompilerParams(dimension_semantics=sem, vmem_limit_bytes=VMEM_LIMIT)


def _matmul(name, a, b, *, ta=False, tb=False, extras=(), epilogue=None, out_dtypes=(F32,), sums=(),
            prologue=None, col_blocked=False, tm=1024, tn=512, tk=4096):
    M, K = (a.shape[1], a.shape[0]) if ta else a.shape
    N = b.shape[0] if tb else b.shape[1]
    tm, tn, tk = min(tm, M), min(tn, N), min(tk, K)
    assert M % tm == 0 and N % tn == 0 and K % tk == 0, (name, M, N, K)
    assert not (sums or prologue) or (tn == N and tk == K), name
    assert not prologue or not (ta or col_blocked), name
    nk = K // tk
    n_ex, n_out, n_sum = len(extras), len(out_dtypes), len(sums)
    n_pro = 1 if prologue else 0
    dims = (((0 if ta else 1,), (1 if tb else 0,)), ((), ()))

    def body(*refs):
        a_ref, b_ref = refs[0], refs[1]
        ex_refs = refs[2 + n_pro:2 + n_pro + n_ex]
        o_refs = refs[2 + n_pro + n_ex:2 + n_pro + n_ex + n_out]
        s_refs = refs[2 + n_pro + n_ex + n_out:2 + n_pro + n_ex + n_out + n_sum]
        k = pl.program_id(2)
        if prologue:
            left = prologue[0](a_ref[...], refs[2][...]).astype(BF16)
            refs[2 + n_pro + n_ex + n_out + n_sum][...] = left
        else:
            left = a_ref[...].astype(BF16)
        part = _dot(left, b_ref[...].astype(BF16), dims)

        def finish(acc):
            outs = (acc,) if epilogue is None else epilogue(acc, *[e[...] for e in ex_refs])
            for o_ref, o in zip(o_refs, outs[:n_out]):
                o_ref[...] = o.astype(o_ref.dtype)
            if n_sum:
                @pl.when(pl.program_id(0) == 0)
                def _():
                    for s_ref in s_refs:
                        s_ref[...] = jnp.zeros_like(s_ref)

                for s_ref, v in zip(s_refs, outs[n_out:]):
                    s_ref[...] += v

        if nk == 1:
            finish(part)
        else:
            acc_ref = refs[-1]

            @pl.when(k == 0)
            def _():
                acc_ref[...] = part

            @pl.when(jnp.logical_and(k > 0, k < nk - 1))
            def _():
                acc_ref[...] += part

            @pl.when(k == nk - 1)
            def _():
                finish(acc_ref[...] + part)

    a_spec = pl.BlockSpec((tk, tm), lambda i, j, k: (k, i)) if ta else pl.BlockSpec((tm, tk), lambda i, j, k: (i, k))
    b_spec = pl.BlockSpec((tn, tk), lambda i, j, k: (j, k)) if tb else pl.BlockSpec((tk, tn), lambda i, j, k: (k, j))
    ex_specs = [pl.BlockSpec((1, tn), lambda i, j, k: (0, j)) if e.shape[0] == 1 else
                pl.BlockSpec((tm, tn), lambda i, j, k: (i, j)) for e in extras]
    if col_blocked:
        out_specs = [pl.BlockSpec((None, tm, tn), lambda i, j, k: (j, i, 0)) for _ in out_dtypes]
        out_shape = [jax.ShapeDtypeStruct((N // tn, M, tn), dt) for dt in out_dtypes]
    else:
        wide = [(dt, N) if not isinstance(dt, tuple) else dt for dt in out_dtypes]
        assert all(w == N for _, w in wide) or tn == N, name
        out_specs = [pl.BlockSpec((tm, tn if w == N else w), lambda i, j, k: (i, j)) for _, w in wide]
        out_shape = [jax.ShapeDtypeStruct((M, w), dt) for dt, w in wide]
    out_specs += [pl.BlockSpec(s, lambda i, j, k: (0, 0)) for s in sums]
    out_shape += [jax.ShapeDtypeStruct(s, F32) for s in sums]
    pro_specs, pro_args = [], []
    if prologue:
        pro_specs, pro_args = [pl.BlockSpec((1, tk), lambda i, j, k: (0, 0))], [prologue[1]]
        out_specs.append(pl.BlockSpec((tm, tk), lambda i, j, k: (i, 0)))
        out_shape.append(jax.ShapeDtypeStruct((M, K), BF16))
    outs = pl.pallas_call(
        body, name=name, grid=(M // tm, N // tn, nk),
        in_specs=[a_spec, b_spec, *pro_specs, *ex_specs], out_specs=out_specs, out_shape=out_shape,
        scratch_shapes=[pltpu.VMEM((tm, tn), F32)] if nk > 1 else [],
        compiler_params=_params(("arbitrary",) * 3 if sums else ("parallel", "parallel", "arbitrary")),
    )(a, b, *pro_args, *extras)
    return outs[0] if len(outs) == 1 else outs


def _rowwise(name, fn, rows, small, outs, sums=(), tile=256):
    specs, args = [], []
    T = None
    for r in rows:
        arr, cb, w = r if isinstance(r, tuple) else (r, 0, r.shape[1])
        T = arr.shape[0]
        specs.append((w, cb))
        args.append(arr)
    tile = min(tile, T)
    assert T % tile == 0
    n_r, n_s, n_o, n_a = len(rows), len(small), len(outs), len(sums)

    def body(*refs):
        r_refs = refs[:n_r]
        s_refs = refs[n_r:n_r + n_s]
        o_refs = refs[n_r + n_s:n_r + n_s + n_o]
        a_refs = refs[n_r + n_s + n_o:]
        res = fn(*[r[...] for r in r_refs], *[s[...] for s in s_refs])
        res = res if isinstance(res, (tuple, list)) else (res,)
        for o_ref, o in zip(o_refs, res[:n_o]):
            o_ref[...] = o.astype(o_ref.dtype)

        @pl.when(pl.program_id(0) == 0)
        def _():
            for a_ref in a_refs:
                a_ref[...] = jnp.zeros_like(a_ref)

        for a_ref, v in zip(a_refs, res[n_o:]):
            a_ref[...] += v.astype(F32)

    in_specs = [pl.BlockSpec((tile, w), functools.partial(lambda i, cb: (i, cb), cb=cb)) for w, cb in specs]
    in_specs += [pl.BlockSpec(s.shape, functools.partial(lambda i, nd: (0,) * nd, nd=s.ndim)) for s in small]
    out_specs = [pl.BlockSpec((tile, w), lambda i: (i, 0)) for w, _ in outs]
    out_specs += [pl.BlockSpec(s, functools.partial(lambda i, nd: (0,) * nd, nd=len(s))) for s in sums]
    out_shape = [jax.ShapeDtypeStruct((T, w), dt) for w, dt in outs]
    out_shape += [jax.ShapeDtypeStruct(s, F32) for s in sums]
    res = pl.pallas_call(
        body, name=name, grid=(T // tile,), in_specs=in_specs, out_specs=out_specs, out_shape=out_shape,
        compiler_params=_params(("arbitrary",)),
    )(*args, *small)
    return res[0] if len(res) == 1 else res


def _rms(x, g):
    return x * lax.rsqrt(jnp.mean(x * x, axis=-1, keepdims=True) + EPS) * g


def _glu_branch(y, pre, b_glu, g_out):
    g = jax.nn.gelu(y)
    return _rms(g * jax.nn.sigmoid(pre + b_glu), g_out)


def _split_dot(x, tri_bf):
    hi = x.astype(BF16)
    lo = (x - hi.astype(F32)).astype(BF16)
    return _dot(hi, tri_bf) + _dot(lo, tri_bf)


def _softplus(z):
    return jnp.maximum(z, 0.0) + jnp.log(1.0 + jnp.exp(-jnp.abs(z)))


def _head(h):
    return slice(h * HEAD_DIM, (h + 1) * HEAD_DIM)


def _head_mean(x, seg):
    return _split_dot(x, seg) * (1.0 / HEAD_DIM)


def _qk_norm(proj, gains):
    r = lax.div(lax.broadcasted_iota(jnp.int32, (SB_WIDTH, SB_WIDTH), 0), HEAD_DIM)
    c = lax.div(lax.broadcasted_iota(jnp.int32, (SB_WIDTH, SB_WIDTH), 1), HEAD_DIM)
    seg = (r == c).astype(BF16)
    q, k, v = (proj[:, i * SB_WIDTH:(i + 1) * SB_WIDTH] for i in range(3))
    qn = q * lax.rsqrt(_head_mean(q * q, seg) + EPS) * gains[:, 0:SB_WIDTH]
    kn = k * lax.rsqrt(_head_mean(k * k, seg) + EPS) * gains[:, SB_WIDTH:2 * SB_WIDTH]
    return proj, qn, kn, v


def _qk_norm_bwd(x, g, dy, seg):
    r = lax.rsqrt(_head_mean(x * x, seg) + EPS)
    gdy = g * dy
    dx = r * gdy - x * (r * r * r) * _head_mean(gdy * x, seg)
    return dx, jnp.sum(dy * x * r, axis=0, keepdims=True)


def _split_heads(refs, scratch, L):
    def chunk(i, _):
        r = pl.ds(pl.multiple_of(i * QBLOCK, QBLOCK), QBLOCK)
        for ref, s in zip(refs, scratch):
            for h in range(2):
                s[h, r, :] = ref[r, _head(h)]
        return 0

    lax.fori_loop(0, L // QBLOCK, chunk, 0)


Q_HALVES = KBLOCK // QBLOCK
_CHAINS = [(h, r) for h in range(2) for r in range(Q_HALVES)]


def _valid(i, kb):
    row = lax.broadcasted_iota(jnp.int32, (QBLOCK, KBLOCK), 0)
    col = lax.broadcasted_iota(jnp.int32, (QBLOCK, KBLOCK), 1)
    return col + (kb * KBLOCK - i * QBLOCK) < row


def _attn_fwd(qn, kn, vb, B, L):
    n_pairs = L // KBLOCK
    n_hp = N_HEADS // 2
    nc = len(_CHAINS)

    def body(q_ref, k_ref, v_ref, o_ref, a_ref, q_s, k_s, v_s, after_s, z_s, stage_s, sems):
        _split_heads((q_ref, k_ref, v_ref), (q_s, k_s, v_s), L)
        r2 = lax.broadcasted_iota(jnp.int32, (KBLOCK, KBLOCK), 0)
        c2 = lax.broadcasted_iota(jnp.int32, (KBLOCK, KBLOCK), 1)
        after_s[...] = (r2 > c2).astype(after_s.dtype)
        g = pl.program_id(0) * n_hp + pl.program_id(1)

        def q_pair(p, _):
            rows = [pl.ds(pl.multiple_of((p * Q_HALVES + r) * QBLOCK, QBLOCK), QBLOCK) for r in range(Q_HALVES)]
            q_c = [q_s[h, rows[r], :] for h, r in _CHAINS]
            cs = range(nc)

            def scores(kb):
                rk = pl.ds(pl.multiple_of(kb * KBLOCK, KBLOCK), KBLOCK)
                return [_dot(q_c[c], k_s[_CHAINS[c][0], rk, :], _NT) for c in cs]

            def saved(kb):
                return pltpu.make_async_copy(stage_s.at[kb & 1], a_ref.at[g, p, kb], sems.at[kb & 1])

            def k_block(kb, carry, diagonal):
                rk = pl.ds(pl.multiple_of(kb * KBLOCK, KBLOCK), KBLOCK)
                if diagonal:
                    valid = [_valid(p * Q_HALVES + r, kb) for r in range(Q_HALVES)]
                    keep = lambda c, t: jnp.where(valid[_CHAINS[c][1]], t, 0.0)
                    z = scores(kb)
                else:
                    keep = lambda c, t: t
                    z = [z_s[(kb + 1) & 1, c] for c in cs]
                ahead = scores(jnp.maximum(kb - 1, 0))
                for c in cs:
                    z_s[kb & 1, c] = ahead[c]
                sp = [_softplus(z[c]) for c in cs]
                lsig = [z[c] - sp[c] for c in cs]
                lom = [keep(c, -sp[c]) for c in cs]
                tail = [_split_dot(lom[c], after_s[...]) + carry[c][0] for c in cs]
                a = [keep(c, jnp.exp(lsig[c] + tail[c])).astype(v_s.dtype) for c in cs]
                acc = [carry[c][1] + _dot(a[c], v_s[_CHAINS[c][0], rk, :]) for c in cs]
                for c in cs:
                    stage_s[kb & 1, c] = a[c]
                saved(kb).start()
                return tuple((carry[c][0] + jnp.sum(lom[c], axis=1, keepdims=True), acc[c]) for c in cs)

            def next_block(n, carry):
                kb = p - n

                @pl.when(n >= 2)
                def _():
                    saved(kb + 2).wait()

                return k_block(kb, carry, False)

            init = (jnp.zeros((QBLOCK, 1), F32), jnp.zeros((QBLOCK, HEAD_DIM), F32))
            first = k_block(p, (init,) * nc, True)
            res = lax.fori_loop(1, p + 1, next_block, first)
            saved(0).wait()

            @pl.when(p >= 1)
            def _():
                saved(1).wait()

            for r in range(Q_HALVES):
                o_ref[rows[r], :] = jnp.concatenate([res[c][1] for c in cs if _CHAINS[c][1] == r], axis=1)
            return 0

        lax.fori_loop(0, n_pairs, q_pair, 0)

    spec = pl.BlockSpec((L, LANES), lambda b, p: (b, p))
    return pl.pallas_call(
        body, name="attn_fwd", grid=(B, n_hp),
        in_specs=[spec] * 3, out_specs=[spec, _ANY],
        out_shape=[jax.ShapeDtypeStruct((B * L, SB_WIDTH), F32),
                   jax.ShapeDtypeStruct((B * n_hp, n_pairs, n_pairs, nc, QBLOCK, KBLOCK), BF16)],
        scratch_shapes=[pltpu.VMEM((2, L, HEAD_DIM), BF16)] * 3 + [pltpu.VMEM((KBLOCK, KBLOCK), BF16)]
        + [pltpu.VMEM((2, nc, QBLOCK, KBLOCK), F32), pltpu.VMEM((2, nc, QBLOCK, KBLOCK), BF16),
           pltpu.SemaphoreType.DMA((2,))],
        compiler_params=_params(("parallel", "parallel")),
    )(qn, kn, vb)


def _attn_bwd(qn, kn, vb, kept_a, d_sb, proj, gq2, gk2, B, L):
    n_pairs = L // KBLOCK
    n_hp = N_HEADS // 2
    nc = len(_CHAINS)
    slots = 3
    scale = 1.0 / math.sqrt(HEAD_DIM)

    def body(q_ref, k_ref, v_ref, do_ref, qraw_ref, kraw_ref, gq_ref, gk_ref, a_ref,
             dq_ref, dk_ref, dv_ref, dgq_ref, dgk_ref,
             q_s, k_s, v_s, qt_s, dkt_s, dvt_s, before_s, stage_s, sems):
        _split_heads((q_ref, k_ref, v_ref), (q_s, k_s, v_s), L)
        g = pl.program_id(0) * n_hp + pl.program_id(1)
        lane_head = [lax.div(lax.broadcasted_iota(jnp.int32, (LANES, LANES), d), HEAD_DIM) for d in (0, 1)]
        seg = (lane_head[0] == lane_head[1]).astype(BF16)

        def transpose_q(i, _):
            r = pl.ds(pl.multiple_of(i * QBLOCK, QBLOCK), QBLOCK)
            qt_s[:, r] = q_ref[r, :].astype(F32).T.astype(qt_s.dtype)
            return 0

        lax.fori_loop(0, L // QBLOCK, transpose_q, 0)
        dkt_s[...] = jnp.zeros_like(dkt_s)
        dvt_s[...] = jnp.zeros_like(dvt_s)
        r2 = lax.broadcasted_iota(jnp.int32, (KBLOCK, KBLOCK), 0)
        c2 = lax.broadcasted_iota(jnp.int32, (KBLOCK, KBLOCK), 1)
        before_s[...] = (r2 < c2).astype(before_s.dtype)

        def q_pair(p, dgq):
            rows = [pl.ds(pl.multiple_of((p * Q_HALVES + r) * QBLOCK, QBLOCK), QBLOCK) for r in range(Q_HALVES)]
            pair = pl.ds(pl.multiple_of(p * KBLOCK, KBLOCK), KBLOCK)
            do2 = do_ref[pair, :]
            do_t = do2.T.astype(v_s.dtype)
            cs = range(nc)
            hs = range(2)
            q_c = [q_s[h, rows[r], :] for h, r in _CHAINS]
            do_c = [do2[r * QBLOCK:(r + 1) * QBLOCK, _head(h)].astype(v_s.dtype) for h, r in _CHAINS]
            qt_h = [qt_s[_head(h), pair] for h in hs]
            dot_h = [do_t[_head(h), :] for h in hs]

            def kept(kb):
                slot = lax.rem(kb, slots)
                return pltpu.make_async_copy(a_ref.at[g, p, kb], stage_s.at[slot], sems.at[slot])

            def k_block(kb, carry, diagonal):
                rk = pl.ds(pl.multiple_of(kb * KBLOCK, KBLOCK), KBLOCK)
                if diagonal:
                    valid = [_valid(p * Q_HALVES + r, kb) for r in range(Q_HALVES)]
                    keep = lambda c, t: jnp.where(valid[_CHAINS[c][1]], t, 0.0)
                else:
                    keep = lambda c, t: t

                    @pl.when(kb + 2 <= p)
                    def _():
                        kept(kb + 2).start()

                kept(kb).wait()
                slot = lax.rem(kb, slots)
                k_b = [k_s[h, rk, :] for h in hs]
                z = [_dot(q_c[c], k_b[_CHAINS[c][0]], _NT) for c in cs]
                da = [_dot(do_c[c], v_s[_CHAINS[c][0], rk, :], _NT) for c in cs]
                a = [stage_s[slot, c] for c in cs]
                dla = [a[c].astype(F32) * da[c] for c in cs]
                for h in hs:
                    a_h = jnp.concatenate([a[c] for c in cs if _CHAINS[c][0] == h], axis=0)
                    dvt_s[_head(h), rk] += _dot(dot_h[h], a_h)
                d_lom = [carry[c][0] + _split_dot(dla[c], before_s[...]) for c in cs]
                beta = [jax.nn.sigmoid(z[c]) for c in cs]
                dz_b = [(dla[c] * (1.0 - beta[c]) - keep(c, beta[c] * d_lom[c])).astype(v_s.dtype) for c in cs]
                dq_acc = [carry[c][1] + _dot(dz_b[c], k_b[_CHAINS[c][0]]) for c in cs]
                for h in hs:
                    dz_h = jnp.concatenate([dz_b[c] for c in cs if _CHAINS[c][0] == h], axis=0)
                    dkt_s[_head(h), rk] += _dot(qt_h[h], dz_h)
                return tuple((carry[c][0] + jnp.sum(dla[c], axis=1, keepdims=True), dq_acc[c]) for c in cs)

            init = (jnp.zeros((QBLOCK, 1), F32), jnp.zeros((QBLOCK, HEAD_DIM), F32))
            kept(0).start()

            @pl.when(p >= 1)
            def _():
                kept(1).start()

            before = lax.fori_loop(0, p, lambda kb, carry: k_block(kb, carry, False), (init,) * nc)
            res = k_block(p, before, True)
            for r in range(Q_HALVES):
                d_qn = jnp.concatenate([res[c][1] for c in cs if _CHAINS[c][1] == r], axis=1) * scale
                dq, dg = _qk_norm_bwd(qraw_ref[rows[r], :], gq_ref[...], d_qn, seg)
                dq_ref[rows[r], :] = dq.astype(dq_ref.dtype)
                dgq = dgq + dg
            return dgq

        dgq = lax.fori_loop(0, n_pairs, q_pair, jnp.zeros((1, LANES), F32))

        def transpose_out(i, dgk):
            r = pl.ds(pl.multiple_of(i * QBLOCK, QBLOCK), QBLOCK)
            dk, dg = _qk_norm_bwd(kraw_ref[r, :], gk_ref[...], dkt_s[:, r].T, seg)
            dk_ref[r, :] = dk.astype(dk_ref.dtype)
            dv_ref[r, :] = dvt_s[:, r].T.astype(dv_ref.dtype)
            return dgk + dg

        dgk = lax.fori_loop(0, L // QBLOCK, transpose_out, jnp.zeros((1, LANES), F32))

        @pl.when(jnp.logical_and(pl.program_id(0) == 0, pl.program_id(1) == 0))
        def _():
            dgq_ref[...] = jnp.zeros_like(dgq_ref)
            dgk_ref[...] = jnp.zeros_like(dgk_ref)

        dgq_ref[...] += dgq[:, _head(0)] + dgq[:, _head(1)]
        dgk_ref[...] += dgk[:, _head(0)] + dgk[:, _head(1)]

    spec = pl.BlockSpec((L, LANES), lambda b, p: (b, p))
    gain = pl.BlockSpec((1, LANES), lambda b, p: (0, 0))
    gain_grad = pl.BlockSpec((1, HEAD_DIM), lambda b, p: (0, 0))
    return pl.pallas_call(
        body, name="attn_bwd", grid=(B, n_hp),
        in_specs=[spec] * 4 + [spec, pl.BlockSpec((L, LANES), lambda b, p: (b, n_hp + p)), gain, gain, _ANY],
        out_specs=[spec] * 3 + [gain_grad] * 2,
        out_shape=[jax.ShapeDtypeStruct((B * L, SB_WIDTH), BF16)] * 3 + [jax.ShapeDtypeStruct((1, HEAD_DIM), F32)] * 2,
        scratch_shapes=[pltpu.VMEM((2, L, HEAD_DIM), BF16)] * 3 + [pltpu.VMEM((LANES, L), BF16)]
        + [pltpu.VMEM((LANES, L), F32)] * 2 + [pltpu.VMEM((KBLOCK, KBLOCK), BF16)]
        + [pltpu.VMEM((slots, nc, QBLOCK, KBLOCK), BF16), pltpu.SemaphoreType.DMA((slots,))],
        compiler_params=_params(("arbitrary", "arbitrary")),
    )(qn, kn, vb, d_sb, proj, proj, gq2, gk2, kept_a)


def _ssm_discretise(lam_re, lam_im, log_dt, b_re, b_im):
    dt = jnp.exp(log_dt)
    mag = jnp.exp(lam_re * dt)
    lbr = mag * jnp.cos(lam_im * dt)
    lbi = mag * jnp.sin(lam_im * dt)
    den = lam_re * lam_re + lam_im * lam_im
    nr, ni = lbr - 1.0, lbi
    cr = (nr * lam_re + ni * lam_im) / den
    ci = (ni * lam_re - nr * lam_im) / den
    return lbr, lbi, cr * b_re - ci * b_im, cr * b_im + ci * b_re


def _ssm_prep(lam_re, lam_im, log_dt, b_re_t, b_im_t):
    def body(lr, li, ld, br, bi, o_lr, o_li, o_br, o_bi):
        res = _ssm_discretise(lr[...], li[...], ld[...], br[...], bi[...])
        for o, v in zip((o_lr, o_li, o_br, o_bi), res):
            o[...] = v

    return pl.pallas_call(
        body, name="ssm_prep",
        out_shape=[jax.ShapeDtypeStruct(lam_re.shape, F32)] * 2 + [jax.ShapeDtypeStruct(b_re_t.shape, F32)] * 2,
    )(lam_re, lam_im, log_dt, b_re_t, b_im_t)


def _ssm_prep_bwd(lam_re, lam_im, log_dt, b_re_t, b_im_t, d_lr, d_li, d_br, d_bi):
    def body(lr, li, ld, br, bi, g_lr, g_li, g_br, g_bi, o_lr, o_li, o_ld, o_br, o_bi):
        _, vjp = jax.vjp(_ssm_discretise, lr[...], li[...], ld[...], br[...], bi[...])
        res = vjp((g_lr[...], g_li[...], g_br[...], g_bi[...]))
        for o, v in zip((o_lr, o_li, o_ld, o_br, o_bi), res):
            o[...] = v

    return pl.pallas_call(
        body, name="ssm_prep_bwd",
        out_shape=[jax.ShapeDtypeStruct(lam_re.shape, F32)] * 2 + [jax.ShapeDtypeStruct(log_dt.shape, F32)]
        + [jax.ShapeDtypeStruct(b_re_t.shape, F32)] * 2,
    )(lam_re, lam_im, log_dt, b_re_t, b_im_t, d_lr, d_li, d_br, d_bi)


def _block_diag(m):
    m4 = m.reshape(SSM_COLS, 8, SSM_GROUP, SSM_STATE)
    return jnp.einsum("aghp,gk->aghkp", m4, jnp.eye(8, dtype=m.dtype)).reshape(SSM_COLS, LANES, 512)


def _block_diag_take(d):
    d6 = d.reshape(SSM_COLS, 8, SSM_GROUP, 2, 8, SSM_STATE)
    return jnp.einsum("aghrgp->raghp", d6).reshape(2, SSM_GROUPS, SSM_GROUP, SSM_STATE)


def _cmul(ar, ai, br, bi):
    return ar * br - ai * bi, ar * bi + ai * br


def _power(lr, li, n):
    assert n & (n - 1) == 0
    for _ in range(n.bit_length() - 1):
        lr, li = _cmul(lr, li, lr, li)
    return lr, li


def _ssm_fwd(u_p, w_b, lam_r, lam_i, c_m, d_skip, B, L, tj):
    J = L // N_CHUNK
    njt = J // tj
    R = tj * N_CHUNK
    H = 512

    def body(u_ref, wb_ref, lr_ref, li_ref, cm_ref, d_ref, y_ref, gel_ref, x_ref, xin_ref, bu_s, st_s, xin_s):
        ph, jt = pl.program_id(2), pl.program_id(3)
        lr, li = lr_ref[...], li_ref[...]

        @pl.when(jnp.logical_and(ph == 0, jt == 0))
        def _():
            st_s[...] = jnp.zeros_like(st_s)

        @pl.when(ph == 0)
        def _():
            bu_s[jt] = _dot(u_ref[...].astype(BF16), wb_ref[...].astype(BF16))

        def scan(store):
            def step(j, carry):
                xr, xi = carry
                r = pl.ds(pl.multiple_of(j * N_CHUNK, N_CHUNK), N_CHUNK)
                nr = lr * xr - li * xi + bu_s[jt, r, 0:H]
                ni = lr * xi + li * xr + bu_s[jt, r, H:2 * H]
                if store:
                    x_ref[r, 0:H] = nr
                    x_ref[r, H:2 * H] = ni
                return nr, ni

            xr, xi = lax.fori_loop(0, tj, step, (st_s[:, 0:H], st_s[:, H:2 * H]))
            st_s[:, 0:H] = xr
            st_s[:, H:2 * H] = xi

        @pl.when(ph == 0)
        def _():
            scan(False)

            @pl.when(jt == njt - 1)
            def _():
                pr, pi = _power(lr[0:1], li[0:1], J)
                xin_s[0:1, :] = jnp.zeros((1, 2 * H), F32)
                for c in range(1, N_CHUNK):
                    qr, qi = _cmul(pr, pi, xin_s[c - 1:c, 0:H], xin_s[c - 1:c, H:2 * H])
                    xin_s[c:c + 1, 0:H] = qr + st_s[c - 1:c, 0:H]
                    xin_s[c:c + 1, H:2 * H] = qi + st_s[c - 1:c, H:2 * H]
                xin_ref[...] = xin_s[...]
                st_s[...] = xin_s[...]

        @pl.when(ph == 1)
        def _():
            scan(True)
            y = _dot(x_ref[...].astype(BF16), cm_ref[...].astype(BF16)) + d_ref[...] * u_ref[...]
            y_ref[...] = y
            gel_ref[...] = jax.nn.gelu(y).astype(gel_ref.dtype)

    return pl.pallas_call(
        body, name="ssm_fwd", grid=(SSM_COLS, B, 2, njt),
        in_specs=[
            pl.BlockSpec((None, R, LANES), lambda i, b, ph, jt: (b, jt, i)),
            pl.BlockSpec((None, LANES, 2 * H), lambda i, b, ph, jt: (i, 0, 0)),
            pl.BlockSpec((None, N_CHUNK, H), lambda i, b, ph, jt: (i, 0, 0)),
            pl.BlockSpec((None, N_CHUNK, H), lambda i, b, ph, jt: (i, 0, 0)),
            pl.BlockSpec((None, 2 * H, LANES), lambda i, b, ph, jt: (i, 0, 0)),
            pl.BlockSpec((1, LANES), lambda i, b, ph, jt: (0, i)),
        ],
        out_specs=[
            pl.BlockSpec((None, R, LANES), lambda i, b, ph, jt: (b, jt * ph, i)),
            pl.BlockSpec((None, R, LANES), lambda i, b, ph, jt: (b, jt * ph, i)),
            pl.BlockSpec((None, R, 2 * H), lambda i, b, ph, jt: (b, jt * ph, i)),
            pl.BlockSpec((None, None, N_CHUNK, 2 * H), lambda i, b, ph, jt: (b, i, 0, 0)),
        ],
        out_shape=[
            jax.ShapeDtypeStruct((B, L, SSM_WIDTH), F32),
            jax.ShapeDtypeStruct((B, L, SSM_WIDTH), BF16),
            jax.ShapeDtypeStruct((B, L, SSM_COLS * 2 * H), F32),
            jax.ShapeDtypeStruct((B, SSM_COLS, N_CHUNK, 2 * H), F32),
        ],
        scratch_shapes=[pltpu.VMEM((njt, R, 2 * H), F32), pltpu.VMEM((N_CHUNK, 2 * H), F32),
                        pltpu.VMEM((N_CHUNK, 2 * H), F32)],
        compiler_params=_params(("arbitrary",) * 4),
    )(u_p, w_b, lam_r, lam_i, c_m, d_skip)


def _ssm_bwd(dy_p, u_p, x, xin, w_bt, lam_r, lam_i, c_mt, d_skip, B, L, tj):
    J = L // N_CHUNK
    njt = J // tj
    R = tj * N_CHUNK
    H = 512
    x4 = x.reshape(B, J, N_CHUNK, SSM_COLS * 2 * H)

    def body(dy_ref, u_ref, x_ref, xp_ref, xin_ref, wbt_ref, lr_ref, li_ref, cmt_ref, d_ref,
             du_ref, dwb_ref, dcm_ref, dlr_ref, dli_ref, dd_ref, ca_s, a_s, st_s, dl_s):
        b, ph, jt = pl.program_id(1), pl.program_id(2), pl.program_id(3)
        jr = njt - 1 - jt
        lr, li = lr_ref[...], -li_ref[...]

        @pl.when(jnp.logical_and(b == 0, jnp.logical_and(ph == 0, jt == 0)))
        def _():
            dwb_ref[...] = jnp.zeros_like(dwb_ref)
            dcm_ref[...] = jnp.zeros_like(dcm_ref)
            dlr_ref[...] = jnp.zeros_like(dlr_ref)
            dli_ref[...] = jnp.zeros_like(dli_ref)
            dd_ref[...] = jnp.zeros_like(dd_ref)
            dl_s[...] = jnp.zeros_like(dl_s)

        @pl.when(jnp.logical_and(ph == 0, jt == 0))
        def _():
            st_s[...] = jnp.zeros_like(st_s)

        @pl.when(ph == 0)
        def _():
            ca_s[jt] = _dot(dy_ref[...].astype(BF16), cmt_ref[...].astype(BF16))

        def scan(store):
            def step(n, carry):
                ar, ai = carry
                r = pl.ds(pl.multiple_of((tj - 1 - n) * N_CHUNK, N_CHUNK), N_CHUNK)
                nr = lr * ar - li * ai + ca_s[jt, r, 0:H]
                ni = lr * ai + li * ar + ca_s[jt, r, H:2 * H]
                if store:
                    a_s[r, 0:H] = nr
                    a_s[r, H:2 * H] = ni
                return nr, ni

            ar, ai = lax.fori_loop(0, tj, step, (st_s[:, 0:H], st_s[:, H:2 * H]))
            st_s[:, 0:H] = ar
            st_s[:, H:2 * H] = ai

        @pl.when(ph == 0)
        def _():
            scan(False)

            @pl.when(jt == njt - 1)
            def _():
                pr, pi = _power(lr[0:1], li[0:1], J)
                a_s[N_CHUNK - 1:N_CHUNK, :] = jnp.zeros((1, 2 * H), F32)
                for c in range(N_CHUNK - 2, -1, -1):
                    qr, qi = _cmul(pr, pi, a_s[c + 1:c + 2, 0:H], a_s[c + 1:c + 2, H:2 * H])
                    a_s[c:c + 1, 0:H] = qr + st_s[c + 1:c + 2, 0:H]
                    a_s[c:c + 1, H:2 * H] = qi + st_s[c + 1:c + 2, H:2 * H]
                st_s[...] = a_s[0:N_CHUNK, :]

        @pl.when(ph == 1)
        def _():
            scan(True)
            dy = dy_ref[...]
            u = u_ref[...]
            a_b = a_s[...].astype(BF16)
            du_ref[...] = (_dot(a_b, wbt_ref[...].astype(BF16)) + d_ref[...] * dy).astype(du_ref.dtype)
            dwb_ref[...] += _dot(u.astype(BF16), a_b, _TN)
            dcm_ref[...] += _dot(x_ref[...].astype(BF16), dy.astype(BF16), _TN)
            dd_ref[...] += jnp.sum(dy * u, axis=0, keepdims=True)

            first = jnp.where(jr == 0, xin_ref[...], xp_ref[...])
            a0r, a0i = a_s[0:N_CHUNK, 0:H], a_s[0:N_CHUNK, H:2 * H]
            acc0 = (a0r * first[:, 0:H] + a0i * first[:, H:2 * H], a0i * first[:, 0:H] - a0r * first[:, H:2 * H])

            def step(j, carry):
                sr, si = carry
                r = pl.ds(pl.multiple_of(j * N_CHUNK, N_CHUNK), N_CHUNK)
                rp = pl.ds(pl.multiple_of((j - 1) * N_CHUNK, N_CHUNK), N_CHUNK)
                ar, ai = a_s[r, 0:H], a_s[r, H:2 * H]
                xr, xi = x_ref[rp, 0:H], x_ref[rp, H:2 * H]
                return sr + ar * xr + ai * xi, si + ai * xr - ar * xi

            sr, si = lax.fori_loop(1, tj, step, acc0)
            dl_s[:, 0:H] += sr
            dl_s[:, H:2 * H] += si

            @pl.when(jnp.logical_and(b == B - 1, jt == njt - 1))
            def _():
                dlr_ref[...] = jnp.sum(dl_s[:, 0:H], axis=0, keepdims=True)
                dli_ref[...] = jnp.sum(dl_s[:, H:2 * H], axis=0, keepdims=True)
                dl_s[...] = jnp.zeros_like(dl_s)

    rev = lambda ph, jt: (njt - 1 - jt) * ph + (njt - 1) * (1 - ph)
    return pl.pallas_call(
        body, name="ssm_bwd", grid=(SSM_COLS, B, 2, njt),
        in_specs=[
            pl.BlockSpec((None, R, LANES), lambda i, b, ph, jt: (b, njt - 1 - jt, i)),
            pl.BlockSpec((None, R, LANES), lambda i, b, ph, jt: (b, njt - 1 - jt, i)),
            pl.BlockSpec((None, R, 2 * H), lambda i, b, ph, jt: (b, rev(ph, jt), i)),
            pl.BlockSpec((None, None, N_CHUNK, 2 * H),
                         lambda i, b, ph, jt: (b, jnp.maximum((njt - 1 - jt) * tj - 1, 0), 0, i)),
            pl.BlockSpec((None, None, N_CHUNK, 2 * H), lambda i, b, ph, jt: (b, i, 0, 0)),
            pl.BlockSpec((None, 2 * H, LANES), lambda i, b, ph, jt: (i, 0, 0)),
            pl.BlockSpec((None, N_CHUNK, H), lambda i, b, ph, jt: (i, 0, 0)),
            pl.BlockSpec((None, N_CHUNK, H), lambda i, b, ph, jt: (i, 0, 0)),
            pl.BlockSpec((None, LANES, 2 * H), lambda i, b, ph, jt: (i, 0, 0)),
            pl.BlockSpec((1, LANES), lambda i, b, ph, jt: (0, i)),
        ],
        out_specs=[
            pl.BlockSpec((None, R, LANES), lambda i, b, ph, jt: (b, rev(ph, jt), i)),
            pl.BlockSpec((None, LANES, 2 * H), lambda i, b, ph, jt: (i, 0, 0)),
            pl.BlockSpec((None, 2 * H, LANES), lambda i, b, ph, jt: (i, 0, 0)),
            pl.BlockSpec((None, 1, H), lambda i, b, ph, jt: (i, 0, 0)),
            pl.BlockSpec((None, 1, H), lambda i, b, ph, jt: (i, 0, 0)),
            pl.BlockSpec((1, LANES), lambda i, b, ph, jt: (0, i)),
        ],
        out_shape=[
            jax.ShapeDtypeStruct((B, L, SSM_WIDTH), BF16),
            jax.ShapeDtypeStruct((SSM_COLS, LANES, 2 * H), F32),
            jax.ShapeDtypeStruct((SSM_COLS, 2 * H, LANES), F32),
            jax.ShapeDtypeStruct((SSM_COLS, 1, H), F32),
            jax.ShapeDtypeStruct((SSM_COLS, 1, H), F32),
            jax.ShapeDtypeStruct((1, SSM_WIDTH), F32),
        ],
        scratch_shapes=[pltpu.VMEM((njt, R, 2 * H), F32), pltpu.VMEM((R, 2 * H), F32),
                        pltpu.VMEM((N_CHUNK, 2 * H), F32), pltpu.VMEM((N_CHUNK, 2 * H), F32)],
        compiler_params=_params(("arbitrary",) * 4),
    )(dy_p, u_p, x, x4, xin, w_bt, lam_r, lam_i, c_mt, d_skip)


def _to_scan_layout(t, B, L):
    C = t.shape[-1]
    return t.reshape(B, N_CHUNK, L // N_CHUNK, C).transpose(0, 2, 1, 3).reshape(B, L, C)


def _from_scan_layout(t, B, L):
    C = t.shape[-1]
    return t.reshape(B, L // N_CHUNK, N_CHUNK, C).transpose(0, 2, 1, 3).reshape(B * L, C)


def _local_step(x, target, p, w_in, late_weights, mlp_grads_ready=None, rest_grads_ready=None, order=None, *,
                ssm_tile=128):
    B, L, D = x.shape
    T = B * L
    x2 = x.reshape(T, D)
    row = lambda v: v.reshape(1, -1)
    g1, g2, ga, gs, b_glu = row(p["norm1_g"]), row(p["norm2_g"]), row(p["attn_out_g"]), row(p["ssm_out_g"]), row(p["b_glu"])
    g1_first = g1 if order is None else g1 + order
    gq8 = jnp.tile(row(p["q_norm_g"]), (1, N_HEADS))
    gk8 = jnp.tile(row(p["k_norm_g"]), (1, N_HEADS))

    G, P, Hh = SSM_GROUPS, SSM_STATE, SSM_GROUP
    lam_re3, lam_im3 = p["ssm_lambda_re"].reshape(G, 1, P), p["ssm_lambda_im"].reshape(G, 1, P)
    log_dt3 = p["ssm_log_dt"].reshape(G, 1, 1)
    b_re_t, b_im_t = p["ssm_b_re"].transpose(0, 2, 1), p["ssm_b_im"].transpose(0, 2, 1)
    lbr, lbi, bbr, bbi = _ssm_prep(lam_re3, lam_im3, log_dt3, b_re_t, b_im_t)
    w_b = jnp.concatenate([_block_diag(bbr), _block_diag(bbi)], axis=2)
    c_mt = jnp.concatenate([_block_diag(p["ssm_c_re"]), -_block_diag(p["ssm_c_im"])], axis=2)
    w_bt, c_m = w_b.transpose(0, 2, 1), c_mt.transpose(0, 2, 1)
    lam_r = jnp.broadcast_to(lbr.reshape(SSM_COLS, 1, 512), (SSM_COLS, N_CHUNK, 512))
    lam_i = jnp.broadcast_to(lbi.reshape(SSM_COLS, 1, 512), (SSM_COLS, N_CHUNK, 512))
    d_skip = p["ssm_d"].reshape(1, SSM_WIDTH)

    qk_gains = jnp.concatenate([gq8 * (1.0 / math.sqrt(HEAD_DIM)), gk8, jnp.ones((1, 2 * SB_WIDTH), F32)], axis=1)
    head = (BF16, SB_WIDTH)
    proj, qn, kn, vb, xn = _matmul("proj", x2, w_in, prologue=(_rms, g1_first), extras=[qk_gains],
                                   out_dtypes=(F32, head, head, head), epilogue=_qk_norm, tm=512, tn=w_in.shape[1])
    sb, attn_kept = _attn_fwd(qn, kn, vb, B, L)
    u_p = _to_scan_layout(proj[:, 3 * SB_WIDTH:], B, L)
    y_p, gel_p, xs, xin = _ssm_fwd(u_p, w_b, lam_r, lam_i, c_m, d_skip, B, L, ssm_tile)
    y2, gel = y_p.reshape(T, SSM_WIDTH), gel_p.reshape(T, SSM_WIDTH)
    w_glu, w_out, w_mlp_in, w_mlp_out = late_weights(gel)
    pre, ssm_n = _matmul("glu_gate", gel, w_glu, extras=[y2, b_glu, gs], out_dtypes=(F32, BF16),
                         epilogue=lambda acc, y, b, g: (acc, _glu_branch(y, acc, b, g)))
    sb_n = _rowwise("attn_out_norm", _rms, [sb], [ga], [(SB_WIDTH, BF16)])
    mixed = jnp.concatenate([sb_n, _from_scan_layout(ssm_n, B, L)], axis=1)
    def residual_and_norm(acc, res, g):
        h = acc + res
        return h, _rms(h, g)

    h1, hn = _matmul("out_proj", mixed, w_out, extras=[x2, g2], out_dtypes=(F32, BF16), tn=D,
                     epilogue=residual_and_norm)
    act, a_pre = _matmul("mlp_in", hn, w_mlp_in, out_dtypes=(BF16, BF16), tn=1024,
                         epilogue=lambda acc: (jnp.square(jnp.maximum(acc, 0.0)), acc))

    def loss_fn(acc, h, t):
        diff = acc + h - t
        part = jnp.sum(jnp.sum(diff * diff, axis=0, keepdims=True), axis=1, keepdims=True)
        d = diff * (1.0 / D)
        return d, d, part * (0.5 / D)

    d_out, d_out_b, loss = _matmul("mlp_out", act, w_mlp_out, extras=[h1, target.reshape(T, D)],
                                   out_dtypes=(F32, BF16), sums=[(1, 1)], epilogue=loss_fn, tm=512, tn=D)

    d_apre = _matmul("mlp_out_dx", d_out_b, w_mlp_out, tb=True, extras=[a_pre], out_dtypes=(BF16,), tn=1024,
                     epilogue=lambda acc, ap: (acc * (2.0 * jnp.maximum(ap.astype(F32), 0.0)),))
    both = lambda acc: (acc, acc)
    g_w_mlp_out, g_w_mlp_out_b = _matmul("mlp_out_dw", act, d_out_b, ta=True, out_dtypes=(F32, BF16), epilogue=both)
    g_w_mlp_in, g_w_mlp_in_b = _matmul("mlp_in_dw", hn, d_apre, ta=True, col_blocked=True, out_dtypes=(F32, BF16),
                                       epilogue=both, tn=w_mlp_in.shape[1] // N_DEV)
    if mlp_grads_ready is not None:
        g2 = g2 + mlp_grads_ready(g_w_mlp_out, g_w_mlp_out_b, g_w_mlp_in, g_w_mlp_in_b)

    def norm_bwd_res(dy, h, res, g):
        _, vjp = jax.vjp(_rms, h, g)
        dh, dg = vjp(dy)
        return res + dh, dg

    def norm_bwd_res2(dy, h, res, g):
        d, dg = norm_bwd_res(dy, h, res, g)
        return d, d, dg

    d_h1, d_h1_b, g_norm2 = _matmul("mlp_in_dx", d_apre, w_mlp_in, tb=True, extras=[h1, d_out, g2],
                                    out_dtypes=(F32, BF16), sums=[(1, D)], epilogue=norm_bwd_res2, tm=512, tn=D)

    d_mixed = _matmul("out_proj_dx", d_h1_b, w_out, tb=True, tn=1024)
    g_w_out, g_w_out_b = _matmul("out_proj_dw", mixed, d_h1_b, ta=True, out_dtypes=(F32, BF16), epilogue=both)

    def norm_bwd(h, dy, g):
        _, vjp = jax.vjp(_rms, h, g)
        return vjp(dy)

    d_sb, g_attn_out = _rowwise("attn_out_norm_bwd", norm_bwd, [sb, (d_mixed, 0, SB_WIDTH)], [ga],
                                [(SB_WIDTH, F32)], sums=[(1, SB_WIDTH)])
    d_ssm_n = _to_scan_layout(d_mixed[:, SB_WIDTH:], B, L).reshape(T, SSM_WIDTH)

    def glu_bwd(y, pre_, dy, bg, g):
        _, vjp = jax.vjp(_glu_branch, y, pre_, bg, g)
        d_y, d_pre, d_bg, d_g = vjp(dy)
        return d_y, d_pre, d_bg, d_g

    d_y_direct, d_pre, g_b_glu, g_ssm_out = _rowwise(
        "glu_out_bwd", glu_bwd, [y2, pre, d_ssm_n], [b_glu, gs], [(SSM_WIDTH, F32), (SSM_WIDTH, BF16)],
        sums=[(1, SSM_WIDTH), (1, SSM_WIDTH)])
    g_w_glu, g_w_glu_b = _matmul("glu_gate_dw", gel, d_pre, ta=True, out_dtypes=(F32, BF16), epilogue=both)

    def gelu_bwd(dg, y, dy0):
        _, vjp = jax.vjp(jax.nn.gelu, y)
        return (dy0 + vjp(dg)[0],)

    d_y = _matmul("glu_gate_dx", d_pre, w_glu, tb=True, extras=[y2, d_y_direct], epilogue=gelu_bwd)

    du_p, d_wb, d_cm, d_lr, d_li, g_d = _ssm_bwd(
        d_y.reshape(B, L, SSM_WIDTH), u_p, xs, xin, w_bt, lam_r, lam_i, c_mt, d_skip, B, L, ssm_tile)
    d_bb = _block_diag_take(d_wb.reshape(SSM_COLS, LANES, 2, 512))
    d_c = _block_diag_take(d_cm.transpose(0, 2, 1).reshape(SSM_COLS, LANES, 2, 512))
    g_lam_re, g_lam_im, g_log_dt, g_b_re_t, g_b_im_t = _ssm_prep_bwd(
        lam_re3, lam_im3, log_dt3, b_re_t, b_im_t,
        d_lr.reshape(G, 1, P), d_li.reshape(G, 1, P), d_bb[0], d_bb[1])
    d_q, d_k, d_v, g_q, g_k = _attn_bwd(qn, kn, vb, attn_kept, d_sb, proj, gq8[:, :LANES], gk8[:, :LANES], B, L)

    d_proj = jnp.concatenate([d_q, d_k, d_v, _from_scan_layout(du_p, B, L)], axis=1)
    g_w_in, g_w_in_b = _matmul("proj_dw", xn, d_proj, ta=True, col_blocked=True, out_dtypes=(F32, BF16),
                               epilogue=both, tn=w_in.shape[1] // N_DEV)
    if rest_grads_ready is not None:
        g1 = g1 + rest_grads_ready([g_w_in, g_w_glu, g_w_out], [g_w_in_b, g_w_glu_b, g_w_out_b])
    grad_x, g_norm1 = _matmul("proj_dx", d_proj, w_in, tb=True, extras=[x2, d_h1, g1], sums=[(1, D)],
                              epilogue=norm_bwd_res, tm=512, tn=D)

    small = {
        "norm1_g": g_norm1.reshape(-1),
        "q_norm_g": g_q.reshape(-1),
        "k_norm_g": g_k.reshape(-1),
        "ssm_lambda_re": g_lam_re.reshape(G, P),
        "ssm_lambda_im": g_lam_im.reshape(G, P),
        "ssm_log_dt": g_log_dt.reshape(G),
        "ssm_b_re": g_b_re_t.transpose(0, 2, 1),
        "ssm_b_im": g_b_im_t.transpose(0, 2, 1),
        "ssm_c_re": d_c[0],
        "ssm_c_im": -d_c[1],
        "ssm_d": g_d.reshape(G, Hh),
        "b_glu": g_b_glu.reshape(-1),
        "attn_out_g": g_attn_out.reshape(-1),
        "ssm_out_g": g_ssm_out.reshape(-1),
        "norm2_g": g_norm2.reshape(-1),
    }
    big = {"w_in": g_w_in, "w_glu": g_w_glu, "w_out": g_w_out, "w_mlp_in": g_w_mlp_in, "w_mlp_out": g_w_mlp_out}
    return loss[0, 0], grad_x.reshape(B, L, D), small, big


_ANY = pl.BlockSpec(memory_space=pl.ANY)
_MESH = pl.DeviceIdType.MESH


def _all_gather(name, shards):
    n = len(shards)

    def body(*refs):
        in_refs, out_refs = refs[:n], refs[n:2 * n]
        send_sems, recv_sems, local_sems = refs[2 * n:]
        x, y, c = lax.axis_index("x"), lax.axis_index("y"), lax.axis_index("c")
        me, sibling = (x, y, c), (x, y, 1 - c)
        chips = [(1 - x, y), (x, 1 - y), (1 - x, 1 - y)]

        def copy(a, k, block, to, src=None):
            px, py, pc = block
            rows = out_refs[a].at[4 * px + 2 * py + pc]
            return pltpu.make_async_remote_copy(
                src_ref=rows if src is None else src, dst_ref=rows, send_sem=send_sems.at[a, k],
                recv_sem=recv_sems.at[a, k], device_id=to, device_id_type=_MESH)

        mine = [pltpu.make_async_copy(in_refs[a], out_refs[a].at[4 * x + 2 * y + c], local_sems.at[a]) for a in range(n)]
        first, passed = [], []
        for a in range(n):
            mine[a].start()
            first.append(copy(a, 0, me, sibling, src=in_refs[a]))
            first += [copy(a, 1 + j, me, (*chip, c), src=in_refs[a]) for j, chip in enumerate(chips)]
        for cp in first:
            cp.start()
        for j, chip in enumerate(chips):
            for a in range(n):
                copy(a, 1 + j, (*chip, c), me).wait_recv()
                fwd = copy(a, 4 + j, (*chip, c), sibling)
                fwd.start()
                passed.append(fwd)
        for a in range(n):
            copy(a, 0, sibling, me).wait_recv()
            for j, chip in enumerate(chips):
                copy(a, 4 + j, (*chip, 1 - c), me).wait_recv()
        for cp in first + passed:
            cp.wait_send()
        for cp in mine:
            cp.wait()

    return pl.pallas_call(
        body, name=name,
        in_specs=[_ANY] * n, out_specs=[_ANY] * n,
        out_shape=[jax.ShapeDtypeStruct((N_DEV, *s.shape), s.dtype) for s in shards],
        scratch_shapes=[pltpu.SemaphoreType.DMA((n, 7)), pltpu.SemaphoreType.DMA((n, 7)), pltpu.SemaphoreType.DMA((n,))],
    )(*shards)


_HBM = pl.BlockSpec(memory_space=pltpu.HBM)
_SEM = pl.BlockSpec(memory_space=pltpu.SEMAPHORE)
_EFFECT = pltpu.SideEffectType.DATAFLOW_SIDE_EFFECTING
_FLIPS = [(dx, dy, dc) for dx in (0, 1) for dy in (0, 1) for dc in (0, 1) if (dx, dy, dc) != (0, 0, 0)]


def _exchange_start(name, srcs, lands, per_peer):
    n = len(srcs)

    def body(*refs):
        src_refs, land_refs = refs[:n], refs[n:2 * n]
        send_sems, recv_sems = refs[2 * n:3 * n], refs[3 * n:4 * n]
        token = refs[-1]
        x, y, c = lax.axis_index("x"), lax.axis_index("y"), lax.axis_index("c")
        me = 4 * x + 2 * y + c
        for dx, dy, dc in _FLIPS:
            px, py, pc = (1 - x if dx else x), (1 - y if dy else y), (1 - c if dc else c)
            for a in range(n):
                pltpu.make_async_remote_copy(
                    src_ref=src_refs[a].at[4 * px + 2 * py + pc] if per_peer else src_refs[a],
                    dst_ref=land_refs[a].at[me], send_sem=send_sems[a], recv_sem=recv_sems[a],
                    device_id=(px, py, pc), device_id_type=_MESH).start()
        token[...] = jnp.zeros_like(token)

    hbm = lambda t: pltpu.with_memory_space_constraint(t, pltpu.HBM)
    res = pl.pallas_call(
        body, name=name,
        out_shape=(*[pltpu.SemaphoreType.DMA(())] * (2 * n), *[pltpu.HBM(t.shape, t.dtype) for t in (*srcs, *lands)],
                   jax.ShapeDtypeStruct((8, LANES), F32)),
        in_specs=[_HBM] * (2 * n),
        out_specs=(*[_SEM] * (2 * n), *[_HBM] * (2 * n), pl.BlockSpec(memory_space=pltpu.VMEM)),
        input_output_aliases={i: 2 * n + i for i in range(2 * n)},
        compiler_params=pltpu.CompilerParams(has_side_effects=_EFFECT),
    )(*[hbm(t) for t in (*srcs, *lands)])
    return res[:-1], res[-1]


def _exchange_wait(name, handle, after):
    n = len(handle) // 4
    sems, thru = handle[:2 * n], handle[2 * n:]

    def body(*refs):
        land_refs = refs[n:2 * n]
        send_sems, recv_sems = refs[2 * n:3 * n], refs[3 * n:4 * n]
        me = (lax.axis_index("x"), lax.axis_index("y"), lax.axis_index("c"))
        for a in range(n):
            seven = land_refs[a].at[pl.ds(0, len(_FLIPS))]
            all_copies = pltpu.make_async_remote_copy(
                src_ref=seven, dst_ref=seven, send_sem=send_sems[a], recv_sem=recv_sems[a], device_id=me,
                device_id_type=_MESH)
            all_copies.wait_send()
            all_copies.wait_recv()

    res = pl.pallas_call(
        body, name=name, out_shape=tuple(pltpu.HBM(t.shape, t.dtype) for t in thru),
        in_specs=[*[_HBM] * (2 * n), *[_SEM] * (2 * n), _ANY], out_specs=tuple([_HBM] * (2 * n)),
        input_output_aliases={i: i for i in range(2 * n)},
        compiler_params=pltpu.CompilerParams(has_side_effects=_EFFECT),
    )(*thru, *sems, after)
    return res[n:]


def _adamw_gathered(name, own, parts, me, w, m, v):
    r, c = w.shape
    tr = min(r, 256)

    def body(me_ref, own_ref, p_ref, w_ref, m_ref, v_ref, g_out, d_out, m_out, v_out):
        g = own_ref[...]
        for j in range(N_DEV):
            g = g + p_ref[j].astype(F32)
        delta, m_new, v_new = _adamw(w_ref[...], g, m_ref[...], v_ref[...])
        g_out[...] = g
        d_out[...] = delta
        m_out[...] = m_new
        v_out[...] = v_new

    spec = pl.BlockSpec((tr, c), lambda i, me_ref: (i, 0))
    return pl.pallas_call(
        body, name=name,
        grid_spec=pltpu.PrefetchScalarGridSpec(
            num_scalar_prefetch=1, grid=(r // tr,),
            in_specs=[pl.BlockSpec((None, tr, c), lambda i, me_ref: (me_ref[0], i, 0)),
                      pl.BlockSpec((N_DEV, tr, c), lambda i, me_ref: (0, i, 0)), spec, spec, spec],
            out_specs=[spec] * 4),
        out_shape=[jax.ShapeDtypeStruct((r, c), F32)] * 4,
        compiler_params=_params(("parallel",)),
    )(me, own, parts, w, m, v)


def _adamw(w, g, m, v):
    m = ADAM_B1 * m + (1.0 - ADAM_B1) * g
    v = ADAM_B2 * v + (1.0 - ADAM_B2) * jnp.square(g)
    m_hat = m / (1.0 - ADAM_B1 ** ADAM_STEP)
    v_hat = v / (1.0 - ADAM_B2 ** ADAM_STEP)
    delta = -ADAM_LR * (m_hat / (jnp.sqrt(v_hat) + ADAM_EPS) + ADAM_WD * w)
    return delta, m, v


def _adamw_small(name, parts, w, m, v):
    _, r, c = parts.shape
    tr = 8

    def body(p_ref, w_ref, m_ref, v_ref, g_out, d_out, m_out, v_out):
        g = p_ref[0]
        for j in range(1, N_DEV):
            g = g + p_ref[j]
        delta, m_new, v_new = _adamw(w_ref[...], g, m_ref[...], v_ref[...])
        g_out[...] = g
        d_out[...] = delta
        m_out[...] = m_new
        v_out[...] = v_new

    spec = pl.BlockSpec((tr, c), lambda i: (i, 0))
    return pl.pallas_call(
        body, name=name, grid=(r // tr,),
        in_specs=[pl.BlockSpec((N_DEV, tr, c), lambda i: (0, i, 0)), spec, spec, spec],
        out_specs=[spec] * 4, out_shape=[jax.ShapeDtypeStruct((r, c), F32)] * 4,
        compiler_params=_params(("parallel",)),
    )(parts, w, m, v)


_WEIGHTS = ["norm1_g", "w_in", "q_norm_g", "k_norm_g", "ssm_lambda_re", "ssm_lambda_im", "ssm_log_dt", "ssm_b_re",
            "ssm_b_im", "ssm_c_re", "ssm_c_im", "ssm_d", "w_glu", "b_glu", "attn_out_g", "ssm_out_g", "w_out",
            "norm2_g", "w_mlp_in", "w_mlp_out"]
_BIG = ["w_in", "w_glu", "w_out", "w_mlp_in", "w_mlp_out"]
_SMALL = [n for n in _WEIGHTS if n not in _BIG]
_PACK_COLS = 1024


def _pack(tree, last=None):
    flat = [tree[n].reshape(-1).astype(F32) for n in _SMALL]
    size = sum(f.shape[0] for f in flat)
    rows = -(-(size + 1) // (_PACK_COLS * 8)) * 8
    pad = jnp.zeros((rows * _PACK_COLS - size - 1,), F32)
    tail = jnp.zeros((1,), F32) if last is None else last.reshape(1).astype(F32)
    return jnp.concatenate(flat + [pad, tail]).reshape(rows, _PACK_COLS)


def _unpack(buf, like):
    flat, out, off = buf.reshape(-1), {}, 0
    for n in _SMALL:
        size = like[n].size
        out[n] = flat[off:off + size].reshape(like[n].shape)
        off += size
    return out


def kernel(x, norm1_g, w_in, q_norm_g, k_norm_g, ssm_lambda_re, ssm_lambda_im, ssm_log_dt, ssm_b_re, ssm_b_im, ssm_c_re, ssm_c_im, ssm_d, w_glu, b_glu, attn_out_g, ssm_out_g, w_out, norm2_g, w_mlp_in, w_mlp_out, loss_target, m_norm1_g, m_w_in, m_q_norm_g, m_k_norm_g, m_ssm_lambda_re, m_ssm_lambda_im, m_ssm_log_dt, m_ssm_b_re, m_ssm_b_im, m_ssm_c_re, m_ssm_c_im, m_ssm_d, m_w_glu, m_b_glu, m_attn_out_g, m_ssm_out_g, m_w_out, m_norm2_g, m_w_mlp_in, m_w_mlp_out, v_norm1_g, v_w_in, v_q_norm_g, v_k_norm_g, v_ssm_lambda_re, v_ssm_lambda_im, v_ssm_log_dt, v_ssm_b_re, v_ssm_b_im, v_ssm_c_re, v_ssm_c_im, v_ssm_d, v_w_glu, v_b_glu, v_attn_out_g, v_ssm_out_g, v_w_out, v_norm2_g, v_w_mlp_in, v_w_mlp_out):
    w = dict(zip(_WEIGHTS, (norm1_g, w_in, q_norm_g, k_norm_g, ssm_lambda_re, ssm_lambda_im, ssm_log_dt, ssm_b_re, ssm_b_im, ssm_c_re, ssm_c_im, ssm_d, w_glu, b_glu, attn_out_g, ssm_out_g, w_out, norm2_g, w_mlp_in, w_mlp_out)))
    m = dict(zip(_WEIGHTS, (m_norm1_g, m_w_in, m_q_norm_g, m_k_norm_g, m_ssm_lambda_re, m_ssm_lambda_im, m_ssm_log_dt, m_ssm_b_re, m_ssm_b_im, m_ssm_c_re, m_ssm_c_im, m_ssm_d, m_w_glu, m_b_glu, m_attn_out_g, m_ssm_out_g, m_w_out, m_norm2_g, m_w_mlp_in, m_w_mlp_out)))
    v = dict(zip(_WEIGHTS, (v_norm1_g, v_w_in, v_q_norm_g, v_k_norm_g, v_ssm_lambda_re, v_ssm_lambda_im, v_ssm_log_dt, v_ssm_b_re, v_ssm_b_im, v_ssm_c_re, v_ssm_c_im, v_ssm_d, v_w_glu, v_b_glu, v_attn_out_g, v_ssm_out_g, v_w_out, v_norm2_g, v_w_mlp_in, v_w_mlp_out)))
    core = lax.axis_index("c").astype(jnp.int32).reshape(1)
    chip = (2 * lax.axis_index("x") + lax.axis_index("y")).astype(jnp.int32).reshape(1)

    me = (2 * chip + core).astype(jnp.int32)

    def landing(own=None, like=None):
        own = jnp.zeros_like(like) if own is None else own
        return lax.dynamic_update_slice(lax.empty((N_DEV, *like.shape), like.dtype), own[None], (me[0], 0, 0))

    (w_in_blocks,) = _all_gather("w_in_all_gather", [w_in.astype(BF16)])
    w_in_full = w_in_blocks.transpose(1, 0, 2).reshape(w_in.shape[0], -1)
    late = [n for n in _BIG if n != "w_in"]
    shards = [w[n].astype(BF16) for n in late]
    w_in_blocks, shards = lax.optimization_barrier((w_in_blocks, shards))
    weights_handle, weights_token = _exchange_start(
        "weights_send", shards, [landing(s, s) for s in shards], per_peer=False)

    def late_weights(after):
        got = dict(zip(late, _exchange_wait("weights_arrive", weights_handle, after)))
        return (got["w_glu"].reshape(-1, w_glu.shape[1]), got["w_out"].reshape(-1, w_out.shape[1]),
                got["w_mlp_in"].transpose(1, 0, 2).reshape(w_mlp_in.shape[0], -1),
                got["w_mlp_out"].reshape(-1, w_mlp_out.shape[1]))

    mlp = ["w_mlp_out", "w_mlp_in"]
    sent = {}

    def send_grads(name, names, own, own_b):
        blocks = lambda g, n: g.reshape(N_DEV, *w[n].shape)
        sent[name + "_own"] = [blocks(g, n) for g, n in zip(own, names)]
        srcs = [blocks(g, n) for g, n in zip(own_b, names)]
        sent[name], token = _exchange_start(name, srcs, [landing(like=s[0]) for s in srcs], per_peer=True)
        return token[0, 0]

    def mlp_grads_ready(g_out, g_out_b, g_in, g_in_b):
        return send_grads("mlp_grads_send", mlp, [g_out, g_in], [g_out_b, g_in_b])

    rest = ["w_in", "w_glu", "w_out"]

    def rest_grads_ready(own, own_b):
        return send_grads("rest_grads_send", rest, own, own_b)

    loss_local, grad_x, g_small, g_big = _local_step(
        x, loss_target, {n: w[n] for n in _SMALL}, w_in_full, late_weights, mlp_grads_ready, rest_grads_ready,
        weights_token[0, 0])

    grads, delta, new_m, new_v = {}, {}, {}, {}
    small = _pack(g_small, last=loss_local)
    small_handle, small_token = _exchange_start("small_grads_send", [small], [landing(small, small)], per_peer=False)

    for send, arrive, names in (("mlp_grads_send", "mlp_grads_arrive", mlp),
                                ("rest_grads_send", "rest_grads_arrive", rest)):
        for n, own, part in zip(names, sent[send + "_own"], _exchange_wait(arrive, sent[send], small_token)):
            grads[n], delta[n], new_m[n], new_v[n] = _adamw_gathered("adamw_" + n, own, part, me, w[n], m[n], v[n])

    shards_done = lax.optimization_barrier(tuple(new_v[n] for n in _BIG))
    (small_parts,) = _exchange_wait("small_grads_arrive", small_handle, shards_done[-1])
    packed = _adamw_small("adamw_small", small_parts, _pack(w), _pack(m), _pack(v))
    for tree, buf in zip((grads, delta, new_m, new_v), packed):
        tree.update(_unpack(buf, w))
    loss = packed[0][-1, -1]

    return (loss, grad_x, *[grads[n] for n in _WEIGHTS], *[delta[n] for n in _WEIGHTS],
            *[new_m[n] for n in _WEIGHTS], *[new_v[n] for n in _WEIGHTS])
```

```python
import functools
import math

import jax
import jax.numpy as jnp
from jax import lax
from jax.experimental import pallas as pl
from jax.experimental.pallas import tpu as pltpu

F32 = jnp.float32
BF16 = jnp.bfloat16

EPS = 1e-6
HEAD_DIM = 64
N_HEADS = 8
SB_WIDTH = 512
SSM_WIDTH = 512
SSM_GROUP = 16
SSM_GROUPS = 32
SSM_STATE = 64
QBLOCK = 128
KBLOCK = 256
N_CHUNK = 8
SSM_COLS = 4
LANES = 128
N_DEV = 8

ADAM_LR = 0.001
ADAM_B1 = 0.9
ADAM_B2 = 0.999
ADAM_EPS = 1e-08
ADAM_WD = 0.01
ADAM_STEP = 10

VMEM_LIMIT = 56 * 1024 * 1024

_NT = (((1,), (1,)), ((), ()))
_NN = (((1,), (0,)), ((), ()))
_TN = (((0,), (0,)), ((), ()))


def _dot(a, b, dims=_NN):
    return lax.dot_general(a, b, dims, preferred_element_type=F32)


def _params(sem):
    return pltpu.CompilerParams(dimension_semantics=sem, vmem_limit_bytes=VMEM_LIMIT)


def _matmul(name, a, b, *, ta=False, tb=False, extras=(), epilogue=None, out_dtypes=(F32,), sums=(),
            prologue=None, col_blocked=False, tm=1024, tn=512, tk=4096):
    M, K = (a.shape[1], a.shape[0]) if ta else a.shape
    N = b.shape[0] if tb else b.shape[1]
    tm, tn, tk = min(tm, M), min(tn, N), min(tk, K)
    assert M % tm == 0 and N % tn == 0 and K % tk == 0, (name, M, N, K)
    assert not (sums or prologue) or (tn == N and tk == K), name
    assert not prologue or not (ta or col_blocked), name
    nk = K // tk
    n_ex, n_out, n_sum = len(extras), len(out_dtypes), len(sums)
    n_pro = 1 if prologue else 0
    dims = (((0 if ta else 1,), (1 if tb else 0,)), ((), ()))

    def body(*refs):
        a_ref, b_ref = refs[0], refs[1]
        ex_refs = refs[2 + n_pro:2 + n_pro + n_ex]
        o_refs = refs[2 + n_pro + n_ex:2 + n_pro + n_ex + n_out]
        s_refs = refs[2 + n_pro + n_ex + n_out:2 + n_pro + n_ex + n_out + n_sum]
        k = pl.program_id(2)
        if prologue:
            left = prologue[0](a_ref[...], refs[2][...]).astype(BF16)
            refs[2 + n_pro + n_ex + n_out + n_sum][...] = left
        else:
            left = a_ref[...].astype(BF16)
        part = _dot(left, b_ref[...].astype(BF16), dims)

        def finish(acc):
            outs = (acc,) if epilogue is None else epilogue(acc, *[e[...] for e in ex_refs])
            for o_ref, o in zip(o_refs, outs[:n_out]):
                o_ref[...] = o.astype(o_ref.dtype)
            if n_sum:
                @pl.when(pl.program_id(0) == 0)
                def _():
                    for s_ref in s_refs:
                        s_ref[...] = jnp.zeros_like(s_ref)

                for s_ref, v in zip(s_refs, outs[n_out:]):
                    s_ref[...] += v

        if nk == 1:
            finish(part)
        else:
            acc_ref = refs[-1]

            @pl.when(k == 0)
            def _():
                acc_ref[...] = part

            @pl.when(jnp.logical_and(k > 0, k < nk - 1))
            def _():
                acc_ref[...] += part

            @pl.when(k == nk - 1)
            def _():
                finish(acc_ref[...] + part)

    a_spec = pl.BlockSpec((tk, tm), lambda i, j, k: (k, i)) if ta else pl.BlockSpec((tm, tk), lambda i, j, k: (i, k))
    b_spec = pl.BlockSpec((tn, tk), lambda i, j, k: (j, k)) if tb else pl.BlockSpec((tk, tn), lambda i, j, k: (k, j))
    ex_specs = [pl.BlockSpec((1, tn), lambda i, j, k: (0, j)) if e.shape[0] == 1 else
                pl.BlockSpec((tm, tn), lambda i, j, k: (i, j)) for e in extras]
    if col_blocked:
        out_specs = [pl.BlockSpec((None, tm, tn), lambda i, j, k: (j, i, 0)) for _ in out_dtypes]
        out_shape = [jax.ShapeDtypeStruct((N // tn, M, tn), dt) for dt in out_dtypes]
    else:
        wide = [(dt, N) if not isinstance(dt, tuple) else dt for dt in out_dtypes]
        assert all(w == N for _, w in wide) or tn == N, name
        out_specs = [pl.BlockSpec((tm, tn if w == N else w), lambda i, j, k: (i, j)) for _, w in wide]
        out_shape = [jax.ShapeDtypeStruct((M, w), dt) for dt, w in wide]
    out_specs += [pl.BlockSpec(s, lambda i, j, k: (0, 0)) for s in sums]
    out_shape += [jax.ShapeDtypeStruct(s, F32) for s in sums]
    pro_specs, pro_args = [], []
    if prologue:
        pro_specs, pro_args = [pl.BlockSpec((1, tk), lambda i, j, k: (0, 0))], [prologue[1]]
        out_specs.append(pl.BlockSpec((tm, tk), lambda i, j, k: (i, 0)))
        out_shape.append(jax.ShapeDtypeStruct((M, K), BF16))
    outs = pl.pallas_call(
        body, name=name, grid=(M // tm, N // tn, nk),
        in_specs=[a_spec, b_spec, *pro_specs, *ex_specs], out_specs=out_specs, out_shape=out_shape,
        scratch_shapes=[pltpu.VMEM((tm, tn), F32)] if nk > 1 else [],
        compiler_params=_params(("arbitrary",) * 3 if sums else ("parallel", "parallel", "arbitrary")),
    )(a, b, *pro_args, *extras)
    return outs[0] if len(outs) == 1 else outs


def _rowwise(name, fn, rows, small, outs, sums=(), tile=512):
    specs, args = [], []
    T = None
    for r in rows:
        arr, cb, w = r if isinstance(r, tuple) else (r, 0, r.shape[1])
        T = arr.shape[0]
        specs.append((w, cb))
        args.append(arr)
    tile = min(tile, T)
    assert T % tile == 0
    n_r, n_s, n_o, n_a = len(rows), len(small), len(outs), len(sums)

    def body(*refs):
        r_refs = refs[:n_r]
        s_refs = refs[n_r:n_r + n_s]
        o_refs = refs[n_r + n_s:n_r + n_s + n_o]
        a_refs = refs[n_r + n_s + n_o:]
        res = fn(*[r[...] for r in r_refs], *[s[...] for s in s_refs])
        res = res if isinstance(res, (tuple, list)) else (res,)
        for o_ref, o in zip(o_refs, res[:n_o]):
            o_ref[...] = o.astype(o_ref.dtype)

        @pl.when(pl.program_id(0) == 0)
        def _():
            for a_ref in a_refs:
                a_ref[...] = jnp.zeros_like(a_ref)

        for a_ref, v in zip(a_refs, res[n_o:]):
            a_ref[...] += v.astype(F32)

    in_specs = [pl.BlockSpec((tile, w), functools.partial(lambda i, cb: (i, cb), cb=cb)) for w, cb in specs]
    in_specs += [pl.BlockSpec(s.shape, functools.partial(lambda i, nd: (0,) * nd, nd=s.ndim)) for s in small]
    out_specs = [pl.BlockSpec((tile, w), lambda i: (i, 0)) for w, _ in outs]
    out_specs += [pl.BlockSpec(s, functools.partial(lambda i, nd: (0,) * nd, nd=len(s))) for s in sums]
    out_shape = [jax.ShapeDtypeStruct((T, w), dt) for w, dt in outs]
    out_shape += [jax.ShapeDtypeStruct(s, F32) for s in sums]
    res = pl.pallas_call(
        body, name=name, grid=(T // tile,), in_specs=in_specs, out_specs=out_specs, out_shape=out_shape,
        compiler_params=_params(("arbitrary",)),
    )(*args, *small)
    return res[0] if len(res) == 1 else res


def _rms(x, g):
    return x * lax.rsqrt(jnp.mean(x * x, axis=-1, keepdims=True) + EPS) * g


def _glu_branch(y, pre, b_glu, g_out):
    g = jax.nn.gelu(y)
    return _rms(g * jax.nn.sigmoid(pre + b_glu), g_out)


def _split_dot(x, tri_bf):
    hi = x.astype(BF16)
    lo = (x - hi.astype(F32)).astype(BF16)
    return _dot(hi, tri_bf) + _dot(lo, tri_bf)


def _softplus(z):
    return jnp.maximum(z, 0.0) + jnp.log(1.0 + jnp.exp(-jnp.abs(z)))


def _head(h):
    return slice(h * HEAD_DIM, (h + 1) * HEAD_DIM)


def _head_mean(x, seg):
    return _split_dot(x, seg) * (1.0 / HEAD_DIM)


def _qk_norm(proj, gains):
    r = lax.div(lax.broadcasted_iota(jnp.int32, (SB_WIDTH, SB_WIDTH), 0), HEAD_DIM)
    c = lax.div(lax.broadcasted_iota(jnp.int32, (SB_WIDTH, SB_WIDTH), 1), HEAD_DIM)
    seg = (r == c).astype(BF16)
    q, k, v = (proj[:, i * SB_WIDTH:(i + 1) * SB_WIDTH] for i in range(3))
    qn = q * lax.rsqrt(_head_mean(q * q, seg) + EPS) * gains[:, 0:SB_WIDTH]
    kn = k * lax.rsqrt(_head_mean(k * k, seg) + EPS) * gains[:, SB_WIDTH:2 * SB_WIDTH]
    return proj, qn, kn, v


def _qk_norm_bwd(x, g, dy, seg):
    r = lax.rsqrt(_head_mean(x * x, seg) + EPS)
    gdy = g * dy
    dx = r * gdy - x * (r * r * r) * _head_mean(gdy * x, seg)
    return dx, jnp.sum(dy * x * r, axis=0, keepdims=True)


def _split_heads(refs, scratch, L):
    def chunk(i, _):
        r = pl.ds(pl.multiple_of(i * QBLOCK, QBLOCK), QBLOCK)
        for ref, s in zip(refs, scratch):
            for h in range(2):
                s[h, r, :] = ref[r, _head(h)]
        return 0

    lax.fori_loop(0, L // QBLOCK, chunk, 0)


Q_HALVES = KBLOCK // QBLOCK
_CHAINS = [(h, r) for h in range(2) for r in range(Q_HALVES)]


def _valid(i, kb):
    row = lax.broadcasted_iota(jnp.int32, (QBLOCK, KBLOCK), 0)
    col = lax.broadcasted_iota(jnp.int32, (QBLOCK, KBLOCK), 1)
    return col + (kb * KBLOCK - i * QBLOCK) < row


def _attn_fwd(qn, kn, vb, B, L):
    n_pairs = L // KBLOCK
    n_hp = N_HEADS // 2
    nc = len(_CHAINS)

    def body(q_ref, k_ref, v_ref, o_ref, a_ref, q_s, k_s, v_s, after_s, z_s, stage_s, sems):
        _split_heads((q_ref, k_ref, v_ref), (q_s, k_s, v_s), L)
        r2 = lax.broadcasted_iota(jnp.int32, (KBLOCK, KBLOCK), 0)
        c2 = lax.broadcasted_iota(jnp.int32, (KBLOCK, KBLOCK), 1)
        after_s[...] = (r2 > c2).astype(after_s.dtype)
        g = pl.program_id(0) * n_hp + pl.program_id(1)

        def q_pair(p, _):
            rows = [pl.ds(pl.multiple_of((p * Q_HALVES + r) * QBLOCK, QBLOCK), QBLOCK) for r in range(Q_HALVES)]
            q_c = [q_s[h, rows[r], :] for h, r in _CHAINS]
            cs = range(nc)

            def scores(kb):
                rk = pl.ds(pl.multiple_of(kb * KBLOCK, KBLOCK), KBLOCK)
                return [_dot(q_c[c], k_s[_CHAINS[c][0], rk, :], _NT) for c in cs]

            def saved(kb):
                return pltpu.make_async_copy(stage_s.at[kb & 1], a_ref.at[g, p, kb], sems.at[kb & 1])

            def k_block(kb, carry, diagonal):
                rk = pl.ds(pl.multiple_of(kb * KBLOCK, KBLOCK), KBLOCK)
                if diagonal:
                    valid = [_valid(p * Q_HALVES + r, kb) for r in range(Q_HALVES)]
                    keep = lambda c, t: jnp.where(valid[_CHAINS[c][1]], t, 0.0)
                    z = scores(kb)
                else:
                    keep = lambda c, t: t
                    z = [z_s[(kb + 1) & 1, c] for c in cs]
                ahead = scores(jnp.maximum(kb - 1, 0))
                for c in cs:
                    z_s[kb & 1, c] = ahead[c]
                sp = [_softplus(z[c]) for c in cs]
                lsig = [z[c] - sp[c] for c in cs]
                lom = [keep(c, -sp[c]) for c in cs]
                tail = [_split_dot(lom[c], after_s[...]) + carry[c][0] for c in cs]
                a = [keep(c, jnp.exp(lsig[c] + tail[c])).astype(v_s.dtype) for c in cs]
                acc = [carry[c][1] + _dot(a[c], v_s[_CHAINS[c][0], rk, :]) for c in cs]
                for c in cs:
                    stage_s[kb & 1, c] = a[c]
                saved(kb).start()
                return tuple((carry[c][0] + jnp.sum(lom[c], axis=1, keepdims=True), acc[c]) for c in cs)

            def next_block(n, carry):
                kb = p - n

                @pl.when(n >= 2)
                def _():
                    saved(kb + 2).wait()

                return k_block(kb, carry, False)

            init = (jnp.zeros((QBLOCK, 1), F32), jnp.zeros((QBLOCK, HEAD_DIM), F32))
            first = k_block(p, (init,) * nc, True)
            res = lax.fori_loop(1, p + 1, next_block, first)
            saved(0).wait()

            @pl.when(p >= 1)
            def _():
                saved(1).wait()

            for r in range(Q_HALVES):
                o_ref[rows[r], :] = jnp.concatenate([res[c][1] for c in cs if _CHAINS[c][1] == r], axis=1)
            return 0

        lax.fori_loop(0, n_pairs, q_pair, 0)

    spec = pl.BlockSpec((L, LANES), lambda b, p: (b, p))
    return pl.pallas_call(
        body, name="attn_fwd", grid=(B, n_hp),
        in_specs=[spec] * 3, out_specs=[spec, _ANY],
        out_shape=[jax.ShapeDtypeStruct((B * L, SB_WIDTH), F32),
                   jax.ShapeDtypeStruct((B * n_hp, n_pairs, n_pairs, nc, QBLOCK, KBLOCK), BF16)],
        scratch_shapes=[pltpu.VMEM((2, L, HEAD_DIM), BF16)] * 3 + [pltpu.VMEM((KBLOCK, KBLOCK), BF16)]
        + [pltpu.VMEM((2, nc, QBLOCK, KBLOCK), F32), pltpu.VMEM((2, nc, QBLOCK, KBLOCK), BF16),
           pltpu.SemaphoreType.DMA((2,))],
        compiler_params=_params(("parallel", "parallel")),
    )(qn, kn, vb)


def _attn_bwd(qn, kn, vb, kept_a, d_sb, proj, gq2, gk2, B, L):
    n_pairs = L // KBLOCK
    n_hp = N_HEADS // 2
    nc = len(_CHAINS)
    slots = 3
    scale = 1.0 / math.sqrt(HEAD_DIM)

    def body(q_ref, k_ref, v_ref, do_ref, qraw_ref, kraw_ref, gq_ref, gk_ref, a_ref,
             dq_ref, dk_ref, dv_ref, dgq_ref, dgk_ref,
             q_s, k_s, v_s, qt_s, dkt_s, dvt_s, before_s, stage_s, sems):
        _split_heads((q_ref, k_ref, v_ref), (q_s, k_s, v_s), L)
        g = pl.program_id(0) * n_hp + pl.program_id(1)
        lane_head = [lax.div(lax.broadcasted_iota(jnp.int32, (LANES, LANES), d), HEAD_DIM) for d in (0, 1)]
        seg = (lane_head[0] == lane_head[1]).astype(BF16)

        def transpose_q(i, _):
            r = pl.ds(pl.multiple_of(i * QBLOCK, QBLOCK), QBLOCK)
            qt_s[:, r] = q_ref[r, :].astype(F32).T.astype(qt_s.dtype)
            return 0

        lax.fori_loop(0, L // QBLOCK, transpose_q, 0)
        dkt_s[...] = jnp.zeros_like(dkt_s)
        dvt_s[...] = jnp.zeros_like(dvt_s)
        r2 = lax.broadcasted_iota(jnp.int32, (KBLOCK, KBLOCK), 0)
        c2 = lax.broadcasted_iota(jnp.int32, (KBLOCK, KBLOCK), 1)
        before_s[...] = (r2 < c2).astype(before_s.dtype)

        def q_pair(p, dgq):
            rows = [pl.ds(pl.multiple_of((p * Q_HALVES + r) * QBLOCK, QBLOCK), QBLOCK) for r in range(Q_HALVES)]
            pair = pl.ds(pl.multiple_of(p * KBLOCK, KBLOCK), KBLOCK)
            do2 = do_ref[pair, :]
            do_t = do2.T.astype(v_s.dtype)
            cs = range(nc)
            hs = range(2)
            q_c = [q_s[h, rows[r], :] for h, r in _CHAINS]
            do_c = [do2[r * QBLOCK:(r + 1) * QBLOCK, _head(h)].astype(v_s.dtype) for h, r in _CHAINS]
            qt_h = [qt_s[_head(h), pair] for h in hs]
            dot_h = [do_t[_head(h), :] for h in hs]

            def kept(kb):
                slot = lax.rem(kb, slots)
                return pltpu.make_async_copy(a_ref.at[g, p, kb], stage_s.at[slot], sems.at[slot])

            def k_block(kb, carry, diagonal):
                rk = pl.ds(pl.multiple_of(kb * KBLOCK, KBLOCK), KBLOCK)
                if diagonal:
                    valid = [_valid(p * Q_HALVES + r, kb) for r in range(Q_HALVES)]
                    keep = lambda c, t: jnp.where(valid[_CHAINS[c][1]], t, 0.0)
                else:
                    keep = lambda c, t: t

                    @pl.when(kb + 2 <= p)
                    def _():
                        kept(kb + 2).start()

                kept(kb).wait()
                slot = lax.rem(kb, slots)
                k_b = [k_s[h, rk, :] for h in hs]
                z = [_dot(q_c[c], k_b[_CHAINS[c][0]], _NT) for c in cs]
                da = [_dot(do_c[c], v_s[_CHAINS[c][0], rk, :], _NT) for c in cs]
                a = [stage_s[slot, c] for c in cs]
                dla = [a[c].astype(F32) * da[c] for c in cs]
                for h in hs:
                    a_h = jnp.concatenate([a[c] for c in cs if _CHAINS[c][0] == h], axis=0)
                    dvt_s[_head(h), rk] += _dot(dot_h[h], a_h)
                d_lom = [carry[c][0] + _split_dot(dla[c], before_s[...]) for c in cs]
                beta = [jax.nn.sigmoid(z[c]) for c in cs]
                dz_b = [(dla[c] * (1.0 - beta[c]) - keep(c, beta[c] * d_lom[c])).astype(v_s.dtype) for c in cs]
                dq_acc = [carry[c][1] + _dot(dz_b[c], k_b[_CHAINS[c][0]]) for c in cs]
                for h in hs:
                    dz_h = jnp.concatenate([dz_b[c] for c in cs if _CHAINS[c][0] == h], axis=0)
                    dkt_s[_head(h), rk] += _dot(qt_h[h], dz_h)
                return tuple((carry[c][0] + jnp.sum(dla[c], axis=1, keepdims=True), dq_acc[c]) for c in cs)

            init = (jnp.zeros((QBLOCK, 1), F32), jnp.zeros((QBLOCK, HEAD_DIM), F32))
            kept(0).start()

            @pl.when(p >= 1)
            def _():
                kept(1).start()

            before = lax.fori_loop(0, p, lambda kb, carry: k_block(kb, carry, False), (init,) * nc)
            res = k_block(p, before, True)
            for r in range(Q_HALVES):
                d_qn = jnp.concatenate([res[c][1] for c in cs if _CHAINS[c][1] == r], axis=1) * scale
                dq, dg = _qk_norm_bwd(qraw_ref[rows[r], :], gq_ref[...], d_qn, seg)
                dq_ref[rows[r], :] = dq.astype(dq_ref.dtype)
                dgq = dgq + dg
            return dgq

        dgq = lax.fori_loop(0, n_pairs, q_pair, jnp.zeros((1, LANES), F32))

        def transpose_out(i, dgk):
            r = pl.ds(pl.multiple_of(i * QBLOCK, QBLOCK), QBLOCK)
            dk, dg = _qk_norm_bwd(kraw_ref[r, :], gk_ref[...], dkt_s[:, r].T, seg)
            dk_ref[r, :] = dk.astype(dk_ref.dtype)
            dv_ref[r, :] = dvt_s[:, r].T.astype(dv_ref.dtype)
            return dgk + dg

        dgk = lax.fori_loop(0, L // QBLOCK, transpose_out, jnp.zeros((1, LANES), F32))

        @pl.when(jnp.logical_and(pl.program_id(0) == 0, pl.program_id(1) == 0))
        def _():
            dgq_ref[...] = jnp.zeros_like(dgq_ref)
            dgk_ref[...] = jnp.zeros_like(dgk_ref)

        dgq_ref[...] += dgq[:, _head(0)] + dgq[:, _head(1)]
        dgk_ref[...] += dgk[:, _head(0)] + dgk[:, _head(1)]

    spec = pl.BlockSpec((L, LANES), lambda b, p: (b, p))
    gain = pl.BlockSpec((1, LANES), lambda b, p: (0, 0))
    gain_grad = pl.BlockSpec((1, HEAD_DIM), lambda b, p: (0, 0))
    return pl.pallas_call(
        body, name="attn_bwd", grid=(B, n_hp),
        in_specs=[spec] * 4 + [spec, pl.BlockSpec((L, LANES), lambda b, p: (b, n_hp + p)), gain, gain, _ANY],
        out_specs=[spec] * 3 + [gain_grad] * 2,
        out_shape=[jax.ShapeDtypeStruct((B * L, SB_WIDTH), BF16)] * 3 + [jax.ShapeDtypeStruct((1, HEAD_DIM), F32)] * 2,
        scratch_shapes=[pltpu.VMEM((2, L, HEAD_DIM), BF16)] * 3 + [pltpu.VMEM((LANES, L), BF16)]
        + [pltpu.VMEM((LANES, L), F32)] * 2 + [pltpu.VMEM((KBLOCK, KBLOCK), BF16)]
        + [pltpu.VMEM((slots, nc, QBLOCK, KBLOCK), BF16), pltpu.SemaphoreType.DMA((slots,))],
        compiler_params=_params(("arbitrary", "arbitrary")),
    )(qn, kn, vb, d_sb, proj, proj, gq2, gk2, kept_a)


def _ssm_discretise(lam_re, lam_im, log_dt, b_re, b_im):
    dt = jnp.exp(log_dt)
    mag = jnp.exp(lam_re * dt)
    lbr = mag * jnp.cos(lam_im * dt)
    lbi = mag * jnp.sin(lam_im * dt)
    den = lam_re * lam_re + lam_im * lam_im
    nr, ni = lbr - 1.0, lbi
    cr = (nr * lam_re + ni * lam_im) / den
    ci = (ni * lam_re - nr * lam_im) / den
    return lbr, lbi, cr * b_re - ci * b_im, cr * b_im + ci * b_re


def _ssm_prep(lam_re, lam_im, log_dt, b_re_t, b_im_t):
    def body(lr, li, ld, br, bi, o_lr, o_li, o_br, o_bi):
        res = _ssm_discretise(lr[...], li[...], ld[...], br[...], bi[...])
        for o, v in zip((o_lr, o_li, o_br, o_bi), res):
            o[...] = v

    return pl.pallas_call(
        body, name="ssm_prep",
        out_shape=[jax.ShapeDtypeStruct(lam_re.shape, F32)] * 2 + [jax.ShapeDtypeStruct(b_re_t.shape, F32)] * 2,
    )(lam_re, lam_im, log_dt, b_re_t, b_im_t)


def _ssm_prep_bwd(lam_re, lam_im, log_dt, b_re_t, b_im_t, d_lr, d_li, d_br, d_bi):
    def body(lr, li, ld, br, bi, g_lr, g_li, g_br, g_bi, o_lr, o_li, o_ld, o_br, o_bi):
        _, vjp = jax.vjp(_ssm_discretise, lr[...], li[...], ld[...], br[...], bi[...])
        res = vjp((g_lr[...], g_li[...], g_br[...], g_bi[...]))
        for o, v in zip((o_lr, o_li, o_ld, o_br, o_bi), res):
            o[...] = v

    return pl.pallas_call(
        body, name="ssm_prep_bwd",
        out_shape=[jax.ShapeDtypeStruct(lam_re.shape, F32)] * 2 + [jax.ShapeDtypeStruct(log_dt.shape, F32)]
        + [jax.ShapeDtypeStruct(b_re_t.shape, F32)] * 2,
    )(lam_re, lam_im, log_dt, b_re_t, b_im_t, d_lr, d_li, d_br, d_bi)


def _block_diag(m):
    m4 = m.reshape(SSM_COLS, 8, SSM_GROUP, SSM_STATE)
    return jnp.einsum("aghp,gk->aghkp", m4, jnp.eye(8, dtype=m.dtype)).reshape(SSM_COLS, LANES, 512)


def _block_diag_take(d):
    d6 = d.reshape(SSM_COLS, 8, SSM_GROUP, 2, 8, SSM_STATE)
    return jnp.einsum("aghrgp->raghp", d6).reshape(2, SSM_GROUPS, SSM_GROUP, SSM_STATE)


def _cmul(ar, ai, br, bi):
    return ar * br - ai * bi, ar * bi + ai * br


def _power(lr, li, n):
    assert n & (n - 1) == 0
    for _ in range(n.bit_length() - 1):
        lr, li = _cmul(lr, li, lr, li)
    return lr, li


def _ssm_fwd(u_p, w_b, lam_r, lam_i, c_m, d_skip, B, L, tj):
    J = L // N_CHUNK
    njt = J // tj
    R = tj * N_CHUNK
    H = 512

    def body(u_ref, wb_ref, lr_ref, li_ref, cm_ref, d_ref, y_ref, gel_ref, x_ref, xin_ref, bu_s, st_s, xin_s):
        ph, jt = pl.program_id(2), pl.program_id(3)
        lr, li = lr_ref[...], li_ref[...]

        @pl.when(jnp.logical_and(ph == 0, jt == 0))
        def _():
            st_s[...] = jnp.zeros_like(st_s)

        @pl.when(ph == 0)
        def _():
            bu_s[jt] = _dot(u_ref[...].astype(BF16), wb_ref[...].astype(BF16))

        def scan(store):
            def step(j, carry):
                xr, xi = carry
                r = pl.ds(pl.multiple_of(j * N_CHUNK, N_CHUNK), N_CHUNK)
                nr = lr * xr - li * xi + bu_s[jt, r, 0:H]
                ni = lr * xi + li * xr + bu_s[jt, r, H:2 * H]
                if store:
                    x_ref[r, 0:H] = nr
                    x_ref[r, H:2 * H] = ni
                return nr, ni

            xr, xi = lax.fori_loop(0, tj, step, (st_s[:, 0:H], st_s[:, H:2 * H]))
            st_s[:, 0:H] = xr
            st_s[:, H:2 * H] = xi

        @pl.when(ph == 0)
        def _():
            scan(False)

            @pl.when(jt == njt - 1)
            def _():
                pr, pi = _power(lr[0:1], li[0:1], J)
                xin_s[0:1, :] = jnp.zeros((1, 2 * H), F32)
                for c in range(1, N_CHUNK):
                    qr, qi = _cmul(pr, pi, xin_s[c - 1:c, 0:H], xin_s[c - 1:c, H:2 * H])
                    xin_s[c:c + 1, 0:H] = qr + st_s[c - 1:c, 0:H]
                    xin_s[c:c + 1, H:2 * H] = qi + st_s[c - 1:c, H:2 * H]
                xin_ref[...] = xin_s[...]
                st_s[...] = xin_s[...]

        @pl.when(ph == 1)
        def _():
            scan(True)
            y = _dot(x_ref[...].astype(BF16), cm_ref[...].astype(BF16)) + d_ref[...] * u_ref[...]
            y_ref[...] = y
            gel_ref[...] = jax.nn.gelu(y).astype(gel_ref.dtype)

    return pl.pallas_call(
        body, name="ssm_fwd", grid=(SSM_COLS, B, 2, njt),
        in_specs=[
            pl.BlockSpec((None, R, LANES), lambda i, b, ph, jt: (b, jt, i)),
            pl.BlockSpec((None, LANES, 2 * H), lambda i, b, ph, jt: (i, 0, 0)),
            pl.BlockSpec((None, N_CHUNK, H), lambda i, b, ph, jt: (i, 0, 0)),
            pl.BlockSpec((None, N_CHUNK, H), lambda i, b, ph, jt: (i, 0, 0)),
            pl.BlockSpec((None, 2 * H, LANES), lambda i, b, ph, jt: (i, 0, 0)),
            pl.BlockSpec((1, LANES), lambda i, b, ph, jt: (0, i)),
        ],
        out_specs=[
            pl.BlockSpec((None, R, LANES), lambda i, b, ph, jt: (b, jt * ph, i)),
            pl.BlockSpec((None, R, LANES), lambda i, b, ph, jt: (b, jt * ph, i)),
            pl.BlockSpec((None, R, 2 * H), lambda i, b, ph, jt: (b, jt * ph, i)),
            pl.BlockSpec((None, None, N_CHUNK, 2 * H), lambda i, b, ph, jt: (b, i, 0, 0)),
        ],
        out_shape=[
            jax.ShapeDtypeStruct((B, L, SSM_WIDTH), F32),
            jax.ShapeDtypeStruct((B, L, SSM_WIDTH), BF16),
            jax.ShapeDtypeStruct((B, L, SSM_COLS * 2 * H), F32),
            jax.ShapeDtypeStruct((B, SSM_COLS, N_CHUNK, 2 * H), F32),
        ],
        scratch_shapes=[pltpu.VMEM((njt, R, 2 * H), F32), pltpu.VMEM((N_CHUNK, 2 * H), F32),
                        pltpu.VMEM((N_CHUNK, 2 * H), F32)],
        compiler_params=_params(("arbitrary",) * 4),
    )(u_p, w_b, lam_r, lam_i, c_m, d_skip)


def _ssm_bwd(dy_p, u_p, x, xin, w_bt, lam_r, lam_i, c_mt, d_skip, B, L, tj):
    J = L // N_CHUNK
    njt = J // tj
    R = tj * N_CHUNK
    H = 512
    x4 = x.reshape(B, J, N_CHUNK, SSM_COLS * 2 * H)

    def body(dy_ref, u_ref, x_ref, xp_ref, xin_ref, wbt_ref, lr_ref, li_ref, cmt_ref, d_ref,
             du_ref, dwb_ref, dcm_ref, dlr_ref, dli_ref, dd_ref, ca_s, a_s, st_s, dl_s):
        b, ph, jt = pl.program_id(1), pl.program_id(2), pl.program_id(3)
        jr = njt - 1 - jt
        lr, li = lr_ref[...], -li_ref[...]

        @pl.when(jnp.logical_and(b == 0, jnp.logical_and(ph == 0, jt == 0)))
        def _():
            dwb_ref[...] = jnp.zeros_like(dwb_ref)
            dcm_ref[...] = jnp.zeros_like(dcm_ref)
            dlr_ref[...] = jnp.zeros_like(dlr_ref)
            dli_ref[...] = jnp.zeros_like(dli_ref)
            dd_ref[...] = jnp.zeros_like(dd_ref)
            dl_s[...] = jnp.zeros_like(dl_s)

        @pl.when(jnp.logical_and(ph == 0, jt == 0))
        def _():
            st_s[...] = jnp.zeros_like(st_s)

        @pl.when(ph == 0)
        def _():
            ca_s[jt] = _dot(dy_ref[...].astype(BF16), cmt_ref[...].astype(BF16))

        def scan(store):
            def step(n, carry):
                ar, ai = carry
                r = pl.ds(pl.multiple_of((tj - 1 - n) * N_CHUNK, N_CHUNK), N_CHUNK)
                nr = lr * ar - li * ai + ca_s[jt, r, 0:H]
                ni = lr * ai + li * ar + ca_s[jt, r, H:2 * H]
                if store:
                    a_s[r, 0:H] = nr
                    a_s[r, H:2 * H] = ni
                return nr, ni

            ar, ai = lax.fori_loop(0, tj, step, (st_s[:, 0:H], st_s[:, H:2 * H]))
            st_s[:, 0:H] = ar
            st_s[:, H:2 * H] = ai

        @pl.when(ph == 0)
        def _():
            scan(False)

            @pl.when(jt == njt - 1)
            def _():
                pr, pi = _power(lr[0:1], li[0:1], J)
                a_s[N_CHUNK - 1:N_CHUNK, :] = jnp.zeros((1, 2 * H), F32)
                for c in range(N_CHUNK - 2, -1, -1):
                    qr, qi = _cmul(pr, pi, a_s[c + 1:c + 2, 0:H], a_s[c + 1:c + 2, H:2 * H])
                    a_s[c:c + 1, 0:H] = qr + st_s[c + 1:c + 2, 0:H]
                    a_s[c:c + 1, H:2 * H] = qi + st_s[c + 1:c + 2, H:2 * H]
                st_s[...] = a_s[0:N_CHUNK, :]

        @pl.when(ph == 1)
        def _():
            scan(True)
            dy = dy_ref[...]
            u = u_ref[...]
            a_b = a_s[...].astype(BF16)
            du_ref[...] = (_dot(a_b, wbt_ref[...].astype(BF16)) + d_ref[...] * dy).astype(du_ref.dtype)
            dwb_ref[...] += _dot(u.astype(BF16), a_b, _TN)
            dcm_ref[...] += _dot(x_ref[...].astype(BF16), dy.astype(BF16), _TN)
            dd_ref[...] += jnp.sum(dy * u, axis=0, keepdims=True)

            first = jnp.where(jr == 0, xin_ref[...], xp_ref[...])
            a0r, a0i = a_s[0:N_CHUNK, 0:H], a_s[0:N_CHUNK, H:2 * H]
            acc0 = (a0r * first[:, 0:H] + a0i * first[:, H:2 * H], a0i * first[:, 0:H] - a0r * first[:, H:2 * H])

            def step(j, carry):
                sr, si = carry
                r = pl.ds(pl.multiple_of(j * N_CHUNK, N_CHUNK), N_CHUNK)
                rp = pl.ds(pl.multiple_of((j - 1) * N_CHUNK, N_CHUNK), N_CHUNK)
                ar, ai = a_s[r, 0:H], a_s[r, H:2 * H]
                xr, xi = x_ref[rp, 0:H], x_ref[rp, H:2 * H]
                return sr + ar * xr + ai * xi, si + ai * xr - ar * xi

            sr, si = lax.fori_loop(1, tj, step, acc0)
            dl_s[:, 0:H] += sr
            dl_s[:, H:2 * H] += si

            @pl.when(jnp.logical_and(b == B - 1, jt == njt - 1))
            def _():
                dlr_ref[...] = jnp.sum(dl_s[:, 0:H], axis=0, keepdims=True)
                dli_ref[...] = jnp.sum(dl_s[:, H:2 * H], axis=0, keepdims=True)
                dl_s[...] = jnp.zeros_like(dl_s)

    rev = lambda ph, jt: (njt - 1 - jt) * ph + (njt - 1) * (1 - ph)
    return pl.pallas_call(
        body, name="ssm_bwd", grid=(SSM_COLS, B, 2, njt),
        in_specs=[
            pl.BlockSpec((None, R, LANES), lambda i, b, ph, jt: (b, njt - 1 - jt, i)),
            pl.BlockSpec((None, R, LANES), lambda i, b, ph, jt: (b, njt - 1 - jt, i)),
            pl.BlockSpec((None, R, 2 * H), lambda i, b, ph, jt: (b, rev(ph, jt), i)),
            pl.BlockSpec((None, None, N_CHUNK, 2 * H),
                         lambda i, b, ph, jt: (b, jnp.maximum((njt - 1 - jt) * tj - 1, 0), 0, i)),
            pl.BlockSpec((None, None, N_CHUNK, 2 * H), lambda i, b, ph, jt: (b, i, 0, 0)),
            pl.BlockSpec((None, 2 * H, LANES), lambda i, b, ph, jt: (i, 0, 0)),
            pl.BlockSpec((None, N_CHUNK, H), lambda i, b, ph, jt: (i, 0, 0)),
            pl.BlockSpec((None, N_CHUNK, H), lambda i, b, ph, jt: (i, 0, 0)),
            pl.BlockSpec((None, LANES, 2 * H), lambda i, b, ph, jt: (i, 0, 0)),
            pl.BlockSpec((1, LANES), lambda i, b, ph, jt: (0, i)),
        ],
        out_specs=[
            pl.BlockSpec((None, R, LANES), lambda i, b, ph, jt: (b, rev(ph, jt), i)),
            pl.BlockSpec((None, LANES, 2 * H), lambda i, b, ph, jt: (i, 0, 0)),
            pl.BlockSpec((None, 2 * H, LANES), lambda i, b, ph, jt: (i, 0, 0)),
            pl.BlockSpec((None, 1, H), lambda i, b, ph, jt: (i, 0, 0)),
            pl.BlockSpec((None, 1, H), lambda i, b, ph, jt: (i, 0, 0)),
            pl.BlockSpec((1, LANES), lambda i, b, ph, jt: (0, i)),
        ],
        out_shape=[
            jax.ShapeDtypeStruct((B, L, SSM_WIDTH), BF16),
            jax.ShapeDtypeStruct((SSM_COLS, LANES, 2 * H), F32),
            jax.ShapeDtypeStruct((SSM_COLS, 2 * H, LANES), F32),
            jax.ShapeDtypeStruct((SSM_COLS, 1, H), F32),
            jax.ShapeDtypeStruct((SSM_COLS, 1, H), F32),
            jax.ShapeDtypeStruct((1, SSM_WIDTH), F32),
        ],
        scratch_shapes=[pltpu.VMEM((njt, R, 2 * H), F32), pltpu.VMEM((R, 2 * H), F32),
                        pltpu.VMEM((N_CHUNK, 2 * H), F32), pltpu.VMEM((N_CHUNK, 2 * H), F32)],
        compiler_params=_params(("arbitrary",) * 4),
    )(dy_p, u_p, x, x4, xin, w_bt, lam_r, lam_i, c_mt, d_skip)


def _to_scan_layout(t, B, L):
    C = t.shape[-1]
    return t.reshape(B, N_CHUNK, L // N_CHUNK, C).transpose(0, 2, 1, 3).reshape(B, L, C)


def _from_scan_layout(t, B, L):
    C = t.shape[-1]
    return t.reshape(B, L // N_CHUNK, N_CHUNK, C).transpose(0, 2, 1, 3).reshape(B * L, C)


def _local_step(x, target, p, w_in, late_weights, mlp_grads_ready=None, rest_grads_ready=None, order=None, *,
                ssm_tile=128):
    B, L, D = x.shape
    T = B * L
    x2 = x.reshape(T, D)
    row = lambda v: v.reshape(1, -1)
    g1, g2, ga, gs, b_glu = row(p["norm1_g"]), row(p["norm2_g"]), row(p["attn_out_g"]), row(p["ssm_out_g"]), row(p["b_glu"])
    g1_first = g1 if order is None else g1 + order
    gq8 = jnp.tile(row(p["q_norm_g"]), (1, N_HEADS))
    gk8 = jnp.tile(row(p["k_norm_g"]), (1, N_HEADS))

    G, P, Hh = SSM_GROUPS, SSM_STATE, SSM_GROUP
    lam_re3, lam_im3 = p["ssm_lambda_re"].reshape(G, 1, P), p["ssm_lambda_im"].reshape(G, 1, P)
    log_dt3 = p["ssm_log_dt"].reshape(G, 1, 1)
    b_re_t, b_im_t = p["ssm_b_re"].transpose(0, 2, 1), p["ssm_b_im"].transpose(0, 2, 1)
    lbr, lbi, bbr, bbi = _ssm_prep(lam_re3, lam_im3, log_dt3, b_re_t, b_im_t)
    w_b = jnp.concatenate([_block_diag(bbr), _block_diag(bbi)], axis=2)
    c_mt = jnp.concatenate([_block_diag(p["ssm_c_re"]), -_block_diag(p["ssm_c_im"])], axis=2)
    w_bt, c_m = w_b.transpose(0, 2, 1), c_mt.transpose(0, 2, 1)
    lam_r = jnp.broadcast_to(lbr.reshape(SSM_COLS, 1, 512), (SSM_COLS, N_CHUNK, 512))
    lam_i = jnp.broadcast_to(lbi.reshape(SSM_COLS, 1, 512), (SSM_COLS, N_CHUNK, 512))
    d_skip = p["ssm_d"].reshape(1, SSM_WIDTH)

    qk_gains = jnp.concatenate([gq8 * (1.0 / math.sqrt(HEAD_DIM)), gk8, jnp.ones((1, 2 * SB_WIDTH), F32)], axis=1)
    head = (BF16, SB_WIDTH)
    proj, qn, kn, vb, xn = _matmul("proj", x2, w_in, prologue=(_rms, g1_first), extras=[qk_gains],
                                   out_dtypes=(F32, head, head, head), epilogue=_qk_norm, tm=512, tn=w_in.shape[1])
    sb, attn_kept = _attn_fwd(qn, kn, vb, B, L)
    u_p = _to_scan_layout(proj[:, 3 * SB_WIDTH:], B, L)
    y_p, gel_p, xs, xin = _ssm_fwd(u_p, w_b, lam_r, lam_i, c_m, d_skip, B, L, ssm_tile)
    y2, gel = y_p.reshape(T, SSM_WIDTH), gel_p.reshape(T, SSM_WIDTH)
    w_glu, w_out, w_mlp_in, w_mlp_out = late_weights(gel)
    pre, ssm_n = _matmul("glu_gate", gel, w_glu, extras=[y2, b_glu, gs], out_dtypes=(F32, BF16),
                         epilogue=lambda acc, y, b, g: (acc, _glu_branch(y, acc, b, g)))
    mixed = _rowwise(
        "attn_out_norm", lambda s, n, g: jnp.concatenate([_rms(s, g).astype(BF16), n.astype(BF16)], axis=1),
        [sb, _from_scan_layout(ssm_n, B, L)], [ga], [(D, BF16)])
    def residual_and_norm(acc, res, g):
        h = acc + res
        return h, _rms(h, g)

    h1, hn = _matmul("out_proj", mixed, w_out, extras=[x2, g2], out_dtypes=(F32, BF16), tn=D,
                     epilogue=residual_and_norm)
    act, a_pre = _matmul("mlp_in", hn, w_mlp_in, out_dtypes=(BF16, BF16), tn=1024,
                         epilogue=lambda acc: (jnp.square(jnp.maximum(acc, 0.0)), acc))

    def loss_fn(acc, h, t):
        diff = acc + h - t
        part = jnp.sum(jnp.sum(diff * diff, axis=0, keepdims=True), axis=1, keepdims=True)
        d = diff * (1.0 / D)
        return d, d, part * (0.5 / D)

    d_out, d_out_b, loss = _matmul("mlp_out", act, w_mlp_out, extras=[h1, target.reshape(T, D)],
                                   out_dtypes=(F32, BF16), sums=[(1, 1)], epilogue=loss_fn, tm=512, tn=D)

    d_apre = _matmul("mlp_out_dx", d_out_b, w_mlp_out, tb=True, extras=[a_pre], out_dtypes=(BF16,), tn=1024,
                     epilogue=lambda acc, ap: (acc * (2.0 * jnp.maximum(ap.astype(F32), 0.0)),))
    both = lambda acc: (acc, acc)
    g_w_mlp_out, g_w_mlp_out_b = _matmul("mlp_out_dw", act, d_out_b, ta=True, out_dtypes=(F32, BF16), epilogue=both)
    g_w_mlp_in, g_w_mlp_in_b = _matmul("mlp_in_dw", hn, d_apre, ta=True, col_blocked=True, out_dtypes=(F32, BF16),
                                       epilogue=both, tn=w_mlp_in.shape[1] // N_DEV)
    if mlp_grads_ready is not None:
        g2 = g2 + mlp_grads_ready(g_w_mlp_out, g_w_mlp_out_b, g_w_mlp_in, g_w_mlp_in_b)

    def norm_bwd_res(dy, h, res, g):
        _, vjp = jax.vjp(_rms, h, g)
        dh, dg = vjp(dy)
        return res + dh, dg

    def norm_bwd_res2(dy, h, res, g):
        d, dg = norm_bwd_res(dy, h, res, g)
        return d, d, dg

    d_h1, d_h1_b, g_norm2 = _matmul("mlp_in_dx", d_apre, w_mlp_in, tb=True, extras=[h1, d_out, g2],
                                    out_dtypes=(F32, BF16), sums=[(1, D)], epilogue=norm_bwd_res2, tm=512, tn=D)

    d_mixed = _matmul("out_proj_dx", d_h1_b, w_out, tb=True, tn=1024)
    g_w_out, g_w_out_b = _matmul("out_proj_dw", mixed, d_h1_b, ta=True, out_dtypes=(F32, BF16), epilogue=both)

    def norm_bwd(h, dy, g):
        _, vjp = jax.vjp(_rms, h, g)
        return vjp(dy)

    d_sb, g_attn_out = _rowwise("attn_out_norm_bwd", norm_bwd, [sb, (d_mixed, 0, SB_WIDTH)], [ga],
                                [(SB_WIDTH, F32)], sums=[(1, SB_WIDTH)])
    d_ssm_n = _to_scan_layout(d_mixed[:, SB_WIDTH:], B, L).reshape(T, SSM_WIDTH)

    def glu_bwd(y, pre_, dy, bg, g):
        _, vjp = jax.vjp(_glu_branch, y, pre_, bg, g)
        d_y, d_pre, d_bg, d_g = vjp(dy)
        return d_y, d_pre, d_bg, d_g

    d_y_direct, d_pre, g_b_glu, g_ssm_out = _rowwise(
        "glu_out_bwd", glu_bwd, [y2, pre, d_ssm_n], [b_glu, gs], [(SSM_WIDTH, F32), (SSM_WIDTH, BF16)],
        sums=[(1, SSM_WIDTH), (1, SSM_WIDTH)])
    g_w_glu, g_w_glu_b = _matmul("glu_gate_dw", gel, d_pre, ta=True, out_dtypes=(F32, BF16), epilogue=both)

    def gelu_bwd(dg, y, dy0):
        _, vjp = jax.vjp(jax.nn.gelu, y)
        return (dy0 + vjp(dg)[0],)

    d_y = _matmul("glu_gate_dx", d_pre, w_glu, tb=True, extras=[y2, d_y_direct], epilogue=gelu_bwd)

    du_p, d_wb, d_cm, d_lr, d_li, g_d = _ssm_bwd(
        d_y.reshape(B, L, SSM_WIDTH), u_p, xs, xin, w_bt, lam_r, lam_i, c_mt, d_skip, B, L, ssm_tile)
    d_bb = _block_diag_take(d_wb.reshape(SSM_COLS, LANES, 2, 512))
    d_c = _block_diag_take(d_cm.transpose(0, 2, 1).reshape(SSM_COLS, LANES, 2, 512))
    g_lam_re, g_lam_im, g_log_dt, g_b_re_t, g_b_im_t = _ssm_prep_bwd(
        lam_re3, lam_im3, log_dt3, b_re_t, b_im_t,
        d_lr.reshape(G, 1, P), d_li.reshape(G, 1, P), d_bb[0], d_bb[1])
    d_q, d_k, d_v, g_q, g_k = _attn_bwd(qn, kn, vb, attn_kept, d_sb, proj, gq8[:, :LANES], gk8[:, :LANES], B, L)

    d_proj = jnp.concatenate([d_q, d_k, d_v, _from_scan_layout(du_p, B, L)], axis=1)
    g_w_in, g_w_in_b = _matmul("proj_dw", xn, d_proj, ta=True, col_blocked=True, out_dtypes=(F32, BF16),
                               epilogue=both, tn=w_in.shape[1] // N_DEV)
    if rest_grads_ready is not None:
        g1 = g1 + rest_grads_ready([g_w_in, g_w_glu, g_w_out], [g_w_in_b, g_w_glu_b, g_w_out_b])
    grad_x, g_norm1 = _matmul("proj_dx", d_proj, w_in, tb=True, extras=[x2, d_h1, g1], sums=[(1, D)],
                              epilogue=norm_bwd_res, tm=512, tn=D)

    small = {
        "norm1_g": g_norm1.reshape(-1),
        "q_norm_g": g_q.reshape(-1),
        "k_norm_g": g_k.reshape(-1),
        "ssm_lambda_re": g_lam_re.reshape(G, P),
        "ssm_lambda_im": g_lam_im.reshape(G, P),
        "ssm_log_dt": g_log_dt.reshape(G),
        "ssm_b_re": g_b_re_t.transpose(0, 2, 1),
        "ssm_b_im": g_b_im_t.transpose(0, 2, 1),
        "ssm_c_re": d_c[0],
        "ssm_c_im": -d_c[1],
        "ssm_d": g_d.reshape(G, Hh),
        "b_glu": g_b_glu.reshape(-1),
        "attn_out_g": g_attn_out.reshape(-1),
        "ssm_out_g": g_ssm_out.reshape(-1),
        "norm2_g": g_norm2.reshape(-1),
    }
    big = {"w_in": g_w_in, "w_glu": g_w_glu, "w_out": g_w_out, "w_mlp_in": g_w_mlp_in, "w_mlp_out": g_w_mlp_out}
    return loss[0, 0], grad_x.reshape(B, L, D), small, big


_ANY = pl.BlockSpec(memory_space=pl.ANY)
_MESH = pl.DeviceIdType.MESH


def _all_gather(name, shards):
    n = len(shards)

    def body(*refs):
        in_refs, out_refs = refs[:n], refs[n:2 * n]
        send_sems, recv_sems, local_sems = refs[2 * n:]
        x, y, c = lax.axis_index("x"), lax.axis_index("y"), lax.axis_index("c")
        me, sibling = (x, y, c), (x, y, 1 - c)
        chips = [(1 - x, y), (x, 1 - y), (1 - x, 1 - y)]

        def copy(a, k, block, to, src=None):
            px, py, pc = block
            rows = out_refs[a].at[4 * px + 2 * py + pc]
            return pltpu.make_async_remote_copy(
                src_ref=rows if src is None else src, dst_ref=rows, send_sem=send_sems.at[a, k],
                recv_sem=recv_sems.at[a, k], device_id=to, device_id_type=_MESH)

        mine = [pltpu.make_async_copy(in_refs[a], out_refs[a].at[4 * x + 2 * y + c], local_sems.at[a]) for a in range(n)]
        first, passed = [], []
        for a in range(n):
            mine[a].start()
            first.append(copy(a, 0, me, sibling, src=in_refs[a]))
            first += [copy(a, 1 + j, me, (*chip, c), src=in_refs[a]) for j, chip in enumerate(chips)]
        for cp in first:
            cp.start()
        for j, chip in enumerate(chips):
            for a in range(n):
                copy(a, 1 + j, (*chip, c), me).wait_recv()
                fwd = copy(a, 4 + j, (*chip, c), sibling)
                fwd.start()
                passed.append(fwd)
        for a in range(n):
            copy(a, 0, sibling, me).wait_recv()
            for j, chip in enumerate(chips):
                copy(a, 4 + j, (*chip, 1 - c), me).wait_recv()
        for cp in first + passed:
            cp.wait_send()
        for cp in mine:
            cp.wait()

    return pl.pallas_call(
        body, name=name,
        in_specs=[_ANY] * n, out_specs=[_ANY] * n,
        out_shape=[jax.ShapeDtypeStruct((N_DEV, *s.shape), s.dtype) for s in shards],
        scratch_shapes=[pltpu.SemaphoreType.DMA((n, 7)), pltpu.SemaphoreType.DMA((n, 7)), pltpu.SemaphoreType.DMA((n,))],
    )(*shards)


_HBM = pl.BlockSpec(memory_space=pltpu.HBM)
_SEM = pl.BlockSpec(memory_space=pltpu.SEMAPHORE)
_EFFECT = pltpu.SideEffectType.DATAFLOW_SIDE_EFFECTING
_FLIPS = [(dx, dy, dc) for dx in (0, 1) for dy in (0, 1) for dc in (0, 1) if (dx, dy, dc) != (0, 0, 0)]


def _exchange_start(name, srcs, lands, per_peer):
    n = len(srcs)

    def body(*refs):
        src_refs, land_refs = refs[:n], refs[n:2 * n]
        send_sems, recv_sems = refs[2 * n:3 * n], refs[3 * n:4 * n]
        token = refs[-1]
        x, y, c = lax.axis_index("x"), lax.axis_index("y"), lax.axis_index("c")
        me = 4 * x + 2 * y + c
        for dx, dy, dc in _FLIPS:
            px, py, pc = (1 - x if dx else x), (1 - y if dy else y), (1 - c if dc else c)
            for a in range(n):
                pltpu.make_async_remote_copy(
                    src_ref=src_refs[a].at[4 * px + 2 * py + pc] if per_peer else src_refs[a],
                    dst_ref=land_refs[a].at[me], send_sem=send_sems[a], recv_sem=recv_sems[a],
                    device_id=(px, py, pc), device_id_type=_MESH).start()
        token[...] = jnp.zeros_like(token)

    hbm = lambda t: pltpu.with_memory_space_constraint(t, pltpu.HBM)
    res = pl.pallas_call(
        body, name=name,
        out_shape=(*[pltpu.SemaphoreType.DMA(())] * (2 * n), *[pltpu.HBM(t.shape, t.dtype) for t in (*srcs, *lands)],
                   jax.ShapeDtypeStruct((8, LANES), F32)),
        in_specs=[_HBM] * (2 * n),
        out_specs=(*[_SEM] * (2 * n), *[_HBM] * (2 * n), pl.BlockSpec(memory_space=pltpu.VMEM)),
        input_output_aliases={i: 2 * n + i for i in range(2 * n)},
        compiler_params=pltpu.CompilerParams(has_side_effects=_EFFECT),
    )(*[hbm(t) for t in (*srcs, *lands)])
    return res[:-1], res[-1]


def _exchange_wait(name, handle, after):
    n = len(handle) // 4
    sems, thru = handle[:2 * n], handle[2 * n:]

    def body(*refs):
        land_refs = refs[n:2 * n]
        send_sems, recv_sems = refs[2 * n:3 * n], refs[3 * n:4 * n]
        me = (lax.axis_index("x"), lax.axis_index("y"), lax.axis_index("c"))
        for a in range(n):
            seven = land_refs[a].at[pl.ds(0, len(_FLIPS))]
            all_copies = pltpu.make_async_remote_copy(
                src_ref=seven, dst_ref=seven, send_sem=send_sems[a], recv_sem=recv_sems[a], device_id=me,
                device_id_type=_MESH)
            all_copies.wait_send()
            all_copies.wait_recv()

    res = pl.pallas_call(
        body, name=name, out_shape=tuple(pltpu.HBM(t.shape, t.dtype) for t in thru),
        in_specs=[*[_HBM] * (2 * n), *[_SEM] * (2 * n), _ANY], out_specs=tuple([_HBM] * (2 * n)),
        input_output_aliases={i: i for i in range(2 * n)},
        compiler_params=pltpu.CompilerParams(has_side_effects=_EFFECT),
    )(*thru, *sems, after)
    return res[n:]


def _adamw_gathered(name, own, parts, me, w, m, v):
    r, c = w.shape
    tr = min(r, 256)

    def body(me_ref, own_ref, p_ref, w_ref, m_ref, v_ref, g_out, d_out, m_out, v_out):
        g = own_ref[...]
        for j in range(N_DEV):
            g = g + p_ref[j].astype(F32)
        delta, m_new, v_new = _adamw(w_ref[...], g, m_ref[...], v_ref[...])
        g_out[...] = g
        d_out[...] = delta
        m_out[...] = m_new
        v_out[...] = v_new

    spec = pl.BlockSpec((tr, c), lambda i, me_ref: (i, 0))
    return pl.pallas_call(
        body, name=name,
        grid_spec=pltpu.PrefetchScalarGridSpec(
            num_scalar_prefetch=1, grid=(r // tr,),
            in_specs=[pl.BlockSpec((None, tr, c), lambda i, me_ref: (me_ref[0], i, 0)),
                      pl.BlockSpec((N_DEV, tr, c), lambda i, me_ref: (0, i, 0)), spec, spec, spec],
            out_specs=[spec] * 4),
        out_shape=[jax.ShapeDtypeStruct((r, c), F32)] * 4,
        compiler_params=_params(("parallel",)),
    )(me, own, parts, w, m, v)


def _adamw(w, g, m, v):
    m = ADAM_B1 * m + (1.0 - ADAM_B1) * g
    v = ADAM_B2 * v + (1.0 - ADAM_B2) * jnp.square(g)
    m_hat = m / (1.0 - ADAM_B1 ** ADAM_STEP)
    v_hat = v / (1.0 - ADAM_B2 ** ADAM_STEP)
    delta = -ADAM_LR * (m_hat / (jnp.sqrt(v_hat) + ADAM_EPS) + ADAM_WD * w)
    return delta, m, v


def _adamw_small(name, parts, w, m, v):
    _, r, c = parts.shape
    tr = 8

    def body(p_ref, w_ref, m_ref, v_ref, g_out, d_out, m_out, v_out):
        g = p_ref[0]
        for j in range(1, N_DEV):
            g = g + p_ref[j]
        delta, m_new, v_new = _adamw(w_ref[...], g, m_ref[...], v_ref[...])
        g_out[...] = g
        d_out[...] = delta
        m_out[...] = m_new
        v_out[...] = v_new

    spec = pl.BlockSpec((tr, c), lambda i: (i, 0))
    return pl.pallas_call(
        body, name=name, grid=(r // tr,),
        in_specs=[pl.BlockSpec((N_DEV, tr, c), lambda i: (0, i, 0)), spec, spec, spec],
        out_specs=[spec] * 4, out_shape=[jax.ShapeDtypeStruct((r, c), F32)] * 4,
        compiler_params=_params(("parallel",)),
    )(parts, w, m, v)


_WEIGHTS = ["norm1_g", "w_in", "q_norm_g", "k_norm_g", "ssm_lambda_re", "ssm_lambda_im", "ssm_log_dt", "ssm_b_re",
            "ssm_b_im", "ssm_c_re", "ssm_c_im", "ssm_d", "w_glu", "b_glu", "attn_out_g", "ssm_out_g", "w_out",
            "norm2_g", "w_mlp_in", "w_mlp_out"]
_BIG = ["w_in", "w_glu", "w_out", "w_mlp_in", "w_mlp_out"]
_SMALL = [n for n in _WEIGHTS if n not in _BIG]
_PACK_COLS = 1024


def _pack(tree, last=None):
    flat = [tree[n].reshape(-1).astype(F32) for n in _SMALL]
    size = sum(f.shape[0] for f in flat)
    rows = -(-(size + 1) // (_PACK_COLS * 8)) * 8
    pad = jnp.zeros((rows * _PACK_COLS - size - 1,), F32)
    tail = jnp.zeros((1,), F32) if last is None else last.reshape(1).astype(F32)
    return jnp.concatenate(flat + [pad, tail]).reshape(rows, _PACK_COLS)


def _unpack(buf, like):
    flat, out, off = buf.reshape(-1), {}, 0
    for n in _SMALL:
        size = like[n].size
        out[n] = flat[off:off + size].reshape(like[n].shape)
        off += size
    return out


def kernel(x, norm1_g, w_in, q_norm_g, k_norm_g, ssm_lambda_re, ssm_lambda_im, ssm_log_dt, ssm_b_re, ssm_b_im, ssm_c_re, ssm_c_im, ssm_d, w_glu, b_glu, attn_out_g, ssm_out_g, w_out, norm2_g, w_mlp_in, w_mlp_out, loss_target, m_norm1_g, m_w_in, m_q_norm_g, m_k_norm_g, m_ssm_lambda_re, m_ssm_lambda_im, m_ssm_log_dt, m_ssm_b_re, m_ssm_b_im, m_ssm_c_re, m_ssm_c_im, m_ssm_d, m_w_glu, m_b_glu, m_attn_out_g, m_ssm_out_g, m_w_out, m_norm2_g, m_w_mlp_in, m_w_mlp_out, v_norm1_g, v_w_in, v_q_norm_g, v_k_norm_g, v_ssm_lambda_re, v_ssm_lambda_im, v_ssm_log_dt, v_ssm_b_re, v_ssm_b_im, v_ssm_c_re, v_ssm_c_im, v_ssm_d, v_w_glu, v_b_glu, v_attn_out_g, v_ssm_out_g, v_w_out, v_norm2_g, v_w_mlp_in, v_w_mlp_out):
    w = dict(zip(_WEIGHTS, (norm1_g, w_in, q_norm_g, k_norm_g, ssm_lambda_re, ssm_lambda_im, ssm_log_dt, ssm_b_re, ssm_b_im, ssm_c_re, ssm_c_im, ssm_d, w_glu, b_glu, attn_out_g, ssm_out_g, w_out, norm2_g, w_mlp_in, w_mlp_out)))
    m = dict(zip(_WEIGHTS, (m_norm1_g, m_w_in, m_q_norm_g, m_k_norm_g, m_ssm_lambda_re, m_ssm_lambda_im, m_ssm_log_dt, m_ssm_b_re, m_ssm_b_im, m_ssm_c_re, m_ssm_c_im, m_ssm_d, m_w_glu, m_b_glu, m_attn_out_g, m_ssm_out_g, m_w_out, m_norm2_g, m_w_mlp_in, m_w_mlp_out)))
    v = dict(zip(_WEIGHTS, (v_norm1_g, v_w_in, v_q_norm_g, v_k_norm_g, v_ssm_lambda_re, v_ssm_lambda_im, v_ssm_log_dt, v_ssm_b_re, v_ssm_b_im, v_ssm_c_re, v_ssm_c_im, v_ssm_d, v_w_glu, v_b_glu, v_attn_out_g, v_ssm_out_g, v_w_out, v_norm2_g, v_w_mlp_in, v_w_mlp_out)))
    core = lax.axis_index("c").astype(jnp.int32).reshape(1)
    chip = (2 * lax.axis_index("x") + lax.axis_index("y")).astype(jnp.int32).reshape(1)

    me = (2 * chip + core).astype(jnp.int32)

    def landing(own=None, like=None):
        own = jnp.zeros_like(like) if own is None else own
        return lax.dynamic_update_slice(lax.empty((N_DEV, *like.shape), like.dtype), own[None], (me[0], 0, 0))

    (w_in_blocks,) = _all_gather("w_in_all_gather", [w_in.astype(BF16)])
    w_in_full = w_in_blocks.transpose(1, 0, 2).reshape(w_in.shape[0], -1)
    late = [n for n in _BIG if n != "w_in"]
    shards = [w[n].astype(BF16) for n in late]
    w_in_blocks, shards = lax.optimization_barrier((w_in_blocks, shards))
    weights_handle, weights_token = _exchange_start(
        "weights_send", shards, [landing(s, s) for s in shards], per_peer=False)

    def late_weights(after):
        got = dict(zip(late, _exchange_wait("weights_arrive", weights_handle, after)))
        return (got["w_glu"].reshape(-1, w_glu.shape[1]), got["w_out"].reshape(-1, w_out.shape[1]),
                got["w_mlp_in"].transpose(1, 0, 2).reshape(w_mlp_in.shape[0], -1),
                got["w_mlp_out"].reshape(-1, w_mlp_out.shape[1]))

    mlp = ["w_mlp_out", "w_mlp_in"]
    sent = {}

    def send_grads(name, names, own, own_b):
        blocks = lambda g, n: g.reshape(N_DEV, *w[n].shape)
        sent[name + "_own"] = [blocks(g, n) for g, n in zip(own, names)]
        srcs = [blocks(g, n) for g, n in zip(own_b, names)]
        sent[name], token = _exchange_start(name, srcs, [landing(like=s[0]) for s in srcs], per_peer=True)
        return token[0, 0]

    def mlp_grads_ready(g_out, g_out_b, g_in, g_in_b):
        return send_grads("mlp_grads_send", mlp, [g_out, g_in], [g_out_b, g_in_b])

    rest = ["w_in", "w_glu", "w_out"]

    def rest_grads_ready(own, own_b):
        return send_grads("rest_grads_send", rest, own, own_b)

    loss_local, grad_x, g_small, g_big = _local_step(
        x, loss_target, {n: w[n] for n in _SMALL}, w_in_full, late_weights, mlp_grads_ready, rest_grads_ready,
        weights_token[0, 0])

    grads, delta, new_m, new_v = {}, {}, {}, {}
    small = _pack(g_small, last=loss_local)
    small_handle, small_token = _exchange_start("small_grads_send", [small], [landing(small, small)], per_peer=False)

    for send, arrive, names in (("mlp_grads_send", "mlp_grads_arrive", mlp),
                                ("rest_grads_send", "rest_grads_arrive", rest)):
        for n, own, part in zip(names, sent[send + "_own"], _exchange_wait(arrive, sent[send], small_token)):
            grads[n], delta[n], new_m[n], new_v[n] = _adamw_gathered("adamw_" + n, own, part, me, w[n], m[n], v[n])

    shards_done = lax.optimization_barrier(tuple(new_v[n] for n in _BIG))
    (small_parts,) = _exchange_wait("small_grads_arrive", small_handle, shards_done[-1])
    packed = _adamw_small("adamw_small", small_parts, _pack(w), _pack(m), _pack(v))
    for tree, buf in zip((grads, delta, new_m, new_v), packed):
        tree.update(_unpack(buf, w))
    loss = packed[0][-1, -1]

    return (loss, grad_x, *[grads[n] for n in _WEIGHTS], *[delta[n] for n in _WEIGHTS],
            *[new_m[n] for n in _WEIGHTS], *[new_v[n] for n in _WEIGHTS])
```

```python
import functools
import math

import jax
import jax.numpy as jnp
from jax import lax
from jax.experimental import pallas as pl
from jax.experimental.pallas import tpu as pltpu

F32 = jnp.float32
BF16 = jnp.bfloat16

EPS = 1e-6
HEAD_DIM = 64
N_HEADS = 8
SB_WIDTH = 512
SSM_WIDTH = 512
SSM_GROUP = 16
SSM_GROUPS = 32
SSM_STATE = 64
QBLOCK = 128
KBLOCK = 256
N_CHUNK = 8
SSM_COLS = 4
LANES = 128
N_DEV = 8

ADAM_LR = 0.001
ADAM_B1 = 0.9
ADAM_B2 = 0.999
ADAM_EPS = 1e-08
ADAM_WD = 0.01
ADAM_STEP = 10

VMEM_LIMIT = 56 * 1024 * 1024

_NT = (((1,), (1,)), ((), ()))
_NN = (((1,), (0,)), ((), ()))
_TN = (((0,), (0,)), ((), ()))


def _dot(a, b, dims=_NN):
    return lax.dot_general(a, b, dims, preferred_element_type=F32)


def _params(sem):
    return pltpu.CompilerParams(dimension_semantics=sem, vmem_limit_bytes=VMEM_LIMIT)


def _matmul(name, a, b, *, ta=False, tb=False, extras=(), epilogue=None, out_dtypes=(F32,), sums=(),
            prologue=None, col_blocked=False, tm=1024, tn=512, tk=4096):
    M, K = (a.shape[1], a.shape[0]) if ta else a.shape
    N = b.shape[0] if tb else b.shape[1]
    tm, tn, tk = min(tm, M), min(tn, N), min(tk, K)
    assert M % tm == 0 and N % tn == 0 and K % tk == 0, (name, M, N, K)
    assert not (sums or prologue) or (tn == N and tk == K), name
    assert not prologue or not (ta or col_blocked), name
    nk = K // tk
    n_ex, n_out, n_sum = len(extras), len(out_dtypes), len(sums)
    n_pro = 1 if prologue else 0
    dims = (((0 if ta else 1,), (1 if tb else 0,)), ((), ()))

    def body(*refs):
        a_ref, b_ref = refs[0], refs[1]
        ex_refs = refs[2 + n_pro:2 + n_pro + n_ex]
        o_refs = refs[2 + n_pro + n_ex:2 + n_pro + n_ex + n_out]
        s_refs = refs[2 + n_pro + n_ex + n_out:2 + n_pro + n_ex + n_out + n_sum]
        k = pl.program_id(2)
        if prologue:
            left = prologue[0](a_ref[...], refs[2][...]).astype(BF16)
            refs[2 + n_pro + n_ex + n_out + n_sum][...] = left
        else:
            left = a_ref[...].astype(BF16)
        part = _dot(left, b_ref[...].astype(BF16), dims)

        def finish(acc):
            outs = (acc,) if epilogue is None else epilogue(acc, *[e[...] for e in ex_refs])
            for o_ref, o in zip(o_refs, outs[:n_out]):
                o_ref[...] = o.astype(o_ref.dtype)
            if n_sum:
                @pl.when(pl.program_id(0) == 0)
                def _():
                    for s_ref in s_refs:
                        s_ref[...] = jnp.zeros_like(s_ref)

                for s_ref, v in zip(s_refs, outs[n_out:]):
                    s_ref[...] += v

        if nk == 1:
            finish(part)
        else:
            acc_ref = refs[-1]

            @pl.when(k == 0)
            def _():
                acc_ref[...] = part

            @pl.when(jnp.logical_and(k > 0, k < nk - 1))
            def _():
                acc_ref[...] += part

            @pl.when(k == nk - 1)
            def _():
                finish(acc_ref[...] + part)

    a_spec = pl.BlockSpec((tk, tm), lambda i, j, k: (k, i)) if ta else pl.BlockSpec((tm, tk), lambda i, j, k: (i, k))
    b_spec = pl.BlockSpec((tn, tk), lambda i, j, k: (j, k)) if tb else pl.BlockSpec((tk, tn), lambda i, j, k: (k, j))
    ex_specs = [pl.BlockSpec((1, tn), lambda i, j, k: (0, j)) if e.shape[0] == 1 else
                pl.BlockSpec((tm, tn), lambda i, j, k: (i, j)) for e in extras]
    if col_blocked:
        out_specs = [pl.BlockSpec((None, tm, tn), lambda i, j, k: (j, i, 0)) for _ in out_dtypes]
        out_shape = [jax.ShapeDtypeStruct((N // tn, M, tn), dt) for dt in out_dtypes]
    else:
        wide = [(dt, N) if not isinstance(dt, tuple) else dt for dt in out_dtypes]
        assert all(w == N for _, w in wide) or tn == N, name
        out_specs = [pl.BlockSpec((tm, tn if w == N else w), lambda i, j, k: (i, j)) for _, w in wide]
        out_shape = [jax.ShapeDtypeStruct((M, w), dt) for dt, w in wide]
    out_specs += [pl.BlockSpec(s, lambda i, j, k: (0, 0)) for s in sums]
    out_shape += [jax.ShapeDtypeStruct(s, F32) for s in sums]
    pro_specs, pro_args = [], []
    if prologue:
        pro_specs, pro_args = [pl.BlockSpec((1, tk), lambda i, j, k: (0, 0))], [prologue[1]]
        out_specs.append(pl.BlockSpec((tm, tk), lambda i, j, k: (i, 0)))
        out_shape.append(jax.ShapeDtypeStruct((M, K), BF16))
    outs = pl.pallas_call(
        body, name=name, grid=(M // tm, N // tn, nk),
        in_specs=[a_spec, b_spec, *pro_specs, *ex_specs], out_specs=out_specs, out_shape=out_shape,
        scratch_shapes=[pltpu.VMEM((tm, tn), F32)] if nk > 1 else [],
        compiler_params=_params(("arbitrary",) * 3 if sums else ("parallel", "parallel", "arbitrary")),
    )(a, b, *pro_args, *extras)
    return outs[0] if len(outs) == 1 else outs


def _rowwise(name, fn, rows, small, outs, sums=(), tile=512):
    specs, args = [], []
    T = None
    for r in rows:
        arr, cb, w = r if isinstance(r, tuple) else (r, 0, r.shape[1])
        T = arr.shape[0]
        specs.append((w, cb))
        args.append(arr)
    tile = min(tile, T)
    assert T % tile == 0
    n_r, n_s, n_o, n_a = len(rows), len(small), len(outs), len(sums)

    def body(*refs):
        r_refs = refs[:n_r]
        s_refs = refs[n_r:n_r + n_s]
        o_refs = refs[n_r + n_s:n_r + n_s + n_o]
        a_refs = refs[n_r + n_s + n_o:]
        res = fn(*[r[...] for r in r_refs], *[s[...] for s in s_refs])
        res = res if isinstance(res, (tuple, list)) else (res,)
        for o_ref, o in zip(o_refs, res[:n_o]):
            o_ref[...] = o.astype(o_ref.dtype)

        @pl.when(pl.program_id(0) == 0)
        def _():
            for a_ref in a_refs:
                a_ref[...] = jnp.zeros_like(a_ref)

        for a_ref, v in zip(a_refs, res[n_o:]):
            a_ref[...] += v.astype(F32)

    in_specs = [pl.BlockSpec((tile, w), functools.partial(lambda i, cb: (i, cb), cb=cb)) for w, cb in specs]
    in_specs += [pl.BlockSpec(s.shape, functools.partial(lambda i, nd: (0,) * nd, nd=s.ndim)) for s in small]
    out_specs = [pl.BlockSpec((tile, w), lambda i: (i, 0)) for w, _ in outs]
    out_specs += [pl.BlockSpec(s, functools.partial(lambda i, nd: (0,) * nd, nd=len(s))) for s in sums]
    out_shape = [jax.ShapeDtypeStruct((T, w), dt) for w, dt in outs]
    out_shape += [jax.ShapeDtypeStruct(s, F32) for s in sums]
    res = pl.pallas_call(
        body, name=name, grid=(T // tile,), in_specs=in_specs, out_specs=out_specs, out_shape=out_shape,
        compiler_params=_params(("arbitrary",)),
    )(*args, *small)
    return res[0] if len(res) == 1 else res


def _rms(x, g):
    return x * lax.rsqrt(jnp.mean(x * x, axis=-1, keepdims=True) + EPS) * g


def _glu_branch(y, pre, b_glu, g_out):
    g = jax.nn.gelu(y)
    return _rms(g * jax.nn.sigmoid(pre + b_glu), g_out)


def _split_dot(x, tri_bf):
    hi = x.astype(BF16)
    lo = (x - hi.astype(F32)).astype(BF16)
    return _dot(hi, tri_bf) + _dot(lo, tri_bf)


def _softplus(z):
    return jnp.maximum(z, 0.0) + jnp.log(1.0 + jnp.exp(-jnp.abs(z)))


def _head(h):
    return slice(h * HEAD_DIM, (h + 1) * HEAD_DIM)


def _head_mean(x, seg):
    return _split_dot(x, seg) * (1.0 / HEAD_DIM)


def _qk_norm(proj, gains):
    r = lax.div(lax.broadcasted_iota(jnp.int32, (SB_WIDTH, SB_WIDTH), 0), HEAD_DIM)
    c = lax.div(lax.broadcasted_iota(jnp.int32, (SB_WIDTH, SB_WIDTH), 1), HEAD_DIM)
    seg = (r == c).astype(BF16)
    q, k, v = (proj[:, i * SB_WIDTH:(i + 1) * SB_WIDTH] for i in range(3))
    qn = q * lax.rsqrt(_head_mean(q * q, seg) + EPS) * gains[:, 0:SB_WIDTH]
    kn = k * lax.rsqrt(_head_mean(k * k, seg) + EPS) * gains[:, SB_WIDTH:2 * SB_WIDTH]
    return proj, qn, kn, v


def _qk_norm_bwd(x, g, dy, seg):
    r = lax.rsqrt(_head_mean(x * x, seg) + EPS)
    gdy = g * dy
    dx = r * gdy - x * (r * r * r) * _head_mean(gdy * x, seg)
    return dx, jnp.sum(dy * x * r, axis=0, keepdims=True)


def _split_heads(refs, scratch, L):
    def chunk(i, _):
        r = pl.ds(pl.multiple_of(i * QBLOCK, QBLOCK), QBLOCK)
        for ref, s in zip(refs, scratch):
            for h in range(2):
                s[h, r, :] = ref[r, _head(h)]
        return 0

    lax.fori_loop(0, L // QBLOCK, chunk, 0)


Q_HALVES = KBLOCK // QBLOCK
_CHAINS = [(h, r) for h in range(2) for r in range(Q_HALVES)]


def _valid(i, kb):
    row = lax.broadcasted_iota(jnp.int32, (QBLOCK, KBLOCK), 0)
    col = lax.broadcasted_iota(jnp.int32, (QBLOCK, KBLOCK), 1)
    return col + (kb * KBLOCK - i * QBLOCK) < row


def _attn_fwd(qn, kn, vb, B, L):
    n_pairs = L // KBLOCK
    n_hp = N_HEADS // 2
    nc = len(_CHAINS)

    def body(q_ref, k_ref, v_ref, o_ref, a_ref, q_s, k_s, v_s, after_s, z_s, stage_s, sems):
        _split_heads((q_ref, k_ref, v_ref), (q_s, k_s, v_s), L)
        r2 = lax.broadcasted_iota(jnp.int32, (KBLOCK, KBLOCK), 0)
        c2 = lax.broadcasted_iota(jnp.int32, (KBLOCK, KBLOCK), 1)
        after_s[...] = (r2 > c2).astype(after_s.dtype)
        g = pl.program_id(0) * n_hp + pl.program_id(1)

        def q_pair(p, _):
            rows = [pl.ds(pl.multiple_of((p * Q_HALVES + r) * QBLOCK, QBLOCK), QBLOCK) for r in range(Q_HALVES)]
            q_c = [q_s[h, rows[r], :] for h, r in _CHAINS]
            cs = range(nc)

            def scores(kb):
                rk = pl.ds(pl.multiple_of(kb * KBLOCK, KBLOCK), KBLOCK)
                return [_dot(q_c[c], k_s[_CHAINS[c][0], rk, :], _NT) for c in cs]

            def saved(kb):
                return pltpu.make_async_copy(stage_s.at[kb & 1], a_ref.at[g, p, kb], sems.at[kb & 1])

            def k_block(kb, carry, diagonal):
                rk = pl.ds(pl.multiple_of(kb * KBLOCK, KBLOCK), KBLOCK)
                if diagonal:
                    valid = [_valid(p * Q_HALVES + r, kb) for r in range(Q_HALVES)]
                    keep = lambda c, t: jnp.where(valid[_CHAINS[c][1]], t, 0.0)
                    z = scores(kb)
                else:
                    keep = lambda c, t: t
                    z = [z_s[(kb + 1) & 1, c] for c in cs]
                ahead = scores(jnp.maximum(kb - 1, 0))
                for c in cs:
                    z_s[kb & 1, c] = ahead[c]
                sp = [_softplus(z[c]) for c in cs]
                lsig = [z[c] - sp[c] for c in cs]
                lom = [keep(c, -sp[c]) for c in cs]
                tail = [_split_dot(lom[c], after_s[...]) + carry[c][0] for c in cs]
                a = [keep(c, jnp.exp(lsig[c] + tail[c])).astype(v_s.dtype) for c in cs]
                acc = [carry[c][1] + _dot(a[c], v_s[_CHAINS[c][0], rk, :]) for c in cs]
                for c in cs:
                    stage_s[kb & 1, c] = a[c]
                saved(kb).start()
                return tuple((carry[c][0] + jnp.sum(lom[c], axis=1, keepdims=True), acc[c]) for c in cs)

            def next_block(n, carry):
                kb = p - n

                @pl.when(n >= 2)
                def _():
                    saved(kb + 2).wait()

                return k_block(kb, carry, False)

            init = (jnp.zeros((QBLOCK, 1), F32), jnp.zeros((QBLOCK, HEAD_DIM), F32))
            first = k_block(p, (init,) * nc, True)
            res = lax.fori_loop(1, p + 1, next_block, first)
            saved(0).wait()

            @pl.when(p >= 1)
            def _():
                saved(1).wait()

            for r in range(Q_HALVES):
                o_ref[rows[r], :] = jnp.concatenate([res[c][1] for c in cs if _CHAINS[c][1] == r], axis=1)
            return 0

        lax.fori_loop(0, n_pairs, q_pair, 0)

    spec = pl.BlockSpec((L, LANES), lambda b, p: (b, p))
    return pl.pallas_call(
        body, name="attn_fwd", grid=(B, n_hp),
        in_specs=[spec] * 3, out_specs=[spec, _ANY],
        out_shape=[jax.ShapeDtypeStruct((B * L, SB_WIDTH), F32),
                   jax.ShapeDtypeStruct((B * n_hp, n_pairs, n_pairs, nc, QBLOCK, KBLOCK), BF16)],
        scratch_shapes=[pltpu.VMEM((2, L, HEAD_DIM), BF16)] * 3 + [pltpu.VMEM((KBLOCK, KBLOCK), BF16)]
        + [pltpu.VMEM((2, nc, QBLOCK, KBLOCK), F32), pltpu.VMEM((2, nc, QBLOCK, KBLOCK), BF16),
           pltpu.SemaphoreType.DMA((2,))],
        compiler_params=_params(("parallel", "parallel")),
    )(qn, kn, vb)


def _attn_bwd(qn, kn, vb, kept_a, d_sb, proj, gq2, gk2, B, L):
    n_pairs = L // KBLOCK
    n_hp = N_HEADS // 2
    nc = len(_CHAINS)
    slots = 3
    scale = 1.0 / math.sqrt(HEAD_DIM)

    def body(q_ref, k_ref, v_ref, do_ref, qraw_ref, kraw_ref, gq_ref, gk_ref, a_ref,
             dq_ref, dk_ref, dv_ref, dgq_ref, dgk_ref,
             q_s, k_s, v_s, qt_s, dkt_s, dvt_s, before_s, stage_s, sems):
        _split_heads((q_ref, k_ref, v_ref), (q_s, k_s, v_s), L)
        g = pl.program_id(0) * n_hp + pl.program_id(1)
        lane_head = [lax.div(lax.broadcasted_iota(jnp.int32, (LANES, LANES), d), HEAD_DIM) for d in (0, 1)]
        seg = (lane_head[0] == lane_head[1]).astype(BF16)

        def transpose_q(i, _):
            r = pl.ds(pl.multiple_of(i * QBLOCK, QBLOCK), QBLOCK)
            qt_s[:, r] = q_ref[r, :].astype(F32).T.astype(qt_s.dtype)
            return 0

        lax.fori_loop(0, L // QBLOCK, transpose_q, 0)
        dkt_s[...] = jnp.zeros_like(dkt_s)
        dvt_s[...] = jnp.zeros_like(dvt_s)
        r2 = lax.broadcasted_iota(jnp.int32, (KBLOCK, KBLOCK), 0)
        c2 = lax.broadcasted_iota(jnp.int32, (KBLOCK, KBLOCK), 1)
        before_s[...] = (r2 < c2).astype(before_s.dtype)

        def q_pair(p, dgq):
            rows = [pl.ds(pl.multiple_of((p * Q_HALVES + r) * QBLOCK, QBLOCK), QBLOCK) for r in range(Q_HALVES)]
            pair = pl.ds(pl.multiple_of(p * KBLOCK, KBLOCK), KBLOCK)
            do2 = do_ref[pair, :]
            do_t = do2.T.astype(v_s.dtype)
            cs = range(nc)
            hs = range(2)
            q_c = [q_s[h, rows[r], :] for h, r in _CHAINS]
            do_c = [do2[r * QBLOCK:(r + 1) * QBLOCK, _head(h)].astype(v_s.dtype) for h, r in _CHAINS]
            qt_h = [qt_s[_head(h), pair] for h in hs]
            dot_h = [do_t[_head(h), :] for h in hs]

            def kept(kb):
                slot = lax.rem(kb, slots)
                return pltpu.make_async_copy(a_ref.at[g, p, kb], stage_s.at[slot], sems.at[slot])

            def k_block(kb, carry, diagonal):
                rk = pl.ds(pl.multiple_of(kb * KBLOCK, KBLOCK), KBLOCK)
                if diagonal:
                    valid = [_valid(p * Q_HALVES + r, kb) for r in range(Q_HALVES)]
                    keep = lambda c, t: jnp.where(valid[_CHAINS[c][1]], t, 0.0)
                else:
                    keep = lambda c, t: t

                    @pl.when(kb + 2 <= p)
                    def _():
                        kept(kb + 2).start()

                kept(kb).wait()
                slot = lax.rem(kb, slots)
                k_b = [k_s[h, rk, :] for h in hs]
                z = [_dot(q_c[c], k_b[_CHAINS[c][0]], _NT) for c in cs]
                da = [_dot(do_c[c], v_s[_CHAINS[c][0], rk, :], _NT) for c in cs]
                a = [stage_s[slot, c] for c in cs]
                dla = [a[c].astype(F32) * da[c] for c in cs]
                for h in hs:
                    a_h = jnp.concatenate([a[c] for c in cs if _CHAINS[c][0] == h], axis=0)
                    dvt_s[_head(h), rk] += _dot(dot_h[h], a_h)
                d_lom = [carry[c][0] + _split_dot(dla[c], before_s[...]) for c in cs]
                beta = [jax.nn.sigmoid(z[c]) for c in cs]
                dz_b = [(dla[c] * (1.0 - beta[c]) - keep(c, beta[c] * d_lom[c])).astype(v_s.dtype) for c in cs]
                dq_acc = [carry[c][1] + _dot(dz_b[c], k_b[_CHAINS[c][0]]) for c in cs]
                for h in hs:
                    dz_h = jnp.concatenate([dz_b[c] for c in cs if _CHAINS[c][0] == h], axis=0)
                    dkt_s[_head(h), rk] += _dot(qt_h[h], dz_h)
                return tuple((carry[c][0] + jnp.sum(dla[c], axis=1, keepdims=True), dq_acc[c]) for c in cs)

            init = (jnp.zeros((QBLOCK, 1), F32), jnp.zeros((QBLOCK, HEAD_DIM), F32))
            kept(0).start()

            @pl.when(p >= 1)
            def _():
                kept(1).start()

            before = lax.fori_loop(0, p, lambda kb, carry: k_block(kb, carry, False), (init,) * nc)
            res = k_block(p, before, True)
            for r in range(Q_HALVES):
                d_qn = jnp.concatenate([res[c][1] for c in cs if _CHAINS[c][1] == r], axis=1) * scale
                dq, dg = _qk_norm_bwd(qraw_ref[rows[r], :], gq_ref[...], d_qn, seg)
                dq_ref[rows[r], :] = dq.astype(dq_ref.dtype)
                dgq = dgq + dg
            return dgq

        dgq = lax.fori_loop(0, n_pairs, q_pair, jnp.zeros((1, LANES), F32))

        def transpose_out(i, dgk):
            r = pl.ds(pl.multiple_of(i * QBLOCK, QBLOCK), QBLOCK)
            dk, dg = _qk_norm_bwd(kraw_ref[r, :], gk_ref[...], dkt_s[:, r].T, seg)
            dk_ref[r, :] = dk.astype(dk_ref.dtype)
            dv_ref[r, :] = dvt_s[:, r].T.astype(dv_ref.dtype)
            return dgk + dg

        dgk = lax.fori_loop(0, L // QBLOCK, transpose_out, jnp.zeros((1, LANES), F32))

        @pl.when(jnp.logical_and(pl.program_id(0) == 0, pl.program_id(1) == 0))
        def _():
            dgq_ref[...] = jnp.zeros_like(dgq_ref)
            dgk_ref[...] = jnp.zeros_like(dgk_ref)

        dgq_ref[...] += dgq[:, _head(0)] + dgq[:, _head(1)]
        dgk_ref[...] += dgk[:, _head(0)] + dgk[:, _head(1)]

    spec = pl.BlockSpec((L, LANES), lambda b, p: (b, p))
    gain = pl.BlockSpec((1, LANES), lambda b, p: (0, 0))
    gain_grad = pl.BlockSpec((1, HEAD_DIM), lambda b, p: (0, 0))
    return pl.pallas_call(
        body, name="attn_bwd", grid=(B, n_hp),
        in_specs=[spec] * 4 + [spec, pl.BlockSpec((L, LANES), lambda b, p: (b, n_hp + p)), gain, gain, _ANY],
        out_specs=[spec] * 3 + [gain_grad] * 2,
        out_shape=[jax.ShapeDtypeStruct((B * L, SB_WIDTH), BF16)] * 3 + [jax.ShapeDtypeStruct((1, HEAD_DIM), F32)] * 2,
        scratch_shapes=[pltpu.VMEM((2, L, HEAD_DIM), BF16)] * 3 + [pltpu.VMEM((LANES, L), BF16)]
        + [pltpu.VMEM((LANES, L), F32)] * 2 + [pltpu.VMEM((KBLOCK, KBLOCK), BF16)]
        + [pltpu.VMEM((slots, nc, QBLOCK, KBLOCK), BF16), pltpu.SemaphoreType.DMA((slots,))],
        compiler_params=_params(("arbitrary", "arbitrary")),
    )(qn, kn, vb, d_sb, proj, proj, gq2, gk2, kept_a)


def _ssm_discretise(lam_re, lam_im, log_dt, b_re, b_im):
    dt = jnp.exp(log_dt)
    mag = jnp.exp(lam_re * dt)
    lbr = mag * jnp.cos(lam_im * dt)
    lbi = mag * jnp.sin(lam_im * dt)
    den = lam_re * lam_re + lam_im * lam_im
    nr, ni = lbr - 1.0, lbi
    cr = (nr * lam_re + ni * lam_im) / den
    ci = (ni * lam_re - nr * lam_im) / den
    return lbr, lbi, cr * b_re - ci * b_im, cr * b_im + ci * b_re


def _ssm_prep(lam_re, lam_im, log_dt, b_re_t, b_im_t):
    def body(lr, li, ld, br, bi, o_lr, o_li, o_br, o_bi):
        res = _ssm_discretise(lr[...], li[...], ld[...], br[...], bi[...])
        for o, v in zip((o_lr, o_li, o_br, o_bi), res):
            o[...] = v

    return pl.pallas_call(
        body, name="ssm_prep",
        out_shape=[jax.ShapeDtypeStruct(lam_re.shape, F32)] * 2 + [jax.ShapeDtypeStruct(b_re_t.shape, F32)] * 2,
    )(lam_re, lam_im, log_dt, b_re_t, b_im_t)


def _ssm_prep_bwd(lam_re, lam_im, log_dt, b_re_t, b_im_t, d_lr, d_li, d_br, d_bi):
    def body(lr, li, ld, br, bi, g_lr, g_li, g_br, g_bi, o_lr, o_li, o_ld, o_br, o_bi):
        _, vjp = jax.vjp(_ssm_discretise, lr[...], li[...], ld[...], br[...], bi[...])
        res = vjp((g_lr[...], g_li[...], g_br[...], g_bi[...]))
        for o, v in zip((o_lr, o_li, o_ld, o_br, o_bi), res):
            o[...] = v

    return pl.pallas_call(
        body, name="ssm_prep_bwd",
        out_shape=[jax.ShapeDtypeStruct(lam_re.shape, F32)] * 2 + [jax.ShapeDtypeStruct(log_dt.shape, F32)]
        + [jax.ShapeDtypeStruct(b_re_t.shape, F32)] * 2,
    )(lam_re, lam_im, log_dt, b_re_t, b_im_t, d_lr, d_li, d_br, d_bi)


def _block_diag(m):
    m4 = m.reshape(SSM_COLS, 8, SSM_GROUP, SSM_STATE)
    return jnp.einsum("aghp,gk->aghkp", m4, jnp.eye(8, dtype=m.dtype)).reshape(SSM_COLS, LANES, 512)


def _block_diag_take(d):
    d6 = d.reshape(SSM_COLS, 8, SSM_GROUP, 2, 8, SSM_STATE)
    return jnp.einsum("aghrgp->raghp", d6).reshape(2, SSM_GROUPS, SSM_GROUP, SSM_STATE)


def _cmul(ar, ai, br, bi):
    return ar * br - ai * bi, ar * bi + ai * br


def _power(lr, li, n):
    assert n & (n - 1) == 0
    for _ in range(n.bit_length() - 1):
        lr, li = _cmul(lr, li, lr, li)
    return lr, li


def _ssm_fwd(u_p, w_b, lam_r, lam_i, c_m, d_skip, B, L, tj):
    J = L // N_CHUNK
    njt = J // tj
    R = tj * N_CHUNK
    H = 512

    def body(u_ref, wb_ref, lr_ref, li_ref, cm_ref, d_ref, y_ref, gel_ref, x_ref, xin_ref, bu_s, st_s, xin_s):
        ph, jt = pl.program_id(2), pl.program_id(3)
        lr, li = lr_ref[...], li_ref[...]

        @pl.when(jnp.logical_and(ph == 0, jt == 0))
        def _():
            st_s[...] = jnp.zeros_like(st_s)

        @pl.when(ph == 0)
        def _():
            bu_s[jt] = _dot(u_ref[...].astype(BF16), wb_ref[...].astype(BF16))

        def scan(store):
            def step(j, carry):
                xr, xi = carry
                r = pl.ds(pl.multiple_of(j * N_CHUNK, N_CHUNK), N_CHUNK)
                nr = lr * xr - li * xi + bu_s[jt, r, 0:H]
                ni = lr * xi + li * xr + bu_s[jt, r, H:2 * H]
                if store:
                    x_ref[r, 0:H] = nr
                    x_ref[r, H:2 * H] = ni
                return nr, ni

            xr, xi = lax.fori_loop(0, tj, step, (st_s[:, 0:H], st_s[:, H:2 * H]))
            st_s[:, 0:H] = xr
            st_s[:, H:2 * H] = xi

        @pl.when(ph == 0)
        def _():
            scan(False)

            @pl.when(jt == njt - 1)
            def _():
                pr, pi = _power(lr[0:1], li[0:1], J)
                xin_s[0:1, :] = jnp.zeros((1, 2 * H), F32)
                for c in range(1, N_CHUNK):
                    qr, qi = _cmul(pr, pi, xin_s[c - 1:c, 0:H], xin_s[c - 1:c, H:2 * H])
                    xin_s[c:c + 1, 0:H] = qr + st_s[c - 1:c, 0:H]
                    xin_s[c:c + 1, H:2 * H] = qi + st_s[c - 1:c, H:2 * H]
                xin_ref[...] = xin_s[...]
                st_s[...] = xin_s[...]

        @pl.when(ph == 1)
        def _():
            scan(True)
            y = _dot(x_ref[...].astype(BF16), cm_ref[...].astype(BF16)) + d_ref[...] * u_ref[...]
            y_ref[...] = y
            gel_ref[...] = jax.nn.gelu(y).astype(gel_ref.dtype)

    return pl.pallas_call(
        body, name="ssm_fwd", grid=(SSM_COLS, B, 2, njt),
        in_specs=[
            pl.BlockSpec((None, R, LANES), lambda i, b, ph, jt: (b, jt, i)),
            pl.BlockSpec((None, LANES, 2 * H), lambda i, b, ph, jt: (i, 0, 0)),
            pl.BlockSpec((None, N_CHUNK, H), lambda i, b, ph, jt: (i, 0, 0)),
            pl.BlockSpec((None, N_CHUNK, H), lambda i, b, ph, jt: (i, 0, 0)),
            pl.BlockSpec((None, 2 * H, LANES), lambda i, b, ph, jt: (i, 0, 0)),
            pl.BlockSpec((1, LANES), lambda i, b, ph, jt: (0, i)),
        ],
        out_specs=[
            pl.BlockSpec((None, R, LANES), lambda i, b, ph, jt: (b, jt * ph, i)),
            pl.BlockSpec((None, R, LANES), lambda i, b, ph, jt: (b, jt * ph, i)),
            pl.BlockSpec((None, R, 2 * H), lambda i, b, ph, jt: (b, jt * ph, i)),
            pl.BlockSpec((None, None, N_CHUNK, 2 * H), lambda i, b, ph, jt: (b, i, 0, 0)),
        ],
        out_shape=[
            jax.ShapeDtypeStruct((B, L, SSM_WIDTH), F32),
            jax.ShapeDtypeStruct((B, L, SSM_WIDTH), BF16),
            jax.ShapeDtypeStruct((B, L, SSM_COLS * 2 * H), F32),
            jax.ShapeDtypeStruct((B, SSM_COLS, N_CHUNK, 2 * H), F32),
        ],
        scratch_shapes=[pltpu.VMEM((njt, R, 2 * H), F32), pltpu.VMEM((N_CHUNK, 2 * H), F32),
                        pltpu.VMEM((N_CHUNK, 2 * H), F32)],
        compiler_params=_params(("arbitrary",) * 4),
    )(u_p, w_b, lam_r, lam_i, c_m, d_skip)


def _ssm_bwd(dy_p, u_p, x, xin, w_bt, lam_r, lam_i, c_mt, d_skip, B, L, tj):
    J = L // N_CHUNK
    njt = J // tj
    R = tj * N_CHUNK
    H = 512
    x4 = x.reshape(B, J, N_CHUNK, SSM_COLS * 2 * H)

    def body(dy_ref, u_ref, x_ref, xp_ref, xin_ref, wbt_ref, lr_ref, li_ref, cmt_ref, d_ref,
             du_ref, dwb_ref, dcm_ref, dlr_ref, dli_ref, dd_ref, ca_s, a_s, st_s, dl_s):
        b, ph, jt = pl.program_id(1), pl.program_id(2), pl.program_id(3)
        jr = njt - 1 - jt
        lr, li = lr_ref[...], -li_ref[...]

        @pl.when(jnp.logical_and(b == 0, jnp.logical_and(ph == 0, jt == 0)))
        def _():
            dwb_ref[...] = jnp.zeros_like(dwb_ref)
            dcm_ref[...] = jnp.zeros_like(dcm_ref)
            dlr_ref[...] = jnp.zeros_like(dlr_ref)
            dli_ref[...] = jnp.zeros_like(dli_ref)
            dd_ref[...] = jnp.zeros_like(dd_ref)
            dl_s[...] = jnp.zeros_like(dl_s)

        @pl.when(jnp.logical_and(ph == 0, jt == 0))
        def _():
            st_s[...] = jnp.zeros_like(st_s)

        @pl.when(ph == 0)
        def _():
            ca_s[jt] = _dot(dy_ref[...].astype(BF16), cmt_ref[...].astype(BF16))

        def scan(store):
            def step(n, carry):
                ar, ai = carry
                r = pl.ds(pl.multiple_of((tj - 1 - n) * N_CHUNK, N_CHUNK), N_CHUNK)
                nr = lr * ar - li * ai + ca_s[jt, r, 0:H]
                ni = lr * ai + li * ar + ca_s[jt, r, H:2 * H]
                if store:
                    a_s[r, 0:H] = nr
                    a_s[r, H:2 * H] = ni
                return nr, ni

            ar, ai = lax.fori_loop(0, tj, step, (st_s[:, 0:H], st_s[:, H:2 * H]))
            st_s[:, 0:H] = ar
            st_s[:, H:2 * H] = ai

        @pl.when(ph == 0)
        def _():
            scan(False)

            @pl.when(jt == njt - 1)
            def _():
                pr, pi = _power(lr[0:1], li[0:1], J)
                a_s[N_CHUNK - 1:N_CHUNK, :] = jnp.zeros((1, 2 * H), F32)
                for c in range(N_CHUNK - 2, -1, -1):
                    qr, qi = _cmul(pr, pi, a_s[c + 1:c + 2, 0:H], a_s[c + 1:c + 2, H:2 * H])
                    a_s[c:c + 1, 0:H] = qr + st_s[c + 1:c + 2, 0:H]
                    a_s[c:c + 1, H:2 * H] = qi + st_s[c + 1:c + 2, H:2 * H]
                st_s[...] = a_s[0:N_CHUNK, :]

        @pl.when(ph == 1)
        def _():
            scan(True)
            dy = dy_ref[...]
            u = u_ref[...]
            a_b = a_s[...].astype(BF16)
            du_ref[...] = (_dot(a_b, wbt_ref[...].astype(BF16)) + d_ref[...] * dy).astype(du_ref.dtype)
            dwb_ref[...] += _dot(u.astype(BF16), a_b, _TN)
            dcm_ref[...] += _dot(x_ref[...].astype(BF16), dy.astype(BF16), _TN)
            dd_ref[...] += jnp.sum(dy * u, axis=0, keepdims=True)

            first = jnp.where(jr == 0, xin_ref[...], xp_ref[...])
            a0r, a0i = a_s[0:N_CHUNK, 0:H], a_s[0:N_CHUNK, H:2 * H]
            acc0 = (a0r * first[:, 0:H] + a0i * first[:, H:2 * H], a0i * first[:, 0:H] - a0r * first[:, H:2 * H])

            def step(j, carry):
                sr, si = carry
                r = pl.ds(pl.multiple_of(j * N_CHUNK, N_CHUNK), N_CHUNK)
                rp = pl.ds(pl.multiple_of((j - 1) * N_CHUNK, N_CHUNK), N_CHUNK)
                ar, ai = a_s[r, 0:H], a_s[r, H:2 * H]
                xr, xi = x_ref[rp, 0:H], x_ref[rp, H:2 * H]
                return sr + ar * xr + ai * xi, si + ai * xr - ar * xi

            sr, si = lax.fori_loop(1, tj, step, acc0)
            dl_s[:, 0:H] += sr
            dl_s[:, H:2 * H] += si

            @pl.when(jnp.logical_and(b == B - 1, jt == njt - 1))
            def _():
                dlr_ref[...] = jnp.sum(dl_s[:, 0:H], axis=0, keepdims=True)
                dli_ref[...] = jnp.sum(dl_s[:, H:2 * H], axis=0, keepdims=True)
                dl_s[...] = jnp.zeros_like(dl_s)

    rev = lambda ph, jt: (njt - 1 - jt) * ph + (njt - 1) * (1 - ph)
    return pl.pallas_call(
        body, name="ssm_bwd", grid=(SSM_COLS, B, 2, njt),
        in_specs=[
            pl.BlockSpec((None, R, LANES), lambda i, b, ph, jt: (b, njt - 1 - jt, i)),
            pl.BlockSpec((None, R, LANES), lambda i, b, ph, jt: (b, njt - 1 - jt, i)),
            pl.BlockSpec((None, R, 2 * H), lambda i, b, ph, jt: (b, rev(ph, jt), i)),
            pl.BlockSpec((None, None, N_CHUNK, 2 * H),
                         lambda i, b, ph, jt: (b, jnp.maximum((njt - 1 - jt) * tj - 1, 0), 0, i)),
            pl.BlockSpec((None, None, N_CHUNK, 2 * H), lambda i, b, ph, jt: (b, i, 0, 0)),
            pl.BlockSpec((None, 2 * H, LANES), lambda i, b, ph, jt: (i, 0, 0)),
            pl.BlockSpec((None, N_CHUNK, H), lambda i, b, ph, jt: (i, 0, 0)),
            pl.BlockSpec((None, N_CHUNK, H), lambda i, b, ph, jt: (i, 0, 0)),
            pl.BlockSpec((None, LANES, 2 * H), lambda i, b, ph, jt: (i, 0, 0)),
            pl.BlockSpec((1, LANES), lambda i, b, ph, jt: (0, i)),
        ],
        out_specs=[
            pl.BlockSpec((None, R, LANES), lambda i, b, ph, jt: (b, rev(ph, jt), i)),
            pl.BlockSpec((None, LANES, 2 * H), lambda i, b, ph, jt: (i, 0, 0)),
            pl.BlockSpec((None, 2 * H, LANES), lambda i, b, ph, jt: (i, 0, 0)),
            pl.BlockSpec((None, 1, H), lambda i, b, ph, jt: (i, 0, 0)),
            pl.BlockSpec((None, 1, H), lambda i, b, ph, jt: (i, 0, 0)),
            pl.BlockSpec((1, LANES), lambda i, b, ph, jt: (0, i)),
        ],
        out_shape=[
            jax.ShapeDtypeStruct((B, L, SSM_WIDTH), BF16),
            jax.ShapeDtypeStruct((SSM_COLS, LANES, 2 * H), F32),
            jax.ShapeDtypeStruct((SSM_COLS, 2 * H, LANES), F32),
            jax.ShapeDtypeStruct((SSM_COLS, 1, H), F32),
            jax.ShapeDtypeStruct((SSM_COLS, 1, H), F32),
            jax.ShapeDtypeStruct((1, SSM_WIDTH), F32),
        ],
        scratch_shapes=[pltpu.VMEM((njt, R, 2 * H), F32), pltpu.VMEM((R, 2 * H), F32),
                        pltpu.VMEM((N_CHUNK, 2 * H), F32), pltpu.VMEM((N_CHUNK, 2 * H), F32)],
        compiler_params=_params(("arbitrary",) * 4),
    )(dy_p, u_p, x, x4, xin, w_bt, lam_r, lam_i, c_mt, d_skip)


def _to_scan_layout(t, B, L):
    C = t.shape[-1]
    return t.reshape(B, N_CHUNK, L // N_CHUNK, C).transpose(0, 2, 1, 3).reshape(B, L, C)


def _from_scan_layout(t, B, L):
    C = t.shape[-1]
    return t.reshape(B, L // N_CHUNK, N_CHUNK, C).transpose(0, 2, 1, 3).reshape(B * L, C)


def _local_step(x, target, p, w_in, late_weights, mlp_grads_ready=None, rest_grads_ready=None, order=None, *,
                ssm_tile=128):
    B, L, D = x.shape
    T = B * L
    x2 = x.reshape(T, D)
    row = lambda v: v.reshape(1, -1)
    g1, g2, ga, gs, b_glu = row(p["norm1_g"]), row(p["norm2_g"]), row(p["attn_out_g"]), row(p["ssm_out_g"]), row(p["b_glu"])
    g1_first = g1 if order is None else g1 + order
    gq8 = jnp.tile(row(p["q_norm_g"]), (1, N_HEADS))
    gk8 = jnp.tile(row(p["k_norm_g"]), (1, N_HEADS))

    G, P, Hh = SSM_GROUPS, SSM_STATE, SSM_GROUP
    lam_re3, lam_im3 = p["ssm_lambda_re"].reshape(G, 1, P), p["ssm_lambda_im"].reshape(G, 1, P)
    log_dt3 = p["ssm_log_dt"].reshape(G, 1, 1)
    b_re_t, b_im_t = p["ssm_b_re"].transpose(0, 2, 1), p["ssm_b_im"].transpose(0, 2, 1)
    lbr, lbi, bbr, bbi = _ssm_prep(lam_re3, lam_im3, log_dt3, b_re_t, b_im_t)
    w_b = jnp.concatenate([_block_diag(bbr), _block_diag(bbi)], axis=2)
    c_mt = jnp.concatenate([_block_diag(p["ssm_c_re"]), -_block_diag(p["ssm_c_im"])], axis=2)
    w_bt, c_m = w_b.transpose(0, 2, 1), c_mt.transpose(0, 2, 1)
    lam_r = jnp.broadcast_to(lbr.reshape(SSM_COLS, 1, 512), (SSM_COLS, N_CHUNK, 512))
    lam_i = jnp.broadcast_to(lbi.reshape(SSM_COLS, 1, 512), (SSM_COLS, N_CHUNK, 512))
    d_skip = p["ssm_d"].reshape(1, SSM_WIDTH)

    qk_gains = jnp.concatenate([gq8 * (1.0 / math.sqrt(HEAD_DIM)), gk8, jnp.ones((1, 2 * SB_WIDTH), F32)], axis=1)
    head = (BF16, SB_WIDTH)
    proj, qn, kn, vb, xn = _matmul("proj", x2, w_in, prologue=(_rms, g1_first), extras=[qk_gains],
                                   out_dtypes=(F32, head, head, head), epilogue=_qk_norm, tm=512, tn=w_in.shape[1])
    sb, attn_kept = _attn_fwd(qn, kn, vb, B, L)
    u_p = _to_scan_layout(proj[:, 3 * SB_WIDTH:], B, L)
    y_p, gel_p, xs, xin = _ssm_fwd(u_p, w_b, lam_r, lam_i, c_m, d_skip, B, L, ssm_tile)
    y2, gel = y_p.reshape(T, SSM_WIDTH), gel_p.reshape(T, SSM_WIDTH)
    gel, sb = lax.optimization_barrier((gel, sb))
    w_glu, w_out, w_mlp_in, w_mlp_out = late_weights(gel)
    pre, ssm_n = _matmul("glu_gate", gel, w_glu, extras=[y2, b_glu, gs], out_dtypes=(F32, BF16),
                         epilogue=lambda acc, y, b, g: (acc, _glu_branch(y, acc, b, g)))
    mixed = _rowwise(
        "attn_out_norm", lambda s, n, g: jnp.concatenate([_rms(s, g).astype(BF16), n.astype(BF16)], axis=1),
        [sb, _from_scan_layout(ssm_n, B, L)], [ga], [(D, BF16)])
    def residual_and_norm(acc, res, g):
        h = acc + res
        return h, _rms(h, g)

    h1, hn = _matmul("out_proj", mixed, w_out, extras=[x2, g2], out_dtypes=(F32, BF16), tn=D,
                     epilogue=residual_and_norm)
    act, a_pre = _matmul("mlp_in", hn, w_mlp_in, out_dtypes=(BF16, BF16), tn=1024,
                         epilogue=lambda acc: (jnp.square(jnp.maximum(acc, 0.0)), acc))

    def loss_fn(acc, h, t):
        diff = acc + h - t
        part = jnp.sum(jnp.sum(diff * diff, axis=0, keepdims=True), axis=1, keepdims=True)
        d = diff * (1.0 / D)
        return d, d, part * (0.5 / D)

    d_out, d_out_b, loss = _matmul("mlp_out", act, w_mlp_out, extras=[h1, target.reshape(T, D)],
                                   out_dtypes=(F32, BF16), sums=[(1, 1)], epilogue=loss_fn, tm=512, tn=D)

    d_apre = _matmul("mlp_out_dx", d_out_b, w_mlp_out, tb=True, extras=[a_pre], out_dtypes=(BF16,), tn=1024,
                     epilogue=lambda acc, ap: (acc * (2.0 * jnp.maximum(ap.astype(F32), 0.0)),))
    both = lambda acc: (acc, acc)
    g_w_mlp_out, g_w_mlp_out_b = _matmul("mlp_out_dw", act, d_out_b, ta=True, out_dtypes=(F32, BF16), epilogue=both)
    g_w_mlp_in, g_w_mlp_in_b = _matmul("mlp_in_dw", hn, d_apre, ta=True, col_blocked=True, out_dtypes=(F32, BF16),
                                       epilogue=both, tn=w_mlp_in.shape[1] // N_DEV)
    if mlp_grads_ready is not None:
        g2 = g2 + mlp_grads_ready(g_w_mlp_out, g_w_mlp_out_b, g_w_mlp_in, g_w_mlp_in_b)

    def norm_bwd_res(dy, h, res, g):
        _, vjp = jax.vjp(_rms, h, g)
        dh, dg = vjp(dy)
        return res + dh, dg

    def norm_bwd_res2(dy, h, res, g):
        d, dg = norm_bwd_res(dy, h, res, g)
        return d, d, dg

    d_h1, d_h1_b, g_norm2 = _matmul("mlp_in_dx", d_apre, w_mlp_in, tb=True, extras=[h1, d_out, g2],
                                    out_dtypes=(F32, BF16), sums=[(1, D)], epilogue=norm_bwd_res2, tm=512, tn=D)

    d_mixed = _matmul("out_proj_dx", d_h1_b, w_out, tb=True, tn=1024)
    g_w_out, g_w_out_b = _matmul("out_proj_dw", mixed, d_h1_b, ta=True, out_dtypes=(F32, BF16), epilogue=both)

    def norm_bwd(h, dy, g):
        _, vjp = jax.vjp(_rms, h, g)
        return vjp(dy)

    d_sb, g_attn_out = _rowwise("attn_out_norm_bwd", norm_bwd, [sb, (d_mixed, 0, SB_WIDTH)], [ga],
                                [(SB_WIDTH, F32)], sums=[(1, SB_WIDTH)])
    d_ssm_n = _to_scan_layout(d_mixed[:, SB_WIDTH:], B, L).reshape(T, SSM_WIDTH)

    def glu_bwd(y, pre_, dy, bg, g):
        _, vjp = jax.vjp(_glu_branch, y, pre_, bg, g)
        d_y, d_pre, d_bg, d_g = vjp(dy)
        return d_y, d_pre, d_bg, d_g

    d_y_direct, d_pre, g_b_glu, g_ssm_out = _rowwise(
        "glu_out_bwd", glu_bwd, [y2, pre, d_ssm_n], [b_glu, gs], [(SSM_WIDTH, F32), (SSM_WIDTH, BF16)],
        sums=[(1, SSM_WIDTH), (1, SSM_WIDTH)])
    g_w_glu, g_w_glu_b = _matmul("glu_gate_dw", gel, d_pre, ta=True, out_dtypes=(F32, BF16), epilogue=both)

    def gelu_bwd(dg, y, dy0):
        _, vjp = jax.vjp(jax.nn.gelu, y)
        return (dy0 + vjp(dg)[0],)

    d_y = _matmul("glu_gate_dx", d_pre, w_glu, tb=True, extras=[y2, d_y_direct], epilogue=gelu_bwd)

    du_p, d_wb, d_cm, d_lr, d_li, g_d = _ssm_bwd(
        d_y.reshape(B, L, SSM_WIDTH), u_p, xs, xin, w_bt, lam_r, lam_i, c_mt, d_skip, B, L, ssm_tile)
    d_bb = _block_diag_take(d_wb.reshape(SSM_COLS, LANES, 2, 512))
    d_c = _block_diag_take(d_cm.transpose(0, 2, 1).reshape(SSM_COLS, LANES, 2, 512))
    g_lam_re, g_lam_im, g_log_dt, g_b_re_t, g_b_im_t = _ssm_prep_bwd(
        lam_re3, lam_im3, log_dt3, b_re_t, b_im_t,
        d_lr.reshape(G, 1, P), d_li.reshape(G, 1, P), d_bb[0], d_bb[1])
    d_q, d_k, d_v, g_q, g_k = _attn_bwd(qn, kn, vb, attn_kept, d_sb, proj, gq8[:, :LANES], gk8[:, :LANES], B, L)

    d_proj = jnp.concatenate([d_q, d_k, d_v, _from_scan_layout(du_p, B, L)], axis=1)
    g_w_in, g_w_in_b = _matmul("proj_dw", xn, d_proj, ta=True, col_blocked=True, out_dtypes=(F32, BF16),
                               epilogue=both, tn=w_in.shape[1] // N_DEV)
    if rest_grads_ready is not None:
        g1 = g1 + rest_grads_ready([g_w_in, g_w_glu, g_w_out], [g_w_in_b, g_w_glu_b, g_w_out_b])
    grad_x, g_norm1 = _matmul("proj_dx", d_proj, w_in, tb=True, extras=[x2, d_h1, g1], sums=[(1, D)],
                              epilogue=norm_bwd_res, tm=512, tn=D)

    small = {
        "norm1_g": g_norm1.reshape(-1),
        "q_norm_g": g_q.reshape(-1),
        "k_norm_g": g_k.reshape(-1),
        "ssm_lambda_re": g_lam_re.reshape(G, P),
        "ssm_lambda_im": g_lam_im.reshape(G, P),
        "ssm_log_dt": g_log_dt.reshape(G),
        "ssm_b_re": g_b_re_t.transpose(0, 2, 1),
        "ssm_b_im": g_b_im_t.transpose(0, 2, 1),
        "ssm_c_re": d_c[0],
        "ssm_c_im": -d_c[1],
        "ssm_d": g_d.reshape(G, Hh),
        "b_glu": g_b_glu.reshape(-1),
        "attn_out_g": g_attn_out.reshape(-1),
        "ssm_out_g": g_ssm_out.reshape(-1),
        "norm2_g": g_norm2.reshape(-1),
    }
    big = {"w_in": g_w_in, "w_glu": g_w_glu, "w_out": g_w_out, "w_mlp_in": g_w_mlp_in, "w_mlp_out": g_w_mlp_out}
    return loss[0, 0], grad_x.reshape(B, L, D), small, big


_ANY = pl.BlockSpec(memory_space=pl.ANY)
_MESH = pl.DeviceIdType.MESH


def _all_gather(name, shards):
    n = len(shards)

    def body(*refs):
        in_refs, out_refs = refs[:n], refs[n:2 * n]
        send_sems, recv_sems, local_sems = refs[2 * n:]
        x, y, c = lax.axis_index("x"), lax.axis_index("y"), lax.axis_index("c")
        me, sibling = (x, y, c), (x, y, 1 - c)
        chips = [(1 - x, y), (x, 1 - y), (1 - x, 1 - y)]

        def copy(a, k, block, to, src=None):
            px, py, pc = block
            rows = out_refs[a].at[4 * px + 2 * py + pc]
            return pltpu.make_async_remote_copy(
                src_ref=rows if src is None else src, dst_ref=rows, send_sem=send_sems.at[a, k],
                recv_sem=recv_sems.at[a, k], device_id=to, device_id_type=_MESH)

        mine = [pltpu.make_async_copy(in_refs[a], out_refs[a].at[4 * x + 2 * y + c], local_sems.at[a]) for a in range(n)]
        first, passed = [], []
        for a in range(n):
            mine[a].start()
            first.append(copy(a, 0, me, sibling, src=in_refs[a]))
            first += [copy(a, 1 + j, me, (*chip, c), src=in_refs[a]) for j, chip in enumerate(chips)]
        for cp in first:
            cp.start()
        for j, chip in enumerate(chips):
            for a in range(n):
                copy(a, 1 + j, (*chip, c), me).wait_recv()
                fwd = copy(a, 4 + j, (*chip, c), sibling)
                fwd.start()
                passed.append(fwd)
        for a in range(n):
            copy(a, 0, sibling, me).wait_recv()
            for j, chip in enumerate(chips):
                copy(a, 4 + j, (*chip, 1 - c), me).wait_recv()
        for cp in first + passed:
            cp.wait_send()
        for cp in mine:
            cp.wait()

    return pl.pallas_call(
        body, name=name,
        in_specs=[_ANY] * n, out_specs=[_ANY] * n,
        out_shape=[jax.ShapeDtypeStruct((N_DEV, *s.shape), s.dtype) for s in shards],
        scratch_shapes=[pltpu.SemaphoreType.DMA((n, 7)), pltpu.SemaphoreType.DMA((n, 7)), pltpu.SemaphoreType.DMA((n,))],
    )(*shards)


_HBM = pl.BlockSpec(memory_space=pltpu.HBM)
_SEM = pl.BlockSpec(memory_space=pltpu.SEMAPHORE)
_EFFECT = pltpu.SideEffectType.DATAFLOW_SIDE_EFFECTING
_FLIPS = [(dx, dy, dc) for dx in (0, 1) for dy in (0, 1) for dc in (0, 1) if (dx, dy, dc) != (0, 0, 0)]


def _exchange_start(name, srcs, lands, per_peer):
    n = len(srcs)

    def body(*refs):
        src_refs, land_refs = refs[:n], refs[n:2 * n]
        send_sems, recv_sems = refs[2 * n:3 * n], refs[3 * n:4 * n]
        token = refs[-1]
        x, y, c = lax.axis_index("x"), lax.axis_index("y"), lax.axis_index("c")
        me = 4 * x + 2 * y + c
        for dx, dy, dc in _FLIPS:
            px, py, pc = (1 - x if dx else x), (1 - y if dy else y), (1 - c if dc else c)
            for a in range(n):
                pltpu.make_async_remote_copy(
                    src_ref=src_refs[a].at[4 * px + 2 * py + pc] if per_peer else src_refs[a],
                    dst_ref=land_refs[a].at[me], send_sem=send_sems[a], recv_sem=recv_sems[a],
                    device_id=(px, py, pc), device_id_type=_MESH).start()
        token[...] = jnp.zeros_like(token)

    hbm = lambda t: pltpu.with_memory_space_constraint(t, pltpu.HBM)
    res = pl.pallas_call(
        body, name=name,
        out_shape=(*[pltpu.SemaphoreType.DMA(())] * (2 * n), *[pltpu.HBM(t.shape, t.dtype) for t in (*srcs, *lands)],
                   jax.ShapeDtypeStruct((8, LANES), F32)),
        in_specs=[_HBM] * (2 * n),
        out_specs=(*[_SEM] * (2 * n), *[_HBM] * (2 * n), pl.BlockSpec(memory_space=pltpu.VMEM)),
        input_output_aliases={i: 2 * n + i for i in range(2 * n)},
        compiler_params=pltpu.CompilerParams(has_side_effects=_EFFECT),
    )(*[hbm(t) for t in (*srcs, *lands)])
    return res[:-1], res[-1]


def _exchange_wait(name, handle, after):
    n = len(handle) // 4
    sems, thru = handle[:2 * n], handle[2 * n:]

    def body(*refs):
        land_refs = refs[n:2 * n]
        send_sems, recv_sems = refs[2 * n:3 * n], refs[3 * n:4 * n]
        me = (lax.axis_index("x"), lax.axis_index("y"), lax.axis_index("c"))
        for a in range(n):
            seven = land_refs[a].at[pl.ds(0, len(_FLIPS))]
            all_copies = pltpu.make_async_remote_copy(
                src_ref=seven, dst_ref=seven, send_sem=send_sems[a], recv_sem=recv_sems[a], device_id=me,
                device_id_type=_MESH)
            all_copies.wait_send()
            all_copies.wait_recv()

    res = pl.pallas_call(
        body, name=name, out_shape=tuple(pltpu.HBM(t.shape, t.dtype) for t in thru),
        in_specs=[*[_HBM] * (2 * n), *[_SEM] * (2 * n), _ANY], out_specs=tuple([_HBM] * (2 * n)),
        input_output_aliases={i: i for i in range(2 * n)},
        compiler_params=pltpu.CompilerParams(has_side_effects=_EFFECT),
    )(*thru, *sems, after)
    return res[n:]


def _adamw_gathered(name, own, parts, me, w, m, v):
    r, c = w.shape
    tr = min(r, 256)

    def body(me_ref, own_ref, p_ref, w_ref, m_ref, v_ref, g_out, d_out, m_out, v_out):
        g = own_ref[...]
        for j in range(N_DEV):
            g = g + p_ref[j].astype(F32)
        delta, m_new, v_new = _adamw(w_ref[...], g, m_ref[...], v_ref[...])
        g_out[...] = g
        d_out[...] = delta
        m_out[...] = m_new
        v_out[...] = v_new

    spec = pl.BlockSpec((tr, c), lambda i, me_ref: (i, 0))
    return pl.pallas_call(
        body, name=name,
        grid_spec=pltpu.PrefetchScalarGridSpec(
            num_scalar_prefetch=1, grid=(r // tr,),
            in_specs=[pl.BlockSpec((None, tr, c), lambda i, me_ref: (me_ref[0], i, 0)),
                      pl.BlockSpec((N_DEV, tr, c), lambda i, me_ref: (0, i, 0)), spec, spec, spec],
            out_specs=[spec] * 4),
        out_shape=[jax.ShapeDtypeStruct((r, c), F32)] * 4,
        compiler_params=_params(("parallel",)),
    )(me, own, parts, w, m, v)


def _adamw(w, g, m, v):
    m = ADAM_B1 * m + (1.0 - ADAM_B1) * g
    v = ADAM_B2 * v + (1.0 - ADAM_B2) * jnp.square(g)
    m_hat = m / (1.0 - ADAM_B1 ** ADAM_STEP)
    v_hat = v / (1.0 - ADAM_B2 ** ADAM_STEP)
    delta = -ADAM_LR * (m_hat / (jnp.sqrt(v_hat) + ADAM_EPS) + ADAM_WD * w)
    return delta, m, v


def _adamw_small(name, parts, w, m, v):
    _, r, c = parts.shape
    tr = 8

    def body(p_ref, w_ref, m_ref, v_ref, g_out, d_out, m_out, v_out):
        g = p_ref[0]
        for j in range(1, N_DEV):
            g = g + p_ref[j]
        delta, m_new, v_new = _adamw(w_ref[...], g, m_ref[...], v_ref[...])
        g_out[...] = g
        d_out[...] = delta
        m_out[...] = m_new
        v_out[...] = v_new

    spec = pl.BlockSpec((tr, c), lambda i: (i, 0))
    return pl.pallas_call(
        body, name=name, grid=(r // tr,),
        in_specs=[pl.BlockSpec((N_DEV, tr, c), lambda i: (0, i, 0)), spec, spec, spec],
        out_specs=[spec] * 4, out_shape=[jax.ShapeDtypeStruct((r, c), F32)] * 4,
        compiler_params=_params(("parallel",)),
    )(parts, w, m, v)


_WEIGHTS = ["norm1_g", "w_in", "q_norm_g", "k_norm_g", "ssm_lambda_re", "ssm_lambda_im", "ssm_log_dt", "ssm_b_re",
            "ssm_b_im", "ssm_c_re", "ssm_c_im", "ssm_d", "w_glu", "b_glu", "attn_out_g", "ssm_out_g", "w_out",
            "norm2_g", "w_mlp_in", "w_mlp_out"]
_BIG = ["w_in", "w_glu", "w_out", "w_mlp_in", "w_mlp_out"]
_SMALL = [n for n in _WEIGHTS if n not in _BIG]
_PACK_COLS = 1024


def _pack(tree, last=None):
    flat = [tree[n].reshape(-1).astype(F32) for n in _SMALL]
    size = sum(f.shape[0] for f in flat)
    rows = -(-(size + 1) // (_PACK_COLS * 8)) * 8
    pad = jnp.zeros((rows * _PACK_COLS - size - 1,), F32)
    tail = jnp.zeros((1,), F32) if last is None else last.reshape(1).astype(F32)
    return jnp.concatenate(flat + [pad, tail]).reshape(rows, _PACK_COLS)


def _unpack(buf, like):
    flat, out, off = buf.reshape(-1), {}, 0
    for n in _SMALL:
        size = like[n].size
        out[n] = flat[off:off + size].reshape(like[n].shape)
        off += size
    return out


def kernel(x, norm1_g, w_in, q_norm_g, k_norm_g, ssm_lambda_re, ssm_lambda_im, ssm_log_dt, ssm_b_re, ssm_b_im, ssm_c_re, ssm_c_im, ssm_d, w_glu, b_glu, attn_out_g, ssm_out_g, w_out, norm2_g, w_mlp_in, w_mlp_out, loss_target, m_norm1_g, m_w_in, m_q_norm_g, m_k_norm_g, m_ssm_lambda_re, m_ssm_lambda_im, m_ssm_log_dt, m_ssm_b_re, m_ssm_b_im, m_ssm_c_re, m_ssm_c_im, m_ssm_d, m_w_glu, m_b_glu, m_attn_out_g, m_ssm_out_g, m_w_out, m_norm2_g, m_w_mlp_in, m_w_mlp_out, v_norm1_g, v_w_in, v_q_norm_g, v_k_norm_g, v_ssm_lambda_re, v_ssm_lambda_im, v_ssm_log_dt, v_ssm_b_re, v_ssm_b_im, v_ssm_c_re, v_ssm_c_im, v_ssm_d, v_w_glu, v_b_glu, v_attn_out_g, v_ssm_out_g, v_w_out, v_norm2_g, v_w_mlp_in, v_w_mlp_out):
    w = dict(zip(_WEIGHTS, (norm1_g, w_in, q_norm_g, k_norm_g, ssm_lambda_re, ssm_lambda_im, ssm_log_dt, ssm_b_re, ssm_b_im, ssm_c_re, ssm_c_im, ssm_d, w_glu, b_glu, attn_out_g, ssm_out_g, w_out, norm2_g, w_mlp_in, w_mlp_out)))
    m = dict(zip(_WEIGHTS, (m_norm1_g, m_w_in, m_q_norm_g, m_k_norm_g, m_ssm_lambda_re, m_ssm_lambda_im, m_ssm_log_dt, m_ssm_b_re, m_ssm_b_im, m_ssm_c_re, m_ssm_c_im, m_ssm_d, m_w_glu, m_b_glu, m_attn_out_g, m_ssm_out_g, m_w_out, m_norm2_g, m_w_mlp_in, m_w_mlp_out)))
    v = dict(zip(_WEIGHTS, (v_norm1_g, v_w_in, v_q_norm_g, v_k_norm_g, v_ssm_lambda_re, v_ssm_lambda_im, v_ssm_log_dt, v_ssm_b_re, v_ssm_b_im, v_ssm_c_re, v_ssm_c_im, v_ssm_d, v_w_glu, v_b_glu, v_attn_out_g, v_ssm_out_g, v_w_out, v_norm2_g, v_w_mlp_in, v_w_mlp_out)))
    core = lax.axis_index("c").astype(jnp.int32).reshape(1)
    chip = (2 * lax.axis_index("x") + lax.axis_index("y")).astype(jnp.int32).reshape(1)

    me = (2 * chip + core).astype(jnp.int32)

    def landing(own=None, like=None):
        own = jnp.zeros_like(like) if own is None else own
        return lax.dynamic_update_slice(lax.empty((N_DEV, *like.shape), like.dtype), own[None], (me[0], 0, 0))

    (w_in_blocks,) = _all_gather("w_in_all_gather", [w_in.astype(BF16)])
    w_in_full = w_in_blocks.transpose(1, 0, 2).reshape(w_in.shape[0], -1)
    late = [n for n in _BIG if n != "w_in"]
    shards = [w[n].astype(BF16) for n in late]
    w_in_blocks, shards = lax.optimization_barrier((w_in_blocks, shards))
    weights_handle, weights_token = _exchange_start(
        "weights_send", shards, [landing(s, s) for s in shards], per_peer=False)

    def late_weights(after):
        got = dict(zip(late, _exchange_wait("weights_arrive", weights_handle, after)))
        return (got["w_glu"].reshape(-1, w_glu.shape[1]), got["w_out"].reshape(-1, w_out.shape[1]),
                got["w_mlp_in"].transpose(1, 0, 2).reshape(w_mlp_in.shape[0], -1),
                got["w_mlp_out"].reshape(-1, w_mlp_out.shape[1]))

    mlp = ["w_mlp_out", "w_mlp_in"]
    sent = {}

    def send_grads(name, names, own, own_b):
        blocks = lambda g, n: g.reshape(N_DEV, *w[n].shape)
        sent[name + "_own"] = [blocks(g, n) for g, n in zip(own, names)]
        srcs = [blocks(g, n) for g, n in zip(own_b, names)]
        sent[name], token = _exchange_start(name, srcs, [landing(like=s[0]) for s in srcs], per_peer=True)
        return token[0, 0]

    def mlp_grads_ready(g_out, g_out_b, g_in, g_in_b):
        return send_grads("mlp_grads_send", mlp, [g_out, g_in], [g_out_b, g_in_b])

    rest = ["w_in", "w_glu", "w_out"]

    def rest_grads_ready(own, own_b):
        return send_grads("rest_grads_send", rest, own, own_b)

    loss_local, grad_x, g_small, g_big = _local_step(
        x, loss_target, {n: w[n] for n in _SMALL}, w_in_full, late_weights, mlp_grads_ready, rest_grads_ready,
        weights_token[0, 0])

    grads, delta, new_m, new_v = {}, {}, {}, {}
    small = _pack(g_small, last=loss_local)
    small_handle, small_token = _exchange_start("small_grads_send", [small], [landing(small, small)], per_peer=False)

    for send, arrive, names in (("mlp_grads_send", "mlp_grads_arrive", mlp),
                                ("rest_grads_send", "rest_grads_arrive", rest)):
        for n, own, part in zip(names, sent[send + "_own"], _exchange_wait(arrive, sent[send], small_token)):
            grads[n], delta[n], new_m[n], new_v[n] = _adamw_gathered("adamw_" + n, own, part, me, w[n], m[n], v[n])

    shards_done = lax.optimization_barrier(tuple(new_v[n] for n in _BIG))
    (small_parts,) = _exchange_wait("small_grads_arrive", small_handle, shards_done[-1])
    packed = _adamw_small("adamw_small", small_parts, _pack(w), _pack(m), _pack(v))
    for tree, buf in zip((grads, delta, new_m, new_v), packed):
        tree.update(_unpack(buf, w))
    loss = packed[0][-1, -1]

    return (loss, grad_x, *[grads[n] for n in _WEIGHTS], *[delta[n] for n in _WEIGHTS],
            *[new_m[n] for n in _WEIGHTS], *[new_v[n] for n in _WEIGHTS])
```

```python
import functools
import math

import jax
import jax.numpy as jnp
from jax import lax
from jax.experimental import pallas as pl
from jax.experimental.pallas import tpu as pltpu

F32 = jnp.float32
BF16 = jnp.bfloat16

EPS = 1e-6
HEAD_DIM = 64
N_HEADS = 8
SB_WIDTH = 512
SSM_WIDTH = 512
SSM_GROUP = 16
SSM_GROUPS = 32
SSM_STATE = 64
QBLOCK = 128
KBLOCK = 256
N_CHUNK = 8
SSM_COLS = 4
LANES = 128
N_DEV = 8

ADAM_LR = 0.001
ADAM_B1 = 0.9
ADAM_B2 = 0.999
ADAM_EPS = 1e-08
ADAM_WD = 0.01
ADAM_STEP = 10

VMEM_LIMIT = 56 * 1024 * 1024

_NT = (((1,), (1,)), ((), ()))
_NN = (((1,), (0,)), ((), ()))
_TN = (((0,), (0,)), ((), ()))


def _dot(a, b, dims=_NN):
    return lax.dot_general(a, b, dims, preferred_element_type=F32)


def _params(sem):
    return pltpu.CompilerParams(dimension_semantics=sem, vmem_limit_bytes=VMEM_LIMIT)


def _matmul(name, a, b, *, ta=False, tb=False, extras=(), epilogue=None, out_dtypes=(F32,), sums=(),
            prologue=None, col_blocked=False, tm=1024, tn=512, tk=4096):
    M, K = (a.shape[1], a.shape[0]) if ta else a.shape
    N = b.shape[0] if tb else b.shape[1]
    tm, tn, tk = min(tm, M), min(tn, N), min(tk, K)
    assert M % tm == 0 and N % tn == 0 and K % tk == 0, (name, M, N, K)
    assert not (sums or prologue) or (tn == N and tk == K), name
    assert not prologue or not (ta or col_blocked), name
    nk = K // tk
    n_ex, n_out, n_sum = len(extras), len(out_dtypes), len(sums)
    n_pro = 1 if prologue else 0
    dims = (((0 if ta else 1,), (1 if tb else 0,)), ((), ()))

    def body(*refs):
        a_ref, b_ref = refs[0], refs[1]
        ex_refs = refs[2 + n_pro:2 + n_pro + n_ex]
        o_refs = refs[2 + n_pro + n_ex:2 + n_pro + n_ex + n_out]
        s_refs = refs[2 + n_pro + n_ex + n_out:2 + n_pro + n_ex + n_out + n_sum]
        k = pl.program_id(2)
        if prologue:
            left = prologue[0](a_ref[...], refs[2][...]).astype(BF16)
            refs[2 + n_pro + n_ex + n_out + n_sum][...] = left
        else:
            left = a_ref[...].astype(BF16)
        part = _dot(left, b_ref[...].astype(BF16), dims)

        def finish(acc):
            outs = (acc,) if epilogue is None else epilogue(acc, *[e[...] for e in ex_refs])
            for o_ref, o in zip(o_refs, outs[:n_out]):
                o_ref[...] = o.astype(o_ref.dtype)
            if n_sum:
                @pl.when(pl.program_id(0) == 0)
                def _():
                    for s_ref in s_refs:
                        s_ref[...] = jnp.zeros_like(s_ref)

                for s_ref, v in zip(s_refs, outs[n_out:]):
                    s_ref[...] += v

        if nk == 1:
            finish(part)
        else:
            acc_ref = refs[-1]

            @pl.when(k == 0)
            def _():
                acc_ref[...] = part

            @pl.when(jnp.logical_and(k > 0, k < nk - 1))
            def _():
                acc_ref[...] += part

            @pl.when(k == nk - 1)
            def _():
                finish(acc_ref[...] + part)

    a_spec = pl.BlockSpec((tk, tm), lambda i, j, k: (k, i)) if ta else pl.BlockSpec((tm, tk), lambda i, j, k: (i, k))
    b_spec = pl.BlockSpec((tn, tk), lambda i, j, k: (j, k)) if tb else pl.BlockSpec((tk, tn), lambda i, j, k: (k, j))
    ex_specs = [pl.BlockSpec((1, tn), lambda i, j, k: (0, j)) if e.shape[0] == 1 else
                pl.BlockSpec((tm, tn), lambda i, j, k: (i, j)) for e in extras]
    if col_blocked:
        out_specs = [pl.BlockSpec((None, tm, tn), lambda i, j, k: (j, i, 0)) for _ in out_dtypes]
        out_shape = [jax.ShapeDtypeStruct((N // tn, M, tn), dt) for dt in out_dtypes]
    else:
        wide = [(dt, N) if not isinstance(dt, tuple) else dt for dt in out_dtypes]
        assert all(w == N for _, w in wide) or tn == N, name
        out_specs = [pl.BlockSpec((tm, tn if w == N else w), lambda i, j, k: (i, j)) for _, w in wide]
        out_shape = [jax.ShapeDtypeStruct((M, w), dt) for dt, w in wide]
    out_specs += [pl.BlockSpec(s, lambda i, j, k: (0, 0)) for s in sums]
    out_shape += [jax.ShapeDtypeStruct(s, F32) for s in sums]
    pro_specs, pro_args = [], []
    if prologue:
        pro_specs, pro_args = [pl.BlockSpec((1, tk), lambda i, j, k: (0, 0))], [prologue[1]]
        out_specs.append(pl.BlockSpec((tm, tk), lambda i, j, k: (i, 0)))
        out_shape.append(jax.ShapeDtypeStruct((M, K), BF16))
    outs = pl.pallas_call(
        body, name=name, grid=(M // tm, N // tn, nk),
        in_specs=[a_spec, b_spec, *pro_specs, *ex_specs], out_specs=out_specs, out_shape=out_shape,
        scratch_shapes=[pltpu.VMEM((tm, tn), F32)] if nk > 1 else [],
        compiler_params=_params(("arbitrary",) * 3 if sums else ("parallel", "parallel", "arbitrary")),
    )(a, b, *pro_args, *extras)
    return outs[0] if len(outs) == 1 else outs


def _rowwise(name, fn, rows, small, outs, sums=(), tile=512):
    specs, args = [], []
    T = None
    for r in rows:
        arr, cb, w = r if isinstance(r, tuple) else (r, 0, r.shape[1])
        T = arr.shape[0]
        specs.append((w, cb))
        args.append(arr)
    tile = min(tile, T)
    assert T % tile == 0
    n_r, n_s, n_o, n_a = len(rows), len(small), len(outs), len(sums)

    def body(*refs):
        r_refs = refs[:n_r]
        s_refs = refs[n_r:n_r + n_s]
        o_refs = refs[n_r + n_s:n_r + n_s + n_o]
        a_refs = refs[n_r + n_s + n_o:]
        res = fn(*[r[...] for r in r_refs], *[s[...] for s in s_refs])
        res = res if isinstance(res, (tuple, list)) else (res,)
        for o_ref, o in zip(o_refs, res[:n_o]):
            o_ref[...] = o.astype(o_ref.dtype)

        @pl.when(pl.program_id(0) == 0)
        def _():
            for a_ref in a_refs:
                a_ref[...] = jnp.zeros_like(a_ref)

        for a_ref, v in zip(a_refs, res[n_o:]):
            a_ref[...] += v.astype(F32)

    in_specs = [pl.BlockSpec((tile, w), functools.partial(lambda i, cb: (i, cb), cb=cb)) for w, cb in specs]
    in_specs += [pl.BlockSpec(s.shape, functools.partial(lambda i, nd: (0,) * nd, nd=s.ndim)) for s in small]
    out_specs = [pl.BlockSpec((tile, w), lambda i: (i, 0)) for w, _ in outs]
    out_specs += [pl.BlockSpec(s, functools.partial(lambda i, nd: (0,) * nd, nd=len(s))) for s in sums]
    out_shape = [jax.ShapeDtypeStruct((T, w), dt) for w, dt in outs]
    out_shape += [jax.ShapeDtypeStruct(s, F32) for s in sums]
    res = pl.pallas_call(
        body, name=name, grid=(T // tile,), in_specs=in_specs, out_specs=out_specs, out_shape=out_shape,
        compiler_params=_params(("arbitrary",)),
    )(*args, *small)
    return res[0] if len(res) == 1 else res


def _rms(x, g):
    return x * lax.rsqrt(jnp.mean(x * x, axis=-1, keepdims=True) + EPS) * g


def _glu_branch(y, pre, b_glu, g_out):
    g = jax.nn.gelu(y)
    return _rms(g * jax.nn.sigmoid(pre + b_glu), g_out)


def _split_dot(x, tri_bf):
    hi = x.astype(BF16)
    lo = (x - hi.astype(F32)).astype(BF16)
    return _dot(hi, tri_bf) + _dot(lo, tri_bf)


def _softplus(z):
    return jnp.maximum(z, 0.0) + jnp.log(1.0 + jnp.exp(-jnp.abs(z)))


def _head(h):
    return slice(h * HEAD_DIM, (h + 1) * HEAD_DIM)


def _head_mean(x, seg):
    return _split_dot(x, seg) * (1.0 / HEAD_DIM)


def _qk_norm(proj, gains):
    r = lax.div(lax.broadcasted_iota(jnp.int32, (SB_WIDTH, SB_WIDTH), 0), HEAD_DIM)
    c = lax.div(lax.broadcasted_iota(jnp.int32, (SB_WIDTH, SB_WIDTH), 1), HEAD_DIM)
    seg = (r == c).astype(BF16)
    q, k, v = (proj[:, i * SB_WIDTH:(i + 1) * SB_WIDTH] for i in range(3))
    qn = q * lax.rsqrt(_head_mean(q * q, seg) + EPS) * gains[:, 0:SB_WIDTH]
    kn = k * lax.rsqrt(_head_mean(k * k, seg) + EPS) * gains[:, SB_WIDTH:2 * SB_WIDTH]
    return proj, qn, kn, v


def _qk_norm_bwd(x, g, dy, seg):
    r = lax.rsqrt(_head_mean(x * x, seg) + EPS)
    gdy = g * dy
    dx = r * gdy - x * (r * r * r) * _head_mean(gdy * x, seg)
    return dx, jnp.sum(dy * x * r, axis=0, keepdims=True)


def _split_heads(refs, scratch, L):
    def chunk(i, _):
        r = pl.ds(pl.multiple_of(i * QBLOCK, QBLOCK), QBLOCK)
        for ref, s in zip(refs, scratch):
            for h in range(2):
                s[h, r, :] = ref[r, _head(h)]
        return 0

    lax.fori_loop(0, L // QBLOCK, chunk, 0)


Q_HALVES = KBLOCK // QBLOCK
_CHAINS = [(h, r) for h in range(2) for r in range(Q_HALVES)]


def _valid(i, kb):
    row = lax.broadcasted_iota(jnp.int32, (QBLOCK, KBLOCK), 0)
    col = lax.broadcasted_iota(jnp.int32, (QBLOCK, KBLOCK), 1)
    return col + (kb * KBLOCK - i * QBLOCK) < row


def _attn_fwd(qn, kn, vb, B, L):
    n_pairs = L // KBLOCK
    n_hp = N_HEADS // 2
    nc = len(_CHAINS)

    def body(q_ref, k_ref, v_ref, o_ref, a_ref, q_s, k_s, v_s, after_s, z_s, stage_s, sems):
        _split_heads((q_ref, k_ref, v_ref), (q_s, k_s, v_s), L)
        r2 = lax.broadcasted_iota(jnp.int32, (KBLOCK, KBLOCK), 0)
        c2 = lax.broadcasted_iota(jnp.int32, (KBLOCK, KBLOCK), 1)
        after_s[...] = (r2 > c2).astype(after_s.dtype)
        g = pl.program_id(0) * n_hp + pl.program_id(1)

        def q_pair(p, _):
            rows = [pl.ds(pl.multiple_of((p * Q_HALVES + r) * QBLOCK, QBLOCK), QBLOCK) for r in range(Q_HALVES)]
            q_c = [q_s[h, rows[r], :] for h, r in _CHAINS]
            cs = range(nc)

            def scores(kb):
                rk = pl.ds(pl.multiple_of(kb * KBLOCK, KBLOCK), KBLOCK)
                return [_dot(q_c[c], k_s[_CHAINS[c][0], rk, :], _NT) for c in cs]

            def saved(kb):
                return pltpu.make_async_copy(stage_s.at[kb & 1], a_ref.at[g, p, kb], sems.at[kb & 1])

            def k_block(kb, carry, diagonal):
                rk = pl.ds(pl.multiple_of(kb * KBLOCK, KBLOCK), KBLOCK)
                if diagonal:
                    valid = [_valid(p * Q_HALVES + r, kb) for r in range(Q_HALVES)]
                    keep = lambda c, t: jnp.where(valid[_CHAINS[c][1]], t, 0.0)
                    z = scores(kb)
                else:
                    keep = lambda c, t: t
                    z = [z_s[(kb + 1) & 1, c] for c in cs]
                ahead = scores(jnp.maximum(kb - 1, 0))
                for c in cs:
                    z_s[kb & 1, c] = ahead[c]
                sp = [_softplus(z[c]) for c in cs]
                lsig = [z[c] - sp[c] for c in cs]
                lom = [keep(c, -sp[c]) for c in cs]
                tail = [_split_dot(lom[c], after_s[...]) + carry[c][0] for c in cs]
                a = [keep(c, jnp.exp(lsig[c] + tail[c])).astype(v_s.dtype) for c in cs]
                acc = [carry[c][1] + _dot(a[c], v_s[_CHAINS[c][0], rk, :]) for c in cs]
                for c in cs:
                    stage_s[kb & 1, c] = a[c]
                saved(kb).start()
                return tuple((carry[c][0] + jnp.sum(lom[c], axis=1, keepdims=True), acc[c]) for c in cs)

            def next_block(n, carry):
                kb = p - n

                @pl.when(n >= 2)
                def _():
                    saved(kb + 2).wait()

                return k_block(kb, carry, False)

            init = (jnp.zeros((QBLOCK, 1), F32), jnp.zeros((QBLOCK, HEAD_DIM), F32))
            first = k_block(p, (init,) * nc, True)
            res = lax.fori_loop(1, p + 1, next_block, first)
            saved(0).wait()

            @pl.when(p >= 1)
            def _():
                saved(1).wait()

            for r in range(Q_HALVES):
                o_ref[rows[r], :] = jnp.concatenate([res[c][1] for c in cs if _CHAINS[c][1] == r], axis=1)
            return 0

        lax.fori_loop(0, n_pairs, q_pair, 0)

    spec = pl.BlockSpec((L, LANES), lambda b, p: (b, p))
    return pl.pallas_call(
        body, name="attn_fwd", grid=(B, n_hp),
        in_specs=[spec] * 3, out_specs=[spec, _ANY],
        out_shape=[jax.ShapeDtypeStruct((B * L, SB_WIDTH), F32),
                   jax.ShapeDtypeStruct((B * n_hp, n_pairs, n_pairs, nc, QBLOCK, KBLOCK), BF16)],
        scratch_shapes=[pltpu.VMEM((2, L, HEAD_DIM), BF16)] * 3 + [pltpu.VMEM((KBLOCK, KBLOCK), BF16)]
        + [pltpu.VMEM((2, nc, QBLOCK, KBLOCK), F32), pltpu.VMEM((2, nc, QBLOCK, KBLOCK), BF16),
           pltpu.SemaphoreType.DMA((2,))],
        compiler_params=_params(("parallel", "parallel")),
    )(qn, kn, vb)


def _attn_bwd(qn, kn, vb, kept_a, d_sb, proj, gq2, gk2, B, L):
    n_pairs = L // KBLOCK
    n_hp = N_HEADS // 2
    nc = len(_CHAINS)
    slots = 4
    scale = 1.0 / math.sqrt(HEAD_DIM)

    def body(q_ref, k_ref, v_ref, do_ref, qraw_ref, kraw_ref, gq_ref, gk_ref, a_ref,
             dq_ref, dk_ref, dv_ref, dgq_ref, dgk_ref,
             q_s, k_s, v_s, qt_s, dkt_s, dvt_s, before_s, stage_s, sems):
        _split_heads((q_ref, k_ref, v_ref), (q_s, k_s, v_s), L)
        g = pl.program_id(0) * n_hp + pl.program_id(1)
        lane_head = [lax.div(lax.broadcasted_iota(jnp.int32, (LANES, LANES), d), HEAD_DIM) for d in (0, 1)]
        seg = (lane_head[0] == lane_head[1]).astype(BF16)

        def transpose_q(i, _):
            r = pl.ds(pl.multiple_of(i * QBLOCK, QBLOCK), QBLOCK)
            qt_s[:, r] = q_ref[r, :].astype(F32).T.astype(qt_s.dtype)
            return 0

        lax.fori_loop(0, L // QBLOCK, transpose_q, 0)
        dkt_s[...] = jnp.zeros_like(dkt_s)
        dvt_s[...] = jnp.zeros_like(dvt_s)
        r2 = lax.broadcasted_iota(jnp.int32, (KBLOCK, KBLOCK), 0)
        c2 = lax.broadcasted_iota(jnp.int32, (KBLOCK, KBLOCK), 1)
        before_s[...] = (r2 < c2).astype(before_s.dtype)

        def q_pair(p, dgq):
            rows = [pl.ds(pl.multiple_of((p * Q_HALVES + r) * QBLOCK, QBLOCK), QBLOCK) for r in range(Q_HALVES)]
            pair = pl.ds(pl.multiple_of(p * KBLOCK, KBLOCK), KBLOCK)
            do2 = do_ref[pair, :]
            do_t = do2.T.astype(v_s.dtype)
            cs = range(nc)
            hs = range(2)
            q_c = [q_s[h, rows[r], :] for h, r in _CHAINS]
            do_c = [do2[r * QBLOCK:(r + 1) * QBLOCK, _head(h)].astype(v_s.dtype) for h, r in _CHAINS]
            qt_h = [qt_s[_head(h), pair] for h in hs]
            dot_h = [do_t[_head(h), :] for h in hs]

            def kept(kb):
                slot = lax.rem(kb, slots)
                return pltpu.make_async_copy(a_ref.at[g, p, kb], stage_s.at[slot], sems.at[slot])

            def k_block(kb, carry, diagonal):
                rk = pl.ds(pl.multiple_of(kb * KBLOCK, KBLOCK), KBLOCK)
                if diagonal:
                    valid = [_valid(p * Q_HALVES + r, kb) for r in range(Q_HALVES)]
                    keep = lambda c, t: jnp.where(valid[_CHAINS[c][1]], t, 0.0)
                else:
                    keep = lambda c, t: t

                    @pl.when(kb + slots - 1 <= p)
                    def _():
                        kept(kb + slots - 1).start()

                kept(kb).wait()
                slot = lax.rem(kb, slots)
                k_b = [k_s[h, rk, :] for h in hs]
                z = [_dot(q_c[c], k_b[_CHAINS[c][0]], _NT) for c in cs]
                da = [_dot(do_c[c], v_s[_CHAINS[c][0], rk, :], _NT) for c in cs]
                a = [stage_s[slot, c] for c in cs]
                dla = [a[c].astype(F32) * da[c] for c in cs]
                for h in hs:
                    a_h = jnp.concatenate([a[c] for c in cs if _CHAINS[c][0] == h], axis=0)
                    dvt_s[_head(h), rk] += _dot(dot_h[h], a_h)
                d_lom = [carry[c][0] + _split_dot(dla[c], before_s[...]) for c in cs]
                beta = [jax.nn.sigmoid(z[c]) for c in cs]
                dz_b = [(dla[c] * (1.0 - beta[c]) - keep(c, beta[c] * d_lom[c])).astype(v_s.dtype) for c in cs]
                dq_acc = [carry[c][1] + _dot(dz_b[c], k_b[_CHAINS[c][0]]) for c in cs]
                for h in hs:
                    dz_h = jnp.concatenate([dz_b[c] for c in cs if _CHAINS[c][0] == h], axis=0)
                    dkt_s[_head(h), rk] += _dot(qt_h[h], dz_h)
                return tuple((carry[c][0] + jnp.sum(dla[c], axis=1, keepdims=True), dq_acc[c]) for c in cs)

            init = (jnp.zeros((QBLOCK, 1), F32), jnp.zeros((QBLOCK, HEAD_DIM), F32))
            kept(0).start()
            for first in range(1, slots - 1):
                @pl.when(p >= first)
                def _():
                    kept(first).start()

            before = lax.fori_loop(0, p, lambda kb, carry: k_block(kb, carry, False), (init,) * nc)
            res = k_block(p, before, True)
            for r in range(Q_HALVES):
                d_qn = jnp.concatenate([res[c][1] for c in cs if _CHAINS[c][1] == r], axis=1) * scale
                dq, dg = _qk_norm_bwd(qraw_ref[rows[r], :], gq_ref[...], d_qn, seg)
                dq_ref[rows[r], :] = dq.astype(dq_ref.dtype)
                dgq = dgq + dg
            return dgq

        dgq = lax.fori_loop(0, n_pairs, q_pair, jnp.zeros((1, LANES), F32))

        def transpose_out(i, dgk):
            r = pl.ds(pl.multiple_of(i * QBLOCK, QBLOCK), QBLOCK)
            dk, dg = _qk_norm_bwd(kraw_ref[r, :], gk_ref[...], dkt_s[:, r].T, seg)
            dk_ref[r, :] = dk.astype(dk_ref.dtype)
            dv_ref[r, :] = dvt_s[:, r].T.astype(dv_ref.dtype)
            return dgk + dg

        dgk = lax.fori_loop(0, L // QBLOCK, transpose_out, jnp.zeros((1, LANES), F32))

        @pl.when(jnp.logical_and(pl.program_id(0) == 0, pl.program_id(1) == 0))
        def _():
            dgq_ref[...] = jnp.zeros_like(dgq_ref)
            dgk_ref[...] = jnp.zeros_like(dgk_ref)

        dgq_ref[...] += dgq[:, _head(0)] + dgq[:, _head(1)]
        dgk_ref[...] += dgk[:, _head(0)] + dgk[:, _head(1)]

    spec = pl.BlockSpec((L, LANES), lambda b, p: (b, p))
    gain = pl.BlockSpec((1, LANES), lambda b, p: (0, 0))
    gain_grad = pl.BlockSpec((1, HEAD_DIM), lambda b, p: (0, 0))
    return pl.pallas_call(
        body, name="attn_bwd", grid=(B, n_hp),
        in_specs=[spec] * 4 + [spec, pl.BlockSpec((L, LANES), lambda b, p: (b, n_hp + p)), gain, gain, _ANY],
        out_specs=[spec] * 3 + [gain_grad] * 2,
        out_shape=[jax.ShapeDtypeStruct((B * L, SB_WIDTH), BF16)] * 3 + [jax.ShapeDtypeStruct((1, HEAD_DIM), F32)] * 2,
        scratch_shapes=[pltpu.VMEM((2, L, HEAD_DIM), BF16)] * 3 + [pltpu.VMEM((LANES, L), BF16)]
        + [pltpu.VMEM((LANES, L), F32)] * 2 + [pltpu.VMEM((KBLOCK, KBLOCK), BF16)]
        + [pltpu.VMEM((slots, nc, QBLOCK, KBLOCK), BF16), pltpu.SemaphoreType.DMA((slots,))],
        compiler_params=_params(("arbitrary", "arbitrary")),
    )(qn, kn, vb, d_sb, proj, proj, gq2, gk2, kept_a)


def _ssm_discretise(lam_re, lam_im, log_dt, b_re, b_im):
    dt = jnp.exp(log_dt)
    mag = jnp.exp(lam_re * dt)
    lbr = mag * jnp.cos(lam_im * dt)
    lbi = mag * jnp.sin(lam_im * dt)
    den = lam_re * lam_re + lam_im * lam_im
    nr, ni = lbr - 1.0, lbi
    cr = (nr * lam_re + ni * lam_im) / den
    ci = (ni * lam_re - nr * lam_im) / den
    return lbr, lbi, cr * b_re - ci * b_im, cr * b_im + ci * b_re


def _ssm_prep(lam_re, lam_im, log_dt, b_re_t, b_im_t):
    def body(lr, li, ld, br, bi, o_lr, o_li, o_br, o_bi):
        res = _ssm_discretise(lr[...], li[...], ld[...], br[...], bi[...])
        for o, v in zip((o_lr, o_li, o_br, o_bi), res):
            o[...] = v

    return pl.pallas_call(
        body, name="ssm_prep",
        out_shape=[jax.ShapeDtypeStruct(lam_re.shape, F32)] * 2 + [jax.ShapeDtypeStruct(b_re_t.shape, F32)] * 2,
    )(lam_re, lam_im, log_dt, b_re_t, b_im_t)


def _ssm_prep_bwd(lam_re, lam_im, log_dt, b_re_t, b_im_t, d_lr, d_li, d_br, d_bi):
    def body(lr, li, ld, br, bi, g_lr, g_li, g_br, g_bi, o_lr, o_li, o_ld, o_br, o_bi):
        _, vjp = jax.vjp(_ssm_discretise, lr[...], li[...], ld[...], br[...], bi[...])
        res = vjp((g_lr[...], g_li[...], g_br[...], g_bi[...]))
        for o, v in zip((o_lr, o_li, o_ld, o_br, o_bi), res):
            o[...] = v

    return pl.pallas_call(
        body, name="ssm_prep_bwd",
        out_shape=[jax.ShapeDtypeStruct(lam_re.shape, F32)] * 2 + [jax.ShapeDtypeStruct(log_dt.shape, F32)]
        + [jax.ShapeDtypeStruct(b_re_t.shape, F32)] * 2,
    )(lam_re, lam_im, log_dt, b_re_t, b_im_t, d_lr, d_li, d_br, d_bi)


def _block_diag(m):
    m4 = m.reshape(SSM_COLS, 8, SSM_GROUP, SSM_STATE)
    return jnp.einsum("aghp,gk->aghkp", m4, jnp.eye(8, dtype=m.dtype)).reshape(SSM_COLS, LANES, 512)


def _block_diag_take(d):
    d6 = d.reshape(SSM_COLS, 8, SSM_GROUP, 2, 8, SSM_STATE)
    return jnp.einsum("aghrgp->raghp", d6).reshape(2, SSM_GROUPS, SSM_GROUP, SSM_STATE)


def _cmul(ar, ai, br, bi):
    return ar * br - ai * bi, ar * bi + ai * br


def _power(lr, li, n):
    assert n & (n - 1) == 0
    for _ in range(n.bit_length() - 1):
        lr, li = _cmul(lr, li, lr, li)
    return lr, li


def _ssm_fwd(u_p, w_b, lam_r, lam_i, c_m, d_skip, B, L, tj):
    J = L // N_CHUNK
    njt = J // tj
    R = tj * N_CHUNK
    H = 512

    def body(u_ref, wb_ref, lr_ref, li_ref, cm_ref, d_ref, y_ref, gel_ref, x_ref, xin_ref, bu_s, st_s, xin_s):
        ph, jt = pl.program_id(2), pl.program_id(3)
        lr, li = lr_ref[...], li_ref[...]

        @pl.when(jnp.logical_and(ph == 0, jt == 0))
        def _():
            st_s[...] = jnp.zeros_like(st_s)

        @pl.when(ph == 0)
        def _():
            bu_s[jt] = _dot(u_ref[...].astype(BF16), wb_ref[...].astype(BF16))

        def scan(store):
            def step(j, carry):
                xr, xi = carry
                r = pl.ds(pl.multiple_of(j * N_CHUNK, N_CHUNK), N_CHUNK)
                nr = lr * xr - li * xi + bu_s[jt, r, 0:H]
                ni = lr * xi + li * xr + bu_s[jt, r, H:2 * H]
                if store:
                    x_ref[r, 0:H] = nr
                    x_ref[r, H:2 * H] = ni
                return nr, ni

            xr, xi = lax.fori_loop(0, tj, step, (st_s[:, 0:H], st_s[:, H:2 * H]))
            st_s[:, 0:H] = xr
            st_s[:, H:2 * H] = xi

        @pl.when(ph == 0)
        def _():
            scan(False)

            @pl.when(jt == njt - 1)
            def _():
                pr, pi = _power(lr[0:1], li[0:1], J)
                xin_s[0:1, :] = jnp.zeros((1, 2 * H), F32)
                for c in range(1, N_CHUNK):
                    qr, qi = _cmul(pr, pi, xin_s[c - 1:c, 0:H], xin_s[c - 1:c, H:2 * H])
                    xin_s[c:c + 1, 0:H] = qr + st_s[c - 1:c, 0:H]
                    xin_s[c:c + 1, H:2 * H] = qi + st_s[c - 1:c, H:2 * H]
                xin_ref[...] = xin_s[...]
                st_s[...] = xin_s[...]

        @pl.when(ph == 1)
        def _():
            scan(True)
            y = _dot(x_ref[...].astype(BF16), cm_ref[...].astype(BF16)) + d_ref[...] * u_ref[...]
            y_ref[...] = y
            gel_ref[...] = jax.nn.gelu(y).astype(gel_ref.dtype)

    return pl.pallas_call(
        body, name="ssm_fwd", grid=(SSM_COLS, B, 2, njt),
        in_specs=[
            pl.BlockSpec((None, R, LANES), lambda i, b, ph, jt: (b, jt, i)),
            pl.BlockSpec((None, LANES, 2 * H), lambda i, b, ph, jt: (i, 0, 0)),
            pl.BlockSpec((None, N_CHUNK, H), lambda i, b, ph, jt: (i, 0, 0)),
            pl.BlockSpec((None, N_CHUNK, H), lambda i, b, ph, jt: (i, 0, 0)),
            pl.BlockSpec((None, 2 * H, LANES), lambda i, b, ph, jt: (i, 0, 0)),
            pl.BlockSpec((1, LANES), lambda i, b, ph, jt: (0, i)),
        ],
        out_specs=[
            pl.BlockSpec((None, R, LANES), lambda i, b, ph, jt: (b, jt * ph, i)),
            pl.BlockSpec((None, R, LANES), lambda i, b, ph, jt: (b, jt * ph, i)),
            pl.BlockSpec((None, R, 2 * H), lambda i, b, ph, jt: (b, jt * ph, i)),
            pl.BlockSpec((None, None, N_CHUNK, 2 * H), lambda i, b, ph, jt: (b, i, 0, 0)),
        ],
        out_shape=[
            jax.ShapeDtypeStruct((B, L, SSM_WIDTH), F32),
            jax.ShapeDtypeStruct((B, L, SSM_WIDTH), BF16),
            jax.ShapeDtypeStruct((B, L, SSM_COLS * 2 * H), F32),
            jax.ShapeDtypeStruct((B, SSM_COLS, N_CHUNK, 2 * H), F32),
        ],
        scratch_shapes=[pltpu.VMEM((njt, R, 2 * H), F32), pltpu.VMEM((N_CHUNK, 2 * H), F32),
                        pltpu.VMEM((N_CHUNK, 2 * H), F32)],
        compiler_params=_params(("arbitrary",) * 4),
    )(u_p, w_b, lam_r, lam_i, c_m, d_skip)


def _ssm_bwd(dy_p, u_p, x, xin, w_bt, lam_r, lam_i, c_mt, d_skip, B, L, tj):
    J = L // N_CHUNK
    njt = J // tj
    R = tj * N_CHUNK
    H = 512
    x4 = x.reshape(B, J, N_CHUNK, SSM_COLS * 2 * H)

    def body(dy_ref, u_ref, x_ref, xp_ref, xin_ref, wbt_ref, lr_ref, li_ref, cmt_ref, d_ref,
             du_ref, dwb_ref, dcm_ref, dlr_ref, dli_ref, dd_ref, ca_s, a_s, st_s, dl_s):
        b, ph, jt = pl.program_id(1), pl.program_id(2), pl.program_id(3)
        jr = njt - 1 - jt
        lr, li = lr_ref[...], -li_ref[...]

        @pl.when(jnp.logical_and(b == 0, jnp.logical_and(ph == 0, jt == 0)))
        def _():
            dwb_ref[...] = jnp.zeros_like(dwb_ref)
            dcm_ref[...] = jnp.zeros_like(dcm_ref)
            dlr_ref[...] = jnp.zeros_like(dlr_ref)
            dli_ref[...] = jnp.zeros_like(dli_ref)
            dd_ref[...] = jnp.zeros_like(dd_ref)
            dl_s[...] = jnp.zeros_like(dl_s)

        @pl.when(jnp.logical_and(ph == 0, jt == 0))
        def _():
            st_s[...] = jnp.zeros_like(st_s)

        @pl.when(ph == 0)
        def _():
            ca_s[jt] = _dot(dy_ref[...].astype(BF16), cmt_ref[...].astype(BF16))

        def scan(store):
            def step(n, carry):
                ar, ai = carry
                r = pl.ds(pl.multiple_of((tj - 1 - n) * N_CHUNK, N_CHUNK), N_CHUNK)
                nr = lr * ar - li * ai + ca_s[jt, r, 0:H]
                ni = lr * ai + li * ar + ca_s[jt, r, H:2 * H]
                if store:
                    a_s[r, 0:H] = nr
                    a_s[r, H:2 * H] = ni
                return nr, ni

            ar, ai = lax.fori_loop(0, tj, step, (st_s[:, 0:H], st_s[:, H:2 * H]))
            st_s[:, 0:H] = ar
            st_s[:, H:2 * H] = ai

        @pl.when(ph == 0)
        def _():
            scan(False)

            @pl.when(jt == njt - 1)
            def _():
                pr, pi = _power(lr[0:1], li[0:1], J)
                a_s[N_CHUNK - 1:N_CHUNK, :] = jnp.zeros((1, 2 * H), F32)
                for c in range(N_CHUNK - 2, -1, -1):
                    qr, qi = _cmul(pr, pi, a_s[c + 1:c + 2, 0:H], a_s[c + 1:c + 2, H:2 * H])
                    a_s[c:c + 1, 0:H] = qr + st_s[c + 1:c + 2, 0:H]
                    a_s[c:c + 1, H:2 * H] = qi + st_s[c + 1:c + 2, H:2 * H]
                st_s[...] = a_s[0:N_CHUNK, :]

        @pl.when(ph == 1)
        def _():
            scan(True)
            dy = dy_ref[...]
            u = u_ref[...]
            a_b = a_s[...].astype(BF16)
            du_ref[...] = (_dot(a_b, wbt_ref[...].astype(BF16)) + d_ref[...] * dy).astype(du_ref.dtype)
            dwb_ref[...] += _dot(u.astype(BF16), a_b, _TN)
            dcm_ref[...] += _dot(x_ref[...].astype(BF16), dy.astype(BF16), _TN)
            dd_ref[...] += jnp.sum(dy * u, axis=0, keepdims=True)

            first = jnp.where(jr == 0, xin_ref[...], xp_ref[...])
            a0r, a0i = a_s[0:N_CHUNK, 0:H], a_s[0:N_CHUNK, H:2 * H]
            acc0 = (a0r * first[:, 0:H] + a0i * first[:, H:2 * H], a0i * first[:, 0:H] - a0r * first[:, H:2 * H])

            def step(j, carry):
                sr, si = carry
                r = pl.ds(pl.multiple_of(j * N_CHUNK, N_CHUNK), N_CHUNK)
                rp = pl.ds(pl.multiple_of((j - 1) * N_CHUNK, N_CHUNK), N_CHUNK)
                ar, ai = a_s[r, 0:H], a_s[r, H:2 * H]
                xr, xi = x_ref[rp, 0:H], x_ref[rp, H:2 * H]
                return sr + ar * xr + ai * xi, si + ai * xr - ar * xi

            sr, si = lax.fori_loop(1, tj, step, acc0)
            dl_s[:, 0:H] += sr
            dl_s[:, H:2 * H] += si

            @pl.when(jnp.logical_and(b == B - 1, jt == njt - 1))
            def _():
                dlr_ref[...] = jnp.sum(dl_s[:, 0:H], axis=0, keepdims=True)
                dli_ref[...] = jnp.sum(dl_s[:, H:2 * H], axis=0, keepdims=True)
                dl_s[...] = jnp.zeros_like(dl_s)

    rev = lambda ph, jt: (njt - 1 - jt) * ph + (njt - 1) * (1 - ph)
    return pl.pallas_call(
        body, name="ssm_bwd", grid=(SSM_COLS, B, 2, njt),
        in_specs=[
            pl.BlockSpec((None, R, LANES), lambda i, b, ph, jt: (b, njt - 1 - jt, i)),
            pl.BlockSpec((None, R, LANES), lambda i, b, ph, jt: (b, njt - 1 - jt, i)),
            pl.BlockSpec((None, R, 2 * H), lambda i, b, ph, jt: (b, rev(ph, jt), i)),
            pl.BlockSpec((None, None, N_CHUNK, 2 * H),
                         lambda i, b, ph, jt: (b, jnp.maximum((njt - 1 - jt) * tj - 1, 0), 0, i)),
            pl.BlockSpec((None, None, N_CHUNK, 2 * H), lambda i, b, ph, jt: (b, i, 0, 0)),
            pl.BlockSpec((None, 2 * H, LANES), lambda i, b, ph, jt: (i, 0, 0)),
            pl.BlockSpec((None, N_CHUNK, H), lambda i, b, ph, jt: (i, 0, 0)),
            pl.BlockSpec((None, N_CHUNK, H), lambda i, b, ph, jt: (i, 0, 0)),
            pl.BlockSpec((None, LANES, 2 * H), lambda i, b, ph, jt: (i, 0, 0)),
            pl.BlockSpec((1, LANES), lambda i, b, ph, jt: (0, i)),
        ],
        out_specs=[
            pl.BlockSpec((None, R, LANES), lambda i, b, ph, jt: (b, rev(ph, jt), i)),
            pl.BlockSpec((None, LANES, 2 * H), lambda i, b, ph, jt: (i, 0, 0)),
            pl.BlockSpec((None, 2 * H, LANES), lambda i, b, ph, jt: (i, 0, 0)),
            pl.BlockSpec((None, 1, H), lambda i, b, ph, jt: (i, 0, 0)),
            pl.BlockSpec((None, 1, H), lambda i, b, ph, jt: (i, 0, 0)),
            pl.BlockSpec((1, LANES), lambda i, b, ph, jt: (0, i)),
        ],
        out_shape=[
            jax.ShapeDtypeStruct((B, L, SSM_WIDTH), BF16),
            jax.ShapeDtypeStruct((SSM_COLS, LANES, 2 * H), F32),
            jax.ShapeDtypeStruct((SSM_COLS, 2 * H, LANES), F32),
            jax.ShapeDtypeStruct((SSM_COLS, 1, H), F32),
            jax.ShapeDtypeStruct((SSM_COLS, 1, H), F32),
            jax.ShapeDtypeStruct((1, SSM_WIDTH), F32),
        ],
        scratch_shapes=[pltpu.VMEM((njt, R, 2 * H), F32), pltpu.VMEM((R, 2 * H), F32),
                        pltpu.VMEM((N_CHUNK, 2 * H), F32), pltpu.VMEM((N_CHUNK, 2 * H), F32)],
        compiler_params=_params(("arbitrary",) * 4),
    )(dy_p, u_p, x, x4, xin, w_bt, lam_r, lam_i, c_mt, d_skip)


def _to_scan_layout(t, B, L):
    C = t.shape[-1]
    return t.reshape(B, N_CHUNK, L // N_CHUNK, C).transpose(0, 2, 1, 3).reshape(B, L, C)


def _from_scan_layout(t, B, L):
    C = t.shape[-1]
    return t.reshape(B, L // N_CHUNK, N_CHUNK, C).transpose(0, 2, 1, 3).reshape(B * L, C)


def _local_step(x, target, p, w_in, late_weights, mlp_grads_ready=None, rest_grads_ready=None, order=None, *,
                ssm_tile=128):
    B, L, D = x.shape
    T = B * L
    x2 = x.reshape(T, D)
    row = lambda v: v.reshape(1, -1)
    g1, g2, ga, gs, b_glu = row(p["norm1_g"]), row(p["norm2_g"]), row(p["attn_out_g"]), row(p["ssm_out_g"]), row(p["b_glu"])
    g1_first = g1 if order is None else g1 + order
    gq8 = jnp.tile(row(p["q_norm_g"]), (1, N_HEADS))
    gk8 = jnp.tile(row(p["k_norm_g"]), (1, N_HEADS))

    G, P, Hh = SSM_GROUPS, SSM_STATE, SSM_GROUP
    lam_re3, lam_im3 = p["ssm_lambda_re"].reshape(G, 1, P), p["ssm_lambda_im"].reshape(G, 1, P)
    log_dt3 = p["ssm_log_dt"].reshape(G, 1, 1)
    b_re_t, b_im_t = p["ssm_b_re"].transpose(0, 2, 1), p["ssm_b_im"].transpose(0, 2, 1)
    lbr, lbi, bbr, bbi = _ssm_prep(lam_re3, lam_im3, log_dt3, b_re_t, b_im_t)
    w_b = jnp.concatenate([_block_diag(bbr), _block_diag(bbi)], axis=2)
    c_mt = jnp.concatenate([_block_diag(p["ssm_c_re"]), -_block_diag(p["ssm_c_im"])], axis=2)
    w_bt, c_m = w_b.transpose(0, 2, 1), c_mt.transpose(0, 2, 1)
    lam_r = jnp.broadcast_to(lbr.reshape(SSM_COLS, 1, 512), (SSM_COLS, N_CHUNK, 512))
    lam_i = jnp.broadcast_to(lbi.reshape(SSM_COLS, 1, 512), (SSM_COLS, N_CHUNK, 512))
    d_skip = p["ssm_d"].reshape(1, SSM_WIDTH)

    qk_gains = jnp.concatenate([gq8 * (1.0 / math.sqrt(HEAD_DIM)), gk8, jnp.ones((1, 2 * SB_WIDTH), F32)], axis=1)
    head = (BF16, SB_WIDTH)
    proj, qn, kn, vb, xn = _matmul("proj", x2, w_in, prologue=(_rms, g1_first), extras=[qk_gains],
                                   out_dtypes=(F32, head, head, head), epilogue=_qk_norm, tm=512, tn=w_in.shape[1])
    sb, attn_kept = _attn_fwd(qn, kn, vb, B, L)
    u_p = _to_scan_layout(proj[:, 3 * SB_WIDTH:], B, L)
    y_p, gel_p, xs, xin = _ssm_fwd(u_p, w_b, lam_r, lam_i, c_m, d_skip, B, L, ssm_tile)
    y2, gel = y_p.reshape(T, SSM_WIDTH), gel_p.reshape(T, SSM_WIDTH)
    gel, sb = lax.optimization_barrier((gel, sb))
    w_glu, w_out, w_mlp_in, w_mlp_out = late_weights(gel)
    pre, ssm_n = _matmul("glu_gate", gel, w_glu, extras=[y2, b_glu, gs], out_dtypes=(F32, BF16),
                         epilogue=lambda acc, y, b, g: (acc, _glu_branch(y, acc, b, g)))
    mixed = _rowwise(
        "attn_out_norm", lambda s, n, g: jnp.concatenate([_rms(s, g).astype(BF16), n.astype(BF16)], axis=1),
        [sb, _from_scan_layout(ssm_n, B, L)], [ga], [(D, BF16)])
    def residual_and_norm(acc, res, g):
        h = acc + res
        return h, _rms(h, g)

    h1, hn = _matmul("out_proj", mixed, w_out, extras=[x2, g2], out_dtypes=(F32, BF16), tn=D,
                     epilogue=residual_and_norm)
    act, a_pre = _matmul("mlp_in", hn, w_mlp_in, out_dtypes=(BF16, BF16), tn=1024,
                         epilogue=lambda acc: (jnp.square(jnp.maximum(acc, 0.0)), acc))

    def loss_fn(acc, h, t):
        diff = acc + h - t
        part = jnp.sum(jnp.sum(diff * diff, axis=0, keepdims=True), axis=1, keepdims=True)
        d = diff * (1.0 / D)
        return d, d, part * (0.5 / D)

    d_out, d_out_b, loss = _matmul("mlp_out", act, w_mlp_out, extras=[h1, target.reshape(T, D)],
                                   out_dtypes=(F32, BF16), sums=[(1, 1)], epilogue=loss_fn, tm=512, tn=D)

    d_apre = _matmul("mlp_out_dx", d_out_b, w_mlp_out, tb=True, extras=[a_pre], out_dtypes=(BF16,), tn=1024,
                     epilogue=lambda acc, ap: (acc * (2.0 * jnp.maximum(ap.astype(F32), 0.0)),))
    both = lambda acc: (acc, acc)
    g_w_mlp_out, g_w_mlp_out_b = _matmul("mlp_out_dw", act, d_out_b, ta=True, out_dtypes=(F32, BF16), epilogue=both)
    g_w_mlp_in, g_w_mlp_in_b = _matmul("mlp_in_dw", hn, d_apre, ta=True, col_blocked=True, out_dtypes=(F32, BF16),
                                       epilogue=both, tn=w_mlp_in.shape[1] // N_DEV)
    if mlp_grads_ready is not None:
        g2 = g2 + mlp_grads_ready(g_w_mlp_out, g_w_mlp_out_b, g_w_mlp_in, g_w_mlp_in_b)

    def norm_bwd_res(dy, h, res, g):
        _, vjp = jax.vjp(_rms, h, g)
        dh, dg = vjp(dy)
        return res + dh, dg

    def norm_bwd_res2(dy, h, res, g):
        d, dg = norm_bwd_res(dy, h, res, g)
        return d, d, dg

    d_h1, d_h1_b, g_norm2 = _matmul("mlp_in_dx", d_apre, w_mlp_in, tb=True, extras=[h1, d_out, g2],
                                    out_dtypes=(F32, BF16), sums=[(1, D)], epilogue=norm_bwd_res2, tm=512, tn=D)

    d_mixed = _matmul("out_proj_dx", d_h1_b, w_out, tb=True, tn=1024)
    g_w_out, g_w_out_b = _matmul("out_proj_dw", mixed, d_h1_b, ta=True, out_dtypes=(F32, BF16), epilogue=both)

    def norm_bwd(h, dy, g):
        _, vjp = jax.vjp(_rms, h, g)
        return vjp(dy)

    d_sb, g_attn_out = _rowwise("attn_out_norm_bwd", norm_bwd, [sb, (d_mixed, 0, SB_WIDTH)], [ga],
                                [(SB_WIDTH, F32)], sums=[(1, SB_WIDTH)])
    d_ssm_n = _to_scan_layout(d_mixed[:, SB_WIDTH:], B, L).reshape(T, SSM_WIDTH)

    def glu_bwd(y, pre_, dy, bg, g):
        _, vjp = jax.vjp(_glu_branch, y, pre_, bg, g)
        d_y, d_pre, d_bg, d_g = vjp(dy)
        return d_y, d_pre, d_bg, d_g

    d_y_direct, d_pre, g_b_glu, g_ssm_out = _rowwise(
        "glu_out_bwd", glu_bwd, [y2, pre, d_ssm_n], [b_glu, gs], [(SSM_WIDTH, F32), (SSM_WIDTH, BF16)],
        sums=[(1, SSM_WIDTH), (1, SSM_WIDTH)])
    g_w_glu, g_w_glu_b = _matmul("glu_gate_dw", gel, d_pre, ta=True, out_dtypes=(F32, BF16), epilogue=both)

    def gelu_bwd(dg, y, dy0):
        _, vjp = jax.vjp(jax.nn.gelu, y)
        return (dy0 + vjp(dg)[0],)

    d_y = _matmul("glu_gate_dx", d_pre, w_glu, tb=True, extras=[y2, d_y_direct], epilogue=gelu_bwd)

    du_p, d_wb, d_cm, d_lr, d_li, g_d = _ssm_bwd(
        d_y.reshape(B, L, SSM_WIDTH), u_p, xs, xin, w_bt, lam_r, lam_i, c_mt, d_skip, B, L, ssm_tile)
    d_bb = _block_diag_take(d_wb.reshape(SSM_COLS, LANES, 2, 512))
    d_c = _block_diag_take(d_cm.transpose(0, 2, 1).reshape(SSM_COLS, LANES, 2, 512))
    g_lam_re, g_lam_im, g_log_dt, g_b_re_t, g_b_im_t = _ssm_prep_bwd(
        lam_re3, lam_im3, log_dt3, b_re_t, b_im_t,
        d_lr.reshape(G, 1, P), d_li.reshape(G, 1, P), d_bb[0], d_bb[1])
    d_q, d_k, d_v, g_q, g_k = _attn_bwd(qn, kn, vb, attn_kept, d_sb, proj, gq8[:, :LANES], gk8[:, :LANES], B, L)

    d_proj = jnp.concatenate([d_q, d_k, d_v, _from_scan_layout(du_p, B, L)], axis=1)
    g_w_in, g_w_in_b = _matmul("proj_dw", xn, d_proj, ta=True, col_blocked=True, out_dtypes=(F32, BF16),
                               epilogue=both, tn=w_in.shape[1] // N_DEV)
    if rest_grads_ready is not None:
        g1 = g1 + rest_grads_ready([g_w_in, g_w_glu, g_w_out], [g_w_in_b, g_w_glu_b, g_w_out_b])
    grad_x, g_norm1 = _matmul("proj_dx", d_proj, w_in, tb=True, extras=[x2, d_h1, g1], sums=[(1, D)],
                              epilogue=norm_bwd_res, tm=512, tn=D)

    small = {
        "norm1_g": g_norm1.reshape(-1),
        "q_norm_g": g_q.reshape(-1),
        "k_norm_g": g_k.reshape(-1),
        "ssm_lambda_re": g_lam_re.reshape(G, P),
        "ssm_lambda_im": g_lam_im.reshape(G, P),
        "ssm_log_dt": g_log_dt.reshape(G),
        "ssm_b_re": g_b_re_t.transpose(0, 2, 1),
        "ssm_b_im": g_b_im_t.transpose(0, 2, 1),
        "ssm_c_re": d_c[0],
        "ssm_c_im": -d_c[1],
        "ssm_d": g_d.reshape(G, Hh),
        "b_glu": g_b_glu.reshape(-1),
        "attn_out_g": g_attn_out.reshape(-1),
        "ssm_out_g": g_ssm_out.reshape(-1),
        "norm2_g": g_norm2.reshape(-1),
    }
    big = {"w_in": g_w_in, "w_glu": g_w_glu, "w_out": g_w_out, "w_mlp_in": g_w_mlp_in, "w_mlp_out": g_w_mlp_out}
    return loss[0, 0], grad_x.reshape(B, L, D), small, big


_ANY = pl.BlockSpec(memory_space=pl.ANY)
_MESH = pl.DeviceIdType.MESH


def _all_gather(name, shards):
    n = len(shards)

    def body(*refs):
        in_refs, out_refs = refs[:n], refs[n:2 * n]
        send_sems, recv_sems, local_sems = refs[2 * n:]
        x, y, c = lax.axis_index("x"), lax.axis_index("y"), lax.axis_index("c")
        me, sibling = (x, y, c), (x, y, 1 - c)
        chips = [(1 - x, y), (x, 1 - y), (1 - x, 1 - y)]

        def copy(a, k, block, to, src=None):
            px, py, pc = block
            rows = out_refs[a].at[4 * px + 2 * py + pc]
            return pltpu.make_async_remote_copy(
                src_ref=rows if src is None else src, dst_ref=rows, send_sem=send_sems.at[a, k],
                recv_sem=recv_sems.at[a, k], device_id=to, device_id_type=_MESH)

        mine = [pltpu.make_async_copy(in_refs[a], out_refs[a].at[4 * x + 2 * y + c], local_sems.at[a]) for a in range(n)]
        first, passed = [], []
        for a in range(n):
            mine[a].start()
            first.append(copy(a, 0, me, sibling, src=in_refs[a]))
            first += [copy(a, 1 + j, me, (*chip, c), src=in_refs[a]) for j, chip in enumerate(chips)]
        for cp in first:
            cp.start()
        for j, chip in enumerate(chips):
            for a in range(n):
                copy(a, 1 + j, (*chip, c), me).wait_recv()
                fwd = copy(a, 4 + j, (*chip, c), sibling)
                fwd.start()
                passed.append(fwd)
        for a in range(n):
            copy(a, 0, sibling, me).wait_recv()
            for j, chip in enumerate(chips):
                copy(a, 4 + j, (*chip, 1 - c), me).wait_recv()
        for cp in first + passed:
            cp.wait_send()
        for cp in mine:
            cp.wait()

    return pl.pallas_call(
        body, name=name,
        in_specs=[_ANY] * n, out_specs=[_ANY] * n,
        out_shape=[jax.ShapeDtypeStruct((N_DEV, *s.shape), s.dtype) for s in shards],
        scratch_shapes=[pltpu.SemaphoreType.DMA((n, 7)), pltpu.SemaphoreType.DMA((n, 7)), pltpu.SemaphoreType.DMA((n,))],
    )(*shards)


_HBM = pl.BlockSpec(memory_space=pltpu.HBM)
_SEM = pl.BlockSpec(memory_space=pltpu.SEMAPHORE)
_EFFECT = pltpu.SideEffectType.DATAFLOW_SIDE_EFFECTING
_FLIPS = [(dx, dy, dc) for dx in (0, 1) for dy in (0, 1) for dc in (0, 1) if (dx, dy, dc) != (0, 0, 0)]


def _exchange_start(name, srcs, lands, per_peer):
    n = len(srcs)

    def body(*refs):
        src_refs, land_refs = refs[:n], refs[n:2 * n]
        send_sems, recv_sems = refs[2 * n:3 * n], refs[3 * n:4 * n]
        token = refs[-1]
        x, y, c = lax.axis_index("x"), lax.axis_index("y"), lax.axis_index("c")
        me = 4 * x + 2 * y + c
        for dx, dy, dc in _FLIPS:
            px, py, pc = (1 - x if dx else x), (1 - y if dy else y), (1 - c if dc else c)
            for a in range(n):
                pltpu.make_async_remote_copy(
                    src_ref=src_refs[a].at[4 * px + 2 * py + pc] if per_peer else src_refs[a],
                    dst_ref=land_refs[a].at[me], send_sem=send_sems[a], recv_sem=recv_sems[a],
                    device_id=(px, py, pc), device_id_type=_MESH).start()
        token[...] = jnp.zeros_like(token)

    hbm = lambda t: pltpu.with_memory_space_constraint(t, pltpu.HBM)
    res = pl.pallas_call(
        body, name=name,
        out_shape=(*[pltpu.SemaphoreType.DMA(())] * (2 * n), *[pltpu.HBM(t.shape, t.dtype) for t in (*srcs, *lands)],
                   jax.ShapeDtypeStruct((8, LANES), F32)),
        in_specs=[_HBM] * (2 * n),
        out_specs=(*[_SEM] * (2 * n), *[_HBM] * (2 * n), pl.BlockSpec(memory_space=pltpu.VMEM)),
        input_output_aliases={i: 2 * n + i for i in range(2 * n)},
        compiler_params=pltpu.CompilerParams(has_side_effects=_EFFECT),
    )(*[hbm(t) for t in (*srcs, *lands)])
    return res[:-1], res[-1]


def _exchange_wait(name, handle, after):
    n = len(handle) // 4
    sems, thru = handle[:2 * n], handle[2 * n:]

    def body(*refs):
        land_refs = refs[n:2 * n]
        send_sems, recv_sems = refs[2 * n:3 * n], refs[3 * n:4 * n]
        me = (lax.axis_index("x"), lax.axis_index("y"), lax.axis_index("c"))
        for a in range(n):
            seven = land_refs[a].at[pl.ds(0, len(_FLIPS))]
            all_copies = pltpu.make_async_remote_copy(
                src_ref=seven, dst_ref=seven, send_sem=send_sems[a], recv_sem=recv_sems[a], device_id=me,
                device_id_type=_MESH)
            all_copies.wait_send()
            all_copies.wait_recv()

    res = pl.pallas_call(
        body, name=name, out_shape=tuple(pltpu.HBM(t.shape, t.dtype) for t in thru),
        in_specs=[*[_HBM] * (2 * n), *[_SEM] * (2 * n), _ANY], out_specs=tuple([_HBM] * (2 * n)),
        input_output_aliases={i: i for i in range(2 * n)},
        compiler_params=pltpu.CompilerParams(has_side_effects=_EFFECT),
    )(*thru, *sems, after)
    return res[n:]


def _adamw_gathered(name, own, parts, me, w, m, v):
    r, c = w.shape
    tr = min(r, 256)

    def body(me_ref, own_ref, p_ref, w_ref, m_ref, v_ref, g_out, d_out, m_out, v_out):
        g = own_ref[...]
        for j in range(N_DEV):
            g = g + p_ref[j].astype(F32)
        delta, m_new, v_new = _adamw(w_ref[...], g, m_ref[...], v_ref[...])
        g_out[...] = g
        d_out[...] = delta
        m_out[...] = m_new
        v_out[...] = v_new

    spec = pl.BlockSpec((tr, c), lambda i, me_ref: (i, 0))
    return pl.pallas_call(
        body, name=name,
        grid_spec=pltpu.PrefetchScalarGridSpec(
            num_scalar_prefetch=1, grid=(r // tr,),
            in_specs=[pl.BlockSpec((None, tr, c), lambda i, me_ref: (me_ref[0], i, 0)),
                      pl.BlockSpec((N_DEV, tr, c), lambda i, me_ref: (0, i, 0)), spec, spec, spec],
            out_specs=[spec] * 4),
        out_shape=[jax.ShapeDtypeStruct((r, c), F32)] * 4,
        compiler_params=_params(("parallel",)),
    )(me, own, parts, w, m, v)


def _adamw(w, g, m, v):
    m = ADAM_B1 * m + (1.0 - ADAM_B1) * g
    v = ADAM_B2 * v + (1.0 - ADAM_B2) * jnp.square(g)
    m_hat = m / (1.0 - ADAM_B1 ** ADAM_STEP)
    v_hat = v / (1.0 - ADAM_B2 ** ADAM_STEP)
    delta = -ADAM_LR * (m_hat / (jnp.sqrt(v_hat) + ADAM_EPS) + ADAM_WD * w)
    return delta, m, v


def _adamw_small(name, parts, w, m, v):
    _, r, c = parts.shape
    tr = 8

    def body(p_ref, w_ref, m_ref, v_ref, g_out, d_out, m_out, v_out):
        g = p_ref[0]
        for j in range(1, N_DEV):
            g = g + p_ref[j]
        delta, m_new, v_new = _adamw(w_ref[...], g, m_ref[...], v_ref[...])
        g_out[...] = g
        d_out[...] = delta
        m_out[...] = m_new
        v_out[...] = v_new

    spec = pl.BlockSpec((tr, c), lambda i: (i, 0))
    return pl.pallas_call(
        body, name=name, grid=(r // tr,),
        in_specs=[pl.BlockSpec((N_DEV, tr, c), lambda i: (0, i, 0)), spec, spec, spec],
        out_specs=[spec] * 4, out_shape=[jax.ShapeDtypeStruct((r, c), F32)] * 4,
        compiler_params=_params(("parallel",)),
    )(parts, w, m, v)


_WEIGHTS = ["norm1_g", "w_in", "q_norm_g", "k_norm_g", "ssm_lambda_re", "ssm_lambda_im", "ssm_log_dt", "ssm_b_re",
            "ssm_b_im", "ssm_c_re", "ssm_c_im", "ssm_d", "w_glu", "b_glu", "attn_out_g", "ssm_out_g", "w_out",
            "norm2_g", "w_mlp_in", "w_mlp_out"]
_BIG = ["w_in", "w_glu", "w_out", "w_mlp_in", "w_mlp_out"]
_SMALL = [n for n in _WEIGHTS if n not in _BIG]
_PACK_COLS = 1024


def _pack(tree, last=None):
    flat = [tree[n].reshape(-1).astype(F32) for n in _SMALL]
    size = sum(f.shape[0] for f in flat)
    rows = -(-(size + 1) // (_PACK_COLS * 8)) * 8
    pad = jnp.zeros((rows * _PACK_COLS - size - 1,), F32)
    tail = jnp.zeros((1,), F32) if last is None else last.reshape(1).astype(F32)
    return jnp.concatenate(flat + [pad, tail]).reshape(rows, _PACK_COLS)


def _unpack(buf, like):
    flat, out, off = buf.reshape(-1), {}, 0
    for n in _SMALL:
        size = like[n].size
        out[n] = flat[off:off + size].reshape(like[n].shape)
        off += size
    return out


def kernel(x, norm1_g, w_in, q_norm_g, k_norm_g, ssm_lambda_re, ssm_lambda_im, ssm_log_dt, ssm_b_re, ssm_b_im, ssm_c_re, ssm_c_im, ssm_d, w_glu, b_glu, attn_out_g, ssm_out_g, w_out, norm2_g, w_mlp_in, w_mlp_out, loss_target, m_norm1_g, m_w_in, m_q_norm_g, m_k_norm_g, m_ssm_lambda_re, m_ssm_lambda_im, m_ssm_log_dt, m_ssm_b_re, m_ssm_b_im, m_ssm_c_re, m_ssm_c_im, m_ssm_d, m_w_glu, m_b_glu, m_attn_out_g, m_ssm_out_g, m_w_out, m_norm2_g, m_w_mlp_in, m_w_mlp_out, v_norm1_g, v_w_in, v_q_norm_g, v_k_norm_g, v_ssm_lambda_re, v_ssm_lambda_im, v_ssm_log_dt, v_ssm_b_re, v_ssm_b_im, v_ssm_c_re, v_ssm_c_im, v_ssm_d, v_w_glu, v_b_glu, v_attn_out_g, v_ssm_out_g, v_w_out, v_norm2_g, v_w_mlp_in, v_w_mlp_out):
    w = dict(zip(_WEIGHTS, (norm1_g, w_in, q_norm_g, k_norm_g, ssm_lambda_re, ssm_lambda_im, ssm_log_dt, ssm_b_re, ssm_b_im, ssm_c_re, ssm_c_im, ssm_d, w_glu, b_glu, attn_out_g, ssm_out_g, w_out, norm2_g, w_mlp_in, w_mlp_out)))
    m = dict(zip(_WEIGHTS, (m_norm1_g, m_w_in, m_q_norm_g, m_k_norm_g, m_ssm_lambda_re, m_ssm_lambda_im, m_ssm_log_dt, m_ssm_b_re, m_ssm_b_im, m_ssm_c_re, m_ssm_c_im, m_ssm_d, m_w_glu, m_b_glu, m_attn_out_g, m_ssm_out_g, m_w_out, m_norm2_g, m_w_mlp_in, m_w_mlp_out)))
    v = dict(zip(_WEIGHTS, (v_norm1_g, v_w_in, v_q_norm_g, v_k_norm_g, v_ssm_lambda_re, v_ssm_lambda_im, v_ssm_log_dt, v_ssm_b_re, v_ssm_b_im, v_ssm_c_re, v_ssm_c_im, v_ssm_d, v_w_glu, v_b_glu, v_attn_out_g, v_ssm_out_g, v_w_out, v_norm2_g, v_w_mlp_in, v_w_mlp_out)))
    core = lax.axis_index("c").astype(jnp.int32).reshape(1)
    chip = (2 * lax.axis_index("x") + lax.axis_index("y")).astype(jnp.int32).reshape(1)

    me = (2 * chip + core).astype(jnp.int32)

    def landing(own=None, like=None):
        own = jnp.zeros_like(like) if own is None else own
        return lax.dynamic_update_slice(lax.empty((N_DEV, *like.shape), like.dtype), own[None], (me[0], 0, 0))

    (w_in_blocks,) = _all_gather("w_in_all_gather", [w_in.astype(BF16)])
    w_in_full = w_in_blocks.transpose(1, 0, 2).reshape(w_in.shape[0], -1)
    late = [n for n in _BIG if n != "w_in"]
    shards = [w[n].astype(BF16) for n in late]
    w_in_blocks, shards = lax.optimization_barrier((w_in_blocks, shards))
    weights_handle, weights_token = _exchange_start(
        "weights_send", shards, [landing(s, s) for s in shards], per_peer=False)

    def late_weights(after):
        got = dict(zip(late, _exchange_wait("weights_arrive", weights_handle, after)))
        return (got["w_glu"].reshape(-1, w_glu.shape[1]), got["w_out"].reshape(-1, w_out.shape[1]),
                got["w_mlp_in"].transpose(1, 0, 2).reshape(w_mlp_in.shape[0], -1),
                got["w_mlp_out"].reshape(-1, w_mlp_out.shape[1]))

    mlp = ["w_mlp_out", "w_mlp_in"]
    sent = {}

    def send_grads(name, names, own, own_b):
        blocks = lambda g, n: g.reshape(N_DEV, *w[n].shape)
        sent[name + "_own"] = [blocks(g, n) for g, n in zip(own, names)]
        srcs = [blocks(g, n) for g, n in zip(own_b, names)]
        sent[name], token = _exchange_start(name, srcs, [landing(like=s[0]) for s in srcs], per_peer=True)
        return token[0, 0]

    def mlp_grads_ready(g_out, g_out_b, g_in, g_in_b):
        return send_grads("mlp_grads_send", mlp, [g_out, g_in], [g_out_b, g_in_b])

    rest = ["w_in", "w_glu", "w_out"]

    def rest_grads_ready(own, own_b):
        return send_grads("rest_grads_send", rest, own, own_b)

    loss_local, grad_x, g_small, g_big = _local_step(
        x, loss_target, {n: w[n] for n in _SMALL}, w_in_full, late_weights, mlp_grads_ready, rest_grads_ready,
        weights_token[0, 0])

    grads, delta, new_m, new_v = {}, {}, {}, {}
    small = _pack(g_small, last=loss_local)
    small_handle, small_token = _exchange_start("small_grads_send", [small], [landing(small, small)], per_peer=False)

    for send, arrive, names in (("mlp_grads_send", "mlp_grads_arrive", mlp),
                                ("rest_grads_send", "rest_grads_arrive", rest)):
        for n, own, part in zip(names, sent[send + "_own"], _exchange_wait(arrive, sent[send], small_token)):
            grads[n], delta[n], new_m[n], new_v[n] = _adamw_gathered("adamw_" + n, own, part, me, w[n], m[n], v[n])

    shards_done = lax.optimization_barrier(tuple(new_v[n] for n in _BIG))
    (small_parts,) = _exchange_wait("small_grads_arrive", small_handle, shards_done[-1])
    packed = _adamw_small("adamw_small", small_parts, _pack(w), _pack(m), _pack(v))
    for tree, buf in zip((grads, delta, new_m, new_v), packed):
        tree.update(_unpack(buf, w))
    loss = packed[0][-1, -1]

    return (loss, grad_x, *[grads[n] for n in _WEIGHTS], *[delta[n] for n in _WEIGHTS],
            *[new_m[n] for n in _WEIGHTS], *[new_v[n] for n in _WEIGHTS])
```

```python
import functools
import math

import jax
import jax.numpy as jnp
from jax import lax
from jax.experimental import pallas as pl
from jax.experimental.pallas import tpu as pltpu

F32 = jnp.float32
BF16 = jnp.bfloat16

EPS = 1e-6
HEAD_DIM = 64
N_HEADS = 8
SB_WIDTH = 512
SSM_WIDTH = 512
SSM_GROUP = 16
SSM_GROUPS = 32
SSM_STATE = 64
QBLOCK = 128
KBLOCK = 256
N_CHUNK = 8
SSM_COLS = 4
LANES = 128
N_DEV = 8

ADAM_LR = 0.001
ADAM_B1 = 0.9
ADAM_B2 = 0.999
ADAM_EPS = 1e-08
ADAM_WD = 0.01
ADAM_STEP = 10

VMEM_LIMIT = 56 * 1024 * 1024

_NT = (((1,), (1,)), ((), ()))
_NN = (((1,), (0,)), ((), ()))
_TN = (((0,), (0,)), ((), ()))


def _dot(a, b, dims=_NN):
    return lax.dot_general(a, b, dims, preferred_element_type=F32)


def _params(sem):
    return pltpu.CompilerParams(dimension_semantics=sem, vmem_limit_bytes=VMEM_LIMIT)


def _matmul(name, a, b, *, ta=False, tb=False, extras=(), epilogue=None, out_dtypes=(F32,), sums=(),
            prologue=None, col_blocked=False, tm=1024, tn=512, tk=4096):
    M, K = (a.shape[1], a.shape[0]) if ta else a.shape
    N = b.shape[0] if tb else b.shape[1]
    tm, tn, tk = min(tm, M), min(tn, N), min(tk, K)
    assert M % tm == 0 and N % tn == 0 and K % tk == 0, (name, M, N, K)
    assert not (sums or prologue) or (tn == N and tk == K), name
    assert not prologue or not (ta or col_blocked), name
    nk = K // tk
    n_ex, n_out, n_sum = len(extras), len(out_dtypes), len(sums)
    n_pro = 1 if prologue else 0
    dims = (((0 if ta else 1,), (1 if tb else 0,)), ((), ()))

    def body(*refs):
        a_ref, b_ref = refs[0], refs[1]
        ex_refs = refs[2 + n_pro:2 + n_pro + n_ex]
        o_refs = refs[2 + n_pro + n_ex:2 + n_pro + n_ex + n_out]
        s_refs = refs[2 + n_pro + n_ex + n_out:2 + n_pro + n_ex + n_out + n_sum]
        k = pl.program_id(2)
        if prologue:
            left = prologue[0](a_ref[...], refs[2][...]).astype(BF16)
            refs[2 + n_pro + n_ex + n_out + n_sum][...] = left
        else:
            left = a_ref[...].astype(BF16)
        part = _dot(left, b_ref[...].astype(BF16), dims)

        def finish(acc):
            outs = (acc,) if epilogue is None else epilogue(acc, *[e[...] for e in ex_refs])
            for o_ref, o in zip(o_refs, outs[:n_out]):
                o_ref[...] = o.astype(o_ref.dtype)
            if n_sum:
                @pl.when(pl.program_id(0) == 0)
                def _():
                    for s_ref in s_refs:
                        s_ref[...] = jnp.zeros_like(s_ref)

                for s_ref, v in zip(s_refs, outs[n_out:]):
                    s_ref[...] += v

        if nk == 1:
            finish(part)
        else:
            acc_ref = refs[-1]

            @pl.when(k == 0)
            def _():
                acc_ref[...] = part

            @pl.when(jnp.logical_and(k > 0, k < nk - 1))
            def _():
                acc_ref[...] += part

            @pl.when(k == nk - 1)
            def _():
                finish(acc_ref[...] + part)

    a_spec = pl.BlockSpec((tk, tm), lambda i, j, k: (k, i)) if ta else pl.BlockSpec((tm, tk), lambda i, j, k: (i, k))
    b_spec = pl.BlockSpec((tn, tk), lambda i, j, k: (j, k)) if tb else pl.BlockSpec((tk, tn), lambda i, j, k: (k, j))
    ex_specs = [pl.BlockSpec((1, tn), lambda i, j, k: (0, j)) if e.shape[0] == 1 else
                pl.BlockSpec((tm, tn), lambda i, j, k: (i, j)) for e in extras]
    if col_blocked:
        out_specs = [pl.BlockSpec((None, tm, tn), lambda i, j, k: (j, i, 0)) for _ in out_dtypes]
        out_shape = [jax.ShapeDtypeStruct((N // tn, M, tn), dt) for dt in out_dtypes]
    else:
        wide = [(dt, N) if not isinstance(dt, tuple) else dt for dt in out_dtypes]
        assert all(w == N for _, w in wide) or tn == N, name
        out_specs = [pl.BlockSpec((tm, tn if w == N else w), lambda i, j, k: (i, j)) for _, w in wide]
        out_shape = [jax.ShapeDtypeStruct((M, w), dt) for dt, w in wide]
    out_specs += [pl.BlockSpec(s, lambda i, j, k: (0, 0)) for s in sums]
    out_shape += [jax.ShapeDtypeStruct(s, F32) for s in sums]
    pro_specs, pro_args = [], []
    if prologue:
        pro_specs, pro_args = [pl.BlockSpec((1, tk), lambda i, j, k: (0, 0))], [prologue[1]]
        out_specs.append(pl.BlockSpec((tm, tk), lambda i, j, k: (i, 0)))
        out_shape.append(jax.ShapeDtypeStruct((M, K), BF16))
    outs = pl.pallas_call(
        body, name=name, grid=(M // tm, N // tn, nk),
        in_specs=[a_spec, b_spec, *pro_specs, *ex_specs], out_specs=out_specs, out_shape=out_shape,
        scratch_shapes=[pltpu.VMEM((tm, tn), F32)] if nk > 1 else [],
        compiler_params=_params(("arbitrary",) * 3 if sums else ("parallel", "parallel", "arbitrary")),
    )(a, b, *pro_args, *extras)
    return outs[0] if len(outs) == 1 else outs


def _rowwise(name, fn, rows, small, outs, sums=(), tile=512):
    specs, args = [], []
    T = None
    for r in rows:
        arr, cb, w = r if isinstance(r, tuple) else (r, 0, r.shape[1])
        T = arr.shape[0]
        specs.append((w, cb))
        args.append(arr)
    tile = min(tile, T)
    assert T % tile == 0
    n_r, n_s, n_o, n_a = len(rows), len(small), len(outs), len(sums)

    def body(*refs):
        r_refs = refs[:n_r]
        s_refs = refs[n_r:n_r + n_s]
        o_refs = refs[n_r + n_s:n_r + n_s + n_o]
        a_refs = refs[n_r + n_s + n_o:]
        res = fn(*[r[...] for r in r_refs], *[s[...] for s in s_refs])
        res = res if isinstance(res, (tuple, list)) else (res,)
        for o_ref, o in zip(o_refs, res[:n_o]):
            o_ref[...] = o.astype(o_ref.dtype)

        @pl.when(pl.program_id(0) == 0)
        def _():
            for a_ref in a_refs:
                a_ref[...] = jnp.zeros_like(a_ref)

        for a_ref, v in zip(a_refs, res[n_o:]):
            a_ref[...] += v.astype(F32)

    in_specs = [pl.BlockSpec((tile, w), functools.partial(lambda i, cb: (i, cb), cb=cb)) for w, cb in specs]
    in_specs += [pl.BlockSpec(s.shape, functools.partial(lambda i, nd: (0,) * nd, nd=s.ndim)) for s in small]
    out_specs = [pl.BlockSpec((tile, w), lambda i: (i, 0)) for w, _ in outs]
    out_specs += [pl.BlockSpec(s, functools.partial(lambda i, nd: (0,) * nd, nd=len(s))) for s in sums]
    out_shape = [jax.ShapeDtypeStruct((T, w), dt) for w, dt in outs]
    out_shape += [jax.ShapeDtypeStruct(s, F32) for s in sums]
    res = pl.pallas_call(
        body, name=name, grid=(T // tile,), in_specs=in_specs, out_specs=out_specs, out_shape=out_shape,
        compiler_params=_params(("arbitrary",)),
    )(*args, *small)
    return res[0] if len(res) == 1 else res


def _rms(x, g):
    return x * lax.rsqrt(jnp.mean(x * x, axis=-1, keepdims=True) + EPS) * g


def _glu_branch(y, pre, b_glu, g_out):
    g = jax.nn.gelu(y)
    return _rms(g * jax.nn.sigmoid(pre + b_glu), g_out)


def _split_dot(x, tri_bf):
    hi = x.astype(BF16)
    lo = (x - hi.astype(F32)).astype(BF16)
    return _dot(hi, tri_bf) + _dot(lo, tri_bf)


def _softplus(z):
    return jnp.maximum(z, 0.0) + jnp.log(1.0 + jnp.exp(-jnp.abs(z)))


def _head(h):
    return slice(h * HEAD_DIM, (h + 1) * HEAD_DIM)


def _head_mean(x, seg):
    return _split_dot(x, seg) * (1.0 / HEAD_DIM)


def _qk_norm(proj, gains):
    r = lax.div(lax.broadcasted_iota(jnp.int32, (SB_WIDTH, SB_WIDTH), 0), HEAD_DIM)
    c = lax.div(lax.broadcasted_iota(jnp.int32, (SB_WIDTH, SB_WIDTH), 1), HEAD_DIM)
    seg = (r == c).astype(BF16)
    q, k, v = (proj[:, i * SB_WIDTH:(i + 1) * SB_WIDTH] for i in range(3))
    qn = q * lax.rsqrt(_head_mean(q * q, seg) + EPS) * gains[:, 0:SB_WIDTH]
    kn = k * lax.rsqrt(_head_mean(k * k, seg) + EPS) * gains[:, SB_WIDTH:2 * SB_WIDTH]
    return proj, qn, kn, v


def _qk_norm_bwd(x, g, dy, seg):
    r = lax.rsqrt(_head_mean(x * x, seg) + EPS)
    gdy = g * dy
    dx = r * gdy - x * (r * r * r) * _head_mean(gdy * x, seg)
    return dx, jnp.sum(dy * x * r, axis=0, keepdims=True)


def _split_heads(refs, scratch, L):
    def chunk(i, _):
        r = pl.ds(pl.multiple_of(i * QBLOCK, QBLOCK), QBLOCK)
        for ref, s in zip(refs, scratch):
            for h in range(2):
                s[h, r, :] = ref[r, _head(h)]
        return 0

    lax.fori_loop(0, L // QBLOCK, chunk, 0)


Q_HALVES = KBLOCK // QBLOCK
_CHAINS = [(h, r) for h in range(2) for r in range(Q_HALVES)]


def _valid(i, kb):
    row = lax.broadcasted_iota(jnp.int32, (QBLOCK, KBLOCK), 0)
    col = lax.broadcasted_iota(jnp.int32, (QBLOCK, KBLOCK), 1)
    return col + (kb * KBLOCK - i * QBLOCK) < row


def _attn_fwd(qn, kn, vb, B, L):
    n_pairs = L // KBLOCK
    n_hp = N_HEADS // 2
    nc = len(_CHAINS)

    def body(q_ref, k_ref, v_ref, o_ref, a_ref, q_s, k_s, v_s, after_s, z_s, stage_s, sems):
        _split_heads((q_ref, k_ref, v_ref), (q_s, k_s, v_s), L)
        r2 = lax.broadcasted_iota(jnp.int32, (KBLOCK, KBLOCK), 0)
        c2 = lax.broadcasted_iota(jnp.int32, (KBLOCK, KBLOCK), 1)
        after_s[...] = (r2 > c2).astype(after_s.dtype)
        g = pl.program_id(0) * n_hp + pl.program_id(1)

        def q_pair(p, _):
            rows = [pl.ds(pl.multiple_of((p * Q_HALVES + r) * QBLOCK, QBLOCK), QBLOCK) for r in range(Q_HALVES)]
            q_c = [q_s[h, rows[r], :] for h, r in _CHAINS]
            cs = range(nc)

            def scores(kb):
                rk = pl.ds(pl.multiple_of(kb * KBLOCK, KBLOCK), KBLOCK)
                return [_dot(q_c[c], k_s[_CHAINS[c][0], rk, :], _NT) for c in cs]

            def saved(kb):
                return pltpu.make_async_copy(stage_s.at[kb & 1], a_ref.at[g, p, kb], sems.at[kb & 1])

            def k_block(kb, carry, diagonal):
                rk = pl.ds(pl.multiple_of(kb * KBLOCK, KBLOCK), KBLOCK)
                if diagonal:
                    valid = [_valid(p * Q_HALVES + r, kb) for r in range(Q_HALVES)]
                    keep = lambda c, t: jnp.where(valid[_CHAINS[c][1]], t, 0.0)
                    z = scores(kb)
                else:
                    keep = lambda c, t: t
                    z = [z_s[(kb + 1) & 1, c] for c in cs]
                ahead = scores(jnp.maximum(kb - 1, 0))
                for c in cs:
                    z_s[kb & 1, c] = ahead[c]
                sp = [_softplus(z[c]) for c in cs]
                lsig = [z[c] - sp[c] for c in cs]
                lom = [keep(c, -sp[c]) for c in cs]
                tail = [_split_dot(lom[c], after_s[...]) + carry[c][0] for c in cs]
                a = [keep(c, jnp.exp(lsig[c] + tail[c])).astype(v_s.dtype) for c in cs]
                acc = [carry[c][1] + _dot(a[c], v_s[_CHAINS[c][0], rk, :]) for c in cs]
                for c in cs:
                    stage_s[kb & 1, c] = a[c]
                saved(kb).start()
                return tuple((carry[c][0] + jnp.sum(lom[c], axis=1, keepdims=True), acc[c]) for c in cs)

            def next_block(n, carry):
                kb = p - n

                @pl.when(n >= 2)
                def _():
                    saved(kb + 2).wait()

                return k_block(kb, carry, False)

            init = (jnp.zeros((QBLOCK, 1), F32), jnp.zeros((QBLOCK, HEAD_DIM), F32))
            first = k_block(p, (init,) * nc, True)
            res = lax.fori_loop(1, p + 1, next_block, first)
            saved(0).wait()

            @pl.when(p >= 1)
            def _():
                saved(1).wait()

            for r in range(Q_HALVES):
                o_ref[rows[r], :] = jnp.concatenate([res[c][1] for c in cs if _CHAINS[c][1] == r], axis=1)
            return 0

        lax.fori_loop(0, n_pairs, q_pair, 0)

    spec = pl.BlockSpec((L, LANES), lambda b, p: (b, p))
    return pl.pallas_call(
        body, name="attn_fwd", grid=(B, n_hp),
        in_specs=[spec] * 3, out_specs=[spec, _ANY],
        out_shape=[jax.ShapeDtypeStruct((B * L, SB_WIDTH), F32),
                   jax.ShapeDtypeStruct((B * n_hp, n_pairs, n_pairs, nc, QBLOCK, KBLOCK), BF16)],
        scratch_shapes=[pltpu.VMEM((2, L, HEAD_DIM), BF16)] * 3 + [pltpu.VMEM((KBLOCK, KBLOCK), BF16)]
        + [pltpu.VMEM((2, nc, QBLOCK, KBLOCK), F32), pltpu.VMEM((2, nc, QBLOCK, KBLOCK), BF16),
           pltpu.SemaphoreType.DMA((2,))],
        compiler_params=_params(("parallel", "parallel")),
    )(qn, kn, vb)


def _attn_bwd(qn, kn, vb, kept_a, d_sb, proj, gq2, gk2, B, L):
    n_pairs = L // KBLOCK
    n_hp = N_HEADS // 2
    nc = len(_CHAINS)
    slots = 3
    scale = 1.0 / math.sqrt(HEAD_DIM)

    def body(q_ref, k_ref, v_ref, do_ref, qraw_ref, kraw_ref, gq_ref, gk_ref, a_ref,
             dq_ref, dk_ref, dv_ref, dgq_ref, dgk_ref,
             q_s, k_s, v_s, qt_s, dkt_s, dvt_s, before_s, stage_s, sems):
        _split_heads((q_ref, k_ref, v_ref), (q_s, k_s, v_s), L)
        g = pl.program_id(0) * n_hp + pl.program_id(1)
        lane_head = [lax.div(lax.broadcasted_iota(jnp.int32, (LANES, LANES), d), HEAD_DIM) for d in (0, 1)]
        seg = (lane_head[0] == lane_head[1]).astype(BF16)

        def transpose_q(i, _):
            r = pl.ds(pl.multiple_of(i * QBLOCK, QBLOCK), QBLOCK)
            qt_s[:, r] = q_ref[r, :].astype(F32).T.astype(qt_s.dtype)
            return 0

        lax.fori_loop(0, L // QBLOCK, transpose_q, 0)
        dkt_s[...] = jnp.zeros_like(dkt_s)
        dvt_s[...] = jnp.zeros_like(dvt_s)
        r2 = lax.broadcasted_iota(jnp.int32, (KBLOCK, KBLOCK), 0)
        c2 = lax.broadcasted_iota(jnp.int32, (KBLOCK, KBLOCK), 1)
        before_s[...] = (r2 < c2).astype(before_s.dtype)

        def q_pair(p, dgq):
            rows = [pl.ds(pl.multiple_of((p * Q_HALVES + r) * QBLOCK, QBLOCK), QBLOCK) for r in range(Q_HALVES)]
            pair = pl.ds(pl.multiple_of(p * KBLOCK, KBLOCK), KBLOCK)
            do2 = do_ref[pair, :]
            do_t = do2.T.astype(v_s.dtype)
            cs = range(nc)
            hs = range(2)
            q_c = [q_s[h, rows[r], :] for h, r in _CHAINS]
            do_c = [do2[r * QBLOCK:(r + 1) * QBLOCK, _head(h)].astype(v_s.dtype) for h, r in _CHAINS]
            qt_h = [qt_s[_head(h), pair] for h in hs]
            dot_h = [do_t[_head(h), :] for h in hs]

            def kept(kb):
                slot = lax.rem(kb, slots)
                return pltpu.make_async_copy(a_ref.at[g, p, kb], stage_s.at[slot], sems.at[slot])

            def k_block(kb, carry, diagonal):
                rk = pl.ds(pl.multiple_of(kb * KBLOCK, KBLOCK), KBLOCK)
                if diagonal:
                    valid = [_valid(p * Q_HALVES + r, kb) for r in range(Q_HALVES)]
                    keep = lambda c, t: jnp.where(valid[_CHAINS[c][1]], t, 0.0)
                else:
                    keep = lambda c, t: t

                    @pl.when(kb + 2 <= p)
                    def _():
                        kept(kb + 2).start()

                kept(kb).wait()
                slot = lax.rem(kb, slots)
                k_b = [k_s[h, rk, :] for h in hs]
                z = [_dot(q_c[c], k_b[_CHAINS[c][0]], _NT) for c in cs]
                da = [_dot(do_c[c], v_s[_CHAINS[c][0], rk, :], _NT) for c in cs]
                a = [stage_s[slot, c] for c in cs]
                dla = [a[c].astype(F32) * da[c] for c in cs]
                for h in hs:
                    a_h = jnp.concatenate([a[c] for c in cs if _CHAINS[c][0] == h], axis=0)
                    dvt_s[_head(h), rk] += _dot(dot_h[h], a_h)
                d_lom = [carry[c][0] + _split_dot(dla[c], before_s[...]) for c in cs]
                beta = [jax.nn.sigmoid(z[c]) for c in cs]
                dz_b = [(dla[c] * (1.0 - beta[c]) - keep(c, beta[c] * d_lom[c])).astype(v_s.dtype) for c in cs]
                dq_acc = [carry[c][1] + _dot(dz_b[c], k_b[_CHAINS[c][0]]) for c in cs]
                for h in hs:
                    dz_h = jnp.concatenate([dz_b[c] for c in cs if _CHAINS[c][0] == h], axis=0)
                    dkt_s[_head(h), rk] += _dot(qt_h[h], dz_h)
                return tuple((carry[c][0] + jnp.sum(dla[c], axis=1, keepdims=True), dq_acc[c]) for c in cs)

            init = (jnp.zeros((QBLOCK, 1), F32), jnp.zeros((QBLOCK, HEAD_DIM), F32))
            kept(0).start()

            @pl.when(p >= 1)
            def _():
                kept(1).start()

            before = lax.fori_loop(0, p, lambda kb, carry: k_block(kb, carry, False), (init,) * nc)
            res = k_block(p, before, True)
            for r in range(Q_HALVES):
                d_qn = jnp.concatenate([res[c][1] for c in cs if _CHAINS[c][1] == r], axis=1) * scale
                dq, dg = _qk_norm_bwd(qraw_ref[rows[r], :], gq_ref[...], d_qn, seg)
                dq_ref[rows[r], :] = dq.astype(dq_ref.dtype)
                dgq = dgq + dg
            return dgq

        dgq = lax.fori_loop(0, n_pairs, q_pair, jnp.zeros((1, LANES), F32))

        def transpose_out(i, dgk):
            r = pl.ds(pl.multiple_of(i * QBLOCK, QBLOCK), QBLOCK)
            dk, dg = _qk_norm_bwd(kraw_ref[r, :], gk_ref[...], dkt_s[:, r].T, seg)
            dk_ref[r, :] = dk.astype(dk_ref.dtype)
            dv_ref[r, :] = dvt_s[:, r].T.astype(dv_ref.dtype)
            return dgk + dg

        dgk = lax.fori_loop(0, L // QBLOCK, transpose_out, jnp.zeros((1, LANES), F32))

        @pl.when(jnp.logical_and(pl.program_id(0) == 0, pl.program_id(1) == 0))
        def _():
            dgq_ref[...] = jnp.zeros_like(dgq_ref)
            dgk_ref[...] = jnp.zeros_like(dgk_ref)

        dgq_ref[...] += dgq[:, _head(0)] + dgq[:, _head(1)]
        dgk_ref[...] += dgk[:, _head(0)] + dgk[:, _head(1)]

    spec = pl.BlockSpec((L, LANES), lambda b, p: (b, p))
    gain = pl.BlockSpec((1, LANES), lambda b, p: (0, 0))
    gain_grad = pl.BlockSpec((1, HEAD_DIM), lambda b, p: (0, 0))
    return pl.pallas_call(
        body, name="attn_bwd", grid=(B, n_hp),
        in_specs=[spec] * 4 + [spec, pl.BlockSpec((L, LANES), lambda b, p: (b, n_hp + p)), gain, gain, _ANY],
        out_specs=[spec] * 3 + [gain_grad] * 2,
        out_shape=[jax.ShapeDtypeStruct((B * L, SB_WIDTH), BF16)] * 3 + [jax.ShapeDtypeStruct((1, HEAD_DIM), F32)] * 2,
        scratch_shapes=[pltpu.VMEM((2, L, HEAD_DIM), BF16)] * 3 + [pltpu.VMEM((LANES, L), BF16)]
        + [pltpu.VMEM((LANES, L), F32)] * 2 + [pltpu.VMEM((KBLOCK, KBLOCK), BF16)]
        + [pltpu.VMEM((slots, nc, QBLOCK, KBLOCK), BF16), pltpu.SemaphoreType.DMA((slots,))],
        compiler_params=_params(("arbitrary", "arbitrary")),
    )(qn, kn, vb, d_sb, proj, proj, gq2, gk2, kept_a)


def _ssm_discretise(lam_re, lam_im, log_dt, b_re, b_im):
    dt = jnp.exp(log_dt)
    mag = jnp.exp(lam_re * dt)
    lbr = mag * jnp.cos(lam_im * dt)
    lbi = mag * jnp.sin(lam_im * dt)
    den = lam_re * lam_re + lam_im * lam_im
    nr, ni = lbr - 1.0, lbi
    cr = (nr * lam_re + ni * lam_im) / den
    ci = (ni * lam_re - nr * lam_im) / den
    return lbr, lbi, cr * b_re - ci * b_im, cr * b_im + ci * b_re


def _ssm_prep(lam_re, lam_im, log_dt, b_re_t, b_im_t):
    def body(lr, li, ld, br, bi, o_lr, o_li, o_br, o_bi):
        res = _ssm_discretise(lr[...], li[...], ld[...], br[...], bi[...])
        for o, v in zip((o_lr, o_li, o_br, o_bi), res):
            o[...] = v

    return pl.pallas_call(
        body, name="ssm_prep",
        out_shape=[jax.ShapeDtypeStruct(lam_re.shape, F32)] * 2 + [jax.ShapeDtypeStruct(b_re_t.shape, F32)] * 2,
    )(lam_re, lam_im, log_dt, b_re_t, b_im_t)


def _ssm_prep_bwd(lam_re, lam_im, log_dt, b_re_t, b_im_t, d_lr, d_li, d_br, d_bi):
    def body(lr, li, ld, br, bi, g_lr, g_li, g_br, g_bi, o_lr, o_li, o_ld, o_br, o_bi):
        _, vjp = jax.vjp(_ssm_discretise, lr[...], li[...], ld[...], br[...], bi[...])
        res = vjp((g_lr[...], g_li[...], g_br[...], g_bi[...]))
        for o, v in zip((o_lr, o_li, o_ld, o_br, o_bi), res):
            o[...] = v

    return pl.pallas_call(
        body, name="ssm_prep_bwd",
        out_shape=[jax.ShapeDtypeStruct(lam_re.shape, F32)] * 2 + [jax.ShapeDtypeStruct(log_dt.shape, F32)]
        + [jax.ShapeDtypeStruct(b_re_t.shape, F32)] * 2,
    )(lam_re, lam_im, log_dt, b_re_t, b_im_t, d_lr, d_li, d_br, d_bi)


def _block_diag(m):
    m4 = m.reshape(SSM_COLS, 8, SSM_GROUP, SSM_STATE)
    return jnp.einsum("aghp,gk->aghkp", m4, jnp.eye(8, dtype=m.dtype)).reshape(SSM_COLS, LANES, 512)


def _block_diag_take(d):
    d6 = d.reshape(SSM_COLS, 8, SSM_GROUP, 2, 8, SSM_STATE)
    return jnp.einsum("aghrgp->raghp", d6).reshape(2, SSM_GROUPS, SSM_GROUP, SSM_STATE)


def _cmul(ar, ai, br, bi):
    return ar * br - ai * bi, ar * bi + ai * br


def _power(lr, li, n):
    assert n & (n - 1) == 0
    for _ in range(n.bit_length() - 1):
        lr, li = _cmul(lr, li, lr, li)
    return lr, li


def _ssm_fwd(u_p, w_b, lam_r, lam_i, c_m, d_skip, B, L, tj):
    J = L // N_CHUNK
    njt = J // tj
    R = tj * N_CHUNK
    H = 512

    def body(u_ref, wb_ref, lr_ref, li_ref, cm_ref, d_ref, y_ref, gel_ref, x_ref, xin_ref, bu_s, st_s, xin_s):
        ph, jt = pl.program_id(2), pl.program_id(3)
        lr, li = lr_ref[...], li_ref[...]

        @pl.when(jnp.logical_and(ph == 0, jt == 0))
        def _():
            st_s[...] = jnp.zeros_like(st_s)

        @pl.when(ph == 0)
        def _():
            bu_s[jt] = _dot(u_ref[...].astype(BF16), wb_ref[...].astype(BF16))

        def scan(store):
            def step(j, carry):
                xr, xi = carry
                r = pl.ds(pl.multiple_of(j * N_CHUNK, N_CHUNK), N_CHUNK)
                nr = lr * xr - li * xi + bu_s[jt, r, 0:H]
                ni = lr * xi + li * xr + bu_s[jt, r, H:2 * H]
                if store:
                    x_ref[r, 0:H] = nr
                    x_ref[r, H:2 * H] = ni
                return nr, ni

            xr, xi = lax.fori_loop(0, tj, step, (st_s[:, 0:H], st_s[:, H:2 * H]))
            st_s[:, 0:H] = xr
            st_s[:, H:2 * H] = xi

        @pl.when(ph == 0)
        def _():
            scan(False)

            @pl.when(jt == njt - 1)
            def _():
                pr, pi = _power(lr[0:1], li[0:1], J)
                xin_s[0:1, :] = jnp.zeros((1, 2 * H), F32)
                for c in range(1, N_CHUNK):
                    qr, qi = _cmul(pr, pi, xin_s[c - 1:c, 0:H], xin_s[c - 1:c, H:2 * H])
                    xin_s[c:c + 1, 0:H] = qr + st_s[c - 1:c, 0:H]
                    xin_s[c:c + 1, H:2 * H] = qi + st_s[c - 1:c, H:2 * H]
                xin_ref[...] = xin_s[...]
                st_s[...] = xin_s[...]

        @pl.when(ph == 1)
        def _():
            scan(True)
            y = _dot(x_ref[...].astype(BF16), cm_ref[...].astype(BF16)) + d_ref[...] * u_ref[...]
            y_ref[...] = y
            gel_ref[...] = jax.nn.gelu(y).astype(gel_ref.dtype)

    return pl.pallas_call(
        body, name="ssm_fwd", grid=(SSM_COLS, B, 2, njt),
        in_specs=[
            pl.BlockSpec((None, R, LANES), lambda i, b, ph, jt: (b, jt, i)),
            pl.BlockSpec((None, LANES, 2 * H), lambda i, b, ph, jt: (i, 0, 0)),
            pl.BlockSpec((None, N_CHUNK, H), lambda i, b, ph, jt: (i, 0, 0)),
            pl.BlockSpec((None, N_CHUNK, H), lambda i, b, ph, jt: (i, 0, 0)),
            pl.BlockSpec((None, 2 * H, LANES), lambda i, b, ph, jt: (i, 0, 0)),
            pl.BlockSpec((1, LANES), lambda i, b, ph, jt: (0, i)),
        ],
        out_specs=[
            pl.BlockSpec((None, R, LANES), lambda i, b, ph, jt: (b, jt * ph, i)),
            pl.BlockSpec((None, R, LANES), lambda i, b, ph, jt: (b, jt * ph, i)),
            pl.BlockSpec((None, R, 2 * H), lambda i, b, ph, jt: (b, jt * ph, i)),
            pl.BlockSpec((None, None, N_CHUNK, 2 * H), lambda i, b, ph, jt: (b, i, 0, 0)),
        ],
        out_shape=[
            jax.ShapeDtypeStruct((B, L, SSM_WIDTH), F32),
            jax.ShapeDtypeStruct((B, L, SSM_WIDTH), BF16),
            jax.ShapeDtypeStruct((B, L, SSM_COLS * 2 * H), F32),
            jax.ShapeDtypeStruct((B, SSM_COLS, N_CHUNK, 2 * H), F32),
        ],
        scratch_shapes=[pltpu.VMEM((njt, R, 2 * H), F32), pltpu.VMEM((N_CHUNK, 2 * H), F32),
                        pltpu.VMEM((N_CHUNK, 2 * H), F32)],
        compiler_params=_params(("arbitrary",) * 4),
    )(u_p, w_b, lam_r, lam_i, c_m, d_skip)


def _ssm_bwd(dy_p, u_p, x, xin, w_bt, lam_r, lam_i, c_mt, d_skip, B, L, tj):
    J = L // N_CHUNK
    njt = J // tj
    R = tj * N_CHUNK
    H = 512
    x4 = x.reshape(B, J, N_CHUNK, SSM_COLS * 2 * H)

    def body(dy_ref, u_ref, x_ref, xp_ref, xin_ref, wbt_ref, lr_ref, li_ref, cmt_ref, d_ref,
             du_ref, dwb_ref, dcm_ref, dlr_ref, dli_ref, dd_ref, ca_s, a_s, st_s, dl_s):
        b, ph, jt = pl.program_id(1), pl.program_id(2), pl.program_id(3)
        jr = njt - 1 - jt
        lr, li = lr_ref[...], -li_ref[...]

        @pl.when(jnp.logical_and(b == 0, jnp.logical_and(ph == 0, jt == 0)))
        def _():
            dwb_ref[...] = jnp.zeros_like(dwb_ref)
            dcm_ref[...] = jnp.zeros_like(dcm_ref)
            dlr_ref[...] = jnp.zeros_like(dlr_ref)
            dli_ref[...] = jnp.zeros_like(dli_ref)
            dd_ref[...] = jnp.zeros_like(dd_ref)
            dl_s[...] = jnp.zeros_like(dl_s)

        @pl.when(jnp.logical_and(ph == 0, jt == 0))
        def _():
            st_s[...] = jnp.zeros_like(st_s)

        @pl.when(ph == 0)
        def _():
            ca_s[jt] = _dot(dy_ref[...].astype(BF16), cmt_ref[...].astype(BF16))

        def scan(store):
            def step(n, carry):
                ar, ai = carry
                r = pl.ds(pl.multiple_of((tj - 1 - n) * N_CHUNK, N_CHUNK), N_CHUNK)
                nr = lr * ar - li * ai + ca_s[jt, r, 0:H]
                ni = lr * ai + li * ar + ca_s[jt, r, H:2 * H]
                if store:
                    a_s[r, 0:H] = nr
                    a_s[r, H:2 * H] = ni
                return nr, ni

            ar, ai = lax.fori_loop(0, tj, step, (st_s[:, 0:H], st_s[:, H:2 * H]))
            st_s[:, 0:H] = ar
            st_s[:, H:2 * H] = ai

        @pl.when(ph == 0)
        def _():
            scan(False)

            @pl.when(jt == njt - 1)
            def _():
                pr, pi = _power(lr[0:1], li[0:1], J)
                a_s[N_CHUNK - 1:N_CHUNK, :] = jnp.zeros((1, 2 * H), F32)
                for c in range(N_CHUNK - 2, -1, -1):
                    qr, qi = _cmul(pr, pi, a_s[c + 1:c + 2, 0:H], a_s[c + 1:c + 2, H:2 * H])
                    a_s[c:c + 1, 0:H] = qr + st_s[c + 1:c + 2, 0:H]
                    a_s[c:c + 1, H:2 * H] = qi + st_s[c + 1:c + 2, H:2 * H]
                st_s[...] = a_s[0:N_CHUNK, :]

        @pl.when(ph == 1)
        def _():
            scan(True)
            dy = dy_ref[...]
            u = u_ref[...]
            a_b = a_s[...].astype(BF16)
            du_ref[...] = (_dot(a_b, wbt_ref[...].astype(BF16)) + d_ref[...] * dy).astype(du_ref.dtype)
            dwb_ref[...] += _dot(u.astype(BF16), a_b, _TN)
            dcm_ref[...] += _dot(x_ref[...].astype(BF16), dy.astype(BF16), _TN)
            dd_ref[...] += jnp.sum(dy * u, axis=0, keepdims=True)

            first = jnp.where(jr == 0, xin_ref[...], xp_ref[...])
            a0r, a0i = a_s[0:N_CHUNK, 0:H], a_s[0:N_CHUNK, H:2 * H]
            acc0 = (a0r * first[:, 0:H] + a0i * first[:, H:2 * H], a0i * first[:, 0:H] - a0r * first[:, H:2 * H])

            def step(j, carry):
                sr, si = carry
                r = pl.ds(pl.multiple_of(j * N_CHUNK, N_CHUNK), N_CHUNK)
                rp = pl.ds(pl.multiple_of((j - 1) * N_CHUNK, N_CHUNK), N_CHUNK)
                ar, ai = a_s[r, 0:H], a_s[r, H:2 * H]
                xr, xi = x_ref[rp, 0:H], x_ref[rp, H:2 * H]
                return sr + ar * xr + ai * xi, si + ai * xr - ar * xi

            sr, si = lax.fori_loop(1, tj, step, acc0)
            dl_s[:, 0:H] += sr
            dl_s[:, H:2 * H] += si

            @pl.when(jnp.logical_and(b == B - 1, jt == njt - 1))
            def _():
                dlr_ref[...] = jnp.sum(dl_s[:, 0:H], axis=0, keepdims=True)
                dli_ref[...] = jnp.sum(dl_s[:, H:2 * H], axis=0, keepdims=True)
                dl_s[...] = jnp.zeros_like(dl_s)

    rev = lambda ph, jt: (njt - 1 - jt) * ph + (njt - 1) * (1 - ph)
    return pl.pallas_call(
        body, name="ssm_bwd", grid=(SSM_COLS, B, 2, njt),
        in_specs=[
            pl.BlockSpec((None, R, LANES), lambda i, b, ph, jt: (b, njt - 1 - jt, i)),
            pl.BlockSpec((None, R, LANES), lambda i, b, ph, jt: (b, njt - 1 - jt, i)),
            pl.BlockSpec((None, R, 2 * H), lambda i, b, ph, jt: (b, rev(ph, jt), i)),
            pl.BlockSpec((None, None, N_CHUNK, 2 * H),
                         lambda i, b, ph, jt: (b, jnp.maximum((njt - 1 - jt) * tj - 1, 0), 0, i)),
            pl.BlockSpec((None, None, N_CHUNK, 2 * H), lambda i, b, ph, jt: (b, i, 0, 0)),
            pl.BlockSpec((None, 2 * H, LANES), lambda i, b, ph, jt: (i, 0, 0)),
            pl.BlockSpec((None, N_CHUNK, H), lambda i, b, ph, jt: (i, 0, 0)),
            pl.BlockSpec((None, N_CHUNK, H), lambda i, b, ph, jt: (i, 0, 0)),
            pl.BlockSpec((None, LANES, 2 * H), lambda i, b, ph, jt: (i, 0, 0)),
            pl.BlockSpec((1, LANES), lambda i, b, ph, jt: (0, i)),
        ],
        out_specs=[
            pl.BlockSpec((None, R, LANES), lambda i, b, ph, jt: (b, rev(ph, jt), i)),
            pl.BlockSpec((None, LANES, 2 * H), lambda i, b, ph, jt: (i, 0, 0)),
            pl.BlockSpec((None, 2 * H, LANES), lambda i, b, ph, jt: (i, 0, 0)),
            pl.BlockSpec((None, 1, H), lambda i, b, ph, jt: (i, 0, 0)),
            pl.BlockSpec((None, 1, H), lambda i, b, ph, jt: (i, 0, 0)),
            pl.BlockSpec((1, LANES), lambda i, b, ph, jt: (0, i)),
        ],
        out_shape=[
            jax.ShapeDtypeStruct((B, L, SSM_WIDTH), BF16),
            jax.ShapeDtypeStruct((SSM_COLS, LANES, 2 * H), F32),
            jax.ShapeDtypeStruct((SSM_COLS, 2 * H, LANES), F32),
            jax.ShapeDtypeStruct((SSM_COLS, 1, H), F32),
            jax.ShapeDtypeStruct((SSM_COLS, 1, H), F32),
            jax.ShapeDtypeStruct((1, SSM_WIDTH), F32),
        ],
        scratch_shapes=[pltpu.VMEM((njt, R, 2 * H), F32), pltpu.VMEM((R, 2 * H), F32),
                        pltpu.VMEM((N_CHUNK, 2 * H), F32), pltpu.VMEM((N_CHUNK, 2 * H), F32)],
        compiler_params=_params(("arbitrary",) * 4),
    )(dy_p, u_p, x, x4, xin, w_bt, lam_r, lam_i, c_mt, d_skip)


def _to_scan_layout(t, B, L):
    C = t.shape[-1]
    return t.reshape(B, N_CHUNK, L // N_CHUNK, C).transpose(0, 2, 1, 3).reshape(B, L, C)


def _from_scan_layout(t, B, L):
    C = t.shape[-1]
    return t.reshape(B, L // N_CHUNK, N_CHUNK, C).transpose(0, 2, 1, 3).reshape(B * L, C)


def _local_step(x, target, p, w_in, late_weights, mlp_grads_ready=None, rest_grads_ready=None, order=None, *,
                ssm_tile=256):
    B, L, D = x.shape
    T = B * L
    x2 = x.reshape(T, D)
    row = lambda v: v.reshape(1, -1)
    g1, g2, ga, gs, b_glu = row(p["norm1_g"]), row(p["norm2_g"]), row(p["attn_out_g"]), row(p["ssm_out_g"]), row(p["b_glu"])
    g1_first = g1 if order is None else g1 + order
    gq8 = jnp.tile(row(p["q_norm_g"]), (1, N_HEADS))
    gk8 = jnp.tile(row(p["k_norm_g"]), (1, N_HEADS))

    G, P, Hh = SSM_GROUPS, SSM_STATE, SSM_GROUP
    lam_re3, lam_im3 = p["ssm_lambda_re"].reshape(G, 1, P), p["ssm_lambda_im"].reshape(G, 1, P)
    log_dt3 = p["ssm_log_dt"].reshape(G, 1, 1)
    b_re_t, b_im_t = p["ssm_b_re"].transpose(0, 2, 1), p["ssm_b_im"].transpose(0, 2, 1)
    lbr, lbi, bbr, bbi = _ssm_prep(lam_re3, lam_im3, log_dt3, b_re_t, b_im_t)
    w_b = jnp.concatenate([_block_diag(bbr), _block_diag(bbi)], axis=2)
    c_mt = jnp.concatenate([_block_diag(p["ssm_c_re"]), -_block_diag(p["ssm_c_im"])], axis=2)
    w_bt, c_m = w_b.transpose(0, 2, 1), c_mt.transpose(0, 2, 1)
    lam_r = jnp.broadcast_to(lbr.reshape(SSM_COLS, 1, 512), (SSM_COLS, N_CHUNK, 512))
    lam_i = jnp.broadcast_to(lbi.reshape(SSM_COLS, 1, 512), (SSM_COLS, N_CHUNK, 512))
    d_skip = p["ssm_d"].reshape(1, SSM_WIDTH)

    qk_gains = jnp.concatenate([gq8 * (1.0 / math.sqrt(HEAD_DIM)), gk8, jnp.ones((1, 2 * SB_WIDTH), F32)], axis=1)
    head = (BF16, SB_WIDTH)
    proj, qn, kn, vb, xn = _matmul("proj", x2, w_in, prologue=(_rms, g1_first), extras=[qk_gains],
                                   out_dtypes=(F32, head, head, head), epilogue=_qk_norm, tm=512, tn=w_in.shape[1])
    sb, attn_kept = _attn_fwd(qn, kn, vb, B, L)
    u_p = _to_scan_layout(proj[:, 3 * SB_WIDTH:], B, L)
    y_p, gel_p, xs, xin = _ssm_fwd(u_p, w_b, lam_r, lam_i, c_m, d_skip, B, L, ssm_tile)
    y2, gel = y_p.reshape(T, SSM_WIDTH), gel_p.reshape(T, SSM_WIDTH)
    gel, sb = lax.optimization_barrier((gel, sb))
    w_glu, w_out, w_mlp_in, w_mlp_out = late_weights(gel)
    pre, ssm_n = _matmul("glu_gate", gel, w_glu, extras=[y2, b_glu, gs], out_dtypes=(F32, BF16),
                         epilogue=lambda acc, y, b, g: (acc, _glu_branch(y, acc, b, g)))
    mixed = _rowwise(
        "attn_out_norm", lambda s, n, g: jnp.concatenate([_rms(s, g).astype(BF16), n.astype(BF16)], axis=1),
        [sb, _from_scan_layout(ssm_n, B, L)], [ga], [(D, BF16)])
    def residual_and_norm(acc, res, g):
        h = acc + res
        return h, _rms(h, g)

    h1, hn = _matmul("out_proj", mixed, w_out, extras=[x2, g2], out_dtypes=(F32, BF16), tn=D,
                     epilogue=residual_and_norm)
    act, a_pre = _matmul("mlp_in", hn, w_mlp_in, out_dtypes=(BF16, BF16), tn=1024,
                         epilogue=lambda acc: (jnp.square(jnp.maximum(acc, 0.0)), acc))

    def loss_fn(acc, h, t):
        diff = acc + h - t
        part = jnp.sum(jnp.sum(diff * diff, axis=0, keepdims=True), axis=1, keepdims=True)
        d = diff * (1.0 / D)
        return d, d, part * (0.5 / D)

    d_out, d_out_b, loss = _matmul("mlp_out", act, w_mlp_out, extras=[h1, target.reshape(T, D)],
                                   out_dtypes=(F32, BF16), sums=[(1, 1)], epilogue=loss_fn, tm=512, tn=D)

    d_apre = _matmul("mlp_out_dx", d_out_b, w_mlp_out, tb=True, extras=[a_pre], out_dtypes=(BF16,), tn=1024,
                     epilogue=lambda acc, ap: (acc * (2.0 * jnp.maximum(ap.astype(F32), 0.0)),))
    both = lambda acc: (acc, acc)
    g_w_mlp_out, g_w_mlp_out_b = _matmul("mlp_out_dw", act, d_out_b, ta=True, out_dtypes=(F32, BF16), epilogue=both)
    g_w_mlp_in, g_w_mlp_in_b = _matmul("mlp_in_dw", hn, d_apre, ta=True, col_blocked=True, out_dtypes=(F32, BF16),
                                       epilogue=both, tn=w_mlp_in.shape[1] // N_DEV)
    if mlp_grads_ready is not None:
        g2 = g2 + mlp_grads_ready(g_w_mlp_out, g_w_mlp_out_b, g_w_mlp_in, g_w_mlp_in_b)

    def norm_bwd_res(dy, h, res, g):
        _, vjp = jax.vjp(_rms, h, g)
        dh, dg = vjp(dy)
        return res + dh, dg

    def norm_bwd_res2(dy, h, res, g):
        d, dg = norm_bwd_res(dy, h, res, g)
        return d, d, dg

    d_h1, d_h1_b, g_norm2 = _matmul("mlp_in_dx", d_apre, w_mlp_in, tb=True, extras=[h1, d_out, g2],
                                    out_dtypes=(F32, BF16), sums=[(1, D)], epilogue=norm_bwd_res2, tm=512, tn=D)

    d_mixed = _matmul("out_proj_dx", d_h1_b, w_out, tb=True, tn=1024)
    g_w_out, g_w_out_b = _matmul("out_proj_dw", mixed, d_h1_b, ta=True, out_dtypes=(F32, BF16), epilogue=both)

    def norm_bwd(h, dy, g):
        _, vjp = jax.vjp(_rms, h, g)
        return vjp(dy)

    d_sb, g_attn_out = _rowwise("attn_out_norm_bwd", norm_bwd, [sb, (d_mixed, 0, SB_WIDTH)], [ga],
                                [(SB_WIDTH, F32)], sums=[(1, SB_WIDTH)])
    d_ssm_n = _to_scan_layout(d_mixed[:, SB_WIDTH:], B, L).reshape(T, SSM_WIDTH)

    def glu_bwd(y, pre_, dy, bg, g):
        _, vjp = jax.vjp(_glu_branch, y, pre_, bg, g)
        d_y, d_pre, d_bg, d_g = vjp(dy)
        return d_y, d_pre, d_bg, d_g

    d_y_direct, d_pre, g_b_glu, g_ssm_out = _rowwise(
        "glu_out_bwd", glu_bwd, [y2, pre, d_ssm_n], [b_glu, gs], [(SSM_WIDTH, F32), (SSM_WIDTH, BF16)],
        sums=[(1, SSM_WIDTH), (1, SSM_WIDTH)])
    g_w_glu, g_w_glu_b = _matmul("glu_gate_dw", gel, d_pre, ta=True, out_dtypes=(F32, BF16), epilogue=both)

    def gelu_bwd(dg, y, dy0):
        _, vjp = jax.vjp(jax.nn.gelu, y)
        return (dy0 + vjp(dg)[0],)

    d_y = _matmul("glu_gate_dx", d_pre, w_glu, tb=True, extras=[y2, d_y_direct], epilogue=gelu_bwd)

    du_p, d_wb, d_cm, d_lr, d_li, g_d = _ssm_bwd(
        d_y.reshape(B, L, SSM_WIDTH), u_p, xs, xin, w_bt, lam_r, lam_i, c_mt, d_skip, B, L, ssm_tile)
    d_bb = _block_diag_take(d_wb.reshape(SSM_COLS, LANES, 2, 512))
    d_c = _block_diag_take(d_cm.transpose(0, 2, 1).reshape(SSM_COLS, LANES, 2, 512))
    g_lam_re, g_lam_im, g_log_dt, g_b_re_t, g_b_im_t = _ssm_prep_bwd(
        lam_re3, lam_im3, log_dt3, b_re_t, b_im_t,
        d_lr.reshape(G, 1, P), d_li.reshape(G, 1, P), d_bb[0], d_bb[1])
    d_q, d_k, d_v, g_q, g_k = _attn_bwd(qn, kn, vb, attn_kept, d_sb, proj, gq8[:, :LANES], gk8[:, :LANES], B, L)

    d_proj = jnp.concatenate([d_q, d_k, d_v, _from_scan_layout(du_p, B, L)], axis=1)
    g_w_in, g_w_in_b = _matmul("proj_dw", xn, d_proj, ta=True, col_blocked=True, out_dtypes=(F32, BF16),
                               epilogue=both, tn=w_in.shape[1] // N_DEV)
    if rest_grads_ready is not None:
        g1 = g1 + rest_grads_ready([g_w_in, g_w_glu, g_w_out], [g_w_in_b, g_w_glu_b, g_w_out_b])
    grad_x, g_norm1 = _matmul("proj_dx", d_proj, w_in, tb=True, extras=[x2, d_h1, g1], sums=[(1, D)],
                              epilogue=norm_bwd_res, tm=512, tn=D)

    small = {
        "norm1_g": g_norm1.reshape(-1),
        "q_norm_g": g_q.reshape(-1),
        "k_norm_g": g_k.reshape(-1),
        "ssm_lambda_re": g_lam_re.reshape(G, P),
        "ssm_lambda_im": g_lam_im.reshape(G, P),
        "ssm_log_dt": g_log_dt.reshape(G),
        "ssm_b_re": g_b_re_t.transpose(0, 2, 1),
        "ssm_b_im": g_b_im_t.transpose(0, 2, 1),
        "ssm_c_re": d_c[0],
        "ssm_c_im": -d_c[1],
        "ssm_d": g_d.reshape(G, Hh),
        "b_glu": g_b_glu.reshape(-1),
        "attn_out_g": g_attn_out.reshape(-1),
        "ssm_out_g": g_ssm_out.reshape(-1),
        "norm2_g": g_norm2.reshape(-1),
    }
    big = {"w_in": g_w_in, "w_glu": g_w_glu, "w_out": g_w_out, "w_mlp_in": g_w_mlp_in, "w_mlp_out": g_w_mlp_out}
    return loss[0, 0], grad_x.reshape(B, L, D), small, big


_ANY = pl.BlockSpec(memory_space=pl.ANY)
_MESH = pl.DeviceIdType.MESH


def _all_gather(name, shards):
    n = len(shards)

    def body(*refs):
        in_refs, out_refs = refs[:n], refs[n:2 * n]
        send_sems, recv_sems, local_sems = refs[2 * n:]
        x, y, c = lax.axis_index("x"), lax.axis_index("y"), lax.axis_index("c")
        me, sibling = (x, y, c), (x, y, 1 - c)
        chips = [(1 - x, y), (x, 1 - y), (1 - x, 1 - y)]

        def copy(a, k, block, to, src=None):
            px, py, pc = block
            rows = out_refs[a].at[4 * px + 2 * py + pc]
            return pltpu.make_async_remote_copy(
                src_ref=rows if src is None else src, dst_ref=rows, send_sem=send_sems.at[a, k],
                recv_sem=recv_sems.at[a, k], device_id=to, device_id_type=_MESH)

        mine = [pltpu.make_async_copy(in_refs[a], out_refs[a].at[4 * x + 2 * y + c], local_sems.at[a]) for a in range(n)]
        first, passed = [], []
        for a in range(n):
            mine[a].start()
            first.append(copy(a, 0, me, sibling, src=in_refs[a]))
            first += [copy(a, 1 + j, me, (*chip, c), src=in_refs[a]) for j, chip in enumerate(chips)]
        for cp in first:
            cp.start()
        for j, chip in enumerate(chips):
            for a in range(n):
                copy(a, 1 + j, (*chip, c), me).wait_recv()
                fwd = copy(a, 4 + j, (*chip, c), sibling)
                fwd.start()
                passed.append(fwd)
        for a in range(n):
            copy(a, 0, sibling, me).wait_recv()
            for j, chip in enumerate(chips):
                copy(a, 4 + j, (*chip, 1 - c), me).wait_recv()
        for cp in first + passed:
            cp.wait_send()
        for cp in mine:
            cp.wait()

    return pl.pallas_call(
        body, name=name,
        in_specs=[_ANY] * n, out_specs=[_ANY] * n,
        out_shape=[jax.ShapeDtypeStruct((N_DEV, *s.shape), s.dtype) for s in shards],
        scratch_shapes=[pltpu.SemaphoreType.DMA((n, 7)), pltpu.SemaphoreType.DMA((n, 7)), pltpu.SemaphoreType.DMA((n,))],
    )(*shards)


_HBM = pl.BlockSpec(memory_space=pltpu.HBM)
_SEM = pl.BlockSpec(memory_space=pltpu.SEMAPHORE)
_EFFECT = pltpu.SideEffectType.DATAFLOW_SIDE_EFFECTING
_FLIPS = [(dx, dy, dc) for dx in (0, 1) for dy in (0, 1) for dc in (0, 1) if (dx, dy, dc) != (0, 0, 0)]


def _exchange_start(name, srcs, lands, per_peer):
    n = len(srcs)

    def body(*refs):
        src_refs, land_refs = refs[:n], refs[n:2 * n]
        send_sems, recv_sems = refs[2 * n:3 * n], refs[3 * n:4 * n]
        token = refs[-1]
        x, y, c = lax.axis_index("x"), lax.axis_index("y"), lax.axis_index("c")
        me = 4 * x + 2 * y + c
        for dx, dy, dc in _FLIPS:
            px, py, pc = (1 - x if dx else x), (1 - y if dy else y), (1 - c if dc else c)
            for a in range(n):
                pltpu.make_async_remote_copy(
                    src_ref=src_refs[a].at[4 * px + 2 * py + pc] if per_peer else src_refs[a],
                    dst_ref=land_refs[a].at[me], send_sem=send_sems[a], recv_sem=recv_sems[a],
                    device_id=(px, py, pc), device_id_type=_MESH).start()
        token[...] = jnp.zeros_like(token)

    hbm = lambda t: pltpu.with_memory_space_constraint(t, pltpu.HBM)
    res = pl.pallas_call(
        body, name=name,
        out_shape=(*[pltpu.SemaphoreType.DMA(())] * (2 * n), *[pltpu.HBM(t.shape, t.dtype) for t in (*srcs, *lands)],
                   jax.ShapeDtypeStruct((8, LANES), F32)),
        in_specs=[_HBM] * (2 * n),
        out_specs=(*[_SEM] * (2 * n), *[_HBM] * (2 * n), pl.BlockSpec(memory_space=pltpu.VMEM)),
        input_output_aliases={i: 2 * n + i for i in range(2 * n)},
        compiler_params=pltpu.CompilerParams(has_side_effects=_EFFECT),
    )(*[hbm(t) for t in (*srcs, *lands)])
    return res[:-1], res[-1]


def _exchange_wait(name, handle, after):
    n = len(handle) // 4
    sems, thru = handle[:2 * n], handle[2 * n:]

    def body(*refs):
        land_refs = refs[n:2 * n]
        send_sems, recv_sems = refs[2 * n:3 * n], refs[3 * n:4 * n]
        me = (lax.axis_index("x"), lax.axis_index("y"), lax.axis_index("c"))
        for a in range(n):
            seven = land_refs[a].at[pl.ds(0, len(_FLIPS))]
            all_copies = pltpu.make_async_remote_copy(
                src_ref=seven, dst_ref=seven, send_sem=send_sems[a], recv_sem=recv_sems[a], device_id=me,
                device_id_type=_MESH)
            all_copies.wait_send()
            all_copies.wait_recv()

    res = pl.pallas_call(
        body, name=name, out_shape=tuple(pltpu.HBM(t.shape, t.dtype) for t in thru),
        in_specs=[*[_HBM] * (2 * n), *[_SEM] * (2 * n), _ANY], out_specs=tuple([_HBM] * (2 * n)),
        input_output_aliases={i: i for i in range(2 * n)},
        compiler_params=pltpu.CompilerParams(has_side_effects=_EFFECT),
    )(*thru, *sems, after)
    return res[n:]


def _adamw_gathered(name, own, parts, me, w, m, v):
    r, c = w.shape
    tr = min(r, 256)

    def body(me_ref, own_ref, p_ref, w_ref, m_ref, v_ref, g_out, d_out, m_out, v_out):
        g = own_ref[...]
        for j in range(N_DEV):
            g = g + p_ref[j].astype(F32)
        delta, m_new, v_new = _adamw(w_ref[...], g, m_ref[...], v_ref[...])
        g_out[...] = g
        d_out[...] = delta
        m_out[...] = m_new
        v_out[...] = v_new

    spec = pl.BlockSpec((tr, c), lambda i, me_ref: (i, 0))
    return pl.pallas_call(
        body, name=name,
        grid_spec=pltpu.PrefetchScalarGridSpec(
            num_scalar_prefetch=1, grid=(r // tr,),
            in_specs=[pl.BlockSpec((None, tr, c), lambda i, me_ref: (me_ref[0], i, 0)),
                      pl.BlockSpec((N_DEV, tr, c), lambda i, me_ref: (0, i, 0)), spec, spec, spec],
            out_specs=[spec] * 4),
        out_shape=[jax.ShapeDtypeStruct((r, c), F32)] * 4,
        compiler_params=_params(("parallel",)),
    )(me, own, parts, w, m, v)


def _adamw(w, g, m, v):
    m = ADAM_B1 * m + (1.0 - ADAM_B1) * g
    v = ADAM_B2 * v + (1.0 - ADAM_B2) * jnp.square(g)
    m_hat = m / (1.0 - ADAM_B1 ** ADAM_STEP)
    v_hat = v / (1.0 - ADAM_B2 ** ADAM_STEP)
    delta = -ADAM_LR * (m_hat / (jnp.sqrt(v_hat) + ADAM_EPS) + ADAM_WD * w)
    return delta, m, v


def _adamw_small(name, parts, w, m, v):
    _, r, c = parts.shape
    tr = 8

    def body(p_ref, w_ref, m_ref, v_ref, g_out, d_out, m_out, v_out):
        g = p_ref[0]
        for j in range(1, N_DEV):
            g = g + p_ref[j]
        delta, m_new, v_new = _adamw(w_ref[...], g, m_ref[...], v_ref[...])
        g_out[...] = g
        d_out[...] = delta
        m_out[...] = m_new
        v_out[...] = v_new

    spec = pl.BlockSpec((tr, c), lambda i: (i, 0))
    return pl.pallas_call(
        body, name=name, grid=(r // tr,),
        in_specs=[pl.BlockSpec((N_DEV, tr, c), lambda i: (0, i, 0)), spec, spec, spec],
        out_specs=[spec] * 4, out_shape=[jax.ShapeDtypeStruct((r, c), F32)] * 4,
        compiler_params=_params(("parallel",)),
    )(parts, w, m, v)


_WEIGHTS = ["norm1_g", "w_in", "q_norm_g", "k_norm_g", "ssm_lambda_re", "ssm_lambda_im", "ssm_log_dt", "ssm_b_re",
            "ssm_b_im", "ssm_c_re", "ssm_c_im", "ssm_d", "w_glu", "b_glu", "attn_out_g", "ssm_out_g", "w_out",
            "norm2_g", "w_mlp_in", "w_mlp_out"]
_BIG = ["w_in", "w_glu", "w_out", "w_mlp_in", "w_mlp_out"]
_SMALL = [n for n in _WEIGHTS if n not in _BIG]
_PACK_COLS = 1024


def _pack(tree, last=None):
    flat = [tree[n].reshape(-1).astype(F32) for n in _SMALL]
    size = sum(f.shape[0] for f in flat)
    rows = -(-(size + 1) // (_PACK_COLS * 8)) * 8
    pad = jnp.zeros((rows * _PACK_COLS - size - 1,), F32)
    tail = jnp.zeros((1,), F32) if last is None else last.reshape(1).astype(F32)
    return jnp.concatenate(flat + [pad, tail]).reshape(rows, _PACK_COLS)


def _unpack(buf, like):
    flat, out, off = buf.reshape(-1), {}, 0
    for n in _SMALL:
        size = like[n].size
        out[n] = flat[off:off + size].reshape(like[n].shape)
        off += size
    return out


def kernel(x, norm1_g, w_in, q_norm_g, k_norm_g, ssm_lambda_re, ssm_lambda_im, ssm_log_dt, ssm_b_re, ssm_b_im, ssm_c_re, ssm_c_im, ssm_d, w_glu, b_glu, attn_out_g, ssm_out_g, w_out, norm2_g, w_mlp_in, w_mlp_out, loss_target, m_norm1_g, m_w_in, m_q_norm_g, m_k_norm_g, m_ssm_lambda_re, m_ssm_lambda_im, m_ssm_log_dt, m_ssm_b_re, m_ssm_b_im, m_ssm_c_re, m_ssm_c_im, m_ssm_d, m_w_glu, m_b_glu, m_attn_out_g, m_ssm_out_g, m_w_out, m_norm2_g, m_w_mlp_in, m_w_mlp_out, v_norm1_g, v_w_in, v_q_norm_g, v_k_norm_g, v_ssm_lambda_re, v_ssm_lambda_im, v_ssm_log_dt, v_ssm_b_re, v_ssm_b_im, v_ssm_c_re, v_ssm_c_im, v_ssm_d, v_w_glu, v_b_glu, v_attn_out_g, v_ssm_out_g, v_w_out, v_norm2_g, v_w_mlp_in, v_w_mlp_out):
    w = dict(zip(_WEIGHTS, (norm1_g, w_in, q_norm_g, k_norm_g, ssm_lambda_re, ssm_lambda_im, ssm_log_dt, ssm_b_re, ssm_b_im, ssm_c_re, ssm_c_im, ssm_d, w_glu, b_glu, attn_out_g, ssm_out_g, w_out, norm2_g, w_mlp_in, w_mlp_out)))
    m = dict(zip(_WEIGHTS, (m_norm1_g, m_w_in, m_q_norm_g, m_k_norm_g, m_ssm_lambda_re, m_ssm_lambda_im, m_ssm_log_dt, m_ssm_b_re, m_ssm_b_im, m_ssm_c_re, m_ssm_c_im, m_ssm_d, m_w_glu, m_b_glu, m_attn_out_g, m_ssm_out_g, m_w_out, m_norm2_g, m_w_mlp_in, m_w_mlp_out)))
    v = dict(zip(_WEIGHTS, (v_norm1_g, v_w_in, v_q_norm_g, v_k_norm_g, v_ssm_lambda_re, v_ssm_lambda_im, v_ssm_log_dt, v_ssm_b_re, v_ssm_b_im, v_ssm_c_re, v_ssm_c_im, v_ssm_d, v_w_glu, v_b_glu, v_attn_out_g, v_ssm_out_g, v_w_out, v_norm2_g, v_w_mlp_in, v_w_mlp_out)))
    core = lax.axis_index("c").astype(jnp.int32).reshape(1)
    chip = (2 * lax.axis_index("x") + lax.axis_index("y")).astype(jnp.int32).reshape(1)

    me = (2 * chip + core).astype(jnp.int32)

    def landing(own=None, like=None):
        own = jnp.zeros_like(like) if own is None else own
        return lax.dynamic_update_slice(lax.empty((N_DEV, *like.shape), like.dtype), own[None], (me[0], 0, 0))

    (w_in_blocks,) = _all_gather("w_in_all_gather", [w_in.astype(BF16)])
    w_in_full = w_in_blocks.transpose(1, 0, 2).reshape(w_in.shape[0], -1)
    late = [n for n in _BIG if n != "w_in"]
    shards = [w[n].astype(BF16) for n in late]
    w_in_blocks, shards = lax.optimization_barrier((w_in_blocks, shards))
    weights_handle, weights_token = _exchange_start(
        "weights_send", shards, [landing(s, s) for s in shards], per_peer=False)

    def late_weights(after):
        got = dict(zip(late, _exchange_wait("weights_arrive", weights_handle, after)))
        return (got["w_glu"].reshape(-1, w_glu.shape[1]), got["w_out"].reshape(-1, w_out.shape[1]),
                got["w_mlp_in"].transpose(1, 0, 2).reshape(w_mlp_in.shape[0], -1),
                got["w_mlp_out"].reshape(-1, w_mlp_out.shape[1]))

    mlp = ["w_mlp_out", "w_mlp_in"]
    sent = {}

    def send_grads(name, names, own, own_b):
        blocks = lambda g, n: g.reshape(N_DEV, *w[n].shape)
        sent[name + "_own"] = [blocks(g, n) for g, n in zip(own, names)]
        srcs = [blocks(g, n) for g, n in zip(own_b, names)]
        sent[name], token = _exchange_start(name, srcs, [landing(like=s[0]) for s in srcs], per_peer=True)
        return token[0, 0]

    def mlp_grads_ready(g_out, g_out_b, g_in, g_in_b):
        return send_grads("mlp_grads_send", mlp, [g_out, g_in], [g_out_b, g_in_b])

    rest = ["w_in", "w_glu", "w_out"]

    def rest_grads_ready(own, own_b):
        return send_grads("rest_grads_send", rest, own, own_b)

    loss_local, grad_x, g_small, g_big = _local_step(
        x, loss_target, {n: w[n] for n in _SMALL}, w_in_full, late_weights, mlp_grads_ready, rest_grads_ready,
        weights_token[0, 0])

    grads, delta, new_m, new_v = {}, {}, {}, {}
    small = _pack(g_small, last=loss_local)
    small_handle, small_token = _exchange_start("small_grads_send", [small], [landing(small, small)], per_peer=False)

    for send, arrive, names in (("mlp_grads_send", "mlp_grads_arrive", mlp),
                                ("rest_grads_send", "rest_grads_arrive", rest)):
        for n, own, part in zip(names, sent[send + "_own"], _exchange_wait(arrive, sent[send], small_token)):
            grads[n], delta[n], new_m[n], new_v[n] = _adamw_gathered("adamw_" + n, own, part, me, w[n], m[n], v[n])

    shards_done = lax.optimization_barrier(tuple(new_v[n] for n in _BIG))
    (small_parts,) = _exchange_wait("small_grads_arrive", small_handle, shards_done[-1])
    packed = _adamw_small("adamw_small", small_parts, _pack(w), _pack(m), _pack(v))
    for tree, buf in zip((grads, delta, new_m, new_v), packed):
        tree.update(_unpack(buf, w))
    loss = packed[0][-1, -1]

    return (loss, grad_x, *[grads[n] for n in _WEIGHTS], *[delta[n] for n in _WEIGHTS],
            *[new_m[n] for n in _WEIGHTS], *[new_v[n] for n in _WEIGHTS])
```

```python
import functools
import math

import jax
import jax.numpy as jnp
from jax import lax
from jax.experimental import pallas as pl
from jax.experimental.pallas import tpu as pltpu

F32 = jnp.float32
BF16 = jnp.bfloat16

EPS = 1e-6
HEAD_DIM = 64
N_HEADS = 8
SB_WIDTH = 512
SSM_WIDTH = 512
SSM_GROUP = 16
SSM_GROUPS = 32
SSM_STATE = 64
QBLOCK = 128
KBLOCK = 256
N_CHUNK = 8
SSM_COLS = 4
LANES = 128
N_DEV = 8

ADAM_LR = 0.001
ADAM_B1 = 0.9
ADAM_B2 = 0.999
ADAM_EPS = 1e-08
ADAM_WD = 0.01
ADAM_STEP = 10

VMEM_LIMIT = 56 * 1024 * 1024

_NT = (((1,), (1,)), ((), ()))
_NN = (((1,), (0,)), ((), ()))
_TN = (((0,), (0,)), ((), ()))


def _dot(a, b, dims=_NN):
    return lax.dot_general(a, b, dims, preferred_element_type=F32)


def _params(sem):
    return pltpu.CompilerParams(dimension_semantics=sem, vmem_limit_bytes=VMEM_LIMIT)


def _matmul(name, a, b, *, ta=False, tb=False, extras=(), epilogue=None, out_dtypes=(F32,), sums=(),
            prologue=None, col_blocked=False, tm=1024, tn=512, tk=4096):
    M, K = (a.shape[1], a.shape[0]) if ta else a.shape
    N = b.shape[0] if tb else b.shape[1]
    tm, tn, tk = min(tm, M), min(tn, N), min(tk, K)
    assert M % tm == 0 and N % tn == 0 and K % tk == 0, (name, M, N, K)
    assert not (sums or prologue) or (tn == N and tk == K), name
    assert not prologue or not (ta or col_blocked), name
    nk = K // tk
    n_ex, n_out, n_sum = len(extras), len(out_dtypes), len(sums)
    n_pro = 1 if prologue else 0
    dims = (((0 if ta else 1,), (1 if tb else 0,)), ((), ()))

    def body(*refs):
        a_ref, b_ref = refs[0], refs[1]
        ex_refs = refs[2 + n_pro:2 + n_pro + n_ex]
        o_refs = refs[2 + n_pro + n_ex:2 + n_pro + n_ex + n_out]
        s_refs = refs[2 + n_pro + n_ex + n_out:2 + n_pro + n_ex + n_out + n_sum]
        k = pl.program_id(2)
        if prologue:
            left = prologue[0](a_ref[...], refs[2][...]).astype(BF16)
            refs[2 + n_pro + n_ex + n_out + n_sum][...] = left
        else:
            left = a_ref[...].astype(BF16)
        part = _dot(left, b_ref[...].astype(BF16), dims)

        def finish(acc):
            outs = (acc,) if epilogue is None else epilogue(acc, *[e[...] for e in ex_refs])
            for o_ref, o in zip(o_refs, outs[:n_out]):
                o_ref[...] = o.astype(o_ref.dtype)
            if n_sum:
                @pl.when(pl.program_id(0) == 0)
                def _():
                    for s_ref in s_refs:
                        s_ref[...] = jnp.zeros_like(s_ref)

                for s_ref, v in zip(s_refs, outs[n_out:]):
                    s_ref[...] += v

        if nk == 1:
            finish(part)
        else:
            acc_ref = refs[-1]

            @pl.when(k == 0)
            def _():
                acc_ref[...] = part

            @pl.when(jnp.logical_and(k > 0, k < nk - 1))
            def _():
                acc_ref[...] += part

            @pl.when(k == nk - 1)
            def _():
                finish(acc_ref[...] + part)

    a_spec = pl.BlockSpec((tk, tm), lambda i, j, k: (k, i)) if ta else pl.BlockSpec((tm, tk), lambda i, j, k: (i, k))
    b_spec = pl.BlockSpec((tn, tk), lambda i, j, k: (j, k)) if tb else pl.BlockSpec((tk, tn), lambda i, j, k: (k, j))
    ex_specs = [pl.BlockSpec((1, tn), lambda i, j, k: (0, j)) if e.shape[0] == 1 else
                pl.BlockSpec((tm, tn), lambda i, j, k: (i, j)) for e in extras]
    if col_blocked:
        out_specs = [pl.BlockSpec((None, tm, tn), lambda i, j, k: (j, i, 0)) for _ in out_dtypes]
        out_shape = [jax.ShapeDtypeStruct((N // tn, M, tn), dt) for dt in out_dtypes]
    else:
        wide = [(dt, N) if not isinstance(dt, tuple) else dt for dt in out_dtypes]
        assert all(w == N for _, w in wide) or tn == N, name
        out_specs = [pl.BlockSpec((tm, tn if w == N else w), lambda i, j, k: (i, j)) for _, w in wide]
        out_shape = [jax.ShapeDtypeStruct((M, w), dt) for dt, w in wide]
    out_specs += [pl.BlockSpec(s, lambda i, j, k: (0, 0)) for s in sums]
    out_shape += [jax.ShapeDtypeStruct(s, F32) for s in sums]
    pro_specs, pro_args = [], []
    if prologue:
        pro_specs, pro_args = [pl.BlockSpec((1, tk), lambda i, j, k: (0, 0))], [prologue[1]]
        out_specs.append(pl.BlockSpec((tm, tk), lambda i, j, k: (i, 0)))
        out_shape.append(jax.ShapeDtypeStruct((M, K), BF16))
    outs = pl.pallas_call(
        body, name=name, grid=(M // tm, N // tn, nk),
        in_specs=[a_spec, b_spec, *pro_specs, *ex_specs], out_specs=out_specs, out_shape=out_shape,
        scratch_shapes=[pltpu.VMEM((tm, tn), F32)] if nk > 1 else [],
        compiler_params=_params(("arbitrary",) * 3 if sums else ("parallel", "parallel", "arbitrary")),
    )(a, b, *pro_args, *extras)
    return outs[0] if len(outs) == 1 else outs


def _rowwise(name, fn, rows, small, outs, sums=(), tile=512):
    specs, args = [], []
    T = None
    for r in rows:
        arr, cb, w = r if isinstance(r, tuple) else (r, 0, r.shape[1])
        T = arr.shape[0]
        specs.append((w, cb))
        args.append(arr)
    tile = min(tile, T)
    assert T % tile == 0
    n_r, n_s, n_o, n_a = len(rows), len(small), len(outs), len(sums)

    def body(*refs):
        r_refs = refs[:n_r]
        s_refs = refs[n_r:n_r + n_s]
        o_refs = refs[n_r + n_s:n_r + n_s + n_o]
        a_refs = refs[n_r + n_s + n_o:]
        res = fn(*[r[...] for r in r_refs], *[s[...] for s in s_refs])
        res = res if isinstance(res, (tuple, list)) else (res,)
        for o_ref, o in zip(o_refs, res[:n_o]):
            o_ref[...] = o.astype(o_ref.dtype)

        @pl.when(pl.program_id(0) == 0)
        def _():
            for a_ref in a_refs:
                a_ref[...] = jnp.zeros_like(a_ref)

        for a_ref, v in zip(a_refs, res[n_o:]):
            a_ref[...] += v.astype(F32)

    in_specs = [pl.BlockSpec((tile, w), functools.partial(lambda i, cb: (i, cb), cb=cb)) for w, cb in specs]
    in_specs += [pl.BlockSpec(s.shape, functools.partial(lambda i, nd: (0,) * nd, nd=s.ndim)) for s in small]
    out_specs = [pl.BlockSpec((tile, w), lambda i: (i, 0)) for w, _ in outs]
    out_specs += [pl.BlockSpec(s, functools.partial(lambda i, nd: (0,) * nd, nd=len(s))) for s in sums]
    out_shape = [jax.ShapeDtypeStruct((T, w), dt) for w, dt in outs]
    out_shape += [jax.ShapeDtypeStruct(s, F32) for s in sums]
    res = pl.pallas_call(
        body, name=name, grid=(T // tile,), in_specs=in_specs, out_specs=out_specs, out_shape=out_shape,
        compiler_params=_params(("arbitrary",)),
    )(*args, *small)
    return res[0] if len(res) == 1 else res


def _rms(x, g):
    return x * lax.rsqrt(jnp.mean(x * x, axis=-1, keepdims=True) + EPS) * g


def _glu_branch(y, pre, b_glu, g_out):
    g = jax.nn.gelu(y)
    return _rms(g * jax.nn.sigmoid(pre + b_glu), g_out)


def _split_dot(x, tri_bf):
    hi = x.astype(BF16)
    lo = (x - hi.astype(F32)).astype(BF16)
    return _dot(hi, tri_bf) + _dot(lo, tri_bf)


def _softplus(z):
    return jnp.maximum(z, 0.0) + jnp.log(1.0 + jnp.exp(-jnp.abs(z)))


def _head(h):
    return slice(h * HEAD_DIM, (h + 1) * HEAD_DIM)


def _head_mean(x, seg):
    return _split_dot(x, seg) * (1.0 / HEAD_DIM)


def _qk_norm(proj, gains):
    r = lax.div(lax.broadcasted_iota(jnp.int32, (SB_WIDTH, SB_WIDTH), 0), HEAD_DIM)
    c = lax.div(lax.broadcasted_iota(jnp.int32, (SB_WIDTH, SB_WIDTH), 1), HEAD_DIM)
    seg = (r == c).astype(BF16)
    q, k, v = (proj[:, i * SB_WIDTH:(i + 1) * SB_WIDTH] for i in range(3))
    qn = q * lax.rsqrt(_head_mean(q * q, seg) + EPS) * gains[:, 0:SB_WIDTH]
    kn = k * lax.rsqrt(_head_mean(k * k, seg) + EPS) * gains[:, SB_WIDTH:2 * SB_WIDTH]
    return proj, qn, kn, v


def _qk_norm_bwd(x, g, dy, seg):
    r = lax.rsqrt(_head_mean(x * x, seg) + EPS)
    gdy = g * dy
    dx = r * gdy - x * (r * r * r) * _head_mean(gdy * x, seg)
    return dx, jnp.sum(dy * x * r, axis=0, keepdims=True)


def _split_heads(refs, scratch, L):
    def chunk(i, _):
        r = pl.ds(pl.multiple_of(i * QBLOCK, QBLOCK), QBLOCK)
        for ref, s in zip(refs, scratch):
            for h in range(2):
                s[h, r, :] = ref[r, _head(h)]
        return 0

    lax.fori_loop(0, L // QBLOCK, chunk, 0)


Q_HALVES = KBLOCK // QBLOCK
_CHAINS = [(h, r) for h in range(2) for r in range(Q_HALVES)]


def _valid(i, kb):
    row = lax.broadcasted_iota(jnp.int32, (QBLOCK, KBLOCK), 0)
    col = lax.broadcasted_iota(jnp.int32, (QBLOCK, KBLOCK), 1)
    return col + (kb * KBLOCK - i * QBLOCK) < row


def _attn_fwd(qn, kn, vb, B, L):
    n_pairs = L // KBLOCK
    n_hp = N_HEADS // 2
    nc = len(_CHAINS)

    def body(q_ref, k_ref, v_ref, o_ref, a_ref, q_s, k_s, v_s, after_s, z_s, stage_s, sems):
        _split_heads((q_ref, k_ref, v_ref), (q_s, k_s, v_s), L)
        r2 = lax.broadcasted_iota(jnp.int32, (KBLOCK, KBLOCK), 0)
        c2 = lax.broadcasted_iota(jnp.int32, (KBLOCK, KBLOCK), 1)
        after_s[...] = (r2 > c2).astype(after_s.dtype)
        g = pl.program_id(0) * n_hp + pl.program_id(1)

        def q_pair(p, _):
            rows = [pl.ds(pl.multiple_of((p * Q_HALVES + r) * QBLOCK, QBLOCK), QBLOCK) for r in range(Q_HALVES)]
            q_c = [q_s[h, rows[r], :] for h, r in _CHAINS]
            cs = range(nc)

            def scores(kb):
                rk = pl.ds(pl.multiple_of(kb * KBLOCK, KBLOCK), KBLOCK)
                return [_dot(q_c[c], k_s[_CHAINS[c][0], rk, :], _NT) for c in cs]

            def saved(kb):
                return pltpu.make_async_copy(stage_s.at[kb & 1], a_ref.at[g, p, kb], sems.at[kb & 1])

            def k_block(kb, carry, diagonal):
                rk = pl.ds(pl.multiple_of(kb * KBLOCK, KBLOCK), KBLOCK)
                if diagonal:
                    valid = [_valid(p * Q_HALVES + r, kb) for r in range(Q_HALVES)]
                    keep = lambda c, t: jnp.where(valid[_CHAINS[c][1]], t, 0.0)
                    z = scores(kb)
                else:
                    keep = lambda c, t: t
                    z = [z_s[(kb + 1) & 1, c] for c in cs]
                ahead = scores(jnp.maximum(kb - 1, 0))
                for c in cs:
                    z_s[kb & 1, c] = ahead[c]
                sp = [_softplus(z[c]) for c in cs]
                lsig = [z[c] - sp[c] for c in cs]
                lom = [keep(c, -sp[c]) for c in cs]
                tail = [_split_dot(lom[c], after_s[...]) + carry[c][0] for c in cs]
                a = [keep(c, jnp.exp(lsig[c] + tail[c])).astype(v_s.dtype) for c in cs]
                acc = [carry[c][1] + _dot(a[c], v_s[_CHAINS[c][0], rk, :]) for c in cs]
                for c in cs:
                    stage_s[kb & 1, c] = a[c]
                saved(kb).start()
                return tuple((carry[c][0] + jnp.sum(lom[c], axis=1, keepdims=True), acc[c]) for c in cs)

            def next_block(n, carry):
                kb = p - n

                @pl.when(n >= 2)
                def _():
                    saved(kb + 2).wait()

                return k_block(kb, carry, False)

            init = (jnp.zeros((QBLOCK, 1), F32), jnp.zeros((QBLOCK, HEAD_DIM), F32))
            first = k_block(p, (init,) * nc, True)
            res = lax.fori_loop(1, p + 1, next_block, first)
            saved(0).wait()

            @pl.when(p >= 1)
            def _():
                saved(1).wait()

            for r in range(Q_HALVES):
                o_ref[rows[r], :] = jnp.concatenate([res[c][1] for c in cs if _CHAINS[c][1] == r], axis=1)
            return 0

        lax.fori_loop(0, n_pairs, q_pair, 0)

    spec = pl.BlockSpec((L, LANES), lambda b, p: (b, p))
    return pl.pallas_call(
        body, name="attn_fwd", grid=(B, n_hp),
        in_specs=[spec] * 3, out_specs=[spec, _ANY],
        out_shape=[jax.ShapeDtypeStruct((B * L, SB_WIDTH), F32),
                   jax.ShapeDtypeStruct((B * n_hp, n_pairs, n_pairs, nc, QBLOCK, KBLOCK), BF16)],
        scratch_shapes=[pltpu.VMEM((2, L, HEAD_DIM), BF16)] * 3 + [pltpu.VMEM((KBLOCK, KBLOCK), BF16)]
        + [pltpu.VMEM((2, nc, QBLOCK, KBLOCK), F32), pltpu.VMEM((2, nc, QBLOCK, KBLOCK), BF16),
           pltpu.SemaphoreType.DMA((2,))],
        compiler_params=_params(("parallel", "parallel")),
    )(qn, kn, vb)


def _attn_bwd(qn, kn, vb, kept_a, d_sb, proj, gq2, gk2, B, L):
    n_pairs = L // KBLOCK
    n_hp = N_HEADS // 2
    nc = len(_CHAINS)
    slots = 3
    scale = 1.0 / math.sqrt(HEAD_DIM)

    def body(q_ref, k_ref, v_ref, do_ref, qraw_ref, kraw_ref, gq_ref, gk_ref, a_ref,
             dq_ref, dk_ref, dv_ref, dgq_ref, dgk_ref,
             q_s, k_s, v_s, qt_s, dkt_s, dvt_s, before_s, stage_s, sems):
        _split_heads((q_ref, k_ref, v_ref), (q_s, k_s, v_s), L)
        g = pl.program_id(0) * n_hp + pl.program_id(1)
        lane_head = [lax.div(lax.broadcasted_iota(jnp.int32, (LANES, LANES), d), HEAD_DIM) for d in (0, 1)]
        seg = (lane_head[0] == lane_head[1]).astype(BF16)

        def transpose_q(i, _):
            r = pl.ds(pl.multiple_of(i * QBLOCK, QBLOCK), QBLOCK)
            qt_s[:, r] = q_ref[r, :].astype(F32).T.astype(qt_s.dtype)
            return 0

        lax.fori_loop(0, L // QBLOCK, transpose_q, 0)
        dkt_s[...] = jnp.zeros_like(dkt_s)
        dvt_s[...] = jnp.zeros_like(dvt_s)
        r2 = lax.broadcasted_iota(jnp.int32, (KBLOCK, KBLOCK), 0)
        c2 = lax.broadcasted_iota(jnp.int32, (KBLOCK, KBLOCK), 1)
        before_s[...] = (r2 < c2).astype(before_s.dtype)

        def q_pair(p, dgq):
            rows = [pl.ds(pl.multiple_of((p * Q_HALVES + r) * QBLOCK, QBLOCK), QBLOCK) for r in range(Q_HALVES)]
            pair = pl.ds(pl.multiple_of(p * KBLOCK, KBLOCK), KBLOCK)
            do2 = do_ref[pair, :]
            do_t = do2.T.astype(v_s.dtype)
            cs = range(nc)
            hs = range(2)
            q_c = [q_s[h, rows[r], :] for h, r in _CHAINS]
            do_c = [do2[r * QBLOCK:(r + 1) * QBLOCK, _head(h)].astype(v_s.dtype) for h, r in _CHAINS]
            qt_h = [qt_s[_head(h), pair] for h in hs]
            dot_h = [do_t[_head(h), :] for h in hs]

            def kept(kb):
                slot = lax.rem(kb, slots)
                return pltpu.make_async_copy(a_ref.at[g, p, kb], stage_s.at[slot], sems.at[slot])

            def k_block(kb, carry, diagonal):
                rk = pl.ds(pl.multiple_of(kb * KBLOCK, KBLOCK), KBLOCK)
                if diagonal:
                    valid = [_valid(p * Q_HALVES + r, kb) for r in range(Q_HALVES)]
                    keep = lambda c, t: jnp.where(valid[_CHAINS[c][1]], t, 0.0)
                else:
                    keep = lambda c, t: t

                    @pl.when(kb + 2 <= p)
                    def _():
                        kept(kb + 2).start()

                kept(kb).wait()
                slot = lax.rem(kb, slots)
                k_b = [k_s[h, rk, :] for h in hs]
                z = [_dot(q_c[c], k_b[_CHAINS[c][0]], _NT) for c in cs]
                da = [_dot(do_c[c], v_s[_CHAINS[c][0], rk, :], _NT) for c in cs]
                a = [stage_s[slot, c] for c in cs]
                dla = [a[c].astype(F32) * da[c] for c in cs]
                for h in hs:
                    a_h = jnp.concatenate([a[c] for c in cs if _CHAINS[c][0] == h], axis=0)
                    dvt_s[_head(h), rk] += _dot(dot_h[h], a_h)
                d_lom = [carry[c][0] + _split_dot(dla[c], before_s[...]) for c in cs]
                beta = [jax.nn.sigmoid(z[c]) for c in cs]
                dz_b = [(dla[c] * (1.0 - beta[c]) - keep(c, beta[c] * d_lom[c])).astype(v_s.dtype) for c in cs]
                dq_acc = [carry[c][1] + _dot(dz_b[c], k_b[_CHAINS[c][0]]) for c in cs]
                for h in hs:
                    dz_h = jnp.concatenate([dz_b[c] for c in cs if _CHAINS[c][0] == h], axis=0)
                    dkt_s[_head(h), rk] += _dot(qt_h[h], dz_h)
                return tuple((carry[c][0] + jnp.sum(dla[c], axis=1, keepdims=True), dq_acc[c]) for c in cs)

            init = (jnp.zeros((QBLOCK, 1), F32), jnp.zeros((QBLOCK, HEAD_DIM), F32))
            kept(0).start()

            @pl.when(p >= 1)
            def _():
                kept(1).start()

            before = lax.fori_loop(0, p, lambda kb, carry: k_block(kb, carry, False), (init,) * nc)
            res = k_block(p, before, True)
            for r in range(Q_HALVES):
                d_qn = jnp.concatenate([res[c][1] for c in cs if _CHAINS[c][1] == r], axis=1) * scale
                dq, dg = _qk_norm_bwd(qraw_ref[rows[r], :], gq_ref[...], d_qn, seg)
                dq_ref[rows[r], :] = dq.astype(dq_ref.dtype)
                dgq = dgq + dg
            return dgq

        dgq = lax.fori_loop(0, n_pairs, q_pair, jnp.zeros((1, LANES), F32))

        def transpose_out(i, dgk):
            r = pl.ds(pl.multiple_of(i * QBLOCK, QBLOCK), QBLOCK)
            dk, dg = _qk_norm_bwd(kraw_ref[r, :], gk_ref[...], dkt_s[:, r].T, seg)
            dk_ref[r, :] = dk.astype(dk_ref.dtype)
            dv_ref[r, :] = dvt_s[:, r].T.astype(dv_ref.dtype)
            return dgk + dg

        dgk = lax.fori_loop(0, L // QBLOCK, transpose_out, jnp.zeros((1, LANES), F32))

        @pl.when(jnp.logical_and(pl.program_id(0) == 0, pl.program_id(1) == 0))
        def _():
            dgq_ref[...] = jnp.zeros_like(dgq_ref)
            dgk_ref[...] = jnp.zeros_like(dgk_ref)

        dgq_ref[...] += dgq[:, _head(0)] + dgq[:, _head(1)]
        dgk_ref[...] += dgk[:, _head(0)] + dgk[:, _head(1)]

    spec = pl.BlockSpec((L, LANES), lambda b, p: (b, p))
    gain = pl.BlockSpec((1, LANES), lambda b, p: (0, 0))
    gain_grad = pl.BlockSpec((1, HEAD_DIM), lambda b, p: (0, 0))
    return pl.pallas_call(
        body, name="attn_bwd", grid=(B, n_hp),
        in_specs=[spec] * 4 + [spec, pl.BlockSpec((L, LANES), lambda b, p: (b, n_hp + p)), gain, gain, _ANY],
        out_specs=[spec] * 3 + [gain_grad] * 2,
        out_shape=[jax.ShapeDtypeStruct((B * L, SB_WIDTH), BF16)] * 3 + [jax.ShapeDtypeStruct((1, HEAD_DIM), F32)] * 2,
        scratch_shapes=[pltpu.VMEM((2, L, HEAD_DIM), BF16)] * 3 + [pltpu.VMEM((LANES, L), BF16)]
        + [pltpu.VMEM((LANES, L), F32)] * 2 + [pltpu.VMEM((KBLOCK, KBLOCK), BF16)]
        + [pltpu.VMEM((slots, nc, QBLOCK, KBLOCK), BF16), pltpu.SemaphoreType.DMA((slots,))],
        compiler_params=_params(("arbitrary", "arbitrary")),
    )(qn, kn, vb, d_sb, proj, proj, gq2, gk2, kept_a)


def _ssm_discretise(lam_re, lam_im, log_dt, b_re, b_im):
    dt = jnp.exp(log_dt)
    mag = jnp.exp(lam_re * dt)
    lbr = mag * jnp.cos(lam_im * dt)
    lbi = mag * jnp.sin(lam_im * dt)
    den = lam_re * lam_re + lam_im * lam_im
    nr, ni = lbr - 1.0, lbi
    cr = (nr * lam_re + ni * lam_im) / den
    ci = (ni * lam_re - nr * lam_im) / den
    return lbr, lbi, cr * b_re - ci * b_im, cr * b_im + ci * b_re


def _ssm_prep(lam_re, lam_im, log_dt, b_re_t, b_im_t):
    def body(lr, li, ld, br, bi, o_lr, o_li, o_br, o_bi):
        res = _ssm_discretise(lr[...], li[...], ld[...], br[...], bi[...])
        for o, v in zip((o_lr, o_li, o_br, o_bi), res):
            o[...] = v

    return pl.pallas_call(
        body, name="ssm_prep",
        out_shape=[jax.ShapeDtypeStruct(lam_re.shape, F32)] * 2 + [jax.ShapeDtypeStruct(b_re_t.shape, F32)] * 2,
    )(lam_re, lam_im, log_dt, b_re_t, b_im_t)


def _ssm_prep_bwd(lam_re, lam_im, log_dt, b_re_t, b_im_t, d_lr, d_li, d_br, d_bi):
    def body(lr, li, ld, br, bi, g_lr, g_li, g_br, g_bi, o_lr, o_li, o_ld, o_br, o_bi):
        _, vjp = jax.vjp(_ssm_discretise, lr[...], li[...], ld[...], br[...], bi[...])
        res = vjp((g_lr[...], g_li[...], g_br[...], g_bi[...]))
        for o, v in zip((o_lr, o_li, o_ld, o_br, o_bi), res):
            o[...] = v

    return pl.pallas_call(
        body, name="ssm_prep_bwd",
        out_shape=[jax.ShapeDtypeStruct(lam_re.shape, F32)] * 2 + [jax.ShapeDtypeStruct(log_dt.shape, F32)]
        + [jax.ShapeDtypeStruct(b_re_t.shape, F32)] * 2,
    )(lam_re, lam_im, log_dt, b_re_t, b_im_t, d_lr, d_li, d_br, d_bi)


def _block_diag(m):
    m4 = m.reshape(SSM_COLS, 8, SSM_GROUP, SSM_STATE)
    return jnp.einsum("aghp,gk->aghkp", m4, jnp.eye(8, dtype=m.dtype)).reshape(SSM_COLS, LANES, 512)


def _block_diag_take(d):
    d6 = d.reshape(SSM_COLS, 8, SSM_GROUP, 2, 8, SSM_STATE)
    return jnp.einsum("aghrgp->raghp", d6).reshape(2, SSM_GROUPS, SSM_GROUP, SSM_STATE)


def _cmul(ar, ai, br, bi):
    return ar * br - ai * bi, ar * bi + ai * br


def _power(lr, li, n):
    assert n & (n - 1) == 0
    for _ in range(n.bit_length() - 1):
        lr, li = _cmul(lr, li, lr, li)
    return lr, li


def _ssm_fwd(u_p, w_b, lam_r, lam_i, c_m, d_skip, B, L, tj):
    J = L // N_CHUNK
    njt = J // tj
    R = tj * N_CHUNK
    H = 512

    def body(u_ref, wb_ref, lr_ref, li_ref, cm_ref, d_ref, y_ref, gel_ref, x_ref, xin_ref, bu_s, st_s, xin_s):
        ph, jt = pl.program_id(2), pl.program_id(3)
        lr, li = lr_ref[...], li_ref[...]

        @pl.when(jnp.logical_and(ph == 0, jt == 0))
        def _():
            st_s[...] = jnp.zeros_like(st_s)

        @pl.when(ph == 0)
        def _():
            bu_s[jt] = _dot(u_ref[...].astype(BF16), wb_ref[...].astype(BF16))

        def scan(store):
            def step(j, carry):
                xr, xi = carry
                r = pl.ds(pl.multiple_of(j * N_CHUNK, N_CHUNK), N_CHUNK)
                nr = lr * xr - li * xi + bu_s[jt, r, 0:H]
                ni = lr * xi + li * xr + bu_s[jt, r, H:2 * H]
                if store:
                    x_ref[r, 0:H] = nr
                    x_ref[r, H:2 * H] = ni
                return nr, ni

            xr, xi = lax.fori_loop(0, tj, step, (st_s[:, 0:H], st_s[:, H:2 * H]))
            st_s[:, 0:H] = xr
            st_s[:, H:2 * H] = xi

        @pl.when(ph == 0)
        def _():
            scan(False)

            @pl.when(jt == njt - 1)
            def _():
                pr, pi = _power(lr[0:1], li[0:1], J)
                xin_s[0:1, :] = jnp.zeros((1, 2 * H), F32)
                for c in range(1, N_CHUNK):
                    qr, qi = _cmul(pr, pi, xin_s[c - 1:c, 0:H], xin_s[c - 1:c, H:2 * H])
                    xin_s[c:c + 1, 0:H] = qr + st_s[c - 1:c, 0:H]
                    xin_s[c:c + 1, H:2 * H] = qi + st_s[c - 1:c, H:2 * H]
                xin_ref[...] = xin_s[...]
                st_s[...] = xin_s[...]

        @pl.when(ph == 1)
        def _():
            scan(True)
            y = _dot(x_ref[...].astype(BF16), cm_ref[...].astype(BF16)) + d_ref[...] * u_ref[...]
            y_ref[...] = y
            gel_ref[...] = jax.nn.gelu(y).astype(gel_ref.dtype)

    return pl.pallas_call(
        body, name="ssm_fwd", grid=(SSM_COLS, B, 2, njt),
        in_specs=[
            pl.BlockSpec((None, R, LANES), lambda i, b, ph, jt: (b, jt, i)),
            pl.BlockSpec((None, LANES, 2 * H), lambda i, b, ph, jt: (i, 0, 0)),
            pl.BlockSpec((None, N_CHUNK, H), lambda i, b, ph, jt: (i, 0, 0)),
            pl.BlockSpec((None, N_CHUNK, H), lambda i, b, ph, jt: (i, 0, 0)),
            pl.BlockSpec((None, 2 * H, LANES), lambda i, b, ph, jt: (i, 0, 0)),
            pl.BlockSpec((1, LANES), lambda i, b, ph, jt: (0, i)),
        ],
        out_specs=[
            pl.BlockSpec((None, R, LANES), lambda i, b, ph, jt: (b, jt * ph, i)),
            pl.BlockSpec((None, R, LANES), lambda i, b, ph, jt: (b, jt * ph, i)),
            pl.BlockSpec((None, R, 2 * H), lambda i, b, ph, jt: (b, jt * ph, i)),
            pl.BlockSpec((None, None, N_CHUNK, 2 * H), lambda i, b, ph, jt: (b, i, 0, 0)),
        ],
        out_shape=[
            jax.ShapeDtypeStruct((B, L, SSM_WIDTH), F32),
            jax.ShapeDtypeStruct((B, L, SSM_WIDTH), BF16),
            jax.ShapeDtypeStruct((B, L, SSM_COLS * 2 * H), F32),
            jax.ShapeDtypeStruct((B, SSM_COLS, N_CHUNK, 2 * H), F32),
        ],
        scratch_shapes=[pltpu.VMEM((njt, R, 2 * H), F32), pltpu.VMEM((N_CHUNK, 2 * H), F32),
                        pltpu.VMEM((N_CHUNK, 2 * H), F32)],
        compiler_params=_params(("arbitrary",) * 4),
    )(u_p, w_b, lam_r, lam_i, c_m, d_skip)


def _ssm_bwd(dy_p, u_p, x, xin, w_bt, lam_r, lam_i, c_mt, d_skip, B, L, tj):
    J = L // N_CHUNK
    njt = J // tj
    R = tj * N_CHUNK
    H = 512
    x4 = x.reshape(B, J, N_CHUNK, SSM_COLS * 2 * H)

    def body(dy_ref, u_ref, x_ref, xp_ref, xin_ref, wbt_ref, lr_ref, li_ref, cmt_ref, d_ref,
             du_ref, dwb_ref, dcm_ref, dlr_ref, dli_ref, dd_ref, ca_s, a_s, st_s, dl_s):
        b, ph, jt = pl.program_id(1), pl.program_id(2), pl.program_id(3)
        jr = njt - 1 - jt
        lr, li = lr_ref[...], -li_ref[...]

        @pl.when(jnp.logical_and(b == 0, jnp.logical_and(ph == 0, jt == 0)))
        def _():
            dwb_ref[...] = jnp.zeros_like(dwb_ref)
            dcm_ref[...] = jnp.zeros_like(dcm_ref)
            dlr_ref[...] = jnp.zeros_like(dlr_ref)
            dli_ref[...] = jnp.zeros_like(dli_ref)
            dd_ref[...] = jnp.zeros_like(dd_ref)
            dl_s[...] = jnp.zeros_like(dl_s)

        @pl.when(jnp.logical_and(ph == 0, jt == 0))
        def _():
            st_s[...] = jnp.zeros_like(st_s)

        @pl.when(ph == 0)
        def _():
            ca_s[jt] = _dot(dy_ref[...].astype(BF16), cmt_ref[...].astype(BF16))

        def scan(store):
            def step(n, carry):
                ar, ai = carry
                r = pl.ds(pl.multiple_of((tj - 1 - n) * N_CHUNK, N_CHUNK), N_CHUNK)
                nr = lr * ar - li * ai + ca_s[jt, r, 0:H]
                ni = lr * ai + li * ar + ca_s[jt, r, H:2 * H]
                if store:
                    a_s[r, 0:H] = nr
                    a_s[r, H:2 * H] = ni
                return nr, ni

            ar, ai = lax.fori_loop(0, tj, step, (st_s[:, 0:H], st_s[:, H:2 * H]))
            st_s[:, 0:H] = ar
            st_s[:, H:2 * H] = ai

        @pl.when(ph == 0)
        def _():
            scan(False)

            @pl.when(jt == njt - 1)
            def _():
                pr, pi = _power(lr[0:1], li[0:1], J)
                a_s[N_CHUNK - 1:N_CHUNK, :] = jnp.zeros((1, 2 * H), F32)
                for c in range(N_CHUNK - 2, -1, -1):
                    qr, qi = _cmul(pr, pi, a_s[c + 1:c + 2, 0:H], a_s[c + 1:c + 2, H:2 * H])
                    a_s[c:c + 1, 0:H] = qr + st_s[c + 1:c + 2, 0:H]
                    a_s[c:c + 1, H:2 * H] = qi + st_s[c + 1:c + 2, H:2 * H]
                st_s[...] = a_s[0:N_CHUNK, :]

        @pl.when(ph == 1)
        def _():
            scan(True)
            dy = dy_ref[...]
            u = u_ref[...]
            a_b = a_s[...].astype(BF16)
            du_ref[...] = (_dot(a_b, wbt_ref[...].astype(BF16)) + d_ref[...] * dy).astype(du_ref.dtype)
            dwb_ref[...] += _dot(u.astype(BF16), a_b, _TN)
            dcm_ref[...] += _dot(x_ref[...].astype(BF16), dy.astype(BF16), _TN)
            dd_ref[...] += jnp.sum(dy * u, axis=0, keepdims=True)

            first = jnp.where(jr == 0, xin_ref[...], xp_ref[...])
            a0r, a0i = a_s[0:N_CHUNK, 0:H], a_s[0:N_CHUNK, H:2 * H]
            acc0 = (a0r * first[:, 0:H] + a0i * first[:, H:2 * H], a0i * first[:, 0:H] - a0r * first[:, H:2 * H])

            def step(j, carry):
                sr, si = carry
                r = pl.ds(pl.multiple_of(j * N_CHUNK, N_CHUNK), N_CHUNK)
                rp = pl.ds(pl.multiple_of((j - 1) * N_CHUNK, N_CHUNK), N_CHUNK)
                ar, ai = a_s[r, 0:H], a_s[r, H:2 * H]
                xr, xi = x_ref[rp, 0:H], x_ref[rp, H:2 * H]
                return sr + ar * xr + ai * xi, si + ai * xr - ar * xi

            sr, si = lax.fori_loop(1, tj, step, acc0)
            dl_s[:, 0:H] += sr
            dl_s[:, H:2 * H] += si

            @pl.when(jnp.logical_and(b == B - 1, jt == njt - 1))
            def _():
                dlr_ref[...] = jnp.sum(dl_s[:, 0:H], axis=0, keepdims=True)
                dli_ref[...] = jnp.sum(dl_s[:, H:2 * H], axis=0, keepdims=True)
                dl_s[...] = jnp.zeros_like(dl_s)

    rev = lambda ph, jt: (njt - 1 - jt) * ph + (njt - 1) * (1 - ph)
    return pl.pallas_call(
        body, name="ssm_bwd", grid=(SSM_COLS, B, 2, njt),
        in_specs=[
            pl.BlockSpec((None, R, LANES), lambda i, b, ph, jt: (b, njt - 1 - jt, i)),
            pl.BlockSpec((None, R, LANES), lambda i, b, ph, jt: (b, njt - 1 - jt, i)),
            pl.BlockSpec((None, R, 2 * H), lambda i, b, ph, jt: (b, rev(ph, jt), i)),
            pl.BlockSpec((None, None, N_CHUNK, 2 * H),
                         lambda i, b, ph, jt: (b, jnp.maximum((njt - 1 - jt) * tj - 1, 0), 0, i)),
            pl.BlockSpec((None, None, N_CHUNK, 2 * H), lambda i, b, ph, jt: (b, i, 0, 0)),
            pl.BlockSpec((None, 2 * H, LANES), lambda i, b, ph, jt: (i, 0, 0)),
            pl.BlockSpec((None, N_CHUNK, H), lambda i, b, ph, jt: (i, 0, 0)),
            pl.BlockSpec((None, N_CHUNK, H), lambda i, b, ph, jt: (i, 0, 0)),
            pl.BlockSpec((None, LANES, 2 * H), lambda i, b, ph, jt: (i, 0, 0)),
            pl.BlockSpec((1, LANES), lambda i, b, ph, jt: (0, i)),
        ],
        out_specs=[
            pl.BlockSpec((None, R, LANES), lambda i, b, ph, jt: (b, rev(ph, jt), i)),
            pl.BlockSpec((None, LANES, 2 * H), lambda i, b, ph, jt: (i, 0, 0)),
            pl.BlockSpec((None, 2 * H, LANES), lambda i, b, ph, jt: (i, 0, 0)),
            pl.BlockSpec((None, 1, H), lambda i, b, ph, jt: (i, 0, 0)),
            pl.BlockSpec((None, 1, H), lambda i, b, ph, jt: (i, 0, 0)),
            pl.BlockSpec((1, LANES), lambda i, b, ph, jt: (0, i)),
        ],
        out_shape=[
            jax.ShapeDtypeStruct((B, L, SSM_WIDTH), BF16),
            jax.ShapeDtypeStruct((SSM_COLS, LANES, 2 * H), F32),
            jax.ShapeDtypeStruct((SSM_COLS, 2 * H, LANES), F32),
            jax.ShapeDtypeStruct((SSM_COLS, 1, H), F32),
            jax.ShapeDtypeStruct((SSM_COLS, 1, H), F32),
            jax.ShapeDtypeStruct((1, SSM_WIDTH), F32),
        ],
        scratch_shapes=[pltpu.VMEM((njt, R, 2 * H), F32), pltpu.VMEM((R, 2 * H), F32),
                        pltpu.VMEM((N_CHUNK, 2 * H), F32), pltpu.VMEM((N_CHUNK, 2 * H), F32)],
        compiler_params=_params(("arbitrary",) * 4),
    )(dy_p, u_p, x, x4, xin, w_bt, lam_r, lam_i, c_mt, d_skip)


def _to_scan_layout(t, B, L):
    C = t.shape[-1]
    return t.reshape(B, N_CHUNK, L // N_CHUNK, C).transpose(0, 2, 1, 3).reshape(B, L, C)


def _from_scan_layout(t, B, L):
    C = t.shape[-1]
    return t.reshape(B, L // N_CHUNK, N_CHUNK, C).transpose(0, 2, 1, 3).reshape(B * L, C)


def _local_step(x, target, p, w_in, late_weights, mlp_grads_ready=None, rest_grads_ready=None, order=None, *,
                ssm_tile=256):
    B, L, D = x.shape
    T = B * L
    x2 = x.reshape(T, D)
    row = lambda v: v.reshape(1, -1)
    g1, g2, ga, gs, b_glu = row(p["norm1_g"]), row(p["norm2_g"]), row(p["attn_out_g"]), row(p["ssm_out_g"]), row(p["b_glu"])
    g1_first = g1 if order is None else g1 + order
    gq8 = jnp.tile(row(p["q_norm_g"]), (1, N_HEADS))
    gk8 = jnp.tile(row(p["k_norm_g"]), (1, N_HEADS))

    G, P, Hh = SSM_GROUPS, SSM_STATE, SSM_GROUP
    lam_re3, lam_im3 = p["ssm_lambda_re"].reshape(G, 1, P), p["ssm_lambda_im"].reshape(G, 1, P)
    log_dt3 = p["ssm_log_dt"].reshape(G, 1, 1)
    b_re_t, b_im_t = p["ssm_b_re"].transpose(0, 2, 1), p["ssm_b_im"].transpose(0, 2, 1)
    lbr, lbi, bbr, bbi = _ssm_prep(lam_re3, lam_im3, log_dt3, b_re_t, b_im_t)
    w_b = jnp.concatenate([_block_diag(bbr), _block_diag(bbi)], axis=2)
    c_mt = jnp.concatenate([_block_diag(p["ssm_c_re"]), -_block_diag(p["ssm_c_im"])], axis=2)
    w_bt, c_m = w_b.transpose(0, 2, 1), c_mt.transpose(0, 2, 1)
    lam_r = jnp.broadcast_to(lbr.reshape(SSM_COLS, 1, 512), (SSM_COLS, N_CHUNK, 512))
    lam_i = jnp.broadcast_to(lbi.reshape(SSM_COLS, 1, 512), (SSM_COLS, N_CHUNK, 512))
    d_skip = p["ssm_d"].reshape(1, SSM_WIDTH)

    qk_gains = jnp.concatenate([gq8 * (1.0 / math.sqrt(HEAD_DIM)), gk8, jnp.ones((1, 2 * SB_WIDTH), F32)], axis=1)
    head = (BF16, SB_WIDTH)
    proj, qn, kn, vb, xn = _matmul("proj", x2, w_in, prologue=(_rms, g1_first), extras=[qk_gains],
                                   out_dtypes=(F32, head, head, head), epilogue=_qk_norm, tm=512, tn=w_in.shape[1])
    sb, attn_kept = _attn_fwd(qn, kn, vb, B, L)
    u_p = _to_scan_layout(proj[:, 3 * SB_WIDTH:], B, L)
    y_p, gel_p, xs, xin = _ssm_fwd(u_p, w_b, lam_r, lam_i, c_m, d_skip, B, L, ssm_tile)
    y2, gel = y_p.reshape(T, SSM_WIDTH), gel_p.reshape(T, SSM_WIDTH)
    gel, sb = lax.optimization_barrier((gel, sb))
    w_glu, w_out, w_mlp_in, w_mlp_out = late_weights(gel)
    pre, ssm_n = _matmul("glu_gate", gel, w_glu, extras=[y2, b_glu, gs], out_dtypes=(F32, BF16),
                         epilogue=lambda acc, y, b, g: (acc, _glu_branch(y, acc, b, g)))
    mixed = _rowwise(
        "attn_out_norm", lambda s, n, g: jnp.concatenate([_rms(s, g).astype(BF16), n.astype(BF16)], axis=1),
        [sb, _from_scan_layout(ssm_n, B, L)], [ga], [(D, BF16)])
    def residual_and_norm(acc, res, g):
        h = acc + res
        return h, _rms(h, g)

    h1, hn = _matmul("out_proj", mixed, w_out, extras=[x2, g2], out_dtypes=(F32, BF16), tn=D,
                     epilogue=residual_and_norm)
    act, a_pre = _matmul("mlp_in", hn, w_mlp_in, out_dtypes=(BF16, BF16), tn=2048,
                         epilogue=lambda acc: (jnp.square(jnp.maximum(acc, 0.0)), acc))

    def loss_fn(acc, h, t):
        diff = acc + h - t
        part = jnp.sum(jnp.sum(diff * diff, axis=0, keepdims=True), axis=1, keepdims=True)
        d = diff * (1.0 / D)
        return d, d, part * (0.5 / D)

    d_out, d_out_b, loss = _matmul("mlp_out", act, w_mlp_out, extras=[h1, target.reshape(T, D)],
                                   out_dtypes=(F32, BF16), sums=[(1, 1)], epilogue=loss_fn, tm=512, tn=D)

    d_apre = _matmul("mlp_out_dx", d_out_b, w_mlp_out, tb=True, extras=[a_pre], out_dtypes=(BF16,), tn=2048,
                     epilogue=lambda acc, ap: (acc * (2.0 * jnp.maximum(ap.astype(F32), 0.0)),))
    both = lambda acc: (acc, acc)
    g_w_mlp_out, g_w_mlp_out_b = _matmul("mlp_out_dw", act, d_out_b, ta=True, out_dtypes=(F32, BF16), epilogue=both)
    g_w_mlp_in, g_w_mlp_in_b = _matmul("mlp_in_dw", hn, d_apre, ta=True, col_blocked=True, out_dtypes=(F32, BF16),
                                       epilogue=both, tn=w_mlp_in.shape[1] // N_DEV)
    if mlp_grads_ready is not None:
        g2 = g2 + mlp_grads_ready(g_w_mlp_out, g_w_mlp_out_b, g_w_mlp_in, g_w_mlp_in_b)

    def norm_bwd_res(dy, h, res, g):
        _, vjp = jax.vjp(_rms, h, g)
        dh, dg = vjp(dy)
        return res + dh, dg

    def norm_bwd_res2(dy, h, res, g):
        d, dg = norm_bwd_res(dy, h, res, g)
        return d, d, dg

    d_h1, d_h1_b, g_norm2 = _matmul("mlp_in_dx", d_apre, w_mlp_in, tb=True, extras=[h1, d_out, g2],
                                    out_dtypes=(F32, BF16), sums=[(1, D)], epilogue=norm_bwd_res2, tm=512, tn=D)

    d_mixed = _matmul("out_proj_dx", d_h1_b, w_out, tb=True, tn=1024)
    g_w_out, g_w_out_b = _matmul("out_proj_dw", mixed, d_h1_b, ta=True, out_dtypes=(F32, BF16), epilogue=both)

    def norm_bwd(h, dy, g):
        _, vjp = jax.vjp(_rms, h, g)
        return vjp(dy)

    d_sb, g_attn_out = _rowwise("attn_out_norm_bwd", norm_bwd, [sb, (d_mixed, 0, SB_WIDTH)], [ga],
                                [(SB_WIDTH, F32)], sums=[(1, SB_WIDTH)])
    d_ssm_n = _to_scan_layout(d_mixed[:, SB_WIDTH:], B, L).reshape(T, SSM_WIDTH)

    def glu_bwd(y, pre_, dy, bg, g):
        _, vjp = jax.vjp(_glu_branch, y, pre_, bg, g)
        d_y, d_pre, d_bg, d_g = vjp(dy)
        return d_y, d_pre, d_bg, d_g

    d_y_direct, d_pre, g_b_glu, g_ssm_out = _rowwise(
        "glu_out_bwd", glu_bwd, [y2, pre, d_ssm_n], [b_glu, gs], [(SSM_WIDTH, F32), (SSM_WIDTH, BF16)],
        sums=[(1, SSM_WIDTH), (1, SSM_WIDTH)])
    g_w_glu, g_w_glu_b = _matmul("glu_gate_dw", gel, d_pre, ta=True, out_dtypes=(F32, BF16), epilogue=both)

    def gelu_bwd(dg, y, dy0):
        _, vjp = jax.vjp(jax.nn.gelu, y)
        return (dy0 + vjp(dg)[0],)

    d_y = _matmul("glu_gate_dx", d_pre, w_glu, tb=True, extras=[y2, d_y_direct], epilogue=gelu_bwd)

    du_p, d_wb, d_cm, d_lr, d_li, g_d = _ssm_bwd(
        d_y.reshape(B, L, SSM_WIDTH), u_p, xs, xin, w_bt, lam_r, lam_i, c_mt, d_skip, B, L, ssm_tile)
    d_bb = _block_diag_take(d_wb.reshape(SSM_COLS, LANES, 2, 512))
    d_c = _block_diag_take(d_cm.transpose(0, 2, 1).reshape(SSM_COLS, LANES, 2, 512))
    g_lam_re, g_lam_im, g_log_dt, g_b_re_t, g_b_im_t = _ssm_prep_bwd(
        lam_re3, lam_im3, log_dt3, b_re_t, b_im_t,
        d_lr.reshape(G, 1, P), d_li.reshape(G, 1, P), d_bb[0], d_bb[1])
    d_q, d_k, d_v, g_q, g_k = _attn_bwd(qn, kn, vb, attn_kept, d_sb, proj, gq8[:, :LANES], gk8[:, :LANES], B, L)

    d_proj = jnp.concatenate([d_q, d_k, d_v, _from_scan_layout(du_p, B, L)], axis=1)
    g_w_in, g_w_in_b = _matmul("proj_dw", xn, d_proj, ta=True, col_blocked=True, out_dtypes=(F32, BF16),
                               epilogue=both, tn=w_in.shape[1] // N_DEV)
    if rest_grads_ready is not None:
        g1 = g1 + rest_grads_ready([g_w_in, g_w_glu, g_w_out], [g_w_in_b, g_w_glu_b, g_w_out_b])
    grad_x, g_norm1 = _matmul("proj_dx", d_proj, w_in, tb=True, extras=[x2, d_h1, g1], sums=[(1, D)],
                              epilogue=norm_bwd_res, tm=512, tn=D)

    small = {
        "norm1_g": g_norm1.reshape(-1),
        "q_norm_g": g_q.reshape(-1),
        "k_norm_g": g_k.reshape(-1),
        "ssm_lambda_re": g_lam_re.reshape(G, P),
        "ssm_lambda_im": g_lam_im.reshape(G, P),
        "ssm_log_dt": g_log_dt.reshape(G),
        "ssm_b_re": g_b_re_t.transpose(0, 2, 1),
        "ssm_b_im": g_b_im_t.transpose(0, 2, 1),
        "ssm_c_re": d_c[0],
        "ssm_c_im": -d_c[1],
        "ssm_d": g_d.reshape(G, Hh),
        "b_glu": g_b_glu.reshape(-1),
        "attn_out_g": g_attn_out.reshape(-1),
        "ssm_out_g": g_ssm_out.reshape(-1),
        "norm2_g": g_norm2.reshape(-1),
    }
    big = {"w_in": g_w_in, "w_glu": g_w_glu, "w_out": g_w_out, "w_mlp_in": g_w_mlp_in, "w_mlp_out": g_w_mlp_out}
    return loss[0, 0], grad_x.reshape(B, L, D), small, big


_ANY = pl.BlockSpec(memory_space=pl.ANY)
_MESH = pl.DeviceIdType.MESH


def _all_gather(name, shards):
    n = len(shards)

    def body(*refs):
        in_refs, out_refs = refs[:n], refs[n:2 * n]
        send_sems, recv_sems, local_sems = refs[2 * n:]
        x, y, c = lax.axis_index("x"), lax.axis_index("y"), lax.axis_index("c")
        me, sibling = (x, y, c), (x, y, 1 - c)
        chips = [(1 - x, y), (x, 1 - y), (1 - x, 1 - y)]

        def copy(a, k, block, to, src=None):
            px, py, pc = block
            rows = out_refs[a].at[4 * px + 2 * py + pc]
            return pltpu.make_async_remote_copy(
                src_ref=rows if src is None else src, dst_ref=rows, send_sem=send_sems.at[a, k],
                recv_sem=recv_sems.at[a, k], device_id=to, device_id_type=_MESH)

        mine = [pltpu.make_async_copy(in_refs[a], out_refs[a].at[4 * x + 2 * y + c], local_sems.at[a]) for a in range(n)]
        first, passed = [], []
        for a in range(n):
            mine[a].start()
            first.append(copy(a, 0, me, sibling, src=in_refs[a]))
            first += [copy(a, 1 + j, me, (*chip, c), src=in_refs[a]) for j, chip in enumerate(chips)]
        for cp in first:
            cp.start()
        for j, chip in enumerate(chips):
            for a in range(n):
                copy(a, 1 + j, (*chip, c), me).wait_recv()
                fwd = copy(a, 4 + j, (*chip, c), sibling)
                fwd.start()
                passed.append(fwd)
        for a in range(n):
            copy(a, 0, sibling, me).wait_recv()
            for j, chip in enumerate(chips):
                copy(a, 4 + j, (*chip, 1 - c), me).wait_recv()
        for cp in first + passed:
            cp.wait_send()
        for cp in mine:
            cp.wait()

    return pl.pallas_call(
        body, name=name,
        in_specs=[_ANY] * n, out_specs=[_ANY] * n,
        out_shape=[jax.ShapeDtypeStruct((N_DEV, *s.shape), s.dtype) for s in shards],
        scratch_shapes=[pltpu.SemaphoreType.DMA((n, 7)), pltpu.SemaphoreType.DMA((n, 7)), pltpu.SemaphoreType.DMA((n,))],
    )(*shards)


_HBM = pl.BlockSpec(memory_space=pltpu.HBM)
_SEM = pl.BlockSpec(memory_space=pltpu.SEMAPHORE)
_EFFECT = pltpu.SideEffectType.DATAFLOW_SIDE_EFFECTING
_FLIPS = [(dx, dy, dc) for dx in (0, 1) for dy in (0, 1) for dc in (0, 1) if (dx, dy, dc) != (0, 0, 0)]


def _exchange_start(name, srcs, lands, per_peer):
    n = len(srcs)

    def body(*refs):
        src_refs, land_refs = refs[:n], refs[n:2 * n]
        send_sems, recv_sems = refs[2 * n:3 * n], refs[3 * n:4 * n]
        token = refs[-1]
        x, y, c = lax.axis_index("x"), lax.axis_index("y"), lax.axis_index("c")
        me = 4 * x + 2 * y + c
        for dx, dy, dc in _FLIPS:
            px, py, pc = (1 - x if dx else x), (1 - y if dy else y), (1 - c if dc else c)
            for a in range(n):
                pltpu.make_async_remote_copy(
                    src_ref=src_refs[a].at[4 * px + 2 * py + pc] if per_peer else src_refs[a],
                    dst_ref=land_refs[a].at[me], send_sem=send_sems[a], recv_sem=recv_sems[a],
                    device_id=(px, py, pc), device_id_type=_MESH).start()
        token[...] = jnp.zeros_like(token)

    hbm = lambda t: pltpu.with_memory_space_constraint(t, pltpu.HBM)
    res = pl.pallas_call(
        body, name=name,
        out_shape=(*[pltpu.SemaphoreType.DMA(())] * (2 * n), *[pltpu.HBM(t.shape, t.dtype) for t in (*srcs, *lands)],
                   jax.ShapeDtypeStruct((8, LANES), F32)),
        in_specs=[_HBM] * (2 * n),
        out_specs=(*[_SEM] * (2 * n), *[_HBM] * (2 * n), pl.BlockSpec(memory_space=pltpu.VMEM)),
        input_output_aliases={i: 2 * n + i for i in range(2 * n)},
        compiler_params=pltpu.CompilerParams(has_side_effects=_EFFECT),
    )(*[hbm(t) for t in (*srcs, *lands)])
    return res[:-1], res[-1]


def _exchange_wait(name, handle, after):
    n = len(handle) // 4
    sems, thru = handle[:2 * n], handle[2 * n:]

    def body(*refs):
        land_refs = refs[n:2 * n]
        send_sems, recv_sems = refs[2 * n:3 * n], refs[3 * n:4 * n]
        me = (lax.axis_index("x"), lax.axis_index("y"), lax.axis_index("c"))
        for a in range(n):
            seven = land_refs[a].at[pl.ds(0, len(_FLIPS))]
            all_copies = pltpu.make_async_remote_copy(
                src_ref=seven, dst_ref=seven, send_sem=send_sems[a], recv_sem=recv_sems[a], device_id=me,
                device_id_type=_MESH)
            all_copies.wait_send()
            all_copies.wait_recv()

    res = pl.pallas_call(
        body, name=name, out_shape=tuple(pltpu.HBM(t.shape, t.dtype) for t in thru),
        in_specs=[*[_HBM] * (2 * n), *[_SEM] * (2 * n), _ANY], out_specs=tuple([_HBM] * (2 * n)),
        input_output_aliases={i: i for i in range(2 * n)},
        compiler_params=pltpu.CompilerParams(has_side_effects=_EFFECT),
    )(*thru, *sems, after)
    return res[n:]


def _adamw_gathered(name, own, parts, me, w, m, v):
    r, c = w.shape
    tr = min(r, 256)

    def body(me_ref, own_ref, p_ref, w_ref, m_ref, v_ref, g_out, d_out, m_out, v_out):
        g = own_ref[...]
        for j in range(N_DEV):
            g = g + p_ref[j].astype(F32)
        delta, m_new, v_new = _adamw(w_ref[...], g, m_ref[...], v_ref[...])
        g_out[...] = g
        d_out[...] = delta
        m_out[...] = m_new
        v_out[...] = v_new

    spec = pl.BlockSpec((tr, c), lambda i, me_ref: (i, 0))
    return pl.pallas_call(
        body, name=name,
        grid_spec=pltpu.PrefetchScalarGridSpec(
            num_scalar_prefetch=1, grid=(r // tr,),
            in_specs=[pl.BlockSpec((None, tr, c), lambda i, me_ref: (me_ref[0], i, 0)),
                      pl.BlockSpec((N_DEV, tr, c), lambda i, me_ref: (0, i, 0)), spec, spec, spec],
            out_specs=[spec] * 4),
        out_shape=[jax.ShapeDtypeStruct((r, c), F32)] * 4,
        compiler_params=_params(("parallel",)),
    )(me, own, parts, w, m, v)


def _adamw(w, g, m, v):
    m = ADAM_B1 * m + (1.0 - ADAM_B1) * g
    v = ADAM_B2 * v + (1.0 - ADAM_B2) * jnp.square(g)
    m_hat = m / (1.0 - ADAM_B1 ** ADAM_STEP)
    v_hat = v / (1.0 - ADAM_B2 ** ADAM_STEP)
    delta = -ADAM_LR * (m_hat / (jnp.sqrt(v_hat) + ADAM_EPS) + ADAM_WD * w)
    return delta, m, v


def _adamw_small(name, parts, w, m, v):
    _, r, c = parts.shape
    tr = 8

    def body(p_ref, w_ref, m_ref, v_ref, g_out, d_out, m_out, v_out):
        g = p_ref[0]
        for j in range(1, N_DEV):
            g = g + p_ref[j]
        delta, m_new, v_new = _adamw(w_ref[...], g, m_ref[...], v_ref[...])
        g_out[...] = g
        d_out[...] = delta
        m_out[...] = m_new
        v_out[...] = v_new

    spec = pl.BlockSpec((tr, c), lambda i: (i, 0))
    return pl.pallas_call(
        body, name=name, grid=(r // tr,),
        in_specs=[pl.BlockSpec((N_DEV, tr, c), lambda i: (0, i, 0)), spec, spec, spec],
        out_specs=[spec] * 4, out_shape=[jax.ShapeDtypeStruct((r, c), F32)] * 4,
        compiler_params=_params(("parallel",)),
    )(parts, w, m, v)


_WEIGHTS = ["norm1_g", "w_in", "q_norm_g", "k_norm_g", "ssm_lambda_re", "ssm_lambda_im", "ssm_log_dt", "ssm_b_re",
            "ssm_b_im", "ssm_c_re", "ssm_c_im", "ssm_d", "w_glu", "b_glu", "attn_out_g", "ssm_out_g", "w_out",
            "norm2_g", "w_mlp_in", "w_mlp_out"]
_BIG = ["w_in", "w_glu", "w_out", "w_mlp_in", "w_mlp_out"]
_SMALL = [n for n in _WEIGHTS if n not in _BIG]
_PACK_COLS = 1024


def _pack(tree, last=None):
    flat = [tree[n].reshape(-1).astype(F32) for n in _SMALL]
    size = sum(f.shape[0] for f in flat)
    rows = -(-(size + 1) // (_PACK_COLS * 8)) * 8
    pad = jnp.zeros((rows * _PACK_COLS - size - 1,), F32)
    tail = jnp.zeros((1,), F32) if last is None else last.reshape(1).astype(F32)
    return jnp.concatenate(flat + [pad, tail]).reshape(rows, _PACK_COLS)


def _unpack(buf, like):
    flat, out, off = buf.reshape(-1), {}, 0
    for n in _SMALL:
        size = like[n].size
        out[n] = flat[off:off + size].reshape(like[n].shape)
        off += size
    return out


def kernel(x, norm1_g, w_in, q_norm_g, k_norm_g, ssm_lambda_re, ssm_lambda_im, ssm_log_dt, ssm_b_re, ssm_b_im, ssm_c_re, ssm_c_im, ssm_d, w_glu, b_glu, attn_out_g, ssm_out_g, w_out, norm2_g, w_mlp_in, w_mlp_out, loss_target, m_norm1_g, m_w_in, m_q_norm_g, m_k_norm_g, m_ssm_lambda_re, m_ssm_lambda_im, m_ssm_log_dt, m_ssm_b_re, m_ssm_b_im, m_ssm_c_re, m_ssm_c_im, m_ssm_d, m_w_glu, m_b_glu, m_attn_out_g, m_ssm_out_g, m_w_out, m_norm2_g, m_w_mlp_in, m_w_mlp_out, v_norm1_g, v_w_in, v_q_norm_g, v_k_norm_g, v_ssm_lambda_re, v_ssm_lambda_im, v_ssm_log_dt, v_ssm_b_re, v_ssm_b_im, v_ssm_c_re, v_ssm_c_im, v_ssm_d, v_w_glu, v_b_glu, v_attn_out_g, v_ssm_out_g, v_w_out, v_norm2_g, v_w_mlp_in, v_w_mlp_out):
    w = dict(zip(_WEIGHTS, (norm1_g, w_in, q_norm_g, k_norm_g, ssm_lambda_re, ssm_lambda_im, ssm_log_dt, ssm_b_re, ssm_b_im, ssm_c_re, ssm_c_im, ssm_d, w_glu, b_glu, attn_out_g, ssm_out_g, w_out, norm2_g, w_mlp_in, w_mlp_out)))
    m = dict(zip(_WEIGHTS, (m_norm1_g, m_w_in, m_q_norm_g, m_k_norm_g, m_ssm_lambda_re, m_ssm_lambda_im, m_ssm_log_dt, m_ssm_b_re, m_ssm_b_im, m_ssm_c_re, m_ssm_c_im, m_ssm_d, m_w_glu, m_b_glu, m_attn_out_g, m_ssm_out_g, m_w_out, m_norm2_g, m_w_mlp_in, m_w_mlp_out)))
    v = dict(zip(_WEIGHTS, (v_norm1_g, v_w_in, v_q_norm_g, v_k_norm_g, v_ssm_lambda_re, v_ssm_lambda_im, v_ssm_log_dt, v_ssm_b_re, v_ssm_b_im, v_ssm_c_re, v_ssm_c_im, v_ssm_d, v_w_glu, v_b_glu, v_attn_out_g, v_ssm_out_g, v_w_out, v_norm2_g, v_w_mlp_in, v_w_mlp_out)))
    core = lax.axis_index("c").astype(jnp.int32).reshape(1)
    chip = (2 * lax.axis_index("x") + lax.axis_index("y")).astype(jnp.int32).reshape(1)

    me = (2 * chip + core).astype(jnp.int32)

    def landing(own=None, like=None):
        own = jnp.zeros_like(like) if own is None else own
        return lax.dynamic_update_slice(lax.empty((N_DEV, *like.shape), like.dtype), own[None], (me[0], 0, 0))

    (w_in_blocks,) = _all_gather("w_in_all_gather", [w_in.astype(BF16)])
    w_in_full = w_in_blocks.transpose(1, 0, 2).reshape(w_in.shape[0], -1)
    late = [n for n in _BIG if n != "w_in"]
    shards = [w[n].astype(BF16) for n in late]
    w_in_blocks, shards = lax.optimization_barrier((w_in_blocks, shards))
    weights_handle, weights_token = _exchange_start(
        "weights_send", shards, [landing(s, s) for s in shards], per_peer=False)

    def late_weights(after):
        got = dict(zip(late, _exchange_wait("weights_arrive", weights_handle, after)))
        return (got["w_glu"].reshape(-1, w_glu.shape[1]), got["w_out"].reshape(-1, w_out.shape[1]),
                got["w_mlp_in"].transpose(1, 0, 2).reshape(w_mlp_in.shape[0], -1),
                got["w_mlp_out"].reshape(-1, w_mlp_out.shape[1]))

    mlp = ["w_mlp_out", "w_mlp_in"]
    sent = {}

    def send_grads(name, names, own, own_b):
        blocks = lambda g, n: g.reshape(N_DEV, *w[n].shape)
        sent[name + "_own"] = [blocks(g, n) for g, n in zip(own, names)]
        srcs = [blocks(g, n) for g, n in zip(own_b, names)]
        sent[name], token = _exchange_start(name, srcs, [landing(like=s[0]) for s in srcs], per_peer=True)
        return token[0, 0]

    def mlp_grads_ready(g_out, g_out_b, g_in, g_in_b):
        return send_grads("mlp_grads_send", mlp, [g_out, g_in], [g_out_b, g_in_b])

    rest = ["w_in", "w_glu", "w_out"]

    def rest_grads_ready(own, own_b):
        return send_grads("rest_grads_send", rest, own, own_b)

    loss_local, grad_x, g_small, g_big = _local_step(
        x, loss_target, {n: w[n] for n in _SMALL}, w_in_full, late_weights, mlp_grads_ready, rest_grads_ready,
        weights_token[0, 0])

    grads, delta, new_m, new_v = {}, {}, {}, {}
    small = _pack(g_small, last=loss_local)
    small_handle, small_token = _exchange_start("small_grads_send", [small], [landing(small, small)], per_peer=False)

    for send, arrive, names in (("mlp_grads_send", "mlp_grads_arrive", mlp),
                                ("rest_grads_send", "rest_grads_arrive", rest)):
        for n, own, part in zip(names, sent[send + "_own"], _exchange_wait(arrive, sent[send], small_token)):
            grads[n], delta[n], new_m[n], new_v[n] = _adamw_gathered("adamw_" + n, own, part, me, w[n], m[n], v[n])

    shards_done = lax.optimization_barrier(tuple(new_v[n] for n in _BIG))
    (small_parts,) = _exchange_wait("small_grads_arrive", small_handle, shards_done[-1])
    packed = _adamw_small("adamw_small", small_parts, _pack(w), _pack(m), _pack(v))
    for tree, buf in zip((grads, delta, new_m, new_v), packed):
        tree.update(_unpack(buf, w))
    loss = packed[0][-1, -1]

    return (loss, grad_x, *[grads[n] for n in _WEIGHTS], *[delta[n] for n in _WEIGHTS],
            *[new_m[n] for n in _WEIGHTS], *[new_v[n] for n in _WEIGHTS])
```

```python
import functools
import math

import jax
import jax.numpy as jnp
from jax import lax
from jax.experimental import pallas as pl
from jax.experimental.pallas import tpu as pltpu

F32 = jnp.float32
BF16 = jnp.bfloat16

EPS = 1e-6
HEAD_DIM = 64
N_HEADS = 8
SB_WIDTH = 512
SSM_WIDTH = 512
SSM_GROUP = 16
SSM_GROUPS = 32
SSM_STATE = 64
QBLOCK = 128
KBLOCK = 256
N_CHUNK = 8
SSM_COLS = 4
LANES = 128
N_DEV = 8

ADAM_LR = 0.001
ADAM_B1 = 0.9
ADAM_B2 = 0.999
ADAM_EPS = 1e-08
ADAM_WD = 0.01
ADAM_STEP = 10

VMEM_LIMIT = 56 * 1024 * 1024

_NT = (((1,), (1,)), ((), ()))
_NN = (((1,), (0,)), ((), ()))
_TN = (((0,), (0,)), ((), ()))


def _dot(a, b, dims=_NN):
    return lax.dot_general(a, b, dims, preferred_element_type=F32)


def _params(sem):
    return pltpu.CompilerParams(dimension_semantics=sem, vmem_limit_bytes=VMEM_LIMIT)


def _matmul(name, a, b, *, ta=False, tb=False, extras=(), epilogue=None, out_dtypes=(F32,), sums=(),
            prologue=None, col_blocked=False, tm=1024, tn=512, tk=4096):
    M, K = (a.shape[1], a.shape[0]) if ta else a.shape
    N = b.shape[0] if tb else b.shape[1]
    tm, tn, tk = min(tm, M), min(tn, N), min(tk, K)
    assert M % tm == 0 and N % tn == 0 and K % tk == 0, (name, M, N, K)
    assert not (sums or prologue) or (tn == N and tk == K), name
    assert not prologue or not (ta or col_blocked), name
    nk = K // tk
    n_ex, n_out, n_sum = len(extras), len(out_dtypes), len(sums)
    n_pro = 1 if prologue else 0
    dims = (((0 if ta else 1,), (1 if tb else 0,)), ((), ()))

    def body(*refs):
        a_ref, b_ref = refs[0], refs[1]
        ex_refs = refs[2 + n_pro:2 + n_pro + n_ex]
        o_refs = refs[2 + n_pro + n_ex:2 + n_pro + n_ex + n_out]
        s_refs = refs[2 + n_pro + n_ex + n_out:2 + n_pro + n_ex + n_out + n_sum]
        k = pl.program_id(2)
        if prologue:
            left = prologue[0](a_ref[...], refs[2][...]).astype(BF16)
            refs[2 + n_pro + n_ex + n_out + n_sum][...] = left
        else:
            left = a_ref[...].astype(BF16)
        part = _dot(left, b_ref[...].astype(BF16), dims)

        def finish(acc):
            outs = (acc,) if epilogue is None else epilogue(acc, *[e[...] for e in ex_refs])
            for o_ref, o in zip(o_refs, outs[:n_out]):
                o_ref[...] = o.astype(o_ref.dtype)
            if n_sum:
                @pl.when(pl.program_id(0) == 0)
                def _():
                    for s_ref in s_refs:
                        s_ref[...] = jnp.zeros_like(s_ref)

                for s_ref, v in zip(s_refs, outs[n_out:]):
                    s_ref[...] += v

        if nk == 1:
            finish(part)
        else:
            acc_ref = refs[-1]

            @pl.when(k == 0)
            def _():
                acc_ref[...] = part

            @pl.when(jnp.logical_and(k > 0, k < nk - 1))
            def _():
                acc_ref[...] += part

            @pl.when(k == nk - 1)
            def _():
                finish(acc_ref[...] + part)

    a_spec = pl.BlockSpec((tk, tm), lambda i, j, k: (k, i)) if ta else pl.BlockSpec((tm, tk), lambda i, j, k: (i, k))
    b_spec = pl.BlockSpec((tn, tk), lambda i, j, k: (j, k)) if tb else pl.BlockSpec((tk, tn), lambda i, j, k: (k, j))
    ex_specs = [pl.BlockSpec((1, tn), lambda i, j, k: (0, j)) if e.shape[0] == 1 else
                pl.BlockSpec((tm, tn), lambda i, j, k: (i, j)) for e in extras]
    if col_blocked:
        out_specs = [pl.BlockSpec((None, tm, tn), lambda i, j, k: (j, i, 0)) for _ in out_dtypes]
        out_shape = [jax.ShapeDtypeStruct((N // tn, M, tn), dt) for dt in out_dtypes]
    else:
        wide = [(dt, N) if not isinstance(dt, tuple) else dt for dt in out_dtypes]
        assert all(w == N for _, w in wide) or tn == N, name
        out_specs = [pl.BlockSpec((tm, tn if w == N else w), lambda i, j, k: (i, j)) for _, w in wide]
        out_shape = [jax.ShapeDtypeStruct((M, w), dt) for dt, w in wide]
    out_specs += [pl.BlockSpec(s, lambda i, j, k: (0, 0)) for s in sums]
    out_shape += [jax.ShapeDtypeStruct(s, F32) for s in sums]
    pro_specs, pro_args = [], []
    if prologue:
        pro_specs, pro_args = [pl.BlockSpec((1, tk), lambda i, j, k: (0, 0))], [prologue[1]]
        out_specs.append(pl.BlockSpec((tm, tk), lambda i, j, k: (i, 0)))
        out_shape.append(jax.ShapeDtypeStruct((M, K), BF16))
    outs = pl.pallas_call(
        body, name=name, grid=(M // tm, N // tn, nk),
        in_specs=[a_spec, b_spec, *pro_specs, *ex_specs], out_specs=out_specs, out_shape=out_shape,
        scratch_shapes=[pltpu.VMEM((tm, tn), F32)] if nk > 1 else [],
        compiler_params=_params(("arbitrary",) * 3 if sums else ("parallel", "parallel", "arbitrary")),
    )(a, b, *pro_args, *extras)
    return outs[0] if len(outs) == 1 else outs


def _rowwise(name, fn, rows, small, outs, sums=(), tile=512):
    specs, args = [], []
    T = None
    for r in rows:
        arr, cb, w = r if isinstance(r, tuple) else (r, 0, r.shape[1])
        T = arr.shape[0]
        specs.append((w, cb))
        args.append(arr)
    tile = min(tile, T)
    assert T % tile == 0
    n_r, n_s, n_o, n_a = len(rows), len(small), len(outs), len(sums)

    def body(*refs):
        r_refs = refs[:n_r]
        s_refs = refs[n_r:n_r + n_s]
        o_refs = refs[n_r + n_s:n_r + n_s + n_o]
        a_refs = refs[n_r + n_s + n_o:]
        res = fn(*[r[...] for r in r_refs], *[s[...] for s in s_refs])
        res = res if isinstance(res, (tuple, list)) else (res,)
        for o_ref, o in zip(o_refs, res[:n_o]):
            o_ref[...] = o.astype(o_ref.dtype)

        @pl.when(pl.program_id(0) == 0)
        def _():
            for a_ref in a_refs:
                a_ref[...] = jnp.zeros_like(a_ref)

        for a_ref, v in zip(a_refs, res[n_o:]):
            a_ref[...] += v.astype(F32)

    in_specs = [pl.BlockSpec((tile, w), functools.partial(lambda i, cb: (i, cb), cb=cb)) for w, cb in specs]
    in_specs += [pl.BlockSpec(s.shape, functools.partial(lambda i, nd: (0,) * nd, nd=s.ndim)) for s in small]
    out_specs = [pl.BlockSpec((tile, w), lambda i: (i, 0)) for w, _ in outs]
    out_specs += [pl.BlockSpec(s, functools.partial(lambda i, nd: (0,) * nd, nd=len(s))) for s in sums]
    out_shape = [jax.ShapeDtypeStruct((T, w), dt) for w, dt in outs]
    out_shape += [jax.ShapeDtypeStruct(s, F32) for s in sums]
    res = pl.pallas_call(
        body, name=name, grid=(T // tile,), in_specs=in_specs, out_specs=out_specs, out_shape=out_shape,
        compiler_params=_params(("arbitrary",)),
    )(*args, *small)
    return res[0] if len(res) == 1 else res


def _rms(x, g):
    return x * lax.rsqrt(jnp.mean(x * x, axis=-1, keepdims=True) + EPS) * g


def _glu_branch(y, pre, b_glu, g_out):
    g = jax.nn.gelu(y)
    return _rms(g * jax.nn.sigmoid(pre + b_glu), g_out)


def _split_dot(x, tri_bf):
    hi = x.astype(BF16)
    lo = (x - hi.astype(F32)).astype(BF16)
    return _dot(hi, tri_bf) + _dot(lo, tri_bf)


def _softplus(z):
    return jnp.maximum(z, 0.0) + jnp.log(1.0 + jnp.exp(-jnp.abs(z)))


def _head(h):
    return slice(h * HEAD_DIM, (h + 1) * HEAD_DIM)


def _head_mean(x, seg):
    return _split_dot(x, seg) * (1.0 / HEAD_DIM)


def _qk_norm(proj, gains):
    r = lax.div(lax.broadcasted_iota(jnp.int32, (SB_WIDTH, SB_WIDTH), 0), HEAD_DIM)
    c = lax.div(lax.broadcasted_iota(jnp.int32, (SB_WIDTH, SB_WIDTH), 1), HEAD_DIM)
    seg = (r == c).astype(BF16)
    q, k, v = (proj[:, i * SB_WIDTH:(i + 1) * SB_WIDTH] for i in range(3))
    qn = q * lax.rsqrt(_head_mean(q * q, seg) + EPS) * gains[:, 0:SB_WIDTH]
    kn = k * lax.rsqrt(_head_mean(k * k, seg) + EPS) * gains[:, SB_WIDTH:2 * SB_WIDTH]
    return proj, qn, kn, v


def _qk_norm_bwd(x, g, dy, seg):
    r = lax.rsqrt(_head_mean(x * x, seg) + EPS)
    gdy = g * dy
    dx = r * gdy - x * (r * r * r) * _head_mean(gdy * x, seg)
    return dx, jnp.sum(dy * x * r, axis=0, keepdims=True)


def _split_heads(refs, scratch, L):
    def chunk(i, _):
        r = pl.ds(pl.multiple_of(i * QBLOCK, QBLOCK), QBLOCK)
        for ref, s in zip(refs, scratch):
            for h in range(2):
                s[h, r, :] = ref[r, _head(h)]
        return 0

    lax.fori_loop(0, L // QBLOCK, chunk, 0)


Q_HALVES = KBLOCK // QBLOCK
_CHAINS = [(h, r) for h in range(2) for r in range(Q_HALVES)]


def _valid(i, kb):
    row = lax.broadcasted_iota(jnp.int32, (QBLOCK, KBLOCK), 0)
    col = lax.broadcasted_iota(jnp.int32, (QBLOCK, KBLOCK), 1)
    return col + (kb * KBLOCK - i * QBLOCK) < row


def _attn_fwd(qn, kn, vb, B, L):
    n_pairs = L // KBLOCK
    n_hp = N_HEADS // 2
    nc = len(_CHAINS)

    def body(q_ref, k_ref, v_ref, o_ref, a_ref, q_s, k_s, v_s, after_s, z_s, stage_s, sems):
        _split_heads((q_ref, k_ref, v_ref), (q_s, k_s, v_s), L)
        r2 = lax.broadcasted_iota(jnp.int32, (KBLOCK, KBLOCK), 0)
        c2 = lax.broadcasted_iota(jnp.int32, (KBLOCK, KBLOCK), 1)
        after_s[...] = (r2 > c2).astype(after_s.dtype)
        g = pl.program_id(0) * n_hp + pl.program_id(1)

        def q_pair(p, _):
            rows = [pl.ds(pl.multiple_of((p * Q_HALVES + r) * QBLOCK, QBLOCK), QBLOCK) for r in range(Q_HALVES)]
            q_c = [q_s[h, rows[r], :] for h, r in _CHAINS]
            cs = range(nc)

            def scores(kb):
                rk = pl.ds(pl.multiple_of(kb * KBLOCK, KBLOCK), KBLOCK)
                return [_dot(q_c[c], k_s[_CHAINS[c][0], rk, :], _NT) for c in cs]

            def saved(kb):
                return pltpu.make_async_copy(stage_s.at[kb & 1], a_ref.at[g, p, kb], sems.at[kb & 1])

            def k_block(kb, carry, diagonal):
                rk = pl.ds(pl.multiple_of(kb * KBLOCK, KBLOCK), KBLOCK)
                if diagonal:
                    valid = [_valid(p * Q_HALVES + r, kb) for r in range(Q_HALVES)]
                    keep = lambda c, t: jnp.where(valid[_CHAINS[c][1]], t, 0.0)
                    z = scores(kb)
                else:
                    keep = lambda c, t: t
                    z = [z_s[(kb + 1) & 1, c] for c in cs]
                ahead = scores(jnp.maximum(kb - 1, 0))
                for c in cs:
                    z_s[kb & 1, c] = ahead[c]
                sp = [_softplus(z[c]) for c in cs]
                lsig = [z[c] - sp[c] for c in cs]
                lom = [keep(c, -sp[c]) for c in cs]
                tail = [_split_dot(lom[c], after_s[...]) + carry[c][0] for c in cs]
                a = [keep(c, jnp.exp(lsig[c] + tail[c])).astype(v_s.dtype) for c in cs]
                acc = [carry[c][1] + _dot(a[c], v_s[_CHAINS[c][0], rk, :]) for c in cs]
                for c in cs:
                    stage_s[kb & 1, c] = a[c]
                saved(kb).start()
                return tuple((carry[c][0] + jnp.sum(lom[c], axis=1, keepdims=True), acc[c]) for c in cs)

            def next_block(n, carry):
                kb = p - n

                @pl.when(n >= 2)
                def _():
                    saved(kb + 2).wait()

                return k_block(kb, carry, False)

            init = (jnp.zeros((QBLOCK, 1), F32), jnp.zeros((QBLOCK, HEAD_DIM), F32))
            first = k_block(p, (init,) * nc, True)
            res = lax.fori_loop(1, p + 1, next_block, first)
            saved(0).wait()

            @pl.when(p >= 1)
            def _():
                saved(1).wait()

            for r in range(Q_HALVES):
                o_ref[rows[r], :] = jnp.concatenate([res[c][1] for c in cs if _CHAINS[c][1] == r], axis=1)
            return 0

        lax.fori_loop(0, n_pairs, q_pair, 0)

    spec = pl.BlockSpec((L, LANES), lambda b, p: (b, p))
    return pl.pallas_call(
        body, name="attn_fwd", grid=(B, n_hp),
        in_specs=[spec] * 3, out_specs=[spec, _ANY],
        out_shape=[jax.ShapeDtypeStruct((B * L, SB_WIDTH), F32),
                   jax.ShapeDtypeStruct((B * n_hp, n_pairs, n_pairs, nc, QBLOCK, KBLOCK), BF16)],
        scratch_shapes=[pltpu.VMEM((2, L, HEAD_DIM), BF16)] * 3 + [pltpu.VMEM((KBLOCK, KBLOCK), BF16)]
        + [pltpu.VMEM((2, nc, QBLOCK, KBLOCK), F32), pltpu.VMEM((2, nc, QBLOCK, KBLOCK), BF16),
           pltpu.SemaphoreType.DMA((2,))],
        compiler_params=_params(("parallel", "parallel")),
    )(qn, kn, vb)


def _attn_bwd(qn, kn, vb, kept_a, d_sb, proj, gq2, gk2, B, L):
    n_pairs = L // KBLOCK
    n_hp = N_HEADS // 2
    nc = len(_CHAINS)
    slots = 3
    scale = 1.0 / math.sqrt(HEAD_DIM)

    def body(q_ref, k_ref, v_ref, do_ref, qraw_ref, kraw_ref, gq_ref, gk_ref, a_ref,
             dq_ref, dk_ref, dv_ref, dgq_ref, dgk_ref,
             q_s, k_s, v_s, qt_s, dkt_s, dvt_s, before_s, stage_s, sems):
        _split_heads((q_ref, k_ref, v_ref), (q_s, k_s, v_s), L)
        g = pl.program_id(0) * n_hp + pl.program_id(1)
        lane_head = [lax.div(lax.broadcasted_iota(jnp.int32, (LANES, LANES), d), HEAD_DIM) for d in (0, 1)]
        seg = (lane_head[0] == lane_head[1]).astype(BF16)

        def transpose_q(i, _):
            r = pl.ds(pl.multiple_of(i * QBLOCK, QBLOCK), QBLOCK)
            qt_s[:, r] = q_ref[r, :].astype(F32).T.astype(qt_s.dtype)
            return 0

        lax.fori_loop(0, L // QBLOCK, transpose_q, 0)
        dkt_s[...] = jnp.zeros_like(dkt_s)
        dvt_s[...] = jnp.zeros_like(dvt_s)
        r2 = lax.broadcasted_iota(jnp.int32, (KBLOCK, KBLOCK), 0)
        c2 = lax.broadcasted_iota(jnp.int32, (KBLOCK, KBLOCK), 1)
        before_s[...] = (r2 < c2).astype(before_s.dtype)

        def q_pair(p, dgq):
            rows = [pl.ds(pl.multiple_of((p * Q_HALVES + r) * QBLOCK, QBLOCK), QBLOCK) for r in range(Q_HALVES)]
            pair = pl.ds(pl.multiple_of(p * KBLOCK, KBLOCK), KBLOCK)
            do2 = do_ref[pair, :]
            do_t = do2.T.astype(v_s.dtype)
            cs = range(nc)
            hs = range(2)
            q_c = [q_s[h, rows[r], :] for h, r in _CHAINS]
            do_c = [do2[r * QBLOCK:(r + 1) * QBLOCK, _head(h)].astype(v_s.dtype) for h, r in _CHAINS]
            qt_h = [qt_s[_head(h), pair] for h in hs]
            dot_h = [do_t[_head(h), :] for h in hs]

            def kept(kb):
                slot = lax.rem(kb, slots)
                return pltpu.make_async_copy(a_ref.at[g, p, kb], stage_s.at[slot], sems.at[slot])

            def k_block(kb, carry, diagonal):
                rk = pl.ds(pl.multiple_of(kb * KBLOCK, KBLOCK), KBLOCK)
                if diagonal:
                    valid = [_valid(p * Q_HALVES + r, kb) for r in range(Q_HALVES)]
                    keep = lambda c, t: jnp.where(valid[_CHAINS[c][1]], t, 0.0)
                else:
                    keep = lambda c, t: t

                    @pl.when(kb + 2 <= p)
                    def _():
                        kept(kb + 2).start()

                kept(kb).wait()
                slot = lax.rem(kb, slots)
                k_b = [k_s[h, rk, :] for h in hs]
                z = [_dot(q_c[c], k_b[_CHAINS[c][0]], _NT) for c in cs]
                da = [_dot(do_c[c], v_s[_CHAINS[c][0], rk, :], _NT) for c in cs]
                a = [stage_s[slot, c] for c in cs]
                dla = [a[c].astype(F32) * da[c] for c in cs]
                for h in hs:
                    a_h = jnp.concatenate([a[c] for c in cs if _CHAINS[c][0] == h], axis=0)
                    dvt_s[_head(h), rk] += _dot(dot_h[h], a_h)
                d_lom = [carry[c][0] + _dot(dla[c].astype(BF16), before_s[...]) for c in cs]
                beta = [jax.nn.sigmoid(z[c]) for c in cs]
                dz_b = [(dla[c] * (1.0 - beta[c]) - keep(c, beta[c] * d_lom[c])).astype(v_s.dtype) for c in cs]
                dq_acc = [carry[c][1] + _dot(dz_b[c], k_b[_CHAINS[c][0]]) for c in cs]
                for h in hs:
                    dz_h = jnp.concatenate([dz_b[c] for c in cs if _CHAINS[c][0] == h], axis=0)
                    dkt_s[_head(h), rk] += _dot(qt_h[h], dz_h)
                return tuple((carry[c][0] + jnp.sum(dla[c], axis=1, keepdims=True), dq_acc[c]) for c in cs)

            init = (jnp.zeros((QBLOCK, 1), F32), jnp.zeros((QBLOCK, HEAD_DIM), F32))
            kept(0).start()

            @pl.when(p >= 1)
            def _():
                kept(1).start()

            before = lax.fori_loop(0, p, lambda kb, carry: k_block(kb, carry, False), (init,) * nc)
            res = k_block(p, before, True)
            for r in range(Q_HALVES):
                d_qn = jnp.concatenate([res[c][1] for c in cs if _CHAINS[c][1] == r], axis=1) * scale
                dq, dg = _qk_norm_bwd(qraw_ref[rows[r], :], gq_ref[...], d_qn, seg)
                dq_ref[rows[r], :] = dq.astype(dq_ref.dtype)
                dgq = dgq + dg
            return dgq

        dgq = lax.fori_loop(0, n_pairs, q_pair, jnp.zeros((1, LANES), F32))

        def transpose_out(i, dgk):
            r = pl.ds(pl.multiple_of(i * QBLOCK, QBLOCK), QBLOCK)
            dk, dg = _qk_norm_bwd(kraw_ref[r, :], gk_ref[...], dkt_s[:, r].T, seg)
            dk_ref[r, :] = dk.astype(dk_ref.dtype)
            dv_ref[r, :] = dvt_s[:, r].T.astype(dv_ref.dtype)
            return dgk + dg

        dgk = lax.fori_loop(0, L // QBLOCK, transpose_out, jnp.zeros((1, LANES), F32))

        @pl.when(jnp.logical_and(pl.program_id(0) == 0, pl.program_id(1) == 0))
        def _():
            dgq_ref[...] = jnp.zeros_like(dgq_ref)
            dgk_ref[...] = jnp.zeros_like(dgk_ref)

        dgq_ref[...] += dgq[:, _head(0)] + dgq[:, _head(1)]
        dgk_ref[...] += dgk[:, _head(0)] + dgk[:, _head(1)]

    spec = pl.BlockSpec((L, LANES), lambda b, p: (b, p))
    gain = pl.BlockSpec((1, LANES), lambda b, p: (0, 0))
    gain_grad = pl.BlockSpec((1, HEAD_DIM), lambda b, p: (0, 0))
    return pl.pallas_call(
        body, name="attn_bwd", grid=(B, n_hp),
        in_specs=[spec] * 4 + [spec, pl.BlockSpec((L, LANES), lambda b, p: (b, n_hp + p)), gain, gain, _ANY],
        out_specs=[spec] * 3 + [gain_grad] * 2,
        out_shape=[jax.ShapeDtypeStruct((B * L, SB_WIDTH), BF16)] * 3 + [jax.ShapeDtypeStruct((1, HEAD_DIM), F32)] * 2,
        scratch_shapes=[pltpu.VMEM((2, L, HEAD_DIM), BF16)] * 3 + [pltpu.VMEM((LANES, L), BF16)]
        + [pltpu.VMEM((LANES, L), F32)] * 2 + [pltpu.VMEM((KBLOCK, KBLOCK), BF16)]
        + [pltpu.VMEM((slots, nc, QBLOCK, KBLOCK), BF16), pltpu.SemaphoreType.DMA((slots,))],
        compiler_params=_params(("arbitrary", "arbitrary")),
    )(qn, kn, vb, d_sb, proj, proj, gq2, gk2, kept_a)


def _ssm_discretise(lam_re, lam_im, log_dt, b_re, b_im):
    dt = jnp.exp(log_dt)
    mag = jnp.exp(lam_re * dt)
    lbr = mag * jnp.cos(lam_im * dt)
    lbi = mag * jnp.sin(lam_im * dt)
    den = lam_re * lam_re + lam_im * lam_im
    nr, ni = lbr - 1.0, lbi
    cr = (nr * lam_re + ni * lam_im) / den
    ci = (ni * lam_re - nr * lam_im) / den
    return lbr, lbi, cr * b_re - ci * b_im, cr * b_im + ci * b_re


def _ssm_prep(lam_re, lam_im, log_dt, b_re_t, b_im_t):
    def body(lr, li, ld, br, bi, o_lr, o_li, o_br, o_bi):
        res = _ssm_discretise(lr[...], li[...], ld[...], br[...], bi[...])
        for o, v in zip((o_lr, o_li, o_br, o_bi), res):
            o[...] = v

    return pl.pallas_call(
        body, name="ssm_prep",
        out_shape=[jax.ShapeDtypeStruct(lam_re.shape, F32)] * 2 + [jax.ShapeDtypeStruct(b_re_t.shape, F32)] * 2,
    )(lam_re, lam_im, log_dt, b_re_t, b_im_t)


def _ssm_prep_bwd(lam_re, lam_im, log_dt, b_re_t, b_im_t, d_lr, d_li, d_br, d_bi):
    def body(lr, li, ld, br, bi, g_lr, g_li, g_br, g_bi, o_lr, o_li, o_ld, o_br, o_bi):
        _, vjp = jax.vjp(_ssm_discretise, lr[...], li[...], ld[...], br[...], bi[...])
        res = vjp((g_lr[...], g_li[...], g_br[...], g_bi[...]))
        for o, v in zip((o_lr, o_li, o_ld, o_br, o_bi), res):
            o[...] = v

    return pl.pallas_call(
        body, name="ssm_prep_bwd",
        out_shape=[jax.ShapeDtypeStruct(lam_re.shape, F32)] * 2 + [jax.ShapeDtypeStruct(log_dt.shape, F32)]
        + [jax.ShapeDtypeStruct(b_re_t.shape, F32)] * 2,
    )(lam_re, lam_im, log_dt, b_re_t, b_im_t, d_lr, d_li, d_br, d_bi)


def _block_diag(m):
    m4 = m.reshape(SSM_COLS, 8, SSM_GROUP, SSM_STATE)
    return jnp.einsum("aghp,gk->aghkp", m4, jnp.eye(8, dtype=m.dtype)).reshape(SSM_COLS, LANES, 512)


def _block_diag_take(d):
    d6 = d.reshape(SSM_COLS, 8, SSM_GROUP, 2, 8, SSM_STATE)
    return jnp.einsum("aghrgp->raghp", d6).reshape(2, SSM_GROUPS, SSM_GROUP, SSM_STATE)


def _cmul(ar, ai, br, bi):
    return ar * br - ai * bi, ar * bi + ai * br


def _power(lr, li, n):
    assert n & (n - 1) == 0
    for _ in range(n.bit_length() - 1):
        lr, li = _cmul(lr, li, lr, li)
    return lr, li


def _ssm_fwd(u_p, w_b, lam_r, lam_i, c_m, d_skip, B, L, tj):
    J = L // N_CHUNK
    njt = J // tj
    R = tj * N_CHUNK
    H = 512

    def body(u_ref, wb_ref, lr_ref, li_ref, cm_ref, d_ref, y_ref, gel_ref, x_ref, xin_ref, bu_s, st_s, xin_s):
        ph, jt = pl.program_id(2), pl.program_id(3)
        lr, li = lr_ref[...], li_ref[...]

        @pl.when(jnp.logical_and(ph == 0, jt == 0))
        def _():
            st_s[...] = jnp.zeros_like(st_s)

        @pl.when(ph == 0)
        def _():
            bu_s[jt] = _dot(u_ref[...].astype(BF16), wb_ref[...].astype(BF16))

        def scan(store):
            def step(j, carry):
                xr, xi = carry
                r = pl.ds(pl.multiple_of(j * N_CHUNK, N_CHUNK), N_CHUNK)
                nr = lr * xr - li * xi + bu_s[jt, r, 0:H]
                ni = lr * xi + li * xr + bu_s[jt, r, H:2 * H]
                if store:
                    x_ref[r, 0:H] = nr
                    x_ref[r, H:2 * H] = ni
                return nr, ni

            xr, xi = lax.fori_loop(0, tj, step, (st_s[:, 0:H], st_s[:, H:2 * H]))
            st_s[:, 0:H] = xr
            st_s[:, H:2 * H] = xi

        @pl.when(ph == 0)
        def _():
            scan(False)

            @pl.when(jt == njt - 1)
            def _():
                pr, pi = _power(lr[0:1], li[0:1], J)
                xin_s[0:1, :] = jnp.zeros((1, 2 * H), F32)
                for c in range(1, N_CHUNK):
                    qr, qi = _cmul(pr, pi, xin_s[c - 1:c, 0:H], xin_s[c - 1:c, H:2 * H])
                    xin_s[c:c + 1, 0:H] = qr + st_s[c - 1:c, 0:H]
                    xin_s[c:c + 1, H:2 * H] = qi + st_s[c - 1:c, H:2 * H]
                xin_ref[...] = xin_s[...]
                st_s[...] = xin_s[...]

        @pl.when(ph == 1)
        def _():
            scan(True)
            y = _dot(x_ref[...].astype(BF16), cm_ref[...].astype(BF16)) + d_ref[...] * u_ref[...]
            y_ref[...] = y
            gel_ref[...] = jax.nn.gelu(y).astype(gel_ref.dtype)

    return pl.pallas_call(
        body, name="ssm_fwd", grid=(SSM_COLS, B, 2, njt),
        in_specs=[
            pl.BlockSpec((None, R, LANES), lambda i, b, ph, jt: (b, jt, i)),
            pl.BlockSpec((None, LANES, 2 * H), lambda i, b, ph, jt: (i, 0, 0)),
            pl.BlockSpec((None, N_CHUNK, H), lambda i, b, ph, jt: (i, 0, 0)),
            pl.BlockSpec((None, N_CHUNK, H), lambda i, b, ph, jt: (i, 0, 0)),
            pl.BlockSpec((None, 2 * H, LANES), lambda i, b, ph, jt: (i, 0, 0)),
            pl.BlockSpec((1, LANES), lambda i, b, ph, jt: (0, i)),
        ],
        out_specs=[
            pl.BlockSpec((None, R, LANES), lambda i, b, ph, jt: (b, jt * ph, i)),
            pl.BlockSpec((None, R, LANES), lambda i, b, ph, jt: (b, jt * ph, i)),
            pl.BlockSpec((None, R, 2 * H), lambda i, b, ph, jt: (b, jt * ph, i)),
            pl.BlockSpec((None, None, N_CHUNK, 2 * H), lambda i, b, ph, jt: (b, i, 0, 0)),
        ],
        out_shape=[
            jax.ShapeDtypeStruct((B, L, SSM_WIDTH), F32),
            jax.ShapeDtypeStruct((B, L, SSM_WIDTH), BF16),
            jax.ShapeDtypeStruct((B, L, SSM_COLS * 2 * H), F32),
            jax.ShapeDtypeStruct((B, SSM_COLS, N_CHUNK, 2 * H), F32),
        ],
        scratch_shapes=[pltpu.VMEM((njt, R, 2 * H), F32), pltpu.VMEM((N_CHUNK, 2 * H), F32),
                        pltpu.VMEM((N_CHUNK, 2 * H), F32)],
        compiler_params=_params(("arbitrary",) * 4),
    )(u_p, w_b, lam_r, lam_i, c_m, d_skip)


def _ssm_bwd(dy_p, u_p, x, xin, w_bt, lam_r, lam_i, c_mt, d_skip, B, L, tj):
    J = L // N_CHUNK
    njt = J // tj
    R = tj * N_CHUNK
    H = 512
    x4 = x.reshape(B, J, N_CHUNK, SSM_COLS * 2 * H)

    def body(dy_ref, u_ref, x_ref, xp_ref, xin_ref, wbt_ref, lr_ref, li_ref, cmt_ref, d_ref,
             du_ref, dwb_ref, dcm_ref, dlr_ref, dli_ref, dd_ref, ca_s, a_s, st_s, dl_s):
        b, ph, jt = pl.program_id(1), pl.program_id(2), pl.program_id(3)
        jr = njt - 1 - jt
        lr, li = lr_ref[...], -li_ref[...]

        @pl.when(jnp.logical_and(b == 0, jnp.logical_and(ph == 0, jt == 0)))
        def _():
            dwb_ref[...] = jnp.zeros_like(dwb_ref)
            dcm_ref[...] = jnp.zeros_like(dcm_ref)
            dlr_ref[...] = jnp.zeros_like(dlr_ref)
            dli_ref[...] = jnp.zeros_like(dli_ref)
            dd_ref[...] = jnp.zeros_like(dd_ref)
            dl_s[...] = jnp.zeros_like(dl_s)

        @pl.when(jnp.logical_and(ph == 0, jt == 0))
        def _():
            st_s[...] = jnp.zeros_like(st_s)

        @pl.when(ph == 0)
        def _():
            ca_s[jt] = _dot(dy_ref[...].astype(BF16), cmt_ref[...].astype(BF16))

        def scan(store):
            def step(n, carry):
                ar, ai = carry
                r = pl.ds(pl.multiple_of((tj - 1 - n) * N_CHUNK, N_CHUNK), N_CHUNK)
                nr = lr * ar - li * ai + ca_s[jt, r, 0:H]
                ni = lr * ai + li * ar + ca_s[jt, r, H:2 * H]
                if store:
                    a_s[r, 0:H] = nr
                    a_s[r, H:2 * H] = ni
                return nr, ni

            ar, ai = lax.fori_loop(0, tj, step, (st_s[:, 0:H], st_s[:, H:2 * H]))
            st_s[:, 0:H] = ar
            st_s[:, H:2 * H] = ai

        @pl.when(ph == 0)
        def _():
            scan(False)

            @pl.when(jt == njt - 1)
            def _():
                pr, pi = _power(lr[0:1], li[0:1], J)
                a_s[N_CHUNK - 1:N_CHUNK, :] = jnp.zeros((1, 2 * H), F32)
                for c in range(N_CHUNK - 2, -1, -1):
                    qr, qi = _cmul(pr, pi, a_s[c + 1:c + 2, 0:H], a_s[c + 1:c + 2, H:2 * H])
                    a_s[c:c + 1, 0:H] = qr + st_s[c + 1:c + 2, 0:H]
                    a_s[c:c + 1, H:2 * H] = qi + st_s[c + 1:c + 2, H:2 * H]
                st_s[...] = a_s[0:N_CHUNK, :]

        @pl.when(ph == 1)
        def _():
            scan(True)
            dy = dy_ref[...]
            u = u_ref[...]
            a_b = a_s[...].astype(BF16)
            du_ref[...] = (_dot(a_b, wbt_ref[...].astype(BF16)) + d_ref[...] * dy).astype(du_ref.dtype)
            dwb_ref[...] += _dot(u.astype(BF16), a_b, _TN)
            dcm_ref[...] += _dot(x_ref[...].astype(BF16), dy.astype(BF16), _TN)
            dd_ref[...] += jnp.sum(dy * u, axis=0, keepdims=True)

            first = jnp.where(jr == 0, xin_ref[...], xp_ref[...])
            a0r, a0i = a_s[0:N_CHUNK, 0:H], a_s[0:N_CHUNK, H:2 * H]
            acc0 = (a0r * first[:, 0:H] + a0i * first[:, H:2 * H], a0i * first[:, 0:H] - a0r * first[:, H:2 * H])

            def step(j, carry):
                sr, si = carry
                r = pl.ds(pl.multiple_of(j * N_CHUNK, N_CHUNK), N_CHUNK)
                rp = pl.ds(pl.multiple_of((j - 1) * N_CHUNK, N_CHUNK), N_CHUNK)
                ar, ai = a_s[r, 0:H], a_s[r, H:2 * H]
                xr, xi = x_ref[rp, 0:H], x_ref[rp, H:2 * H]
                return sr + ar * xr + ai * xi, si + ai * xr - ar * xi

            sr, si = lax.fori_loop(1, tj, step, acc0)
            dl_s[:, 0:H] += sr
            dl_s[:, H:2 * H] += si

            @pl.when(jnp.logical_and(b == B - 1, jt == njt - 1))
            def _():
                dlr_ref[...] = jnp.sum(dl_s[:, 0:H], axis=0, keepdims=True)
                dli_ref[...] = jnp.sum(dl_s[:, H:2 * H], axis=0, keepdims=True)
                dl_s[...] = jnp.zeros_like(dl_s)

    rev = lambda ph, jt: (njt - 1 - jt) * ph + (njt - 1) * (1 - ph)
    return pl.pallas_call(
        body, name="ssm_bwd", grid=(SSM_COLS, B, 2, njt),
        in_specs=[
            pl.BlockSpec((None, R, LANES), lambda i, b, ph, jt: (b, njt - 1 - jt, i)),
            pl.BlockSpec((None, R, LANES), lambda i, b, ph, jt: (b, njt - 1 - jt, i)),
            pl.BlockSpec((None, R, 2 * H), lambda i, b, ph, jt: (b, rev(ph, jt), i)),
            pl.BlockSpec((None, None, N_CHUNK, 2 * H),
                         lambda i, b, ph, jt: (b, jnp.maximum((njt - 1 - jt) * tj - 1, 0), 0, i)),
            pl.BlockSpec((None, None, N_CHUNK, 2 * H), lambda i, b, ph, jt: (b, i, 0, 0)),
            pl.BlockSpec((None, 2 * H, LANES), lambda i, b, ph, jt: (i, 0, 0)),
            pl.BlockSpec((None, N_CHUNK, H), lambda i, b, ph, jt: (i, 0, 0)),
            pl.BlockSpec((None, N_CHUNK, H), lambda i, b, ph, jt: (i, 0, 0)),
            pl.BlockSpec((None, LANES, 2 * H), lambda i, b, ph, jt: (i, 0, 0)),
            pl.BlockSpec((1, LANES), lambda i, b, ph, jt: (0, i)),
        ],
        out_specs=[
            pl.BlockSpec((None, R, LANES), lambda i, b, ph, jt: (b, rev(ph, jt), i)),
            pl.BlockSpec((None, LANES, 2 * H), lambda i, b, ph, jt: (i, 0, 0)),
            pl.BlockSpec((None, 2 * H, LANES), lambda i, b, ph, jt: (i, 0, 0)),
            pl.BlockSpec((None, 1, H), lambda i, b, ph, jt: (i, 0, 0)),
            pl.BlockSpec((None, 1, H), lambda i, b, ph, jt: (i, 0, 0)),
            pl.BlockSpec((1, LANES), lambda i, b, ph, jt: (0, i)),
        ],
        out_shape=[
            jax.ShapeDtypeStruct((B, L, SSM_WIDTH), BF16),
            jax.ShapeDtypeStruct((SSM_COLS, LANES, 2 * H), F32),
            jax.ShapeDtypeStruct((SSM_COLS, 2 * H, LANES), F32),
            jax.ShapeDtypeStruct((SSM_COLS, 1, H), F32),
            jax.ShapeDtypeStruct((SSM_COLS, 1, H), F32),
            jax.ShapeDtypeStruct((1, SSM_WIDTH), F32),
        ],
        scratch_shapes=[pltpu.VMEM((njt, R, 2 * H), F32), pltpu.VMEM((R, 2 * H), F32),
                        pltpu.VMEM((N_CHUNK, 2 * H), F32), pltpu.VMEM((N_CHUNK, 2 * H), F32)],
        compiler_params=_params(("arbitrary",) * 4),
    )(dy_p, u_p, x, x4, xin, w_bt, lam_r, lam_i, c_mt, d_skip)


def _to_scan_layout(t, B, L):
    C = t.shape[-1]
    return t.reshape(B, N_CHUNK, L // N_CHUNK, C).transpose(0, 2, 1, 3).reshape(B, L, C)


def _from_scan_layout(t, B, L):
    C = t.shape[-1]
    return t.reshape(B, L // N_CHUNK, N_CHUNK, C).transpose(0, 2, 1, 3).reshape(B * L, C)


def _local_step(x, target, p, w_in, late_weights, mlp_grads_ready=None, rest_grads_ready=None, order=None, *,
                ssm_tile=256):
    B, L, D = x.shape
    T = B * L
    x2 = x.reshape(T, D)
    row = lambda v: v.reshape(1, -1)
    g1, g2, ga, gs, b_glu = row(p["norm1_g"]), row(p["norm2_g"]), row(p["attn_out_g"]), row(p["ssm_out_g"]), row(p["b_glu"])
    g1_first = g1 if order is None else g1 + order
    gq8 = jnp.tile(row(p["q_norm_g"]), (1, N_HEADS))
    gk8 = jnp.tile(row(p["k_norm_g"]), (1, N_HEADS))

    G, P, Hh = SSM_GROUPS, SSM_STATE, SSM_GROUP
    lam_re3, lam_im3 = p["ssm_lambda_re"].reshape(G, 1, P), p["ssm_lambda_im"].reshape(G, 1, P)
    log_dt3 = p["ssm_log_dt"].reshape(G, 1, 1)
    b_re_t, b_im_t = p["ssm_b_re"].transpose(0, 2, 1), p["ssm_b_im"].transpose(0, 2, 1)
    lbr, lbi, bbr, bbi = _ssm_prep(lam_re3, lam_im3, log_dt3, b_re_t, b_im_t)
    w_b = jnp.concatenate([_block_diag(bbr), _block_diag(bbi)], axis=2)
    c_mt = jnp.concatenate([_block_diag(p["ssm_c_re"]), -_block_diag(p["ssm_c_im"])], axis=2)
    w_bt, c_m = w_b.transpose(0, 2, 1), c_mt.transpose(0, 2, 1)
    lam_r = jnp.broadcast_to(lbr.reshape(SSM_COLS, 1, 512), (SSM_COLS, N_CHUNK, 512))
    lam_i = jnp.broadcast_to(lbi.reshape(SSM_COLS, 1, 512), (SSM_COLS, N_CHUNK, 512))
    d_skip = p["ssm_d"].reshape(1, SSM_WIDTH)

    qk_gains = jnp.concatenate([gq8 * (1.0 / math.sqrt(HEAD_DIM)), gk8, jnp.ones((1, 2 * SB_WIDTH), F32)], axis=1)
    head = (BF16, SB_WIDTH)
    proj, qn, kn, vb, xn = _matmul("proj", x2, w_in, prologue=(_rms, g1_first), extras=[qk_gains],
                                   out_dtypes=(F32, head, head, head), epilogue=_qk_norm, tm=512, tn=w_in.shape[1])
    sb, attn_kept = _attn_fwd(qn, kn, vb, B, L)
    u_p = _to_scan_layout(proj[:, 3 * SB_WIDTH:], B, L)
    y_p, gel_p, xs, xin = _ssm_fwd(u_p, w_b, lam_r, lam_i, c_m, d_skip, B, L, ssm_tile)
    y2, gel = y_p.reshape(T, SSM_WIDTH), gel_p.reshape(T, SSM_WIDTH)
    gel, sb = lax.optimization_barrier((gel, sb))
    w_glu, w_out, w_mlp_in, w_mlp_out = late_weights(gel)
    pre, ssm_n = _matmul("glu_gate", gel, w_glu, extras=[y2, b_glu, gs], out_dtypes=(F32, BF16),
                         epilogue=lambda acc, y, b, g: (acc, _glu_branch(y, acc, b, g)))
    mixed = _rowwise(
        "attn_out_norm", lambda s, n, g: jnp.concatenate([_rms(s, g).astype(BF16), n.astype(BF16)], axis=1),
        [sb, _from_scan_layout(ssm_n, B, L)], [ga], [(D, BF16)])
    def residual_and_norm(acc, res, g):
        h = acc + res
        return h, _rms(h, g)

    h1, hn = _matmul("out_proj", mixed, w_out, extras=[x2, g2], out_dtypes=(F32, BF16), tn=D,
                     epilogue=residual_and_norm)
    act, a_pre = _matmul("mlp_in", hn, w_mlp_in, out_dtypes=(BF16, BF16), tn=2048,
                         epilogue=lambda acc: (jnp.square(jnp.maximum(acc, 0.0)), acc))

    def loss_fn(acc, h, t):
        diff = acc + h - t
        part = jnp.sum(jnp.sum(diff * diff, axis=0, keepdims=True), axis=1, keepdims=True)
        d = diff * (1.0 / D)
        return d, d, part * (0.5 / D)

    d_out, d_out_b, loss = _matmul("mlp_out", act, w_mlp_out, extras=[h1, target.reshape(T, D)],
                                   out_dtypes=(F32, BF16), sums=[(1, 1)], epilogue=loss_fn, tm=512, tn=D)

    d_apre = _matmul("mlp_out_dx", d_out_b, w_mlp_out, tb=True, extras=[a_pre], out_dtypes=(BF16,), tn=2048,
                     epilogue=lambda acc, ap: (acc * (2.0 * jnp.maximum(ap.astype(F32), 0.0)),))
    both = lambda acc: (acc, acc)
    g_w_mlp_out, g_w_mlp_out_b = _matmul("mlp_out_dw", act, d_out_b, ta=True, out_dtypes=(F32, BF16), epilogue=both)
    g_w_mlp_in, g_w_mlp_in_b = _matmul("mlp_in_dw", hn, d_apre, ta=True, col_blocked=True, out_dtypes=(F32, BF16),
                                       epilogue=both, tn=w_mlp_in.shape[1] // N_DEV)
    if mlp_grads_ready is not None:
        g2 = g2 + mlp_grads_ready(g_w_mlp_out, g_w_mlp_out_b, g_w_mlp_in, g_w_mlp_in_b)

    def norm_bwd_res(dy, h, res, g):
        _, vjp = jax.vjp(_rms, h, g)
        dh, dg = vjp(dy)
        return res + dh, dg

    def norm_bwd_res2(dy, h, res, g):
        d, dg = norm_bwd_res(dy, h, res, g)
        return d, d, dg

    d_h1, d_h1_b, g_norm2 = _matmul("mlp_in_dx", d_apre, w_mlp_in, tb=True, extras=[h1, d_out, g2],
                                    out_dtypes=(F32, BF16), sums=[(1, D)], epilogue=norm_bwd_res2, tm=512, tn=D)

    d_mixed = _matmul("out_proj_dx", d_h1_b, w_out, tb=True, tn=1024)
    g_w_out, g_w_out_b = _matmul("out_proj_dw", mixed, d_h1_b, ta=True, out_dtypes=(F32, BF16), epilogue=both)

    def norm_bwd(h, dy, g):
        _, vjp = jax.vjp(_rms, h, g)
        return vjp(dy)

    d_sb, g_attn_out = _rowwise("attn_out_norm_bwd", norm_bwd, [sb, (d_mixed, 0, SB_WIDTH)], [ga],
                                [(SB_WIDTH, F32)], sums=[(1, SB_WIDTH)])
    d_ssm_n = _to_scan_layout(d_mixed[:, SB_WIDTH:], B, L).reshape(T, SSM_WIDTH)

    def glu_bwd(y, pre_, dy, bg, g):
        _, vjp = jax.vjp(_glu_branch, y, pre_, bg, g)
        d_y, d_pre, d_bg, d_g = vjp(dy)
        return d_y, d_pre, d_bg, d_g

    d_y_direct, d_pre, g_b_glu, g_ssm_out = _rowwise(
        "glu_out_bwd", glu_bwd, [y2, pre, d_ssm_n], [b_glu, gs], [(SSM_WIDTH, F32), (SSM_WIDTH, BF16)],
        sums=[(1, SSM_WIDTH), (1, SSM_WIDTH)])
    g_w_glu, g_w_glu_b = _matmul("glu_gate_dw", gel, d_pre, ta=True, out_dtypes=(F32, BF16), epilogue=both)

    def gelu_bwd(dg, y, dy0):
        _, vjp = jax.vjp(jax.nn.gelu, y)
        return (dy0 + vjp(dg)[0],)

    d_y = _matmul("glu_gate_dx", d_pre, w_glu, tb=True, extras=[y2, d_y_direct], epilogue=gelu_bwd)

    du_p, d_wb, d_cm, d_lr, d_li, g_d = _ssm_bwd(
        d_y.reshape(B, L, SSM_WIDTH), u_p, xs, xin, w_bt, lam_r, lam_i, c_mt, d_skip, B, L, ssm_tile)
    d_bb = _block_diag_take(d_wb.reshape(SSM_COLS, LANES, 2, 512))
    d_c = _block_diag_take(d_cm.transpose(0, 2, 1).reshape(SSM_COLS, LANES, 2, 512))
    g_lam_re, g_lam_im, g_log_dt, g_b_re_t, g_b_im_t = _ssm_prep_bwd(
        lam_re3, lam_im3, log_dt3, b_re_t, b_im_t,
        d_lr.reshape(G, 1, P), d_li.reshape(G, 1, P), d_bb[0], d_bb[1])
    d_q, d_k, d_v, g_q, g_k = _attn_bwd(qn, kn, vb, attn_kept, d_sb, proj, gq8[:, :LANES], gk8[:, :LANES], B, L)

    d_proj = jnp.concatenate([d_q, d_k, d_v, _from_scan_layout(du_p, B, L)], axis=1)
    g_w_in, g_w_in_b = _matmul("proj_dw", xn, d_proj, ta=True, col_blocked=True, out_dtypes=(F32, BF16),
                               epilogue=both, tn=w_in.shape[1] // N_DEV)
    if rest_grads_ready is not None:
        g1 = g1 + rest_grads_ready([g_w_in, g_w_glu, g_w_out], [g_w_in_b, g_w_glu_b, g_w_out_b])
    grad_x, g_norm1 = _matmul("proj_dx", d_proj, w_in, tb=True, extras=[x2, d_h1, g1], sums=[(1, D)],
                              epilogue=norm_bwd_res, tm=512, tn=D)

    small = {
        "norm1_g": g_norm1.reshape(-1),
        "q_norm_g": g_q.reshape(-1),
        "k_norm_g": g_k.reshape(-1),
        "ssm_lambda_re": g_lam_re.reshape(G, P),
        "ssm_lambda_im": g_lam_im.reshape(G, P),
        "ssm_log_dt": g_log_dt.reshape(G),
        "ssm_b_re": g_b_re_t.transpose(0, 2, 1),
        "ssm_b_im": g_b_im_t.transpose(0, 2, 1),
        "ssm_c_re": d_c[0],
        "ssm_c_im": -d_c[1],
        "ssm_d": g_d.reshape(G, Hh),
        "b_glu": g_b_glu.reshape(-1),
        "attn_out_g": g_attn_out.reshape(-1),
        "ssm_out_g": g_ssm_out.reshape(-1),
        "norm2_g": g_norm2.reshape(-1),
    }
    big = {"w_in": g_w_in, "w_glu": g_w_glu, "w_out": g_w_out, "w_mlp_in": g_w_mlp_in, "w_mlp_out": g_w_mlp_out}
    return loss[0, 0], grad_x.reshape(B, L, D), small, big


_ANY = pl.BlockSpec(memory_space=pl.ANY)
_MESH = pl.DeviceIdType.MESH


def _all_gather(name, shards):
    n = len(shards)

    def body(*refs):
        in_refs, out_refs = refs[:n], refs[n:2 * n]
        send_sems, recv_sems, local_sems = refs[2 * n:]
        x, y, c = lax.axis_index("x"), lax.axis_index("y"), lax.axis_index("c")
        me, sibling = (x, y, c), (x, y, 1 - c)
        chips = [(1 - x, y), (x, 1 - y), (1 - x, 1 - y)]

        def copy(a, k, block, to, src=None):
            px, py, pc = block
            rows = out_refs[a].at[4 * px + 2 * py + pc]
            return pltpu.make_async_remote_copy(
                src_ref=rows if src is None else src, dst_ref=rows, send_sem=send_sems.at[a, k],
                recv_sem=recv_sems.at[a, k], device_id=to, device_id_type=_MESH)

        mine = [pltpu.make_async_copy(in_refs[a], out_refs[a].at[4 * x + 2 * y + c], local_sems.at[a]) for a in range(n)]
        first, passed = [], []
        for a in range(n):
            mine[a].start()
            first.append(copy(a, 0, me, sibling, src=in_refs[a]))
            first += [copy(a, 1 + j, me, (*chip, c), src=in_refs[a]) for j, chip in enumerate(chips)]
        for cp in first:
            cp.start()
        for j, chip in enumerate(chips):
            for a in range(n):
                copy(a, 1 + j, (*chip, c), me).wait_recv()
                fwd = copy(a, 4 + j, (*chip, c), sibling)
                fwd.start()
                passed.append(fwd)
        for a in range(n):
            copy(a, 0, sibling, me).wait_recv()
            for j, chip in enumerate(chips):
                copy(a, 4 + j, (*chip, 1 - c), me).wait_recv()
        for cp in first + passed:
            cp.wait_send()
        for cp in mine:
            cp.wait()

    return pl.pallas_call(
        body, name=name,
        in_specs=[_ANY] * n, out_specs=[_ANY] * n,
        out_shape=[jax.ShapeDtypeStruct((N_DEV, *s.shape), s.dtype) for s in shards],
        scratch_shapes=[pltpu.SemaphoreType.DMA((n, 7)), pltpu.SemaphoreType.DMA((n, 7)), pltpu.SemaphoreType.DMA((n,))],
    )(*shards)


_HBM = pl.BlockSpec(memory_space=pltpu.HBM)
_SEM = pl.BlockSpec(memory_space=pltpu.SEMAPHORE)
_EFFECT = pltpu.SideEffectType.DATAFLOW_SIDE_EFFECTING
_FLIPS = [(dx, dy, dc) for dx in (0, 1) for dy in (0, 1) for dc in (0, 1) if (dx, dy, dc) != (0, 0, 0)]


def _exchange_start(name, srcs, lands, per_peer):
    n = len(srcs)

    def body(*refs):
        src_refs, land_refs = refs[:n], refs[n:2 * n]
        send_sems, recv_sems = refs[2 * n:3 * n], refs[3 * n:4 * n]
        token = refs[-1]
        x, y, c = lax.axis_index("x"), lax.axis_index("y"), lax.axis_index("c")
        me = 4 * x + 2 * y + c
        for dx, dy, dc in _FLIPS:
            px, py, pc = (1 - x if dx else x), (1 - y if dy else y), (1 - c if dc else c)
            for a in range(n):
                pltpu.make_async_remote_copy(
                    src_ref=src_refs[a].at[4 * px + 2 * py + pc] if per_peer else src_refs[a],
                    dst_ref=land_refs[a].at[me], send_sem=send_sems[a], recv_sem=recv_sems[a],
                    device_id=(px, py, pc), device_id_type=_MESH).start()
        token[...] = jnp.zeros_like(token)

    hbm = lambda t: pltpu.with_memory_space_constraint(t, pltpu.HBM)
    res = pl.pallas_call(
        body, name=name,
        out_shape=(*[pltpu.SemaphoreType.DMA(())] * (2 * n), *[pltpu.HBM(t.shape, t.dtype) for t in (*srcs, *lands)],
                   jax.ShapeDtypeStruct((8, LANES), F32)),
        in_specs=[_HBM] * (2 * n),
        out_specs=(*[_SEM] * (2 * n), *[_HBM] * (2 * n), pl.BlockSpec(memory_space=pltpu.VMEM)),
        input_output_aliases={i: 2 * n + i for i in range(2 * n)},
        compiler_params=pltpu.CompilerParams(has_side_effects=_EFFECT),
    )(*[hbm(t) for t in (*srcs, *lands)])
    return res[:-1], res[-1]


def _exchange_wait(name, handle, after):
    n = len(handle) // 4
    sems, thru = handle[:2 * n], handle[2 * n:]

    def body(*refs):
        land_refs = refs[n:2 * n]
        send_sems, recv_sems = refs[2 * n:3 * n], refs[3 * n:4 * n]
        me = (lax.axis_index("x"), lax.axis_index("y"), lax.axis_index("c"))
        for a in range(n):
            seven = land_refs[a].at[pl.ds(0, len(_FLIPS))]
            all_copies = pltpu.make_async_remote_copy(
                src_ref=seven, dst_ref=seven, send_sem=send_sems[a], recv_sem=recv_sems[a], device_id=me,
                device_id_type=_MESH)
            all_copies.wait_send()
            all_copies.wait_recv()

    res = pl.pallas_call(
        body, name=name, out_shape=tuple(pltpu.HBM(t.shape, t.dtype) for t in thru),
        in_specs=[*[_HBM] * (2 * n), *[_SEM] * (2 * n), _ANY], out_specs=tuple([_HBM] * (2 * n)),
        input_output_aliases={i: i for i in range(2 * n)},
        compiler_params=pltpu.CompilerParams(has_side_effects=_EFFECT),
    )(*thru, *sems, after)
    return res[n:]


def _adamw_gathered(name, own, parts, me, w, m, v):
    r, c = w.shape
    tr = min(r, 256)

    def body(me_ref, own_ref, p_ref, w_ref, m_ref, v_ref, g_out, d_out, m_out, v_out):
        g = own_ref[...]
        for j in range(N_DEV):
            g = g + p_ref[j].astype(F32)
        delta, m_new, v_new = _adamw(w_ref[...], g, m_ref[...], v_ref[...])
        g_out[...] = g
        d_out[...] = delta
        m_out[...] = m_new
        v_out[...] = v_new

    spec = pl.BlockSpec((tr, c), lambda i, me_ref: (i, 0))
    return pl.pallas_call(
        body, name=name,
        grid_spec=pltpu.PrefetchScalarGridSpec(
            num_scalar_prefetch=1, grid=(r // tr,),
            in_specs=[pl.BlockSpec((None, tr, c), lambda i, me_ref: (me_ref[0], i, 0)),
                      pl.BlockSpec((N_DEV, tr, c), lambda i, me_ref: (0, i, 0)), spec, spec, spec],
            out_specs=[spec] * 4),
        out_shape=[jax.ShapeDtypeStruct((r, c), F32)] * 4,
        compiler_params=_params(("parallel",)),
    )(me, own, parts, w, m, v)


def _adamw(w, g, m, v):
    m = ADAM_B1 * m + (1.0 - ADAM_B1) * g
    v = ADAM_B2 * v + (1.0 - ADAM_B2) * jnp.square(g)
    m_hat = m / (1.0 - ADAM_B1 ** ADAM_STEP)
    v_hat = v / (1.0 - ADAM_B2 ** ADAM_STEP)
    delta = -ADAM_LR * (m_hat / (jnp.sqrt(v_hat) + ADAM_EPS) + ADAM_WD * w)
    return delta, m, v


def _adamw_small(name, parts, w, m, v):
    _, r, c = parts.shape
    tr = 8

    def body(p_ref, w_ref, m_ref, v_ref, g_out, d_out, m_out, v_out):
        g = p_ref[0]
        for j in range(1, N_DEV):
            g = g + p_ref[j]
        delta, m_new, v_new = _adamw(w_ref[...], g, m_ref[...], v_ref[...])
        g_out[...] = g
        d_out[...] = delta
        m_out[...] = m_new
        v_out[...] = v_new

    spec = pl.BlockSpec((tr, c), lambda i: (i, 0))
    return pl.pallas_call(
        body, name=name, grid=(r // tr,),
        in_specs=[pl.BlockSpec((N_DEV, tr, c), lambda i: (0, i, 0)), spec, spec, spec],
        out_specs=[spec] * 4, out_shape=[jax.ShapeDtypeStruct((r, c), F32)] * 4,
        compiler_params=_params(("parallel",)),
    )(parts, w, m, v)


_WEIGHTS = ["norm1_g", "w_in", "q_norm_g", "k_norm_g", "ssm_lambda_re", "ssm_lambda_im", "ssm_log_dt", "ssm_b_re",
            "ssm_b_im", "ssm_c_re", "ssm_c_im", "ssm_d", "w_glu", "b_glu", "attn_out_g", "ssm_out_g", "w_out",
            "norm2_g", "w_mlp_in", "w_mlp_out"]
_BIG = ["w_in", "w_glu", "w_out", "w_mlp_in", "w_mlp_out"]
_SMALL = [n for n in _WEIGHTS if n not in _BIG]
_PACK_COLS = 1024


def _pack(tree, last=None):
    flat = [tree[n].reshape(-1).astype(F32) for n in _SMALL]
    size = sum(f.shape[0] for f in flat)
    rows = -(-(size + 1) // (_PACK_COLS * 8)) * 8
    pad = jnp.zeros((rows * _PACK_COLS - size - 1,), F32)
    tail = jnp.zeros((1,), F32) if last is None else last.reshape(1).astype(F32)
    return jnp.concatenate(flat + [pad, tail]).reshape(rows, _PACK_COLS)


def _unpack(buf, like):
    flat, out, off = buf.reshape(-1), {}, 0
    for n in _SMALL:
        size = like[n].size
        out[n] = flat[off:off + size].reshape(like[n].shape)
        off += size
    return out


def kernel(x, norm1_g, w_in, q_norm_g, k_norm_g, ssm_lambda_re, ssm_lambda_im, ssm_log_dt, ssm_b_re, ssm_b_im, ssm_c_re, ssm_c_im, ssm_d, w_glu, b_glu, attn_out_g, ssm_out_g, w_out, norm2_g, w_mlp_in, w_mlp_out, loss_target, m_norm1_g, m_w_in, m_q_norm_g, m_k_norm_g, m_ssm_lambda_re, m_ssm_lambda_im, m_ssm_log_dt, m_ssm_b_re, m_ssm_b_im, m_ssm_c_re, m_ssm_c_im, m_ssm_d, m_w_glu, m_b_glu, m_attn_out_g, m_ssm_out_g, m_w_out, m_norm2_g, m_w_mlp_in, m_w_mlp_out, v_norm1_g, v_w_in, v_q_norm_g, v_k_norm_g, v_ssm_lambda_re, v_ssm_lambda_im, v_ssm_log_dt, v_ssm_b_re, v_ssm_b_im, v_ssm_c_re, v_ssm_c_im, v_ssm_d, v_w_glu, v_b_glu, v_attn_out_g, v_ssm_out_g, v_w_out, v_norm2_g, v_w_mlp_in, v_w_mlp_out):
    w = dict(zip(_WEIGHTS, (norm1_g, w_in, q_norm_g, k_norm_g, ssm_lambda_re, ssm_lambda_im, ssm_log_dt, ssm_b_re, ssm_b_im, ssm_c_re, ssm_c_im, ssm_d, w_glu, b_glu, attn_out_g, ssm_out_g, w_out, norm2_g, w_mlp_in, w_mlp_out)))
    m = dict(zip(_WEIGHTS, (m_norm1_g, m_w_in, m_q_norm_g, m_k_norm_g, m_ssm_lambda_re, m_ssm_lambda_im, m_ssm_log_dt, m_ssm_b_re, m_ssm_b_im, m_ssm_c_re, m_ssm_c_im, m_ssm_d, m_w_glu, m_b_glu, m_attn_out_g, m_ssm_out_g, m_w_out, m_norm2_g, m_w_mlp_in, m_w_mlp_out)))
    v = dict(zip(_WEIGHTS, (v_norm1_g, v_w_in, v_q_norm_g, v_k_norm_g, v_ssm_lambda_re, v_ssm_lambda_im, v_ssm_log_dt, v_ssm_b_re, v_ssm_b_im, v_ssm_c_re, v_ssm_c_im, v_ssm_d, v_w_glu, v_b_glu, v_attn_out_g, v_ssm_out_g, v_w_out, v_norm2_g, v_w_mlp_in, v_w_mlp_out)))
    core = lax.axis_index("c").astype(jnp.int32).reshape(1)
    chip = (2 * lax.axis_index("x") + lax.axis_index("y")).astype(jnp.int32).reshape(1)

    me = (2 * chip + core).astype(jnp.int32)

    def landing(own=None, like=None):
        own = jnp.zeros_like(like) if own is None else own
        return lax.dynamic_update_slice(lax.empty((N_DEV, *like.shape), like.dtype), own[None], (me[0], 0, 0))

    (w_in_blocks,) = _all_gather("w_in_all_gather", [w_in.astype(BF16)])
    w_in_full = w_in_blocks.transpose(1, 0, 2).reshape(w_in.shape[0], -1)
    late = [n for n in _BIG if n != "w_in"]
    shards = [w[n].astype(BF16) for n in late]
    w_in_blocks, shards = lax.optimization_barrier((w_in_blocks, shards))
    weights_handle, weights_token = _exchange_start(
        "weights_send", shards, [landing(s, s) for s in shards], per_peer=False)

    def late_weights(after):
        got = dict(zip(late, _exchange_wait("weights_arrive", weights_handle, after)))
        return (got["w_glu"].reshape(-1, w_glu.shape[1]), got["w_out"].reshape(-1, w_out.shape[1]),
                got["w_mlp_in"].transpose(1, 0, 2).reshape(w_mlp_in.shape[0], -1),
                got["w_mlp_out"].reshape(-1, w_mlp_out.shape[1]))

    mlp = ["w_mlp_out", "w_mlp_in"]
    sent = {}

    def send_grads(name, names, own, own_b):
        blocks = lambda g, n: g.reshape(N_DEV, *w[n].shape)
        sent[name + "_own"] = [blocks(g, n) for g, n in zip(own, names)]
        srcs = [blocks(g, n) for g, n in zip(own_b, names)]
        sent[name], token = _exchange_start(name, srcs, [landing(like=s[0]) for s in srcs], per_peer=True)
        return token[0, 0]

    def mlp_grads_ready(g_out, g_out_b, g_in, g_in_b):
        return send_grads("mlp_grads_send", mlp, [g_out, g_in], [g_out_b, g_in_b])

    rest = ["w_in", "w_glu", "w_out"]

    def rest_grads_ready(own, own_b):
        return send_grads("rest_grads_send", rest, own, own_b)

    loss_local, grad_x, g_small, g_big = _local_step(
        x, loss_target, {n: w[n] for n in _SMALL}, w_in_full, late_weights, mlp_grads_ready, rest_grads_ready,
        weights_token[0, 0])

    grads, delta, new_m, new_v = {}, {}, {}, {}
    small = _pack(g_small, last=loss_local)
    small_handle, small_token = _exchange_start("small_grads_send", [small], [landing(small, small)], per_peer=False)

    for send, arrive, names in (("mlp_grads_send", "mlp_grads_arrive", mlp),
                                ("rest_grads_send", "rest_grads_arrive", rest)):
        for n, own, part in zip(names, sent[send + "_own"], _exchange_wait(arrive, sent[send], small_token)):
            grads[n], delta[n], new_m[n], new_v[n] = _adamw_gathered("adamw_" + n, own, part, me, w[n], m[n], v[n])

    shards_done = lax.optimization_barrier(tuple(new_v[n] for n in _BIG))
    (small_parts,) = _exchange_wait("small_grads_arrive", small_handle, shards_done[-1])
    packed = _adamw_small("adamw_small", small_parts, _pack(w), _pack(m), _pack(v))
    for tree, buf in zip((grads, delta, new_m, new_v), packed):
        tree.update(_unpack(buf, w))
    loss = packed[0][-1, -1]

    return (loss, grad_x, *[grads[n] for n in _WEIGHTS], *[delta[n] for n in _WEIGHTS],
            *[new_m[n] for n in _WEIGHTS], *[new_v[n] for n in _WEIGHTS])
```

```python
import functools
import math

import jax
import jax.numpy as jnp
from jax import lax
from jax.experimental import pallas as pl
from jax.experimental.pallas import tpu as pltpu

F32 = jnp.float32
BF16 = jnp.bfloat16

EPS = 1e-6
HEAD_DIM = 64
N_HEADS = 8
SB_WIDTH = 512
SSM_WIDTH = 512
SSM_GROUP = 16
SSM_GROUPS = 32
SSM_STATE = 64
QBLOCK = 128
KBLOCK = 256
N_CHUNK = 8
SSM_COLS = 4
LANES = 128
N_DEV = 8

ADAM_LR = 0.001
ADAM_B1 = 0.9
ADAM_B2 = 0.999
ADAM_EPS = 1e-08
ADAM_WD = 0.01
ADAM_STEP = 10

VMEM_LIMIT = 56 * 1024 * 1024

_NT = (((1,), (1,)), ((), ()))
_NN = (((1,), (0,)), ((), ()))
_TN = (((0,), (0,)), ((), ()))


def _dot(a, b, dims=_NN):
    return lax.dot_general(a, b, dims, preferred_element_type=F32)


def _params(sem):
    return pltpu.CompilerParams(dimension_semantics=sem, vmem_limit_bytes=VMEM_LIMIT)


def _matmul(name, a, b, *, ta=False, tb=False, extras=(), epilogue=None, out_dtypes=(F32,), sums=(),
            prologue=None, col_blocked=False, tm=1024, tn=512, tk=4096):
    M, K = (a.shape[1], a.shape[0]) if ta else a.shape
    N = b.shape[0] if tb else b.shape[1]
    tm, tn, tk = min(tm, M), min(tn, N), min(tk, K)
    assert M % tm == 0 and N % tn == 0 and K % tk == 0, (name, M, N, K)
    assert not (sums or prologue) or (tn == N and tk == K), name
    assert not prologue or not (ta or col_blocked), name
    nk = K // tk
    n_ex, n_out, n_sum = len(extras), len(out_dtypes), len(sums)
    n_pro = 1 if prologue else 0
    dims = (((0 if ta else 1,), (1 if tb else 0,)), ((), ()))

    def body(*refs):
        a_ref, b_ref = refs[0], refs[1]
        ex_refs = refs[2 + n_pro:2 + n_pro + n_ex]
        o_refs = refs[2 + n_pro + n_ex:2 + n_pro + n_ex + n_out]
        s_refs = refs[2 + n_pro + n_ex + n_out:2 + n_pro + n_ex + n_out + n_sum]
        k = pl.program_id(2)
        if prologue:
            left = prologue[0](a_ref[...], refs[2][...]).astype(BF16)
            refs[2 + n_pro + n_ex + n_out + n_sum][...] = left
        else:
            left = a_ref[...].astype(BF16)
        part = _dot(left, b_ref[...].astype(BF16), dims)

        def finish(acc):
            outs = (acc,) if epilogue is None else epilogue(acc, *[e[...] for e in ex_refs])
            for o_ref, o in zip(o_refs, outs[:n_out]):
                o_ref[...] = o.astype(o_ref.dtype)
            if n_sum:
                @pl.when(pl.program_id(0) == 0)
                def _():
                    for s_ref in s_refs:
                        s_ref[...] = jnp.zeros_like(s_ref)

                for s_ref, v in zip(s_refs, outs[n_out:]):
                    s_ref[...] += v

        if nk == 1:
            finish(part)
        else:
            acc_ref = refs[-1]

            @pl.when(k == 0)
            def _():
                acc_ref[...] = part

            @pl.when(jnp.logical_and(k > 0, k < nk - 1))
            def _():
                acc_ref[...] += part

            @pl.when(k == nk - 1)
            def _():
                finish(acc_ref[...] + part)

    a_spec = pl.BlockSpec((tk, tm), lambda i, j, k: (k, i)) if ta else pl.BlockSpec((tm, tk), lambda i, j, k: (i, k))
    b_spec = pl.BlockSpec((tn, tk), lambda i, j, k: (j, k)) if tb else pl.BlockSpec((tk, tn), lambda i, j, k: (k, j))
    ex_specs = [pl.BlockSpec((1, tn), lambda i, j, k: (0, j)) if e.shape[0] == 1 else
                pl.BlockSpec((tm, tn), lambda i, j, k: (i, j)) for e in extras]
    if col_blocked:
        out_specs = [pl.BlockSpec((None, tm, tn), lambda i, j, k: (j, i, 0)) for _ in out_dtypes]
        out_shape = [jax.ShapeDtypeStruct((N // tn, M, tn), dt) for dt in out_dtypes]
    else:
        wide = [(dt, N) if not isinstance(dt, tuple) else dt for dt in out_dtypes]
        assert all(w == N for _, w in wide) or tn == N, name
        out_specs = [pl.BlockSpec((tm, tn if w == N else w), lambda i, j, k: (i, j)) for _, w in wide]
        out_shape = [jax.ShapeDtypeStruct((M, w), dt) for dt, w in wide]
    out_specs += [pl.BlockSpec(s, lambda i, j, k: (0, 0)) for s in sums]
    out_shape += [jax.ShapeDtypeStruct(s, F32) for s in sums]
    pro_specs, pro_args = [], []
    if prologue:
        pro_specs, pro_args = [pl.BlockSpec((1, tk), lambda i, j, k: (0, 0))], [prologue[1]]
        out_specs.append(pl.BlockSpec((tm, tk), lambda i, j, k: (i, 0)))
        out_shape.append(jax.ShapeDtypeStruct((M, K), BF16))
    outs = pl.pallas_call(
        body, name=name, grid=(M // tm, N // tn, nk),
        in_specs=[a_spec, b_spec, *pro_specs, *ex_specs], out_specs=out_specs, out_shape=out_shape,
        scratch_shapes=[pltpu.VMEM((tm, tn), F32)] if nk > 1 else [],
        compiler_params=_params(("arbitrary",) * 3 if sums else ("parallel", "parallel", "arbitrary")),
    )(a, b, *pro_args, *extras)
    return outs[0] if len(outs) == 1 else outs


def _rowwise(name, fn, rows, small, outs, sums=(), tile=1024):
    specs, args = [], []
    T = None
    for r in rows:
        arr, cb, w = r if isinstance(r, tuple) else (r, 0, r.shape[1])
        T = arr.shape[0]
        specs.append((w, cb))
        args.append(arr)
    tile = min(tile, T)
    assert T % tile == 0
    n_r, n_s, n_o, n_a = len(rows), len(small), len(outs), len(sums)

    def body(*refs):
        r_refs = refs[:n_r]
        s_refs = refs[n_r:n_r + n_s]
        o_refs = refs[n_r + n_s:n_r + n_s + n_o]
        a_refs = refs[n_r + n_s + n_o:]
        res = fn(*[r[...] for r in r_refs], *[s[...] for s in s_refs])
        res = res if isinstance(res, (tuple, list)) else (res,)
        for o_ref, o in zip(o_refs, res[:n_o]):
            o_ref[...] = o.astype(o_ref.dtype)

        @pl.when(pl.program_id(0) == 0)
        def _():
            for a_ref in a_refs:
                a_ref[...] = jnp.zeros_like(a_ref)

        for a_ref, v in zip(a_refs, res[n_o:]):
            a_ref[...] += v.astype(F32)

    in_specs = [pl.BlockSpec((tile, w), functools.partial(lambda i, cb: (i, cb), cb=cb)) for w, cb in specs]
    in_specs += [pl.BlockSpec(s.shape, functools.partial(lambda i, nd: (0,) * nd, nd=s.ndim)) for s in small]
    out_specs = [pl.BlockSpec((tile, w), lambda i: (i, 0)) for w, _ in outs]
    out_specs += [pl.BlockSpec(s, functools.partial(lambda i, nd: (0,) * nd, nd=len(s))) for s in sums]
    out_shape = [jax.ShapeDtypeStruct((T, w), dt) for w, dt in outs]
    out_shape += [jax.ShapeDtypeStruct(s, F32) for s in sums]
    res = pl.pallas_call(
        body, name=name, grid=(T // tile,), in_specs=in_specs, out_specs=out_specs, out_shape=out_shape,
        compiler_params=_params(("arbitrary",)),
    )(*args, *small)
    return res[0] if len(res) == 1 else res


def _rms(x, g):
    return x * lax.rsqrt(jnp.mean(x * x, axis=-1, keepdims=True) + EPS) * g


def _glu_branch(y, pre, b_glu, g_out):
    g = jax.nn.gelu(y)
    return _rms(g * jax.nn.sigmoid(pre + b_glu), g_out)


def _split_dot(x, tri_bf):
    hi = x.astype(BF16)
    lo = (x - hi.astype(F32)).astype(BF16)
    return _dot(hi, tri_bf) + _dot(lo, tri_bf)


def _softplus(z):
    return jnp.maximum(z, 0.0) + jnp.log(1.0 + jnp.exp(-jnp.abs(z)))


def _head(h):
    return slice(h * HEAD_DIM, (h + 1) * HEAD_DIM)


def _head_mean(x, seg):
    return _split_dot(x, seg) * (1.0 / HEAD_DIM)


def _qk_norm(proj, gains):
    r = lax.div(lax.broadcasted_iota(jnp.int32, (SB_WIDTH, SB_WIDTH), 0), HEAD_DIM)
    c = lax.div(lax.broadcasted_iota(jnp.int32, (SB_WIDTH, SB_WIDTH), 1), HEAD_DIM)
    seg = (r == c).astype(BF16)
    q, k, v = (proj[:, i * SB_WIDTH:(i + 1) * SB_WIDTH] for i in range(3))
    qn = q * lax.rsqrt(_head_mean(q * q, seg) + EPS) * gains[:, 0:SB_WIDTH]
    kn = k * lax.rsqrt(_head_mean(k * k, seg) + EPS) * gains[:, SB_WIDTH:2 * SB_WIDTH]
    return proj, qn, kn, v


def _qk_norm_bwd(x, g, dy, seg):
    r = lax.rsqrt(_head_mean(x * x, seg) + EPS)
    gdy = g * dy
    dx = r * gdy - x * (r * r * r) * _head_mean(gdy * x, seg)
    return dx, jnp.sum(dy * x * r, axis=0, keepdims=True)


def _split_heads(refs, scratch, L):
    def chunk(i, _):
        r = pl.ds(pl.multiple_of(i * QBLOCK, QBLOCK), QBLOCK)
        for ref, s in zip(refs, scratch):
            for h in range(2):
                s[h, r, :] = ref[r, _head(h)]
        return 0

    lax.fori_loop(0, L // QBLOCK, chunk, 0)


Q_HALVES = KBLOCK // QBLOCK
_CHAINS = [(h, r) for h in range(2) for r in range(Q_HALVES)]


def _valid(i, kb):
    row = lax.broadcasted_iota(jnp.int32, (QBLOCK, KBLOCK), 0)
    col = lax.broadcasted_iota(jnp.int32, (QBLOCK, KBLOCK), 1)
    return col + (kb * KBLOCK - i * QBLOCK) < row


def _attn_fwd(qn, kn, vb, B, L):
    n_pairs = L // KBLOCK
    n_hp = N_HEADS // 2
    nc = len(_CHAINS)

    def body(q_ref, k_ref, v_ref, o_ref, a_ref, q_s, k_s, v_s, after_s, z_s, stage_s, sems):
        _split_heads((q_ref, k_ref, v_ref), (q_s, k_s, v_s), L)
        r2 = lax.broadcasted_iota(jnp.int32, (KBLOCK, KBLOCK), 0)
        c2 = lax.broadcasted_iota(jnp.int32, (KBLOCK, KBLOCK), 1)
        after_s[...] = (r2 > c2).astype(after_s.dtype)
        g = pl.program_id(0) * n_hp + pl.program_id(1)

        def q_pair(p, _):
            rows = [pl.ds(pl.multiple_of((p * Q_HALVES + r) * QBLOCK, QBLOCK), QBLOCK) for r in range(Q_HALVES)]
            q_c = [q_s[h, rows[r], :] for h, r in _CHAINS]
            cs = range(nc)

            def scores(kb):
                rk = pl.ds(pl.multiple_of(kb * KBLOCK, KBLOCK), KBLOCK)
                return [_dot(q_c[c], k_s[_CHAINS[c][0], rk, :], _NT) for c in cs]

            def saved(kb):
                return pltpu.make_async_copy(stage_s.at[kb & 1], a_ref.at[g, p, kb], sems.at[kb & 1])

            def k_block(kb, carry, diagonal):
                rk = pl.ds(pl.multiple_of(kb * KBLOCK, KBLOCK), KBLOCK)
                if diagonal:
                    valid = [_valid(p * Q_HALVES + r, kb) for r in range(Q_HALVES)]
                    keep = lambda c, t: jnp.where(valid[_CHAINS[c][1]], t, 0.0)
                    z = scores(kb)
                else:
                    keep = lambda c, t: t
                    z = [z_s[(kb + 1) & 1, c] for c in cs]
                ahead = scores(jnp.maximum(kb - 1, 0))
                for c in cs:
                    z_s[kb & 1, c] = ahead[c]
                sp = [_softplus(z[c]) for c in cs]
                lsig = [z[c] - sp[c] for c in cs]
                lom = [keep(c, -sp[c]) for c in cs]
                tail = [_split_dot(lom[c], after_s[...]) + carry[c][0] for c in cs]
                a = [keep(c, jnp.exp(lsig[c] + tail[c])).astype(v_s.dtype) for c in cs]
                acc = [carry[c][1] + _dot(a[c], v_s[_CHAINS[c][0], rk, :]) for c in cs]
                for c in cs:
                    stage_s[kb & 1, c] = a[c]
                saved(kb).start()
                return tuple((carry[c][0] + jnp.sum(lom[c], axis=1, keepdims=True), acc[c]) for c in cs)

            def next_block(n, carry):
                kb = p - n

                @pl.when(n >= 2)
                def _():
                    saved(kb + 2).wait()

                return k_block(kb, carry, False)

            init = (jnp.zeros((QBLOCK, 1), F32), jnp.zeros((QBLOCK, HEAD_DIM), F32))
            first = k_block(p, (init,) * nc, True)
            res = lax.fori_loop(1, p + 1, next_block, first)
            saved(0).wait()

            @pl.when(p >= 1)
            def _():
                saved(1).wait()

            for r in range(Q_HALVES):
                o_ref[rows[r], :] = jnp.concatenate([res[c][1] for c in cs if _CHAINS[c][1] == r], axis=1)
            return 0

        lax.fori_loop(0, n_pairs, q_pair, 0)

    spec = pl.BlockSpec((L, LANES), lambda b, p: (b, p))
    return pl.pallas_call(
        body, name="attn_fwd", grid=(B, n_hp),
        in_specs=[spec] * 3, out_specs=[spec, _ANY],
        out_shape=[jax.ShapeDtypeStruct((B * L, SB_WIDTH), F32),
                   jax.ShapeDtypeStruct((B * n_hp, n_pairs, n_pairs, nc, QBLOCK, KBLOCK), BF16)],
        scratch_shapes=[pltpu.VMEM((2, L, HEAD_DIM), BF16)] * 3 + [pltpu.VMEM((KBLOCK, KBLOCK), BF16)]
        + [pltpu.VMEM((2, nc, QBLOCK, KBLOCK), F32), pltpu.VMEM((2, nc, QBLOCK, KBLOCK), BF16),
           pltpu.SemaphoreType.DMA((2,))],
        compiler_params=_params(("parallel", "parallel")),
    )(qn, kn, vb)


def _attn_bwd(qn, kn, vb, kept_a, d_sb, proj, gq2, gk2, B, L):
    n_pairs = L // KBLOCK
    n_hp = N_HEADS // 2
    nc = len(_CHAINS)
    slots = 3
    scale = 1.0 / math.sqrt(HEAD_DIM)

    def body(q_ref, k_ref, v_ref, do_ref, qraw_ref, kraw_ref, gq_ref, gk_ref, a_ref,
             dq_ref, dk_ref, dv_ref, dgq_ref, dgk_ref,
             q_s, k_s, v_s, qt_s, dkt_s, dvt_s, before_s, stage_s, sems):
        _split_heads((q_ref, k_ref, v_ref), (q_s, k_s, v_s), L)
        g = pl.program_id(0) * n_hp + pl.program_id(1)
        lane_head = [lax.div(lax.broadcasted_iota(jnp.int32, (LANES, LANES), d), HEAD_DIM) for d in (0, 1)]
        seg = (lane_head[0] == lane_head[1]).astype(BF16)

        def transpose_q(i, _):
            r = pl.ds(pl.multiple_of(i * QBLOCK, QBLOCK), QBLOCK)
            qt_s[:, r] = q_ref[r, :].astype(F32).T.astype(qt_s.dtype)
            return 0

        lax.fori_loop(0, L // QBLOCK, transpose_q, 0)
        dkt_s[...] = jnp.zeros_like(dkt_s)
        dvt_s[...] = jnp.zeros_like(dvt_s)
        r2 = lax.broadcasted_iota(jnp.int32, (KBLOCK, KBLOCK), 0)
        c2 = lax.broadcasted_iota(jnp.int32, (KBLOCK, KBLOCK), 1)
        before_s[...] = (r2 < c2).astype(before_s.dtype)

        def q_pair(p, dgq):
            rows = [pl.ds(pl.multiple_of((p * Q_HALVES + r) * QBLOCK, QBLOCK), QBLOCK) for r in range(Q_HALVES)]
            pair = pl.ds(pl.multiple_of(p * KBLOCK, KBLOCK), KBLOCK)
            do2 = do_ref[pair, :]
            do_t = do2.T.astype(v_s.dtype)
            cs = range(nc)
            hs = range(2)
            q_c = [q_s[h, rows[r], :] for h, r in _CHAINS]
            do_c = [do2[r * QBLOCK:(r + 1) * QBLOCK, _head(h)].astype(v_s.dtype) for h, r in _CHAINS]
            qt_h = [qt_s[_head(h), pair] for h in hs]
            dot_h = [do_t[_head(h), :] for h in hs]

            def kept(kb):
                slot = lax.rem(kb, slots)
                return pltpu.make_async_copy(a_ref.at[g, p, kb], stage_s.at[slot], sems.at[slot])

            def k_block(kb, carry, diagonal):
                rk = pl.ds(pl.multiple_of(kb * KBLOCK, KBLOCK), KBLOCK)
                if diagonal:
                    valid = [_valid(p * Q_HALVES + r, kb) for r in range(Q_HALVES)]
                    keep = lambda c, t: jnp.where(valid[_CHAINS[c][1]], t, 0.0)
                else:
                    keep = lambda c, t: t

                    @pl.when(kb + 2 <= p)
                    def _():
                        kept(kb + 2).start()

                kept(kb).wait()
                slot = lax.rem(kb, slots)
                k_b = [k_s[h, rk, :] for h in hs]
                z = [_dot(q_c[c], k_b[_CHAINS[c][0]], _NT) for c in cs]
                da = [_dot(do_c[c], v_s[_CHAINS[c][0], rk, :], _NT) for c in cs]
                a = [stage_s[slot, c] for c in cs]
                dla = [a[c].astype(F32) * da[c] for c in cs]
                for h in hs:
                    a_h = jnp.concatenate([a[c] for c in cs if _CHAINS[c][0] == h], axis=0)
                    dvt_s[_head(h), rk] += _dot(dot_h[h], a_h)
                d_lom = [carry[c][0] + _dot(dla[c].astype(BF16), before_s[...]) for c in cs]
                beta = [jax.nn.sigmoid(z[c]) for c in cs]
                dz_b = [(dla[c] * (1.0 - beta[c]) - keep(c, beta[c] * d_lom[c])).astype(v_s.dtype) for c in cs]
                dq_acc = [carry[c][1] + _dot(dz_b[c], k_b[_CHAINS[c][0]]) for c in cs]
                for h in hs:
                    dz_h = jnp.concatenate([dz_b[c] for c in cs if _CHAINS[c][0] == h], axis=0)
                    dkt_s[_head(h), rk] += _dot(qt_h[h], dz_h)
                return tuple((carry[c][0] + jnp.sum(dla[c], axis=1, keepdims=True), dq_acc[c]) for c in cs)

            init = (jnp.zeros((QBLOCK, 1), F32), jnp.zeros((QBLOCK, HEAD_DIM), F32))
            kept(0).start()

            @pl.when(p >= 1)
            def _():
                kept(1).start()

            before = lax.fori_loop(0, p, lambda kb, carry: k_block(kb, carry, False), (init,) * nc)
            res = k_block(p, before, True)
            for r in range(Q_HALVES):
                d_qn = jnp.concatenate([res[c][1] for c in cs if _CHAINS[c][1] == r], axis=1) * scale
                dq, dg = _qk_norm_bwd(qraw_ref[rows[r], :], gq_ref[...], d_qn, seg)
                dq_ref[rows[r], :] = dq.astype(dq_ref.dtype)
                dgq = dgq + dg
            return dgq

        dgq = lax.fori_loop(0, n_pairs, q_pair, jnp.zeros((1, LANES), F32))

        def transpose_out(i, dgk):
            r = pl.ds(pl.multiple_of(i * QBLOCK, QBLOCK), QBLOCK)
            dk, dg = _qk_norm_bwd(kraw_ref[r, :], gk_ref[...], dkt_s[:, r].T, seg)
            dk_ref[r, :] = dk.astype(dk_ref.dtype)
            dv_ref[r, :] = dvt_s[:, r].T.astype(dv_ref.dtype)
            return dgk + dg

        dgk = lax.fori_loop(0, L // QBLOCK, transpose_out, jnp.zeros((1, LANES), F32))

        @pl.when(jnp.logical_and(pl.program_id(0) == 0, pl.program_id(1) == 0))
        def _():
            dgq_ref[...] = jnp.zeros_like(dgq_ref)
            dgk_ref[...] = jnp.zeros_like(dgk_ref)

        dgq_ref[...] += dgq[:, _head(0)] + dgq[:, _head(1)]
        dgk_ref[...] += dgk[:, _head(0)] + dgk[:, _head(1)]

    spec = pl.BlockSpec((L, LANES), lambda b, p: (b, p))
    gain = pl.BlockSpec((1, LANES), lambda b, p: (0, 0))
    gain_grad = pl.BlockSpec((1, HEAD_DIM), lambda b, p: (0, 0))
    return pl.pallas_call(
        body, name="attn_bwd", grid=(B, n_hp),
        in_specs=[spec] * 4 + [spec, pl.BlockSpec((L, LANES), lambda b, p: (b, n_hp + p)), gain, gain, _ANY],
        out_specs=[spec] * 3 + [gain_grad] * 2,
        out_shape=[jax.ShapeDtypeStruct((B * L, SB_WIDTH), BF16)] * 3 + [jax.ShapeDtypeStruct((1, HEAD_DIM), F32)] * 2,
        scratch_shapes=[pltpu.VMEM((2, L, HEAD_DIM), BF16)] * 3 + [pltpu.VMEM((LANES, L), BF16)]
        + [pltpu.VMEM((LANES, L), F32)] * 2 + [pltpu.VMEM((KBLOCK, KBLOCK), BF16)]
        + [pltpu.VMEM((slots, nc, QBLOCK, KBLOCK), BF16), pltpu.SemaphoreType.DMA((slots,))],
        compiler_params=_params(("arbitrary", "arbitrary")),
    )(qn, kn, vb, d_sb, proj, proj, gq2, gk2, kept_a)


def _ssm_discretise(lam_re, lam_im, log_dt, b_re, b_im):
    dt = jnp.exp(log_dt)
    mag = jnp.exp(lam_re * dt)
    lbr = mag * jnp.cos(lam_im * dt)
    lbi = mag * jnp.sin(lam_im * dt)
    den = lam_re * lam_re + lam_im * lam_im
    nr, ni = lbr - 1.0, lbi
    cr = (nr * lam_re + ni * lam_im) / den
    ci = (ni * lam_re - nr * lam_im) / den
    return lbr, lbi, cr * b_re - ci * b_im, cr * b_im + ci * b_re


def _ssm_prep(lam_re, lam_im, log_dt, b_re_t, b_im_t):
    def body(lr, li, ld, br, bi, o_lr, o_li, o_br, o_bi):
        res = _ssm_discretise(lr[...], li[...], ld[...], br[...], bi[...])
        for o, v in zip((o_lr, o_li, o_br, o_bi), res):
            o[...] = v

    return pl.pallas_call(
        body, name="ssm_prep",
        out_shape=[jax.ShapeDtypeStruct(lam_re.shape, F32)] * 2 + [jax.ShapeDtypeStruct(b_re_t.shape, F32)] * 2,
    )(lam_re, lam_im, log_dt, b_re_t, b_im_t)


def _ssm_prep_bwd(lam_re, lam_im, log_dt, b_re_t, b_im_t, d_lr, d_li, d_br, d_bi):
    def body(lr, li, ld, br, bi, g_lr, g_li, g_br, g_bi, o_lr, o_li, o_ld, o_br, o_bi):
        _, vjp = jax.vjp(_ssm_discretise, lr[...], li[...], ld[...], br[...], bi[...])
        res = vjp((g_lr[...], g_li[...], g_br[...], g_bi[...]))
        for o, v in zip((o_lr, o_li, o_ld, o_br, o_bi), res):
            o[...] = v

    return pl.pallas_call(
        body, name="ssm_prep_bwd",
        out_shape=[jax.ShapeDtypeStruct(lam_re.shape, F32)] * 2 + [jax.ShapeDtypeStruct(log_dt.shape, F32)]
        + [jax.ShapeDtypeStruct(b_re_t.shape, F32)] * 2,
    )(lam_re, lam_im, log_dt, b_re_t, b_im_t, d_lr, d_li, d_br, d_bi)


def _block_diag(m):
    m4 = m.reshape(SSM_COLS, 8, SSM_GROUP, SSM_STATE)
    return jnp.einsum("aghp,gk->aghkp", m4, jnp.eye(8, dtype=m.dtype)).reshape(SSM_COLS, LANES, 512)


def _block_diag_take(d):
    d6 = d.reshape(SSM_COLS, 8, SSM_GROUP, 2, 8, SSM_STATE)
    return jnp.einsum("aghrgp->raghp", d6).reshape(2, SSM_GROUPS, SSM_GROUP, SSM_STATE)


def _cmul(ar, ai, br, bi):
    return ar * br - ai * bi, ar * bi + ai * br


def _power(lr, li, n):
    assert n & (n - 1) == 0
    for _ in range(n.bit_length() - 1):
        lr, li = _cmul(lr, li, lr, li)
    return lr, li


def _ssm_fwd(u_p, w_b, lam_r, lam_i, c_m, d_skip, B, L, tj):
    J = L // N_CHUNK
    njt = J // tj
    R = tj * N_CHUNK
    H = 512

    def body(u_ref, wb_ref, lr_ref, li_ref, cm_ref, d_ref, y_ref, gel_ref, x_ref, xin_ref, bu_s, st_s, xin_s):
        ph, jt = pl.program_id(2), pl.program_id(3)
        lr, li = lr_ref[...], li_ref[...]

        @pl.when(jnp.logical_and(ph == 0, jt == 0))
        def _():
            st_s[...] = jnp.zeros_like(st_s)

        @pl.when(ph == 0)
        def _():
            bu_s[jt] = _dot(u_ref[...].astype(BF16), wb_ref[...].astype(BF16))

        def scan(store):
            def step(j, carry):
                xr, xi = carry
                r = pl.ds(pl.multiple_of(j * N_CHUNK, N_CHUNK), N_CHUNK)
                nr = lr * xr - li * xi + bu_s[jt, r, 0:H]
                ni = lr * xi + li * xr + bu_s[jt, r, H:2 * H]
                if store:
                    x_ref[r, 0:H] = nr
                    x_ref[r, H:2 * H] = ni
                return nr, ni

            xr, xi = lax.fori_loop(0, tj, step, (st_s[:, 0:H], st_s[:, H:2 * H]))
            st_s[:, 0:H] = xr
            st_s[:, H:2 * H] = xi

        @pl.when(ph == 0)
        def _():
            scan(False)

            @pl.when(jt == njt - 1)
            def _():
                pr, pi = _power(lr[0:1], li[0:1], J)
                xin_s[0:1, :] = jnp.zeros((1, 2 * H), F32)
                for c in range(1, N_CHUNK):
                    qr, qi = _cmul(pr, pi, xin_s[c - 1:c, 0:H], xin_s[c - 1:c, H:2 * H])
                    xin_s[c:c + 1, 0:H] = qr + st_s[c - 1:c, 0:H]
                    xin_s[c:c + 1, H:2 * H] = qi + st_s[c - 1:c, H:2 * H]
                xin_ref[...] = xin_s[...]
                st_s[...] = xin_s[...]

        @pl.when(ph == 1)
        def _():
            scan(True)
            y = _dot(x_ref[...].astype(BF16), cm_ref[...].astype(BF16)) + d_ref[...] * u_ref[...]
            y_ref[...] = y
            gel_ref[...] = jax.nn.gelu(y).astype(gel_ref.dtype)

    return pl.pallas_call(
        body, name="ssm_fwd", grid=(SSM_COLS, B, 2, njt),
        in_specs=[
            pl.BlockSpec((None, R, LANES), lambda i, b, ph, jt: (b, jt, i)),
            pl.BlockSpec((None, LANES, 2 * H), lambda i, b, ph, jt: (i, 0, 0)),
            pl.BlockSpec((None, N_CHUNK, H), lambda i, b, ph, jt: (i, 0, 0)),
            pl.BlockSpec((None, N_CHUNK, H), lambda i, b, ph, jt: (i, 0, 0)),
            pl.BlockSpec((None, 2 * H, LANES), lambda i, b, ph, jt: (i, 0, 0)),
            pl.BlockSpec((1, LANES), lambda i, b, ph, jt: (0, i)),
        ],
        out_specs=[
            pl.BlockSpec((None, R, LANES), lambda i, b, ph, jt: (b, jt * ph, i)),
            pl.BlockSpec((None, R, LANES), lambda i, b, ph, jt: (b, jt * ph, i)),
            pl.BlockSpec((None, R, 2 * H), lambda i, b, ph, jt: (b, jt * ph, i)),
            pl.BlockSpec((None, None, N_CHUNK, 2 * H), lambda i, b, ph, jt: (b, i, 0, 0)),
        ],
        out_shape=[
            jax.ShapeDtypeStruct((B, L, SSM_WIDTH), F32),
            jax.ShapeDtypeStruct((B, L, SSM_WIDTH), BF16),
            jax.ShapeDtypeStruct((B, L, SSM_COLS * 2 * H), F32),
            jax.ShapeDtypeStruct((B, SSM_COLS, N_CHUNK, 2 * H), F32),
        ],
        scratch_shapes=[pltpu.VMEM((njt, R, 2 * H), F32), pltpu.VMEM((N_CHUNK, 2 * H), F32),
                        pltpu.VMEM((N_CHUNK, 2 * H), F32)],
        compiler_params=_params(("arbitrary",) * 4),
    )(u_p, w_b, lam_r, lam_i, c_m, d_skip)


def _ssm_bwd(dy_p, u_p, x, xin, w_bt, lam_r, lam_i, c_mt, d_skip, B, L, tj):
    J = L // N_CHUNK
    njt = J // tj
    R = tj * N_CHUNK
    H = 512
    x4 = x.reshape(B, J, N_CHUNK, SSM_COLS * 2 * H)

    def body(dy_ref, u_ref, x_ref, xp_ref, xin_ref, wbt_ref, lr_ref, li_ref, cmt_ref, d_ref,
             du_ref, dwb_ref, dcm_ref, dlr_ref, dli_ref, dd_ref, ca_s, a_s, st_s, dl_s):
        b, ph, jt = pl.program_id(1), pl.program_id(2), pl.program_id(3)
        jr = njt - 1 - jt
        lr, li = lr_ref[...], -li_ref[...]

        @pl.when(jnp.logical_and(b == 0, jnp.logical_and(ph == 0, jt == 0)))
        def _():
            dwb_ref[...] = jnp.zeros_like(dwb_ref)
            dcm_ref[...] = jnp.zeros_like(dcm_ref)
            dlr_ref[...] = jnp.zeros_like(dlr_ref)
            dli_ref[...] = jnp.zeros_like(dli_ref)
            dd_ref[...] = jnp.zeros_like(dd_ref)
            dl_s[...] = jnp.zeros_like(dl_s)

        @pl.when(jnp.logical_and(ph == 0, jt == 0))
        def _():
            st_s[...] = jnp.zeros_like(st_s)

        @pl.when(ph == 0)
        def _():
            ca_s[jt] = _dot(dy_ref[...].astype(BF16), cmt_ref[...].astype(BF16))

        def scan(store):
            def step(n, carry):
                ar, ai = carry
                r = pl.ds(pl.multiple_of((tj - 1 - n) * N_CHUNK, N_CHUNK), N_CHUNK)
                nr = lr * ar - li * ai + ca_s[jt, r, 0:H]
                ni = lr * ai + li * ar + ca_s[jt, r, H:2 * H]
                if store:
                    a_s[r, 0:H] = nr
                    a_s[r, H:2 * H] = ni
                return nr, ni

            ar, ai = lax.fori_loop(0, tj, step, (st_s[:, 0:H], st_s[:, H:2 * H]))
            st_s[:, 0:H] = ar
            st_s[:, H:2 * H] = ai

        @pl.when(ph == 0)
        def _():
            scan(False)

            @pl.when(jt == njt - 1)
            def _():
                pr, pi = _power(lr[0:1], li[0:1], J)
                a_s[N_CHUNK - 1:N_CHUNK, :] = jnp.zeros((1, 2 * H), F32)
                for c in range(N_CHUNK - 2, -1, -1):
                    qr, qi = _cmul(pr, pi, a_s[c + 1:c + 2, 0:H], a_s[c + 1:c + 2, H:2 * H])
                    a_s[c:c + 1, 0:H] = qr + st_s[c + 1:c + 2, 0:H]
                    a_s[c:c + 1, H:2 * H] = qi + st_s[c + 1:c + 2, H:2 * H]
                st_s[...] = a_s[0:N_CHUNK, :]

        @pl.when(ph == 1)
        def _():
            scan(True)
            dy = dy_ref[...]
            u = u_ref[...]
            a_b = a_s[...].astype(BF16)
            du_ref[...] = (_dot(a_b, wbt_ref[...].astype(BF16)) + d_ref[...] * dy).astype(du_ref.dtype)
            dwb_ref[...] += _dot(u.astype(BF16), a_b, _TN)
            dcm_ref[...] += _dot(x_ref[...].astype(BF16), dy.astype(BF16), _TN)
            dd_ref[...] += jnp.sum(dy * u, axis=0, keepdims=True)

            first = jnp.where(jr == 0, xin_ref[...], xp_ref[...])
            a0r, a0i = a_s[0:N_CHUNK, 0:H], a_s[0:N_CHUNK, H:2 * H]
            acc0 = (a0r * first[:, 0:H] + a0i * first[:, H:2 * H], a0i * first[:, 0:H] - a0r * first[:, H:2 * H])

            def step(j, carry):
                sr, si = carry
                r = pl.ds(pl.multiple_of(j * N_CHUNK, N_CHUNK), N_CHUNK)
                rp = pl.ds(pl.multiple_of((j - 1) * N_CHUNK, N_CHUNK), N_CHUNK)
                ar, ai = a_s[r, 0:H], a_s[r, H:2 * H]
                xr, xi = x_ref[rp, 0:H], x_ref[rp, H:2 * H]
                return sr + ar * xr + ai * xi, si + ai * xr - ar * xi

            sr, si = lax.fori_loop(1, tj, step, acc0)
            dl_s[:, 0:H] += sr
            dl_s[:, H:2 * H] += si

            @pl.when(jnp.logical_and(b == B - 1, jt == njt - 1))
            def _():
                dlr_ref[...] = jnp.sum(dl_s[:, 0:H], axis=0, keepdims=True)
                dli_ref[...] = jnp.sum(dl_s[:, H:2 * H], axis=0, keepdims=True)
                dl_s[...] = jnp.zeros_like(dl_s)

    rev = lambda ph, jt: (njt - 1 - jt) * ph + (njt - 1) * (1 - ph)
    return pl.pallas_call(
        body, name="ssm_bwd", grid=(SSM_COLS, B, 2, njt),
        in_specs=[
            pl.BlockSpec((None, R, LANES), lambda i, b, ph, jt: (b, njt - 1 - jt, i)),
            pl.BlockSpec((None, R, LANES), lambda i, b, ph, jt: (b, njt - 1 - jt, i)),
            pl.BlockSpec((None, R, 2 * H), lambda i, b, ph, jt: (b, rev(ph, jt), i)),
            pl.BlockSpec((None, None, N_CHUNK, 2 * H),
                         lambda i, b, ph, jt: (b, jnp.maximum((njt - 1 - jt) * tj - 1, 0), 0, i)),
            pl.BlockSpec((None, None, N_CHUNK, 2 * H), lambda i, b, ph, jt: (b, i, 0, 0)),
            pl.BlockSpec((None, 2 * H, LANES), lambda i, b, ph, jt: (i, 0, 0)),
            pl.BlockSpec((None, N_CHUNK, H), lambda i, b, ph, jt: (i, 0, 0)),
            pl.BlockSpec((None, N_CHUNK, H), lambda i, b, ph, jt: (i, 0, 0)),
            pl.BlockSpec((None, LANES, 2 * H), lambda i, b, ph, jt: (i, 0, 0)),
            pl.BlockSpec((1, LANES), lambda i, b, ph, jt: (0, i)),
        ],
        out_specs=[
            pl.BlockSpec((None, R, LANES), lambda i, b, ph, jt: (b, rev(ph, jt), i)),
            pl.BlockSpec((None, LANES, 2 * H), lambda i, b, ph, jt: (i, 0, 0)),
            pl.BlockSpec((None, 2 * H, LANES), lambda i, b, ph, jt: (i, 0, 0)),
            pl.BlockSpec((None, 1, H), lambda i, b, ph, jt: (i, 0, 0)),
            pl.BlockSpec((None, 1, H), lambda i, b, ph, jt: (i, 0, 0)),
            pl.BlockSpec((1, LANES), lambda i, b, ph, jt: (0, i)),
        ],
        out_shape=[
            jax.ShapeDtypeStruct((B, L, SSM_WIDTH), BF16),
            jax.ShapeDtypeStruct((SSM_COLS, LANES, 2 * H), F32),
            jax.ShapeDtypeStruct((SSM_COLS, 2 * H, LANES), F32),
            jax.ShapeDtypeStruct((SSM_COLS, 1, H), F32),
            jax.ShapeDtypeStruct((SSM_COLS, 1, H), F32),
            jax.ShapeDtypeStruct((1, SSM_WIDTH), F32),
        ],
        scratch_shapes=[pltpu.VMEM((njt, R, 2 * H), F32), pltpu.VMEM((R, 2 * H), F32),
                        pltpu.VMEM((N_CHUNK, 2 * H), F32), pltpu.VMEM((N_CHUNK, 2 * H), F32)],
        compiler_params=_params(("arbitrary",) * 4),
    )(dy_p, u_p, x, x4, xin, w_bt, lam_r, lam_i, c_mt, d_skip)


def _to_scan_layout(t, B, L):
    C = t.shape[-1]
    return t.reshape(B, N_CHUNK, L // N_CHUNK, C).transpose(0, 2, 1, 3).reshape(B, L, C)


def _from_scan_layout(t, B, L):
    C = t.shape[-1]
    return t.reshape(B, L // N_CHUNK, N_CHUNK, C).transpose(0, 2, 1, 3).reshape(B * L, C)


def _local_step(x, target, p, w_in, late_weights, mlp_grads_ready=None, rest_grads_ready=None, order=None, *,
                ssm_tile=256):
    B, L, D = x.shape
    T = B * L
    x2 = x.reshape(T, D)
    row = lambda v: v.reshape(1, -1)
    g1, g2, ga, gs, b_glu = row(p["norm1_g"]), row(p["norm2_g"]), row(p["attn_out_g"]), row(p["ssm_out_g"]), row(p["b_glu"])
    g1_first = g1 if order is None else g1 + order
    gq8 = jnp.tile(row(p["q_norm_g"]), (1, N_HEADS))
    gk8 = jnp.tile(row(p["k_norm_g"]), (1, N_HEADS))

    G, P, Hh = SSM_GROUPS, SSM_STATE, SSM_GROUP
    lam_re3, lam_im3 = p["ssm_lambda_re"].reshape(G, 1, P), p["ssm_lambda_im"].reshape(G, 1, P)
    log_dt3 = p["ssm_log_dt"].reshape(G, 1, 1)
    b_re_t, b_im_t = p["ssm_b_re"].transpose(0, 2, 1), p["ssm_b_im"].transpose(0, 2, 1)
    lbr, lbi, bbr, bbi = _ssm_prep(lam_re3, lam_im3, log_dt3, b_re_t, b_im_t)
    w_b = jnp.concatenate([_block_diag(bbr), _block_diag(bbi)], axis=2)
    c_mt = jnp.concatenate([_block_diag(p["ssm_c_re"]), -_block_diag(p["ssm_c_im"])], axis=2)
    w_bt, c_m = w_b.transpose(0, 2, 1), c_mt.transpose(0, 2, 1)
    lam_r = jnp.broadcast_to(lbr.reshape(SSM_COLS, 1, 512), (SSM_COLS, N_CHUNK, 512))
    lam_i = jnp.broadcast_to(lbi.reshape(SSM_COLS, 1, 512), (SSM_COLS, N_CHUNK, 512))
    d_skip = p["ssm_d"].reshape(1, SSM_WIDTH)

    qk_gains = jnp.concatenate([gq8 * (1.0 / math.sqrt(HEAD_DIM)), gk8, jnp.ones((1, 2 * SB_WIDTH), F32)], axis=1)
    head = (BF16, SB_WIDTH)
    proj, qn, kn, vb, xn = _matmul("proj", x2, w_in, prologue=(_rms, g1_first), extras=[qk_gains],
                                   out_dtypes=(F32, head, head, head), epilogue=_qk_norm, tm=512, tn=w_in.shape[1])
    sb, attn_kept = _attn_fwd(qn, kn, vb, B, L)
    u_p = _to_scan_layout(proj[:, 3 * SB_WIDTH:], B, L)
    y_p, gel_p, xs, xin = _ssm_fwd(u_p, w_b, lam_r, lam_i, c_m, d_skip, B, L, ssm_tile)
    y2, gel = y_p.reshape(T, SSM_WIDTH), gel_p.reshape(T, SSM_WIDTH)
    gel, sb = lax.optimization_barrier((gel, sb))
    w_glu, w_out, w_mlp_in, w_mlp_out = late_weights(gel)
    pre, ssm_n = _matmul("glu_gate", gel, w_glu, extras=[y2, b_glu, gs], out_dtypes=(F32, BF16),
                         epilogue=lambda acc, y, b, g: (acc, _glu_branch(y, acc, b, g)))
    mixed = _rowwise(
        "attn_out_norm", lambda s, n, g: jnp.concatenate([_rms(s, g).astype(BF16), n.astype(BF16)], axis=1),
        [sb, _from_scan_layout(ssm_n, B, L)], [ga], [(D, BF16)])
    def residual_and_norm(acc, res, g):
        h = acc + res
        return h, _rms(h, g)

    h1, hn = _matmul("out_proj", mixed, w_out, extras=[x2, g2], out_dtypes=(F32, BF16), tn=D,
                     epilogue=residual_and_norm)
    act, a_pre = _matmul("mlp_in", hn, w_mlp_in, out_dtypes=(BF16, BF16), tn=2048,
                         epilogue=lambda acc: (jnp.square(jnp.maximum(acc, 0.0)), acc))

    def loss_fn(acc, h, t):
        diff = acc + h - t
        part = jnp.sum(jnp.sum(diff * diff, axis=0, keepdims=True), axis=1, keepdims=True)
        d = diff * (1.0 / D)
        return d, d, part * (0.5 / D)

    d_out, d_out_b, loss = _matmul("mlp_out", act, w_mlp_out, extras=[h1, target.reshape(T, D)],
                                   out_dtypes=(F32, BF16), sums=[(1, 1)], epilogue=loss_fn, tm=512, tn=D)

    d_apre = _matmul("mlp_out_dx", d_out_b, w_mlp_out, tb=True, extras=[a_pre], out_dtypes=(BF16,), tn=2048,
                     epilogue=lambda acc, ap: (acc * (2.0 * jnp.maximum(ap.astype(F32), 0.0)),))
    both = lambda acc: (acc, acc)
    g_w_mlp_out, g_w_mlp_out_b = _matmul("mlp_out_dw", act, d_out_b, ta=True, out_dtypes=(F32, BF16), epilogue=both,
                                         tn=1024)
    g_w_mlp_in, g_w_mlp_in_b = _matmul("mlp_in_dw", hn, d_apre, ta=True, col_blocked=True, out_dtypes=(F32, BF16),
                                       epilogue=both, tn=w_mlp_in.shape[1] // N_DEV)
    if mlp_grads_ready is not None:
        g2 = g2 + mlp_grads_ready(g_w_mlp_out, g_w_mlp_out_b, g_w_mlp_in, g_w_mlp_in_b)

    def norm_bwd_res(dy, h, res, g):
        _, vjp = jax.vjp(_rms, h, g)
        dh, dg = vjp(dy)
        return res + dh, dg

    def norm_bwd_res2(dy, h, res, g):
        d, dg = norm_bwd_res(dy, h, res, g)
        return d, d, dg

    d_h1, d_h1_b, g_norm2 = _matmul("mlp_in_dx", d_apre, w_mlp_in, tb=True, extras=[h1, d_out, g2],
                                    out_dtypes=(F32, BF16), sums=[(1, D)], epilogue=norm_bwd_res2, tm=512, tn=D)

    d_mixed = _matmul("out_proj_dx", d_h1_b, w_out, tb=True, tn=1024)
    g_w_out, g_w_out_b = _matmul("out_proj_dw", mixed, d_h1_b, ta=True, out_dtypes=(F32, BF16), epilogue=both)

    def norm_bwd(h, dy, g):
        _, vjp = jax.vjp(_rms, h, g)
        return vjp(dy)

    d_sb, g_attn_out = _rowwise("attn_out_norm_bwd", norm_bwd, [sb, (d_mixed, 0, SB_WIDTH)], [ga],
                                [(SB_WIDTH, F32)], sums=[(1, SB_WIDTH)])
    d_ssm_n = _to_scan_layout(d_mixed[:, SB_WIDTH:], B, L).reshape(T, SSM_WIDTH)

    def glu_bwd(y, pre_, dy, bg, g):
        _, vjp = jax.vjp(_glu_branch, y, pre_, bg, g)
        d_y, d_pre, d_bg, d_g = vjp(dy)
        return d_y, d_pre, d_bg, d_g

    d_y_direct, d_pre, g_b_glu, g_ssm_out = _rowwise(
        "glu_out_bwd", glu_bwd, [y2, pre, d_ssm_n], [b_glu, gs], [(SSM_WIDTH, F32), (SSM_WIDTH, BF16)],
        sums=[(1, SSM_WIDTH), (1, SSM_WIDTH)])
    g_w_glu, g_w_glu_b = _matmul("glu_gate_dw", gel, d_pre, ta=True, out_dtypes=(F32, BF16), epilogue=both)

    def gelu_bwd(dg, y, dy0):
        _, vjp = jax.vjp(jax.nn.gelu, y)
        return (dy0 + vjp(dg)[0],)

    d_y = _matmul("glu_gate_dx", d_pre, w_glu, tb=True, extras=[y2, d_y_direct], epilogue=gelu_bwd)

    du_p, d_wb, d_cm, d_lr, d_li, g_d = _ssm_bwd(
        d_y.reshape(B, L, SSM_WIDTH), u_p, xs, xin, w_bt, lam_r, lam_i, c_mt, d_skip, B, L, ssm_tile)
    d_bb = _block_diag_take(d_wb.reshape(SSM_COLS, LANES, 2, 512))
    d_c = _block_diag_take(d_cm.transpose(0, 2, 1).reshape(SSM_COLS, LANES, 2, 512))
    g_lam_re, g_lam_im, g_log_dt, g_b_re_t, g_b_im_t = _ssm_prep_bwd(
        lam_re3, lam_im3, log_dt3, b_re_t, b_im_t,
        d_lr.reshape(G, 1, P), d_li.reshape(G, 1, P), d_bb[0], d_bb[1])
    d_q, d_k, d_v, g_q, g_k = _attn_bwd(qn, kn, vb, attn_kept, d_sb, proj, gq8[:, :LANES], gk8[:, :LANES], B, L)

    d_proj = jnp.concatenate([d_q, d_k, d_v, _from_scan_layout(du_p, B, L)], axis=1)
    g_w_in, g_w_in_b = _matmul("proj_dw", xn, d_proj, ta=True, col_blocked=True, out_dtypes=(F32, BF16),
                               epilogue=both, tn=w_in.shape[1] // N_DEV)
    if rest_grads_ready is not None:
        g1 = g1 + rest_grads_ready([g_w_in, g_w_glu, g_w_out], [g_w_in_b, g_w_glu_b, g_w_out_b])
    grad_x, g_norm1 = _matmul("proj_dx", d_proj, w_in, tb=True, extras=[x2, d_h1, g1], sums=[(1, D)],
                              epilogue=norm_bwd_res, tm=512, tn=D)

    small = {
        "norm1_g": g_norm1.reshape(-1),
        "q_norm_g": g_q.reshape(-1),
        "k_norm_g": g_k.reshape(-1),
        "ssm_lambda_re": g_lam_re.reshape(G, P),
        "ssm_lambda_im": g_lam_im.reshape(G, P),
        "ssm_log_dt": g_log_dt.reshape(G),
        "ssm_b_re": g_b_re_t.transpose(0, 2, 1),
        "ssm_b_im": g_b_im_t.transpose(0, 2, 1),
        "ssm_c_re": d_c[0],
        "ssm_c_im": -d_c[1],
        "ssm_d": g_d.reshape(G, Hh),
        "b_glu": g_b_glu.reshape(-1),
        "attn_out_g": g_attn_out.reshape(-1),
        "ssm_out_g": g_ssm_out.reshape(-1),
        "norm2_g": g_norm2.reshape(-1),
    }
    big = {"w_in": g_w_in, "w_glu": g_w_glu, "w_out": g_w_out, "w_mlp_in": g_w_mlp_in, "w_mlp_out": g_w_mlp_out}
    return loss[0, 0], grad_x.reshape(B, L, D), small, big


_ANY = pl.BlockSpec(memory_space=pl.ANY)
_MESH = pl.DeviceIdType.MESH


def _all_gather(name, shards):
    n = len(shards)

    def body(*refs):
        in_refs, out_refs = refs[:n], refs[n:2 * n]
        send_sems, recv_sems, local_sems = refs[2 * n:]
        x, y, c = lax.axis_index("x"), lax.axis_index("y"), lax.axis_index("c")
        me, sibling = (x, y, c), (x, y, 1 - c)
        chips = [(1 - x, y), (x, 1 - y), (1 - x, 1 - y)]

        def copy(a, k, block, to, src=None):
            px, py, pc = block
            rows = out_refs[a].at[4 * px + 2 * py + pc]
            return pltpu.make_async_remote_copy(
                src_ref=rows if src is None else src, dst_ref=rows, send_sem=send_sems.at[a, k],
                recv_sem=recv_sems.at[a, k], device_id=to, device_id_type=_MESH)

        mine = [pltpu.make_async_copy(in_refs[a], out_refs[a].at[4 * x + 2 * y + c], local_sems.at[a]) for a in range(n)]
        first, passed = [], []
        for a in range(n):
            mine[a].start()
            first.append(copy(a, 0, me, sibling, src=in_refs[a]))
            first += [copy(a, 1 + j, me, (*chip, c), src=in_refs[a]) for j, chip in enumerate(chips)]
        for cp in first:
            cp.start()
        for j, chip in enumerate(chips):
            for a in range(n):
                copy(a, 1 + j, (*chip, c), me).wait_recv()
                fwd = copy(a, 4 + j, (*chip, c), sibling)
                fwd.start()
                passed.append(fwd)
        for a in range(n):
            copy(a, 0, sibling, me).wait_recv()
            for j, chip in enumerate(chips):
                copy(a, 4 + j, (*chip, 1 - c), me).wait_recv()
        for cp in first + passed:
            cp.wait_send()
        for cp in mine:
            cp.wait()

    return pl.pallas_call(
        body, name=name,
        in_specs=[_ANY] * n, out_specs=[_ANY] * n,
        out_shape=[jax.ShapeDtypeStruct((N_DEV, *s.shape), s.dtype) for s in shards],
        scratch_shapes=[pltpu.SemaphoreType.DMA((n, 7)), pltpu.SemaphoreType.DMA((n, 7)), pltpu.SemaphoreType.DMA((n,))],
    )(*shards)


_HBM = pl.BlockSpec(memory_space=pltpu.HBM)
_SEM = pl.BlockSpec(memory_space=pltpu.SEMAPHORE)
_EFFECT = pltpu.SideEffectType.DATAFLOW_SIDE_EFFECTING
_FLIPS = [(dx, dy, dc) for dx in (0, 1) for dy in (0, 1) for dc in (0, 1) if (dx, dy, dc) != (0, 0, 0)]


def _exchange_start(name, srcs, lands, per_peer):
    n = len(srcs)

    def body(*refs):
        src_refs, land_refs = refs[:n], refs[n:2 * n]
        send_sems, recv_sems = refs[2 * n:3 * n], refs[3 * n:4 * n]
        token = refs[-1]
        x, y, c = lax.axis_index("x"), lax.axis_index("y"), lax.axis_index("c")
        me = 4 * x + 2 * y + c
        for dx, dy, dc in _FLIPS:
            px, py, pc = (1 - x if dx else x), (1 - y if dy else y), (1 - c if dc else c)
            for a in range(n):
                pltpu.make_async_remote_copy(
                    src_ref=src_refs[a].at[4 * px + 2 * py + pc] if per_peer else src_refs[a],
                    dst_ref=land_refs[a].at[me], send_sem=send_sems[a], recv_sem=recv_sems[a],
                    device_id=(px, py, pc), device_id_type=_MESH).start()
        token[...] = jnp.zeros_like(token)

    hbm = lambda t: pltpu.with_memory_space_constraint(t, pltpu.HBM)
    res = pl.pallas_call(
        body, name=name,
        out_shape=(*[pltpu.SemaphoreType.DMA(())] * (2 * n), *[pltpu.HBM(t.shape, t.dtype) for t in (*srcs, *lands)],
                   jax.ShapeDtypeStruct((8, LANES), F32)),
        in_specs=[_HBM] * (2 * n),
        out_specs=(*[_SEM] * (2 * n), *[_HBM] * (2 * n), pl.BlockSpec(memory_space=pltpu.VMEM)),
        input_output_aliases={i: 2 * n + i for i in range(2 * n)},
        compiler_params=pltpu.CompilerParams(has_side_effects=_EFFECT),
    )(*[hbm(t) for t in (*srcs, *lands)])
    return res[:-1], res[-1]


def _exchange_wait(name, handle, after):
    n = len(handle) // 4
    sems, thru = handle[:2 * n], handle[2 * n:]

    def body(*refs):
        land_refs = refs[n:2 * n]
        send_sems, recv_sems = refs[2 * n:3 * n], refs[3 * n:4 * n]
        me = (lax.axis_index("x"), lax.axis_index("y"), lax.axis_index("c"))
        for a in range(n):
            seven = land_refs[a].at[pl.ds(0, len(_FLIPS))]
            all_copies = pltpu.make_async_remote_copy(
                src_ref=seven, dst_ref=seven, send_sem=send_sems[a], recv_sem=recv_sems[a], device_id=me,
                device_id_type=_MESH)
            all_copies.wait_send()
            all_copies.wait_recv()

    res = pl.pallas_call(
        body, name=name, out_shape=tuple(pltpu.HBM(t.shape, t.dtype) for t in thru),
        in_specs=[*[_HBM] * (2 * n), *[_SEM] * (2 * n), _ANY], out_specs=tuple([_HBM] * (2 * n)),
        input_output_aliases={i: i for i in range(2 * n)},
        compiler_params=pltpu.CompilerParams(has_side_effects=_EFFECT),
    )(*thru, *sems, after)
    return res[n:]


def _adamw_gathered(name, own, parts, me, w, m, v):
    r, c = w.shape
    tr = min(r, 256)

    def body(me_ref, own_ref, p_ref, w_ref, m_ref, v_ref, g_out, d_out, m_out, v_out):
        g = own_ref[...]
        for j in range(N_DEV):
            g = g + p_ref[j].astype(F32)
        delta, m_new, v_new = _adamw(w_ref[...], g, m_ref[...], v_ref[...])
        g_out[...] = g
        d_out[...] = delta
        m_out[...] = m_new
        v_out[...] = v_new

    spec = pl.BlockSpec((tr, c), lambda i, me_ref: (i, 0))
    return pl.pallas_call(
        body, name=name,
        grid_spec=pltpu.PrefetchScalarGridSpec(
            num_scalar_prefetch=1, grid=(r // tr,),
            in_specs=[pl.BlockSpec((None, tr, c), lambda i, me_ref: (me_ref[0], i, 0)),
                      pl.BlockSpec((N_DEV, tr, c), lambda i, me_ref: (0, i, 0)), spec, spec, spec],
            out_specs=[spec] * 4),
        out_shape=[jax.ShapeDtypeStruct((r, c), F32)] * 4,
        compiler_params=_params(("parallel",)),
    )(me, own, parts, w, m, v)


def _adamw(w, g, m, v):
    m = ADAM_B1 * m + (1.0 - ADAM_B1) * g
    v = ADAM_B2 * v + (1.0 - ADAM_B2) * jnp.square(g)
    m_hat = m / (1.0 - ADAM_B1 ** ADAM_STEP)
    v_hat = v / (1.0 - ADAM_B2 ** ADAM_STEP)
    delta = -ADAM_LR * (m_hat / (jnp.sqrt(v_hat) + ADAM_EPS) + ADAM_WD * w)
    return delta, m, v


def _adamw_small(name, parts, w, m, v):
    _, r, c = parts.shape
    tr = 8

    def body(p_ref, w_ref, m_ref, v_ref, g_out, d_out, m_out, v_out):
        g = p_ref[0]
        for j in range(1, N_DEV):
            g = g + p_ref[j]
        delta, m_new, v_new = _adamw(w_ref[...], g, m_ref[...], v_ref[...])
        g_out[...] = g
        d_out[...] = delta
        m_out[...] = m_new
        v_out[...] = v_new

    spec = pl.BlockSpec((tr, c), lambda i: (i, 0))
    return pl.pallas_call(
        body, name=name, grid=(r // tr,),
        in_specs=[pl.BlockSpec((N_DEV, tr, c), lambda i: (0, i, 0)), spec, spec, spec],
        out_specs=[spec] * 4, out_shape=[jax.ShapeDtypeStruct((r, c), F32)] * 4,
        compiler_params=_params(("parallel",)),
    )(parts, w, m, v)


_WEIGHTS = ["norm1_g", "w_in", "q_norm_g", "k_norm_g", "ssm_lambda_re", "ssm_lambda_im", "ssm_log_dt", "ssm_b_re",
            "ssm_b_im", "ssm_c_re", "ssm_c_im", "ssm_d", "w_glu", "b_glu", "attn_out_g", "ssm_out_g", "w_out",
            "norm2_g", "w_mlp_in", "w_mlp_out"]
_BIG = ["w_in", "w_glu", "w_out", "w_mlp_in", "w_mlp_out"]
_SMALL = [n for n in _WEIGHTS if n not in _BIG]
_PACK_COLS = 1024


def _pack(tree, last=None):
    flat = [tree[n].reshape(-1).astype(F32) for n in _SMALL]
    size = sum(f.shape[0] for f in flat)
    rows = -(-(size + 1) // (_PACK_COLS * 8)) * 8
    pad = jnp.zeros((rows * _PACK_COLS - size - 1,), F32)
    tail = jnp.zeros((1,), F32) if last is None else last.reshape(1).astype(F32)
    return jnp.concatenate(flat + [pad, tail]).reshape(rows, _PACK_COLS)


def _unpack(buf, like):
    flat, out, off = buf.reshape(-1), {}, 0
    for n in _SMALL:
        size = like[n].size
        out[n] = flat[off:off + size].reshape(like[n].shape)
        off += size
    return out


def kernel(x, norm1_g, w_in, q_norm_g, k_norm_g, ssm_lambda_re, ssm_lambda_im, ssm_log_dt, ssm_b_re, ssm_b_im, ssm_c_re, ssm_c_im, ssm_d, w_glu, b_glu, attn_out_g, ssm_out_g, w_out, norm2_g, w_mlp_in, w_mlp_out, loss_target, m_norm1_g, m_w_in, m_q_norm_g, m_k_norm_g, m_ssm_lambda_re, m_ssm_lambda_im, m_ssm_log_dt, m_ssm_b_re, m_ssm_b_im, m_ssm_c_re, m_ssm_c_im, m_ssm_d, m_w_glu, m_b_glu, m_attn_out_g, m_ssm_out_g, m_w_out, m_norm2_g, m_w_mlp_in, m_w_mlp_out, v_norm1_g, v_w_in, v_q_norm_g, v_k_norm_g, v_ssm_lambda_re, v_ssm_lambda_im, v_ssm_log_dt, v_ssm_b_re, v_ssm_b_im, v_ssm_c_re, v_ssm_c_im, v_ssm_d, v_w_glu, v_b_glu, v_attn_out_g, v_ssm_out_g, v_w_out, v_norm2_g, v_w_mlp_in, v_w_mlp_out):
    w = dict(zip(_WEIGHTS, (norm1_g, w_in, q_norm_g, k_norm_g, ssm_lambda_re, ssm_lambda_im, ssm_log_dt, ssm_b_re, ssm_b_im, ssm_c_re, ssm_c_im, ssm_d, w_glu, b_glu, attn_out_g, ssm_out_g, w_out, norm2_g, w_mlp_in, w_mlp_out)))
    m = dict(zip(_WEIGHTS, (m_norm1_g, m_w_in, m_q_norm_g, m_k_norm_g, m_ssm_lambda_re, m_ssm_lambda_im, m_ssm_log_dt, m_ssm_b_re, m_ssm_b_im, m_ssm_c_re, m_ssm_c_im, m_ssm_d, m_w_glu, m_b_glu, m_attn_out_g, m_ssm_out_g, m_w_out, m_norm2_g, m_w_mlp_in, m_w_mlp_out)))
    v = dict(zip(_WEIGHTS, (v_norm1_g, v_w_in, v_q_norm_g, v_k_norm_g, v_ssm_lambda_re, v_ssm_lambda_im, v_ssm_log_dt, v_ssm_b_re, v_ssm_b_im, v_ssm_c_re, v_ssm_c_im, v_ssm_d, v_w_glu, v_b_glu, v_attn_out_g, v_ssm_out_g, v_w_out, v_norm2_g, v_w_mlp_in, v_w_mlp_out)))
    core = lax.axis_index("c").astype(jnp.int32).reshape(1)
    chip = (2 * lax.axis_index("x") + lax.axis_index("y")).astype(jnp.int32).reshape(1)

    me = (2 * chip + core).astype(jnp.int32)

    def landing(own=None, like=None):
        own = jnp.zeros_like(like) if own is None else own
        return lax.dynamic_update_slice(lax.empty((N_DEV, *like.shape), like.dtype), own[None], (me[0], 0, 0))

    (w_in_blocks,) = _all_gather("w_in_all_gather", [w_in.astype(BF16)])
    w_in_full = w_in_blocks.transpose(1, 0, 2).reshape(w_in.shape[0], -1)
    late = [n for n in _BIG if n != "w_in"]
    shards = [w[n].astype(BF16) for n in late]
    w_in_blocks, shards = lax.optimization_barrier((w_in_blocks, shards))
    weights_handle, weights_token = _exchange_start(
        "weights_send", shards, [landing(s, s) for s in shards], per_peer=False)

    def late_weights(after):
        got = dict(zip(late, _exchange_wait("weights_arrive", weights_handle, after)))
        return (got["w_glu"].reshape(-1, w_glu.shape[1]), got["w_out"].reshape(-1, w_out.shape[1]),
                got["w_mlp_in"].transpose(1, 0, 2).reshape(w_mlp_in.shape[0], -1),
                got["w_mlp_out"].reshape(-1, w_mlp_out.shape[1]))

    mlp = ["w_mlp_out", "w_mlp_in"]
    sent = {}

    def send_grads(name, names, own, own_b):
        blocks = lambda g, n: g.reshape(N_DEV, *w[n].shape)
        sent[name + "_own"] = [blocks(g, n) for g, n in zip(own, names)]
        srcs = [blocks(g, n) for g, n in zip(own_b, names)]
        sent[name], token = _exchange_start(name, srcs, [landing(like=s[0]) for s in srcs], per_peer=True)
        return token[0, 0]

    def mlp_grads_ready(g_out, g_out_b, g_in, g_in_b):
        return send_grads("mlp_grads_send", mlp, [g_out, g_in], [g_out_b, g_in_b])

    rest = ["w_in", "w_glu", "w_out"]

    def rest_grads_ready(own, own_b):
        return send_grads("rest_grads_send", rest, own, own_b)

    loss_local, grad_x, g_small, g_big = _local_step(
        x, loss_target, {n: w[n] for n in _SMALL}, w_in_full, late_weights, mlp_grads_ready, rest_grads_ready,
        weights_token[0, 0])

    grads, delta, new_m, new_v = {}, {}, {}, {}
    small = _pack(g_small, last=loss_local)
    small_handle, small_token = _exchange_start("small_grads_send", [small], [landing(small, small)], per_peer=False)

    for send, arrive, names in (("mlp_grads_send", "mlp_grads_arrive", mlp),
                                ("rest_grads_send", "rest_grads_arrive", rest)):
        for n, own, part in zip(names, sent[send + "_own"], _exchange_wait(arrive, sent[send], small_token)):
            grads[n], delta[n], new_m[n], new_v[n] = _adamw_gathered("adamw_" + n, own, part, me, w[n], m[n], v[n])

    shards_done = lax.optimization_barrier(tuple(new_v[n] for n in _BIG))
    (small_parts,) = _exchange_wait("small_grads_arrive", small_handle, shards_done[-1])
    packed = _adamw_small("adamw_small", small_parts, _pack(w), _pack(m), _pack(v))
    for tree, buf in zip((grads, delta, new_m, new_v), packed):
        tree.update(_unpack(buf, w))
    loss = packed[0][-1, -1]

    return (loss, grad_x, *[grads[n] for n in _WEIGHTS], *[delta[n] for n in _WEIGHTS],
            *[new_m[n] for n in _WEIGHTS], *[new_v[n] for n in _WEIGHTS])
```

```python
import functools
import math

import jax
import jax.numpy as jnp
from jax import lax
from jax.experimental import pallas as pl
from jax.experimental.pallas import tpu as pltpu

F32 = jnp.float32
BF16 = jnp.bfloat16

EPS = 1e-6
HEAD_DIM = 64
N_HEADS = 8
SB_WIDTH = 512
SSM_WIDTH = 512
SSM_GROUP = 16
SSM_GROUPS = 32
SSM_STATE = 64
QBLOCK = 128
KBLOCK = 256
N_CHUNK = 8
SSM_COLS = 4
LANES = 128
N_DEV = 8

ADAM_LR = 0.001
ADAM_B1 = 0.9
ADAM_B2 = 0.999
ADAM_EPS = 1e-08
ADAM_WD = 0.01
ADAM_STEP = 10

VMEM_LIMIT = 56 * 1024 * 1024

_NT = (((1,), (1,)), ((), ()))
_NN = (((1,), (0,)), ((), ()))
_TN = (((0,), (0,)), ((), ()))


def _dot(a, b, dims=_NN):
    return lax.dot_general(a, b, dims, preferred_element_type=F32)


def _params(sem):
    return pltpu.CompilerParams(dimension_semantics=sem, vmem_limit_bytes=VMEM_LIMIT)


def _matmul(name, a, b, *, ta=False, tb=False, extras=(), epilogue=None, out_dtypes=(F32,), sums=(),
            prologue=None, col_blocked=False, tm=1024, tn=512, tk=4096):
    M, K = (a.shape[1], a.shape[0]) if ta else a.shape
    N = b.shape[0] if tb else b.shape[1]
    tm, tn, tk = min(tm, M), min(tn, N), min(tk, K)
    assert M % tm == 0 and N % tn == 0 and K % tk == 0, (name, M, N, K)
    assert not (sums or prologue) or (tn == N and tk == K), name
    assert not prologue or not (ta or col_blocked), name
    nk = K // tk
    n_ex, n_out, n_sum = len(extras), len(out_dtypes), len(sums)
    n_pro = 1 if prologue else 0
    dims = (((0 if ta else 1,), (1 if tb else 0,)), ((), ()))

    def body(*refs):
        a_ref, b_ref = refs[0], refs[1]
        ex_refs = refs[2 + n_pro:2 + n_pro + n_ex]
        o_refs = refs[2 + n_pro + n_ex:2 + n_pro + n_ex + n_out]
        s_refs = refs[2 + n_pro + n_ex + n_out:2 + n_pro + n_ex + n_out + n_sum]
        k = pl.program_id(2)
        if prologue:
            left = prologue[0](a_ref[...], refs[2][...]).astype(BF16)
            refs[2 + n_pro + n_ex + n_out + n_sum][...] = left
        else:
            left = a_ref[...].astype(BF16)
        part = _dot(left, b_ref[...].astype(BF16), dims)

        def finish(acc):
            outs = (acc,) if epilogue is None else epilogue(acc, *[e[...] for e in ex_refs])
            for o_ref, o in zip(o_refs, outs[:n_out]):
                o_ref[...] = o.astype(o_ref.dtype)
            if n_sum:
                @pl.when(pl.program_id(0) == 0)
                def _():
                    for s_ref in s_refs:
                        s_ref[...] = jnp.zeros_like(s_ref)

                for s_ref, v in zip(s_refs, outs[n_out:]):
                    s_ref[...] += v

        if nk == 1:
            finish(part)
        else:
            acc_ref = refs[-1]

            @pl.when(k == 0)
            def _():
                acc_ref[...] = part

            @pl.when(jnp.logical_and(k > 0, k < nk - 1))
            def _():
                acc_ref[...] += part

            @pl.when(k == nk - 1)
            def _():
                finish(acc_ref[...] + part)

    a_spec = pl.BlockSpec((tk, tm), lambda i, j, k: (k, i)) if ta else pl.BlockSpec((tm, tk), lambda i, j, k: (i, k))
    b_spec = pl.BlockSpec((tn, tk), lambda i, j, k: (j, k)) if tb else pl.BlockSpec((tk, tn), lambda i, j, k: (k, j))
    ex_specs = [pl.BlockSpec((1, tn), lambda i, j, k: (0, j)) if e.shape[0] == 1 else
                pl.BlockSpec((tm, tn), lambda i, j, k: (i, j)) for e in extras]
    if col_blocked:
        out_specs = [pl.BlockSpec((None, tm, tn), lambda i, j, k: (j, i, 0)) for _ in out_dtypes]
        out_shape = [jax.ShapeDtypeStruct((N // tn, M, tn), dt) for dt in out_dtypes]
    else:
        wide = [(dt, N) if not isinstance(dt, tuple) else dt for dt in out_dtypes]
        assert all(w == N for _, w in wide) or tn == N, name
        out_specs = [pl.BlockSpec((tm, tn if w == N else w), lambda i, j, k: (i, j)) for _, w in wide]
        out_shape = [jax.ShapeDtypeStruct((M, w), dt) for dt, w in wide]
    out_specs += [pl.BlockSpec(s, lambda i, j, k: (0, 0)) for s in sums]
    out_shape += [jax.ShapeDtypeStruct(s, F32) for s in sums]
    pro_specs, pro_args = [], []
    if prologue:
        pro_specs, pro_args = [pl.BlockSpec((1, tk), lambda i, j, k: (0, 0))], [prologue[1]]
        out_specs.append(pl.BlockSpec((tm, tk), lambda i, j, k: (i, 0)))
        out_shape.append(jax.ShapeDtypeStruct((M, K), BF16))
    outs = pl.pallas_call(
        body, name=name, grid=(M // tm, N // tn, nk),
        in_specs=[a_spec, b_spec, *pro_specs, *ex_specs], out_specs=out_specs, out_shape=out_shape,
        scratch_shapes=[pltpu.VMEM((tm, tn), F32)] if nk > 1 else [],
        compiler_params=_params(("arbitrary",) * 3 if sums else ("parallel", "parallel", "arbitrary")),
    )(a, b, *pro_args, *extras)
    return outs[0] if len(outs) == 1 else outs


def _rowwise(name, fn, rows, small, outs, sums=(), tile=512):
    specs, args = [], []
    T = None
    for r in rows:
        arr, cb, w = r if isinstance(r, tuple) else (r, 0, r.shape[1])
        T = arr.shape[0]
        specs.append((w, cb))
        args.append(arr)
    tile = min(tile, T)
    assert T % tile == 0
    n_r, n_s, n_o, n_a = len(rows), len(small), len(outs), len(sums)

    def body(*refs):
        r_refs = refs[:n_r]
        s_refs = refs[n_r:n_r + n_s]
        o_refs = refs[n_r + n_s:n_r + n_s + n_o]
        a_refs = refs[n_r + n_s + n_o:]
        res = fn(*[r[...] for r in r_refs], *[s[...] for s in s_refs])
        res = res if isinstance(res, (tuple, list)) else (res,)
        for o_ref, o in zip(o_refs, res[:n_o]):
            o_ref[...] = o.astype(o_ref.dtype)

        @pl.when(pl.program_id(0) == 0)
        def _():
            for a_ref in a_refs:
                a_ref[...] = jnp.zeros_like(a_ref)

        for a_ref, v in zip(a_refs, res[n_o:]):
            a_ref[...] += v.astype(F32)

    in_specs = [pl.BlockSpec((tile, w), functools.partial(lambda i, cb: (i, cb), cb=cb)) for w, cb in specs]
    in_specs += [pl.BlockSpec(s.shape, functools.partial(lambda i, nd: (0,) * nd, nd=s.ndim)) for s in small]
    out_specs = [pl.BlockSpec((tile, w), lambda i: (i, 0)) for w, _ in outs]
    out_specs += [pl.BlockSpec(s, functools.partial(lambda i, nd: (0,) * nd, nd=len(s))) for s in sums]
    out_shape = [jax.ShapeDtypeStruct((T, w), dt) for w, dt in outs]
    out_shape += [jax.ShapeDtypeStruct(s, F32) for s in sums]
    res = pl.pallas_call(
        body, name=name, grid=(T // tile,), in_specs=in_specs, out_specs=out_specs, out_shape=out_shape,
        compiler_params=_params(("arbitrary",)),
    )(*args, *small)
    return res[0] if len(res) == 1 else res


def _rms(x, g):
    return x * lax.rsqrt(jnp.mean(x * x, axis=-1, keepdims=True) + EPS) * g


def _glu_branch(y, pre, b_glu, g_out):
    g = jax.nn.gelu(y)
    return _rms(g * jax.nn.sigmoid(pre + b_glu), g_out)


def _split_dot(x, tri_bf):
    hi = x.astype(BF16)
    lo = (x - hi.astype(F32)).astype(BF16)
    return _dot(hi, tri_bf) + _dot(lo, tri_bf)


def _softplus(z):
    return jnp.maximum(z, 0.0) + jnp.log(1.0 + jnp.exp(-jnp.abs(z)))


def _head(h):
    return slice(h * HEAD_DIM, (h + 1) * HEAD_DIM)


def _head_mean(x, seg):
    return _split_dot(x, seg) * (1.0 / HEAD_DIM)


def _qk_norm(proj, gains):
    r = lax.div(lax.broadcasted_iota(jnp.int32, (SB_WIDTH, SB_WIDTH), 0), HEAD_DIM)
    c = lax.div(lax.broadcasted_iota(jnp.int32, (SB_WIDTH, SB_WIDTH), 1), HEAD_DIM)
    seg = (r == c).astype(BF16)
    q, k, v = (proj[:, i * SB_WIDTH:(i + 1) * SB_WIDTH] for i in range(3))
    qn = q * lax.rsqrt(_head_mean(q * q, seg) + EPS) * gains[:, 0:SB_WIDTH]
    kn = k * lax.rsqrt(_head_mean(k * k, seg) + EPS) * gains[:, SB_WIDTH:2 * SB_WIDTH]
    return proj, qn, kn, v


def _qk_norm_bwd(x, g, dy, seg):
    r = lax.rsqrt(_head_mean(x * x, seg) + EPS)
    gdy = g * dy
    dx = r * gdy - x * (r * r * r) * _head_mean(gdy * x, seg)
    return dx, jnp.sum(dy * x * r, axis=0, keepdims=True)


def _split_heads(refs, scratch, L):
    def chunk(i, _):
        r = pl.ds(pl.multiple_of(i * QBLOCK, QBLOCK), QBLOCK)
        for ref, s in zip(refs, scratch):
            for h in range(2):
                s[h, r, :] = ref[r, _head(h)]
        return 0

    lax.fori_loop(0, L // QBLOCK, chunk, 0)


Q_HALVES = KBLOCK // QBLOCK
_CHAINS = [(h, r) for h in range(2) for r in range(Q_HALVES)]


def _valid(i, kb):
    row = lax.broadcasted_iota(jnp.int32, (QBLOCK, KBLOCK), 0)
    col = lax.broadcasted_iota(jnp.int32, (QBLOCK, KBLOCK), 1)
    return col + (kb * KBLOCK - i * QBLOCK) < row


def _attn_fwd(qn, kn, vb, B, L):
    n_pairs = L // KBLOCK
    n_hp = N_HEADS // 2
    nc = len(_CHAINS)

    def body(q_ref, k_ref, v_ref, o_ref, a_ref, q_s, k_s, v_s, after_s, z_s, stage_s, sems):
        _split_heads((q_ref, k_ref, v_ref), (q_s, k_s, v_s), L)
        r2 = lax.broadcasted_iota(jnp.int32, (KBLOCK, KBLOCK), 0)
        c2 = lax.broadcasted_iota(jnp.int32, (KBLOCK, KBLOCK), 1)
        after_s[...] = (r2 > c2).astype(after_s.dtype)
        g = pl.program_id(0) * n_hp + pl.program_id(1)

        def q_pair(p, _):
            rows = [pl.ds(pl.multiple_of((p * Q_HALVES + r) * QBLOCK, QBLOCK), QBLOCK) for r in range(Q_HALVES)]
            q_c = [q_s[h, rows[r], :] for h, r in _CHAINS]
            cs = range(nc)

            def scores(kb):
                rk = pl.ds(pl.multiple_of(kb * KBLOCK, KBLOCK), KBLOCK)
                return [_dot(q_c[c], k_s[_CHAINS[c][0], rk, :], _NT) for c in cs]

            def saved(kb):
                return pltpu.make_async_copy(stage_s.at[kb & 1], a_ref.at[g, p, kb], sems.at[kb & 1])

            def k_block(kb, carry, diagonal):
                rk = pl.ds(pl.multiple_of(kb * KBLOCK, KBLOCK), KBLOCK)
                if diagonal:
                    valid = [_valid(p * Q_HALVES + r, kb) for r in range(Q_HALVES)]
                    keep = lambda c, t: jnp.where(valid[_CHAINS[c][1]], t, 0.0)
                    z = scores(kb)
                else:
                    keep = lambda c, t: t
                    z = [z_s[(kb + 1) & 1, c] for c in cs]
                ahead = scores(jnp.maximum(kb - 1, 0))
                for c in cs:
                    z_s[kb & 1, c] = ahead[c]
                sp = [_softplus(z[c]) for c in cs]
                lsig = [z[c] - sp[c] for c in cs]
                lom = [keep(c, -sp[c]) for c in cs]
                tail = [_split_dot(lom[c], after_s[...]) + carry[c][0] for c in cs]
                a = [keep(c, jnp.exp(lsig[c] + tail[c])).astype(v_s.dtype) for c in cs]
                acc = [carry[c][1] + _dot(a[c], v_s[_CHAINS[c][0], rk, :]) for c in cs]
                for c in cs:
                    stage_s[kb & 1, c] = a[c]
                saved(kb).start()
                return tuple((carry[c][0] + jnp.sum(lom[c], axis=1, keepdims=True), acc[c]) for c in cs)

            def next_block(n, carry):
                kb = p - n

                @pl.when(n >= 2)
                def _():
                    saved(kb + 2).wait()

                return k_block(kb, carry, False)

            init = (jnp.zeros((QBLOCK, 1), F32), jnp.zeros((QBLOCK, HEAD_DIM), F32))
            first = k_block(p, (init,) * nc, True)
            res = lax.fori_loop(1, p + 1, next_block, first)
            saved(0).wait()

            @pl.when(p >= 1)
            def _():
                saved(1).wait()

            for r in range(Q_HALVES):
                o_ref[rows[r], :] = jnp.concatenate([res[c][1] for c in cs if _CHAINS[c][1] == r], axis=1)
            return 0

        lax.fori_loop(0, n_pairs, q_pair, 0)

    spec = pl.BlockSpec((L, LANES), lambda b, p: (b, p))
    return pl.pallas_call(
        body, name="attn_fwd", grid=(B, n_hp),
        in_specs=[spec] * 3, out_specs=[spec, _ANY],
        out_shape=[jax.ShapeDtypeStruct((B * L, SB_WIDTH), F32),
                   jax.ShapeDtypeStruct((B * n_hp, n_pairs, n_pairs, nc, QBLOCK, KBLOCK), BF16)],
        scratch_shapes=[pltpu.VMEM((2, L, HEAD_DIM), BF16)] * 3 + [pltpu.VMEM((KBLOCK, KBLOCK), BF16)]
        + [pltpu.VMEM((2, nc, QBLOCK, KBLOCK), F32), pltpu.VMEM((2, nc, QBLOCK, KBLOCK), BF16),
           pltpu.SemaphoreType.DMA((2,))],
        compiler_params=_params(("parallel", "parallel")),
    )(qn, kn, vb)


def _attn_bwd(qn, kn, vb, kept_a, d_sb, proj, gq2, gk2, B, L):
    n_pairs = L // KBLOCK
    n_hp = N_HEADS // 2
    nc = len(_CHAINS)
    slots = 3
    scale = 1.0 / math.sqrt(HEAD_DIM)

    def body(q_ref, k_ref, v_ref, do_ref, qraw_ref, kraw_ref, gq_ref, gk_ref, a_ref,
             dq_ref, dk_ref, dv_ref, dgq_ref, dgk_ref,
             q_s, k_s, v_s, qt_s, dkt_s, dvt_s, before_s, stage_s, sems):
        _split_heads((q_ref, k_ref, v_ref), (q_s, k_s, v_s), L)
        g = pl.program_id(0) * n_hp + pl.program_id(1)
        lane_head = [lax.div(lax.broadcasted_iota(jnp.int32, (LANES, LANES), d), HEAD_DIM) for d in (0, 1)]
        seg = (lane_head[0] == lane_head[1]).astype(BF16)

        def transpose_q(i, _):
            r = pl.ds(pl.multiple_of(i * QBLOCK, QBLOCK), QBLOCK)
            qt_s[:, r] = q_ref[r, :].astype(F32).T.astype(qt_s.dtype)
            return 0

        lax.fori_loop(0, L // QBLOCK, transpose_q, 0)
        dkt_s[...] = jnp.zeros_like(dkt_s)
        dvt_s[...] = jnp.zeros_like(dvt_s)
        r2 = lax.broadcasted_iota(jnp.int32, (KBLOCK, KBLOCK), 0)
        c2 = lax.broadcasted_iota(jnp.int32, (KBLOCK, KBLOCK), 1)
        before_s[...] = (r2 < c2).astype(before_s.dtype)

        def q_pair(p, dgq):
            rows = [pl.ds(pl.multiple_of((p * Q_HALVES + r) * QBLOCK, QBLOCK), QBLOCK) for r in range(Q_HALVES)]
            pair = pl.ds(pl.multiple_of(p * KBLOCK, KBLOCK), KBLOCK)
            do2 = do_ref[pair, :]
            do_t = do2.T.astype(v_s.dtype)
            cs = range(nc)
            hs = range(2)
            q_c = [q_s[h, rows[r], :] for h, r in _CHAINS]
            do_c = [do2[r * QBLOCK:(r + 1) * QBLOCK, _head(h)].astype(v_s.dtype) for h, r in _CHAINS]
            qt_h = [qt_s[_head(h), pair] for h in hs]
            dot_h = [do_t[_head(h), :] for h in hs]

            def kept(kb):
                slot = lax.rem(kb, slots)
                return pltpu.make_async_copy(a_ref.at[g, p, kb], stage_s.at[slot], sems.at[slot])

            def k_block(kb, carry, diagonal):
                rk = pl.ds(pl.multiple_of(kb * KBLOCK, KBLOCK), KBLOCK)
                if diagonal:
                    valid = [_valid(p * Q_HALVES + r, kb) for r in range(Q_HALVES)]
                    keep = lambda c, t: jnp.where(valid[_CHAINS[c][1]], t, 0.0)
                else:
                    keep = lambda c, t: t

                    @pl.when(kb + 2 <= p)
                    def _():
                        kept(kb + 2).start()

                kept(kb).wait()
                slot = lax.rem(kb, slots)
                k_b = [k_s[h, rk, :] for h in hs]
                z = [_dot(q_c[c], k_b[_CHAINS[c][0]], _NT) for c in cs]
                da = [_dot(do_c[c], v_s[_CHAINS[c][0], rk, :], _NT) for c in cs]
                a = [stage_s[slot, c] for c in cs]
                dla = [a[c].astype(F32) * da[c] for c in cs]
                for h in hs:
                    a_h = jnp.concatenate([a[c] for c in cs if _CHAINS[c][0] == h], axis=0)
                    dvt_s[_head(h), rk] += _dot(dot_h[h], a_h)
                d_lom = [carry[c][0] + _dot(dla[c].astype(BF16), before_s[...]) for c in cs]
                beta = [jax.nn.sigmoid(z[c]) for c in cs]
                dz_b = [(dla[c] * (1.0 - beta[c]) - keep(c, beta[c] * d_lom[c])).astype(v_s.dtype) for c in cs]
                dq_acc = [carry[c][1] + _dot(dz_b[c], k_b[_CHAINS[c][0]]) for c in cs]
                for h in hs:
                    dz_h = jnp.concatenate([dz_b[c] for c in cs if _CHAINS[c][0] == h], axis=0)
                    dkt_s[_head(h), rk] += _dot(qt_h[h], dz_h)
                return tuple((carry[c][0] + jnp.sum(dla[c], axis=1, keepdims=True), dq_acc[c]) for c in cs)

            init = (jnp.zeros((QBLOCK, 1), F32), jnp.zeros((QBLOCK, HEAD_DIM), F32))
            kept(0).start()

            @pl.when(p >= 1)
            def _():
                kept(1).start()

            before = lax.fori_loop(0, p, lambda kb, carry: k_block(kb, carry, False), (init,) * nc)
            res = k_block(p, before, True)
            for r in range(Q_HALVES):
                d_qn = jnp.concatenate([res[c][1] for c in cs if _CHAINS[c][1] == r], axis=1) * scale
                dq, dg = _qk_norm_bwd(qraw_ref[rows[r], :], gq_ref[...], d_qn, seg)
                dq_ref[rows[r], :] = dq.astype(dq_ref.dtype)
                dgq = dgq + dg
            return dgq

        dgq = lax.fori_loop(0, n_pairs, q_pair, jnp.zeros((1, LANES), F32))

        def transpose_out(i, dgk):
            r = pl.ds(pl.multiple_of(i * QBLOCK, QBLOCK), QBLOCK)
            dk, dg = _qk_norm_bwd(kraw_ref[r, :], gk_ref[...], dkt_s[:, r].T, seg)
            dk_ref[r, :] = dk.astype(dk_ref.dtype)
            dv_ref[r, :] = dvt_s[:, r].T.astype(dv_ref.dtype)
            return dgk + dg

        dgk = lax.fori_loop(0, L // QBLOCK, transpose_out, jnp.zeros((1, LANES), F32))

        @pl.when(jnp.logical_and(pl.program_id(0) == 0, pl.program_id(1) == 0))
        def _():
            dgq_ref[...] = jnp.zeros_like(dgq_ref)
            dgk_ref[...] = jnp.zeros_like(dgk_ref)

        dgq_ref[...] += dgq[:, _head(0)] + dgq[:, _head(1)]
        dgk_ref[...] += dgk[:, _head(0)] + dgk[:, _head(1)]

    spec = pl.BlockSpec((L, LANES), lambda b, p: (b, p))
    gain = pl.BlockSpec((1, LANES), lambda b, p: (0, 0))
    gain_grad = pl.BlockSpec((1, HEAD_DIM), lambda b, p: (0, 0))
    return pl.pallas_call(
        body, name="attn_bwd", grid=(B, n_hp),
        in_specs=[spec] * 4 + [spec, pl.BlockSpec((L, LANES), lambda b, p: (b, n_hp + p)), gain, gain, _ANY],
        out_specs=[spec] * 3 + [gain_grad] * 2,
        out_shape=[jax.ShapeDtypeStruct((B * L, SB_WIDTH), BF16)] * 3 + [jax.ShapeDtypeStruct((1, HEAD_DIM), F32)] * 2,
        scratch_shapes=[pltpu.VMEM((2, L, HEAD_DIM), BF16)] * 3 + [pltpu.VMEM((LANES, L), BF16)]
        + [pltpu.VMEM((LANES, L), F32)] * 2 + [pltpu.VMEM((KBLOCK, KBLOCK), BF16)]
        + [pltpu.VMEM((slots, nc, QBLOCK, KBLOCK), BF16), pltpu.SemaphoreType.DMA((slots,))],
        compiler_params=_params(("arbitrary", "arbitrary")),
    )(qn, kn, vb, d_sb, proj, proj, gq2, gk2, kept_a)


def _ssm_discretise(lam_re, lam_im, log_dt, b_re, b_im):
    dt = jnp.exp(log_dt)
    mag = jnp.exp(lam_re * dt)
    lbr = mag * jnp.cos(lam_im * dt)
    lbi = mag * jnp.sin(lam_im * dt)
    den = lam_re * lam_re + lam_im * lam_im
    nr, ni = lbr - 1.0, lbi
    cr = (nr * lam_re + ni * lam_im) / den
    ci = (ni * lam_re - nr * lam_im) / den
    return lbr, lbi, cr * b_re - ci * b_im, cr * b_im + ci * b_re


def _ssm_prep(lam_re, lam_im, log_dt, b_re_t, b_im_t):
    def body(lr, li, ld, br, bi, o_lr, o_li, o_br, o_bi):
        res = _ssm_discretise(lr[...], li[...], ld[...], br[...], bi[...])
        for o, v in zip((o_lr, o_li, o_br, o_bi), res):
            o[...] = v

    return pl.pallas_call(
        body, name="ssm_prep",
        out_shape=[jax.ShapeDtypeStruct(lam_re.shape, F32)] * 2 + [jax.ShapeDtypeStruct(b_re_t.shape, F32)] * 2,
    )(lam_re, lam_im, log_dt, b_re_t, b_im_t)


def _ssm_prep_bwd(lam_re, lam_im, log_dt, b_re_t, b_im_t, d_lr, d_li, d_br, d_bi):
    def body(lr, li, ld, br, bi, g_lr, g_li, g_br, g_bi, o_lr, o_li, o_ld, o_br, o_bi):
        _, vjp = jax.vjp(_ssm_discretise, lr[...], li[...], ld[...], br[...], bi[...])
        res = vjp((g_lr[...], g_li[...], g_br[...], g_bi[...]))
        for o, v in zip((o_lr, o_li, o_ld, o_br, o_bi), res):
            o[...] = v

    return pl.pallas_call(
        body, name="ssm_prep_bwd",
        out_shape=[jax.ShapeDtypeStruct(lam_re.shape, F32)] * 2 + [jax.ShapeDtypeStruct(log_dt.shape, F32)]
        + [jax.ShapeDtypeStruct(b_re_t.shape, F32)] * 2,
    )(lam_re, lam_im, log_dt, b_re_t, b_im_t, d_lr, d_li, d_br, d_bi)


def _block_diag(m):
    m4 = m.reshape(SSM_COLS, 8, SSM_GROUP, SSM_STATE)
    return jnp.einsum("aghp,gk->aghkp", m4, jnp.eye(8, dtype=m.dtype)).reshape(SSM_COLS, LANES, 512)


def _block_diag_take(d):
    d6 = d.reshape(SSM_COLS, 8, SSM_GROUP, 2, 8, SSM_STATE)
    return jnp.einsum("aghrgp->raghp", d6).reshape(2, SSM_GROUPS, SSM_GROUP, SSM_STATE)


def _cmul(ar, ai, br, bi):
    return ar * br - ai * bi, ar * bi + ai * br


def _power(lr, li, n):
    assert n & (n - 1) == 0
    for _ in range(n.bit_length() - 1):
        lr, li = _cmul(lr, li, lr, li)
    return lr, li


def _ssm_fwd(u_p, w_b, lam_r, lam_i, c_m, d_skip, B, L, tj):
    J = L // N_CHUNK
    njt = J // tj
    R = tj * N_CHUNK
    H = 512

    def body(u_ref, wb_ref, lr_ref, li_ref, cm_ref, d_ref, y_ref, gel_ref, x_ref, xin_ref, bu_s, st_s, xin_s):
        ph, jt = pl.program_id(2), pl.program_id(3)
        lr, li = lr_ref[...], li_ref[...]

        @pl.when(jnp.logical_and(ph == 0, jt == 0))
        def _():
            st_s[...] = jnp.zeros_like(st_s)

        @pl.when(ph == 0)
        def _():
            bu_s[jt] = _dot(u_ref[...].astype(BF16), wb_ref[...].astype(BF16))

        def scan(store):
            def step(j, carry):
                xr, xi = carry
                r = pl.ds(pl.multiple_of(j * N_CHUNK, N_CHUNK), N_CHUNK)
                nr = lr * xr - li * xi + bu_s[jt, r, 0:H]
                ni = lr * xi + li * xr + bu_s[jt, r, H:2 * H]
                if store:
                    x_ref[r, 0:H] = nr
                    x_ref[r, H:2 * H] = ni
                return nr, ni

            xr, xi = lax.fori_loop(0, tj, step, (st_s[:, 0:H], st_s[:, H:2 * H]))
            st_s[:, 0:H] = xr
            st_s[:, H:2 * H] = xi

        @pl.when(ph == 0)
        def _():
            scan(False)

            @pl.when(jt == njt - 1)
            def _():
                pr, pi = _power(lr[0:1], li[0:1], J)
                xin_s[0:1, :] = jnp.zeros((1, 2 * H), F32)
                for c in range(1, N_CHUNK):
                    qr, qi = _cmul(pr, pi, xin_s[c - 1:c, 0:H], xin_s[c - 1:c, H:2 * H])
                    xin_s[c:c + 1, 0:H] = qr + st_s[c - 1:c, 0:H]
                    xin_s[c:c + 1, H:2 * H] = qi + st_s[c - 1:c, H:2 * H]
                xin_ref[...] = xin_s[...]
                st_s[...] = xin_s[...]

        @pl.when(ph == 1)
        def _():
            scan(True)
            y = _dot(x_ref[...].astype(BF16), cm_ref[...].astype(BF16)) + d_ref[...] * u_ref[...]
            y_ref[...] = y
            gel_ref[...] = jax.nn.gelu(y).astype(gel_ref.dtype)

    return pl.pallas_call(
        body, name="ssm_fwd", grid=(SSM_COLS, B, 2, njt),
        in_specs=[
            pl.BlockSpec((None, R, LANES), lambda i, b, ph, jt: (b, jt, i)),
            pl.BlockSpec((None, LANES, 2 * H), lambda i, b, ph, jt: (i, 0, 0)),
            pl.BlockSpec((None, N_CHUNK, H), lambda i, b, ph, jt: (i, 0, 0)),
            pl.BlockSpec((None, N_CHUNK, H), lambda i, b, ph, jt: (i, 0, 0)),
            pl.BlockSpec((None, 2 * H, LANES), lambda i, b, ph, jt: (i, 0, 0)),
            pl.BlockSpec((1, LANES), lambda i, b, ph, jt: (0, i)),
        ],
        out_specs=[
            pl.BlockSpec((None, R, LANES), lambda i, b, ph, jt: (b, jt * ph, i)),
            pl.BlockSpec((None, R, LANES), lambda i, b, ph, jt: (b, jt * ph, i)),
            pl.BlockSpec((None, R, 2 * H), lambda i, b, ph, jt: (b, jt * ph, i)),
            pl.BlockSpec((None, None, N_CHUNK, 2 * H), lambda i, b, ph, jt: (b, i, 0, 0)),
        ],
        out_shape=[
            jax.ShapeDtypeStruct((B, L, SSM_WIDTH), F32),
            jax.ShapeDtypeStruct((B, L, SSM_WIDTH), BF16),
            jax.ShapeDtypeStruct((B, L, SSM_COLS * 2 * H), F32),
            jax.ShapeDtypeStruct((B, SSM_COLS, N_CHUNK, 2 * H), F32),
        ],
        scratch_shapes=[pltpu.VMEM((njt, R, 2 * H), F32), pltpu.VMEM((N_CHUNK, 2 * H), F32),
                        pltpu.VMEM((N_CHUNK, 2 * H), F32)],
        compiler_params=_params(("arbitrary",) * 4),
    )(u_p, w_b, lam_r, lam_i, c_m, d_skip)


def _ssm_bwd(dy_p, u_p, x, xin, w_bt, lam_r, lam_i, c_mt, d_skip, B, L, tj):
    J = L // N_CHUNK
    njt = J // tj
    R = tj * N_CHUNK
    H = 512
    x4 = x.reshape(B, J, N_CHUNK, SSM_COLS * 2 * H)

    def body(dy_ref, u_ref, x_ref, xp_ref, xin_ref, wbt_ref, lr_ref, li_ref, cmt_ref, d_ref,
             du_ref, dwb_ref, dcm_ref, dlr_ref, dli_ref, dd_ref, ca_s, a_s, st_s, dl_s):
        b, ph, jt = pl.program_id(1), pl.program_id(2), pl.program_id(3)
        jr = njt - 1 - jt
        lr, li = lr_ref[...], -li_ref[...]

        @pl.when(jnp.logical_and(b == 0, jnp.logical_and(ph == 0, jt == 0)))
        def _():
            dwb_ref[...] = jnp.zeros_like(dwb_ref)
            dcm_ref[...] = jnp.zeros_like(dcm_ref)
            dlr_ref[...] = jnp.zeros_like(dlr_ref)
            dli_ref[...] = jnp.zeros_like(dli_ref)
            dd_ref[...] = jnp.zeros_like(dd_ref)
            dl_s[...] = jnp.zeros_like(dl_s)

        @pl.when(jnp.logical_and(ph == 0, jt == 0))
        def _():
            st_s[...] = jnp.zeros_like(st_s)

        @pl.when(ph == 0)
        def _():
            ca_s[jt] = _dot(dy_ref[...].astype(BF16), cmt_ref[...].astype(BF16))

        def scan(store):
            def step(n, carry):
                ar, ai = carry
                r = pl.ds(pl.multiple_of((tj - 1 - n) * N_CHUNK, N_CHUNK), N_CHUNK)
                nr = lr * ar - li * ai + ca_s[jt, r, 0:H]
                ni = lr * ai + li * ar + ca_s[jt, r, H:2 * H]
                if store:
                    a_s[r, 0:H] = nr
                    a_s[r, H:2 * H] = ni
                return nr, ni

            ar, ai = lax.fori_loop(0, tj, step, (st_s[:, 0:H], st_s[:, H:2 * H]))
            st_s[:, 0:H] = ar
            st_s[:, H:2 * H] = ai

        @pl.when(ph == 0)
        def _():
            scan(False)

            @pl.when(jt == njt - 1)
            def _():
                pr, pi = _power(lr[0:1], li[0:1], J)
                a_s[N_CHUNK - 1:N_CHUNK, :] = jnp.zeros((1, 2 * H), F32)
                for c in range(N_CHUNK - 2, -1, -1):
                    qr, qi = _cmul(pr, pi, a_s[c + 1:c + 2, 0:H], a_s[c + 1:c + 2, H:2 * H])
                    a_s[c:c + 1, 0:H] = qr + st_s[c + 1:c + 2, 0:H]
                    a_s[c:c + 1, H:2 * H] = qi + st_s[c + 1:c + 2, H:2 * H]
                st_s[...] = a_s[0:N_CHUNK, :]

        @pl.when(ph == 1)
        def _():
            scan(True)
            dy = dy_ref[...]
            u = u_ref[...]
            a_b = a_s[...].astype(BF16)
            du_ref[...] = (_dot(a_b, wbt_ref[...].astype(BF16)) + d_ref[...] * dy).astype(du_ref.dtype)
            dwb_ref[...] += _dot(u.astype(BF16), a_b, _TN)
            dcm_ref[...] += _dot(x_ref[...].astype(BF16), dy.astype(BF16), _TN)
            dd_ref[...] += jnp.sum(dy * u, axis=0, keepdims=True)

            first = jnp.where(jr == 0, xin_ref[...], xp_ref[...])
            a0r, a0i = a_s[0:N_CHUNK, 0:H], a_s[0:N_CHUNK, H:2 * H]
            acc0 = (a0r * first[:, 0:H] + a0i * first[:, H:2 * H], a0i * first[:, 0:H] - a0r * first[:, H:2 * H])

            def step(j, carry):
                sr, si = carry
                r = pl.ds(pl.multiple_of(j * N_CHUNK, N_CHUNK), N_CHUNK)
                rp = pl.ds(pl.multiple_of((j - 1) * N_CHUNK, N_CHUNK), N_CHUNK)
                ar, ai = a_s[r, 0:H], a_s[r, H:2 * H]
                xr, xi = x_ref[rp, 0:H], x_ref[rp, H:2 * H]
                return sr + ar * xr + ai * xi, si + ai * xr - ar * xi

            sr, si = lax.fori_loop(1, tj, step, acc0)
            dl_s[:, 0:H] += sr
            dl_s[:, H:2 * H] += si

            @pl.when(jnp.logical_and(b == B - 1, jt == njt - 1))
            def _():
                dlr_ref[...] = jnp.sum(dl_s[:, 0:H], axis=0, keepdims=True)
                dli_ref[...] = jnp.sum(dl_s[:, H:2 * H], axis=0, keepdims=True)
                dl_s[...] = jnp.zeros_like(dl_s)

    rev = lambda ph, jt: (njt - 1 - jt) * ph + (njt - 1) * (1 - ph)
    return pl.pallas_call(
        body, name="ssm_bwd", grid=(SSM_COLS, B, 2, njt),
        in_specs=[
            pl.BlockSpec((None, R, LANES), lambda i, b, ph, jt: (b, njt - 1 - jt, i)),
            pl.BlockSpec((None, R, LANES), lambda i, b, ph, jt: (b, njt - 1 - jt, i)),
            pl.BlockSpec((None, R, 2 * H), lambda i, b, ph, jt: (b, rev(ph, jt), i)),
            pl.BlockSpec((None, None, N_CHUNK, 2 * H),
                         lambda i, b, ph, jt: (b, jnp.maximum((njt - 1 - jt) * tj - 1, 0), 0, i)),
            pl.BlockSpec((None, None, N_CHUNK, 2 * H), lambda i, b, ph, jt: (b, i, 0, 0)),
            pl.BlockSpec((None, 2 * H, LANES), lambda i, b, ph, jt: (i, 0, 0)),
            pl.BlockSpec((None, N_CHUNK, H), lambda i, b, ph, jt: (i, 0, 0)),
            pl.BlockSpec((None, N_CHUNK, H), lambda i, b, ph, jt: (i, 0, 0)),
            pl.BlockSpec((None, LANES, 2 * H), lambda i, b, ph, jt: (i, 0, 0)),
            pl.BlockSpec((1, LANES), lambda i, b, ph, jt: (0, i)),
        ],
        out_specs=[
            pl.BlockSpec((None, R, LANES), lambda i, b, ph, jt: (b, rev(ph, jt), i)),
            pl.BlockSpec((None, LANES, 2 * H), lambda i, b, ph, jt: (i, 0, 0)),
            pl.BlockSpec((None, 2 * H, LANES), lambda i, b, ph, jt: (i, 0, 0)),
            pl.BlockSpec((None, 1, H), lambda i, b, ph, jt: (i, 0, 0)),
            pl.BlockSpec((None, 1, H), lambda i, b, ph, jt: (i, 0, 0)),
            pl.BlockSpec((1, LANES), lambda i, b, ph, jt: (0, i)),
        ],
        out_shape=[
            jax.ShapeDtypeStruct((B, L, SSM_WIDTH), BF16),
            jax.ShapeDtypeStruct((SSM_COLS, LANES, 2 * H), F32),
            jax.ShapeDtypeStruct((SSM_COLS, 2 * H, LANES), F32),
            jax.ShapeDtypeStruct((SSM_COLS, 1, H), F32),
            jax.ShapeDtypeStruct((SSM_COLS, 1, H), F32),
            jax.ShapeDtypeStruct((1, SSM_WIDTH), F32),
        ],
        scratch_shapes=[pltpu.VMEM((njt, R, 2 * H), F32), pltpu.VMEM((R, 2 * H), F32),
                        pltpu.VMEM((N_CHUNK, 2 * H), F32), pltpu.VMEM((N_CHUNK, 2 * H), F32)],
        compiler_params=_params(("arbitrary",) * 4),
    )(dy_p, u_p, x, x4, xin, w_bt, lam_r, lam_i, c_mt, d_skip)


def _to_scan_layout(t, B, L):
    C = t.shape[-1]
    return t.reshape(B, N_CHUNK, L // N_CHUNK, C).transpose(0, 2, 1, 3).reshape(B, L, C)


def _from_scan_layout(t, B, L):
    C = t.shape[-1]
    return t.reshape(B, L // N_CHUNK, N_CHUNK, C).transpose(0, 2, 1, 3).reshape(B * L, C)


def _local_step(x, target, p, w_in, late_weights, mlp_grads_ready=None, rest_grads_ready=None, order=None, *,
                ssm_tile=256):
    B, L, D = x.shape
    T = B * L
    x2 = x.reshape(T, D)
    row = lambda v: v.reshape(1, -1)
    g1, g2, ga, gs, b_glu = row(p["norm1_g"]), row(p["norm2_g"]), row(p["attn_out_g"]), row(p["ssm_out_g"]), row(p["b_glu"])
    gq8 = jnp.tile(row(p["q_norm_g"]), (1, N_HEADS))
    gk8 = jnp.tile(row(p["k_norm_g"]), (1, N_HEADS))

    G, P, Hh = SSM_GROUPS, SSM_STATE, SSM_GROUP
    lam_re3, lam_im3 = p["ssm_lambda_re"].reshape(G, 1, P), p["ssm_lambda_im"].reshape(G, 1, P)
    log_dt3 = p["ssm_log_dt"].reshape(G, 1, 1)
    b_re_t, b_im_t = p["ssm_b_re"].transpose(0, 2, 1), p["ssm_b_im"].transpose(0, 2, 1)
    lbr, lbi, bbr, bbi = _ssm_prep(lam_re3, lam_im3, log_dt3, b_re_t, b_im_t)
    w_b = jnp.concatenate([_block_diag(bbr), _block_diag(bbi)], axis=2)
    c_mt = jnp.concatenate([_block_diag(p["ssm_c_re"]), -_block_diag(p["ssm_c_im"])], axis=2)
    w_bt, c_m = w_b.transpose(0, 2, 1), c_mt.transpose(0, 2, 1)
    lam_r = jnp.broadcast_to(lbr.reshape(SSM_COLS, 1, 512), (SSM_COLS, N_CHUNK, 512))
    lam_i = jnp.broadcast_to(lbi.reshape(SSM_COLS, 1, 512), (SSM_COLS, N_CHUNK, 512))
    d_skip = p["ssm_d"].reshape(1, SSM_WIDTH)
    if callable(w_in):
        w_in, order = w_in(lax.optimization_barrier((w_b, c_mt, lam_r))[0])
    g1_first = g1 if order is None else g1 + order

    qk_gains = jnp.concatenate([gq8 * (1.0 / math.sqrt(HEAD_DIM)), gk8, jnp.ones((1, 2 * SB_WIDTH), F32)], axis=1)
    head = (BF16, SB_WIDTH)
    proj, qn, kn, vb, xn = _matmul("proj", x2, w_in, prologue=(_rms, g1_first), extras=[qk_gains],
                                   out_dtypes=(F32, head, head, head), epilogue=_qk_norm, tm=512, tn=w_in.shape[1])
    sb, attn_kept = _attn_fwd(qn, kn, vb, B, L)
    u_p = _to_scan_layout(proj[:, 3 * SB_WIDTH:], B, L)
    y_p, gel_p, xs, xin = _ssm_fwd(u_p, w_b, lam_r, lam_i, c_m, d_skip, B, L, ssm_tile)
    y2, gel = y_p.reshape(T, SSM_WIDTH), gel_p.reshape(T, SSM_WIDTH)
    gel, sb = lax.optimization_barrier((gel, sb))
    w_glu, w_out, w_mlp_in, w_mlp_out = late_weights(gel)
    pre, ssm_n = _matmul("glu_gate", gel, w_glu, extras=[y2, b_glu, gs], out_dtypes=(F32, BF16),
                         epilogue=lambda acc, y, b, g: (acc, _glu_branch(y, acc, b, g)))
    mixed = _rowwise(
        "attn_out_norm", lambda s, n, g: jnp.concatenate([_rms(s, g).astype(BF16), n.astype(BF16)], axis=1),
        [sb, _from_scan_layout(ssm_n, B, L)], [ga], [(D, BF16)])
    def residual_and_norm(acc, res, g):
        h = acc + res
        return h, _rms(h, g)

    h1, hn = _matmul("out_proj", mixed, w_out, extras=[x2, g2], out_dtypes=(F32, BF16), tn=D,
                     epilogue=residual_and_norm)
    act, a_pre = _matmul("mlp_in", hn, w_mlp_in, out_dtypes=(BF16, BF16), tn=2048,
                         epilogue=lambda acc: (jnp.square(jnp.maximum(acc, 0.0)), acc))

    def loss_fn(acc, h, t):
        diff = acc + h - t
        part = jnp.sum(jnp.sum(diff * diff, axis=0, keepdims=True), axis=1, keepdims=True)
        d = diff * (1.0 / D)
        return d, d, part * (0.5 / D)

    d_out, d_out_b, loss = _matmul("mlp_out", act, w_mlp_out, extras=[h1, target.reshape(T, D)],
                                   out_dtypes=(F32, BF16), sums=[(1, 1)], epilogue=loss_fn, tm=512, tn=D)

    d_apre = _matmul("mlp_out_dx", d_out_b, w_mlp_out, tb=True, extras=[a_pre], out_dtypes=(BF16,), tn=2048,
                     epilogue=lambda acc, ap: (acc * (2.0 * jnp.maximum(ap.astype(F32), 0.0)),))
    both = lambda acc: (acc, acc)
    g_w_mlp_out, g_w_mlp_out_b = _matmul("mlp_out_dw", act, d_out_b, ta=True, out_dtypes=(F32, BF16), epilogue=both)
    g_w_mlp_in, g_w_mlp_in_b = _matmul("mlp_in_dw", hn, d_apre, ta=True, col_blocked=True, out_dtypes=(F32, BF16),
                                       epilogue=both, tn=w_mlp_in.shape[1] // N_DEV)
    if mlp_grads_ready is not None:
        g2 = g2 + mlp_grads_ready(g_w_mlp_out, g_w_mlp_out_b, g_w_mlp_in, g_w_mlp_in_b)

    def norm_bwd_res(dy, h, res, g):
        _, vjp = jax.vjp(_rms, h, g)
        dh, dg = vjp(dy)
        return res + dh, dg

    def norm_bwd_res2(dy, h, res, g):
        d, dg = norm_bwd_res(dy, h, res, g)
        return d, d, dg

    d_h1, d_h1_b, g_norm2 = _matmul("mlp_in_dx", d_apre, w_mlp_in, tb=True, extras=[h1, d_out, g2],
                                    out_dtypes=(F32, BF16), sums=[(1, D)], epilogue=norm_bwd_res2, tm=512, tn=D)

    d_mixed = _matmul("out_proj_dx", d_h1_b, w_out, tb=True, tn=1024)
    g_w_out, g_w_out_b = _matmul("out_proj_dw", mixed, d_h1_b, ta=True, out_dtypes=(F32, BF16), epilogue=both)

    def norm_bwd(h, dy, g):
        _, vjp = jax.vjp(_rms, h, g)
        return vjp(dy)

    d_sb, g_attn_out = _rowwise("attn_out_norm_bwd", norm_bwd, [sb, (d_mixed, 0, SB_WIDTH)], [ga],
                                [(SB_WIDTH, F32)], sums=[(1, SB_WIDTH)])
    d_ssm_n = _to_scan_layout(d_mixed[:, SB_WIDTH:], B, L).reshape(T, SSM_WIDTH)

    def glu_bwd(y, pre_, dy, bg, g):
        _, vjp = jax.vjp(_glu_branch, y, pre_, bg, g)
        d_y, d_pre, d_bg, d_g = vjp(dy)
        return d_y, d_pre, d_bg, d_g

    d_y_direct, d_pre, g_b_glu, g_ssm_out = _rowwise(
        "glu_out_bwd", glu_bwd, [y2, pre, d_ssm_n], [b_glu, gs], [(SSM_WIDTH, F32), (SSM_WIDTH, BF16)],
        sums=[(1, SSM_WIDTH), (1, SSM_WIDTH)])
    g_w_glu, g_w_glu_b = _matmul("glu_gate_dw", gel, d_pre, ta=True, out_dtypes=(F32, BF16), epilogue=both)

    def gelu_bwd(dg, y, dy0):
        _, vjp = jax.vjp(jax.nn.gelu, y)
        return (dy0 + vjp(dg)[0],)

    d_y = _matmul("glu_gate_dx", d_pre, w_glu, tb=True, extras=[y2, d_y_direct], epilogue=gelu_bwd)

    du_p, d_wb, d_cm, d_lr, d_li, g_d = _ssm_bwd(
        d_y.reshape(B, L, SSM_WIDTH), u_p, xs, xin, w_bt, lam_r, lam_i, c_mt, d_skip, B, L, ssm_tile)
    d_bb = _block_diag_take(d_wb.reshape(SSM_COLS, LANES, 2, 512))
    d_c = _block_diag_take(d_cm.transpose(0, 2, 1).reshape(SSM_COLS, LANES, 2, 512))
    g_lam_re, g_lam_im, g_log_dt, g_b_re_t, g_b_im_t = _ssm_prep_bwd(
        lam_re3, lam_im3, log_dt3, b_re_t, b_im_t,
        d_lr.reshape(G, 1, P), d_li.reshape(G, 1, P), d_bb[0], d_bb[1])
    d_q, d_k, d_v, g_q, g_k = _attn_bwd(qn, kn, vb, attn_kept, d_sb, proj, gq8[:, :LANES], gk8[:, :LANES], B, L)

    d_proj = jnp.concatenate([d_q, d_k, d_v, _from_scan_layout(du_p, B, L)], axis=1)
    g_w_in, g_w_in_b = _matmul("proj_dw", xn, d_proj, ta=True, col_blocked=True, out_dtypes=(F32, BF16),
                               epilogue=both, tn=w_in.shape[1] // N_DEV)
    if rest_grads_ready is not None:
        g1 = g1 + rest_grads_ready([g_w_in, g_w_glu, g_w_out], [g_w_in_b, g_w_glu_b, g_w_out_b])
    grad_x, g_norm1 = _matmul("proj_dx", d_proj, w_in, tb=True, extras=[x2, d_h1, g1], sums=[(1, D)],
                              epilogue=norm_bwd_res, tm=512, tn=D)

    small = {
        "norm1_g": g_norm1.reshape(-1),
        "q_norm_g": g_q.reshape(-1),
        "k_norm_g": g_k.reshape(-1),
        "ssm_lambda_re": g_lam_re.reshape(G, P),
        "ssm_lambda_im": g_lam_im.reshape(G, P),
        "ssm_log_dt": g_log_dt.reshape(G),
        "ssm_b_re": g_b_re_t.transpose(0, 2, 1),
        "ssm_b_im": g_b_im_t.transpose(0, 2, 1),
        "ssm_c_re": d_c[0],
        "ssm_c_im": -d_c[1],
        "ssm_d": g_d.reshape(G, Hh),
        "b_glu": g_b_glu.reshape(-1),
        "attn_out_g": g_attn_out.reshape(-1),
        "ssm_out_g": g_ssm_out.reshape(-1),
        "norm2_g": g_norm2.reshape(-1),
    }
    big = {"w_in": g_w_in, "w_glu": g_w_glu, "w_out": g_w_out, "w_mlp_in": g_w_mlp_in, "w_mlp_out": g_w_mlp_out}
    return loss[0, 0], grad_x.reshape(B, L, D), small, big


_ANY = pl.BlockSpec(memory_space=pl.ANY)
_MESH = pl.DeviceIdType.MESH


def _all_gather(name, shards):
    n = len(shards)

    def body(*refs):
        in_refs, out_refs = refs[:n], refs[n:2 * n]
        send_sems, recv_sems, local_sems = refs[2 * n:]
        x, y, c = lax.axis_index("x"), lax.axis_index("y"), lax.axis_index("c")
        me, sibling = (x, y, c), (x, y, 1 - c)
        chips = [(1 - x, y), (x, 1 - y), (1 - x, 1 - y)]

        def copy(a, k, block, to, src=None):
            px, py, pc = block
            rows = out_refs[a].at[4 * px + 2 * py + pc]
            return pltpu.make_async_remote_copy(
                src_ref=rows if src is None else src, dst_ref=rows, send_sem=send_sems.at[a, k],
                recv_sem=recv_sems.at[a, k], device_id=to, device_id_type=_MESH)

        mine = [pltpu.make_async_copy(in_refs[a], out_refs[a].at[4 * x + 2 * y + c], local_sems.at[a]) for a in range(n)]
        first, passed = [], []
        for a in range(n):
            mine[a].start()
            first.append(copy(a, 0, me, sibling, src=in_refs[a]))
            first += [copy(a, 1 + j, me, (*chip, c), src=in_refs[a]) for j, chip in enumerate(chips)]
        for cp in first:
            cp.start()
        for j, chip in enumerate(chips):
            for a in range(n):
                copy(a, 1 + j, (*chip, c), me).wait_recv()
                fwd = copy(a, 4 + j, (*chip, c), sibling)
                fwd.start()
                passed.append(fwd)
        for a in range(n):
            copy(a, 0, sibling, me).wait_recv()
            for j, chip in enumerate(chips):
                copy(a, 4 + j, (*chip, 1 - c), me).wait_recv()
        for cp in first + passed:
            cp.wait_send()
        for cp in mine:
            cp.wait()

    return pl.pallas_call(
        body, name=name,
        in_specs=[_ANY] * n, out_specs=[_ANY] * n,
        out_shape=[jax.ShapeDtypeStruct((N_DEV, *s.shape), s.dtype) for s in shards],
        scratch_shapes=[pltpu.SemaphoreType.DMA((n, 7)), pltpu.SemaphoreType.DMA((n, 7)), pltpu.SemaphoreType.DMA((n,))],
    )(*shards)


_HBM = pl.BlockSpec(memory_space=pltpu.HBM)
_SEM = pl.BlockSpec(memory_space=pltpu.SEMAPHORE)
_EFFECT = pltpu.SideEffectType.DATAFLOW_SIDE_EFFECTING
_FLIPS = [(dx, dy, dc) for dx in (0, 1) for dy in (0, 1) for dc in (0, 1) if (dx, dy, dc) != (0, 0, 0)]


def _exchange_start(name, srcs, lands, per_peer):
    n = len(srcs)

    def body(*refs):
        src_refs, land_refs = refs[:n], refs[n:2 * n]
        send_sems, recv_sems = refs[2 * n:3 * n], refs[3 * n:4 * n]
        token = refs[-1]
        x, y, c = lax.axis_index("x"), lax.axis_index("y"), lax.axis_index("c")
        me = 4 * x + 2 * y + c
        for dx, dy, dc in _FLIPS:
            px, py, pc = (1 - x if dx else x), (1 - y if dy else y), (1 - c if dc else c)
            for a in range(n):
                pltpu.make_async_remote_copy(
                    src_ref=src_refs[a].at[4 * px + 2 * py + pc] if per_peer else src_refs[a],
                    dst_ref=land_refs[a].at[me], send_sem=send_sems[a], recv_sem=recv_sems[a],
                    device_id=(px, py, pc), device_id_type=_MESH).start()
        token[...] = jnp.zeros_like(token)

    hbm = lambda t: pltpu.with_memory_space_constraint(t, pltpu.HBM)
    res = pl.pallas_call(
        body, name=name,
        out_shape=(*[pltpu.SemaphoreType.DMA(())] * (2 * n), *[pltpu.HBM(t.shape, t.dtype) for t in (*srcs, *lands)],
                   jax.ShapeDtypeStruct((8, LANES), F32)),
        in_specs=[_HBM] * (2 * n),
        out_specs=(*[_SEM] * (2 * n), *[_HBM] * (2 * n), pl.BlockSpec(memory_space=pltpu.VMEM)),
        input_output_aliases={i: 2 * n + i for i in range(2 * n)},
        compiler_params=pltpu.CompilerParams(has_side_effects=_EFFECT),
    )(*[hbm(t) for t in (*srcs, *lands)])
    return res[:-1], res[-1]


def _exchange_wait(name, handle, after):
    n = len(handle) // 4
    sems, thru = handle[:2 * n], handle[2 * n:]

    def body(*refs):
        land_refs = refs[n:2 * n]
        send_sems, recv_sems = refs[2 * n:3 * n], refs[3 * n:4 * n]
        me = (lax.axis_index("x"), lax.axis_index("y"), lax.axis_index("c"))
        for a in range(n):
            seven = land_refs[a].at[pl.ds(0, len(_FLIPS))]
            all_copies = pltpu.make_async_remote_copy(
                src_ref=seven, dst_ref=seven, send_sem=send_sems[a], recv_sem=recv_sems[a], device_id=me,
                device_id_type=_MESH)
            all_copies.wait_send()
            all_copies.wait_recv()

    res = pl.pallas_call(
        body, name=name, out_shape=tuple(pltpu.HBM(t.shape, t.dtype) for t in thru),
        in_specs=[*[_HBM] * (2 * n), *[_SEM] * (2 * n), _ANY], out_specs=tuple([_HBM] * (2 * n)),
        input_output_aliases={i: i for i in range(2 * n)},
        compiler_params=pltpu.CompilerParams(has_side_effects=_EFFECT),
    )(*thru, *sems, after)
    return res[n:]


def _adamw_gathered(name, own, parts, me, w, m, v):
    r, c = w.shape
    tr = min(r, 256)

    def body(me_ref, own_ref, p_ref, w_ref, m_ref, v_ref, g_out, d_out, m_out, v_out):
        g = own_ref[...]
        for j in range(N_DEV):
            g = g + p_ref[j].astype(F32)
        delta, m_new, v_new = _adamw(w_ref[...], g, m_ref[...], v_ref[...])
        g_out[...] = g
        d_out[...] = delta
        m_out[...] = m_new
        v_out[...] = v_new

    spec = pl.BlockSpec((tr, c), lambda i, me_ref: (i, 0))
    return pl.pallas_call(
        body, name=name,
        grid_spec=pltpu.PrefetchScalarGridSpec(
            num_scalar_prefetch=1, grid=(r // tr,),
            in_specs=[pl.BlockSpec((None, tr, c), lambda i, me_ref: (me_ref[0], i, 0)),
                      pl.BlockSpec((N_DEV, tr, c), lambda i, me_ref: (0, i, 0)), spec, spec, spec],
            out_specs=[spec] * 4),
        out_shape=[jax.ShapeDtypeStruct((r, c), F32)] * 4,
        compiler_params=_params(("parallel",)),
    )(me, own, parts, w, m, v)


def _adamw(w, g, m, v):
    m = ADAM_B1 * m + (1.0 - ADAM_B1) * g
    v = ADAM_B2 * v + (1.0 - ADAM_B2) * jnp.square(g)
    m_hat = m / (1.0 - ADAM_B1 ** ADAM_STEP)
    v_hat = v / (1.0 - ADAM_B2 ** ADAM_STEP)
    delta = -ADAM_LR * (m_hat / (jnp.sqrt(v_hat) + ADAM_EPS) + ADAM_WD * w)
    return delta, m, v


def _adamw_small(name, parts, w, m, v):
    _, r, c = parts.shape
    tr = 8

    def body(p_ref, w_ref, m_ref, v_ref, g_out, d_out, m_out, v_out):
        g = p_ref[0]
        for j in range(1, N_DEV):
            g = g + p_ref[j]
        delta, m_new, v_new = _adamw(w_ref[...], g, m_ref[...], v_ref[...])
        g_out[...] = g
        d_out[...] = delta
        m_out[...] = m_new
        v_out[...] = v_new

    spec = pl.BlockSpec((tr, c), lambda i: (i, 0))
    return pl.pallas_call(
        body, name=name, grid=(r // tr,),
        in_specs=[pl.BlockSpec((N_DEV, tr, c), lambda i: (0, i, 0)), spec, spec, spec],
        out_specs=[spec] * 4, out_shape=[jax.ShapeDtypeStruct((r, c), F32)] * 4,
        compiler_params=_params(("parallel",)),
    )(parts, w, m, v)


_WEIGHTS = ["norm1_g", "w_in", "q_norm_g", "k_norm_g", "ssm_lambda_re", "ssm_lambda_im", "ssm_log_dt", "ssm_b_re",
            "ssm_b_im", "ssm_c_re", "ssm_c_im", "ssm_d", "w_glu", "b_glu", "attn_out_g", "ssm_out_g", "w_out",
            "norm2_g", "w_mlp_in", "w_mlp_out"]
_BIG = ["w_in", "w_glu", "w_out", "w_mlp_in", "w_mlp_out"]
_SMALL = [n for n in _WEIGHTS if n not in _BIG]
_PACK_COLS = 1024


def _pack(tree, last=None):
    flat = [tree[n].reshape(-1).astype(F32) for n in _SMALL]
    size = sum(f.shape[0] for f in flat)
    rows = -(-(size + 1) // (_PACK_COLS * 8)) * 8
    pad = jnp.zeros((rows * _PACK_COLS - size - 1,), F32)
    tail = jnp.zeros((1,), F32) if last is None else last.reshape(1).astype(F32)
    return jnp.concatenate(flat + [pad, tail]).reshape(rows, _PACK_COLS)


def _unpack(buf, like):
    flat, out, off = buf.reshape(-1), {}, 0
    for n in _SMALL:
        size = like[n].size
        out[n] = flat[off:off + size].reshape(like[n].shape)
        off += size
    return out


def kernel(x, norm1_g, w_in, q_norm_g, k_norm_g, ssm_lambda_re, ssm_lambda_im, ssm_log_dt, ssm_b_re, ssm_b_im, ssm_c_re, ssm_c_im, ssm_d, w_glu, b_glu, attn_out_g, ssm_out_g, w_out, norm2_g, w_mlp_in, w_mlp_out, loss_target, m_norm1_g, m_w_in, m_q_norm_g, m_k_norm_g, m_ssm_lambda_re, m_ssm_lambda_im, m_ssm_log_dt, m_ssm_b_re, m_ssm_b_im, m_ssm_c_re, m_ssm_c_im, m_ssm_d, m_w_glu, m_b_glu, m_attn_out_g, m_ssm_out_g, m_w_out, m_norm2_g, m_w_mlp_in, m_w_mlp_out, v_norm1_g, v_w_in, v_q_norm_g, v_k_norm_g, v_ssm_lambda_re, v_ssm_lambda_im, v_ssm_log_dt, v_ssm_b_re, v_ssm_b_im, v_ssm_c_re, v_ssm_c_im, v_ssm_d, v_w_glu, v_b_glu, v_attn_out_g, v_ssm_out_g, v_w_out, v_norm2_g, v_w_mlp_in, v_w_mlp_out):
    w = dict(zip(_WEIGHTS, (norm1_g, w_in, q_norm_g, k_norm_g, ssm_lambda_re, ssm_lambda_im, ssm_log_dt, ssm_b_re, ssm_b_im, ssm_c_re, ssm_c_im, ssm_d, w_glu, b_glu, attn_out_g, ssm_out_g, w_out, norm2_g, w_mlp_in, w_mlp_out)))
    m = dict(zip(_WEIGHTS, (m_norm1_g, m_w_in, m_q_norm_g, m_k_norm_g, m_ssm_lambda_re, m_ssm_lambda_im, m_ssm_log_dt, m_ssm_b_re, m_ssm_b_im, m_ssm_c_re, m_ssm_c_im, m_ssm_d, m_w_glu, m_b_glu, m_attn_out_g, m_ssm_out_g, m_w_out, m_norm2_g, m_w_mlp_in, m_w_mlp_out)))
    v = dict(zip(_WEIGHTS, (v_norm1_g, v_w_in, v_q_norm_g, v_k_norm_g, v_ssm_lambda_re, v_ssm_lambda_im, v_ssm_log_dt, v_ssm_b_re, v_ssm_b_im, v_ssm_c_re, v_ssm_c_im, v_ssm_d, v_w_glu, v_b_glu, v_attn_out_g, v_ssm_out_g, v_w_out, v_norm2_g, v_w_mlp_in, v_w_mlp_out)))
    core = lax.axis_index("c").astype(jnp.int32).reshape(1)
    chip = (2 * lax.axis_index("x") + lax.axis_index("y")).astype(jnp.int32).reshape(1)

    me = (2 * chip + core).astype(jnp.int32)

    def landing(own=None, like=None):
        own = jnp.zeros_like(like) if own is None else own
        return lax.dynamic_update_slice(lax.empty((N_DEV, *like.shape), like.dtype), own[None], (me[0], 0, 0))

    w_in_shard = w_in.astype(BF16)
    w_in_handle, _ = _exchange_start("w_in_send", [w_in_shard], [landing(w_in_shard, w_in_shard)], per_peer=False)
    late = [n for n in _BIG if n != "w_in"]
    weights = {}

    def first_weight(after):
        (blocks,) = _exchange_wait("w_in_arrive", w_in_handle, after)
        shards = [w[n].astype(BF16) for n in late]
        blocks, shards = lax.optimization_barrier((blocks, shards))
        weights["handle"], token = _exchange_start(
            "weights_send", shards, [landing(s, s) for s in shards], per_peer=False)
        return blocks.transpose(1, 0, 2).reshape(w_in.shape[0], -1), token[0, 0]

    def late_weights(after):
        got = dict(zip(late, _exchange_wait("weights_arrive", weights["handle"], after)))
        return (got["w_glu"].reshape(-1, w_glu.shape[1]), got["w_out"].reshape(-1, w_out.shape[1]),
                got["w_mlp_in"].transpose(1, 0, 2).reshape(w_mlp_in.shape[0], -1),
                got["w_mlp_out"].reshape(-1, w_mlp_out.shape[1]))

    mlp = ["w_mlp_out", "w_mlp_in"]
    sent = {}

    def send_grads(name, names, own, own_b):
        blocks = lambda g, n: g.reshape(N_DEV, *w[n].shape)
        sent[name + "_own"] = [blocks(g, n) for g, n in zip(own, names)]
        srcs = [blocks(g, n) for g, n in zip(own_b, names)]
        sent[name], token = _exchange_start(name, srcs, [landing(like=s[0]) for s in srcs], per_peer=True)
        return token[0, 0]

    def mlp_grads_ready(g_out, g_out_b, g_in, g_in_b):
        return send_grads("mlp_grads_send", mlp, [g_out, g_in], [g_out_b, g_in_b])

    rest = ["w_in", "w_glu", "w_out"]

    def rest_grads_ready(own, own_b):
        return send_grads("rest_grads_send", rest, own, own_b)

    loss_local, grad_x, g_small, g_big = _local_step(
        x, loss_target, {n: w[n] for n in _SMALL}, first_weight, late_weights, mlp_grads_ready, rest_grads_ready)

    grads, delta, new_m, new_v = {}, {}, {}, {}
    small = _pack(g_small, last=loss_local)
    small_handle, small_token = _exchange_start("small_grads_send", [small], [landing(small, small)], per_peer=False)

    for send, arrive, names in (("mlp_grads_send", "mlp_grads_arrive", mlp),
                                ("rest_grads_send", "rest_grads_arrive", rest)):
        for n, own, part in zip(names, sent[send + "_own"], _exchange_wait(arrive, sent[send], small_token)):
            grads[n], delta[n], new_m[n], new_v[n] = _adamw_gathered("adamw_" + n, own, part, me, w[n], m[n], v[n])

    shards_done = lax.optimization_barrier(tuple(new_v[n] for n in _BIG))
    (small_parts,) = _exchange_wait("small_grads_arrive", small_handle, shards_done[-1])
    packed = _adamw_small("adamw_small", small_parts, _pack(w), _pack(m), _pack(v))
    for tree, buf in zip((grads, delta, new_m, new_v), packed):
        tree.update(_unpack(buf, w))
    loss = packed[0][-1, -1]

    return (loss, grad_x, *[grads[n] for n in _WEIGHTS], *[delta[n] for n in _WEIGHTS],
            *[new_m[n] for n in _WEIGHTS], *[new_v[n] for n in _WEIGHTS])
```
